```python
import math
import jax
import jax.numpy as jnp
from jax import lax
import numpy as np

D_MODEL = 1024
BATCH = 8
SEQ = 4096
DEPTH = 2

N_MIXERS = 2
N_POOL_LAYERS = (DEPTH + 1) // 2
N_ATTN_LAYERS = DEPTH // 2

POOL_WINDOWS = (2, 4, 8, 16)
N_POOL_GROUPS = len(POOL_WINDOWS)
POOL_GC = D_MODEL // N_POOL_GROUPS

HEAD_DIM = 64
N_HEADS = D_MODEL // HEAD_DIM
N_KV_HEADS = 4
N_GROUPS = N_HEADS // N_KV_HEADS
WINDOW = 128
BLOCK = 128
QKV_DIM = (N_HEADS + 2 * N_KV_HEADS) * HEAD_DIM

D_FF = 2816
CONV_W = 3

RMS_EPS = 1e-6

kernel_name = "hybrid_pool_swa_sink_alibi_convffn"


def rms_norm(x, g):
    xf = x.astype(jnp.float32)
    y = xf * lax.rsqrt(jnp.mean(xf * xf, axis=-1, keepdims=True) + RMS_EPS)
    return (y * g.astype(jnp.float32)).astype(x.dtype)


def alibi_slopes(n_heads):
    h = jnp.arange(1, n_heads + 1, dtype=jnp.float32)
    return jnp.exp2(-8.0 / n_heads * h)


def pool_mixer(h, w, b, scale):
    B, S, D = h.shape
    hf = h.astype(jnp.float32)
    csum = jnp.cumsum(hf, axis=1)
    t1 = jnp.arange(1, S + 1)
    outs = []
    for gi, win in enumerate(POOL_WINDOWS):
        sl = slice(gi * POOL_GC, (gi + 1) * POOL_GC)
        c = csum[..., sl]
        lo = jnp.pad(c, ((0, 0), (win, 0), (0, 0)))[:, :S]
        count = jnp.minimum(t1, win).astype(jnp.float32)
        mean = (c - lo) / count[None, :, None]
        outs.append(mean - hf[..., sl])
    p = jnp.stack(outs, axis=2).astype(h.dtype)
    y = jnp.einsum('bsgc,gcd->bsgd', p, w) + b
    return y.reshape(B, S, D) * scale


def swa_sink_attention(h, w_qkv, b_qkv, sinks, w_o):
    B, S, _ = h.shape
    nb = S // BLOCK
    qkv = h @ w_qkv + b_qkv
    q = qkv[..., :N_HEADS * HEAD_DIM].reshape(B, nb, BLOCK, N_KV_HEADS, N_GROUPS, HEAD_DIM)
    k = qkv[..., N_HEADS * HEAD_DIM:(N_HEADS + N_KV_HEADS) * HEAD_DIM].reshape(B, S, N_KV_HEADS, HEAD_DIM)
    v = qkv[..., (N_HEADS + N_KV_HEADS) * HEAD_DIM:].reshape(B, S, N_KV_HEADS, HEAD_DIM)

    def band(t):
        tp = jnp.pad(t, ((0, 0), (BLOCK, 0), (0, 0), (0, 0)))[:, :S]
        prev = tp.reshape(B, nb, BLOCK, N_KV_HEADS, HEAD_DIM)
        cur = t.reshape(B, nb, BLOCK, N_KV_HEADS, HEAD_DIM)
        return jnp.concatenate([prev, cur], axis=2)

    kb, vb = band(k), band(v)
    scores = jnp.einsum('bnqkgd,bnskd->bnkgqs', q, kb).astype(jnp.float32) * (HEAD_DIM ** -0.5)

    qi = jnp.arange(BLOCK)[:, None]
    kj = jnp.arange(2 * BLOCK)[None, :]
    dist = qi + BLOCK - kj
    key_pos = jnp.arange(nb)[:, None] * BLOCK - BLOCK + jnp.arange(2 * BLOCK)[None, :]
    mask = ((dist >= 0) & (dist < WINDOW))[None] & (key_pos >= 0)[:, None, :]

    slopes = alibi_slopes(N_HEADS).reshape(N_KV_HEADS, N_GROUPS)
    bias = -slopes[:, :, None, None] * dist.astype(jnp.float32)[None, None]
    scores = jnp.where(mask[None, :, None, None], scores + bias, -jnp.inf)

    sink = sinks.astype(jnp.float32).reshape(N_KV_HEADS, N_GROUPS)[None, None, :, :, None, None]
    m = jnp.maximum(jnp.max(scores, axis=-1, keepdims=True), sink)
    p = jnp.exp(scores - m)
    denom = jnp.sum(p, axis=-1, keepdims=True) + jnp.exp(sink - m)
    probs = (p / denom).astype(v.dtype)
    o = jnp.einsum('bnkgqs,bnskd->bnqkgd', probs, vb).reshape(B, S, N_HEADS * HEAD_DIM)
    return o @ w_o


def conv_glu_ffn(h, w_in, conv_w, conv_b, w_out):
    u = h @ w_in
    c = lax.conv_general_dilated(
        u, conv_w[:, None, :].astype(u.dtype), window_strides=(1,),
        padding=[(CONV_W - 1, 0)], dimension_numbers=('NWC', 'WIO', 'NWC'),
        feature_group_count=2 * D_FF) + conv_b
    a, g = c[..., :D_FF], c[..., D_FF:]
    return (a * jax.nn.silu(g)) @ w_out


def _fwd_setup_inputs(seed: int = 0) -> dict:
    key = jax.random.key(seed)
    ks = jax.random.split(key, 16)
    f32 = jnp.float32
    nrm = lambda k, shape, s: jax.random.normal(k, shape, f32) * s
    return {
        "x": nrm(ks[0], (BATCH, SEQ, D_MODEL), 1.0),
        "pool_w": nrm(ks[1], (N_POOL_LAYERS, N_POOL_GROUPS, POOL_GC, POOL_GC), POOL_GC ** -0.5),
        "pool_b": nrm(ks[2], (N_POOL_LAYERS, N_POOL_GROUPS, POOL_GC), 0.02),
        "pool_scale": 1.0 + nrm(ks[3], (N_POOL_LAYERS, D_MODEL), 0.05),
        "attn_w_qkv": nrm(ks[4], (N_ATTN_LAYERS, D_MODEL, QKV_DIM), D_MODEL ** -0.5),
        "attn_b_qkv": nrm(ks[5], (N_ATTN_LAYERS, QKV_DIM), 0.02),
        "attn_sinks": nrm(ks[6], (N_ATTN_LAYERS, N_HEADS), 0.5),
        "attn_w_o": nrm(ks[7], (N_ATTN_LAYERS, N_HEADS * HEAD_DIM, D_MODEL), (N_HEADS * HEAD_DIM) ** -0.5),
        "norm_mix": 1.0 + nrm(ks[8], (DEPTH, D_MODEL), 0.05),
        "norm_ffn": 1.0 + nrm(ks[9], (DEPTH, D_MODEL), 0.05),
        "ffn_w_in": nrm(ks[10], (DEPTH, D_MODEL, 2 * D_FF), D_MODEL ** -0.5),
        "ffn_conv_w": nrm(ks[11], (DEPTH, CONV_W, 2 * D_FF), CONV_W ** -0.5),
        "ffn_conv_b": nrm(ks[12], (DEPTH, 2 * D_FF), 0.02),
        "ffn_w_out": nrm(ks[13], (DEPTH, D_FF, D_MODEL), D_FF ** -0.5),
        "norm_f": 1.0 + nrm(ks[14], (D_MODEL,), 0.05),
    }


def _fwd_reference(x, pool_w, pool_b, pool_scale, attn_w_qkv, attn_b_qkv, attn_sinks, attn_w_o,
              norm_mix, norm_ffn, ffn_w_in, ffn_conv_w, ffn_conv_b, ffn_w_out, norm_f):
    h = x
    for i in range(DEPTH):
        hn = rms_norm(h, norm_mix[i])
        j = i // N_MIXERS
        if i % N_MIXERS == 0:
            mix = pool_mixer(hn, pool_w[j], pool_b[j], pool_scale[j])
        else:
            mix = swa_sink_attention(hn, attn_w_qkv[j], attn_b_qkv[j], attn_sinks[j], attn_w_o[j])
        h = h + mix
        hn = rms_norm(h, norm_ffn[i])
        h = h + conv_glu_ffn(hn, ffn_w_in[i], ffn_conv_w[i], ffn_conv_b[i], ffn_w_out[i])
    return rms_norm(h, norm_f)


import jax as _jax
import jax.numpy as _jnp

TWIN_FORMAT = 'train_step'
FWD_PARAMS = ['x', 'pool_w', 'pool_b', 'pool_scale', 'attn_w_qkv', 'attn_b_qkv', 'attn_sinks', 'attn_w_o', 'norm_mix', 'norm_ffn', 'ffn_w_in', 'ffn_conv_w', 'ffn_conv_b', 'ffn_w_out', 'norm_f']
TWIN_WEIGHTS = ['pool_w', 'pool_b', 'pool_scale', 'attn_w_qkv', 'attn_b_qkv', 'attn_sinks', 'attn_w_o', 'norm_mix', 'norm_ffn', 'ffn_w_in', 'ffn_conv_w', 'ffn_conv_b', 'ffn_w_out', 'norm_f']
TWIN_DIFF_INPUT = 'x'
TWIN_INPUTS = ['x', 'pool_w', 'pool_b', 'pool_scale', 'attn_w_qkv', 'attn_b_qkv', 'attn_sinks', 'attn_w_o', 'norm_mix', 'norm_ffn', 'ffn_w_in', 'ffn_conv_w', 'ffn_conv_b', 'ffn_w_out', 'norm_f', 'loss_target', 'm_pool_w', 'm_pool_b', 'm_pool_scale', 'm_attn_w_qkv', 'm_attn_b_qkv', 'm_attn_sinks', 'm_attn_w_o', 'm_norm_mix', 'm_norm_ffn', 'm_ffn_w_in', 'm_ffn_conv_w', 'm_ffn_conv_b', 'm_ffn_w_out', 'm_norm_f', 'v_pool_w', 'v_pool_b', 'v_pool_scale', 'v_attn_w_qkv', 'v_attn_b_qkv', 'v_attn_sinks', 'v_attn_w_o', 'v_norm_mix', 'v_norm_ffn', 'v_ffn_w_in', 'v_ffn_conv_w', 'v_ffn_conv_b', 'v_ffn_w_out', 'v_norm_f']
TWIN_OUTPUTS = ['loss', 'grad_x', 'grad_pool_w', 'grad_pool_b', 'grad_pool_scale', 'grad_attn_w_qkv', 'grad_attn_b_qkv', 'grad_attn_sinks', 'grad_attn_w_o', 'grad_norm_mix', 'grad_norm_ffn', 'grad_ffn_w_in', 'grad_ffn_conv_w', 'grad_ffn_conv_b', 'grad_ffn_w_out', 'grad_norm_f', 'delta_pool_w', 'delta_pool_b', 'delta_pool_scale', 'delta_attn_w_qkv', 'delta_attn_b_qkv', 'delta_attn_sinks', 'delta_attn_w_o', 'delta_norm_mix', 'delta_norm_ffn', 'delta_ffn_w_in', 'delta_ffn_conv_w', 'delta_ffn_conv_b', 'delta_ffn_w_out', 'delta_norm_f', 'new_m_pool_w', 'new_m_pool_b', 'new_m_pool_scale', 'new_m_attn_w_qkv', 'new_m_attn_b_qkv', 'new_m_attn_sinks', 'new_m_attn_w_o', 'new_m_norm_mix', 'new_m_norm_ffn', 'new_m_ffn_w_in', 'new_m_ffn_conv_w', 'new_m_ffn_conv_b', 'new_m_ffn_w_out', 'new_m_norm_f', 'new_v_pool_w', 'new_v_pool_b', 'new_v_pool_scale', 'new_v_attn_w_qkv', 'new_v_attn_b_qkv', 'new_v_attn_sinks', 'new_v_attn_w_o', 'new_v_norm_mix', 'new_v_norm_ffn', 'new_v_ffn_w_in', 'new_v_ffn_conv_w', 'new_v_ffn_conv_b', 'new_v_ffn_w_out', 'new_v_norm_f']
TWIN_LEAF_KINDS = {'loss': 'loss', 'grad_x': 'grad_x', 'grad_pool_w': 'grad_w', 'grad_pool_b': 'grad_w', 'grad_pool_scale': 'grad_w', 'grad_attn_w_qkv': 'grad_w', 'grad_attn_b_qkv': 'grad_w', 'grad_attn_sinks': 'grad_w', 'grad_attn_w_o': 'grad_w', 'grad_norm_mix': 'grad_w', 'grad_norm_ffn': 'grad_w', 'grad_ffn_w_in': 'grad_w', 'grad_ffn_conv_w': 'grad_w', 'grad_ffn_conv_b': 'grad_w', 'grad_ffn_w_out': 'grad_w', 'grad_norm_f': 'grad_w', 'delta_pool_w': 'delta_w', 'delta_pool_b': 'delta_w', 'delta_pool_scale': 'delta_w', 'delta_attn_w_qkv': 'delta_w', 'delta_attn_b_qkv': 'delta_w', 'delta_attn_sinks': 'delta_w', 'delta_attn_w_o': 'delta_w', 'delta_norm_mix': 'delta_w', 'delta_norm_ffn': 'delta_w', 'delta_ffn_w_in': 'delta_w', 'delta_ffn_conv_w': 'delta_w', 'delta_ffn_conv_b': 'delta_w', 'delta_ffn_w_out': 'delta_w', 'delta_norm_f': 'delta_w', 'new_m_pool_w': 'new_m', 'new_m_pool_b': 'new_m', 'new_m_pool_scale': 'new_m', 'new_m_attn_w_qkv': 'new_m', 'new_m_attn_b_qkv': 'new_m', 'new_m_attn_sinks': 'new_m', 'new_m_attn_w_o': 'new_m', 'new_m_norm_mix': 'new_m', 'new_m_norm_ffn': 'new_m', 'new_m_ffn_w_in': 'new_m', 'new_m_ffn_conv_w': 'new_m', 'new_m_ffn_conv_b': 'new_m', 'new_m_ffn_w_out': 'new_m', 'new_m_norm_f': 'new_m', 'new_v_pool_w': 'new_v', 'new_v_pool_b': 'new_v', 'new_v_pool_scale': 'new_v', 'new_v_attn_w_qkv': 'new_v', 'new_v_attn_b_qkv': 'new_v', 'new_v_attn_sinks': 'new_v', 'new_v_attn_w_o': 'new_v', 'new_v_norm_mix': 'new_v', 'new_v_norm_ffn': 'new_v', 'new_v_ffn_w_in': 'new_v', 'new_v_ffn_conv_w': 'new_v', 'new_v_ffn_conv_b': 'new_v', 'new_v_ffn_w_out': 'new_v', 'new_v_norm_f': 'new_v'}


def _forward(args):
    return _fwd_reference(*[args[k] for k in FWD_PARAMS])


def _output_shape():
    out = _jax.eval_shape(lambda: _forward(_fwd_setup_inputs(0)))
    return out.shape, out.dtype

N_MICROBATCH = 1
ADAM_LR = 0.001
ADAM_B1 = 0.9
ADAM_B2 = 0.999
ADAM_EPS = 1e-08
ADAM_WD = 0.01
ADAM_STEP = 10
PER_EXAMPLE_BATCH_AXIS = {'x': 0, 'loss_target': 0}
SHARED_INPUTS = []
_WEIGHT_DTYPES = {'pool_w': _jnp.float32, 'pool_b': _jnp.float32, 'pool_scale': _jnp.float32, 'attn_w_qkv': _jnp.float32, 'attn_b_qkv': _jnp.float32, 'attn_sinks': _jnp.float32, 'attn_w_o': _jnp.float32, 'norm_mix': _jnp.float32, 'norm_ffn': _jnp.float32, 'ffn_w_in': _jnp.float32, 'ffn_conv_w': _jnp.float32, 'ffn_conv_b': _jnp.float32, 'ffn_w_out': _jnp.float32, 'norm_f': _jnp.float32}
MOMENT_SCALE = {'pool_w': 1.478934e-01, 'pool_b': 1.820808e-01, 'pool_scale': 7.829467e-01, 'attn_w_qkv': 4.745691e-02, 'attn_b_qkv': 1.047619e-01, 'attn_sinks': 4.430453e-02, 'attn_w_o': 3.855360e-02, 'norm_mix': 1.153226e-01, 'norm_ffn': 1.105066e-01, 'ffn_w_in': 4.694449e-02, 'ffn_conv_w': 4.684548e-02, 'ffn_conv_b': 4.563760e-02, 'ffn_w_out': 7.682861e-02, 'norm_f': 3.213140e+01}


def _to_microbatches(a, axis):
    t = _jnp.moveaxis(a, axis, 0)
    t = t.reshape((N_MICROBATCH, t.shape[0] // N_MICROBATCH) + t.shape[1:])
    return _jnp.moveaxis(t, 1, axis + 1)


def setup_inputs(seed: int = 0) -> dict:
    inp = _fwd_setup_inputs(seed)
    key = _jax.random.fold_in(_jax.random.key(seed), 7919)
    shape, _ = _output_shape()
    out = dict(inp)
    out["loss_target"] = _jax.random.normal(_jax.random.fold_in(key, 0), shape, _jnp.float32)
    for i, name in enumerate(TWIN_WEIGHTS):
        w = inp[name].astype(_jnp.float32)
        if MOMENT_SCALE is None:
            s = _jnp.sqrt(_jnp.mean(_jnp.square(w)) + 1e-30)
        else:
            s = MOMENT_SCALE[name]
        km, kv = _jax.random.split(_jax.random.fold_in(key, i + 1))
        out[name] = w
        out["m_" + name] = s * _jax.random.normal(km, w.shape, _jnp.float32)
        out["v_" + name] = (s * s) * _jax.random.uniform(kv, w.shape, _jnp.float32, 0.5, 1.5)
    if N_MICROBATCH > 1:
        for name, axis in PER_EXAMPLE_BATCH_AXIS.items():
            out[name] = _to_microbatches(out[name], axis)
    return {'x': out['x'], 'pool_w': out['pool_w'], 'pool_b': out['pool_b'], 'pool_scale': out['pool_scale'], 'attn_w_qkv': out['attn_w_qkv'], 'attn_b_qkv': out['attn_b_qkv'], 'attn_sinks': out['attn_sinks'], 'attn_w_o': out['attn_w_o'], 'norm_mix': out['norm_mix'], 'norm_ffn': out['norm_ffn'], 'ffn_w_in': out['ffn_w_in'], 'ffn_conv_w': out['ffn_conv_w'], 'ffn_conv_b': out['ffn_conv_b'], 'ffn_w_out': out['ffn_w_out'], 'norm_f': out['norm_f'], 'loss_target': out['loss_target'], 'm_pool_w': out['m_pool_w'], 'm_pool_b': out['m_pool_b'], 'm_pool_scale': out['m_pool_scale'], 'm_attn_w_qkv': out['m_attn_w_qkv'], 'm_attn_b_qkv': out['m_attn_b_qkv'], 'm_attn_sinks': out['m_attn_sinks'], 'm_attn_w_o': out['m_attn_w_o'], 'm_norm_mix': out['m_norm_mix'], 'm_norm_ffn': out['m_norm_ffn'], 'm_ffn_w_in': out['m_ffn_w_in'], 'm_ffn_conv_w': out['m_ffn_conv_w'], 'm_ffn_conv_b': out['m_ffn_conv_b'], 'm_ffn_w_out': out['m_ffn_w_out'], 'm_norm_f': out['m_norm_f'], 'v_pool_w': out['v_pool_w'], 'v_pool_b': out['v_pool_b'], 'v_pool_scale': out['v_pool_scale'], 'v_attn_w_qkv': out['v_attn_w_qkv'], 'v_attn_b_qkv': out['v_attn_b_qkv'], 'v_attn_sinks': out['v_attn_sinks'], 'v_attn_w_o': out['v_attn_w_o'], 'v_norm_mix': out['v_norm_mix'], 'v_norm_ffn': out['v_norm_ffn'], 'v_ffn_w_in': out['v_ffn_w_in'], 'v_ffn_conv_w': out['v_ffn_conv_w'], 'v_ffn_conv_b': out['v_ffn_conv_b'], 'v_ffn_w_out': out['v_ffn_w_out'], 'v_norm_f': out['v_norm_f']}


def _loss(weights, diff, rest, loss_target):
    with _jax.named_scope("forward"):
        args = {**rest, TWIN_DIFF_INPUT: diff, **{k: w.astype(_WEIGHT_DTYPES[k]) for k, w in weights.items()}}
        y = _forward(args)
    with _jax.named_scope("loss_head"):
        err = _jnp.square(y.astype(_jnp.float32) - loss_target)
        return 0.5 * _jnp.sum(_jnp.mean(err, axis=-1)) if err.ndim else 0.5 * err


def _adamw(w, g, m, v):
    m = ADAM_B1 * m + (1.0 - ADAM_B1) * g
    v = ADAM_B2 * v + (1.0 - ADAM_B2) * _jnp.square(g)
    m_hat = m / (1.0 - ADAM_B1 ** ADAM_STEP)
    v_hat = v / (1.0 - ADAM_B2 ** ADAM_STEP)
    delta = -ADAM_LR * (m_hat / (_jnp.sqrt(v_hat) + ADAM_EPS) + ADAM_WD * w)
    return delta, m, v


def reference(x, pool_w, pool_b, pool_scale, attn_w_qkv, attn_b_qkv, attn_sinks, attn_w_o, norm_mix, norm_ffn, ffn_w_in, ffn_conv_w, ffn_conv_b, ffn_w_out, norm_f, loss_target, m_pool_w, m_pool_b, m_pool_scale, m_attn_w_qkv, m_attn_b_qkv, m_attn_sinks, m_attn_w_o, m_norm_mix, m_norm_ffn, m_ffn_w_in, m_ffn_conv_w, m_ffn_conv_b, m_ffn_w_out, m_norm_f, v_pool_w, v_pool_b, v_pool_scale, v_attn_w_qkv, v_attn_b_qkv, v_attn_sinks, v_attn_w_o, v_norm_mix, v_norm_ffn, v_ffn_w_in, v_ffn_conv_w, v_ffn_conv_b, v_ffn_w_out, v_norm_f):
    given = dict(x=x, pool_w=pool_w, pool_b=pool_b, pool_scale=pool_scale, attn_w_qkv=attn_w_qkv, attn_b_qkv=attn_b_qkv, attn_sinks=attn_sinks, attn_w_o=attn_w_o, norm_mix=norm_mix, norm_ffn=norm_ffn, ffn_w_in=ffn_w_in, ffn_conv_w=ffn_conv_w, ffn_conv_b=ffn_conv_b, ffn_w_out=ffn_w_out, norm_f=norm_f, loss_target=loss_target, m_pool_w=m_pool_w, m_pool_b=m_pool_b, m_pool_scale=m_pool_scale, m_attn_w_qkv=m_attn_w_qkv, m_attn_b_qkv=m_attn_b_qkv, m_attn_sinks=m_attn_sinks, m_attn_w_o=m_attn_w_o, m_norm_mix=m_norm_mix, m_norm_ffn=m_norm_ffn, m_ffn_w_in=m_ffn_w_in, m_ffn_conv_w=m_ffn_conv_w, m_ffn_conv_b=m_ffn_conv_b, m_ffn_w_out=m_ffn_w_out, m_norm_f=m_norm_f, v_pool_w=v_pool_w, v_pool_b=v_pool_b, v_pool_scale=v_pool_scale, v_attn_w_qkv=v_attn_w_qkv, v_attn_b_qkv=v_attn_b_qkv, v_attn_sinks=v_attn_sinks, v_attn_w_o=v_attn_w_o, v_norm_mix=v_norm_mix, v_norm_ffn=v_norm_ffn, v_ffn_w_in=v_ffn_w_in, v_ffn_conv_w=v_ffn_conv_w, v_ffn_conv_b=v_ffn_conv_b, v_ffn_w_out=v_ffn_w_out, v_norm_f=v_norm_f)
    weights = {n: given[n] for n in TWIN_WEIGHTS}
    shared = {n: given[n] for n in SHARED_INPUTS}
    per_example = {n: given[n] for n in ['x']}
    grad_fn = _jax.value_and_grad(_loss, argnums=(0, 1))

    def one_microbatch(ex, loss_target):
        ex = dict(ex)
        diff = ex.pop(TWIN_DIFF_INPUT)
        return grad_fn(weights, diff, {**shared, **ex}, loss_target)

    if N_MICROBATCH == 1:
        loss, (grad_w, grad_x) = one_microbatch(per_example, given["loss_target"])
    else:
        def body(carry, xs):
            loss_sum, grad_sum = carry
            l_k, (gw_k, gx_k) = one_microbatch(xs[0], xs[1])
            with _jax.named_scope("update"):
                return (loss_sum + l_k, _jax.tree.map(_jnp.add, grad_sum, gw_k)), gx_k

        init = (_jnp.zeros((), _jnp.float32), _jax.tree.map(_jnp.zeros_like, weights))
        (loss, grad_w), grad_x = _jax.lax.scan(body, init, (per_example, given["loss_target"]))
    with _jax.named_scope("update"):
        delta_w, new_m, new_v = {}, {}, {}
        for n in TWIN_WEIGHTS:
            delta_w[n], new_m[n], new_v[n] = _adamw(weights[n], grad_w[n], given["m_" + n], given["v_" + n])
    return (loss, grad_x, *[grad_w[n] for n in TWIN_WEIGHTS], *[delta_w[n] for n in TWIN_WEIGHTS],
            *[new_m[n] for n in TWIN_WEIGHTS], *[new_v[n] for n in TWIN_WEIGHTS])
```

```python
import functools

import jax
import jax.numpy as jnp
from jax import lax
from jax.experimental import pallas as pl
from jax.experimental.pallas import tpu as pltpu

F32 = jnp.float32
MXU_DTYPE = jnp.bfloat16
WIRE_DTYPE = jnp.bfloat16

D_MODEL = 1024
D_FF = 2816
QKV_DIM = 1536
HEAD_DIM = 64
N_HEADS = 16
N_KV_HEADS = 4
ATT_BLOCK = 128
POOL_WINDOWS = (2, 4, 8, 16)
POOL_GC = 256
POOL_HALO = 16
CONV_HALO = 8
RMS_EPS = 1e-6
ATT_SCALE = HEAD_DIM ** -0.5
ALIBI_SLOPES = tuple(2.0 ** (-8.0 / N_HEADS * (h + 1)) for h in range(N_HEADS))

ADAM_LR = 0.001
ADAM_B1 = 0.9
ADAM_B2 = 0.999
ADAM_EPS = 1e-08
ADAM_WD = 0.01
ADAM_STEP = 10

N_CHIPS = 4
MESH = pl.DeviceIdType.MESH
VMEM_LIMIT_BYTES = 56 * 1024 * 1024
ANY = pl.BlockSpec(memory_space=pl.ANY)


def _params(*semantics):
    return pltpu.CompilerParams(dimension_semantics=semantics, vmem_limit_bytes=VMEM_LIMIT_BYTES)


def _rms(x, g):
    return x * lax.rsqrt(jnp.mean(x * x, axis=-1, keepdims=True) + RMS_EPS) * g


def _rms_bwd(x, g, dy):
    rstd = lax.rsqrt(jnp.mean(x * x, axis=-1, keepdims=True) + RMS_EPS)
    xhat = x * rstd
    dxhat = dy * g
    dx = rstd * (dxhat - xhat * jnp.mean(dxhat * xhat, axis=-1, keepdims=True))
    return dx, dy * xhat


def _shift_down(x, s):
    return pltpu.roll(x, s, axis=0)


def _shift_up(x, s):
    return pltpu.roll(x, x.shape[0] - s, axis=0)


def _sigmoid(x):
    return 1.0 / (1.0 + jnp.exp(-x))


_DOT_DIMS = {"nn": ((1,), (0,)), "nt": ((1,), (1,)), "tn": ((0,), (0,))}


def _matmul(a, b, *, mode, name, grid, a_spec, b_spec, o_spec, out_shape, bias=None, bias_spec=None, residual=None,
            residual_spec=None, into=None):
    nk = grid[2]
    dims = (_DOT_DIMS[mode], ((), ()))
    acc_shape = tuple(d for d in o_spec.block_shape if d is not None)

    def body(*refs):
        refs = list(refs)
        a_ref, b_ref = refs[0], refs[1]
        pos = 2
        bias_ref = res_ref = None
        if bias is not None:
            bias_ref = refs[pos]
            pos += 1
        if residual is not None:
            res_ref = refs[pos]
            pos += 1
        if into is not None:
            pos += 1
        o_ref = refs[pos]
        acc_ref = refs[pos + 1] if nk > 1 else None
        prod = lax.dot_general(a_ref[...].astype(MXU_DTYPE), b_ref[...].astype(MXU_DTYPE), dims,
                               preferred_element_type=F32)

        def finish(acc):
            if bias_ref is not None:
                acc = acc + bias_ref[...]
            if res_ref is not None:
                acc = acc + res_ref[...]
            o_ref[...] = acc.astype(o_ref.dtype)

        if nk == 1:
            finish(prod)
            return
        k = pl.program_id(2)

        @pl.when(k == 0)
        def _():
            acc_ref[...] = prod

        @pl.when(k > 0)
        def _():
            acc_ref[...] += prod

        @pl.when(k == nk - 1)
        def _():
            finish(acc_ref[...])

    operands, in_specs = [a, b], [a_spec, b_spec]
    if bias is not None:
        operands.append(bias)
        in_specs.append(bias_spec)
    if residual is not None:
        operands.append(residual)
        in_specs.append(residual_spec)
    aliases = {}
    if into is not None:
        aliases = {len(operands): 0}
        operands.append(into)
        in_specs.append(ANY)
    return pl.pallas_call(
        body, name=name, grid=grid, in_specs=in_specs, out_specs=o_spec, out_shape=out_shape,
        scratch_shapes=[pltpu.VMEM(acc_shape, F32)] if nk > 1 else [],
        input_output_aliases=aliases, compiler_params=_params("parallel", "parallel", "arbitrary"),
    )(*operands)


def _tile(n, want):
    t = min(n, want)
    assert n % t == 0, (n, want)
    return t


def _mm_nn(a, b, name, *, b_lead=None, bias=None, residual=None, out_dtype=F32, tm=1024, tn=1024, tk=1024):
    (m, k), n = a.shape, b.shape[-1]
    tm, tn, tk = _tile(m, tm), _tile(n, tn), _tile(k, tk)
    if b_lead is None:
        b_spec = pl.BlockSpec((tk, tn), lambda i, j, kk: (kk, j))
    else:
        b_spec = pl.BlockSpec((None, tk, tn), lambda i, j, kk: (b_lead, kk, j))
    return _matmul(
        a, b, mode="nn", name=name, grid=(m // tm, n // tn, k // tk),
        a_spec=pl.BlockSpec((tm, tk), lambda i, j, kk: (i, kk)), b_spec=b_spec,
        o_spec=pl.BlockSpec((tm, tn), lambda i, j, kk: (i, j)), out_shape=jax.ShapeDtypeStruct((m, n), out_dtype),
        bias=bias, bias_spec=pl.BlockSpec((1, tn), lambda i, j, kk: (0, j)),
        residual=residual, residual_spec=pl.BlockSpec((tm, tn), lambda i, j, kk: (i, j)))


def _mm_nt(a, b, name, *, b_lead=None, out_dtype=F32, tm=1024, tn=1024, tk=1024):
    (m, k), n = a.shape, b.shape[-2]
    tm, tn, tk = _tile(m, tm), _tile(n, tn), _tile(k, tk)
    if b_lead is None:
        b_spec = pl.BlockSpec((tn, tk), lambda i, j, kk: (j, kk))
    else:
        b_spec = pl.BlockSpec((None, tn, tk), lambda i, j, kk: (b_lead, j, kk))
    return _matmul(
        a, b, mode="nt", name=name, grid=(m // tm, n // tn, k // tk),
        a_spec=pl.BlockSpec((tm, tk), lambda i, j, kk: (i, kk)), b_spec=b_spec,
        o_spec=pl.BlockSpec((tm, tn), lambda i, j, kk: (i, j)), out_shape=jax.ShapeDtypeStruct((m, n), out_dtype))


def _mm_tn(a, b, name, *, lead=None, n_lead=None, into=None, tm=1024, tn=1024, tk=1024):
    (k, m), n = a.shape, b.shape[-1]
    tm, tn, tk = _tile(m, tm), _tile(n, tn), _tile(k, tk)
    if lead is None:
        o_spec = pl.BlockSpec((tm, tn), lambda i, j, kk: (i, j))
        out_shape = jax.ShapeDtypeStruct((m, n), F32)
    else:
        o_spec = pl.BlockSpec((None, tm, tn), lambda i, j, kk: (lead, i, j))
        out_shape = jax.ShapeDtypeStruct((n_lead, m, n), F32)
    return _matmul(
        a, b, mode="tn", name=name, grid=(m // tm, n // tn, k // tk),
        a_spec=pl.BlockSpec((tk, tm), lambda i, j, kk: (kk, i)),
        b_spec=pl.BlockSpec((tk, tn), lambda i, j, kk: (kk, j)), o_spec=o_spec, out_shape=out_shape, into=into)


ROW_TILE = 512


def _rms_fwd_call(h, g, name, out_dtype):
    t = h.shape[0]
    tt = _tile(t, ROW_TILE)

    def body(h_ref, g_ref, o_ref):
        o_ref[...] = _rms(h_ref[...], g_ref[...]).astype(o_ref.dtype)

    row = pl.BlockSpec((tt, D_MODEL), lambda i: (i, 0))
    vec = pl.BlockSpec((1, D_MODEL), lambda i: (0, 0))
    return pl.pallas_call(body, name=name, grid=(t // tt,), in_specs=[row, vec], out_specs=row,
                          out_shape=jax.ShapeDtypeStruct(h.shape, out_dtype), compiler_params=_params("parallel"))(h, g)


def _rms_bwd_call(h, g, dy, dres, name):
    t = h.shape[0]
    tt = _tile(t, ROW_TILE)

    def body(h_ref, g_ref, dy_ref, dres_ref, dx_ref, dg_ref):
        dx, dgt = _rms_bwd(h_ref[...], g_ref[...], dy_ref[...].astype(F32))
        dx_ref[...] = dres_ref[...] + dx

        @pl.when(pl.program_id(0) == 0)
        def _():
            dg_ref[...] = jnp.zeros_like(dg_ref)

        dg_ref[...] += jnp.sum(dgt, axis=0, keepdims=True)

    row = pl.BlockSpec((tt, D_MODEL), lambda i: (i, 0))
    vec = pl.BlockSpec((1, D_MODEL), lambda i: (0, 0))
    return pl.pallas_call(
        body, name=name, grid=(t // tt,), in_specs=[row, vec, row, row], out_specs=[row, vec],
        out_shape=[jax.ShapeDtypeStruct(h.shape, F32), jax.ShapeDtypeStruct((1, D_MODEL), F32)],
        compiler_params=_params("arbitrary"))(h, g, dy, dres)


def _loss_call(h, g, target):
    t = h.shape[0]
    tt = _tile(t, ROW_TILE)

    def body(h_ref, g_ref, t_ref, loss_ref, dh_ref, dg_ref):
        x, gv = h_ref[...], g_ref[...]
        err = _rms(x, gv) - t_ref[...]
        dx, dgt = _rms_bwd(x, gv, err * (1.0 / D_MODEL))
        dh_ref[...] = dx

        @pl.when(pl.program_id(0) == 0)
        def _():
            dg_ref[...] = jnp.zeros_like(dg_ref)
            loss_ref[...] = jnp.zeros_like(loss_ref)

        dg_ref[...] += jnp.sum(dgt, axis=0, keepdims=True)
        per_token = jnp.mean(err * err, axis=-1, keepdims=True)
        loss_ref[...] += 0.5 * jnp.sum(per_token, axis=0, keepdims=True)

    row = pl.BlockSpec((tt, D_MODEL), lambda i: (i, 0))
    vec = pl.BlockSpec((1, D_MODEL), lambda i: (0, 0))
    one = pl.BlockSpec((1, 1), lambda i: (0, 0))
    return pl.pallas_call(
        body, name="loss_head", grid=(t // tt,), in_specs=[row, vec, row], out_specs=[one, row, vec],
        out_shape=[jax.ShapeDtypeStruct((1, 1), F32), jax.ShapeDtypeStruct(h.shape, F32),
                   jax.ShapeDtypeStruct((1, D_MODEL), F32)],
        compiler_params=_params("arbitrary"))(h, g, target)


def _pool_windows_causal(ext, first_row):
    tt = ext.shape[0] - POOL_HALO
    t = first_row + lax.broadcasted_iota(jnp.int32, (tt, 1), 0)
    outs = []
    for gi, win in enumerate(POOL_WINDOWS):
        cols = slice(gi * POOL_GC, (gi + 1) * POOL_GC)
        s = ext[:, cols]
        sh = 1
        while sh < win:
            s = s + _shift_down(s, sh)
            sh *= 2
        count = jnp.minimum(t + 1, win).astype(F32)
        outs.append(s[POOL_HALO:] / count - ext[POOL_HALO:, cols])
    return outs


def _prev_halo_spec(tt, halo, width):
    per = tt // halo
    return pl.BlockSpec((halo, width), lambda i: (jnp.maximum(i * per - 1, 0), 0))


def _pool_fwd_call(h0, g_mix, pool_w, pool_b, pool_scale, g_ffn):
    t = h0.shape[0]
    tt = _tile(t, ROW_TILE)

    def body(h_ref, halo_ref, gm_ref, w_ref, b_ref, sc_ref, gf_ref, h1_ref, hn_ref):
        i = pl.program_id(0)
        x, gm = h_ref[...], gm_ref[...]
        top = jnp.where(i > 0, _rms(halo_ref[...], gm), 0.0)
        ext = jnp.concatenate([top, _rms(x, gm)], axis=0)
        ps = _pool_windows_causal(ext, i * tt)
        ys = [jnp.dot(p.astype(MXU_DTYPE), w_ref[gi], preferred_element_type=F32) for gi, p in enumerate(ps)]
        mix = (jnp.concatenate(ys, axis=1) + b_ref[...]) * sc_ref[...]
        h1 = x + mix
        h1_ref[...] = h1
        hn_ref[...] = _rms(h1, gf_ref[...]).astype(hn_ref.dtype)

    row = pl.BlockSpec((tt, D_MODEL), lambda i: (i, 0))
    vec = pl.BlockSpec((1, D_MODEL), lambda i: (0, 0))
    wsp = pl.BlockSpec((len(POOL_WINDOWS), POOL_GC, POOL_GC), lambda i: (0, 0, 0))
    return pl.pallas_call(
        body, name="pool_fwd", grid=(t // tt,),
        in_specs=[row, _prev_halo_spec(tt, POOL_HALO, D_MODEL), vec, wsp, vec, vec, vec], out_specs=[row, row],
        out_shape=[jax.ShapeDtypeStruct(h0.shape, F32), jax.ShapeDtypeStruct(h0.shape, MXU_DTYPE)],
        compiler_params=_params("parallel"))(h0, h0, g_mix, pool_w, pool_b, pool_scale, g_ffn)


def _pool_bwd_call(h0, dh1, g_mix, pool_w, pool_b, pool_scale):
    t = h0.shape[0]
    tt = _tile(t, ROW_TILE)
    nt = t // tt
    per = tt // POOL_HALO

    def body(h_ref, halo_ref, d_ref, dnext_ref, gm_ref, w_ref, b_ref, sc_ref, gx_ref, dw_ref, dv_ref):
        i = pl.program_id(0)
        x, gm, sc = h_ref[...], gm_ref[...], sc_ref[...]
        top = jnp.where(i > 0, _rms(halo_ref[...], gm), 0.0)
        ext = jnp.concatenate([top, _rms(x, gm)], axis=0)
        ps = _pool_windows_causal(ext, i * tt)

        dy = d_ref[...]
        dy_ext = jnp.concatenate([dy, jnp.where(i < nt - 1, dnext_ref[...], 0.0)], axis=0)
        dpre_ext = dy_ext * sc
        t_ext = i * tt + lax.broadcasted_iota(jnp.int32, (tt + POOL_HALO, 1), 0)

        @pl.when(i == 0)
        def _():
            dw_ref[...] = jnp.zeros_like(dw_ref)
            dv_ref[...] = jnp.zeros_like(dv_ref)

        dhn, ypre = [], []
        for gi, win in enumerate(POOL_WINDOWS):
            cols = slice(gi * POOL_GC, (gi + 1) * POOL_GC)
            w = w_ref[gi]
            p16 = ps[gi].astype(MXU_DTYPE)
            dpre16 = dpre_ext[:, cols].astype(MXU_DTYPE)
            ypre.append(jnp.dot(p16, w, preferred_element_type=F32))
            dw_ref[gi] += lax.dot_general(p16, dpre16[:tt], (((0,), (0,)), ((), ())), preferred_element_type=F32)
            dp_ext = lax.dot_general(dpre16, w, (((1,), (1,)), ((), ())), preferred_element_type=F32)
            s = dp_ext / jnp.minimum(t_ext + 1, win).astype(F32)
            sh = 1
            while sh < win:
                s = s + _shift_up(s, sh)
                sh *= 2
            dhn.append(s[:tt] - dp_ext[:tt])
        dhn = jnp.concatenate(dhn, axis=1)
        ypre = jnp.concatenate(ypre, axis=1) + b_ref[...]
        dx, dgt = _rms_bwd(x, gm, dhn)
        gx_ref[...] = dy + dx
        dv_ref[0:1, :] += jnp.sum(dpre_ext[:tt], axis=0, keepdims=True)
        dv_ref[1:2, :] += jnp.sum(dy * ypre, axis=0, keepdims=True)
        dv_ref[2:3, :] += jnp.sum(dgt, axis=0, keepdims=True)

    row = pl.BlockSpec((tt, D_MODEL), lambda i: (i, 0))
    vec = pl.BlockSpec((1, D_MODEL), lambda i: (0, 0))
    nxt = pl.BlockSpec((POOL_HALO, D_MODEL), lambda i: (jnp.minimum((i + 1) * per, t // POOL_HALO - 1), 0))
    wsp = pl.BlockSpec((len(POOL_WINDOWS), POOL_GC, POOL_GC), lambda i: (0, 0, 0))
    return pl.pallas_call(
        body, name="pool_bwd", grid=(nt,),
        in_specs=[row, _prev_halo_spec(tt, POOL_HALO, D_MODEL), row, nxt, vec, wsp, vec, vec],
        out_specs=[row, wsp, pl.BlockSpec((8, D_MODEL), lambda i: (0, 0))],
        out_shape=[jax.ShapeDtypeStruct(h0.shape, F32),
                   jax.ShapeDtypeStruct((len(POOL_WINDOWS), POOL_GC, POOL_GC), F32),
                   jax.ShapeDtypeStruct((8, D_MODEL), F32)],
        compiler_params=_params("arbitrary"))(h0, h0, dh1, dh1, g_mix, pool_w, pool_b, pool_scale)


CONV_COLS = 256


def _causal_conv(u_ext, w, b):
    return w[0:1] * _shift_down(u_ext, 2) + w[1:2] * _shift_down(u_ext, 1) + w[2:3] * u_ext + b


def _conv_glu_fwd_call(u, conv_w, conv_b, name):
    _, t, f = u.shape
    tt, tc = _tile(t, ROW_TILE), _tile(f, CONV_COLS)
    per = tt // CONV_HALO

    def body(u_ref, up_ref, w_ref, b_ref, z_ref):
        i = pl.program_id(0)
        c = []
        for hf in range(2):
            top = jnp.where(i > 0, up_ref[hf], 0.0)
            ext = jnp.concatenate([top, u_ref[hf]], axis=0)
            c.append(_causal_conv(ext, w_ref[:, hf, :], b_ref[hf:hf + 1, :])[CONV_HALO:])
        z_ref[...] = (c[0] * c[1] * _sigmoid(c[1])).astype(z_ref.dtype)

    return pl.pallas_call(
        body, name=name, grid=(t // tt, f // tc),
        in_specs=[pl.BlockSpec((2, tt, tc), lambda i, j: (0, i, j)),
                  pl.BlockSpec((2, CONV_HALO, tc), lambda i, j: (0, jnp.maximum(i * per - 1, 0), j)),
                  pl.BlockSpec((3, 2, tc), lambda i, j: (0, 0, j)), pl.BlockSpec((2, tc), lambda i, j: (0, j))],
        out_specs=pl.BlockSpec((tt, tc), lambda i, j: (i, j)),
        out_shape=jax.ShapeDtypeStruct((t, f), MXU_DTYPE),
        compiler_params=_params("parallel", "parallel"))(u, u, conv_w, conv_b)


def _conv_glu_bwd_call(u, dz, conv_w, conv_b, name):
    _, t, f = u.shape
    tt, tc = _tile(t, ROW_TILE), _tile(f, CONV_COLS)
    nt = t // tt
    per = tt // CONV_HALO

    def body(u_ref, up_ref, un_ref, dz_ref, dzn_ref, w_ref, b_ref, du_ref, dw_ref, db_ref):
        i = pl.program_id(1)
        u_ext, c = [], []
        for hf in range(2):
            top = jnp.where(i > 0, up_ref[hf], 0.0)
            ext = jnp.concatenate([top, u_ref[hf], un_ref[hf]], axis=0)
            u_ext.append(ext)
            c.append(_causal_conv(ext, w_ref[:, hf, :], b_ref[hf:hf + 1, :]))
        a, g = c
        sg = _sigmoid(g)
        dz_next = jnp.where(i < nt - 1, dzn_ref[...].astype(F32), 0.0)
        dz_ext = jnp.concatenate([jnp.zeros((CONV_HALO, tc), F32), dz_ref[...].astype(F32), dz_next], axis=0)
        dc = [dz_ext * g * sg, dz_ext * a * sg * (1.0 + g * (1.0 - sg))]

        @pl.when(i == 0)
        def _():
            dw_ref[...] = jnp.zeros_like(dw_ref)
            db_ref[...] = jnp.zeros_like(db_ref)

        main = slice(CONV_HALO, CONV_HALO + tt)
        for hf in range(2):
            w = w_ref[:, hf, :]
            d = dc[hf]
            du = w[2:3] * d + w[1:2] * _shift_up(d, 1) + w[0:1] * _shift_up(d, 2)
            du_ref[hf] = du[main].astype(du_ref.dtype)
            dm = d[main]
            ue = u_ext[hf]
            dw_ref[0:1, hf, :] += jnp.sum(dm * _shift_down(ue, 2)[main], axis=0, keepdims=True)
            dw_ref[1:2, hf, :] += jnp.sum(dm * _shift_down(ue, 1)[main], axis=0, keepdims=True)
            dw_ref[2:3, hf, :] += jnp.sum(dm * ue[main], axis=0, keepdims=True)
            db_ref[hf:hf + 1, :] += jnp.sum(dm, axis=0, keepdims=True)

    last = t // CONV_HALO - 1
    return pl.pallas_call(
        body, name=name, grid=(f // tc, nt),
        in_specs=[pl.BlockSpec((2, tt, tc), lambda j, i: (0, i, j)),
                  pl.BlockSpec((2, CONV_HALO, tc), lambda j, i: (0, jnp.maximum(i * per - 1, 0), j)),
                  pl.BlockSpec((2, CONV_HALO, tc), lambda j, i: (0, jnp.minimum((i + 1) * per, last), j)),
                  pl.BlockSpec((tt, tc), lambda j, i: (i, j)),
                  pl.BlockSpec((CONV_HALO, tc), lambda j, i: (jnp.minimum((i + 1) * per, last), j)),
                  pl.BlockSpec((3, 2, tc), lambda j, i: (0, 0, j)), pl.BlockSpec((2, tc), lambda j, i: (0, j))],
        out_specs=[pl.BlockSpec((2, tt, tc), lambda j, i: (0, i, j)),
                   pl.BlockSpec((3, 2, tc), lambda j, i: (0, 0, j)), pl.BlockSpec((2, tc), lambda j, i: (0, j))],
        out_shape=[jax.ShapeDtypeStruct(u.shape, MXU_DTYPE), jax.ShapeDtypeStruct((3, 2, f), F32),
                   jax.ShapeDtypeStruct((2, f), F32)],
        compiler_params=_params("parallel", "arbitrary"))(u, u, u, dz, dz, conv_w, conv_b)


def _kv_even_odd(band, k):
    pair, parity = divmod(k, 2)
    blk = band[:, 128 * pair:128 * (pair + 1)].astype(F32)
    lane = lax.broadcasted_iota(jnp.int32, blk.shape, 1)
    own = jnp.where((lane >= HEAD_DIM) == (parity == 1), blk, 0.0)
    swapped = pltpu.roll(own, HEAD_DIM, axis=1)
    even, odd = (own, swapped) if parity == 0 else (swapped, own)
    return even.astype(MXU_DTYPE), odd.astype(MXU_DTYPE)


def _stack_pairs(ref, k):
    return jnp.concatenate([ref[:, 256 * k:256 * k + 128], ref[:, 256 * k + 128:256 * k + 256]], axis=0)


_GROUP_ORDER = (0, 2, 1, 3)


def _attn_probs(qs, k_even, k_odd, k, n, sink_ref):
    nt_dims = (((1,), (1,)), ((), ()))
    s = jnp.concatenate([lax.dot_general(qs, k_even, nt_dims, preferred_element_type=F32),
                         lax.dot_general(qs, k_odd, nt_dims, preferred_element_type=F32)], axis=0) * ATT_SCALE
    row = lax.broadcasted_iota(jnp.int32, s.shape, 0)
    col = lax.broadcasted_iota(jnp.int32, s.shape, 1)
    dist = (row & (ATT_BLOCK - 1)) + ATT_BLOCK - col
    ok = (dist >= 0) & (dist < ATT_BLOCK) & ((col >= ATT_BLOCK) | (n > 0))
    rb = lax.broadcasted_iota(jnp.int32, (s.shape[0], 1), 0) // ATT_BLOCK
    slope = jnp.zeros((s.shape[0], 1), F32)
    sink = jnp.zeros((s.shape[0], 1), F32)
    for r, gq in enumerate(_GROUP_ORDER):
        head = 4 * k + gq
        slope = jnp.where(rb == r, ALIBI_SLOPES[head], slope)
        sink = jnp.where(rb == r, sink_ref[head], sink)
    s = jnp.where(ok, s - slope * dist.astype(F32), -jnp.inf)
    m = jnp.maximum(jnp.max(s, axis=-1, keepdims=True), sink)
    p = jnp.exp(s - m)
    es = jnp.exp(sink - m)
    return p, jnp.sum(p, axis=-1, keepdims=True) + es, es


def _attn_specs(with_do):
    prev = lambda n: jnp.maximum(n - 1, 0)
    specs = [pl.BlockSpec(memory_space=pltpu.SMEM),
             pl.BlockSpec((ATT_BLOCK, D_MODEL), lambda n: (n, 0)),
             pl.BlockSpec((ATT_BLOCK, 256), lambda n: (prev(n), 4)), pl.BlockSpec((ATT_BLOCK, 256), lambda n: (n, 4)),
             pl.BlockSpec((ATT_BLOCK, 256), lambda n: (prev(n), 5)), pl.BlockSpec((ATT_BLOCK, 256), lambda n: (n, 5))]
    if with_do:
        specs.append(pl.BlockSpec((ATT_BLOCK, D_MODEL), lambda n: (n, 0)))
    return specs


def _attn_fwd_call(qkv, sinks):
    t = qkv.shape[0]

    def body(sink_ref, q_ref, kp_ref, kc_ref, vp_ref, vc_ref, o_ref):
        n = pl.program_id(0)
        kband = jnp.concatenate([kp_ref[...], kc_ref[...]], axis=0)
        vband = jnp.concatenate([vp_ref[...], vc_ref[...]], axis=0)
        for k in range(N_KV_HEADS):
            k_even, k_odd = _kv_even_odd(kband, k)
            v_even, v_odd = _kv_even_odd(vband, k)
            p, denom, _ = _attn_probs(_stack_pairs(q_ref, k), k_even, k_odd, k, n, sink_ref)
            probs = (p / denom).astype(MXU_DTYPE)
            o = (jnp.dot(probs[:256], v_even, preferred_element_type=F32)
                 + jnp.dot(probs[256:], v_odd, preferred_element_type=F32))
            o_ref[:, 256 * k:256 * k + 128] = o[:128].astype(o_ref.dtype)
            o_ref[:, 256 * k + 128:256 * k + 256] = o[128:].astype(o_ref.dtype)

    return pl.pallas_call(
        body, name="attn_fwd", grid=(t // ATT_BLOCK,), in_specs=_attn_specs(False),
        out_specs=pl.BlockSpec((ATT_BLOCK, D_MODEL), lambda n: (n, 0)),
        out_shape=jax.ShapeDtypeStruct((t, D_MODEL), MXU_DTYPE),
        compiler_params=_params("parallel"))(sinks, qkv, qkv, qkv, qkv, qkv)


def _attn_bwd_call(qkv, sinks, do):
    t = qkv.shape[0]
    nb = t // ATT_BLOCK
    tn_dims = (((0,), (0,)), ((), ()))
    nt_dims = (((1,), (1,)), ((), ()))

    def to_native(even_part, odd_part, parity):
        lane = lax.broadcasted_iota(jnp.int32, even_part.shape, 1)
        lo = lane < HEAD_DIM
        e, o = jnp.where(lo, even_part, 0.0), jnp.where(lo, 0.0, odd_part)
        if parity == 0:
            return e + pltpu.roll(o, HEAD_DIM, axis=1)
        return pltpu.roll(e, HEAD_DIM, axis=1) + o

    def body(sink_ref, q_ref, kp_ref, kc_ref, vp_ref, vc_ref, do_ref, dq_ref, dk_ref, dv_ref, db_ref, dsink_ref):
        n = pl.program_id(0)

        @pl.when(n == 0)
        def _():
            dk_ref[...] = jnp.zeros_like(dk_ref)
            dv_ref[...] = jnp.zeros_like(dv_ref)
            db_ref[...] = jnp.zeros_like(db_ref)
            dsink_ref[...] = jnp.zeros_like(dsink_ref)

        kband = jnp.concatenate([kp_ref[...], kc_ref[...]], axis=0)
        vband = jnp.concatenate([vp_ref[...], vc_ref[...]], axis=0)
        dk_pairs = [jnp.zeros((2 * ATT_BLOCK, 128), F32), jnp.zeros((2 * ATT_BLOCK, 128), F32)]
        dv_pairs = [jnp.zeros((2 * ATT_BLOCK, 128), F32), jnp.zeros((2 * ATT_BLOCK, 128), F32)]
        sink_lane = lax.broadcasted_iota(jnp.int32, (8, 128), 1)
        sink_row = lax.broadcasted_iota(jnp.int32, (8, 128), 0)
        dsink = jnp.zeros((8, 128), F32)
        for k in range(N_KV_HEADS):
            k_even, k_odd = _kv_even_odd(kband, k)
            v_even, v_odd = _kv_even_odd(vband, k)
            qs = _stack_pairs(q_ref, k)
            dos = _stack_pairs(do_ref, k)
            p, denom, es = _attn_probs(qs, k_even, k_odd, k, n, sink_ref)
            probs = p / denom
            dp = jnp.concatenate([lax.dot_general(dos, v_even, nt_dims, preferred_element_type=F32),
                                  lax.dot_general(dos, v_odd, nt_dims, preferred_element_type=F32)], axis=0)
            delta = jnp.sum(probs * dp, axis=-1, keepdims=True)
            ds16 = (probs * (dp - delta) * ATT_SCALE).astype(MXU_DTYPE)
            p16 = probs.astype(MXU_DTYPE)
            dsink_rows = -(es / denom) * delta
            for r, gq in enumerate(_GROUP_ORDER):
                tot = jnp.sum(dsink_rows[ATT_BLOCK * r:ATT_BLOCK * (r + 1)], axis=0, keepdims=True)
                dsink = dsink + jnp.where((sink_lane == 4 * k + gq) & (sink_row == 0), tot, 0.0)
            dqs = (jnp.dot(ds16[:256], k_even, preferred_element_type=F32)
                   + jnp.dot(ds16[256:], k_odd, preferred_element_type=F32))
            dq_ref[:, 256 * k:256 * k + 128] = dqs[:128].astype(dq_ref.dtype)
            dq_ref[:, 256 * k + 128:256 * k + 256] = dqs[128:].astype(dq_ref.dtype)
            db_ref[:, 256 * k:256 * k + 128] += jnp.sum(dqs[:128], axis=0, keepdims=True)
            db_ref[:, 256 * k + 128:256 * k + 256] += jnp.sum(dqs[128:], axis=0, keepdims=True)
            pair, parity = divmod(k, 2)
            dk_pairs[pair] = dk_pairs[pair] + to_native(
                lax.dot_general(ds16[:256], qs, tn_dims, preferred_element_type=F32),
                lax.dot_general(ds16[256:], qs, tn_dims, preferred_element_type=F32), parity)
            dv_pairs[pair] = dv_pairs[pair] + to_native(
                lax.dot_general(p16[:256], dos, tn_dims, preferred_element_type=F32),
                lax.dot_general(p16[256:], dos, tn_dims, preferred_element_type=F32), parity)
        dsink_ref[...] += dsink
        dk_band = jnp.concatenate(dk_pairs, axis=1)
        dv_band = jnp.concatenate(dv_pairs, axis=1)
        cur = pl.multiple_of(n * ATT_BLOCK, ATT_BLOCK)
        dk_ref[pl.ds(cur, ATT_BLOCK), :] += dk_band[ATT_BLOCK:]
        dv_ref[pl.ds(cur, ATT_BLOCK), :] += dv_band[ATT_BLOCK:]

        @pl.when(n > 0)
        def _():
            prv = pl.multiple_of((n - 1) * ATT_BLOCK, ATT_BLOCK)
            dk_ref[pl.ds(prv, ATT_BLOCK), :] += dk_band[:ATT_BLOCK]
            dv_ref[pl.ds(prv, ATT_BLOCK), :] += dv_band[:ATT_BLOCK]

        @pl.when(n == nb - 1)
        def _():
            db_ref[:, 1024:1280] = jnp.sum(dk_ref[...], axis=0, keepdims=True)
            db_ref[:, 1280:1536] = jnp.sum(dv_ref[...], axis=0, keepdims=True)

    whole = lambda shape: pl.BlockSpec(shape, lambda n: (0, 0))
    return pl.pallas_call(
        body, name="attn_bwd", grid=(nb,), in_specs=_attn_specs(True),
        out_specs=[pl.BlockSpec((ATT_BLOCK, D_MODEL), lambda n: (n, 0)), whole((t, 256)), whole((t, 256)),
                   whole((1, QKV_DIM)), whole((8, 128))],
        out_shape=[jax.ShapeDtypeStruct((t, D_MODEL), MXU_DTYPE), jax.ShapeDtypeStruct((t, 256), F32),
                   jax.ShapeDtypeStruct((t, 256), F32), jax.ShapeDtypeStruct((1, QKV_DIM), F32),
                   jax.ShapeDtypeStruct((8, 128), F32)],
        compiler_params=_params("arbitrary"))(sinks, qkv, qkv, qkv, qkv, qkv, do)


FF_TILE = 1408


def _ffn_fwd(hn, h, w_in, w_out, conv_w, conv_b, layer):
    t = hn.shape[0]
    tm = _tile(t, 512)
    halves = D_FF // FF_TILE
    u = _matmul(
        hn, w_in, mode="nn", name=f"ffn{layer}_in", grid=(t // tm, 2 * halves, 1),
        a_spec=pl.BlockSpec((tm, D_MODEL), lambda i, j, k: (i, 0)),
        b_spec=pl.BlockSpec((None, D_MODEL, FF_TILE), lambda i, j, k: (layer, 0, j)),
        o_spec=pl.BlockSpec((None, tm, FF_TILE), lambda i, j, k: (j // halves, i, j % halves)),
        out_shape=jax.ShapeDtypeStruct((2, t, D_FF), F32))
    z = _conv_glu_fwd_call(u, conv_w, conv_b, f"ffn{layer}_glu")
    h_out = _mm_nn(z, w_out, f"ffn{layer}_out", b_lead=layer, residual=h, tk=FF_TILE)
    return h_out, u, z


def _ffn_bwd(dh, hn, u, z, w_in, w_out, conv_w, conv_b, layer, g_in_acc, g_out_acc):
    t = hn.shape[0]
    halves = D_FF // FF_TILE
    tm = _tile(t, 1024)
    dz = _mm_nt(dh, w_out, f"ffn{layer}_dz", b_lead=layer, out_dtype=MXU_DTYPE, tn=FF_TILE)
    g_out = _mm_tn(z, dh, f"ffn{layer}_dwout", lead=layer, n_lead=2, into=g_out_acc, tm=FF_TILE)
    du, dcw, dcb = _conv_glu_bwd_call(u, dz, conv_w, conv_b, f"ffn{layer}_dglu")
    g_in = _matmul(
        hn, du, mode="tn", name=f"ffn{layer}_dwin", grid=(1, 2 * halves, t // tm),
        a_spec=pl.BlockSpec((tm, D_MODEL), lambda i, j, k: (k, 0)),
        b_spec=pl.BlockSpec((None, tm, FF_TILE), lambda i, j, k: (j // halves, k, j % halves)),
        o_spec=pl.BlockSpec((None, D_MODEL, FF_TILE), lambda i, j, k: (layer, 0, j)),
        out_shape=jax.ShapeDtypeStruct((2, D_MODEL, 2 * D_FF), F32), into=g_in_acc)
    dhn = _matmul(
        du, w_in, mode="nt", name=f"ffn{layer}_dhn", grid=(t // tm, 1, 2 * halves),
        a_spec=pl.BlockSpec((None, tm, FF_TILE), lambda i, j, k: (k // halves, i, k % halves)),
        b_spec=pl.BlockSpec((None, D_MODEL, FF_TILE), lambda i, j, k: (layer, 0, k)),
        o_spec=pl.BlockSpec((tm, D_MODEL), lambda i, j, k: (i, 0)),
        out_shape=jax.ShapeDtypeStruct((t, D_MODEL), F32))
    return dhn, g_in, g_out, dcw, dcb


def _local_step(x, target, w, *, norm_mix, norm_ffn, norm_f, pool_b, pool_scale, b_qkv, sinks, conv_w, conv_b):
    cw = [conv_w[l].reshape(3, 2, D_FF) for l in range(2)]
    cb = [conv_b[l].reshape(2, D_FF) for l in range(2)]
    gm = [norm_mix[l:l + 1] for l in range(2)]
    gf = [norm_ffn[l:l + 1] for l in range(2)]

    h1, hn_f0 = _pool_fwd_call(x, gm[0], w["pool_w"], pool_b, pool_scale, gf[0])
    h2, u0, z0 = _ffn_fwd(hn_f0, h1, w["w_in"], w["w_out"], cw[0], cb[0], 0)
    hn_a = _rms_fwd_call(h2, gm[1], "rms_attn", MXU_DTYPE)
    qkv = _mm_nn(hn_a, w["w_qkv"], "attn_qkv", bias=b_qkv, out_dtype=MXU_DTYPE, tn=768)
    o = _attn_fwd_call(qkv, sinks)
    h3 = _mm_nn(o, w["w_o"], "attn_out", residual=h2)
    hn_f1 = _rms_fwd_call(h3, gf[1], "rms_ffn1", MXU_DTYPE)
    h4, u1, z1 = _ffn_fwd(hn_f1, h3, w["w_in"], w["w_out"], cw[1], cb[1], 1)
    loss, dh, d_norm_f = _loss_call(h4, norm_f, target)

    dhn, g_in, g_out, dcw1, dcb1 = _ffn_bwd(dh, hn_f1, u1, z1, w["w_in"], w["w_out"], cw[1], cb[1], 1, None, None)
    dh, d_gf1 = _rms_bwd_call(h3, gf[1], dhn, dh, "rms_ffn1_bwd")
    do = _mm_nt(dh, w["w_o"], "attn_do", out_dtype=MXU_DTYPE)
    g_o = _mm_tn(o, dh, "attn_dwo")
    dq, dk, dv, d_bqkv, d_sinks = _attn_bwd_call(qkv, sinks, do)
    dqkv = jnp.concatenate([dq, dk.astype(MXU_DTYPE), dv.astype(MXU_DTYPE)], axis=1)
    g_qkv = _mm_tn(hn_a, dqkv, "attn_dwqkv", tn=768)
    dhn = _mm_nt(dqkv, w["w_qkv"], "attn_dhn", tk=768)
    dh, d_gm1 = _rms_bwd_call(h2, gm[1], dhn, dh, "rms_attn_bwd")
    dhn, g_in, g_out, dcw0, dcb0 = _ffn_bwd(dh, hn_f0, u0, z0, w["w_in"], w["w_out"], cw[0], cb[0], 0, g_in, g_out)
    dh, d_gf0 = _rms_bwd_call(h1, gf[0], dhn, dh, "rms_ffn0_bwd")
    grad_x, g_pool, dvec = _pool_bwd_call(x, dh, gm[0], w["pool_w"], pool_b, pool_scale)

    grads = {
        "w_in": g_in, "w_out": g_out, "w_qkv": g_qkv, "w_o": g_o, "pool_w": g_pool,
        "pool_b": dvec[0:1], "pool_scale": dvec[1:2], "b_qkv": d_bqkv, "sinks": d_sinks[0:1, :N_HEADS],
        "norm_mix": jnp.concatenate([dvec[2:3], d_gm1], axis=0), "norm_ffn": jnp.concatenate([d_gf0, d_gf1], axis=0),
        "conv_w": jnp.stack([dcw0.reshape(3, 2 * D_FF), dcw1.reshape(3, 2 * D_FF)]),
        "conv_b": jnp.stack([dcb0.reshape(2 * D_FF), dcb1.reshape(2 * D_FF)]), "norm_f": d_norm_f,
    }
    return loss, grad_x, grads


W_IN_COLS = 2 * D_FF // N_CHIPS
W_OUT_ROWS = D_FF // N_CHIPS
QKV_COLS = QKV_DIM // N_CHIPS
W_O_ROWS = D_MODEL // N_CHIPS
POOL_ROWS = POOL_GC // N_CHIPS
BIG = ("w_in", "w_out", "w_qkv", "w_o", "pool_w")


def _mesh_place():
    x, y, c = lax.axis_index("x"), lax.axis_index("y"), lax.axis_index("c")
    chips = [(1 - x, y), (x, 1 - y), (1 - x, 1 - y)]
    return x, y, c, chips


def _col(k, width):
    return pl.ds(pl.multiple_of(k * width, 128), width)


def _row(k, height):
    return pl.ds(pl.multiple_of(k * height, 16), height)


def _full_piece(name, ref, k, h):
    if name == "w_in":
        return ref.at[h, :, _col(k, W_IN_COLS)]
    if name == "w_out":
        return ref.at[h, _row(k, W_OUT_ROWS), :]
    if name == "w_qkv":
        return ref.at[_row(h, D_MODEL // 2), _col(k, QKV_COLS)]
    if name == "w_o":
        return ref.at[_row(2 * k + h, W_O_ROWS // 2), :]
    return ref.at[pl.ds(2 * h, 2), _row(k, POOL_ROWS), :]


def _shard_half(name, ref, h):
    if name in ("w_in", "w_out"):
        return ref.at[h]
    if name == "w_qkv":
        return ref.at[_row(h, D_MODEL // 2), :]
    if name == "w_o":
        return ref.at[_row(h, W_O_ROWS // 2), :]
    return ref.at[pl.ds(2 * h, 2)]


def _full_shard(name, ref, k):
    if name == "w_in":
        return ref.at[:, :, _col(k, W_IN_COLS)]
    if name == "w_out":
        return ref.at[:, _row(k, W_OUT_ROWS), :]
    if name == "w_qkv":
        return ref.at[:, _col(k, QKV_COLS)]
    if name == "w_o":
        return ref.at[_row(k, W_O_ROWS), :]
    return ref.at[:, _row(k, POOL_ROWS), :]


FULL_SHAPES = {"w_in": (2, D_MODEL, 2 * D_FF), "w_out": (2, D_FF, D_MODEL), "w_qkv": (D_MODEL, QKV_DIM),
               "w_o": (D_MODEL, D_MODEL), "pool_w": (4, POOL_GC, POOL_GC)}
SHARD_SHAPES = {"w_in": (2, D_MODEL, W_IN_COLS), "w_out": (2, W_OUT_ROWS, D_MODEL), "w_qkv": (D_MODEL, QKV_COLS),
                "w_o": (W_O_ROWS, D_MODEL), "pool_w": (4, POOL_ROWS, POOL_GC)}
HALF_SHAPES = {"w_in": (D_MODEL, W_IN_COLS), "w_out": (W_OUT_ROWS, D_MODEL), "w_qkv": (D_MODEL // 2, QKV_COLS),
               "w_o": (W_O_ROWS // 2, D_MODEL), "pool_w": (2, POOL_ROWS, POOL_GC)}


def _gather_weights(shards, small):
    nb = len(BIG)

    def body(*refs):
        ins, small_in = refs[:nb], refs[nb]
        outs, small_out = refs[nb + 1:2 * nb + 1], refs[2 * nb + 1]
        send_sems, recv_sems, local_sems, small_send, small_recv = refs[2 * nb + 2:]
        x, y, c, chips = _mesh_place()
        me = 2 * x + y
        sibling = (x, y, 1 - c)

        def remote(src, dst, t, slot, dev):
            return pltpu.make_async_remote_copy(src_ref=src, dst_ref=dst, send_sem=send_sems.at[t, slot],
                                                recv_sem=recv_sems.at[t, slot], device_id=dev, device_id_type=MESH)

        local = [pltpu.make_async_copy(ins[t], _full_shard(nm, outs[t], me), local_sems.at[t])
                 for t, nm in enumerate(BIG)]
        local.append(pltpu.make_async_copy(small_in, small_out.at[me], local_sems.at[nb]))
        for cp in local:
            cp.start()
        sends = []
        for j, chip in enumerate(chips):
            for t, nm in enumerate(BIG):
                sends.append(remote(_shard_half(nm, ins[t], c), _full_piece(nm, outs[t], me, c), t, j, (*chip, c)))
            sends.append(pltpu.make_async_remote_copy(
                src_ref=small_in, dst_ref=small_out.at[me], send_sem=small_send.at[j], recv_sem=small_recv.at[j],
                device_id=(*chip, c), device_id_type=MESH))
        for cp in sends:
            cp.start()
        for j, (cx, cy) in enumerate(chips):
            kj = 2 * cx + cy
            for t, nm in enumerate(BIG):
                got = _full_piece(nm, outs[t], kj, c)
                remote(got, got, t, j, sibling).wait_recv()
                fwd = remote(got, got, t, 3 + j, sibling)
                fwd.start()
                sends.append(fwd)
        for j, (cx, cy) in enumerate(chips):
            kj = 2 * cx + cy
            for t, nm in enumerate(BIG):
                theirs = _full_piece(nm, outs[t], kj, 1 - c)
                remote(theirs, theirs, t, 3 + j, sibling).wait_recv()
            got = small_out.at[kj]
            pltpu.make_async_remote_copy(src_ref=got, dst_ref=got, send_sem=small_send.at[j],
                                         recv_sem=small_recv.at[j], device_id=sibling, device_id_type=MESH).wait_recv()
        for cp in sends:
            cp.wait_send()
        for cp in local:
            cp.wait()

    out_shape = [jax.ShapeDtypeStruct(FULL_SHAPES[nm], WIRE_DTYPE) for nm in BIG]
    out_shape.append(jax.ShapeDtypeStruct((N_CHIPS,) + small.shape, F32))
    res = pl.pallas_call(
        body, name="gather_weights", in_specs=[ANY] * (nb + 1), out_specs=[ANY] * (nb + 1), out_shape=out_shape,
        scratch_shapes=[pltpu.SemaphoreType.DMA((nb, 6)), pltpu.SemaphoreType.DMA((nb, 6)),
                        pltpu.SemaphoreType.DMA((nb + 1,)), pltpu.SemaphoreType.DMA((3,)),
                        pltpu.SemaphoreType.DMA((3,))],
    )(*[shards[nm] for nm in BIG], small)
    return dict(zip(BIG, res[:nb])), res[nb]


def _other_half_src(name, ref, h):
    if name in ("w_in", "w_out"):
        return ref.at[h]
    if name == "w_qkv":
        return ref.at[_row(h, D_MODEL // 2), :]
    if name == "w_o":
        return ref.at[:, pl.ds(h, 1)]
    return ref.at[pl.ds(2 * h, 2)]


SLAB_SHAPES = {"w_in": (D_MODEL, 2 * D_FF), "w_out": (D_FF, D_MODEL), "w_qkv": (D_MODEL // 2, QKV_DIM),
               "w_o": (N_CHIPS, 1, W_O_ROWS // 2, D_MODEL), "pool_w": (2, POOL_GC, POOL_GC)}


def _pair_exchange(grads):
    nb = len(BIG)

    def body(*refs):
        ins, outs = refs[:nb], refs[nb:2 * nb]
        send_sems, recv_sems = refs[2 * nb:]
        x, y, c, _ = _mesh_place()
        copies = [pltpu.make_async_remote_copy(
            src_ref=_other_half_src(nm, ins[t], 1 - c), dst_ref=outs[t], send_sem=send_sems.at[t],
            recv_sem=recv_sems.at[t], device_id=(x, y, 1 - c), device_id_type=MESH) for t, nm in enumerate(BIG)]
        for cp in copies:
            cp.start()
        for cp in copies:
            cp.wait()

    res = pl.pallas_call(
        body, name="pair_exchange", in_specs=[ANY] * nb, out_specs=[ANY] * nb,
        out_shape=[jax.ShapeDtypeStruct(SLAB_SHAPES[nm], F32) for nm in BIG],
        scratch_shapes=[pltpu.SemaphoreType.DMA((nb,)), pltpu.SemaphoreType.DMA((nb,))],
    )(*[grads[nm] for nm in BIG])
    return dict(zip(BIG, res))


def _pair_sum(name, grad, slab, c_arr):
    shape = SLAB_SHAPES[name]
    if name == "w_in":
        blk = (256, 2 * D_FF)
        grid = (D_MODEL // 256,)
        g_spec = pl.BlockSpec((None,) + blk, lambda i, c: (c[0], i, 0))
        s_spec = pl.BlockSpec(blk, lambda i, c: (i, 0))
    elif name == "w_out":
        blk = (W_OUT_ROWS, D_MODEL)
        grid = (N_CHIPS,)
        g_spec = pl.BlockSpec((None,) + blk, lambda i, c: (c[0], i, 0))
        s_spec = pl.BlockSpec(blk, lambda i, c: (i, 0))
    elif name == "w_qkv":
        grid = (1,)
        g_spec = pl.BlockSpec(shape, lambda i, c: (c[0], 0))
        s_spec = pl.BlockSpec(shape, lambda i, c: (0, 0))
    elif name == "w_o":
        grid = (1,)
        g_spec = pl.BlockSpec(shape, lambda i, c: (0, c[0], 0, 0))
        s_spec = pl.BlockSpec(shape, lambda i, c: (0, 0, 0, 0))
    else:
        grid = (1,)
        g_spec = pl.BlockSpec(shape, lambda i, c: (c[0], 0, 0))
        s_spec = pl.BlockSpec(shape, lambda i, c: (0, 0, 0))

    def body(c_ref, g_ref, s_ref, o_ref):
        o_ref[...] = (g_ref[...] + s_ref[...]).astype(o_ref.dtype)

    return pl.pallas_call(
        body, name=f"pair_sum_{name}",
        grid_spec=pltpu.PrefetchScalarGridSpec(num_scalar_prefetch=1, grid=grid, in_specs=[g_spec, s_spec],
                                               out_specs=s_spec),
        out_shape=jax.ShapeDtypeStruct(shape, WIRE_DTYPE), compiler_params=_params("parallel"),
    )(c_arr, grad, slab)


def _slab_piece(name, ref, k):
    if name == "w_in":
        return ref.at[:, _col(k, W_IN_COLS)]
    if name == "w_out":
        return ref.at[_row(k, W_OUT_ROWS), :]
    if name == "w_qkv":
        return ref.at[:, _col(k, QKV_COLS)]
    if name == "w_o":
        return ref.at[k, 0]
    return ref.at[:, _row(k, POOL_ROWS), :]


def _chip_scatter(slabs):
    nb = len(BIG)

    def body(*refs):
        ins, outs = refs[:nb], refs[nb:2 * nb]
        send_sems, recv_sems, local_sems = refs[2 * nb:]
        x, y, c, chips = _mesh_place()
        me = 2 * x + y
        local = [pltpu.make_async_copy(_slab_piece(nm, ins[t], me), outs[t].at[me], local_sems.at[t])
                 for t, nm in enumerate(BIG)]
        for cp in local:
            cp.start()
        copies = []
        for j, (cx, cy) in enumerate(chips):
            kj = 2 * cx + cy
            for t, nm in enumerate(BIG):
                copies.append((pltpu.make_async_remote_copy(
                    src_ref=_slab_piece(nm, ins[t], kj), dst_ref=outs[t].at[me], send_sem=send_sems.at[t, j],
                    recv_sem=recv_sems.at[t, j], device_id=(cx, cy, c), device_id_type=MESH), t, kj))
        for cp, _, _ in copies:
            cp.start()
        for j, (cx, cy) in enumerate(chips):
            kj = 2 * cx + cy
            for t, nm in enumerate(BIG):
                got = outs[t].at[kj]
                pltpu.make_async_remote_copy(src_ref=got, dst_ref=got, send_sem=send_sems.at[t, j],
                                             recv_sem=recv_sems.at[t, j], device_id=(cx, cy, c),
                                             device_id_type=MESH).wait_recv()
        for cp, _, _ in copies:
            cp.wait_send()
        for cp in local:
            cp.wait()

    res = pl.pallas_call(
        body, name="chip_scatter", in_specs=[ANY] * nb, out_specs=[ANY] * nb,
        out_shape=[jax.ShapeDtypeStruct((N_CHIPS,) + HALF_SHAPES[nm], WIRE_DTYPE) for nm in BIG],
        scratch_shapes=[pltpu.SemaphoreType.DMA((nb, 3)), pltpu.SemaphoreType.DMA((nb, 3)),
                        pltpu.SemaphoreType.DMA((nb,))],
    )(*[slabs[nm] for nm in BIG])
    return dict(zip(BIG, res))


def _sum_chips(name, parts):
    shape = HALF_SHAPES[name]
    nd = len(shape)
    if name == "w_in":
        blk, grid = (256, W_IN_COLS), (D_MODEL // 256,)
    else:
        blk, grid = shape, (1,)
    idx = lambda i: (i,) + (0,) * (nd - 1)

    def body(p_ref, o_ref):
        acc = p_ref[0].astype(F32)
        for k in range(1, N_CHIPS):
            acc = acc + p_ref[k].astype(F32)
        o_ref[...] = acc

    return pl.pallas_call(
        body, name=f"sum_chips_{name}", grid=grid,
        in_specs=[pl.BlockSpec((N_CHIPS,) + blk, lambda i: (0,) + idx(i))],
        out_specs=pl.BlockSpec(blk, idx), out_shape=jax.ShapeDtypeStruct(shape, F32),
        compiler_params=_params("parallel"))(parts)


def _pair_allgather(halves):
    nb = len(BIG)

    def body(*refs):
        ins, outs = refs[:nb], refs[nb:2 * nb]
        send_sems, recv_sems, local_sems = refs[2 * nb:]
        x, y, c, _ = _mesh_place()
        local = [pltpu.make_async_copy(ins[t], _shard_half(nm, outs[t], c), local_sems.at[t])
                 for t, nm in enumerate(BIG)]
        copies = [pltpu.make_async_remote_copy(
            src_ref=ins[t], dst_ref=_shard_half(nm, outs[t], c), send_sem=send_sems.at[t], recv_sem=recv_sems.at[t],
            device_id=(x, y, 1 - c), device_id_type=MESH) for t, nm in enumerate(BIG)]
        for cp in local + copies:
            cp.start()
        for t, nm in enumerate(BIG):
            theirs = _shard_half(nm, outs[t], 1 - c)
            pltpu.make_async_remote_copy(src_ref=theirs, dst_ref=theirs, send_sem=send_sems.at[t],
                                         recv_sem=recv_sems.at[t], device_id=(x, y, 1 - c),
                                         device_id_type=MESH).wait_recv()
        for cp in copies:
            cp.wait_send()
        for cp in local:
            cp.wait()

    res = pl.pallas_call(
        body, name="pair_allgather", in_specs=[ANY] * nb, out_specs=[ANY] * nb,
        out_shape=[jax.ShapeDtypeStruct(SHARD_SHAPES[nm], F32) for nm in BIG],
        scratch_shapes=[pltpu.SemaphoreType.DMA((nb,)), pltpu.SemaphoreType.DMA((nb,)),
                        pltpu.SemaphoreType.DMA((nb,))],
    )(*[halves[nm] for nm in BIG])
    return dict(zip(BIG, res))


N_DEV = 8


def _all_reduce_small(vec):
    def body(v_ref, o_ref, slots, send_sems, recv_sems):
        x, y, c, _ = _mesh_place()
        me = 4 * x + 2 * y + c
        slots[me] = v_ref[...]
        copies = []
        for d in range(1, N_DEV):
            px, py, pc = x ^ (d >> 2), y ^ ((d >> 1) & 1), c ^ (d & 1)
            copies.append(pltpu.make_async_remote_copy(
                src_ref=v_ref, dst_ref=slots.at[me], send_sem=send_sems.at[d - 1], recv_sem=recv_sems.at[d - 1],
                device_id=(px, py, pc), device_id_type=MESH))
        for cp in copies:
            cp.start()
        for d in range(1, N_DEV):
            peer = 4 * (x ^ (d >> 2)) + 2 * (y ^ ((d >> 1) & 1)) + (c ^ (d & 1))
            got = slots.at[peer]
            pltpu.make_async_remote_copy(src_ref=got, dst_ref=got, send_sem=send_sems.at[d - 1],
                                         recv_sem=recv_sems.at[d - 1], device_id=(x, y, c),
                                         device_id_type=MESH).wait_recv()
        for cp in copies:
            cp.wait_send()
        acc = slots[0]
        for d in range(1, N_DEV):
            acc = acc + slots[d]
        o_ref[...] = acc

    vm = pl.BlockSpec(memory_space=pltpu.VMEM)
    return pl.pallas_call(
        body, name="all_reduce_small", in_specs=[vm], out_specs=vm, out_shape=jax.ShapeDtypeStruct(vec.shape, F32),
        scratch_shapes=[pltpu.VMEM((N_DEV,) + vec.shape, F32), pltpu.SemaphoreType.DMA((N_DEV - 1,)),
                        pltpu.SemaphoreType.DMA((N_DEV - 1,))],
    )(vec)


def _adamw_call(w, g, m, v, name):
    shape = w.shape
    cols = shape[-1]
    rows = w.size // cols
    tr = rows
    for cand in (256, 128, 64, 32, 16, 8):
        if rows > cand and rows % cand == 0 and cand * cols * 4 <= 2 * 1024 * 1024:
            tr = cand
            break

    def body(w_ref, g_ref, m_ref, v_ref, d_ref, nm_ref, nv_ref):
        gv = g_ref[...]
        mn = ADAM_B1 * m_ref[...] + (1.0 - ADAM_B1) * gv
        vn = ADAM_B2 * v_ref[...] + (1.0 - ADAM_B2) * jnp.square(gv)
        m_hat = mn / (1.0 - ADAM_B1 ** ADAM_STEP)
        v_hat = vn / (1.0 - ADAM_B2 ** ADAM_STEP)
        d_ref[...] = -ADAM_LR * (m_hat / (jnp.sqrt(v_hat) + ADAM_EPS) + ADAM_WD * w_ref[...])
        nm_ref[...] = mn
        nv_ref[...] = vn

    spec = pl.BlockSpec((tr, cols), lambda i: (i, 0))
    flat = lambda a: a.reshape(rows, cols)
    outs = pl.pallas_call(
        body, name=name, grid=(rows // tr,), in_specs=[spec] * 4, out_specs=[spec] * 3,
        out_shape=[jax.ShapeDtypeStruct((rows, cols), F32)] * 3, compiler_params=_params("parallel"),
    )(flat(w), flat(g), flat(m), flat(v))
    return tuple(o.reshape(shape) for o in outs)


WEIGHTS = ("pool_w", "pool_b", "pool_scale", "attn_w_qkv", "attn_b_qkv", "attn_sinks", "attn_w_o", "norm_mix",
           "norm_ffn", "ffn_w_in", "ffn_conv_w", "ffn_conv_b", "ffn_w_out", "norm_f")
SMALL_GATHER = 128 * 71
_SMALL_SIZES = (("pool_b", 1024), ("pool_scale", 1024), ("b_qkv", 1536), ("sinks", 128), ("norm_mix", 2048),
                ("norm_ffn", 2048), ("conv_w", 6 * 2 * D_FF), ("conv_b", 4 * D_FF), ("norm_f", 1024))
SMALL_COLS = 6784


def _pack_small(grads):
    flat = jnp.concatenate([jnp.pad(grads[nm].reshape(-1), (0, size - grads[nm].size)) for nm, size in _SMALL_SIZES])
    return jnp.pad(flat, (0, 8 * SMALL_COLS - flat.size)).reshape(8, SMALL_COLS)


def _unpack_small(vec):
    flat = vec.reshape(-1)
    out, off = {}, 0
    for nm, size in _SMALL_SIZES:
        out[nm] = flat[off:off + size]
        off += size
    return out


def kernel(x, pool_w, pool_b, pool_scale, attn_w_qkv, attn_b_qkv, attn_sinks, attn_w_o, norm_mix, norm_ffn, ffn_w_in, ffn_conv_w, ffn_conv_b, ffn_w_out, norm_f, loss_target, m_pool_w, m_pool_b, m_pool_scale, m_attn_w_qkv, m_attn_b_qkv, m_attn_sinks, m_attn_w_o, m_norm_mix, m_norm_ffn, m_ffn_w_in, m_ffn_conv_w, m_ffn_conv_b, m_ffn_w_out, m_norm_f, v_pool_w, v_pool_b, v_pool_scale, v_attn_w_qkv, v_attn_b_qkv, v_attn_sinks, v_attn_w_o, v_norm_mix, v_norm_ffn, v_ffn_w_in, v_ffn_conv_w, v_ffn_conv_b, v_ffn_w_out, v_norm_f):
    weights = dict(pool_w=pool_w, pool_b=pool_b, pool_scale=pool_scale, attn_w_qkv=attn_w_qkv, attn_b_qkv=attn_b_qkv,
                   attn_sinks=attn_sinks, attn_w_o=attn_w_o, norm_mix=norm_mix, norm_ffn=norm_ffn, ffn_w_in=ffn_w_in,
                   ffn_conv_w=ffn_conv_w, ffn_conv_b=ffn_conv_b, ffn_w_out=ffn_w_out, norm_f=norm_f)
    m_in = dict(pool_w=m_pool_w, pool_b=m_pool_b, pool_scale=m_pool_scale, attn_w_qkv=m_attn_w_qkv,
                attn_b_qkv=m_attn_b_qkv, attn_sinks=m_attn_sinks, attn_w_o=m_attn_w_o, norm_mix=m_norm_mix,
                norm_ffn=m_norm_ffn, ffn_w_in=m_ffn_w_in, ffn_conv_w=m_ffn_conv_w, ffn_conv_b=m_ffn_conv_b,
                ffn_w_out=m_ffn_w_out, norm_f=m_norm_f)
    v_in = dict(pool_w=v_pool_w, pool_b=v_pool_b, pool_scale=v_pool_scale, attn_w_qkv=v_attn_w_qkv,
                attn_b_qkv=v_attn_b_qkv, attn_sinks=v_attn_sinks, attn_w_o=v_attn_w_o, norm_mix=v_norm_mix,
                norm_ffn=v_norm_ffn, ffn_w_in=v_ffn_w_in, ffn_conv_w=v_ffn_conv_w, ffn_conv_b=v_ffn_conv_b,
                ffn_w_out=v_ffn_w_out, norm_f=v_norm_f)
    chip = 2 * lax.axis_index("x") + lax.axis_index("y")
    core = lax.axis_index("c")

    shards = {"w_in": ffn_w_in.astype(WIRE_DTYPE), "w_out": ffn_w_out.astype(WIRE_DTYPE),
              "w_qkv": attn_w_qkv[0].astype(WIRE_DTYPE), "w_o": attn_w_o[0].astype(WIRE_DTYPE),
              "pool_w": pool_w[0].astype(WIRE_DTYPE)}
    small = jnp.concatenate([pool_b.reshape(-1), attn_b_qkv.reshape(-1), ffn_conv_w.reshape(-1)]).reshape(1, SMALL_GATHER)
    full, small_all = _gather_weights(shards, small)
    small_all = small_all.reshape(N_CHIPS, SMALL_GATHER)
    pool_b_full = small_all[:, :256].reshape(N_CHIPS, 4, POOL_ROWS).transpose(1, 0, 2).reshape(1, D_MODEL)
    b_qkv_full = small_all[:, 256:640].reshape(1, QKV_DIM)
    conv_w_full = small_all[:, 640:].reshape(N_CHIPS, 2, 3, W_IN_COLS).transpose(1, 2, 0, 3).reshape(2, 3, 2 * D_FF)

    loss, grad_x, grads = _local_step(
        x[0], loss_target[0], full, norm_mix=norm_mix, norm_ffn=norm_ffn, norm_f=norm_f.reshape(1, D_MODEL),
        pool_b=pool_b_full, pool_scale=pool_scale, b_qkv=b_qkv_full, sinks=attn_sinks[0], conv_w=conv_w_full,
        conv_b=ffn_conv_b)
    loss = lax.psum(loss[0, 0], ("x", "y", "c"))

    big = {nm: grads[nm] for nm in BIG}
    big["w_o"] = big["w_o"].reshape(N_CHIPS, 2, W_O_ROWS // 2, D_MODEL)
    slabs = _pair_exchange(big)
    c_arr = core.reshape(1).astype(jnp.int32)
    sums = {nm: _pair_sum(nm, big[nm], slabs[nm], c_arr) for nm in BIG}
    parts = _chip_scatter(sums)
    halves = {nm: _sum_chips(nm, parts[nm]) for nm in BIG}
    reduced = _pair_allgather(halves)

    small_g = _unpack_small(_all_reduce_small(_pack_small(grads)))
    pool_b_g = lax.dynamic_slice_in_dim(small_g["pool_b"].reshape(4, N_CHIPS, POOL_ROWS), chip, 1, axis=1)
    b_qkv_g = lax.dynamic_slice_in_dim(small_g["b_qkv"].reshape(N_CHIPS, QKV_COLS), chip, 1, axis=0)
    conv_w_g = lax.dynamic_slice_in_dim(small_g["conv_w"].reshape(2, 3, N_CHIPS, W_IN_COLS), chip, 1, axis=2)
    grad_w = {
        "pool_w": reduced["pool_w"].reshape(pool_w.shape), "pool_b": pool_b_g.reshape(pool_b.shape),
        "pool_scale": small_g["pool_scale"].reshape(pool_scale.shape),
        "attn_w_qkv": reduced["w_qkv"].reshape(attn_w_qkv.shape), "attn_b_qkv": b_qkv_g.reshape(attn_b_qkv.shape),
        "attn_sinks": small_g["sinks"][:N_HEADS].reshape(attn_sinks.shape),
        "attn_w_o": reduced["w_o"].reshape(attn_w_o.shape), "norm_mix": small_g["norm_mix"].reshape(norm_mix.shape),
        "norm_ffn": small_g["norm_ffn"].reshape(norm_ffn.shape), "ffn_w_in": reduced["w_in"],
        "ffn_conv_w": conv_w_g.reshape(ffn_conv_w.shape), "ffn_conv_b": small_g["conv_b"].reshape(ffn_conv_b.shape),
        "ffn_w_out": reduced["w_out"], "norm_f": small_g["norm_f"].reshape(norm_f.shape),
    }

    delta, new_m, new_v = {}, {}, {}
    for nm in WEIGHTS:
        delta[nm], new_m[nm], new_v[nm] = _adamw_call(weights[nm], grad_w[nm], m_in[nm], v_in[nm], f"adamw_{nm}")
    return (loss, grad_x.reshape(x.shape), *[grad_w[nm] for nm in WEIGHTS], *[delta[nm] for nm in WEIGHTS],
            *[new_m[nm] for nm in WEIGHTS], *[new_v[nm] for nm in WEIGHTS])
```

```python
import functools

import jax
import jax.numpy as jnp
from jax import lax
from jax.experimental import pallas as pl
from jax.experimental.pallas import tpu as pltpu

F32 = jnp.float32
MXU_DTYPE = jnp.bfloat16
WIRE_DTYPE = jnp.bfloat16

D_MODEL = 1024
D_FF = 2816
QKV_DIM = 1536
HEAD_DIM = 64
N_HEADS = 16
N_KV_HEADS = 4
ATT_BLOCK = 128
POOL_WINDOWS = (2, 4, 8, 16)
POOL_GC = 256
POOL_HALO = 16
CONV_HALO = 8
RMS_EPS = 1e-6
ATT_SCALE = HEAD_DIM ** -0.5
ALIBI_SLOPES = tuple(2.0 ** (-8.0 / N_HEADS * (h + 1)) for h in range(N_HEADS))

ADAM_LR = 0.001
ADAM_B1 = 0.9
ADAM_B2 = 0.999
ADAM_EPS = 1e-08
ADAM_WD = 0.01
ADAM_STEP = 10

N_CHIPS = 4
MESH = pl.DeviceIdType.MESH
VMEM_LIMIT_BYTES = 56 * 1024 * 1024
ANY = pl.BlockSpec(memory_space=pl.ANY)


def _params(*semantics):
    return pltpu.CompilerParams(dimension_semantics=semantics, vmem_limit_bytes=VMEM_LIMIT_BYTES)


def _rms(x, g):
    return x * lax.rsqrt(jnp.mean(x * x, axis=-1, keepdims=True) + RMS_EPS) * g


def _rms_bwd(x, g, dy):
    rstd = lax.rsqrt(jnp.mean(x * x, axis=-1, keepdims=True) + RMS_EPS)
    xhat = x * rstd
    dxhat = dy * g
    dx = rstd * (dxhat - xhat * jnp.mean(dxhat * xhat, axis=-1, keepdims=True))
    return dx, dy * xhat


def _shift_down(x, s):
    return pltpu.roll(x, s, axis=0)


def _shift_up(x, s):
    return pltpu.roll(x, x.shape[0] - s, axis=0)


def _sigmoid(x):
    return 1.0 / (1.0 + jnp.exp(-x))


_DOT_DIMS = {"nn": ((1,), (0,)), "nt": ((1,), (1,)), "tn": ((0,), (0,))}


def _matmul(a, b, *, mode, name, grid, a_spec, b_spec, o_spec, out_shape, bias=None, bias_spec=None, residual=None,
            residual_spec=None, into=None):
    nk = grid[2]
    dims = (_DOT_DIMS[mode], ((), ()))
    acc_shape = tuple(d for d in o_spec.block_shape if d is not None)

    def body(*refs):
        refs = list(refs)
        a_ref, b_ref = refs[0], refs[1]
        pos = 2
        bias_ref = res_ref = None
        if bias is not None:
            bias_ref = refs[pos]
            pos += 1
        if residual is not None:
            res_ref = refs[pos]
            pos += 1
        if into is not None:
            pos += 1
        o_ref = refs[pos]
        acc_ref = refs[pos + 1] if nk > 1 else None
        prod = lax.dot_general(a_ref[...].astype(MXU_DTYPE), b_ref[...].astype(MXU_DTYPE), dims,
                               preferred_element_type=F32)

        def finish(acc):
            if bias_ref is not None:
                acc = acc + bias_ref[...]
            if res_ref is not None:
                acc = acc + res_ref[...]
            o_ref[...] = acc.astype(o_ref.dtype)

        if nk == 1:
            finish(prod)
            return
        k = pl.program_id(2)

        @pl.when(k == 0)
        def _():
            acc_ref[...] = prod

        @pl.when(k > 0)
        def _():
            acc_ref[...] += prod

        @pl.when(k == nk - 1)
        def _():
            finish(acc_ref[...])

    operands, in_specs = [a, b], [a_spec, b_spec]
    if bias is not None:
        operands.append(bias)
        in_specs.append(bias_spec)
    if residual is not None:
        operands.append(residual)
        in_specs.append(residual_spec)
    aliases = {}
    if into is not None:
        aliases = {len(operands): 0}
        operands.append(into)
        in_specs.append(ANY)
    return pl.pallas_call(
        body, name=name, grid=grid, in_specs=in_specs, out_specs=o_spec, out_shape=out_shape,
        scratch_shapes=[pltpu.VMEM(acc_shape, F32)] if nk > 1 else [],
        input_output_aliases=aliases, compiler_params=_params("parallel", "parallel", "arbitrary"),
    )(*operands)


def _tile(n, want):
    t = min(n, want)
    assert n % t == 0, (n, want)
    return t


def _mm_nn(a, b, name, *, b_lead=None, bias=None, residual=None, out_dtype=F32, tm=1024, tn=1024, tk=1024):
    (m, k), n = a.shape, b.shape[-1]
    tm, tn, tk = _tile(m, tm), _tile(n, tn), _tile(k, tk)
    if b_lead is None:
        b_spec = pl.BlockSpec((tk, tn), lambda i, j, kk: (kk, j))
    else:
        b_spec = pl.BlockSpec((None, tk, tn), lambda i, j, kk: (b_lead, kk, j))
    return _matmul(
        a, b, mode="nn", name=name, grid=(m // tm, n // tn, k // tk),
        a_spec=pl.BlockSpec((tm, tk), lambda i, j, kk: (i, kk)), b_spec=b_spec,
        o_spec=pl.BlockSpec((tm, tn), lambda i, j, kk: (i, j)), out_shape=jax.ShapeDtypeStruct((m, n), out_dtype),
        bias=bias, bias_spec=pl.BlockSpec((1, tn), lambda i, j, kk: (0, j)),
        residual=residual, residual_spec=pl.BlockSpec((tm, tn), lambda i, j, kk: (i, j)))


def _mm_nt(a, b, name, *, b_lead=None, out_dtype=F32, tm=1024, tn=1024, tk=1024):
    (m, k), n = a.shape, b.shape[-2]
    tm, tn, tk = _tile(m, tm), _tile(n, tn), _tile(k, tk)
    if b_lead is None:
        b_spec = pl.BlockSpec((tn, tk), lambda i, j, kk: (j, kk))
    else:
        b_spec = pl.BlockSpec((None, tn, tk), lambda i, j, kk: (b_lead, j, kk))
    return _matmul(
        a, b, mode="nt", name=name, grid=(m // tm, n // tn, k // tk),
        a_spec=pl.BlockSpec((tm, tk), lambda i, j, kk: (i, kk)), b_spec=b_spec,
        o_spec=pl.BlockSpec((tm, tn), lambda i, j, kk: (i, j)), out_shape=jax.ShapeDtypeStruct((m, n), out_dtype))


def _mm_tn(a, b, name, *, lead=None, n_lead=None, into=None, tm=1024, tn=1024, tk=1024):
    (k, m), n = a.shape, b.shape[-1]
    tm, tn, tk = _tile(m, tm), _tile(n, tn), _tile(k, tk)
    if lead is None:
        o_spec = pl.BlockSpec((tm, tn), lambda i, j, kk: (i, j))
        out_shape = jax.ShapeDtypeStruct((m, n), F32)
    else:
        o_spec = pl.BlockSpec((None, tm, tn), lambda i, j, kk: (lead, i, j))
        out_shape = jax.ShapeDtypeStruct((n_lead, m, n), F32)
    return _matmul(
        a, b, mode="tn", name=name, grid=(m // tm, n // tn, k // tk),
        a_spec=pl.BlockSpec((tk, tm), lambda i, j, kk: (kk, i)),
        b_spec=pl.BlockSpec((tk, tn), lambda i, j, kk: (kk, j)), o_spec=o_spec, out_shape=out_shape, into=into)


ROW_TILE = 512


def _rms_fwd_call(h, g, name, out_dtype):
    t = h.shape[0]
    tt = _tile(t, ROW_TILE)

    def body(h_ref, g_ref, o_ref):
        o_ref[...] = _rms(h_ref[...], g_ref[...]).astype(o_ref.dtype)

    row = pl.BlockSpec((tt, D_MODEL), lambda i: (i, 0))
    vec = pl.BlockSpec((1, D_MODEL), lambda i: (0, 0))
    return pl.pallas_call(body, name=name, grid=(t // tt,), in_specs=[row, vec], out_specs=row,
                          out_shape=jax.ShapeDtypeStruct(h.shape, out_dtype), compiler_params=_params("parallel"))(h, g)


def _rms_bwd_call(h, g, dy, dres, name):
    t = h.shape[0]
    tt = _tile(t, ROW_TILE)

    def body(h_ref, g_ref, dy_ref, dres_ref, dx_ref, dg_ref):
        dx, dgt = _rms_bwd(h_ref[...], g_ref[...], dy_ref[...].astype(F32))
        dx_ref[...] = dres_ref[...] + dx

        @pl.when(pl.program_id(0) == 0)
        def _():
            dg_ref[...] = jnp.zeros_like(dg_ref)

        dg_ref[...] += jnp.sum(dgt, axis=0, keepdims=True)

    row = pl.BlockSpec((tt, D_MODEL), lambda i: (i, 0))
    vec = pl.BlockSpec((1, D_MODEL), lambda i: (0, 0))
    return pl.pallas_call(
        body, name=name, grid=(t // tt,), in_specs=[row, vec, row, row], out_specs=[row, vec],
        out_shape=[jax.ShapeDtypeStruct(h.shape, F32), jax.ShapeDtypeStruct((1, D_MODEL), F32)],
        compiler_params=_params("arbitrary"))(h, g, dy, dres)


def _loss_call(h, g, target):
    t = h.shape[0]
    tt = _tile(t, ROW_TILE)

    def body(h_ref, g_ref, t_ref, loss_ref, dh_ref, dg_ref):
        x, gv = h_ref[...], g_ref[...]
        err = _rms(x, gv) - t_ref[...]
        dx, dgt = _rms_bwd(x, gv, err * (1.0 / D_MODEL))
        dh_ref[...] = dx

        @pl.when(pl.program_id(0) == 0)
        def _():
            dg_ref[...] = jnp.zeros_like(dg_ref)
            loss_ref[...] = jnp.zeros_like(loss_ref)

        dg_ref[...] += jnp.sum(dgt, axis=0, keepdims=True)
        per_token = jnp.mean(err * err, axis=-1, keepdims=True)
        loss_ref[...] += 0.5 * jnp.sum(per_token, axis=0, keepdims=True)

    row = pl.BlockSpec((tt, D_MODEL), lambda i: (i, 0))
    vec = pl.BlockSpec((1, D_MODEL), lambda i: (0, 0))
    one = pl.BlockSpec((1, 1), lambda i: (0, 0))
    return pl.pallas_call(
        body, name="loss_head", grid=(t // tt,), in_specs=[row, vec, row], out_specs=[one, row, vec],
        out_shape=[jax.ShapeDtypeStruct((1, 1), F32), jax.ShapeDtypeStruct(h.shape, F32),
                   jax.ShapeDtypeStruct((1, D_MODEL), F32)],
        compiler_params=_params("arbitrary"))(h, g, target)


def _pool_windows_causal(ext, first_row):
    tt = ext.shape[0] - POOL_HALO
    t = first_row + lax.broadcasted_iota(jnp.int32, (tt, 1), 0)
    outs = []
    for gi, win in enumerate(POOL_WINDOWS):
        cols = slice(gi * POOL_GC, (gi + 1) * POOL_GC)
        s = ext[:, cols]
        sh = 1
        while sh < win:
            s = s + _shift_down(s, sh)
            sh *= 2
        count = jnp.minimum(t + 1, win).astype(F32)
        outs.append(s[POOL_HALO:] / count - ext[POOL_HALO:, cols])
    return outs


def _prev_halo_spec(tt, halo, width):
    per = tt // halo
    return pl.BlockSpec((halo, width), lambda i: (jnp.maximum(i * per - 1, 0), 0))


def _pool_fwd_call(h0, g_mix, pool_w, pool_b, pool_scale, g_ffn):
    t = h0.shape[0]
    tt = _tile(t, ROW_TILE)

    def body(h_ref, halo_ref, gm_ref, w_ref, b_ref, sc_ref, gf_ref, h1_ref, hn_ref):
        i = pl.program_id(0)
        x, gm = h_ref[...], gm_ref[...]
        top = jnp.where(i > 0, _rms(halo_ref[...], gm), 0.0)
        ext = jnp.concatenate([top, _rms(x, gm)], axis=0)
        ps = _pool_windows_causal(ext, i * tt)
        ys = [jnp.dot(p.astype(MXU_DTYPE), w_ref[gi], preferred_element_type=F32) for gi, p in enumerate(ps)]
        mix = (jnp.concatenate(ys, axis=1) + b_ref[...]) * sc_ref[...]
        h1 = x + mix
        h1_ref[...] = h1
        hn_ref[...] = _rms(h1, gf_ref[...]).astype(hn_ref.dtype)

    row = pl.BlockSpec((tt, D_MODEL), lambda i: (i, 0))
    vec = pl.BlockSpec((1, D_MODEL), lambda i: (0, 0))
    wsp = pl.BlockSpec((len(POOL_WINDOWS), POOL_GC, POOL_GC), lambda i: (0, 0, 0))
    return pl.pallas_call(
        body, name="pool_fwd", grid=(t // tt,),
        in_specs=[row, _prev_halo_spec(tt, POOL_HALO, D_MODEL), vec, wsp, vec, vec, vec], out_specs=[row, row],
        out_shape=[jax.ShapeDtypeStruct(h0.shape, F32), jax.ShapeDtypeStruct(h0.shape, MXU_DTYPE)],
        compiler_params=_params("parallel"))(h0, h0, g_mix, pool_w, pool_b, pool_scale, g_ffn)


def _pool_bwd_call(h0, dh1, g_mix, pool_w, pool_b, pool_scale):
    t = h0.shape[0]
    tt = _tile(t, ROW_TILE)
    nt = t // tt
    per = tt // POOL_HALO

    def body(h_ref, halo_ref, d_ref, dnext_ref, gm_ref, w_ref, b_ref, sc_ref, gx_ref, dw_ref, dv_ref):
        i = pl.program_id(0)
        x, gm, sc = h_ref[...], gm_ref[...], sc_ref[...]
        top = jnp.where(i > 0, _rms(halo_ref[...], gm), 0.0)
        ext = jnp.concatenate([top, _rms(x, gm)], axis=0)
        ps = _pool_windows_causal(ext, i * tt)

        dy = d_ref[...]
        dy_ext = jnp.concatenate([dy, jnp.where(i < nt - 1, dnext_ref[...], 0.0)], axis=0)
        dpre_ext = dy_ext * sc
        t_ext = i * tt + lax.broadcasted_iota(jnp.int32, (tt + POOL_HALO, 1), 0)

        @pl.when(i == 0)
        def _():
            dw_ref[...] = jnp.zeros_like(dw_ref)
            dv_ref[...] = jnp.zeros_like(dv_ref)

        dhn, ypre = [], []
        for gi, win in enumerate(POOL_WINDOWS):
            cols = slice(gi * POOL_GC, (gi + 1) * POOL_GC)
            w = w_ref[gi]
            p16 = ps[gi].astype(MXU_DTYPE)
            dpre16 = dpre_ext[:, cols].astype(MXU_DTYPE)
            ypre.append(jnp.dot(p16, w, preferred_element_type=F32))
            dw_ref[gi] += lax.dot_general(p16, dpre16[:tt], (((0,), (0,)), ((), ())), preferred_element_type=F32)
            dp_ext = lax.dot_general(dpre16, w, (((1,), (1,)), ((), ())), preferred_element_type=F32)
            s = dp_ext / jnp.minimum(t_ext + 1, win).astype(F32)
            sh = 1
            while sh < win:
                s = s + _shift_up(s, sh)
                sh *= 2
            dhn.append(s[:tt] - dp_ext[:tt])
        dhn = jnp.concatenate(dhn, axis=1)
        ypre = jnp.concatenate(ypre, axis=1) + b_ref[...]
        dx, dgt = _rms_bwd(x, gm, dhn)
        gx_ref[...] = dy + dx
        dv_ref[0:1, :] += jnp.sum(dpre_ext[:tt], axis=0, keepdims=True)
        dv_ref[1:2, :] += jnp.sum(dy * ypre, axis=0, keepdims=True)
        dv_ref[2:3, :] += jnp.sum(dgt, axis=0, keepdims=True)

    row = pl.BlockSpec((tt, D_MODEL), lambda i: (i, 0))
    vec = pl.BlockSpec((1, D_MODEL), lambda i: (0, 0))
    nxt = pl.BlockSpec((POOL_HALO, D_MODEL), lambda i: (jnp.minimum((i + 1) * per, t // POOL_HALO - 1), 0))
    wsp = pl.BlockSpec((len(POOL_WINDOWS), POOL_GC, POOL_GC), lambda i: (0, 0, 0))
    return pl.pallas_call(
        body, name="pool_bwd", grid=(nt,),
        in_specs=[row, _prev_halo_spec(tt, POOL_HALO, D_MODEL), row, nxt, vec, wsp, vec, vec],
        out_specs=[row, wsp, pl.BlockSpec((8, D_MODEL), lambda i: (0, 0))],
        out_shape=[jax.ShapeDtypeStruct(h0.shape, F32),
                   jax.ShapeDtypeStruct((len(POOL_WINDOWS), POOL_GC, POOL_GC), F32),
                   jax.ShapeDtypeStruct((8, D_MODEL), F32)],
        compiler_params=_params("arbitrary"))(h0, h0, dh1, dh1, g_mix, pool_w, pool_b, pool_scale)


CONV_ROWS = 256
CONV_SUB = 64
LANES = 128


def _causal_conv(u_ext, w, b):
    return w[0:1] * _shift_down(u_ext, 2) + w[1:2] * _shift_down(u_ext, 1) + w[2:3] * u_ext + b


def _fold8(x):
    acc = x[0:8]
    for r in range(8, x.shape[0], 8):
        acc = acc + x[r:r + 8]
    return acc


def _conv_glu_fwd_call(u, conv_w, conv_b, name):
    _, t, f = u.shape
    tt = _tile(t, CONV_ROWS)
    sub = _tile(tt, CONV_SUB)
    per = tt // CONV_HALO

    def body(u_ref, up_ref, w_ref, b_ref, z_ref, ext_ref):
        i = pl.program_id(0)
        ext_ref[:, :CONV_HALO, :] = jnp.where(i > 0, up_ref[...], 0.0)
        ext_ref[:, CONV_HALO:, :] = u_ref[...]

        def chunk(j, carry):
            lanes = pl.ds(pl.multiple_of(j * LANES, LANES), LANES)
            w, b = w_ref[:, :, lanes], b_ref[:, lanes]
            for s in range(tt // sub):
                rows = pl.ds(s * sub, sub + CONV_HALO)
                c = [_causal_conv(ext_ref[hf, rows, lanes], w[:, hf, :], b[hf:hf + 1, :])[CONV_HALO:] for hf in range(2)]
                z_ref[pl.ds(s * sub, sub), lanes] = (c[0] * c[1] * _sigmoid(c[1])).astype(z_ref.dtype)
            return carry

        lax.fori_loop(0, f // LANES, chunk, 0)

    return pl.pallas_call(
        body, name=name, grid=(t // tt,),
        in_specs=[pl.BlockSpec((2, tt, f), lambda i: (0, i, 0)),
                  pl.BlockSpec((2, CONV_HALO, f), lambda i: (0, jnp.maximum(i * per - 1, 0), 0)),
                  pl.BlockSpec((3, 2, f), lambda i: (0, 0, 0)), pl.BlockSpec((2, f), lambda i: (0, 0))],
        out_specs=pl.BlockSpec((tt, f), lambda i: (i, 0)),
        out_shape=jax.ShapeDtypeStruct((t, f), MXU_DTYPE),
        scratch_shapes=[pltpu.VMEM((2, CONV_HALO + tt, f), F32)],
        compiler_params=_params("parallel"))(u, u, conv_w, conv_b)


def _conv_glu_bwd_call(u, dz, conv_w, conv_b, name):
    _, t, f = u.shape
    tt = _tile(t, CONV_ROWS)
    sub = _tile(tt, CONV_SUB)
    nt = t // tt
    per = tt // CONV_HALO
    halo = CONV_HALO

    def body(u_ref, up_ref, un_ref, dz_ref, dzn_ref, w_ref, b_ref, du_ref, dw_ref, db_ref, uext, dzext, dwacc, dbacc):
        i = pl.program_id(0)
        uext[:, :halo, :] = jnp.where(i > 0, up_ref[...], 0.0)
        uext[:, halo:halo + tt, :] = u_ref[...]
        uext[:, halo + tt:, :] = un_ref[...]
        dzext[:halo, :] = jnp.zeros((halo, f), F32)
        dzext[halo:halo + tt, :] = dz_ref[...].astype(F32)
        dzext[halo + tt:, :] = jnp.where(i < nt - 1, dzn_ref[...].astype(F32), 0.0)

        @pl.when(i == 0)
        def _():
            dwacc[...] = jnp.zeros_like(dwacc)
            dbacc[...] = jnp.zeros_like(dbacc)

        main = slice(halo, halo + sub)

        def chunk(j, carry):
            lanes = pl.ds(pl.multiple_of(j * LANES, LANES), LANES)
            w, b = w_ref[:, :, lanes], b_ref[:, lanes]
            for s in range(tt // sub):
                rows = pl.ds(s * sub, sub + 2 * halo)
                ue = [uext[hf, rows, lanes] for hf in range(2)]
                a, g = [_causal_conv(ue[hf], w[:, hf, :], b[hf:hf + 1, :]) for hf in range(2)]
                sg = _sigmoid(g)
                dzs = dzext[rows, lanes]
                dc = [dzs * g * sg, dzs * a * sg * (1.0 + g * (1.0 - sg))]
                for hf in range(2):
                    d = dc[hf]
                    d1, d2 = _shift_up(d, 1), _shift_up(d, 2)
                    du = w[2:3, hf, :] * d + w[1:2, hf, :] * d1 + w[0:1, hf, :] * d2
                    du_ref[hf, pl.ds(s * sub, sub), lanes] = du[main].astype(du_ref.dtype)
                    um = ue[hf][main]
                    for k, dd in enumerate((d2, d1, d)):
                        dwacc[k, hf, :, lanes] += _fold8(dd[main] * um)
                    dbacc[hf, :, lanes] += _fold8(d[main])
            return carry

        lax.fori_loop(0, f // LANES, chunk, 0)

        @pl.when(i == nt - 1)
        def _():
            dw_ref[...] = jnp.sum(dwacc[...], axis=2)
            db_ref[...] = jnp.sum(dbacc[...], axis=1)

    last = t // halo - 1
    return pl.pallas_call(
        body, name=name, grid=(nt,),
        in_specs=[pl.BlockSpec((2, tt, f), lambda i: (0, i, 0)),
                  pl.BlockSpec((2, halo, f), lambda i: (0, jnp.maximum(i * per - 1, 0), 0)),
                  pl.BlockSpec((2, halo, f), lambda i: (0, jnp.minimum((i + 1) * per, last), 0)),
                  pl.BlockSpec((tt, f), lambda i: (i, 0)),
                  pl.BlockSpec((halo, f), lambda i: (jnp.minimum((i + 1) * per, last), 0)),
                  pl.BlockSpec((3, 2, f), lambda i: (0, 0, 0)), pl.BlockSpec((2, f), lambda i: (0, 0))],
        out_specs=[pl.BlockSpec((2, tt, f), lambda i: (0, i, 0)),
                   pl.BlockSpec((3, 2, f), lambda i: (0, 0, 0)), pl.BlockSpec((2, f), lambda i: (0, 0))],
        out_shape=[jax.ShapeDtypeStruct(u.shape, MXU_DTYPE), jax.ShapeDtypeStruct((3, 2, f), F32),
                   jax.ShapeDtypeStruct((2, f), F32)],
        scratch_shapes=[pltpu.VMEM((2, tt + 2 * halo, f), F32), pltpu.VMEM((tt + 2 * halo, f), F32),
                        pltpu.VMEM((3, 2, 8, f), F32), pltpu.VMEM((2, 8, f), F32)],
        compiler_params=_params("arbitrary"))(u, u, u, dz, dz, conv_w, conv_b)


def _kv_even_odd(band, k):
    pair, parity = divmod(k, 2)
    blk = band[:, 128 * pair:128 * (pair + 1)].astype(F32)
    lane = lax.broadcasted_iota(jnp.int32, blk.shape, 1)
    own = jnp.where((lane >= HEAD_DIM) == (parity == 1), blk, 0.0)
    swapped = pltpu.roll(own, HEAD_DIM, axis=1)
    even, odd = (own, swapped) if parity == 0 else (swapped, own)
    return even.astype(MXU_DTYPE), odd.astype(MXU_DTYPE)


def _stack_pairs(ref, k):
    return jnp.concatenate([ref[:, 256 * k:256 * k + 128], ref[:, 256 * k + 128:256 * k + 256]], axis=0)


_GROUP_ORDER = (0, 2, 1, 3)


def _attn_fill_tables(bias_ref, sinkcol_ref, sink_ref):
    shape = (4 * ATT_BLOCK, 2 * ATT_BLOCK)
    row = lax.broadcasted_iota(jnp.int32, shape, 0)
    col = lax.broadcasted_iota(jnp.int32, shape, 1)
    dist = (row & (ATT_BLOCK - 1)) + ATT_BLOCK - col
    ok = (dist >= 0) & (dist < ATT_BLOCK)
    distf = dist.astype(F32)
    rb = lax.broadcasted_iota(jnp.int32, (shape[0], 1), 0) // ATT_BLOCK
    for k in range(N_KV_HEADS):
        slope = jnp.zeros((shape[0], 1), F32)
        sink = jnp.zeros((shape[0], 1), F32)
        for r, gq in enumerate(_GROUP_ORDER):
            head = 4 * k + gq
            slope = jnp.where(rb == r, ALIBI_SLOPES[head], slope)
            sink = jnp.where(rb == r, sink_ref[head], sink)
        bias = jnp.where(ok, -slope * distf, -jnp.inf)
        bias_ref[0, k] = bias
        bias_ref[1, k] = jnp.where(col >= ATT_BLOCK, bias, -jnp.inf)
        sinkcol_ref[k] = sink


def _attn_probs(qs, k_even, k_odd, bias, sink):
    nt_dims = (((1,), (1,)), ((), ()))
    s = jnp.concatenate([lax.dot_general(qs, k_even, nt_dims, preferred_element_type=F32),
                         lax.dot_general(qs, k_odd, nt_dims, preferred_element_type=F32)], axis=0) + bias
    m = jnp.maximum(jnp.max(s, axis=-1, keepdims=True), sink)
    p = jnp.exp(s - m)
    es = jnp.exp(sink - m)
    return p, 1.0 / (jnp.sum(p, axis=-1, keepdims=True) + es), es


_ATTN_TABLES = [pltpu.VMEM((2, N_KV_HEADS, 4 * ATT_BLOCK, 2 * ATT_BLOCK), F32),
                pltpu.VMEM((N_KV_HEADS, 4 * ATT_BLOCK, 1), F32)]


def _attn_specs(with_do):
    prev = lambda n: jnp.maximum(n - 1, 0)
    specs = [pl.BlockSpec(memory_space=pltpu.SMEM),
             pl.BlockSpec((ATT_BLOCK, D_MODEL), lambda n: (n, 0)),
             pl.BlockSpec((ATT_BLOCK, 256), lambda n: (prev(n), 4)), pl.BlockSpec((ATT_BLOCK, 256), lambda n: (n, 4)),
             pl.BlockSpec((ATT_BLOCK, 256), lambda n: (prev(n), 5)), pl.BlockSpec((ATT_BLOCK, 256), lambda n: (n, 5))]
    if with_do:
        specs.append(pl.BlockSpec((ATT_BLOCK, D_MODEL), lambda n: (n, 0)))
    return specs


def _attn_fwd_call(qkv, sinks):
    t = qkv.shape[0]

    def body(sink_ref, q_ref, kp_ref, kc_ref, vp_ref, vc_ref, o_ref, bias_ref, sinkcol_ref):
        n = pl.program_id(0)

        @pl.when(n == 0)
        def _():
            _attn_fill_tables(bias_ref, sinkcol_ref, sink_ref)

        first = 1 - jnp.minimum(n, 1)
        kband = jnp.concatenate([kp_ref[...], kc_ref[...]], axis=0)
        vband = jnp.concatenate([vp_ref[...], vc_ref[...]], axis=0)
        for k in range(N_KV_HEADS):
            k_even, k_odd = _kv_even_odd(kband, k)
            v_even, v_odd = _kv_even_odd(vband, k)
            qs = _stack_pairs(q_ref, k) * ATT_SCALE
            p, rdenom, _ = _attn_probs(qs, k_even, k_odd, bias_ref[first, k], sinkcol_ref[k])
            probs = (p * rdenom).astype(MXU_DTYPE)
            o = (jnp.dot(probs[:256], v_even, preferred_element_type=F32)
                 + jnp.dot(probs[256:], v_odd, preferred_element_type=F32))
            o_ref[:, 256 * k:256 * k + 128] = o[:128].astype(o_ref.dtype)
            o_ref[:, 256 * k + 128:256 * k + 256] = o[128:].astype(o_ref.dtype)

    return pl.pallas_call(
        body, name="attn_fwd", grid=(t // ATT_BLOCK,), in_specs=_attn_specs(False),
        out_specs=pl.BlockSpec((ATT_BLOCK, D_MODEL), lambda n: (n, 0)),
        out_shape=jax.ShapeDtypeStruct((t, D_MODEL), MXU_DTYPE), scratch_shapes=_ATTN_TABLES,
        compiler_params=_params("arbitrary"))(sinks, qkv, qkv, qkv, qkv, qkv)


def _attn_bwd_call(qkv, sinks, do):
    t = qkv.shape[0]
    nb = t // ATT_BLOCK
    tn_dims = (((0,), (0,)), ((), ()))
    nt_dims = (((1,), (1,)), ((), ()))

    def to_native(even_part, odd_part, parity):
        lane = lax.broadcasted_iota(jnp.int32, even_part.shape, 1)
        lo = lane < HEAD_DIM
        e, o = jnp.where(lo, even_part, 0.0), jnp.where(lo, 0.0, odd_part)
        if parity == 0:
            return e + pltpu.roll(o, HEAD_DIM, axis=1)
        return pltpu.roll(e, HEAD_DIM, axis=1) + o

    def body(sink_ref, q_ref, kp_ref, kc_ref, vp_ref, vc_ref, do_ref, dq_ref, dk_ref, dv_ref, db_ref, dsink_ref,
             bias_ref, sinkcol_ref):
        n = pl.program_id(0)

        @pl.when(n == 0)
        def _():
            _attn_fill_tables(bias_ref, sinkcol_ref, sink_ref)
            dk_ref[...] = jnp.zeros_like(dk_ref)
            dv_ref[...] = jnp.zeros_like(dv_ref)
            db_ref[...] = jnp.zeros_like(db_ref)
            dsink_ref[...] = jnp.zeros_like(dsink_ref)

        kband = jnp.concatenate([kp_ref[...], kc_ref[...]], axis=0)
        vband = jnp.concatenate([vp_ref[...], vc_ref[...]], axis=0)
        dk_pairs = [jnp.zeros((2 * ATT_BLOCK, 128), F32), jnp.zeros((2 * ATT_BLOCK, 128), F32)]
        dv_pairs = [jnp.zeros((2 * ATT_BLOCK, 128), F32), jnp.zeros((2 * ATT_BLOCK, 128), F32)]
        sink_lane = lax.broadcasted_iota(jnp.int32, (8, 128), 1)
        sink_row = lax.broadcasted_iota(jnp.int32, (8, 128), 0)
        dsink = jnp.zeros((8, 128), F32)
        first = 1 - jnp.minimum(n, 1)
        for k in range(N_KV_HEADS):
            k_even, k_odd = _kv_even_odd(kband, k)
            v_even, v_odd = _kv_even_odd(vband, k)
            qs = _stack_pairs(q_ref, k) * ATT_SCALE
            dos = _stack_pairs(do_ref, k)
            p, rdenom, es = _attn_probs(qs, k_even, k_odd, bias_ref[first, k], sinkcol_ref[k])
            probs = p * rdenom
            dp = jnp.concatenate([lax.dot_general(dos, v_even, nt_dims, preferred_element_type=F32),
                                  lax.dot_general(dos, v_odd, nt_dims, preferred_element_type=F32)], axis=0)
            delta = jnp.sum(probs * dp, axis=-1, keepdims=True)
            ds16 = (probs * (dp - delta)).astype(MXU_DTYPE)
            p16 = probs.astype(MXU_DTYPE)
            dsink_rows = -(es * rdenom) * delta
            for r, gq in enumerate(_GROUP_ORDER):
                tot = jnp.sum(dsink_rows[ATT_BLOCK * r:ATT_BLOCK * (r + 1)], axis=0, keepdims=True)
                dsink = dsink + jnp.where((sink_lane == 4 * k + gq) & (sink_row == 0), tot, 0.0)
            dqs = (jnp.dot(ds16[:256], k_even, preferred_element_type=F32)
                   + jnp.dot(ds16[256:], k_odd, preferred_element_type=F32)) * ATT_SCALE
            dq_ref[:, 256 * k:256 * k + 128] = dqs[:128].astype(dq_ref.dtype)
            dq_ref[:, 256 * k + 128:256 * k + 256] = dqs[128:].astype(dq_ref.dtype)
            db_ref[:, 256 * k:256 * k + 128] += jnp.sum(dqs[:128], axis=0, keepdims=True)
            db_ref[:, 256 * k + 128:256 * k + 256] += jnp.sum(dqs[128:], axis=0, keepdims=True)
            pair, parity = divmod(k, 2)
            dk_pairs[pair] = dk_pairs[pair] + to_native(
                lax.dot_general(ds16[:256], qs, tn_dims, preferred_element_type=F32),
                lax.dot_general(ds16[256:], qs, tn_dims, preferred_element_type=F32), parity)
            dv_pairs[pair] = dv_pairs[pair] + to_native(
                lax.dot_general(p16[:256], dos, tn_dims, preferred_element_type=F32),
                lax.dot_general(p16[256:], dos, tn_dims, preferred_element_type=F32), parity)
        dsink_ref[...] += dsink
        dk_band = jnp.concatenate(dk_pairs, axis=1)
        dv_band = jnp.concatenate(dv_pairs, axis=1)
        cur = pl.multiple_of(n * ATT_BLOCK, ATT_BLOCK)
        dk_ref[pl.ds(cur, ATT_BLOCK), :] += dk_band[ATT_BLOCK:]
        dv_ref[pl.ds(cur, ATT_BLOCK), :] += dv_band[ATT_BLOCK:]

        @pl.when(n > 0)
        def _():
            prv = pl.multiple_of((n - 1) * ATT_BLOCK, ATT_BLOCK)
            dk_ref[pl.ds(prv, ATT_BLOCK), :] += dk_band[:ATT_BLOCK]
            dv_ref[pl.ds(prv, ATT_BLOCK), :] += dv_band[:ATT_BLOCK]

        @pl.when(n == nb - 1)
        def _():
            db_ref[:, 1024:1280] = jnp.sum(dk_ref[...], axis=0, keepdims=True)
            db_ref[:, 1280:1536] = jnp.sum(dv_ref[...], axis=0, keepdims=True)

    whole = lambda shape: pl.BlockSpec(shape, lambda n: (0, 0))
    return pl.pallas_call(
        body, name="attn_bwd", grid=(nb,), in_specs=_attn_specs(True),
        out_specs=[pl.BlockSpec((ATT_BLOCK, D_MODEL), lambda n: (n, 0)), whole((t, 256)), whole((t, 256)),
                   whole((1, QKV_DIM)), whole((8, 128))],
        out_shape=[jax.ShapeDtypeStruct((t, D_MODEL), MXU_DTYPE), jax.ShapeDtypeStruct((t, 256), F32),
                   jax.ShapeDtypeStruct((t, 256), F32), jax.ShapeDtypeStruct((1, QKV_DIM), F32),
                   jax.ShapeDtypeStruct((8, 128), F32)],
        scratch_shapes=_ATTN_TABLES, compiler_params=_params("arbitrary"))(sinks, qkv, qkv, qkv, qkv, qkv, do)


FF_TILE = 1408


def _ffn_fwd(hn, h, w_in, w_out, conv_w, conv_b, layer):
    t = hn.shape[0]
    tm = _tile(t, 512)
    halves = D_FF // FF_TILE
    u = _matmul(
        hn, w_in, mode="nn", name=f"ffn{layer}_in", grid=(t // tm, 2 * halves, 1),
        a_spec=pl.BlockSpec((tm, D_MODEL), lambda i, j, k: (i, 0)),
        b_spec=pl.BlockSpec((None, D_MODEL, FF_TILE), lambda i, j, k: (layer, 0, j)),
        o_spec=pl.BlockSpec((None, tm, FF_TILE), lambda i, j, k: (j // halves, i, j % halves)),
        out_shape=jax.ShapeDtypeStruct((2, t, D_FF), F32))
    z = _conv_glu_fwd_call(u, conv_w, conv_b, f"ffn{layer}_glu")
    h_out = _mm_nn(z, w_out, f"ffn{layer}_out", b_lead=layer, residual=h, tk=FF_TILE)
    return h_out, u, z


def _ffn_bwd(dh, hn, u, z, w_in, w_out, conv_w, conv_b, layer, g_in_acc, g_out_acc):
    t = hn.shape[0]
    halves = D_FF // FF_TILE
    tm = _tile(t, 1024)
    dz = _mm_nt(dh, w_out, f"ffn{layer}_dz", b_lead=layer, out_dtype=MXU_DTYPE, tn=FF_TILE)
    g_out = _mm_tn(z, dh, f"ffn{layer}_dwout", lead=layer, n_lead=2, into=g_out_acc, tm=FF_TILE)
    du, dcw, dcb = _conv_glu_bwd_call(u, dz, conv_w, conv_b, f"ffn{layer}_dglu")
    g_in = _matmul(
        hn, du, mode="tn", name=f"ffn{layer}_dwin", grid=(1, 2 * halves, t // tm),
        a_spec=pl.BlockSpec((tm, D_MODEL), lambda i, j, k: (k, 0)),
        b_spec=pl.BlockSpec((None, tm, FF_TILE), lambda i, j, k: (j // halves, k, j % halves)),
        o_spec=pl.BlockSpec((None, D_MODEL, FF_TILE), lambda i, j, k: (layer, 0, j)),
        out_shape=jax.ShapeDtypeStruct((2, D_MODEL, 2 * D_FF), F32), into=g_in_acc)
    dhn = _matmul(
        du, w_in, mode="nt", name=f"ffn{layer}_dhn", grid=(t // tm, 1, 2 * halves),
        a_spec=pl.BlockSpec((None, tm, FF_TILE), lambda i, j, k: (k // halves, i, k % halves)),
        b_spec=pl.BlockSpec((None, D_MODEL, FF_TILE), lambda i, j, k: (layer, 0, k)),
        o_spec=pl.BlockSpec((tm, D_MODEL), lambda i, j, k: (i, 0)),
        out_shape=jax.ShapeDtypeStruct((t, D_MODEL), F32))
    return dhn, g_in, g_out, dcw, dcb


def _local_step(x, target, w, *, norm_mix, norm_ffn, norm_f, pool_b, pool_scale, b_qkv, sinks, conv_w, conv_b):
    cw = [conv_w[l].reshape(3, 2, D_FF) for l in range(2)]
    cb = [conv_b[l].reshape(2, D_FF) for l in range(2)]
    gm = [norm_mix[l:l + 1] for l in range(2)]
    gf = [norm_ffn[l:l + 1] for l in range(2)]

    h1, hn_f0 = _pool_fwd_call(x, gm[0], w["pool_w"], pool_b, pool_scale, gf[0])
    h2, u0, z0 = _ffn_fwd(hn_f0, h1, w["w_in"], w["w_out"], cw[0], cb[0], 0)
    hn_a = _rms_fwd_call(h2, gm[1], "rms_attn", MXU_DTYPE)
    qkv = _mm_nn(hn_a, w["w_qkv"], "attn_qkv", bias=b_qkv, out_dtype=MXU_DTYPE, tn=768)
    o = _attn_fwd_call(qkv, sinks)
    h3 = _mm_nn(o, w["w_o"], "attn_out", residual=h2)
    hn_f1 = _rms_fwd_call(h3, gf[1], "rms_ffn1", MXU_DTYPE)
    h4, u1, z1 = _ffn_fwd(hn_f1, h3, w["w_in"], w["w_out"], cw[1], cb[1], 1)
    loss, dh, d_norm_f = _loss_call(h4, norm_f, target)

    dhn, g_in, g_out, dcw1, dcb1 = _ffn_bwd(dh, hn_f1, u1, z1, w["w_in"], w["w_out"], cw[1], cb[1], 1, None, None)
    dh, d_gf1 = _rms_bwd_call(h3, gf[1], dhn, dh, "rms_ffn1_bwd")
    do = _mm_nt(dh, w["w_o"], "attn_do", out_dtype=MXU_DTYPE)
    g_o = _mm_tn(o, dh, "attn_dwo")
    dq, dk, dv, d_bqkv, d_sinks = _attn_bwd_call(qkv, sinks, do)
    dqkv = jnp.concatenate([dq, dk.astype(MXU_DTYPE), dv.astype(MXU_DTYPE)], axis=1)
    g_qkv = _mm_tn(hn_a, dqkv, "attn_dwqkv", tn=768)
    dhn = _mm_nt(dqkv, w["w_qkv"], "attn_dhn", tk=768)
    dh, d_gm1 = _rms_bwd_call(h2, gm[1], dhn, dh, "rms_attn_bwd")
    dhn, g_in, g_out, dcw0, dcb0 = _ffn_bwd(dh, hn_f0, u0, z0, w["w_in"], w["w_out"], cw[0], cb[0], 0, g_in, g_out)
    dh, d_gf0 = _rms_bwd_call(h1, gf[0], dhn, dh, "rms_ffn0_bwd")
    grad_x, g_pool, dvec = _pool_bwd_call(x, dh, gm[0], w["pool_w"], pool_b, pool_scale)

    grads = {
        "w_in": g_in, "w_out": g_out, "w_qkv": g_qkv, "w_o": g_o, "pool_w": g_pool,
        "pool_b": dvec[0:1], "pool_scale": dvec[1:2], "b_qkv": d_bqkv, "sinks": d_sinks[0:1, :N_HEADS],
        "norm_mix": jnp.concatenate([dvec[2:3], d_gm1], axis=0), "norm_ffn": jnp.concatenate([d_gf0, d_gf1], axis=0),
        "conv_w": jnp.stack([dcw0.reshape(3, 2 * D_FF), dcw1.reshape(3, 2 * D_FF)]),
        "conv_b": jnp.stack([dcb0.reshape(2 * D_FF), dcb1.reshape(2 * D_FF)]), "norm_f": d_norm_f,
    }
    return loss, grad_x, grads


W_IN_COLS = 2 * D_FF // N_CHIPS
W_OUT_ROWS = D_FF // N_CHIPS
QKV_COLS = QKV_DIM // N_CHIPS
W_O_ROWS = D_MODEL // N_CHIPS
POOL_ROWS = POOL_GC // N_CHIPS
BIG = ("w_in", "w_out", "w_qkv", "w_o", "pool_w")


def _mesh_place():
    x, y, c = lax.axis_index("x"), lax.axis_index("y"), lax.axis_index("c")
    chips = [(1 - x, y), (x, 1 - y), (1 - x, 1 - y)]
    return x, y, c, chips


def _col(k, width):
    return pl.ds(pl.multiple_of(k * width, 128), width)


def _row(k, height):
    return pl.ds(pl.multiple_of(k * height, 16), height)


def _full_piece(name, ref, k, h):
    if name == "w_in":
        return ref.at[h, :, _col(k, W_IN_COLS)]
    if name == "w_out":
        return ref.at[h, _row(k, W_OUT_ROWS), :]
    if name == "w_qkv":
        return ref.at[_row(h, D_MODEL // 2), _col(k, QKV_COLS)]
    if name == "w_o":
        return ref.at[_row(2 * k + h, W_O_ROWS // 2), :]
    return ref.at[pl.ds(2 * h, 2), _row(k, POOL_ROWS), :]


def _shard_half(name, ref, h):
    if name in ("w_in", "w_out"):
        return ref.at[h]
    if name == "w_qkv":
        return ref.at[_row(h, D_MODEL // 2), :]
    if name == "w_o":
        return ref.at[_row(h, W_O_ROWS // 2), :]
    return ref.at[pl.ds(2 * h, 2)]


FULL_SHAPES = {"w_in": (2, D_MODEL, 2 * D_FF), "w_out": (2, D_FF, D_MODEL), "w_qkv": (D_MODEL, QKV_DIM),
               "w_o": (D_MODEL, D_MODEL), "pool_w": (4, POOL_GC, POOL_GC)}
SHARD_SHAPES = {"w_in": (2, D_MODEL, W_IN_COLS), "w_out": (2, W_OUT_ROWS, D_MODEL), "w_qkv": (D_MODEL, QKV_COLS),
                "w_o": (W_O_ROWS, D_MODEL), "pool_w": (4, POOL_ROWS, POOL_GC)}
HALF_SHAPES = {"w_in": (D_MODEL, W_IN_COLS), "w_out": (W_OUT_ROWS, D_MODEL), "w_qkv": (D_MODEL // 2, QKV_COLS),
               "w_o": (W_O_ROWS // 2, D_MODEL), "pool_w": (2, POOL_ROWS, POOL_GC)}


def _place_shard(name, shard, me_arr):
    if name == "w_in":
        blk, grid = (1, 256, W_IN_COLS), (2, D_MODEL // 256)
        src, dst = (lambda a, b, me: (a, b, 0)), (lambda a, b, me: (a, b, me[0]))
    elif name == "w_out":
        blk, grid = (1, W_OUT_ROWS, D_MODEL), (2, 1)
        src, dst = (lambda a, b, me: (a, 0, 0)), (lambda a, b, me: (a, me[0], 0))
    elif name == "w_qkv":
        blk, grid = (256, QKV_COLS), (1, D_MODEL // 256)
        src, dst = (lambda a, b, me: (b, 0)), (lambda a, b, me: (b, me[0]))
    elif name == "w_o":
        blk, grid = (W_O_ROWS, D_MODEL), (1, 1)
        src, dst = (lambda a, b, me: (0, 0)), (lambda a, b, me: (me[0], 0))
    else:
        blk, grid = (4, POOL_ROWS, POOL_GC), (1, 1)
        src, dst = (lambda a, b, me: (0, 0, 0)), (lambda a, b, me: (0, me[0], 0))

    def body(me_ref, s_ref, o_ref):
        o_ref[...] = s_ref[...].astype(o_ref.dtype)

    return pl.pallas_call(
        body, name=f"place_{name}",
        grid_spec=pltpu.PrefetchScalarGridSpec(num_scalar_prefetch=1, grid=grid, in_specs=[pl.BlockSpec(blk, src)],
                                               out_specs=pl.BlockSpec(blk, dst)),
        out_shape=jax.ShapeDtypeStruct(FULL_SHAPES[name], WIRE_DTYPE), compiler_params=_params("parallel", "parallel"),
    )(me_arr, shard)


def _gather_weights(full, small):
    nb = len(BIG)

    def body(*refs):
        small_in = refs[nb]
        outs, small_out = refs[nb + 1:2 * nb + 1], refs[2 * nb + 1]
        send_sems, recv_sems, local_sems, small_send, small_recv = refs[2 * nb + 2:]
        x, y, c, chips = _mesh_place()
        me = 2 * x + y
        sibling = (x, y, 1 - c)

        def remote(src, dst, t, slot, dev):
            return pltpu.make_async_remote_copy(src_ref=src, dst_ref=dst, send_sem=send_sems.at[t, slot],
                                                recv_sem=recv_sems.at[t, slot], device_id=dev, device_id_type=MESH)

        local = [pltpu.make_async_copy(small_in, small_out.at[me], local_sems.at[0])]
        for cp in local:
            cp.start()
        sends = []
        for j, chip in enumerate(chips):
            for t, nm in enumerate(BIG):
                mine = _full_piece(nm, outs[t], me, c)
                sends.append(remote(mine, mine, t, j, (*chip, c)))
            sends.append(pltpu.make_async_remote_copy(
                src_ref=small_in, dst_ref=small_out.at[me], send_sem=small_send.at[j], recv_sem=small_recv.at[j],
                device_id=(*chip, c), device_id_type=MESH))
        for cp in sends:
            cp.start()
        for j, (cx, cy) in enumerate(chips):
            kj = 2 * cx + cy
            for t, nm in enumerate(BIG):
                got = _full_piece(nm, outs[t], kj, c)
                remote(got, got, t, j, sibling).wait_recv()
                fwd = remote(got, got, t, 3 + j, sibling)
                fwd.start()
                sends.append(fwd)
        for j, (cx, cy) in enumerate(chips):
            kj = 2 * cx + cy
            for t, nm in enumerate(BIG):
                theirs = _full_piece(nm, outs[t], kj, 1 - c)
                remote(theirs, theirs, t, 3 + j, sibling).wait_recv()
            got = small_out.at[kj]
            pltpu.make_async_remote_copy(src_ref=got, dst_ref=got, send_sem=small_send.at[j],
                                         recv_sem=small_recv.at[j], device_id=sibling, device_id_type=MESH).wait_recv()
        for cp in sends:
            cp.wait_send()
        for cp in local:
            cp.wait()

    out_shape = [jax.ShapeDtypeStruct(FULL_SHAPES[nm], WIRE_DTYPE) for nm in BIG]
    out_shape.append(jax.ShapeDtypeStruct((N_CHIPS,) + small.shape, F32))
    res = pl.pallas_call(
        body, name="gather_weights", in_specs=[ANY] * (nb + 1), out_specs=[ANY] * (nb + 1), out_shape=out_shape,
        input_output_aliases={t: t for t in range(nb)},
        scratch_shapes=[pltpu.SemaphoreType.DMA((nb, 6)), pltpu.SemaphoreType.DMA((nb, 6)),
                        pltpu.SemaphoreType.DMA((1,)), pltpu.SemaphoreType.DMA((3,)),
                        pltpu.SemaphoreType.DMA((3,))],
    )(*[full[nm] for nm in BIG], small)
    return dict(zip(BIG, res[:nb])), res[nb]


def _other_half_src(name, ref, h):
    if name in ("w_in", "w_out"):
        return ref.at[h]
    if name == "w_qkv":
        return ref.at[_row(h, D_MODEL // 2), :]
    if name == "w_o":
        return ref.at[:, pl.ds(h, 1)]
    return ref.at[pl.ds(2 * h, 2)]


SLAB_SHAPES = {"w_in": (D_MODEL, 2 * D_FF), "w_out": (D_FF, D_MODEL), "w_qkv": (D_MODEL // 2, QKV_DIM),
               "w_o": (N_CHIPS, 1, W_O_ROWS // 2, D_MODEL), "pool_w": (2, POOL_GC, POOL_GC)}


def _pair_exchange(grads):
    nb = len(BIG)

    def body(*refs):
        ins, outs = refs[:nb], refs[nb:2 * nb]
        send_sems, recv_sems = refs[2 * nb:]
        x, y, c, _ = _mesh_place()
        copies = [pltpu.make_async_remote_copy(
            src_ref=_other_half_src(nm, ins[t], 1 - c), dst_ref=outs[t], send_sem=send_sems.at[t],
            recv_sem=recv_sems.at[t], device_id=(x, y, 1 - c), device_id_type=MESH) for t, nm in enumerate(BIG)]
        for cp in copies:
            cp.start()
        for cp in copies:
            cp.wait()

    res = pl.pallas_call(
        body, name="pair_exchange", in_specs=[ANY] * nb, out_specs=[ANY] * nb,
        out_shape=[jax.ShapeDtypeStruct(SLAB_SHAPES[nm], F32) for nm in BIG],
        scratch_shapes=[pltpu.SemaphoreType.DMA((nb,)), pltpu.SemaphoreType.DMA((nb,))],
    )(*[grads[nm] for nm in BIG])
    return dict(zip(BIG, res))


def _pair_sum(name, grad, slab, c_arr):
    shape = SLAB_SHAPES[name]
    if name == "w_in":
        blk = (256, 2 * D_FF)
        grid = (D_MODEL // 256,)
        g_spec = pl.BlockSpec((None,) + blk, lambda i, c: (c[0], i, 0))
        s_spec = pl.BlockSpec(blk, lambda i, c: (i, 0))
    elif name == "w_out":
        blk = (W_OUT_ROWS, D_MODEL)
        grid = (N_CHIPS,)
        g_spec = pl.BlockSpec((None,) + blk, lambda i, c: (c[0], i, 0))
        s_spec = pl.BlockSpec(blk, lambda i, c: (i, 0))
    elif name == "w_qkv":
        grid = (1,)
        g_spec = pl.BlockSpec(shape, lambda i, c: (c[0], 0))
        s_spec = pl.BlockSpec(shape, lambda i, c: (0, 0))
    elif name == "w_o":
        grid = (1,)
        g_spec = pl.BlockSpec(shape, lambda i, c: (0, c[0], 0, 0))
        s_spec = pl.BlockSpec(shape, lambda i, c: (0, 0, 0, 0))
    else:
        grid = (1,)
        g_spec = pl.BlockSpec(shape, lambda i, c: (c[0], 0, 0))
        s_spec = pl.BlockSpec(shape, lambda i, c: (0, 0, 0))

    def body(c_ref, g_ref, s_ref, o_ref):
        o_ref[...] = (g_ref[...] + s_ref[...]).astype(o_ref.dtype)

    return pl.pallas_call(
        body, name=f"pair_sum_{name}",
        grid_spec=pltpu.PrefetchScalarGridSpec(num_scalar_prefetch=1, grid=grid, in_specs=[g_spec, s_spec],
                                               out_specs=s_spec),
        out_shape=jax.ShapeDtypeStruct(shape, WIRE_DTYPE), compiler_params=_params("parallel"),
    )(c_arr, grad, slab)


def _slab_piece(name, ref, k):
    if name == "w_in":
        return ref.at[:, _col(k, W_IN_COLS)]
    if name == "w_out":
        return ref.at[_row(k, W_OUT_ROWS), :]
    if name == "w_qkv":
        return ref.at[:, _col(k, QKV_COLS)]
    if name == "w_o":
        return ref.at[k, 0]
    return ref.at[:, _row(k, POOL_ROWS), :]


def _chip_scatter(slabs):
    nb = len(BIG)

    def body(*refs):
        ins, outs = refs[:nb], refs[nb:2 * nb]
        send_sems, recv_sems = refs[2 * nb:]
        _, _, c, chips = _mesh_place()
        copies = []
        for j, (cx, cy) in enumerate(chips):
            for t, nm in enumerate(BIG):
                copies.append(pltpu.make_async_remote_copy(
                    src_ref=_slab_piece(nm, ins[t], 2 * cx + cy), dst_ref=outs[t].at[j], send_sem=send_sems.at[t, j],
                    recv_sem=recv_sems.at[t, j], device_id=(cx, cy, c), device_id_type=MESH))
        for cp in copies:
            cp.start()
        for j, (cx, cy) in enumerate(chips):
            for t, nm in enumerate(BIG):
                got = outs[t].at[j]
                pltpu.make_async_remote_copy(src_ref=got, dst_ref=got, send_sem=send_sems.at[t, j],
                                             recv_sem=recv_sems.at[t, j], device_id=(cx, cy, c),
                                             device_id_type=MESH).wait_recv()
        for cp in copies:
            cp.wait_send()

    res = pl.pallas_call(
        body, name="chip_scatter", in_specs=[ANY] * nb, out_specs=[ANY] * nb,
        out_shape=[jax.ShapeDtypeStruct((N_CHIPS - 1,) + HALF_SHAPES[nm], WIRE_DTYPE) for nm in BIG],
        scratch_shapes=[pltpu.SemaphoreType.DMA((nb, 3)), pltpu.SemaphoreType.DMA((nb, 3))],
    )(*[slabs[nm] for nm in BIG])
    return dict(zip(BIG, res))


def _sum_chips(name, slab, parts, place):
    half = HALF_SHAPES[name]
    if name == "w_in":
        blk, grid = (256, W_IN_COLS), (D_MODEL // 256,)
        own = pl.BlockSpec(blk, lambda i, p: (i, p[0]))
        got = pl.BlockSpec((N_CHIPS - 1,) + blk, lambda i, p: (0, i, 0))
        out = pl.BlockSpec((None,) + blk, lambda i, p: (p[1], i, 0))
    elif name == "w_out":
        grid = (1,)
        own = pl.BlockSpec(half, lambda i, p: (p[0], 0))
        got = pl.BlockSpec((N_CHIPS - 1,) + half, lambda i, p: (0, 0, 0))
        out = pl.BlockSpec((None,) + half, lambda i, p: (p[1], 0, 0))
    elif name == "w_qkv":
        grid = (1,)
        own = pl.BlockSpec(half, lambda i, p: (0, p[0]))
        got = pl.BlockSpec((N_CHIPS - 1,) + half, lambda i, p: (0, 0, 0))
        out = pl.BlockSpec(half, lambda i, p: (p[1], 0))
    elif name == "w_o":
        grid = (1,)
        own = pl.BlockSpec((None, None) + half, lambda i, p: (p[0], 0, 0, 0))
        got = pl.BlockSpec((N_CHIPS - 1,) + half, lambda i, p: (0, 0, 0))
        out = pl.BlockSpec(half, lambda i, p: (p[1], 0))
    else:
        grid = (1,)
        own = pl.BlockSpec(half, lambda i, p: (0, p[0], 0))
        got = pl.BlockSpec((N_CHIPS - 1,) + half, lambda i, p: (0, 0, 0, 0))
        out = pl.BlockSpec(half, lambda i, p: (p[1], 0, 0))

    def body(p_ref, own_ref, got_ref, o_ref):
        acc = own_ref[...].astype(F32)
        for j in range(N_CHIPS - 1):
            acc = acc + got_ref[j].astype(F32)
        o_ref[...] = acc

    return pl.pallas_call(
        body, name=f"sum_chips_{name}",
        grid_spec=pltpu.PrefetchScalarGridSpec(num_scalar_prefetch=1, grid=grid, in_specs=[own, got], out_specs=out),
        out_shape=jax.ShapeDtypeStruct(SHARD_SHAPES[name], F32), compiler_params=_params("parallel"),
    )(place, slab, parts)


def _pair_allgather(shards):
    nb = len(BIG)

    def body(*refs):
        outs = refs[nb:2 * nb]
        send_sems, recv_sems = refs[2 * nb:]
        x, y, c, _ = _mesh_place()
        copies = []
        for t, nm in enumerate(BIG):
            mine = _shard_half(nm, outs[t], c)
            copies.append(pltpu.make_async_remote_copy(
                src_ref=mine, dst_ref=mine, send_sem=send_sems.at[t], recv_sem=recv_sems.at[t],
                device_id=(x, y, 1 - c), device_id_type=MESH))
        for cp in copies:
            cp.start()
        for t, nm in enumerate(BIG):
            theirs = _shard_half(nm, outs[t], 1 - c)
            pltpu.make_async_remote_copy(src_ref=theirs, dst_ref=theirs, send_sem=send_sems.at[t],
                                         recv_sem=recv_sems.at[t], device_id=(x, y, 1 - c),
                                         device_id_type=MESH).wait_recv()
        for cp in copies:
            cp.wait_send()

    res = pl.pallas_call(
        body, name="pair_allgather", in_specs=[ANY] * nb, out_specs=[ANY] * nb,
        out_shape=[jax.ShapeDtypeStruct(SHARD_SHAPES[nm], F32) for nm in BIG],
        input_output_aliases={t: t for t in range(nb)},
        scratch_shapes=[pltpu.SemaphoreType.DMA((nb,)), pltpu.SemaphoreType.DMA((nb,))],
    )(*[shards[nm] for nm in BIG])
    return dict(zip(BIG, res))


N_DEV = 8


def _all_reduce_small(vec):
    def body(v_ref, o_ref, slots, send_sems, recv_sems):
        x, y, c, _ = _mesh_place()
        me = 4 * x + 2 * y + c
        slots[me] = v_ref[...]
        copies = []
        for d in range(1, N_DEV):
            px, py, pc = x ^ (d >> 2), y ^ ((d >> 1) & 1), c ^ (d & 1)
            copies.append(pltpu.make_async_remote_copy(
                src_ref=v_ref, dst_ref=slots.at[me], send_sem=send_sems.at[d - 1], recv_sem=recv_sems.at[d - 1],
                device_id=(px, py, pc), device_id_type=MESH))
        for cp in copies:
            cp.start()
        for d in range(1, N_DEV):
            peer = 4 * (x ^ (d >> 2)) + 2 * (y ^ ((d >> 1) & 1)) + (c ^ (d & 1))
            got = slots.at[peer]
            pltpu.make_async_remote_copy(src_ref=got, dst_ref=got, send_sem=send_sems.at[d - 1],
                                         recv_sem=recv_sems.at[d - 1], device_id=(x, y, c),
                                         device_id_type=MESH).wait_recv()
        for cp in copies:
            cp.wait_send()
        acc = slots[0]
        for d in range(1, N_DEV):
            acc = acc + slots[d]
        o_ref[...] = acc

    vm = pl.BlockSpec(memory_space=pltpu.VMEM)
    return pl.pallas_call(
        body, name="all_reduce_small", in_specs=[vm], out_specs=vm, out_shape=jax.ShapeDtypeStruct(vec.shape, F32),
        scratch_shapes=[pltpu.VMEM((N_DEV,) + vec.shape, F32), pltpu.SemaphoreType.DMA((N_DEV - 1,)),
                        pltpu.SemaphoreType.DMA((N_DEV - 1,))],
    )(vec)


def _adamw_call(w, g, m, v, name):
    shape = w.shape
    cols = shape[-1]
    rows = w.size // cols
    tr = rows
    for cand in (256, 128, 64, 32, 16, 8):
        if rows > cand and rows % cand == 0 and cand * cols * 4 <= 2 * 1024 * 1024:
            tr = cand
            break

    def body(w_ref, g_ref, m_ref, v_ref, d_ref, nm_ref, nv_ref):
        gv = g_ref[...]
        mn = ADAM_B1 * m_ref[...] + (1.0 - ADAM_B1) * gv
        vn = ADAM_B2 * v_ref[...] + (1.0 - ADAM_B2) * jnp.square(gv)
        m_hat = mn / (1.0 - ADAM_B1 ** ADAM_STEP)
        v_hat = vn / (1.0 - ADAM_B2 ** ADAM_STEP)
        d_ref[...] = -ADAM_LR * (m_hat / (jnp.sqrt(v_hat) + ADAM_EPS) + ADAM_WD * w_ref[...])
        nm_ref[...] = mn
        nv_ref[...] = vn

    spec = pl.BlockSpec((tr, cols), lambda i: (i, 0))
    flat = lambda a: a.reshape(rows, cols)
    outs = pl.pallas_call(
        body, name=name, grid=(rows // tr,), in_specs=[spec] * 4, out_specs=[spec] * 3,
        out_shape=[jax.ShapeDtypeStruct((rows, cols), F32)] * 3, compiler_params=_params("parallel"),
    )(flat(w), flat(g), flat(m), flat(v))
    return tuple(o.reshape(shape) for o in outs)


WEIGHTS = ("pool_w", "pool_b", "pool_scale", "attn_w_qkv", "attn_b_qkv", "attn_sinks", "attn_w_o", "norm_mix",
           "norm_ffn", "ffn_w_in", "ffn_conv_w", "ffn_conv_b", "ffn_w_out", "norm_f")
SMALL_GATHER = 128 * 71
_SMALL_SIZES = (("pool_b", 1024), ("pool_scale", 1024), ("b_qkv", 1536), ("sinks", 128), ("norm_mix", 2048),
                ("norm_ffn", 2048), ("conv_w", 6 * 2 * D_FF), ("conv_b", 4 * D_FF), ("norm_f", 1024))
SMALL_COLS = 6784


def _pack_small(grads):
    flat = jnp.concatenate([jnp.pad(grads[nm].reshape(-1), (0, size - grads[nm].size)) for nm, size in _SMALL_SIZES])
    return jnp.pad(flat, (0, 8 * SMALL_COLS - flat.size)).reshape(8, SMALL_COLS)


def _unpack_small(vec):
    flat = vec.reshape(-1)
    out, off = {}, 0
    for nm, size in _SMALL_SIZES:
        out[nm] = flat[off:off + size]
        off += size
    return out


def kernel(x, pool_w, pool_b, pool_scale, attn_w_qkv, attn_b_qkv, attn_sinks, attn_w_o, norm_mix, norm_ffn, ffn_w_in, ffn_conv_w, ffn_conv_b, ffn_w_out, norm_f, loss_target, m_pool_w, m_pool_b, m_pool_scale, m_attn_w_qkv, m_attn_b_qkv, m_attn_sinks, m_attn_w_o, m_norm_mix, m_norm_ffn, m_ffn_w_in, m_ffn_conv_w, m_ffn_conv_b, m_ffn_w_out, m_norm_f, v_pool_w, v_pool_b, v_pool_scale, v_attn_w_qkv, v_attn_b_qkv, v_attn_sinks, v_attn_w_o, v_norm_mix, v_norm_ffn, v_ffn_w_in, v_ffn_conv_w, v_ffn_conv_b, v_ffn_w_out, v_norm_f):
    weights = dict(pool_w=pool_w, pool_b=pool_b, pool_scale=pool_scale, attn_w_qkv=attn_w_qkv, attn_b_qkv=attn_b_qkv,
                   attn_sinks=attn_sinks, attn_w_o=attn_w_o, norm_mix=norm_mix, norm_ffn=norm_ffn, ffn_w_in=ffn_w_in,
                   ffn_conv_w=ffn_conv_w, ffn_conv_b=ffn_conv_b, ffn_w_out=ffn_w_out, norm_f=norm_f)
    m_in = dict(pool_w=m_pool_w, pool_b=m_pool_b, pool_scale=m_pool_scale, attn_w_qkv=m_attn_w_qkv,
                attn_b_qkv=m_attn_b_qkv, attn_sinks=m_attn_sinks, attn_w_o=m_attn_w_o, norm_mix=m_norm_mix,
                norm_ffn=m_norm_ffn, ffn_w_in=m_ffn_w_in, ffn_conv_w=m_ffn_conv_w, ffn_conv_b=m_ffn_conv_b,
                ffn_w_out=m_ffn_w_out, norm_f=m_norm_f)
    v_in = dict(pool_w=v_pool_w, pool_b=v_pool_b, pool_scale=v_pool_scale, attn_w_qkv=v_attn_w_qkv,
                attn_b_qkv=v_attn_b_qkv, attn_sinks=v_attn_sinks, attn_w_o=v_attn_w_o, norm_mix=v_norm_mix,
                norm_ffn=v_norm_ffn, ffn_w_in=v_ffn_w_in, ffn_conv_w=v_ffn_conv_w, ffn_conv_b=v_ffn_conv_b,
                ffn_w_out=v_ffn_w_out, norm_f=v_norm_f)
    chip = 2 * lax.axis_index("x") + lax.axis_index("y")
    core = lax.axis_index("c")

    place = jnp.stack([chip, core]).astype(jnp.int32)
    shards = {"w_in": ffn_w_in, "w_out": ffn_w_out, "w_qkv": attn_w_qkv[0], "w_o": attn_w_o[0], "pool_w": pool_w[0]}
    placed = {nm: _place_shard(nm, shards[nm], place) for nm in BIG}
    small = jnp.concatenate([pool_b.reshape(-1), attn_b_qkv.reshape(-1), ffn_conv_w.reshape(-1)]).reshape(1, SMALL_GATHER)
    full, small_all = _gather_weights(placed, small)
    small_all = small_all.reshape(N_CHIPS, SMALL_GATHER)
    pool_b_full = small_all[:, :256].reshape(N_CHIPS, 4, POOL_ROWS).transpose(1, 0, 2).reshape(1, D_MODEL)
    b_qkv_full = small_all[:, 256:640].reshape(1, QKV_DIM)
    conv_w_full = small_all[:, 640:].reshape(N_CHIPS, 2, 3, W_IN_COLS).transpose(1, 2, 0, 3).reshape(2, 3, 2 * D_FF)

    loss, grad_x, grads = _local_step(
        x[0], loss_target[0], full, norm_mix=norm_mix, norm_ffn=norm_ffn, norm_f=norm_f.reshape(1, D_MODEL),
        pool_b=pool_b_full, pool_scale=pool_scale, b_qkv=b_qkv_full, sinks=attn_sinks[0], conv_w=conv_w_full,
        conv_b=ffn_conv_b)
    loss = lax.psum(loss[0, 0], ("x", "y", "c"))

    big = {nm: grads[nm] for nm in BIG}
    big["w_o"] = big["w_o"].reshape(N_CHIPS, 2, W_O_ROWS // 2, D_MODEL)
    slabs = _pair_exchange(big)
    sums = {nm: _pair_sum(nm, big[nm], slabs[nm], place[1:]) for nm in BIG}
    parts = _chip_scatter(sums)
    halves = {nm: _sum_chips(nm, sums[nm], parts[nm], place) for nm in BIG}
    reduced = _pair_allgather(halves)

    small_g = _unpack_small(_all_reduce_small(_pack_small(grads)))
    pool_b_g = lax.dynamic_slice_in_dim(small_g["pool_b"].reshape(4, N_CHIPS, POOL_ROWS), chip, 1, axis=1)
    b_qkv_g = lax.dynamic_slice_in_dim(small_g["b_qkv"].reshape(N_CHIPS, QKV_COLS), chip, 1, axis=0)
    conv_w_g = lax.dynamic_slice_in_dim(small_g["conv_w"].reshape(2, 3, N_CHIPS, W_IN_COLS), chip, 1, axis=2)
    grad_w = {
        "pool_w": reduced["pool_w"].reshape(pool_w.shape), "pool_b": pool_b_g.reshape(pool_b.shape),
        "pool_scale": small_g["pool_scale"].reshape(pool_scale.shape),
        "attn_w_qkv": reduced["w_qkv"].reshape(attn_w_qkv.shape), "attn_b_qkv": b_qkv_g.reshape(attn_b_qkv.shape),
        "attn_sinks": small_g["sinks"][:N_HEADS].reshape(attn_sinks.shape),
        "attn_w_o": reduced["w_o"].reshape(attn_w_o.shape), "norm_mix": small_g["norm_mix"].reshape(norm_mix.shape),
        "norm_ffn": small_g["norm_ffn"].reshape(norm_ffn.shape), "ffn_w_in": reduced["w_in"],
        "ffn_conv_w": conv_w_g.reshape(ffn_conv_w.shape), "ffn_conv_b": small_g["conv_b"].reshape(ffn_conv_b.shape),
        "ffn_w_out": reduced["w_out"], "norm_f": small_g["norm_f"].reshape(norm_f.shape),
    }

    delta, new_m, new_v = {}, {}, {}
    for nm in WEIGHTS:
        delta[nm], new_m[nm], new_v[nm] = _adamw_call(weights[nm], grad_w[nm], m_in[nm], v_in[nm], f"adamw_{nm}")
    return (loss, grad_x.reshape(x.shape), *[grad_w[nm] for nm in WEIGHTS], *[delta[nm] for nm in WEIGHTS],
            *[new_m[nm] for nm in WEIGHTS], *[new_v[nm] for nm in WEIGHTS])
```

```python
import functools
from typing import Callable, NamedTuple

import jax
import jax.numpy as jnp
from jax import lax
from jax.experimental import pallas as pl
from jax.experimental.pallas import tpu as pltpu

F32 = jnp.float32
MXU_DTYPE = jnp.bfloat16
WIRE_DTYPE = jnp.bfloat16

D_MODEL = 1024
D_FF = 2816
QKV_DIM = 1536
HEAD_DIM = 64
N_HEADS = 16
N_KV_HEADS = 4
ATT_BLOCK = 128
POOL_WINDOWS = (2, 4, 8, 16)
POOL_GC = 256
POOL_HALO = 16
CONV_HALO = 8
RMS_EPS = 1e-6
ATT_SCALE = HEAD_DIM ** -0.5
ALIBI_SLOPES = tuple(2.0 ** (-8.0 / N_HEADS * (h + 1)) for h in range(N_HEADS))

ADAM_LR = 0.001
ADAM_B1 = 0.9
ADAM_B2 = 0.999
ADAM_EPS = 1e-08
ADAM_WD = 0.01
ADAM_STEP = 10

N_CHIPS = 4
MESH = pl.DeviceIdType.MESH
VMEM_LIMIT_BYTES = 56 * 1024 * 1024
ANY = pl.BlockSpec(memory_space=pl.ANY)


def _params(*semantics):
    return pltpu.CompilerParams(dimension_semantics=semantics, vmem_limit_bytes=VMEM_LIMIT_BYTES)


def _rms(x, g):
    return x * lax.rsqrt(jnp.mean(x * x, axis=-1, keepdims=True) + RMS_EPS) * g


def _rms_bwd(x, g, dy):
    rstd = lax.rsqrt(jnp.mean(x * x, axis=-1, keepdims=True) + RMS_EPS)
    xhat = x * rstd
    dxhat = dy * g
    dx = rstd * (dxhat - xhat * jnp.mean(dxhat * xhat, axis=-1, keepdims=True))
    return dx, dy * xhat


def _shift_down(x, s):
    return pltpu.roll(x, s, axis=0)


def _shift_up(x, s):
    return pltpu.roll(x, x.shape[0] - s, axis=0)


def _sigmoid(x):
    return 1.0 / (1.0 + jnp.exp(-x))


_DOT_DIMS = {"nn": ((1,), (0,)), "nt": ((1,), (1,)), "tn": ((0,), (0,))}


class _Hosted(NamedTuple):
    arrays: list
    sems: list
    first: Callable
    last: Callable


def _grid_edges(grid):
    ids = [pl.program_id(d) for d in range(len(grid))]
    first = functools.reduce(lambda p, q: p & q, [i == 0 for i in ids])
    last = functools.reduce(lambda p, q: p & q, [i == n - 1 for i, n in zip(ids, grid)])
    return first, last


def _matmul(a, b, *, mode, name, grid, a_spec, b_spec, o_spec, out_shape, bias=None, bias_spec=None, residual=None,
            residual_spec=None, into=None, hosted=None):
    nk = grid[2]
    dims = (_DOT_DIMS[mode], ((), ()))
    acc_shape = tuple(d for d in o_spec.block_shape if d is not None)
    n_host = len(hosted.arrays) if hosted else 0

    def body(*refs):
        refs = list(refs)
        a_ref, b_ref = refs[0], refs[1]
        pos = 2
        bias_ref = res_ref = None
        if bias is not None:
            bias_ref = refs[pos]
            pos += 1
        if residual is not None:
            res_ref = refs[pos]
            pos += 1
        if into is not None:
            pos += 1
        pos += n_host
        o_ref = refs[pos]
        host_refs = refs[pos + 1:pos + 1 + n_host]
        pos += 1 + n_host
        acc_ref = refs[pos] if nk > 1 else None
        host_sems = refs[pos + (1 if nk > 1 else 0):]
        if hosted:
            is_first, is_last = _grid_edges(grid)

            @pl.when(is_first)
            def _():
                hosted.first(host_refs, host_sems)

        prod = lax.dot_general(a_ref[...].astype(MXU_DTYPE), b_ref[...].astype(MXU_DTYPE), dims,
                               preferred_element_type=F32)

        def finish(acc):
            if bias_ref is not None:
                acc = acc + bias_ref[...]
            if res_ref is not None:
                acc = acc + res_ref[...]
            o_ref[...] = acc.astype(o_ref.dtype)

        if nk == 1:
            finish(prod)
        else:
            k = pl.program_id(2)

            @pl.when(k == 0)
            def _():
                acc_ref[...] = prod

            @pl.when(k > 0)
            def _():
                acc_ref[...] += prod

            @pl.when(k == nk - 1)
            def _():
                finish(acc_ref[...])

        if hosted:
            @pl.when(is_last)
            def _():
                hosted.last(host_refs, host_sems)

    operands, in_specs = [a, b], [a_spec, b_spec]
    if bias is not None:
        operands.append(bias)
        in_specs.append(bias_spec)
    if residual is not None:
        operands.append(residual)
        in_specs.append(residual_spec)
    aliases = {}
    if into is not None:
        aliases = {len(operands): 0}
        operands.append(into)
        in_specs.append(ANY)
    scratch = [pltpu.VMEM(acc_shape, F32)] if nk > 1 else []
    if not hosted:
        return pl.pallas_call(
            body, name=name, grid=grid, in_specs=in_specs, out_specs=o_spec, out_shape=out_shape,
            scratch_shapes=scratch, input_output_aliases=aliases,
            compiler_params=_params("parallel", "parallel", "arbitrary"),
        )(*operands)
    for t, arr in enumerate(hosted.arrays):
        aliases[len(operands)] = 1 + t
        operands.append(arr)
        in_specs.append(ANY)
    return pl.pallas_call(
        body, name=name, grid=grid, in_specs=in_specs, out_specs=[o_spec] + [ANY] * n_host,
        out_shape=[out_shape] + [jax.ShapeDtypeStruct(arr.shape, arr.dtype) for arr in hosted.arrays],
        scratch_shapes=scratch + list(hosted.sems), input_output_aliases=aliases,
        compiler_params=_params("arbitrary", "arbitrary", "arbitrary"),
    )(*operands)


def _tile(n, want):
    t = min(n, want)
    assert n % t == 0, (n, want)
    return t


def _mm_nn(a, b, name, *, b_lead=None, bias=None, residual=None, out_dtype=F32, tm=1024, tn=1024, tk=1024,
           hosted=None):
    (m, k), n = a.shape, b.shape[-1]
    tm, tn, tk = _tile(m, tm), _tile(n, tn), _tile(k, tk)
    if b_lead is None:
        b_spec = pl.BlockSpec((tk, tn), lambda i, j, kk: (kk, j))
    else:
        b_spec = pl.BlockSpec((None, tk, tn), lambda i, j, kk: (b_lead, kk, j))
    return _matmul(
        a, b, mode="nn", name=name, grid=(m // tm, n // tn, k // tk),
        a_spec=pl.BlockSpec((tm, tk), lambda i, j, kk: (i, kk)), b_spec=b_spec,
        o_spec=pl.BlockSpec((tm, tn), lambda i, j, kk: (i, j)), out_shape=jax.ShapeDtypeStruct((m, n), out_dtype),
        bias=bias, bias_spec=pl.BlockSpec((1, tn), lambda i, j, kk: (0, j)),
        residual=residual, residual_spec=pl.BlockSpec((tm, tn), lambda i, j, kk: (i, j)), hosted=hosted)


def _mm_nt(a, b, name, *, b_lead=None, out_dtype=F32, tm=1024, tn=1024, tk=1024):
    (m, k), n = a.shape, b.shape[-2]
    tm, tn, tk = _tile(m, tm), _tile(n, tn), _tile(k, tk)
    if b_lead is None:
        b_spec = pl.BlockSpec((tn, tk), lambda i, j, kk: (j, kk))
    else:
        b_spec = pl.BlockSpec((None, tn, tk), lambda i, j, kk: (b_lead, j, kk))
    return _matmul(
        a, b, mode="nt", name=name, grid=(m // tm, n // tn, k // tk),
        a_spec=pl.BlockSpec((tm, tk), lambda i, j, kk: (i, kk)), b_spec=b_spec,
        o_spec=pl.BlockSpec((tm, tn), lambda i, j, kk: (i, j)), out_shape=jax.ShapeDtypeStruct((m, n), out_dtype))


def _mm_tn(a, b, name, *, lead=None, n_lead=None, into=None, tm=1024, tn=1024, tk=1024):
    (k, m), n = a.shape, b.shape[-1]
    tm, tn, tk = _tile(m, tm), _tile(n, tn), _tile(k, tk)
    if lead is None:
        o_spec = pl.BlockSpec((tm, tn), lambda i, j, kk: (i, j))
        out_shape = jax.ShapeDtypeStruct((m, n), F32)
    else:
        o_spec = pl.BlockSpec((None, tm, tn), lambda i, j, kk: (lead, i, j))
        out_shape = jax.ShapeDtypeStruct((n_lead, m, n), F32)
    return _matmul(
        a, b, mode="tn", name=name, grid=(m // tm, n // tn, k // tk),
        a_spec=pl.BlockSpec((tk, tm), lambda i, j, kk: (kk, i)),
        b_spec=pl.BlockSpec((tk, tn), lambda i, j, kk: (kk, j)), o_spec=o_spec, out_shape=out_shape, into=into)


ROW_TILE = 512


def _rms_fwd_call(h, g, name, out_dtype):
    t = h.shape[0]
    tt = _tile(t, ROW_TILE)

    def body(h_ref, g_ref, o_ref):
        o_ref[...] = _rms(h_ref[...], g_ref[...]).astype(o_ref.dtype)

    row = pl.BlockSpec((tt, D_MODEL), lambda i: (i, 0))
    vec = pl.BlockSpec((1, D_MODEL), lambda i: (0, 0))
    return pl.pallas_call(body, name=name, grid=(t // tt,), in_specs=[row, vec], out_specs=row,
                          out_shape=jax.ShapeDtypeStruct(h.shape, out_dtype), compiler_params=_params("parallel"))(h, g)


def _rms_bwd_call(h, g, dy, dres, name):
    t = h.shape[0]
    tt = _tile(t, ROW_TILE)

    def body(h_ref, g_ref, dy_ref, dres_ref, dx_ref, dg_ref):
        dx, dgt = _rms_bwd(h_ref[...], g_ref[...], dy_ref[...].astype(F32))
        dx_ref[...] = dres_ref[...] + dx

        @pl.when(pl.program_id(0) == 0)
        def _():
            dg_ref[...] = jnp.zeros_like(dg_ref)

        dg_ref[...] += jnp.sum(dgt, axis=0, keepdims=True)

    row = pl.BlockSpec((tt, D_MODEL), lambda i: (i, 0))
    vec = pl.BlockSpec((1, D_MODEL), lambda i: (0, 0))
    return pl.pallas_call(
        body, name=name, grid=(t // tt,), in_specs=[row, vec, row, row], out_specs=[row, vec],
        out_shape=[jax.ShapeDtypeStruct(h.shape, F32), jax.ShapeDtypeStruct((1, D_MODEL), F32)],
        compiler_params=_params("arbitrary"))(h, g, dy, dres)


def _loss_call(h, g, target):
    t = h.shape[0]
    tt = _tile(t, ROW_TILE)

    def body(h_ref, g_ref, t_ref, loss_ref, dh_ref, dg_ref):
        x, gv = h_ref[...], g_ref[...]
        err = _rms(x, gv) - t_ref[...]
        dx, dgt = _rms_bwd(x, gv, err * (1.0 / D_MODEL))
        dh_ref[...] = dx

        @pl.when(pl.program_id(0) == 0)
        def _():
            dg_ref[...] = jnp.zeros_like(dg_ref)
            loss_ref[...] = jnp.zeros_like(loss_ref)

        dg_ref[...] += jnp.sum(dgt, axis=0, keepdims=True)
        per_token = jnp.mean(err * err, axis=-1, keepdims=True)
        loss_ref[...] += 0.5 * jnp.sum(per_token, axis=0, keepdims=True)

    row = pl.BlockSpec((tt, D_MODEL), lambda i: (i, 0))
    vec = pl.BlockSpec((1, D_MODEL), lambda i: (0, 0))
    one = pl.BlockSpec((1, 1), lambda i: (0, 0))
    return pl.pallas_call(
        body, name="loss_head", grid=(t // tt,), in_specs=[row, vec, row], out_specs=[one, row, vec],
        out_shape=[jax.ShapeDtypeStruct((1, 1), F32), jax.ShapeDtypeStruct(h.shape, F32),
                   jax.ShapeDtypeStruct((1, D_MODEL), F32)],
        compiler_params=_params("arbitrary"))(h, g, target)


def _pool_windows_causal(ext, first_row):
    tt = ext.shape[0] - POOL_HALO
    t = first_row + lax.broadcasted_iota(jnp.int32, (tt, 1), 0)
    outs = []
    for gi, win in enumerate(POOL_WINDOWS):
        cols = slice(gi * POOL_GC, (gi + 1) * POOL_GC)
        s = ext[:, cols]
        sh = 1
        while sh < win:
            s = s + _shift_down(s, sh)
            sh *= 2
        count = jnp.minimum(t + 1, win).astype(F32)
        outs.append(s[POOL_HALO:] / count - ext[POOL_HALO:, cols])
    return outs


def _prev_halo_spec(tt, halo, width):
    per = tt // halo
    return pl.BlockSpec((halo, width), lambda i: (jnp.maximum(i * per - 1, 0), 0))


def _pool_fwd_call(h0, g_mix, pool_w, pool_b, pool_scale, g_ffn):
    t = h0.shape[0]
    tt = _tile(t, ROW_TILE)

    def body(h_ref, halo_ref, gm_ref, w_ref, b_ref, sc_ref, gf_ref, h1_ref, hn_ref):
        i = pl.program_id(0)
        x, gm = h_ref[...], gm_ref[...]
        top = jnp.where(i > 0, _rms(halo_ref[...], gm), 0.0)
        ext = jnp.concatenate([top, _rms(x, gm)], axis=0)
        ps = _pool_windows_causal(ext, i * tt)
        ys = [jnp.dot(p.astype(MXU_DTYPE), w_ref[gi], preferred_element_type=F32) for gi, p in enumerate(ps)]
        mix = (jnp.concatenate(ys, axis=1) + b_ref[...]) * sc_ref[...]
        h1 = x + mix
        h1_ref[...] = h1
        hn_ref[...] = _rms(h1, gf_ref[...]).astype(hn_ref.dtype)

    row = pl.BlockSpec((tt, D_MODEL), lambda i: (i, 0))
    vec = pl.BlockSpec((1, D_MODEL), lambda i: (0, 0))
    wsp = pl.BlockSpec((len(POOL_WINDOWS), POOL_GC, POOL_GC), lambda i: (0, 0, 0))
    return pl.pallas_call(
        body, name="pool_fwd", grid=(t // tt,),
        in_specs=[row, _prev_halo_spec(tt, POOL_HALO, D_MODEL), vec, wsp, vec, vec, vec], out_specs=[row, row],
        out_shape=[jax.ShapeDtypeStruct(h0.shape, F32), jax.ShapeDtypeStruct(h0.shape, MXU_DTYPE)],
        compiler_params=_params("parallel"))(h0, h0, g_mix, pool_w, pool_b, pool_scale, g_ffn)


def _pool_bwd_call(h0, dh1, g_mix, pool_w, pool_b, pool_scale):
    t = h0.shape[0]
    tt = _tile(t, ROW_TILE)
    nt = t // tt
    per = tt // POOL_HALO

    def body(h_ref, halo_ref, d_ref, dnext_ref, gm_ref, w_ref, b_ref, sc_ref, gx_ref, dw_ref, dv_ref):
        i = pl.program_id(0)
        x, gm, sc = h_ref[...], gm_ref[...], sc_ref[...]
        top = jnp.where(i > 0, _rms(halo_ref[...], gm), 0.0)
        ext = jnp.concatenate([top, _rms(x, gm)], axis=0)
        ps = _pool_windows_causal(ext, i * tt)

        dy = d_ref[...]
        dy_ext = jnp.concatenate([dy, jnp.where(i < nt - 1, dnext_ref[...], 0.0)], axis=0)
        dpre_ext = dy_ext * sc
        t_ext = i * tt + lax.broadcasted_iota(jnp.int32, (tt + POOL_HALO, 1), 0)

        @pl.when(i == 0)
        def _():
            dw_ref[...] = jnp.zeros_like(dw_ref)
            dv_ref[...] = jnp.zeros_like(dv_ref)

        dhn, ypre = [], []
        for gi, win in enumerate(POOL_WINDOWS):
            cols = slice(gi * POOL_GC, (gi + 1) * POOL_GC)
            w = w_ref[gi]
            p16 = ps[gi].astype(MXU_DTYPE)
            dpre16 = dpre_ext[:, cols].astype(MXU_DTYPE)
            ypre.append(jnp.dot(p16, w, preferred_element_type=F32))
            dw_ref[gi] += lax.dot_general(p16, dpre16[:tt], (((0,), (0,)), ((), ())), preferred_element_type=F32)
            dp_ext = lax.dot_general(dpre16, w, (((1,), (1,)), ((), ())), preferred_element_type=F32)
            s = dp_ext / jnp.minimum(t_ext + 1, win).astype(F32)
            sh = 1
            while sh < win:
                s = s + _shift_up(s, sh)
                sh *= 2
            dhn.append(s[:tt] - dp_ext[:tt])
        dhn = jnp.concatenate(dhn, axis=1)
        ypre = jnp.concatenate(ypre, axis=1) + b_ref[...]
        dx, dgt = _rms_bwd(x, gm, dhn)
        gx_ref[...] = dy + dx
        dv_ref[0:1, :] += jnp.sum(dpre_ext[:tt], axis=0, keepdims=True)
        dv_ref[1:2, :] += jnp.sum(dy * ypre, axis=0, keepdims=True)
        dv_ref[2:3, :] += jnp.sum(dgt, axis=0, keepdims=True)

    row = pl.BlockSpec((tt, D_MODEL), lambda i: (i, 0))
    vec = pl.BlockSpec((1, D_MODEL), lambda i: (0, 0))
    nxt = pl.BlockSpec((POOL_HALO, D_MODEL), lambda i: (jnp.minimum((i + 1) * per, t // POOL_HALO - 1), 0))
    wsp = pl.BlockSpec((len(POOL_WINDOWS), POOL_GC, POOL_GC), lambda i: (0, 0, 0))
    return pl.pallas_call(
        body, name="pool_bwd", grid=(nt,),
        in_specs=[row, _prev_halo_spec(tt, POOL_HALO, D_MODEL), row, nxt, vec, wsp, vec, vec],
        out_specs=[row, wsp, pl.BlockSpec((8, D_MODEL), lambda i: (0, 0))],
        out_shape=[jax.ShapeDtypeStruct(h0.shape, F32),
                   jax.ShapeDtypeStruct((len(POOL_WINDOWS), POOL_GC, POOL_GC), F32),
                   jax.ShapeDtypeStruct((8, D_MODEL), F32)],
        compiler_params=_params("arbitrary"))(h0, h0, dh1, dh1, g_mix, pool_w, pool_b, pool_scale)


CONV_ROWS = 256
CONV_SUB = 64
LANES = 128


def _causal_conv(u_ext, w, b):
    return w[0:1] * _shift_down(u_ext, 2) + w[1:2] * _shift_down(u_ext, 1) + w[2:3] * u_ext + b


def _fold8(x):
    acc = x[0:8]
    for r in range(8, x.shape[0], 8):
        acc = acc + x[r:r + 8]
    return acc


def _conv_glu_fwd_call(u, conv_w, conv_b, name, hosted=None):
    _, t, f = u.shape
    tt = _tile(t, CONV_ROWS)
    sub = _tile(tt, CONV_SUB)
    per = tt // CONV_HALO
    n_host = len(hosted.arrays) if hosted else 0

    def body(u_ref, up_ref, w_ref, b_ref, *rest):
        z_ref = rest[n_host]
        host_refs = rest[n_host + 1:2 * n_host + 1]
        ext_ref = rest[2 * n_host + 1]
        host_sems = rest[2 * n_host + 2:]
        i = pl.program_id(0)
        if hosted:
            @pl.when(i == 0)
            def _():
                hosted.first(host_refs, host_sems)

        ext_ref[:, :CONV_HALO, :] = jnp.where(i > 0, up_ref[...], 0.0)
        ext_ref[:, CONV_HALO:, :] = u_ref[...]

        def chunk(j, carry):
            lanes = pl.ds(pl.multiple_of(j * LANES, LANES), LANES)
            w, b = w_ref[:, :, lanes], b_ref[:, lanes]
            for s in range(tt // sub):
                rows = pl.ds(s * sub, sub + CONV_HALO)
                c = [_causal_conv(ext_ref[hf, rows, lanes], w[:, hf, :], b[hf:hf + 1, :])[CONV_HALO:] for hf in range(2)]
                z_ref[pl.ds(s * sub, sub), lanes] = (c[0] * c[1] * _sigmoid(c[1])).astype(z_ref.dtype)
            return carry

        lax.fori_loop(0, f // LANES, chunk, 0)
        if hosted:
            @pl.when(i == t // tt - 1)
            def _():
                hosted.last(host_refs, host_sems)

    in_specs = [pl.BlockSpec((2, tt, f), lambda i: (0, i, 0)),
                pl.BlockSpec((2, CONV_HALO, f), lambda i: (0, jnp.maximum(i * per - 1, 0), 0)),
                pl.BlockSpec((3, 2, f), lambda i: (0, 0, 0)), pl.BlockSpec((2, f), lambda i: (0, 0))]
    z_spec = pl.BlockSpec((tt, f), lambda i: (i, 0))
    z_shape = jax.ShapeDtypeStruct((t, f), MXU_DTYPE)
    scratch = [pltpu.VMEM((2, CONV_HALO + tt, f), F32)]
    if not hosted:
        return pl.pallas_call(body, name=name, grid=(t // tt,), in_specs=in_specs, out_specs=z_spec, out_shape=z_shape,
                              scratch_shapes=scratch, compiler_params=_params("parallel"))(u, u, conv_w, conv_b)
    return pl.pallas_call(
        body, name=name, grid=(t // tt,), in_specs=in_specs + [ANY] * n_host, out_specs=[z_spec] + [ANY] * n_host,
        out_shape=[z_shape] + [jax.ShapeDtypeStruct(arr.shape, arr.dtype) for arr in hosted.arrays],
        scratch_shapes=scratch + list(hosted.sems), input_output_aliases={4 + j: 1 + j for j in range(n_host)},
        compiler_params=_params("arbitrary"))(u, u, conv_w, conv_b, *hosted.arrays)


def _conv_glu_bwd_call(u, dz, conv_w, conv_b, name):
    _, t, f = u.shape
    tt = _tile(t, CONV_ROWS)
    sub = _tile(tt, CONV_SUB)
    nt = t // tt
    per = tt // CONV_HALO
    halo = CONV_HALO

    def body(u_ref, up_ref, un_ref, dz_ref, dzn_ref, w_ref, b_ref, du_ref, dw_ref, db_ref, uext, dzext, dwacc, dbacc):
        i = pl.program_id(0)
        uext[:, :halo, :] = jnp.where(i > 0, up_ref[...], 0.0)
        uext[:, halo:halo + tt, :] = u_ref[...]
        uext[:, halo + tt:, :] = un_ref[...]
        dzext[:halo, :] = jnp.zeros((halo, f), F32)
        dzext[halo:halo + tt, :] = dz_ref[...].astype(F32)
        dzext[halo + tt:, :] = jnp.where(i < nt - 1, dzn_ref[...].astype(F32), 0.0)

        @pl.when(i == 0)
        def _():
            dwacc[...] = jnp.zeros_like(dwacc)
            dbacc[...] = jnp.zeros_like(dbacc)

        main = slice(halo, halo + sub)

        def chunk(j, carry):
            lanes = pl.ds(pl.multiple_of(j * LANES, LANES), LANES)
            w, b = w_ref[:, :, lanes], b_ref[:, lanes]
            for s in range(tt // sub):
                rows = pl.ds(s * sub, sub + 2 * halo)
                ue = [uext[hf, rows, lanes] for hf in range(2)]
                a, g = [_causal_conv(ue[hf], w[:, hf, :], b[hf:hf + 1, :]) for hf in range(2)]
                sg = _sigmoid(g)
                dzs = dzext[rows, lanes]
                dc = [dzs * g * sg, dzs * a * sg * (1.0 + g * (1.0 - sg))]
                for hf in range(2):
                    d = dc[hf]
                    d1, d2 = _shift_up(d, 1), _shift_up(d, 2)
                    du = w[2:3, hf, :] * d + w[1:2, hf, :] * d1 + w[0:1, hf, :] * d2
                    du_ref[hf, pl.ds(s * sub, sub), lanes] = du[main].astype(du_ref.dtype)
                    um = ue[hf][main]
                    for k, dd in enumerate((d2, d1, d)):
                        dwacc[k, hf, :, lanes] += _fold8(dd[main] * um)
                    dbacc[hf, :, lanes] += _fold8(d[main])
            return carry

        lax.fori_loop(0, f // LANES, chunk, 0)

        @pl.when(i == nt - 1)
        def _():
            dw_ref[...] = jnp.sum(dwacc[...], axis=2)
            db_ref[...] = jnp.sum(dbacc[...], axis=1)

    last = t // halo - 1
    return pl.pallas_call(
        body, name=name, grid=(nt,),
        in_specs=[pl.BlockSpec((2, tt, f), lambda i: (0, i, 0)),
                  pl.BlockSpec((2, halo, f), lambda i: (0, jnp.maximum(i * per - 1, 0), 0)),
                  pl.BlockSpec((2, halo, f), lambda i: (0, jnp.minimum((i + 1) * per, last), 0)),
                  pl.BlockSpec((tt, f), lambda i: (i, 0)),
                  pl.BlockSpec((halo, f), lambda i: (jnp.minimum((i + 1) * per, last), 0)),
                  pl.BlockSpec((3, 2, f), lambda i: (0, 0, 0)), pl.BlockSpec((2, f), lambda i: (0, 0))],
        out_specs=[pl.BlockSpec((2, tt, f), lambda i: (0, i, 0)),
                   pl.BlockSpec((3, 2, f), lambda i: (0, 0, 0)), pl.BlockSpec((2, f), lambda i: (0, 0))],
        out_shape=[jax.ShapeDtypeStruct(u.shape, MXU_DTYPE), jax.ShapeDtypeStruct((3, 2, f), F32),
                   jax.ShapeDtypeStruct((2, f), F32)],
        scratch_shapes=[pltpu.VMEM((2, tt + 2 * halo, f), F32), pltpu.VMEM((tt + 2 * halo, f), F32),
                        pltpu.VMEM((3, 2, 8, f), F32), pltpu.VMEM((2, 8, f), F32)],
        compiler_params=_params("arbitrary"))(u, u, u, dz, dz, conv_w, conv_b)


def _kv_even_odd(band, k):
    pair, parity = divmod(k, 2)
    blk = band[:, 128 * pair:128 * (pair + 1)].astype(F32)
    lane = lax.broadcasted_iota(jnp.int32, blk.shape, 1)
    own = jnp.where((lane >= HEAD_DIM) == (parity == 1), blk, 0.0)
    swapped = pltpu.roll(own, HEAD_DIM, axis=1)
    even, odd = (own, swapped) if parity == 0 else (swapped, own)
    return even.astype(MXU_DTYPE), odd.astype(MXU_DTYPE)


def _stack_pairs(ref, k):
    return jnp.concatenate([ref[:, 256 * k:256 * k + 128], ref[:, 256 * k + 128:256 * k + 256]], axis=0)


_GROUP_ORDER = (0, 2, 1, 3)


def _attn_fill_tables(bias_ref):
    shape = (4 * ATT_BLOCK, 2 * ATT_BLOCK)
    row = lax.broadcasted_iota(jnp.int32, shape, 0)
    col = lax.broadcasted_iota(jnp.int32, shape, 1)
    dist = (row & (ATT_BLOCK - 1)) + ATT_BLOCK - col
    ok = (dist >= 0) & (dist < ATT_BLOCK)
    distf = dist.astype(F32)
    rb = lax.broadcasted_iota(jnp.int32, (shape[0], 1), 0) // ATT_BLOCK
    for k in range(N_KV_HEADS):
        slope = jnp.zeros((shape[0], 1), F32)
        for r, gq in enumerate(_GROUP_ORDER):
            slope = jnp.where(rb == r, ALIBI_SLOPES[4 * k + gq], slope)
        bias = jnp.where(ok, -slope * distf, -jnp.inf)
        bias_ref[0, k] = bias
        bias_ref[1, k] = jnp.where(col >= ATT_BLOCK, bias, -jnp.inf)


def _attn_probs(qs, k_even, k_odd, bias, k, sink_ref):
    nt_dims = (((1,), (1,)), ((), ()))
    s = jnp.concatenate([lax.dot_general(qs, k_even, nt_dims, preferred_element_type=F32),
                         lax.dot_general(qs, k_odd, nt_dims, preferred_element_type=F32)], axis=0) + bias
    rb = lax.broadcasted_iota(jnp.int32, (s.shape[0], 1), 0) // ATT_BLOCK
    sink = jnp.zeros((s.shape[0], 1), F32)
    for r, gq in enumerate(_GROUP_ORDER):
        sink = jnp.where(rb == r, sink_ref[4 * k + gq], sink)
    m = jnp.maximum(jnp.max(s, axis=-1, keepdims=True), sink)
    p = jnp.exp(s - m)
    es = jnp.exp(sink - m)
    return p, 1.0 / (jnp.sum(p, axis=-1, keepdims=True) + es), es


_ATTN_TABLES = [pltpu.VMEM((2, N_KV_HEADS, 4 * ATT_BLOCK, 2 * ATT_BLOCK), F32)]


def _attn_specs(with_do):
    prev = lambda n: jnp.maximum(n - 1, 0)
    specs = [pl.BlockSpec(memory_space=pltpu.SMEM),
             pl.BlockSpec((ATT_BLOCK, D_MODEL), lambda n: (n, 0)),
             pl.BlockSpec((ATT_BLOCK, 256), lambda n: (prev(n), 4)), pl.BlockSpec((ATT_BLOCK, 256), lambda n: (n, 4)),
             pl.BlockSpec((ATT_BLOCK, 256), lambda n: (prev(n), 5)), pl.BlockSpec((ATT_BLOCK, 256), lambda n: (n, 5))]
    if with_do:
        specs.append(pl.BlockSpec((ATT_BLOCK, D_MODEL), lambda n: (n, 0)))
    return specs


def _attn_fwd_call(qkv, sinks):
    t = qkv.shape[0]

    def body(sink_ref, q_ref, kp_ref, kc_ref, vp_ref, vc_ref, o_ref, bias_ref):
        n = pl.program_id(0)

        @pl.when(n == 0)
        def _():
            _attn_fill_tables(bias_ref)

        first = 1 - jnp.minimum(n, 1)
        kband = jnp.concatenate([kp_ref[...], kc_ref[...]], axis=0)
        vband = jnp.concatenate([vp_ref[...], vc_ref[...]], axis=0)
        for k in range(N_KV_HEADS):
            k_even, k_odd = _kv_even_odd(kband, k)
            v_even, v_odd = _kv_even_odd(vband, k)
            qs = _stack_pairs(q_ref, k) * ATT_SCALE
            p, rdenom, _ = _attn_probs(qs, k_even, k_odd, bias_ref[first, k], k, sink_ref)
            probs = (p * rdenom).astype(MXU_DTYPE)
            o = (jnp.dot(probs[:256], v_even, preferred_element_type=F32)
                 + jnp.dot(probs[256:], v_odd, preferred_element_type=F32))
            o_ref[:, 256 * k:256 * k + 128] = o[:128].astype(o_ref.dtype)
            o_ref[:, 256 * k + 128:256 * k + 256] = o[128:].astype(o_ref.dtype)

    return pl.pallas_call(
        body, name="attn_fwd", grid=(t // ATT_BLOCK,), in_specs=_attn_specs(False),
        out_specs=pl.BlockSpec((ATT_BLOCK, D_MODEL), lambda n: (n, 0)),
        out_shape=jax.ShapeDtypeStruct((t, D_MODEL), MXU_DTYPE), scratch_shapes=_ATTN_TABLES,
        compiler_params=_params("arbitrary"))(sinks, qkv, qkv, qkv, qkv, qkv)


def _attn_bwd_call(qkv, sinks, do):
    t = qkv.shape[0]
    nb = t // ATT_BLOCK
    tn_dims = (((0,), (0,)), ((), ()))
    nt_dims = (((1,), (1,)), ((), ()))

    def to_native(even_part, odd_part, parity):
        lane = lax.broadcasted_iota(jnp.int32, even_part.shape, 1)
        lo = lane < HEAD_DIM
        e, o = jnp.where(lo, even_part, 0.0), jnp.where(lo, 0.0, odd_part)
        if parity == 0:
            return e + pltpu.roll(o, HEAD_DIM, axis=1)
        return pltpu.roll(e, HEAD_DIM, axis=1) + o

    def body(sink_ref, q_ref, kp_ref, kc_ref, vp_ref, vc_ref, do_ref, dq_ref, dk_ref, dv_ref, db_ref, dsink_ref,
             bias_ref):
        n = pl.program_id(0)

        @pl.when(n == 0)
        def _():
            _attn_fill_tables(bias_ref)
            dk_ref[...] = jnp.zeros_like(dk_ref)
            dv_ref[...] = jnp.zeros_like(dv_ref)
            db_ref[...] = jnp.zeros_like(db_ref)
            dsink_ref[...] = jnp.zeros_like(dsink_ref)

        kband = jnp.concatenate([kp_ref[...], kc_ref[...]], axis=0)
        vband = jnp.concatenate([vp_ref[...], vc_ref[...]], axis=0)
        dk_pairs = [jnp.zeros((2 * ATT_BLOCK, 128), F32), jnp.zeros((2 * ATT_BLOCK, 128), F32)]
        dv_pairs = [jnp.zeros((2 * ATT_BLOCK, 128), F32), jnp.zeros((2 * ATT_BLOCK, 128), F32)]
        sink_lane = lax.broadcasted_iota(jnp.int32, (8, 128), 1)
        sink_row = lax.broadcasted_iota(jnp.int32, (8, 128), 0)
        dsink = jnp.zeros((8, 128), F32)
        first = 1 - jnp.minimum(n, 1)
        for k in range(N_KV_HEADS):
            k_even, k_odd = _kv_even_odd(kband, k)
            v_even, v_odd = _kv_even_odd(vband, k)
            qs = _stack_pairs(q_ref, k) * ATT_SCALE
            dos = _stack_pairs(do_ref, k)
            p, rdenom, es = _attn_probs(qs, k_even, k_odd, bias_ref[first, k], k, sink_ref)
            probs = p * rdenom
            dp = jnp.concatenate([lax.dot_general(dos, v_even, nt_dims, preferred_element_type=F32),
                                  lax.dot_general(dos, v_odd, nt_dims, preferred_element_type=F32)], axis=0)
            delta = jnp.sum(probs * dp, axis=-1, keepdims=True)
            ds16 = (probs * (dp - delta)).astype(MXU_DTYPE)
            p16 = probs.astype(MXU_DTYPE)
            dsink_rows = -(es * rdenom) * delta
            for r, gq in enumerate(_GROUP_ORDER):
                tot = jnp.sum(dsink_rows[ATT_BLOCK * r:ATT_BLOCK * (r + 1)], axis=0, keepdims=True)
                dsink = dsink + jnp.where((sink_lane == 4 * k + gq) & (sink_row == 0), tot, 0.0)
            dqs = (jnp.dot(ds16[:256], k_even, preferred_element_type=F32)
                   + jnp.dot(ds16[256:], k_odd, preferred_element_type=F32)) * ATT_SCALE
            dq_ref[:, 256 * k:256 * k + 128] = dqs[:128].astype(dq_ref.dtype)
            dq_ref[:, 256 * k + 128:256 * k + 256] = dqs[128:].astype(dq_ref.dtype)
            db_ref[:, 256 * k:256 * k + 128] += jnp.sum(dqs[:128], axis=0, keepdims=True)
            db_ref[:, 256 * k + 128:256 * k + 256] += jnp.sum(dqs[128:], axis=0, keepdims=True)
            pair, parity = divmod(k, 2)
            dk_pairs[pair] = dk_pairs[pair] + to_native(
                lax.dot_general(ds16[:256], qs, tn_dims, preferred_element_type=F32),
                lax.dot_general(ds16[256:], qs, tn_dims, preferred_element_type=F32), parity)
            dv_pairs[pair] = dv_pairs[pair] + to_native(
                lax.dot_general(p16[:256], dos, tn_dims, preferred_element_type=F32),
                lax.dot_general(p16[256:], dos, tn_dims, preferred_element_type=F32), parity)
        dsink_ref[...] += dsink
        dk_band = jnp.concatenate(dk_pairs, axis=1)
        dv_band = jnp.concatenate(dv_pairs, axis=1)
        cur = pl.multiple_of(n * ATT_BLOCK, ATT_BLOCK)
        dk_ref[pl.ds(cur, ATT_BLOCK), :] += dk_band[ATT_BLOCK:]
        dv_ref[pl.ds(cur, ATT_BLOCK), :] += dv_band[ATT_BLOCK:]

        @pl.when(n > 0)
        def _():
            prv = pl.multiple_of((n - 1) * ATT_BLOCK, ATT_BLOCK)
            dk_ref[pl.ds(prv, ATT_BLOCK), :] += dk_band[:ATT_BLOCK]
            dv_ref[pl.ds(prv, ATT_BLOCK), :] += dv_band[:ATT_BLOCK]

        @pl.when(n == nb - 1)
        def _():
            db_ref[:, 1024:1280] = jnp.sum(dk_ref[...], axis=0, keepdims=True)
            db_ref[:, 1280:1536] = jnp.sum(dv_ref[...], axis=0, keepdims=True)

    whole = lambda shape: pl.BlockSpec(shape, lambda n: (0, 0))
    return pl.pallas_call(
        body, name="attn_bwd", grid=(nb,), in_specs=_attn_specs(True),
        out_specs=[pl.BlockSpec((ATT_BLOCK, D_MODEL), lambda n: (n, 0)), whole((t, 256)), whole((t, 256)),
                   whole((1, QKV_DIM)), whole((8, 128))],
        out_shape=[jax.ShapeDtypeStruct((t, D_MODEL), MXU_DTYPE), jax.ShapeDtypeStruct((t, 256), F32),
                   jax.ShapeDtypeStruct((t, 256), F32), jax.ShapeDtypeStruct((1, QKV_DIM), F32),
                   jax.ShapeDtypeStruct((8, 128), F32)],
        scratch_shapes=_ATTN_TABLES, compiler_params=_params("arbitrary"))(sinks, qkv, qkv, qkv, qkv, qkv, do)


FF_TILE = 1408


def _ffn_fwd(hn, h, w, conv_w, conv_b, layer, host=None):
    t = hn.shape[0]
    tm = _tile(t, 512)
    halves = D_FF // FF_TILE

    def carry(stage, call):
        if host is None:
            return call(None), w
        res = call(host(stage, w))
        return res[0], {**w, **dict(zip(_late_names(stage), res[1:]))}

    u, w = carry("in", lambda hosted: _matmul(
        hn, w["w_in"], mode="nn", name=f"ffn{layer}_in", grid=(t // tm, 2 * halves, 1),
        a_spec=pl.BlockSpec((tm, D_MODEL), lambda i, j, k: (i, 0)),
        b_spec=pl.BlockSpec((None, D_MODEL, FF_TILE), lambda i, j, k: (layer, 0, j)),
        o_spec=pl.BlockSpec((None, tm, FF_TILE), lambda i, j, k: (j // halves, i, j % halves)),
        out_shape=jax.ShapeDtypeStruct((2, t, D_FF), F32), hosted=hosted))
    z, w = carry("glu", lambda hosted: _conv_glu_fwd_call(u, conv_w, conv_b, f"ffn{layer}_glu", hosted=hosted))
    h_out, w = carry("out", lambda hosted: _mm_nn(z, w["w_out"], f"ffn{layer}_out", b_lead=layer, residual=h,
                                                  tk=FF_TILE, hosted=hosted))
    return h_out, u, z, w


def _ffn_bwd(dh, hn, u, z, w_in, w_out, conv_w, conv_b, layer, g_in_acc, g_out_acc):
    t = hn.shape[0]
    halves = D_FF // FF_TILE
    tm = _tile(t, 1024)
    dz = _mm_nt(dh, w_out, f"ffn{layer}_dz", b_lead=layer, out_dtype=MXU_DTYPE, tn=FF_TILE)
    g_out = _mm_tn(z, dh, f"ffn{layer}_dwout", lead=layer, n_lead=2, into=g_out_acc, tm=FF_TILE)
    du, dcw, dcb = _conv_glu_bwd_call(u, dz, conv_w, conv_b, f"ffn{layer}_dglu")
    g_in = _matmul(
        hn, du, mode="tn", name=f"ffn{layer}_dwin", grid=(1, 2 * halves, t // tm),
        a_spec=pl.BlockSpec((tm, D_MODEL), lambda i, j, k: (k, 0)),
        b_spec=pl.BlockSpec((None, tm, FF_TILE), lambda i, j, k: (j // halves, k, j % halves)),
        o_spec=pl.BlockSpec((None, D_MODEL, FF_TILE), lambda i, j, k: (layer, 0, j)),
        out_shape=jax.ShapeDtypeStruct((2, D_MODEL, 2 * D_FF), F32), into=g_in_acc)
    dhn = _matmul(
        du, w_in, mode="nt", name=f"ffn{layer}_dhn", grid=(t // tm, 1, 2 * halves),
        a_spec=pl.BlockSpec((None, tm, FF_TILE), lambda i, j, k: (k // halves, i, k % halves)),
        b_spec=pl.BlockSpec((None, D_MODEL, FF_TILE), lambda i, j, k: (layer, 0, k)),
        o_spec=pl.BlockSpec((tm, D_MODEL), lambda i, j, k: (i, 0)),
        out_shape=jax.ShapeDtypeStruct((t, D_MODEL), F32))
    return dhn, g_in, g_out, dcw, dcb


def _local_step(x, target, w, *, norm_mix, norm_ffn, norm_f, pool_b, pool_scale, b_qkv, sinks, conv_w, conv_b,
                late_gather=None):
    cw = [conv_w[l].reshape(3, 2, D_FF) for l in range(2)]
    cb = [conv_b[l].reshape(2, D_FF) for l in range(2)]
    gm = [norm_mix[l:l + 1] for l in range(2)]
    gf = [norm_ffn[l:l + 1] for l in range(2)]

    h1, hn_f0 = _pool_fwd_call(x, gm[0], w["pool_w"], pool_b, pool_scale, gf[0])
    h2, u0, z0, w = _ffn_fwd(hn_f0, h1, w, cw[0], cb[0], 0, host=late_gather)
    hn_a = _rms_fwd_call(h2, gm[1], "rms_attn", MXU_DTYPE)
    qkv = _mm_nn(hn_a, w["w_qkv"], "attn_qkv", bias=b_qkv, out_dtype=MXU_DTYPE, tn=768)
    o = _attn_fwd_call(qkv, sinks)
    h3 = _mm_nn(o, w["w_o"], "attn_out", residual=h2)
    hn_f1 = _rms_fwd_call(h3, gf[1], "rms_ffn1", MXU_DTYPE)
    h4, u1, z1, w = _ffn_fwd(hn_f1, h3, w, cw[1], cb[1], 1)
    loss, dh, d_norm_f = _loss_call(h4, norm_f, target)

    dhn, g_in, g_out, dcw1, dcb1 = _ffn_bwd(dh, hn_f1, u1, z1, w["w_in"], w["w_out"], cw[1], cb[1], 1, None, None)
    dh, d_gf1 = _rms_bwd_call(h3, gf[1], dhn, dh, "rms_ffn1_bwd")
    do = _mm_nt(dh, w["w_o"], "attn_do", out_dtype=MXU_DTYPE)
    g_o = _mm_tn(o, dh, "attn_dwo")
    dq, dk, dv, d_bqkv, d_sinks = _attn_bwd_call(qkv, sinks, do)
    dqkv = jnp.concatenate([dq, dk.astype(MXU_DTYPE), dv.astype(MXU_DTYPE)], axis=1)
    g_qkv = _mm_tn(hn_a, dqkv, "attn_dwqkv", tn=768)
    dhn = _mm_nt(dqkv, w["w_qkv"], "attn_dhn", tk=768)
    dh, d_gm1 = _rms_bwd_call(h2, gm[1], dhn, dh, "rms_attn_bwd")
    dhn, g_in, g_out, dcw0, dcb0 = _ffn_bwd(dh, hn_f0, u0, z0, w["w_in"], w["w_out"], cw[0], cb[0], 0, g_in, g_out)
    dh, d_gf0 = _rms_bwd_call(h1, gf[0], dhn, dh, "rms_ffn0_bwd")
    grad_x, g_pool, dvec = _pool_bwd_call(x, dh, gm[0], w["pool_w"], pool_b, pool_scale)

    grads = {
        "w_in": g_in, "w_out": g_out, "w_qkv": g_qkv, "w_o": g_o, "pool_w": g_pool,
        "pool_b": dvec[0:1], "pool_scale": dvec[1:2], "b_qkv": d_bqkv, "sinks": d_sinks[0:1, :N_HEADS],
        "norm_mix": jnp.concatenate([dvec[2:3], d_gm1], axis=0), "norm_ffn": jnp.concatenate([d_gf0, d_gf1], axis=0),
        "conv_w": jnp.stack([dcw0.reshape(3, 2 * D_FF), dcw1.reshape(3, 2 * D_FF)]),
        "conv_b": jnp.stack([dcb0.reshape(2 * D_FF), dcb1.reshape(2 * D_FF)]), "norm_f": d_norm_f,
    }
    return loss, grad_x, grads


W_IN_COLS = 2 * D_FF // N_CHIPS
W_OUT_ROWS = D_FF // N_CHIPS
QKV_COLS = QKV_DIM // N_CHIPS
W_O_ROWS = D_MODEL // N_CHIPS
POOL_ROWS = POOL_GC // N_CHIPS
BIG = ("w_in", "w_out", "w_qkv", "w_o", "pool_w")


def _mesh_place():
    x, y, c = lax.axis_index("x"), lax.axis_index("y"), lax.axis_index("c")
    chips = [(1 - x, y), (x, 1 - y), (1 - x, 1 - y)]
    return x, y, c, chips


def _col(k, width):
    return pl.ds(pl.multiple_of(k * width, 128), width)


def _row(k, height):
    return pl.ds(pl.multiple_of(k * height, 16), height)


def _full_piece(name, ref, k, h):
    if name == "w_in":
        return ref.at[h, :, _col(k, W_IN_COLS)]
    if name == "w_out":
        return ref.at[h, _row(k, W_OUT_ROWS), :]
    if name == "w_qkv":
        return ref.at[_row(h, D_MODEL // 2), _col(k, QKV_COLS)]
    if name == "w_o":
        return ref.at[_row(2 * k + h, W_O_ROWS // 2), :]
    return ref.at[pl.ds(2 * h, 2), _row(k, POOL_ROWS), :]


def _shard_half(name, ref, h):
    if name in ("w_in", "w_out"):
        return ref.at[h]
    if name == "w_qkv":
        return ref.at[_row(h, D_MODEL // 2), :]
    if name == "w_o":
        return ref.at[_row(h, W_O_ROWS // 2), :]
    return ref.at[pl.ds(2 * h, 2)]


FULL_SHAPES = {"w_in": (2, D_MODEL, 2 * D_FF), "w_out": (2, D_FF, D_MODEL), "w_qkv": (D_MODEL, QKV_DIM),
               "w_o": (D_MODEL, D_MODEL), "pool_w": (4, POOL_GC, POOL_GC)}
SHARD_SHAPES = {"w_in": (2, D_MODEL, W_IN_COLS), "w_out": (2, W_OUT_ROWS, D_MODEL), "w_qkv": (D_MODEL, QKV_COLS),
                "w_o": (W_O_ROWS, D_MODEL), "pool_w": (4, POOL_ROWS, POOL_GC)}
HALF_SHAPES = {"w_in": (D_MODEL, W_IN_COLS), "w_out": (W_OUT_ROWS, D_MODEL), "w_qkv": (D_MODEL // 2, QKV_COLS),
               "w_o": (W_O_ROWS // 2, D_MODEL), "pool_w": (2, POOL_ROWS, POOL_GC)}


def _place_shard(name, shard, me_arr):
    if name == "w_in":
        blk, grid = (1, 256, W_IN_COLS), (2, D_MODEL // 256)
        src, dst = (lambda a, b, me: (a, b, 0)), (lambda a, b, me: (a, b, me[0]))
    elif name == "w_out":
        blk, grid = (1, W_OUT_ROWS, D_MODEL), (2, 1)
        src, dst = (lambda a, b, me: (a, 0, 0)), (lambda a, b, me: (a, me[0], 0))
    elif name == "w_qkv":
        blk, grid = (256, QKV_COLS), (1, D_MODEL // 256)
        src, dst = (lambda a, b, me: (b, 0)), (lambda a, b, me: (b, me[0]))
    elif name == "w_o":
        blk, grid = (W_O_ROWS, D_MODEL), (1, 1)
        src, dst = (lambda a, b, me: (0, 0)), (lambda a, b, me: (me[0], 0))
    else:
        blk, grid = (4, POOL_ROWS, POOL_GC), (1, 1)
        src, dst = (lambda a, b, me: (0, 0, 0)), (lambda a, b, me: (0, me[0], 0))

    def body(me_ref, s_ref, o_ref):
        o_ref[...] = s_ref[...].astype(o_ref.dtype)

    return pl.pallas_call(
        body, name=f"place_{name}",
        grid_spec=pltpu.PrefetchScalarGridSpec(num_scalar_prefetch=1, grid=grid, in_specs=[pl.BlockSpec(blk, src)],
                                               out_specs=pl.BlockSpec(blk, dst)),
        out_shape=jax.ShapeDtypeStruct(FULL_SHAPES[name], WIRE_DTYPE), compiler_params=_params("parallel", "parallel"),
    )(me_arr, shard)


def _when_mover(half, c, fn):
    if half is None:
        fn(c)
    else:
        pl.when(c == half)(lambda: fn(half))


def _when_taker(half, c, fn):
    if half is None:
        fn(1 - c)
    else:
        pl.when(c != half)(lambda: fn(half))


def _remote(ref, sems, t, j, dev):
    send, recv = sems
    return pltpu.make_async_remote_copy(src_ref=ref, dst_ref=ref, send_sem=send.at[t, j], recv_sem=recv.at[t, j],
                                        device_id=dev, device_id_type=MESH)


def _gather_ici(items, refs, sems, finish):
    x, y, c, chips = _mesh_place()
    me = 2 * x + y
    for t, (nm, half) in enumerate(items):
        def go(h, t=t, nm=nm):
            mine = _full_piece(nm, refs[t], me, h)
            for j, (cx, cy) in enumerate(chips):
                if not finish:
                    _remote(mine, sems, t, j, (cx, cy, c)).start()
                else:
                    _remote(_full_piece(nm, refs[t], 2 * cx + cy, h), sems, t, j, (cx, cy, c)).wait_recv()
                    _remote(mine, sems, t, j, (cx, cy, c)).wait_send()

        _when_mover(half, c, go)


def _gather_pass(items, refs, sems, finish):
    x, y, c, chips = _mesh_place()
    sibling = (x, y, 1 - c)
    for t, (nm, half) in enumerate(items):
        def give(h, t=t, nm=nm):
            for j, (cx, cy) in enumerate(chips):
                got = _remote(_full_piece(nm, refs[t], 2 * cx + cy, h), sems, t, j, sibling)
                if not finish:
                    got.start()
                else:
                    got.wait_send()

        def take(h, t=t, nm=nm):
            for j, (cx, cy) in enumerate(chips):
                _remote(_full_piece(nm, refs[t], 2 * cx + cy, h), sems, t, j, sibling).wait_recv()

        _when_mover(half, c, give)
        if finish:
            _when_taker(half, c, take)


def _gather_sems(n_items):
    return [pltpu.SemaphoreType.DMA((n_items, 3)), pltpu.SemaphoreType.DMA((n_items, 3))]


EARLY_ITEMS = (("pool_w", None), ("w_in", 0), ("w_out", 0))
LATE_ITEMS_A = (("w_out", 1), ("w_qkv", None), ("w_o", None))
LATE_ITEMS_B = (("w_in", 1),)


def _names(items):
    return [nm for nm, _ in items]


def _gather_early(full, small):
    items = EARLY_ITEMS
    n = len(items)

    def body(*refs):
        small_in = refs[n]
        outs, small_out = refs[n + 1:2 * n + 1], refs[2 * n + 1]
        scratch = refs[2 * n + 2:]
        ici_sems, pass_sems = scratch[0:2], scratch[2:4]
        local_sem, small_send, small_recv = scratch[4:]
        x, y, c, chips = _mesh_place()
        me = 2 * x + y
        local = pltpu.make_async_copy(small_in, small_out.at[me], local_sem.at[0])
        local.start()
        small_sends = [pltpu.make_async_remote_copy(
            src_ref=small_in, dst_ref=small_out.at[me], send_sem=small_send.at[j], recv_sem=small_recv.at[j],
            device_id=(cx, cy, c), device_id_type=MESH) for j, (cx, cy) in enumerate(chips)]
        for cp in small_sends:
            cp.start()
        _gather_ici(items, outs, ici_sems, finish=False)
        _gather_ici(items, outs, ici_sems, finish=True)
        _gather_pass(items, outs, pass_sems, finish=False)
        _gather_pass(items, outs, pass_sems, finish=True)
        for j, (cx, cy) in enumerate(chips):
            got = small_out.at[2 * cx + cy]
            pltpu.make_async_remote_copy(src_ref=got, dst_ref=got, send_sem=small_send.at[j],
                                         recv_sem=small_recv.at[j], device_id=(cx, cy, c), device_id_type=MESH).wait_recv()
        for cp in small_sends:
            cp.wait_send()
        local.wait()

    arrays = [full[nm] for nm in _names(items)]
    out_shape = [jax.ShapeDtypeStruct(a.shape, a.dtype) for a in arrays]
    out_shape.append(jax.ShapeDtypeStruct((N_CHIPS,) + small.shape, F32))
    res = pl.pallas_call(
        body, name="gather_early", in_specs=[ANY] * (n + 1), out_specs=[ANY] * (n + 1), out_shape=out_shape,
        input_output_aliases={t: t for t in range(n)},
        scratch_shapes=_gather_sems(n) + _gather_sems(n) + [pltpu.SemaphoreType.DMA((1,)), pltpu.SemaphoreType.DMA((3,)),
                                                            pltpu.SemaphoreType.DMA((3,))],
    )(*arrays, small)
    return {**full, **dict(zip(_names(items), res[:n]))}, res[n]


def _late_gather_host(stage, w):
    a, b = LATE_ITEMS_A, LATE_ITEMS_B
    if stage == "in":
        return _Hosted([w[nm] for nm in _names(a)], _gather_sems(len(a)),
                       lambda refs, sems: _gather_ici(a, refs, sems, finish=False),
                       lambda refs, sems: _gather_ici(a, refs, sems, finish=True))
    if stage == "glu":
        na = len(a)

        def run(refs, sems, finish):
            _gather_pass(a, refs[:na], sems[:2], finish)
            _gather_ici(b, refs[na:], sems[2:], finish)

        return _Hosted([w[nm] for nm in _names(a) + _names(b)], _gather_sems(na) + _gather_sems(len(b)),
                       functools.partial(run, finish=False), functools.partial(run, finish=True))
    return _Hosted([w[nm] for nm in _names(b)], _gather_sems(len(b)),
                   lambda refs, sems: _gather_pass(b, refs, sems, finish=False),
                   lambda refs, sems: _gather_pass(b, refs, sems, finish=True))


def _late_names(stage):
    return {"in": _names(LATE_ITEMS_A), "glu": _names(LATE_ITEMS_A) + _names(LATE_ITEMS_B), "out": _names(LATE_ITEMS_B)}[stage]


def _other_half_src(name, ref, h):
    if name in ("w_in", "w_out"):
        return ref.at[h]
    if name == "w_qkv":
        return ref.at[_row(h, D_MODEL // 2), :]
    if name == "w_o":
        return ref.at[:, pl.ds(h, 1)]
    return ref.at[pl.ds(2 * h, 2)]


SLAB_SHAPES = {"w_in": (D_MODEL, 2 * D_FF), "w_out": (D_FF, D_MODEL), "w_qkv": (D_MODEL // 2, QKV_DIM),
               "w_o": (N_CHIPS, 1, W_O_ROWS // 2, D_MODEL), "pool_w": (2, POOL_GC, POOL_GC)}


def _pair_exchange(grads):
    nb = len(BIG)

    def body(*refs):
        ins, outs = refs[:nb], refs[nb:2 * nb]
        send_sems, recv_sems = refs[2 * nb:]
        x, y, c, _ = _mesh_place()
        copies = [pltpu.make_async_remote_copy(
            src_ref=_other_half_src(nm, ins[t], 1 - c), dst_ref=outs[t], send_sem=send_sems.at[t],
            recv_sem=recv_sems.at[t], device_id=(x, y, 1 - c), device_id_type=MESH) for t, nm in enumerate(BIG)]
        for cp in copies:
            cp.start()
        for cp in copies:
            cp.wait()

    res = pl.pallas_call(
        body, name="pair_exchange", in_specs=[ANY] * nb, out_specs=[ANY] * nb,
        out_shape=[jax.ShapeDtypeStruct(SLAB_SHAPES[nm], F32) for nm in BIG],
        scratch_shapes=[pltpu.SemaphoreType.DMA((nb,)), pltpu.SemaphoreType.DMA((nb,))],
    )(*[grads[nm] for nm in BIG])
    return dict(zip(BIG, res))


def _pair_sum(name, grad, slab, c_arr):
    shape = SLAB_SHAPES[name]
    if name == "w_in":
        blk = (256, 2 * D_FF)
        grid = (D_MODEL // 256,)
        g_spec = pl.BlockSpec((None,) + blk, lambda i, c: (c[0], i, 0))
        s_spec = pl.BlockSpec(blk, lambda i, c: (i, 0))
    elif name == "w_out":
        blk = (W_OUT_ROWS, D_MODEL)
        grid = (N_CHIPS,)
        g_spec = pl.BlockSpec((None,) + blk, lambda i, c: (c[0], i, 0))
        s_spec = pl.BlockSpec(blk, lambda i, c: (i, 0))
    elif name == "w_qkv":
        grid = (1,)
        g_spec = pl.BlockSpec(shape, lambda i, c: (c[0], 0))
        s_spec = pl.BlockSpec(shape, lambda i, c: (0, 0))
    elif name == "w_o":
        grid = (1,)
        g_spec = pl.BlockSpec(shape, lambda i, c: (0, c[0], 0, 0))
        s_spec = pl.BlockSpec(shape, lambda i, c: (0, 0, 0, 0))
    else:
        grid = (1,)
        g_spec = pl.BlockSpec(shape, lambda i, c: (c[0], 0, 0))
        s_spec = pl.BlockSpec(shape, lambda i, c: (0, 0, 0))

    def body(c_ref, g_ref, s_ref, o_ref):
        o_ref[...] = (g_ref[...] + s_ref[...]).astype(o_ref.dtype)

    return pl.pallas_call(
        body, name=f"pair_sum_{name}",
        grid_spec=pltpu.PrefetchScalarGridSpec(num_scalar_prefetch=1, grid=grid, in_specs=[g_spec, s_spec],
                                               out_specs=s_spec),
        out_shape=jax.ShapeDtypeStruct(shape, WIRE_DTYPE), compiler_params=_params("parallel"),
    )(c_arr, grad, slab)


def _slab_piece(name, ref, k):
    if name == "w_in":
        return ref.at[:, _col(k, W_IN_COLS)]
    if name == "w_out":
        return ref.at[_row(k, W_OUT_ROWS), :]
    if name == "w_qkv":
        return ref.at[:, _col(k, QKV_COLS)]
    if name == "w_o":
        return ref.at[k, 0]
    return ref.at[:, _row(k, POOL_ROWS), :]


def _chip_scatter(slabs):
    nb = len(BIG)

    def body(*refs):
        ins, outs = refs[:nb], refs[nb:2 * nb]
        send_sems, recv_sems = refs[2 * nb:]
        _, _, c, chips = _mesh_place()
        copies = []
        for j, (cx, cy) in enumerate(chips):
            for t, nm in enumerate(BIG):
                copies.append(pltpu.make_async_remote_copy(
                    src_ref=_slab_piece(nm, ins[t], 2 * cx + cy), dst_ref=outs[t].at[j], send_sem=send_sems.at[t, j],
                    recv_sem=recv_sems.at[t, j], device_id=(cx, cy, c), device_id_type=MESH))
        for cp in copies:
            cp.start()
        for j, (cx, cy) in enumerate(chips):
            for t, nm in enumerate(BIG):
                got = outs[t].at[j]
                pltpu.make_async_remote_copy(src_ref=got, dst_ref=got, send_sem=send_sems.at[t, j],
                                             recv_sem=recv_sems.at[t, j], device_id=(cx, cy, c),
                                             device_id_type=MESH).wait_recv()
        for cp in copies:
            cp.wait_send()

    res = pl.pallas_call(
        body, name="chip_scatter", in_specs=[ANY] * nb, out_specs=[ANY] * nb,
        out_shape=[jax.ShapeDtypeStruct((N_CHIPS - 1,) + HALF_SHAPES[nm], WIRE_DTYPE) for nm in BIG],
        scratch_shapes=[pltpu.SemaphoreType.DMA((nb, 3)), pltpu.SemaphoreType.DMA((nb, 3))],
    )(*[slabs[nm] for nm in BIG])
    return dict(zip(BIG, res))


def _sum_chips(name, slab, parts, place):
    half = HALF_SHAPES[name]
    if name == "w_in":
        blk, grid = (256, W_IN_COLS), (D_MODEL // 256,)
        own = pl.BlockSpec(blk, lambda i, p: (i, p[0]))
        got = pl.BlockSpec((N_CHIPS - 1,) + blk, lambda i, p: (0, i, 0))
        out = pl.BlockSpec((None,) + blk, lambda i, p: (p[1], i, 0))
    elif name == "w_out":
        grid = (1,)
        own = pl.BlockSpec(half, lambda i, p: (p[0], 0))
        got = pl.BlockSpec((N_CHIPS - 1,) + half, lambda i, p: (0, 0, 0))
        out = pl.BlockSpec((None,) + half, lambda i, p: (p[1], 0, 0))
    elif name == "w_qkv":
        grid = (1,)
        own = pl.BlockSpec(half, lambda i, p: (0, p[0]))
        got = pl.BlockSpec((N_CHIPS - 1,) + half, lambda i, p: (0, 0, 0))
        out = pl.BlockSpec(half, lambda i, p: (p[1], 0))
    elif name == "w_o":
        grid = (1,)
        own = pl.BlockSpec((None, None) + half, lambda i, p: (p[0], 0, 0, 0))
        got = pl.BlockSpec((N_CHIPS - 1,) + half, lambda i, p: (0, 0, 0))
        out = pl.BlockSpec(half, lambda i, p: (p[1], 0))
    else:
        grid = (1,)
        own = pl.BlockSpec(half, lambda i, p: (0, p[0], 0))
        got = pl.BlockSpec((N_CHIPS - 1,) + half, lambda i, p: (0, 0, 0, 0))
        out = pl.BlockSpec(half, lambda i, p: (p[1], 0, 0))

    def body(p_ref, own_ref, got_ref, o_ref):
        acc = own_ref[...].astype(F32)
        for j in range(N_CHIPS - 1):
            acc = acc + got_ref[j].astype(F32)
        o_ref[...] = acc

    return pl.pallas_call(
        body, name=f"sum_chips_{name}",
        grid_spec=pltpu.PrefetchScalarGridSpec(num_scalar_prefetch=1, grid=grid, in_specs=[own, got], out_specs=out),
        out_shape=jax.ShapeDtypeStruct(SHARD_SHAPES[name], F32), compiler_params=_params("parallel"),
    )(place, slab, parts)


def _pair_allgather(shards):
    nb = len(BIG)

    def body(*refs):
        outs = refs[nb:2 * nb]
        send_sems, recv_sems = refs[2 * nb:]
        x, y, c, _ = _mesh_place()
        copies = []
        for t, nm in enumerate(BIG):
            mine = _shard_half(nm, outs[t], c)
            copies.append(pltpu.make_async_remote_copy(
                src_ref=mine, dst_ref=mine, send_sem=send_sems.at[t], recv_sem=recv_sems.at[t],
                device_id=(x, y, 1 - c), device_id_type=MESH))
        for cp in copies:
            cp.start()
        for t, nm in enumerate(BIG):
            theirs = _shard_half(nm, outs[t], 1 - c)
            pltpu.make_async_remote_copy(src_ref=theirs, dst_ref=theirs, send_sem=send_sems.at[t],
                                         recv_sem=recv_sems.at[t], device_id=(x, y, 1 - c),
                                         device_id_type=MESH).wait_recv()
        for cp in copies:
            cp.wait_send()

    res = pl.pallas_call(
        body, name="pair_allgather", in_specs=[ANY] * nb, out_specs=[ANY] * nb,
        out_shape=[jax.ShapeDtypeStruct(SHARD_SHAPES[nm], F32) for nm in BIG],
        input_output_aliases={t: t for t in range(nb)},
        scratch_shapes=[pltpu.SemaphoreType.DMA((nb,)), pltpu.SemaphoreType.DMA((nb,))],
    )(*[shards[nm] for nm in BIG])
    return dict(zip(BIG, res))


N_DEV = 8


def _all_reduce_small(vec):
    def body(v_ref, o_ref, slots, send_sems, recv_sems):
        x, y, c, _ = _mesh_place()
        me = 4 * x + 2 * y + c
        slots[me] = v_ref[...]
        copies = []
        for d in range(1, N_DEV):
            px, py, pc = x ^ (d >> 2), y ^ ((d >> 1) & 1), c ^ (d & 1)
            copies.append(pltpu.make_async_remote_copy(
                src_ref=v_ref, dst_ref=slots.at[me], send_sem=send_sems.at[d - 1], recv_sem=recv_sems.at[d - 1],
                device_id=(px, py, pc), device_id_type=MESH))
        for cp in copies:
            cp.start()
        for d in range(1, N_DEV):
            peer = 4 * (x ^ (d >> 2)) + 2 * (y ^ ((d >> 1) & 1)) + (c ^ (d & 1))
            got = slots.at[peer]
            pltpu.make_async_remote_copy(src_ref=got, dst_ref=got, send_sem=send_sems.at[d - 1],
                                         recv_sem=recv_sems.at[d - 1], device_id=(x, y, c),
                                         device_id_type=MESH).wait_recv()
        for cp in copies:
            cp.wait_send()
        acc = slots[0]
        for d in range(1, N_DEV):
            acc = acc + slots[d]
        o_ref[...] = acc

    vm = pl.BlockSpec(memory_space=pltpu.VMEM)
    return pl.pallas_call(
        body, name="all_reduce_small", in_specs=[vm], out_specs=vm, out_shape=jax.ShapeDtypeStruct(vec.shape, F32),
        scratch_shapes=[pltpu.VMEM((N_DEV,) + vec.shape, F32), pltpu.SemaphoreType.DMA((N_DEV - 1,)),
                        pltpu.SemaphoreType.DMA((N_DEV - 1,))],
    )(vec)


def _adamw_call(w, g, m, v, name):
    shape = w.shape
    cols = shape[-1]
    rows = w.size // cols
    tr = rows
    for cand in (256, 128, 64, 32, 16, 8):
        if rows > cand and rows % cand == 0 and cand * cols * 4 <= 2 * 1024 * 1024:
            tr = cand
            break

    def body(w_ref, g_ref, m_ref, v_ref, d_ref, nm_ref, nv_ref):
        gv = g_ref[...]
        mn = ADAM_B1 * m_ref[...] + (1.0 - ADAM_B1) * gv
        vn = ADAM_B2 * v_ref[...] + (1.0 - ADAM_B2) * jnp.square(gv)
        m_hat = mn / (1.0 - ADAM_B1 ** ADAM_STEP)
        v_hat = vn / (1.0 - ADAM_B2 ** ADAM_STEP)
        d_ref[...] = -ADAM_LR * (m_hat / (jnp.sqrt(v_hat) + ADAM_EPS) + ADAM_WD * w_ref[...])
        nm_ref[...] = mn
        nv_ref[...] = vn

    spec = pl.BlockSpec((tr, cols), lambda i: (i, 0))
    flat = lambda a: a.reshape(rows, cols)
    outs = pl.pallas_call(
        body, name=name, grid=(rows // tr,), in_specs=[spec] * 4, out_specs=[spec] * 3,
        out_shape=[jax.ShapeDtypeStruct((rows, cols), F32)] * 3, compiler_params=_params("parallel"),
    )(flat(w), flat(g), flat(m), flat(v))
    return tuple(o.reshape(shape) for o in outs)


WEIGHTS = ("pool_w", "pool_b", "pool_scale", "attn_w_qkv", "attn_b_qkv", "attn_sinks", "attn_w_o", "norm_mix",
           "norm_ffn", "ffn_w_in", "ffn_conv_w", "ffn_conv_b", "ffn_w_out", "norm_f")
SMALL_GATHER = 128 * 71
_SMALL_SIZES = (("pool_b", 1024), ("pool_scale", 1024), ("b_qkv", 1536), ("sinks", 128), ("norm_mix", 2048),
                ("norm_ffn", 2048), ("conv_w", 6 * 2 * D_FF), ("conv_b", 4 * D_FF), ("norm_f", 1024))
SMALL_COLS = 6784


def _pack_small(grads):
    flat = jnp.concatenate([jnp.pad(grads[nm].reshape(-1), (0, size - grads[nm].size)) for nm, size in _SMALL_SIZES])
    return jnp.pad(flat, (0, 8 * SMALL_COLS - flat.size)).reshape(8, SMALL_COLS)


def _unpack_small(vec):
    flat = vec.reshape(-1)
    out, off = {}, 0
    for nm, size in _SMALL_SIZES:
        out[nm] = flat[off:off + size]
        off += size
    return out


def kernel(x, pool_w, pool_b, pool_scale, attn_w_qkv, attn_b_qkv, attn_sinks, attn_w_o, norm_mix, norm_ffn, ffn_w_in, ffn_conv_w, ffn_conv_b, ffn_w_out, norm_f, loss_target, m_pool_w, m_pool_b, m_pool_scale, m_attn_w_qkv, m_attn_b_qkv, m_attn_sinks, m_attn_w_o, m_norm_mix, m_norm_ffn, m_ffn_w_in, m_ffn_conv_w, m_ffn_conv_b, m_ffn_w_out, m_norm_f, v_pool_w, v_pool_b, v_pool_scale, v_attn_w_qkv, v_attn_b_qkv, v_attn_sinks, v_attn_w_o, v_norm_mix, v_norm_ffn, v_ffn_w_in, v_ffn_conv_w, v_ffn_conv_b, v_ffn_w_out, v_norm_f):
    weights = dict(pool_w=pool_w, pool_b=pool_b, pool_scale=pool_scale, attn_w_qkv=attn_w_qkv, attn_b_qkv=attn_b_qkv,
                   attn_sinks=attn_sinks, attn_w_o=attn_w_o, norm_mix=norm_mix, norm_ffn=norm_ffn, ffn_w_in=ffn_w_in,
                   ffn_conv_w=ffn_conv_w, ffn_conv_b=ffn_conv_b, ffn_w_out=ffn_w_out, norm_f=norm_f)
    m_in = dict(pool_w=m_pool_w, pool_b=m_pool_b, pool_scale=m_pool_scale, attn_w_qkv=m_attn_w_qkv,
                attn_b_qkv=m_attn_b_qkv, attn_sinks=m_attn_sinks, attn_w_o=m_attn_w_o, norm_mix=m_norm_mix,
                norm_ffn=m_norm_ffn, ffn_w_in=m_ffn_w_in, ffn_conv_w=m_ffn_conv_w, ffn_conv_b=m_ffn_conv_b,
                ffn_w_out=m_ffn_w_out, norm_f=m_norm_f)
    v_in = dict(pool_w=v_pool_w, pool_b=v_pool_b, pool_scale=v_pool_scale, attn_w_qkv=v_attn_w_qkv,
                attn_b_qkv=v_attn_b_qkv, attn_sinks=v_attn_sinks, attn_w_o=v_attn_w_o, norm_mix=v_norm_mix,
                norm_ffn=v_norm_ffn, ffn_w_in=v_ffn_w_in, ffn_conv_w=v_ffn_conv_w, ffn_conv_b=v_ffn_conv_b,
                ffn_w_out=v_ffn_w_out, norm_f=v_norm_f)
    chip = 2 * lax.axis_index("x") + lax.axis_index("y")
    core = lax.axis_index("c")

    place = jnp.stack([chip, core]).astype(jnp.int32)
    shards = {"w_in": ffn_w_in, "w_out": ffn_w_out, "w_qkv": attn_w_qkv[0], "w_o": attn_w_o[0], "pool_w": pool_w[0]}
    placed = {nm: _place_shard(nm, shards[nm], place) for nm in BIG}
    small = jnp.concatenate([pool_b.reshape(-1), attn_b_qkv.reshape(-1), ffn_conv_w.reshape(-1)]).reshape(1, SMALL_GATHER)
    full, small_all = _gather_early(placed, small)
    small_all = small_all.reshape(N_CHIPS, SMALL_GATHER)
    pool_b_full = small_all[:, :256].reshape(N_CHIPS, 4, POOL_ROWS).transpose(1, 0, 2).reshape(1, D_MODEL)
    b_qkv_full = small_all[:, 256:640].reshape(1, QKV_DIM)
    conv_w_full = small_all[:, 640:].reshape(N_CHIPS, 2, 3, W_IN_COLS).transpose(1, 2, 0, 3).reshape(2, 3, 2 * D_FF)

    loss, grad_x, grads = _local_step(
        x[0], loss_target[0], full, norm_mix=norm_mix, norm_ffn=norm_ffn, norm_f=norm_f.reshape(1, D_MODEL),
        pool_b=pool_b_full, pool_scale=pool_scale, b_qkv=b_qkv_full, sinks=attn_sinks[0], conv_w=conv_w_full,
        conv_b=ffn_conv_b, late_gather=_late_gather_host)
    loss = lax.psum(loss[0, 0], ("x", "y", "c"))

    big = {nm: grads[nm] for nm in BIG}
    big["w_o"] = big["w_o"].reshape(N_CHIPS, 2, W_O_ROWS // 2, D_MODEL)
    slabs = _pair_exchange(big)
    sums = {nm: _pair_sum(nm, big[nm], slabs[nm], place[1:]) for nm in BIG}
    parts = _chip_scatter(sums)
    halves = {nm: _sum_chips(nm, sums[nm], parts[nm], place) for nm in BIG}
    reduced = _pair_allgather(halves)

    small_g = _unpack_small(_all_reduce_small(_pack_small(grads)))
    pool_b_g = lax.dynamic_slice_in_dim(small_g["pool_b"].reshape(4, N_CHIPS, POOL_ROWS), chip, 1, axis=1)
    b_qkv_g = lax.dynamic_slice_in_dim(small_g["b_qkv"].reshape(N_CHIPS, QKV_COLS), chip, 1, axis=0)
    conv_w_g = lax.dynamic_slice_in_dim(small_g["conv_w"].reshape(2, 3, N_CHIPS, W_IN_COLS), chip, 1, axis=2)
    grad_w = {
        "pool_w": reduced["pool_w"].reshape(pool_w.shape), "pool_b": pool_b_g.reshape(pool_b.shape),
        "pool_scale": small_g["pool_scale"].reshape(pool_scale.shape),
        "attn_w_qkv": reduced["w_qkv"].reshape(attn_w_qkv.shape), "attn_b_qkv": b_qkv_g.reshape(attn_b_qkv.shape),
        "attn_sinks": small_g["sinks"][:N_HEADS].reshape(attn_sinks.shape),
        "attn_w_o": reduced["w_o"].reshape(attn_w_o.shape), "norm_mix": small_g["norm_mix"].reshape(norm_mix.shape),
        "norm_ffn": small_g["norm_ffn"].reshape(norm_ffn.shape), "ffn_w_in": reduced["w_in"],
        "ffn_conv_w": conv_w_g.reshape(ffn_conv_w.shape), "ffn_conv_b": small_g["conv_b"].reshape(ffn_conv_b.shape),
        "ffn_w_out": reduced["w_out"], "norm_f": small_g["norm_f"].reshape(norm_f.shape),
    }

    delta, new_m, new_v = {}, {}, {}
    for nm in WEIGHTS:
        delta[nm], new_m[nm], new_v[nm] = _adamw_call(weights[nm], grad_w[nm], m_in[nm], v_in[nm], f"adamw_{nm}")
    return (loss, grad_x.reshape(x.shape), *[grad_w[nm] for nm in WEIGHTS], *[delta[nm] for nm in WEIGHTS],
            *[new_m[nm] for nm in WEIGHTS], *[new_v[nm] for nm in WEIGHTS])
```

```python
import functools
from typing import Callable, NamedTuple

import jax
import jax.numpy as jnp
from jax import lax
from jax.experimental import pallas as pl
from jax.experimental.pallas import tpu as pltpu

F32 = jnp.float32
MXU_DTYPE = jnp.bfloat16
WIRE_DTYPE = jnp.bfloat16

D_MODEL = 1024
D_FF = 2816
QKV_DIM = 1536
HEAD_DIM = 64
N_HEADS = 16
N_KV_HEADS = 4
ATT_BLOCK = 128
POOL_WINDOWS = (2, 4, 8, 16)
POOL_GC = 256
POOL_HALO = 16
CONV_HALO = 8
RMS_EPS = 1e-6
ATT_SCALE = HEAD_DIM ** -0.5
ALIBI_SLOPES = tuple(2.0 ** (-8.0 / N_HEADS * (h + 1)) for h in range(N_HEADS))

ADAM_LR = 0.001
ADAM_B1 = 0.9
ADAM_B2 = 0.999
ADAM_EPS = 1e-08
ADAM_WD = 0.01
ADAM_STEP = 10

N_CHIPS = 4
MESH = pl.DeviceIdType.MESH
VMEM_LIMIT_BYTES = 56 * 1024 * 1024
ANY = pl.BlockSpec(memory_space=pl.ANY)


def _params(*semantics):
    return pltpu.CompilerParams(dimension_semantics=semantics, vmem_limit_bytes=VMEM_LIMIT_BYTES)


def _rms(x, g):
    return x * lax.rsqrt(jnp.mean(x * x, axis=-1, keepdims=True) + RMS_EPS) * g


def _rms_bwd(x, g, dy):
    rstd = lax.rsqrt(jnp.mean(x * x, axis=-1, keepdims=True) + RMS_EPS)
    xhat = x * rstd
    dxhat = dy * g
    dx = rstd * (dxhat - xhat * jnp.mean(dxhat * xhat, axis=-1, keepdims=True))
    return dx, dy * xhat


def _shift_down(x, s):
    return pltpu.roll(x, s, axis=0)


def _shift_up(x, s):
    return pltpu.roll(x, x.shape[0] - s, axis=0)


def _sigmoid(x):
    return 1.0 / (1.0 + jnp.exp(-x))


_DOT_DIMS = {"nn": ((1,), (0,)), "nt": ((1,), (1,)), "tn": ((0,), (0,))}


class _Hosted(NamedTuple):
    sources: list
    arrays: list
    new: list
    sems: list
    first: Callable
    last: Callable


def _call(body, hosted, *, name, grid, in_specs, out_specs, out_shape, operands, semantics, scratch_shapes=(),
          aliases=None):
    n_in, n_out, n_scr = len(in_specs), len(out_specs), len(scratch_shapes)
    aliases = dict(aliases or {})
    if hosted is None:
        def plain(*refs):
            body(refs[:n_in], refs[n_in:n_in + n_out], refs[n_in + n_out:], lambda: None, lambda: None)

        return pl.pallas_call(plain, name=name, grid=grid, in_specs=list(in_specs), out_specs=list(out_specs),
                              out_shape=list(out_shape), scratch_shapes=list(scratch_shapes),
                              input_output_aliases=aliases, compiler_params=_params(*semantics))(*operands)
    ns, na, nn = len(hosted.sources), len(hosted.arrays), len(hosted.new)

    def carrying(*refs):
        ins, src = refs[:n_in], refs[n_in:n_in + ns]
        o0 = n_in + ns + na
        outs, arr = refs[o0:o0 + n_out], refs[o0 + n_out:o0 + n_out + na]
        new = refs[o0 + n_out + na:o0 + n_out + na + nn]
        s0 = o0 + n_out + na + nn
        scratch, sems = refs[s0:s0 + n_scr], refs[s0 + n_scr:]
        ids = [pl.program_id(d) for d in range(len(grid))]
        is_first = functools.reduce(lambda p, q: p & q, [i == 0 for i in ids])
        is_last = functools.reduce(lambda p, q: p & q, [i == n - 1 for i, n in zip(ids, grid)])
        body(ins, outs, scratch, lambda: pl.when(is_first)(lambda: hosted.first(src, arr, new, sems)),
             lambda: pl.when(is_last)(lambda: hosted.last(src, arr, new, sems)))

    for t in range(na):
        aliases[n_in + ns + t] = n_out + t
    return pl.pallas_call(
        carrying, name=name, grid=grid, in_specs=list(in_specs) + [ANY] * (ns + na),
        out_specs=list(out_specs) + [ANY] * (na + nn),
        out_shape=list(out_shape) + [jax.ShapeDtypeStruct(a.shape, a.dtype) for a in hosted.arrays] + list(hosted.new),
        scratch_shapes=list(scratch_shapes) + list(hosted.sems), input_output_aliases=aliases,
        compiler_params=_params(*["arbitrary"] * len(grid)))(*operands, *hosted.sources, *hosted.arrays)


def _matmul(a, b, *, mode, name, grid, a_spec, b_spec, o_spec, out_shape, bias=None, bias_spec=None, residual=None,
            residual_spec=None, into=None, hosted=None):
    nk = grid[2]
    dims = (_DOT_DIMS[mode], ((), ()))
    acc_shape = tuple(d for d in o_spec.block_shape if d is not None)

    def body(ins, outs, scratch, first, last):
        a_ref, b_ref = ins[0], ins[1]
        pos = 2
        bias_ref = res_ref = None
        if bias is not None:
            bias_ref = ins[pos]
            pos += 1
        if residual is not None:
            res_ref = ins[pos]
        o_ref = outs[0]
        acc_ref = scratch[0] if nk > 1 else None
        first()
        prod = lax.dot_general(a_ref[...].astype(MXU_DTYPE), b_ref[...].astype(MXU_DTYPE), dims,
                               preferred_element_type=F32)

        def finish(acc):
            if bias_ref is not None:
                acc = acc + bias_ref[...]
            if res_ref is not None:
                acc = acc + res_ref[...]
            o_ref[...] = acc.astype(o_ref.dtype)

        if nk == 1:
            finish(prod)
        else:
            k = pl.program_id(2)

            @pl.when(k == 0)
            def _():
                acc_ref[...] = prod

            @pl.when(k > 0)
            def _():
                acc_ref[...] += prod

            @pl.when(k == nk - 1)
            def _():
                finish(acc_ref[...])

        last()

    operands, in_specs = [a, b], [a_spec, b_spec]
    if bias is not None:
        operands.append(bias)
        in_specs.append(bias_spec)
    if residual is not None:
        operands.append(residual)
        in_specs.append(residual_spec)
    aliases = {}
    if into is not None:
        aliases = {len(operands): 0}
        operands.append(into)
        in_specs.append(ANY)
    res = _call(body, hosted, name=name, grid=grid, in_specs=in_specs, out_specs=[o_spec], out_shape=[out_shape],
                operands=operands, semantics=("parallel", "parallel", "arbitrary"),
                scratch_shapes=[pltpu.VMEM(acc_shape, F32)] if nk > 1 else [], aliases=aliases)
    return res if hosted else res[0]


def _tile(n, want):
    t = min(n, want)
    assert n % t == 0, (n, want)
    return t


def _mm_nn(a, b, name, *, b_lead=None, bias=None, residual=None, out_dtype=F32, tm=1024, tn=1024, tk=1024,
           hosted=None):
    (m, k), n = a.shape, b.shape[-1]
    tm, tn, tk = _tile(m, tm), _tile(n, tn), _tile(k, tk)
    if b_lead is None:
        b_spec = pl.BlockSpec((tk, tn), lambda i, j, kk: (kk, j))
    else:
        b_spec = pl.BlockSpec((None, tk, tn), lambda i, j, kk: (b_lead, kk, j))
    return _matmul(
        a, b, mode="nn", name=name, grid=(m // tm, n // tn, k // tk),
        a_spec=pl.BlockSpec((tm, tk), lambda i, j, kk: (i, kk)), b_spec=b_spec,
        o_spec=pl.BlockSpec((tm, tn), lambda i, j, kk: (i, j)), out_shape=jax.ShapeDtypeStruct((m, n), out_dtype),
        bias=bias, bias_spec=pl.BlockSpec((1, tn), lambda i, j, kk: (0, j)),
        residual=residual, residual_spec=pl.BlockSpec((tm, tn), lambda i, j, kk: (i, j)), hosted=hosted)


def _mm_nt(a, b, name, *, b_lead=None, out_dtype=F32, tm=1024, tn=1024, tk=1024, hosted=None):
    (m, k), n = a.shape, b.shape[-2]
    tm, tn, tk = _tile(m, tm), _tile(n, tn), _tile(k, tk)
    if b_lead is None:
        b_spec = pl.BlockSpec((tn, tk), lambda i, j, kk: (j, kk))
    else:
        b_spec = pl.BlockSpec((None, tn, tk), lambda i, j, kk: (b_lead, j, kk))
    return _matmul(
        a, b, mode="nt", name=name, grid=(m // tm, n // tn, k // tk),
        a_spec=pl.BlockSpec((tm, tk), lambda i, j, kk: (i, kk)), b_spec=b_spec,
        o_spec=pl.BlockSpec((tm, tn), lambda i, j, kk: (i, j)), out_shape=jax.ShapeDtypeStruct((m, n), out_dtype),
        hosted=hosted)


def _mm_tn(a, b, name, *, lead=None, n_lead=None, into=None, tm=1024, tn=1024, tk=1024):
    (k, m), n = a.shape, b.shape[-1]
    tm, tn, tk = _tile(m, tm), _tile(n, tn), _tile(k, tk)
    if lead is None:
        o_spec = pl.BlockSpec((tm, tn), lambda i, j, kk: (i, j))
        out_shape = jax.ShapeDtypeStruct((m, n), F32)
    else:
        o_spec = pl.BlockSpec((None, tm, tn), lambda i, j, kk: (lead, i, j))
        out_shape = jax.ShapeDtypeStruct((n_lead, m, n), F32)
    return _matmul(
        a, b, mode="tn", name=name, grid=(m // tm, n // tn, k // tk),
        a_spec=pl.BlockSpec((tk, tm), lambda i, j, kk: (kk, i)),
        b_spec=pl.BlockSpec((tk, tn), lambda i, j, kk: (kk, j)), o_spec=o_spec, out_shape=out_shape, into=into)


ROW_TILE = 512


def _rms_fwd_call(h, g, name, out_dtype):
    t = h.shape[0]
    tt = _tile(t, ROW_TILE)

    def body(h_ref, g_ref, o_ref):
        o_ref[...] = _rms(h_ref[...], g_ref[...]).astype(o_ref.dtype)

    row = pl.BlockSpec((tt, D_MODEL), lambda i: (i, 0))
    vec = pl.BlockSpec((1, D_MODEL), lambda i: (0, 0))
    return pl.pallas_call(body, name=name, grid=(t // tt,), in_specs=[row, vec], out_specs=row,
                          out_shape=jax.ShapeDtypeStruct(h.shape, out_dtype), compiler_params=_params("parallel"))(h, g)


def _rms_bwd_call(h, g, dy, dres, name):
    t = h.shape[0]
    tt = _tile(t, ROW_TILE)

    def body(h_ref, g_ref, dy_ref, dres_ref, dx_ref, dg_ref):
        dx, dgt = _rms_bwd(h_ref[...], g_ref[...], dy_ref[...].astype(F32))
        dx_ref[...] = dres_ref[...] + dx

        @pl.when(pl.program_id(0) == 0)
        def _():
            dg_ref[...] = jnp.zeros_like(dg_ref)

        dg_ref[...] += jnp.sum(dgt, axis=0, keepdims=True)

    row = pl.BlockSpec((tt, D_MODEL), lambda i: (i, 0))
    vec = pl.BlockSpec((1, D_MODEL), lambda i: (0, 0))
    return pl.pallas_call(
        body, name=name, grid=(t // tt,), in_specs=[row, vec, row, row], out_specs=[row, vec],
        out_shape=[jax.ShapeDtypeStruct(h.shape, F32), jax.ShapeDtypeStruct((1, D_MODEL), F32)],
        compiler_params=_params("arbitrary"))(h, g, dy, dres)


def _loss_call(h, g, target):
    t = h.shape[0]
    tt = _tile(t, ROW_TILE)

    def body(h_ref, g_ref, t_ref, loss_ref, dh_ref, dg_ref):
        x, gv = h_ref[...], g_ref[...]
        err = _rms(x, gv) - t_ref[...]
        dx, dgt = _rms_bwd(x, gv, err * (1.0 / D_MODEL))
        dh_ref[...] = dx

        @pl.when(pl.program_id(0) == 0)
        def _():
            dg_ref[...] = jnp.zeros_like(dg_ref)
            loss_ref[...] = jnp.zeros_like(loss_ref)

        dg_ref[...] += jnp.sum(dgt, axis=0, keepdims=True)
        per_token = jnp.mean(err * err, axis=-1, keepdims=True)
        loss_ref[...] += 0.5 * jnp.sum(per_token, axis=0, keepdims=True)

    row = pl.BlockSpec((tt, D_MODEL), lambda i: (i, 0))
    vec = pl.BlockSpec((1, D_MODEL), lambda i: (0, 0))
    one = pl.BlockSpec((1, 1), lambda i: (0, 0))
    return pl.pallas_call(
        body, name="loss_head", grid=(t // tt,), in_specs=[row, vec, row], out_specs=[one, row, vec],
        out_shape=[jax.ShapeDtypeStruct((1, 1), F32), jax.ShapeDtypeStruct(h.shape, F32),
                   jax.ShapeDtypeStruct((1, D_MODEL), F32)],
        compiler_params=_params("arbitrary"))(h, g, target)


def _pool_windows_causal(ext, first_row):
    tt = ext.shape[0] - POOL_HALO
    t = first_row + lax.broadcasted_iota(jnp.int32, (tt, 1), 0)
    outs = []
    for gi, win in enumerate(POOL_WINDOWS):
        cols = slice(gi * POOL_GC, (gi + 1) * POOL_GC)
        s = ext[:, cols]
        sh = 1
        while sh < win:
            s = s + _shift_down(s, sh)
            sh *= 2
        count = jnp.minimum(t + 1, win).astype(F32)
        outs.append(s[POOL_HALO:] / count - ext[POOL_HALO:, cols])
    return outs


def _prev_halo_spec(tt, halo, width):
    per = tt // halo
    return pl.BlockSpec((halo, width), lambda i: (jnp.maximum(i * per - 1, 0), 0))


def _pool_fwd_call(h0, g_mix, pool_w, pool_b, pool_scale, g_ffn):
    t = h0.shape[0]
    tt = _tile(t, ROW_TILE)

    def body(h_ref, halo_ref, gm_ref, w_ref, b_ref, sc_ref, gf_ref, h1_ref, hn_ref):
        i = pl.program_id(0)
        x, gm = h_ref[...], gm_ref[...]
        top = jnp.where(i > 0, _rms(halo_ref[...], gm), 0.0)
        ext = jnp.concatenate([top, _rms(x, gm)], axis=0)
        ps = _pool_windows_causal(ext, i * tt)
        ys = [jnp.dot(p.astype(MXU_DTYPE), w_ref[gi], preferred_element_type=F32) for gi, p in enumerate(ps)]
        mix = (jnp.concatenate(ys, axis=1) + b_ref[...]) * sc_ref[...]
        h1 = x + mix
        h1_ref[...] = h1
        hn_ref[...] = _rms(h1, gf_ref[...]).astype(hn_ref.dtype)

    row = pl.BlockSpec((tt, D_MODEL), lambda i: (i, 0))
    vec = pl.BlockSpec((1, D_MODEL), lambda i: (0, 0))
    wsp = pl.BlockSpec((len(POOL_WINDOWS), POOL_GC, POOL_GC), lambda i: (0, 0, 0))
    return pl.pallas_call(
        body, name="pool_fwd", grid=(t // tt,),
        in_specs=[row, _prev_halo_spec(tt, POOL_HALO, D_MODEL), vec, wsp, vec, vec, vec], out_specs=[row, row],
        out_shape=[jax.ShapeDtypeStruct(h0.shape, F32), jax.ShapeDtypeStruct(h0.shape, MXU_DTYPE)],
        compiler_params=_params("parallel"))(h0, h0, g_mix, pool_w, pool_b, pool_scale, g_ffn)


def _pool_bwd_call(h0, dh1, g_mix, pool_w, pool_b, pool_scale, hosted=None):
    t = h0.shape[0]
    tt = _tile(t, ROW_TILE)
    nt = t // tt
    per = tt // POOL_HALO

    def body(ins, outs, scratch, first, last):
        h_ref, halo_ref, d_ref, dnext_ref, gm_ref, w_ref, b_ref, sc_ref = ins
        gx_ref, dw_ref, dv_ref = outs
        i = pl.program_id(0)
        first()
        x, gm, sc = h_ref[...], gm_ref[...], sc_ref[...]
        top = jnp.where(i > 0, _rms(halo_ref[...], gm), 0.0)
        ext = jnp.concatenate([top, _rms(x, gm)], axis=0)
        ps = _pool_windows_causal(ext, i * tt)

        dy = d_ref[...]
        dy_ext = jnp.concatenate([dy, jnp.where(i < nt - 1, dnext_ref[...], 0.0)], axis=0)
        dpre_ext = dy_ext * sc
        t_ext = i * tt + lax.broadcasted_iota(jnp.int32, (tt + POOL_HALO, 1), 0)

        @pl.when(i == 0)
        def _():
            dw_ref[...] = jnp.zeros_like(dw_ref)
            dv_ref[...] = jnp.zeros_like(dv_ref)

        dhn, ypre = [], []
        for gi, win in enumerate(POOL_WINDOWS):
            cols = slice(gi * POOL_GC, (gi + 1) * POOL_GC)
            w = w_ref[gi]
            p16 = ps[gi].astype(MXU_DTYPE)
            dpre16 = dpre_ext[:, cols].astype(MXU_DTYPE)
            ypre.append(jnp.dot(p16, w, preferred_element_type=F32))
            dw_ref[gi] += lax.dot_general(p16, dpre16[:tt], (((0,), (0,)), ((), ())), preferred_element_type=F32)
            dp_ext = lax.dot_general(dpre16, w, (((1,), (1,)), ((), ())), preferred_element_type=F32)
            s = dp_ext / jnp.minimum(t_ext + 1, win).astype(F32)
            sh = 1
            while sh < win:
                s = s + _shift_up(s, sh)
                sh *= 2
            dhn.append(s[:tt] - dp_ext[:tt])
        dhn = jnp.concatenate(dhn, axis=1)
        ypre = jnp.concatenate(ypre, axis=1) + b_ref[...]
        dx, dgt = _rms_bwd(x, gm, dhn)
        gx_ref[...] = dy + dx
        dv_ref[0:1, :] += jnp.sum(dpre_ext[:tt], axis=0, keepdims=True)
        dv_ref[1:2, :] += jnp.sum(dy * ypre, axis=0, keepdims=True)
        dv_ref[2:3, :] += jnp.sum(dgt, axis=0, keepdims=True)
        last()

    row = pl.BlockSpec((tt, D_MODEL), lambda i: (i, 0))
    vec = pl.BlockSpec((1, D_MODEL), lambda i: (0, 0))
    nxt = pl.BlockSpec((POOL_HALO, D_MODEL), lambda i: (jnp.minimum((i + 1) * per, t // POOL_HALO - 1), 0))
    wsp = pl.BlockSpec((len(POOL_WINDOWS), POOL_GC, POOL_GC), lambda i: (0, 0, 0))
    return _call(
        body, hosted, name="pool_bwd", grid=(nt,),
        in_specs=[row, _prev_halo_spec(tt, POOL_HALO, D_MODEL), row, nxt, vec, wsp, vec, vec],
        out_specs=[row, wsp, pl.BlockSpec((8, D_MODEL), lambda i: (0, 0))],
        out_shape=[jax.ShapeDtypeStruct(h0.shape, F32),
                   jax.ShapeDtypeStruct((len(POOL_WINDOWS), POOL_GC, POOL_GC), F32),
                   jax.ShapeDtypeStruct((8, D_MODEL), F32)],
        operands=(h0, h0, dh1, dh1, g_mix, pool_w, pool_b, pool_scale), semantics=("arbitrary",))


CONV_ROWS = 256
CONV_SUB = 64
LANES = 128


def _causal_conv(u_ext, w, b):
    return w[0:1] * _shift_down(u_ext, 2) + w[1:2] * _shift_down(u_ext, 1) + w[2:3] * u_ext + b


def _fold8(x):
    acc = x[0:8]
    for r in range(8, x.shape[0], 8):
        acc = acc + x[r:r + 8]
    return acc


def _conv_glu_fwd_call(u, conv_w, conv_b, name, hosted=None):
    _, t, f = u.shape
    tt = _tile(t, CONV_ROWS)
    sub = _tile(tt, CONV_SUB)
    per = tt // CONV_HALO

    def body(ins, outs, scratch, first, last):
        u_ref, up_ref, w_ref, b_ref = ins
        z_ref, ext_ref = outs[0], scratch[0]
        i = pl.program_id(0)
        first()
        ext_ref[:, :CONV_HALO, :] = jnp.where(i > 0, up_ref[...], 0.0)
        ext_ref[:, CONV_HALO:, :] = u_ref[...]

        def chunk(j, carry):
            lanes = pl.ds(pl.multiple_of(j * LANES, LANES), LANES)
            w, b = w_ref[:, :, lanes], b_ref[:, lanes]
            for s in range(tt // sub):
                rows = pl.ds(s * sub, sub + CONV_HALO)
                c = [_causal_conv(ext_ref[hf, rows, lanes], w[:, hf, :], b[hf:hf + 1, :])[CONV_HALO:] for hf in range(2)]
                z_ref[pl.ds(s * sub, sub), lanes] = (c[0] * c[1] * _sigmoid(c[1])).astype(z_ref.dtype)
            return carry

        lax.fori_loop(0, f // LANES, chunk, 0)
        last()

    res = _call(
        body, hosted, name=name, grid=(t // tt,),
        in_specs=[pl.BlockSpec((2, tt, f), lambda i: (0, i, 0)),
                  pl.BlockSpec((2, CONV_HALO, f), lambda i: (0, jnp.maximum(i * per - 1, 0), 0)),
                  pl.BlockSpec((3, 2, f), lambda i: (0, 0, 0)), pl.BlockSpec((2, f), lambda i: (0, 0))],
        out_specs=[pl.BlockSpec((tt, f), lambda i: (i, 0))], out_shape=[jax.ShapeDtypeStruct((t, f), MXU_DTYPE)],
        operands=(u, u, conv_w, conv_b), semantics=("parallel",),
        scratch_shapes=[pltpu.VMEM((2, CONV_HALO + tt, f), F32)])
    return res if hosted else res[0]


def _conv_glu_bwd_call(u, dz, conv_w, conv_b, name):
    _, t, f = u.shape
    tt = _tile(t, CONV_ROWS)
    sub = _tile(tt, CONV_SUB)
    nt = t // tt
    per = tt // CONV_HALO
    halo = CONV_HALO

    def body(u_ref, up_ref, un_ref, dz_ref, dzn_ref, w_ref, b_ref, du_ref, dw_ref, db_ref, uext, dzext, dwacc, dbacc):
        i = pl.program_id(0)
        uext[:, :halo, :] = jnp.where(i > 0, up_ref[...], 0.0)
        uext[:, halo:halo + tt, :] = u_ref[...]
        uext[:, halo + tt:, :] = un_ref[...]
        dzext[:halo, :] = jnp.zeros((halo, f), F32)
        dzext[halo:halo + tt, :] = dz_ref[...].astype(F32)
        dzext[halo + tt:, :] = jnp.where(i < nt - 1, dzn_ref[...].astype(F32), 0.0)

        @pl.when(i == 0)
        def _():
            dwacc[...] = jnp.zeros_like(dwacc)
            dbacc[...] = jnp.zeros_like(dbacc)

        main = slice(halo, halo + sub)

        def chunk(j, carry):
            lanes = pl.ds(pl.multiple_of(j * LANES, LANES), LANES)
            w, b = w_ref[:, :, lanes], b_ref[:, lanes]
            for s in range(tt // sub):
                rows = pl.ds(s * sub, sub + 2 * halo)
                ue = [uext[hf, rows, lanes] for hf in range(2)]
                a, g = [_causal_conv(ue[hf], w[:, hf, :], b[hf:hf + 1, :]) for hf in range(2)]
                sg = _sigmoid(g)
                dzs = dzext[rows, lanes]
                dc = [dzs * g * sg, dzs * a * sg * (1.0 + g * (1.0 - sg))]
                for hf in range(2):
                    d = dc[hf]
                    d1, d2 = _shift_up(d, 1), _shift_up(d, 2)
                    du = w[2:3, hf, :] * d + w[1:2, hf, :] * d1 + w[0:1, hf, :] * d2
                    du_ref[hf, pl.ds(s * sub, sub), lanes] = du[main].astype(du_ref.dtype)
                    um = ue[hf][main]
                    for k, dd in enumerate((d2, d1, d)):
                        dwacc[k, hf, :, lanes] += _fold8(dd[main] * um)
                    dbacc[hf, :, lanes] += _fold8(d[main])
            return carry

        lax.fori_loop(0, f // LANES, chunk, 0)

        @pl.when(i == nt - 1)
        def _():
            dw_ref[...] = jnp.sum(dwacc[...], axis=2)
            db_ref[...] = jnp.sum(dbacc[...], axis=1)

    last = t // halo - 1
    return pl.pallas_call(
        body, name=name, grid=(nt,),
        in_specs=[pl.BlockSpec((2, tt, f), lambda i: (0, i, 0)),
                  pl.BlockSpec((2, halo, f), lambda i: (0, jnp.maximum(i * per - 1, 0), 0)),
                  pl.BlockSpec((2, halo, f), lambda i: (0, jnp.minimum((i + 1) * per, last), 0)),
                  pl.BlockSpec((tt, f), lambda i: (i, 0)),
                  pl.BlockSpec((halo, f), lambda i: (jnp.minimum((i + 1) * per, last), 0)),
                  pl.BlockSpec((3, 2, f), lambda i: (0, 0, 0)), pl.BlockSpec((2, f), lambda i: (0, 0))],
        out_specs=[pl.BlockSpec((2, tt, f), lambda i: (0, i, 0)),
                   pl.BlockSpec((3, 2, f), lambda i: (0, 0, 0)), pl.BlockSpec((2, f), lambda i: (0, 0))],
        out_shape=[jax.ShapeDtypeStruct(u.shape, MXU_DTYPE), jax.ShapeDtypeStruct((3, 2, f), F32),
                   jax.ShapeDtypeStruct((2, f), F32)],
        scratch_shapes=[pltpu.VMEM((2, tt + 2 * halo, f), F32), pltpu.VMEM((tt + 2 * halo, f), F32),
                        pltpu.VMEM((3, 2, 8, f), F32), pltpu.VMEM((2, 8, f), F32)],
        compiler_params=_params("arbitrary"))(u, u, u, dz, dz, conv_w, conv_b)


def _kv_even_odd(band, k):
    pair, parity = divmod(k, 2)
    blk = band[:, 128 * pair:128 * (pair + 1)].astype(F32)
    lane = lax.broadcasted_iota(jnp.int32, blk.shape, 1)
    own = jnp.where((lane >= HEAD_DIM) == (parity == 1), blk, 0.0)
    swapped = pltpu.roll(own, HEAD_DIM, axis=1)
    even, odd = (own, swapped) if parity == 0 else (swapped, own)
    return even.astype(MXU_DTYPE), odd.astype(MXU_DTYPE)


def _stack_pairs(ref, k):
    return jnp.concatenate([ref[:, 256 * k:256 * k + 128], ref[:, 256 * k + 128:256 * k + 256]], axis=0)


_GROUP_ORDER = (0, 2, 1, 3)


def _attn_fill_tables(bias_ref):
    shape = (4 * ATT_BLOCK, 2 * ATT_BLOCK)
    row = lax.broadcasted_iota(jnp.int32, shape, 0)
    col = lax.broadcasted_iota(jnp.int32, shape, 1)
    dist = (row & (ATT_BLOCK - 1)) + ATT_BLOCK - col
    ok = (dist >= 0) & (dist < ATT_BLOCK)
    distf = dist.astype(F32)
    rb = lax.broadcasted_iota(jnp.int32, (shape[0], 1), 0) // ATT_BLOCK
    for k in range(N_KV_HEADS):
        slope = jnp.zeros((shape[0], 1), F32)
        for r, gq in enumerate(_GROUP_ORDER):
            slope = jnp.where(rb == r, ALIBI_SLOPES[4 * k + gq], slope)
        bias = jnp.where(ok, -slope * distf, -jnp.inf)
        bias_ref[0, k] = bias
        bias_ref[1, k] = jnp.where(col >= ATT_BLOCK, bias, -jnp.inf)


def _attn_probs(qs, k_even, k_odd, bias, k, sink_ref):
    nt_dims = (((1,), (1,)), ((), ()))
    s = jnp.concatenate([lax.dot_general(qs, k_even, nt_dims, preferred_element_type=F32),
                         lax.dot_general(qs, k_odd, nt_dims, preferred_element_type=F32)], axis=0) + bias
    rb = lax.broadcasted_iota(jnp.int32, (s.shape[0], 1), 0) // ATT_BLOCK
    sink = jnp.zeros((s.shape[0], 1), F32)
    for r, gq in enumerate(_GROUP_ORDER):
        sink = jnp.where(rb == r, sink_ref[4 * k + gq], sink)
    m = jnp.maximum(jnp.max(s, axis=-1, keepdims=True), sink)
    p = jnp.exp(s - m)
    es = jnp.exp(sink - m)
    return p, 1.0 / (jnp.sum(p, axis=-1, keepdims=True) + es), es


_ATTN_TABLES = [pltpu.VMEM((2, N_KV_HEADS, 4 * ATT_BLOCK, 2 * ATT_BLOCK), F32)]


def _attn_specs(with_do):
    prev = lambda n: jnp.maximum(n - 1, 0)
    specs = [pl.BlockSpec(memory_space=pltpu.SMEM),
             pl.BlockSpec((ATT_BLOCK, D_MODEL), lambda n: (n, 0)),
             pl.BlockSpec((ATT_BLOCK, 256), lambda n: (prev(n), 4)), pl.BlockSpec((ATT_BLOCK, 256), lambda n: (n, 4)),
             pl.BlockSpec((ATT_BLOCK, 256), lambda n: (prev(n), 5)), pl.BlockSpec((ATT_BLOCK, 256), lambda n: (n, 5))]
    if with_do:
        specs.append(pl.BlockSpec((ATT_BLOCK, D_MODEL), lambda n: (n, 0)))
    return specs


def _attn_fwd_call(qkv, sinks):
    t = qkv.shape[0]

    def body(sink_ref, q_ref, kp_ref, kc_ref, vp_ref, vc_ref, o_ref, bias_ref):
        n = pl.program_id(0)

        @pl.when(n == 0)
        def _():
            _attn_fill_tables(bias_ref)

        first = 1 - jnp.minimum(n, 1)
        kband = jnp.concatenate([kp_ref[...], kc_ref[...]], axis=0)
        vband = jnp.concatenate([vp_ref[...], vc_ref[...]], axis=0)
        for k in range(N_KV_HEADS):
            k_even, k_odd = _kv_even_odd(kband, k)
            v_even, v_odd = _kv_even_odd(vband, k)
            qs = _stack_pairs(q_ref, k) * ATT_SCALE
            p, rdenom, _ = _attn_probs(qs, k_even, k_odd, bias_ref[first, k], k, sink_ref)
            probs = (p * rdenom).astype(MXU_DTYPE)
            o = (jnp.dot(probs[:256], v_even, preferred_element_type=F32)
                 + jnp.dot(probs[256:], v_odd, preferred_element_type=F32))
            o_ref[:, 256 * k:256 * k + 128] = o[:128].astype(o_ref.dtype)
            o_ref[:, 256 * k + 128:256 * k + 256] = o[128:].astype(o_ref.dtype)

    return pl.pallas_call(
        body, name="attn_fwd", grid=(t // ATT_BLOCK,), in_specs=_attn_specs(False),
        out_specs=pl.BlockSpec((ATT_BLOCK, D_MODEL), lambda n: (n, 0)),
        out_shape=jax.ShapeDtypeStruct((t, D_MODEL), MXU_DTYPE), scratch_shapes=_ATTN_TABLES,
        compiler_params=_params("arbitrary"))(sinks, qkv, qkv, qkv, qkv, qkv)


def _attn_bwd_call(qkv, sinks, do, hosted=None):
    t = qkv.shape[0]
    nb = t // ATT_BLOCK
    tn_dims = (((0,), (0,)), ((), ()))
    nt_dims = (((1,), (1,)), ((), ()))

    def to_native(even_part, odd_part, parity):
        lane = lax.broadcasted_iota(jnp.int32, even_part.shape, 1)
        lo = lane < HEAD_DIM
        e, o = jnp.where(lo, even_part, 0.0), jnp.where(lo, 0.0, odd_part)
        if parity == 0:
            return e + pltpu.roll(o, HEAD_DIM, axis=1)
        return pltpu.roll(e, HEAD_DIM, axis=1) + o

    def body(ins, outs, scratch, first, last):
        sink_ref, q_ref, kp_ref, kc_ref, vp_ref, vc_ref, do_ref = ins
        dq_ref, dk_ref, dv_ref, db_ref, dsink_ref = outs
        bias_ref = scratch[0]
        n = pl.program_id(0)
        first()

        @pl.when(n == 0)
        def _():
            _attn_fill_tables(bias_ref)
            dk_ref[...] = jnp.zeros_like(dk_ref)
            dv_ref[...] = jnp.zeros_like(dv_ref)
            db_ref[...] = jnp.zeros_like(db_ref)
            dsink_ref[...] = jnp.zeros_like(dsink_ref)

        kband = jnp.concatenate([kp_ref[...], kc_ref[...]], axis=0)
        vband = jnp.concatenate([vp_ref[...], vc_ref[...]], axis=0)
        dk_pairs = [jnp.zeros((2 * ATT_BLOCK, 128), F32), jnp.zeros((2 * ATT_BLOCK, 128), F32)]
        dv_pairs = [jnp.zeros((2 * ATT_BLOCK, 128), F32), jnp.zeros((2 * ATT_BLOCK, 128), F32)]
        sink_lane = lax.broadcasted_iota(jnp.int32, (8, 128), 1)
        sink_row = lax.broadcasted_iota(jnp.int32, (8, 128), 0)
        dsink = jnp.zeros((8, 128), F32)
        first = 1 - jnp.minimum(n, 1)
        for k in range(N_KV_HEADS):
            k_even, k_odd = _kv_even_odd(kband, k)
            v_even, v_odd = _kv_even_odd(vband, k)
            qs = _stack_pairs(q_ref, k) * ATT_SCALE
            dos = _stack_pairs(do_ref, k)
            p, rdenom, es = _attn_probs(qs, k_even, k_odd, bias_ref[first, k], k, sink_ref)
            probs = p * rdenom
            dp = jnp.concatenate([lax.dot_general(dos, v_even, nt_dims, preferred_element_type=F32),
                                  lax.dot_general(dos, v_odd, nt_dims, preferred_element_type=F32)], axis=0)
            delta = jnp.sum(probs * dp, axis=-1, keepdims=True)
            ds16 = (probs * (dp - delta)).astype(MXU_DTYPE)
            p16 = probs.astype(MXU_DTYPE)
            dsink_rows = -(es * rdenom) * delta
            for r, gq in enumerate(_GROUP_ORDER):
                tot = jnp.sum(dsink_rows[ATT_BLOCK * r:ATT_BLOCK * (r + 1)], axis=0, keepdims=True)
                dsink = dsink + jnp.where((sink_lane == 4 * k + gq) & (sink_row == 0), tot, 0.0)
            dqs = (jnp.dot(ds16[:256], k_even, preferred_element_type=F32)
                   + jnp.dot(ds16[256:], k_odd, preferred_element_type=F32)) * ATT_SCALE
            dq_ref[:, 256 * k:256 * k + 128] = dqs[:128].astype(dq_ref.dtype)
            dq_ref[:, 256 * k + 128:256 * k + 256] = dqs[128:].astype(dq_ref.dtype)
            db_ref[:, 256 * k:256 * k + 128] += jnp.sum(dqs[:128], axis=0, keepdims=True)
            db_ref[:, 256 * k + 128:256 * k + 256] += jnp.sum(dqs[128:], axis=0, keepdims=True)
            pair, parity = divmod(k, 2)
            dk_pairs[pair] = dk_pairs[pair] + to_native(
                lax.dot_general(ds16[:256], qs, tn_dims, preferred_element_type=F32),
                lax.dot_general(ds16[256:], qs, tn_dims, preferred_element_type=F32), parity)
            dv_pairs[pair] = dv_pairs[pair] + to_native(
                lax.dot_general(p16[:256], dos, tn_dims, preferred_element_type=F32),
                lax.dot_general(p16[256:], dos, tn_dims, preferred_element_type=F32), parity)
        dsink_ref[...] += dsink
        dk_band = jnp.concatenate(dk_pairs, axis=1)
        dv_band = jnp.concatenate(dv_pairs, axis=1)
        cur = pl.multiple_of(n * ATT_BLOCK, ATT_BLOCK)
        dk_ref[pl.ds(cur, ATT_BLOCK), :] += dk_band[ATT_BLOCK:]
        dv_ref[pl.ds(cur, ATT_BLOCK), :] += dv_band[ATT_BLOCK:]

        @pl.when(n > 0)
        def _():
            prv = pl.multiple_of((n - 1) * ATT_BLOCK, ATT_BLOCK)
            dk_ref[pl.ds(prv, ATT_BLOCK), :] += dk_band[:ATT_BLOCK]
            dv_ref[pl.ds(prv, ATT_BLOCK), :] += dv_band[:ATT_BLOCK]

        @pl.when(n == nb - 1)
        def _():
            db_ref[:, 1024:1280] = jnp.sum(dk_ref[...], axis=0, keepdims=True)
            db_ref[:, 1280:1536] = jnp.sum(dv_ref[...], axis=0, keepdims=True)

        last()

    whole = lambda shape: pl.BlockSpec(shape, lambda n: (0, 0))
    return _call(
        body, hosted, name="attn_bwd", grid=(nb,), in_specs=_attn_specs(True),
        out_specs=[pl.BlockSpec((ATT_BLOCK, D_MODEL), lambda n: (n, 0)), whole((t, 256)), whole((t, 256)),
                   whole((1, QKV_DIM)), whole((8, 128))],
        out_shape=[jax.ShapeDtypeStruct((t, D_MODEL), MXU_DTYPE), jax.ShapeDtypeStruct((t, 256), F32),
                   jax.ShapeDtypeStruct((t, 256), F32), jax.ShapeDtypeStruct((1, QKV_DIM), F32),
                   jax.ShapeDtypeStruct((8, 128), F32)],
        operands=(sinks, qkv, qkv, qkv, qkv, qkv, do), semantics=("arbitrary",), scratch_shapes=_ATTN_TABLES)


FF_TILE = 1408


def _ffn_fwd(hn, h, w, conv_w, conv_b, layer, host=None):
    t = hn.shape[0]
    tm = _tile(t, 512)
    halves = D_FF // FF_TILE

    def carry(stage, call):
        if host is None:
            return call(None), w
        res = call(host(stage, w))
        return res[0], {**w, **dict(zip(_late_names(stage), res[1:]))}

    u, w = carry("in", lambda hosted: _matmul(
        hn, w["w_in"], mode="nn", name=f"ffn{layer}_in", grid=(t // tm, 2 * halves, 1),
        a_spec=pl.BlockSpec((tm, D_MODEL), lambda i, j, k: (i, 0)),
        b_spec=pl.BlockSpec((None, D_MODEL, FF_TILE), lambda i, j, k: (layer, 0, j)),
        o_spec=pl.BlockSpec((None, tm, FF_TILE), lambda i, j, k: (j // halves, i, j % halves)),
        out_shape=jax.ShapeDtypeStruct((2, t, D_FF), F32), hosted=hosted))
    z, w = carry("glu", lambda hosted: _conv_glu_fwd_call(u, conv_w, conv_b, f"ffn{layer}_glu", hosted=hosted))
    h_out, w = carry("out", lambda hosted: _mm_nn(z, w["w_out"], f"ffn{layer}_out", b_lead=layer, residual=h,
                                                  tk=FF_TILE, hosted=hosted))
    return h_out, u, z, w


def _ffn_bwd(dh, hn, u, z, w_in, w_out, conv_w, conv_b, layer, g_in_acc, g_out_acc, host_dz=None, host_dhn=None):
    t = hn.shape[0]
    halves = D_FF // FF_TILE
    tm = _tile(t, 1024)
    dz = _mm_nt(dh, w_out, f"ffn{layer}_dz", b_lead=layer, out_dtype=MXU_DTYPE, tn=FF_TILE, hosted=host_dz)
    dz, from_dz = (dz[0], dz[1:]) if host_dz else (dz, [])
    g_out = _mm_tn(z, dh, f"ffn{layer}_dwout", lead=layer, n_lead=2, into=g_out_acc, tm=FF_TILE)
    du, dcw, dcb = _conv_glu_bwd_call(u, dz, conv_w, conv_b, f"ffn{layer}_dglu")
    g_in = _matmul(
        hn, du, mode="tn", name=f"ffn{layer}_dwin", grid=(1, 2 * halves, t // tm),
        a_spec=pl.BlockSpec((tm, D_MODEL), lambda i, j, k: (k, 0)),
        b_spec=pl.BlockSpec((None, tm, FF_TILE), lambda i, j, k: (j // halves, k, j % halves)),
        o_spec=pl.BlockSpec((None, D_MODEL, FF_TILE), lambda i, j, k: (layer, 0, j)),
        out_shape=jax.ShapeDtypeStruct((2, D_MODEL, 2 * D_FF), F32), into=g_in_acc)
    hosted = host_dhn(g_in, g_out) if host_dhn else None
    dhn = _matmul(
        du, w_in, mode="nt", name=f"ffn{layer}_dhn", grid=(t // tm, 1, 2 * halves),
        a_spec=pl.BlockSpec((None, tm, FF_TILE), lambda i, j, k: (k // halves, i, k % halves)),
        b_spec=pl.BlockSpec((None, D_MODEL, FF_TILE), lambda i, j, k: (layer, 0, k)),
        o_spec=pl.BlockSpec((tm, D_MODEL), lambda i, j, k: (i, 0)),
        out_shape=jax.ShapeDtypeStruct((t, D_MODEL), F32), hosted=hosted)
    dhn, from_dhn = (dhn[0], dhn[1:]) if hosted else (dhn, [])
    return dhn, g_in, g_out, dcw, dcb, from_dz, from_dhn


def _local_step(x, target, w, *, norm_mix, norm_ffn, norm_f, pool_b, pool_scale, b_qkv, sinks, conv_w, conv_b,
                late_gather=None, reducer=None):
    cw = [conv_w[l].reshape(3, 2, D_FF) for l in range(2)]
    cb = [conv_b[l].reshape(2, D_FF) for l in range(2)]
    gm = [norm_mix[l:l + 1] for l in range(2)]
    gf = [norm_ffn[l:l + 1] for l in range(2)]

    h1, hn_f0 = _pool_fwd_call(x, gm[0], w["pool_w"], pool_b, pool_scale, gf[0])
    h2, u0, z0, w = _ffn_fwd(hn_f0, h1, w, cw[0], cb[0], 0, host=late_gather)
    hn_a = _rms_fwd_call(h2, gm[1], "rms_attn", MXU_DTYPE)
    qkv = _mm_nn(hn_a, w["w_qkv"], "attn_qkv", bias=b_qkv, out_dtype=MXU_DTYPE, tn=768)
    o = _attn_fwd_call(qkv, sinks)
    h3 = _mm_nn(o, w["w_o"], "attn_out", residual=h2)
    hn_f1 = _rms_fwd_call(h3, gf[1], "rms_ffn1", MXU_DTYPE)
    h4, u1, z1, w = _ffn_fwd(hn_f1, h3, w, cw[1], cb[1], 1)
    loss, dh, d_norm_f = _loss_call(h4, norm_f, target)

    rd = reducer
    first, rest = _Reducer.FIRST, _Reducer.REST
    dhn, g_in, g_out, dcw1, dcb1, _, slabs = _ffn_bwd(
        dh, hn_f1, u1, z1, w["w_in"], w["w_out"], cw[1], cb[1], 1, None, None,
        host_dhn=(lambda gi, go: rd.swap(first, {"w_in": gi, "w_out": go})) if rd else None)
    if rd:
        rd.pair_sums(first, slabs)
    dh, d_gf1 = _rms_bwd_call(h3, gf[1], dhn, dh, "rms_ffn1_bwd")
    do = _mm_nt(dh, w["w_o"], "attn_do", out_dtype=MXU_DTYPE)
    g_o = _mm_tn(o, dh, "attn_dwo")
    dq, dk, dv, d_bqkv, d_sinks, *parts = _attn_bwd_call(qkv, sinks, do, hosted=rd.scatter(first) if rd else None)
    if rd:
        rd.chip_sums(first, parts)
    dqkv = jnp.concatenate([dq, dk.astype(MXU_DTYPE), dv.astype(MXU_DTYPE)], axis=1)
    g_qkv = _mm_tn(hn_a, dqkv, "attn_dwqkv", tn=768)
    dhn = _mm_nt(dqkv, w["w_qkv"], "attn_dhn", tk=768)
    dh, d_gm1 = _rms_bwd_call(h2, gm[1], dhn, dh, "rms_attn_bwd")
    g_o4 = g_o.reshape(N_CHIPS, 2, W_O_ROWS // 2, D_MODEL)
    dhn, g_in, g_out, dcw0, dcb0, updated, slabs = _ffn_bwd(
        dh, hn_f0, u0, z0, w["w_in"], w["w_out"], cw[0], cb[0], 0, g_in, g_out,
        host_dz=rd.share(first) if rd else None,
        host_dhn=(lambda gi, go: rd.swap(rest, {"w_in": gi, "w_out": go, "w_qkv": g_qkv, "w_o": g_o4})) if rd else None)
    if rd:
        rd.shared(first, updated)
        rd.pair_sums(rest, slabs)
    dh, d_gf0 = _rms_bwd_call(h1, gf[0], dhn, dh, "rms_ffn0_bwd")
    grad_x, g_pool, dvec, *parts = _pool_bwd_call(x, dh, gm[0], w["pool_w"], pool_b, pool_scale,
                                                  hosted=rd.scatter(rest) if rd else None)
    if rd:
        rd.chip_sums(rest, parts)
        rd.shared(rest, _alone(rd.share(rest), "share_rest"))

    grads = {
        "w_in": g_in, "w_out": g_out, "w_qkv": g_qkv, "w_o": g_o, "pool_w": g_pool,
        "pool_b": dvec[0:1], "pool_scale": dvec[1:2], "b_qkv": d_bqkv, "sinks": d_sinks[0:1, :N_HEADS],
        "norm_mix": jnp.concatenate([dvec[2:3], d_gm1], axis=0), "norm_ffn": jnp.concatenate([d_gf0, d_gf1], axis=0),
        "conv_w": jnp.stack([dcw0.reshape(3, 2 * D_FF), dcw1.reshape(3, 2 * D_FF)]),
        "conv_b": jnp.stack([dcb0.reshape(2 * D_FF), dcb1.reshape(2 * D_FF)]), "norm_f": d_norm_f,
    }
    return loss, grad_x, grads


W_IN_COLS = 2 * D_FF // N_CHIPS
W_OUT_ROWS = D_FF // N_CHIPS
QKV_COLS = QKV_DIM // N_CHIPS
W_O_ROWS = D_MODEL // N_CHIPS
POOL_ROWS = POOL_GC // N_CHIPS
BIG = ("w_in", "w_out", "w_qkv", "w_o", "pool_w")


def _mesh_place():
    x, y, c = lax.axis_index("x"), lax.axis_index("y"), lax.axis_index("c")
    chips = [(1 - x, y), (x, 1 - y), (1 - x, 1 - y)]
    return x, y, c, chips


def _col(k, width):
    return pl.ds(pl.multiple_of(k * width, 128), width)


def _row(k, height):
    return pl.ds(pl.multiple_of(k * height, 16), height)


def _full_piece(name, ref, k, h):
    if name == "w_in":
        return ref.at[h, :, _col(k, W_IN_COLS)]
    if name == "w_out":
        return ref.at[h, _row(k, W_OUT_ROWS), :]
    if name == "w_qkv":
        return ref.at[_row(h, D_MODEL // 2), _col(k, QKV_COLS)]
    if name == "w_o":
        return ref.at[_row(2 * k + h, W_O_ROWS // 2), :]
    return ref.at[pl.ds(2 * h, 2), _row(k, POOL_ROWS), :]


def _shard_half(name, ref, h):
    if name in ("w_in", "w_out"):
        return ref.at[h]
    if name == "w_qkv":
        return ref.at[_row(h, D_MODEL // 2), :]
    if name == "w_o":
        return ref.at[_row(h, W_O_ROWS // 2), :]
    return ref.at[pl.ds(2 * h, 2)]


FULL_SHAPES = {"w_in": (2, D_MODEL, 2 * D_FF), "w_out": (2, D_FF, D_MODEL), "w_qkv": (D_MODEL, QKV_DIM),
               "w_o": (D_MODEL, D_MODEL), "pool_w": (4, POOL_GC, POOL_GC)}
SHARD_SHAPES = {"w_in": (2, D_MODEL, W_IN_COLS), "w_out": (2, W_OUT_ROWS, D_MODEL), "w_qkv": (D_MODEL, QKV_COLS),
                "w_o": (W_O_ROWS, D_MODEL), "pool_w": (4, POOL_ROWS, POOL_GC)}
HALF_SHAPES = {"w_in": (D_MODEL, W_IN_COLS), "w_out": (W_OUT_ROWS, D_MODEL), "w_qkv": (D_MODEL // 2, QKV_COLS),
               "w_o": (W_O_ROWS // 2, D_MODEL), "pool_w": (2, POOL_ROWS, POOL_GC)}


def _place_shard(name, shard, me_arr):
    if name == "w_in":
        blk, grid = (1, 256, W_IN_COLS), (2, D_MODEL // 256)
        src, dst = (lambda a, b, me: (a, b, 0)), (lambda a, b, me: (a, b, me[0]))
    elif name == "w_out":
        blk, grid = (1, W_OUT_ROWS, D_MODEL), (2, 1)
        src, dst = (lambda a, b, me: (a, 0, 0)), (lambda a, b, me: (a, me[0], 0))
    elif name == "w_qkv":
        blk, grid = (256, QKV_COLS), (1, D_MODEL // 256)
        src, dst = (lambda a, b, me: (b, 0)), (lambda a, b, me: (b, me[0]))
    elif name == "w_o":
        blk, grid = (W_O_ROWS, D_MODEL), (1, 1)
        src, dst = (lambda a, b, me: (0, 0)), (lambda a, b, me: (me[0], 0))
    else:
        blk, grid = (4, POOL_ROWS, POOL_GC), (1, 1)
        src, dst = (lambda a, b, me: (0, 0, 0)), (lambda a, b, me: (0, me[0], 0))

    def body(me_ref, s_ref, o_ref):
        o_ref[...] = s_ref[...].astype(o_ref.dtype)

    return pl.pallas_call(
        body, name=f"place_{name}",
        grid_spec=pltpu.PrefetchScalarGridSpec(num_scalar_prefetch=1, grid=grid, in_specs=[pl.BlockSpec(blk, src)],
                                               out_specs=pl.BlockSpec(blk, dst)),
        out_shape=jax.ShapeDtypeStruct(FULL_SHAPES[name], WIRE_DTYPE), compiler_params=_params("parallel", "parallel"),
    )(me_arr, shard)


def _when_mover(half, c, fn):
    if half is None:
        fn(c)
    else:
        pl.when(c == half)(lambda: fn(half))


def _when_taker(half, c, fn):
    if half is None:
        fn(1 - c)
    else:
        pl.when(c != half)(lambda: fn(half))


def _remote(ref, sems, t, j, dev):
    send, recv = sems
    return pltpu.make_async_remote_copy(src_ref=ref, dst_ref=ref, send_sem=send.at[t, j], recv_sem=recv.at[t, j],
                                        device_id=dev, device_id_type=MESH)


def _gather_ici(items, refs, sems, finish):
    x, y, c, chips = _mesh_place()
    me = 2 * x + y
    for t, (nm, half) in enumerate(items):
        def go(h, t=t, nm=nm):
            mine = _full_piece(nm, refs[t], me, h)
            for j, (cx, cy) in enumerate(chips):
                if not finish:
                    _remote(mine, sems, t, j, (cx, cy, c)).start()
                else:
                    _remote(_full_piece(nm, refs[t], 2 * cx + cy, h), sems, t, j, (cx, cy, c)).wait_recv()
                    _remote(mine, sems, t, j, (cx, cy, c)).wait_send()

        _when_mover(half, c, go)


def _gather_pass(items, refs, sems, finish):
    x, y, c, chips = _mesh_place()
    sibling = (x, y, 1 - c)
    for t, (nm, half) in enumerate(items):
        def give(h, t=t, nm=nm):
            for j, (cx, cy) in enumerate(chips):
                got = _remote(_full_piece(nm, refs[t], 2 * cx + cy, h), sems, t, j, sibling)
                if not finish:
                    got.start()
                else:
                    got.wait_send()

        def take(h, t=t, nm=nm):
            for j, (cx, cy) in enumerate(chips):
                _remote(_full_piece(nm, refs[t], 2 * cx + cy, h), sems, t, j, sibling).wait_recv()

        _when_mover(half, c, give)
        if finish:
            _when_taker(half, c, take)


def _gather_sems(n_items):
    return [pltpu.SemaphoreType.DMA((n_items, 3)), pltpu.SemaphoreType.DMA((n_items, 3))]


EARLY_ITEMS = (("pool_w", None), ("w_in", 0), ("w_out", 0))
LATE_ITEMS_A = (("w_out", 1), ("w_qkv", None), ("w_o", None))
LATE_ITEMS_B = (("w_in", 1),)


def _names(items):
    return [nm for nm, _ in items]


def _gather_early(full, small):
    items = EARLY_ITEMS
    n = len(items)

    def body(*refs):
        small_in = refs[n]
        outs, small_out = refs[n + 1:2 * n + 1], refs[2 * n + 1]
        scratch = refs[2 * n + 2:]
        ici_sems, pass_sems = scratch[0:2], scratch[2:4]
        local_sem, small_send, small_recv = scratch[4:]
        x, y, c, chips = _mesh_place()
        me = 2 * x + y
        local = pltpu.make_async_copy(small_in, small_out.at[me], local_sem.at[0])
        local.start()
        small_sends = [pltpu.make_async_remote_copy(
            src_ref=small_in, dst_ref=small_out.at[me], send_sem=small_send.at[j], recv_sem=small_recv.at[j],
            device_id=(cx, cy, c), device_id_type=MESH) for j, (cx, cy) in enumerate(chips)]
        for cp in small_sends:
            cp.start()
        _gather_ici(items, outs, ici_sems, finish=False)
        _gather_ici(items, outs, ici_sems, finish=True)
        _gather_pass(items, outs, pass_sems, finish=False)
        _gather_pass(items, outs, pass_sems, finish=True)
        for j, (cx, cy) in enumerate(chips):
            got = small_out.at[2 * cx + cy]
            pltpu.make_async_remote_copy(src_ref=got, dst_ref=got, send_sem=small_send.at[j],
                                         recv_sem=small_recv.at[j], device_id=(cx, cy, c), device_id_type=MESH).wait_recv()
        for cp in small_sends:
            cp.wait_send()
        local.wait()

    arrays = [full[nm] for nm in _names(items)]
    out_shape = [jax.ShapeDtypeStruct(a.shape, a.dtype) for a in arrays]
    out_shape.append(jax.ShapeDtypeStruct((N_CHIPS,) + small.shape, F32))
    res = pl.pallas_call(
        body, name="gather_early", in_specs=[ANY] * (n + 1), out_specs=[ANY] * (n + 1), out_shape=out_shape,
        input_output_aliases={t: t for t in range(n)},
        scratch_shapes=_gather_sems(n) + _gather_sems(n) + [pltpu.SemaphoreType.DMA((1,)), pltpu.SemaphoreType.DMA((3,)),
                                                            pltpu.SemaphoreType.DMA((3,))],
    )(*arrays, small)
    return {**full, **dict(zip(_names(items), res[:n]))}, res[n]


def _late_gather_host(stage, w):
    a, b = LATE_ITEMS_A, LATE_ITEMS_B
    na = len(a)

    def run(src, refs, new, sems, finish):
        if stage == "in":
            _gather_ici(a, refs, sems, finish)
        elif stage == "glu":
            _gather_pass(a, refs[:na], sems[:2], finish)
            _gather_ici(b, refs[na:], sems[2:], finish)
        else:
            _gather_pass(b, refs, sems, finish)

    sems = {"in": _gather_sems(na), "glu": _gather_sems(na) + _gather_sems(len(b)), "out": _gather_sems(len(b))}[stage]
    return _Hosted([], [w[nm] for nm in _late_names(stage)], [], sems, functools.partial(run, finish=False),
                   functools.partial(run, finish=True))


def _late_names(stage):
    return {"in": _names(LATE_ITEMS_A), "glu": _names(LATE_ITEMS_A) + _names(LATE_ITEMS_B), "out": _names(LATE_ITEMS_B)}[stage]


def _other_half_src(name, ref, h):
    if name in ("w_in", "w_out"):
        return ref.at[h]
    if name == "w_qkv":
        return ref.at[_row(h, D_MODEL // 2), :]
    if name == "w_o":
        return ref.at[:, pl.ds(h, 1)]
    return ref.at[pl.ds(2 * h, 2)]


SLAB_SHAPES = {"w_in": (D_MODEL, 2 * D_FF), "w_out": (D_FF, D_MODEL), "w_qkv": (D_MODEL // 2, QKV_DIM),
               "w_o": (N_CHIPS, 1, W_O_ROWS // 2, D_MODEL), "pool_w": (2, POOL_GC, POOL_GC)}


def _item_tag(item):
    return item[0] if item[1] is None else f"{item[0]}{item[1]}"


def _reduce_swap(items, grads, slabs, sems, finish):
    x, y, c, _ = _mesh_place()
    send, recv = sems
    for t, (nm, half) in enumerate(items):
        def copy(h, t=t, nm=nm):
            return pltpu.make_async_remote_copy(
                src_ref=_other_half_src(nm, grads[t], h), dst_ref=slabs[t], send_sem=send.at[t], recv_sem=recv.at[t],
                device_id=(x, y, 1 - c), device_id_type=MESH)

        def give(h):
            if finish:
                copy(h).wait_send()
            else:
                copy(h).start()

        if half is None:
            give(1 - c)
            if finish:
                copy(c).wait_recv()
        else:
            pl.when(c != half)(lambda: give(half))
            if finish:
                pl.when(c == half)(lambda: copy(half).wait_recv())


def _reduce_scatter(items, sums, parts, sems, finish):
    _, _, c, chips = _mesh_place()
    send, recv = sems
    for t, (nm, half) in enumerate(items):
        def go(h, t=t, nm=nm):
            for j, (cx, cy) in enumerate(chips):
                cp = pltpu.make_async_remote_copy(
                    src_ref=_slab_piece(nm, sums[t], 2 * cx + cy), dst_ref=parts[t].at[j], send_sem=send.at[t, j],
                    recv_sem=recv.at[t, j], device_id=(cx, cy, c), device_id_type=MESH)
                if finish:
                    cp.wait_recv()
                    cp.wait_send()
                else:
                    cp.start()

        _when_mover(half, c, go)


def _reduce_share(items, shards, sems, finish):
    x, y, c, _ = _mesh_place()
    send, recv = sems
    for t, (nm, half) in enumerate(items):
        def copy(h, t=t, nm=nm):
            part = _shard_half(nm, shards[t], h)
            return pltpu.make_async_remote_copy(src_ref=part, dst_ref=part, send_sem=send.at[t], recv_sem=recv.at[t],
                                                device_id=(x, y, 1 - c), device_id_type=MESH)

        def give(h):
            if finish:
                copy(h).wait_send()
            else:
                copy(h).start()

        _when_mover(half, c, give)
        if finish:
            _when_taker(half, c, lambda h: copy(h).wait_recv())


def _pair_sems(n):
    return [pltpu.SemaphoreType.DMA((n,)), pltpu.SemaphoreType.DMA((n,))]


def _swap_host(items, grads):
    def run(src, arr, new, sems, finish):
        _reduce_swap(items, src, new, sems, finish)

    return _Hosted([grads[nm] for nm, _ in items], [], [jax.ShapeDtypeStruct(SLAB_SHAPES[nm], F32) for nm, _ in items],
                   _pair_sems(len(items)), functools.partial(run, finish=False), functools.partial(run, finish=True))


def _scatter_host(items, sums):
    def run(src, arr, new, sems, finish):
        _reduce_scatter(items, src, new, sems, finish)

    new = [jax.ShapeDtypeStruct((N_CHIPS - 1,) + HALF_SHAPES[nm], WIRE_DTYPE) for nm, _ in items]
    return _Hosted(list(sums), [], new, _gather_sems(len(items)), functools.partial(run, finish=False),
                   functools.partial(run, finish=True))


def _share_host(items, shards):
    def run(src, arr, new, sems, finish):
        _reduce_share(items, arr, sems, finish)

    return _Hosted([], list(shards), [], _pair_sems(len(items)), functools.partial(run, finish=False),
                   functools.partial(run, finish=True))


def _alone(hosted, name):
    def body(ins, outs, scratch, first, last):
        first()
        last()

    return _call(body, hosted, name=name, grid=(1,), in_specs=[], out_specs=[], out_shape=[], operands=(),
                 semantics=("arbitrary",))


def _pair_sum(item, grad, slab, place):
    name, half = item
    shape = SLAB_SHAPES[name]
    hsel = (lambda p: p[1]) if half is None else (lambda p: half)
    if name == "w_in":
        blk = (256, 2 * D_FF)
        grid = (D_MODEL // 256,)
        g_spec = pl.BlockSpec((None,) + blk, lambda i, p: (hsel(p), i, 0))
        s_spec = pl.BlockSpec(blk, lambda i, p: (i, 0))
    elif name == "w_out":
        blk = (W_OUT_ROWS, D_MODEL)
        grid = (N_CHIPS,)
        g_spec = pl.BlockSpec((None,) + blk, lambda i, p: (hsel(p), i, 0))
        s_spec = pl.BlockSpec(blk, lambda i, p: (i, 0))
    elif name == "w_qkv":
        grid = (1,)
        g_spec = pl.BlockSpec(shape, lambda i, p: (hsel(p), 0))
        s_spec = pl.BlockSpec(shape, lambda i, p: (0, 0))
    else:
        grid = (1,)
        g_spec = pl.BlockSpec(shape, lambda i, p: (0, hsel(p), 0, 0))
        s_spec = pl.BlockSpec(shape, lambda i, p: (0, 0, 0, 0))

    def body(p_ref, g_ref, s_ref, o_ref):
        @pl.when(hsel(p_ref) == p_ref[1])
        def _():
            o_ref[...] = (g_ref[...] + s_ref[...]).astype(o_ref.dtype)

    return pl.pallas_call(
        body, name=f"pair_sum_{_item_tag(item)}",
        grid_spec=pltpu.PrefetchScalarGridSpec(num_scalar_prefetch=1, grid=grid, in_specs=[g_spec, s_spec],
                                               out_specs=s_spec),
        out_shape=jax.ShapeDtypeStruct(shape, WIRE_DTYPE), compiler_params=_params("parallel"),
    )(place, grad, slab)


def _slab_piece(name, ref, k):
    if name == "w_in":
        return ref.at[:, _col(k, W_IN_COLS)]
    if name == "w_out":
        return ref.at[_row(k, W_OUT_ROWS), :]
    if name == "w_qkv":
        return ref.at[:, _col(k, QKV_COLS)]
    if name == "w_o":
        return ref.at[k, 0]
    return ref.at[:, _row(k, POOL_ROWS), :]


def _sum_chips(item, slab, parts, place, into=None):
    name, sel = item
    half = HALF_SHAPES[name]
    hsel = (lambda p: p[1]) if sel is None else (lambda p: sel)
    if name == "w_in":
        blk, grid = (256, W_IN_COLS), (D_MODEL // 256,)
        own = pl.BlockSpec(blk, lambda i, p: (i, p[0]))
        got = pl.BlockSpec((N_CHIPS - 1,) + blk, lambda i, p: (0, i, 0))
        out = pl.BlockSpec((None,) + blk, lambda i, p: (hsel(p), i, 0))
    elif name == "w_out":
        grid = (1,)
        own = pl.BlockSpec(half, lambda i, p: (p[0], 0))
        got = pl.BlockSpec((N_CHIPS - 1,) + half, lambda i, p: (0, 0, 0))
        out = pl.BlockSpec((None,) + half, lambda i, p: (hsel(p), 0, 0))
    elif name == "w_qkv":
        grid = (1,)
        own = pl.BlockSpec(half, lambda i, p: (0, p[0]))
        got = pl.BlockSpec((N_CHIPS - 1,) + half, lambda i, p: (0, 0, 0))
        out = pl.BlockSpec(half, lambda i, p: (hsel(p), 0))
    else:
        grid = (1,)
        own = pl.BlockSpec((None, None) + half, lambda i, p: (p[0], 0, 0, 0))
        got = pl.BlockSpec((N_CHIPS - 1,) + half, lambda i, p: (0, 0, 0))
        out = pl.BlockSpec(half, lambda i, p: (hsel(p), 0))

    def body(p_ref, own_ref, got_ref, *rest):
        o_ref = rest[-1]

        @pl.when(hsel(p_ref) == p_ref[1])
        def _():
            acc = own_ref[...].astype(F32)
            for j in range(N_CHIPS - 1):
                acc = acc + got_ref[j].astype(F32)
            o_ref[...] = acc

    in_specs, operands, aliases = [own, got], [place, slab, parts], {}
    if into is not None:
        in_specs.append(ANY)
        operands.append(into)
        aliases = {3: 0}
    return pl.pallas_call(
        body, name=f"sum_chips_{_item_tag(item)}",
        grid_spec=pltpu.PrefetchScalarGridSpec(num_scalar_prefetch=1, grid=grid, in_specs=in_specs, out_specs=out),
        out_shape=jax.ShapeDtypeStruct(SHARD_SHAPES[name], F32), input_output_aliases=aliases,
        compiler_params=_params("parallel"),
    )(*operands)


class _Reducer:
    FIRST = (("w_in", 1), ("w_out", 1))
    REST = (("w_in", 0), ("w_out", 0), ("w_qkv", None), ("w_o", None))

    def __init__(self, place):
        self.place = place
        self.shards = {}
        self.state = {}

    def swap(self, items, grads):
        self.state[items] = {"grads": [grads[nm] for nm, _ in items]}
        return _swap_host(items, grads)

    def pair_sums(self, items, slabs):
        st = self.state[items]
        st["sums"] = [_pair_sum(it, g, s, self.place) for it, g, s in zip(items, st["grads"], slabs)]

    def scatter(self, items):
        return _scatter_host(items, self.state[items]["sums"])

    def chip_sums(self, items, parts):
        for it, s, p in zip(items, self.state[items]["sums"], parts):
            self.shards[it[0]] = _sum_chips(it, s, p, self.place, into=self.shards.get(it[0]))

    def share(self, items):
        return _share_host(items, [self.shards[nm] for nm, _ in items])

    def shared(self, items, updated):
        self.shards.update(dict(zip(_names(items), updated)))


N_DEV = 8


def _all_reduce_small(vec, g_pool):
    def body(v_ref, gp_ref, o_ref, po_ref, slots, pslots, send_sems, recv_sems, psend, precv, hsend, hrecv):
        x, y, c, _ = _mesh_place()
        me = 4 * x + 2 * y + c

        def pool_piece(chip, h):
            return gp_ref.at[pl.ds(2 * h, 2), pl.ds(pl.multiple_of(chip * POOL_ROWS, POOL_ROWS), POOL_ROWS), :]

        slots[me] = v_ref[...]
        pslots[me] = pool_piece(2 * x + y, c)[...]
        copies = []
        for d in range(1, N_DEV):
            peer = (x ^ (d >> 2), y ^ ((d >> 1) & 1), c ^ (d & 1))
            copies.append(pltpu.make_async_remote_copy(
                src_ref=v_ref, dst_ref=slots.at[me], send_sem=send_sems.at[d - 1], recv_sem=recv_sems.at[d - 1],
                device_id=peer, device_id_type=MESH))
            copies.append(pltpu.make_async_remote_copy(
                src_ref=pool_piece(2 * peer[0] + peer[1], peer[2]), dst_ref=pslots.at[me], send_sem=psend.at[d - 1],
                recv_sem=precv.at[d - 1], device_id=peer, device_id_type=MESH))
        for cp in copies:
            cp.start()
        for d in range(1, N_DEV):
            peer = 4 * (x ^ (d >> 2)) + 2 * (y ^ ((d >> 1) & 1)) + (c ^ (d & 1))
            for buf, ss, rs in ((slots, send_sems, recv_sems), (pslots, psend, precv)):
                got = buf.at[peer]
                pltpu.make_async_remote_copy(src_ref=got, dst_ref=got, send_sem=ss.at[d - 1], recv_sem=rs.at[d - 1],
                                             device_id=(x, y, c), device_id_type=MESH).wait_recv()
        for cp in copies:
            cp.wait_send()
        acc, pacc = slots[0], pslots[0]
        for d in range(1, N_DEV):
            acc, pacc = acc + slots[d], pacc + pslots[d]
        o_ref[...] = acc
        po_ref[pl.ds(2 * c, 2)] = pacc
        mine, theirs = po_ref.at[pl.ds(2 * c, 2)], po_ref.at[pl.ds(2 * (1 - c), 2)]
        give = pltpu.make_async_remote_copy(src_ref=mine, dst_ref=mine, send_sem=hsend.at[0], recv_sem=hrecv.at[0],
                                            device_id=(x, y, 1 - c), device_id_type=MESH)
        give.start()
        pltpu.make_async_remote_copy(src_ref=theirs, dst_ref=theirs, send_sem=hsend.at[0], recv_sem=hrecv.at[0],
                                     device_id=(x, y, 1 - c), device_id_type=MESH).wait_recv()
        give.wait_send()

    vm = pl.BlockSpec(memory_space=pltpu.VMEM)
    piece = (2, POOL_ROWS, POOL_GC)
    return pl.pallas_call(
        body, name="all_reduce_small", in_specs=[vm, vm], out_specs=[vm, vm],
        out_shape=[jax.ShapeDtypeStruct(vec.shape, F32), jax.ShapeDtypeStruct(SHARD_SHAPES["pool_w"], F32)],
        scratch_shapes=[pltpu.VMEM((N_DEV,) + vec.shape, F32), pltpu.VMEM((N_DEV,) + piece, F32)]
        + [pltpu.SemaphoreType.DMA((N_DEV - 1,))] * 4 + [pltpu.SemaphoreType.DMA((1,))] * 2,
    )(vec, g_pool)


def _adamw_call(w, g, m, v, name):
    shape = w.shape
    cols = shape[-1]
    rows = w.size // cols
    tr = rows
    for cand in (256, 128, 64, 32, 16, 8):
        if rows > cand and rows % cand == 0 and cand * cols * 4 <= 2 * 1024 * 1024:
            tr = cand
            break

    def body(w_ref, g_ref, m_ref, v_ref, d_ref, nm_ref, nv_ref):
        gv = g_ref[...]
        mn = ADAM_B1 * m_ref[...] + (1.0 - ADAM_B1) * gv
        vn = ADAM_B2 * v_ref[...] + (1.0 - ADAM_B2) * jnp.square(gv)
        m_hat = mn / (1.0 - ADAM_B1 ** ADAM_STEP)
        v_hat = vn / (1.0 - ADAM_B2 ** ADAM_STEP)
        d_ref[...] = -ADAM_LR * (m_hat / (jnp.sqrt(v_hat) + ADAM_EPS) + ADAM_WD * w_ref[...])
        nm_ref[...] = mn
        nv_ref[...] = vn

    spec = pl.BlockSpec((tr, cols), lambda i: (i, 0))
    flat = lambda a: a.reshape(rows, cols)
    outs = pl.pallas_call(
        body, name=name, grid=(rows // tr,), in_specs=[spec] * 4, out_specs=[spec] * 3,
        out_shape=[jax.ShapeDtypeStruct((rows, cols), F32)] * 3, compiler_params=_params("parallel"),
    )(flat(w), flat(g), flat(m), flat(v))
    return tuple(o.reshape(shape) for o in outs)


WEIGHTS = ("pool_w", "pool_b", "pool_scale", "attn_w_qkv", "attn_b_qkv", "attn_sinks", "attn_w_o", "norm_mix",
           "norm_ffn", "ffn_w_in", "ffn_conv_w", "ffn_conv_b", "ffn_w_out", "norm_f")
SMALL_GATHER = 128 * 71
_SMALL_SIZES = (("pool_b", 1024), ("pool_scale", 1024), ("b_qkv", 1536), ("sinks", 128), ("norm_mix", 2048),
                ("norm_ffn", 2048), ("conv_w", 6 * 2 * D_FF), ("conv_b", 4 * D_FF), ("norm_f", 1024))
SMALL_COLS = 6784


def _pack_small(grads):
    flat = jnp.concatenate([jnp.pad(grads[nm].reshape(-1), (0, size - grads[nm].size)) for nm, size in _SMALL_SIZES])
    return jnp.pad(flat, (0, 8 * SMALL_COLS - flat.size)).reshape(8, SMALL_COLS)


def _unpack_small(vec):
    flat = vec.reshape(-1)
    out, off = {}, 0
    for nm, size in _SMALL_SIZES:
        out[nm] = flat[off:off + size]
        off += size
    return out


def kernel(x, pool_w, pool_b, pool_scale, attn_w_qkv, attn_b_qkv, attn_sinks, attn_w_o, norm_mix, norm_ffn, ffn_w_in, ffn_conv_w, ffn_conv_b, ffn_w_out, norm_f, loss_target, m_pool_w, m_pool_b, m_pool_scale, m_attn_w_qkv, m_attn_b_qkv, m_attn_sinks, m_attn_w_o, m_norm_mix, m_norm_ffn, m_ffn_w_in, m_ffn_conv_w, m_ffn_conv_b, m_ffn_w_out, m_norm_f, v_pool_w, v_pool_b, v_pool_scale, v_attn_w_qkv, v_attn_b_qkv, v_attn_sinks, v_attn_w_o, v_norm_mix, v_norm_ffn, v_ffn_w_in, v_ffn_conv_w, v_ffn_conv_b, v_ffn_w_out, v_norm_f):
    weights = dict(pool_w=pool_w, pool_b=pool_b, pool_scale=pool_scale, attn_w_qkv=attn_w_qkv, attn_b_qkv=attn_b_qkv,
                   attn_sinks=attn_sinks, attn_w_o=attn_w_o, norm_mix=norm_mix, norm_ffn=norm_ffn, ffn_w_in=ffn_w_in,
                   ffn_conv_w=ffn_conv_w, ffn_conv_b=ffn_conv_b, ffn_w_out=ffn_w_out, norm_f=norm_f)
    m_in = dict(pool_w=m_pool_w, pool_b=m_pool_b, pool_scale=m_pool_scale, attn_w_qkv=m_attn_w_qkv,
                attn_b_qkv=m_attn_b_qkv, attn_sinks=m_attn_sinks, attn_w_o=m_attn_w_o, norm_mix=m_norm_mix,
                norm_ffn=m_norm_ffn, ffn_w_in=m_ffn_w_in, ffn_conv_w=m_ffn_conv_w, ffn_conv_b=m_ffn_conv_b,
                ffn_w_out=m_ffn_w_out, norm_f=m_norm_f)
    v_in = dict(pool_w=v_pool_w, pool_b=v_pool_b, pool_scale=v_pool_scale, attn_w_qkv=v_attn_w_qkv,
                attn_b_qkv=v_attn_b_qkv, attn_sinks=v_attn_sinks, attn_w_o=v_attn_w_o, norm_mix=v_norm_mix,
                norm_ffn=v_norm_ffn, ffn_w_in=v_ffn_w_in, ffn_conv_w=v_ffn_conv_w, ffn_conv_b=v_ffn_conv_b,
                ffn_w_out=v_ffn_w_out, norm_f=v_norm_f)
    chip = 2 * lax.axis_index("x") + lax.axis_index("y")
    core = lax.axis_index("c")

    place = jnp.stack([chip, core]).astype(jnp.int32)
    shards = {"w_in": ffn_w_in, "w_out": ffn_w_out, "w_qkv": attn_w_qkv[0], "w_o": attn_w_o[0], "pool_w": pool_w[0]}
    placed = {nm: _place_shard(nm, shards[nm], place) for nm in BIG}
    small = jnp.concatenate([pool_b.reshape(-1), attn_b_qkv.reshape(-1), ffn_conv_w.reshape(-1)]).reshape(1, SMALL_GATHER)
    full, small_all = _gather_early(placed, small)
    small_all = small_all.reshape(N_CHIPS, SMALL_GATHER)
    pool_b_full = small_all[:, :256].reshape(N_CHIPS, 4, POOL_ROWS).transpose(1, 0, 2).reshape(1, D_MODEL)
    b_qkv_full = small_all[:, 256:640].reshape(1, QKV_DIM)
    conv_w_full = small_all[:, 640:].reshape(N_CHIPS, 2, 3, W_IN_COLS).transpose(1, 2, 0, 3).reshape(2, 3, 2 * D_FF)

    reducer = _Reducer(place)
    loss, grad_x, grads = _local_step(
        x[0], loss_target[0], full, norm_mix=norm_mix, norm_ffn=norm_ffn, norm_f=norm_f.reshape(1, D_MODEL),
        pool_b=pool_b_full, pool_scale=pool_scale, b_qkv=b_qkv_full, sinks=attn_sinks[0], conv_w=conv_w_full,
        conv_b=ffn_conv_b, late_gather=_late_gather_host, reducer=reducer)
    loss = lax.psum(loss[0, 0], ("x", "y", "c"))
    reduced = dict(reducer.shards)

    small_sum, reduced["pool_w"] = _all_reduce_small(_pack_small(grads), grads["pool_w"])
    small_g = _unpack_small(small_sum)
    pool_b_g = lax.dynamic_slice_in_dim(small_g["pool_b"].reshape(4, N_CHIPS, POOL_ROWS), chip, 1, axis=1)
    b_qkv_g = lax.dynamic_slice_in_dim(small_g["b_qkv"].reshape(N_CHIPS, QKV_COLS), chip, 1, axis=0)
    conv_w_g = lax.dynamic_slice_in_dim(small_g["conv_w"].reshape(2, 3, N_CHIPS, W_IN_COLS), chip, 1, axis=2)
    grad_w = {
        "pool_w": reduced["pool_w"].reshape(pool_w.shape), "pool_b": pool_b_g.reshape(pool_b.shape),
        "pool_scale": small_g["pool_scale"].reshape(pool_scale.shape),
        "attn_w_qkv": reduced["w_qkv"].reshape(attn_w_qkv.shape), "attn_b_qkv": b_qkv_g.reshape(attn_b_qkv.shape),
        "attn_sinks": small_g["sinks"][:N_HEADS].reshape(attn_sinks.shape),
        "attn_w_o": reduced["w_o"].reshape(attn_w_o.shape), "norm_mix": small_g["norm_mix"].reshape(norm_mix.shape),
        "norm_ffn": small_g["norm_ffn"].reshape(norm_ffn.shape), "ffn_w_in": reduced["w_in"],
        "ffn_conv_w": conv_w_g.reshape(ffn_conv_w.shape), "ffn_conv_b": small_g["conv_b"].reshape(ffn_conv_b.shape),
        "ffn_w_out": reduced["w_out"], "norm_f": small_g["norm_f"].reshape(norm_f.shape),
    }

    delta, new_m, new_v = {}, {}, {}
    for nm in WEIGHTS:
        delta[nm], new_m[nm], new_v[nm] = _adamw_call(weights[nm], grad_w[nm], m_in[nm], v_in[nm], f"adamw_{nm}")
    return (loss, grad_x.reshape(x.shape), *[grad_w[nm] for nm in WEIGHTS], *[delta[nm] for nm in WEIGHTS],
            *[new_m[nm] for nm in WEIGHTS], *[new_v[nm] for nm in WEIGHTS])
```

```python
import functools
from typing import Callable, NamedTuple

import jax
import jax.numpy as jnp
from jax import lax
from jax.experimental import pallas as pl
from jax.experimental.pallas import tpu as pltpu

F32 = jnp.float32
MXU_DTYPE = jnp.bfloat16
WIRE_DTYPE = jnp.bfloat16

D_MODEL = 1024
D_FF = 2816
QKV_DIM = 1536
HEAD_DIM = 64
N_HEADS = 16
N_KV_HEADS = 4
ATT_BLOCK = 128
POOL_WINDOWS = (2, 4, 8, 16)
POOL_GC = 256
POOL_HALO = 16
CONV_HALO = 8
RMS_EPS = 1e-6
ATT_SCALE = HEAD_DIM ** -0.5
ALIBI_SLOPES = tuple(2.0 ** (-8.0 / N_HEADS * (h + 1)) for h in range(N_HEADS))

ADAM_LR = 0.001
ADAM_B1 = 0.9
ADAM_B2 = 0.999
ADAM_EPS = 1e-08
ADAM_WD = 0.01
ADAM_STEP = 10

N_CHIPS = 4
MESH = pl.DeviceIdType.MESH
VMEM_LIMIT_BYTES = 56 * 1024 * 1024
ANY = pl.BlockSpec(memory_space=pl.ANY)


def _params(*semantics):
    return pltpu.CompilerParams(dimension_semantics=semantics, vmem_limit_bytes=VMEM_LIMIT_BYTES)


def _rms(x, g):
    return x * lax.rsqrt(jnp.mean(x * x, axis=-1, keepdims=True) + RMS_EPS) * g


def _rms_bwd(x, g, dy):
    rstd = lax.rsqrt(jnp.mean(x * x, axis=-1, keepdims=True) + RMS_EPS)
    xhat = x * rstd
    dxhat = dy * g
    dx = rstd * (dxhat - xhat * jnp.mean(dxhat * xhat, axis=-1, keepdims=True))
    return dx, dy * xhat


def _shift_down(x, s):
    return pltpu.roll(x, s, axis=0)


def _shift_up(x, s):
    return pltpu.roll(x, x.shape[0] - s, axis=0)


def _sigmoid(x):
    return 1.0 / (1.0 + jnp.exp(-x))


_DOT_DIMS = {"nn": ((1,), (0,)), "nt": ((1,), (1,)), "tn": ((0,), (0,))}


class _Hosted(NamedTuple):
    sources: list
    arrays: list
    new: list
    sems: list
    first: Callable
    last: Callable


def _call(body, hosted, *, name, grid, in_specs, out_specs, out_shape, operands, semantics, scratch_shapes=(),
          aliases=None):
    n_in, n_out, n_scr = len(in_specs), len(out_specs), len(scratch_shapes)
    aliases = dict(aliases or {})
    if hosted is None:
        def plain(*refs):
            body(refs[:n_in], refs[n_in:n_in + n_out], refs[n_in + n_out:], lambda: None, lambda: None)

        return pl.pallas_call(plain, name=name, grid=grid, in_specs=list(in_specs), out_specs=list(out_specs),
                              out_shape=list(out_shape), scratch_shapes=list(scratch_shapes),
                              input_output_aliases=aliases, compiler_params=_params(*semantics))(*operands)
    ns, na, nn = len(hosted.sources), len(hosted.arrays), len(hosted.new)

    def carrying(*refs):
        ins, src = refs[:n_in], refs[n_in:n_in + ns]
        o0 = n_in + ns + na
        outs, arr = refs[o0:o0 + n_out], refs[o0 + n_out:o0 + n_out + na]
        new = refs[o0 + n_out + na:o0 + n_out + na + nn]
        s0 = o0 + n_out + na + nn
        scratch, sems = refs[s0:s0 + n_scr], refs[s0 + n_scr:]
        ids = [pl.program_id(d) for d in range(len(grid))]
        is_first = functools.reduce(lambda p, q: p & q, [i == 0 for i in ids])
        is_last = functools.reduce(lambda p, q: p & q, [i == n - 1 for i, n in zip(ids, grid)])
        body(ins, outs, scratch, lambda: pl.when(is_first)(lambda: hosted.first(src, arr, new, sems)),
             lambda: pl.when(is_last)(lambda: hosted.last(src, arr, new, sems)))

    for t in range(na):
        aliases[n_in + ns + t] = n_out + t
    return pl.pallas_call(
        carrying, name=name, grid=grid, in_specs=list(in_specs) + [ANY] * (ns + na),
        out_specs=list(out_specs) + [ANY] * (na + nn),
        out_shape=list(out_shape) + [jax.ShapeDtypeStruct(a.shape, a.dtype) for a in hosted.arrays] + list(hosted.new),
        scratch_shapes=list(scratch_shapes) + list(hosted.sems), input_output_aliases=aliases,
        compiler_params=_params(*["arbitrary"] * len(grid)))(*operands, *hosted.sources, *hosted.arrays)


def _matmul(a, b, *, mode, name, grid, a_spec, b_spec, o_spec, out_shape, bias=None, bias_spec=None, residual=None,
            residual_spec=None, into=None, hosted=None):
    nk = grid[2]
    dims = (_DOT_DIMS[mode], ((), ()))
    acc_shape = tuple(d for d in o_spec.block_shape if d is not None)

    def body(ins, outs, scratch, first, last):
        a_ref, b_ref = ins[0], ins[1]
        pos = 2
        bias_ref = res_ref = None
        if bias is not None:
            bias_ref = ins[pos]
            pos += 1
        if residual is not None:
            res_ref = ins[pos]
        o_ref = outs[0]
        acc_ref = scratch[0] if nk > 1 else None
        first()
        prod = lax.dot_general(a_ref[...].astype(MXU_DTYPE), b_ref[...].astype(MXU_DTYPE), dims,
                               preferred_element_type=F32)

        def finish(acc):
            if bias_ref is not None:
                acc = acc + bias_ref[...]
            if res_ref is not None:
                acc = acc + res_ref[...]
            o_ref[...] = acc.astype(o_ref.dtype)

        if nk == 1:
            finish(prod)
        else:
            k = pl.program_id(2)

            @pl.when(k == 0)
            def _():
                acc_ref[...] = prod

            @pl.when(k > 0)
            def _():
                acc_ref[...] += prod

            @pl.when(k == nk - 1)
            def _():
                finish(acc_ref[...])

        last()

    operands, in_specs = [a, b], [a_spec, b_spec]
    if bias is not None:
        operands.append(bias)
        in_specs.append(bias_spec)
    if residual is not None:
        operands.append(residual)
        in_specs.append(residual_spec)
    aliases = {}
    if into is not None:
        aliases = {len(operands): 0}
        operands.append(into)
        in_specs.append(ANY)
    res = _call(body, hosted, name=name, grid=grid, in_specs=in_specs, out_specs=[o_spec], out_shape=[out_shape],
                operands=operands, semantics=("parallel", "parallel", "arbitrary"),
                scratch_shapes=[pltpu.VMEM(acc_shape, F32)] if nk > 1 else [], aliases=aliases)
    return res if hosted else res[0]


def _tile(n, want):
    t = min(n, want)
    assert n % t == 0, (n, want)
    return t


def _mm_nn(a, b, name, *, b_lead=None, bias=None, residual=None, out_dtype=F32, tm=1024, tn=1024, tk=1024,
           hosted=None):
    (m, k), n = a.shape, b.shape[-1]
    tm, tn, tk = _tile(m, tm), _tile(n, tn), _tile(k, tk)
    if b_lead is None:
        b_spec = pl.BlockSpec((tk, tn), lambda i, j, kk: (kk, j))
    else:
        b_spec = pl.BlockSpec((None, tk, tn), lambda i, j, kk: (b_lead, kk, j))
    return _matmul(
        a, b, mode="nn", name=name, grid=(m // tm, n // tn, k // tk),
        a_spec=pl.BlockSpec((tm, tk), lambda i, j, kk: (i, kk)), b_spec=b_spec,
        o_spec=pl.BlockSpec((tm, tn), lambda i, j, kk: (i, j)), out_shape=jax.ShapeDtypeStruct((m, n), out_dtype),
        bias=bias, bias_spec=pl.BlockSpec((1, tn), lambda i, j, kk: (0, j)),
        residual=residual, residual_spec=pl.BlockSpec((tm, tn), lambda i, j, kk: (i, j)), hosted=hosted)


def _mm_nt(a, b, name, *, b_lead=None, out_dtype=F32, tm=1024, tn=1024, tk=1024, hosted=None):
    (m, k), n = a.shape, b.shape[-2]
    tm, tn, tk = _tile(m, tm), _tile(n, tn), _tile(k, tk)
    if b_lead is None:
        b_spec = pl.BlockSpec((tn, tk), lambda i, j, kk: (j, kk))
    else:
        b_spec = pl.BlockSpec((None, tn, tk), lambda i, j, kk: (b_lead, j, kk))
    return _matmul(
        a, b, mode="nt", name=name, grid=(m // tm, n // tn, k // tk),
        a_spec=pl.BlockSpec((tm, tk), lambda i, j, kk: (i, kk)), b_spec=b_spec,
        o_spec=pl.BlockSpec((tm, tn), lambda i, j, kk: (i, j)), out_shape=jax.ShapeDtypeStruct((m, n), out_dtype),
        hosted=hosted)


def _mm_tn(a, b, name, *, lead=None, n_lead=None, into=None, tm=1024, tn=1024, tk=1024):
    (k, m), n = a.shape, b.shape[-1]
    tm, tn, tk = _tile(m, tm), _tile(n, tn), _tile(k, tk)
    if lead is None:
        o_spec = pl.BlockSpec((tm, tn), lambda i, j, kk: (i, j))
        out_shape = jax.ShapeDtypeStruct((m, n), F32)
    else:
        o_spec = pl.BlockSpec((None, tm, tn), lambda i, j, kk: (lead, i, j))
        out_shape = jax.ShapeDtypeStruct((n_lead, m, n), F32)
    return _matmul(
        a, b, mode="tn", name=name, grid=(m // tm, n // tn, k // tk),
        a_spec=pl.BlockSpec((tk, tm), lambda i, j, kk: (kk, i)),
        b_spec=pl.BlockSpec((tk, tn), lambda i, j, kk: (kk, j)), o_spec=o_spec, out_shape=out_shape, into=into)


ROW_TILE = 512


def _rms_fwd_call(h, g, name, out_dtype):
    t = h.shape[0]
    tt = _tile(t, ROW_TILE)

    def body(h_ref, g_ref, o_ref):
        o_ref[...] = _rms(h_ref[...], g_ref[...]).astype(o_ref.dtype)

    row = pl.BlockSpec((tt, D_MODEL), lambda i: (i, 0))
    vec = pl.BlockSpec((1, D_MODEL), lambda i: (0, 0))
    return pl.pallas_call(body, name=name, grid=(t // tt,), in_specs=[row, vec], out_specs=row,
                          out_shape=jax.ShapeDtypeStruct(h.shape, out_dtype), compiler_params=_params("parallel"))(h, g)


def _rms_bwd_call(h, g, dy, dres, name):
    t = h.shape[0]
    tt = _tile(t, ROW_TILE)

    def body(h_ref, g_ref, dy_ref, dres_ref, dx_ref, dx16_ref, dg_ref):
        dx, dgt = _rms_bwd(h_ref[...], g_ref[...], dy_ref[...].astype(F32))
        dx = dres_ref[...] + dx
        dx_ref[...] = dx
        dx16_ref[...] = dx.astype(dx16_ref.dtype)

        @pl.when(pl.program_id(0) == 0)
        def _():
            dg_ref[...] = jnp.zeros_like(dg_ref)

        dg_ref[...] += jnp.sum(dgt, axis=0, keepdims=True)

    row = pl.BlockSpec((tt, D_MODEL), lambda i: (i, 0))
    vec = pl.BlockSpec((1, D_MODEL), lambda i: (0, 0))
    return pl.pallas_call(
        body, name=name, grid=(t // tt,), in_specs=[row, vec, row, row], out_specs=[row, row, vec],
        out_shape=[jax.ShapeDtypeStruct(h.shape, F32), jax.ShapeDtypeStruct(h.shape, MXU_DTYPE),
                   jax.ShapeDtypeStruct((1, D_MODEL), F32)],
        compiler_params=_params("arbitrary"))(h, g, dy, dres)


def _loss_call(h, g, target):
    t = h.shape[0]
    tt = _tile(t, ROW_TILE)

    def body(h_ref, g_ref, t_ref, loss_ref, dh_ref, dh16_ref, dg_ref):
        x, gv = h_ref[...], g_ref[...]
        err = _rms(x, gv) - t_ref[...]
        dx, dgt = _rms_bwd(x, gv, err * (1.0 / D_MODEL))
        dh_ref[...] = dx
        dh16_ref[...] = dx.astype(dh16_ref.dtype)

        @pl.when(pl.program_id(0) == 0)
        def _():
            dg_ref[...] = jnp.zeros_like(dg_ref)
            loss_ref[...] = jnp.zeros_like(loss_ref)

        dg_ref[...] += jnp.sum(dgt, axis=0, keepdims=True)
        per_token = jnp.mean(err * err, axis=-1, keepdims=True)
        loss_ref[...] += 0.5 * jnp.sum(per_token, axis=0, keepdims=True)

    row = pl.BlockSpec((tt, D_MODEL), lambda i: (i, 0))
    vec = pl.BlockSpec((1, D_MODEL), lambda i: (0, 0))
    one = pl.BlockSpec((1, 1), lambda i: (0, 0))
    return pl.pallas_call(
        body, name="loss_head", grid=(t // tt,), in_specs=[row, vec, row], out_specs=[one, row, row, vec],
        out_shape=[jax.ShapeDtypeStruct((1, 1), F32), jax.ShapeDtypeStruct(h.shape, F32),
                   jax.ShapeDtypeStruct(h.shape, MXU_DTYPE), jax.ShapeDtypeStruct((1, D_MODEL), F32)],
        compiler_params=_params("arbitrary"))(h, g, target)


def _pool_windows_causal(ext, first_row):
    tt = ext.shape[0] - POOL_HALO
    t = first_row + lax.broadcasted_iota(jnp.int32, (tt, 1), 0)
    outs = []
    for gi, win in enumerate(POOL_WINDOWS):
        cols = slice(gi * POOL_GC, (gi + 1) * POOL_GC)
        s = ext[:, cols]
        sh = 1
        while sh < win:
            s = s + _shift_down(s, sh)
            sh *= 2
        count = jnp.minimum(t + 1, win).astype(F32)
        outs.append(s[POOL_HALO:] / count - ext[POOL_HALO:, cols])
    return outs


def _prev_halo_spec(tt, halo, width):
    per = tt // halo
    return pl.BlockSpec((halo, width), lambda i: (jnp.maximum(i * per - 1, 0), 0))


def _pool_fwd_call(h0, g_mix, pool_w, pool_b, pool_scale, g_ffn):
    t = h0.shape[0]
    tt = _tile(t, ROW_TILE)

    def body(h_ref, halo_ref, gm_ref, w_ref, b_ref, sc_ref, gf_ref, h1_ref, hn_ref):
        i = pl.program_id(0)
        x, gm = h_ref[...], gm_ref[...]
        top = jnp.where(i > 0, _rms(halo_ref[...], gm), 0.0)
        ext = jnp.concatenate([top, _rms(x, gm)], axis=0)
        ps = _pool_windows_causal(ext, i * tt)
        ys = [jnp.dot(p.astype(MXU_DTYPE), w_ref[gi], preferred_element_type=F32) for gi, p in enumerate(ps)]
        mix = (jnp.concatenate(ys, axis=1) + b_ref[...]) * sc_ref[...]
        h1 = x + mix
        h1_ref[...] = h1
        hn_ref[...] = _rms(h1, gf_ref[...]).astype(hn_ref.dtype)

    row = pl.BlockSpec((tt, D_MODEL), lambda i: (i, 0))
    vec = pl.BlockSpec((1, D_MODEL), lambda i: (0, 0))
    wsp = pl.BlockSpec((len(POOL_WINDOWS), POOL_GC, POOL_GC), lambda i: (0, 0, 0))
    return pl.pallas_call(
        body, name="pool_fwd", grid=(t // tt,),
        in_specs=[row, _prev_halo_spec(tt, POOL_HALO, D_MODEL), vec, wsp, vec, vec, vec], out_specs=[row, row],
        out_shape=[jax.ShapeDtypeStruct(h0.shape, F32), jax.ShapeDtypeStruct(h0.shape, MXU_DTYPE)],
        compiler_params=_params("parallel"))(h0, h0, g_mix, pool_w, pool_b, pool_scale, g_ffn)


def _pool_bwd_call(h0, dh1, g_mix, pool_w, pool_b, pool_scale, hosted=None):
    t = h0.shape[0]
    tt = _tile(t, ROW_TILE)
    nt = t // tt
    per = tt // POOL_HALO

    def body(ins, outs, scratch, first, last):
        h_ref, halo_ref, d_ref, dnext_ref, gm_ref, w_ref, b_ref, sc_ref = ins
        gx_ref, dw_ref, dv_ref = outs
        i = pl.program_id(0)
        first()
        x, gm, sc = h_ref[...], gm_ref[...], sc_ref[...]
        top = jnp.where(i > 0, _rms(halo_ref[...], gm), 0.0)
        ext = jnp.concatenate([top, _rms(x, gm)], axis=0)
        ps = _pool_windows_causal(ext, i * tt)

        dy = d_ref[...]
        dy_ext = jnp.concatenate([dy, jnp.where(i < nt - 1, dnext_ref[...], 0.0)], axis=0)
        dpre_ext = dy_ext * sc
        t_ext = i * tt + lax.broadcasted_iota(jnp.int32, (tt + POOL_HALO, 1), 0)

        @pl.when(i == 0)
        def _():
            dw_ref[...] = jnp.zeros_like(dw_ref)
            dv_ref[...] = jnp.zeros_like(dv_ref)

        dhn, ypre = [], []
        for gi, win in enumerate(POOL_WINDOWS):
            cols = slice(gi * POOL_GC, (gi + 1) * POOL_GC)
            w = w_ref[gi]
            p16 = ps[gi].astype(MXU_DTYPE)
            dpre16 = dpre_ext[:, cols].astype(MXU_DTYPE)
            ypre.append(jnp.dot(p16, w, preferred_element_type=F32))
            dw_ref[gi] += lax.dot_general(p16, dpre16[:tt], (((0,), (0,)), ((), ())), preferred_element_type=F32)
            dp_ext = lax.dot_general(dpre16, w, (((1,), (1,)), ((), ())), preferred_element_type=F32)
            s = dp_ext / jnp.minimum(t_ext + 1, win).astype(F32)
            sh = 1
            while sh < win:
                s = s + _shift_up(s, sh)
                sh *= 2
            dhn.append(s[:tt] - dp_ext[:tt])
        dhn = jnp.concatenate(dhn, axis=1)
        ypre = jnp.concatenate(ypre, axis=1) + b_ref[...]
        dx, dgt = _rms_bwd(x, gm, dhn)
        gx_ref[...] = dy + dx
        dv_ref[0:1, :] += jnp.sum(dpre_ext[:tt], axis=0, keepdims=True)
        dv_ref[1:2, :] += jnp.sum(dy * ypre, axis=0, keepdims=True)
        dv_ref[2:3, :] += jnp.sum(dgt, axis=0, keepdims=True)
        last()

    row = pl.BlockSpec((tt, D_MODEL), lambda i: (i, 0))
    vec = pl.BlockSpec((1, D_MODEL), lambda i: (0, 0))
    nxt = pl.BlockSpec((POOL_HALO, D_MODEL), lambda i: (jnp.minimum((i + 1) * per, t // POOL_HALO - 1), 0))
    wsp = pl.BlockSpec((len(POOL_WINDOWS), POOL_GC, POOL_GC), lambda i: (0, 0, 0))
    return _call(
        body, hosted, name="pool_bwd", grid=(nt,),
        in_specs=[row, _prev_halo_spec(tt, POOL_HALO, D_MODEL), row, nxt, vec, wsp, vec, vec],
        out_specs=[row, wsp, pl.BlockSpec((8, D_MODEL), lambda i: (0, 0))],
        out_shape=[jax.ShapeDtypeStruct(h0.shape, F32),
                   jax.ShapeDtypeStruct((len(POOL_WINDOWS), POOL_GC, POOL_GC), F32),
                   jax.ShapeDtypeStruct((8, D_MODEL), F32)],
        operands=(h0, h0, dh1, dh1, g_mix, pool_w, pool_b, pool_scale), semantics=("arbitrary",))


CONV_ROWS = 256
CONV_SUB = 64
LANES = 128


def _causal_conv(u_ext, w, b):
    return w[0:1] * _shift_down(u_ext, 2) + w[1:2] * _shift_down(u_ext, 1) + w[2:3] * u_ext + b


def _fold8(x):
    acc = x[0:8]
    for r in range(8, x.shape[0], 8):
        acc = acc + x[r:r + 8]
    return acc


def _conv_glu_fwd_call(u, conv_w, conv_b, name, hosted=None):
    _, t, f = u.shape
    tt = _tile(t, CONV_ROWS)
    sub = _tile(tt, CONV_SUB)
    per = tt // CONV_HALO

    def body(ins, outs, scratch, first, last):
        u_ref, up_ref, w_ref, b_ref = ins
        z_ref, ext_ref = outs[0], scratch[0]
        i = pl.program_id(0)
        first()
        ext_ref[:, :CONV_HALO, :] = jnp.where(i > 0, up_ref[...], 0.0)
        ext_ref[:, CONV_HALO:, :] = u_ref[...]

        def chunk(j, carry):
            lanes = pl.ds(pl.multiple_of(j * LANES, LANES), LANES)
            w, b = w_ref[:, :, lanes], b_ref[:, lanes]
            for s in range(tt // sub):
                rows = pl.ds(s * sub, sub + CONV_HALO)
                c = [_causal_conv(ext_ref[hf, rows, lanes], w[:, hf, :], b[hf:hf + 1, :])[CONV_HALO:] for hf in range(2)]
                z_ref[pl.ds(s * sub, sub), lanes] = (c[0] * c[1] * _sigmoid(c[1])).astype(z_ref.dtype)
            return carry

        lax.fori_loop(0, f // LANES, chunk, 0)
        last()

    res = _call(
        body, hosted, name=name, grid=(t // tt,),
        in_specs=[pl.BlockSpec((2, tt, f), lambda i: (0, i, 0)),
                  pl.BlockSpec((2, CONV_HALO, f), lambda i: (0, jnp.maximum(i * per - 1, 0), 0)),
                  pl.BlockSpec((3, 2, f), lambda i: (0, 0, 0)), pl.BlockSpec((2, f), lambda i: (0, 0))],
        out_specs=[pl.BlockSpec((tt, f), lambda i: (i, 0))], out_shape=[jax.ShapeDtypeStruct((t, f), MXU_DTYPE)],
        operands=(u, u, conv_w, conv_b), semantics=("parallel",),
        scratch_shapes=[pltpu.VMEM((2, CONV_HALO + tt, f), F32)])
    return res if hosted else res[0]


def _conv_glu_bwd_call(u, dz, conv_w, conv_b, name):
    _, t, f = u.shape
    tt = _tile(t, CONV_ROWS)
    sub = _tile(tt, CONV_SUB)
    nt = t // tt
    per = tt // CONV_HALO
    halo = CONV_HALO

    def body(u_ref, up_ref, un_ref, dz_ref, dzn_ref, w_ref, b_ref, du_ref, dw_ref, db_ref, uext, dzext, dwacc, dbacc):
        i = pl.program_id(0)
        uext[:, :halo, :] = jnp.where(i > 0, up_ref[...], 0.0)
        uext[:, halo:halo + tt, :] = u_ref[...]
        uext[:, halo + tt:, :] = un_ref[...]
        dzext[:halo, :] = jnp.zeros((halo, f), F32)
        dzext[halo:halo + tt, :] = dz_ref[...].astype(F32)
        dzext[halo + tt:, :] = jnp.where(i < nt - 1, dzn_ref[...].astype(F32), 0.0)

        @pl.when(i == 0)
        def _():
            dwacc[...] = jnp.zeros_like(dwacc)
            dbacc[...] = jnp.zeros_like(dbacc)

        main = slice(halo, halo + sub)

        def chunk(j, carry):
            lanes = pl.ds(pl.multiple_of(j * LANES, LANES), LANES)
            w, b = w_ref[:, :, lanes], b_ref[:, lanes]
            for s in range(tt // sub):
                rows = pl.ds(s * sub, sub + 2 * halo)
                ue = [uext[hf, rows, lanes] for hf in range(2)]
                a, g = [_causal_conv(ue[hf], w[:, hf, :], b[hf:hf + 1, :]) for hf in range(2)]
                sg = _sigmoid(g)
                dzs = dzext[rows, lanes]
                dc = [dzs * g * sg, dzs * a * sg * (1.0 + g * (1.0 - sg))]
                for hf in range(2):
                    d = dc[hf]
                    d1, d2 = _shift_up(d, 1), _shift_up(d, 2)
                    du = w[2:3, hf, :] * d + w[1:2, hf, :] * d1 + w[0:1, hf, :] * d2
                    du_ref[hf, pl.ds(s * sub, sub), lanes] = du[main].astype(du_ref.dtype)
                    um = ue[hf][main]
                    for k, dd in enumerate((d2, d1, d)):
                        dwacc[k, hf, :, lanes] += _fold8(dd[main] * um)
                    dbacc[hf, :, lanes] += _fold8(d[main])
            return carry

        lax.fori_loop(0, f // LANES, chunk, 0)

        @pl.when(i == nt - 1)
        def _():
            dw_ref[...] = jnp.sum(dwacc[...], axis=2)
            db_ref[...] = jnp.sum(dbacc[...], axis=1)

    last = t // halo - 1
    return pl.pallas_call(
        body, name=name, grid=(nt,),
        in_specs=[pl.BlockSpec((2, tt, f), lambda i: (0, i, 0)),
                  pl.BlockSpec((2, halo, f), lambda i: (0, jnp.maximum(i * per - 1, 0), 0)),
                  pl.BlockSpec((2, halo, f), lambda i: (0, jnp.minimum((i + 1) * per, last), 0)),
                  pl.BlockSpec((tt, f), lambda i: (i, 0)),
                  pl.BlockSpec((halo, f), lambda i: (jnp.minimum((i + 1) * per, last), 0)),
                  pl.BlockSpec((3, 2, f), lambda i: (0, 0, 0)), pl.BlockSpec((2, f), lambda i: (0, 0))],
        out_specs=[pl.BlockSpec((2, tt, f), lambda i: (0, i, 0)),
                   pl.BlockSpec((3, 2, f), lambda i: (0, 0, 0)), pl.BlockSpec((2, f), lambda i: (0, 0))],
        out_shape=[jax.ShapeDtypeStruct(u.shape, MXU_DTYPE), jax.ShapeDtypeStruct((3, 2, f), F32),
                   jax.ShapeDtypeStruct((2, f), F32)],
        scratch_shapes=[pltpu.VMEM((2, tt + 2 * halo, f), F32), pltpu.VMEM((tt + 2 * halo, f), F32),
                        pltpu.VMEM((3, 2, 8, f), F32), pltpu.VMEM((2, 8, f), F32)],
        compiler_params=_params("arbitrary"))(u, u, u, dz, dz, conv_w, conv_b)


def _kv_even_odd(band, k):
    pair, parity = divmod(k, 2)
    blk = band[:, 128 * pair:128 * (pair + 1)].astype(F32)
    lane = lax.broadcasted_iota(jnp.int32, blk.shape, 1)
    own = jnp.where((lane >= HEAD_DIM) == (parity == 1), blk, 0.0)
    swapped = pltpu.roll(own, HEAD_DIM, axis=1)
    even, odd = (own, swapped) if parity == 0 else (swapped, own)
    return even.astype(MXU_DTYPE), odd.astype(MXU_DTYPE)


def _stack_pairs(ref, k):
    return jnp.concatenate([ref[:, 256 * k:256 * k + 128], ref[:, 256 * k + 128:256 * k + 256]], axis=0)


_GROUP_ORDER = (0, 2, 1, 3)


def _attn_fill_tables(bias_ref):
    shape = (4 * ATT_BLOCK, 2 * ATT_BLOCK)
    row = lax.broadcasted_iota(jnp.int32, shape, 0)
    col = lax.broadcasted_iota(jnp.int32, shape, 1)
    dist = (row & (ATT_BLOCK - 1)) + ATT_BLOCK - col
    ok = (dist >= 0) & (dist < ATT_BLOCK)
    distf = dist.astype(F32)
    rb = lax.broadcasted_iota(jnp.int32, (shape[0], 1), 0) // ATT_BLOCK
    for k in range(N_KV_HEADS):
        slope = jnp.zeros((shape[0], 1), F32)
        for r, gq in enumerate(_GROUP_ORDER):
            slope = jnp.where(rb == r, ALIBI_SLOPES[4 * k + gq], slope)
        bias = jnp.where(ok, -slope * distf, -jnp.inf)
        bias_ref[0, k] = bias
        bias_ref[1, k] = jnp.where(col >= ATT_BLOCK, bias, -jnp.inf)


def _attn_probs(qs, k_even, k_odd, bias, k, sink_ref):
    nt_dims = (((1,), (1,)), ((), ()))
    s = jnp.concatenate([lax.dot_general(qs, k_even, nt_dims, preferred_element_type=F32),
                         lax.dot_general(qs, k_odd, nt_dims, preferred_element_type=F32)], axis=0) + bias
    rb = lax.broadcasted_iota(jnp.int32, (s.shape[0], 1), 0) // ATT_BLOCK
    sink = jnp.zeros((s.shape[0], 1), F32)
    for r, gq in enumerate(_GROUP_ORDER):
        sink = jnp.where(rb == r, sink_ref[4 * k + gq], sink)
    m = jnp.maximum(jnp.max(s, axis=-1, keepdims=True), sink)
    p = jnp.exp(s - m)
    es = jnp.exp(sink - m)
    return p, 1.0 / (jnp.sum(p, axis=-1, keepdims=True) + es), es


_ATTN_TABLES = [pltpu.VMEM((2, N_KV_HEADS, 4 * ATT_BLOCK, 2 * ATT_BLOCK), F32)]


def _attn_specs(with_do):
    prev = lambda n: jnp.maximum(n - 1, 0)
    specs = [pl.BlockSpec(memory_space=pltpu.SMEM),
             pl.BlockSpec((ATT_BLOCK, D_MODEL), lambda n: (n, 0)),
             pl.BlockSpec((ATT_BLOCK, 256), lambda n: (prev(n), 4)), pl.BlockSpec((ATT_BLOCK, 256), lambda n: (n, 4)),
             pl.BlockSpec((ATT_BLOCK, 256), lambda n: (prev(n), 5)), pl.BlockSpec((ATT_BLOCK, 256), lambda n: (n, 5))]
    if with_do:
        specs.append(pl.BlockSpec((ATT_BLOCK, D_MODEL), lambda n: (n, 0)))
    return specs


def _attn_fwd_call(qkv, sinks):
    t = qkv.shape[0]

    def body(sink_ref, q_ref, kp_ref, kc_ref, vp_ref, vc_ref, o_ref, bias_ref):
        n = pl.program_id(0)

        @pl.when(n == 0)
        def _():
            _attn_fill_tables(bias_ref)

        first = 1 - jnp.minimum(n, 1)
        kband = jnp.concatenate([kp_ref[...], kc_ref[...]], axis=0)
        vband = jnp.concatenate([vp_ref[...], vc_ref[...]], axis=0)
        for k in range(N_KV_HEADS):
            k_even, k_odd = _kv_even_odd(kband, k)
            v_even, v_odd = _kv_even_odd(vband, k)
            qs = _stack_pairs(q_ref, k) * ATT_SCALE
            p, rdenom, _ = _attn_probs(qs, k_even, k_odd, bias_ref[first, k], k, sink_ref)
            probs = (p * rdenom).astype(MXU_DTYPE)
            o = (jnp.dot(probs[:256], v_even, preferred_element_type=F32)
                 + jnp.dot(probs[256:], v_odd, preferred_element_type=F32))
            o_ref[:, 256 * k:256 * k + 128] = o[:128].astype(o_ref.dtype)
            o_ref[:, 256 * k + 128:256 * k + 256] = o[128:].astype(o_ref.dtype)

    return pl.pallas_call(
        body, name="attn_fwd", grid=(t // ATT_BLOCK,), in_specs=_attn_specs(False),
        out_specs=pl.BlockSpec((ATT_BLOCK, D_MODEL), lambda n: (n, 0)),
        out_shape=jax.ShapeDtypeStruct((t, D_MODEL), MXU_DTYPE), scratch_shapes=_ATTN_TABLES,
        compiler_params=_params("arbitrary"))(sinks, qkv, qkv, qkv, qkv, qkv)


def _attn_bwd_call(qkv, sinks, do, hosted=None):
    t = qkv.shape[0]
    nb = t // ATT_BLOCK
    tn_dims = (((0,), (0,)), ((), ()))
    nt_dims = (((1,), (1,)), ((), ()))

    def to_native(even_part, odd_part, parity):
        lane = lax.broadcasted_iota(jnp.int32, even_part.shape, 1)
        lo = lane < HEAD_DIM
        e, o = jnp.where(lo, even_part, 0.0), jnp.where(lo, 0.0, odd_part)
        if parity == 0:
            return e + pltpu.roll(o, HEAD_DIM, axis=1)
        return pltpu.roll(e, HEAD_DIM, axis=1) + o

    def body(ins, outs, scratch, first, last):
        sink_ref, q_ref, kp_ref, kc_ref, vp_ref, vc_ref, do_ref = ins
        dq_ref, dk_ref, dv_ref, db_ref, dsink_ref = outs
        bias_ref = scratch[0]
        n = pl.program_id(0)
        first()

        @pl.when(n == 0)
        def _():
            _attn_fill_tables(bias_ref)
            dk_ref[...] = jnp.zeros_like(dk_ref)
            dv_ref[...] = jnp.zeros_like(dv_ref)
            db_ref[...] = jnp.zeros_like(db_ref)
            dsink_ref[...] = jnp.zeros_like(dsink_ref)

        kband = jnp.concatenate([kp_ref[...], kc_ref[...]], axis=0)
        vband = jnp.concatenate([vp_ref[...], vc_ref[...]], axis=0)
        dk_pairs = [jnp.zeros((2 * ATT_BLOCK, 128), F32), jnp.zeros((2 * ATT_BLOCK, 128), F32)]
        dv_pairs = [jnp.zeros((2 * ATT_BLOCK, 128), F32), jnp.zeros((2 * ATT_BLOCK, 128), F32)]
        sink_lane = lax.broadcasted_iota(jnp.int32, (8, 128), 1)
        sink_row = lax.broadcasted_iota(jnp.int32, (8, 128), 0)
        dsink = jnp.zeros((8, 128), F32)
        first = 1 - jnp.minimum(n, 1)
        for k in range(N_KV_HEADS):
            k_even, k_odd = _kv_even_odd(kband, k)
            v_even, v_odd = _kv_even_odd(vband, k)
            qs = _stack_pairs(q_ref, k) * ATT_SCALE
            dos = _stack_pairs(do_ref, k)
            p, rdenom, es = _attn_probs(qs, k_even, k_odd, bias_ref[first, k], k, sink_ref)
            probs = p * rdenom
            dp = jnp.concatenate([lax.dot_general(dos, v_even, nt_dims, preferred_element_type=F32),
                                  lax.dot_general(dos, v_odd, nt_dims, preferred_element_type=F32)], axis=0)
            delta = jnp.sum(probs * dp, axis=-1, keepdims=True)
            ds16 = (probs * (dp - delta)).astype(MXU_DTYPE)
            p16 = probs.astype(MXU_DTYPE)
            dsink_rows = -(es * rdenom) * delta
            for r, gq in enumerate(_GROUP_ORDER):
                tot = jnp.sum(dsink_rows[ATT_BLOCK * r:ATT_BLOCK * (r + 1)], axis=0, keepdims=True)
                dsink = dsink + jnp.where((sink_lane == 4 * k + gq) & (sink_row == 0), tot, 0.0)
            dqs = (jnp.dot(ds16[:256], k_even, preferred_element_type=F32)
                   + jnp.dot(ds16[256:], k_odd, preferred_element_type=F32)) * ATT_SCALE
            dq_ref[:, 256 * k:256 * k + 128] = dqs[:128].astype(dq_ref.dtype)
            dq_ref[:, 256 * k + 128:256 * k + 256] = dqs[128:].astype(dq_ref.dtype)
            db_ref[:, 256 * k:256 * k + 128] += jnp.sum(dqs[:128], axis=0, keepdims=True)
            db_ref[:, 256 * k + 128:256 * k + 256] += jnp.sum(dqs[128:], axis=0, keepdims=True)
            pair, parity = divmod(k, 2)
            dk_pairs[pair] = dk_pairs[pair] + to_native(
                lax.dot_general(ds16[:256], qs, tn_dims, preferred_element_type=F32),
                lax.dot_general(ds16[256:], qs, tn_dims, preferred_element_type=F32), parity)
            dv_pairs[pair] = dv_pairs[pair] + to_native(
                lax.dot_general(p16[:256], dos, tn_dims, preferred_element_type=F32),
                lax.dot_general(p16[256:], dos, tn_dims, preferred_element_type=F32), parity)
        dsink_ref[...] += dsink
        dk_band = jnp.concatenate(dk_pairs, axis=1)
        dv_band = jnp.concatenate(dv_pairs, axis=1)
        cur = pl.multiple_of(n * ATT_BLOCK, ATT_BLOCK)
        dk_ref[pl.ds(cur, ATT_BLOCK), :] += dk_band[ATT_BLOCK:]
        dv_ref[pl.ds(cur, ATT_BLOCK), :] += dv_band[ATT_BLOCK:]

        @pl.when(n > 0)
        def _():
            prv = pl.multiple_of((n - 1) * ATT_BLOCK, ATT_BLOCK)
            dk_ref[pl.ds(prv, ATT_BLOCK), :] += dk_band[:ATT_BLOCK]
            dv_ref[pl.ds(prv, ATT_BLOCK), :] += dv_band[:ATT_BLOCK]

        @pl.when(n == nb - 1)
        def _():
            db_ref[:, 1024:1280] = jnp.sum(dk_ref[...], axis=0, keepdims=True)
            db_ref[:, 1280:1536] = jnp.sum(dv_ref[...], axis=0, keepdims=True)

        last()

    whole = lambda shape: pl.BlockSpec(shape, lambda n: (0, 0))
    return _call(
        body, hosted, name="attn_bwd", grid=(nb,), in_specs=_attn_specs(True),
        out_specs=[pl.BlockSpec((ATT_BLOCK, D_MODEL), lambda n: (n, 0)), whole((t, 256)), whole((t, 256)),
                   whole((1, QKV_DIM)), whole((8, 128))],
        out_shape=[jax.ShapeDtypeStruct((t, D_MODEL), MXU_DTYPE), jax.ShapeDtypeStruct((t, 256), F32),
                   jax.ShapeDtypeStruct((t, 256), F32), jax.ShapeDtypeStruct((1, QKV_DIM), F32),
                   jax.ShapeDtypeStruct((8, 128), F32)],
        operands=(sinks, qkv, qkv, qkv, qkv, qkv, do), semantics=("arbitrary",), scratch_shapes=_ATTN_TABLES)


FF_TILE = 1408


def _ffn_fwd(hn, h, w, conv_w, conv_b, layer, host=None):
    t = hn.shape[0]
    tm = _tile(t, 512)
    halves = D_FF // FF_TILE

    def carry(stage, call):
        if host is None:
            return call(None), w
        res = call(host(stage, w))
        return res[0], {**w, **dict(zip(_late_names(stage), res[1:]))}

    u, w = carry("in", lambda hosted: _matmul(
        hn, w["w_in"], mode="nn", name=f"ffn{layer}_in", grid=(2 * halves, t // tm, 1),
        a_spec=pl.BlockSpec((tm, D_MODEL), lambda j, i, k: (i, 0)),
        b_spec=pl.BlockSpec((None, D_MODEL, FF_TILE), lambda j, i, k: (layer, 0, j)),
        o_spec=pl.BlockSpec((None, tm, FF_TILE), lambda j, i, k: (j // halves, i, j % halves)),
        out_shape=jax.ShapeDtypeStruct((2, t, D_FF), F32), hosted=hosted))
    z, w = carry("glu", lambda hosted: _conv_glu_fwd_call(u, conv_w, conv_b, f"ffn{layer}_glu", hosted=hosted))
    h_out, w = carry("out", lambda hosted: _mm_nn(z, w["w_out"], f"ffn{layer}_out", b_lead=layer, residual=h,
                                                  tk=D_FF, hosted=hosted))
    return h_out, u, z, w


def _ffn_bwd(dh, hn, u, z, w_in, w_out, conv_w, conv_b, layer, g_in_acc, g_out_acc, host_dz=None, host_dhn=None):
    t = hn.shape[0]
    dz = _mm_nt(dh, w_out, f"ffn{layer}_dz", b_lead=layer, out_dtype=MXU_DTYPE, tn=FF_TILE, hosted=host_dz)
    dz, from_dz = (dz[0], dz[1:]) if host_dz else (dz, [])
    g_out = _mm_tn(z, dh, f"ffn{layer}_dwout", lead=layer, n_lead=2, into=g_out_acc, tm=WGRAD_TILE, tk=t)
    du, dcw, dcb = _conv_glu_bwd_call(u, dz, conv_w, conv_b, f"ffn{layer}_dglu")
    per_half = D_FF // WGRAD_TILE
    g_in = _matmul(
        hn, du, mode="tn", name=f"ffn{layer}_dwin", grid=(1, 2 * per_half, 1),
        a_spec=pl.BlockSpec((t, D_MODEL), lambda i, j, k: (0, 0)),
        b_spec=pl.BlockSpec((None, t, WGRAD_TILE), lambda i, j, k: (j // per_half, 0, j % per_half)),
        o_spec=pl.BlockSpec((None, D_MODEL, WGRAD_TILE), lambda i, j, k: (layer, 0, j)),
        out_shape=jax.ShapeDtypeStruct((2, D_MODEL, 2 * D_FF), F32), into=g_in_acc)
    hosted = host_dhn(g_in, g_out) if host_dhn else None
    dhn = _ffn_dhn_call(du, w_in, layer, hosted)
    dhn, from_dhn = (dhn[0], dhn[1:]) if hosted else (dhn[0], [])
    return dhn, g_in, g_out, dcw, dcb, from_dz, from_dhn


WGRAD_TILE = 256


def _ffn_dhn_call(du, w_in, layer, hosted=None):
    _, t, f = du.shape
    tm = _tile(t, 512)
    nt_dims = (((1,), (1,)), ((), ()))

    def body(ins, outs, scratch, first, last):
        du_ref, w_ref = ins
        first()
        outs[0][...] = (lax.dot_general(du_ref[0], w_ref[:, :f], nt_dims, preferred_element_type=F32)
                        + lax.dot_general(du_ref[1], w_ref[:, f:], nt_dims, preferred_element_type=F32))
        last()

    return _call(
        body, hosted, name=f"ffn{layer}_dhn", grid=(t // tm,),
        in_specs=[pl.BlockSpec((2, tm, f), lambda i: (0, i, 0)),
                  pl.BlockSpec((None, D_MODEL, 2 * f), lambda i: (layer, 0, 0))],
        out_specs=[pl.BlockSpec((tm, D_MODEL), lambda i: (i, 0))], out_shape=[jax.ShapeDtypeStruct((t, D_MODEL), F32)],
        operands=(du, w_in), semantics=("parallel",))


def _local_step(x, target, w, *, norm_mix, norm_ffn, norm_f, pool_b, pool_scale, b_qkv, sinks, conv_w, conv_b,
                late_gather=None, reducer=None):
    cw = [conv_w[l].reshape(3, 2, D_FF) for l in range(2)]
    cb = [conv_b[l].reshape(2, D_FF) for l in range(2)]
    gm = [norm_mix[l:l + 1] for l in range(2)]
    gf = [norm_ffn[l:l + 1] for l in range(2)]

    h1, hn_f0 = _pool_fwd_call(x, gm[0], w["pool_w"], pool_b, pool_scale, gf[0])
    h2, u0, z0, w = _ffn_fwd(hn_f0, h1, w, cw[0], cb[0], 0, host=late_gather)
    hn_a = _rms_fwd_call(h2, gm[1], "rms_attn", MXU_DTYPE)
    qkv = _mm_nn(hn_a, w["w_qkv"], "attn_qkv", bias=b_qkv, out_dtype=MXU_DTYPE, tn=768)
    o = _attn_fwd_call(qkv, sinks)
    h3 = _mm_nn(o, w["w_o"], "attn_out", residual=h2)
    hn_f1 = _rms_fwd_call(h3, gf[1], "rms_ffn1", MXU_DTYPE)
    h4, u1, z1, w = _ffn_fwd(hn_f1, h3, w, cw[1], cb[1], 1)
    loss, dh, dh16, d_norm_f = _loss_call(h4, norm_f, target)

    rd = reducer
    first, rest = _Reducer.FIRST, _Reducer.REST
    dhn, g_in, g_out, dcw1, dcb1, _, slabs = _ffn_bwd(
        dh16, hn_f1, u1, z1, w["w_in"], w["w_out"], cw[1], cb[1], 1, None, None,
        host_dhn=(lambda gi, go: rd.swap(first, {"w_in": gi, "w_out": go})) if rd else None)
    if rd:
        rd.pair_sums(first, slabs)
    dh, dh16, d_gf1 = _rms_bwd_call(h3, gf[1], dhn, dh, "rms_ffn1_bwd")
    do = _mm_nt(dh16, w["w_o"], "attn_do", out_dtype=MXU_DTYPE)
    g_o = _mm_tn(o, dh16, "attn_dwo")
    dq, dk, dv, d_bqkv, d_sinks, *parts = _attn_bwd_call(qkv, sinks, do, hosted=rd.scatter(first) if rd else None)
    if rd:
        rd.chip_sums(first, parts)
    dqkv = jnp.concatenate([dq, dk.astype(MXU_DTYPE), dv.astype(MXU_DTYPE)], axis=1)
    g_qkv = _mm_tn(hn_a, dqkv, "attn_dwqkv", tn=768)
    dhn = _mm_nt(dqkv, w["w_qkv"], "attn_dhn", tk=768)
    dh, dh16, d_gm1 = _rms_bwd_call(h2, gm[1], dhn, dh, "rms_attn_bwd")
    g_o4 = g_o.reshape(N_CHIPS, 2, W_O_ROWS // 2, D_MODEL)
    dhn, g_in, g_out, dcw0, dcb0, updated, slabs = _ffn_bwd(
        dh16, hn_f0, u0, z0, w["w_in"], w["w_out"], cw[0], cb[0], 0, g_in, g_out,
        host_dz=rd.share(first) if rd else None,
        host_dhn=(lambda gi, go: rd.swap(rest, {"w_in": gi, "w_out": go, "w_qkv": g_qkv, "w_o": g_o4})) if rd else None)
    if rd:
        rd.shared(first, updated)
        rd.pair_sums(rest, slabs)
    dh, _, d_gf0 = _rms_bwd_call(h1, gf[0], dhn, dh, "rms_ffn0_bwd")
    grad_x, g_pool, dvec, *parts = _pool_bwd_call(x, dh, gm[0], w["pool_w"], pool_b, pool_scale,
                                                  hosted=rd.scatter(rest) if rd else None)
    if rd:
        rd.chip_sums(rest, parts)
        rd.shared(rest, _alone(rd.share(rest), "share_rest"))

    grads = {
        "w_in": g_in, "w_out": g_out, "w_qkv": g_qkv, "w_o": g_o, "pool_w": g_pool,
        "pool_b": dvec[0:1], "pool_scale": dvec[1:2], "b_qkv": d_bqkv, "sinks": d_sinks[0:1, :N_HEADS],
        "norm_mix": jnp.concatenate([dvec[2:3], d_gm1], axis=0), "norm_ffn": jnp.concatenate([d_gf0, d_gf1], axis=0),
        "conv_w": jnp.stack([dcw0.reshape(3, 2 * D_FF), dcw1.reshape(3, 2 * D_FF)]),
        "conv_b": jnp.stack([dcb0.reshape(2 * D_FF), dcb1.reshape(2 * D_FF)]), "norm_f": d_norm_f,
    }
    return loss, grad_x, grads


W_IN_COLS = 2 * D_FF // N_CHIPS
W_OUT_ROWS = D_FF // N_CHIPS
QKV_COLS = QKV_DIM // N_CHIPS
W_O_ROWS = D_MODEL // N_CHIPS
POOL_ROWS = POOL_GC // N_CHIPS
BIG = ("w_in", "w_out", "w_qkv", "w_o", "pool_w")


def _mesh_place():
    x, y, c = lax.axis_index("x"), lax.axis_index("y"), lax.axis_index("c")
    chips = [(1 - x, y), (x, 1 - y), (1 - x, 1 - y)]
    return x, y, c, chips


def _col(k, width):
    return pl.ds(pl.multiple_of(k * width, 128), width)


def _row(k, height):
    return pl.ds(pl.multiple_of(k * height, 16), height)


def _full_piece(name, ref, k, h):
    if name == "w_in":
        return ref.at[h, :, _col(k, W_IN_COLS)]
    if name == "w_out":
        return ref.at[h, _row(k, W_OUT_ROWS), :]
    if name == "w_qkv":
        return ref.at[_row(h, D_MODEL // 2), _col(k, QKV_COLS)]
    if name == "w_o":
        return ref.at[_row(2 * k + h, W_O_ROWS // 2), :]
    return ref.at[pl.ds(2 * h, 2), _row(k, POOL_ROWS), :]


def _shard_half(name, ref, h):
    if name in ("w_in", "w_out"):
        return ref.at[h]
    if name == "w_qkv":
        return ref.at[_row(h, D_MODEL // 2), :]
    if name == "w_o":
        return ref.at[_row(h, W_O_ROWS // 2), :]
    return ref.at[pl.ds(2 * h, 2)]


FULL_SHAPES = {"w_in": (2, D_MODEL, 2 * D_FF), "w_out": (2, D_FF, D_MODEL), "w_qkv": (D_MODEL, QKV_DIM),
               "w_o": (D_MODEL, D_MODEL), "pool_w": (4, POOL_GC, POOL_GC)}
SHARD_SHAPES = {"w_in": (2, D_MODEL, W_IN_COLS), "w_out": (2, W_OUT_ROWS, D_MODEL), "w_qkv": (D_MODEL, QKV_COLS),
                "w_o": (W_O_ROWS, D_MODEL), "pool_w": (4, POOL_ROWS, POOL_GC)}
HALF_SHAPES = {"w_in": (D_MODEL, W_IN_COLS), "w_out": (W_OUT_ROWS, D_MODEL), "w_qkv": (D_MODEL // 2, QKV_COLS),
               "w_o": (W_O_ROWS // 2, D_MODEL), "pool_w": (2, POOL_ROWS, POOL_GC)}


def _place_shard(name, shard, me_arr):
    if name == "w_in":
        blk, grid = (1, 256, W_IN_COLS), (2, D_MODEL // 256)
        src, dst = (lambda a, b, me: (a, b, 0)), (lambda a, b, me: (a, b, me[0]))
    elif name == "w_out":
        blk, grid = (1, W_OUT_ROWS, D_MODEL), (2, 1)
        src, dst = (lambda a, b, me: (a, 0, 0)), (lambda a, b, me: (a, me[0], 0))
    elif name == "w_qkv":
        blk, grid = (256, QKV_COLS), (1, D_MODEL // 256)
        src, dst = (lambda a, b, me: (b, 0)), (lambda a, b, me: (b, me[0]))
    elif name == "w_o":
        blk, grid = (W_O_ROWS, D_MODEL), (1, 1)
        src, dst = (lambda a, b, me: (0, 0)), (lambda a, b, me: (me[0], 0))
    else:
        blk, grid = (4, POOL_ROWS, POOL_GC), (1, 1)
        src, dst = (lambda a, b, me: (0, 0, 0)), (lambda a, b, me: (0, me[0], 0))

    def body(me_ref, s_ref, o_ref):
        o_ref[...] = s_ref[...].astype(o_ref.dtype)

    return pl.pallas_call(
        body, name=f"place_{name}",
        grid_spec=pltpu.PrefetchScalarGridSpec(num_scalar_prefetch=1, grid=grid, in_specs=[pl.BlockSpec(blk, src)],
                                               out_specs=pl.BlockSpec(blk, dst)),
        out_shape=jax.ShapeDtypeStruct(FULL_SHAPES[name], WIRE_DTYPE), compiler_params=_params("parallel", "parallel"),
    )(me_arr, shard)


def _when_mover(half, c, fn):
    if half is None:
        fn(c)
    else:
        pl.when(c == half)(lambda: fn(half))


def _when_taker(half, c, fn):
    if half is None:
        fn(1 - c)
    else:
        pl.when(c != half)(lambda: fn(half))


def _remote(ref, sems, t, j, dev):
    send, recv = sems
    return pltpu.make_async_remote_copy(src_ref=ref, dst_ref=ref, send_sem=send.at[t, j], recv_sem=recv.at[t, j],
                                        device_id=dev, device_id_type=MESH)


def _gather_ici(items, refs, sems, finish):
    x, y, c, chips = _mesh_place()
    me = 2 * x + y
    for t, (nm, half) in enumerate(items):
        def go(h, t=t, nm=nm):
            mine = _full_piece(nm, refs[t], me, h)
            for j, (cx, cy) in enumerate(chips):
                if not finish:
                    _remote(mine, sems, t, j, (cx, cy, c)).start()
                else:
                    _remote(_full_piece(nm, refs[t], 2 * cx + cy, h), sems, t, j, (cx, cy, c)).wait_recv()
                    _remote(mine, sems, t, j, (cx, cy, c)).wait_send()

        _when_mover(half, c, go)


def _gather_pass(items, refs, sems, finish):
    x, y, c, chips = _mesh_place()
    sibling = (x, y, 1 - c)
    for t, (nm, half) in enumerate(items):
        def give(h, t=t, nm=nm):
            for j, (cx, cy) in enumerate(chips):
                got = _remote(_full_piece(nm, refs[t], 2 * cx + cy, h), sems, t, j, sibling)
                if not finish:
                    got.start()
                else:
                    got.wait_send()

        def take(h, t=t, nm=nm):
            for j, (cx, cy) in enumerate(chips):
                _remote(_full_piece(nm, refs[t], 2 * cx + cy, h), sems, t, j, sibling).wait_recv()

        _when_mover(half, c, give)
        if finish:
            _when_taker(half, c, take)


def _gather_sems(n_items):
    return [pltpu.SemaphoreType.DMA((n_items, 3)), pltpu.SemaphoreType.DMA((n_items, 3))]


EARLY_ITEMS = (("pool_w", None), ("w_in", 0), ("w_out", 0))
LATE_ITEMS_A = (("w_out", 1), ("w_qkv", None), ("w_o", None))
LATE_ITEMS_B = (("w_in", 1),)


def _names(items):
    return [nm for nm, _ in items]


def _gather_early(full, small):
    items = EARLY_ITEMS
    n = len(items)

    def body(*refs):
        small_in = refs[n]
        outs, small_out = refs[n + 1:2 * n + 1], refs[2 * n + 1]
        scratch = refs[2 * n + 2:]
        ici_sems, pass_sems = scratch[0:2], scratch[2:4]
        local_sem, small_send, small_recv = scratch[4:]
        x, y, c, chips = _mesh_place()
        me = 2 * x + y
        local = pltpu.make_async_copy(small_in, small_out.at[me], local_sem.at[0])
        local.start()
        small_sends = [pltpu.make_async_remote_copy(
            src_ref=small_in, dst_ref=small_out.at[me], send_sem=small_send.at[j], recv_sem=small_recv.at[j],
            device_id=(cx, cy, c), device_id_type=MESH) for j, (cx, cy) in enumerate(chips)]
        for cp in small_sends:
            cp.start()
        _gather_ici(items, outs, ici_sems, finish=False)
        _gather_ici(items, outs, ici_sems, finish=True)
        _gather_pass(items, outs, pass_sems, finish=False)
        _gather_pass(items, outs, pass_sems, finish=True)
        for j, (cx, cy) in enumerate(chips):
            got = small_out.at[2 * cx + cy]
            pltpu.make_async_remote_copy(src_ref=got, dst_ref=got, send_sem=small_send.at[j],
                                         recv_sem=small_recv.at[j], device_id=(cx, cy, c), device_id_type=MESH).wait_recv()
        for cp in small_sends:
            cp.wait_send()
        local.wait()

    arrays = [full[nm] for nm in _names(items)]
    out_shape = [jax.ShapeDtypeStruct(a.shape, a.dtype) for a in arrays]
    out_shape.append(jax.ShapeDtypeStruct((N_CHIPS,) + small.shape, F32))
    res = pl.pallas_call(
        body, name="gather_early", in_specs=[ANY] * (n + 1), out_specs=[ANY] * (n + 1), out_shape=out_shape,
        input_output_aliases={t: t for t in range(n)},
        scratch_shapes=_gather_sems(n) + _gather_sems(n) + [pltpu.SemaphoreType.DMA((1,)), pltpu.SemaphoreType.DMA((3,)),
                                                            pltpu.SemaphoreType.DMA((3,))],
    )(*arrays, small)
    return {**full, **dict(zip(_names(items), res[:n]))}, res[n]


def _late_gather_host(stage, w):
    a, b = LATE_ITEMS_A, LATE_ITEMS_B
    na = len(a)

    def run(src, refs, new, sems, finish):
        if stage == "in":
            _gather_ici(a, refs, sems, finish)
        elif stage == "glu":
            _gather_pass(a, refs[:na], sems[:2], finish)
            _gather_ici(b, refs[na:], sems[2:], finish)
        else:
            _gather_pass(b, refs, sems, finish)

    sems = {"in": _gather_sems(na), "glu": _gather_sems(na) + _gather_sems(len(b)), "out": _gather_sems(len(b))}[stage]
    return _Hosted([], [w[nm] for nm in _late_names(stage)], [], sems, functools.partial(run, finish=False),
                   functools.partial(run, finish=True))


def _late_names(stage):
    return {"in": _names(LATE_ITEMS_A), "glu": _names(LATE_ITEMS_A) + _names(LATE_ITEMS_B), "out": _names(LATE_ITEMS_B)}[stage]


def _other_half_src(name, ref, h):
    if name in ("w_in", "w_out"):
        return ref.at[h]
    if name == "w_qkv":
        return ref.at[_row(h, D_MODEL // 2), :]
    if name == "w_o":
        return ref.at[:, pl.ds(h, 1)]
    return ref.at[pl.ds(2 * h, 2)]


SLAB_SHAPES = {"w_in": (D_MODEL, 2 * D_FF), "w_out": (D_FF, D_MODEL), "w_qkv": (D_MODEL // 2, QKV_DIM),
               "w_o": (N_CHIPS, 1, W_O_ROWS // 2, D_MODEL), "pool_w": (2, POOL_GC, POOL_GC)}


def _item_tag(item):
    return item[0] if item[1] is None else f"{item[0]}{item[1]}"


def _reduce_swap(items, grads, slabs, sems, finish):
    x, y, c, _ = _mesh_place()
    send, recv = sems
    for t, (nm, half) in enumerate(items):
        def copy(h, t=t, nm=nm):
            return pltpu.make_async_remote_copy(
                src_ref=_other_half_src(nm, grads[t], h), dst_ref=slabs[t], send_sem=send.at[t], recv_sem=recv.at[t],
                device_id=(x, y, 1 - c), device_id_type=MESH)

        def give(h):
            if finish:
                copy(h).wait_send()
            else:
                copy(h).start()

        if half is None:
            give(1 - c)
            if finish:
                copy(c).wait_recv()
        else:
            pl.when(c != half)(lambda: give(half))
            if finish:
                pl.when(c == half)(lambda: copy(half).wait_recv())


def _reduce_scatter(items, sums, parts, sems, finish):
    _, _, c, chips = _mesh_place()
    send, recv = sems
    for t, (nm, half) in enumerate(items):
        def go(h, t=t, nm=nm):
            for j, (cx, cy) in enumerate(chips):
                cp = pltpu.make_async_remote_copy(
                    src_ref=_slab_piece(nm, sums[t], 2 * cx + cy), dst_ref=parts[t].at[j], send_sem=send.at[t, j],
                    recv_sem=recv.at[t, j], device_id=(cx, cy, c), device_id_type=MESH)
                if finish:
                    cp.wait_recv()
                    cp.wait_send()
                else:
                    cp.start()

        _when_mover(half, c, go)


def _reduce_share(items, shards, sems, finish):
    x, y, c, _ = _mesh_place()
    send, recv = sems
    for t, (nm, half) in enumerate(items):
        def copy(h, t=t, nm=nm):
            part = _shard_half(nm, shards[t], h)
            return pltpu.make_async_remote_copy(src_ref=part, dst_ref=part, send_sem=send.at[t], recv_sem=recv.at[t],
                                                device_id=(x, y, 1 - c), device_id_type=MESH)

        def give(h):
            if finish:
                copy(h).wait_send()
            else:
                copy(h).start()

        _when_mover(half, c, give)
        if finish:
            _when_taker(half, c, lambda h: copy(h).wait_recv())


def _pair_sems(n):
    return [pltpu.SemaphoreType.DMA((n,)), pltpu.SemaphoreType.DMA((n,))]


def _swap_host(items, grads):
    def run(src, arr, new, sems, finish):
        _reduce_swap(items, src, new, sems, finish)

    return _Hosted([grads[nm] for nm, _ in items], [], [jax.ShapeDtypeStruct(SLAB_SHAPES[nm], F32) for nm, _ in items],
                   _pair_sems(len(items)), functools.partial(run, finish=False), functools.partial(run, finish=True))


def _scatter_host(items, sums):
    def run(src, arr, new, sems, finish):
        _reduce_scatter(items, src, new, sems, finish)

    new = [jax.ShapeDtypeStruct((N_CHIPS - 1,) + HALF_SHAPES[nm], WIRE_DTYPE) for nm, _ in items]
    return _Hosted(list(sums), [], new, _gather_sems(len(items)), functools.partial(run, finish=False),
                   functools.partial(run, finish=True))


def _share_host(items, shards):
    def run(src, arr, new, sems, finish):
        _reduce_share(items, arr, sems, finish)

    return _Hosted([], list(shards), [], _pair_sems(len(items)), functools.partial(run, finish=False),
                   functools.partial(run, finish=True))


def _alone(hosted, name):
    def body(ins, outs, scratch, first, last):
        first()
        last()

    return _call(body, hosted, name=name, grid=(1,), in_specs=[], out_specs=[], out_shape=[], operands=(),
                 semantics=("arbitrary",))


def _pair_sum(item, grad, slab, place):
    name, half = item
    shape = SLAB_SHAPES[name]
    hsel = (lambda p: p[1]) if half is None else (lambda p: half)
    if name == "w_in":
        blk = (256, 2 * D_FF)
        grid = (D_MODEL // 256,)
        g_spec = pl.BlockSpec((None,) + blk, lambda i, p: (hsel(p), i, 0))
        s_spec = pl.BlockSpec(blk, lambda i, p: (i, 0))
    elif name == "w_out":
        blk = (W_OUT_ROWS, D_MODEL)
        grid = (N_CHIPS,)
        g_spec = pl.BlockSpec((None,) + blk, lambda i, p: (hsel(p), i, 0))
        s_spec = pl.BlockSpec(blk, lambda i, p: (i, 0))
    elif name == "w_qkv":
        grid = (1,)
        g_spec = pl.BlockSpec(shape, lambda i, p: (hsel(p), 0))
        s_spec = pl.BlockSpec(shape, lambda i, p: (0, 0))
    else:
        grid = (1,)
        g_spec = pl.BlockSpec(shape, lambda i, p: (0, hsel(p), 0, 0))
        s_spec = pl.BlockSpec(shape, lambda i, p: (0, 0, 0, 0))

    def body(p_ref, g_ref, s_ref, o_ref):
        @pl.when(hsel(p_ref) == p_ref[1])
        def _():
            o_ref[...] = (g_ref[...] + s_ref[...]).astype(o_ref.dtype)

    return pl.pallas_call(
        body, name=f"pair_sum_{_item_tag(item)}",
        grid_spec=pltpu.PrefetchScalarGridSpec(num_scalar_prefetch=1, grid=grid, in_specs=[g_spec, s_spec],
                                               out_specs=s_spec),
        out_shape=jax.ShapeDtypeStruct(shape, WIRE_DTYPE), compiler_params=_params("parallel"),
    )(place, grad, slab)


def _slab_piece(name, ref, k):
    if name == "w_in":
        return ref.at[:, _col(k, W_IN_COLS)]
    if name == "w_out":
        return ref.at[_row(k, W_OUT_ROWS), :]
    if name == "w_qkv":
        return ref.at[:, _col(k, QKV_COLS)]
    if name == "w_o":
        return ref.at[k, 0]
    return ref.at[:, _row(k, POOL_ROWS), :]


def _sum_chips(item, slab, parts, place, into=None):
    name, sel = item
    half = HALF_SHAPES[name]
    hsel = (lambda p: p[1]) if sel is None else (lambda p: sel)
    if name == "w_in":
        blk, grid = (256, W_IN_COLS), (D_MODEL // 256,)
        own = pl.BlockSpec(blk, lambda i, p: (i, p[0]))
        got = pl.BlockSpec((N_CHIPS - 1,) + blk, lambda i, p: (0, i, 0))
        out = pl.BlockSpec((None,) + blk, lambda i, p: (hsel(p), i, 0))
    elif name == "w_out":
        grid = (1,)
        own = pl.BlockSpec(half, lambda i, p: (p[0], 0))
        got = pl.BlockSpec((N_CHIPS - 1,) + half, lambda i, p: (0, 0, 0))
        out = pl.BlockSpec((None,) + half, lambda i, p: (hsel(p), 0, 0))
    elif name == "w_qkv":
        grid = (1,)
        own = pl.BlockSpec(half, lambda i, p: (0, p[0]))
        got = pl.BlockSpec((N_CHIPS - 1,) + half, lambda i, p: (0, 0, 0))
        out = pl.BlockSpec(half, lambda i, p: (hsel(p), 0))
    else:
        grid = (1,)
        own = pl.BlockSpec((None, None) + half, lambda i, p: (p[0], 0, 0, 0))
        got = pl.BlockSpec((N_CHIPS - 1,) + half, lambda i, p: (0, 0, 0))
        out = pl.BlockSpec(half, lambda i, p: (hsel(p), 0))

    def body(p_ref, own_ref, got_ref, *rest):
        o_ref = rest[-1]

        @pl.when(hsel(p_ref) == p_ref[1])
        def _():
            acc = own_ref[...].astype(F32)
            for j in range(N_CHIPS - 1):
                acc = acc + got_ref[j].astype(F32)
            o_ref[...] = acc

    in_specs, operands, aliases = [own, got], [place, slab, parts], {}
    if into is not None:
        in_specs.append(ANY)
        operands.append(into)
        aliases = {3: 0}
    return pl.pallas_call(
        body, name=f"sum_chips_{_item_tag(item)}",
        grid_spec=pltpu.PrefetchScalarGridSpec(num_scalar_prefetch=1, grid=grid, in_specs=in_specs, out_specs=out),
        out_shape=jax.ShapeDtypeStruct(SHARD_SHAPES[name], F32), input_output_aliases=aliases,
        compiler_params=_params("parallel"),
    )(*operands)


class _Reducer:
    FIRST = (("w_in", 1), ("w_out", 1))
    REST = (("w_in", 0), ("w_out", 0), ("w_qkv", None), ("w_o", None))

    def __init__(self, place):
        self.place = place
        self.shards = {}
        self.state = {}

    def swap(self, items, grads):
        self.state[items] = {"grads": [grads[nm] for nm, _ in items]}
        return _swap_host(items, grads)

    def pair_sums(self, items, slabs):
        st = self.state[items]
        st["sums"] = [_pair_sum(it, g, s, self.place) for it, g, s in zip(items, st["grads"], slabs)]

    def scatter(self, items):
        return _scatter_host(items, self.state[items]["sums"])

    def chip_sums(self, items, parts):
        for it, s, p in zip(items, self.state[items]["sums"], parts):
            self.shards[it[0]] = _sum_chips(it, s, p, self.place, into=self.shards.get(it[0]))

    def share(self, items):
        return _share_host(items, [self.shards[nm] for nm, _ in items])

    def shared(self, items, updated):
        self.shards.update(dict(zip(_names(items), updated)))


N_DEV = 8


def _all_reduce_small(vec, g_pool):
    def body(v_ref, gp_ref, o_ref, po_ref, slots, pslots, send_sems, recv_sems, psend, precv, hsend, hrecv):
        x, y, c, _ = _mesh_place()
        me = 4 * x + 2 * y + c

        def pool_piece(chip, h):
            return gp_ref.at[pl.ds(2 * h, 2), pl.ds(pl.multiple_of(chip * POOL_ROWS, POOL_ROWS), POOL_ROWS), :]

        slots[me] = v_ref[...]
        pslots[me] = pool_piece(2 * x + y, c)[...]
        copies = []
        for d in range(1, N_DEV):
            peer = (x ^ (d >> 2), y ^ ((d >> 1) & 1), c ^ (d & 1))
            copies.append(pltpu.make_async_remote_copy(
                src_ref=v_ref, dst_ref=slots.at[me], send_sem=send_sems.at[d - 1], recv_sem=recv_sems.at[d - 1],
                device_id=peer, device_id_type=MESH))
            copies.append(pltpu.make_async_remote_copy(
                src_ref=pool_piece(2 * peer[0] + peer[1], peer[2]), dst_ref=pslots.at[me], send_sem=psend.at[d - 1],
                recv_sem=precv.at[d - 1], device_id=peer, device_id_type=MESH))
        for cp in copies:
            cp.start()
        for d in range(1, N_DEV):
            peer = 4 * (x ^ (d >> 2)) + 2 * (y ^ ((d >> 1) & 1)) + (c ^ (d & 1))
            for buf, ss, rs in ((slots, send_sems, recv_sems), (pslots, psend, precv)):
                got = buf.at[peer]
                pltpu.make_async_remote_copy(src_ref=got, dst_ref=got, send_sem=ss.at[d - 1], recv_sem=rs.at[d - 1],
                                             device_id=(x, y, c), device_id_type=MESH).wait_recv()
        for cp in copies:
            cp.wait_send()
        acc, pacc = slots[0], pslots[0]
        for d in range(1, N_DEV):
            acc, pacc = acc + slots[d], pacc + pslots[d]
        o_ref[...] = acc
        po_ref[pl.ds(2 * c, 2)] = pacc
        mine, theirs = po_ref.at[pl.ds(2 * c, 2)], po_ref.at[pl.ds(2 * (1 - c), 2)]
        give = pltpu.make_async_remote_copy(src_ref=mine, dst_ref=mine, send_sem=hsend.at[0], recv_sem=hrecv.at[0],
                                            device_id=(x, y, 1 - c), device_id_type=MESH)
        give.start()
        pltpu.make_async_remote_copy(src_ref=theirs, dst_ref=theirs, send_sem=hsend.at[0], recv_sem=hrecv.at[0],
                                     device_id=(x, y, 1 - c), device_id_type=MESH).wait_recv()
        give.wait_send()

    vm = pl.BlockSpec(memory_space=pltpu.VMEM)
    piece = (2, POOL_ROWS, POOL_GC)
    return pl.pallas_call(
        body, name="all_reduce_small", in_specs=[vm, vm], out_specs=[vm, vm],
        out_shape=[jax.ShapeDtypeStruct(vec.shape, F32), jax.ShapeDtypeStruct(SHARD_SHAPES["pool_w"], F32)],
        scratch_shapes=[pltpu.VMEM((N_DEV,) + vec.shape, F32), pltpu.VMEM((N_DEV,) + piece, F32)]
        + [pltpu.SemaphoreType.DMA((N_DEV - 1,))] * 4 + [pltpu.SemaphoreType.DMA((1,))] * 2,
    )(vec, g_pool)


def _adamw_call(w, g, m, v, name):
    shape = w.shape
    cols = shape[-1]
    rows = w.size // cols
    tr = rows
    for cand in (256, 128, 64, 32, 16, 8):
        if rows > cand and rows % cand == 0 and cand * cols * 4 <= 2 * 1024 * 1024:
            tr = cand
            break

    def body(w_ref, g_ref, m_ref, v_ref, d_ref, nm_ref, nv_ref):
        gv = g_ref[...]
        mn = ADAM_B1 * m_ref[...] + (1.0 - ADAM_B1) * gv
        vn = ADAM_B2 * v_ref[...] + (1.0 - ADAM_B2) * jnp.square(gv)
        m_hat = mn / (1.0 - ADAM_B1 ** ADAM_STEP)
        v_hat = vn / (1.0 - ADAM_B2 ** ADAM_STEP)
        d_ref[...] = -ADAM_LR * (m_hat / (jnp.sqrt(v_hat) + ADAM_EPS) + ADAM_WD * w_ref[...])
        nm_ref[...] = mn
        nv_ref[...] = vn

    spec = pl.BlockSpec((tr, cols), lambda i: (i, 0))
    flat = lambda a: a.reshape(rows, cols)
    outs = pl.pallas_call(
        body, name=name, grid=(rows // tr,), in_specs=[spec] * 4, out_specs=[spec] * 3,
        out_shape=[jax.ShapeDtypeStruct((rows, cols), F32)] * 3, compiler_params=_params("parallel"),
    )(flat(w), flat(g), flat(m), flat(v))
    return tuple(o.reshape(shape) for o in outs)


WEIGHTS = ("pool_w", "pool_b", "pool_scale", "attn_w_qkv", "attn_b_qkv", "attn_sinks", "attn_w_o", "norm_mix",
           "norm_ffn", "ffn_w_in", "ffn_conv_w", "ffn_conv_b", "ffn_w_out", "norm_f")
SMALL_GATHER = 128 * 71
_SMALL_SIZES = (("pool_b", 1024), ("pool_scale", 1024), ("b_qkv", 1536), ("sinks", 128), ("norm_mix", 2048),
                ("norm_ffn", 2048), ("conv_w", 6 * 2 * D_FF), ("conv_b", 4 * D_FF), ("norm_f", 1024))
SMALL_COLS = 6784


def _pack_small(grads):
    flat = jnp.concatenate([jnp.pad(grads[nm].reshape(-1), (0, size - grads[nm].size)) for nm, size in _SMALL_SIZES])
    return jnp.pad(flat, (0, 8 * SMALL_COLS - flat.size)).reshape(8, SMALL_COLS)


def _unpack_small(vec):
    flat = vec.reshape(-1)
    out, off = {}, 0
    for nm, size in _SMALL_SIZES:
        out[nm] = flat[off:off + size]
        off += size
    return out


def kernel(x, pool_w, pool_b, pool_scale, attn_w_qkv, attn_b_qkv, attn_sinks, attn_w_o, norm_mix, norm_ffn, ffn_w_in, ffn_conv_w, ffn_conv_b, ffn_w_out, norm_f, loss_target, m_pool_w, m_pool_b, m_pool_scale, m_attn_w_qkv, m_attn_b_qkv, m_attn_sinks, m_attn_w_o, m_norm_mix, m_norm_ffn, m_ffn_w_in, m_ffn_conv_w, m_ffn_conv_b, m_ffn_w_out, m_norm_f, v_pool_w, v_pool_b, v_pool_scale, v_attn_w_qkv, v_attn_b_qkv, v_attn_sinks, v_attn_w_o, v_norm_mix, v_norm_ffn, v_ffn_w_in, v_ffn_conv_w, v_ffn_conv_b, v_ffn_w_out, v_norm_f):
    weights = dict(pool_w=pool_w, pool_b=pool_b, pool_scale=pool_scale, attn_w_qkv=attn_w_qkv, attn_b_qkv=attn_b_qkv,
                   attn_sinks=attn_sinks, attn_w_o=attn_w_o, norm_mix=norm_mix, norm_ffn=norm_ffn, ffn_w_in=ffn_w_in,
                   ffn_conv_w=ffn_conv_w, ffn_conv_b=ffn_conv_b, ffn_w_out=ffn_w_out, norm_f=norm_f)
    m_in = dict(pool_w=m_pool_w, pool_b=m_pool_b, pool_scale=m_pool_scale, attn_w_qkv=m_attn_w_qkv,
                attn_b_qkv=m_attn_b_qkv, attn_sinks=m_attn_sinks, attn_w_o=m_attn_w_o, norm_mix=m_norm_mix,
                norm_ffn=m_norm_ffn, ffn_w_in=m_ffn_w_in, ffn_conv_w=m_ffn_conv_w, ffn_conv_b=m_ffn_conv_b,
                ffn_w_out=m_ffn_w_out, norm_f=m_norm_f)
    v_in = dict(pool_w=v_pool_w, pool_b=v_pool_b, pool_scale=v_pool_scale, attn_w_qkv=v_attn_w_qkv,
                attn_b_qkv=v_attn_b_qkv, attn_sinks=v_attn_sinks, attn_w_o=v_attn_w_o, norm_mix=v_norm_mix,
                norm_ffn=v_norm_ffn, ffn_w_in=v_ffn_w_in, ffn_conv_w=v_ffn_conv_w, ffn_conv_b=v_ffn_conv_b,
                ffn_w_out=v_ffn_w_out, norm_f=v_norm_f)
    chip = 2 * lax.axis_index("x") + lax.axis_index("y")
    core = lax.axis_index("c")

    place = jnp.stack([chip, core]).astype(jnp.int32)
    shards = {"w_in": ffn_w_in, "w_out": ffn_w_out, "w_qkv": attn_w_qkv[0], "w_o": attn_w_o[0], "pool_w": pool_w[0]}
    placed = {nm: _place_shard(nm, shards[nm], place) for nm in BIG}
    small = jnp.concatenate([pool_b.reshape(-1), attn_b_qkv.reshape(-1), ffn_conv_w.reshape(-1)]).reshape(1, SMALL_GATHER)
    full, small_all = _gather_early(placed, small)
    small_all = small_all.reshape(N_CHIPS, SMALL_GATHER)
    pool_b_full = small_all[:, :256].reshape(N_CHIPS, 4, POOL_ROWS).transpose(1, 0, 2).reshape(1, D_MODEL)
    b_qkv_full = small_all[:, 256:640].reshape(1, QKV_DIM)
    conv_w_full = small_all[:, 640:].reshape(N_CHIPS, 2, 3, W_IN_COLS).transpose(1, 2, 0, 3).reshape(2, 3, 2 * D_FF)

    reducer = _Reducer(place)
    loss, grad_x, grads = _local_step(
        x[0], loss_target[0], full, norm_mix=norm_mix, norm_ffn=norm_ffn, norm_f=norm_f.reshape(1, D_MODEL),
        pool_b=pool_b_full, pool_scale=pool_scale, b_qkv=b_qkv_full, sinks=attn_sinks[0], conv_w=conv_w_full,
        conv_b=ffn_conv_b, late_gather=_late_gather_host, reducer=reducer)
    loss = lax.psum(loss[0, 0], ("x", "y", "c"))
    reduced = dict(reducer.shards)

    small_sum, reduced["pool_w"] = _all_reduce_small(_pack_small(grads), grads["pool_w"])
    small_g = _unpack_small(small_sum)
    pool_b_g = lax.dynamic_slice_in_dim(small_g["pool_b"].reshape(4, N_CHIPS, POOL_ROWS), chip, 1, axis=1)
    b_qkv_g = lax.dynamic_slice_in_dim(small_g["b_qkv"].reshape(N_CHIPS, QKV_COLS), chip, 1, axis=0)
    conv_w_g = lax.dynamic_slice_in_dim(small_g["conv_w"].reshape(2, 3, N_CHIPS, W_IN_COLS), chip, 1, axis=2)
    grad_w = {
        "pool_w": reduced["pool_w"].reshape(pool_w.shape), "pool_b": pool_b_g.reshape(pool_b.shape),
        "pool_scale": small_g["pool_scale"].reshape(pool_scale.shape),
        "attn_w_qkv": reduced["w_qkv"].reshape(attn_w_qkv.shape), "attn_b_qkv": b_qkv_g.reshape(attn_b_qkv.shape),
        "attn_sinks": small_g["sinks"][:N_HEADS].reshape(attn_sinks.shape),
        "attn_w_o": reduced["w_o"].reshape(attn_w_o.shape), "norm_mix": small_g["norm_mix"].reshape(norm_mix.shape),
        "norm_ffn": small_g["norm_ffn"].reshape(norm_ffn.shape), "ffn_w_in": reduced["w_in"],
        "ffn_conv_w": conv_w_g.reshape(ffn_conv_w.shape), "ffn_conv_b": small_g["conv_b"].reshape(ffn_conv_b.shape),
        "ffn_w_out": reduced["w_out"], "norm_f": small_g["norm_f"].reshape(norm_f.shape),
    }

    delta, new_m, new_v = {}, {}, {}
    for nm in WEIGHTS:
        delta[nm], new_m[nm], new_v[nm] = _adamw_call(weights[nm], grad_w[nm], m_in[nm], v_in[nm], f"adamw_{nm}")
    return (loss, grad_x.reshape(x.shape), *[grad_w[nm] for nm in WEIGHTS], *[delta[nm] for nm in WEIGHTS],
            *[new_m[nm] for nm in WEIGHTS], *[new_v[nm] for nm in WEIGHTS])
```

```python
import functools
from typing import Callable, NamedTuple

import jax
import jax.numpy as jnp
from jax import lax
from jax.experimental import pallas as pl
from jax.experimental.pallas import tpu as pltpu

F32 = jnp.float32
MXU_DTYPE = jnp.bfloat16
WIRE_DTYPE = jnp.bfloat16

D_MODEL = 1024
D_FF = 2816
QKV_DIM = 1536
HEAD_DIM = 64
N_HEADS = 16
N_KV_HEADS = 4
ATT_BLOCK = 128
POOL_WINDOWS = (2, 4, 8, 16)
POOL_GC = 256
POOL_HALO = 16
CONV_HALO = 8
RMS_EPS = 1e-6
ATT_SCALE = HEAD_DIM ** -0.5
ALIBI_SLOPES = tuple(2.0 ** (-8.0 / N_HEADS * (h + 1)) for h in range(N_HEADS))

ADAM_LR = 0.001
ADAM_B1 = 0.9
ADAM_B2 = 0.999
ADAM_EPS = 1e-08
ADAM_WD = 0.01
ADAM_STEP = 10

N_CHIPS = 4
MESH = pl.DeviceIdType.MESH
VMEM_LIMIT_BYTES = 56 * 1024 * 1024
ANY = pl.BlockSpec(memory_space=pl.ANY)


def _params(*semantics):
    return pltpu.CompilerParams(dimension_semantics=semantics, vmem_limit_bytes=VMEM_LIMIT_BYTES)


def _rms(x, g):
    return x * lax.rsqrt(jnp.mean(x * x, axis=-1, keepdims=True) + RMS_EPS) * g


def _rms_bwd(x, g, dy):
    rstd = lax.rsqrt(jnp.mean(x * x, axis=-1, keepdims=True) + RMS_EPS)
    xhat = x * rstd
    dxhat = dy * g
    dx = rstd * (dxhat - xhat * jnp.mean(dxhat * xhat, axis=-1, keepdims=True))
    return dx, dy * xhat


def _shift_down(x, s):
    return pltpu.roll(x, s, axis=0)


def _shift_up(x, s):
    return pltpu.roll(x, x.shape[0] - s, axis=0)


def _sigmoid(x):
    return 1.0 / (1.0 + jnp.exp(-x))


_DOT_DIMS = {"nn": ((1,), (0,)), "nt": ((1,), (1,)), "tn": ((0,), (0,))}


class _Hosted(NamedTuple):
    sources: list
    arrays: list
    new: list
    sems: list
    first: Callable
    last: Callable


def _call(body, hosted, *, name, grid, in_specs, out_specs, out_shape, operands, semantics, scratch_shapes=(),
          aliases=None):
    n_in, n_out, n_scr = len(in_specs), len(out_specs), len(scratch_shapes)
    aliases = dict(aliases or {})
    if hosted is None:
        def plain(*refs):
            body(refs[:n_in], refs[n_in:n_in + n_out], refs[n_in + n_out:], lambda: None, lambda: None)

        return pl.pallas_call(plain, name=name, grid=grid, in_specs=list(in_specs), out_specs=list(out_specs),
                              out_shape=list(out_shape), scratch_shapes=list(scratch_shapes),
                              input_output_aliases=aliases, compiler_params=_params(*semantics))(*operands)
    ns, na, nn = len(hosted.sources), len(hosted.arrays), len(hosted.new)

    def carrying(*refs):
        ins, src = refs[:n_in], refs[n_in:n_in + ns]
        o0 = n_in + ns + na
        outs, arr = refs[o0:o0 + n_out], refs[o0 + n_out:o0 + n_out + na]
        new = refs[o0 + n_out + na:o0 + n_out + na + nn]
        s0 = o0 + n_out + na + nn
        scratch, sems = refs[s0:s0 + n_scr], refs[s0 + n_scr:]
        ids = [pl.program_id(d) for d in range(len(grid))]
        is_first = functools.reduce(lambda p, q: p & q, [i == 0 for i in ids])
        is_last = functools.reduce(lambda p, q: p & q, [i == n - 1 for i, n in zip(ids, grid)])
        body(ins, outs, scratch, lambda: pl.when(is_first)(lambda: hosted.first(src, arr, new, sems)),
             lambda: pl.when(is_last)(lambda: hosted.last(src, arr, new, sems)))

    for t in range(na):
        aliases[n_in + ns + t] = n_out + t
    return pl.pallas_call(
        carrying, name=name, grid=grid, in_specs=list(in_specs) + [ANY] * (ns + na),
        out_specs=list(out_specs) + [ANY] * (na + nn),
        out_shape=list(out_shape) + [jax.ShapeDtypeStruct(a.shape, a.dtype) for a in hosted.arrays] + list(hosted.new),
        scratch_shapes=list(scratch_shapes) + list(hosted.sems), input_output_aliases=aliases,
        compiler_params=_params(*["arbitrary"] * len(grid)))(*operands, *hosted.sources, *hosted.arrays)


def _both(h1, h2):
    def run(which, src, arr, new, sems):
        cut = lambda seq, n: (seq[:n], seq[n:])
        (s1, s2), (a1, a2) = cut(src, len(h1.sources)), cut(arr, len(h1.arrays))
        (n1, n2), (m1, m2) = cut(new, len(h1.new)), cut(sems, len(h1.sems))
        getattr(h1, which)(s1, a1, n1, m1)
        getattr(h2, which)(s2, a2, n2, m2)

    return _Hosted(h1.sources + h2.sources, h1.arrays + h2.arrays, h1.new + h2.new, h1.sems + h2.sems,
                   functools.partial(run, "first"), functools.partial(run, "last"))


def _ride(hooks, stage, call, env=None):
    if not hooks or stage not in hooks:
        res = call(None)
        return list(res) if isinstance(res, (list, tuple)) else [res]
    make, done = hooks[stage]
    hosted = make(env)
    res = call(hosted)
    own = len(res) - len(hosted.arrays) - len(hosted.new)
    done(res[own:])
    return list(res[:own])


def _matmul(a, b, *, mode, name, grid, a_spec, b_spec, o_spec, out_shape, bias=None, bias_spec=None, residual=None,
            residual_spec=None, into=None, hosted=None):
    nk = grid[2]
    dims = (_DOT_DIMS[mode], ((), ()))
    acc_shape = tuple(d for d in o_spec.block_shape if d is not None)

    def body(ins, outs, scratch, first, last):
        a_ref, b_ref = ins[0], ins[1]
        pos = 2
        bias_ref = res_ref = None
        if bias is not None:
            bias_ref = ins[pos]
            pos += 1
        if residual is not None:
            res_ref = ins[pos]
        o_ref = outs[0]
        acc_ref = scratch[0] if nk > 1 else None
        first()
        prod = lax.dot_general(a_ref[...].astype(MXU_DTYPE), b_ref[...].astype(MXU_DTYPE), dims,
                               preferred_element_type=F32)

        def finish(acc):
            if bias_ref is not None:
                acc = acc + bias_ref[...]
            if res_ref is not None:
                acc = acc + res_ref[...]
            o_ref[...] = acc.astype(o_ref.dtype)

        if nk == 1:
            finish(prod)
        else:
            k = pl.program_id(2)

            @pl.when(k == 0)
            def _():
                acc_ref[...] = prod

            @pl.when(k > 0)
            def _():
                acc_ref[...] += prod

            @pl.when(k == nk - 1)
            def _():
                finish(acc_ref[...])

        last()

    operands, in_specs = [a, b], [a_spec, b_spec]
    if bias is not None:
        operands.append(bias)
        in_specs.append(bias_spec)
    if residual is not None:
        operands.append(residual)
        in_specs.append(residual_spec)
    aliases = {}
    if into is not None:
        aliases = {len(operands): 0}
        operands.append(into)
        in_specs.append(ANY)
    res = _call(body, hosted, name=name, grid=grid, in_specs=in_specs, out_specs=[o_spec], out_shape=[out_shape],
                operands=operands, semantics=("parallel", "parallel", "arbitrary"),
                scratch_shapes=[pltpu.VMEM(acc_shape, F32)] if nk > 1 else [], aliases=aliases)
    return res if hosted else res[0]


def _tile(n, want):
    t = min(n, want)
    assert n % t == 0, (n, want)
    return t


def _mm_nn(a, b, name, *, b_lead=None, bias=None, residual=None, out_dtype=F32, tm=1024, tn=1024, tk=1024,
           hosted=None):
    (m, k), n = a.shape, b.shape[-1]
    tm, tn, tk = _tile(m, tm), _tile(n, tn), _tile(k, tk)
    if b_lead is None:
        b_spec = pl.BlockSpec((tk, tn), lambda i, j, kk: (kk, j))
    else:
        b_spec = pl.BlockSpec((None, tk, tn), lambda i, j, kk: (b_lead, kk, j))
    return _matmul(
        a, b, mode="nn", name=name, grid=(m // tm, n // tn, k // tk),
        a_spec=pl.BlockSpec((tm, tk), lambda i, j, kk: (i, kk)), b_spec=b_spec,
        o_spec=pl.BlockSpec((tm, tn), lambda i, j, kk: (i, j)), out_shape=jax.ShapeDtypeStruct((m, n), out_dtype),
        bias=bias, bias_spec=pl.BlockSpec((1, tn), lambda i, j, kk: (0, j)),
        residual=residual, residual_spec=pl.BlockSpec((tm, tn), lambda i, j, kk: (i, j)), hosted=hosted)


def _mm_nt(a, b, name, *, b_lead=None, out_dtype=F32, tm=1024, tn=1024, tk=1024, hosted=None):
    (m, k), n = a.shape, b.shape[-2]
    tm, tn, tk = _tile(m, tm), _tile(n, tn), _tile(k, tk)
    if b_lead is None:
        b_spec = pl.BlockSpec((tn, tk), lambda i, j, kk: (j, kk))
    else:
        b_spec = pl.BlockSpec((None, tn, tk), lambda i, j, kk: (b_lead, j, kk))
    return _matmul(
        a, b, mode="nt", name=name, grid=(m // tm, n // tn, k // tk),
        a_spec=pl.BlockSpec((tm, tk), lambda i, j, kk: (i, kk)), b_spec=b_spec,
        o_spec=pl.BlockSpec((tm, tn), lambda i, j, kk: (i, j)), out_shape=jax.ShapeDtypeStruct((m, n), out_dtype),
        hosted=hosted)


def _mm_tn(a, b, name, *, lead=None, n_lead=None, into=None, tm=1024, tn=1024, tk=1024):
    (k, m), n = a.shape, b.shape[-1]
    tm, tn, tk = _tile(m, tm), _tile(n, tn), _tile(k, tk)
    if lead is None:
        o_spec = pl.BlockSpec((tm, tn), lambda i, j, kk: (i, j))
        out_shape = jax.ShapeDtypeStruct((m, n), F32)
    else:
        o_spec = pl.BlockSpec((None, tm, tn), lambda i, j, kk: (lead, i, j))
        out_shape = jax.ShapeDtypeStruct((n_lead, m, n), F32)
    return _matmul(
        a, b, mode="tn", name=name, grid=(m // tm, n // tn, k // tk),
        a_spec=pl.BlockSpec((tk, tm), lambda i, j, kk: (kk, i)),
        b_spec=pl.BlockSpec((tk, tn), lambda i, j, kk: (kk, j)), o_spec=o_spec, out_shape=out_shape, into=into)


ROW_TILE = 512


def _rms_fwd_call(h, g, name, out_dtype):
    t = h.shape[0]
    tt = _tile(t, ROW_TILE)

    def body(h_ref, g_ref, o_ref):
        o_ref[...] = _rms(h_ref[...], g_ref[...]).astype(o_ref.dtype)

    row = pl.BlockSpec((tt, D_MODEL), lambda i: (i, 0))
    vec = pl.BlockSpec((1, D_MODEL), lambda i: (0, 0))
    return pl.pallas_call(body, name=name, grid=(t // tt,), in_specs=[row, vec], out_specs=row,
                          out_shape=jax.ShapeDtypeStruct(h.shape, out_dtype), compiler_params=_params("parallel"))(h, g)


def _rms_bwd_call(h, g, dy, dres, name):
    t = h.shape[0]
    tt = _tile(t, ROW_TILE)

    def body(h_ref, g_ref, dy_ref, dres_ref, dx_ref, dx16_ref, dg_ref):
        dx, dgt = _rms_bwd(h_ref[...], g_ref[...], dy_ref[...].astype(F32))
        dx = dres_ref[...] + dx
        dx_ref[...] = dx
        dx16_ref[...] = dx.astype(dx16_ref.dtype)

        @pl.when(pl.program_id(0) == 0)
        def _():
            dg_ref[...] = jnp.zeros_like(dg_ref)

        dg_ref[...] += jnp.sum(dgt, axis=0, keepdims=True)

    row = pl.BlockSpec((tt, D_MODEL), lambda i: (i, 0))
    vec = pl.BlockSpec((1, D_MODEL), lambda i: (0, 0))
    return pl.pallas_call(
        body, name=name, grid=(t // tt,), in_specs=[row, vec, row, row], out_specs=[row, row, vec],
        out_shape=[jax.ShapeDtypeStruct(h.shape, F32), jax.ShapeDtypeStruct(h.shape, MXU_DTYPE),
                   jax.ShapeDtypeStruct((1, D_MODEL), F32)],
        compiler_params=_params("arbitrary"))(h, g, dy, dres)


def _loss_call(h, g, target):
    t = h.shape[0]
    tt = _tile(t, ROW_TILE)

    def body(h_ref, g_ref, t_ref, loss_ref, dh_ref, dh16_ref, dg_ref):
        x, gv = h_ref[...], g_ref[...]
        err = _rms(x, gv) - t_ref[...]
        dx, dgt = _rms_bwd(x, gv, err * (1.0 / D_MODEL))
        dh_ref[...] = dx
        dh16_ref[...] = dx.astype(dh16_ref.dtype)

        @pl.when(pl.program_id(0) == 0)
        def _():
            dg_ref[...] = jnp.zeros_like(dg_ref)
            loss_ref[...] = jnp.zeros_like(loss_ref)

        dg_ref[...] += jnp.sum(dgt, axis=0, keepdims=True)
        per_token = jnp.mean(err * err, axis=-1, keepdims=True)
        loss_ref[...] += 0.5 * jnp.sum(per_token, axis=0, keepdims=True)

    row = pl.BlockSpec((tt, D_MODEL), lambda i: (i, 0))
    vec = pl.BlockSpec((1, D_MODEL), lambda i: (0, 0))
    one = pl.BlockSpec((1, 1), lambda i: (0, 0))
    return pl.pallas_call(
        body, name="loss_head", grid=(t // tt,), in_specs=[row, vec, row], out_specs=[one, row, row, vec],
        out_shape=[jax.ShapeDtypeStruct((1, 1), F32), jax.ShapeDtypeStruct(h.shape, F32),
                   jax.ShapeDtypeStruct(h.shape, MXU_DTYPE), jax.ShapeDtypeStruct((1, D_MODEL), F32)],
        compiler_params=_params("arbitrary"))(h, g, target)


def _pool_windows_causal(ext, first_row):
    tt = ext.shape[0] - POOL_HALO
    t = first_row + lax.broadcasted_iota(jnp.int32, (tt, 1), 0)
    outs = []
    for gi, win in enumerate(POOL_WINDOWS):
        cols = slice(gi * POOL_GC, (gi + 1) * POOL_GC)
        s = ext[:, cols]
        sh = 1
        while sh < win:
            s = s + _shift_down(s, sh)
            sh *= 2
        count = jnp.minimum(t + 1, win).astype(F32)
        outs.append(s[POOL_HALO:] / count - ext[POOL_HALO:, cols])
    return outs


def _prev_halo_spec(tt, halo, width):
    per = tt // halo
    return pl.BlockSpec((halo, width), lambda i: (jnp.maximum(i * per - 1, 0), 0))


def _pool_fwd_call(h0, g_mix, pool_w, pool_b, pool_scale, g_ffn):
    t = h0.shape[0]
    tt = _tile(t, ROW_TILE)

    def body(h_ref, halo_ref, gm_ref, w_ref, b_ref, sc_ref, gf_ref, h1_ref, hn_ref):
        i = pl.program_id(0)
        x, gm = h_ref[...], gm_ref[...]
        top = jnp.where(i > 0, _rms(halo_ref[...], gm), 0.0)
        ext = jnp.concatenate([top, _rms(x, gm)], axis=0)
        ps = _pool_windows_causal(ext, i * tt)
        ys = [jnp.dot(p.astype(MXU_DTYPE), w_ref[gi], preferred_element_type=F32) for gi, p in enumerate(ps)]
        mix = (jnp.concatenate(ys, axis=1) + b_ref[...]) * sc_ref[...]
        h1 = x + mix
        h1_ref[...] = h1
        hn_ref[...] = _rms(h1, gf_ref[...]).astype(hn_ref.dtype)

    row = pl.BlockSpec((tt, D_MODEL), lambda i: (i, 0))
    vec = pl.BlockSpec((1, D_MODEL), lambda i: (0, 0))
    wsp = pl.BlockSpec((len(POOL_WINDOWS), POOL_GC, POOL_GC), lambda i: (0, 0, 0))
    return pl.pallas_call(
        body, name="pool_fwd", grid=(t // tt,),
        in_specs=[row, _prev_halo_spec(tt, POOL_HALO, D_MODEL), vec, wsp, vec, vec, vec], out_specs=[row, row],
        out_shape=[jax.ShapeDtypeStruct(h0.shape, F32), jax.ShapeDtypeStruct(h0.shape, MXU_DTYPE)],
        compiler_params=_params("parallel"))(h0, h0, g_mix, pool_w, pool_b, pool_scale, g_ffn)


def _pool_bwd_call(h0, dh1, g_mix, pool_w, pool_b, pool_scale, hosted=None):
    t = h0.shape[0]
    tt = _tile(t, ROW_TILE)
    nt = t // tt
    per = tt // POOL_HALO

    def body(ins, outs, scratch, first, last):
        h_ref, halo_ref, d_ref, dnext_ref, gm_ref, w_ref, b_ref, sc_ref = ins
        gx_ref, dw_ref, dv_ref = outs
        i = pl.program_id(0)
        first()
        x, gm, sc = h_ref[...], gm_ref[...], sc_ref[...]
        top = jnp.where(i > 0, _rms(halo_ref[...], gm), 0.0)
        ext = jnp.concatenate([top, _rms(x, gm)], axis=0)
        ps = _pool_windows_causal(ext, i * tt)

        dy = d_ref[...]
        dy_ext = jnp.concatenate([dy, jnp.where(i < nt - 1, dnext_ref[...], 0.0)], axis=0)
        dpre_ext = dy_ext * sc
        t_ext = i * tt + lax.broadcasted_iota(jnp.int32, (tt + POOL_HALO, 1), 0)

        @pl.when(i == 0)
        def _():
            dw_ref[...] = jnp.zeros_like(dw_ref)
            dv_ref[...] = jnp.zeros_like(dv_ref)

        dhn, ypre = [], []
        for gi, win in enumerate(POOL_WINDOWS):
            cols = slice(gi * POOL_GC, (gi + 1) * POOL_GC)
            w = w_ref[gi]
            p16 = ps[gi].astype(MXU_DTYPE)
            dpre16 = dpre_ext[:, cols].astype(MXU_DTYPE)
            ypre.append(jnp.dot(p16, w, preferred_element_type=F32))
            dw_ref[gi] += lax.dot_general(p16, dpre16[:tt], (((0,), (0,)), ((), ())), preferred_element_type=F32)
            dp_ext = lax.dot_general(dpre16, w, (((1,), (1,)), ((), ())), preferred_element_type=F32)
            s = dp_ext / jnp.minimum(t_ext + 1, win).astype(F32)
            sh = 1
            while sh < win:
                s = s + _shift_up(s, sh)
                sh *= 2
            dhn.append(s[:tt] - dp_ext[:tt])
        dhn = jnp.concatenate(dhn, axis=1)
        ypre = jnp.concatenate(ypre, axis=1) + b_ref[...]
        dx, dgt = _rms_bwd(x, gm, dhn)
        gx_ref[...] = dy + dx
        dv_ref[0:1, :] += jnp.sum(dpre_ext[:tt], axis=0, keepdims=True)
        dv_ref[1:2, :] += jnp.sum(dy * ypre, axis=0, keepdims=True)
        dv_ref[2:3, :] += jnp.sum(dgt, axis=0, keepdims=True)
        last()

    row = pl.BlockSpec((tt, D_MODEL), lambda i: (i, 0))
    vec = pl.BlockSpec((1, D_MODEL), lambda i: (0, 0))
    nxt = pl.BlockSpec((POOL_HALO, D_MODEL), lambda i: (jnp.minimum((i + 1) * per, t // POOL_HALO - 1), 0))
    wsp = pl.BlockSpec((len(POOL_WINDOWS), POOL_GC, POOL_GC), lambda i: (0, 0, 0))
    return _call(
        body, hosted, name="pool_bwd", grid=(nt,),
        in_specs=[row, _prev_halo_spec(tt, POOL_HALO, D_MODEL), row, nxt, vec, wsp, vec, vec],
        out_specs=[row, wsp, pl.BlockSpec((8, D_MODEL), lambda i: (0, 0))],
        out_shape=[jax.ShapeDtypeStruct(h0.shape, F32),
                   jax.ShapeDtypeStruct((len(POOL_WINDOWS), POOL_GC, POOL_GC), F32),
                   jax.ShapeDtypeStruct((8, D_MODEL), F32)],
        operands=(h0, h0, dh1, dh1, g_mix, pool_w, pool_b, pool_scale), semantics=("arbitrary",))


CONV_ROWS = 256
CONV_SUB = 64
LANES = 128


def _causal_conv(u_ext, w, b):
    return w[0:1] * _shift_down(u_ext, 2) + w[1:2] * _shift_down(u_ext, 1) + w[2:3] * u_ext + b


def _fold8(x):
    acc = x[0:8]
    for r in range(8, x.shape[0], 8):
        acc = acc + x[r:r + 8]
    return acc


def _conv_glu_fwd_call(u, conv_w, conv_b, name, hosted=None):
    _, t, f = u.shape
    tt = _tile(t, CONV_ROWS)
    sub = _tile(tt, CONV_SUB)
    per = tt // CONV_HALO

    def body(ins, outs, scratch, first, last):
        u_ref, up_ref, w_ref, b_ref = ins
        z_ref, ext_ref = outs[0], scratch[0]
        i = pl.program_id(0)
        first()
        ext_ref[:, :CONV_HALO, :] = jnp.where(i > 0, up_ref[...], 0.0)
        ext_ref[:, CONV_HALO:, :] = u_ref[...]

        def chunk(j, carry):
            lanes = pl.ds(pl.multiple_of(j * LANES, LANES), LANES)
            w, b = w_ref[:, :, lanes], b_ref[:, lanes]
            for s in range(tt // sub):
                rows = pl.ds(s * sub, sub + CONV_HALO)
                c = [_causal_conv(ext_ref[hf, rows, lanes], w[:, hf, :], b[hf:hf + 1, :])[CONV_HALO:] for hf in range(2)]
                z_ref[pl.ds(s * sub, sub), lanes] = (c[0] * c[1] * _sigmoid(c[1])).astype(z_ref.dtype)
            return carry

        lax.fori_loop(0, f // LANES, chunk, 0)
        last()

    res = _call(
        body, hosted, name=name, grid=(t // tt,),
        in_specs=[pl.BlockSpec((2, tt, f), lambda i: (0, i, 0)),
                  pl.BlockSpec((2, CONV_HALO, f), lambda i: (0, jnp.maximum(i * per - 1, 0), 0)),
                  pl.BlockSpec((3, 2, f), lambda i: (0, 0, 0)), pl.BlockSpec((2, f), lambda i: (0, 0))],
        out_specs=[pl.BlockSpec((tt, f), lambda i: (i, 0))], out_shape=[jax.ShapeDtypeStruct((t, f), MXU_DTYPE)],
        operands=(u, u, conv_w, conv_b), semantics=("parallel",),
        scratch_shapes=[pltpu.VMEM((2, CONV_HALO + tt, f), F32)])
    return res if hosted else res[0]


def _conv_glu_bwd_call(u, dz, conv_w, conv_b, name, hosted=None):
    _, t, f = u.shape
    tt = _tile(t, CONV_ROWS)
    sub = _tile(tt, CONV_SUB)
    nt = t // tt
    per = tt // CONV_HALO
    halo = CONV_HALO

    def body(ins, outs, scratch, first, last):
        u_ref, up_ref, un_ref, dz_ref, dzn_ref, w_ref, b_ref = ins
        du_ref, dw_ref, db_ref = outs
        uext, dzext, dwacc, dbacc = scratch
        i = pl.program_id(0)
        first()
        uext[:, :halo, :] = jnp.where(i > 0, up_ref[...], 0.0)
        uext[:, halo:halo + tt, :] = u_ref[...]
        uext[:, halo + tt:, :] = un_ref[...]
        dzext[:halo, :] = jnp.zeros((halo, f), F32)
        dzext[halo:halo + tt, :] = dz_ref[...].astype(F32)
        dzext[halo + tt:, :] = jnp.where(i < nt - 1, dzn_ref[...].astype(F32), 0.0)

        @pl.when(i == 0)
        def _():
            dwacc[...] = jnp.zeros_like(dwacc)
            dbacc[...] = jnp.zeros_like(dbacc)

        main = slice(halo, halo + sub)

        def chunk(j, carry):
            lanes = pl.ds(pl.multiple_of(j * LANES, LANES), LANES)
            w, b = w_ref[:, :, lanes], b_ref[:, lanes]
            for s in range(tt // sub):
                rows = pl.ds(s * sub, sub + 2 * halo)
                ue = [uext[hf, rows, lanes] for hf in range(2)]
                a, g = [_causal_conv(ue[hf], w[:, hf, :], b[hf:hf + 1, :]) for hf in range(2)]
                sg = _sigmoid(g)
                dzs = dzext[rows, lanes]
                dc = [dzs * g * sg, dzs * a * sg * (1.0 + g * (1.0 - sg))]
                for hf in range(2):
                    d = dc[hf]
                    d1, d2 = _shift_up(d, 1), _shift_up(d, 2)
                    du = w[2:3, hf, :] * d + w[1:2, hf, :] * d1 + w[0:1, hf, :] * d2
                    du_ref[hf, pl.ds(s * sub, sub), lanes] = du[main].astype(du_ref.dtype)
                    um = ue[hf][main]
                    for k, dd in enumerate((d2, d1, d)):
                        dwacc[k, hf, :, lanes] += _fold8(dd[main] * um)
                    dbacc[hf, :, lanes] += _fold8(d[main])
            return carry

        lax.fori_loop(0, f // LANES, chunk, 0)

        @pl.when(i == nt - 1)
        def _():
            dw_ref[...] = jnp.sum(dwacc[...], axis=2)
            db_ref[...] = jnp.sum(dbacc[...], axis=1)

        last()

    end = t // halo - 1
    return _call(
        body, hosted, name=name, grid=(nt,),
        in_specs=[pl.BlockSpec((2, tt, f), lambda i: (0, i, 0)),
                  pl.BlockSpec((2, halo, f), lambda i: (0, jnp.maximum(i * per - 1, 0), 0)),
                  pl.BlockSpec((2, halo, f), lambda i: (0, jnp.minimum((i + 1) * per, end), 0)),
                  pl.BlockSpec((tt, f), lambda i: (i, 0)),
                  pl.BlockSpec((halo, f), lambda i: (jnp.minimum((i + 1) * per, end), 0)),
                  pl.BlockSpec((3, 2, f), lambda i: (0, 0, 0)), pl.BlockSpec((2, f), lambda i: (0, 0))],
        out_specs=[pl.BlockSpec((2, tt, f), lambda i: (0, i, 0)),
                   pl.BlockSpec((3, 2, f), lambda i: (0, 0, 0)), pl.BlockSpec((2, f), lambda i: (0, 0))],
        out_shape=[jax.ShapeDtypeStruct(u.shape, MXU_DTYPE), jax.ShapeDtypeStruct((3, 2, f), F32),
                   jax.ShapeDtypeStruct((2, f), F32)],
        operands=(u, u, u, dz, dz, conv_w, conv_b), semantics=("arbitrary",),
        scratch_shapes=[pltpu.VMEM((2, tt + 2 * halo, f), F32), pltpu.VMEM((tt + 2 * halo, f), F32),
                        pltpu.VMEM((3, 2, 8, f), F32), pltpu.VMEM((2, 8, f), F32)])


def _kv_even_odd(band, k):
    pair, parity = divmod(k, 2)
    blk = band[:, 128 * pair:128 * (pair + 1)].astype(F32)
    lane = lax.broadcasted_iota(jnp.int32, blk.shape, 1)
    own = jnp.where((lane >= HEAD_DIM) == (parity == 1), blk, 0.0)
    swapped = pltpu.roll(own, HEAD_DIM, axis=1)
    even, odd = (own, swapped) if parity == 0 else (swapped, own)
    return even.astype(MXU_DTYPE), odd.astype(MXU_DTYPE)


def _stack_pairs(ref, k):
    return jnp.concatenate([ref[:, 256 * k:256 * k + 128], ref[:, 256 * k + 128:256 * k + 256]], axis=0)


_GROUP_ORDER = (0, 2, 1, 3)


def _attn_fill_tables(bias_ref):
    shape = (4 * ATT_BLOCK, 2 * ATT_BLOCK)
    row = lax.broadcasted_iota(jnp.int32, shape, 0)
    col = lax.broadcasted_iota(jnp.int32, shape, 1)
    dist = (row & (ATT_BLOCK - 1)) + ATT_BLOCK - col
    ok = (dist >= 0) & (dist < ATT_BLOCK)
    distf = dist.astype(F32)
    rb = lax.broadcasted_iota(jnp.int32, (shape[0], 1), 0) // ATT_BLOCK
    for k in range(N_KV_HEADS):
        slope = jnp.zeros((shape[0], 1), F32)
        for r, gq in enumerate(_GROUP_ORDER):
            slope = jnp.where(rb == r, ALIBI_SLOPES[4 * k + gq], slope)
        bias = jnp.where(ok, -slope * distf, -jnp.inf)
        bias_ref[0, k] = bias
        bias_ref[1, k] = jnp.where(col >= ATT_BLOCK, bias, -jnp.inf)


def _attn_probs(qs, k_even, k_odd, bias, k, sink_ref):
    nt_dims = (((1,), (1,)), ((), ()))
    s = jnp.concatenate([lax.dot_general(qs, k_even, nt_dims, preferred_element_type=F32),
                         lax.dot_general(qs, k_odd, nt_dims, preferred_element_type=F32)], axis=0) + bias
    rb = lax.broadcasted_iota(jnp.int32, (s.shape[0], 1), 0) // ATT_BLOCK
    sink = jnp.zeros((s.shape[0], 1), F32)
    for r, gq in enumerate(_GROUP_ORDER):
        sink = jnp.where(rb == r, sink_ref[4 * k + gq], sink)
    m = jnp.maximum(jnp.max(s, axis=-1, keepdims=True), sink)
    p = jnp.exp(s - m)
    es = jnp.exp(sink - m)
    return p, 1.0 / (jnp.sum(p, axis=-1, keepdims=True) + es), es


_ATTN_TABLES = [pltpu.VMEM((2, N_KV_HEADS, 4 * ATT_BLOCK, 2 * ATT_BLOCK), F32)]


def _attn_specs(with_do):
    prev = lambda n: jnp.maximum(n - 1, 0)
    specs = [pl.BlockSpec(memory_space=pltpu.SMEM),
             pl.BlockSpec((ATT_BLOCK, D_MODEL), lambda n: (n, 0)),
             pl.BlockSpec((ATT_BLOCK, 256), lambda n: (prev(n), 4)), pl.BlockSpec((ATT_BLOCK, 256), lambda n: (n, 4)),
             pl.BlockSpec((ATT_BLOCK, 256), lambda n: (prev(n), 5)), pl.BlockSpec((ATT_BLOCK, 256), lambda n: (n, 5))]
    if with_do:
        specs.append(pl.BlockSpec((ATT_BLOCK, D_MODEL), lambda n: (n, 0)))
    return specs


def _attn_fwd_call(qkv, sinks, hosted=None):
    t = qkv.shape[0]

    def body(ins, outs, scratch, first, last):
        sink_ref, q_ref, kp_ref, kc_ref, vp_ref, vc_ref = ins
        o_ref, bias_ref = outs[0], scratch[0]
        n = pl.program_id(0)
        first()

        @pl.when(n == 0)
        def _():
            _attn_fill_tables(bias_ref)

        first = 1 - jnp.minimum(n, 1)
        kband = jnp.concatenate([kp_ref[...], kc_ref[...]], axis=0)
        vband = jnp.concatenate([vp_ref[...], vc_ref[...]], axis=0)
        for k in range(N_KV_HEADS):
            k_even, k_odd = _kv_even_odd(kband, k)
            v_even, v_odd = _kv_even_odd(vband, k)
            qs = _stack_pairs(q_ref, k) * ATT_SCALE
            p, rdenom, _ = _attn_probs(qs, k_even, k_odd, bias_ref[first, k], k, sink_ref)
            probs = (p * rdenom).astype(MXU_DTYPE)
            o = (jnp.dot(probs[:256], v_even, preferred_element_type=F32)
                 + jnp.dot(probs[256:], v_odd, preferred_element_type=F32))
            o_ref[:, 256 * k:256 * k + 128] = o[:128].astype(o_ref.dtype)
            o_ref[:, 256 * k + 128:256 * k + 256] = o[128:].astype(o_ref.dtype)
        last()

    return _call(
        body, hosted, name="attn_fwd", grid=(t // ATT_BLOCK,), in_specs=_attn_specs(False),
        out_specs=[pl.BlockSpec((ATT_BLOCK, D_MODEL), lambda n: (n, 0))],
        out_shape=[jax.ShapeDtypeStruct((t, D_MODEL), MXU_DTYPE)], operands=(sinks, qkv, qkv, qkv, qkv, qkv),
        semantics=("arbitrary",), scratch_shapes=_ATTN_TABLES)


def _attn_bwd_call(qkv, sinks, do, hosted=None):
    t = qkv.shape[0]
    nb = t // ATT_BLOCK
    tn_dims = (((0,), (0,)), ((), ()))
    nt_dims = (((1,), (1,)), ((), ()))

    def to_native(even_part, odd_part, parity):
        lane = lax.broadcasted_iota(jnp.int32, even_part.shape, 1)
        lo = lane < HEAD_DIM
        e, o = jnp.where(lo, even_part, 0.0), jnp.where(lo, 0.0, odd_part)
        if parity == 0:
            return e + pltpu.roll(o, HEAD_DIM, axis=1)
        return pltpu.roll(e, HEAD_DIM, axis=1) + o

    def body(ins, outs, scratch, first, last):
        sink_ref, q_ref, kp_ref, kc_ref, vp_ref, vc_ref, do_ref = ins
        dq_ref, dk_ref, dv_ref, db_ref, dsink_ref = outs
        bias_ref = scratch[0]
        n = pl.program_id(0)
        first()

        @pl.when(n == 0)
        def _():
            _attn_fill_tables(bias_ref)
            dk_ref[...] = jnp.zeros_like(dk_ref)
            dv_ref[...] = jnp.zeros_like(dv_ref)
            db_ref[...] = jnp.zeros_like(db_ref)
            dsink_ref[...] = jnp.zeros_like(dsink_ref)

        kband = jnp.concatenate([kp_ref[...], kc_ref[...]], axis=0)
        vband = jnp.concatenate([vp_ref[...], vc_ref[...]], axis=0)
        dk_pairs = [jnp.zeros((2 * ATT_BLOCK, 128), F32), jnp.zeros((2 * ATT_BLOCK, 128), F32)]
        dv_pairs = [jnp.zeros((2 * ATT_BLOCK, 128), F32), jnp.zeros((2 * ATT_BLOCK, 128), F32)]
        sink_lane = lax.broadcasted_iota(jnp.int32, (8, 128), 1)
        sink_row = lax.broadcasted_iota(jnp.int32, (8, 128), 0)
        dsink = jnp.zeros((8, 128), F32)
        first = 1 - jnp.minimum(n, 1)
        for k in range(N_KV_HEADS):
            k_even, k_odd = _kv_even_odd(kband, k)
            v_even, v_odd = _kv_even_odd(vband, k)
            qs = _stack_pairs(q_ref, k) * ATT_SCALE
            dos = _stack_pairs(do_ref, k)
            p, rdenom, es = _attn_probs(qs, k_even, k_odd, bias_ref[first, k], k, sink_ref)
            probs = p * rdenom
            dp = jnp.concatenate([lax.dot_general(dos, v_even, nt_dims, preferred_element_type=F32),
                                  lax.dot_general(dos, v_odd, nt_dims, preferred_element_type=F32)], axis=0)
            delta = jnp.sum(probs * dp, axis=-1, keepdims=True)
            ds16 = (probs * (dp - delta)).astype(MXU_DTYPE)
            p16 = probs.astype(MXU_DTYPE)
            dsink_rows = -(es * rdenom) * delta
            for r, gq in enumerate(_GROUP_ORDER):
                tot = jnp.sum(dsink_rows[ATT_BLOCK * r:ATT_BLOCK * (r + 1)], axis=0, keepdims=True)
                dsink = dsink + jnp.where((sink_lane == 4 * k + gq) & (sink_row == 0), tot, 0.0)
            dqs = (jnp.dot(ds16[:256], k_even, preferred_element_type=F32)
                   + jnp.dot(ds16[256:], k_odd, preferred_element_type=F32)) * ATT_SCALE
            dq_ref[:, 256 * k:256 * k + 128] = dqs[:128].astype(dq_ref.dtype)
            dq_ref[:, 256 * k + 128:256 * k + 256] = dqs[128:].astype(dq_ref.dtype)
            db_ref[:, 256 * k:256 * k + 128] += jnp.sum(dqs[:128], axis=0, keepdims=True)
            db_ref[:, 256 * k + 128:256 * k + 256] += jnp.sum(dqs[128:], axis=0, keepdims=True)
            pair, parity = divmod(k, 2)
            dk_pairs[pair] = dk_pairs[pair] + to_native(
                lax.dot_general(ds16[:256], qs, tn_dims, preferred_element_type=F32),
                lax.dot_general(ds16[256:], qs, tn_dims, preferred_element_type=F32), parity)
            dv_pairs[pair] = dv_pairs[pair] + to_native(
                lax.dot_general(p16[:256], dos, tn_dims, preferred_element_type=F32),
                lax.dot_general(p16[256:], dos, tn_dims, preferred_element_type=F32), parity)
        dsink_ref[...] += dsink
        dk_band = jnp.concatenate(dk_pairs, axis=1)
        dv_band = jnp.concatenate(dv_pairs, axis=1)
        cur = pl.multiple_of(n * ATT_BLOCK, ATT_BLOCK)
        dk_ref[pl.ds(cur, ATT_BLOCK), :] += dk_band[ATT_BLOCK:]
        dv_ref[pl.ds(cur, ATT_BLOCK), :] += dv_band[ATT_BLOCK:]

        @pl.when(n > 0)
        def _():
            prv = pl.multiple_of((n - 1) * ATT_BLOCK, ATT_BLOCK)
            dk_ref[pl.ds(prv, ATT_BLOCK), :] += dk_band[:ATT_BLOCK]
            dv_ref[pl.ds(prv, ATT_BLOCK), :] += dv_band[:ATT_BLOCK]

        @pl.when(n == nb - 1)
        def _():
            db_ref[:, 1024:1280] = jnp.sum(dk_ref[...], axis=0, keepdims=True)
            db_ref[:, 1280:1536] = jnp.sum(dv_ref[...], axis=0, keepdims=True)

        last()

    whole = lambda shape: pl.BlockSpec(shape, lambda n: (0, 0))
    return _call(
        body, hosted, name="attn_bwd", grid=(nb,), in_specs=_attn_specs(True),
        out_specs=[pl.BlockSpec((ATT_BLOCK, D_MODEL), lambda n: (n, 0)), whole((t, 256)), whole((t, 256)),
                   whole((1, QKV_DIM)), whole((8, 128))],
        out_shape=[jax.ShapeDtypeStruct((t, D_MODEL), MXU_DTYPE), jax.ShapeDtypeStruct((t, 256), F32),
                   jax.ShapeDtypeStruct((t, 256), F32), jax.ShapeDtypeStruct((1, QKV_DIM), F32),
                   jax.ShapeDtypeStruct((8, 128), F32)],
        operands=(sinks, qkv, qkv, qkv, qkv, qkv, do), semantics=("arbitrary",), scratch_shapes=_ATTN_TABLES)


FF_TILE = 1408


def _ffn_fwd(hn, h, w, conv_w, conv_b, layer, hooks=None):
    t = hn.shape[0]
    tm = _tile(t, 512)
    halves = D_FF // FF_TILE
    u, = _ride(hooks, "in", lambda hosted: _matmul(
        hn, w["w_in"], mode="nn", name=f"ffn{layer}_in", grid=(2 * halves, t // tm, 1),
        a_spec=pl.BlockSpec((tm, D_MODEL), lambda j, i, k: (i, 0)),
        b_spec=pl.BlockSpec((None, D_MODEL, FF_TILE), lambda j, i, k: (layer, 0, j)),
        o_spec=pl.BlockSpec((None, tm, FF_TILE), lambda j, i, k: (j // halves, i, j % halves)),
        out_shape=jax.ShapeDtypeStruct((2, t, D_FF), F32), hosted=hosted))
    z, = _ride(hooks, "glu", lambda hosted: _conv_glu_fwd_call(u, conv_w, conv_b, f"ffn{layer}_glu", hosted=hosted))
    h_out, = _ride(hooks, "out", lambda hosted: _mm_nn(z, w["w_out"], f"ffn{layer}_out", b_lead=layer, residual=h,
                                                       tk=D_FF, hosted=hosted))
    return h_out, u, z


def _ffn_bwd(dh, hn, u, z, w_in, w_out, conv_w, conv_b, layer, g_in_acc, g_out_acc, hooks=None):
    t = hn.shape[0]
    dz, = _ride(hooks, "dz", lambda hosted: _mm_nt(dh, w_out, f"ffn{layer}_dz", b_lead=layer, out_dtype=MXU_DTYPE,
                                                   tn=FF_TILE, hosted=hosted))
    g_out = _mm_tn(z, dh, f"ffn{layer}_dwout", lead=layer, n_lead=2, into=g_out_acc, tm=WGRAD_TILE, tk=t)
    du, dcw, dcb = _ride(hooks, "dglu", lambda hosted: _conv_glu_bwd_call(u, dz, conv_w, conv_b, f"ffn{layer}_dglu",
                                                                          hosted=hosted), {"g_out": g_out})
    per_half = D_FF // WGRAD_TILE
    g_in, = _ride(hooks, "dwin", lambda hosted: _matmul(
        hn, du, mode="tn", name=f"ffn{layer}_dwin", grid=(1, 2 * per_half, 1),
        a_spec=pl.BlockSpec((t, D_MODEL), lambda i, j, k: (0, 0)),
        b_spec=pl.BlockSpec((None, t, WGRAD_TILE), lambda i, j, k: (j // per_half, 0, j % per_half)),
        o_spec=pl.BlockSpec((None, D_MODEL, WGRAD_TILE), lambda i, j, k: (layer, 0, j)),
        out_shape=jax.ShapeDtypeStruct((2, D_MODEL, 2 * D_FF), F32), into=g_in_acc, hosted=hosted))
    dhn, = _ride(hooks, "dhn", lambda hosted: _ffn_dhn_call(du, w_in, layer, hosted), {"g_in": g_in, "g_out": g_out})
    return dhn, g_in, g_out, dcw, dcb


WGRAD_TILE = 256


def _ffn_dhn_call(du, w_in, layer, hosted=None):
    _, t, f = du.shape
    tm = _tile(t, 512)
    nt_dims = (((1,), (1,)), ((), ()))

    def body(ins, outs, scratch, first, last):
        du_ref, w_ref = ins
        first()
        outs[0][...] = (lax.dot_general(du_ref[0], w_ref[:, :f], nt_dims, preferred_element_type=F32)
                        + lax.dot_general(du_ref[1], w_ref[:, f:], nt_dims, preferred_element_type=F32))
        last()

    return _call(
        body, hosted, name=f"ffn{layer}_dhn", grid=(t // tm,),
        in_specs=[pl.BlockSpec((2, tm, f), lambda i: (0, i, 0)),
                  pl.BlockSpec((None, D_MODEL, 2 * f), lambda i: (layer, 0, 0))],
        out_specs=[pl.BlockSpec((tm, D_MODEL), lambda i: (i, 0))], out_shape=[jax.ShapeDtypeStruct((t, D_MODEL), F32)],
        operands=(du, w_in), semantics=("parallel",))


def _local_step(x, target, w, *, norm_mix, norm_ffn, norm_f, pool_b, pool_scale, b_qkv, sinks, conv_w, conv_b,
                late_gather=None, reducer=None):
    cw = [conv_w[l].reshape(3, 2, D_FF) for l in range(2)]
    cb = [conv_b[l].reshape(2, D_FF) for l in range(2)]
    gm = [norm_mix[l:l + 1] for l in range(2)]
    gf = [norm_ffn[l:l + 1] for l in range(2)]

    w = dict(w)
    fwd_hooks = {}
    if late_gather:
        def gather_hook(stage):
            return (lambda env: late_gather(stage, w)), (lambda got: w.update(zip(_late_names(stage), got)))

        fwd_hooks = {stage: gather_hook(stage) for stage in GATHER_RIDES}

    h1, hn_f0 = _pool_fwd_call(x, gm[0], w["pool_w"], pool_b, pool_scale, gf[0])
    h2, u0, z0 = _ffn_fwd(hn_f0, h1, w, cw[0], cb[0], 0, hooks=fwd_hooks)
    hn_a = _rms_fwd_call(h2, gm[1], "rms_attn", MXU_DTYPE)
    qkv = _mm_nn(hn_a, w["w_qkv"], "attn_qkv", bias=b_qkv, out_dtype=MXU_DTYPE, tn=768)
    o, = _ride(fwd_hooks, "attn_fwd", lambda hosted: _attn_fwd_call(qkv, sinks, hosted))
    h3, = _ride(fwd_hooks, "attn_out", lambda hosted: _mm_nn(o, w["w_o"], "attn_out", residual=h2, hosted=hosted))
    hn_f1 = _rms_fwd_call(h3, gf[1], "rms_ffn1", MXU_DTYPE)
    h4, u1, z1 = _ffn_fwd(hn_f1, h3, w, cw[1], cb[1], 1)
    loss, dh, dh16, d_norm_f = _loss_call(h4, norm_f, target)

    rd = reducer
    first, mid, last = _Reducer.FIRST, _Reducer.MID, _Reducer.LAST
    hooks1 = hooks_attn = hooks0 = hooks_pool = None
    if rd:
        hooks1 = {"dhn": (lambda env: rd.swap(first, {"w_in": env["g_in"], "w_out": env["g_out"]}),
                          lambda got: rd.pair_sums(first, got))}
        hooks_attn = {"attn_bwd": (lambda env: rd.scatter(first), lambda got: rd.chip_sums(first, got))}

        def mid_and_last(got):
            rd.shared(mid, got[:len(mid)])
            rd.pair_sums(last, got[len(mid):])

        hooks0 = {
            "dz": (lambda env: rd.share(first), lambda got: rd.shared(first, got)),
            "dglu": (lambda env: rd.swap(mid, {"w_out": env["g_out"], "w_qkv": g_qkv, "w_o": g_o4}),
                     lambda got: rd.pair_sums(mid, got)),
            "dwin": (lambda env: rd.scatter(mid), lambda got: rd.chip_sums(mid, got)),
            "dhn": (lambda env: _both(rd.share(mid), rd.swap(last, {"w_in": env["g_in"]})), mid_and_last),
        }
        hooks_pool = {"pool_bwd": (lambda env: rd.scatter(last), lambda got: rd.chip_sums(last, got))}
    dhn, g_in, g_out, dcw1, dcb1 = _ffn_bwd(dh16, hn_f1, u1, z1, w["w_in"], w["w_out"], cw[1], cb[1], 1, None, None,
                                            hooks=hooks1)
    dh, dh16, d_gf1 = _rms_bwd_call(h3, gf[1], dhn, dh, "rms_ffn1_bwd")
    do = _mm_nt(dh16, w["w_o"], "attn_do", out_dtype=MXU_DTYPE)
    g_o = _mm_tn(o, dh16, "attn_dwo")
    g_o4 = g_o.reshape(N_CHIPS, 2, W_O_ROWS // 2, D_MODEL)
    dq, dk, dv, d_bqkv, d_sinks = _ride(hooks_attn, "attn_bwd", lambda hosted: _attn_bwd_call(qkv, sinks, do, hosted))
    dqkv = jnp.concatenate([dq, dk.astype(MXU_DTYPE), dv.astype(MXU_DTYPE)], axis=1)
    g_qkv = _mm_tn(hn_a, dqkv, "attn_dwqkv", tn=768)
    dhn = _mm_nt(dqkv, w["w_qkv"], "attn_dhn", tk=768)
    dh, dh16, d_gm1 = _rms_bwd_call(h2, gm[1], dhn, dh, "rms_attn_bwd")
    dhn, g_in, g_out, dcw0, dcb0 = _ffn_bwd(dh16, hn_f0, u0, z0, w["w_in"], w["w_out"], cw[0], cb[0], 0, g_in, g_out,
                                            hooks=hooks0)
    dh, _, d_gf0 = _rms_bwd_call(h1, gf[0], dhn, dh, "rms_ffn0_bwd")
    grad_x, g_pool, dvec = _ride(hooks_pool, "pool_bwd", lambda hosted: _pool_bwd_call(
        x, dh, gm[0], w["pool_w"], pool_b, pool_scale, hosted))
    if rd:
        rd.shared(last, _alone(rd.share(last), "share_last"))

    grads = {
        "w_in": g_in, "w_out": g_out, "w_qkv": g_qkv, "w_o": g_o, "pool_w": g_pool,
        "pool_b": dvec[0:1], "pool_scale": dvec[1:2], "b_qkv": d_bqkv, "sinks": d_sinks[0:1, :N_HEADS],
        "norm_mix": jnp.concatenate([dvec[2:3], d_gm1], axis=0), "norm_ffn": jnp.concatenate([d_gf0, d_gf1], axis=0),
        "conv_w": jnp.stack([dcw0.reshape(3, 2 * D_FF), dcw1.reshape(3, 2 * D_FF)]),
        "conv_b": jnp.stack([dcb0.reshape(2 * D_FF), dcb1.reshape(2 * D_FF)]), "norm_f": d_norm_f,
    }
    return loss, grad_x, grads


W_IN_COLS = 2 * D_FF // N_CHIPS
W_OUT_ROWS = D_FF // N_CHIPS
QKV_COLS = QKV_DIM // N_CHIPS
W_O_ROWS = D_MODEL // N_CHIPS
POOL_ROWS = POOL_GC // N_CHIPS
BIG = ("w_in", "w_out", "w_qkv", "w_o", "pool_w")


def _mesh_place():
    x, y, c = lax.axis_index("x"), lax.axis_index("y"), lax.axis_index("c")
    chips = [(1 - x, y), (x, 1 - y), (1 - x, 1 - y)]
    return x, y, c, chips


def _col(k, width):
    return pl.ds(pl.multiple_of(k * width, 128), width)


def _row(k, height):
    return pl.ds(pl.multiple_of(k * height, 16), height)


def _full_piece(name, ref, k, h):
    if name == "w_in":
        return ref.at[h, :, _col(k, W_IN_COLS)]
    if name == "w_out":
        return ref.at[h, _row(k, W_OUT_ROWS), :]
    if name == "w_qkv":
        return ref.at[_row(h, D_MODEL // 2), _col(k, QKV_COLS)]
    if name == "w_o":
        return ref.at[_row(2 * k + h, W_O_ROWS // 2), :]
    return ref.at[pl.ds(2 * h, 2), _row(k, POOL_ROWS), :]


def _shard_half(name, ref, h):
    if name in ("w_in", "w_out"):
        return ref.at[h]
    if name == "w_qkv":
        return ref.at[_row(h, D_MODEL // 2), :]
    if name == "w_o":
        return ref.at[_row(h, W_O_ROWS // 2), :]
    return ref.at[pl.ds(2 * h, 2)]


FULL_SHAPES = {"w_in": (2, D_MODEL, 2 * D_FF), "w_out": (2, D_FF, D_MODEL), "w_qkv": (D_MODEL, QKV_DIM),
               "w_o": (D_MODEL, D_MODEL), "pool_w": (4, POOL_GC, POOL_GC)}
SHARD_SHAPES = {"w_in": (2, D_MODEL, W_IN_COLS), "w_out": (2, W_OUT_ROWS, D_MODEL), "w_qkv": (D_MODEL, QKV_COLS),
                "w_o": (W_O_ROWS, D_MODEL), "pool_w": (4, POOL_ROWS, POOL_GC)}
HALF_SHAPES = {"w_in": (D_MODEL, W_IN_COLS), "w_out": (W_OUT_ROWS, D_MODEL), "w_qkv": (D_MODEL // 2, QKV_COLS),
               "w_o": (W_O_ROWS // 2, D_MODEL), "pool_w": (2, POOL_ROWS, POOL_GC)}


def _place_shard(name, shard, me_arr):
    if name == "w_in":
        blk, grid = (1, 256, W_IN_COLS), (2, D_MODEL // 256)
        src, dst = (lambda a, b, me: (a, b, 0)), (lambda a, b, me: (a, b, me[0]))
    elif name == "w_out":
        blk, grid = (1, W_OUT_ROWS, D_MODEL), (2, 1)
        src, dst = (lambda a, b, me: (a, 0, 0)), (lambda a, b, me: (a, me[0], 0))
    elif name == "w_qkv":
        blk, grid = (256, QKV_COLS), (1, D_MODEL // 256)
        src, dst = (lambda a, b, me: (b, 0)), (lambda a, b, me: (b, me[0]))
    elif name == "w_o":
        blk, grid = (W_O_ROWS, D_MODEL), (1, 1)
        src, dst = (lambda a, b, me: (0, 0)), (lambda a, b, me: (me[0], 0))
    else:
        blk, grid = (4, POOL_ROWS, POOL_GC), (1, 1)
        src, dst = (lambda a, b, me: (0, 0, 0)), (lambda a, b, me: (0, me[0], 0))

    def body(me_ref, s_ref, o_ref):
        o_ref[...] = s_ref[...].astype(o_ref.dtype)

    return pl.pallas_call(
        body, name=f"place_{name}",
        grid_spec=pltpu.PrefetchScalarGridSpec(num_scalar_prefetch=1, grid=grid, in_specs=[pl.BlockSpec(blk, src)],
                                               out_specs=pl.BlockSpec(blk, dst)),
        out_shape=jax.ShapeDtypeStruct(FULL_SHAPES[name], WIRE_DTYPE), compiler_params=_params("parallel", "parallel"),
    )(me_arr, shard)


def _when_mover(half, c, fn):
    if half is None:
        fn(c)
    else:
        pl.when(c == half)(lambda: fn(half))


def _when_taker(half, c, fn):
    if half is None:
        fn(1 - c)
    else:
        pl.when(c != half)(lambda: fn(half))


def _remote(ref, sems, t, j, dev):
    send, recv = sems
    return pltpu.make_async_remote_copy(src_ref=ref, dst_ref=ref, send_sem=send.at[t, j], recv_sem=recv.at[t, j],
                                        device_id=dev, device_id_type=MESH)


def _gather_ici(items, refs, sems, finish):
    x, y, c, chips = _mesh_place()
    me = 2 * x + y
    for t, (nm, half) in enumerate(items):
        def go(h, t=t, nm=nm):
            mine = _full_piece(nm, refs[t], me, h)
            for j, (cx, cy) in enumerate(chips):
                if not finish:
                    _remote(mine, sems, t, j, (cx, cy, c)).start()
                else:
                    _remote(_full_piece(nm, refs[t], 2 * cx + cy, h), sems, t, j, (cx, cy, c)).wait_recv()
                    _remote(mine, sems, t, j, (cx, cy, c)).wait_send()

        _when_mover(half, c, go)


def _gather_pass(items, refs, sems, finish):
    x, y, c, chips = _mesh_place()
    sibling = (x, y, 1 - c)
    for t, (nm, half) in enumerate(items):
        def give(h, t=t, nm=nm):
            for j, (cx, cy) in enumerate(chips):
                got = _remote(_full_piece(nm, refs[t], 2 * cx + cy, h), sems, t, j, sibling)
                if not finish:
                    got.start()
                else:
                    got.wait_send()

        def take(h, t=t, nm=nm):
            for j, (cx, cy) in enumerate(chips):
                _remote(_full_piece(nm, refs[t], 2 * cx + cy, h), sems, t, j, sibling).wait_recv()

        _when_mover(half, c, give)
        if finish:
            _when_taker(half, c, take)


def _gather_sems(n_items):
    return [pltpu.SemaphoreType.DMA((n_items, 3)), pltpu.SemaphoreType.DMA((n_items, 3))]


EARLY_ITEMS = (("pool_w", None), ("w_in", 0))
_LATE_A = (("w_out", 0), ("w_qkv", None), ("w_o", None))
_LATE_B = (("w_in", 1),)
_LATE_C = (("w_out", 1),)
GATHER_RIDES = {"in": ((), _LATE_A), "glu": (_LATE_A, _LATE_B), "out": (_LATE_B, ()), "attn_fwd": ((), _LATE_C),
                "attn_out": (_LATE_C, ())}


def _names(items):
    return [nm for nm, _ in items]


def _gather_early(full, small):
    items = EARLY_ITEMS
    n = len(items)

    def body(*refs):
        small_in = refs[n]
        outs, small_out = refs[n + 1:2 * n + 1], refs[2 * n + 1]
        scratch = refs[2 * n + 2:]
        ici_sems, pass_sems = scratch[0:2], scratch[2:4]
        local_sem, small_send, small_recv = scratch[4:]
        x, y, c, chips = _mesh_place()
        me = 2 * x + y
        local = pltpu.make_async_copy(small_in, small_out.at[me], local_sem.at[0])
        local.start()
        small_sends = [pltpu.make_async_remote_copy(
            src_ref=small_in, dst_ref=small_out.at[me], send_sem=small_send.at[j], recv_sem=small_recv.at[j],
            device_id=(cx, cy, c), device_id_type=MESH) for j, (cx, cy) in enumerate(chips)]
        for cp in small_sends:
            cp.start()
        _gather_ici(items, outs, ici_sems, finish=False)
        _gather_ici(items, outs, ici_sems, finish=True)
        _gather_pass(items, outs, pass_sems, finish=False)
        _gather_pass(items, outs, pass_sems, finish=True)
        for j, (cx, cy) in enumerate(chips):
            got = small_out.at[2 * cx + cy]
            pltpu.make_async_remote_copy(src_ref=got, dst_ref=got, send_sem=small_send.at[j],
                                         recv_sem=small_recv.at[j], device_id=(cx, cy, c), device_id_type=MESH).wait_recv()
        for cp in small_sends:
            cp.wait_send()
        local.wait()

    arrays = [full[nm] for nm in _names(items)]
    out_shape = [jax.ShapeDtypeStruct(a.shape, a.dtype) for a in arrays]
    out_shape.append(jax.ShapeDtypeStruct((N_CHIPS,) + small.shape, F32))
    res = pl.pallas_call(
        body, name="gather_early", in_specs=[ANY] * (n + 1), out_specs=[ANY] * (n + 1), out_shape=out_shape,
        input_output_aliases={t: t for t in range(n)},
        scratch_shapes=_gather_sems(n) + _gather_sems(n) + [pltpu.SemaphoreType.DMA((1,)), pltpu.SemaphoreType.DMA((3,)),
                                                            pltpu.SemaphoreType.DMA((3,))],
    )(*arrays, small)
    return {**full, **dict(zip(_names(items), res[:n]))}, res[n]


def _late_gather_host(stage, w):
    passing, sending = GATHER_RIDES[stage]
    n_pass = len(passing)

    def run(src, refs, new, sems, finish):
        if passing:
            _gather_pass(passing, refs[:n_pass], sems[:2], finish)
        if sending:
            _gather_ici(sending, refs[n_pass:], sems[-2:], finish)

    sems = (_gather_sems(n_pass) if passing else []) + (_gather_sems(len(sending)) if sending else [])
    return _Hosted([], [w[nm] for nm in _late_names(stage)], [], sems, functools.partial(run, finish=False),
                   functools.partial(run, finish=True))


def _late_names(stage):
    passing, sending = GATHER_RIDES[stage]
    return _names(passing) + _names(sending)


def _other_half_src(name, ref, h):
    if name in ("w_in", "w_out"):
        return ref.at[h]
    if name == "w_qkv":
        return ref.at[_row(h, D_MODEL // 2), :]
    if name == "w_o":
        return ref.at[:, pl.ds(h, 1)]
    return ref.at[pl.ds(2 * h, 2)]


SLAB_SHAPES = {"w_in": (D_MODEL, 2 * D_FF), "w_out": (D_FF, D_MODEL), "w_qkv": (D_MODEL // 2, QKV_DIM),
               "w_o": (N_CHIPS, 1, W_O_ROWS // 2, D_MODEL), "pool_w": (2, POOL_GC, POOL_GC)}


def _item_tag(item):
    return item[0] if item[1] is None else f"{item[0]}{item[1]}"


def _reduce_swap(items, grads, slabs, sems, finish):
    x, y, c, _ = _mesh_place()
    send, recv = sems
    for t, (nm, half) in enumerate(items):
        def copy(h, t=t, nm=nm):
            return pltpu.make_async_remote_copy(
                src_ref=_other_half_src(nm, grads[t], h), dst_ref=slabs[t], send_sem=send.at[t], recv_sem=recv.at[t],
                device_id=(x, y, 1 - c), device_id_type=MESH)

        def give(h):
            if finish:
                copy(h).wait_send()
            else:
                copy(h).start()

        if half is None:
            give(1 - c)
            if finish:
                copy(c).wait_recv()
        else:
            pl.when(c != half)(lambda: give(half))
            if finish:
                pl.when(c == half)(lambda: copy(half).wait_recv())


def _reduce_scatter(items, sums, parts, sems, finish):
    _, _, c, chips = _mesh_place()
    send, recv = sems
    for t, (nm, half) in enumerate(items):
        def go(h, t=t, nm=nm):
            for j, (cx, cy) in enumerate(chips):
                cp = pltpu.make_async_remote_copy(
                    src_ref=_slab_piece(nm, sums[t], 2 * cx + cy), dst_ref=parts[t].at[j], send_sem=send.at[t, j],
                    recv_sem=recv.at[t, j], device_id=(cx, cy, c), device_id_type=MESH)
                if finish:
                    cp.wait_recv()
                    cp.wait_send()
                else:
                    cp.start()

        _when_mover(half, c, go)


def _reduce_share(items, shards, sems, finish):
    x, y, c, _ = _mesh_place()
    send, recv = sems
    for t, (nm, half) in enumerate(items):
        def copy(h, t=t, nm=nm):
            part = _shard_half(nm, shards[t], h)
            return pltpu.make_async_remote_copy(src_ref=part, dst_ref=part, send_sem=send.at[t], recv_sem=recv.at[t],
                                                device_id=(x, y, 1 - c), device_id_type=MESH)

        def give(h):
            if finish:
                copy(h).wait_send()
            else:
                copy(h).start()

        _when_mover(half, c, give)
        if finish:
            _when_taker(half, c, lambda h: copy(h).wait_recv())


def _pair_sems(n):
    return [pltpu.SemaphoreType.DMA((n,)), pltpu.SemaphoreType.DMA((n,))]


def _swap_host(items, grads):
    def run(src, arr, new, sems, finish):
        _reduce_swap(items, src, new, sems, finish)

    return _Hosted([grads[nm] for nm, _ in items], [], [jax.ShapeDtypeStruct(SLAB_SHAPES[nm], F32) for nm, _ in items],
                   _pair_sems(len(items)), functools.partial(run, finish=False), functools.partial(run, finish=True))


def _scatter_host(items, sums):
    def run(src, arr, new, sems, finish):
        _reduce_scatter(items, src, new, sems, finish)

    new = [jax.ShapeDtypeStruct((N_CHIPS - 1,) + HALF_SHAPES[nm], WIRE_DTYPE) for nm, _ in items]
    return _Hosted(list(sums), [], new, _gather_sems(len(items)), functools.partial(run, finish=False),
                   functools.partial(run, finish=True))


def _share_host(items, shards):
    def run(src, arr, new, sems, finish):
        _reduce_share(items, arr, sems, finish)

    return _Hosted([], list(shards), [], _pair_sems(len(items)), functools.partial(run, finish=False),
                   functools.partial(run, finish=True))


def _alone(hosted, name):
    def body(ins, outs, scratch, first, last):
        first()
        last()

    return _call(body, hosted, name=name, grid=(1,), in_specs=[], out_specs=[], out_shape=[], operands=(),
                 semantics=("arbitrary",))


def _pair_sum(item, grad, slab, place):
    name, half = item
    shape = SLAB_SHAPES[name]
    hsel = (lambda p: p[1]) if half is None else (lambda p: half)
    if name == "w_in":
        blk = (256, 2 * D_FF)
        grid = (D_MODEL // 256,)
        g_spec = pl.BlockSpec((None,) + blk, lambda i, p: (hsel(p), i, 0))
        s_spec = pl.BlockSpec(blk, lambda i, p: (i, 0))
    elif name == "w_out":
        blk = (W_OUT_ROWS, D_MODEL)
        grid = (N_CHIPS,)
        g_spec = pl.BlockSpec((None,) + blk, lambda i, p: (hsel(p), i, 0))
        s_spec = pl.BlockSpec(blk, lambda i, p: (i, 0))
    elif name == "w_qkv":
        grid = (1,)
        g_spec = pl.BlockSpec(shape, lambda i, p: (hsel(p), 0))
        s_spec = pl.BlockSpec(shape, lambda i, p: (0, 0))
    else:
        grid = (1,)
        g_spec = pl.BlockSpec(shape, lambda i, p: (0, hsel(p), 0, 0))
        s_spec = pl.BlockSpec(shape, lambda i, p: (0, 0, 0, 0))

    def body(p_ref, g_ref, s_ref, o_ref):
        @pl.when(hsel(p_ref) == p_ref[1])
        def _():
            o_ref[...] = (g_ref[...] + s_ref[...]).astype(o_ref.dtype)

    return pl.pallas_call(
        body, name=f"pair_sum_{_item_tag(item)}",
        grid_spec=pltpu.PrefetchScalarGridSpec(num_scalar_prefetch=1, grid=grid, in_specs=[g_spec, s_spec],
                                               out_specs=s_spec),
        out_shape=jax.ShapeDtypeStruct(shape, WIRE_DTYPE), compiler_params=_params("parallel"),
    )(place, grad, slab)


def _slab_piece(name, ref, k):
    if name == "w_in":
        return ref.at[:, _col(k, W_IN_COLS)]
    if name == "w_out":
        return ref.at[_row(k, W_OUT_ROWS), :]
    if name == "w_qkv":
        return ref.at[:, _col(k, QKV_COLS)]
    if name == "w_o":
        return ref.at[k, 0]
    return ref.at[:, _row(k, POOL_ROWS), :]


def _sum_chips(item, slab, parts, place, into=None):
    name, sel = item
    half = HALF_SHAPES[name]
    hsel = (lambda p: p[1]) if sel is None else (lambda p: sel)
    if name == "w_in":
        blk, grid = (256, W_IN_COLS), (D_MODEL // 256,)
        own = pl.BlockSpec(blk, lambda i, p: (i, p[0]))
        got = pl.BlockSpec((N_CHIPS - 1,) + blk, lambda i, p: (0, i, 0))
        out = pl.BlockSpec((None,) + blk, lambda i, p: (hsel(p), i, 0))
    elif name == "w_out":
        grid = (1,)
        own = pl.BlockSpec(half, lambda i, p: (p[0], 0))
        got = pl.BlockSpec((N_CHIPS - 1,) + half, lambda i, p: (0, 0, 0))
        out = pl.BlockSpec((None,) + half, lambda i, p: (hsel(p), 0, 0))
    elif name == "w_qkv":
        grid = (1,)
        own = pl.BlockSpec(half, lambda i, p: (0, p[0]))
        got = pl.BlockSpec((N_CHIPS - 1,) + half, lambda i, p: (0, 0, 0))
        out = pl.BlockSpec(half, lambda i, p: (hsel(p), 0))
    else:
        grid = (1,)
        own = pl.BlockSpec((None, None) + half, lambda i, p: (p[0], 0, 0, 0))
        got = pl.BlockSpec((N_CHIPS - 1,) + half, lambda i, p: (0, 0, 0))
        out = pl.BlockSpec(half, lambda i, p: (hsel(p), 0))

    def body(p_ref, own_ref, got_ref, *rest):
        o_ref = rest[-1]

        @pl.when(hsel(p_ref) == p_ref[1])
        def _():
            acc = own_ref[...].astype(F32)
            for j in range(N_CHIPS - 1):
                acc = acc + got_ref[j].astype(F32)
            o_ref[...] = acc

    in_specs, operands, aliases = [own, got], [place, slab, parts], {}
    if into is not None:
        in_specs.append(ANY)
        operands.append(into)
        aliases = {3: 0}
    return pl.pallas_call(
        body, name=f"sum_chips_{_item_tag(item)}",
        grid_spec=pltpu.PrefetchScalarGridSpec(num_scalar_prefetch=1, grid=grid, in_specs=in_specs, out_specs=out),
        out_shape=jax.ShapeDtypeStruct(SHARD_SHAPES[name], F32), input_output_aliases=aliases,
        compiler_params=_params("parallel"),
    )(*operands)


class _Reducer:
    FIRST = (("w_in", 1), ("w_out", 1))
    MID = (("w_out", 0), ("w_qkv", None), ("w_o", None))
    LAST = (("w_in", 0),)

    def __init__(self, place):
        self.place = place
        self.shards = {}
        self.state = {}

    def swap(self, items, grads):
        self.state[items] = {"grads": [grads[nm] for nm, _ in items]}
        return _swap_host(items, grads)

    def pair_sums(self, items, slabs):
        st = self.state[items]
        st["sums"] = [_pair_sum(it, g, s, self.place) for it, g, s in zip(items, st["grads"], slabs)]

    def scatter(self, items):
        return _scatter_host(items, self.state[items]["sums"])

    def chip_sums(self, items, parts):
        for it, s, p in zip(items, self.state[items]["sums"], parts):
            self.shards[it[0]] = _sum_chips(it, s, p, self.place, into=self.shards.get(it[0]))

    def share(self, items):
        return _share_host(items, [self.shards[nm] for nm, _ in items])

    def shared(self, items, updated):
        self.shards.update(dict(zip(_names(items), updated)))


N_DEV = 8


def _all_reduce_small(vec, g_pool):
    def body(v_ref, gp_ref, o_ref, po_ref, slots, pslots, send_sems, recv_sems, psend, precv, hsend, hrecv):
        x, y, c, _ = _mesh_place()
        me = 4 * x + 2 * y + c

        def pool_piece(chip, h):
            return gp_ref.at[pl.ds(2 * h, 2), pl.ds(pl.multiple_of(chip * POOL_ROWS, POOL_ROWS), POOL_ROWS), :]

        slots[me] = v_ref[...]
        pslots[me] = pool_piece(2 * x + y, c)[...]
        copies = []
        for d in range(1, N_DEV):
            peer = (x ^ (d >> 2), y ^ ((d >> 1) & 1), c ^ (d & 1))
            copies.append(pltpu.make_async_remote_copy(
                src_ref=v_ref, dst_ref=slots.at[me], send_sem=send_sems.at[d - 1], recv_sem=recv_sems.at[d - 1],
                device_id=peer, device_id_type=MESH))
            copies.append(pltpu.make_async_remote_copy(
                src_ref=pool_piece(2 * peer[0] + peer[1], peer[2]), dst_ref=pslots.at[me], send_sem=psend.at[d - 1],
                recv_sem=precv.at[d - 1], device_id=peer, device_id_type=MESH))
        for cp in copies:
            cp.start()
        for d in range(1, N_DEV):
            peer = 4 * (x ^ (d >> 2)) + 2 * (y ^ ((d >> 1) & 1)) + (c ^ (d & 1))
            for buf, ss, rs in ((slots, send_sems, recv_sems), (pslots, psend, precv)):
                got = buf.at[peer]
                pltpu.make_async_remote_copy(src_ref=got, dst_ref=got, send_sem=ss.at[d - 1], recv_sem=rs.at[d - 1],
                                             device_id=(x, y, c), device_id_type=MESH).wait_recv()
        for cp in copies:
            cp.wait_send()
        acc, pacc = slots[0], pslots[0]
        for d in range(1, N_DEV):
            acc, pacc = acc + slots[d], pacc + pslots[d]
        o_ref[...] = acc
        po_ref[pl.ds(2 * c, 2)] = pacc
        mine, theirs = po_ref.at[pl.ds(2 * c, 2)], po_ref.at[pl.ds(2 * (1 - c), 2)]
        give = pltpu.make_async_remote_copy(src_ref=mine, dst_ref=mine, send_sem=hsend.at[0], recv_sem=hrecv.at[0],
                                            device_id=(x, y, 1 - c), device_id_type=MESH)
        give.start()
        pltpu.make_async_remote_copy(src_ref=theirs, dst_ref=theirs, send_sem=hsend.at[0], recv_sem=hrecv.at[0],
                                     device_id=(x, y, 1 - c), device_id_type=MESH).wait_recv()
        give.wait_send()

    vm = pl.BlockSpec(memory_space=pltpu.VMEM)
    piece = (2, POOL_ROWS, POOL_GC)
    return pl.pallas_call(
        body, name="all_reduce_small", in_specs=[vm, vm], out_specs=[vm, vm],
        out_shape=[jax.ShapeDtypeStruct(vec.shape, F32), jax.ShapeDtypeStruct(SHARD_SHAPES["pool_w"], F32)],
        scratch_shapes=[pltpu.VMEM((N_DEV,) + vec.shape, F32), pltpu.VMEM((N_DEV,) + piece, F32)]
        + [pltpu.SemaphoreType.DMA((N_DEV - 1,))] * 4 + [pltpu.SemaphoreType.DMA((1,))] * 2,
    )(vec, g_pool)


def _adamw_call(w, g, m, v, name):
    shape = w.shape
    cols = shape[-1]
    rows = w.size // cols
    tr = rows
    for cand in (256, 128, 64, 32, 16, 8):
        if rows > cand and rows % cand == 0 and cand * cols * 4 <= 2 * 1024 * 1024:
            tr = cand
            break

    def body(w_ref, g_ref, m_ref, v_ref, d_ref, nm_ref, nv_ref):
        gv = g_ref[...]
        mn = ADAM_B1 * m_ref[...] + (1.0 - ADAM_B1) * gv
        vn = ADAM_B2 * v_ref[...] + (1.0 - ADAM_B2) * jnp.square(gv)
        m_hat = mn / (1.0 - ADAM_B1 ** ADAM_STEP)
        v_hat = vn / (1.0 - ADAM_B2 ** ADAM_STEP)
        d_ref[...] = -ADAM_LR * (m_hat / (jnp.sqrt(v_hat) + ADAM_EPS) + ADAM_WD * w_ref[...])
        nm_ref[...] = mn
        nv_ref[...] = vn

    spec = pl.BlockSpec((tr, cols), lambda i: (i, 0))
    flat = lambda a: a.reshape(rows, cols)
    outs = pl.pallas_call(
        body, name=name, grid=(rows // tr,), in_specs=[spec] * 4, out_specs=[spec] * 3,
        out_shape=[jax.ShapeDtypeStruct((rows, cols), F32)] * 3, compiler_params=_params("parallel"),
    )(flat(w), flat(g), flat(m), flat(v))
    return tuple(o.reshape(shape) for o in outs)


WEIGHTS = ("pool_w", "pool_b", "pool_scale", "attn_w_qkv", "attn_b_qkv", "attn_sinks", "attn_w_o", "norm_mix",
           "norm_ffn", "ffn_w_in", "ffn_conv_w", "ffn_conv_b", "ffn_w_out", "norm_f")
SMALL_GATHER = 128 * 71
_SMALL_SIZES = (("pool_b", 1024), ("pool_scale", 1024), ("b_qkv", 1536), ("sinks", 128), ("norm_mix", 2048),
                ("norm_ffn", 2048), ("conv_w", 6 * 2 * D_FF), ("conv_b", 4 * D_FF), ("norm_f", 1024))
SMALL_COLS = 6784


def _pack_small(grads):
    flat = jnp.concatenate([jnp.pad(grads[nm].reshape(-1), (0, size - grads[nm].size)) for nm, size in _SMALL_SIZES])
    return jnp.pad(flat, (0, 8 * SMALL_COLS - flat.size)).reshape(8, SMALL_COLS)


def _unpack_small(vec):
    flat = vec.reshape(-1)
    out, off = {}, 0
    for nm, size in _SMALL_SIZES:
        out[nm] = flat[off:off + size]
        off += size
    return out


def kernel(x, pool_w, pool_b, pool_scale, attn_w_qkv, attn_b_qkv, attn_sinks, attn_w_o, norm_mix, norm_ffn, ffn_w_in, ffn_conv_w, ffn_conv_b, ffn_w_out, norm_f, loss_target, m_pool_w, m_pool_b, m_pool_scale, m_attn_w_qkv, m_attn_b_qkv, m_attn_sinks, m_attn_w_o, m_norm_mix, m_norm_ffn, m_ffn_w_in, m_ffn_conv_w, m_ffn_conv_b, m_ffn_w_out, m_norm_f, v_pool_w, v_pool_b, v_pool_scale, v_attn_w_qkv, v_attn_b_qkv, v_attn_sinks, v_attn_w_o, v_norm_mix, v_norm_ffn, v_ffn_w_in, v_ffn_conv_w, v_ffn_conv_b, v_ffn_w_out, v_norm_f):
    weights = dict(pool_w=pool_w, pool_b=pool_b, pool_scale=pool_scale, attn_w_qkv=attn_w_qkv, attn_b_qkv=attn_b_qkv,
                   attn_sinks=attn_sinks, attn_w_o=attn_w_o, norm_mix=norm_mix, norm_ffn=norm_ffn, ffn_w_in=ffn_w_in,
                   ffn_conv_w=ffn_conv_w, ffn_conv_b=ffn_conv_b, ffn_w_out=ffn_w_out, norm_f=norm_f)
    m_in = dict(pool_w=m_pool_w, pool_b=m_pool_b, pool_scale=m_pool_scale, attn_w_qkv=m_attn_w_qkv,
                attn_b_qkv=m_attn_b_qkv, attn_sinks=m_attn_sinks, attn_w_o=m_attn_w_o, norm_mix=m_norm_mix,
                norm_ffn=m_norm_ffn, ffn_w_in=m_ffn_w_in, ffn_conv_w=m_ffn_conv_w, ffn_conv_b=m_ffn_conv_b,
                ffn_w_out=m_ffn_w_out, norm_f=m_norm_f)
    v_in = dict(pool_w=v_pool_w, pool_b=v_pool_b, pool_scale=v_pool_scale, attn_w_qkv=v_attn_w_qkv,
                attn_b_qkv=v_attn_b_qkv, attn_sinks=v_attn_sinks, attn_w_o=v_attn_w_o, norm_mix=v_norm_mix,
                norm_ffn=v_norm_ffn, ffn_w_in=v_ffn_w_in, ffn_conv_w=v_ffn_conv_w, ffn_conv_b=v_ffn_conv_b,
                ffn_w_out=v_ffn_w_out, norm_f=v_norm_f)
    chip = 2 * lax.axis_index("x") + lax.axis_index("y")
    core = lax.axis_index("c")

    place = jnp.stack([chip, core]).astype(jnp.int32)
    shards = {"w_in": ffn_w_in, "w_out": ffn_w_out, "w_qkv": attn_w_qkv[0], "w_o": attn_w_o[0], "pool_w": pool_w[0]}
    placed = {nm: _place_shard(nm, shards[nm], place) for nm in BIG}
    small = jnp.concatenate([pool_b.reshape(-1), attn_b_qkv.reshape(-1), ffn_conv_w.reshape(-1)]).reshape(1, SMALL_GATHER)
    full, small_all = _gather_early(placed, small)
    small_all = small_all.reshape(N_CHIPS, SMALL_GATHER)
    pool_b_full = small_all[:, :256].reshape(N_CHIPS, 4, POOL_ROWS).transpose(1, 0, 2).reshape(1, D_MODEL)
    b_qkv_full = small_all[:, 256:640].reshape(1, QKV_DIM)
    conv_w_full = small_all[:, 640:].reshape(N_CHIPS, 2, 3, W_IN_COLS).transpose(1, 2, 0, 3).reshape(2, 3, 2 * D_FF)

    reducer = _Reducer(place)
    loss, grad_x, grads = _local_step(
        x[0], loss_target[0], full, norm_mix=norm_mix, norm_ffn=norm_ffn, norm_f=norm_f.reshape(1, D_MODEL),
        pool_b=pool_b_full, pool_scale=pool_scale, b_qkv=b_qkv_full, sinks=attn_sinks[0], conv_w=conv_w_full,
        conv_b=ffn_conv_b, late_gather=_late_gather_host, reducer=reducer)
    loss = lax.psum(loss[0, 0], ("x", "y", "c"))
    reduced = dict(reducer.shards)

    small_sum, reduced["pool_w"] = _all_reduce_small(_pack_small(grads), grads["pool_w"])
    small_g = _unpack_small(small_sum)
    pool_b_g = lax.dynamic_slice_in_dim(small_g["pool_b"].reshape(4, N_CHIPS, POOL_ROWS), chip, 1, axis=1)
    b_qkv_g = lax.dynamic_slice_in_dim(small_g["b_qkv"].reshape(N_CHIPS, QKV_COLS), chip, 1, axis=0)
    conv_w_g = lax.dynamic_slice_in_dim(small_g["conv_w"].reshape(2, 3, N_CHIPS, W_IN_COLS), chip, 1, axis=2)
    grad_w = {
        "pool_w": reduced["pool_w"].reshape(pool_w.shape), "pool_b": pool_b_g.reshape(pool_b.shape),
        "pool_scale": small_g["pool_scale"].reshape(pool_scale.shape),
        "attn_w_qkv": reduced["w_qkv"].reshape(attn_w_qkv.shape), "attn_b_qkv": b_qkv_g.reshape(attn_b_qkv.shape),
        "attn_sinks": small_g["sinks"][:N_HEADS].reshape(attn_sinks.shape),
        "attn_w_o": reduced["w_o"].reshape(attn_w_o.shape), "norm_mix": small_g["norm_mix"].reshape(norm_mix.shape),
        "norm_ffn": small_g["norm_ffn"].reshape(norm_ffn.shape), "ffn_w_in": reduced["w_in"],
        "ffn_conv_w": conv_w_g.reshape(ffn_conv_w.shape), "ffn_conv_b": small_g["conv_b"].reshape(ffn_conv_b.shape),
        "ffn_w_out": reduced["w_out"], "norm_f": small_g["norm_f"].reshape(norm_f.shape),
    }

    delta, new_m, new_v = {}, {}, {}
    for nm in WEIGHTS:
        delta[nm], new_m[nm], new_v[nm] = _adamw_call(weights[nm], grad_w[nm], m_in[nm], v_in[nm], f"adamw_{nm}")
    return (loss, grad_x.reshape(x.shape), *[grad_w[nm] for nm in WEIGHTS], *[delta[nm] for nm in WEIGHTS],
            *[new_m[nm] for nm in WEIGHTS], *[new_v[nm] for nm in WEIGHTS])
```

```python
import functools
from typing import Callable, NamedTuple

import jax
import jax.numpy as jnp
from jax import lax
from jax.experimental import pallas as pl
from jax.experimental.pallas import tpu as pltpu

F32 = jnp.float32
MXU_DTYPE = jnp.bfloat16
WIRE_DTYPE = jnp.bfloat16
U_DTYPE = jnp.bfloat16

D_MODEL = 1024
D_FF = 2816
QKV_DIM = 1536
HEAD_DIM = 64
N_HEADS = 16
N_KV_HEADS = 4
ATT_BLOCK = 128
POOL_WINDOWS = (2, 4, 8, 16)
POOL_GC = 256
POOL_HALO = 16
CONV_HALO = 8
RMS_EPS = 1e-6
ATT_SCALE = HEAD_DIM ** -0.5
ALIBI_SLOPES = tuple(2.0 ** (-8.0 / N_HEADS * (h + 1)) for h in range(N_HEADS))

ADAM_LR = 0.001
ADAM_B1 = 0.9
ADAM_B2 = 0.999
ADAM_EPS = 1e-08
ADAM_WD = 0.01
ADAM_STEP = 10

N_CHIPS = 4
MESH = pl.DeviceIdType.MESH
VMEM_LIMIT_BYTES = 56 * 1024 * 1024
ANY = pl.BlockSpec(memory_space=pl.ANY)


def _params(*semantics):
    return pltpu.CompilerParams(dimension_semantics=semantics, vmem_limit_bytes=VMEM_LIMIT_BYTES)


def _rms(x, g):
    return x * lax.rsqrt(jnp.mean(x * x, axis=-1, keepdims=True) + RMS_EPS) * g


def _rms_bwd(x, g, dy):
    rstd = lax.rsqrt(jnp.mean(x * x, axis=-1, keepdims=True) + RMS_EPS)
    xhat = x * rstd
    dxhat = dy * g
    dx = rstd * (dxhat - xhat * jnp.mean(dxhat * xhat, axis=-1, keepdims=True))
    return dx, dy * xhat


def _shift_down(x, s):
    return pltpu.roll(x, s, axis=0)


def _shift_up(x, s):
    return pltpu.roll(x, x.shape[0] - s, axis=0)


def _sigmoid(x):
    return 1.0 / (1.0 + jnp.exp(-x))


_DOT_DIMS = {"nn": ((1,), (0,)), "nt": ((1,), (1,)), "tn": ((0,), (0,))}


class _Hosted(NamedTuple):
    sources: list
    arrays: list
    new: list
    sems: list
    first: Callable
    last: Callable


def _call(body, hosted, *, name, grid, in_specs, out_specs, out_shape, operands, semantics, scratch_shapes=(),
          aliases=None):
    n_in, n_out, n_scr = len(in_specs), len(out_specs), len(scratch_shapes)
    aliases = dict(aliases or {})
    if hosted is None:
        def plain(*refs):
            body(refs[:n_in], refs[n_in:n_in + n_out], refs[n_in + n_out:], lambda: None, lambda: None)

        return pl.pallas_call(plain, name=name, grid=grid, in_specs=list(in_specs), out_specs=list(out_specs),
                              out_shape=list(out_shape), scratch_shapes=list(scratch_shapes),
                              input_output_aliases=aliases, compiler_params=_params(*semantics))(*operands)
    ns, na, nn = len(hosted.sources), len(hosted.arrays), len(hosted.new)

    def carrying(*refs):
        ins, src = refs[:n_in], refs[n_in:n_in + ns]
        o0 = n_in + ns + na
        outs, arr = refs[o0:o0 + n_out], refs[o0 + n_out:o0 + n_out + na]
        new = refs[o0 + n_out + na:o0 + n_out + na + nn]
        s0 = o0 + n_out + na + nn
        scratch, sems = refs[s0:s0 + n_scr], refs[s0 + n_scr:]
        ids = [pl.program_id(d) for d in range(len(grid))]
        is_first = functools.reduce(lambda p, q: p & q, [i == 0 for i in ids])
        is_last = functools.reduce(lambda p, q: p & q, [i == n - 1 for i, n in zip(ids, grid)])
        body(ins, outs, scratch, lambda: pl.when(is_first)(lambda: hosted.first(src, arr, new, sems)),
             lambda: pl.when(is_last)(lambda: hosted.last(src, arr, new, sems)))

    for t in range(na):
        aliases[n_in + ns + t] = n_out + t
    return pl.pallas_call(
        carrying, name=name, grid=grid, in_specs=list(in_specs) + [ANY] * (ns + na),
        out_specs=list(out_specs) + [ANY] * (na + nn),
        out_shape=list(out_shape) + [jax.ShapeDtypeStruct(a.shape, a.dtype) for a in hosted.arrays] + list(hosted.new),
        scratch_shapes=list(scratch_shapes) + list(hosted.sems), input_output_aliases=aliases,
        compiler_params=_params(*["arbitrary"] * len(grid)))(*operands, *hosted.sources, *hosted.arrays)


def _both(h1, h2):
    def run(which, src, arr, new, sems):
        cut = lambda seq, n: (seq[:n], seq[n:])
        (s1, s2), (a1, a2) = cut(src, len(h1.sources)), cut(arr, len(h1.arrays))
        (n1, n2), (m1, m2) = cut(new, len(h1.new)), cut(sems, len(h1.sems))
        getattr(h1, which)(s1, a1, n1, m1)
        getattr(h2, which)(s2, a2, n2, m2)

    return _Hosted(h1.sources + h2.sources, h1.arrays + h2.arrays, h1.new + h2.new, h1.sems + h2.sems,
                   functools.partial(run, "first"), functools.partial(run, "last"))


def _ride(hooks, stage, call, env=None):
    if not hooks or stage not in hooks:
        res = call(None)
        return list(res) if isinstance(res, (list, tuple)) else [res]
    make, done = hooks[stage]
    hosted = make(env)
    res = call(hosted)
    own = len(res) - len(hosted.arrays) - len(hosted.new)
    done(res[own:])
    return list(res[:own])


def _matmul(a, b, *, mode, name, grid, a_spec, b_spec, o_spec, out_shape, bias=None, bias_spec=None, residual=None,
            residual_spec=None, into=None, hosted=None):
    nk = grid[2]
    dims = (_DOT_DIMS[mode], ((), ()))
    acc_shape = tuple(d for d in o_spec.block_shape if d is not None)

    def body(ins, outs, scratch, first, last):
        a_ref, b_ref = ins[0], ins[1]
        pos = 2
        bias_ref = res_ref = None
        if bias is not None:
            bias_ref = ins[pos]
            pos += 1
        if residual is not None:
            res_ref = ins[pos]
        o_ref = outs[0]
        acc_ref = scratch[0] if nk > 1 else None
        first()
        prod = lax.dot_general(a_ref[...].astype(MXU_DTYPE), b_ref[...].astype(MXU_DTYPE), dims,
                               preferred_element_type=F32)

        def finish(acc):
            if bias_ref is not None:
                acc = acc + bias_ref[...]
            if res_ref is not None:
                acc = acc + res_ref[...]
            o_ref[...] = acc.astype(o_ref.dtype)

        if nk == 1:
            finish(prod)
        else:
            k = pl.program_id(2)

            @pl.when(k == 0)
            def _():
                acc_ref[...] = prod

            @pl.when(k > 0)
            def _():
                acc_ref[...] += prod

            @pl.when(k == nk - 1)
            def _():
                finish(acc_ref[...])

        last()

    operands, in_specs = [a, b], [a_spec, b_spec]
    if bias is not None:
        operands.append(bias)
        in_specs.append(bias_spec)
    if residual is not None:
        operands.append(residual)
        in_specs.append(residual_spec)
    aliases = {}
    if into is not None:
        aliases = {len(operands): 0}
        operands.append(into)
        in_specs.append(ANY)
    res = _call(body, hosted, name=name, grid=grid, in_specs=in_specs, out_specs=[o_spec], out_shape=[out_shape],
                operands=operands, semantics=("parallel", "parallel", "arbitrary"),
                scratch_shapes=[pltpu.VMEM(acc_shape, F32)] if nk > 1 else [], aliases=aliases)
    return res if hosted else res[0]


def _tile(n, want):
    t = min(n, want)
    assert n % t == 0, (n, want)
    return t


def _mm_nn(a, b, name, *, b_lead=None, bias=None, residual=None, out_dtype=F32, tm=1024, tn=1024, tk=1024,
           hosted=None):
    (m, k), n = a.shape, b.shape[-1]
    tm, tn, tk = _tile(m, tm), _tile(n, tn), _tile(k, tk)
    if b_lead is None:
        b_spec = pl.BlockSpec((tk, tn), lambda i, j, kk: (kk, j))
    else:
        b_spec = pl.BlockSpec((None, tk, tn), lambda i, j, kk: (b_lead, kk, j))
    return _matmul(
        a, b, mode="nn", name=name, grid=(m // tm, n // tn, k // tk),
        a_spec=pl.BlockSpec((tm, tk), lambda i, j, kk: (i, kk)), b_spec=b_spec,
        o_spec=pl.BlockSpec((tm, tn), lambda i, j, kk: (i, j)), out_shape=jax.ShapeDtypeStruct((m, n), out_dtype),
        bias=bias, bias_spec=pl.BlockSpec((1, tn), lambda i, j, kk: (0, j)),
        residual=residual, residual_spec=pl.BlockSpec((tm, tn), lambda i, j, kk: (i, j)), hosted=hosted)


def _mm_nt(a, b, name, *, b_lead=None, out_dtype=F32, tm=1024, tn=1024, tk=1024, hosted=None):
    (m, k), n = a.shape, b.shape[-2]
    tm, tn, tk = _tile(m, tm), _tile(n, tn), _tile(k, tk)
    if b_lead is None:
        b_spec = pl.BlockSpec((tn, tk), lambda i, j, kk: (j, kk))
    else:
        b_spec = pl.BlockSpec((None, tn, tk), lambda i, j, kk: (b_lead, j, kk))
    return _matmul(
        a, b, mode="nt", name=name, grid=(m // tm, n // tn, k // tk),
        a_spec=pl.BlockSpec((tm, tk), lambda i, j, kk: (i, kk)), b_spec=b_spec,
        o_spec=pl.BlockSpec((tm, tn), lambda i, j, kk: (i, j)), out_shape=jax.ShapeDtypeStruct((m, n), out_dtype),
        hosted=hosted)


def _mm_tn(a, b, name, *, lead=None, n_lead=None, into=None, tm=1024, tn=1024, tk=1024):
    (k, m), n = a.shape, b.shape[-1]
    tm, tn, tk = _tile(m, tm), _tile(n, tn), _tile(k, tk)
    if lead is None:
        o_spec = pl.BlockSpec((tm, tn), lambda i, j, kk: (i, j))
        out_shape = jax.ShapeDtypeStruct((m, n), F32)
    else:
        o_spec = pl.BlockSpec((None, tm, tn), lambda i, j, kk: (lead, i, j))
        out_shape = jax.ShapeDtypeStruct((n_lead, m, n), F32)
    return _matmul(
        a, b, mode="tn", name=name, grid=(m // tm, n // tn, k // tk),
        a_spec=pl.BlockSpec((tk, tm), lambda i, j, kk: (kk, i)),
        b_spec=pl.BlockSpec((tk, tn), lambda i, j, kk: (kk, j)), o_spec=o_spec, out_shape=out_shape, into=into)


ROW_TILE = 512


def _rms_fwd_call(h, g, name, out_dtype):
    t = h.shape[0]
    tt = _tile(t, ROW_TILE)

    def body(h_ref, g_ref, o_ref):
        o_ref[...] = _rms(h_ref[...], g_ref[...]).astype(o_ref.dtype)

    row = pl.BlockSpec((tt, D_MODEL), lambda i: (i, 0))
    vec = pl.BlockSpec((1, D_MODEL), lambda i: (0, 0))
    return pl.pallas_call(body, name=name, grid=(t // tt,), in_specs=[row, vec], out_specs=row,
                          out_shape=jax.ShapeDtypeStruct(h.shape, out_dtype), compiler_params=_params("parallel"))(h, g)


def _rms_bwd_call(h, g, dy, dres, name):
    t = h.shape[0]
    tt = _tile(t, ROW_TILE)

    def body(h_ref, g_ref, dy_ref, dres_ref, dx_ref, dx16_ref, dg_ref):
        dx, dgt = _rms_bwd(h_ref[...], g_ref[...], dy_ref[...].astype(F32))
        dx = dres_ref[...] + dx
        dx_ref[...] = dx
        dx16_ref[...] = dx.astype(dx16_ref.dtype)

        @pl.when(pl.program_id(0) == 0)
        def _():
            dg_ref[...] = jnp.zeros_like(dg_ref)

        dg_ref[...] += jnp.sum(dgt, axis=0, keepdims=True)

    row = pl.BlockSpec((tt, D_MODEL), lambda i: (i, 0))
    vec = pl.BlockSpec((1, D_MODEL), lambda i: (0, 0))
    return pl.pallas_call(
        body, name=name, grid=(t // tt,), in_specs=[row, vec, row, row], out_specs=[row, row, vec],
        out_shape=[jax.ShapeDtypeStruct(h.shape, F32), jax.ShapeDtypeStruct(h.shape, MXU_DTYPE),
                   jax.ShapeDtypeStruct((1, D_MODEL), F32)],
        compiler_params=_params("arbitrary"))(h, g, dy, dres)


def _loss_call(h, g, target):
    t = h.shape[0]
    tt = _tile(t, ROW_TILE)

    def body(h_ref, g_ref, t_ref, loss_ref, dh_ref, dh16_ref, dg_ref):
        x, gv = h_ref[...], g_ref[...]
        err = _rms(x, gv) - t_ref[...]
        dx, dgt = _rms_bwd(x, gv, err * (1.0 / D_MODEL))
        dh_ref[...] = dx
        dh16_ref[...] = dx.astype(dh16_ref.dtype)

        @pl.when(pl.program_id(0) == 0)
        def _():
            dg_ref[...] = jnp.zeros_like(dg_ref)
            loss_ref[...] = jnp.zeros_like(loss_ref)

        dg_ref[...] += jnp.sum(dgt, axis=0, keepdims=True)
        per_token = jnp.mean(err * err, axis=-1, keepdims=True)
        loss_ref[...] += 0.5 * jnp.sum(per_token, axis=0, keepdims=True)

    row = pl.BlockSpec((tt, D_MODEL), lambda i: (i, 0))
    vec = pl.BlockSpec((1, D_MODEL), lambda i: (0, 0))
    one = pl.BlockSpec((1, 1), lambda i: (0, 0))
    return pl.pallas_call(
        body, name="loss_head", grid=(t // tt,), in_specs=[row, vec, row], out_specs=[one, row, row, vec],
        out_shape=[jax.ShapeDtypeStruct((1, 1), F32), jax.ShapeDtypeStruct(h.shape, F32),
                   jax.ShapeDtypeStruct(h.shape, MXU_DTYPE), jax.ShapeDtypeStruct((1, D_MODEL), F32)],
        compiler_params=_params("arbitrary"))(h, g, target)


def _pool_windows_causal(ext, first_row):
    tt = ext.shape[0] - POOL_HALO
    t = first_row + lax.broadcasted_iota(jnp.int32, (tt, 1), 0)
    outs = []
    for gi, win in enumerate(POOL_WINDOWS):
        cols = slice(gi * POOL_GC, (gi + 1) * POOL_GC)
        s = ext[:, cols]
        sh = 1
        while sh < win:
            s = s + _shift_down(s, sh)
            sh *= 2
        count = jnp.minimum(t + 1, win).astype(F32)
        outs.append(s[POOL_HALO:] / count - ext[POOL_HALO:, cols])
    return outs


def _prev_halo_spec(tt, halo, width):
    per = tt // halo
    return pl.BlockSpec((halo, width), lambda i: (jnp.maximum(i * per - 1, 0), 0))


def _pool_fwd_call(h0, g_mix, pool_w, pool_b, pool_scale, g_ffn):
    t = h0.shape[0]
    tt = _tile(t, ROW_TILE)

    def body(h_ref, halo_ref, gm_ref, w_ref, b_ref, sc_ref, gf_ref, h1_ref, hn_ref):
        i = pl.program_id(0)
        x, gm = h_ref[...], gm_ref[...]
        top = jnp.where(i > 0, _rms(halo_ref[...], gm), 0.0)
        ext = jnp.concatenate([top, _rms(x, gm)], axis=0)
        ps = _pool_windows_causal(ext, i * tt)
        ys = [jnp.dot(p.astype(MXU_DTYPE), w_ref[gi], preferred_element_type=F32) for gi, p in enumerate(ps)]
        mix = (jnp.concatenate(ys, axis=1) + b_ref[...]) * sc_ref[...]
        h1 = x + mix
        h1_ref[...] = h1
        hn_ref[...] = _rms(h1, gf_ref[...]).astype(hn_ref.dtype)

    row = pl.BlockSpec((tt, D_MODEL), lambda i: (i, 0))
    vec = pl.BlockSpec((1, D_MODEL), lambda i: (0, 0))
    wsp = pl.BlockSpec((len(POOL_WINDOWS), POOL_GC, POOL_GC), lambda i: (0, 0, 0))
    return pl.pallas_call(
        body, name="pool_fwd", grid=(t // tt,),
        in_specs=[row, _prev_halo_spec(tt, POOL_HALO, D_MODEL), vec, wsp, vec, vec, vec], out_specs=[row, row],
        out_shape=[jax.ShapeDtypeStruct(h0.shape, F32), jax.ShapeDtypeStruct(h0.shape, MXU_DTYPE)],
        compiler_params=_params("parallel"))(h0, h0, g_mix, pool_w, pool_b, pool_scale, g_ffn)


def _pool_bwd_call(h0, dh1, g_mix, pool_w, pool_b, pool_scale, hosted=None):
    t = h0.shape[0]
    tt = _tile(t, ROW_TILE)
    nt = t // tt
    per = tt // POOL_HALO

    def body(ins, outs, scratch, first, last):
        h_ref, halo_ref, d_ref, dnext_ref, gm_ref, w_ref, b_ref, sc_ref = ins
        gx_ref, dw_ref, dv_ref = outs
        i = pl.program_id(0)
        first()
        x, gm, sc = h_ref[...], gm_ref[...], sc_ref[...]
        top = jnp.where(i > 0, _rms(halo_ref[...], gm), 0.0)
        ext = jnp.concatenate([top, _rms(x, gm)], axis=0)
        ps = _pool_windows_causal(ext, i * tt)

        dy = d_ref[...]
        dy_ext = jnp.concatenate([dy, jnp.where(i < nt - 1, dnext_ref[...], 0.0)], axis=0)
        dpre_ext = dy_ext * sc
        t_ext = i * tt + lax.broadcasted_iota(jnp.int32, (tt + POOL_HALO, 1), 0)

        @pl.when(i == 0)
        def _():
            dw_ref[...] = jnp.zeros_like(dw_ref)
            dv_ref[...] = jnp.zeros_like(dv_ref)

        dhn, ypre = [], []
        for gi, win in enumerate(POOL_WINDOWS):
            cols = slice(gi * POOL_GC, (gi + 1) * POOL_GC)
            w = w_ref[gi]
            p16 = ps[gi].astype(MXU_DTYPE)
            dpre16 = dpre_ext[:, cols].astype(MXU_DTYPE)
            ypre.append(jnp.dot(p16, w, preferred_element_type=F32))
            dw_ref[gi] += lax.dot_general(p16, dpre16[:tt], (((0,), (0,)), ((), ())), preferred_element_type=F32)
            dp_ext = lax.dot_general(dpre16, w, (((1,), (1,)), ((), ())), preferred_element_type=F32)
            s = dp_ext / jnp.minimum(t_ext + 1, win).astype(F32)
            sh = 1
            while sh < win:
                s = s + _shift_up(s, sh)
                sh *= 2
            dhn.append(s[:tt] - dp_ext[:tt])
        dhn = jnp.concatenate(dhn, axis=1)
        ypre = jnp.concatenate(ypre, axis=1) + b_ref[...]
        dx, dgt = _rms_bwd(x, gm, dhn)
        gx_ref[...] = dy + dx
        dv_ref[0:1, :] += jnp.sum(dpre_ext[:tt], axis=0, keepdims=True)
        dv_ref[1:2, :] += jnp.sum(dy * ypre, axis=0, keepdims=True)
        dv_ref[2:3, :] += jnp.sum(dgt, axis=0, keepdims=True)
        last()

    row = pl.BlockSpec((tt, D_MODEL), lambda i: (i, 0))
    vec = pl.BlockSpec((1, D_MODEL), lambda i: (0, 0))
    nxt = pl.BlockSpec((POOL_HALO, D_MODEL), lambda i: (jnp.minimum((i + 1) * per, t // POOL_HALO - 1), 0))
    wsp = pl.BlockSpec((len(POOL_WINDOWS), POOL_GC, POOL_GC), lambda i: (0, 0, 0))
    return _call(
        body, hosted, name="pool_bwd", grid=(nt,),
        in_specs=[row, _prev_halo_spec(tt, POOL_HALO, D_MODEL), row, nxt, vec, wsp, vec, vec],
        out_specs=[row, wsp, pl.BlockSpec((8, D_MODEL), lambda i: (0, 0))],
        out_shape=[jax.ShapeDtypeStruct(h0.shape, F32),
                   jax.ShapeDtypeStruct((len(POOL_WINDOWS), POOL_GC, POOL_GC), F32),
                   jax.ShapeDtypeStruct((8, D_MODEL), F32)],
        operands=(h0, h0, dh1, dh1, g_mix, pool_w, pool_b, pool_scale), semantics=("arbitrary",))


CONV_ROWS = 256
CONV_SUB = 64
LANES = 128


def _causal_conv(u_ext, w, b):
    return w[0:1] * _shift_down(u_ext, 2) + w[1:2] * _shift_down(u_ext, 1) + w[2:3] * u_ext + b


def _fold8(x):
    acc = x[0:8]
    for r in range(8, x.shape[0], 8):
        acc = acc + x[r:r + 8]
    return acc


def _conv_glu_fwd_call(u, conv_w, conv_b, name, hosted=None):
    _, t, f = u.shape
    tt = _tile(t, CONV_ROWS)
    sub = _tile(tt, CONV_SUB)
    per = tt // CONV_HALO

    def body(ins, outs, scratch, first, last):
        u_ref, up_ref, w_ref, b_ref = ins
        z_ref, ext_ref = outs[0], scratch[0]
        i = pl.program_id(0)
        first()
        ext_ref[:, :CONV_HALO, :] = jnp.where(i > 0, up_ref[...].astype(F32), 0.0)
        ext_ref[:, CONV_HALO:, :] = u_ref[...].astype(F32)

        def chunk(j, carry):
            lanes = pl.ds(pl.multiple_of(j * LANES, LANES), LANES)
            w, b = w_ref[:, :, lanes], b_ref[:, lanes]
            for s in range(tt // sub):
                rows = pl.ds(s * sub, sub + CONV_HALO)
                c = [_causal_conv(ext_ref[hf, rows, lanes], w[:, hf, :], b[hf:hf + 1, :])[CONV_HALO:] for hf in range(2)]
                z_ref[pl.ds(s * sub, sub), lanes] = (c[0] * c[1] * _sigmoid(c[1])).astype(z_ref.dtype)
            return carry

        lax.fori_loop(0, f // LANES, chunk, 0)
        last()

    res = _call(
        body, hosted, name=name, grid=(t // tt,),
        in_specs=[pl.BlockSpec((2, tt, f), lambda i: (0, i, 0)),
                  pl.BlockSpec((2, CONV_HALO, f), lambda i: (0, jnp.maximum(i * per - 1, 0), 0)),
                  pl.BlockSpec((3, 2, f), lambda i: (0, 0, 0)), pl.BlockSpec((2, f), lambda i: (0, 0))],
        out_specs=[pl.BlockSpec((tt, f), lambda i: (i, 0))], out_shape=[jax.ShapeDtypeStruct((t, f), MXU_DTYPE)],
        operands=(u, u, conv_w, conv_b), semantics=("parallel",),
        scratch_shapes=[pltpu.VMEM((2, CONV_HALO + tt, f), F32)])
    return res if hosted else res[0]


def _conv_glu_bwd_call(u, dz, conv_w, conv_b, name, hosted=None):
    _, t, f = u.shape
    tt = _tile(t, CONV_ROWS)
    sub = _tile(tt, 2 * CONV_SUB)
    nt = t // tt
    per = tt // CONV_HALO
    halo = CONV_HALO

    def body(ins, outs, scratch, first, last):
        u_ref, up_ref, un_ref, dz_ref, dzn_ref, w_ref, b_ref = ins
        du_ref, dw_ref, db_ref = outs
        uext, dzext, dwacc, dbacc = scratch
        i = pl.program_id(0)
        first()
        uext[:, :halo, :] = jnp.where(i > 0, up_ref[...].astype(F32), 0.0)
        uext[:, halo:halo + tt, :] = u_ref[...].astype(F32)
        uext[:, halo + tt:, :] = un_ref[...].astype(F32)
        dzext[:halo, :] = jnp.zeros((halo, f), F32)
        dzext[halo:halo + tt, :] = dz_ref[...].astype(F32)
        dzext[halo + tt:, :] = jnp.where(i < nt - 1, dzn_ref[...].astype(F32), 0.0)

        @pl.when(i == 0)
        def _():
            dwacc[...] = jnp.zeros_like(dwacc)
            dbacc[...] = jnp.zeros_like(dbacc)

        main = slice(halo, halo + sub)

        def chunk(j, carry):
            lanes = pl.ds(pl.multiple_of(j * LANES, LANES), LANES)
            w, b = w_ref[:, :, lanes], b_ref[:, lanes]
            for s in range(tt // sub):
                rows = pl.ds(s * sub, sub + 2 * halo)
                ue = [uext[hf, rows, lanes] for hf in range(2)]
                a, g = [_causal_conv(ue[hf], w[:, hf, :], b[hf:hf + 1, :]) for hf in range(2)]
                sg = _sigmoid(g)
                dzs = dzext[rows, lanes]
                gs = g * sg
                dc = [dzs * gs, dzs * a * (sg + gs - gs * sg)]
                for hf in range(2):
                    d = dc[hf]
                    d1, d2 = _shift_up(d, 1), _shift_up(d, 2)
                    du = w[2:3, hf, :] * d + w[1:2, hf, :] * d1 + w[0:1, hf, :] * d2
                    du_ref[hf, pl.ds(s * sub, sub), lanes] = du[main].astype(du_ref.dtype)
                    um = ue[hf][main]
                    for k, dd in enumerate((d2, d1, d)):
                        dwacc[k, hf, :, lanes] += _fold8(dd[main] * um)
                    dbacc[hf, :, lanes] += _fold8(d[main])
            return carry

        lax.fori_loop(0, f // LANES, chunk, 0)

        @pl.when(i == nt - 1)
        def _():
            dw_ref[...] = jnp.sum(dwacc[...], axis=2)
            db_ref[...] = jnp.sum(dbacc[...], axis=1)

        last()

    end = t // halo - 1
    return _call(
        body, hosted, name=name, grid=(nt,),
        in_specs=[pl.BlockSpec((2, tt, f), lambda i: (0, i, 0)),
                  pl.BlockSpec((2, halo, f), lambda i: (0, jnp.maximum(i * per - 1, 0), 0)),
                  pl.BlockSpec((2, halo, f), lambda i: (0, jnp.minimum((i + 1) * per, end), 0)),
                  pl.BlockSpec((tt, f), lambda i: (i, 0)),
                  pl.BlockSpec((halo, f), lambda i: (jnp.minimum((i + 1) * per, end), 0)),
                  pl.BlockSpec((3, 2, f), lambda i: (0, 0, 0)), pl.BlockSpec((2, f), lambda i: (0, 0))],
        out_specs=[pl.BlockSpec((2, tt, f), lambda i: (0, i, 0)),
                   pl.BlockSpec((3, 2, f), lambda i: (0, 0, 0)), pl.BlockSpec((2, f), lambda i: (0, 0))],
        out_shape=[jax.ShapeDtypeStruct(u.shape, MXU_DTYPE), jax.ShapeDtypeStruct((3, 2, f), F32),
                   jax.ShapeDtypeStruct((2, f), F32)],
        operands=(u, u, u, dz, dz, conv_w, conv_b), semantics=("arbitrary",),
        scratch_shapes=[pltpu.VMEM((2, tt + 2 * halo, f), F32), pltpu.VMEM((tt + 2 * halo, f), F32),
                        pltpu.VMEM((3, 2, 8, f), F32), pltpu.VMEM((2, 8, f), F32)])


def _kv_even_odd(band, k):
    pair, parity = divmod(k, 2)
    blk = band[:, 128 * pair:128 * (pair + 1)].astype(F32)
    lane = lax.broadcasted_iota(jnp.int32, blk.shape, 1)
    own = jnp.where((lane >= HEAD_DIM) == (parity == 1), blk, 0.0)
    swapped = pltpu.roll(own, HEAD_DIM, axis=1)
    even, odd = (own, swapped) if parity == 0 else (swapped, own)
    return even.astype(MXU_DTYPE), odd.astype(MXU_DTYPE)


def _stack_pairs(ref, k):
    return jnp.concatenate([ref[:, 256 * k:256 * k + 128], ref[:, 256 * k + 128:256 * k + 256]], axis=0)


_GROUP_ORDER = (0, 2, 1, 3)


def _attn_fill_tables(bias_ref):
    shape = (4 * ATT_BLOCK, 2 * ATT_BLOCK)
    row = lax.broadcasted_iota(jnp.int32, shape, 0)
    col = lax.broadcasted_iota(jnp.int32, shape, 1)
    dist = (row & (ATT_BLOCK - 1)) + ATT_BLOCK - col
    ok = (dist >= 0) & (dist < ATT_BLOCK)
    distf = dist.astype(F32)
    rb = lax.broadcasted_iota(jnp.int32, (shape[0], 1), 0) // ATT_BLOCK
    for k in range(N_KV_HEADS):
        slope = jnp.zeros((shape[0], 1), F32)
        for r, gq in enumerate(_GROUP_ORDER):
            slope = jnp.where(rb == r, ALIBI_SLOPES[4 * k + gq], slope)
        bias = jnp.where(ok, -slope * distf, -jnp.inf)
        bias_ref[0, k] = bias
        bias_ref[1, k] = jnp.where(col >= ATT_BLOCK, bias, -jnp.inf)


def _attn_probs(qs, k_even, k_odd, bias, k, sink_ref):
    nt_dims = (((1,), (1,)), ((), ()))
    s = jnp.concatenate([lax.dot_general(qs, k_even, nt_dims, preferred_element_type=F32),
                         lax.dot_general(qs, k_odd, nt_dims, preferred_element_type=F32)], axis=0) + bias
    rb = lax.broadcasted_iota(jnp.int32, (s.shape[0], 1), 0) // ATT_BLOCK
    sink = jnp.zeros((s.shape[0], 1), F32)
    for r, gq in enumerate(_GROUP_ORDER):
        sink = jnp.where(rb == r, sink_ref[4 * k + gq], sink)
    m = jnp.maximum(jnp.max(s, axis=-1, keepdims=True), sink)
    p = jnp.exp(s - m)
    es = jnp.exp(sink - m)
    return p, 1.0 / (jnp.sum(p, axis=-1, keepdims=True) + es), es


_ATTN_TABLES = [pltpu.VMEM((2, N_KV_HEADS, 4 * ATT_BLOCK, 2 * ATT_BLOCK), F32)]


def _attn_specs(with_do):
    prev = lambda n: jnp.maximum(n - 1, 0)
    specs = [pl.BlockSpec(memory_space=pltpu.SMEM),
             pl.BlockSpec((ATT_BLOCK, D_MODEL), lambda n: (n, 0)),
             pl.BlockSpec((ATT_BLOCK, 256), lambda n: (prev(n), 4)), pl.BlockSpec((ATT_BLOCK, 256), lambda n: (n, 4)),
             pl.BlockSpec((ATT_BLOCK, 256), lambda n: (prev(n), 5)), pl.BlockSpec((ATT_BLOCK, 256), lambda n: (n, 5))]
    if with_do:
        specs.append(pl.BlockSpec((ATT_BLOCK, D_MODEL), lambda n: (n, 0)))
    return specs


def _attn_fwd_call(qkv, sinks, hosted=None):
    t = qkv.shape[0]

    def body(ins, outs, scratch, first, last):
        sink_ref, q_ref, kp_ref, kc_ref, vp_ref, vc_ref = ins
        o_ref, bias_ref = outs[0], scratch[0]
        n = pl.program_id(0)
        first()

        @pl.when(n == 0)
        def _():
            _attn_fill_tables(bias_ref)

        first = 1 - jnp.minimum(n, 1)
        kband = jnp.concatenate([kp_ref[...], kc_ref[...]], axis=0)
        vband = jnp.concatenate([vp_ref[...], vc_ref[...]], axis=0)
        for k in range(N_KV_HEADS):
            k_even, k_odd = _kv_even_odd(kband, k)
            v_even, v_odd = _kv_even_odd(vband, k)
            qs = _stack_pairs(q_ref, k) * ATT_SCALE
            p, rdenom, _ = _attn_probs(qs, k_even, k_odd, bias_ref[first, k], k, sink_ref)
            probs = (p * rdenom).astype(MXU_DTYPE)
            o = (jnp.dot(probs[:256], v_even, preferred_element_type=F32)
                 + jnp.dot(probs[256:], v_odd, preferred_element_type=F32))
            o_ref[:, 256 * k:256 * k + 128] = o[:128].astype(o_ref.dtype)
            o_ref[:, 256 * k + 128:256 * k + 256] = o[128:].astype(o_ref.dtype)
        last()

    return _call(
        body, hosted, name="attn_fwd", grid=(t // ATT_BLOCK,), in_specs=_attn_specs(False),
        out_specs=[pl.BlockSpec((ATT_BLOCK, D_MODEL), lambda n: (n, 0))],
        out_shape=[jax.ShapeDtypeStruct((t, D_MODEL), MXU_DTYPE)], operands=(sinks, qkv, qkv, qkv, qkv, qkv),
        semantics=("arbitrary",), scratch_shapes=_ATTN_TABLES)


def _attn_bwd_call(qkv, sinks, do, hosted=None):
    t = qkv.shape[0]
    nb = t // ATT_BLOCK
    tn_dims = (((0,), (0,)), ((), ()))
    nt_dims = (((1,), (1,)), ((), ()))

    def to_native(even_part, odd_part, parity):
        lane = lax.broadcasted_iota(jnp.int32, even_part.shape, 1)
        lo = lane < HEAD_DIM
        e, o = jnp.where(lo, even_part, 0.0), jnp.where(lo, 0.0, odd_part)
        if parity == 0:
            return e + pltpu.roll(o, HEAD_DIM, axis=1)
        return pltpu.roll(e, HEAD_DIM, axis=1) + o

    def body(ins, outs, scratch, first, last):
        sink_ref, q_ref, kp_ref, kc_ref, vp_ref, vc_ref, do_ref = ins
        dq_ref, dk_ref, dv_ref, db_ref, dsink_ref = outs
        bias_ref = scratch[0]
        n = pl.program_id(0)
        first()

        @pl.when(n == 0)
        def _():
            _attn_fill_tables(bias_ref)
            dk_ref[...] = jnp.zeros_like(dk_ref)
            dv_ref[...] = jnp.zeros_like(dv_ref)
            db_ref[...] = jnp.zeros_like(db_ref)
            dsink_ref[...] = jnp.zeros_like(dsink_ref)

        kband = jnp.concatenate([kp_ref[...], kc_ref[...]], axis=0)
        vband = jnp.concatenate([vp_ref[...], vc_ref[...]], axis=0)
        dk_pairs = [jnp.zeros((2 * ATT_BLOCK, 128), F32), jnp.zeros((2 * ATT_BLOCK, 128), F32)]
        dv_pairs = [jnp.zeros((2 * ATT_BLOCK, 128), F32), jnp.zeros((2 * ATT_BLOCK, 128), F32)]
        sink_lane = lax.broadcasted_iota(jnp.int32, (8, 128), 1)
        sink_row = lax.broadcasted_iota(jnp.int32, (8, 128), 0)
        dsink = jnp.zeros((8, 128), F32)
        first = 1 - jnp.minimum(n, 1)
        for k in range(N_KV_HEADS):
            k_even, k_odd = _kv_even_odd(kband, k)
            v_even, v_odd = _kv_even_odd(vband, k)
            qs = _stack_pairs(q_ref, k) * ATT_SCALE
            dos = _stack_pairs(do_ref, k)
            p, rdenom, es = _attn_probs(qs, k_even, k_odd, bias_ref[first, k], k, sink_ref)
            probs = p * rdenom
            dp = jnp.concatenate([lax.dot_general(dos, v_even, nt_dims, preferred_element_type=F32),
                                  lax.dot_general(dos, v_odd, nt_dims, preferred_element_type=F32)], axis=0)
            delta = jnp.sum(probs * dp, axis=-1, keepdims=True)
            ds16 = (probs * (dp - delta)).astype(MXU_DTYPE)
            p16 = probs.astype(MXU_DTYPE)
            dsink_rows = -(es * rdenom) * delta
            for r, gq in enumerate(_GROUP_ORDER):
                tot = jnp.sum(dsink_rows[ATT_BLOCK * r:ATT_BLOCK * (r + 1)], axis=0, keepdims=True)
                dsink = dsink + jnp.where((sink_lane == 4 * k + gq) & (sink_row == 0), tot, 0.0)
            dqs = (jnp.dot(ds16[:256], k_even, preferred_element_type=F32)
                   + jnp.dot(ds16[256:], k_odd, preferred_element_type=F32)) * ATT_SCALE
            dq_ref[:, 256 * k:256 * k + 128] = dqs[:128].astype(dq_ref.dtype)
            dq_ref[:, 256 * k + 128:256 * k + 256] = dqs[128:].astype(dq_ref.dtype)
            db_ref[:, 256 * k:256 * k + 128] += jnp.sum(dqs[:128], axis=0, keepdims=True)
            db_ref[:, 256 * k + 128:256 * k + 256] += jnp.sum(dqs[128:], axis=0, keepdims=True)
            pair, parity = divmod(k, 2)
            dk_pairs[pair] = dk_pairs[pair] + to_native(
                lax.dot_general(ds16[:256], qs, tn_dims, preferred_element_type=F32),
                lax.dot_general(ds16[256:], qs, tn_dims, preferred_element_type=F32), parity)
            dv_pairs[pair] = dv_pairs[pair] + to_native(
                lax.dot_general(p16[:256], dos, tn_dims, preferred_element_type=F32),
                lax.dot_general(p16[256:], dos, tn_dims, preferred_element_type=F32), parity)
        dsink_ref[...] += dsink
        dk_band = jnp.concatenate(dk_pairs, axis=1)
        dv_band = jnp.concatenate(dv_pairs, axis=1)
        cur = pl.multiple_of(n * ATT_BLOCK, ATT_BLOCK)
        dk_ref[pl.ds(cur, ATT_BLOCK), :] += dk_band[ATT_BLOCK:]
        dv_ref[pl.ds(cur, ATT_BLOCK), :] += dv_band[ATT_BLOCK:]

        @pl.when(n > 0)
        def _():
            prv = pl.multiple_of((n - 1) * ATT_BLOCK, ATT_BLOCK)
            dk_ref[pl.ds(prv, ATT_BLOCK), :] += dk_band[:ATT_BLOCK]
            dv_ref[pl.ds(prv, ATT_BLOCK), :] += dv_band[:ATT_BLOCK]

        @pl.when(n == nb - 1)
        def _():
            db_ref[:, 1024:1280] = jnp.sum(dk_ref[...], axis=0, keepdims=True)
            db_ref[:, 1280:1536] = jnp.sum(dv_ref[...], axis=0, keepdims=True)

        last()

    whole = lambda shape: pl.BlockSpec(shape, lambda n: (0, 0))
    return _call(
        body, hosted, name="attn_bwd", grid=(nb,), in_specs=_attn_specs(True),
        out_specs=[pl.BlockSpec((ATT_BLOCK, D_MODEL), lambda n: (n, 0)), whole((t, 256)), whole((t, 256)),
                   whole((1, QKV_DIM)), whole((8, 128))],
        out_shape=[jax.ShapeDtypeStruct((t, D_MODEL), MXU_DTYPE), jax.ShapeDtypeStruct((t, 256), F32),
                   jax.ShapeDtypeStruct((t, 256), F32), jax.ShapeDtypeStruct((1, QKV_DIM), F32),
                   jax.ShapeDtypeStruct((8, 128), F32)],
        operands=(sinks, qkv, qkv, qkv, qkv, qkv, do), semantics=("arbitrary",), scratch_shapes=_ATTN_TABLES)


FF_TILE = 1408


def _ffn_fwd(hn, h, w, conv_w, conv_b, layer, hooks=None):
    t = hn.shape[0]
    tm = _tile(t, 512)
    halves = D_FF // FF_TILE
    u, = _ride(hooks, "in", lambda hosted: _matmul(
        hn, w["w_in"], mode="nn", name=f"ffn{layer}_in", grid=(2 * halves, t // tm, 1),
        a_spec=pl.BlockSpec((tm, D_MODEL), lambda j, i, k: (i, 0)),
        b_spec=pl.BlockSpec((None, D_MODEL, FF_TILE), lambda j, i, k: (layer, 0, j)),
        o_spec=pl.BlockSpec((None, tm, FF_TILE), lambda j, i, k: (j // halves, i, j % halves)),
        out_shape=jax.ShapeDtypeStruct((2, t, D_FF), U_DTYPE), hosted=hosted))
    z, = _ride(hooks, "glu", lambda hosted: _conv_glu_fwd_call(u, conv_w, conv_b, f"ffn{layer}_glu", hosted=hosted))
    h_out, = _ride(hooks, "out", lambda hosted: _mm_nn(z, w["w_out"], f"ffn{layer}_out", b_lead=layer, residual=h,
                                                       tk=D_FF, hosted=hosted))
    return h_out, u, z


def _ffn_bwd(dh, hn, u, z, w_in, w_out, conv_w, conv_b, layer, g_in_acc, g_out_acc, hooks=None):
    t = hn.shape[0]
    dz, = _ride(hooks, "dz", lambda hosted: _mm_nt(dh, w_out, f"ffn{layer}_dz", b_lead=layer, out_dtype=MXU_DTYPE,
                                                   tn=FF_TILE, hosted=hosted))
    g_out = _mm_tn(z, dh, f"ffn{layer}_dwout", lead=layer, n_lead=2, into=g_out_acc, tm=WGRAD_TILE, tk=t)
    du, dcw, dcb = _ride(hooks, "dglu", lambda hosted: _conv_glu_bwd_call(u, dz, conv_w, conv_b, f"ffn{layer}_dglu",
                                                                          hosted=hosted), {"g_out": g_out})
    per_half = D_FF // WGRAD_TILE
    g_in, = _ride(hooks, "dwin", lambda hosted: _matmul(
        hn, du, mode="tn", name=f"ffn{layer}_dwin", grid=(1, 2 * per_half, 1),
        a_spec=pl.BlockSpec((t, D_MODEL), lambda i, j, k: (0, 0)),
        b_spec=pl.BlockSpec((None, t, WGRAD_TILE), lambda i, j, k: (j // per_half, 0, j % per_half)),
        o_spec=pl.BlockSpec((None, D_MODEL, WGRAD_TILE), lambda i, j, k: (layer, 0, j)),
        out_shape=jax.ShapeDtypeStruct((2, D_MODEL, 2 * D_FF), F32), into=g_in_acc, hosted=hosted))
    dhn, = _ride(hooks, "dhn", lambda hosted: _ffn_dhn_call(du, w_in, layer, hosted), {"g_in": g_in, "g_out": g_out})
    return dhn, g_in, g_out, dcw, dcb


WGRAD_TILE = 256


def _ffn_dhn_call(du, w_in, layer, hosted=None):
    _, t, f = du.shape
    tm = _tile(t, 512)
    nt_dims = (((1,), (1,)), ((), ()))

    def body(ins, outs, scratch, first, last):
        du_ref, w_ref = ins
        first()
        outs[0][...] = (lax.dot_general(du_ref[0], w_ref[:, :f], nt_dims, preferred_element_type=F32)
                        + lax.dot_general(du_ref[1], w_ref[:, f:], nt_dims, preferred_element_type=F32))
        last()

    return _call(
        body, hosted, name=f"ffn{layer}_dhn", grid=(t // tm,),
        in_specs=[pl.BlockSpec((2, tm, f), lambda i: (0, i, 0)),
                  pl.BlockSpec((None, D_MODEL, 2 * f), lambda i: (layer, 0, 0))],
        out_specs=[pl.BlockSpec((tm, D_MODEL), lambda i: (i, 0))], out_shape=[jax.ShapeDtypeStruct((t, D_MODEL), F32)],
        operands=(du, w_in), semantics=("parallel",))


def _local_step(x, target, w, *, norm_mix, norm_ffn, norm_f, pool_b, pool_scale, b_qkv, sinks, conv_w, conv_b,
                late_gather=None, reducer=None):
    cw = [conv_w[l].reshape(3, 2, D_FF) for l in range(2)]
    cb = [conv_b[l].reshape(2, D_FF) for l in range(2)]
    gm = [norm_mix[l:l + 1] for l in range(2)]
    gf = [norm_ffn[l:l + 1] for l in range(2)]

    w = dict(w)
    fwd_hooks = {}
    if late_gather:
        def gather_hook(stage):
            return (lambda env: late_gather(stage, w)), (lambda got: w.update(zip(_late_names(stage), got)))

        fwd_hooks = {stage: gather_hook(stage) for stage in GATHER_RIDES}

    h1, hn_f0 = _pool_fwd_call(x, gm[0], w["pool_w"], pool_b, pool_scale, gf[0])
    h2, u0, z0 = _ffn_fwd(hn_f0, h1, w, cw[0], cb[0], 0, hooks=fwd_hooks)
    hn_a = _rms_fwd_call(h2, gm[1], "rms_attn", MXU_DTYPE)
    qkv = _mm_nn(hn_a, w["w_qkv"], "attn_qkv", bias=b_qkv, out_dtype=MXU_DTYPE, tn=768)
    o, = _ride(fwd_hooks, "attn_fwd", lambda hosted: _attn_fwd_call(qkv, sinks, hosted))
    h3, = _ride(fwd_hooks, "attn_out", lambda hosted: _mm_nn(o, w["w_o"], "attn_out", residual=h2, hosted=hosted))
    hn_f1 = _rms_fwd_call(h3, gf[1], "rms_ffn1", MXU_DTYPE)
    h4, u1, z1 = _ffn_fwd(hn_f1, h3, w, cw[1], cb[1], 1)
    loss, dh, dh16, d_norm_f = _loss_call(h4, norm_f, target)

    rd = reducer
    first, mid, last = _Reducer.FIRST, _Reducer.MID, _Reducer.LAST
    hooks1 = hooks_attn = hooks0 = hooks_pool = None
    if rd:
        hooks1 = {"dhn": (lambda env: rd.swap(first, {"w_in": env["g_in"], "w_out": env["g_out"]}),
                          lambda got: rd.pair_sums(first, got))}
        hooks_attn = {"attn_bwd": (lambda env: rd.scatter(first), lambda got: rd.chip_sums(first, got))}

        def mid_and_last(got):
            rd.shared(mid, got[:len(mid)])
            rd.pair_sums(last, got[len(mid):])

        hooks0 = {
            "dz": (lambda env: rd.share(first), lambda got: rd.shared(first, got)),
            "dglu": (lambda env: rd.swap(mid, {"w_out": env["g_out"], "w_qkv": g_qkv, "w_o": g_o4}),
                     lambda got: rd.pair_sums(mid, got)),
            "dwin": (lambda env: rd.scatter(mid), lambda got: rd.chip_sums(mid, got)),
            "dhn": (lambda env: _both(rd.share(mid), rd.swap(last, {"w_in": env["g_in"]})), mid_and_last),
        }
        hooks_pool = {"pool_bwd": (lambda env: rd.scatter(last), lambda got: rd.chip_sums(last, got))}
    dhn, g_in, g_out, dcw1, dcb1 = _ffn_bwd(dh16, hn_f1, u1, z1, w["w_in"], w["w_out"], cw[1], cb[1], 1, None, None,
                                            hooks=hooks1)
    dh, dh16, d_gf1 = _rms_bwd_call(h3, gf[1], dhn, dh, "rms_ffn1_bwd")
    do = _mm_nt(dh16, w["w_o"], "attn_do", out_dtype=MXU_DTYPE)
    g_o = _mm_tn(o, dh16, "attn_dwo", tn=WGRAD_TILE, tk=x.shape[0])
    g_o4 = g_o.reshape(N_CHIPS, 2, W_O_ROWS // 2, D_MODEL)
    dq, dk, dv, d_bqkv, d_sinks = _ride(hooks_attn, "attn_bwd", lambda hosted: _attn_bwd_call(qkv, sinks, do, hosted))
    dqkv = jnp.concatenate([dq, dk.astype(MXU_DTYPE), dv.astype(MXU_DTYPE)], axis=1)
    g_qkv = _mm_tn(hn_a, dqkv, "attn_dwqkv", tn=WGRAD_TILE, tk=x.shape[0])
    dhn = _mm_nt(dqkv, w["w_qkv"], "attn_dhn", tk=QKV_DIM)
    dh, dh16, d_gm1 = _rms_bwd_call(h2, gm[1], dhn, dh, "rms_attn_bwd")
    dhn, g_in, g_out, dcw0, dcb0 = _ffn_bwd(dh16, hn_f0, u0, z0, w["w_in"], w["w_out"], cw[0], cb[0], 0, g_in, g_out,
                                            hooks=hooks0)
    dh, _, d_gf0 = _rms_bwd_call(h1, gf[0], dhn, dh, "rms_ffn0_bwd")
    grad_x, g_pool, dvec = _ride(hooks_pool, "pool_bwd", lambda hosted: _pool_bwd_call(
        x, dh, gm[0], w["pool_w"], pool_b, pool_scale, hosted))
    if rd:
        rd.shared(last, _alone(rd.share(last), "share_last"))

    grads = {
        "w_in": g_in, "w_out": g_out, "w_qkv": g_qkv, "w_o": g_o, "pool_w": g_pool,
        "pool_b": dvec[0:1], "pool_scale": dvec[1:2], "b_qkv": d_bqkv, "sinks": d_sinks[0:1, :N_HEADS],
        "norm_mix": jnp.concatenate([dvec[2:3], d_gm1], axis=0), "norm_ffn": jnp.concatenate([d_gf0, d_gf1], axis=0),
        "conv_w": jnp.stack([dcw0.reshape(3, 2 * D_FF), dcw1.reshape(3, 2 * D_FF)]),
        "conv_b": jnp.stack([dcb0.reshape(2 * D_FF), dcb1.reshape(2 * D_FF)]), "norm_f": d_norm_f,
    }
    return loss, grad_x, grads


W_IN_COLS = 2 * D_FF // N_CHIPS
W_OUT_ROWS = D_FF // N_CHIPS
QKV_COLS = QKV_DIM // N_CHIPS
W_O_ROWS = D_MODEL // N_CHIPS
POOL_ROWS = POOL_GC // N_CHIPS
BIG = ("w_in", "w_out", "w_qkv", "w_o", "pool_w")


def _mesh_place():
    x, y, c = lax.axis_index("x"), lax.axis_index("y"), lax.axis_index("c")
    chips = [(1 - x, y), (x, 1 - y), (1 - x, 1 - y)]
    return x, y, c, chips


def _col(k, width):
    return pl.ds(pl.multiple_of(k * width, 128), width)


def _row(k, height):
    return pl.ds(pl.multiple_of(k * height, 16), height)


def _full_piece(name, ref, k, h):
    if name == "w_in":
        return ref.at[h, :, _col(k, W_IN_COLS)]
    if name == "w_out":
        return ref.at[h, _row(k, W_OUT_ROWS), :]
    if name == "w_qkv":
        return ref.at[_row(h, D_MODEL // 2), _col(k, QKV_COLS)]
    if name == "w_o":
        return ref.at[_row(2 * k + h, W_O_ROWS // 2), :]
    return ref.at[pl.ds(2 * h, 2), _row(k, POOL_ROWS), :]


def _shard_half(name, ref, h):
    if name in ("w_in", "w_out"):
        return ref.at[h]
    if name == "w_qkv":
        return ref.at[_row(h, D_MODEL // 2), :]
    if name == "w_o":
        return ref.at[_row(h, W_O_ROWS // 2), :]
    return ref.at[pl.ds(2 * h, 2)]


FULL_SHAPES = {"w_in": (2, D_MODEL, 2 * D_FF), "w_out": (2, D_FF, D_MODEL), "w_qkv": (D_MODEL, QKV_DIM),
               "w_o": (D_MODEL, D_MODEL), "pool_w": (4, POOL_GC, POOL_GC)}
SHARD_SHAPES = {"w_in": (2, D_MODEL, W_IN_COLS), "w_out": (2, W_OUT_ROWS, D_MODEL), "w_qkv": (D_MODEL, QKV_COLS),
                "w_o": (W_O_ROWS, D_MODEL), "pool_w": (4, POOL_ROWS, POOL_GC)}
HALF_SHAPES = {"w_in": (D_MODEL, W_IN_COLS), "w_out": (W_OUT_ROWS, D_MODEL), "w_qkv": (D_MODEL // 2, QKV_COLS),
               "w_o": (W_O_ROWS // 2, D_MODEL), "pool_w": (2, POOL_ROWS, POOL_GC)}


def _place_shard(name, shard, me_arr):
    if name == "w_in":
        blk, grid = (1, 256, W_IN_COLS), (2, D_MODEL // 256)
        src, dst = (lambda a, b, me: (a, b, 0)), (lambda a, b, me: (a, b, me[0]))
    elif name == "w_out":
        blk, grid = (1, W_OUT_ROWS, D_MODEL), (2, 1)
        src, dst = (lambda a, b, me: (a, 0, 0)), (lambda a, b, me: (a, me[0], 0))
    elif name == "w_qkv":
        blk, grid = (256, QKV_COLS), (1, D_MODEL // 256)
        src, dst = (lambda a, b, me: (b, 0)), (lambda a, b, me: (b, me[0]))
    elif name == "w_o":
        blk, grid = (W_O_ROWS, D_MODEL), (1, 1)
        src, dst = (lambda a, b, me: (0, 0)), (lambda a, b, me: (me[0], 0))
    else:
        blk, grid = (4, POOL_ROWS, POOL_GC), (1, 1)
        src, dst = (lambda a, b, me: (0, 0, 0)), (lambda a, b, me: (0, me[0], 0))

    def body(me_ref, s_ref, o_ref):
        o_ref[...] = s_ref[...].astype(o_ref.dtype)

    return pl.pallas_call(
        body, name=f"place_{name}",
        grid_spec=pltpu.PrefetchScalarGridSpec(num_scalar_prefetch=1, grid=grid, in_specs=[pl.BlockSpec(blk, src)],
                                               out_specs=pl.BlockSpec(blk, dst)),
        out_shape=jax.ShapeDtypeStruct(FULL_SHAPES[name], WIRE_DTYPE), compiler_params=_params("parallel", "parallel"),
    )(me_arr, shard)


def _when_mover(half, c, fn):
    if half is None:
        fn(c)
    else:
        pl.when(c == half)(lambda: fn(half))


def _when_taker(half, c, fn):
    if half is None:
        fn(1 - c)
    else:
        pl.when(c != half)(lambda: fn(half))


def _remote(ref, sems, t, j, dev):
    send, recv = sems
    return pltpu.make_async_remote_copy(src_ref=ref, dst_ref=ref, send_sem=send.at[t, j], recv_sem=recv.at[t, j],
                                        device_id=dev, device_id_type=MESH)


def _gather_ici(items, refs, sems, finish):
    x, y, c, chips = _mesh_place()
    me = 2 * x + y
    for t, (nm, half) in enumerate(items):
        def go(h, t=t, nm=nm):
            mine = _full_piece(nm, refs[t], me, h)
            for j, (cx, cy) in enumerate(chips):
                if not finish:
                    _remote(mine, sems, t, j, (cx, cy, c)).start()
                else:
                    _remote(_full_piece(nm, refs[t], 2 * cx + cy, h), sems, t, j, (cx, cy, c)).wait_recv()
                    _remote(mine, sems, t, j, (cx, cy, c)).wait_send()

        _when_mover(half, c, go)


def _gather_pass(items, refs, sems, finish):
    x, y, c, chips = _mesh_place()
    sibling = (x, y, 1 - c)
    for t, (nm, half) in enumerate(items):
        def give(h, t=t, nm=nm):
            for j, (cx, cy) in enumerate(chips):
                got = _remote(_full_piece(nm, refs[t], 2 * cx + cy, h), sems, t, j, sibling)
                if not finish:
                    got.start()
                else:
                    got.wait_send()

        def take(h, t=t, nm=nm):
            for j, (cx, cy) in enumerate(chips):
                _remote(_full_piece(nm, refs[t], 2 * cx + cy, h), sems, t, j, sibling).wait_recv()

        _when_mover(half, c, give)
        if finish:
            _when_taker(half, c, take)


def _gather_sems(n_items):
    return [pltpu.SemaphoreType.DMA((n_items, 3)), pltpu.SemaphoreType.DMA((n_items, 3))]


EARLY_ITEMS = (("pool_w", None), ("w_in", 0))
_LATE_A = (("w_out", 0), ("w_qkv", None), ("w_o", None))
_LATE_B = (("w_in", 1),)
_LATE_C = (("w_out", 1),)
GATHER_RIDES = {"in": ((), _LATE_A), "glu": (_LATE_A, _LATE_B), "out": (_LATE_B, ()), "attn_fwd": ((), _LATE_C),
                "attn_out": (_LATE_C, ())}


def _names(items):
    return [nm for nm, _ in items]


def _gather_early(full, small):
    items = EARLY_ITEMS
    n = len(items)

    def body(*refs):
        small_in = refs[n]
        outs, small_out = refs[n + 1:2 * n + 1], refs[2 * n + 1]
        scratch = refs[2 * n + 2:]
        ici_sems, pass_sems = scratch[0:2], scratch[2:4]
        local_sem, small_send, small_recv = scratch[4:]
        x, y, c, chips = _mesh_place()
        me = 2 * x + y
        local = pltpu.make_async_copy(small_in, small_out.at[me], local_sem.at[0])
        local.start()
        small_sends = [pltpu.make_async_remote_copy(
            src_ref=small_in, dst_ref=small_out.at[me], send_sem=small_send.at[j], recv_sem=small_recv.at[j],
            device_id=(cx, cy, c), device_id_type=MESH) for j, (cx, cy) in enumerate(chips)]
        for cp in small_sends:
            cp.start()
        _gather_ici(items, outs, ici_sems, finish=False)
        _gather_ici(items, outs, ici_sems, finish=True)
        _gather_pass(items, outs, pass_sems, finish=False)
        _gather_pass(items, outs, pass_sems, finish=True)
        for j, (cx, cy) in enumerate(chips):
            got = small_out.at[2 * cx + cy]
            pltpu.make_async_remote_copy(src_ref=got, dst_ref=got, send_sem=small_send.at[j],
                                         recv_sem=small_recv.at[j], device_id=(cx, cy, c), device_id_type=MESH).wait_recv()
        for cp in small_sends:
            cp.wait_send()
        local.wait()

    arrays = [full[nm] for nm in _names(items)]
    out_shape = [jax.ShapeDtypeStruct(a.shape, a.dtype) for a in arrays]
    out_shape.append(jax.ShapeDtypeStruct((N_CHIPS,) + small.shape, F32))
    res = pl.pallas_call(
        body, name="gather_early", in_specs=[ANY] * (n + 1), out_specs=[ANY] * (n + 1), out_shape=out_shape,
        input_output_aliases={t: t for t in range(n)},
        scratch_shapes=_gather_sems(n) + _gather_sems(n) + [pltpu.SemaphoreType.DMA((1,)), pltpu.SemaphoreType.DMA((3,)),
                                                            pltpu.SemaphoreType.DMA((3,))],
    )(*arrays, small)
    return {**full, **dict(zip(_names(items), res[:n]))}, res[n]


def _late_gather_host(stage, w):
    passing, sending = GATHER_RIDES[stage]
    n_pass = len(passing)

    def run(src, refs, new, sems, finish):
        if passing:
            _gather_pass(passing, refs[:n_pass], sems[:2], finish)
        if sending:
            _gather_ici(sending, refs[n_pass:], sems[-2:], finish)

    sems = (_gather_sems(n_pass) if passing else []) + (_gather_sems(len(sending)) if sending else [])
    return _Hosted([], [w[nm] for nm in _late_names(stage)], [], sems, functools.partial(run, finish=False),
                   functools.partial(run, finish=True))


def _late_names(stage):
    passing, sending = GATHER_RIDES[stage]
    return _names(passing) + _names(sending)


def _other_half_src(name, ref, h):
    if name in ("w_in", "w_out"):
        return ref.at[h]
    if name == "w_qkv":
        return ref.at[_row(h, D_MODEL // 2), :]
    if name == "w_o":
        return ref.at[:, pl.ds(h, 1)]
    return ref.at[pl.ds(2 * h, 2)]


SLAB_SHAPES = {"w_in": (D_MODEL, 2 * D_FF), "w_out": (D_FF, D_MODEL), "w_qkv": (D_MODEL // 2, QKV_DIM),
               "w_o": (N_CHIPS, 1, W_O_ROWS // 2, D_MODEL), "pool_w": (2, POOL_GC, POOL_GC)}


def _item_tag(item):
    return item[0] if item[1] is None else f"{item[0]}{item[1]}"


def _reduce_swap(items, grads, slabs, sems, finish):
    x, y, c, _ = _mesh_place()
    send, recv = sems
    for t, (nm, half) in enumerate(items):
        def copy(h, t=t, nm=nm):
            return pltpu.make_async_remote_copy(
                src_ref=_other_half_src(nm, grads[t], h), dst_ref=slabs[t], send_sem=send.at[t], recv_sem=recv.at[t],
                device_id=(x, y, 1 - c), device_id_type=MESH)

        def give(h):
            if finish:
                copy(h).wait_send()
            else:
                copy(h).start()

        if half is None:
            give(1 - c)
            if finish:
                copy(c).wait_recv()
        else:
            pl.when(c != half)(lambda: give(half))
            if finish:
                pl.when(c == half)(lambda: copy(half).wait_recv())


def _reduce_scatter(items, sums, parts, sems, finish):
    _, _, c, chips = _mesh_place()
    send, recv = sems
    for t, (nm, half) in enumerate(items):
        def go(h, t=t, nm=nm):
            for j, (cx, cy) in enumerate(chips):
                cp = pltpu.make_async_remote_copy(
                    src_ref=_slab_piece(nm, sums[t], 2 * cx + cy), dst_ref=parts[t].at[j], send_sem=send.at[t, j],
                    recv_sem=recv.at[t, j], device_id=(cx, cy, c), device_id_type=MESH)
                if finish:
                    cp.wait_recv()
                    cp.wait_send()
                else:
                    cp.start()

        _when_mover(half, c, go)


def _reduce_share(items, shards, sems, finish):
    x, y, c, _ = _mesh_place()
    send, recv = sems
    for t, (nm, half) in enumerate(items):
        def copy(h, t=t, nm=nm):
            part = _shard_half(nm, shards[t], h)
            return pltpu.make_async_remote_copy(src_ref=part, dst_ref=part, send_sem=send.at[t], recv_sem=recv.at[t],
                                                device_id=(x, y, 1 - c), device_id_type=MESH)

        def give(h):
            if finish:
                copy(h).wait_send()
            else:
                copy(h).start()

        _when_mover(half, c, give)
        if finish:
            _when_taker(half, c, lambda h: copy(h).wait_recv())


def _pair_sems(n):
    return [pltpu.SemaphoreType.DMA((n,)), pltpu.SemaphoreType.DMA((n,))]


def _swap_host(items, grads):
    def run(src, arr, new, sems, finish):
        _reduce_swap(items, src, new, sems, finish)

    return _Hosted([grads[nm] for nm, _ in items], [], [jax.ShapeDtypeStruct(SLAB_SHAPES[nm], F32) for nm, _ in items],
                   _pair_sems(len(items)), functools.partial(run, finish=False), functools.partial(run, finish=True))


def _scatter_host(items, sums):
    def run(src, arr, new, sems, finish):
        _reduce_scatter(items, src, new, sems, finish)

    new = [jax.ShapeDtypeStruct((N_CHIPS - 1,) + HALF_SHAPES[nm], WIRE_DTYPE) for nm, _ in items]
    return _Hosted(list(sums), [], new, _gather_sems(len(items)), functools.partial(run, finish=False),
                   functools.partial(run, finish=True))


def _share_host(items, shards):
    def run(src, arr, new, sems, finish):
        _reduce_share(items, arr, sems, finish)

    return _Hosted([], list(shards), [], _pair_sems(len(items)), functools.partial(run, finish=False),
                   functools.partial(run, finish=True))


def _alone(hosted, name):
    def body(ins, outs, scratch, first, last):
        first()
        last()

    return _call(body, hosted, name=name, grid=(1,), in_specs=[], out_specs=[], out_shape=[], operands=(),
                 semantics=("arbitrary",))


def _pair_sum(item, grad, slab, place):
    name, half = item
    shape = SLAB_SHAPES[name]
    hsel = (lambda p: p[1]) if half is None else (lambda p: half)
    if name == "w_in":
        blk = (256, 2 * D_FF)
        grid = (D_MODEL // 256,)
        g_spec = pl.BlockSpec((None,) + blk, lambda i, p: (hsel(p), i, 0))
        s_spec = pl.BlockSpec(blk, lambda i, p: (i, 0))
    elif name == "w_out":
        blk = (W_OUT_ROWS, D_MODEL)
        grid = (N_CHIPS,)
        g_spec = pl.BlockSpec((None,) + blk, lambda i, p: (hsel(p), i, 0))
        s_spec = pl.BlockSpec(blk, lambda i, p: (i, 0))
    elif name == "w_qkv":
        grid = (1,)
        g_spec = pl.BlockSpec(shape, lambda i, p: (hsel(p), 0))
        s_spec = pl.BlockSpec(shape, lambda i, p: (0, 0))
    else:
        grid = (1,)
        g_spec = pl.BlockSpec(shape, lambda i, p: (0, hsel(p), 0, 0))
        s_spec = pl.BlockSpec(shape, lambda i, p: (0, 0, 0, 0))

    def body(p_ref, g_ref, s_ref, o_ref):
        @pl.when(hsel(p_ref) == p_ref[1])
        def _():
            o_ref[...] = (g_ref[...] + s_ref[...]).astype(o_ref.dtype)

    return pl.pallas_call(
        body, name=f"pair_sum_{_item_tag(item)}",
        grid_spec=pltpu.PrefetchScalarGridSpec(num_scalar_prefetch=1, grid=grid, in_specs=[g_spec, s_spec],
                                               out_specs=s_spec),
        out_shape=jax.ShapeDtypeStruct(shape, WIRE_DTYPE), compiler_params=_params("parallel"),
    )(place, grad, slab)


def _slab_piece(name, ref, k):
    if name == "w_in":
        return ref.at[:, _col(k, W_IN_COLS)]
    if name == "w_out":
        return ref.at[_row(k, W_OUT_ROWS), :]
    if name == "w_qkv":
        return ref.at[:, _col(k, QKV_COLS)]
    if name == "w_o":
        return ref.at[k, 0]
    return ref.at[:, _row(k, POOL_ROWS), :]


def _sum_chips(item, slab, parts, place, into=None):
    name, sel = item
    half = HALF_SHAPES[name]
    hsel = (lambda p: p[1]) if sel is None else (lambda p: sel)
    if name == "w_in":
        blk, grid = (256, W_IN_COLS), (D_MODEL // 256,)
        own = pl.BlockSpec(blk, lambda i, p: (i, p[0]))
        got = pl.BlockSpec((N_CHIPS - 1,) + blk, lambda i, p: (0, i, 0))
        out = pl.BlockSpec((None,) + blk, lambda i, p: (hsel(p), i, 0))
    elif name == "w_out":
        grid = (1,)
        own = pl.BlockSpec(half, lambda i, p: (p[0], 0))
        got = pl.BlockSpec((N_CHIPS - 1,) + half, lambda i, p: (0, 0, 0))
        out = pl.BlockSpec((None,) + half, lambda i, p: (hsel(p), 0, 0))
    elif name == "w_qkv":
        grid = (1,)
        own = pl.BlockSpec(half, lambda i, p: (0, p[0]))
        got = pl.BlockSpec((N_CHIPS - 1,) + half, lambda i, p: (0, 0, 0))
        out = pl.BlockSpec(half, lambda i, p: (hsel(p), 0))
    else:
        grid = (1,)
        own = pl.BlockSpec((None, None) + half, lambda i, p: (p[0], 0, 0, 0))
        got = pl.BlockSpec((N_CHIPS - 1,) + half, lambda i, p: (0, 0, 0))
        out = pl.BlockSpec(half, lambda i, p: (hsel(p), 0))

    def body(p_ref, own_ref, got_ref, *rest):
        o_ref = rest[-1]

        @pl.when(hsel(p_ref) == p_ref[1])
        def _():
            acc = own_ref[...].astype(F32)
            for j in range(N_CHIPS - 1):
                acc = acc + got_ref[j].astype(F32)
            o_ref[...] = acc

    in_specs, operands, aliases = [own, got], [place, slab, parts], {}
    if into is not None:
        in_specs.append(ANY)
        operands.append(into)
        aliases = {3: 0}
    return pl.pallas_call(
        body, name=f"sum_chips_{_item_tag(item)}",
        grid_spec=pltpu.PrefetchScalarGridSpec(num_scalar_prefetch=1, grid=grid, in_specs=in_specs, out_specs=out),
        out_shape=jax.ShapeDtypeStruct(SHARD_SHAPES[name], F32), input_output_aliases=aliases,
        compiler_params=_params("parallel"),
    )(*operands)


class _Reducer:
    FIRST = (("w_in", 1), ("w_out", 1))
    MID = (("w_out", 0), ("w_qkv", None), ("w_o", None))
    LAST = (("w_in", 0),)

    def __init__(self, place):
        self.place = place
        self.shards = {}
        self.state = {}

    def swap(self, items, grads):
        self.state[items] = {"grads": [grads[nm] for nm, _ in items]}
        return _swap_host(items, grads)

    def pair_sums(self, items, slabs):
        st = self.state[items]
        st["sums"] = [_pair_sum(it, g, s, self.place) for it, g, s in zip(items, st["grads"], slabs)]

    def scatter(self, items):
        return _scatter_host(items, self.state[items]["sums"])

    def chip_sums(self, items, parts):
        for it, s, p in zip(items, self.state[items]["sums"], parts):
            self.shards[it[0]] = _sum_chips(it, s, p, self.place, into=self.shards.get(it[0]))

    def share(self, items):
        return _share_host(items, [self.shards[nm] for nm, _ in items])

    def shared(self, items, updated):
        self.shards.update(dict(zip(_names(items), updated)))


N_DEV = 8


def _all_reduce_small(vec, g_pool):
    def body(v_ref, gp_ref, o_ref, po_ref, slots, pslots, send_sems, recv_sems, psend, precv, hsend, hrecv):
        x, y, c, _ = _mesh_place()
        me = 4 * x + 2 * y + c

        def pool_piece(chip, h):
            return gp_ref.at[pl.ds(2 * h, 2), pl.ds(pl.multiple_of(chip * POOL_ROWS, POOL_ROWS), POOL_ROWS), :]

        slots[me] = v_ref[...]
        pslots[me] = pool_piece(2 * x + y, c)[...]
        copies = []
        for d in range(1, N_DEV):
            peer = (x ^ (d >> 2), y ^ ((d >> 1) & 1), c ^ (d & 1))
            copies.append(pltpu.make_async_remote_copy(
                src_ref=v_ref, dst_ref=slots.at[me], send_sem=send_sems.at[d - 1], recv_sem=recv_sems.at[d - 1],
                device_id=peer, device_id_type=MESH))
            copies.append(pltpu.make_async_remote_copy(
                src_ref=pool_piece(2 * peer[0] + peer[1], peer[2]), dst_ref=pslots.at[me], send_sem=psend.at[d - 1],
                recv_sem=precv.at[d - 1], device_id=peer, device_id_type=MESH))
        for cp in copies:
            cp.start()
        for d in range(1, N_DEV):
            peer = 4 * (x ^ (d >> 2)) + 2 * (y ^ ((d >> 1) & 1)) + (c ^ (d & 1))
            for buf, ss, rs in ((slots, send_sems, recv_sems), (pslots, psend, precv)):
                got = buf.at[peer]
                pltpu.make_async_remote_copy(src_ref=got, dst_ref=got, send_sem=ss.at[d - 1], recv_sem=rs.at[d - 1],
                                             device_id=(x, y, c), device_id_type=MESH).wait_recv()
        for cp in copies:
            cp.wait_send()
        acc, pacc = slots[0], pslots[0]
        for d in range(1, N_DEV):
            acc, pacc = acc + slots[d], pacc + pslots[d]
        o_ref[...] = acc
        po_ref[pl.ds(2 * c, 2)] = pacc
        mine, theirs = po_ref.at[pl.ds(2 * c, 2)], po_ref.at[pl.ds(2 * (1 - c), 2)]
        give = pltpu.make_async_remote_copy(src_ref=mine, dst_ref=mine, send_sem=hsend.at[0], recv_sem=hrecv.at[0],
                                            device_id=(x, y, 1 - c), device_id_type=MESH)
        give.start()
        pltpu.make_async_remote_copy(src_ref=theirs, dst_ref=theirs, send_sem=hsend.at[0], recv_sem=hrecv.at[0],
                                     device_id=(x, y, 1 - c), device_id_type=MESH).wait_recv()
        give.wait_send()

    vm = pl.BlockSpec(memory_space=pltpu.VMEM)
    piece = (2, POOL_ROWS, POOL_GC)
    return pl.pallas_call(
        body, name="all_reduce_small", in_specs=[vm, vm], out_specs=[vm, vm],
        out_shape=[jax.ShapeDtypeStruct(vec.shape, F32), jax.ShapeDtypeStruct(SHARD_SHAPES["pool_w"], F32)],
        scratch_shapes=[pltpu.VMEM((N_DEV,) + vec.shape, F32), pltpu.VMEM((N_DEV,) + piece, F32)]
        + [pltpu.SemaphoreType.DMA((N_DEV - 1,))] * 4 + [pltpu.SemaphoreType.DMA((1,))] * 2,
    )(vec, g_pool)


def _adamw_call(w, g, m, v, name):
    shape = w.shape
    cols = shape[-1]
    rows = w.size // cols
    tr = rows
    for cand in (256, 128, 64, 32, 16, 8):
        if rows > cand and rows % cand == 0 and cand * cols * 4 <= 2 * 1024 * 1024:
            tr = cand
            break

    def body(w_ref, g_ref, m_ref, v_ref, d_ref, nm_ref, nv_ref):
        gv = g_ref[...]
        mn = ADAM_B1 * m_ref[...] + (1.0 - ADAM_B1) * gv
        vn = ADAM_B2 * v_ref[...] + (1.0 - ADAM_B2) * jnp.square(gv)
        m_hat = mn / (1.0 - ADAM_B1 ** ADAM_STEP)
        v_hat = vn / (1.0 - ADAM_B2 ** ADAM_STEP)
        d_ref[...] = -ADAM_LR * (m_hat / (jnp.sqrt(v_hat) + ADAM_EPS) + ADAM_WD * w_ref[...])
        nm_ref[...] = mn
        nv_ref[...] = vn

    spec = pl.BlockSpec((tr, cols), lambda i: (i, 0))
    flat = lambda a: a.reshape(rows, cols)
    outs = pl.pallas_call(
        body, name=name, grid=(rows // tr,), in_specs=[spec] * 4, out_specs=[spec] * 3,
        out_shape=[jax.ShapeDtypeStruct((rows, cols), F32)] * 3, compiler_params=_params("parallel"),
    )(flat(w), flat(g), flat(m), flat(v))
    return tuple(o.reshape(shape) for o in outs)


WEIGHTS = ("pool_w", "pool_b", "pool_scale", "attn_w_qkv", "attn_b_qkv", "attn_sinks", "attn_w_o", "norm_mix",
           "norm_ffn", "ffn_w_in", "ffn_conv_w", "ffn_conv_b", "ffn_w_out", "norm_f")
SMALL_GATHER = 128 * 71
_SMALL_SIZES = (("pool_b", 1024), ("pool_scale", 1024), ("b_qkv", 1536), ("sinks", 128), ("norm_mix", 2048),
                ("norm_ffn", 2048), ("conv_w", 6 * 2 * D_FF), ("conv_b", 4 * D_FF), ("norm_f", 1024))
SMALL_COLS = 6784


def _pack_small(grads):
    flat = jnp.concatenate([jnp.pad(grads[nm].reshape(-1), (0, size - grads[nm].size)) for nm, size in _SMALL_SIZES])
    return jnp.pad(flat, (0, 8 * SMALL_COLS - flat.size)).reshape(8, SMALL_COLS)


def _unpack_small(vec):
    flat = vec.reshape(-1)
    out, off = {}, 0
    for nm, size in _SMALL_SIZES:
        out[nm] = flat[off:off + size]
        off += size
    return out


def kernel(x, pool_w, pool_b, pool_scale, attn_w_qkv, attn_b_qkv, attn_sinks, attn_w_o, norm_mix, norm_ffn, ffn_w_in, ffn_conv_w, ffn_conv_b, ffn_w_out, norm_f, loss_target, m_pool_w, m_pool_b, m_pool_scale, m_attn_w_qkv, m_attn_b_qkv, m_attn_sinks, m_attn_w_o, m_norm_mix, m_norm_ffn, m_ffn_w_in, m_ffn_conv_w, m_ffn_conv_b, m_ffn_w_out, m_norm_f, v_pool_w, v_pool_b, v_pool_scale, v_attn_w_qkv, v_attn_b_qkv, v_attn_sinks, v_attn_w_o, v_norm_mix, v_norm_ffn, v_ffn_w_in, v_ffn_conv_w, v_ffn_conv_b, v_ffn_w_out, v_norm_f):
    weights = dict(pool_w=pool_w, pool_b=pool_b, pool_scale=pool_scale, attn_w_qkv=attn_w_qkv, attn_b_qkv=attn_b_qkv,
                   attn_sinks=attn_sinks, attn_w_o=attn_w_o, norm_mix=norm_mix, norm_ffn=norm_ffn, ffn_w_in=ffn_w_in,
                   ffn_conv_w=ffn_conv_w, ffn_conv_b=ffn_conv_b, ffn_w_out=ffn_w_out, norm_f=norm_f)
    m_in = dict(pool_w=m_pool_w, pool_b=m_pool_b, pool_scale=m_pool_scale, attn_w_qkv=m_attn_w_qkv,
                attn_b_qkv=m_attn_b_qkv, attn_sinks=m_attn_sinks, attn_w_o=m_attn_w_o, norm_mix=m_norm_mix,
                norm_ffn=m_norm_ffn, ffn_w_in=m_ffn_w_in, ffn_conv_w=m_ffn_conv_w, ffn_conv_b=m_ffn_conv_b,
                ffn_w_out=m_ffn_w_out, norm_f=m_norm_f)
    v_in = dict(pool_w=v_pool_w, pool_b=v_pool_b, pool_scale=v_pool_scale, attn_w_qkv=v_attn_w_qkv,
                attn_b_qkv=v_attn_b_qkv, attn_sinks=v_attn_sinks, attn_w_o=v_attn_w_o, norm_mix=v_norm_mix,
                norm_ffn=v_norm_ffn, ffn_w_in=v_ffn_w_in, ffn_conv_w=v_ffn_conv_w, ffn_conv_b=v_ffn_conv_b,
                ffn_w_out=v_ffn_w_out, norm_f=v_norm_f)
    chip = 2 * lax.axis_index("x") + lax.axis_index("y")
    core = lax.axis_index("c")

    place = jnp.stack([chip, core]).astype(jnp.int32)
    shards = {"w_in": ffn_w_in, "w_out": ffn_w_out, "w_qkv": attn_w_qkv[0], "w_o": attn_w_o[0], "pool_w": pool_w[0]}
    placed = {nm: _place_shard(nm, shards[nm], place) for nm in BIG}
    small = jnp.concatenate([pool_b.reshape(-1), attn_b_qkv.reshape(-1), ffn_conv_w.reshape(-1)]).reshape(1, SMALL_GATHER)
    full, small_all = _gather_early(placed, small)
    small_all = small_all.reshape(N_CHIPS, SMALL_GATHER)
    pool_b_full = small_all[:, :256].reshape(N_CHIPS, 4, POOL_ROWS).transpose(1, 0, 2).reshape(1, D_MODEL)
    b_qkv_full = small_all[:, 256:640].reshape(1, QKV_DIM)
    conv_w_full = small_all[:, 640:].reshape(N_CHIPS, 2, 3, W_IN_COLS).transpose(1, 2, 0, 3).reshape(2, 3, 2 * D_FF)

    reducer = _Reducer(place)
    loss, grad_x, grads = _local_step(
        x[0], loss_target[0], full, norm_mix=norm_mix, norm_ffn=norm_ffn, norm_f=norm_f.reshape(1, D_MODEL),
        pool_b=pool_b_full, pool_scale=pool_scale, b_qkv=b_qkv_full, sinks=attn_sinks[0], conv_w=conv_w_full,
        conv_b=ffn_conv_b, late_gather=_late_gather_host, reducer=reducer)
    loss = lax.psum(loss[0, 0], ("x", "y", "c"))
    reduced = dict(reducer.shards)

    small_sum, reduced["pool_w"] = _all_reduce_small(_pack_small(grads), grads["pool_w"])
    small_g = _unpack_small(small_sum)
    pool_b_g = lax.dynamic_slice_in_dim(small_g["pool_b"].reshape(4, N_CHIPS, POOL_ROWS), chip, 1, axis=1)
    b_qkv_g = lax.dynamic_slice_in_dim(small_g["b_qkv"].reshape(N_CHIPS, QKV_COLS), chip, 1, axis=0)
    conv_w_g = lax.dynamic_slice_in_dim(small_g["conv_w"].reshape(2, 3, N_CHIPS, W_IN_COLS), chip, 1, axis=2)
    grad_w = {
        "pool_w": reduced["pool_w"].reshape(pool_w.shape), "pool_b": pool_b_g.reshape(pool_b.shape),
        "pool_scale": small_g["pool_scale"].reshape(pool_scale.shape),
        "attn_w_qkv": reduced["w_qkv"].reshape(attn_w_qkv.shape), "attn_b_qkv": b_qkv_g.reshape(attn_b_qkv.shape),
        "attn_sinks": small_g["sinks"][:N_HEADS].reshape(attn_sinks.shape),
        "attn_w_o": reduced["w_o"].reshape(attn_w_o.shape), "norm_mix": small_g["norm_mix"].reshape(norm_mix.shape),
        "norm_ffn": small_g["norm_ffn"].reshape(norm_ffn.shape), "ffn_w_in": reduced["w_in"],
        "ffn_conv_w": conv_w_g.reshape(ffn_conv_w.shape), "ffn_conv_b": small_g["conv_b"].reshape(ffn_conv_b.shape),
        "ffn_w_out": reduced["w_out"], "norm_f": small_g["norm_f"].reshape(norm_f.shape),
    }

    delta, new_m, new_v = {}, {}, {}
    for nm in WEIGHTS:
        delta[nm], new_m[nm], new_v[nm] = _adamw_call(weights[nm], grad_w[nm], m_in[nm], v_in[nm], f"adamw_{nm}")
    return (loss, grad_x.reshape(x.shape), *[grad_w[nm] for nm in WEIGHTS], *[delta[nm] for nm in WEIGHTS],
            *[new_m[nm] for nm in WEIGHTS], *[new_v[nm] for nm in WEIGHTS])
```

```python
import functools
from typing import Callable, NamedTuple

import jax
import jax.numpy as jnp
from jax import lax
from jax.experimental import pallas as pl
from jax.experimental.pallas import tpu as pltpu

F32 = jnp.float32
MXU_DTYPE = jnp.bfloat16
WIRE_DTYPE = jnp.bfloat16
U_DTYPE = jnp.float32

D_MODEL = 1024
D_FF = 2816
QKV_DIM = 1536
HEAD_DIM = 64
N_HEADS = 16
N_KV_HEADS = 4
ATT_BLOCK = 128
POOL_WINDOWS = (2, 4, 8, 16)
POOL_GC = 256
POOL_HALO = 16
CONV_HALO = 8
RMS_EPS = 1e-6
ATT_SCALE = HEAD_DIM ** -0.5
ALIBI_SLOPES = tuple(2.0 ** (-8.0 / N_HEADS * (h + 1)) for h in range(N_HEADS))

ADAM_LR = 0.001
ADAM_B1 = 0.9
ADAM_B2 = 0.999
ADAM_EPS = 1e-08
ADAM_WD = 0.01
ADAM_STEP = 10

N_CHIPS = 4
MESH = pl.DeviceIdType.MESH
VMEM_LIMIT_BYTES = 56 * 1024 * 1024
ANY = pl.BlockSpec(memory_space=pl.ANY)


def _params(*semantics):
    return pltpu.CompilerParams(dimension_semantics=semantics, vmem_limit_bytes=VMEM_LIMIT_BYTES)


def _rms(x, g):
    return x * lax.rsqrt(jnp.mean(x * x, axis=-1, keepdims=True) + RMS_EPS) * g


def _rms_bwd(x, g, dy):
    rstd = lax.rsqrt(jnp.mean(x * x, axis=-1, keepdims=True) + RMS_EPS)
    xhat = x * rstd
    dxhat = dy * g
    dx = rstd * (dxhat - xhat * jnp.mean(dxhat * xhat, axis=-1, keepdims=True))
    return dx, dy * xhat


def _shift_down(x, s):
    return pltpu.roll(x, s, axis=0)


def _shift_up(x, s):
    return pltpu.roll(x, x.shape[0] - s, axis=0)


def _sigmoid(x):
    return 1.0 / (1.0 + jnp.exp(-x))


_DOT_DIMS = {"nn": ((1,), (0,)), "nt": ((1,), (1,)), "tn": ((0,), (0,))}


class _Hosted(NamedTuple):
    sources: list
    arrays: list
    new: list
    sems: list
    first: Callable
    last: Callable


def _call(body, hosted, *, name, grid, in_specs, out_specs, out_shape, operands, semantics, scratch_shapes=(),
          aliases=None):
    n_in, n_out, n_scr = len(in_specs), len(out_specs), len(scratch_shapes)
    aliases = dict(aliases or {})
    if hosted is None:
        def plain(*refs):
            body(refs[:n_in], refs[n_in:n_in + n_out], refs[n_in + n_out:], lambda: None, lambda: None)

        return pl.pallas_call(plain, name=name, grid=grid, in_specs=list(in_specs), out_specs=list(out_specs),
                              out_shape=list(out_shape), scratch_shapes=list(scratch_shapes),
                              input_output_aliases=aliases, compiler_params=_params(*semantics))(*operands)
    ns, na, nn = len(hosted.sources), len(hosted.arrays), len(hosted.new)

    def carrying(*refs):
        ins, src = refs[:n_in], refs[n_in:n_in + ns]
        o0 = n_in + ns + na
        outs, arr = refs[o0:o0 + n_out], refs[o0 + n_out:o0 + n_out + na]
        new = refs[o0 + n_out + na:o0 + n_out + na + nn]
        s0 = o0 + n_out + na + nn
        scratch, sems = refs[s0:s0 + n_scr], refs[s0 + n_scr:]
        ids = [pl.program_id(d) for d in range(len(grid))]
        is_first = functools.reduce(lambda p, q: p & q, [i == 0 for i in ids])
        is_last = functools.reduce(lambda p, q: p & q, [i == n - 1 for i, n in zip(ids, grid)])
        body(ins, outs, scratch, lambda: pl.when(is_first)(lambda: hosted.first(src, arr, new, sems)),
             lambda: pl.when(is_last)(lambda: hosted.last(src, arr, new, sems)))

    for t in range(na):
        aliases[n_in + ns + t] = n_out + t
    return pl.pallas_call(
        carrying, name=name, grid=grid, in_specs=list(in_specs) + [ANY] * (ns + na),
        out_specs=list(out_specs) + [ANY] * (na + nn),
        out_shape=list(out_shape) + [jax.ShapeDtypeStruct(a.shape, a.dtype) for a in hosted.arrays] + list(hosted.new),
        scratch_shapes=list(scratch_shapes) + list(hosted.sems), input_output_aliases=aliases,
        compiler_params=_params(*["arbitrary"] * len(grid)))(*operands, *hosted.sources, *hosted.arrays)


def _both(h1, h2):
    def run(which, src, arr, new, sems):
        cut = lambda seq, n: (seq[:n], seq[n:])
        (s1, s2), (a1, a2) = cut(src, len(h1.sources)), cut(arr, len(h1.arrays))
        (n1, n2), (m1, m2) = cut(new, len(h1.new)), cut(sems, len(h1.sems))
        getattr(h1, which)(s1, a1, n1, m1)
        getattr(h2, which)(s2, a2, n2, m2)

    return _Hosted(h1.sources + h2.sources, h1.arrays + h2.arrays, h1.new + h2.new, h1.sems + h2.sems,
                   functools.partial(run, "first"), functools.partial(run, "last"))


def _ride(hooks, stage, call, env=None):
    if not hooks or stage not in hooks:
        res = call(None)
        return list(res) if isinstance(res, (list, tuple)) else [res]
    make, done = hooks[stage]
    hosted = make(env)
    res = call(hosted)
    own = len(res) - len(hosted.arrays) - len(hosted.new)
    done(res[own:])
    return list(res[:own])


def _matmul(a, b, *, mode, name, grid, a_spec, b_spec, o_spec, out_shape, bias=None, bias_spec=None, residual=None,
            residual_spec=None, into=None, hosted=None, norm_g=None):
    nk = grid[2]
    dims = (_DOT_DIMS[mode], ((), ()))
    acc_shape = tuple(d for d in o_spec.block_shape if d is not None)

    def body(ins, outs, scratch, first, last):
        a_ref, b_ref = ins[0], ins[1]
        pos = 2
        bias_ref = res_ref = None
        if bias is not None:
            bias_ref = ins[pos]
            pos += 1
        if residual is not None:
            res_ref = ins[pos]
            pos += 1
        o_ref = outs[0]
        acc_ref = scratch[0] if nk > 1 else None
        first()
        prod = lax.dot_general(a_ref[...].astype(MXU_DTYPE), b_ref[...].astype(MXU_DTYPE), dims,
                               preferred_element_type=F32)

        def finish(acc):
            if bias_ref is not None:
                acc = acc + bias_ref[...]
            if res_ref is not None:
                acc = acc + res_ref[...]
            o_ref[...] = acc.astype(o_ref.dtype)
            if norm_g is not None:
                outs[1][...] = _rms(acc, ins[pos][...]).astype(outs[1].dtype)

        if nk == 1:
            finish(prod)
        else:
            k = pl.program_id(2)

            @pl.when(k == 0)
            def _():
                acc_ref[...] = prod

            @pl.when(k > 0)
            def _():
                acc_ref[...] += prod

            @pl.when(k == nk - 1)
            def _():
                finish(acc_ref[...])

        last()

    operands, in_specs = [a, b], [a_spec, b_spec]
    if bias is not None:
        operands.append(bias)
        in_specs.append(bias_spec)
    if residual is not None:
        operands.append(residual)
        in_specs.append(residual_spec)
    out_specs, out_shapes = [o_spec], [out_shape]
    if norm_g is not None:
        assert acc_shape[-1] == norm_g.shape[-1], "the fused norm needs whole rows in an output tile"
        operands.append(norm_g)
        in_specs.append(bias_spec)
        out_specs.append(o_spec)
        out_shapes.append(jax.ShapeDtypeStruct(out_shape.shape, MXU_DTYPE))
    aliases = {}
    if into is not None:
        aliases = {len(operands): 0}
        operands.append(into)
        in_specs.append(ANY)
    res = _call(body, hosted, name=name, grid=grid, in_specs=in_specs, out_specs=out_specs, out_shape=out_shapes,
                operands=operands, semantics=("parallel", "parallel", "arbitrary"),
                scratch_shapes=[pltpu.VMEM(acc_shape, F32)] if nk > 1 else [], aliases=aliases)
    return res if hosted or norm_g is not None else res[0]


def _tile(n, want):
    t = min(n, want)
    assert n % t == 0, (n, want)
    return t


def _mm_nn(a, b, name, *, b_lead=None, bias=None, residual=None, out_dtype=F32, tm=1024, tn=1024, tk=1024,
           hosted=None, norm_g=None):
    (m, k), n = a.shape, b.shape[-1]
    tm, tn, tk = _tile(m, tm), _tile(n, tn), _tile(k, tk)
    if b_lead is None:
        b_spec = pl.BlockSpec((tk, tn), lambda i, j, kk: (kk, j))
    else:
        b_spec = pl.BlockSpec((None, tk, tn), lambda i, j, kk: (b_lead, kk, j))
    return _matmul(
        a, b, mode="nn", name=name, grid=(m // tm, n // tn, k // tk),
        a_spec=pl.BlockSpec((tm, tk), lambda i, j, kk: (i, kk)), b_spec=b_spec,
        o_spec=pl.BlockSpec((tm, tn), lambda i, j, kk: (i, j)), out_shape=jax.ShapeDtypeStruct((m, n), out_dtype),
        bias=bias, bias_spec=pl.BlockSpec((1, tn), lambda i, j, kk: (0, j)),
        residual=residual, residual_spec=pl.BlockSpec((tm, tn), lambda i, j, kk: (i, j)), hosted=hosted,
        norm_g=norm_g)


def _mm_nt(a, b, name, *, b_lead=None, out_dtype=F32, tm=1024, tn=1024, tk=1024, hosted=None):
    (m, k), n = a.shape, b.shape[-2]
    tm, tn, tk = _tile(m, tm), _tile(n, tn), _tile(k, tk)
    if b_lead is None:
        b_spec = pl.BlockSpec((tn, tk), lambda i, j, kk: (j, kk))
    else:
        b_spec = pl.BlockSpec((None, tn, tk), lambda i, j, kk: (b_lead, j, kk))
    return _matmul(
        a, b, mode="nt", name=name, grid=(m // tm, n // tn, k // tk),
        a_spec=pl.BlockSpec((tm, tk), lambda i, j, kk: (i, kk)), b_spec=b_spec,
        o_spec=pl.BlockSpec((tm, tn), lambda i, j, kk: (i, j)), out_shape=jax.ShapeDtypeStruct((m, n), out_dtype),
        hosted=hosted)


def _mm_tn(a, b, name, *, lead=None, n_lead=None, into=None, tm=1024, tn=1024, tk=1024):
    (k, m), n = a.shape, b.shape[-1]
    tm, tn, tk = _tile(m, tm), _tile(n, tn), _tile(k, tk)
    if lead is None:
        o_spec = pl.BlockSpec((tm, tn), lambda i, j, kk: (i, j))
        out_shape = jax.ShapeDtypeStruct((m, n), F32)
    else:
        o_spec = pl.BlockSpec((None, tm, tn), lambda i, j, kk: (lead, i, j))
        out_shape = jax.ShapeDtypeStruct((n_lead, m, n), F32)
    return _matmul(
        a, b, mode="tn", name=name, grid=(m // tm, n // tn, k // tk),
        a_spec=pl.BlockSpec((tk, tm), lambda i, j, kk: (kk, i)),
        b_spec=pl.BlockSpec((tk, tn), lambda i, j, kk: (kk, j)), o_spec=o_spec, out_shape=out_shape, into=into)


ROW_TILE = 512


def _loss_call(h, g, target):
    t = h.shape[0]
    tt = _tile(t, ROW_TILE)

    def body(h_ref, g_ref, t_ref, loss_ref, dh_ref, dh16_ref, dg_ref):
        x, gv = h_ref[...], g_ref[...]
        err = _rms(x, gv) - t_ref[...]
        dx, dgt = _rms_bwd(x, gv, err * (1.0 / D_MODEL))
        dh_ref[...] = dx
        dh16_ref[...] = dx.astype(dh16_ref.dtype)

        @pl.when(pl.program_id(0) == 0)
        def _():
            dg_ref[...] = jnp.zeros_like(dg_ref)
            loss_ref[...] = jnp.zeros_like(loss_ref)

        dg_ref[...] += jnp.sum(dgt, axis=0, keepdims=True)
        per_token = jnp.mean(err * err, axis=-1, keepdims=True)
        loss_ref[...] += 0.5 * jnp.sum(per_token, axis=0, keepdims=True)

    row = pl.BlockSpec((tt, D_MODEL), lambda i: (i, 0))
    vec = pl.BlockSpec((1, D_MODEL), lambda i: (0, 0))
    one = pl.BlockSpec((1, 1), lambda i: (0, 0))
    return pl.pallas_call(
        body, name="loss_head", grid=(t // tt,), in_specs=[row, vec, row], out_specs=[one, row, row, vec],
        out_shape=[jax.ShapeDtypeStruct((1, 1), F32), jax.ShapeDtypeStruct(h.shape, F32),
                   jax.ShapeDtypeStruct(h.shape, MXU_DTYPE), jax.ShapeDtypeStruct((1, D_MODEL), F32)],
        compiler_params=_params("arbitrary"))(h, g, target)


def _pool_windows_causal(ext, first_row):
    tt = ext.shape[0] - POOL_HALO
    t = first_row + lax.broadcasted_iota(jnp.int32, (tt, 1), 0)
    outs = []
    for gi, win in enumerate(POOL_WINDOWS):
        cols = slice(gi * POOL_GC, (gi + 1) * POOL_GC)
        s = ext[:, cols]
        sh = 1
        while sh < win:
            s = s + _shift_down(s, sh)
            sh *= 2
        count = jnp.minimum(t + 1, win).astype(F32)
        outs.append(s[POOL_HALO:] / count - ext[POOL_HALO:, cols])
    return outs


def _prev_halo_spec(tt, halo, width):
    per = tt // halo
    return pl.BlockSpec((halo, width), lambda i: (jnp.maximum(i * per - 1, 0), 0))


def _pool_fwd_call(h0, g_mix, pool_w, pool_b, pool_scale, g_ffn):
    t = h0.shape[0]
    tt = _tile(t, ROW_TILE)

    def body(h_ref, halo_ref, gm_ref, w_ref, b_ref, sc_ref, gf_ref, h1_ref, hn_ref):
        i = pl.program_id(0)
        x, gm = h_ref[...], gm_ref[...]
        top = jnp.where(i > 0, _rms(halo_ref[...], gm), 0.0)
        ext = jnp.concatenate([top, _rms(x, gm)], axis=0)
        ps = _pool_windows_causal(ext, i * tt)
        ys = [jnp.dot(p.astype(MXU_DTYPE), w_ref[gi], preferred_element_type=F32) for gi, p in enumerate(ps)]
        mix = (jnp.concatenate(ys, axis=1) + b_ref[...]) * sc_ref[...]
        h1 = x + mix
        h1_ref[...] = h1
        hn_ref[...] = _rms(h1, gf_ref[...]).astype(hn_ref.dtype)

    row = pl.BlockSpec((tt, D_MODEL), lambda i: (i, 0))
    vec = pl.BlockSpec((1, D_MODEL), lambda i: (0, 0))
    wsp = pl.BlockSpec((len(POOL_WINDOWS), POOL_GC, POOL_GC), lambda i: (0, 0, 0))
    return pl.pallas_call(
        body, name="pool_fwd", grid=(t // tt,),
        in_specs=[row, _prev_halo_spec(tt, POOL_HALO, D_MODEL), vec, wsp, vec, vec, vec], out_specs=[row, row],
        out_shape=[jax.ShapeDtypeStruct(h0.shape, F32), jax.ShapeDtypeStruct(h0.shape, MXU_DTYPE)],
        compiler_params=_params("parallel"))(h0, h0, g_mix, pool_w, pool_b, pool_scale, g_ffn)


def _pool_bwd_call(h0, dh1, g_mix, pool_w, pool_b, pool_scale, hosted=None):
    t = h0.shape[0]
    tt = _tile(t, ROW_TILE)
    nt = t // tt
    per = tt // POOL_HALO

    def body(ins, outs, scratch, first, last):
        h_ref, halo_ref, d_ref, dnext_ref, gm_ref, w_ref, b_ref, sc_ref = ins
        gx_ref, dw_ref, dv_ref = outs
        i = pl.program_id(0)
        first()
        x, gm, sc = h_ref[...], gm_ref[...], sc_ref[...]
        top = jnp.where(i > 0, _rms(halo_ref[...], gm), 0.0)
        ext = jnp.concatenate([top, _rms(x, gm)], axis=0)
        ps = _pool_windows_causal(ext, i * tt)

        dy = d_ref[...]
        dy_ext = jnp.concatenate([dy, jnp.where(i < nt - 1, dnext_ref[...], 0.0)], axis=0)
        dpre_ext = dy_ext * sc
        t_ext = i * tt + lax.broadcasted_iota(jnp.int32, (tt + POOL_HALO, 1), 0)

        @pl.when(i == 0)
        def _():
            dw_ref[...] = jnp.zeros_like(dw_ref)
            dv_ref[...] = jnp.zeros_like(dv_ref)

        dhn, ypre = [], []
        for gi, win in enumerate(POOL_WINDOWS):
            cols = slice(gi * POOL_GC, (gi + 1) * POOL_GC)
            w = w_ref[gi]
            p16 = ps[gi].astype(MXU_DTYPE)
            dpre16 = dpre_ext[:, cols].astype(MXU_DTYPE)
            ypre.append(jnp.dot(p16, w, preferred_element_type=F32))
            dw_ref[gi] += lax.dot_general(p16, dpre16[:tt], (((0,), (0,)), ((), ())), preferred_element_type=F32)
            dp_ext = lax.dot_general(dpre16, w, (((1,), (1,)), ((), ())), preferred_element_type=F32)
            s = dp_ext / jnp.minimum(t_ext + 1, win).astype(F32)
            sh = 1
            while sh < win:
                s = s + _shift_up(s, sh)
                sh *= 2
            dhn.append(s[:tt] - dp_ext[:tt])
        dhn = jnp.concatenate(dhn, axis=1)
        ypre = jnp.concatenate(ypre, axis=1) + b_ref[...]
        dx, dgt = _rms_bwd(x, gm, dhn)
        gx_ref[...] = dy + dx
        dv_ref[0:1, :] += jnp.sum(dpre_ext[:tt], axis=0, keepdims=True)
        dv_ref[1:2, :] += jnp.sum(dy * ypre, axis=0, keepdims=True)
        dv_ref[2:3, :] += jnp.sum(dgt, axis=0, keepdims=True)
        last()

    row = pl.BlockSpec((tt, D_MODEL), lambda i: (i, 0))
    vec = pl.BlockSpec((1, D_MODEL), lambda i: (0, 0))
    nxt = pl.BlockSpec((POOL_HALO, D_MODEL), lambda i: (jnp.minimum((i + 1) * per, t // POOL_HALO - 1), 0))
    wsp = pl.BlockSpec((len(POOL_WINDOWS), POOL_GC, POOL_GC), lambda i: (0, 0, 0))
    return _call(
        body, hosted, name="pool_bwd", grid=(nt,),
        in_specs=[row, _prev_halo_spec(tt, POOL_HALO, D_MODEL), row, nxt, vec, wsp, vec, vec],
        out_specs=[row, wsp, pl.BlockSpec((8, D_MODEL), lambda i: (0, 0))],
        out_shape=[jax.ShapeDtypeStruct(h0.shape, F32),
                   jax.ShapeDtypeStruct((len(POOL_WINDOWS), POOL_GC, POOL_GC), F32),
                   jax.ShapeDtypeStruct((8, D_MODEL), F32)],
        operands=(h0, h0, dh1, dh1, g_mix, pool_w, pool_b, pool_scale), semantics=("arbitrary",))


CONV_ROWS = 256
CONV_SUB = 64
LANES = 128


def _causal_conv(u_ext, w, b):
    return w[0:1] * _shift_down(u_ext, 2) + w[1:2] * _shift_down(u_ext, 1) + w[2:3] * u_ext + b


def _fold8(x):
    acc = x[0:8]
    for r in range(8, x.shape[0], 8):
        acc = acc + x[r:r + 8]
    return acc


def _conv_glu_fwd_call(u, conv_w, conv_b, name, hosted=None):
    _, t, f = u.shape
    tt = _tile(t, CONV_ROWS)
    sub = _tile(tt, CONV_SUB)
    per = tt // CONV_HALO

    def body(ins, outs, scratch, first, last):
        u_ref, up_ref, w_ref, b_ref = ins
        z_ref, ext_ref = outs[0], scratch[0]
        i = pl.program_id(0)
        first()
        ext_ref[:, :CONV_HALO, :] = jnp.where(i > 0, up_ref[...].astype(F32), 0.0)
        ext_ref[:, CONV_HALO:, :] = u_ref[...].astype(F32)

        def chunk(j, carry):
            lanes = pl.ds(pl.multiple_of(j * LANES, LANES), LANES)
            w, b = w_ref[:, :, lanes], b_ref[:, lanes]
            for s in range(tt // sub):
                rows = pl.ds(s * sub, sub + CONV_HALO)
                c = [_causal_conv(ext_ref[hf, rows, lanes], w[:, hf, :], b[hf:hf + 1, :])[CONV_HALO:] for hf in range(2)]
                z_ref[pl.ds(s * sub, sub), lanes] = (c[0] * c[1] * _sigmoid(c[1])).astype(z_ref.dtype)
            return carry

        lax.fori_loop(0, f // LANES, chunk, 0)
        last()

    res = _call(
        body, hosted, name=name, grid=(t // tt,),
        in_specs=[pl.BlockSpec((2, tt, f), lambda i: (0, i, 0)),
                  pl.BlockSpec((2, CONV_HALO, f), lambda i: (0, jnp.maximum(i * per - 1, 0), 0)),
                  pl.BlockSpec((3, 2, f), lambda i: (0, 0, 0)), pl.BlockSpec((2, f), lambda i: (0, 0))],
        out_specs=[pl.BlockSpec((tt, f), lambda i: (i, 0))], out_shape=[jax.ShapeDtypeStruct((t, f), MXU_DTYPE)],
        operands=(u, u, conv_w, conv_b), semantics=("parallel",),
        scratch_shapes=[pltpu.VMEM((2, CONV_HALO + tt, f), F32)])
    return res if hosted else res[0]


def _conv_glu_bwd_call(u, dz, conv_w, conv_b, name, hosted=None):
    _, t, f = u.shape
    tt = _tile(t, CONV_ROWS)
    sub = _tile(tt, 2 * CONV_SUB)
    nt = t // tt
    per = tt // CONV_HALO
    halo = CONV_HALO

    def body(ins, outs, scratch, first, last):
        u_ref, up_ref, un_ref, dz_ref, dzn_ref, w_ref, b_ref = ins
        du_ref, dw_ref, db_ref = outs
        uext, dzext, dwacc, dbacc = scratch
        i = pl.program_id(0)
        first()
        uext[:, :halo, :] = jnp.where(i > 0, up_ref[...].astype(F32), 0.0)
        uext[:, halo:halo + tt, :] = u_ref[...].astype(F32)
        uext[:, halo + tt:, :] = un_ref[...].astype(F32)
        dzext[:halo, :] = jnp.zeros((halo, f), F32)
        dzext[halo:halo + tt, :] = dz_ref[...].astype(F32)
        dzext[halo + tt:, :] = jnp.where(i < nt - 1, dzn_ref[...].astype(F32), 0.0)

        @pl.when(i == 0)
        def _():
            dwacc[...] = jnp.zeros_like(dwacc)
            dbacc[...] = jnp.zeros_like(dbacc)

        main = slice(halo, halo + sub)

        def chunk(j, carry):
            lanes = pl.ds(pl.multiple_of(j * LANES, LANES), LANES)
            w, b = w_ref[:, :, lanes], b_ref[:, lanes]
            for s in range(tt // sub):
                rows = pl.ds(s * sub, sub + 2 * halo)
                ue = [uext[hf, rows, lanes] for hf in range(2)]
                a, g = [_causal_conv(ue[hf], w[:, hf, :], b[hf:hf + 1, :]) for hf in range(2)]
                sg = _sigmoid(g)
                dzs = dzext[rows, lanes]
                gs = g * sg
                dc = [dzs * gs, dzs * a * (sg + gs - gs * sg)]
                for hf in range(2):
                    d = dc[hf]
                    d1, d2 = _shift_up(d, 1), _shift_up(d, 2)
                    du = w[2:3, hf, :] * d + w[1:2, hf, :] * d1 + w[0:1, hf, :] * d2
                    du_ref[hf, pl.ds(s * sub, sub), lanes] = du[main].astype(du_ref.dtype)
                    um = ue[hf][main]
                    for k, dd in enumerate((d2, d1, d)):
                        dwacc[k, hf, :, lanes] += _fold8(dd[main] * um)
                    dbacc[hf, :, lanes] += _fold8(d[main])
            return carry

        lax.fori_loop(0, f // LANES, chunk, 0)

        @pl.when(i == nt - 1)
        def _():
            dw_ref[...] = jnp.sum(dwacc[...], axis=2)
            db_ref[...] = jnp.sum(dbacc[...], axis=1)

        last()

    end = t // halo - 1
    return _call(
        body, hosted, name=name, grid=(nt,),
        in_specs=[pl.BlockSpec((2, tt, f), lambda i: (0, i, 0)),
                  pl.BlockSpec((2, halo, f), lambda i: (0, jnp.maximum(i * per - 1, 0), 0)),
                  pl.BlockSpec((2, halo, f), lambda i: (0, jnp.minimum((i + 1) * per, end), 0)),
                  pl.BlockSpec((tt, f), lambda i: (i, 0)),
                  pl.BlockSpec((halo, f), lambda i: (jnp.minimum((i + 1) * per, end), 0)),
                  pl.BlockSpec((3, 2, f), lambda i: (0, 0, 0)), pl.BlockSpec((2, f), lambda i: (0, 0))],
        out_specs=[pl.BlockSpec((2, tt, f), lambda i: (0, i, 0)),
                   pl.BlockSpec((3, 2, f), lambda i: (0, 0, 0)), pl.BlockSpec((2, f), lambda i: (0, 0))],
        out_shape=[jax.ShapeDtypeStruct(u.shape, MXU_DTYPE), jax.ShapeDtypeStruct((3, 2, f), F32),
                   jax.ShapeDtypeStruct((2, f), F32)],
        operands=(u, u, u, dz, dz, conv_w, conv_b), semantics=("arbitrary",),
        scratch_shapes=[pltpu.VMEM((2, tt + 2 * halo, f), F32), pltpu.VMEM((tt + 2 * halo, f), F32),
                        pltpu.VMEM((3, 2, 8, f), F32), pltpu.VMEM((2, 8, f), F32)])


def _kv_even_odd(band, k):
    pair, parity = divmod(k, 2)
    blk = band[:, 128 * pair:128 * (pair + 1)].astype(F32)
    lane = lax.broadcasted_iota(jnp.int32, blk.shape, 1)
    own = jnp.where((lane >= HEAD_DIM) == (parity == 1), blk, 0.0)
    swapped = pltpu.roll(own, HEAD_DIM, axis=1)
    even, odd = (own, swapped) if parity == 0 else (swapped, own)
    return even.astype(MXU_DTYPE), odd.astype(MXU_DTYPE)


def _stack_pairs(ref, k):
    return jnp.concatenate([ref[:, 256 * k:256 * k + 128], ref[:, 256 * k + 128:256 * k + 256]], axis=0)


_GROUP_ORDER = (0, 2, 1, 3)


def _attn_fill_tables(bias_ref):
    shape = (4 * ATT_BLOCK, 2 * ATT_BLOCK)
    row = lax.broadcasted_iota(jnp.int32, shape, 0)
    col = lax.broadcasted_iota(jnp.int32, shape, 1)
    dist = (row & (ATT_BLOCK - 1)) + ATT_BLOCK - col
    ok = (dist >= 0) & (dist < ATT_BLOCK)
    distf = dist.astype(F32)
    rb = lax.broadcasted_iota(jnp.int32, (shape[0], 1), 0) // ATT_BLOCK
    for k in range(N_KV_HEADS):
        slope = jnp.zeros((shape[0], 1), F32)
        for r, gq in enumerate(_GROUP_ORDER):
            slope = jnp.where(rb == r, ALIBI_SLOPES[4 * k + gq], slope)
        bias = jnp.where(ok, -slope * distf, -jnp.inf)
        bias_ref[0, k] = bias
        bias_ref[1, k] = jnp.where(col >= ATT_BLOCK, bias, -jnp.inf)


def _attn_probs(qs, k_even, k_odd, bias, k, sink_ref):
    nt_dims = (((1,), (1,)), ((), ()))
    s = jnp.concatenate([lax.dot_general(qs, k_even, nt_dims, preferred_element_type=F32),
                         lax.dot_general(qs, k_odd, nt_dims, preferred_element_type=F32)], axis=0) + bias
    rb = lax.broadcasted_iota(jnp.int32, (s.shape[0], 1), 0) // ATT_BLOCK
    sink = jnp.zeros((s.shape[0], 1), F32)
    for r, gq in enumerate(_GROUP_ORDER):
        sink = jnp.where(rb == r, sink_ref[4 * k + gq], sink)
    m = jnp.maximum(jnp.max(s, axis=-1, keepdims=True), sink)
    p = jnp.exp(s - m)
    es = jnp.exp(sink - m)
    return p, 1.0 / (jnp.sum(p, axis=-1, keepdims=True) + es), es


_ATTN_TABLES = [pltpu.VMEM((2, N_KV_HEADS, 4 * ATT_BLOCK, 2 * ATT_BLOCK), F32)]


def _attn_specs(with_do):
    prev = lambda n: jnp.maximum(n - 1, 0)
    specs = [pl.BlockSpec(memory_space=pltpu.SMEM),
             pl.BlockSpec((ATT_BLOCK, D_MODEL), lambda n: (n, 0)),
             pl.BlockSpec((ATT_BLOCK, 256), lambda n: (prev(n), 4)), pl.BlockSpec((ATT_BLOCK, 256), lambda n: (n, 4)),
             pl.BlockSpec((ATT_BLOCK, 256), lambda n: (prev(n), 5)), pl.BlockSpec((ATT_BLOCK, 256), lambda n: (n, 5))]
    if with_do:
        specs.append(pl.BlockSpec((ATT_BLOCK, D_MODEL), lambda n: (n, 0)))
    return specs


def _attn_fwd_call(qkv, sinks, hosted=None):
    t = qkv.shape[0]

    def body(ins, outs, scratch, first, last):
        sink_ref, q_ref, kp_ref, kc_ref, vp_ref, vc_ref = ins
        o_ref, bias_ref = outs[0], scratch[0]
        n = pl.program_id(0)
        first()

        @pl.when(n == 0)
        def _():
            _attn_fill_tables(bias_ref)

        first = 1 - jnp.minimum(n, 1)
        kband = jnp.concatenate([kp_ref[...], kc_ref[...]], axis=0)
        vband = jnp.concatenate([vp_ref[...], vc_ref[...]], axis=0)
        for k in range(N_KV_HEADS):
            k_even, k_odd = _kv_even_odd(kband, k)
            v_even, v_odd = _kv_even_odd(vband, k)
            qs = _stack_pairs(q_ref, k) * ATT_SCALE
            p, rdenom, _ = _attn_probs(qs, k_even, k_odd, bias_ref[first, k], k, sink_ref)
            probs = (p * rdenom).astype(MXU_DTYPE)
            o = (jnp.dot(probs[:256], v_even, preferred_element_type=F32)
                 + jnp.dot(probs[256:], v_odd, preferred_element_type=F32))
            o_ref[:, 256 * k:256 * k + 128] = o[:128].astype(o_ref.dtype)
            o_ref[:, 256 * k + 128:256 * k + 256] = o[128:].astype(o_ref.dtype)
        last()

    return _call(
        body, hosted, name="attn_fwd", grid=(t // ATT_BLOCK,), in_specs=_attn_specs(False),
        out_specs=[pl.BlockSpec((ATT_BLOCK, D_MODEL), lambda n: (n, 0))],
        out_shape=[jax.ShapeDtypeStruct((t, D_MODEL), MXU_DTYPE)], operands=(sinks, qkv, qkv, qkv, qkv, qkv),
        semantics=("arbitrary",), scratch_shapes=_ATTN_TABLES)


def _attn_bwd_call(qkv, sinks, do, hosted=None):
    t = qkv.shape[0]
    nb = t // ATT_BLOCK
    tn_dims = (((0,), (0,)), ((), ()))
    nt_dims = (((1,), (1,)), ((), ()))

    def to_native(even_part, odd_part, parity):
        lane = lax.broadcasted_iota(jnp.int32, even_part.shape, 1)
        lo = lane < HEAD_DIM
        e, o = jnp.where(lo, even_part, 0.0), jnp.where(lo, 0.0, odd_part)
        if parity == 0:
            return e + pltpu.roll(o, HEAD_DIM, axis=1)
        return pltpu.roll(e, HEAD_DIM, axis=1) + o

    def body(ins, outs, scratch, first, last):
        sink_ref, q_ref, kp_ref, kc_ref, vp_ref, vc_ref, do_ref = ins
        dq_ref, dk_ref, dv_ref, db_ref, dsink_ref = outs
        bias_ref = scratch[0]
        n = pl.program_id(0)
        first()

        @pl.when(n == 0)
        def _():
            _attn_fill_tables(bias_ref)
            dk_ref[...] = jnp.zeros_like(dk_ref)
            dv_ref[...] = jnp.zeros_like(dv_ref)
            db_ref[...] = jnp.zeros_like(db_ref)
            dsink_ref[...] = jnp.zeros_like(dsink_ref)

        kband = jnp.concatenate([kp_ref[...], kc_ref[...]], axis=0)
        vband = jnp.concatenate([vp_ref[...], vc_ref[...]], axis=0)
        dk_pairs = [jnp.zeros((2 * ATT_BLOCK, 128), F32), jnp.zeros((2 * ATT_BLOCK, 128), F32)]
        dv_pairs = [jnp.zeros((2 * ATT_BLOCK, 128), F32), jnp.zeros((2 * ATT_BLOCK, 128), F32)]
        sink_lane = lax.broadcasted_iota(jnp.int32, (8, 128), 1)
        sink_row = lax.broadcasted_iota(jnp.int32, (8, 128), 0)
        dsink = jnp.zeros((8, 128), F32)
        first = 1 - jnp.minimum(n, 1)
        for k in range(N_KV_HEADS):
            k_even, k_odd = _kv_even_odd(kband, k)
            v_even, v_odd = _kv_even_odd(vband, k)
            qs = _stack_pairs(q_ref, k) * ATT_SCALE
            dos = _stack_pairs(do_ref, k)
            p, rdenom, es = _attn_probs(qs, k_even, k_odd, bias_ref[first, k], k, sink_ref)
            probs = p * rdenom
            dp = jnp.concatenate([lax.dot_general(dos, v_even, nt_dims, preferred_element_type=F32),
                                  lax.dot_general(dos, v_odd, nt_dims, preferred_element_type=F32)], axis=0)
            delta = jnp.sum(probs * dp, axis=-1, keepdims=True)
            ds16 = (probs * (dp - delta)).astype(MXU_DTYPE)
            p16 = probs.astype(MXU_DTYPE)
            dsink_rows = -(es * rdenom) * delta
            for r, gq in enumerate(_GROUP_ORDER):
                tot = jnp.sum(dsink_rows[ATT_BLOCK * r:ATT_BLOCK * (r + 1)], axis=0, keepdims=True)
                dsink = dsink + jnp.where((sink_lane == 4 * k + gq) & (sink_row == 0), tot, 0.0)
            dqs = (jnp.dot(ds16[:256], k_even, preferred_element_type=F32)
                   + jnp.dot(ds16[256:], k_odd, preferred_element_type=F32)) * ATT_SCALE
            dq_ref[:, 256 * k:256 * k + 128] = dqs[:128].astype(dq_ref.dtype)
            dq_ref[:, 256 * k + 128:256 * k + 256] = dqs[128:].astype(dq_ref.dtype)
            db_ref[:, 256 * k:256 * k + 128] += jnp.sum(dqs[:128], axis=0, keepdims=True)
            db_ref[:, 256 * k + 128:256 * k + 256] += jnp.sum(dqs[128:], axis=0, keepdims=True)
            pair, parity = divmod(k, 2)
            dk_pairs[pair] = dk_pairs[pair] + to_native(
                lax.dot_general(ds16[:256], qs, tn_dims, preferred_element_type=F32),
                lax.dot_general(ds16[256:], qs, tn_dims, preferred_element_type=F32), parity)
            dv_pairs[pair] = dv_pairs[pair] + to_native(
                lax.dot_general(p16[:256], dos, tn_dims, preferred_element_type=F32),
                lax.dot_general(p16[256:], dos, tn_dims, preferred_element_type=F32), parity)
        dsink_ref[...] += dsink
        dk_band = jnp.concatenate(dk_pairs, axis=1)
        dv_band = jnp.concatenate(dv_pairs, axis=1)
        cur = pl.multiple_of(n * ATT_BLOCK, ATT_BLOCK)
        dk_ref[pl.ds(cur, ATT_BLOCK), :] += dk_band[ATT_BLOCK:]
        dv_ref[pl.ds(cur, ATT_BLOCK), :] += dv_band[ATT_BLOCK:]

        @pl.when(n > 0)
        def _():
            prv = pl.multiple_of((n - 1) * ATT_BLOCK, ATT_BLOCK)
            dk_ref[pl.ds(prv, ATT_BLOCK), :] += dk_band[:ATT_BLOCK]
            dv_ref[pl.ds(prv, ATT_BLOCK), :] += dv_band[:ATT_BLOCK]

        @pl.when(n == nb - 1)
        def _():
            db_ref[:, 1024:1280] = jnp.sum(dk_ref[...], axis=0, keepdims=True)
            db_ref[:, 1280:1536] = jnp.sum(dv_ref[...], axis=0, keepdims=True)

        last()

    whole = lambda shape: pl.BlockSpec(shape, lambda n: (0, 0))
    return _call(
        body, hosted, name="attn_bwd", grid=(nb,), in_specs=_attn_specs(True),
        out_specs=[pl.BlockSpec((ATT_BLOCK, D_MODEL), lambda n: (n, 0)), whole((t, 256)), whole((t, 256)),
                   whole((1, QKV_DIM)), whole((8, 128))],
        out_shape=[jax.ShapeDtypeStruct((t, D_MODEL), MXU_DTYPE), jax.ShapeDtypeStruct((t, 256), F32),
                   jax.ShapeDtypeStruct((t, 256), F32), jax.ShapeDtypeStruct((1, QKV_DIM), F32),
                   jax.ShapeDtypeStruct((8, 128), F32)],
        operands=(sinks, qkv, qkv, qkv, qkv, qkv, do), semantics=("arbitrary",), scratch_shapes=_ATTN_TABLES)


FF_TILE = 1408


def _ffn_fwd(hn, h, w, conv_w, conv_b, layer, hooks=None, next_norm=None):
    t = hn.shape[0]
    tm = _tile(t, 1024)
    halves = D_FF // FF_TILE
    u, = _ride(hooks, "in", lambda hosted: _matmul(
        hn, w["w_in"], mode="nn", name=f"ffn{layer}_in", grid=(2 * halves, t // tm, 1),
        a_spec=pl.BlockSpec((tm, D_MODEL), lambda j, i, k: (i, 0)),
        b_spec=pl.BlockSpec((None, D_MODEL, FF_TILE), lambda j, i, k: (layer, 0, j)),
        o_spec=pl.BlockSpec((None, tm, FF_TILE), lambda j, i, k: (j // halves, i, j % halves)),
        out_shape=jax.ShapeDtypeStruct((2, t, D_FF), U_DTYPE), hosted=hosted))
    z, = _ride(hooks, "glu", lambda hosted: _conv_glu_fwd_call(u, conv_w, conv_b, f"ffn{layer}_glu", hosted=hosted))
    outs = _ride(hooks, "out", lambda hosted: _mm_nn(z, w["w_out"], f"ffn{layer}_out", b_lead=layer, residual=h,
                                                     tk=D_FF, hosted=hosted, norm_g=next_norm))
    return outs[0], (outs[1] if next_norm is not None else None), u, z


def _ffn_bwd(dh, hn, u, z, w_in, w_out, conv_w, conv_b, layer, g_in_acc, g_out_acc, norm, hooks=None):
    t = hn.shape[0]
    dz, = _ride(hooks, "dz", lambda hosted: _mm_nt(dh, w_out, f"ffn{layer}_dz", b_lead=layer, out_dtype=MXU_DTYPE,
                                                   tn=FF_TILE, hosted=hosted))
    g_out = _mm_tn(z, dh, f"ffn{layer}_dwout", lead=layer, n_lead=2, into=g_out_acc, tm=WGRAD_TILE, tk=t)
    du, dcw, dcb = _ride(hooks, "dglu", lambda hosted: _conv_glu_bwd_call(u, dz, conv_w, conv_b, f"ffn{layer}_dglu",
                                                                          hosted=hosted), {"g_out": g_out})
    per_half = D_FF // WGRAD_TILE
    g_in, = _ride(hooks, "dwin", lambda hosted: _matmul(
        hn, du, mode="tn", name=f"ffn{layer}_dwin", grid=(1, 2 * per_half, 1),
        a_spec=pl.BlockSpec((t, D_MODEL), lambda i, j, k: (0, 0)),
        b_spec=pl.BlockSpec((None, t, WGRAD_TILE), lambda i, j, k: (j // per_half, 0, j % per_half)),
        o_spec=pl.BlockSpec((None, D_MODEL, WGRAD_TILE), lambda i, j, k: (layer, 0, j)),
        out_shape=jax.ShapeDtypeStruct((2, D_MODEL, 2 * D_FF), F32), into=g_in_acc, hosted=hosted))
    dx, dx16, dg = _ride(hooks, "dhn", lambda hosted: _ffn_dhn_call(du, w_in, layer, norm, hosted),
                         {"g_in": g_in, "g_out": g_out})
    return dx, dx16, dg, g_in, g_out, dcw, dcb


WGRAD_TILE = 256


_NT_DIMS = (((1,), (1,)), ((), ()))


def _dhn_call(name, a, a_spec, w, w_spec, product, t, tm, norm, hosted=None):
    h, g, dres = norm

    def body(ins, outs, scratch, first, last):
        a_ref, w_ref, h_ref, g_ref, dres_ref = ins
        dx_ref, dx16_ref, dg_ref = outs
        first()
        dx, dgt = _rms_bwd(h_ref[...], g_ref[...], product(a_ref, w_ref))
        dx = dres_ref[...] + dx
        dx_ref[...] = dx
        dx16_ref[...] = dx.astype(dx16_ref.dtype)

        @pl.when(pl.program_id(0) == 0)
        def _():
            dg_ref[...] = jnp.zeros_like(dg_ref)

        dg_ref[...] += jnp.sum(dgt, axis=0, keepdims=True)
        last()

    row = pl.BlockSpec((tm, D_MODEL), lambda i: (i, 0))
    vec = pl.BlockSpec((1, D_MODEL), lambda i: (0, 0))
    return _call(
        body, hosted, name=name, grid=(t // tm,), in_specs=[a_spec, w_spec, row, vec, row], out_specs=[row, row, vec],
        out_shape=[jax.ShapeDtypeStruct((t, D_MODEL), F32), jax.ShapeDtypeStruct((t, D_MODEL), MXU_DTYPE),
                   jax.ShapeDtypeStruct((1, D_MODEL), F32)],
        operands=(a, w, h, g, dres), semantics=("arbitrary",))


def _ffn_dhn_call(du, w_in, layer, norm, hosted=None):
    _, t, f = du.shape
    tm = _tile(t, 512)

    def product(du_ref, w_ref):
        return (lax.dot_general(du_ref[0], w_ref[:, :f], _NT_DIMS, preferred_element_type=F32)
                + lax.dot_general(du_ref[1], w_ref[:, f:], _NT_DIMS, preferred_element_type=F32))

    return _dhn_call(f"ffn{layer}_dhn", du, pl.BlockSpec((2, tm, f), lambda i: (0, i, 0)), w_in,
                     pl.BlockSpec((None, D_MODEL, 2 * f), lambda i: (layer, 0, 0)), product, t, tm, norm, hosted)


def _attn_dhn_call(dqkv, w_qkv, norm):
    t, n = dqkv.shape
    tm = _tile(t, 1024)

    def product(a_ref, w_ref):
        return lax.dot_general(a_ref[...], w_ref[...], _NT_DIMS, preferred_element_type=F32)

    return _dhn_call("attn_dhn", dqkv, pl.BlockSpec((tm, n), lambda i: (i, 0)), w_qkv,
                     pl.BlockSpec((D_MODEL, n), lambda i: (0, 0)), product, t, tm, norm)


def _local_step(x, target, w, *, norm_mix, norm_ffn, norm_f, pool_b, pool_scale, b_qkv, sinks, conv_w, conv_b,
                late_gather=None, reducer=None):
    cw = [conv_w[l].reshape(3, 2, D_FF) for l in range(2)]
    cb = [conv_b[l].reshape(2, D_FF) for l in range(2)]
    gm = [norm_mix[l:l + 1] for l in range(2)]
    gf = [norm_ffn[l:l + 1] for l in range(2)]

    w = dict(w)
    fwd_hooks = {}
    if late_gather:
        def gather_hook(stage):
            return (lambda env: late_gather(stage, w)), (lambda got: w.update(zip(_late_names(stage), got)))

        fwd_hooks = {stage: gather_hook(stage) for stage in GATHER_RIDES}

    h1, hn_f0 = _pool_fwd_call(x, gm[0], w["pool_w"], pool_b, pool_scale, gf[0])
    h2, hn_a, u0, z0 = _ffn_fwd(hn_f0, h1, w, cw[0], cb[0], 0, hooks=fwd_hooks, next_norm=gm[1])
    qkv = _mm_nn(hn_a, w["w_qkv"], "attn_qkv", bias=b_qkv, out_dtype=MXU_DTYPE, tn=768)
    o, = _ride(fwd_hooks, "attn_fwd", lambda hosted: _attn_fwd_call(qkv, sinks, hosted))
    h3, hn_f1 = _ride(fwd_hooks, "attn_out", lambda hosted: _mm_nn(o, w["w_o"], "attn_out", residual=h2, hosted=hosted,
                                                                   norm_g=gf[1]))
    h4, _, u1, z1 = _ffn_fwd(hn_f1, h3, w, cw[1], cb[1], 1)
    loss, dh, dh16, d_norm_f = _loss_call(h4, norm_f, target)

    rd = reducer
    first, mid, last = _Reducer.FIRST, _Reducer.MID, _Reducer.LAST
    hooks1 = hooks_attn = hooks0 = hooks_pool = None
    if rd:
        hooks1 = {"dhn": (lambda env: rd.swap(first, {"w_in": env["g_in"], "w_out": env["g_out"]}),
                          lambda got: rd.pair_sums(first, got))}
        hooks_attn = {"attn_bwd": (lambda env: rd.scatter(first), lambda got: rd.chip_sums(first, got))}

        def mid_and_last(got):
            rd.shared(mid, got[:len(mid)])
            rd.pair_sums(last, got[len(mid):])

        hooks0 = {
            "dz": (lambda env: rd.share(first), lambda got: rd.shared(first, got)),
            "dglu": (lambda env: rd.swap(mid, {"w_out": env["g_out"], "w_qkv": g_qkv, "w_o": g_o4}),
                     lambda got: rd.pair_sums(mid, got)),
            "dwin": (lambda env: rd.scatter(mid), lambda got: rd.chip_sums(mid, got)),
            "dhn": (lambda env: _both(rd.share(mid), rd.swap(last, {"w_in": env["g_in"]})), mid_and_last),
        }
        hooks_pool = {"pool_bwd": (lambda env: rd.scatter(last), lambda got: rd.chip_sums(last, got))}
    dh, dh16, d_gf1, g_in, g_out, dcw1, dcb1 = _ffn_bwd(
        dh16, hn_f1, u1, z1, w["w_in"], w["w_out"], cw[1], cb[1], 1, None, None, (h3, gf[1], dh), hooks=hooks1)
    do = _mm_nt(dh16, w["w_o"], "attn_do", out_dtype=MXU_DTYPE)
    g_o = _mm_tn(o, dh16, "attn_dwo", tn=WGRAD_TILE, tk=x.shape[0])
    g_o4 = g_o.reshape(N_CHIPS, 2, W_O_ROWS // 2, D_MODEL)
    dq, dk, dv, d_bqkv, d_sinks = _ride(hooks_attn, "attn_bwd", lambda hosted: _attn_bwd_call(qkv, sinks, do, hosted))
    dqkv = jnp.concatenate([dq, dk.astype(MXU_DTYPE), dv.astype(MXU_DTYPE)], axis=1)
    g_qkv = _mm_tn(hn_a, dqkv, "attn_dwqkv", tn=WGRAD_TILE, tk=x.shape[0])
    dh, dh16, d_gm1 = _attn_dhn_call(dqkv, w["w_qkv"], (h2, gm[1], dh))
    dh, _, d_gf0, g_in, g_out, dcw0, dcb0 = _ffn_bwd(
        dh16, hn_f0, u0, z0, w["w_in"], w["w_out"], cw[0], cb[0], 0, g_in, g_out, (h1, gf[0], dh), hooks=hooks0)
    grad_x, g_pool, dvec = _ride(hooks_pool, "pool_bwd", lambda hosted: _pool_bwd_call(
        x, dh, gm[0], w["pool_w"], pool_b, pool_scale, hosted))
    if rd:
        rd.shared(last, _alone(rd.share(last), "share_last"))

    grads = {
        "w_in": g_in, "w_out": g_out, "w_qkv": g_qkv, "w_o": g_o, "pool_w": g_pool,
        "pool_b": dvec[0:1], "pool_scale": dvec[1:2], "b_qkv": d_bqkv, "sinks": d_sinks[0:1, :N_HEADS],
        "norm_mix": jnp.concatenate([dvec[2:3], d_gm1], axis=0), "norm_ffn": jnp.concatenate([d_gf0, d_gf1], axis=0),
        "conv_w": jnp.stack([dcw0.reshape(3, 2 * D_FF), dcw1.reshape(3, 2 * D_FF)]),
        "conv_b": jnp.stack([dcb0.reshape(2 * D_FF), dcb1.reshape(2 * D_FF)]), "norm_f": d_norm_f,
    }
    return loss, grad_x, grads


W_IN_COLS = 2 * D_FF // N_CHIPS
W_OUT_ROWS = D_FF // N_CHIPS
QKV_COLS = QKV_DIM // N_CHIPS
W_O_ROWS = D_MODEL // N_CHIPS
POOL_ROWS = POOL_GC // N_CHIPS
BIG = ("w_in", "w_out", "w_qkv", "w_o", "pool_w")


def _mesh_place():
    x, y, c = lax.axis_index("x"), lax.axis_index("y"), lax.axis_index("c")
    chips = [(1 - x, y), (x, 1 - y), (1 - x, 1 - y)]
    return x, y, c, chips


def _col(k, width):
    return pl.ds(pl.multiple_of(k * width, 128), width)


def _row(k, height):
    return pl.ds(pl.multiple_of(k * height, 16), height)


def _full_piece(name, ref, k, h):
    if name == "w_in":
        return ref.at[h, :, _col(k, W_IN_COLS)]
    if name == "w_out":
        return ref.at[h, _row(k, W_OUT_ROWS), :]
    if name == "w_qkv":
        return ref.at[_row(h, D_MODEL // 2), _col(k, QKV_COLS)]
    if name == "w_o":
        return ref.at[_row(2 * k + h, W_O_ROWS // 2), :]
    return ref.at[pl.ds(2 * h, 2), _row(k, POOL_ROWS), :]


def _shard_half(name, ref, h):
    if name in ("w_in", "w_out"):
        return ref.at[h]
    if name == "w_qkv":
        return ref.at[_row(h, D_MODEL // 2), :]
    if name == "w_o":
        return ref.at[_row(h, W_O_ROWS // 2), :]
    return ref.at[pl.ds(2 * h, 2)]


FULL_SHAPES = {"w_in": (2, D_MODEL, 2 * D_FF), "w_out": (2, D_FF, D_MODEL), "w_qkv": (D_MODEL, QKV_DIM),
               "w_o": (D_MODEL, D_MODEL), "pool_w": (4, POOL_GC, POOL_GC)}
SHARD_SHAPES = {"w_in": (2, D_MODEL, W_IN_COLS), "w_out": (2, W_OUT_ROWS, D_MODEL), "w_qkv": (D_MODEL, QKV_COLS),
                "w_o": (W_O_ROWS, D_MODEL), "pool_w": (4, POOL_ROWS, POOL_GC)}
HALF_SHAPES = {"w_in": (D_MODEL, W_IN_COLS), "w_out": (W_OUT_ROWS, D_MODEL), "w_qkv": (D_MODEL // 2, QKV_COLS),
               "w_o": (W_O_ROWS // 2, D_MODEL), "pool_w": (2, POOL_ROWS, POOL_GC)}


def _place_shard(name, shard, me_arr):
    if name == "w_in":
        blk, grid = (1, 256, W_IN_COLS), (2, D_MODEL // 256)
        src, dst = (lambda a, b, me: (a, b, 0)), (lambda a, b, me: (a, b, me[0]))
    elif name == "w_out":
        blk, grid = (1, W_OUT_ROWS, D_MODEL), (2, 1)
        src, dst = (lambda a, b, me: (a, 0, 0)), (lambda a, b, me: (a, me[0], 0))
    elif name == "w_qkv":
        blk, grid = (256, QKV_COLS), (1, D_MODEL // 256)
        src, dst = (lambda a, b, me: (b, 0)), (lambda a, b, me: (b, me[0]))
    elif name == "w_o":
        blk, grid = (W_O_ROWS, D_MODEL), (1, 1)
        src, dst = (lambda a, b, me: (0, 0)), (lambda a, b, me: (me[0], 0))
    else:
        blk, grid = (4, POOL_ROWS, POOL_GC), (1, 1)
        src, dst = (lambda a, b, me: (0, 0, 0)), (lambda a, b, me: (0, me[0], 0))

    def body(me_ref, s_ref, o_ref):
        o_ref[...] = s_ref[...].astype(o_ref.dtype)

    return pl.pallas_call(
        body, name=f"place_{name}",
        grid_spec=pltpu.PrefetchScalarGridSpec(num_scalar_prefetch=1, grid=grid, in_specs=[pl.BlockSpec(blk, src)],
                                               out_specs=pl.BlockSpec(blk, dst)),
        out_shape=jax.ShapeDtypeStruct(FULL_SHAPES[name], WIRE_DTYPE), compiler_params=_params("parallel", "parallel"),
    )(me_arr, shard)


def _when_mover(half, c, fn):
    if half is None:
        fn(c)
    else:
        pl.when(c == half)(lambda: fn(half))


def _when_taker(half, c, fn):
    if half is None:
        fn(1 - c)
    else:
        pl.when(c != half)(lambda: fn(half))


def _remote(ref, sems, t, j, dev):
    send, recv = sems
    return pltpu.make_async_remote_copy(src_ref=ref, dst_ref=ref, send_sem=send.at[t, j], recv_sem=recv.at[t, j],
                                        device_id=dev, device_id_type=MESH)


def _gather_ici(items, refs, sems, finish):
    x, y, c, chips = _mesh_place()
    me = 2 * x + y
    for t, (nm, half) in enumerate(items):
        def go(h, t=t, nm=nm):
            mine = _full_piece(nm, refs[t], me, h)
            for j, (cx, cy) in enumerate(chips):
                if not finish:
                    _remote(mine, sems, t, j, (cx, cy, c)).start()
                else:
                    _remote(_full_piece(nm, refs[t], 2 * cx + cy, h), sems, t, j, (cx, cy, c)).wait_recv()
                    _remote(mine, sems, t, j, (cx, cy, c)).wait_send()

        _when_mover(half, c, go)


def _gather_pass(items, refs, sems, finish):
    x, y, c, chips = _mesh_place()
    sibling = (x, y, 1 - c)
    for t, (nm, half) in enumerate(items):
        def give(h, t=t, nm=nm):
            for j, (cx, cy) in enumerate(chips):
                got = _remote(_full_piece(nm, refs[t], 2 * cx + cy, h), sems, t, j, sibling)
                if not finish:
                    got.start()
                else:
                    got.wait_send()

        def take(h, t=t, nm=nm):
            for j, (cx, cy) in enumerate(chips):
                _remote(_full_piece(nm, refs[t], 2 * cx + cy, h), sems, t, j, sibling).wait_recv()

        _when_mover(half, c, give)
        if finish:
            _when_taker(half, c, take)


def _gather_sems(n_items):
    return [pltpu.SemaphoreType.DMA((n_items, 3)), pltpu.SemaphoreType.DMA((n_items, 3))]


EARLY_ITEMS = (("pool_w", None), ("w_in", 0))
_LATE_A = (("w_out", 0), ("w_qkv", None), ("w_o", None))
_LATE_B = (("w_in", 1),)
_LATE_C = (("w_out", 1),)
GATHER_RIDES = {"in": ((), _LATE_A), "glu": (_LATE_A, _LATE_B), "out": (_LATE_B, ()), "attn_fwd": ((), _LATE_C),
                "attn_out": (_LATE_C, ())}


def _names(items):
    return [nm for nm, _ in items]


def _gather_early(full, small):
    items = EARLY_ITEMS
    n = len(items)

    def body(*refs):
        small_in = refs[n]
        outs, small_out = refs[n + 1:2 * n + 1], refs[2 * n + 1]
        scratch = refs[2 * n + 2:]
        ici_sems, pass_sems = scratch[0:2], scratch[2:4]
        local_sem, small_send, small_recv = scratch[4:]
        x, y, c, chips = _mesh_place()
        me = 2 * x + y
        local = pltpu.make_async_copy(small_in, small_out.at[me], local_sem.at[0])
        local.start()
        small_sends = [pltpu.make_async_remote_copy(
            src_ref=small_in, dst_ref=small_out.at[me], send_sem=small_send.at[j], recv_sem=small_recv.at[j],
            device_id=(cx, cy, c), device_id_type=MESH) for j, (cx, cy) in enumerate(chips)]
        for cp in small_sends:
            cp.start()
        _gather_ici(items, outs, ici_sems, finish=False)
        _gather_ici(items, outs, ici_sems, finish=True)
        _gather_pass(items, outs, pass_sems, finish=False)
        _gather_pass(items, outs, pass_sems, finish=True)
        for j, (cx, cy) in enumerate(chips):
            got = small_out.at[2 * cx + cy]
            pltpu.make_async_remote_copy(src_ref=got, dst_ref=got, send_sem=small_send.at[j],
                                         recv_sem=small_recv.at[j], device_id=(cx, cy, c), device_id_type=MESH).wait_recv()
        for cp in small_sends:
            cp.wait_send()
        local.wait()

    arrays = [full[nm] for nm in _names(items)]
    out_shape = [jax.ShapeDtypeStruct(a.shape, a.dtype) for a in arrays]
    out_shape.append(jax.ShapeDtypeStruct((N_CHIPS,) + small.shape, F32))
    res = pl.pallas_call(
        body, name="gather_early", in_specs=[ANY] * (n + 1), out_specs=[ANY] * (n + 1), out_shape=out_shape,
        input_output_aliases={t: t for t in range(n)},
        scratch_shapes=_gather_sems(n) + _gather_sems(n) + [pltpu.SemaphoreType.DMA((1,)), pltpu.SemaphoreType.DMA((3,)),
                                                            pltpu.SemaphoreType.DMA((3,))],
    )(*arrays, small)
    return {**full, **dict(zip(_names(items), res[:n]))}, res[n]


def _late_gather_host(stage, w):
    passing, sending = GATHER_RIDES[stage]
    n_pass = len(passing)

    def run(src, refs, new, sems, finish):
        if passing:
            _gather_pass(passing, refs[:n_pass], sems[:2], finish)
        if sending:
            _gather_ici(sending, refs[n_pass:], sems[-2:], finish)

    sems = (_gather_sems(n_pass) if passing else []) + (_gather_sems(len(sending)) if sending else [])
    return _Hosted([], [w[nm] for nm in _late_names(stage)], [], sems, functools.partial(run, finish=False),
                   functools.partial(run, finish=True))


def _late_names(stage):
    passing, sending = GATHER_RIDES[stage]
    return _names(passing) + _names(sending)


def _other_half_src(name, ref, h):
    if name in ("w_in", "w_out"):
        return ref.at[h]
    if name == "w_qkv":
        return ref.at[_row(h, D_MODEL // 2), :]
    if name == "w_o":
        return ref.at[:, pl.ds(h, 1)]
    return ref.at[pl.ds(2 * h, 2)]


SLAB_SHAPES = {"w_in": (D_MODEL, 2 * D_FF), "w_out": (D_FF, D_MODEL), "w_qkv": (D_MODEL // 2, QKV_DIM),
               "w_o": (N_CHIPS, 1, W_O_ROWS // 2, D_MODEL), "pool_w": (2, POOL_GC, POOL_GC)}


def _item_tag(item):
    return item[0] if item[1] is None else f"{item[0]}{item[1]}"


def _reduce_swap(items, grads, slabs, sems, finish):
    x, y, c, _ = _mesh_place()
    send, recv = sems
    for t, (nm, half) in enumerate(items):
        def copy(h, t=t, nm=nm):
            return pltpu.make_async_remote_copy(
                src_ref=_other_half_src(nm, grads[t], h), dst_ref=slabs[t], send_sem=send.at[t], recv_sem=recv.at[t],
                device_id=(x, y, 1 - c), device_id_type=MESH)

        def give(h):
            if finish:
                copy(h).wait_send()
            else:
                copy(h).start()

        if half is None:
            give(1 - c)
            if finish:
                copy(c).wait_recv()
        else:
            pl.when(c != half)(lambda: give(half))
            if finish:
                pl.when(c == half)(lambda: copy(half).wait_recv())


def _reduce_scatter(items, sums, parts, sems, finish):
    _, _, c, chips = _mesh_place()
    send, recv = sems
    for t, (nm, half) in enumerate(items):
        def go(h, t=t, nm=nm):
            for j, (cx, cy) in enumerate(chips):
                cp = pltpu.make_async_remote_copy(
                    src_ref=_slab_piece(nm, sums[t], 2 * cx + cy), dst_ref=parts[t].at[j], send_sem=send.at[t, j],
                    recv_sem=recv.at[t, j], device_id=(cx, cy, c), device_id_type=MESH)
                if finish:
                    cp.wait_recv()
                    cp.wait_send()
                else:
                    cp.start()

        _when_mover(half, c, go)


def _reduce_share(items, shards, sems, finish):
    x, y, c, _ = _mesh_place()
    send, recv = sems
    for t, (nm, half) in enumerate(items):
        def copy(h, t=t, nm=nm):
            part = _shard_half(nm, shards[t], h)
            return pltpu.make_async_remote_copy(src_ref=part, dst_ref=part, send_sem=send.at[t], recv_sem=recv.at[t],
                                                device_id=(x, y, 1 - c), device_id_type=MESH)

        def give(h):
            if finish:
                copy(h).wait_send()
            else:
                copy(h).start()

        _when_mover(half, c, give)
        if finish:
            _when_taker(half, c, lambda h: copy(h).wait_recv())


def _pair_sems(n):
    return [pltpu.SemaphoreType.DMA((n,)), pltpu.SemaphoreType.DMA((n,))]


def _swap_host(items, grads):
    def run(src, arr, new, sems, finish):
        _reduce_swap(items, src, new, sems, finish)

    return _Hosted([grads[nm] for nm, _ in items], [], [jax.ShapeDtypeStruct(SLAB_SHAPES[nm], F32) for nm, _ in items],
                   _pair_sems(len(items)), functools.partial(run, finish=False), functools.partial(run, finish=True))


def _scatter_host(items, sums):
    def run(src, arr, new, sems, finish):
        _reduce_scatter(items, src, new, sems, finish)

    new = [jax.ShapeDtypeStruct((N_CHIPS - 1,) + HALF_SHAPES[nm], WIRE_DTYPE) for nm, _ in items]
    return _Hosted(list(sums), [], new, _gather_sems(len(items)), functools.partial(run, finish=False),
                   functools.partial(run, finish=True))


def _share_host(items, shards):
    def run(src, arr, new, sems, finish):
        _reduce_share(items, arr, sems, finish)

    return _Hosted([], list(shards), [], _pair_sems(len(items)), functools.partial(run, finish=False),
                   functools.partial(run, finish=True))


def _alone(hosted, name):
    def body(ins, outs, scratch, first, last):
        first()
        last()

    return _call(body, hosted, name=name, grid=(1,), in_specs=[], out_specs=[], out_shape=[], operands=(),
                 semantics=("arbitrary",))


def _pair_sum(item, grad, slab, place):
    name, half = item
    shape = SLAB_SHAPES[name]
    hsel = (lambda p: p[1]) if half is None else (lambda p: half)
    if name == "w_in":
        blk = (256, 2 * D_FF)
        grid = (D_MODEL // 256,)
        g_spec = pl.BlockSpec((None,) + blk, lambda i, p: (hsel(p), i, 0))
        s_spec = pl.BlockSpec(blk, lambda i, p: (i, 0))
    elif name == "w_out":
        blk = (W_OUT_ROWS, D_MODEL)
        grid = (N_CHIPS,)
        g_spec = pl.BlockSpec((None,) + blk, lambda i, p: (hsel(p), i, 0))
        s_spec = pl.BlockSpec(blk, lambda i, p: (i, 0))
    elif name == "w_qkv":
        grid = (1,)
        g_spec = pl.BlockSpec(shape, lambda i, p: (hsel(p), 0))
        s_spec = pl.BlockSpec(shape, lambda i, p: (0, 0))
    else:
        grid = (1,)
        g_spec = pl.BlockSpec(shape, lambda i, p: (0, hsel(p), 0, 0))
        s_spec = pl.BlockSpec(shape, lambda i, p: (0, 0, 0, 0))

    def body(p_ref, g_ref, s_ref, o_ref):
        @pl.when(hsel(p_ref) == p_ref[1])
        def _():
            o_ref[...] = (g_ref[...] + s_ref[...]).astype(o_ref.dtype)

    return pl.pallas_call(
        body, name=f"pair_sum_{_item_tag(item)}",
        grid_spec=pltpu.PrefetchScalarGridSpec(num_scalar_prefetch=1, grid=grid, in_specs=[g_spec, s_spec],
                                               out_specs=s_spec),
        out_shape=jax.ShapeDtypeStruct(shape, WIRE_DTYPE), compiler_params=_params("parallel"),
    )(place, grad, slab)


def _slab_piece(name, ref, k):
    if name == "w_in":
        return ref.at[:, _col(k, W_IN_COLS)]
    if name == "w_out":
        return ref.at[_row(k, W_OUT_ROWS), :]
    if name == "w_qkv":
        return ref.at[:, _col(k, QKV_COLS)]
    if name == "w_o":
        return ref.at[k, 0]
    return ref.at[:, _row(k, POOL_ROWS), :]


def _sum_chips(item, slab, parts, place, into=None):
    name, sel = item
    half = HALF_SHAPES[name]
    hsel = (lambda p: p[1]) if sel is None else (lambda p: sel)
    if name == "w_in":
        blk, grid = (256, W_IN_COLS), (D_MODEL // 256,)
        own = pl.BlockSpec(blk, lambda i, p: (i, p[0]))
        got = pl.BlockSpec((N_CHIPS - 1,) + blk, lambda i, p: (0, i, 0))
        out = pl.BlockSpec((None,) + blk, lambda i, p: (hsel(p), i, 0))
    elif name == "w_out":
        grid = (1,)
        own = pl.BlockSpec(half, lambda i, p: (p[0], 0))
        got = pl.BlockSpec((N_CHIPS - 1,) + half, lambda i, p: (0, 0, 0))
        out = pl.BlockSpec((None,) + half, lambda i, p: (hsel(p), 0, 0))
    elif name == "w_qkv":
        grid = (1,)
        own = pl.BlockSpec(half, lambda i, p: (0, p[0]))
        got = pl.BlockSpec((N_CHIPS - 1,) + half, lambda i, p: (0, 0, 0))
        out = pl.BlockSpec(half, lambda i, p: (hsel(p), 0))
    else:
        grid = (1,)
        own = pl.BlockSpec((None, None) + half, lambda i, p: (p[0], 0, 0, 0))
        got = pl.BlockSpec((N_CHIPS - 1,) + half, lambda i, p: (0, 0, 0))
        out = pl.BlockSpec(half, lambda i, p: (hsel(p), 0))

    def body(p_ref, own_ref, got_ref, *rest):
        o_ref = rest[-1]

        @pl.when(hsel(p_ref) == p_ref[1])
        def _():
            acc = own_ref[...].astype(F32)
            for j in range(N_CHIPS - 1):
                acc = acc + got_ref[j].astype(F32)
            o_ref[...] = acc

    in_specs, operands, aliases = [own, got], [place, slab, parts], {}
    if into is not None:
        in_specs.append(ANY)
        operands.append(into)
        aliases = {3: 0}
    return pl.pallas_call(
        body, name=f"sum_chips_{_item_tag(item)}",
        grid_spec=pltpu.PrefetchScalarGridSpec(num_scalar_prefetch=1, grid=grid, in_specs=in_specs, out_specs=out),
        out_shape=jax.ShapeDtypeStruct(SHARD_SHAPES[name], F32), input_output_aliases=aliases,
        compiler_params=_params("parallel"),
    )(*operands)


class _Reducer:
    FIRST = (("w_in", 1), ("w_out", 1))
    MID = (("w_out", 0), ("w_qkv", None), ("w_o", None))
    LAST = (("w_in", 0),)

    def __init__(self, place):
        self.place = place
        self.shards = {}
        self.state = {}

    def swap(self, items, grads):
        self.state[items] = {"grads": [grads[nm] for nm, _ in items]}
        return _swap_host(items, grads)

    def pair_sums(self, items, slabs):
        st = self.state[items]
        st["sums"] = [_pair_sum(it, g, s, self.place) for it, g, s in zip(items, st["grads"], slabs)]

    def scatter(self, items):
        return _scatter_host(items, self.state[items]["sums"])

    def chip_sums(self, items, parts):
        for it, s, p in zip(items, self.state[items]["sums"], parts):
            self.shards[it[0]] = _sum_chips(it, s, p, self.place, into=self.shards.get(it[0]))

    def share(self, items):
        return _share_host(items, [self.shards[nm] for nm, _ in items])

    def shared(self, items, updated):
        self.shards.update(dict(zip(_names(items), updated)))


N_DEV = 8


def _all_reduce_small(vec, g_pool):
    def body(v_ref, gp_ref, o_ref, po_ref, slots, pslots, send_sems, recv_sems, psend, precv, hsend, hrecv):
        x, y, c, _ = _mesh_place()
        me = 4 * x + 2 * y + c

        def pool_piece(chip, h):
            return gp_ref.at[pl.ds(2 * h, 2), pl.ds(pl.multiple_of(chip * POOL_ROWS, POOL_ROWS), POOL_ROWS), :]

        slots[me] = v_ref[...]
        pslots[me] = pool_piece(2 * x + y, c)[...]
        copies = []
        for d in range(1, N_DEV):
            peer = (x ^ (d >> 2), y ^ ((d >> 1) & 1), c ^ (d & 1))
            copies.append(pltpu.make_async_remote_copy(
                src_ref=v_ref, dst_ref=slots.at[me], send_sem=send_sems.at[d - 1], recv_sem=recv_sems.at[d - 1],
                device_id=peer, device_id_type=MESH))
            copies.append(pltpu.make_async_remote_copy(
                src_ref=pool_piece(2 * peer[0] + peer[1], peer[2]), dst_ref=pslots.at[me], send_sem=psend.at[d - 1],
                recv_sem=precv.at[d - 1], device_id=peer, device_id_type=MESH))
        for cp in copies:
            cp.start()
        for d in range(1, N_DEV):
            peer = 4 * (x ^ (d >> 2)) + 2 * (y ^ ((d >> 1) & 1)) + (c ^ (d & 1))
            for buf, ss, rs in ((slots, send_sems, recv_sems), (pslots, psend, precv)):
                got = buf.at[peer]
                pltpu.make_async_remote_copy(src_ref=got, dst_ref=got, send_sem=ss.at[d - 1], recv_sem=rs.at[d - 1],
                                             device_id=(x, y, c), device_id_type=MESH).wait_recv()
        for cp in copies:
            cp.wait_send()
        acc, pacc = slots[0], pslots[0]
        for d in range(1, N_DEV):
            acc, pacc = acc + slots[d], pacc + pslots[d]
        o_ref[...] = acc
        po_ref[pl.ds(2 * c, 2)] = pacc
        mine, theirs = po_ref.at[pl.ds(2 * c, 2)], po_ref.at[pl.ds(2 * (1 - c), 2)]
        give = pltpu.make_async_remote_copy(src_ref=mine, dst_ref=mine, send_sem=hsend.at[0], recv_sem=hrecv.at[0],
                                            device_id=(x, y, 1 - c), device_id_type=MESH)
        give.start()
        pltpu.make_async_remote_copy(src_ref=theirs, dst_ref=theirs, send_sem=hsend.at[0], recv_sem=hrecv.at[0],
                                     device_id=(x, y, 1 - c), device_id_type=MESH).wait_recv()
        give.wait_send()

    vm = pl.BlockSpec(memory_space=pltpu.VMEM)
    piece = (2, POOL_ROWS, POOL_GC)
    return pl.pallas_call(
        body, name="all_reduce_small", in_specs=[vm, vm], out_specs=[vm, vm],
        out_shape=[jax.ShapeDtypeStruct(vec.shape, F32), jax.ShapeDtypeStruct(SHARD_SHAPES["pool_w"], F32)],
        scratch_shapes=[pltpu.VMEM((N_DEV,) + vec.shape, F32), pltpu.VMEM((N_DEV,) + piece, F32)]
        + [pltpu.SemaphoreType.DMA((N_DEV - 1,))] * 4 + [pltpu.SemaphoreType.DMA((1,))] * 2,
    )(vec, g_pool)


def _adamw_call(w, g, m, v, name):
    shape = w.shape
    cols = shape[-1]
    rows = w.size // cols
    tr = rows
    for cand in (256, 128, 64, 32, 16, 8):
        if rows > cand and rows % cand == 0 and cand * cols * 4 <= 2 * 1024 * 1024:
            tr = cand
            break

    def body(w_ref, g_ref, m_ref, v_ref, d_ref, nm_ref, nv_ref):
        gv = g_ref[...]
        mn = ADAM_B1 * m_ref[...] + (1.0 - ADAM_B1) * gv
        vn = ADAM_B2 * v_ref[...] + (1.0 - ADAM_B2) * jnp.square(gv)
        m_hat = mn / (1.0 - ADAM_B1 ** ADAM_STEP)
        v_hat = vn / (1.0 - ADAM_B2 ** ADAM_STEP)
        d_ref[...] = -ADAM_LR * (m_hat / (jnp.sqrt(v_hat) + ADAM_EPS) + ADAM_WD * w_ref[...])
        nm_ref[...] = mn
        nv_ref[...] = vn

    spec = pl.BlockSpec((tr, cols), lambda i: (i, 0))
    flat = lambda a: a.reshape(rows, cols)
    outs = pl.pallas_call(
        body, name=name, grid=(rows // tr,), in_specs=[spec] * 4, out_specs=[spec] * 3,
        out_shape=[jax.ShapeDtypeStruct((rows, cols), F32)] * 3, compiler_params=_params("parallel"),
    )(flat(w), flat(g), flat(m), flat(v))
    return tuple(o.reshape(shape) for o in outs)


WEIGHTS = ("pool_w", "pool_b", "pool_scale", "attn_w_qkv", "attn_b_qkv", "attn_sinks", "attn_w_o", "norm_mix",
           "norm_ffn", "ffn_w_in", "ffn_conv_w", "ffn_conv_b", "ffn_w_out", "norm_f")
SMALL_GATHER = 128 * 71
_SMALL_SIZES = (("pool_b", 1024), ("pool_scale", 1024), ("b_qkv", 1536), ("sinks", 128), ("norm_mix", 2048),
                ("norm_ffn", 2048), ("conv_w", 6 * 2 * D_FF), ("conv_b", 4 * D_FF), ("norm_f", 1024))
SMALL_COLS = 6784


def _pack_small(grads):
    flat = jnp.concatenate([jnp.pad(grads[nm].reshape(-1), (0, size - grads[nm].size)) for nm, size in _SMALL_SIZES])
    return jnp.pad(flat, (0, 8 * SMALL_COLS - flat.size)).reshape(8, SMALL_COLS)


def _unpack_small(vec):
    flat = vec.reshape(-1)
    out, off = {}, 0
    for nm, size in _SMALL_SIZES:
        out[nm] = flat[off:off + size]
        off += size
    return out


def kernel(x, pool_w, pool_b, pool_scale, attn_w_qkv, attn_b_qkv, attn_sinks, attn_w_o, norm_mix, norm_ffn, ffn_w_in, ffn_conv_w, ffn_conv_b, ffn_w_out, norm_f, loss_target, m_pool_w, m_pool_b, m_pool_scale, m_attn_w_qkv, m_attn_b_qkv, m_attn_sinks, m_attn_w_o, m_norm_mix, m_norm_ffn, m_ffn_w_in, m_ffn_conv_w, m_ffn_conv_b, m_ffn_w_out, m_norm_f, v_pool_w, v_pool_b, v_pool_scale, v_attn_w_qkv, v_attn_b_qkv, v_attn_sinks, v_attn_w_o, v_norm_mix, v_norm_ffn, v_ffn_w_in, v_ffn_conv_w, v_ffn_conv_b, v_ffn_w_out, v_norm_f):
    weights = dict(pool_w=pool_w, pool_b=pool_b, pool_scale=pool_scale, attn_w_qkv=attn_w_qkv, attn_b_qkv=attn_b_qkv,
                   attn_sinks=attn_sinks, attn_w_o=attn_w_o, norm_mix=norm_mix, norm_ffn=norm_ffn, ffn_w_in=ffn_w_in,
                   ffn_conv_w=ffn_conv_w, ffn_conv_b=ffn_conv_b, ffn_w_out=ffn_w_out, norm_f=norm_f)
    m_in = dict(pool_w=m_pool_w, pool_b=m_pool_b, pool_scale=m_pool_scale, attn_w_qkv=m_attn_w_qkv,
                attn_b_qkv=m_attn_b_qkv, attn_sinks=m_attn_sinks, attn_w_o=m_attn_w_o, norm_mix=m_norm_mix,
                norm_ffn=m_norm_ffn, ffn_w_in=m_ffn_w_in, ffn_conv_w=m_ffn_conv_w, ffn_conv_b=m_ffn_conv_b,
                ffn_w_out=m_ffn_w_out, norm_f=m_norm_f)
    v_in = dict(pool_w=v_pool_w, pool_b=v_pool_b, pool_scale=v_pool_scale, attn_w_qkv=v_attn_w_qkv,
                attn_b_qkv=v_attn_b_qkv, attn_sinks=v_attn_sinks, attn_w_o=v_attn_w_o, norm_mix=v_norm_mix,
                norm_ffn=v_norm_ffn, ffn_w_in=v_ffn_w_in, ffn_conv_w=v_ffn_conv_w, ffn_conv_b=v_ffn_conv_b,
                ffn_w_out=v_ffn_w_out, norm_f=v_norm_f)
    chip = 2 * lax.axis_index("x") + lax.axis_index("y")
    core = lax.axis_index("c")

    place = jnp.stack([chip, core]).astype(jnp.int32)
    shards = {"w_in": ffn_w_in, "w_out": ffn_w_out, "w_qkv": attn_w_qkv[0], "w_o": attn_w_o[0], "pool_w": pool_w[0]}
    placed = {nm: _place_shard(nm, shards[nm], place) for nm in BIG}
    small = jnp.concatenate([pool_b.reshape(-1), attn_b_qkv.reshape(-1), ffn_conv_w.reshape(-1)]).reshape(1, SMALL_GATHER)
    full, small_all = _gather_early(placed, small)
    small_all = small_all.reshape(N_CHIPS, SMALL_GATHER)
    pool_b_full = small_all[:, :256].reshape(N_CHIPS, 4, POOL_ROWS).transpose(1, 0, 2).reshape(1, D_MODEL)
    b_qkv_full = small_all[:, 256:640].reshape(1, QKV_DIM)
    conv_w_full = small_all[:, 640:].reshape(N_CHIPS, 2, 3, W_IN_COLS).transpose(1, 2, 0, 3).reshape(2, 3, 2 * D_FF)

    reducer = _Reducer(place)
    loss, grad_x, grads = _local_step(
        x[0], loss_target[0], full, norm_mix=norm_mix, norm_ffn=norm_ffn, norm_f=norm_f.reshape(1, D_MODEL),
        pool_b=pool_b_full, pool_scale=pool_scale, b_qkv=b_qkv_full, sinks=attn_sinks[0], conv_w=conv_w_full,
        conv_b=ffn_conv_b, late_gather=_late_gather_host, reducer=reducer)
    loss = lax.psum(loss[0, 0], ("x", "y", "c"))
    reduced = dict(reducer.shards)

    small_sum, reduced["pool_w"] = _all_reduce_small(_pack_small(grads), grads["pool_w"])
    small_g = _unpack_small(small_sum)
    pool_b_g = lax.dynamic_slice_in_dim(small_g["pool_b"].reshape(4, N_CHIPS, POOL_ROWS), chip, 1, axis=1)
    b_qkv_g = lax.dynamic_slice_in_dim(small_g["b_qkv"].reshape(N_CHIPS, QKV_COLS), chip, 1, axis=0)
    conv_w_g = lax.dynamic_slice_in_dim(small_g["conv_w"].reshape(2, 3, N_CHIPS, W_IN_COLS), chip, 1, axis=2)
    grad_w = {
        "pool_w": reduced["pool_w"].reshape(pool_w.shape), "pool_b": pool_b_g.reshape(pool_b.shape),
        "pool_scale": small_g["pool_scale"].reshape(pool_scale.shape),
        "attn_w_qkv": reduced["w_qkv"].reshape(attn_w_qkv.shape), "attn_b_qkv": b_qkv_g.reshape(attn_b_qkv.shape),
        "attn_sinks": small_g["sinks"][:N_HEADS].reshape(attn_sinks.shape),
        "attn_w_o": reduced["w_o"].reshape(attn_w_o.shape), "norm_mix": small_g["norm_mix"].reshape(norm_mix.shape),
        "norm_ffn": small_g["norm_ffn"].reshape(norm_ffn.shape), "ffn_w_in": reduced["w_in"],
        "ffn_conv_w": conv_w_g.reshape(ffn_conv_w.shape), "ffn_conv_b": small_g["conv_b"].reshape(ffn_conv_b.shape),
        "ffn_w_out": reduced["w_out"], "norm_f": small_g["norm_f"].reshape(norm_f.shape),
    }

    delta, new_m, new_v = {}, {}, {}
    for nm in WEIGHTS:
        delta[nm], new_m[nm], new_v[nm] = _adamw_call(weights[nm], grad_w[nm], m_in[nm], v_in[nm], f"adamw_{nm}")
    return (loss, grad_x.reshape(x.shape), *[grad_w[nm] for nm in WEIGHTS], *[delta[nm] for nm in WEIGHTS],
            *[new_m[nm] for nm in WEIGHTS], *[new_v[nm] for nm in WEIGHTS])
```

```python
import functools
from typing import Callable, NamedTuple

import jax
import jax.numpy as jnp
from jax import lax
from jax.experimental import pallas as pl
from jax.experimental.pallas import tpu as pltpu

F32 = jnp.float32
MXU_DTYPE = jnp.bfloat16
WIRE_DTYPE = jnp.bfloat16
U_DTYPE = jnp.float32

D_MODEL = 1024
D_FF = 2816
QKV_DIM = 1536
HEAD_DIM = 64
N_HEADS = 16
N_KV_HEADS = 4
ATT_BLOCK = 128
POOL_WINDOWS = (2, 4, 8, 16)
POOL_GC = 256
POOL_HALO = 16
CONV_HALO = 8
RMS_EPS = 1e-6
ATT_SCALE = HEAD_DIM ** -0.5
ALIBI_SLOPES = tuple(2.0 ** (-8.0 / N_HEADS * (h + 1)) for h in range(N_HEADS))

ADAM_LR = 0.001
ADAM_B1 = 0.9
ADAM_B2 = 0.999
ADAM_EPS = 1e-08
ADAM_WD = 0.01
ADAM_STEP = 10

N_CHIPS = 4
MESH = pl.DeviceIdType.MESH
VMEM_LIMIT_BYTES = 56 * 1024 * 1024
ANY = pl.BlockSpec(memory_space=pl.ANY)


def _params(*semantics):
    return pltpu.CompilerParams(dimension_semantics=semantics, vmem_limit_bytes=VMEM_LIMIT_BYTES)


def _rms(x, g):
    return x * lax.rsqrt(jnp.mean(x * x, axis=-1, keepdims=True) + RMS_EPS) * g


def _rms_bwd(x, g, dy):
    rstd = lax.rsqrt(jnp.mean(x * x, axis=-1, keepdims=True) + RMS_EPS)
    xhat = x * rstd
    dxhat = dy * g
    dx = rstd * (dxhat - xhat * jnp.mean(dxhat * xhat, axis=-1, keepdims=True))
    return dx, dy * xhat


def _shift_down(x, s):
    return pltpu.roll(x, s, axis=0)


def _shift_up(x, s):
    return pltpu.roll(x, x.shape[0] - s, axis=0)


def _sigmoid(x):
    return 1.0 / (1.0 + jnp.exp(-x))


_DOT_DIMS = {"nn": ((1,), (0,)), "nt": ((1,), (1,)), "tn": ((0,), (0,))}


class _Hosted(NamedTuple):
    sources: list
    arrays: list
    new: list
    sems: list
    first: Callable
    last: Callable


def _call(body, hosted, *, name, grid, in_specs, out_specs, out_shape, operands, semantics, scratch_shapes=(),
          aliases=None):
    n_in, n_out, n_scr = len(in_specs), len(out_specs), len(scratch_shapes)
    aliases = dict(aliases or {})
    if hosted is None:
        def plain(*refs):
            body(refs[:n_in], refs[n_in:n_in + n_out], refs[n_in + n_out:], lambda: None, lambda: None)

        return pl.pallas_call(plain, name=name, grid=grid, in_specs=list(in_specs), out_specs=list(out_specs),
                              out_shape=list(out_shape), scratch_shapes=list(scratch_shapes),
                              input_output_aliases=aliases, compiler_params=_params(*semantics))(*operands)
    ns, na, nn = len(hosted.sources), len(hosted.arrays), len(hosted.new)

    def carrying(*refs):
        ins, src = refs[:n_in], refs[n_in:n_in + ns]
        o0 = n_in + ns + na
        outs, arr = refs[o0:o0 + n_out], refs[o0 + n_out:o0 + n_out + na]
        new = refs[o0 + n_out + na:o0 + n_out + na + nn]
        s0 = o0 + n_out + na + nn
        scratch, sems = refs[s0:s0 + n_scr], refs[s0 + n_scr:]
        ids = [pl.program_id(d) for d in range(len(grid))]
        is_first = functools.reduce(lambda p, q: p & q, [i == 0 for i in ids])
        is_last = functools.reduce(lambda p, q: p & q, [i == n - 1 for i, n in zip(ids, grid)])
        body(ins, outs, scratch, lambda: pl.when(is_first)(lambda: hosted.first(src, arr, new, sems)),
             lambda: pl.when(is_last)(lambda: hosted.last(src, arr, new, sems)))

    for t in range(na):
        aliases[n_in + ns + t] = n_out + t
    return pl.pallas_call(
        carrying, name=name, grid=grid, in_specs=list(in_specs) + [ANY] * (ns + na),
        out_specs=list(out_specs) + [ANY] * (na + nn),
        out_shape=list(out_shape) + [jax.ShapeDtypeStruct(a.shape, a.dtype) for a in hosted.arrays] + list(hosted.new),
        scratch_shapes=list(scratch_shapes) + list(hosted.sems), input_output_aliases=aliases,
        compiler_params=_params(*["arbitrary"] * len(grid)))(*operands, *hosted.sources, *hosted.arrays)


def _both(h1, h2):
    def run(which, src, arr, new, sems):
        cut = lambda seq, n: (seq[:n], seq[n:])
        (s1, s2), (a1, a2) = cut(src, len(h1.sources)), cut(arr, len(h1.arrays))
        (n1, n2), (m1, m2) = cut(new, len(h1.new)), cut(sems, len(h1.sems))
        getattr(h1, which)(s1, a1, n1, m1)
        getattr(h2, which)(s2, a2, n2, m2)

    return _Hosted(h1.sources + h2.sources, h1.arrays + h2.arrays, h1.new + h2.new, h1.sems + h2.sems,
                   functools.partial(run, "first"), functools.partial(run, "last"))


def _ride(hooks, stage, call, env=None):
    if not hooks or stage not in hooks:
        res = call(None)
        return list(res) if isinstance(res, (list, tuple)) else [res]
    make, done = hooks[stage]
    hosted = make(env)
    res = call(hosted)
    own = len(res) - len(hosted.arrays) - len(hosted.new)
    done(res[own:])
    return list(res[:own])


def _matmul(a, b, *, mode, name, grid, a_spec, b_spec, o_spec, out_shape, bias=None, bias_spec=None, residual=None,
            residual_spec=None, into=None, hosted=None, norm_g=None):
    nk = grid[2]
    dims = (_DOT_DIMS[mode], ((), ()))
    acc_shape = tuple(d for d in o_spec.block_shape if d is not None)

    def body(ins, outs, scratch, first, last):
        a_ref, b_ref = ins[0], ins[1]
        pos = 2
        bias_ref = res_ref = None
        if bias is not None:
            bias_ref = ins[pos]
            pos += 1
        if residual is not None:
            res_ref = ins[pos]
            pos += 1
        o_ref = outs[0]
        acc_ref = scratch[0] if nk > 1 else None
        first()
        prod = lax.dot_general(a_ref[...].astype(MXU_DTYPE), b_ref[...].astype(MXU_DTYPE), dims,
                               preferred_element_type=F32)

        def finish(acc):
            if bias_ref is not None:
                acc = acc + bias_ref[...]
            if res_ref is not None:
                acc = acc + res_ref[...]
            o_ref[...] = acc.astype(o_ref.dtype)
            if norm_g is not None:
                outs[1][...] = _rms(acc, ins[pos][...]).astype(outs[1].dtype)

        if nk == 1:
            finish(prod)
        else:
            k = pl.program_id(2)

            @pl.when(k == 0)
            def _():
                acc_ref[...] = prod

            @pl.when(k > 0)
            def _():
                acc_ref[...] += prod

            @pl.when(k == nk - 1)
            def _():
                finish(acc_ref[...])

        last()

    operands, in_specs = [a, b], [a_spec, b_spec]
    if bias is not None:
        operands.append(bias)
        in_specs.append(bias_spec)
    if residual is not None:
        operands.append(residual)
        in_specs.append(residual_spec)
    out_specs, out_shapes = [o_spec], [out_shape]
    if norm_g is not None:
        assert acc_shape[-1] == norm_g.shape[-1], "the fused norm needs whole rows in an output tile"
        operands.append(norm_g)
        in_specs.append(bias_spec)
        out_specs.append(o_spec)
        out_shapes.append(jax.ShapeDtypeStruct(out_shape.shape, MXU_DTYPE))
    aliases = {}
    if into is not None:
        aliases = {len(operands): 0}
        operands.append(into)
        in_specs.append(ANY)
    res = _call(body, hosted, name=name, grid=grid, in_specs=in_specs, out_specs=out_specs, out_shape=out_shapes,
                operands=operands, semantics=("parallel", "parallel", "arbitrary"),
                scratch_shapes=[pltpu.VMEM(acc_shape, F32)] if nk > 1 else [], aliases=aliases)
    return res if hosted or norm_g is not None else res[0]


def _tile(n, want):
    t = min(n, want)
    assert n % t == 0, (n, want)
    return t


def _mm_nn(a, b, name, *, b_lead=None, bias=None, residual=None, out_dtype=F32, tm=1024, tn=1024, tk=1024,
           hosted=None, norm_g=None):
    (m, k), n = a.shape, b.shape[-1]
    tm, tn, tk = _tile(m, tm), _tile(n, tn), _tile(k, tk)
    if b_lead is None:
        b_spec = pl.BlockSpec((tk, tn), lambda i, j, kk: (kk, j))
    else:
        b_spec = pl.BlockSpec((None, tk, tn), lambda i, j, kk: (b_lead, kk, j))
    return _matmul(
        a, b, mode="nn", name=name, grid=(m // tm, n // tn, k // tk),
        a_spec=pl.BlockSpec((tm, tk), lambda i, j, kk: (i, kk)), b_spec=b_spec,
        o_spec=pl.BlockSpec((tm, tn), lambda i, j, kk: (i, j)), out_shape=jax.ShapeDtypeStruct((m, n), out_dtype),
        bias=bias, bias_spec=pl.BlockSpec((1, tn), lambda i, j, kk: (0, j)),
        residual=residual, residual_spec=pl.BlockSpec((tm, tn), lambda i, j, kk: (i, j)), hosted=hosted,
        norm_g=norm_g)


def _mm_nt(a, b, name, *, b_lead=None, out_dtype=F32, tm=1024, tn=1024, tk=1024, hosted=None):
    (m, k), n = a.shape, b.shape[-2]
    tm, tn, tk = _tile(m, tm), _tile(n, tn), _tile(k, tk)
    if b_lead is None:
        b_spec = pl.BlockSpec((tn, tk), lambda i, j, kk: (j, kk))
    else:
        b_spec = pl.BlockSpec((None, tn, tk), lambda i, j, kk: (b_lead, j, kk))
    return _matmul(
        a, b, mode="nt", name=name, grid=(m // tm, n // tn, k // tk),
        a_spec=pl.BlockSpec((tm, tk), lambda i, j, kk: (i, kk)), b_spec=b_spec,
        o_spec=pl.BlockSpec((tm, tn), lambda i, j, kk: (i, j)), out_shape=jax.ShapeDtypeStruct((m, n), out_dtype),
        hosted=hosted)


def _mm_tn(a, b, name, *, lead=None, n_lead=None, into=None, tm=1024, tn=1024, tk=1024):
    (k, m), n = a.shape, b.shape[-1]
    tm, tn, tk = _tile(m, tm), _tile(n, tn), _tile(k, tk)
    if lead is None:
        o_spec = pl.BlockSpec((tm, tn), lambda i, j, kk: (i, j))
        out_shape = jax.ShapeDtypeStruct((m, n), F32)
    else:
        o_spec = pl.BlockSpec((None, tm, tn), lambda i, j, kk: (lead, i, j))
        out_shape = jax.ShapeDtypeStruct((n_lead, m, n), F32)
    return _matmul(
        a, b, mode="tn", name=name, grid=(m // tm, n // tn, k // tk),
        a_spec=pl.BlockSpec((tk, tm), lambda i, j, kk: (kk, i)),
        b_spec=pl.BlockSpec((tk, tn), lambda i, j, kk: (kk, j)), o_spec=o_spec, out_shape=out_shape, into=into)


ROW_TILE = 512


def _loss_call(h, g, target):
    t = h.shape[0]
    tt = _tile(t, ROW_TILE)

    def body(h_ref, g_ref, t_ref, loss_ref, dh_ref, dh16_ref, dg_ref):
        x, gv = h_ref[...], g_ref[...]
        err = _rms(x, gv) - t_ref[...]
        dx, dgt = _rms_bwd(x, gv, err * (1.0 / D_MODEL))
        dh_ref[...] = dx
        dh16_ref[...] = dx.astype(dh16_ref.dtype)

        @pl.when(pl.program_id(0) == 0)
        def _():
            dg_ref[...] = jnp.zeros_like(dg_ref)
            loss_ref[...] = jnp.zeros_like(loss_ref)

        dg_ref[...] += jnp.sum(dgt, axis=0, keepdims=True)
        per_token = jnp.mean(err * err, axis=-1, keepdims=True)
        loss_ref[...] += 0.5 * jnp.sum(per_token, axis=0, keepdims=True)

    row = pl.BlockSpec((tt, D_MODEL), lambda i: (i, 0))
    vec = pl.BlockSpec((1, D_MODEL), lambda i: (0, 0))
    one = pl.BlockSpec((1, 1), lambda i: (0, 0))
    return pl.pallas_call(
        body, name="loss_head", grid=(t // tt,), in_specs=[row, vec, row], out_specs=[one, row, row, vec],
        out_shape=[jax.ShapeDtypeStruct((1, 1), F32), jax.ShapeDtypeStruct(h.shape, F32),
                   jax.ShapeDtypeStruct(h.shape, MXU_DTYPE), jax.ShapeDtypeStruct((1, D_MODEL), F32)],
        compiler_params=_params("arbitrary"))(h, g, target)


def _pool_windows_causal(ext, first_row):
    tt = ext.shape[0] - POOL_HALO
    t = first_row + lax.broadcasted_iota(jnp.int32, (tt, 1), 0)
    outs = []
    for gi, win in enumerate(POOL_WINDOWS):
        cols = slice(gi * POOL_GC, (gi + 1) * POOL_GC)
        s = ext[:, cols]
        sh = 1
        while sh < win:
            s = s + _shift_down(s, sh)
            sh *= 2
        count = jnp.minimum(t + 1, win).astype(F32)
        outs.append(s[POOL_HALO:] / count - ext[POOL_HALO:, cols])
    return outs


def _prev_halo_spec(tt, halo, width):
    per = tt // halo
    return pl.BlockSpec((halo, width), lambda i: (jnp.maximum(i * per - 1, 0), 0))


def _pool_fwd_call(h0, g_mix, pool_w, pool_b, pool_scale, g_ffn):
    t = h0.shape[0]
    tt = _tile(t, ROW_TILE)

    def body(h_ref, halo_ref, gm_ref, w_ref, b_ref, sc_ref, gf_ref, h1_ref, hn_ref):
        i = pl.program_id(0)
        x, gm = h_ref[...], gm_ref[...]
        top = jnp.where(i > 0, _rms(halo_ref[...], gm), 0.0)
        ext = jnp.concatenate([top, _rms(x, gm)], axis=0)
        ps = _pool_windows_causal(ext, i * tt)
        ys = [jnp.dot(p.astype(MXU_DTYPE), w_ref[gi], preferred_element_type=F32) for gi, p in enumerate(ps)]
        mix = (jnp.concatenate(ys, axis=1) + b_ref[...]) * sc_ref[...]
        h1 = x + mix
        h1_ref[...] = h1
        hn_ref[...] = _rms(h1, gf_ref[...]).astype(hn_ref.dtype)

    row = pl.BlockSpec((tt, D_MODEL), lambda i: (i, 0))
    vec = pl.BlockSpec((1, D_MODEL), lambda i: (0, 0))
    wsp = pl.BlockSpec((len(POOL_WINDOWS), POOL_GC, POOL_GC), lambda i: (0, 0, 0))
    return pl.pallas_call(
        body, name="pool_fwd", grid=(t // tt,),
        in_specs=[row, _prev_halo_spec(tt, POOL_HALO, D_MODEL), vec, wsp, vec, vec, vec], out_specs=[row, row],
        out_shape=[jax.ShapeDtypeStruct(h0.shape, F32), jax.ShapeDtypeStruct(h0.shape, MXU_DTYPE)],
        compiler_params=_params("parallel"))(h0, h0, g_mix, pool_w, pool_b, pool_scale, g_ffn)


def _pool_bwd_call(h0, dh1, g_mix, pool_w, pool_b, pool_scale, hosted=None):
    t = h0.shape[0]
    tt = _tile(t, ROW_TILE)
    nt = t // tt
    per = tt // POOL_HALO

    def body(ins, outs, scratch, first, last):
        h_ref, halo_ref, d_ref, dnext_ref, gm_ref, w_ref, b_ref, sc_ref = ins
        gx_ref, dw_ref, dv_ref = outs
        i = pl.program_id(0)
        first()
        x, gm, sc = h_ref[...], gm_ref[...], sc_ref[...]
        top = jnp.where(i > 0, _rms(halo_ref[...], gm), 0.0)
        ext = jnp.concatenate([top, _rms(x, gm)], axis=0)
        ps = _pool_windows_causal(ext, i * tt)

        dy = d_ref[...]
        dy_ext = jnp.concatenate([dy, jnp.where(i < nt - 1, dnext_ref[...], 0.0)], axis=0)
        dpre_ext = dy_ext * sc
        t_ext = i * tt + lax.broadcasted_iota(jnp.int32, (tt + POOL_HALO, 1), 0)

        @pl.when(i == 0)
        def _():
            dw_ref[...] = jnp.zeros_like(dw_ref)
            dv_ref[...] = jnp.zeros_like(dv_ref)

        dhn, ypre = [], []
        for gi, win in enumerate(POOL_WINDOWS):
            cols = slice(gi * POOL_GC, (gi + 1) * POOL_GC)
            w = w_ref[gi]
            p16 = ps[gi].astype(MXU_DTYPE)
            dpre16 = dpre_ext[:, cols].astype(MXU_DTYPE)
            ypre.append(jnp.dot(p16, w, preferred_element_type=F32))
            dw_ref[gi] += lax.dot_general(p16, dpre16[:tt], (((0,), (0,)), ((), ())), preferred_element_type=F32)
            dp_ext = lax.dot_general(dpre16, w, (((1,), (1,)), ((), ())), preferred_element_type=F32)
            s = dp_ext / jnp.minimum(t_ext + 1, win).astype(F32)
            sh = 1
            while sh < win:
                s = s + _shift_up(s, sh)
                sh *= 2
            dhn.append(s[:tt] - dp_ext[:tt])
        dhn = jnp.concatenate(dhn, axis=1)
        ypre = jnp.concatenate(ypre, axis=1) + b_ref[...]
        dx, dgt = _rms_bwd(x, gm, dhn)
        gx_ref[...] = dy + dx
        dv_ref[0:1, :] += jnp.sum(dpre_ext[:tt], axis=0, keepdims=True)
        dv_ref[1:2, :] += jnp.sum(dy * ypre, axis=0, keepdims=True)
        dv_ref[2:3, :] += jnp.sum(dgt, axis=0, keepdims=True)
        last()

    row = pl.BlockSpec((tt, D_MODEL), lambda i: (i, 0))
    vec = pl.BlockSpec((1, D_MODEL), lambda i: (0, 0))
    nxt = pl.BlockSpec((POOL_HALO, D_MODEL), lambda i: (jnp.minimum((i + 1) * per, t // POOL_HALO - 1), 0))
    wsp = pl.BlockSpec((len(POOL_WINDOWS), POOL_GC, POOL_GC), lambda i: (0, 0, 0))
    return _call(
        body, hosted, name="pool_bwd", grid=(nt,),
        in_specs=[row, _prev_halo_spec(tt, POOL_HALO, D_MODEL), row, nxt, vec, wsp, vec, vec],
        out_specs=[row, wsp, pl.BlockSpec((8, D_MODEL), lambda i: (0, 0))],
        out_shape=[jax.ShapeDtypeStruct(h0.shape, F32),
                   jax.ShapeDtypeStruct((len(POOL_WINDOWS), POOL_GC, POOL_GC), F32),
                   jax.ShapeDtypeStruct((8, D_MODEL), F32)],
        operands=(h0, h0, dh1, dh1, g_mix, pool_w, pool_b, pool_scale), semantics=("arbitrary",))


CONV_ROWS = 512
CONV_SUB = 64
LANES = 128


def _causal_conv(u_ext, w, b):
    return w[0:1] * _shift_down(u_ext, 2) + w[1:2] * _shift_down(u_ext, 1) + w[2:3] * u_ext + b


def _piece_rows(load, s, sub, tt, top, bottom=None):
    lo, hi = s * sub - CONV_HALO, (s + 1) * sub + (CONV_HALO if bottom else 0)
    parts = [top()] if lo < 0 else []
    lo = max(lo, 0)
    parts.append(load(lo, min(hi, tt) - lo))
    if hi > tt:
        parts.append(bottom())
    return parts[0] if len(parts) == 1 else jnp.concatenate(parts, axis=0)


def _fold8(x):
    acc = x[0:8]
    for r in range(8, x.shape[0], 8):
        acc = acc + x[r:r + 8]
    return acc


def _conv_glu_fwd_call(u, conv_w, conv_b, name, hosted=None):
    _, t, f = u.shape
    tt = _tile(t, CONV_ROWS)
    sub = _tile(tt, CONV_SUB)
    per = tt // CONV_HALO

    def body(ins, outs, scratch, first, last):
        u_ref, up_ref, w_ref, b_ref = ins
        z_ref = outs[0]
        i = pl.program_id(0)
        first()

        def chunk(j, carry):
            lanes = pl.ds(pl.multiple_of(j * LANES, LANES), LANES)
            w, b = w_ref[:, :, lanes], b_ref[:, lanes]
            for s in range(tt // sub):
                c = []
                for hf in range(2):
                    ext = _piece_rows(lambda at, n: u_ref[hf, pl.ds(at, n), lanes].astype(F32), s, sub, tt,
                                      top=lambda: jnp.where(i > 0, up_ref[hf, :, lanes].astype(F32), 0.0))
                    c.append(_causal_conv(ext, w[:, hf, :], b[hf:hf + 1, :])[CONV_HALO:])
                z_ref[pl.ds(s * sub, sub), lanes] = (c[0] * c[1] * _sigmoid(c[1])).astype(z_ref.dtype)
            return carry

        lax.fori_loop(0, f // LANES, chunk, 0)
        last()

    res = _call(
        body, hosted, name=name, grid=(t // tt,),
        in_specs=[pl.BlockSpec((2, tt, f), lambda i: (0, i, 0)),
                  pl.BlockSpec((2, CONV_HALO, f), lambda i: (0, jnp.maximum(i * per - 1, 0), 0)),
                  pl.BlockSpec((3, 2, f), lambda i: (0, 0, 0)), pl.BlockSpec((2, f), lambda i: (0, 0))],
        out_specs=[pl.BlockSpec((tt, f), lambda i: (i, 0))], out_shape=[jax.ShapeDtypeStruct((t, f), MXU_DTYPE)],
        operands=(u, u, conv_w, conv_b), semantics=("parallel",))
    return res if hosted else res[0]


def _conv_glu_bwd_call(u, dz, conv_w, conv_b, name, hosted=None):
    _, t, f = u.shape
    tt = _tile(t, CONV_ROWS)
    sub = _tile(tt, 2 * CONV_SUB)
    nt = t // tt
    per = tt // CONV_HALO
    halo = CONV_HALO

    def body(ins, outs, scratch, first, last):
        u_ref, up_ref, un_ref, dz_ref, dzn_ref, w_ref, b_ref = ins
        du_ref, dw_ref, db_ref = outs
        dwacc, dbacc = scratch
        i = pl.program_id(0)
        first()

        @pl.when(i == 0)
        def _():
            dwacc[...] = jnp.zeros_like(dwacc)
            dbacc[...] = jnp.zeros_like(dbacc)

        main = slice(halo, halo + sub)

        def chunk(j, carry):
            lanes = pl.ds(pl.multiple_of(j * LANES, LANES), LANES)
            w, b = w_ref[:, :, lanes], b_ref[:, lanes]
            for s in range(tt // sub):
                ue = [_piece_rows(lambda at, n: u_ref[hf, pl.ds(at, n), lanes].astype(F32), s, sub, tt,
                                  top=lambda: jnp.where(i > 0, up_ref[hf, :, lanes].astype(F32), 0.0),
                                  bottom=lambda: un_ref[hf, :, lanes].astype(F32)) for hf in range(2)]
                a, g = [_causal_conv(ue[hf], w[:, hf, :], b[hf:hf + 1, :]) for hf in range(2)]
                sg = _sigmoid(g)
                if (s + 1) * sub < tt:
                    below = dz_ref[pl.ds(s * sub, sub + 2 * halo), lanes].astype(F32)[:sub + halo]
                else:
                    below = jnp.concatenate([dz_ref[pl.ds(s * sub, sub), lanes].astype(F32),
                                             jnp.where(i < nt - 1, dzn_ref[:, lanes].astype(F32), 0.0)], axis=0)
                dzs = jnp.concatenate([jnp.zeros((halo, LANES), F32), below], axis=0)
                gs = g * sg
                dc = [dzs * gs, dzs * a * (sg + gs - gs * sg)]
                for hf in range(2):
                    d = dc[hf]
                    d1, d2 = _shift_up(d, 1), _shift_up(d, 2)
                    du = w[2:3, hf, :] * d + w[1:2, hf, :] * d1 + w[0:1, hf, :] * d2
                    du_ref[hf, pl.ds(s * sub, sub), lanes] = du[main].astype(du_ref.dtype)
                    um = ue[hf][main]
                    for k, dd in enumerate((d2, d1, d)):
                        dwacc[k, hf, :, lanes] += _fold8(dd[main] * um)
                    dbacc[hf, :, lanes] += _fold8(d[main])
            return carry

        lax.fori_loop(0, f // LANES, chunk, 0)

        @pl.when(i == nt - 1)
        def _():
            dw_ref[...] = jnp.sum(dwacc[...], axis=2)
            db_ref[...] = jnp.sum(dbacc[...], axis=1)

        last()

    end = t // halo - 1
    return _call(
        body, hosted, name=name, grid=(nt,),
        in_specs=[pl.BlockSpec((2, tt, f), lambda i: (0, i, 0)),
                  pl.BlockSpec((2, halo, f), lambda i: (0, jnp.maximum(i * per - 1, 0), 0)),
                  pl.BlockSpec((2, halo, f), lambda i: (0, jnp.minimum((i + 1) * per, end), 0)),
                  pl.BlockSpec((tt, f), lambda i: (i, 0)),
                  pl.BlockSpec((halo, f), lambda i: (jnp.minimum((i + 1) * per, end), 0)),
                  pl.BlockSpec((3, 2, f), lambda i: (0, 0, 0)), pl.BlockSpec((2, f), lambda i: (0, 0))],
        out_specs=[pl.BlockSpec((2, tt, f), lambda i: (0, i, 0)),
                   pl.BlockSpec((3, 2, f), lambda i: (0, 0, 0)), pl.BlockSpec((2, f), lambda i: (0, 0))],
        out_shape=[jax.ShapeDtypeStruct(u.shape, MXU_DTYPE), jax.ShapeDtypeStruct((3, 2, f), F32),
                   jax.ShapeDtypeStruct((2, f), F32)],
        operands=(u, u, u, dz, dz, conv_w, conv_b), semantics=("arbitrary",),
        scratch_shapes=[pltpu.VMEM((3, 2, 8, f), F32), pltpu.VMEM((2, 8, f), F32)])


def _kv_even_odd(band, k):
    pair, parity = divmod(k, 2)
    blk = band[:, 128 * pair:128 * (pair + 1)].astype(F32)
    lane = lax.broadcasted_iota(jnp.int32, blk.shape, 1)
    own = jnp.where((lane >= HEAD_DIM) == (parity == 1), blk, 0.0)
    swapped = pltpu.roll(own, HEAD_DIM, axis=1)
    even, odd = (own, swapped) if parity == 0 else (swapped, own)
    return even.astype(MXU_DTYPE), odd.astype(MXU_DTYPE)


def _stack_pairs(ref, k):
    return jnp.concatenate([ref[:, 256 * k:256 * k + 128], ref[:, 256 * k + 128:256 * k + 256]], axis=0)


_GROUP_ORDER = (0, 2, 1, 3)


def _attn_fill_tables(bias_ref):
    shape = (4 * ATT_BLOCK, 2 * ATT_BLOCK)
    row = lax.broadcasted_iota(jnp.int32, shape, 0)
    col = lax.broadcasted_iota(jnp.int32, shape, 1)
    dist = (row & (ATT_BLOCK - 1)) + ATT_BLOCK - col
    ok = (dist >= 0) & (dist < ATT_BLOCK)
    distf = dist.astype(F32)
    rb = lax.broadcasted_iota(jnp.int32, (shape[0], 1), 0) // ATT_BLOCK
    for k in range(N_KV_HEADS):
        slope = jnp.zeros((shape[0], 1), F32)
        for r, gq in enumerate(_GROUP_ORDER):
            slope = jnp.where(rb == r, ALIBI_SLOPES[4 * k + gq], slope)
        bias = jnp.where(ok, -slope * distf, -jnp.inf)
        bias_ref[0, k] = bias
        bias_ref[1, k] = jnp.where(col >= ATT_BLOCK, bias, -jnp.inf)


def _attn_probs(qs, k_even, k_odd, bias, k, sink_ref):
    nt_dims = (((1,), (1,)), ((), ()))
    s = jnp.concatenate([lax.dot_general(qs, k_even, nt_dims, preferred_element_type=F32),
                         lax.dot_general(qs, k_odd, nt_dims, preferred_element_type=F32)], axis=0) + bias
    rb = lax.broadcasted_iota(jnp.int32, (s.shape[0], 1), 0) // ATT_BLOCK
    sink = jnp.zeros((s.shape[0], 1), F32)
    for r, gq in enumerate(_GROUP_ORDER):
        sink = jnp.where(rb == r, sink_ref[4 * k + gq], sink)
    m = jnp.maximum(jnp.max(s, axis=-1, keepdims=True), sink)
    p = jnp.exp(s - m)
    es = jnp.exp(sink - m)
    return p, 1.0 / (jnp.sum(p, axis=-1, keepdims=True) + es), es


_ATTN_TABLES = [pltpu.VMEM((2, N_KV_HEADS, 4 * ATT_BLOCK, 2 * ATT_BLOCK), F32)]


def _attn_specs(with_do):
    prev = lambda n: jnp.maximum(n - 1, 0)
    specs = [pl.BlockSpec(memory_space=pltpu.SMEM),
             pl.BlockSpec((ATT_BLOCK, D_MODEL), lambda n: (n, 0)),
             pl.BlockSpec((ATT_BLOCK, 256), lambda n: (prev(n), 4)), pl.BlockSpec((ATT_BLOCK, 256), lambda n: (n, 4)),
             pl.BlockSpec((ATT_BLOCK, 256), lambda n: (prev(n), 5)), pl.BlockSpec((ATT_BLOCK, 256), lambda n: (n, 5))]
    if with_do:
        specs.append(pl.BlockSpec((ATT_BLOCK, D_MODEL), lambda n: (n, 0)))
    return specs


def _attn_fwd_call(qkv, sinks, hosted=None):
    t = qkv.shape[0]

    def body(ins, outs, scratch, first, last):
        sink_ref, q_ref, kp_ref, kc_ref, vp_ref, vc_ref = ins
        o_ref, bias_ref = outs[0], scratch[0]
        n = pl.program_id(0)
        first()

        @pl.when(n == 0)
        def _():
            _attn_fill_tables(bias_ref)

        first = 1 - jnp.minimum(n, 1)
        kband = jnp.concatenate([kp_ref[...], kc_ref[...]], axis=0)
        vband = jnp.concatenate([vp_ref[...], vc_ref[...]], axis=0)
        for k in range(N_KV_HEADS):
            k_even, k_odd = _kv_even_odd(kband, k)
            v_even, v_odd = _kv_even_odd(vband, k)
            qs = _stack_pairs(q_ref, k) * ATT_SCALE
            p, rdenom, _ = _attn_probs(qs, k_even, k_odd, bias_ref[first, k], k, sink_ref)
            probs = (p * rdenom).astype(MXU_DTYPE)
            o = (jnp.dot(probs[:256], v_even, preferred_element_type=F32)
                 + jnp.dot(probs[256:], v_odd, preferred_element_type=F32))
            o_ref[:, 256 * k:256 * k + 128] = o[:128].astype(o_ref.dtype)
            o_ref[:, 256 * k + 128:256 * k + 256] = o[128:].astype(o_ref.dtype)
        last()

    return _call(
        body, hosted, name="attn_fwd", grid=(t // ATT_BLOCK,), in_specs=_attn_specs(False),
        out_specs=[pl.BlockSpec((ATT_BLOCK, D_MODEL), lambda n: (n, 0))],
        out_shape=[jax.ShapeDtypeStruct((t, D_MODEL), MXU_DTYPE)], operands=(sinks, qkv, qkv, qkv, qkv, qkv),
        semantics=("arbitrary",), scratch_shapes=_ATTN_TABLES)


def _attn_bwd_call(qkv, sinks, do, hosted=None):
    t = qkv.shape[0]
    nb = t // ATT_BLOCK
    tn_dims = (((0,), (0,)), ((), ()))
    nt_dims = (((1,), (1,)), ((), ()))

    def to_native(even_part, odd_part, parity):
        lane = lax.broadcasted_iota(jnp.int32, even_part.shape, 1)
        lo = lane < HEAD_DIM
        e, o = jnp.where(lo, even_part, 0.0), jnp.where(lo, 0.0, odd_part)
        if parity == 0:
            return e + pltpu.roll(o, HEAD_DIM, axis=1)
        return pltpu.roll(e, HEAD_DIM, axis=1) + o

    def body(ins, outs, scratch, first, last):
        sink_ref, q_ref, kp_ref, kc_ref, vp_ref, vc_ref, do_ref = ins
        dq_ref, dk_ref, dv_ref, db_ref, dsink_ref = outs
        bias_ref = scratch[0]
        n = pl.program_id(0)
        first()

        @pl.when(n == 0)
        def _():
            _attn_fill_tables(bias_ref)
            dk_ref[...] = jnp.zeros_like(dk_ref)
            dv_ref[...] = jnp.zeros_like(dv_ref)
            db_ref[...] = jnp.zeros_like(db_ref)
            dsink_ref[...] = jnp.zeros_like(dsink_ref)

        kband = jnp.concatenate([kp_ref[...], kc_ref[...]], axis=0)
        vband = jnp.concatenate([vp_ref[...], vc_ref[...]], axis=0)
        dk_pairs = [jnp.zeros((2 * ATT_BLOCK, 128), F32), jnp.zeros((2 * ATT_BLOCK, 128), F32)]
        dv_pairs = [jnp.zeros((2 * ATT_BLOCK, 128), F32), jnp.zeros((2 * ATT_BLOCK, 128), F32)]
        sink_lane = lax.broadcasted_iota(jnp.int32, (8, 128), 1)
        sink_row = lax.broadcasted_iota(jnp.int32, (8, 128), 0)
        dsink = jnp.zeros((8, 128), F32)
        first = 1 - jnp.minimum(n, 1)
        for k in range(N_KV_HEADS):
            k_even, k_odd = _kv_even_odd(kband, k)
            v_even, v_odd = _kv_even_odd(vband, k)
            qs = _stack_pairs(q_ref, k) * ATT_SCALE
            dos = _stack_pairs(do_ref, k)
            p, rdenom, es = _attn_probs(qs, k_even, k_odd, bias_ref[first, k], k, sink_ref)
            probs = p * rdenom
            dp = jnp.concatenate([lax.dot_general(dos, v_even, nt_dims, preferred_element_type=F32),
                                  lax.dot_general(dos, v_odd, nt_dims, preferred_element_type=F32)], axis=0)
            delta = jnp.sum(probs * dp, axis=-1, keepdims=True)
            ds16 = (probs * (dp - delta)).astype(MXU_DTYPE)
            p16 = probs.astype(MXU_DTYPE)
            dsink_rows = -(es * rdenom) * delta
            for r, gq in enumerate(_GROUP_ORDER):
                tot = jnp.sum(dsink_rows[ATT_BLOCK * r:ATT_BLOCK * (r + 1)], axis=0, keepdims=True)
                dsink = dsink + jnp.where((sink_lane == 4 * k + gq) & (sink_row == 0), tot, 0.0)
            dqs = (jnp.dot(ds16[:256], k_even, preferred_element_type=F32)
                   + jnp.dot(ds16[256:], k_odd, preferred_element_type=F32)) * ATT_SCALE
            dq_ref[:, 256 * k:256 * k + 128] = dqs[:128].astype(dq_ref.dtype)
            dq_ref[:, 256 * k + 128:256 * k + 256] = dqs[128:].astype(dq_ref.dtype)
            db_ref[:, 256 * k:256 * k + 128] += jnp.sum(dqs[:128], axis=0, keepdims=True)
            db_ref[:, 256 * k + 128:256 * k + 256] += jnp.sum(dqs[128:], axis=0, keepdims=True)
            pair, parity = divmod(k, 2)
            dk_pairs[pair] = dk_pairs[pair] + to_native(
                lax.dot_general(ds16[:256], qs, tn_dims, preferred_element_type=F32),
                lax.dot_general(ds16[256:], qs, tn_dims, preferred_element_type=F32), parity)
            dv_pairs[pair] = dv_pairs[pair] + to_native(
                lax.dot_general(p16[:256], dos, tn_dims, preferred_element_type=F32),
                lax.dot_general(p16[256:], dos, tn_dims, preferred_element_type=F32), parity)
        dsink_ref[...] += dsink
        dk_band = jnp.concatenate(dk_pairs, axis=1)
        dv_band = jnp.concatenate(dv_pairs, axis=1)
        cur = pl.multiple_of(n * ATT_BLOCK, ATT_BLOCK)
        dk_ref[pl.ds(cur, ATT_BLOCK), :] += dk_band[ATT_BLOCK:]
        dv_ref[pl.ds(cur, ATT_BLOCK), :] += dv_band[ATT_BLOCK:]

        @pl.when(n > 0)
        def _():
            prv = pl.multiple_of((n - 1) * ATT_BLOCK, ATT_BLOCK)
            dk_ref[pl.ds(prv, ATT_BLOCK), :] += dk_band[:ATT_BLOCK]
            dv_ref[pl.ds(prv, ATT_BLOCK), :] += dv_band[:ATT_BLOCK]

        @pl.when(n == nb - 1)
        def _():
            db_ref[:, 1024:1280] = jnp.sum(dk_ref[...], axis=0, keepdims=True)
            db_ref[:, 1280:1536] = jnp.sum(dv_ref[...], axis=0, keepdims=True)

        last()

    whole = lambda shape: pl.BlockSpec(shape, lambda n: (0, 0))
    return _call(
        body, hosted, name="attn_bwd", grid=(nb,), in_specs=_attn_specs(True),
        out_specs=[pl.BlockSpec((ATT_BLOCK, D_MODEL), lambda n: (n, 0)), whole((t, 256)), whole((t, 256)),
                   whole((1, QKV_DIM)), whole((8, 128))],
        out_shape=[jax.ShapeDtypeStruct((t, D_MODEL), MXU_DTYPE), jax.ShapeDtypeStruct((t, 256), F32),
                   jax.ShapeDtypeStruct((t, 256), F32), jax.ShapeDtypeStruct((1, QKV_DIM), F32),
                   jax.ShapeDtypeStruct((8, 128), F32)],
        operands=(sinks, qkv, qkv, qkv, qkv, qkv, do), semantics=("arbitrary",), scratch_shapes=_ATTN_TABLES)


FF_TILE = 1408


def _ffn_fwd(hn, h, w, conv_w, conv_b, layer, hooks=None, next_norm=None):
    t = hn.shape[0]
    tm = _tile(t, 1024)
    halves = D_FF // FF_TILE
    u, = _ride(hooks, "in", lambda hosted: _matmul(
        hn, w["w_in"], mode="nn", name=f"ffn{layer}_in", grid=(2 * halves, t // tm, 1),
        a_spec=pl.BlockSpec((tm, D_MODEL), lambda j, i, k: (i, 0)),
        b_spec=pl.BlockSpec((None, D_MODEL, FF_TILE), lambda j, i, k: (layer, 0, j)),
        o_spec=pl.BlockSpec((None, tm, FF_TILE), lambda j, i, k: (j // halves, i, j % halves)),
        out_shape=jax.ShapeDtypeStruct((2, t, D_FF), U_DTYPE), hosted=hosted))
    z, = _ride(hooks, "glu", lambda hosted: _conv_glu_fwd_call(u, conv_w, conv_b, f"ffn{layer}_glu", hosted=hosted))
    outs = _ride(hooks, "out", lambda hosted: _mm_nn(z, w["w_out"], f"ffn{layer}_out", b_lead=layer, residual=h,
                                                     tk=D_FF, hosted=hosted, norm_g=next_norm))
    return outs[0], (outs[1] if next_norm is not None else None), u, z


def _ffn_bwd(dh, hn, u, z, w_in, w_out, conv_w, conv_b, layer, g_in_acc, g_out_acc, norm, hooks=None):
    t = hn.shape[0]
    dz, = _ride(hooks, "dz", lambda hosted: _mm_nt(dh, w_out, f"ffn{layer}_dz", b_lead=layer, out_dtype=MXU_DTYPE,
                                                   tn=FF_TILE, hosted=hosted))
    g_out = _mm_tn(z, dh, f"ffn{layer}_dwout", lead=layer, n_lead=2, into=g_out_acc, tm=WGRAD_TILE, tk=t)
    du, dcw, dcb = _ride(hooks, "dglu", lambda hosted: _conv_glu_bwd_call(u, dz, conv_w, conv_b, f"ffn{layer}_dglu",
                                                                          hosted=hosted), {"g_out": g_out})
    per_half = D_FF // WGRAD_TILE
    g_in, = _ride(hooks, "dwin", lambda hosted: _matmul(
        hn, du, mode="tn", name=f"ffn{layer}_dwin", grid=(1, 2 * per_half, 1),
        a_spec=pl.BlockSpec((t, D_MODEL), lambda i, j, k: (0, 0)),
        b_spec=pl.BlockSpec((None, t, WGRAD_TILE), lambda i, j, k: (j // per_half, 0, j % per_half)),
        o_spec=pl.BlockSpec((None, D_MODEL, WGRAD_TILE), lambda i, j, k: (layer, 0, j)),
        out_shape=jax.ShapeDtypeStruct((2, D_MODEL, 2 * D_FF), F32), into=g_in_acc, hosted=hosted))
    dx, dx16, dg = _ride(hooks, "dhn", lambda hosted: _ffn_dhn_call(du, w_in, layer, norm, hosted),
                         {"g_in": g_in, "g_out": g_out})
    return dx, dx16, dg, g_in, g_out, dcw, dcb


WGRAD_TILE = 256


_NT_DIMS = (((1,), (1,)), ((), ()))


def _dhn_call(name, a, a_spec, w, w_spec, product, t, tm, norm, hosted=None):
    h, g, dres = norm

    def body(ins, outs, scratch, first, last):
        a_ref, w_ref, h_ref, g_ref, dres_ref = ins
        dx_ref, dx16_ref, dg_ref = outs
        first()
        dx, dgt = _rms_bwd(h_ref[...], g_ref[...], product(a_ref, w_ref))
        dx = dres_ref[...] + dx
        dx_ref[...] = dx
        dx16_ref[...] = dx.astype(dx16_ref.dtype)

        @pl.when(pl.program_id(0) == 0)
        def _():
            dg_ref[...] = jnp.zeros_like(dg_ref)

        dg_ref[...] += jnp.sum(dgt, axis=0, keepdims=True)
        last()

    row = pl.BlockSpec((tm, D_MODEL), lambda i: (i, 0))
    vec = pl.BlockSpec((1, D_MODEL), lambda i: (0, 0))
    return _call(
        body, hosted, name=name, grid=(t // tm,), in_specs=[a_spec, w_spec, row, vec, row], out_specs=[row, row, vec],
        out_shape=[jax.ShapeDtypeStruct((t, D_MODEL), F32), jax.ShapeDtypeStruct((t, D_MODEL), MXU_DTYPE),
                   jax.ShapeDtypeStruct((1, D_MODEL), F32)],
        operands=(a, w, h, g, dres), semantics=("arbitrary",))


def _ffn_dhn_call(du, w_in, layer, norm, hosted=None):
    _, t, f = du.shape
    tm = _tile(t, 512)

    def product(du_ref, w_ref):
        return (lax.dot_general(du_ref[0], w_ref[:, :f], _NT_DIMS, preferred_element_type=F32)
                + lax.dot_general(du_ref[1], w_ref[:, f:], _NT_DIMS, preferred_element_type=F32))

    return _dhn_call(f"ffn{layer}_dhn", du, pl.BlockSpec((2, tm, f), lambda i: (0, i, 0)), w_in,
                     pl.BlockSpec((None, D_MODEL, 2 * f), lambda i: (layer, 0, 0)), product, t, tm, norm, hosted)


def _attn_dhn_call(dqkv, w_qkv, norm):
    t, n = dqkv.shape
    tm = _tile(t, 1024)

    def product(a_ref, w_ref):
        return lax.dot_general(a_ref[...], w_ref[...], _NT_DIMS, preferred_element_type=F32)

    return _dhn_call("attn_dhn", dqkv, pl.BlockSpec((tm, n), lambda i: (i, 0)), w_qkv,
                     pl.BlockSpec((D_MODEL, n), lambda i: (0, 0)), product, t, tm, norm)


def _local_step(x, target, w, *, norm_mix, norm_ffn, norm_f, pool_b, pool_scale, b_qkv, sinks, conv_w, conv_b,
                late_gather=None, reducer=None):
    cw = [conv_w[l].reshape(3, 2, D_FF) for l in range(2)]
    cb = [conv_b[l].reshape(2, D_FF) for l in range(2)]
    gm = [norm_mix[l:l + 1] for l in range(2)]
    gf = [norm_ffn[l:l + 1] for l in range(2)]

    w = dict(w)
    fwd_hooks = {}
    if late_gather:
        def gather_hook(stage):
            return (lambda env: late_gather(stage, w)), (lambda got: w.update(zip(_late_names(stage), got)))

        fwd_hooks = {stage: gather_hook(stage) for stage in GATHER_RIDES}

    h1, hn_f0 = _pool_fwd_call(x, gm[0], w["pool_w"], pool_b, pool_scale, gf[0])
    h2, hn_a, u0, z0 = _ffn_fwd(hn_f0, h1, w, cw[0], cb[0], 0, hooks=fwd_hooks, next_norm=gm[1])
    qkv = _mm_nn(hn_a, w["w_qkv"], "attn_qkv", bias=b_qkv, out_dtype=MXU_DTYPE, tn=768)
    o, = _ride(fwd_hooks, "attn_fwd", lambda hosted: _attn_fwd_call(qkv, sinks, hosted))
    h3, hn_f1 = _ride(fwd_hooks, "attn_out", lambda hosted: _mm_nn(o, w["w_o"], "attn_out", residual=h2, hosted=hosted,
                                                                   norm_g=gf[1]))
    h4, _, u1, z1 = _ffn_fwd(hn_f1, h3, w, cw[1], cb[1], 1)
    loss, dh, dh16, d_norm_f = _loss_call(h4, norm_f, target)

    rd = reducer
    first, mid, last = _Reducer.FIRST, _Reducer.MID, _Reducer.LAST
    hooks1 = hooks_attn = hooks0 = hooks_pool = None
    if rd:
        hooks1 = {"dhn": (lambda env: rd.swap(first, {"w_in": env["g_in"], "w_out": env["g_out"]}),
                          lambda got: rd.pair_sums(first, got))}
        hooks_attn = {"attn_bwd": (lambda env: rd.scatter(first), lambda got: rd.chip_sums(first, got))}

        def mid_and_last(got):
            rd.shared(mid, got[:len(mid)])
            rd.pair_sums(last, got[len(mid):])

        hooks0 = {
            "dz": (lambda env: rd.share(first), lambda got: rd.shared(first, got)),
            "dglu": (lambda env: rd.swap(mid, {"w_out": env["g_out"], "w_qkv": g_qkv, "w_o": g_o4}),
                     lambda got: rd.pair_sums(mid, got)),
            "dwin": (lambda env: rd.scatter(mid), lambda got: rd.chip_sums(mid, got)),
            "dhn": (lambda env: _both(rd.share(mid), rd.swap(last, {"w_in": env["g_in"]})), mid_and_last),
        }
        hooks_pool = {"pool_bwd": (lambda env: rd.scatter(last), lambda got: rd.chip_sums(last, got))}
    dh, dh16, d_gf1, g_in, g_out, dcw1, dcb1 = _ffn_bwd(
        dh16, hn_f1, u1, z1, w["w_in"], w["w_out"], cw[1], cb[1], 1, None, None, (h3, gf[1], dh), hooks=hooks1)
    do = _mm_nt(dh16, w["w_o"], "attn_do", out_dtype=MXU_DTYPE)
    g_o = _mm_tn(o, dh16, "attn_dwo", tn=WGRAD_TILE, tk=x.shape[0])
    g_o4 = g_o.reshape(N_CHIPS, 2, W_O_ROWS // 2, D_MODEL)
    dq, dk, dv, d_bqkv, d_sinks = _ride(hooks_attn, "attn_bwd", lambda hosted: _attn_bwd_call(qkv, sinks, do, hosted))
    dqkv = jnp.concatenate([dq, dk.astype(MXU_DTYPE), dv.astype(MXU_DTYPE)], axis=1)
    g_qkv = _mm_tn(hn_a, dqkv, "attn_dwqkv", tn=WGRAD_TILE, tk=x.shape[0])
    dh, dh16, d_gm1 = _attn_dhn_call(dqkv, w["w_qkv"], (h2, gm[1], dh))
    dh, _, d_gf0, g_in, g_out, dcw0, dcb0 = _ffn_bwd(
        dh16, hn_f0, u0, z0, w["w_in"], w["w_out"], cw[0], cb[0], 0, g_in, g_out, (h1, gf[0], dh), hooks=hooks0)
    grad_x, g_pool, dvec = _ride(hooks_pool, "pool_bwd", lambda hosted: _pool_bwd_call(
        x, dh, gm[0], w["pool_w"], pool_b, pool_scale, hosted))
    if rd:
        rd.shared(last, _alone(rd.share(last), "share_last"))

    grads = {
        "w_in": g_in, "w_out": g_out, "w_qkv": g_qkv, "w_o": g_o, "pool_w": g_pool,
        "pool_b": dvec[0:1], "pool_scale": dvec[1:2], "b_qkv": d_bqkv, "sinks": d_sinks[0:1, :N_HEADS],
        "norm_mix": jnp.concatenate([dvec[2:3], d_gm1], axis=0), "norm_ffn": jnp.concatenate([d_gf0, d_gf1], axis=0),
        "conv_w": jnp.stack([dcw0.reshape(3, 2 * D_FF), dcw1.reshape(3, 2 * D_FF)]),
        "conv_b": jnp.stack([dcb0.reshape(2 * D_FF), dcb1.reshape(2 * D_FF)]), "norm_f": d_norm_f,
    }
    return loss, grad_x, grads


W_IN_COLS = 2 * D_FF // N_CHIPS
W_OUT_ROWS = D_FF // N_CHIPS
QKV_COLS = QKV_DIM // N_CHIPS
W_O_ROWS = D_MODEL // N_CHIPS
POOL_ROWS = POOL_GC // N_CHIPS
BIG = ("w_in", "w_out", "w_qkv", "w_o", "pool_w")


def _mesh_place():
    x, y, c = lax.axis_index("x"), lax.axis_index("y"), lax.axis_index("c")
    chips = [(1 - x, y), (x, 1 - y), (1 - x, 1 - y)]
    return x, y, c, chips


def _col(k, width):
    return pl.ds(pl.multiple_of(k * width, 128), width)


def _row(k, height):
    return pl.ds(pl.multiple_of(k * height, 16), height)


def _full_piece(name, ref, k, h):
    if name == "w_in":
        return ref.at[h, :, _col(k, W_IN_COLS)]
    if name == "w_out":
        return ref.at[h, _row(k, W_OUT_ROWS), :]
    if name == "w_qkv":
        return ref.at[_row(h, D_MODEL // 2), _col(k, QKV_COLS)]
    if name == "w_o":
        return ref.at[_row(2 * k + h, W_O_ROWS // 2), :]
    return ref.at[pl.ds(2 * h, 2), _row(k, POOL_ROWS), :]


def _shard_half(name, ref, h):
    if name in ("w_in", "w_out"):
        return ref.at[h]
    if name == "w_qkv":
        return ref.at[_row(h, D_MODEL // 2), :]
    if name == "w_o":
        return ref.at[_row(h, W_O_ROWS // 2), :]
    return ref.at[pl.ds(2 * h, 2)]


FULL_SHAPES = {"w_in": (2, D_MODEL, 2 * D_FF), "w_out": (2, D_FF, D_MODEL), "w_qkv": (D_MODEL, QKV_DIM),
               "w_o": (D_MODEL, D_MODEL), "pool_w": (4, POOL_GC, POOL_GC)}
SHARD_SHAPES = {"w_in": (2, D_MODEL, W_IN_COLS), "w_out": (2, W_OUT_ROWS, D_MODEL), "w_qkv": (D_MODEL, QKV_COLS),
                "w_o": (W_O_ROWS, D_MODEL), "pool_w": (4, POOL_ROWS, POOL_GC)}
HALF_SHAPES = {"w_in": (D_MODEL, W_IN_COLS), "w_out": (W_OUT_ROWS, D_MODEL), "w_qkv": (D_MODEL // 2, QKV_COLS),
               "w_o": (W_O_ROWS // 2, D_MODEL), "pool_w": (2, POOL_ROWS, POOL_GC)}


def _place_shard(name, shard, me_arr):
    if name == "w_in":
        blk, grid = (1, 256, W_IN_COLS), (2, D_MODEL // 256)
        src, dst = (lambda a, b, me: (a, b, 0)), (lambda a, b, me: (a, b, me[0]))
    elif name == "w_out":
        blk, grid = (1, W_OUT_ROWS, D_MODEL), (2, 1)
        src, dst = (lambda a, b, me: (a, 0, 0)), (lambda a, b, me: (a, me[0], 0))
    elif name == "w_qkv":
        blk, grid = (256, QKV_COLS), (1, D_MODEL // 256)
        src, dst = (lambda a, b, me: (b, 0)), (lambda a, b, me: (b, me[0]))
    elif name == "w_o":
        blk, grid = (W_O_ROWS, D_MODEL), (1, 1)
        src, dst = (lambda a, b, me: (0, 0)), (lambda a, b, me: (me[0], 0))
    else:
        blk, grid = (4, POOL_ROWS, POOL_GC), (1, 1)
        src, dst = (lambda a, b, me: (0, 0, 0)), (lambda a, b, me: (0, me[0], 0))

    def body(me_ref, s_ref, o_ref):
        o_ref[...] = s_ref[...].astype(o_ref.dtype)

    return pl.pallas_call(
        body, name=f"place_{name}",
        grid_spec=pltpu.PrefetchScalarGridSpec(num_scalar_prefetch=1, grid=grid, in_specs=[pl.BlockSpec(blk, src)],
                                               out_specs=pl.BlockSpec(blk, dst)),
        out_shape=jax.ShapeDtypeStruct(FULL_SHAPES[name], WIRE_DTYPE), compiler_params=_params("parallel", "parallel"),
    )(me_arr, shard)


def _when_mover(half, c, fn):
    if half is None:
        fn(c)
    else:
        pl.when(c == half)(lambda: fn(half))


def _when_taker(half, c, fn):
    if half is None:
        fn(1 - c)
    else:
        pl.when(c != half)(lambda: fn(half))


def _remote(ref, sems, t, j, dev):
    send, recv = sems
    return pltpu.make_async_remote_copy(src_ref=ref, dst_ref=ref, send_sem=send.at[t, j], recv_sem=recv.at[t, j],
                                        device_id=dev, device_id_type=MESH)


def _gather_ici(items, refs, sems, finish):
    x, y, c, chips = _mesh_place()
    me = 2 * x + y
    for t, (nm, half) in enumerate(items):
        def go(h, t=t, nm=nm):
            mine = _full_piece(nm, refs[t], me, h)
            for j, (cx, cy) in enumerate(chips):
                if not finish:
                    _remote(mine, sems, t, j, (cx, cy, c)).start()
                else:
                    _remote(_full_piece(nm, refs[t], 2 * cx + cy, h), sems, t, j, (cx, cy, c)).wait_recv()
                    _remote(mine, sems, t, j, (cx, cy, c)).wait_send()

        _when_mover(half, c, go)


def _gather_pass(items, refs, sems, finish):
    x, y, c, chips = _mesh_place()
    sibling = (x, y, 1 - c)
    for t, (nm, half) in enumerate(items):
        def give(h, t=t, nm=nm):
            for j, (cx, cy) in enumerate(chips):
                got = _remote(_full_piece(nm, refs[t], 2 * cx + cy, h), sems, t, j, sibling)
                if not finish:
                    got.start()
                else:
                    got.wait_send()

        def take(h, t=t, nm=nm):
            for j, (cx, cy) in enumerate(chips):
                _remote(_full_piece(nm, refs[t], 2 * cx + cy, h), sems, t, j, sibling).wait_recv()

        _when_mover(half, c, give)
        if finish:
            _when_taker(half, c, take)


def _gather_sems(n_items):
    return [pltpu.SemaphoreType.DMA((n_items, 3)), pltpu.SemaphoreType.DMA((n_items, 3))]


EARLY_ITEMS = (("pool_w", None), ("w_in", 0))
_LATE_A = (("w_out", 0), ("w_qkv", None), ("w_o", None))
_LATE_B = (("w_in", 1),)
_LATE_C = (("w_out", 1),)
GATHER_RIDES = {"in": ((), _LATE_A), "glu": (_LATE_A, _LATE_B), "out": (_LATE_B, ()), "attn_fwd": ((), _LATE_C),
                "attn_out": (_LATE_C, ())}


def _names(items):
    return [nm for nm, _ in items]


def _gather_early(full, small):
    items = EARLY_ITEMS
    n = len(items)

    def body(*refs):
        small_in = refs[n]
        outs, small_out = refs[n + 1:2 * n + 1], refs[2 * n + 1]
        scratch = refs[2 * n + 2:]
        ici_sems, pass_sems = scratch[0:2], scratch[2:4]
        local_sem, small_send, small_recv = scratch[4:]
        x, y, c, chips = _mesh_place()
        me = 2 * x + y
        local = pltpu.make_async_copy(small_in, small_out.at[me], local_sem.at[0])
        local.start()
        small_sends = [pltpu.make_async_remote_copy(
            src_ref=small_in, dst_ref=small_out.at[me], send_sem=small_send.at[j], recv_sem=small_recv.at[j],
            device_id=(cx, cy, c), device_id_type=MESH) for j, (cx, cy) in enumerate(chips)]
        for cp in small_sends:
            cp.start()
        _gather_ici(items, outs, ici_sems, finish=False)
        _gather_ici(items, outs, ici_sems, finish=True)
        _gather_pass(items, outs, pass_sems, finish=False)
        _gather_pass(items, outs, pass_sems, finish=True)
        for j, (cx, cy) in enumerate(chips):
            got = small_out.at[2 * cx + cy]
            pltpu.make_async_remote_copy(src_ref=got, dst_ref=got, send_sem=small_send.at[j],
                                         recv_sem=small_recv.at[j], device_id=(cx, cy, c), device_id_type=MESH).wait_recv()
        for cp in small_sends:
            cp.wait_send()
        local.wait()

    arrays = [full[nm] for nm in _names(items)]
    out_shape = [jax.ShapeDtypeStruct(a.shape, a.dtype) for a in arrays]
    out_shape.append(jax.ShapeDtypeStruct((N_CHIPS,) + small.shape, F32))
    res = pl.pallas_call(
        body, name="gather_early", in_specs=[ANY] * (n + 1), out_specs=[ANY] * (n + 1), out_shape=out_shape,
        input_output_aliases={t: t for t in range(n)},
        scratch_shapes=_gather_sems(n) + _gather_sems(n) + [pltpu.SemaphoreType.DMA((1,)), pltpu.SemaphoreType.DMA((3,)),
                                                            pltpu.SemaphoreType.DMA((3,))],
    )(*arrays, small)
    return {**full, **dict(zip(_names(items), res[:n]))}, res[n]


def _late_gather_host(stage, w):
    passing, sending = GATHER_RIDES[stage]
    n_pass = len(passing)

    def run(src, refs, new, sems, finish):
        if passing:
            _gather_pass(passing, refs[:n_pass], sems[:2], finish)
        if sending:
            _gather_ici(sending, refs[n_pass:], sems[-2:], finish)

    sems = (_gather_sems(n_pass) if passing else []) + (_gather_sems(len(sending)) if sending else [])
    return _Hosted([], [w[nm] for nm in _late_names(stage)], [], sems, functools.partial(run, finish=False),
                   functools.partial(run, finish=True))


def _late_names(stage):
    passing, sending = GATHER_RIDES[stage]
    return _names(passing) + _names(sending)


def _other_half_src(name, ref, h):
    if name in ("w_in", "w_out"):
        return ref.at[h]
    if name == "w_qkv":
        return ref.at[_row(h, D_MODEL // 2), :]
    if name == "w_o":
        return ref.at[:, pl.ds(h, 1)]
    return ref.at[pl.ds(2 * h, 2)]


SLAB_SHAPES = {"w_in": (D_MODEL, 2 * D_FF), "w_out": (D_FF, D_MODEL), "w_qkv": (D_MODEL // 2, QKV_DIM),
               "w_o": (N_CHIPS, 1, W_O_ROWS // 2, D_MODEL), "pool_w": (2, POOL_GC, POOL_GC)}


def _item_tag(item):
    return item[0] if item[1] is None else f"{item[0]}{item[1]}"


def _reduce_swap(items, grads, slabs, sems, finish):
    x, y, c, _ = _mesh_place()
    send, recv = sems
    for t, (nm, half) in enumerate(items):
        def copy(h, t=t, nm=nm):
            return pltpu.make_async_remote_copy(
                src_ref=_other_half_src(nm, grads[t], h), dst_ref=slabs[t], send_sem=send.at[t], recv_sem=recv.at[t],
                device_id=(x, y, 1 - c), device_id_type=MESH)

        def give(h):
            if finish:
                copy(h).wait_send()
            else:
                copy(h).start()

        if half is None:
            give(1 - c)
            if finish:
                copy(c).wait_recv()
        else:
            pl.when(c != half)(lambda: give(half))
            if finish:
                pl.when(c == half)(lambda: copy(half).wait_recv())


def _reduce_scatter(items, sums, parts, sems, finish):
    _, _, c, chips = _mesh_place()
    send, recv = sems
    for t, (nm, half) in enumerate(items):
        def go(h, t=t, nm=nm):
            for j, (cx, cy) in enumerate(chips):
                cp = pltpu.make_async_remote_copy(
                    src_ref=_slab_piece(nm, sums[t], 2 * cx + cy), dst_ref=parts[t].at[j], send_sem=send.at[t, j],
                    recv_sem=recv.at[t, j], device_id=(cx, cy, c), device_id_type=MESH)
                if finish:
                    cp.wait_recv()
                    cp.wait_send()
                else:
                    cp.start()

        _when_mover(half, c, go)


def _reduce_share(items, shards, sems, finish):
    x, y, c, _ = _mesh_place()
    send, recv = sems
    for t, (nm, half) in enumerate(items):
        def copy(h, t=t, nm=nm):
            part = _shard_half(nm, shards[t], h)
            return pltpu.make_async_remote_copy(src_ref=part, dst_ref=part, send_sem=send.at[t], recv_sem=recv.at[t],
                                                device_id=(x, y, 1 - c), device_id_type=MESH)

        def give(h):
            if finish:
                copy(h).wait_send()
            else:
                copy(h).start()

        _when_mover(half, c, give)
        if finish:
            _when_taker(half, c, lambda h: copy(h).wait_recv())


def _pair_sems(n):
    return [pltpu.SemaphoreType.DMA((n,)), pltpu.SemaphoreType.DMA((n,))]


def _swap_host(items, grads):
    def run(src, arr, new, sems, finish):
        _reduce_swap(items, src, new, sems, finish)

    return _Hosted([grads[nm] for nm, _ in items], [], [jax.ShapeDtypeStruct(SLAB_SHAPES[nm], F32) for nm, _ in items],
                   _pair_sems(len(items)), functools.partial(run, finish=False), functools.partial(run, finish=True))


def _scatter_host(items, sums):
    def run(src, arr, new, sems, finish):
        _reduce_scatter(items, src, new, sems, finish)

    new = [jax.ShapeDtypeStruct((N_CHIPS - 1,) + HALF_SHAPES[nm], WIRE_DTYPE) for nm, _ in items]
    return _Hosted(list(sums), [], new, _gather_sems(len(items)), functools.partial(run, finish=False),
                   functools.partial(run, finish=True))


def _share_host(items, shards):
    def run(src, arr, new, sems, finish):
        _reduce_share(items, arr, sems, finish)

    return _Hosted([], list(shards), [], _pair_sems(len(items)), functools.partial(run, finish=False),
                   functools.partial(run, finish=True))


def _alone(hosted, name):
    def body(ins, outs, scratch, first, last):
        first()
        last()

    return _call(body, hosted, name=name, grid=(1,), in_specs=[], out_specs=[], out_shape=[], operands=(),
                 semantics=("arbitrary",))


def _pair_sum(item, grad, slab, place):
    name, half = item
    shape = SLAB_SHAPES[name]
    hsel = (lambda p: p[1]) if half is None else (lambda p: half)
    if name == "w_in":
        blk = (256, 2 * D_FF)
        grid = (D_MODEL // 256,)
        g_spec = pl.BlockSpec((None,) + blk, lambda i, p: (hsel(p), i, 0))
        s_spec = pl.BlockSpec(blk, lambda i, p: (i, 0))
    elif name == "w_out":
        blk = (W_OUT_ROWS, D_MODEL)
        grid = (N_CHIPS,)
        g_spec = pl.BlockSpec((None,) + blk, lambda i, p: (hsel(p), i, 0))
        s_spec = pl.BlockSpec(blk, lambda i, p: (i, 0))
    elif name == "w_qkv":
        grid = (1,)
        g_spec = pl.BlockSpec(shape, lambda i, p: (hsel(p), 0))
        s_spec = pl.BlockSpec(shape, lambda i, p: (0, 0))
    else:
        grid = (1,)
        g_spec = pl.BlockSpec(shape, lambda i, p: (0, hsel(p), 0, 0))
        s_spec = pl.BlockSpec(shape, lambda i, p: (0, 0, 0, 0))

    def body(p_ref, g_ref, s_ref, o_ref):
        @pl.when(hsel(p_ref) == p_ref[1])
        def _():
            o_ref[...] = (g_ref[...] + s_ref[...]).astype(o_ref.dtype)

    return pl.pallas_call(
        body, name=f"pair_sum_{_item_tag(item)}",
        grid_spec=pltpu.PrefetchScalarGridSpec(num_scalar_prefetch=1, grid=grid, in_specs=[g_spec, s_spec],
                                               out_specs=s_spec),
        out_shape=jax.ShapeDtypeStruct(shape, WIRE_DTYPE), compiler_params=_params("parallel"),
    )(place, grad, slab)


def _slab_piece(name, ref, k):
    if name == "w_in":
        return ref.at[:, _col(k, W_IN_COLS)]
    if name == "w_out":
        return ref.at[_row(k, W_OUT_ROWS), :]
    if name == "w_qkv":
        return ref.at[:, _col(k, QKV_COLS)]
    if name == "w_o":
        return ref.at[k, 0]
    return ref.at[:, _row(k, POOL_ROWS), :]


def _sum_chips(item, slab, parts, place, into=None):
    name, sel = item
    half = HALF_SHAPES[name]
    hsel = (lambda p: p[1]) if sel is None else (lambda p: sel)
    if name == "w_in":
        blk, grid = (256, W_IN_COLS), (D_MODEL // 256,)
        own = pl.BlockSpec(blk, lambda i, p: (i, p[0]))
        got = pl.BlockSpec((N_CHIPS - 1,) + blk, lambda i, p: (0, i, 0))
        out = pl.BlockSpec((None,) + blk, lambda i, p: (hsel(p), i, 0))
    elif name == "w_out":
        grid = (1,)
        own = pl.BlockSpec(half, lambda i, p: (p[0], 0))
        got = pl.BlockSpec((N_CHIPS - 1,) + half, lambda i, p: (0, 0, 0))
        out = pl.BlockSpec((None,) + half, lambda i, p: (hsel(p), 0, 0))
    elif name == "w_qkv":
        grid = (1,)
        own = pl.BlockSpec(half, lambda i, p: (0, p[0]))
        got = pl.BlockSpec((N_CHIPS - 1,) + half, lambda i, p: (0, 0, 0))
        out = pl.BlockSpec(half, lambda i, p: (hsel(p), 0))
    else:
        grid = (1,)
        own = pl.BlockSpec((None, None) + half, lambda i, p: (p[0], 0, 0, 0))
        got = pl.BlockSpec((N_CHIPS - 1,) + half, lambda i, p: (0, 0, 0))
        out = pl.BlockSpec(half, lambda i, p: (hsel(p), 0))

    def body(p_ref, own_ref, got_ref, *rest):
        o_ref = rest[-1]

        @pl.when(hsel(p_ref) == p_ref[1])
        def _():
            acc = own_ref[...].astype(F32)
            for j in range(N_CHIPS - 1):
                acc = acc + got_ref[j].astype(F32)
            o_ref[...] = acc

    in_specs, operands, aliases = [own, got], [place, slab, parts], {}
    if into is not None:
        in_specs.append(ANY)
        operands.append(into)
        aliases = {3: 0}
    return pl.pallas_call(
        body, name=f"sum_chips_{_item_tag(item)}",
        grid_spec=pltpu.PrefetchScalarGridSpec(num_scalar_prefetch=1, grid=grid, in_specs=in_specs, out_specs=out),
        out_shape=jax.ShapeDtypeStruct(SHARD_SHAPES[name], F32), input_output_aliases=aliases,
        compiler_params=_params("parallel"),
    )(*operands)


class _Reducer:
    FIRST = (("w_in", 1), ("w_out", 1))
    MID = (("w_out", 0), ("w_qkv", None), ("w_o", None))
    LAST = (("w_in", 0),)

    def __init__(self, place):
        self.place = place
        self.shards = {}
        self.state = {}

    def swap(self, items, grads):
        self.state[items] = {"grads": [grads[nm] for nm, _ in items]}
        return _swap_host(items, grads)

    def pair_sums(self, items, slabs):
        st = self.state[items]
        st["sums"] = [_pair_sum(it, g, s, self.place) for it, g, s in zip(items, st["grads"], slabs)]

    def scatter(self, items):
        return _scatter_host(items, self.state[items]["sums"])

    def chip_sums(self, items, parts):
        for it, s, p in zip(items, self.state[items]["sums"], parts):
            self.shards[it[0]] = _sum_chips(it, s, p, self.place, into=self.shards.get(it[0]))

    def share(self, items):
        return _share_host(items, [self.shards[nm] for nm, _ in items])

    def shared(self, items, updated):
        self.shards.update(dict(zip(_names(items), updated)))


N_DEV = 8


def _all_reduce_small(vec, g_pool):
    def body(v_ref, gp_ref, o_ref, po_ref, slots, pslots, send_sems, recv_sems, psend, precv, hsend, hrecv):
        x, y, c, _ = _mesh_place()
        me = 4 * x + 2 * y + c

        def pool_piece(chip, h):
            return gp_ref.at[pl.ds(2 * h, 2), pl.ds(pl.multiple_of(chip * POOL_ROWS, POOL_ROWS), POOL_ROWS), :]

        slots[me] = v_ref[...]
        pslots[me] = pool_piece(2 * x + y, c)[...]
        copies = []
        for d in range(1, N_DEV):
            peer = (x ^ (d >> 2), y ^ ((d >> 1) & 1), c ^ (d & 1))
            copies.append(pltpu.make_async_remote_copy(
                src_ref=v_ref, dst_ref=slots.at[me], send_sem=send_sems.at[d - 1], recv_sem=recv_sems.at[d - 1],
                device_id=peer, device_id_type=MESH))
            copies.append(pltpu.make_async_remote_copy(
                src_ref=pool_piece(2 * peer[0] + peer[1], peer[2]), dst_ref=pslots.at[me], send_sem=psend.at[d - 1],
                recv_sem=precv.at[d - 1], device_id=peer, device_id_type=MESH))
        for cp in copies:
            cp.start()
        for d in range(1, N_DEV):
            peer = 4 * (x ^ (d >> 2)) + 2 * (y ^ ((d >> 1) & 1)) + (c ^ (d & 1))
            for buf, ss, rs in ((slots, send_sems, recv_sems), (pslots, psend, precv)):
                got = buf.at[peer]
                pltpu.make_async_remote_copy(src_ref=got, dst_ref=got, send_sem=ss.at[d - 1], recv_sem=rs.at[d - 1],
                                             device_id=(x, y, c), device_id_type=MESH).wait_recv()
        for cp in copies:
            cp.wait_send()
        acc, pacc = slots[0], pslots[0]
        for d in range(1, N_DEV):
            acc, pacc = acc + slots[d], pacc + pslots[d]
        o_ref[...] = acc
        po_ref[pl.ds(2 * c, 2)] = pacc
        mine, theirs = po_ref.at[pl.ds(2 * c, 2)], po_ref.at[pl.ds(2 * (1 - c), 2)]
        give = pltpu.make_async_remote_copy(src_ref=mine, dst_ref=mine, send_sem=hsend.at[0], recv_sem=hrecv.at[0],
                                            device_id=(x, y, 1 - c), device_id_type=MESH)
        give.start()
        pltpu.make_async_remote_copy(src_ref=theirs, dst_ref=theirs, send_sem=hsend.at[0], recv_sem=hrecv.at[0],
                                     device_id=(x, y, 1 - c), device_id_type=MESH).wait_recv()
        give.wait_send()

    vm = pl.BlockSpec(memory_space=pltpu.VMEM)
    piece = (2, POOL_ROWS, POOL_GC)
    return pl.pallas_call(
        body, name="all_reduce_small", in_specs=[vm, vm], out_specs=[vm, vm],
        out_shape=[jax.ShapeDtypeStruct(vec.shape, F32), jax.ShapeDtypeStruct(SHARD_SHAPES["pool_w"], F32)],
        scratch_shapes=[pltpu.VMEM((N_DEV,) + vec.shape, F32), pltpu.VMEM((N_DEV,) + piece, F32)]
        + [pltpu.SemaphoreType.DMA((N_DEV - 1,))] * 4 + [pltpu.SemaphoreType.DMA((1,))] * 2,
    )(vec, g_pool)


def _adamw_call(w, g, m, v, name, copy_grad=False):
    shape = w.shape
    cols = shape[-1]
    rows = w.size // cols
    tr = rows
    for cand in (256, 128, 64, 32, 16, 8):
        if rows > cand and rows % cand == 0 and cand * cols * 4 <= 2 * 1024 * 1024:
            tr = cand
            break

    def body(w_ref, g_ref, m_ref, v_ref, d_ref, nm_ref, nv_ref, *g_out):
        gv = g_ref[...]
        if copy_grad:
            g_out[0][...] = gv
        mn = ADAM_B1 * m_ref[...] + (1.0 - ADAM_B1) * gv
        vn = ADAM_B2 * v_ref[...] + (1.0 - ADAM_B2) * jnp.square(gv)
        m_hat = mn / (1.0 - ADAM_B1 ** ADAM_STEP)
        v_hat = vn / (1.0 - ADAM_B2 ** ADAM_STEP)
        d_ref[...] = -ADAM_LR * (m_hat / (jnp.sqrt(v_hat) + ADAM_EPS) + ADAM_WD * w_ref[...])
        nm_ref[...] = mn
        nv_ref[...] = vn

    spec = pl.BlockSpec((tr, cols), lambda i: (i, 0))
    flat = lambda a: a.reshape(rows, cols)
    n_out = 4 if copy_grad else 3
    outs = pl.pallas_call(
        body, name=name, grid=(rows // tr,), in_specs=[spec] * 4, out_specs=[spec] * n_out,
        out_shape=[jax.ShapeDtypeStruct((rows, cols), F32)] * n_out, compiler_params=_params("parallel"),
    )(flat(w), flat(g), flat(m), flat(v))
    return tuple(o.reshape(shape) for o in outs)


WEIGHTS = ("pool_w", "pool_b", "pool_scale", "attn_w_qkv", "attn_b_qkv", "attn_sinks", "attn_w_o", "norm_mix",
           "norm_ffn", "ffn_w_in", "ffn_conv_w", "ffn_conv_b", "ffn_w_out", "norm_f")
SMALL_GATHER = 128 * 71
_SMALL_SIZES = (("pool_b", 1024), ("pool_scale", 1024), ("b_qkv", 1536), ("sinks", 128), ("norm_mix", 2048),
                ("norm_ffn", 2048), ("conv_w", 6 * 2 * D_FF), ("conv_b", 4 * D_FF), ("norm_f", 1024))
SMALL_COLS = 6784


def _pack_small(grads):
    flat = jnp.concatenate([jnp.pad(grads[nm].reshape(-1), (0, size - grads[nm].size)) for nm, size in _SMALL_SIZES])
    return jnp.pad(flat, (0, 8 * SMALL_COLS - flat.size)).reshape(8, SMALL_COLS)


def _unpack_small(vec):
    flat = vec.reshape(-1)
    out, off = {}, 0
    for nm, size in _SMALL_SIZES:
        out[nm] = flat[off:off + size]
        off += size
    return out


def kernel(x, pool_w, pool_b, pool_scale, attn_w_qkv, attn_b_qkv, attn_sinks, attn_w_o, norm_mix, norm_ffn, ffn_w_in, ffn_conv_w, ffn_conv_b, ffn_w_out, norm_f, loss_target, m_pool_w, m_pool_b, m_pool_scale, m_attn_w_qkv, m_attn_b_qkv, m_attn_sinks, m_attn_w_o, m_norm_mix, m_norm_ffn, m_ffn_w_in, m_ffn_conv_w, m_ffn_conv_b, m_ffn_w_out, m_norm_f, v_pool_w, v_pool_b, v_pool_scale, v_attn_w_qkv, v_attn_b_qkv, v_attn_sinks, v_attn_w_o, v_norm_mix, v_norm_ffn, v_ffn_w_in, v_ffn_conv_w, v_ffn_conv_b, v_ffn_w_out, v_norm_f):
    weights = dict(pool_w=pool_w, pool_b=pool_b, pool_scale=pool_scale, attn_w_qkv=attn_w_qkv, attn_b_qkv=attn_b_qkv,
                   attn_sinks=attn_sinks, attn_w_o=attn_w_o, norm_mix=norm_mix, norm_ffn=norm_ffn, ffn_w_in=ffn_w_in,
                   ffn_conv_w=ffn_conv_w, ffn_conv_b=ffn_conv_b, ffn_w_out=ffn_w_out, norm_f=norm_f)
    m_in = dict(pool_w=m_pool_w, pool_b=m_pool_b, pool_scale=m_pool_scale, attn_w_qkv=m_attn_w_qkv,
                attn_b_qkv=m_attn_b_qkv, attn_sinks=m_attn_sinks, attn_w_o=m_attn_w_o, norm_mix=m_norm_mix,
                norm_ffn=m_norm_ffn, ffn_w_in=m_ffn_w_in, ffn_conv_w=m_ffn_conv_w, ffn_conv_b=m_ffn_conv_b,
                ffn_w_out=m_ffn_w_out, norm_f=m_norm_f)
    v_in = dict(pool_w=v_pool_w, pool_b=v_pool_b, pool_scale=v_pool_scale, attn_w_qkv=v_attn_w_qkv,
                attn_b_qkv=v_attn_b_qkv, attn_sinks=v_attn_sinks, attn_w_o=v_attn_w_o, norm_mix=v_norm_mix,
                norm_ffn=v_norm_ffn, ffn_w_in=v_ffn_w_in, ffn_conv_w=v_ffn_conv_w, ffn_conv_b=v_ffn_conv_b,
                ffn_w_out=v_ffn_w_out, norm_f=v_norm_f)
    chip = 2 * lax.axis_index("x") + lax.axis_index("y")
    core = lax.axis_index("c")

    place = jnp.stack([chip, core]).astype(jnp.int32)
    shards = {"w_in": ffn_w_in, "w_out": ffn_w_out, "w_qkv": attn_w_qkv[0], "w_o": attn_w_o[0], "pool_w": pool_w[0]}
    placed = {nm: _place_shard(nm, shards[nm], place) for nm in BIG}
    small = jnp.concatenate([pool_b.reshape(-1), attn_b_qkv.reshape(-1), ffn_conv_w.reshape(-1)]).reshape(1, SMALL_GATHER)
    full, small_all = _gather_early(placed, small)
    small_all = small_all.reshape(N_CHIPS, SMALL_GATHER)
    pool_b_full = small_all[:, :256].reshape(N_CHIPS, 4, POOL_ROWS).transpose(1, 0, 2).reshape(1, D_MODEL)
    b_qkv_full = small_all[:, 256:640].reshape(1, QKV_DIM)
    conv_w_full = small_all[:, 640:].reshape(N_CHIPS, 2, 3, W_IN_COLS).transpose(1, 2, 0, 3).reshape(2, 3, 2 * D_FF)

    reducer = _Reducer(place)
    loss, grad_x, grads = _local_step(
        x[0], loss_target[0], full, norm_mix=norm_mix, norm_ffn=norm_ffn, norm_f=norm_f.reshape(1, D_MODEL),
        pool_b=pool_b_full, pool_scale=pool_scale, b_qkv=b_qkv_full, sinks=attn_sinks[0], conv_w=conv_w_full,
        conv_b=ffn_conv_b, late_gather=_late_gather_host, reducer=reducer)
    loss = lax.psum(loss[0, 0], ("x", "y", "c"))
    reduced = dict(reducer.shards)

    small_sum, reduced["pool_w"] = _all_reduce_small(_pack_small(grads), grads["pool_w"])
    small_g = _unpack_small(small_sum)
    pool_b_g = lax.dynamic_slice_in_dim(small_g["pool_b"].reshape(4, N_CHIPS, POOL_ROWS), chip, 1, axis=1)
    b_qkv_g = lax.dynamic_slice_in_dim(small_g["b_qkv"].reshape(N_CHIPS, QKV_COLS), chip, 1, axis=0)
    conv_w_g = lax.dynamic_slice_in_dim(small_g["conv_w"].reshape(2, 3, N_CHIPS, W_IN_COLS), chip, 1, axis=2)
    grad_w = {
        "pool_w": reduced["pool_w"].reshape(pool_w.shape), "pool_b": pool_b_g.reshape(pool_b.shape),
        "pool_scale": small_g["pool_scale"].reshape(pool_scale.shape),
        "attn_w_qkv": reduced["w_qkv"].reshape(attn_w_qkv.shape), "attn_b_qkv": b_qkv_g.reshape(attn_b_qkv.shape),
        "attn_sinks": small_g["sinks"][:N_HEADS].reshape(attn_sinks.shape),
        "attn_w_o": reduced["w_o"].reshape(attn_w_o.shape), "norm_mix": small_g["norm_mix"].reshape(norm_mix.shape),
        "norm_ffn": small_g["norm_ffn"].reshape(norm_ffn.shape), "ffn_w_in": reduced["w_in"],
        "ffn_conv_w": conv_w_g.reshape(ffn_conv_w.shape), "ffn_conv_b": small_g["conv_b"].reshape(ffn_conv_b.shape),
        "ffn_w_out": reduced["w_out"], "norm_f": small_g["norm_f"].reshape(norm_f.shape),
    }

    delta, new_m, new_v = {}, {}, {}
    exchanged = ("pool_w", "attn_w_qkv", "attn_w_o", "ffn_w_in", "ffn_w_out")
    for nm in WEIGHTS:
        res = _adamw_call(weights[nm], grad_w[nm], m_in[nm], v_in[nm], f"adamw_{nm}", copy_grad=nm in exchanged)
        delta[nm], new_m[nm], new_v[nm] = res[:3]
        if nm in exchanged:
            grad_w[nm] = res[3]
    return (loss, grad_x.reshape(x.shape), *[grad_w[nm] for nm in WEIGHTS], *[delta[nm] for nm in WEIGHTS],
            *[new_m[nm] for nm in WEIGHTS], *[new_v[nm] for nm in WEIGHTS])
```

```python
import functools
from typing import Callable, NamedTuple

import jax
import jax.numpy as jnp
from jax import lax
from jax.experimental import pallas as pl
from jax.experimental.pallas import tpu as pltpu

F32 = jnp.float32
MXU_DTYPE = jnp.bfloat16
WIRE_DTYPE = jnp.bfloat16
U_DTYPE = jnp.float32

D_MODEL = 1024
D_FF = 2816
QKV_DIM = 1536
HEAD_DIM = 64
N_HEADS = 16
N_KV_HEADS = 4
ATT_BLOCK = 128
POOL_WINDOWS = (2, 4, 8, 16)
POOL_GC = 256
POOL_HALO = 16
CONV_HALO = 8
RMS_EPS = 1e-6
ATT_SCALE = HEAD_DIM ** -0.5
ALIBI_SLOPES = tuple(2.0 ** (-8.0 / N_HEADS * (h + 1)) for h in range(N_HEADS))

ADAM_LR = 0.001
ADAM_B1 = 0.9
ADAM_B2 = 0.999
ADAM_EPS = 1e-08
ADAM_WD = 0.01
ADAM_STEP = 10

N_CHIPS = 4
MESH = pl.DeviceIdType.MESH
VMEM_LIMIT_BYTES = 56 * 1024 * 1024
ANY = pl.BlockSpec(memory_space=pl.ANY)


def _params(*semantics):
    return pltpu.CompilerParams(dimension_semantics=semantics, vmem_limit_bytes=VMEM_LIMIT_BYTES)


def _rms(x, g):
    return x * lax.rsqrt(jnp.mean(x * x, axis=-1, keepdims=True) + RMS_EPS) * g


def _rms_bwd(x, g, dy):
    rstd = lax.rsqrt(jnp.mean(x * x, axis=-1, keepdims=True) + RMS_EPS)
    xhat = x * rstd
    dxhat = dy * g
    dx = rstd * (dxhat - xhat * jnp.mean(dxhat * xhat, axis=-1, keepdims=True))
    return dx, dy * xhat


def _shift_down(x, s):
    return pltpu.roll(x, s, axis=0)


def _shift_up(x, s):
    return pltpu.roll(x, x.shape[0] - s, axis=0)


def _sigmoid(x):
    return 1.0 / (1.0 + jnp.exp(-x))


_DOT_DIMS = {"nn": ((1,), (0,)), "nt": ((1,), (1,)), "tn": ((0,), (0,))}


class _Hosted(NamedTuple):
    sources: list
    arrays: list
    new: list
    sems: list
    first: Callable
    last: Callable


def _call(body, hosted, *, name, grid, in_specs, out_specs, out_shape, operands, semantics, scratch_shapes=(),
          aliases=None):
    n_in, n_out, n_scr = len(in_specs), len(out_specs), len(scratch_shapes)
    aliases = dict(aliases or {})
    if hosted is None:
        def plain(*refs):
            body(refs[:n_in], refs[n_in:n_in + n_out], refs[n_in + n_out:], lambda: None, lambda: None)

        return pl.pallas_call(plain, name=name, grid=grid, in_specs=list(in_specs), out_specs=list(out_specs),
                              out_shape=list(out_shape), scratch_shapes=list(scratch_shapes),
                              input_output_aliases=aliases, compiler_params=_params(*semantics))(*operands)
    ns, na, nn = len(hosted.sources), len(hosted.arrays), len(hosted.new)

    def carrying(*refs):
        ins, src = refs[:n_in], refs[n_in:n_in + ns]
        o0 = n_in + ns + na
        outs, arr = refs[o0:o0 + n_out], refs[o0 + n_out:o0 + n_out + na]
        new = refs[o0 + n_out + na:o0 + n_out + na + nn]
        s0 = o0 + n_out + na + nn
        scratch, sems = refs[s0:s0 + n_scr], refs[s0 + n_scr:]
        ids = [pl.program_id(d) for d in range(len(grid))]
        is_first = functools.reduce(lambda p, q: p & q, [i == 0 for i in ids])
        is_last = functools.reduce(lambda p, q: p & q, [i == n - 1 for i, n in zip(ids, grid)])
        body(ins, outs, scratch, lambda: pl.when(is_first)(lambda: hosted.first(src, arr, new, sems)),
             lambda: pl.when(is_last)(lambda: hosted.last(src, arr, new, sems)))

    for t in range(na):
        aliases[n_in + ns + t] = n_out + t
    return pl.pallas_call(
        carrying, name=name, grid=grid, in_specs=list(in_specs) + [ANY] * (ns + na),
        out_specs=list(out_specs) + [ANY] * (na + nn),
        out_shape=list(out_shape) + [jax.ShapeDtypeStruct(a.shape, a.dtype) for a in hosted.arrays] + list(hosted.new),
        scratch_shapes=list(scratch_shapes) + list(hosted.sems), input_output_aliases=aliases,
        compiler_params=_params(*["arbitrary"] * len(grid)))(*operands, *hosted.sources, *hosted.arrays)


def _both(h1, h2):
    def run(which, src, arr, new, sems):
        cut = lambda seq, n: (seq[:n], seq[n:])
        (s1, s2), (a1, a2) = cut(src, len(h1.sources)), cut(arr, len(h1.arrays))
        (n1, n2), (m1, m2) = cut(new, len(h1.new)), cut(sems, len(h1.sems))
        getattr(h1, which)(s1, a1, n1, m1)
        getattr(h2, which)(s2, a2, n2, m2)

    return _Hosted(h1.sources + h2.sources, h1.arrays + h2.arrays, h1.new + h2.new, h1.sems + h2.sems,
                   functools.partial(run, "first"), functools.partial(run, "last"))


def _ride(hooks, stage, call, env=None):
    if not hooks or stage not in hooks:
        res = call(None)
        return list(res) if isinstance(res, (list, tuple)) else [res]
    make, done = hooks[stage]
    hosted = make(env)
    res = call(hosted)
    own = len(res) - len(hosted.arrays) - len(hosted.new)
    done(res[own:])
    return list(res[:own])


def _matmul(a, b, *, mode, name, grid, a_spec, b_spec, o_spec, out_shape, bias=None, bias_spec=None, residual=None,
            residual_spec=None, into=None, hosted=None, norm_g=None):
    nk = grid[2]
    dims = (_DOT_DIMS[mode], ((), ()))
    acc_shape = tuple(d for d in o_spec.block_shape if d is not None)

    def body(ins, outs, scratch, first, last):
        a_ref, b_ref = ins[0], ins[1]
        pos = 2
        bias_ref = res_ref = None
        if bias is not None:
            bias_ref = ins[pos]
            pos += 1
        if residual is not None:
            res_ref = ins[pos]
            pos += 1
        o_ref = outs[0]
        acc_ref = scratch[0] if nk > 1 else None
        first()
        prod = lax.dot_general(a_ref[...].astype(MXU_DTYPE), b_ref[...].astype(MXU_DTYPE), dims,
                               preferred_element_type=F32)

        def finish(acc):
            if bias_ref is not None:
                acc = acc + bias_ref[...]
            if res_ref is not None:
                acc = acc + res_ref[...]
            o_ref[...] = acc.astype(o_ref.dtype)
            if norm_g is not None:
                outs[1][...] = _rms(acc, ins[pos][...]).astype(outs[1].dtype)

        if nk == 1:
            finish(prod)
        else:
            k = pl.program_id(2)

            @pl.when(k == 0)
            def _():
                acc_ref[...] = prod

            @pl.when(k > 0)
            def _():
                acc_ref[...] += prod

            @pl.when(k == nk - 1)
            def _():
                finish(acc_ref[...])

        last()

    operands, in_specs = [a, b], [a_spec, b_spec]
    if bias is not None:
        operands.append(bias)
        in_specs.append(bias_spec)
    if residual is not None:
        operands.append(residual)
        in_specs.append(residual_spec)
    out_specs, out_shapes = [o_spec], [out_shape]
    if norm_g is not None:
        assert acc_shape[-1] == norm_g.shape[-1], "the fused norm needs whole rows in an output tile"
        operands.append(norm_g)
        in_specs.append(bias_spec)
        out_specs.append(o_spec)
        out_shapes.append(jax.ShapeDtypeStruct(out_shape.shape, MXU_DTYPE))
    aliases = {}
    if into is not None:
        aliases = {len(operands): 0}
        operands.append(into)
        in_specs.append(ANY)
    res = _call(body, hosted, name=name, grid=grid, in_specs=in_specs, out_specs=out_specs, out_shape=out_shapes,
                operands=operands, semantics=("parallel", "parallel", "arbitrary"),
                scratch_shapes=[pltpu.VMEM(acc_shape, F32)] if nk > 1 else [], aliases=aliases)
    return res if hosted or norm_g is not None else res[0]


def _tile(n, want):
    t = min(n, want)
    assert n % t == 0, (n, want)
    return t


def _mm_nn(a, b, name, *, b_lead=None, bias=None, residual=None, out_dtype=F32, tm=1024, tn=1024, tk=1024,
           hosted=None, norm_g=None):
    (m, k), n = a.shape, b.shape[-1]
    tm, tn, tk = _tile(m, tm), _tile(n, tn), _tile(k, tk)
    if b_lead is None:
        b_spec = pl.BlockSpec((tk, tn), lambda i, j, kk: (kk, j))
    else:
        b_spec = pl.BlockSpec((None, tk, tn), lambda i, j, kk: (b_lead, kk, j))
    return _matmul(
        a, b, mode="nn", name=name, grid=(m // tm, n // tn, k // tk),
        a_spec=pl.BlockSpec((tm, tk), lambda i, j, kk: (i, kk)), b_spec=b_spec,
        o_spec=pl.BlockSpec((tm, tn), lambda i, j, kk: (i, j)), out_shape=jax.ShapeDtypeStruct((m, n), out_dtype),
        bias=bias, bias_spec=pl.BlockSpec((1, tn), lambda i, j, kk: (0, j)),
        residual=residual, residual_spec=pl.BlockSpec((tm, tn), lambda i, j, kk: (i, j)), hosted=hosted,
        norm_g=norm_g)


def _mm_nt(a, b, name, *, b_lead=None, out_dtype=F32, tm=1024, tn=1024, tk=1024, hosted=None):
    (m, k), n = a.shape, b.shape[-2]
    tm, tn, tk = _tile(m, tm), _tile(n, tn), _tile(k, tk)
    if b_lead is None:
        b_spec = pl.BlockSpec((tn, tk), lambda i, j, kk: (j, kk))
    else:
        b_spec = pl.BlockSpec((None, tn, tk), lambda i, j, kk: (b_lead, j, kk))
    return _matmul(
        a, b, mode="nt", name=name, grid=(m // tm, n // tn, k // tk),
        a_spec=pl.BlockSpec((tm, tk), lambda i, j, kk: (i, kk)), b_spec=b_spec,
        o_spec=pl.BlockSpec((tm, tn), lambda i, j, kk: (i, j)), out_shape=jax.ShapeDtypeStruct((m, n), out_dtype),
        hosted=hosted)


def _mm_tn(a, b, name, *, lead=None, n_lead=None, into=None, tm=1024, tn=1024, tk=1024, out_dtype=F32):
    (k, m), n = a.shape, b.shape[-1]
    tm, tn, tk = _tile(m, tm), _tile(n, tn), _tile(k, tk)
    if lead is None:
        o_spec = pl.BlockSpec((tm, tn), lambda i, j, kk: (i, j))
        out_shape = jax.ShapeDtypeStruct((m, n), out_dtype)
    else:
        o_spec = pl.BlockSpec((None, tm, tn), lambda i, j, kk: (lead, i, j))
        out_shape = jax.ShapeDtypeStruct((n_lead, m, n), out_dtype)
    return _matmul(
        a, b, mode="tn", name=name, grid=(m // tm, n // tn, k // tk),
        a_spec=pl.BlockSpec((tk, tm), lambda i, j, kk: (kk, i)),
        b_spec=pl.BlockSpec((tk, tn), lambda i, j, kk: (kk, j)), o_spec=o_spec, out_shape=out_shape, into=into)


ROW_TILE = 512


def _pool_windows_causal(ext, first_row):
    tt = ext.shape[0] - POOL_HALO
    t = first_row + lax.broadcasted_iota(jnp.int32, (tt, 1), 0)
    outs = []
    for gi, win in enumerate(POOL_WINDOWS):
        cols = slice(gi * POOL_GC, (gi + 1) * POOL_GC)
        s = ext[:, cols]
        sh = 1
        while sh < win:
            s = s + _shift_down(s, sh)
            sh *= 2
        count = jnp.minimum(t + 1, win).astype(F32)
        outs.append(s[POOL_HALO:] / count - ext[POOL_HALO:, cols])
    return outs


def _prev_halo_spec(tt, halo, width):
    per = tt // halo
    return pl.BlockSpec((halo, width), lambda i: (jnp.maximum(i * per - 1, 0), 0))


def _pool_fwd_call(h0, g_mix, pool_w, pool_b, pool_scale, g_ffn):
    t = h0.shape[0]
    tt = _tile(t, ROW_TILE)

    def body(h_ref, halo_ref, gm_ref, w_ref, b_ref, sc_ref, gf_ref, h1_ref, hn_ref):
        i = pl.program_id(0)
        x, gm = h_ref[...], gm_ref[...]
        top = jnp.where(i > 0, _rms(halo_ref[...], gm), 0.0)
        ext = jnp.concatenate([top, _rms(x, gm)], axis=0)
        ps = _pool_windows_causal(ext, i * tt)
        ys = [jnp.dot(p.astype(MXU_DTYPE), w_ref[gi], preferred_element_type=F32) for gi, p in enumerate(ps)]
        mix = (jnp.concatenate(ys, axis=1) + b_ref[...]) * sc_ref[...]
        h1 = x + mix
        h1_ref[...] = h1
        hn_ref[...] = _rms(h1, gf_ref[...]).astype(hn_ref.dtype)

    row = pl.BlockSpec((tt, D_MODEL), lambda i: (i, 0))
    vec = pl.BlockSpec((1, D_MODEL), lambda i: (0, 0))
    wsp = pl.BlockSpec((len(POOL_WINDOWS), POOL_GC, POOL_GC), lambda i: (0, 0, 0))
    return pl.pallas_call(
        body, name="pool_fwd", grid=(t // tt,),
        in_specs=[row, _prev_halo_spec(tt, POOL_HALO, D_MODEL), vec, wsp, vec, vec, vec], out_specs=[row, row],
        out_shape=[jax.ShapeDtypeStruct(h0.shape, F32), jax.ShapeDtypeStruct(h0.shape, MXU_DTYPE)],
        compiler_params=_params("parallel"))(h0, h0, g_mix, pool_w, pool_b, pool_scale, g_ffn)


def _pool_bwd_call(h0, dh1, g_mix, pool_w, pool_b, pool_scale, hosted=None):
    t = h0.shape[0]
    tt = _tile(t, ROW_TILE)
    nt = t // tt
    per = tt // POOL_HALO

    def body(ins, outs, scratch, first, last):
        h_ref, halo_ref, d_ref, dnext_ref, gm_ref, w_ref, b_ref, sc_ref = ins
        gx_ref, dw_ref, dv_ref = outs
        i = pl.program_id(0)
        first()
        x, gm, sc = h_ref[...], gm_ref[...], sc_ref[...]
        top = jnp.where(i > 0, _rms(halo_ref[...], gm), 0.0)
        ext = jnp.concatenate([top, _rms(x, gm)], axis=0)
        ps = _pool_windows_causal(ext, i * tt)

        dy = d_ref[...]
        dy_ext = jnp.concatenate([dy, jnp.where(i < nt - 1, dnext_ref[...], 0.0)], axis=0)
        dpre_ext = dy_ext * sc
        t_ext = i * tt + lax.broadcasted_iota(jnp.int32, (tt + POOL_HALO, 1), 0)

        @pl.when(i == 0)
        def _():
            dw_ref[...] = jnp.zeros_like(dw_ref)
            dv_ref[...] = jnp.zeros_like(dv_ref)

        dhn, ypre = [], []
        for gi, win in enumerate(POOL_WINDOWS):
            cols = slice(gi * POOL_GC, (gi + 1) * POOL_GC)
            w = w_ref[gi]
            p16 = ps[gi].astype(MXU_DTYPE)
            dpre16 = dpre_ext[:, cols].astype(MXU_DTYPE)
            ypre.append(jnp.dot(p16, w, preferred_element_type=F32))
            dw_ref[gi] += lax.dot_general(p16, dpre16[:tt], (((0,), (0,)), ((), ())), preferred_element_type=F32)
            dp_ext = lax.dot_general(dpre16, w, (((1,), (1,)), ((), ())), preferred_element_type=F32)
            s = dp_ext / jnp.minimum(t_ext + 1, win).astype(F32)
            sh = 1
            while sh < win:
                s = s + _shift_up(s, sh)
                sh *= 2
            dhn.append(s[:tt] - dp_ext[:tt])
        dhn = jnp.concatenate(dhn, axis=1)
        ypre = jnp.concatenate(ypre, axis=1) + b_ref[...]
        dx, dgt = _rms_bwd(x, gm, dhn)
        gx_ref[...] = dy + dx
        dv_ref[0:1, :] += jnp.sum(dpre_ext[:tt], axis=0, keepdims=True)
        dv_ref[1:2, :] += jnp.sum(dy * ypre, axis=0, keepdims=True)
        dv_ref[2:3, :] += jnp.sum(dgt, axis=0, keepdims=True)
        last()

    row = pl.BlockSpec((tt, D_MODEL), lambda i: (i, 0))
    vec = pl.BlockSpec((1, D_MODEL), lambda i: (0, 0))
    nxt = pl.BlockSpec((POOL_HALO, D_MODEL), lambda i: (jnp.minimum((i + 1) * per, t // POOL_HALO - 1), 0))
    wsp = pl.BlockSpec((len(POOL_WINDOWS), POOL_GC, POOL_GC), lambda i: (0, 0, 0))
    return _call(
        body, hosted, name="pool_bwd", grid=(nt,),
        in_specs=[row, _prev_halo_spec(tt, POOL_HALO, D_MODEL), row, nxt, vec, wsp, vec, vec],
        out_specs=[row, wsp, pl.BlockSpec((8, D_MODEL), lambda i: (0, 0))],
        out_shape=[jax.ShapeDtypeStruct(h0.shape, F32),
                   jax.ShapeDtypeStruct((len(POOL_WINDOWS), POOL_GC, POOL_GC), F32),
                   jax.ShapeDtypeStruct((8, D_MODEL), F32)],
        operands=(h0, h0, dh1, dh1, g_mix, pool_w, pool_b, pool_scale), semantics=("arbitrary",))


CONV_ROWS = 512
CONV_SUB = 64
LANES = 128


def _causal_conv(u_ext, w, b):
    return w[0:1] * _shift_down(u_ext, 2) + w[1:2] * _shift_down(u_ext, 1) + w[2:3] * u_ext + b


def _piece_rows(load, s, sub, tt, top, bottom=None):
    lo, hi = s * sub - CONV_HALO, (s + 1) * sub + (CONV_HALO if bottom else 0)
    parts = [top()] if lo < 0 else []
    lo = max(lo, 0)
    parts.append(load(lo, min(hi, tt) - lo))
    if hi > tt:
        parts.append(bottom())
    return parts[0] if len(parts) == 1 else jnp.concatenate(parts, axis=0)


def _fold8(x):
    acc = x[0:8]
    for r in range(8, x.shape[0], 8):
        acc = acc + x[r:r + 8]
    return acc


def _conv_glu_fwd_call(u, conv_w, conv_b, name, hosted=None):
    _, t, f = u.shape
    tt = _tile(t, CONV_ROWS)
    sub = _tile(tt, CONV_SUB)
    per = tt // CONV_HALO

    def body(ins, outs, scratch, first, last):
        u_ref, up_ref, w_ref, b_ref = ins
        z_ref = outs[0]
        i = pl.program_id(0)
        first()

        def chunk(j, carry):
            lanes = pl.ds(pl.multiple_of(j * LANES, LANES), LANES)
            w, b = w_ref[:, :, lanes], b_ref[:, lanes]
            for s in range(tt // sub):
                c = []
                for hf in range(2):
                    ext = _piece_rows(lambda at, n: u_ref[hf, pl.ds(at, n), lanes].astype(F32), s, sub, tt,
                                      top=lambda: jnp.where(i > 0, up_ref[hf, :, lanes].astype(F32), 0.0))
                    c.append(_causal_conv(ext, w[:, hf, :], b[hf:hf + 1, :])[CONV_HALO:])
                z_ref[pl.ds(s * sub, sub), lanes] = (c[0] * c[1] * _sigmoid(c[1])).astype(z_ref.dtype)
            return carry

        lax.fori_loop(0, f // LANES, chunk, 0)
        last()

    res = _call(
        body, hosted, name=name, grid=(t // tt,),
        in_specs=[pl.BlockSpec((2, tt, f), lambda i: (0, i, 0)),
                  pl.BlockSpec((2, CONV_HALO, f), lambda i: (0, jnp.maximum(i * per - 1, 0), 0)),
                  pl.BlockSpec((3, 2, f), lambda i: (0, 0, 0)), pl.BlockSpec((2, f), lambda i: (0, 0))],
        out_specs=[pl.BlockSpec((tt, f), lambda i: (i, 0))], out_shape=[jax.ShapeDtypeStruct((t, f), MXU_DTYPE)],
        operands=(u, u, conv_w, conv_b), semantics=("parallel",))
    return res if hosted else res[0]


def _conv_glu_bwd_call(u, dz, conv_w, conv_b, name, hosted=None):
    _, t, f = u.shape
    tt = _tile(t, CONV_ROWS)
    sub = _tile(tt, 2 * CONV_SUB)
    nt = t // tt
    per = tt // CONV_HALO
    halo = CONV_HALO

    def body(ins, outs, scratch, first, last):
        u_ref, up_ref, un_ref, dz_ref, dzn_ref, w_ref, b_ref = ins
        du_ref, dw_ref, db_ref = outs
        dwacc, dbacc = scratch
        i = pl.program_id(0)
        first()

        @pl.when(i == 0)
        def _():
            dwacc[...] = jnp.zeros_like(dwacc)
            dbacc[...] = jnp.zeros_like(dbacc)

        main = slice(halo, halo + sub)

        def chunk(j, carry):
            lanes = pl.ds(pl.multiple_of(j * LANES, LANES), LANES)
            w, b = w_ref[:, :, lanes], b_ref[:, lanes]
            for s in range(tt // sub):
                ue = [_piece_rows(lambda at, n: u_ref[hf, pl.ds(at, n), lanes].astype(F32), s, sub, tt,
                                  top=lambda: jnp.where(i > 0, up_ref[hf, :, lanes].astype(F32), 0.0),
                                  bottom=lambda: un_ref[hf, :, lanes].astype(F32)) for hf in range(2)]
                a, g = [_causal_conv(ue[hf], w[:, hf, :], b[hf:hf + 1, :]) for hf in range(2)]
                sg = _sigmoid(g)
                if (s + 1) * sub < tt:
                    below = dz_ref[pl.ds(s * sub, sub + 2 * halo), lanes].astype(F32)[:sub + halo]
                else:
                    below = jnp.concatenate([dz_ref[pl.ds(s * sub, sub), lanes].astype(F32),
                                             jnp.where(i < nt - 1, dzn_ref[:, lanes].astype(F32), 0.0)], axis=0)
                dzs = jnp.concatenate([jnp.zeros((halo, LANES), F32), below], axis=0)
                gs = g * sg
                dc = [dzs * gs, dzs * a * (sg + gs - gs * sg)]
                for hf in range(2):
                    d = dc[hf]
                    d1, d2 = _shift_up(d, 1), _shift_up(d, 2)
                    du = w[2:3, hf, :] * d + w[1:2, hf, :] * d1 + w[0:1, hf, :] * d2
                    du_ref[hf, pl.ds(s * sub, sub), lanes] = du[main].astype(du_ref.dtype)
                    um = ue[hf][main]
                    for k, dd in enumerate((d2, d1, d)):
                        dwacc[k, hf, :, lanes] += _fold8(dd[main] * um)
                    dbacc[hf, :, lanes] += _fold8(d[main])
            return carry

        lax.fori_loop(0, f // LANES, chunk, 0)

        @pl.when(i == nt - 1)
        def _():
            dw_ref[...] = jnp.sum(dwacc[...], axis=2)
            db_ref[...] = jnp.sum(dbacc[...], axis=1)

        last()

    end = t // halo - 1
    return _call(
        body, hosted, name=name, grid=(nt,),
        in_specs=[pl.BlockSpec((2, tt, f), lambda i: (0, i, 0)),
                  pl.BlockSpec((2, halo, f), lambda i: (0, jnp.maximum(i * per - 1, 0), 0)),
                  pl.BlockSpec((2, halo, f), lambda i: (0, jnp.minimum((i + 1) * per, end), 0)),
                  pl.BlockSpec((tt, f), lambda i: (i, 0)),
                  pl.BlockSpec((halo, f), lambda i: (jnp.minimum((i + 1) * per, end), 0)),
                  pl.BlockSpec((3, 2, f), lambda i: (0, 0, 0)), pl.BlockSpec((2, f), lambda i: (0, 0))],
        out_specs=[pl.BlockSpec((2, tt, f), lambda i: (0, i, 0)),
                   pl.BlockSpec((3, 2, f), lambda i: (0, 0, 0)), pl.BlockSpec((2, f), lambda i: (0, 0))],
        out_shape=[jax.ShapeDtypeStruct(u.shape, MXU_DTYPE), jax.ShapeDtypeStruct((3, 2, f), F32),
                   jax.ShapeDtypeStruct((2, f), F32)],
        operands=(u, u, u, dz, dz, conv_w, conv_b), semantics=("arbitrary",),
        scratch_shapes=[pltpu.VMEM((3, 2, 8, f), F32), pltpu.VMEM((2, 8, f), F32)])


def _kv_even_odd(band, k):
    pair, parity = divmod(k, 2)
    blk = band[:, 128 * pair:128 * (pair + 1)].astype(F32)
    lane = lax.broadcasted_iota(jnp.int32, blk.shape, 1)
    own = jnp.where((lane >= HEAD_DIM) == (parity == 1), blk, 0.0)
    swapped = pltpu.roll(own, HEAD_DIM, axis=1)
    even, odd = (own, swapped) if parity == 0 else (swapped, own)
    return even.astype(MXU_DTYPE), odd.astype(MXU_DTYPE)


def _stack_pairs(ref, k):
    return jnp.concatenate([ref[:, 256 * k:256 * k + 128], ref[:, 256 * k + 128:256 * k + 256]], axis=0)


_GROUP_ORDER = (0, 2, 1, 3)


def _attn_fill_tables(bias_ref):
    shape = (4 * ATT_BLOCK, 2 * ATT_BLOCK)
    row = lax.broadcasted_iota(jnp.int32, shape, 0)
    col = lax.broadcasted_iota(jnp.int32, shape, 1)
    dist = (row & (ATT_BLOCK - 1)) + ATT_BLOCK - col
    ok = (dist >= 0) & (dist < ATT_BLOCK)
    distf = dist.astype(F32)
    rb = lax.broadcasted_iota(jnp.int32, (shape[0], 1), 0) // ATT_BLOCK
    for k in range(N_KV_HEADS):
        slope = jnp.zeros((shape[0], 1), F32)
        for r, gq in enumerate(_GROUP_ORDER):
            slope = jnp.where(rb == r, ALIBI_SLOPES[4 * k + gq], slope)
        bias = jnp.where(ok, -slope * distf, -jnp.inf)
        bias_ref[0, k] = bias
        bias_ref[1, k] = jnp.where(col >= ATT_BLOCK, bias, -jnp.inf)


def _attn_probs(qs, k_even, k_odd, bias, k, sink_ref):
    nt_dims = (((1,), (1,)), ((), ()))
    s = jnp.concatenate([lax.dot_general(qs, k_even, nt_dims, preferred_element_type=F32),
                         lax.dot_general(qs, k_odd, nt_dims, preferred_element_type=F32)], axis=0) + bias
    rb = lax.broadcasted_iota(jnp.int32, (s.shape[0], 1), 0) // ATT_BLOCK
    sink = jnp.zeros((s.shape[0], 1), F32)
    for r, gq in enumerate(_GROUP_ORDER):
        sink = jnp.where(rb == r, sink_ref[4 * k + gq], sink)
    m = jnp.maximum(jnp.max(s, axis=-1, keepdims=True), sink)
    p = jnp.exp(s - m)
    es = jnp.exp(sink - m)
    return p, 1.0 / (jnp.sum(p, axis=-1, keepdims=True) + es), es


_ATTN_TABLES = [pltpu.VMEM((2, N_KV_HEADS, 4 * ATT_BLOCK, 2 * ATT_BLOCK), F32)]


def _attn_specs(with_do):
    prev = lambda n: jnp.maximum(n - 1, 0)
    specs = [pl.BlockSpec(memory_space=pltpu.SMEM),
             pl.BlockSpec((ATT_BLOCK, D_MODEL), lambda n: (n, 0)),
             pl.BlockSpec((ATT_BLOCK, 256), lambda n: (prev(n), 4)), pl.BlockSpec((ATT_BLOCK, 256), lambda n: (n, 4)),
             pl.BlockSpec((ATT_BLOCK, 256), lambda n: (prev(n), 5)), pl.BlockSpec((ATT_BLOCK, 256), lambda n: (n, 5))]
    if with_do:
        specs.append(pl.BlockSpec((ATT_BLOCK, D_MODEL), lambda n: (n, 0)))
    return specs


def _attn_fwd_call(qkv, sinks, hosted=None):
    t = qkv.shape[0]

    def body(ins, outs, scratch, first, last):
        sink_ref, q_ref, kp_ref, kc_ref, vp_ref, vc_ref = ins
        o_ref, bias_ref = outs[0], scratch[0]
        n = pl.program_id(0)
        first()

        @pl.when(n == 0)
        def _():
            _attn_fill_tables(bias_ref)

        first = 1 - jnp.minimum(n, 1)
        kband = jnp.concatenate([kp_ref[...], kc_ref[...]], axis=0)
        vband = jnp.concatenate([vp_ref[...], vc_ref[...]], axis=0)
        for k in range(N_KV_HEADS):
            k_even, k_odd = _kv_even_odd(kband, k)
            v_even, v_odd = _kv_even_odd(vband, k)
            qs = _stack_pairs(q_ref, k) * ATT_SCALE
            p, rdenom, _ = _attn_probs(qs, k_even, k_odd, bias_ref[first, k], k, sink_ref)
            probs = (p * rdenom).astype(MXU_DTYPE)
            o = (jnp.dot(probs[:256], v_even, preferred_element_type=F32)
                 + jnp.dot(probs[256:], v_odd, preferred_element_type=F32))
            o_ref[:, 256 * k:256 * k + 128] = o[:128].astype(o_ref.dtype)
            o_ref[:, 256 * k + 128:256 * k + 256] = o[128:].astype(o_ref.dtype)
        last()

    return _call(
        body, hosted, name="attn_fwd", grid=(t // ATT_BLOCK,), in_specs=_attn_specs(False),
        out_specs=[pl.BlockSpec((ATT_BLOCK, D_MODEL), lambda n: (n, 0))],
        out_shape=[jax.ShapeDtypeStruct((t, D_MODEL), MXU_DTYPE)], operands=(sinks, qkv, qkv, qkv, qkv, qkv),
        semantics=("arbitrary",), scratch_shapes=_ATTN_TABLES)


def _attn_bwd_call(qkv, sinks, do, hosted=None):
    t = qkv.shape[0]
    nb = t // ATT_BLOCK
    tn_dims = (((0,), (0,)), ((), ()))
    nt_dims = (((1,), (1,)), ((), ()))

    def to_native(even_part, odd_part, parity):
        lane = lax.broadcasted_iota(jnp.int32, even_part.shape, 1)
        lo = lane < HEAD_DIM
        e, o = jnp.where(lo, even_part, 0.0), jnp.where(lo, 0.0, odd_part)
        if parity == 0:
            return e + pltpu.roll(o, HEAD_DIM, axis=1)
        return pltpu.roll(e, HEAD_DIM, axis=1) + o

    def body(ins, outs, scratch, first, last):
        sink_ref, q_ref, kp_ref, kc_ref, vp_ref, vc_ref, do_ref = ins
        dq_ref, dk_ref, dv_ref, db_ref, dsink_ref = outs
        bias_ref = scratch[0]
        n = pl.program_id(0)
        first()

        @pl.when(n == 0)
        def _():
            _attn_fill_tables(bias_ref)
            dk_ref[...] = jnp.zeros_like(dk_ref)
            dv_ref[...] = jnp.zeros_like(dv_ref)
            db_ref[...] = jnp.zeros_like(db_ref)
            dsink_ref[...] = jnp.zeros_like(dsink_ref)

        kband = jnp.concatenate([kp_ref[...], kc_ref[...]], axis=0)
        vband = jnp.concatenate([vp_ref[...], vc_ref[...]], axis=0)
        dk_pairs = [jnp.zeros((2 * ATT_BLOCK, 128), F32), jnp.zeros((2 * ATT_BLOCK, 128), F32)]
        dv_pairs = [jnp.zeros((2 * ATT_BLOCK, 128), F32), jnp.zeros((2 * ATT_BLOCK, 128), F32)]
        sink_lane = lax.broadcasted_iota(jnp.int32, (8, 128), 1)
        sink_row = lax.broadcasted_iota(jnp.int32, (8, 128), 0)
        dsink = jnp.zeros((8, 128), F32)
        first = 1 - jnp.minimum(n, 1)
        for k in range(N_KV_HEADS):
            k_even, k_odd = _kv_even_odd(kband, k)
            v_even, v_odd = _kv_even_odd(vband, k)
            qs = _stack_pairs(q_ref, k) * ATT_SCALE
            dos = _stack_pairs(do_ref, k)
            p, rdenom, es = _attn_probs(qs, k_even, k_odd, bias_ref[first, k], k, sink_ref)
            probs = p * rdenom
            dp = jnp.concatenate([lax.dot_general(dos, v_even, nt_dims, preferred_element_type=F32),
                                  lax.dot_general(dos, v_odd, nt_dims, preferred_element_type=F32)], axis=0)
            delta = jnp.sum(probs * dp, axis=-1, keepdims=True)
            ds16 = (probs * (dp - delta)).astype(MXU_DTYPE)
            p16 = probs.astype(MXU_DTYPE)
            dsink_rows = -(es * rdenom) * delta
            for r, gq in enumerate(_GROUP_ORDER):
                tot = jnp.sum(dsink_rows[ATT_BLOCK * r:ATT_BLOCK * (r + 1)], axis=0, keepdims=True)
                dsink = dsink + jnp.where((sink_lane == 4 * k + gq) & (sink_row == 0), tot, 0.0)
            dqs = (jnp.dot(ds16[:256], k_even, preferred_element_type=F32)
                   + jnp.dot(ds16[256:], k_odd, preferred_element_type=F32)) * ATT_SCALE
            dq_ref[:, 256 * k:256 * k + 128] = dqs[:128].astype(dq_ref.dtype)
            dq_ref[:, 256 * k + 128:256 * k + 256] = dqs[128:].astype(dq_ref.dtype)
            db_ref[:, 256 * k:256 * k + 128] += jnp.sum(dqs[:128], axis=0, keepdims=True)
            db_ref[:, 256 * k + 128:256 * k + 256] += jnp.sum(dqs[128:], axis=0, keepdims=True)
            pair, parity = divmod(k, 2)
            dk_pairs[pair] = dk_pairs[pair] + to_native(
                lax.dot_general(ds16[:256], qs, tn_dims, preferred_element_type=F32),
                lax.dot_general(ds16[256:], qs, tn_dims, preferred_element_type=F32), parity)
            dv_pairs[pair] = dv_pairs[pair] + to_native(
                lax.dot_general(p16[:256], dos, tn_dims, preferred_element_type=F32),
                lax.dot_general(p16[256:], dos, tn_dims, preferred_element_type=F32), parity)
        dsink_ref[...] += dsink
        dk_band = jnp.concatenate(dk_pairs, axis=1)
        dv_band = jnp.concatenate(dv_pairs, axis=1)
        cur = pl.multiple_of(n * ATT_BLOCK, ATT_BLOCK)
        dk_ref[pl.ds(cur, ATT_BLOCK), :] += dk_band[ATT_BLOCK:]
        dv_ref[pl.ds(cur, ATT_BLOCK), :] += dv_band[ATT_BLOCK:]

        @pl.when(n > 0)
        def _():
            prv = pl.multiple_of((n - 1) * ATT_BLOCK, ATT_BLOCK)
            dk_ref[pl.ds(prv, ATT_BLOCK), :] += dk_band[:ATT_BLOCK]
            dv_ref[pl.ds(prv, ATT_BLOCK), :] += dv_band[:ATT_BLOCK]

        @pl.when(n == nb - 1)
        def _():
            db_ref[:, 1024:1280] = jnp.sum(dk_ref[...], axis=0, keepdims=True)
            db_ref[:, 1280:1536] = jnp.sum(dv_ref[...], axis=0, keepdims=True)

        last()

    whole = lambda shape: pl.BlockSpec(shape, lambda n: (0, 0))
    return _call(
        body, hosted, name="attn_bwd", grid=(nb,), in_specs=_attn_specs(True),
        out_specs=[pl.BlockSpec((ATT_BLOCK, D_MODEL), lambda n: (n, 0)), whole((t, 256)), whole((t, 256)),
                   whole((1, QKV_DIM)), whole((8, 128))],
        out_shape=[jax.ShapeDtypeStruct((t, D_MODEL), MXU_DTYPE), jax.ShapeDtypeStruct((t, 256), F32),
                   jax.ShapeDtypeStruct((t, 256), F32), jax.ShapeDtypeStruct((1, QKV_DIM), F32),
                   jax.ShapeDtypeStruct((8, 128), F32)],
        operands=(sinks, qkv, qkv, qkv, qkv, qkv, do), semantics=("arbitrary",), scratch_shapes=_ATTN_TABLES)


FF_TILE = 1408


def _ffn_out_loss_call(z, w_out, layer, h, g, target):
    t, f = z.shape
    tm = _tile(t, ROW_TILE)

    def body(z_ref, w_ref, h_ref, g_ref, t_ref, loss_ref, dh_ref, dh16_ref, dg_ref):
        x = h_ref[...] + jnp.dot(z_ref[...], w_ref[...], preferred_element_type=F32)
        gv = g_ref[...]
        err = _rms(x, gv) - t_ref[...]
        dx, dgt = _rms_bwd(x, gv, err * (1.0 / D_MODEL))
        dh_ref[...] = dx
        dh16_ref[...] = dx.astype(dh16_ref.dtype)

        @pl.when(pl.program_id(0) == 0)
        def _():
            dg_ref[...] = jnp.zeros_like(dg_ref)
            loss_ref[...] = jnp.zeros_like(loss_ref)

        dg_ref[...] += jnp.sum(dgt, axis=0, keepdims=True)
        per_token = jnp.mean(err * err, axis=-1, keepdims=True)
        loss_ref[...] += 0.5 * jnp.sum(per_token, axis=0, keepdims=True)

    row = pl.BlockSpec((tm, D_MODEL), lambda i: (i, 0))
    vec = pl.BlockSpec((1, D_MODEL), lambda i: (0, 0))
    return pl.pallas_call(
        body, name=f"ffn{layer}_out_loss", grid=(t // tm,),
        in_specs=[pl.BlockSpec((tm, f), lambda i: (i, 0)), pl.BlockSpec((None, f, D_MODEL), lambda i: (layer, 0, 0)),
                  row, vec, row],
        out_specs=[pl.BlockSpec((1, 1), lambda i: (0, 0)), row, row, vec],
        out_shape=[jax.ShapeDtypeStruct((1, 1), F32), jax.ShapeDtypeStruct((t, D_MODEL), F32),
                   jax.ShapeDtypeStruct((t, D_MODEL), MXU_DTYPE), jax.ShapeDtypeStruct((1, D_MODEL), F32)],
        compiler_params=_params("arbitrary"))(z, w_out, h, g, target)


def _ffn_fwd(hn, h, w, conv_w, conv_b, layer, hooks=None, next_norm=None, loss_head=None):
    t = hn.shape[0]
    tm = _tile(t, 1024)
    halves = D_FF // FF_TILE
    u, = _ride(hooks, "in", lambda hosted: _matmul(
        hn, w["w_in"], mode="nn", name=f"ffn{layer}_in", grid=(2 * halves, t // tm, 1),
        a_spec=pl.BlockSpec((tm, D_MODEL), lambda j, i, k: (i, 0)),
        b_spec=pl.BlockSpec((None, D_MODEL, FF_TILE), lambda j, i, k: (layer, 0, j)),
        o_spec=pl.BlockSpec((None, tm, FF_TILE), lambda j, i, k: (j // halves, i, j % halves)),
        out_shape=jax.ShapeDtypeStruct((2, t, D_FF), U_DTYPE), hosted=hosted))
    z, = _ride(hooks, "glu", lambda hosted: _conv_glu_fwd_call(u, conv_w, conv_b, f"ffn{layer}_glu", hosted=hosted))
    if loss_head is not None:
        return (*_ffn_out_loss_call(z, w["w_out"], layer, h, *loss_head), u, z)
    outs = _ride(hooks, "out", lambda hosted: _mm_nn(z, w["w_out"], f"ffn{layer}_out", b_lead=layer, residual=h,
                                                     tk=D_FF, hosted=hosted, norm_g=next_norm))
    return outs[0], (outs[1] if next_norm is not None else None), u, z


def _ffn_bwd(dh, hn, u, z, w_in, w_out, conv_w, conv_b, layer, g_in_acc, g_out_acc, norm, hooks=None):
    t = hn.shape[0]
    dz, = _ride(hooks, "dz", lambda hosted: _mm_nt(dh, w_out, f"ffn{layer}_dz", b_lead=layer, out_dtype=MXU_DTYPE,
                                                   tn=FF_TILE, hosted=hosted))
    g_out = _mm_tn(z, dh, f"ffn{layer}_dwout", lead=layer, n_lead=2, into=g_out_acc, tm=WGRAD_TILE, tk=t,
                   out_dtype=WIRE_DTYPE)
    du, dcw, dcb = _ride(hooks, "dglu", lambda hosted: _conv_glu_bwd_call(u, dz, conv_w, conv_b, f"ffn{layer}_dglu",
                                                                          hosted=hosted), {"g_out": g_out})
    per_half = D_FF // WGRAD_TILE
    g_in, = _ride(hooks, "dwin", lambda hosted: _matmul(
        hn, du, mode="tn", name=f"ffn{layer}_dwin", grid=(1, 2 * per_half, 1),
        a_spec=pl.BlockSpec((t, D_MODEL), lambda i, j, k: (0, 0)),
        b_spec=pl.BlockSpec((None, t, WGRAD_TILE), lambda i, j, k: (j // per_half, 0, j % per_half)),
        o_spec=pl.BlockSpec((None, D_MODEL, WGRAD_TILE), lambda i, j, k: (layer, 0, j)),
        out_shape=jax.ShapeDtypeStruct((2, D_MODEL, 2 * D_FF), WIRE_DTYPE), into=g_in_acc, hosted=hosted))
    dx, dx16, dg = _ride(hooks, "dhn", lambda hosted: _ffn_dhn_call(du, w_in, layer, norm, hosted),
                         {"g_in": g_in, "g_out": g_out})
    return dx, dx16, dg, g_in, g_out, dcw, dcb


WGRAD_TILE = 256


_NT_DIMS = (((1,), (1,)), ((), ()))


def _dhn_call(name, a, a_spec, w, w_spec, product, t, tm, norm, hosted=None):
    h, g, dres = norm

    def body(ins, outs, scratch, first, last):
        a_ref, w_ref, h_ref, g_ref, dres_ref = ins
        dx_ref, dx16_ref, dg_ref = outs
        first()
        dx, dgt = _rms_bwd(h_ref[...], g_ref[...], product(a_ref, w_ref))
        dx = dres_ref[...] + dx
        dx_ref[...] = dx
        dx16_ref[...] = dx.astype(dx16_ref.dtype)

        @pl.when(pl.program_id(0) == 0)
        def _():
            dg_ref[...] = jnp.zeros_like(dg_ref)

        dg_ref[...] += jnp.sum(dgt, axis=0, keepdims=True)
        last()

    row = pl.BlockSpec((tm, D_MODEL), lambda i: (i, 0))
    vec = pl.BlockSpec((1, D_MODEL), lambda i: (0, 0))
    return _call(
        body, hosted, name=name, grid=(t // tm,), in_specs=[a_spec, w_spec, row, vec, row], out_specs=[row, row, vec],
        out_shape=[jax.ShapeDtypeStruct((t, D_MODEL), F32), jax.ShapeDtypeStruct((t, D_MODEL), MXU_DTYPE),
                   jax.ShapeDtypeStruct((1, D_MODEL), F32)],
        operands=(a, w, h, g, dres), semantics=("arbitrary",))


def _ffn_dhn_call(du, w_in, layer, norm, hosted=None):
    _, t, f = du.shape
    tm = _tile(t, 512)

    def product(du_ref, w_ref):
        return (lax.dot_general(du_ref[0], w_ref[:, :f], _NT_DIMS, preferred_element_type=F32)
                + lax.dot_general(du_ref[1], w_ref[:, f:], _NT_DIMS, preferred_element_type=F32))

    return _dhn_call(f"ffn{layer}_dhn", du, pl.BlockSpec((2, tm, f), lambda i: (0, i, 0)), w_in,
                     pl.BlockSpec((None, D_MODEL, 2 * f), lambda i: (layer, 0, 0)), product, t, tm, norm, hosted)


def _attn_dhn_call(dqkv, w_qkv, norm):
    t, n = dqkv.shape
    tm = _tile(t, 1024)

    def product(a_ref, w_ref):
        return lax.dot_general(a_ref[...], w_ref[...], _NT_DIMS, preferred_element_type=F32)

    return _dhn_call("attn_dhn", dqkv, pl.BlockSpec((tm, n), lambda i: (i, 0)), w_qkv,
                     pl.BlockSpec((D_MODEL, n), lambda i: (0, 0)), product, t, tm, norm)


def _local_step(x, target, w, *, norm_mix, norm_ffn, norm_f, pool_b, pool_scale, b_qkv, sinks, conv_w, conv_b,
                late_gather=None, reducer=None):
    cw = [conv_w[l].reshape(3, 2, D_FF) for l in range(2)]
    cb = [conv_b[l].reshape(2, D_FF) for l in range(2)]
    gm = [norm_mix[l:l + 1] for l in range(2)]
    gf = [norm_ffn[l:l + 1] for l in range(2)]

    w = dict(w)
    fwd_hooks = {}
    if late_gather:
        def gather_hook(stage):
            return (lambda env: late_gather(stage, w)), (lambda got: w.update(zip(_late_names(stage), got)))

        fwd_hooks = {stage: gather_hook(stage) for stage in GATHER_RIDES}

    h1, hn_f0 = _pool_fwd_call(x, gm[0], w["pool_w"], pool_b, pool_scale, gf[0])
    h2, hn_a, u0, z0 = _ffn_fwd(hn_f0, h1, w, cw[0], cb[0], 0, hooks=fwd_hooks, next_norm=gm[1])
    qkv = _mm_nn(hn_a, w["w_qkv"], "attn_qkv", bias=b_qkv, out_dtype=MXU_DTYPE, tn=768)
    o, = _ride(fwd_hooks, "attn_fwd", lambda hosted: _attn_fwd_call(qkv, sinks, hosted))
    h3, hn_f1 = _ride(fwd_hooks, "attn_out", lambda hosted: _mm_nn(o, w["w_o"], "attn_out", residual=h2, hosted=hosted,
                                                                   norm_g=gf[1]))
    loss, dh, dh16, d_norm_f, u1, z1 = _ffn_fwd(hn_f1, h3, w, cw[1], cb[1], 1, loss_head=(norm_f, target))

    rd = reducer
    first, mid, last = _Reducer.FIRST, _Reducer.MID, _Reducer.LAST
    hooks1 = hooks_attn = hooks0 = hooks_pool = None
    if rd:
        hooks1 = {"dhn": (lambda env: rd.swap(first, {"w_in": env["g_in"], "w_out": env["g_out"]}),
                          lambda got: rd.pair_sums(first, got))}
        hooks_attn = {"attn_bwd": (lambda env: rd.scatter(first), lambda got: rd.chip_sums(first, got))}

        def mid_and_last(got):
            rd.shared(mid, got[:len(mid)])
            rd.pair_sums(last, got[len(mid):])

        hooks0 = {
            "dz": (lambda env: rd.share(first), lambda got: rd.shared(first, got)),
            "dglu": (lambda env: rd.swap(mid, {"w_out": env["g_out"], "w_qkv": g_qkv, "w_o": g_o4}),
                     lambda got: rd.pair_sums(mid, got)),
            "dwin": (lambda env: rd.scatter(mid), lambda got: rd.chip_sums(mid, got)),
            "dhn": (lambda env: _both(rd.share(mid), rd.swap(last, {"w_in": env["g_in"]})), mid_and_last),
        }
        hooks_pool = {"pool_bwd": (lambda env: rd.scatter(last), lambda got: rd.chip_sums(last, got))}
    dh, dh16, d_gf1, g_in, g_out, dcw1, dcb1 = _ffn_bwd(
        dh16, hn_f1, u1, z1, w["w_in"], w["w_out"], cw[1], cb[1], 1, None, None, (h3, gf[1], dh), hooks=hooks1)
    do = _mm_nt(dh16, w["w_o"], "attn_do", out_dtype=MXU_DTYPE)
    g_o = _mm_tn(o, dh16, "attn_dwo", tn=WGRAD_TILE, tk=x.shape[0], out_dtype=WIRE_DTYPE)
    g_o4 = g_o.reshape(N_CHIPS, 2, W_O_ROWS // 2, D_MODEL)
    dq, dk, dv, d_bqkv, d_sinks = _ride(hooks_attn, "attn_bwd", lambda hosted: _attn_bwd_call(qkv, sinks, do, hosted))
    dqkv = jnp.concatenate([dq, dk.astype(MXU_DTYPE), dv.astype(MXU_DTYPE)], axis=1)
    g_qkv = _mm_tn(hn_a, dqkv, "attn_dwqkv", tn=WGRAD_TILE, tk=x.shape[0], out_dtype=WIRE_DTYPE)
    dh, dh16, d_gm1 = _attn_dhn_call(dqkv, w["w_qkv"], (h2, gm[1], dh))
    dh, _, d_gf0, g_in, g_out, dcw0, dcb0 = _ffn_bwd(
        dh16, hn_f0, u0, z0, w["w_in"], w["w_out"], cw[0], cb[0], 0, g_in, g_out, (h1, gf[0], dh), hooks=hooks0)
    grad_x, g_pool, dvec = _ride(hooks_pool, "pool_bwd", lambda hosted: _pool_bwd_call(
        x, dh, gm[0], w["pool_w"], pool_b, pool_scale, hosted))
    if rd:
        rd.shared(last, _alone(rd.share(last), "share_last"))

    grads = {
        "w_in": g_in, "w_out": g_out, "w_qkv": g_qkv, "w_o": g_o, "pool_w": g_pool,
        "pool_b": dvec[0:1], "pool_scale": dvec[1:2], "b_qkv": d_bqkv, "sinks": d_sinks[0:1, :N_HEADS],
        "norm_mix": jnp.concatenate([dvec[2:3], d_gm1], axis=0), "norm_ffn": jnp.concatenate([d_gf0, d_gf1], axis=0),
        "conv_w": jnp.stack([dcw0.reshape(3, 2 * D_FF), dcw1.reshape(3, 2 * D_FF)]),
        "conv_b": jnp.stack([dcb0.reshape(2 * D_FF), dcb1.reshape(2 * D_FF)]), "norm_f": d_norm_f,
    }
    return loss, grad_x, grads


W_IN_COLS = 2 * D_FF // N_CHIPS
W_OUT_ROWS = D_FF // N_CHIPS
QKV_COLS = QKV_DIM // N_CHIPS
W_O_ROWS = D_MODEL // N_CHIPS
POOL_ROWS = POOL_GC // N_CHIPS
BIG = ("w_in", "w_out", "w_qkv", "w_o", "pool_w")


def _mesh_place():
    x, y, c = lax.axis_index("x"), lax.axis_index("y"), lax.axis_index("c")
    chips = [(1 - x, y), (x, 1 - y), (1 - x, 1 - y)]
    return x, y, c, chips


def _col(k, width):
    return pl.ds(pl.multiple_of(k * width, 128), width)


def _row(k, height):
    return pl.ds(pl.multiple_of(k * height, 16), height)


def _full_piece(name, ref, k, h):
    if name == "w_in":
        return ref.at[h, :, _col(k, W_IN_COLS)]
    if name == "w_out":
        return ref.at[h, _row(k, W_OUT_ROWS), :]
    if name == "w_qkv":
        return ref.at[_row(h, D_MODEL // 2), _col(k, QKV_COLS)]
    if name == "w_o":
        return ref.at[_row(2 * k + h, W_O_ROWS // 2), :]
    return ref.at[pl.ds(2 * h, 2), _row(k, POOL_ROWS), :]


def _shard_half(name, ref, h):
    if name in ("w_in", "w_out"):
        return ref.at[h]
    if name == "w_qkv":
        return ref.at[_row(h, D_MODEL // 2), :]
    if name == "w_o":
        return ref.at[_row(h, W_O_ROWS // 2), :]
    return ref.at[pl.ds(2 * h, 2)]


FULL_SHAPES = {"w_in": (2, D_MODEL, 2 * D_FF), "w_out": (2, D_FF, D_MODEL), "w_qkv": (D_MODEL, QKV_DIM),
               "w_o": (D_MODEL, D_MODEL), "pool_w": (4, POOL_GC, POOL_GC)}
SHARD_SHAPES = {"w_in": (2, D_MODEL, W_IN_COLS), "w_out": (2, W_OUT_ROWS, D_MODEL), "w_qkv": (D_MODEL, QKV_COLS),
                "w_o": (W_O_ROWS, D_MODEL), "pool_w": (4, POOL_ROWS, POOL_GC)}
HALF_SHAPES = {"w_in": (D_MODEL, W_IN_COLS), "w_out": (W_OUT_ROWS, D_MODEL), "w_qkv": (D_MODEL // 2, QKV_COLS),
               "w_o": (W_O_ROWS // 2, D_MODEL), "pool_w": (2, POOL_ROWS, POOL_GC)}


def _place_shard(name, shard, me_arr):
    if name == "w_in":
        blk, grid = (1, 256, W_IN_COLS), (2, D_MODEL // 256)
        src, dst = (lambda a, b, me: (a, b, 0)), (lambda a, b, me: (a, b, me[0]))
    elif name == "w_out":
        blk, grid = (1, W_OUT_ROWS, D_MODEL), (2, 1)
        src, dst = (lambda a, b, me: (a, 0, 0)), (lambda a, b, me: (a, me[0], 0))
    elif name == "w_qkv":
        blk, grid = (256, QKV_COLS), (1, D_MODEL // 256)
        src, dst = (lambda a, b, me: (b, 0)), (lambda a, b, me: (b, me[0]))
    elif name == "w_o":
        blk, grid = (W_O_ROWS, D_MODEL), (1, 1)
        src, dst = (lambda a, b, me: (0, 0)), (lambda a, b, me: (me[0], 0))
    else:
        blk, grid = (4, POOL_ROWS, POOL_GC), (1, 1)
        src, dst = (lambda a, b, me: (0, 0, 0)), (lambda a, b, me: (0, me[0], 0))

    def body(me_ref, s_ref, o_ref):
        o_ref[...] = s_ref[...].astype(o_ref.dtype)

    return pl.pallas_call(
        body, name=f"place_{name}",
        grid_spec=pltpu.PrefetchScalarGridSpec(num_scalar_prefetch=1, grid=grid, in_specs=[pl.BlockSpec(blk, src)],
                                               out_specs=pl.BlockSpec(blk, dst)),
        out_shape=jax.ShapeDtypeStruct(FULL_SHAPES[name], WIRE_DTYPE), compiler_params=_params("parallel", "parallel"),
    )(me_arr, shard)


def _when_mover(half, c, fn):
    if half is None:
        fn(c)
    else:
        pl.when(c == half)(lambda: fn(half))


def _when_taker(half, c, fn):
    if half is None:
        fn(1 - c)
    else:
        pl.when(c != half)(lambda: fn(half))


def _remote(ref, sems, t, j, dev):
    send, recv = sems
    return pltpu.make_async_remote_copy(src_ref=ref, dst_ref=ref, send_sem=send.at[t, j], recv_sem=recv.at[t, j],
                                        device_id=dev, device_id_type=MESH)


def _gather_ici(items, refs, sems, finish):
    x, y, c, chips = _mesh_place()
    me = 2 * x + y
    for t, (nm, half) in enumerate(items):
        def go(h, t=t, nm=nm):
            mine = _full_piece(nm, refs[t], me, h)
            for j, (cx, cy) in enumerate(chips):
                if not finish:
                    _remote(mine, sems, t, j, (cx, cy, c)).start()
                else:
                    _remote(_full_piece(nm, refs[t], 2 * cx + cy, h), sems, t, j, (cx, cy, c)).wait_recv()
                    _remote(mine, sems, t, j, (cx, cy, c)).wait_send()

        _when_mover(half, c, go)


def _gather_pass(items, refs, sems, finish):
    x, y, c, chips = _mesh_place()
    sibling = (x, y, 1 - c)
    for t, (nm, half) in enumerate(items):
        def give(h, t=t, nm=nm):
            for j, (cx, cy) in enumerate(chips):
                got = _remote(_full_piece(nm, refs[t], 2 * cx + cy, h), sems, t, j, sibling)
                if not finish:
                    got.start()
                else:
                    got.wait_send()

        def take(h, t=t, nm=nm):
            for j, (cx, cy) in enumerate(chips):
                _remote(_full_piece(nm, refs[t], 2 * cx + cy, h), sems, t, j, sibling).wait_recv()

        _when_mover(half, c, give)
        if finish:
            _when_taker(half, c, take)


def _gather_sems(n_items):
    return [pltpu.SemaphoreType.DMA((n_items, 3)), pltpu.SemaphoreType.DMA((n_items, 3))]


EARLY_ITEMS = (("pool_w", None), ("w_in", 0))
_LATE_A = (("w_out", 0), ("w_qkv", None), ("w_o", None))
_LATE_B = (("w_in", 1),)
_LATE_C = (("w_out", 1),)
GATHER_RIDES = {"in": ((), _LATE_A), "glu": (_LATE_A, _LATE_B), "out": (_LATE_B, ()), "attn_fwd": ((), _LATE_C),
                "attn_out": (_LATE_C, ())}


def _names(items):
    return [nm for nm, _ in items]


def _gather_early(full, small):
    items = EARLY_ITEMS
    n = len(items)

    def body(*refs):
        small_in = refs[n]
        outs, small_out = refs[n + 1:2 * n + 1], refs[2 * n + 1]
        scratch = refs[2 * n + 2:]
        ici_sems, pass_sems = scratch[0:2], scratch[2:4]
        local_sem, small_send, small_recv = scratch[4:]
        x, y, c, chips = _mesh_place()
        me = 2 * x + y
        local = pltpu.make_async_copy(small_in, small_out.at[me], local_sem.at[0])
        local.start()
        small_sends = [pltpu.make_async_remote_copy(
            src_ref=small_in, dst_ref=small_out.at[me], send_sem=small_send.at[j], recv_sem=small_recv.at[j],
            device_id=(cx, cy, c), device_id_type=MESH) for j, (cx, cy) in enumerate(chips)]
        for cp in small_sends:
            cp.start()
        _gather_ici(items, outs, ici_sems, finish=False)
        _gather_ici(items, outs, ici_sems, finish=True)
        _gather_pass(items, outs, pass_sems, finish=False)
        _gather_pass(items, outs, pass_sems, finish=True)
        for j, (cx, cy) in enumerate(chips):
            got = small_out.at[2 * cx + cy]
            pltpu.make_async_remote_copy(src_ref=got, dst_ref=got, send_sem=small_send.at[j],
                                         recv_sem=small_recv.at[j], device_id=(cx, cy, c), device_id_type=MESH).wait_recv()
        for cp in small_sends:
            cp.wait_send()
        local.wait()

    arrays = [full[nm] for nm in _names(items)]
    out_shape = [jax.ShapeDtypeStruct(a.shape, a.dtype) for a in arrays]
    out_shape.append(jax.ShapeDtypeStruct((N_CHIPS,) + small.shape, F32))
    res = pl.pallas_call(
        body, name="gather_early", in_specs=[ANY] * (n + 1), out_specs=[ANY] * (n + 1), out_shape=out_shape,
        input_output_aliases={t: t for t in range(n)},
        scratch_shapes=_gather_sems(n) + _gather_sems(n) + [pltpu.SemaphoreType.DMA((1,)), pltpu.SemaphoreType.DMA((3,)),
                                                            pltpu.SemaphoreType.DMA((3,))],
    )(*arrays, small)
    return {**full, **dict(zip(_names(items), res[:n]))}, res[n]


def _late_gather_host(stage, w):
    passing, sending = GATHER_RIDES[stage]
    n_pass = len(passing)

    def run(src, refs, new, sems, finish):
        if passing:
            _gather_pass(passing, refs[:n_pass], sems[:2], finish)
        if sending:
            _gather_ici(sending, refs[n_pass:], sems[-2:], finish)

    sems = (_gather_sems(n_pass) if passing else []) + (_gather_sems(len(sending)) if sending else [])
    return _Hosted([], [w[nm] for nm in _late_names(stage)], [], sems, functools.partial(run, finish=False),
                   functools.partial(run, finish=True))


def _late_names(stage):
    passing, sending = GATHER_RIDES[stage]
    return _names(passing) + _names(sending)


def _other_half_src(name, ref, h):
    if name in ("w_in", "w_out"):
        return ref.at[h]
    if name == "w_qkv":
        return ref.at[_row(h, D_MODEL // 2), :]
    if name == "w_o":
        return ref.at[:, pl.ds(h, 1)]
    return ref.at[pl.ds(2 * h, 2)]


SLAB_SHAPES = {"w_in": (D_MODEL, 2 * D_FF), "w_out": (D_FF, D_MODEL), "w_qkv": (D_MODEL // 2, QKV_DIM),
               "w_o": (N_CHIPS, 1, W_O_ROWS // 2, D_MODEL), "pool_w": (2, POOL_GC, POOL_GC)}


def _item_tag(item):
    return item[0] if item[1] is None else f"{item[0]}{item[1]}"


def _reduce_swap(items, grads, slabs, sems, finish):
    x, y, c, _ = _mesh_place()
    send, recv = sems
    for t, (nm, half) in enumerate(items):
        def copy(h, t=t, nm=nm):
            return pltpu.make_async_remote_copy(
                src_ref=_other_half_src(nm, grads[t], h), dst_ref=slabs[t], send_sem=send.at[t], recv_sem=recv.at[t],
                device_id=(x, y, 1 - c), device_id_type=MESH)

        def give(h):
            if finish:
                copy(h).wait_send()
            else:
                copy(h).start()

        if half is None:
            give(1 - c)
            if finish:
                copy(c).wait_recv()
        else:
            pl.when(c != half)(lambda: give(half))
            if finish:
                pl.when(c == half)(lambda: copy(half).wait_recv())


def _reduce_scatter(items, sums, parts, sems, finish):
    _, _, c, chips = _mesh_place()
    send, recv = sems
    for t, (nm, half) in enumerate(items):
        def go(h, t=t, nm=nm):
            for j, (cx, cy) in enumerate(chips):
                cp = pltpu.make_async_remote_copy(
                    src_ref=_slab_piece(nm, sums[t], 2 * cx + cy), dst_ref=parts[t].at[j], send_sem=send.at[t, j],
                    recv_sem=recv.at[t, j], device_id=(cx, cy, c), device_id_type=MESH)
                if finish:
                    cp.wait_recv()
                    cp.wait_send()
                else:
                    cp.start()

        _when_mover(half, c, go)


def _reduce_share(items, shards, sems, finish):
    x, y, c, _ = _mesh_place()
    send, recv = sems
    for t, (nm, half) in enumerate(items):
        def copy(h, t=t, nm=nm):
            part = _shard_half(nm, shards[t], h)
            return pltpu.make_async_remote_copy(src_ref=part, dst_ref=part, send_sem=send.at[t], recv_sem=recv.at[t],
                                                device_id=(x, y, 1 - c), device_id_type=MESH)

        def give(h):
            if finish:
                copy(h).wait_send()
            else:
                copy(h).start()

        _when_mover(half, c, give)
        if finish:
            _when_taker(half, c, lambda h: copy(h).wait_recv())


def _pair_sems(n):
    return [pltpu.SemaphoreType.DMA((n,)), pltpu.SemaphoreType.DMA((n,))]


def _swap_host(items, grads):
    def run(src, arr, new, sems, finish):
        _reduce_swap(items, src, new, sems, finish)

    return _Hosted([grads[nm] for nm, _ in items], [],
                   [jax.ShapeDtypeStruct(SLAB_SHAPES[nm], grads[nm].dtype) for nm, _ in items],
                   _pair_sems(len(items)), functools.partial(run, finish=False), functools.partial(run, finish=True))


def _scatter_host(items, sums):
    def run(src, arr, new, sems, finish):
        _reduce_scatter(items, src, new, sems, finish)

    new = [jax.ShapeDtypeStruct((N_CHIPS - 1,) + HALF_SHAPES[nm], WIRE_DTYPE) for nm, _ in items]
    return _Hosted(list(sums), [], new, _gather_sems(len(items)), functools.partial(run, finish=False),
                   functools.partial(run, finish=True))


def _share_host(items, shards):
    def run(src, arr, new, sems, finish):
        _reduce_share(items, arr, sems, finish)

    return _Hosted([], list(shards), [], _pair_sems(len(items)), functools.partial(run, finish=False),
                   functools.partial(run, finish=True))


def _alone(hosted, name):
    def body(ins, outs, scratch, first, last):
        first()
        last()

    return _call(body, hosted, name=name, grid=(1,), in_specs=[], out_specs=[], out_shape=[], operands=(),
                 semantics=("arbitrary",))


def _pair_sum(item, grad, slab, place):
    name, half = item
    shape = SLAB_SHAPES[name]
    hsel = (lambda p: p[1]) if half is None else (lambda p: half)
    if name == "w_in":
        blk = (256, 2 * D_FF)
        grid = (D_MODEL // 256,)
        g_spec = pl.BlockSpec((None,) + blk, lambda i, p: (hsel(p), i, 0))
        s_spec = pl.BlockSpec(blk, lambda i, p: (i, 0))
    elif name == "w_out":
        blk = (W_OUT_ROWS, D_MODEL)
        grid = (N_CHIPS,)
        g_spec = pl.BlockSpec((None,) + blk, lambda i, p: (hsel(p), i, 0))
        s_spec = pl.BlockSpec(blk, lambda i, p: (i, 0))
    elif name == "w_qkv":
        grid = (1,)
        g_spec = pl.BlockSpec(shape, lambda i, p: (hsel(p), 0))
        s_spec = pl.BlockSpec(shape, lambda i, p: (0, 0))
    else:
        grid = (1,)
        g_spec = pl.BlockSpec(shape, lambda i, p: (0, hsel(p), 0, 0))
        s_spec = pl.BlockSpec(shape, lambda i, p: (0, 0, 0, 0))

    def body(p_ref, g_ref, s_ref, o_ref):
        @pl.when(hsel(p_ref) == p_ref[1])
        def _():
            o_ref[...] = (g_ref[...].astype(F32) + s_ref[...].astype(F32)).astype(o_ref.dtype)

    return pl.pallas_call(
        body, name=f"pair_sum_{_item_tag(item)}",
        grid_spec=pltpu.PrefetchScalarGridSpec(num_scalar_prefetch=1, grid=grid, in_specs=[g_spec, s_spec],
                                               out_specs=s_spec),
        out_shape=jax.ShapeDtypeStruct(shape, WIRE_DTYPE), compiler_params=_params("parallel"),
    )(place, grad, slab)


def _slab_piece(name, ref, k):
    if name == "w_in":
        return ref.at[:, _col(k, W_IN_COLS)]
    if name == "w_out":
        return ref.at[_row(k, W_OUT_ROWS), :]
    if name == "w_qkv":
        return ref.at[:, _col(k, QKV_COLS)]
    if name == "w_o":
        return ref.at[k, 0]
    return ref.at[:, _row(k, POOL_ROWS), :]


def _sum_chips(item, slab, parts, place, into=None):
    name, sel = item
    half = HALF_SHAPES[name]
    hsel = (lambda p: p[1]) if sel is None else (lambda p: sel)
    if name == "w_in":
        blk, grid = (256, W_IN_COLS), (D_MODEL // 256,)
        own = pl.BlockSpec(blk, lambda i, p: (i, p[0]))
        got = pl.BlockSpec((N_CHIPS - 1,) + blk, lambda i, p: (0, i, 0))
        out = pl.BlockSpec((None,) + blk, lambda i, p: (hsel(p), i, 0))
    elif name == "w_out":
        grid = (1,)
        own = pl.BlockSpec(half, lambda i, p: (p[0], 0))
        got = pl.BlockSpec((N_CHIPS - 1,) + half, lambda i, p: (0, 0, 0))
        out = pl.BlockSpec((None,) + half, lambda i, p: (hsel(p), 0, 0))
    elif name == "w_qkv":
        grid = (1,)
        own = pl.BlockSpec(half, lambda i, p: (0, p[0]))
        got = pl.BlockSpec((N_CHIPS - 1,) + half, lambda i, p: (0, 0, 0))
        out = pl.BlockSpec(half, lambda i, p: (hsel(p), 0))
    else:
        grid = (1,)
        own = pl.BlockSpec((None, None) + half, lambda i, p: (p[0], 0, 0, 0))
        got = pl.BlockSpec((N_CHIPS - 1,) + half, lambda i, p: (0, 0, 0))
        out = pl.BlockSpec(half, lambda i, p: (hsel(p), 0))

    def body(p_ref, own_ref, got_ref, *rest):
        o_ref = rest[-1]

        @pl.when(hsel(p_ref) == p_ref[1])
        def _():
            acc = own_ref[...].astype(F32)
            for j in range(N_CHIPS - 1):
                acc = acc + got_ref[j].astype(F32)
            o_ref[...] = acc

    in_specs, operands, aliases = [own, got], [place, slab, parts], {}
    if into is not None:
        in_specs.append(ANY)
        operands.append(into)
        aliases = {3: 0}
    return pl.pallas_call(
        body, name=f"sum_chips_{_item_tag(item)}",
        grid_spec=pltpu.PrefetchScalarGridSpec(num_scalar_prefetch=1, grid=grid, in_specs=in_specs, out_specs=out),
        out_shape=jax.ShapeDtypeStruct(SHARD_SHAPES[name], F32), input_output_aliases=aliases,
        compiler_params=_params("parallel"),
    )(*operands)


class _Reducer:
    FIRST = (("w_in", 1), ("w_out", 1))
    MID = (("w_out", 0), ("w_qkv", None), ("w_o", None))
    LAST = (("w_in", 0),)

    def __init__(self, place):
        self.place = place
        self.shards = {}
        self.state = {}

    def swap(self, items, grads):
        self.state[items] = {"grads": [grads[nm] for nm, _ in items]}
        return _swap_host(items, grads)

    def pair_sums(self, items, slabs):
        st = self.state[items]
        st["sums"] = [_pair_sum(it, g, s, self.place) for it, g, s in zip(items, st["grads"], slabs)]

    def scatter(self, items):
        return _scatter_host(items, self.state[items]["sums"])

    def chip_sums(self, items, parts):
        for it, s, p in zip(items, self.state[items]["sums"], parts):
            self.shards[it[0]] = _sum_chips(it, s, p, self.place, into=self.shards.get(it[0]))

    def share(self, items):
        return _share_host(items, [self.shards[nm] for nm, _ in items])

    def shared(self, items, updated):
        self.shards.update(dict(zip(_names(items), updated)))


N_DEV = 8


def _all_reduce_small(vec, g_pool):
    def body(v_ref, gp_ref, o_ref, po_ref, slots, pslots, send_sems, recv_sems, psend, precv, hsend, hrecv):
        x, y, c, _ = _mesh_place()
        me = 4 * x + 2 * y + c

        def pool_piece(chip, h):
            return gp_ref.at[pl.ds(2 * h, 2), pl.ds(pl.multiple_of(chip * POOL_ROWS, POOL_ROWS), POOL_ROWS), :]

        slots[me] = v_ref[...]
        pslots[me] = pool_piece(2 * x + y, c)[...]
        copies = []
        for d in range(1, N_DEV):
            peer = (x ^ (d >> 2), y ^ ((d >> 1) & 1), c ^ (d & 1))
            copies.append(pltpu.make_async_remote_copy(
                src_ref=v_ref, dst_ref=slots.at[me], send_sem=send_sems.at[d - 1], recv_sem=recv_sems.at[d - 1],
                device_id=peer, device_id_type=MESH))
            copies.append(pltpu.make_async_remote_copy(
                src_ref=pool_piece(2 * peer[0] + peer[1], peer[2]), dst_ref=pslots.at[me], send_sem=psend.at[d - 1],
                recv_sem=precv.at[d - 1], device_id=peer, device_id_type=MESH))
        for cp in copies:
            cp.start()
        for d in range(1, N_DEV):
            peer = 4 * (x ^ (d >> 2)) + 2 * (y ^ ((d >> 1) & 1)) + (c ^ (d & 1))
            for buf, ss, rs in ((slots, send_sems, recv_sems), (pslots, psend, precv)):
                got = buf.at[peer]
                pltpu.make_async_remote_copy(src_ref=got, dst_ref=got, send_sem=ss.at[d - 1], recv_sem=rs.at[d - 1],
                                             device_id=(x, y, c), device_id_type=MESH).wait_recv()
        for cp in copies:
            cp.wait_send()
        acc, pacc = slots[0], pslots[0]
        for d in range(1, N_DEV):
            acc, pacc = acc + slots[d], pacc + pslots[d]
        o_ref[...] = acc
        po_ref[pl.ds(2 * c, 2)] = pacc
        mine, theirs = po_ref.at[pl.ds(2 * c, 2)], po_ref.at[pl.ds(2 * (1 - c), 2)]
        give = pltpu.make_async_remote_copy(src_ref=mine, dst_ref=mine, send_sem=hsend.at[0], recv_sem=hrecv.at[0],
                                            device_id=(x, y, 1 - c), device_id_type=MESH)
        give.start()
        pltpu.make_async_remote_copy(src_ref=theirs, dst_ref=theirs, send_sem=hsend.at[0], recv_sem=hrecv.at[0],
                                     device_id=(x, y, 1 - c), device_id_type=MESH).wait_recv()
        give.wait_send()

    vm = pl.BlockSpec(memory_space=pltpu.VMEM)
    piece = (2, POOL_ROWS, POOL_GC)
    return pl.pallas_call(
        body, name="all_reduce_small", in_specs=[vm, vm], out_specs=[vm, vm],
        out_shape=[jax.ShapeDtypeStruct(vec.shape, F32), jax.ShapeDtypeStruct(SHARD_SHAPES["pool_w"], F32)],
        scratch_shapes=[pltpu.VMEM((N_DEV,) + vec.shape, F32), pltpu.VMEM((N_DEV,) + piece, F32)]
        + [pltpu.SemaphoreType.DMA((N_DEV - 1,))] * 4 + [pltpu.SemaphoreType.DMA((1,))] * 2,
    )(vec, g_pool)


def _adamw_call(w, g, m, v, name, copy_grad=False):
    shape = w.shape
    cols = shape[-1]
    rows = w.size // cols
    tr = rows
    for cand in (256, 128, 64, 32, 16, 8):
        if rows > cand and rows % cand == 0 and cand * cols * 4 <= 2 * 1024 * 1024:
            tr = cand
            break

    def body(w_ref, g_ref, m_ref, v_ref, d_ref, nm_ref, nv_ref, *g_out):
        gv = g_ref[...]
        if copy_grad:
            g_out[0][...] = gv
        mn = ADAM_B1 * m_ref[...] + (1.0 - ADAM_B1) * gv
        vn = ADAM_B2 * v_ref[...] + (1.0 - ADAM_B2) * jnp.square(gv)
        m_hat = mn / (1.0 - ADAM_B1 ** ADAM_STEP)
        v_hat = vn / (1.0 - ADAM_B2 ** ADAM_STEP)
        d_ref[...] = -ADAM_LR * (m_hat / (jnp.sqrt(v_hat) + ADAM_EPS) + ADAM_WD * w_ref[...])
        nm_ref[...] = mn
        nv_ref[...] = vn

    spec = pl.BlockSpec((tr, cols), lambda i: (i, 0))
    flat = lambda a: a.reshape(rows, cols)
    n_out = 4 if copy_grad else 3
    outs = pl.pallas_call(
        body, name=name, grid=(rows // tr,), in_specs=[spec] * 4, out_specs=[spec] * n_out,
        out_shape=[jax.ShapeDtypeStruct((rows, cols), F32)] * n_out, compiler_params=_params("parallel"),
    )(flat(w), flat(g), flat(m), flat(v))
    return tuple(o.reshape(shape) for o in outs)


WEIGHTS = ("pool_w", "pool_b", "pool_scale", "attn_w_qkv", "attn_b_qkv", "attn_sinks", "attn_w_o", "norm_mix",
           "norm_ffn", "ffn_w_in", "ffn_conv_w", "ffn_conv_b", "ffn_w_out", "norm_f")
SMALL_GATHER = 128 * 71
_SMALL_SIZES = (("pool_b", 1024), ("pool_scale", 1024), ("b_qkv", 1536), ("sinks", 128), ("norm_mix", 2048),
                ("norm_ffn", 2048), ("conv_w", 6 * 2 * D_FF), ("conv_b", 4 * D_FF), ("norm_f", 1024))
SMALL_COLS = 6784


def _pack_small(grads):
    flat = jnp.concatenate([jnp.pad(grads[nm].reshape(-1), (0, size - grads[nm].size)) for nm, size in _SMALL_SIZES])
    return jnp.pad(flat, (0, 8 * SMALL_COLS - flat.size)).reshape(8, SMALL_COLS)


def _unpack_small(vec):
    flat = vec.reshape(-1)
    out, off = {}, 0
    for nm, size in _SMALL_SIZES:
        out[nm] = flat[off:off + size]
        off += size
    return out


def kernel(x, pool_w, pool_b, pool_scale, attn_w_qkv, attn_b_qkv, attn_sinks, attn_w_o, norm_mix, norm_ffn, ffn_w_in, ffn_conv_w, ffn_conv_b, ffn_w_out, norm_f, loss_target, m_pool_w, m_pool_b, m_pool_scale, m_attn_w_qkv, m_attn_b_qkv, m_attn_sinks, m_attn_w_o, m_norm_mix, m_norm_ffn, m_ffn_w_in, m_ffn_conv_w, m_ffn_conv_b, m_ffn_w_out, m_norm_f, v_pool_w, v_pool_b, v_pool_scale, v_attn_w_qkv, v_attn_b_qkv, v_attn_sinks, v_attn_w_o, v_norm_mix, v_norm_ffn, v_ffn_w_in, v_ffn_conv_w, v_ffn_conv_b, v_ffn_w_out, v_norm_f):
    weights = dict(pool_w=pool_w, pool_b=pool_b, pool_scale=pool_scale, attn_w_qkv=attn_w_qkv, attn_b_qkv=attn_b_qkv,
                   attn_sinks=attn_sinks, attn_w_o=attn_w_o, norm_mix=norm_mix, norm_ffn=norm_ffn, ffn_w_in=ffn_w_in,
                   ffn_conv_w=ffn_conv_w, ffn_conv_b=ffn_conv_b, ffn_w_out=ffn_w_out, norm_f=norm_f)
    m_in = dict(pool_w=m_pool_w, pool_b=m_pool_b, pool_scale=m_pool_scale, attn_w_qkv=m_attn_w_qkv,
                attn_b_qkv=m_attn_b_qkv, attn_sinks=m_attn_sinks, attn_w_o=m_attn_w_o, norm_mix=m_norm_mix,
                norm_ffn=m_norm_ffn, ffn_w_in=m_ffn_w_in, ffn_conv_w=m_ffn_conv_w, ffn_conv_b=m_ffn_conv_b,
                ffn_w_out=m_ffn_w_out, norm_f=m_norm_f)
    v_in = dict(pool_w=v_pool_w, pool_b=v_pool_b, pool_scale=v_pool_scale, attn_w_qkv=v_attn_w_qkv,
                attn_b_qkv=v_attn_b_qkv, attn_sinks=v_attn_sinks, attn_w_o=v_attn_w_o, norm_mix=v_norm_mix,
                norm_ffn=v_norm_ffn, ffn_w_in=v_ffn_w_in, ffn_conv_w=v_ffn_conv_w, ffn_conv_b=v_ffn_conv_b,
                ffn_w_out=v_ffn_w_out, norm_f=v_norm_f)
    chip = 2 * lax.axis_index("x") + lax.axis_index("y")
    core = lax.axis_index("c")

    place = jnp.stack([chip, core]).astype(jnp.int32)
    shards = {"w_in": ffn_w_in, "w_out": ffn_w_out, "w_qkv": attn_w_qkv[0], "w_o": attn_w_o[0], "pool_w": pool_w[0]}
    placed = {nm: _place_shard(nm, shards[nm], place) for nm in BIG}
    small = jnp.concatenate([pool_b.reshape(-1), attn_b_qkv.reshape(-1), ffn_conv_w.reshape(-1)]).reshape(1, SMALL_GATHER)
    full, small_all = _gather_early(placed, small)
    small_all = small_all.reshape(N_CHIPS, SMALL_GATHER)
    pool_b_full = small_all[:, :256].reshape(N_CHIPS, 4, POOL_ROWS).transpose(1, 0, 2).reshape(1, D_MODEL)
    b_qkv_full = small_all[:, 256:640].reshape(1, QKV_DIM)
    conv_w_full = small_all[:, 640:].reshape(N_CHIPS, 2, 3, W_IN_COLS).transpose(1, 2, 0, 3).reshape(2, 3, 2 * D_FF)

    reducer = _Reducer(place)
    loss, grad_x, grads = _local_step(
        x[0], loss_target[0], full, norm_mix=norm_mix, norm_ffn=norm_ffn, norm_f=norm_f.reshape(1, D_MODEL),
        pool_b=pool_b_full, pool_scale=pool_scale, b_qkv=b_qkv_full, sinks=attn_sinks[0], conv_w=conv_w_full,
        conv_b=ffn_conv_b, late_gather=_late_gather_host, reducer=reducer)
    loss = lax.psum(loss[0, 0], ("x", "y", "c"))
    reduced = dict(reducer.shards)

    small_sum, reduced["pool_w"] = _all_reduce_small(_pack_small(grads), grads["pool_w"])
    small_g = _unpack_small(small_sum)
    pool_b_g = lax.dynamic_slice_in_dim(small_g["pool_b"].reshape(4, N_CHIPS, POOL_ROWS), chip, 1, axis=1)
    b_qkv_g = lax.dynamic_slice_in_dim(small_g["b_qkv"].reshape(N_CHIPS, QKV_COLS), chip, 1, axis=0)
    conv_w_g = lax.dynamic_slice_in_dim(small_g["conv_w"].reshape(2, 3, N_CHIPS, W_IN_COLS), chip, 1, axis=2)
    grad_w = {
        "pool_w": reduced["pool_w"].reshape(pool_w.shape), "pool_b": pool_b_g.reshape(pool_b.shape),
        "pool_scale": small_g["pool_scale"].reshape(pool_scale.shape),
        "attn_w_qkv": reduced["w_qkv"].reshape(attn_w_qkv.shape), "attn_b_qkv": b_qkv_g.reshape(attn_b_qkv.shape),
        "attn_sinks": small_g["sinks"][:N_HEADS].reshape(attn_sinks.shape),
        "attn_w_o": reduced["w_o"].reshape(attn_w_o.shape), "norm_mix": small_g["norm_mix"].reshape(norm_mix.shape),
        "norm_ffn": small_g["norm_ffn"].reshape(norm_ffn.shape), "ffn_w_in": reduced["w_in"],
        "ffn_conv_w": conv_w_g.reshape(ffn_conv_w.shape), "ffn_conv_b": small_g["conv_b"].reshape(ffn_conv_b.shape),
        "ffn_w_out": reduced["w_out"], "norm_f": small_g["norm_f"].reshape(norm_f.shape),
    }

    delta, new_m, new_v = {}, {}, {}
    exchanged = ("pool_w", "attn_w_qkv", "attn_w_o", "ffn_w_in", "ffn_w_out")
    for nm in WEIGHTS:
        res = _adamw_call(weights[nm], grad_w[nm], m_in[nm], v_in[nm], f"adamw_{nm}", copy_grad=nm in exchanged)
        delta[nm], new_m[nm], new_v[nm] = res[:3]
        if nm in exchanged:
            grad_w[nm] = res[3]
    return (loss, grad_x.reshape(x.shape), *[grad_w[nm] for nm in WEIGHTS], *[delta[nm] for nm in WEIGHTS],
            *[new_m[nm] for nm in WEIGHTS], *[new_v[nm] for nm in WEIGHTS])
```

```python
import functools
from typing import Callable, NamedTuple

import jax
import jax.numpy as jnp
from jax import lax
from jax.experimental import pallas as pl
from jax.experimental.pallas import tpu as pltpu

F32 = jnp.float32
MXU_DTYPE = jnp.bfloat16
WIRE_DTYPE = jnp.bfloat16
U_DTYPE = jnp.float32

D_MODEL = 1024
D_FF = 2816
QKV_DIM = 1536
HEAD_DIM = 64
N_HEADS = 16
N_KV_HEADS = 4
ATT_BLOCK = 128
POOL_WINDOWS = (2, 4, 8, 16)
POOL_GC = 256
POOL_HALO = 16
CONV_HALO = 8
RMS_EPS = 1e-6
ATT_SCALE = HEAD_DIM ** -0.5
ALIBI_SLOPES = tuple(2.0 ** (-8.0 / N_HEADS * (h + 1)) for h in range(N_HEADS))

ADAM_LR = 0.001
ADAM_B1 = 0.9
ADAM_B2 = 0.999
ADAM_EPS = 1e-08
ADAM_WD = 0.01
ADAM_STEP = 10

N_CHIPS = 4
MESH = pl.DeviceIdType.MESH
VMEM_LIMIT_BYTES = 56 * 1024 * 1024
ANY = pl.BlockSpec(memory_space=pl.ANY)


def _params(*semantics):
    return pltpu.CompilerParams(dimension_semantics=semantics, vmem_limit_bytes=VMEM_LIMIT_BYTES)


def _rms(x, g):
    return x * lax.rsqrt(jnp.mean(x * x, axis=-1, keepdims=True) + RMS_EPS) * g


def _rms_bwd(x, g, dy):
    rstd = lax.rsqrt(jnp.mean(x * x, axis=-1, keepdims=True) + RMS_EPS)
    xhat = x * rstd
    dxhat = dy * g
    dx = rstd * (dxhat - xhat * jnp.mean(dxhat * xhat, axis=-1, keepdims=True))
    return dx, dy * xhat


def _shift_down(x, s):
    return pltpu.roll(x, s, axis=0)


def _shift_up(x, s):
    return pltpu.roll(x, x.shape[0] - s, axis=0)


def _sigmoid(x):
    return 1.0 / (1.0 + jnp.exp(-x))


_DOT_DIMS = {"nn": ((1,), (0,)), "nt": ((1,), (1,)), "tn": ((0,), (0,))}


class _Hosted(NamedTuple):
    sources: list
    arrays: list
    new: list
    sems: list
    first: Callable
    last: Callable


def _call(body, hosted, *, name, grid, in_specs, out_specs, out_shape, operands, semantics, scratch_shapes=(),
          aliases=None):
    n_in, n_out, n_scr = len(in_specs), len(out_specs), len(scratch_shapes)
    aliases = dict(aliases or {})
    if hosted is None:
        def plain(*refs):
            body(refs[:n_in], refs[n_in:n_in + n_out], refs[n_in + n_out:], lambda: None, lambda: None)

        return pl.pallas_call(plain, name=name, grid=grid, in_specs=list(in_specs), out_specs=list(out_specs),
                              out_shape=list(out_shape), scratch_shapes=list(scratch_shapes),
                              input_output_aliases=aliases, compiler_params=_params(*semantics))(*operands)
    ns, na, nn = len(hosted.sources), len(hosted.arrays), len(hosted.new)

    def carrying(*refs):
        ins, src = refs[:n_in], refs[n_in:n_in + ns]
        o0 = n_in + ns + na
        outs, arr = refs[o0:o0 + n_out], refs[o0 + n_out:o0 + n_out + na]
        new = refs[o0 + n_out + na:o0 + n_out + na + nn]
        s0 = o0 + n_out + na + nn
        scratch, sems = refs[s0:s0 + n_scr], refs[s0 + n_scr:]
        ids = [pl.program_id(d) for d in range(len(grid))]
        is_first = functools.reduce(lambda p, q: p & q, [i == 0 for i in ids])
        is_last = functools.reduce(lambda p, q: p & q, [i == n - 1 for i, n in zip(ids, grid)])
        body(ins, outs, scratch, lambda: pl.when(is_first)(lambda: hosted.first(src, arr, new, sems)),
             lambda: pl.when(is_last)(lambda: hosted.last(src, arr, new, sems)))

    for t in range(na):
        aliases[n_in + ns + t] = n_out + t
    return pl.pallas_call(
        carrying, name=name, grid=grid, in_specs=list(in_specs) + [ANY] * (ns + na),
        out_specs=list(out_specs) + [ANY] * (na + nn),
        out_shape=list(out_shape) + [jax.ShapeDtypeStruct(a.shape, a.dtype) for a in hosted.arrays] + list(hosted.new),
        scratch_shapes=list(scratch_shapes) + list(hosted.sems), input_output_aliases=aliases,
        compiler_params=_params(*["arbitrary"] * len(grid)))(*operands, *hosted.sources, *hosted.arrays)


def _both(h1, h2):
    def run(which, src, arr, new, sems):
        cut = lambda seq, n: (seq[:n], seq[n:])
        (s1, s2), (a1, a2) = cut(src, len(h1.sources)), cut(arr, len(h1.arrays))
        (n1, n2), (m1, m2) = cut(new, len(h1.new)), cut(sems, len(h1.sems))
        getattr(h1, which)(s1, a1, n1, m1)
        getattr(h2, which)(s2, a2, n2, m2)

    return _Hosted(h1.sources + h2.sources, h1.arrays + h2.arrays, h1.new + h2.new, h1.sems + h2.sems,
                   functools.partial(run, "first"), functools.partial(run, "last"))


def _ride(hooks, stage, call, env=None):
    if not hooks or stage not in hooks:
        res = call(None)
        return list(res) if isinstance(res, (list, tuple)) else [res]
    make, done = hooks[stage]
    hosted = make(env)
    res = call(hosted)
    own = len(res) - len(hosted.arrays) - len(hosted.new)
    done(res[own:])
    return list(res[:own])


def _matmul(a, b, *, mode, name, grid, a_spec, b_spec, o_spec, out_shape, bias=None, bias_spec=None, residual=None,
            residual_spec=None, into=None, hosted=None, norm_g=None):
    nk = grid[2]
    dims = (_DOT_DIMS[mode], ((), ()))
    acc_shape = tuple(d for d in o_spec.block_shape if d is not None)

    def body(ins, outs, scratch, first, last):
        a_ref, b_ref = ins[0], ins[1]
        pos = 2
        bias_ref = res_ref = None
        if bias is not None:
            bias_ref = ins[pos]
            pos += 1
        if residual is not None:
            res_ref = ins[pos]
            pos += 1
        o_ref = outs[0]
        acc_ref = scratch[0] if nk > 1 else None
        first()
        prod = lax.dot_general(a_ref[...].astype(MXU_DTYPE), b_ref[...].astype(MXU_DTYPE), dims,
                               preferred_element_type=F32)

        def finish(acc):
            if bias_ref is not None:
                acc = acc + bias_ref[...]
            if res_ref is not None:
                acc = acc + res_ref[...]
            o_ref[...] = acc.astype(o_ref.dtype)
            if norm_g is not None:
                outs[1][...] = _rms(acc, ins[pos][...]).astype(outs[1].dtype)

        if nk == 1:
            finish(prod)
        else:
            k = pl.program_id(2)

            @pl.when(k == 0)
            def _():
                acc_ref[...] = prod

            @pl.when(k > 0)
            def _():
                acc_ref[...] += prod

            @pl.when(k == nk - 1)
            def _():
                finish(acc_ref[...])

        last()

    operands, in_specs = [a, b], [a_spec, b_spec]
    if bias is not None:
        operands.append(bias)
        in_specs.append(bias_spec)
    if residual is not None:
        operands.append(residual)
        in_specs.append(residual_spec)
    out_specs, out_shapes = [o_spec], [out_shape]
    if norm_g is not None:
        assert acc_shape[-1] == norm_g.shape[-1], "the fused norm needs whole rows in an output tile"
        operands.append(norm_g)
        in_specs.append(bias_spec)
        out_specs.append(o_spec)
        out_shapes.append(jax.ShapeDtypeStruct(out_shape.shape, MXU_DTYPE))
    aliases = {}
    if into is not None:
        aliases = {len(operands): 0}
        operands.append(into)
        in_specs.append(ANY)
    res = _call(body, hosted, name=name, grid=grid, in_specs=in_specs, out_specs=out_specs, out_shape=out_shapes,
                operands=operands, semantics=("parallel", "parallel", "arbitrary"),
                scratch_shapes=[pltpu.VMEM(acc_shape, F32)] if nk > 1 else [], aliases=aliases)
    return res if hosted or norm_g is not None else res[0]


def _tile(n, want):
    t = min(n, want)
    assert n % t == 0, (n, want)
    return t


def _mm_nn(a, b, name, *, b_lead=None, bias=None, residual=None, out_dtype=F32, tm=1024, tn=1024, tk=1024,
           hosted=None, norm_g=None):
    (m, k), n = a.shape, b.shape[-1]
    tm, tn, tk = _tile(m, tm), _tile(n, tn), _tile(k, tk)
    if b_lead is None:
        b_spec = pl.BlockSpec((tk, tn), lambda i, j, kk: (kk, j))
    else:
        b_spec = pl.BlockSpec((None, tk, tn), lambda i, j, kk: (b_lead, kk, j))
    return _matmul(
        a, b, mode="nn", name=name, grid=(m // tm, n // tn, k // tk),
        a_spec=pl.BlockSpec((tm, tk), lambda i, j, kk: (i, kk)), b_spec=b_spec,
        o_spec=pl.BlockSpec((tm, tn), lambda i, j, kk: (i, j)), out_shape=jax.ShapeDtypeStruct((m, n), out_dtype),
        bias=bias, bias_spec=pl.BlockSpec((1, tn), lambda i, j, kk: (0, j)),
        residual=residual, residual_spec=pl.BlockSpec((tm, tn), lambda i, j, kk: (i, j)), hosted=hosted,
        norm_g=norm_g)


def _mm_nt(a, b, name, *, b_lead=None, out_dtype=F32, tm=1024, tn=1024, tk=1024, hosted=None):
    (m, k), n = a.shape, b.shape[-2]
    tm, tn, tk = _tile(m, tm), _tile(n, tn), _tile(k, tk)
    if b_lead is None:
        b_spec = pl.BlockSpec((tn, tk), lambda i, j, kk: (j, kk))
    else:
        b_spec = pl.BlockSpec((None, tn, tk), lambda i, j, kk: (b_lead, j, kk))
    return _matmul(
        a, b, mode="nt", name=name, grid=(m // tm, n // tn, k // tk),
        a_spec=pl.BlockSpec((tm, tk), lambda i, j, kk: (i, kk)), b_spec=b_spec,
        o_spec=pl.BlockSpec((tm, tn), lambda i, j, kk: (i, j)), out_shape=jax.ShapeDtypeStruct((m, n), out_dtype),
        hosted=hosted)


def _mm_tn(a, b, name, *, lead=None, n_lead=None, into=None, tm=1024, tn=1024, tk=1024, out_dtype=F32):
    (k, m), n = a.shape, b.shape[-1]
    tm, tn, tk = _tile(m, tm), _tile(n, tn), _tile(k, tk)
    if lead is None:
        o_spec = pl.BlockSpec((tm, tn), lambda i, j, kk: (i, j))
        out_shape = jax.ShapeDtypeStruct((m, n), out_dtype)
    else:
        o_spec = pl.BlockSpec((None, tm, tn), lambda i, j, kk: (lead, i, j))
        out_shape = jax.ShapeDtypeStruct((n_lead, m, n), out_dtype)
    return _matmul(
        a, b, mode="tn", name=name, grid=(m // tm, n // tn, k // tk),
        a_spec=pl.BlockSpec((tk, tm), lambda i, j, kk: (kk, i)),
        b_spec=pl.BlockSpec((tk, tn), lambda i, j, kk: (kk, j)), o_spec=o_spec, out_shape=out_shape, into=into)


ROW_TILE = 512


def _pool_windows_causal(ext, first_row):
    tt = ext.shape[0] - POOL_HALO
    t = first_row + lax.broadcasted_iota(jnp.int32, (tt, 1), 0)
    outs = []
    for gi, win in enumerate(POOL_WINDOWS):
        cols = slice(gi * POOL_GC, (gi + 1) * POOL_GC)
        s = ext[:, cols]
        sh = 1
        while sh < win:
            s = s + _shift_down(s, sh)
            sh *= 2
        count = jnp.minimum(t + 1, win).astype(F32)
        outs.append(s[POOL_HALO:] / count - ext[POOL_HALO:, cols])
    return outs


def _prev_halo_spec(tt, halo, width):
    per = tt // halo
    return pl.BlockSpec((halo, width), lambda i: (jnp.maximum(i * per - 1, 0), 0))


def _pool_fwd_call(h0, g_mix, pool_w, pool_b, pool_scale, g_ffn, hosted=None):
    t = h0.shape[0]
    tt = _tile(t, ROW_TILE)

    def body(ins, outs, scratch, first, last):
        h_ref, halo_ref, gm_ref, w_ref, b_ref, sc_ref, gf_ref = ins
        h1_ref, hn_ref = outs
        i = pl.program_id(0)
        first()
        x, gm = h_ref[...], gm_ref[...]
        top = jnp.where(i > 0, _rms(halo_ref[...], gm), 0.0)
        ext = jnp.concatenate([top, _rms(x, gm)], axis=0)
        ps = _pool_windows_causal(ext, i * tt)
        ys = [jnp.dot(p.astype(MXU_DTYPE), w_ref[gi], preferred_element_type=F32) for gi, p in enumerate(ps)]
        mix = (jnp.concatenate(ys, axis=1) + b_ref[...]) * sc_ref[...]
        h1 = x + mix
        h1_ref[...] = h1
        hn_ref[...] = _rms(h1, gf_ref[...]).astype(hn_ref.dtype)
        last()

    row = pl.BlockSpec((tt, D_MODEL), lambda i: (i, 0))
    vec = pl.BlockSpec((1, D_MODEL), lambda i: (0, 0))
    wsp = pl.BlockSpec((len(POOL_WINDOWS), POOL_GC, POOL_GC), lambda i: (0, 0, 0))
    return _call(
        body, hosted, name="pool_fwd", grid=(t // tt,),
        in_specs=[row, _prev_halo_spec(tt, POOL_HALO, D_MODEL), vec, wsp, vec, vec, vec], out_specs=[row, row],
        out_shape=[jax.ShapeDtypeStruct(h0.shape, F32), jax.ShapeDtypeStruct(h0.shape, MXU_DTYPE)],
        operands=(h0, h0, g_mix, pool_w, pool_b, pool_scale, g_ffn), semantics=("parallel",))


def _pool_bwd_call(h0, dh1, g_mix, pool_w, pool_b, pool_scale, hosted=None):
    t = h0.shape[0]
    tt = _tile(t, ROW_TILE)
    nt = t // tt
    per = tt // POOL_HALO

    def body(ins, outs, scratch, first, last):
        h_ref, halo_ref, d_ref, dnext_ref, gm_ref, w_ref, b_ref, sc_ref = ins
        gx_ref, dw_ref, dv_ref = outs
        i = pl.program_id(0)
        first()
        x, gm, sc = h_ref[...], gm_ref[...], sc_ref[...]
        top = jnp.where(i > 0, _rms(halo_ref[...], gm), 0.0)
        ext = jnp.concatenate([top, _rms(x, gm)], axis=0)
        ps = _pool_windows_causal(ext, i * tt)

        dy = d_ref[...]
        dy_ext = jnp.concatenate([dy, jnp.where(i < nt - 1, dnext_ref[...], 0.0)], axis=0)
        dpre_ext = dy_ext * sc
        t_ext = i * tt + lax.broadcasted_iota(jnp.int32, (tt + POOL_HALO, 1), 0)

        @pl.when(i == 0)
        def _():
            dw_ref[...] = jnp.zeros_like(dw_ref)
            dv_ref[...] = jnp.zeros_like(dv_ref)

        dhn, ypre = [], []
        for gi, win in enumerate(POOL_WINDOWS):
            cols = slice(gi * POOL_GC, (gi + 1) * POOL_GC)
            w = w_ref[gi]
            p16 = ps[gi].astype(MXU_DTYPE)
            dpre16 = dpre_ext[:, cols].astype(MXU_DTYPE)
            ypre.append(jnp.dot(p16, w, preferred_element_type=F32))
            dw_ref[gi] += lax.dot_general(p16, dpre16[:tt], (((0,), (0,)), ((), ())), preferred_element_type=F32)
            dp_ext = lax.dot_general(dpre16, w, (((1,), (1,)), ((), ())), preferred_element_type=F32)
            s = dp_ext / jnp.minimum(t_ext + 1, win).astype(F32)
            sh = 1
            while sh < win:
                s = s + _shift_up(s, sh)
                sh *= 2
            dhn.append(s[:tt] - dp_ext[:tt])
        dhn = jnp.concatenate(dhn, axis=1)
        ypre = jnp.concatenate(ypre, axis=1) + b_ref[...]
        dx, dgt = _rms_bwd(x, gm, dhn)
        gx_ref[...] = dy + dx
        dv_ref[0:1, :] += jnp.sum(dpre_ext[:tt], axis=0, keepdims=True)
        dv_ref[1:2, :] += jnp.sum(dy * ypre, axis=0, keepdims=True)
        dv_ref[2:3, :] += jnp.sum(dgt, axis=0, keepdims=True)
        last()

    row = pl.BlockSpec((tt, D_MODEL), lambda i: (i, 0))
    vec = pl.BlockSpec((1, D_MODEL), lambda i: (0, 0))
    nxt = pl.BlockSpec((POOL_HALO, D_MODEL), lambda i: (jnp.minimum((i + 1) * per, t // POOL_HALO - 1), 0))
    wsp = pl.BlockSpec((len(POOL_WINDOWS), POOL_GC, POOL_GC), lambda i: (0, 0, 0))
    return _call(
        body, hosted, name="pool_bwd", grid=(nt,),
        in_specs=[row, _prev_halo_spec(tt, POOL_HALO, D_MODEL), row, nxt, vec, wsp, vec, vec],
        out_specs=[row, wsp, pl.BlockSpec((8, D_MODEL), lambda i: (0, 0))],
        out_shape=[jax.ShapeDtypeStruct(h0.shape, F32),
                   jax.ShapeDtypeStruct((len(POOL_WINDOWS), POOL_GC, POOL_GC), F32),
                   jax.ShapeDtypeStruct((8, D_MODEL), F32)],
        operands=(h0, h0, dh1, dh1, g_mix, pool_w, pool_b, pool_scale), semantics=("arbitrary",))


CONV_ROWS = 512
CONV_SUB = 64
LANES = 128


def _causal_conv(u_ext, w, b):
    return w[0:1] * _shift_down(u_ext, 2) + w[1:2] * _shift_down(u_ext, 1) + w[2:3] * u_ext + b


def _piece_rows(load, s, sub, tt, top, bottom=None):
    lo, hi = s * sub - CONV_HALO, (s + 1) * sub + (CONV_HALO if bottom else 0)
    parts = [top()] if lo < 0 else []
    lo = max(lo, 0)
    parts.append(load(lo, min(hi, tt) - lo))
    if hi > tt:
        parts.append(bottom())
    return parts[0] if len(parts) == 1 else jnp.concatenate(parts, axis=0)


def _fold8(x):
    acc = x[0:8]
    for r in range(8, x.shape[0], 8):
        acc = acc + x[r:r + 8]
    return acc


def _conv_glu_fwd_call(u, conv_w, conv_b, name, hosted=None):
    _, t, f = u.shape
    tt = _tile(t, CONV_ROWS)
    sub = _tile(tt, CONV_SUB)
    per = tt // CONV_HALO

    def body(ins, outs, scratch, first, last):
        u_ref, up_ref, w_ref, b_ref = ins
        z_ref = outs[0]
        i = pl.program_id(0)
        first()

        def chunk(j, carry):
            lanes = pl.ds(pl.multiple_of(j * LANES, LANES), LANES)
            w, b = w_ref[:, :, lanes], b_ref[:, lanes]
            for s in range(tt // sub):
                c = []
                for hf in range(2):
                    ext = _piece_rows(lambda at, n: u_ref[hf, pl.ds(at, n), lanes].astype(F32), s, sub, tt,
                                      top=lambda: jnp.where(i > 0, up_ref[hf, :, lanes].astype(F32), 0.0))
                    c.append(_causal_conv(ext, w[:, hf, :], b[hf:hf + 1, :])[CONV_HALO:])
                z_ref[pl.ds(s * sub, sub), lanes] = (c[0] * c[1] * _sigmoid(c[1])).astype(z_ref.dtype)
            return carry

        lax.fori_loop(0, f // LANES, chunk, 0)
        last()

    res = _call(
        body, hosted, name=name, grid=(t // tt,),
        in_specs=[pl.BlockSpec((2, tt, f), lambda i: (0, i, 0)),
                  pl.BlockSpec((2, CONV_HALO, f), lambda i: (0, jnp.maximum(i * per - 1, 0), 0)),
                  pl.BlockSpec((3, 2, f), lambda i: (0, 0, 0)), pl.BlockSpec((2, f), lambda i: (0, 0))],
        out_specs=[pl.BlockSpec((tt, f), lambda i: (i, 0))], out_shape=[jax.ShapeDtypeStruct((t, f), MXU_DTYPE)],
        operands=(u, u, conv_w, conv_b), semantics=("parallel",))
    return res if hosted else res[0]


def _conv_glu_bwd_call(u, dz, conv_w, conv_b, name, hosted=None):
    _, t, f = u.shape
    tt = _tile(t, CONV_ROWS)
    sub = _tile(tt, 2 * CONV_SUB)
    nt = t // tt
    per = tt // CONV_HALO
    halo = CONV_HALO

    def body(ins, outs, scratch, first, last):
        u_ref, up_ref, un_ref, dz_ref, dzn_ref, w_ref, b_ref = ins
        du_ref, dw_ref, db_ref = outs
        dwacc, dbacc = scratch
        i = pl.program_id(0)
        first()

        @pl.when(i == 0)
        def _():
            dwacc[...] = jnp.zeros_like(dwacc)
            dbacc[...] = jnp.zeros_like(dbacc)

        main = slice(halo, halo + sub)

        def chunk(j, carry):
            lanes = pl.ds(pl.multiple_of(j * LANES, LANES), LANES)
            w, b = w_ref[:, :, lanes], b_ref[:, lanes]
            for s in range(tt // sub):
                ue = [_piece_rows(lambda at, n: u_ref[hf, pl.ds(at, n), lanes].astype(F32), s, sub, tt,
                                  top=lambda: jnp.where(i > 0, up_ref[hf, :, lanes].astype(F32), 0.0),
                                  bottom=lambda: un_ref[hf, :, lanes].astype(F32)) for hf in range(2)]
                a, g = [_causal_conv(ue[hf], w[:, hf, :], b[hf:hf + 1, :]) for hf in range(2)]
                sg = _sigmoid(g)
                if (s + 1) * sub < tt:
                    below = dz_ref[pl.ds(s * sub, sub + 2 * halo), lanes].astype(F32)[:sub + halo]
                else:
                    below = jnp.concatenate([dz_ref[pl.ds(s * sub, sub), lanes].astype(F32),
                                             jnp.where(i < nt - 1, dzn_ref[:, lanes].astype(F32), 0.0)], axis=0)
                dzs = jnp.concatenate([jnp.zeros((halo, LANES), F32), below], axis=0)
                gs = g * sg
                dc = [dzs * gs, dzs * a * (sg + gs - gs * sg)]
                for hf in range(2):
                    d = dc[hf]
                    d1, d2 = _shift_up(d, 1), _shift_up(d, 2)
                    du = w[2:3, hf, :] * d + w[1:2, hf, :] * d1 + w[0:1, hf, :] * d2
                    du_ref[hf, pl.ds(s * sub, sub), lanes] = du[main].astype(du_ref.dtype)
                    um = ue[hf][main]
                    for k, dd in enumerate((d2, d1, d)):
                        dwacc[k, hf, :, lanes] += _fold8(dd[main] * um)
                    dbacc[hf, :, lanes] += _fold8(d[main])
            return carry

        lax.fori_loop(0, f // LANES, chunk, 0)

        @pl.when(i == nt - 1)
        def _():
            dw_ref[...] = jnp.sum(dwacc[...], axis=2)
            db_ref[...] = jnp.sum(dbacc[...], axis=1)

        last()

    end = t // halo - 1
    return _call(
        body, hosted, name=name, grid=(nt,),
        in_specs=[pl.BlockSpec((2, tt, f), lambda i: (0, i, 0)),
                  pl.BlockSpec((2, halo, f), lambda i: (0, jnp.maximum(i * per - 1, 0), 0)),
                  pl.BlockSpec((2, halo, f), lambda i: (0, jnp.minimum((i + 1) * per, end), 0)),
                  pl.BlockSpec((tt, f), lambda i: (i, 0)),
                  pl.BlockSpec((halo, f), lambda i: (jnp.minimum((i + 1) * per, end), 0)),
                  pl.BlockSpec((3, 2, f), lambda i: (0, 0, 0)), pl.BlockSpec((2, f), lambda i: (0, 0))],
        out_specs=[pl.BlockSpec((2, tt, f), lambda i: (0, i, 0)),
                   pl.BlockSpec((3, 2, f), lambda i: (0, 0, 0)), pl.BlockSpec((2, f), lambda i: (0, 0))],
        out_shape=[jax.ShapeDtypeStruct(u.shape, MXU_DTYPE), jax.ShapeDtypeStruct((3, 2, f), F32),
                   jax.ShapeDtypeStruct((2, f), F32)],
        operands=(u, u, u, dz, dz, conv_w, conv_b), semantics=("arbitrary",),
        scratch_shapes=[pltpu.VMEM((3, 2, 8, f), F32), pltpu.VMEM((2, 8, f), F32)])


def _kv_even_odd(band, k):
    pair, parity = divmod(k, 2)
    blk = band[:, 128 * pair:128 * (pair + 1)].astype(F32)
    lane = lax.broadcasted_iota(jnp.int32, blk.shape, 1)
    own = jnp.where((lane >= HEAD_DIM) == (parity == 1), blk, 0.0)
    swapped = pltpu.roll(own, HEAD_DIM, axis=1)
    even, odd = (own, swapped) if parity == 0 else (swapped, own)
    return even.astype(MXU_DTYPE), odd.astype(MXU_DTYPE)


def _stack_pairs(ref, k):
    return jnp.concatenate([ref[:, 256 * k:256 * k + 128], ref[:, 256 * k + 128:256 * k + 256]], axis=0)


_GROUP_ORDER = (0, 2, 1, 3)


def _attn_fill_tables(bias_ref):
    shape = (4 * ATT_BLOCK, 2 * ATT_BLOCK)
    row = lax.broadcasted_iota(jnp.int32, shape, 0)
    col = lax.broadcasted_iota(jnp.int32, shape, 1)
    dist = (row & (ATT_BLOCK - 1)) + ATT_BLOCK - col
    ok = (dist >= 0) & (dist < ATT_BLOCK)
    distf = dist.astype(F32)
    rb = lax.broadcasted_iota(jnp.int32, (shape[0], 1), 0) // ATT_BLOCK
    for k in range(N_KV_HEADS):
        slope = jnp.zeros((shape[0], 1), F32)
        for r, gq in enumerate(_GROUP_ORDER):
            slope = jnp.where(rb == r, ALIBI_SLOPES[4 * k + gq], slope)
        bias = jnp.where(ok, -slope * distf, -jnp.inf)
        bias_ref[0, k] = bias
        bias_ref[1, k] = jnp.where(col >= ATT_BLOCK, bias, -jnp.inf)


def _attn_probs(qs, k_even, k_odd, bias, k, sink_ref):
    nt_dims = (((1,), (1,)), ((), ()))
    s = jnp.concatenate([lax.dot_general(qs, k_even, nt_dims, preferred_element_type=F32),
                         lax.dot_general(qs, k_odd, nt_dims, preferred_element_type=F32)], axis=0) + bias
    rb = lax.broadcasted_iota(jnp.int32, (s.shape[0], 1), 0) // ATT_BLOCK
    sink = jnp.zeros((s.shape[0], 1), F32)
    for r, gq in enumerate(_GROUP_ORDER):
        sink = jnp.where(rb == r, sink_ref[4 * k + gq], sink)
    m = jnp.maximum(jnp.max(s, axis=-1, keepdims=True), sink)
    p = jnp.exp(s - m)
    es = jnp.exp(sink - m)
    return p, 1.0 / (jnp.sum(p, axis=-1, keepdims=True) + es), es


_ATTN_TABLES = [pltpu.VMEM((2, N_KV_HEADS, 4 * ATT_BLOCK, 2 * ATT_BLOCK), F32)]


def _attn_specs(with_do):
    prev = lambda n: jnp.maximum(n - 1, 0)
    specs = [pl.BlockSpec(memory_space=pltpu.SMEM),
             pl.BlockSpec((ATT_BLOCK, D_MODEL), lambda n: (n, 0)),
             pl.BlockSpec((ATT_BLOCK, 256), lambda n: (prev(n), 4)), pl.BlockSpec((ATT_BLOCK, 256), lambda n: (n, 4)),
             pl.BlockSpec((ATT_BLOCK, 256), lambda n: (prev(n), 5)), pl.BlockSpec((ATT_BLOCK, 256), lambda n: (n, 5))]
    if with_do:
        specs.append(pl.BlockSpec((ATT_BLOCK, D_MODEL), lambda n: (n, 0)))
    return specs


def _attn_fwd_call(qkv, sinks, hosted=None):
    t = qkv.shape[0]

    def body(ins, outs, scratch, first, last):
        sink_ref, q_ref, kp_ref, kc_ref, vp_ref, vc_ref = ins
        o_ref, bias_ref = outs[0], scratch[0]
        n = pl.program_id(0)
        first()

        @pl.when(n == 0)
        def _():
            _attn_fill_tables(bias_ref)

        first = 1 - jnp.minimum(n, 1)
        kband = jnp.concatenate([kp_ref[...], kc_ref[...]], axis=0)
        vband = jnp.concatenate([vp_ref[...], vc_ref[...]], axis=0)
        for k in range(N_KV_HEADS):
            k_even, k_odd = _kv_even_odd(kband, k)
            v_even, v_odd = _kv_even_odd(vband, k)
            qs = _stack_pairs(q_ref, k) * ATT_SCALE
            p, rdenom, _ = _attn_probs(qs, k_even, k_odd, bias_ref[first, k], k, sink_ref)
            probs = (p * rdenom).astype(MXU_DTYPE)
            o = (jnp.dot(probs[:256], v_even, preferred_element_type=F32)
                 + jnp.dot(probs[256:], v_odd, preferred_element_type=F32))
            o_ref[:, 256 * k:256 * k + 128] = o[:128].astype(o_ref.dtype)
            o_ref[:, 256 * k + 128:256 * k + 256] = o[128:].astype(o_ref.dtype)
        last()

    return _call(
        body, hosted, name="attn_fwd", grid=(t // ATT_BLOCK,), in_specs=_attn_specs(False),
        out_specs=[pl.BlockSpec((ATT_BLOCK, D_MODEL), lambda n: (n, 0))],
        out_shape=[jax.ShapeDtypeStruct((t, D_MODEL), MXU_DTYPE)], operands=(sinks, qkv, qkv, qkv, qkv, qkv),
        semantics=("arbitrary",), scratch_shapes=_ATTN_TABLES)


def _attn_bwd_call(qkv, sinks, do, hosted=None):
    t = qkv.shape[0]
    nb = t // ATT_BLOCK
    tn_dims = (((0,), (0,)), ((), ()))
    nt_dims = (((1,), (1,)), ((), ()))

    def to_native(even_part, odd_part, parity):
        lane = lax.broadcasted_iota(jnp.int32, even_part.shape, 1)
        lo = lane < HEAD_DIM
        e, o = jnp.where(lo, even_part, 0.0), jnp.where(lo, 0.0, odd_part)
        if parity == 0:
            return e + pltpu.roll(o, HEAD_DIM, axis=1)
        return pltpu.roll(e, HEAD_DIM, axis=1) + o

    def body(ins, outs, scratch, first, last):
        sink_ref, q_ref, kp_ref, kc_ref, vp_ref, vc_ref, do_ref = ins
        dq_ref, dk_ref, dv_ref, db_ref, dsink_ref = outs
        bias_ref = scratch[0]
        n = pl.program_id(0)
        first()

        @pl.when(n == 0)
        def _():
            _attn_fill_tables(bias_ref)
            dk_ref[...] = jnp.zeros_like(dk_ref)
            dv_ref[...] = jnp.zeros_like(dv_ref)
            db_ref[...] = jnp.zeros_like(db_ref)
            dsink_ref[...] = jnp.zeros_like(dsink_ref)

        kband = jnp.concatenate([kp_ref[...], kc_ref[...]], axis=0)
        vband = jnp.concatenate([vp_ref[...], vc_ref[...]], axis=0)
        dk_pairs = [jnp.zeros((2 * ATT_BLOCK, 128), F32), jnp.zeros((2 * ATT_BLOCK, 128), F32)]
        dv_pairs = [jnp.zeros((2 * ATT_BLOCK, 128), F32), jnp.zeros((2 * ATT_BLOCK, 128), F32)]
        sink_lane = lax.broadcasted_iota(jnp.int32, (8, 128), 1)
        sink_row = lax.broadcasted_iota(jnp.int32, (8, 128), 0)
        dsink = jnp.zeros((8, 128), F32)
        first = 1 - jnp.minimum(n, 1)
        for k in range(N_KV_HEADS):
            k_even, k_odd = _kv_even_odd(kband, k)
            v_even, v_odd = _kv_even_odd(vband, k)
            qs = _stack_pairs(q_ref, k) * ATT_SCALE
            dos = _stack_pairs(do_ref, k)
            p, rdenom, es = _attn_probs(qs, k_even, k_odd, bias_ref[first, k], k, sink_ref)
            probs = p * rdenom
            dp = jnp.concatenate([lax.dot_general(dos, v_even, nt_dims, preferred_element_type=F32),
                                  lax.dot_general(dos, v_odd, nt_dims, preferred_element_type=F32)], axis=0)
            delta = jnp.sum(probs * dp, axis=-1, keepdims=True)
            ds16 = (probs * (dp - delta)).astype(MXU_DTYPE)
            p16 = probs.astype(MXU_DTYPE)
            dsink_rows = -(es * rdenom) * delta
            for r, gq in enumerate(_GROUP_ORDER):
                tot = jnp.sum(dsink_rows[ATT_BLOCK * r:ATT_BLOCK * (r + 1)], axis=0, keepdims=True)
                dsink = dsink + jnp.where((sink_lane == 4 * k + gq) & (sink_row == 0), tot, 0.0)
            dqs = (jnp.dot(ds16[:256], k_even, preferred_element_type=F32)
                   + jnp.dot(ds16[256:], k_odd, preferred_element_type=F32)) * ATT_SCALE
            dq_ref[:, 256 * k:256 * k + 128] = dqs[:128].astype(dq_ref.dtype)
            dq_ref[:, 256 * k + 128:256 * k + 256] = dqs[128:].astype(dq_ref.dtype)
            db_ref[:, 256 * k:256 * k + 128] += jnp.sum(dqs[:128], axis=0, keepdims=True)
            db_ref[:, 256 * k + 128:256 * k + 256] += jnp.sum(dqs[128:], axis=0, keepdims=True)
            pair, parity = divmod(k, 2)
            dk_pairs[pair] = dk_pairs[pair] + to_native(
                lax.dot_general(ds16[:256], qs, tn_dims, preferred_element_type=F32),
                lax.dot_general(ds16[256:], qs, tn_dims, preferred_element_type=F32), parity)
            dv_pairs[pair] = dv_pairs[pair] + to_native(
                lax.dot_general(p16[:256], dos, tn_dims, preferred_element_type=F32),
                lax.dot_general(p16[256:], dos, tn_dims, preferred_element_type=F32), parity)
        dsink_ref[...] += dsink
        dk_band = jnp.concatenate(dk_pairs, axis=1)
        dv_band = jnp.concatenate(dv_pairs, axis=1)
        cur = pl.multiple_of(n * ATT_BLOCK, ATT_BLOCK)
        dk_ref[pl.ds(cur, ATT_BLOCK), :] += dk_band[ATT_BLOCK:]
        dv_ref[pl.ds(cur, ATT_BLOCK), :] += dv_band[ATT_BLOCK:]

        @pl.when(n > 0)
        def _():
            prv = pl.multiple_of((n - 1) * ATT_BLOCK, ATT_BLOCK)
            dk_ref[pl.ds(prv, ATT_BLOCK), :] += dk_band[:ATT_BLOCK]
            dv_ref[pl.ds(prv, ATT_BLOCK), :] += dv_band[:ATT_BLOCK]

        @pl.when(n == nb - 1)
        def _():
            db_ref[:, 1024:1280] = jnp.sum(dk_ref[...], axis=0, keepdims=True)
            db_ref[:, 1280:1536] = jnp.sum(dv_ref[...], axis=0, keepdims=True)

        last()

    whole = lambda shape: pl.BlockSpec(shape, lambda n: (0, 0))
    return _call(
        body, hosted, name="attn_bwd", grid=(nb,), in_specs=_attn_specs(True),
        out_specs=[pl.BlockSpec((ATT_BLOCK, D_MODEL), lambda n: (n, 0)), whole((t, 256)), whole((t, 256)),
                   whole((1, QKV_DIM)), whole((8, 128))],
        out_shape=[jax.ShapeDtypeStruct((t, D_MODEL), MXU_DTYPE), jax.ShapeDtypeStruct((t, 256), F32),
                   jax.ShapeDtypeStruct((t, 256), F32), jax.ShapeDtypeStruct((1, QKV_DIM), F32),
                   jax.ShapeDtypeStruct((8, 128), F32)],
        operands=(sinks, qkv, qkv, qkv, qkv, qkv, do), semantics=("arbitrary",), scratch_shapes=_ATTN_TABLES)


FF_TILE = 1408


def _gathering_tile(j):
    me = 2 * lax.axis_index("x") + lax.axis_index("y")
    return me ^ (((j & 1) << 1) | (j >> 1))


def _ffn_in_gathering_call(hn, w_in, layer, hosted):
    t = hn.shape[0]
    tm = _tile(t, 1024)
    n_i = t // tm
    halves = D_FF // FF_TILE
    assert FF_TILE == W_IN_COLS and 2 * halves == N_CHIPS

    def body(ins, outs, scratch, first, last):
        hn_ref = ins[0]
        u_ref, w_ref = outs
        wbuf, load_sem, ici_send, ici_recv, hand_send, hand_recv = scratch
        j, i = pl.program_id(0), pl.program_id(1)
        x, y, c, chips = _mesh_place()
        me = 2 * x + y
        mover = c == layer
        first()

        def shard(k):
            return w_ref.at[layer, :, _col(k, W_IN_COLS)]

        def over_ici(jj, k):
            cx, cy = chips[jj]
            return pltpu.make_async_remote_copy(src_ref=shard(k), dst_ref=shard(k), send_sem=ici_send.at[jj],
                                                recv_sem=ici_recv.at[jj], device_id=(cx, cy, c), device_id_type=MESH)

        def handed(jj):
            cx, cy = chips[jj]
            return pltpu.make_async_remote_copy(
                src_ref=shard(2 * cx + cy), dst_ref=shard(2 * cx + cy), send_sem=hand_send.at[jj],
                recv_sem=hand_recv.at[jj], device_id=(x, y, 1 - c), device_id_type=MESH)

        @pl.when((j == 0) & (i == 0) & mover)
        def _():
            for jj in range(N_CHIPS - 1):
                over_ici(jj, me).start()

        for jj, (cx, cy) in enumerate(chips):
            @pl.when((j == jj + 1) & (i == 0) & mover)
            def _(jj=jj, cx=cx, cy=cy):
                over_ici(jj, 2 * cx + cy).wait_recv()
                handed(jj).start()

            @pl.when((j == jj + 1) & (i == 0) & jnp.logical_not(mover))
            def _(jj=jj):
                handed(jj).wait_recv()

        @pl.when(i == 0)
        def _():
            load = pltpu.make_async_copy(shard(_gathering_tile(j)), wbuf, load_sem.at[0])
            load.start()
            load.wait()

        u_ref[...] = jnp.dot(hn_ref[...].astype(MXU_DTYPE), wbuf[...].astype(MXU_DTYPE),
                             preferred_element_type=F32).astype(u_ref.dtype)

        @pl.when((j == N_CHIPS - 1) & (i == n_i - 1) & mover)
        def _():
            for jj in range(N_CHIPS - 1):
                over_ici(jj, me).wait_send()
                handed(jj).wait_send()

        last()

    def u_index(j, i):
        tile = _gathering_tile(j)
        return tile // halves, i, tile % halves

    three = [pltpu.SemaphoreType.DMA((N_CHIPS - 1,))] * 4
    return _call(
        body, hosted, name=f"ffn{layer}_in", grid=(N_CHIPS, n_i),
        in_specs=[pl.BlockSpec((tm, D_MODEL), lambda j, i: (i, 0)), ANY],
        out_specs=[pl.BlockSpec((None, tm, FF_TILE), u_index), ANY],
        out_shape=[jax.ShapeDtypeStruct((2, t, D_FF), U_DTYPE), jax.ShapeDtypeStruct(w_in.shape, w_in.dtype)],
        operands=(hn, w_in), semantics=("arbitrary", "arbitrary"), aliases={1: 1},
        scratch_shapes=[pltpu.VMEM((D_MODEL, W_IN_COLS), w_in.dtype), pltpu.SemaphoreType.DMA((1,))] + three)


def _ffn_out_loss_call(z, w_out, layer, h, g, target):
    t, f = z.shape
    tm = _tile(t, ROW_TILE)

    def body(z_ref, w_ref, h_ref, g_ref, t_ref, loss_ref, dh_ref, dh16_ref, dg_ref):
        x = h_ref[...] + jnp.dot(z_ref[...], w_ref[...], preferred_element_type=F32)
        gv = g_ref[...]
        err = _rms(x, gv) - t_ref[...]
        dx, dgt = _rms_bwd(x, gv, err * (1.0 / D_MODEL))
        dh_ref[...] = dx
        dh16_ref[...] = dx.astype(dh16_ref.dtype)

        @pl.when(pl.program_id(0) == 0)
        def _():
            dg_ref[...] = jnp.zeros_like(dg_ref)
            loss_ref[...] = jnp.zeros_like(loss_ref)

        dg_ref[...] += jnp.sum(dgt, axis=0, keepdims=True)
        per_token = jnp.mean(err * err, axis=-1, keepdims=True)
        loss_ref[...] += 0.5 * jnp.sum(per_token, axis=0, keepdims=True)

    row = pl.BlockSpec((tm, D_MODEL), lambda i: (i, 0))
    vec = pl.BlockSpec((1, D_MODEL), lambda i: (0, 0))
    return pl.pallas_call(
        body, name=f"ffn{layer}_out_loss", grid=(t // tm,),
        in_specs=[pl.BlockSpec((tm, f), lambda i: (i, 0)), pl.BlockSpec((None, f, D_MODEL), lambda i: (layer, 0, 0)),
                  row, vec, row],
        out_specs=[pl.BlockSpec((1, 1), lambda i: (0, 0)), row, row, vec],
        out_shape=[jax.ShapeDtypeStruct((1, 1), F32), jax.ShapeDtypeStruct((t, D_MODEL), F32),
                   jax.ShapeDtypeStruct((t, D_MODEL), MXU_DTYPE), jax.ShapeDtypeStruct((1, D_MODEL), F32)],
        compiler_params=_params("arbitrary"))(z, w_out, h, g, target)


def _ffn_fwd(hn, h, w, conv_w, conv_b, layer, hooks=None, next_norm=None, loss_head=None, gathers_w_in=False):
    t = hn.shape[0]
    tm = _tile(t, 1024)
    halves = D_FF // FF_TILE
    if gathers_w_in:
        u, w["w_in"] = _ride(hooks, "in", lambda hosted: _ffn_in_gathering_call(hn, w["w_in"], layer, hosted))
    else:
        u, = _ride(hooks, "in", lambda hosted: _matmul(
            hn, w["w_in"], mode="nn", name=f"ffn{layer}_in", grid=(2 * halves, t // tm, 1),
            a_spec=pl.BlockSpec((tm, D_MODEL), lambda j, i, k: (i, 0)),
            b_spec=pl.BlockSpec((None, D_MODEL, FF_TILE), lambda j, i, k: (layer, 0, j)),
            o_spec=pl.BlockSpec((None, tm, FF_TILE), lambda j, i, k: (j // halves, i, j % halves)),
            out_shape=jax.ShapeDtypeStruct((2, t, D_FF), U_DTYPE), hosted=hosted))
    z, = _ride(hooks, "glu", lambda hosted: _conv_glu_fwd_call(u, conv_w, conv_b, f"ffn{layer}_glu", hosted=hosted))
    if loss_head is not None:
        return (*_ffn_out_loss_call(z, w["w_out"], layer, h, *loss_head), u, z)
    outs = _ride(hooks, "out", lambda hosted: _mm_nn(z, w["w_out"], f"ffn{layer}_out", b_lead=layer, residual=h,
                                                     tk=D_FF, hosted=hosted, norm_g=next_norm))
    return outs[0], (outs[1] if next_norm is not None else None), u, z


def _ffn_bwd(dh, hn, u, z, w_in, w_out, conv_w, conv_b, layer, g_in_acc, g_out_acc, norm, hooks=None):
    t = hn.shape[0]
    dz, = _ride(hooks, "dz", lambda hosted: _mm_nt(dh, w_out, f"ffn{layer}_dz", b_lead=layer, out_dtype=MXU_DTYPE,
                                                   tn=FF_TILE, hosted=hosted))
    g_out = _mm_tn(z, dh, f"ffn{layer}_dwout", lead=layer, n_lead=2, into=g_out_acc, tm=WGRAD_TILE, tk=t,
                   out_dtype=WIRE_DTYPE)
    du, dcw, dcb = _ride(hooks, "dglu", lambda hosted: _conv_glu_bwd_call(u, dz, conv_w, conv_b, f"ffn{layer}_dglu",
                                                                          hosted=hosted), {"g_out": g_out})
    per_half = D_FF // WGRAD_TILE
    g_in, = _ride(hooks, "dwin", lambda hosted: _matmul(
        hn, du, mode="tn", name=f"ffn{layer}_dwin", grid=(1, 2 * per_half, 1),
        a_spec=pl.BlockSpec((t, D_MODEL), lambda i, j, k: (0, 0)),
        b_spec=pl.BlockSpec((None, t, WGRAD_TILE), lambda i, j, k: (j // per_half, 0, j % per_half)),
        o_spec=pl.BlockSpec((None, D_MODEL, WGRAD_TILE), lambda i, j, k: (layer, 0, j)),
        out_shape=jax.ShapeDtypeStruct((2, D_MODEL, 2 * D_FF), WIRE_DTYPE), into=g_in_acc, hosted=hosted))
    dx, dx16, dg = _ride(hooks, "dhn", lambda hosted: _ffn_dhn_call(du, w_in, layer, norm, hosted),
                         {"g_in": g_in, "g_out": g_out})
    return dx, dx16, dg, g_in, g_out, dcw, dcb


WGRAD_TILE = 256


_NT_DIMS = (((1,), (1,)), ((), ()))


def _dhn_call(name, a, a_spec, w, w_spec, product, t, tm, norm, hosted=None):
    h, g, dres = norm

    def body(ins, outs, scratch, first, last):
        a_ref, w_ref, h_ref, g_ref, dres_ref = ins
        dx_ref, dx16_ref, dg_ref = outs
        first()
        dx, dgt = _rms_bwd(h_ref[...], g_ref[...], product(a_ref, w_ref))
        dx = dres_ref[...] + dx
        dx_ref[...] = dx
        dx16_ref[...] = dx.astype(dx16_ref.dtype)

        @pl.when(pl.program_id(0) == 0)
        def _():
            dg_ref[...] = jnp.zeros_like(dg_ref)

        dg_ref[...] += jnp.sum(dgt, axis=0, keepdims=True)
        last()

    row = pl.BlockSpec((tm, D_MODEL), lambda i: (i, 0))
    vec = pl.BlockSpec((1, D_MODEL), lambda i: (0, 0))
    return _call(
        body, hosted, name=name, grid=(t // tm,), in_specs=[a_spec, w_spec, row, vec, row], out_specs=[row, row, vec],
        out_shape=[jax.ShapeDtypeStruct((t, D_MODEL), F32), jax.ShapeDtypeStruct((t, D_MODEL), MXU_DTYPE),
                   jax.ShapeDtypeStruct((1, D_MODEL), F32)],
        operands=(a, w, h, g, dres), semantics=("arbitrary",))


def _ffn_dhn_call(du, w_in, layer, norm, hosted=None):
    _, t, f = du.shape
    tm = _tile(t, 512)

    def product(du_ref, w_ref):
        return (lax.dot_general(du_ref[0], w_ref[:, :f], _NT_DIMS, preferred_element_type=F32)
                + lax.dot_general(du_ref[1], w_ref[:, f:], _NT_DIMS, preferred_element_type=F32))

    return _dhn_call(f"ffn{layer}_dhn", du, pl.BlockSpec((2, tm, f), lambda i: (0, i, 0)), w_in,
                     pl.BlockSpec((None, D_MODEL, 2 * f), lambda i: (layer, 0, 0)), product, t, tm, norm, hosted)


def _attn_dhn_call(dqkv, w_qkv, norm):
    t, n = dqkv.shape
    tm = _tile(t, 1024)

    def product(a_ref, w_ref):
        return lax.dot_general(a_ref[...], w_ref[...], _NT_DIMS, preferred_element_type=F32)

    return _dhn_call("attn_dhn", dqkv, pl.BlockSpec((tm, n), lambda i: (i, 0)), w_qkv,
                     pl.BlockSpec((D_MODEL, n), lambda i: (0, 0)), product, t, tm, norm)


def _local_step(x, target, w, *, norm_mix, norm_ffn, norm_f, pool_b, pool_scale, b_qkv, sinks, conv_w, conv_b,
                late_gather=None, reducer=None):
    cw = [conv_w[l].reshape(3, 2, D_FF) for l in range(2)]
    cb = [conv_b[l].reshape(2, D_FF) for l in range(2)]
    gm = [norm_mix[l:l + 1] for l in range(2)]
    gf = [norm_ffn[l:l + 1] for l in range(2)]

    w = dict(w)
    fwd_hooks = {}
    if late_gather:
        def gather_hook(stage):
            return (lambda env: late_gather(stage, w)), (lambda got: w.update(zip(_late_names(stage), got)))

        fwd_hooks = {stage: gather_hook(stage) for stage in GATHER_RIDES}

    h1, hn_f0 = _ride(fwd_hooks, "pool_fwd", lambda hosted: _pool_fwd_call(x, gm[0], w["pool_w"], pool_b, pool_scale,
                                                                            gf[0], hosted))
    h2, hn_a, u0, z0 = _ffn_fwd(hn_f0, h1, w, cw[0], cb[0], 0, hooks=fwd_hooks, next_norm=gm[1],
                                gathers_w_in=bool(late_gather))
    qkv = _mm_nn(hn_a, w["w_qkv"], "attn_qkv", bias=b_qkv, out_dtype=MXU_DTYPE, tn=768)
    o, = _ride(fwd_hooks, "attn_fwd", lambda hosted: _attn_fwd_call(qkv, sinks, hosted))
    h3, hn_f1 = _ride(fwd_hooks, "attn_out", lambda hosted: _mm_nn(o, w["w_o"], "attn_out", residual=h2, hosted=hosted,
                                                                   norm_g=gf[1]))
    loss, dh, dh16, d_norm_f, u1, z1 = _ffn_fwd(hn_f1, h3, w, cw[1], cb[1], 1, loss_head=(norm_f, target))

    rd = reducer
    first, mid, last = _Reducer.FIRST, _Reducer.MID, _Reducer.LAST
    hooks1 = hooks_attn = hooks0 = hooks_pool = None
    if rd:
        hooks1 = {"dhn": (lambda env: rd.swap(first, {"w_in": env["g_in"], "w_out": env["g_out"]}),
                          lambda got: rd.pair_sums(first, got))}
        hooks_attn = {"attn_bwd": (lambda env: rd.scatter(first), lambda got: rd.chip_sums(first, got))}

        def mid_and_last(got):
            rd.shared(mid, got[:len(mid)])
            rd.pair_sums(last, got[len(mid):])

        hooks0 = {
            "dz": (lambda env: rd.share(first), lambda got: rd.shared(first, got)),
            "dglu": (lambda env: rd.swap(mid, {"w_out": env["g_out"], "w_qkv": g_qkv, "w_o": g_o4}),
                     lambda got: rd.pair_sums(mid, got)),
            "dwin": (lambda env: rd.scatter(mid), lambda got: rd.chip_sums(mid, got)),
            "dhn": (lambda env: _both(rd.share(mid), rd.swap(last, {"w_in": env["g_in"]})), mid_and_last),
        }
        hooks_pool = {"pool_bwd": (lambda env: rd.scatter(last), lambda got: rd.chip_sums(last, got))}
    dh, dh16, d_gf1, g_in, g_out, dcw1, dcb1 = _ffn_bwd(
        dh16, hn_f1, u1, z1, w["w_in"], w["w_out"], cw[1], cb[1], 1, None, None, (h3, gf[1], dh), hooks=hooks1)
    do = _mm_nt(dh16, w["w_o"], "attn_do", out_dtype=MXU_DTYPE)
    g_o = _mm_tn(o, dh16, "attn_dwo", tn=WGRAD_TILE, tk=x.shape[0], out_dtype=WIRE_DTYPE)
    g_o4 = g_o.reshape(N_CHIPS, 2, W_O_ROWS // 2, D_MODEL)
    dq, dk, dv, d_bqkv, d_sinks = _ride(hooks_attn, "attn_bwd", lambda hosted: _attn_bwd_call(qkv, sinks, do, hosted))
    dqkv = jnp.concatenate([dq, dk.astype(MXU_DTYPE), dv.astype(MXU_DTYPE)], axis=1)
    g_qkv = _mm_tn(hn_a, dqkv, "attn_dwqkv", tn=WGRAD_TILE, tk=x.shape[0], out_dtype=WIRE_DTYPE)
    dh, dh16, d_gm1 = _attn_dhn_call(dqkv, w["w_qkv"], (h2, gm[1], dh))
    dh, _, d_gf0, g_in, g_out, dcw0, dcb0 = _ffn_bwd(
        dh16, hn_f0, u0, z0, w["w_in"], w["w_out"], cw[0], cb[0], 0, g_in, g_out, (h1, gf[0], dh), hooks=hooks0)
    grad_x, g_pool, dvec = _ride(hooks_pool, "pool_bwd", lambda hosted: _pool_bwd_call(
        x, dh, gm[0], w["pool_w"], pool_b, pool_scale, hosted))
    if rd:
        rd.shared(last, _alone(rd.share(last), "share_last"))

    grads = {
        "w_in": g_in, "w_out": g_out, "w_qkv": g_qkv, "w_o": g_o, "pool_w": g_pool,
        "pool_b": dvec[0:1], "pool_scale": dvec[1:2], "b_qkv": d_bqkv, "sinks": d_sinks[0:1, :N_HEADS],
        "norm_mix": jnp.concatenate([dvec[2:3], d_gm1], axis=0), "norm_ffn": jnp.concatenate([d_gf0, d_gf1], axis=0),
        "conv_w": jnp.stack([dcw0.reshape(3, 2 * D_FF), dcw1.reshape(3, 2 * D_FF)]),
        "conv_b": jnp.stack([dcb0.reshape(2 * D_FF), dcb1.reshape(2 * D_FF)]), "norm_f": d_norm_f,
    }
    return loss, grad_x, grads


W_IN_COLS = 2 * D_FF // N_CHIPS
W_OUT_ROWS = D_FF // N_CHIPS
QKV_COLS = QKV_DIM // N_CHIPS
W_O_ROWS = D_MODEL // N_CHIPS
POOL_ROWS = POOL_GC // N_CHIPS
BIG = ("w_in", "w_out", "w_qkv", "w_o", "pool_w")


def _mesh_place():
    x, y, c = lax.axis_index("x"), lax.axis_index("y"), lax.axis_index("c")
    chips = [(1 - x, y), (x, 1 - y), (1 - x, 1 - y)]
    return x, y, c, chips


def _col(k, width):
    return pl.ds(pl.multiple_of(k * width, 128), width)


def _row(k, height):
    return pl.ds(pl.multiple_of(k * height, 16), height)


def _full_piece(name, ref, k, h):
    if name == "w_in":
        return ref.at[h, :, _col(k, W_IN_COLS)]
    if name == "w_out":
        return ref.at[h, _row(k, W_OUT_ROWS), :]
    if name == "w_qkv":
        return ref.at[_row(h, D_MODEL // 2), _col(k, QKV_COLS)]
    if name == "w_o":
        return ref.at[_row(2 * k + h, W_O_ROWS // 2), :]
    return ref.at[pl.ds(2 * h, 2), _row(k, POOL_ROWS), :]


def _shard_half(name, ref, h):
    if name in ("w_in", "w_out"):
        return ref.at[h]
    if name == "w_qkv":
        return ref.at[_row(h, D_MODEL // 2), :]
    if name == "w_o":
        return ref.at[_row(h, W_O_ROWS // 2), :]
    return ref.at[pl.ds(2 * h, 2)]


FULL_SHAPES = {"w_in": (2, D_MODEL, 2 * D_FF), "w_out": (2, D_FF, D_MODEL), "w_qkv": (D_MODEL, QKV_DIM),
               "w_o": (D_MODEL, D_MODEL), "pool_w": (4, POOL_GC, POOL_GC)}
SHARD_SHAPES = {"w_in": (2, D_MODEL, W_IN_COLS), "w_out": (2, W_OUT_ROWS, D_MODEL), "w_qkv": (D_MODEL, QKV_COLS),
                "w_o": (W_O_ROWS, D_MODEL), "pool_w": (4, POOL_ROWS, POOL_GC)}
HALF_SHAPES = {"w_in": (D_MODEL, W_IN_COLS), "w_out": (W_OUT_ROWS, D_MODEL), "w_qkv": (D_MODEL // 2, QKV_COLS),
               "w_o": (W_O_ROWS // 2, D_MODEL), "pool_w": (2, POOL_ROWS, POOL_GC)}


def _place_shard(name, shard, me_arr):
    if name == "w_in":
        blk, grid = (1, 256, W_IN_COLS), (2, D_MODEL // 256)
        src, dst = (lambda a, b, me: (a, b, 0)), (lambda a, b, me: (a, b, me[0]))
    elif name == "w_out":
        blk, grid = (1, W_OUT_ROWS, D_MODEL), (2, 1)
        src, dst = (lambda a, b, me: (a, 0, 0)), (lambda a, b, me: (a, me[0], 0))
    elif name == "w_qkv":
        blk, grid = (256, QKV_COLS), (1, D_MODEL // 256)
        src, dst = (lambda a, b, me: (b, 0)), (lambda a, b, me: (b, me[0]))
    elif name == "w_o":
        blk, grid = (W_O_ROWS, D_MODEL), (1, 1)
        src, dst = (lambda a, b, me: (0, 0)), (lambda a, b, me: (me[0], 0))
    else:
        blk, grid = (4, POOL_ROWS, POOL_GC), (1, 1)
        src, dst = (lambda a, b, me: (0, 0, 0)), (lambda a, b, me: (0, me[0], 0))

    def body(me_ref, s_ref, o_ref):
        o_ref[...] = s_ref[...].astype(o_ref.dtype)

    return pl.pallas_call(
        body, name=f"place_{name}",
        grid_spec=pltpu.PrefetchScalarGridSpec(num_scalar_prefetch=1, grid=grid, in_specs=[pl.BlockSpec(blk, src)],
                                               out_specs=pl.BlockSpec(blk, dst)),
        out_shape=jax.ShapeDtypeStruct(FULL_SHAPES[name], WIRE_DTYPE), compiler_params=_params("parallel", "parallel"),
    )(me_arr, shard)


def _when_mover(half, c, fn):
    if half is None:
        fn(c)
    else:
        pl.when(c == half)(lambda: fn(half))


def _when_taker(half, c, fn):
    if half is None:
        fn(1 - c)
    else:
        pl.when(c != half)(lambda: fn(half))


def _remote(ref, sems, t, j, dev):
    send, recv = sems
    return pltpu.make_async_remote_copy(src_ref=ref, dst_ref=ref, send_sem=send.at[t, j], recv_sem=recv.at[t, j],
                                        device_id=dev, device_id_type=MESH)


def _gather_ici(items, refs, sems, finish):
    x, y, c, chips = _mesh_place()
    me = 2 * x + y
    for t, (nm, half) in enumerate(items):
        def go(h, t=t, nm=nm):
            mine = _full_piece(nm, refs[t], me, h)
            for j, (cx, cy) in enumerate(chips):
                if not finish:
                    _remote(mine, sems, t, j, (cx, cy, c)).start()
                else:
                    _remote(_full_piece(nm, refs[t], 2 * cx + cy, h), sems, t, j, (cx, cy, c)).wait_recv()
                    _remote(mine, sems, t, j, (cx, cy, c)).wait_send()

        _when_mover(half, c, go)


def _gather_pass(items, refs, sems, finish):
    x, y, c, chips = _mesh_place()
    sibling = (x, y, 1 - c)
    for t, (nm, half) in enumerate(items):
        def give(h, t=t, nm=nm):
            for j, (cx, cy) in enumerate(chips):
                got = _remote(_full_piece(nm, refs[t], 2 * cx + cy, h), sems, t, j, sibling)
                if not finish:
                    got.start()
                else:
                    got.wait_send()

        def take(h, t=t, nm=nm):
            for j, (cx, cy) in enumerate(chips):
                _remote(_full_piece(nm, refs[t], 2 * cx + cy, h), sems, t, j, sibling).wait_recv()

        _when_mover(half, c, give)
        if finish:
            _when_taker(half, c, take)


def _gather_sems(n_items):
    return [pltpu.SemaphoreType.DMA((n_items, 3)), pltpu.SemaphoreType.DMA((n_items, 3))]


EARLY_ITEMS = (("pool_w", None),)
_LATE_A = (("w_out", 0),)
_LATE_B = (("w_in", 1),)
_LATE_C = (("w_out", 1),)
_LATE_D = (("w_qkv", None), ("w_o", None))
GATHER_RIDES = {"pool_fwd": ((), _LATE_A, ()), "in": (_LATE_A, (), ()), "glu": ((), _LATE_B, ()),
                "out": (_LATE_B, (), _LATE_D), "attn_fwd": ((), _LATE_C, ()), "attn_out": (_LATE_C, (), ())}


def _names(items):
    return [nm for nm, _ in items]


def _gather_early(full, small):
    items = EARLY_ITEMS
    n = len(items)

    def body(*refs):
        small_in = refs[n]
        outs, small_out = refs[n + 1:2 * n + 1], refs[2 * n + 1]
        scratch = refs[2 * n + 2:]
        ici_sems, pass_sems = scratch[0:2], scratch[2:4]
        local_sem, small_send, small_recv = scratch[4:]
        x, y, c, chips = _mesh_place()
        me = 2 * x + y
        local = pltpu.make_async_copy(small_in, small_out.at[me], local_sem.at[0])
        local.start()
        small_sends = [pltpu.make_async_remote_copy(
            src_ref=small_in, dst_ref=small_out.at[me], send_sem=small_send.at[j], recv_sem=small_recv.at[j],
            device_id=(cx, cy, c), device_id_type=MESH) for j, (cx, cy) in enumerate(chips)]
        for cp in small_sends:
            cp.start()
        _gather_ici(items, outs, ici_sems, finish=False)
        _gather_ici(items, outs, ici_sems, finish=True)
        _gather_pass(items, outs, pass_sems, finish=False)
        _gather_pass(items, outs, pass_sems, finish=True)
        for j, (cx, cy) in enumerate(chips):
            got = small_out.at[2 * cx + cy]
            pltpu.make_async_remote_copy(src_ref=got, dst_ref=got, send_sem=small_send.at[j],
                                         recv_sem=small_recv.at[j], device_id=(cx, cy, c), device_id_type=MESH).wait_recv()
        for cp in small_sends:
            cp.wait_send()
        local.wait()

    arrays = [full[nm] for nm in _names(items)]
    out_shape = [jax.ShapeDtypeStruct(a.shape, a.dtype) for a in arrays]
    out_shape.append(jax.ShapeDtypeStruct((N_CHIPS,) + small.shape, F32))
    res = pl.pallas_call(
        body, name="gather_early", in_specs=[ANY] * (n + 1), out_specs=[ANY] * (n + 1), out_shape=out_shape,
        input_output_aliases={t: t for t in range(n)},
        scratch_shapes=_gather_sems(n) + _gather_sems(n) + [pltpu.SemaphoreType.DMA((1,)), pltpu.SemaphoreType.DMA((3,)),
                                                            pltpu.SemaphoreType.DMA((3,))],
    )(*arrays, small)
    return {**full, **dict(zip(_names(items), res[:n]))}, res[n]


def _late_gather_host(stage, w):
    passing, sending, direct = GATHER_RIDES[stage]
    n_pass, n_send = len(passing), len(sending)

    def run(src, refs, new, sems, finish):
        if passing:
            _gather_pass(passing, refs[:n_pass], sems[:2], finish)
        if sending:
            _gather_ici(sending, refs[n_pass:n_pass + n_send], sems[2 * bool(passing):2 * bool(passing) + 2], finish)
        if direct:
            _gather_direct(direct, refs[n_pass + n_send:], sems[-2:], finish)

    sems = ((_gather_sems(n_pass) if passing else []) + (_gather_sems(n_send) if sending else [])
            + ([pltpu.SemaphoreType.DMA((len(direct), 6))] * 2 if direct else []))
    return _Hosted([], [w[nm] for nm in _late_names(stage)], [], sems, functools.partial(run, finish=False),
                   functools.partial(run, finish=True))


def _late_names(stage):
    return [nm for items in GATHER_RIDES[stage] for nm in _names(items)]


def _gather_direct(items, refs, sems, finish):
    x, y, c, chips = _mesh_place()
    me = 2 * x + y
    for t, (nm, half) in enumerate(items):
        assert half is None
        mine = _full_piece(nm, refs[t], me, c)
        for j, (cx, cy) in enumerate(chips):
            for other in range(2):
                pc = c if other == 0 else 1 - c
                if not finish:
                    _remote(mine, sems, t, 2 * j + other, (cx, cy, pc)).start()
                else:
                    _remote(_full_piece(nm, refs[t], 2 * cx + cy, pc), sems, t, 2 * j + other, (cx, cy, pc)).wait_recv()
                    _remote(mine, sems, t, 2 * j + other, (cx, cy, pc)).wait_send()


def _other_half_src(name, ref, h):
    if name in ("w_in", "w_out"):
        return ref.at[h]
    if name == "w_qkv":
        return ref.at[_row(h, D_MODEL // 2), :]
    if name == "w_o":
        return ref.at[:, pl.ds(h, 1)]
    return ref.at[pl.ds(2 * h, 2)]


SLAB_SHAPES = {"w_in": (D_MODEL, 2 * D_FF), "w_out": (D_FF, D_MODEL), "w_qkv": (D_MODEL // 2, QKV_DIM),
               "w_o": (N_CHIPS, 1, W_O_ROWS // 2, D_MODEL), "pool_w": (2, POOL_GC, POOL_GC)}


def _item_tag(item):
    return item[0] if item[1] is None else f"{item[0]}{item[1]}"


def _reduce_swap(items, grads, slabs, sems, finish):
    x, y, c, _ = _mesh_place()
    send, recv = sems
    for t, (nm, half) in enumerate(items):
        def copy(h, t=t, nm=nm):
            return pltpu.make_async_remote_copy(
                src_ref=_other_half_src(nm, grads[t], h), dst_ref=slabs[t], send_sem=send.at[t], recv_sem=recv.at[t],
                device_id=(x, y, 1 - c), device_id_type=MESH)

        def give(h):
            if finish:
                copy(h).wait_send()
            else:
                copy(h).start()

        if half is None:
            give(1 - c)
            if finish:
                copy(c).wait_recv()
        else:
            pl.when(c != half)(lambda: give(half))
            if finish:
                pl.when(c == half)(lambda: copy(half).wait_recv())


def _reduce_scatter(items, sums, parts, sems, finish):
    _, _, c, chips = _mesh_place()
    send, recv = sems
    for t, (nm, half) in enumerate(items):
        def go(h, t=t, nm=nm):
            for j, (cx, cy) in enumerate(chips):
                cp = pltpu.make_async_remote_copy(
                    src_ref=_slab_piece(nm, sums[t], 2 * cx + cy), dst_ref=parts[t].at[j], send_sem=send.at[t, j],
                    recv_sem=recv.at[t, j], device_id=(cx, cy, c), device_id_type=MESH)
                if finish:
                    cp.wait_recv()
                    cp.wait_send()
                else:
                    cp.start()

        _when_mover(half, c, go)


def _reduce_share(items, shards, sems, finish):
    x, y, c, _ = _mesh_place()
    send, recv = sems
    for t, (nm, half) in enumerate(items):
        def copy(h, t=t, nm=nm):
            part = _shard_half(nm, shards[t], h)
            return pltpu.make_async_remote_copy(src_ref=part, dst_ref=part, send_sem=send.at[t], recv_sem=recv.at[t],
                                                device_id=(x, y, 1 - c), device_id_type=MESH)

        def give(h):
            if finish:
                copy(h).wait_send()
            else:
                copy(h).start()

        _when_mover(half, c, give)
        if finish:
            _when_taker(half, c, lambda h: copy(h).wait_recv())


def _pair_sems(n):
    return [pltpu.SemaphoreType.DMA((n,)), pltpu.SemaphoreType.DMA((n,))]


def _swap_host(items, grads):
    def run(src, arr, new, sems, finish):
        _reduce_swap(items, src, new, sems, finish)

    return _Hosted([grads[nm] for nm, _ in items], [],
                   [jax.ShapeDtypeStruct(SLAB_SHAPES[nm], grads[nm].dtype) for nm, _ in items],
                   _pair_sems(len(items)), functools.partial(run, finish=False), functools.partial(run, finish=True))


def _scatter_host(items, sums):
    def run(src, arr, new, sems, finish):
        _reduce_scatter(items, src, new, sems, finish)

    new = [jax.ShapeDtypeStruct((N_CHIPS - 1,) + HALF_SHAPES[nm], WIRE_DTYPE) for nm, _ in items]
    return _Hosted(list(sums), [], new, _gather_sems(len(items)), functools.partial(run, finish=False),
                   functools.partial(run, finish=True))


def _share_host(items, shards):
    def run(src, arr, new, sems, finish):
        _reduce_share(items, arr, sems, finish)

    return _Hosted([], list(shards), [], _pair_sems(len(items)), functools.partial(run, finish=False),
                   functools.partial(run, finish=True))


def _alone(hosted, name):
    def body(ins, outs, scratch, first, last):
        first()
        last()

    return _call(body, hosted, name=name, grid=(1,), in_specs=[], out_specs=[], out_shape=[], operands=(),
                 semantics=("arbitrary",))


def _pair_sum(item, grad, slab, place):
    name, half = item
    shape = SLAB_SHAPES[name]
    hsel = (lambda p: p[1]) if half is None else (lambda p: half)
    if name == "w_in":
        blk = (256, 2 * D_FF)
        grid = (D_MODEL // 256,)
        g_spec = pl.BlockSpec((None,) + blk, lambda i, p: (hsel(p), i, 0))
        s_spec = pl.BlockSpec(blk, lambda i, p: (i, 0))
    elif name == "w_out":
        blk = (W_OUT_ROWS, D_MODEL)
        grid = (N_CHIPS,)
        g_spec = pl.BlockSpec((None,) + blk, lambda i, p: (hsel(p), i, 0))
        s_spec = pl.BlockSpec(blk, lambda i, p: (i, 0))
    elif name == "w_qkv":
        grid = (1,)
        g_spec = pl.BlockSpec(shape, lambda i, p: (hsel(p), 0))
        s_spec = pl.BlockSpec(shape, lambda i, p: (0, 0))
    else:
        grid = (1,)
        g_spec = pl.BlockSpec(shape, lambda i, p: (0, hsel(p), 0, 0))
        s_spec = pl.BlockSpec(shape, lambda i, p: (0, 0, 0, 0))

    def body(p_ref, g_ref, s_ref, o_ref):
        @pl.when(hsel(p_ref) == p_ref[1])
        def _():
            o_ref[...] = (g_ref[...].astype(F32) + s_ref[...].astype(F32)).astype(o_ref.dtype)

    return pl.pallas_call(
        body, name=f"pair_sum_{_item_tag(item)}",
        grid_spec=pltpu.PrefetchScalarGridSpec(num_scalar_prefetch=1, grid=grid, in_specs=[g_spec, s_spec],
                                               out_specs=s_spec),
        out_shape=jax.ShapeDtypeStruct(shape, WIRE_DTYPE), compiler_params=_params("parallel"),
    )(place, grad, slab)


def _slab_piece(name, ref, k):
    if name == "w_in":
        return ref.at[:, _col(k, W_IN_COLS)]
    if name == "w_out":
        return ref.at[_row(k, W_OUT_ROWS), :]
    if name == "w_qkv":
        return ref.at[:, _col(k, QKV_COLS)]
    if name == "w_o":
        return ref.at[k, 0]
    return ref.at[:, _row(k, POOL_ROWS), :]


def _sum_chips(item, slab, parts, place, into=None):
    name, sel = item
    half = HALF_SHAPES[name]
    hsel = (lambda p: p[1]) if sel is None else (lambda p: sel)
    if name == "w_in":
        blk, grid = (256, W_IN_COLS), (D_MODEL // 256,)
        own = pl.BlockSpec(blk, lambda i, p: (i, p[0]))
        got = pl.BlockSpec((N_CHIPS - 1,) + blk, lambda i, p: (0, i, 0))
        out = pl.BlockSpec((None,) + blk, lambda i, p: (hsel(p), i, 0))
    elif name == "w_out":
        grid = (1,)
        own = pl.BlockSpec(half, lambda i, p: (p[0], 0))
        got = pl.BlockSpec((N_CHIPS - 1,) + half, lambda i, p: (0, 0, 0))
        out = pl.BlockSpec((None,) + half, lambda i, p: (hsel(p), 0, 0))
    elif name == "w_qkv":
        grid = (1,)
        own = pl.BlockSpec(half, lambda i, p: (0, p[0]))
        got = pl.BlockSpec((N_CHIPS - 1,) + half, lambda i, p: (0, 0, 0))
        out = pl.BlockSpec(half, lambda i, p: (hsel(p), 0))
    else:
        grid = (1,)
        own = pl.BlockSpec((None, None) + half, lambda i, p: (p[0], 0, 0, 0))
        got = pl.BlockSpec((N_CHIPS - 1,) + half, lambda i, p: (0, 0, 0))
        out = pl.BlockSpec(half, lambda i, p: (hsel(p), 0))

    def body(p_ref, own_ref, got_ref, *rest):
        o_ref = rest[-1]

        @pl.when(hsel(p_ref) == p_ref[1])
        def _():
            acc = own_ref[...].astype(F32)
            for j in range(N_CHIPS - 1):
                acc = acc + got_ref[j].astype(F32)
            o_ref[...] = acc

    in_specs, operands, aliases = [own, got], [place, slab, parts], {}
    if into is not None:
        in_specs.append(ANY)
        operands.append(into)
        aliases = {3: 0}
    return pl.pallas_call(
        body, name=f"sum_chips_{_item_tag(item)}",
        grid_spec=pltpu.PrefetchScalarGridSpec(num_scalar_prefetch=1, grid=grid, in_specs=in_specs, out_specs=out),
        out_shape=jax.ShapeDtypeStruct(SHARD_SHAPES[name], F32), input_output_aliases=aliases,
        compiler_params=_params("parallel"),
    )(*operands)


class _Reducer:
    FIRST = (("w_in", 1), ("w_out", 1))
    MID = (("w_out", 0), ("w_qkv", None), ("w_o", None))
    LAST = (("w_in", 0),)

    def __init__(self, place):
        self.place = place
        self.shards = {}
        self.state = {}

    def swap(self, items, grads):
        self.state[items] = {"grads": [grads[nm] for nm, _ in items]}
        return _swap_host(items, grads)

    def pair_sums(self, items, slabs):
        st = self.state[items]
        st["sums"] = [_pair_sum(it, g, s, self.place) for it, g, s in zip(items, st["grads"], slabs)]

    def scatter(self, items):
        return _scatter_host(items, self.state[items]["sums"])

    def chip_sums(self, items, parts):
        for it, s, p in zip(items, self.state[items]["sums"], parts):
            self.shards[it[0]] = _sum_chips(it, s, p, self.place, into=self.shards.get(it[0]))

    def share(self, items):
        return _share_host(items, [self.shards[nm] for nm, _ in items])

    def shared(self, items, updated):
        self.shards.update(dict(zip(_names(items), updated)))


N_DEV = 8


def _all_reduce_small(vec, g_pool):
    def body(v_ref, gp_ref, o_ref, po_ref, slots, pslots, send_sems, recv_sems, psend, precv, hsend, hrecv):
        x, y, c, _ = _mesh_place()
        me = 4 * x + 2 * y + c

        def pool_piece(chip, h):
            return gp_ref.at[pl.ds(2 * h, 2), pl.ds(pl.multiple_of(chip * POOL_ROWS, POOL_ROWS), POOL_ROWS), :]

        slots[me] = v_ref[...]
        pslots[me] = pool_piece(2 * x + y, c)[...]
        copies = []
        for d in range(1, N_DEV):
            peer = (x ^ (d >> 2), y ^ ((d >> 1) & 1), c ^ (d & 1))
            copies.append(pltpu.make_async_remote_copy(
                src_ref=v_ref, dst_ref=slots.at[me], send_sem=send_sems.at[d - 1], recv_sem=recv_sems.at[d - 1],
                device_id=peer, device_id_type=MESH))
            copies.append(pltpu.make_async_remote_copy(
                src_ref=pool_piece(2 * peer[0] + peer[1], peer[2]), dst_ref=pslots.at[me], send_sem=psend.at[d - 1],
                recv_sem=precv.at[d - 1], device_id=peer, device_id_type=MESH))
        for cp in copies:
            cp.start()
        for d in range(1, N_DEV):
            peer = 4 * (x ^ (d >> 2)) + 2 * (y ^ ((d >> 1) & 1)) + (c ^ (d & 1))
            for buf, ss, rs in ((slots, send_sems, recv_sems), (pslots, psend, precv)):
                got = buf.at[peer]
                pltpu.make_async_remote_copy(src_ref=got, dst_ref=got, send_sem=ss.at[d - 1], recv_sem=rs.at[d - 1],
                                             device_id=(x, y, c), device_id_type=MESH).wait_recv()
        for cp in copies:
            cp.wait_send()
        acc, pacc = slots[0], pslots[0]
        for d in range(1, N_DEV):
            acc, pacc = acc + slots[d], pacc + pslots[d]
        o_ref[...] = acc
        po_ref[pl.ds(2 * c, 2)] = pacc
        mine, theirs = po_ref.at[pl.ds(2 * c, 2)], po_ref.at[pl.ds(2 * (1 - c), 2)]
        give = pltpu.make_async_remote_copy(src_ref=mine, dst_ref=mine, send_sem=hsend.at[0], recv_sem=hrecv.at[0],
                                            device_id=(x, y, 1 - c), device_id_type=MESH)
        give.start()
        pltpu.make_async_remote_copy(src_ref=theirs, dst_ref=theirs, send_sem=hsend.at[0], recv_sem=hrecv.at[0],
                                     device_id=(x, y, 1 - c), device_id_type=MESH).wait_recv()
        give.wait_send()

    vm = pl.BlockSpec(memory_space=pltpu.VMEM)
    piece = (2, POOL_ROWS, POOL_GC)
    return pl.pallas_call(
        body, name="all_reduce_small", in_specs=[vm, vm], out_specs=[vm, vm],
        out_shape=[jax.ShapeDtypeStruct(vec.shape, F32), jax.ShapeDtypeStruct(SHARD_SHAPES["pool_w"], F32)],
        scratch_shapes=[pltpu.VMEM((N_DEV,) + vec.shape, F32), pltpu.VMEM((N_DEV,) + piece, F32)]
        + [pltpu.SemaphoreType.DMA((N_DEV - 1,))] * 4 + [pltpu.SemaphoreType.DMA((1,))] * 2,
    )(vec, g_pool)


def _adamw_call(w, g, m, v, name, copy_grad=False):
    shape = w.shape
    cols = shape[-1]
    rows = w.size // cols
    tr = rows
    for cand in (256, 128, 64, 32, 16, 8):
        if rows > cand and rows % cand == 0 and cand * cols * 4 <= 2 * 1024 * 1024:
            tr = cand
            break

    def body(w_ref, g_ref, m_ref, v_ref, d_ref, nm_ref, nv_ref, *g_out):
        gv = g_ref[...]
        if copy_grad:
            g_out[0][...] = gv
        mn = ADAM_B1 * m_ref[...] + (1.0 - ADAM_B1) * gv
        vn = ADAM_B2 * v_ref[...] + (1.0 - ADAM_B2) * jnp.square(gv)
        m_hat = mn / (1.0 - ADAM_B1 ** ADAM_STEP)
        v_hat = vn / (1.0 - ADAM_B2 ** ADAM_STEP)
        d_ref[...] = -ADAM_LR * (m_hat / (jnp.sqrt(v_hat) + ADAM_EPS) + ADAM_WD * w_ref[...])
        nm_ref[...] = mn
        nv_ref[...] = vn

    spec = pl.BlockSpec((tr, cols), lambda i: (i, 0))
    flat = lambda a: a.reshape(rows, cols)
    n_out = 4 if copy_grad else 3
    outs = pl.pallas_call(
        body, name=name, grid=(rows // tr,), in_specs=[spec] * 4, out_specs=[spec] * n_out,
        out_shape=[jax.ShapeDtypeStruct((rows, cols), F32)] * n_out, compiler_params=_params("parallel"),
    )(flat(w), flat(g), flat(m), flat(v))
    return tuple(o.reshape(shape) for o in outs)


WEIGHTS = ("pool_w", "pool_b", "pool_scale", "attn_w_qkv", "attn_b_qkv", "attn_sinks", "attn_w_o", "norm_mix",
           "norm_ffn", "ffn_w_in", "ffn_conv_w", "ffn_conv_b", "ffn_w_out", "norm_f")
SMALL_GATHER = 128 * 71
_SMALL_SIZES = (("pool_b", 1024), ("pool_scale", 1024), ("b_qkv", 1536), ("sinks", 128), ("norm_mix", 2048),
                ("norm_ffn", 2048), ("conv_w", 6 * 2 * D_FF), ("conv_b", 4 * D_FF), ("norm_f", 1024))
SMALL_COLS = 6784


def _pack_small(grads):
    flat = jnp.concatenate([jnp.pad(grads[nm].reshape(-1), (0, size - grads[nm].size)) for nm, size in _SMALL_SIZES])
    return jnp.pad(flat, (0, 8 * SMALL_COLS - flat.size)).reshape(8, SMALL_COLS)


def _unpack_small(vec):
    flat = vec.reshape(-1)
    out, off = {}, 0
    for nm, size in _SMALL_SIZES:
        out[nm] = flat[off:off + size]
        off += size
    return out


def kernel(x, pool_w, pool_b, pool_scale, attn_w_qkv, attn_b_qkv, attn_sinks, attn_w_o, norm_mix, norm_ffn, ffn_w_in, ffn_conv_w, ffn_conv_b, ffn_w_out, norm_f, loss_target, m_pool_w, m_pool_b, m_pool_scale, m_attn_w_qkv, m_attn_b_qkv, m_attn_sinks, m_attn_w_o, m_norm_mix, m_norm_ffn, m_ffn_w_in, m_ffn_conv_w, m_ffn_conv_b, m_ffn_w_out, m_norm_f, v_pool_w, v_pool_b, v_pool_scale, v_attn_w_qkv, v_attn_b_qkv, v_attn_sinks, v_attn_w_o, v_norm_mix, v_norm_ffn, v_ffn_w_in, v_ffn_conv_w, v_ffn_conv_b, v_ffn_w_out, v_norm_f):
    weights = dict(pool_w=pool_w, pool_b=pool_b, pool_scale=pool_scale, attn_w_qkv=attn_w_qkv, attn_b_qkv=attn_b_qkv,
                   attn_sinks=attn_sinks, attn_w_o=attn_w_o, norm_mix=norm_mix, norm_ffn=norm_ffn, ffn_w_in=ffn_w_in,
                   ffn_conv_w=ffn_conv_w, ffn_conv_b=ffn_conv_b, ffn_w_out=ffn_w_out, norm_f=norm_f)
    m_in = dict(pool_w=m_pool_w, pool_b=m_pool_b, pool_scale=m_pool_scale, attn_w_qkv=m_attn_w_qkv,
                attn_b_qkv=m_attn_b_qkv, attn_sinks=m_attn_sinks, attn_w_o=m_attn_w_o, norm_mix=m_norm_mix,
                norm_ffn=m_norm_ffn, ffn_w_in=m_ffn_w_in, ffn_conv_w=m_ffn_conv_w, ffn_conv_b=m_ffn_conv_b,
                ffn_w_out=m_ffn_w_out, norm_f=m_norm_f)
    v_in = dict(pool_w=v_pool_w, pool_b=v_pool_b, pool_scale=v_pool_scale, attn_w_qkv=v_attn_w_qkv,
                attn_b_qkv=v_attn_b_qkv, attn_sinks=v_attn_sinks, attn_w_o=v_attn_w_o, norm_mix=v_norm_mix,
                norm_ffn=v_norm_ffn, ffn_w_in=v_ffn_w_in, ffn_conv_w=v_ffn_conv_w, ffn_conv_b=v_ffn_conv_b,
                ffn_w_out=v_ffn_w_out, norm_f=v_norm_f)
    chip = 2 * lax.axis_index("x") + lax.axis_index("y")
    core = lax.axis_index("c")

    place = jnp.stack([chip, core]).astype(jnp.int32)
    shards = {"w_in": ffn_w_in, "w_out": ffn_w_out, "w_qkv": attn_w_qkv[0], "w_o": attn_w_o[0], "pool_w": pool_w[0]}
    placed = {nm: _place_shard(nm, shards[nm], place) for nm in BIG}
    small = jnp.concatenate([pool_b.reshape(-1), attn_b_qkv.reshape(-1), ffn_conv_w.reshape(-1)]).reshape(1, SMALL_GATHER)
    full, small_all = _gather_early(placed, small)
    small_all = small_all.reshape(N_CHIPS, SMALL_GATHER)
    pool_b_full = small_all[:, :256].reshape(N_CHIPS, 4, POOL_ROWS).transpose(1, 0, 2).reshape(1, D_MODEL)
    b_qkv_full = small_all[:, 256:640].reshape(1, QKV_DIM)
    conv_w_full = small_all[:, 640:].reshape(N_CHIPS, 2, 3, W_IN_COLS).transpose(1, 2, 0, 3).reshape(2, 3, 2 * D_FF)

    reducer = _Reducer(place)
    loss, grad_x, grads = _local_step(
        x[0], loss_target[0], full, norm_mix=norm_mix, norm_ffn=norm_ffn, norm_f=norm_f.reshape(1, D_MODEL),
        pool_b=pool_b_full, pool_scale=pool_scale, b_qkv=b_qkv_full, sinks=attn_sinks[0], conv_w=conv_w_full,
        conv_b=ffn_conv_b, late_gather=_late_gather_host, reducer=reducer)
    loss = lax.psum(loss[0, 0], ("x", "y", "c"))
    reduced = dict(reducer.shards)

    small_sum, reduced["pool_w"] = _all_reduce_small(_pack_small(grads), grads["pool_w"])
    small_g = _unpack_small(small_sum)
    pool_b_g = lax.dynamic_slice_in_dim(small_g["pool_b"].reshape(4, N_CHIPS, POOL_ROWS), chip, 1, axis=1)
    b_qkv_g = lax.dynamic_slice_in_dim(small_g["b_qkv"].reshape(N_CHIPS, QKV_COLS), chip, 1, axis=0)
    conv_w_g = lax.dynamic_slice_in_dim(small_g["conv_w"].reshape(2, 3, N_CHIPS, W_IN_COLS), chip, 1, axis=2)
    grad_w = {
        "pool_w": reduced["pool_w"].reshape(pool_w.shape), "pool_b": pool_b_g.reshape(pool_b.shape),
        "pool_scale": small_g["pool_scale"].reshape(pool_scale.shape),
        "attn_w_qkv": reduced["w_qkv"].reshape(attn_w_qkv.shape), "attn_b_qkv": b_qkv_g.reshape(attn_b_qkv.shape),
        "attn_sinks": small_g["sinks"][:N_HEADS].reshape(attn_sinks.shape),
        "attn_w_o": reduced["w_o"].reshape(attn_w_o.shape), "norm_mix": small_g["norm_mix"].reshape(norm_mix.shape),
        "norm_ffn": small_g["norm_ffn"].reshape(norm_ffn.shape), "ffn_w_in": reduced["w_in"],
        "ffn_conv_w": conv_w_g.reshape(ffn_conv_w.shape), "ffn_conv_b": small_g["conv_b"].reshape(ffn_conv_b.shape),
        "ffn_w_out": reduced["w_out"], "norm_f": small_g["norm_f"].reshape(norm_f.shape),
    }

    delta, new_m, new_v = {}, {}, {}
    exchanged = ("pool_w", "attn_w_qkv", "attn_w_o", "ffn_w_in", "ffn_w_out")
    for nm in WEIGHTS:
        res = _adamw_call(weights[nm], grad_w[nm], m_in[nm], v_in[nm], f"adamw_{nm}", copy_grad=nm in exchanged)
        delta[nm], new_m[nm], new_v[nm] = res[:3]
        if nm in exchanged:
            grad_w[nm] = res[3]
    return (loss, grad_x.reshape(x.shape), *[grad_w[nm] for nm in WEIGHTS], *[delta[nm] for nm in WEIGHTS],
            *[new_m[nm] for nm in WEIGHTS], *[new_v[nm] for nm in WEIGHTS])
```

```python
import functools
from typing import Callable, NamedTuple

import jax
import jax.numpy as jnp
from jax import lax
from jax.experimental import pallas as pl
from jax.experimental.pallas import tpu as pltpu

F32 = jnp.float32
MXU_DTYPE = jnp.bfloat16
WIRE_DTYPE = jnp.bfloat16
U_DTYPE = jnp.float32

D_MODEL = 1024
D_FF = 2816
QKV_DIM = 1536
HEAD_DIM = 64
N_HEADS = 16
N_KV_HEADS = 4
ATT_BLOCK = 128
POOL_WINDOWS = (2, 4, 8, 16)
POOL_GC = 256
POOL_HALO = 16
CONV_HALO = 8
RMS_EPS = 1e-6
ATT_SCALE = HEAD_DIM ** -0.5
ALIBI_SLOPES = tuple(2.0 ** (-8.0 / N_HEADS * (h + 1)) for h in range(N_HEADS))

ADAM_LR = 0.001
ADAM_B1 = 0.9
ADAM_B2 = 0.999
ADAM_EPS = 1e-08
ADAM_WD = 0.01
ADAM_STEP = 10

N_CHIPS = 4
MESH = pl.DeviceIdType.MESH
VMEM_LIMIT_BYTES = 56 * 1024 * 1024
ANY = pl.BlockSpec(memory_space=pl.ANY)


def _params(*semantics):
    return pltpu.CompilerParams(dimension_semantics=semantics, vmem_limit_bytes=VMEM_LIMIT_BYTES)


def _rms(x, g):
    return x * lax.rsqrt(jnp.mean(x * x, axis=-1, keepdims=True) + RMS_EPS) * g


def _rms_bwd(x, g, dy):
    rstd = lax.rsqrt(jnp.mean(x * x, axis=-1, keepdims=True) + RMS_EPS)
    xhat = x * rstd
    dxhat = dy * g
    dx = rstd * (dxhat - xhat * jnp.mean(dxhat * xhat, axis=-1, keepdims=True))
    return dx, dy * xhat


def _shift_down(x, s):
    return pltpu.roll(x, s, axis=0)


def _shift_up(x, s):
    return pltpu.roll(x, x.shape[0] - s, axis=0)


def _sigmoid(x):
    return 1.0 / (1.0 + jnp.exp(-x))


_DOT_DIMS = {"nn": ((1,), (0,)), "nt": ((1,), (1,)), "tn": ((0,), (0,))}


class _Hosted(NamedTuple):
    sources: list
    arrays: list
    new: list
    sems: list
    first: Callable
    last: Callable


def _call(body, hosted, *, name, grid, in_specs, out_specs, out_shape, operands, semantics, scratch_shapes=(),
          aliases=None):
    n_in, n_out, n_scr = len(in_specs), len(out_specs), len(scratch_shapes)
    aliases = dict(aliases or {})
    if hosted is None:
        def plain(*refs):
            body(refs[:n_in], refs[n_in:n_in + n_out], refs[n_in + n_out:], lambda: None, lambda: None)

        return pl.pallas_call(plain, name=name, grid=grid, in_specs=list(in_specs), out_specs=list(out_specs),
                              out_shape=list(out_shape), scratch_shapes=list(scratch_shapes),
                              input_output_aliases=aliases, compiler_params=_params(*semantics))(*operands)
    ns, na, nn = len(hosted.sources), len(hosted.arrays), len(hosted.new)

    def carrying(*refs):
        ins, src = refs[:n_in], refs[n_in:n_in + ns]
        o0 = n_in + ns + na
        outs, arr = refs[o0:o0 + n_out], refs[o0 + n_out:o0 + n_out + na]
        new = refs[o0 + n_out + na:o0 + n_out + na + nn]
        s0 = o0 + n_out + na + nn
        scratch, sems = refs[s0:s0 + n_scr], refs[s0 + n_scr:]
        ids = [pl.program_id(d) for d in range(len(grid))]
        is_first = functools.reduce(lambda p, q: p & q, [i == 0 for i in ids])
        is_last = functools.reduce(lambda p, q: p & q, [i == n - 1 for i, n in zip(ids, grid)])
        body(ins, outs, scratch, lambda: pl.when(is_first)(lambda: hosted.first(src, arr, new, sems)),
             lambda: pl.when(is_last)(lambda: hosted.last(src, arr, new, sems)))

    for t in range(na):
        aliases[n_in + ns + t] = n_out + t
    return pl.pallas_call(
        carrying, name=name, grid=grid, in_specs=list(in_specs) + [ANY] * (ns + na),
        out_specs=list(out_specs) + [ANY] * (na + nn),
        out_shape=list(out_shape) + [jax.ShapeDtypeStruct(a.shape, a.dtype) for a in hosted.arrays] + list(hosted.new),
        scratch_shapes=list(scratch_shapes) + list(hosted.sems), input_output_aliases=aliases,
        compiler_params=_params(*["arbitrary"] * len(grid)))(*operands, *hosted.sources, *hosted.arrays)


def _both(h1, h2):
    def run(which, src, arr, new, sems):
        cut = lambda seq, n: (seq[:n], seq[n:])
        (s1, s2), (a1, a2) = cut(src, len(h1.sources)), cut(arr, len(h1.arrays))
        (n1, n2), (m1, m2) = cut(new, len(h1.new)), cut(sems, len(h1.sems))
        getattr(h1, which)(s1, a1, n1, m1)
        getattr(h2, which)(s2, a2, n2, m2)

    return _Hosted(h1.sources + h2.sources, h1.arrays + h2.arrays, h1.new + h2.new, h1.sems + h2.sems,
                   functools.partial(run, "first"), functools.partial(run, "last"))


def _ride(hooks, stage, call, env=None):
    if not hooks or stage not in hooks:
        res = call(None)
        return list(res) if isinstance(res, (list, tuple)) else [res]
    make, done = hooks[stage]
    hosted = make(env)
    res = call(hosted)
    own = len(res) - len(hosted.arrays) - len(hosted.new)
    done(res[own:])
    return list(res[:own])


def _matmul(a, b, *, mode, name, grid, a_spec, b_spec, o_spec, out_shape, bias=None, bias_spec=None, residual=None,
            residual_spec=None, into=None, hosted=None, norm_g=None):
    nk = grid[2]
    dims = (_DOT_DIMS[mode], ((), ()))
    acc_shape = tuple(d for d in o_spec.block_shape if d is not None)

    def body(ins, outs, scratch, first, last):
        a_ref, b_ref = ins[0], ins[1]
        pos = 2
        bias_ref = res_ref = None
        if bias is not None:
            bias_ref = ins[pos]
            pos += 1
        if residual is not None:
            res_ref = ins[pos]
            pos += 1
        o_ref = outs[0]
        acc_ref = scratch[0] if nk > 1 else None
        first()
        prod = lax.dot_general(a_ref[...].astype(MXU_DTYPE), b_ref[...].astype(MXU_DTYPE), dims,
                               preferred_element_type=F32)

        def finish(acc):
            if bias_ref is not None:
                acc = acc + bias_ref[...]
            if res_ref is not None:
                acc = acc + res_ref[...]
            o_ref[...] = acc.astype(o_ref.dtype)
            if norm_g is not None:
                outs[1][...] = _rms(acc, ins[pos][...]).astype(outs[1].dtype)

        if nk == 1:
            finish(prod)
        else:
            k = pl.program_id(2)

            @pl.when(k == 0)
            def _():
                acc_ref[...] = prod

            @pl.when(k > 0)
            def _():
                acc_ref[...] += prod

            @pl.when(k == nk - 1)
            def _():
                finish(acc_ref[...])

        last()

    operands, in_specs = [a, b], [a_spec, b_spec]
    if bias is not None:
        operands.append(bias)
        in_specs.append(bias_spec)
    if residual is not None:
        operands.append(residual)
        in_specs.append(residual_spec)
    out_specs, out_shapes = [o_spec], [out_shape]
    if norm_g is not None:
        assert acc_shape[-1] == norm_g.shape[-1], "the fused norm needs whole rows in an output tile"
        operands.append(norm_g)
        in_specs.append(bias_spec)
        out_specs.append(o_spec)
        out_shapes.append(jax.ShapeDtypeStruct(out_shape.shape, MXU_DTYPE))
    aliases = {}
    if into is not None:
        aliases = {len(operands): 0}
        operands.append(into)
        in_specs.append(ANY)
    res = _call(body, hosted, name=name, grid=grid, in_specs=in_specs, out_specs=out_specs, out_shape=out_shapes,
                operands=operands, semantics=("parallel", "parallel", "arbitrary"),
                scratch_shapes=[pltpu.VMEM(acc_shape, F32)] if nk > 1 else [], aliases=aliases)
    return res if hosted or norm_g is not None else res[0]


def _tile(n, want):
    t = min(n, want)
    assert n % t == 0, (n, want)
    return t


def _mm_nn(a, b, name, *, b_lead=None, bias=None, residual=None, out_dtype=F32, tm=1024, tn=1024, tk=1024,
           hosted=None, norm_g=None):
    (m, k), n = a.shape, b.shape[-1]
    tm, tn, tk = _tile(m, tm), _tile(n, tn), _tile(k, tk)
    if b_lead is None:
        b_spec = pl.BlockSpec((tk, tn), lambda i, j, kk: (kk, j))
    else:
        b_spec = pl.BlockSpec((None, tk, tn), lambda i, j, kk: (b_lead, kk, j))
    return _matmul(
        a, b, mode="nn", name=name, grid=(m // tm, n // tn, k // tk),
        a_spec=pl.BlockSpec((tm, tk), lambda i, j, kk: (i, kk)), b_spec=b_spec,
        o_spec=pl.BlockSpec((tm, tn), lambda i, j, kk: (i, j)), out_shape=jax.ShapeDtypeStruct((m, n), out_dtype),
        bias=bias, bias_spec=pl.BlockSpec((1, tn), lambda i, j, kk: (0, j)),
        residual=residual, residual_spec=pl.BlockSpec((tm, tn), lambda i, j, kk: (i, j)), hosted=hosted,
        norm_g=norm_g)


def _mm_nt(a, b, name, *, b_lead=None, out_dtype=F32, tm=1024, tn=1024, tk=1024, hosted=None):
    (m, k), n = a.shape, b.shape[-2]
    tm, tn, tk = _tile(m, tm), _tile(n, tn), _tile(k, tk)
    if b_lead is None:
        b_spec = pl.BlockSpec((tn, tk), lambda i, j, kk: (j, kk))
    else:
        b_spec = pl.BlockSpec((None, tn, tk), lambda i, j, kk: (b_lead, j, kk))
    return _matmul(
        a, b, mode="nt", name=name, grid=(m // tm, n // tn, k // tk),
        a_spec=pl.BlockSpec((tm, tk), lambda i, j, kk: (i, kk)), b_spec=b_spec,
        o_spec=pl.BlockSpec((tm, tn), lambda i, j, kk: (i, j)), out_shape=jax.ShapeDtypeStruct((m, n), out_dtype),
        hosted=hosted)


def _mm_tn(a, b, name, *, lead=None, n_lead=None, into=None, tm=1024, tn=1024, tk=1024, out_dtype=F32):
    (k, m), n = a.shape, b.shape[-1]
    tm, tn, tk = _tile(m, tm), _tile(n, tn), _tile(k, tk)
    if lead is None:
        o_spec = pl.BlockSpec((tm, tn), lambda i, j, kk: (i, j))
        out_shape = jax.ShapeDtypeStruct((m, n), out_dtype)
    else:
        o_spec = pl.BlockSpec((None, tm, tn), lambda i, j, kk: (lead, i, j))
        out_shape = jax.ShapeDtypeStruct((n_lead, m, n), out_dtype)
    return _matmul(
        a, b, mode="tn", name=name, grid=(m // tm, n // tn, k // tk),
        a_spec=pl.BlockSpec((tk, tm), lambda i, j, kk: (kk, i)),
        b_spec=pl.BlockSpec((tk, tn), lambda i, j, kk: (kk, j)), o_spec=o_spec, out_shape=out_shape, into=into)


ROW_TILE = 512


def _pool_windows_causal(ext, first_row):
    tt = ext.shape[0] - POOL_HALO
    t = first_row + lax.broadcasted_iota(jnp.int32, (tt, 1), 0)
    outs = []
    for gi, win in enumerate(POOL_WINDOWS):
        cols = slice(gi * POOL_GC, (gi + 1) * POOL_GC)
        s = ext[:, cols]
        sh = 1
        while sh < win:
            s = s + _shift_down(s, sh)
            sh *= 2
        count = jnp.minimum(t + 1, win).astype(F32)
        outs.append(s[POOL_HALO:] / count - ext[POOL_HALO:, cols])
    return outs


def _prev_halo_spec(tt, halo, width):
    per = tt // halo
    return pl.BlockSpec((halo, width), lambda i: (jnp.maximum(i * per - 1, 0), 0))


def _pool_fwd_call(h0, g_mix, pool_w, pool_b, pool_scale, g_ffn, hosted=None):
    t = h0.shape[0]
    tt = _tile(t, ROW_TILE)

    def body(ins, outs, scratch, first, last):
        h_ref, halo_ref, gm_ref, w_ref, b_ref, sc_ref, gf_ref = ins
        h1_ref, hn_ref = outs
        i = pl.program_id(0)
        first()
        x, gm = h_ref[...], gm_ref[...]
        top = jnp.where(i > 0, _rms(halo_ref[...], gm), 0.0)
        ext = jnp.concatenate([top, _rms(x, gm)], axis=0)
        ps = _pool_windows_causal(ext, i * tt)
        ys = [jnp.dot(p.astype(MXU_DTYPE), w_ref[gi], preferred_element_type=F32) for gi, p in enumerate(ps)]
        mix = (jnp.concatenate(ys, axis=1) + b_ref[...]) * sc_ref[...]
        h1 = x + mix
        h1_ref[...] = h1
        hn_ref[...] = _rms(h1, gf_ref[...]).astype(hn_ref.dtype)
        last()

    row = pl.BlockSpec((tt, D_MODEL), lambda i: (i, 0))
    vec = pl.BlockSpec((1, D_MODEL), lambda i: (0, 0))
    wsp = pl.BlockSpec((len(POOL_WINDOWS), POOL_GC, POOL_GC), lambda i: (0, 0, 0))
    return _call(
        body, hosted, name="pool_fwd", grid=(t // tt,),
        in_specs=[row, _prev_halo_spec(tt, POOL_HALO, D_MODEL), vec, wsp, vec, vec, vec], out_specs=[row, row],
        out_shape=[jax.ShapeDtypeStruct(h0.shape, F32), jax.ShapeDtypeStruct(h0.shape, MXU_DTYPE)],
        operands=(h0, h0, g_mix, pool_w, pool_b, pool_scale, g_ffn), semantics=("parallel",))


def _pool_bwd_call(h0, dh1, g_mix, pool_w, pool_b, pool_scale, hosted=None):
    t = h0.shape[0]
    tt = _tile(t, ROW_TILE)
    nt = t // tt
    per = tt // POOL_HALO

    def body(ins, outs, scratch, first, last):
        h_ref, halo_ref, d_ref, dnext_ref, gm_ref, w_ref, b_ref, sc_ref = ins
        gx_ref, dw_ref, dv_ref = outs
        i = pl.program_id(0)
        first()
        x, gm, sc = h_ref[...], gm_ref[...], sc_ref[...]
        top = jnp.where(i > 0, _rms(halo_ref[...], gm), 0.0)
        ext = jnp.concatenate([top, _rms(x, gm)], axis=0)
        ps = _pool_windows_causal(ext, i * tt)

        dy = d_ref[...]
        dy_ext = jnp.concatenate([dy, jnp.where(i < nt - 1, dnext_ref[...], 0.0)], axis=0)
        dpre_ext = dy_ext * sc
        t_ext = i * tt + lax.broadcasted_iota(jnp.int32, (tt + POOL_HALO, 1), 0)

        @pl.when(i == 0)
        def _():
            dw_ref[...] = jnp.zeros_like(dw_ref)
            dv_ref[...] = jnp.zeros_like(dv_ref)

        dhn, ypre = [], []
        for gi, win in enumerate(POOL_WINDOWS):
            cols = slice(gi * POOL_GC, (gi + 1) * POOL_GC)
            w = w_ref[gi]
            p16 = ps[gi].astype(MXU_DTYPE)
            dpre16 = dpre_ext[:, cols].astype(MXU_DTYPE)
            ypre.append(jnp.dot(p16, w, preferred_element_type=F32))
            dw_ref[gi] += lax.dot_general(p16, dpre16[:tt], (((0,), (0,)), ((), ())), preferred_element_type=F32)
            dp_ext = lax.dot_general(dpre16, w, (((1,), (1,)), ((), ())), preferred_element_type=F32)
            s = dp_ext / jnp.minimum(t_ext + 1, win).astype(F32)
            sh = 1
            while sh < win:
                s = s + _shift_up(s, sh)
                sh *= 2
            dhn.append(s[:tt] - dp_ext[:tt])
        dhn = jnp.concatenate(dhn, axis=1)
        ypre = jnp.concatenate(ypre, axis=1) + b_ref[...]
        dx, dgt = _rms_bwd(x, gm, dhn)
        gx_ref[...] = dy + dx
        dv_ref[0:1, :] += jnp.sum(dpre_ext[:tt], axis=0, keepdims=True)
        dv_ref[1:2, :] += jnp.sum(dy * ypre, axis=0, keepdims=True)
        dv_ref[2:3, :] += jnp.sum(dgt, axis=0, keepdims=True)
        last()

    row = pl.BlockSpec((tt, D_MODEL), lambda i: (i, 0))
    vec = pl.BlockSpec((1, D_MODEL), lambda i: (0, 0))
    nxt = pl.BlockSpec((POOL_HALO, D_MODEL), lambda i: (jnp.minimum((i + 1) * per, t // POOL_HALO - 1), 0))
    wsp = pl.BlockSpec((len(POOL_WINDOWS), POOL_GC, POOL_GC), lambda i: (0, 0, 0))
    return _call(
        body, hosted, name="pool_bwd", grid=(nt,),
        in_specs=[row, _prev_halo_spec(tt, POOL_HALO, D_MODEL), row, nxt, vec, wsp, vec, vec],
        out_specs=[row, wsp, pl.BlockSpec((8, D_MODEL), lambda i: (0, 0))],
        out_shape=[jax.ShapeDtypeStruct(h0.shape, F32),
                   jax.ShapeDtypeStruct((len(POOL_WINDOWS), POOL_GC, POOL_GC), F32),
                   jax.ShapeDtypeStruct((8, D_MODEL), F32)],
        operands=(h0, h0, dh1, dh1, g_mix, pool_w, pool_b, pool_scale), semantics=("arbitrary",))


CONV_ROWS = 512
CONV_SUB = 64
LANES = 128


def _causal_conv(u_ext, w, b):
    return w[0:1] * _shift_down(u_ext, 2) + w[1:2] * _shift_down(u_ext, 1) + w[2:3] * u_ext + b


def _piece_rows(load, s, sub, tt, top, bottom=None):
    lo, hi = s * sub - CONV_HALO, (s + 1) * sub + (CONV_HALO if bottom else 0)
    parts = [top()] if lo < 0 else []
    lo = max(lo, 0)
    parts.append(load(lo, min(hi, tt) - lo))
    if hi > tt:
        parts.append(bottom())
    return parts[0] if len(parts) == 1 else jnp.concatenate(parts, axis=0)


def _fold8(x):
    acc = x[0:8]
    for r in range(8, x.shape[0], 8):
        acc = acc + x[r:r + 8]
    return acc


def _conv_glu_fwd_call(u, conv_w, conv_b, name, hosted=None):
    _, t, f = u.shape
    tt = _tile(t, CONV_ROWS)
    sub = _tile(tt, CONV_SUB)
    per = tt // CONV_HALO

    def body(ins, outs, scratch, first, last):
        u_ref, up_ref, w_ref, b_ref = ins
        z_ref = outs[0]
        i = pl.program_id(0)
        first()

        def chunk(j, carry):
            lanes = pl.ds(pl.multiple_of(j * LANES, LANES), LANES)
            w, b = w_ref[:, :, lanes], b_ref[:, lanes]
            for s in range(tt // sub):
                c = []
                for hf in range(2):
                    ext = _piece_rows(lambda at, n: u_ref[hf, pl.ds(at, n), lanes].astype(F32), s, sub, tt,
                                      top=lambda: jnp.where(i > 0, up_ref[hf, :, lanes].astype(F32), 0.0))
                    c.append(_causal_conv(ext, w[:, hf, :], b[hf:hf + 1, :])[CONV_HALO:])
                z_ref[pl.ds(s * sub, sub), lanes] = (c[0] * c[1] * _sigmoid(c[1])).astype(z_ref.dtype)
            return carry

        lax.fori_loop(0, f // LANES, chunk, 0)
        last()

    res = _call(
        body, hosted, name=name, grid=(t // tt,),
        in_specs=[pl.BlockSpec((2, tt, f), lambda i: (0, i, 0)),
                  pl.BlockSpec((2, CONV_HALO, f), lambda i: (0, jnp.maximum(i * per - 1, 0), 0)),
                  pl.BlockSpec((3, 2, f), lambda i: (0, 0, 0)), pl.BlockSpec((2, f), lambda i: (0, 0))],
        out_specs=[pl.BlockSpec((tt, f), lambda i: (i, 0))], out_shape=[jax.ShapeDtypeStruct((t, f), MXU_DTYPE)],
        operands=(u, u, conv_w, conv_b), semantics=("parallel",))
    return res if hosted else res[0]


def _conv_glu_bwd_call(u, dz, conv_w, conv_b, name, hosted=None):
    _, t, f = u.shape
    tt = _tile(t, CONV_ROWS)
    sub = _tile(tt, 2 * CONV_SUB)
    nt = t // tt
    per = tt // CONV_HALO
    halo = CONV_HALO

    def body(ins, outs, scratch, first, last):
        u_ref, up_ref, un_ref, dz_ref, dzn_ref, w_ref, b_ref = ins
        du_ref, dw_ref, db_ref = outs
        dwacc, dbacc = scratch
        i = pl.program_id(0)
        first()

        @pl.when(i == 0)
        def _():
            dwacc[...] = jnp.zeros_like(dwacc)
            dbacc[...] = jnp.zeros_like(dbacc)

        main = slice(halo, halo + sub)

        def chunk(j, carry):
            lanes = pl.ds(pl.multiple_of(j * LANES, LANES), LANES)
            w, b = w_ref[:, :, lanes], b_ref[:, lanes]
            for s in range(tt // sub):
                ue = [_piece_rows(lambda at, n: u_ref[hf, pl.ds(at, n), lanes].astype(F32), s, sub, tt,
                                  top=lambda: jnp.where(i > 0, up_ref[hf, :, lanes].astype(F32), 0.0),
                                  bottom=lambda: un_ref[hf, :, lanes].astype(F32)) for hf in range(2)]
                a, g = [_causal_conv(ue[hf], w[:, hf, :], b[hf:hf + 1, :]) for hf in range(2)]
                sg = _sigmoid(g)
                if (s + 1) * sub < tt:
                    below = dz_ref[pl.ds(s * sub, sub + 2 * halo), lanes].astype(F32)[:sub + halo]
                else:
                    below = jnp.concatenate([dz_ref[pl.ds(s * sub, sub), lanes].astype(F32),
                                             jnp.where(i < nt - 1, dzn_ref[:, lanes].astype(F32), 0.0)], axis=0)
                dzs = jnp.concatenate([jnp.zeros((halo, LANES), F32), below], axis=0)
                gs = g * sg
                dc = [dzs * gs, dzs * a * (sg + gs - gs * sg)]
                for hf in range(2):
                    d = dc[hf]
                    d1, d2 = _shift_up(d, 1), _shift_up(d, 2)
                    du = w[2:3, hf, :] * d + w[1:2, hf, :] * d1 + w[0:1, hf, :] * d2
                    du_ref[hf, pl.ds(s * sub, sub), lanes] = du[main].astype(du_ref.dtype)
                    um = ue[hf][main]
                    for k, dd in enumerate((d2, d1, d)):
                        dwacc[k, hf, :, lanes] += _fold8(dd[main] * um)
                    dbacc[hf, :, lanes] += _fold8(d[main])
            return carry

        lax.fori_loop(0, f // LANES, chunk, 0)

        @pl.when(i == nt - 1)
        def _():
            dw_ref[...] = jnp.sum(dwacc[...], axis=2)
            db_ref[...] = jnp.sum(dbacc[...], axis=1)

        last()

    end = t // halo - 1
    return _call(
        body, hosted, name=name, grid=(nt,),
        in_specs=[pl.BlockSpec((2, tt, f), lambda i: (0, i, 0)),
                  pl.BlockSpec((2, halo, f), lambda i: (0, jnp.maximum(i * per - 1, 0), 0)),
                  pl.BlockSpec((2, halo, f), lambda i: (0, jnp.minimum((i + 1) * per, end), 0)),
                  pl.BlockSpec((tt, f), lambda i: (i, 0)),
                  pl.BlockSpec((halo, f), lambda i: (jnp.minimum((i + 1) * per, end), 0)),
                  pl.BlockSpec((3, 2, f), lambda i: (0, 0, 0)), pl.BlockSpec((2, f), lambda i: (0, 0))],
        out_specs=[pl.BlockSpec((2, tt, f), lambda i: (0, i, 0)),
                   pl.BlockSpec((3, 2, f), lambda i: (0, 0, 0)), pl.BlockSpec((2, f), lambda i: (0, 0))],
        out_shape=[jax.ShapeDtypeStruct(u.shape, MXU_DTYPE), jax.ShapeDtypeStruct((3, 2, f), F32),
                   jax.ShapeDtypeStruct((2, f), F32)],
        operands=(u, u, u, dz, dz, conv_w, conv_b), semantics=("arbitrary",),
        scratch_shapes=[pltpu.VMEM((3, 2, 8, f), F32), pltpu.VMEM((2, 8, f), F32)])


def _kv_even_odd(band, k):
    pair, parity = divmod(k, 2)
    blk = band[:, 128 * pair:128 * (pair + 1)].astype(F32)
    lane = lax.broadcasted_iota(jnp.int32, blk.shape, 1)
    own = jnp.where((lane >= HEAD_DIM) == (parity == 1), blk, 0.0)
    swapped = pltpu.roll(own, HEAD_DIM, axis=1)
    even, odd = (own, swapped) if parity == 0 else (swapped, own)
    return even.astype(MXU_DTYPE), odd.astype(MXU_DTYPE)


def _stack_pairs(ref, k):
    return jnp.concatenate([ref[:, 256 * k:256 * k + 128], ref[:, 256 * k + 128:256 * k + 256]], axis=0)


_GROUP_ORDER = (0, 2, 1, 3)


def _attn_fill_tables(bias_ref):
    shape = (4 * ATT_BLOCK, 2 * ATT_BLOCK)
    row = lax.broadcasted_iota(jnp.int32, shape, 0)
    col = lax.broadcasted_iota(jnp.int32, shape, 1)
    dist = (row & (ATT_BLOCK - 1)) + ATT_BLOCK - col
    ok = (dist >= 0) & (dist < ATT_BLOCK)
    distf = dist.astype(F32)
    rb = lax.broadcasted_iota(jnp.int32, (shape[0], 1), 0) // ATT_BLOCK
    for k in range(N_KV_HEADS):
        slope = jnp.zeros((shape[0], 1), F32)
        for r, gq in enumerate(_GROUP_ORDER):
            slope = jnp.where(rb == r, ALIBI_SLOPES[4 * k + gq], slope)
        bias = jnp.where(ok, -slope * distf, -jnp.inf)
        bias_ref[0, k] = bias
        bias_ref[1, k] = jnp.where(col >= ATT_BLOCK, bias, -jnp.inf)


def _attn_probs(qs, k_even, k_odd, bias, k, sink_ref):
    nt_dims = (((1,), (1,)), ((), ()))
    s = jnp.concatenate([lax.dot_general(qs, k_even, nt_dims, preferred_element_type=F32),
                         lax.dot_general(qs, k_odd, nt_dims, preferred_element_type=F32)], axis=0) + bias
    rb = lax.broadcasted_iota(jnp.int32, (s.shape[0], 1), 0) // ATT_BLOCK
    sink = jnp.zeros((s.shape[0], 1), F32)
    for r, gq in enumerate(_GROUP_ORDER):
        sink = jnp.where(rb == r, sink_ref[4 * k + gq], sink)
    m = jnp.maximum(jnp.max(s, axis=-1, keepdims=True), sink)
    p = jnp.exp(s - m)
    es = jnp.exp(sink - m)
    return p, 1.0 / (jnp.sum(p, axis=-1, keepdims=True) + es), es


_ATTN_TABLES = [pltpu.VMEM((2, N_KV_HEADS, 4 * ATT_BLOCK, 2 * ATT_BLOCK), F32)]


def _attn_specs(with_do):
    prev = lambda n: jnp.maximum(n - 1, 0)
    specs = [pl.BlockSpec(memory_space=pltpu.SMEM),
             pl.BlockSpec((ATT_BLOCK, D_MODEL), lambda n: (n, 0)),
             pl.BlockSpec((ATT_BLOCK, 256), lambda n: (prev(n), 4)), pl.BlockSpec((ATT_BLOCK, 256), lambda n: (n, 4)),
             pl.BlockSpec((ATT_BLOCK, 256), lambda n: (prev(n), 5)), pl.BlockSpec((ATT_BLOCK, 256), lambda n: (n, 5))]
    if with_do:
        specs.append(pl.BlockSpec((ATT_BLOCK, D_MODEL), lambda n: (n, 0)))
    return specs


def _attn_fwd_call(qkv, sinks, hosted=None):
    t = qkv.shape[0]

    def body(ins, outs, scratch, first, last):
        sink_ref, q_ref, kp_ref, kc_ref, vp_ref, vc_ref = ins
        o_ref, bias_ref = outs[0], scratch[0]
        n = pl.program_id(0)
        first()

        @pl.when(n == 0)
        def _():
            _attn_fill_tables(bias_ref)

        first = 1 - jnp.minimum(n, 1)
        kband = jnp.concatenate([kp_ref[...], kc_ref[...]], axis=0)
        vband = jnp.concatenate([vp_ref[...], vc_ref[...]], axis=0)
        for k in range(N_KV_HEADS):
            k_even, k_odd = _kv_even_odd(kband, k)
            v_even, v_odd = _kv_even_odd(vband, k)
            qs = _stack_pairs(q_ref, k) * ATT_SCALE
            p, rdenom, _ = _attn_probs(qs, k_even, k_odd, bias_ref[first, k], k, sink_ref)
            probs = (p * rdenom).astype(MXU_DTYPE)
            o = (jnp.dot(probs[:256], v_even, preferred_element_type=F32)
                 + jnp.dot(probs[256:], v_odd, preferred_element_type=F32))
            o_ref[:, 256 * k:256 * k + 128] = o[:128].astype(o_ref.dtype)
            o_ref[:, 256 * k + 128:256 * k + 256] = o[128:].astype(o_ref.dtype)
        last()

    return _call(
        body, hosted, name="attn_fwd", grid=(t // ATT_BLOCK,), in_specs=_attn_specs(False),
        out_specs=[pl.BlockSpec((ATT_BLOCK, D_MODEL), lambda n: (n, 0))],
        out_shape=[jax.ShapeDtypeStruct((t, D_MODEL), MXU_DTYPE)], operands=(sinks, qkv, qkv, qkv, qkv, qkv),
        semantics=("arbitrary",), scratch_shapes=_ATTN_TABLES)


def _attn_bwd_call(qkv, sinks, do, hosted=None):
    t = qkv.shape[0]
    nb = t // ATT_BLOCK
    tn_dims = (((0,), (0,)), ((), ()))
    nt_dims = (((1,), (1,)), ((), ()))

    def to_native(even_part, odd_part, parity):
        lane = lax.broadcasted_iota(jnp.int32, even_part.shape, 1)
        lo = lane < HEAD_DIM
        e, o = jnp.where(lo, even_part, 0.0), jnp.where(lo, 0.0, odd_part)
        if parity == 0:
            return e + pltpu.roll(o, HEAD_DIM, axis=1)
        return pltpu.roll(e, HEAD_DIM, axis=1) + o

    def body(ins, outs, scratch, first, last):
        sink_ref, q_ref, kp_ref, kc_ref, vp_ref, vc_ref, do_ref = ins
        dq_ref, dk_ref, dv_ref, db_ref, dsink_ref = outs
        bias_ref = scratch[0]
        n = pl.program_id(0)
        first()

        @pl.when(n == 0)
        def _():
            _attn_fill_tables(bias_ref)
            dk_ref[...] = jnp.zeros_like(dk_ref)
            dv_ref[...] = jnp.zeros_like(dv_ref)
            db_ref[...] = jnp.zeros_like(db_ref)
            dsink_ref[...] = jnp.zeros_like(dsink_ref)

        kband = jnp.concatenate([kp_ref[...], kc_ref[...]], axis=0)
        vband = jnp.concatenate([vp_ref[...], vc_ref[...]], axis=0)
        dk_pairs = [jnp.zeros((2 * ATT_BLOCK, 128), F32), jnp.zeros((2 * ATT_BLOCK, 128), F32)]
        dv_pairs = [jnp.zeros((2 * ATT_BLOCK, 128), F32), jnp.zeros((2 * ATT_BLOCK, 128), F32)]
        sink_lane = lax.broadcasted_iota(jnp.int32, (8, 128), 1)
        sink_row = lax.broadcasted_iota(jnp.int32, (8, 128), 0)
        dsink = jnp.zeros((8, 128), F32)
        first = 1 - jnp.minimum(n, 1)
        for k in range(N_KV_HEADS):
            k_even, k_odd = _kv_even_odd(kband, k)
            v_even, v_odd = _kv_even_odd(vband, k)
            qs = _stack_pairs(q_ref, k) * ATT_SCALE
            dos = _stack_pairs(do_ref, k)
            p, rdenom, es = _attn_probs(qs, k_even, k_odd, bias_ref[first, k], k, sink_ref)
            probs = p * rdenom
            dp = jnp.concatenate([lax.dot_general(dos, v_even, nt_dims, preferred_element_type=F32),
                                  lax.dot_general(dos, v_odd, nt_dims, preferred_element_type=F32)], axis=0)
            delta = jnp.sum(probs * dp, axis=-1, keepdims=True)
            ds16 = (probs * (dp - delta)).astype(MXU_DTYPE)
            p16 = probs.astype(MXU_DTYPE)
            dsink_rows = -(es * rdenom) * delta
            for r, gq in enumerate(_GROUP_ORDER):
                tot = jnp.sum(dsink_rows[ATT_BLOCK * r:ATT_BLOCK * (r + 1)], axis=0, keepdims=True)
                dsink = dsink + jnp.where((sink_lane == 4 * k + gq) & (sink_row == 0), tot, 0.0)
            dqs = (jnp.dot(ds16[:256], k_even, preferred_element_type=F32)
                   + jnp.dot(ds16[256:], k_odd, preferred_element_type=F32)) * ATT_SCALE
            dq_ref[:, 256 * k:256 * k + 128] = dqs[:128].astype(dq_ref.dtype)
            dq_ref[:, 256 * k + 128:256 * k + 256] = dqs[128:].astype(dq_ref.dtype)
            db_ref[:, 256 * k:256 * k + 128] += jnp.sum(dqs[:128], axis=0, keepdims=True)
            db_ref[:, 256 * k + 128:256 * k + 256] += jnp.sum(dqs[128:], axis=0, keepdims=True)
            pair, parity = divmod(k, 2)
            dk_pairs[pair] = dk_pairs[pair] + to_native(
                lax.dot_general(ds16[:256], qs, tn_dims, preferred_element_type=F32),
                lax.dot_general(ds16[256:], qs, tn_dims, preferred_element_type=F32), parity)
            dv_pairs[pair] = dv_pairs[pair] + to_native(
                lax.dot_general(p16[:256], dos, tn_dims, preferred_element_type=F32),
                lax.dot_general(p16[256:], dos, tn_dims, preferred_element_type=F32), parity)
        dsink_ref[...] += dsink
        dk_band = jnp.concatenate(dk_pairs, axis=1)
        dv_band = jnp.concatenate(dv_pairs, axis=1)
        cur = pl.multiple_of(n * ATT_BLOCK, ATT_BLOCK)
        dk_ref[pl.ds(cur, ATT_BLOCK), :] += dk_band[ATT_BLOCK:]
        dv_ref[pl.ds(cur, ATT_BLOCK), :] += dv_band[ATT_BLOCK:]

        @pl.when(n > 0)
        def _():
            prv = pl.multiple_of((n - 1) * ATT_BLOCK, ATT_BLOCK)
            dk_ref[pl.ds(prv, ATT_BLOCK), :] += dk_band[:ATT_BLOCK]
            dv_ref[pl.ds(prv, ATT_BLOCK), :] += dv_band[:ATT_BLOCK]

        @pl.when(n == nb - 1)
        def _():
            db_ref[:, 1024:1280] = jnp.sum(dk_ref[...], axis=0, keepdims=True)
            db_ref[:, 1280:1536] = jnp.sum(dv_ref[...], axis=0, keepdims=True)

        last()

    whole = lambda shape: pl.BlockSpec(shape, lambda n: (0, 0))
    return _call(
        body, hosted, name="attn_bwd", grid=(nb,), in_specs=_attn_specs(True),
        out_specs=[pl.BlockSpec((ATT_BLOCK, D_MODEL), lambda n: (n, 0)), whole((t, 256)), whole((t, 256)),
                   whole((1, QKV_DIM)), whole((8, 128))],
        out_shape=[jax.ShapeDtypeStruct((t, D_MODEL), MXU_DTYPE), jax.ShapeDtypeStruct((t, 256), F32),
                   jax.ShapeDtypeStruct((t, 256), F32), jax.ShapeDtypeStruct((1, QKV_DIM), F32),
                   jax.ShapeDtypeStruct((8, 128), F32)],
        operands=(sinks, qkv, qkv, qkv, qkv, qkv, do), semantics=("arbitrary",), scratch_shapes=_ATTN_TABLES)


FF_TILE = 1408


def _gathering_tile(j):
    me = 2 * lax.axis_index("x") + lax.axis_index("y")
    return me ^ (((j & 1) << 1) | (j >> 1))


def _ffn_in_gathering_call(hn, w_in, layer, hosted):
    t = hn.shape[0]
    tm = _tile(t, 1024)
    n_i = t // tm
    halves = D_FF // FF_TILE
    assert FF_TILE == W_IN_COLS and 2 * halves == N_CHIPS

    def body(ins, outs, scratch, first, last):
        hn_ref = ins[0]
        u_ref, w_ref = outs
        wbuf, load_sem, ici_send, ici_recv, hand_send, hand_recv = scratch
        j, i = pl.program_id(0), pl.program_id(1)
        x, y, c, chips = _mesh_place()
        me = 2 * x + y
        mover = c == layer
        first()

        def shard(k):
            return w_ref.at[layer, :, _col(k, W_IN_COLS)]

        def over_ici(jj, k):
            cx, cy = chips[jj]
            return pltpu.make_async_remote_copy(src_ref=shard(k), dst_ref=shard(k), send_sem=ici_send.at[jj],
                                                recv_sem=ici_recv.at[jj], device_id=(cx, cy, c), device_id_type=MESH)

        def handed(jj):
            cx, cy = chips[jj]
            return pltpu.make_async_remote_copy(
                src_ref=shard(2 * cx + cy), dst_ref=shard(2 * cx + cy), send_sem=hand_send.at[jj],
                recv_sem=hand_recv.at[jj], device_id=(x, y, 1 - c), device_id_type=MESH)

        far = N_CHIPS - 2

        @pl.when((j == 0) & (i == 0) & mover)
        def _():
            over_ici(far, me).start()
            for jj in NEIGHBOURS:
                handed(jj).start()

        @pl.when((j == far + 1) & (i == 0) & mover)
        def _():
            over_ici(far, 2 * chips[far][0] + chips[far][1]).wait_recv()
            handed(far).start()

        for jj in range(N_CHIPS - 1):
            @pl.when((j == jj + 1) & (i == 0) & jnp.logical_not(mover))
            def _(jj=jj):
                handed(jj).wait_recv()

        @pl.when(i == 0)
        def _():
            load = pltpu.make_async_copy(shard(_gathering_tile(j)), wbuf, load_sem.at[0])
            load.start()
            load.wait()

        u_ref[...] = jnp.dot(hn_ref[...].astype(MXU_DTYPE), wbuf[...].astype(MXU_DTYPE),
                             preferred_element_type=F32).astype(u_ref.dtype)

        @pl.when((j == N_CHIPS - 1) & (i == n_i - 1) & mover)
        def _():
            over_ici(far, me).wait_send()
            for jj in range(N_CHIPS - 1):
                handed(jj).wait_send()

        last()

    def u_index(j, i):
        tile = _gathering_tile(j)
        return tile // halves, i, tile % halves

    three = [pltpu.SemaphoreType.DMA((N_CHIPS - 1,))] * 4
    return _call(
        body, hosted, name=f"ffn{layer}_in", grid=(N_CHIPS, n_i),
        in_specs=[pl.BlockSpec((tm, D_MODEL), lambda j, i: (i, 0)), ANY],
        out_specs=[pl.BlockSpec((None, tm, FF_TILE), u_index), ANY],
        out_shape=[jax.ShapeDtypeStruct((2, t, D_FF), U_DTYPE), jax.ShapeDtypeStruct(w_in.shape, w_in.dtype)],
        operands=(hn, w_in), semantics=("arbitrary", "arbitrary"), aliases={1: 1},
        scratch_shapes=[pltpu.VMEM((D_MODEL, W_IN_COLS), w_in.dtype), pltpu.SemaphoreType.DMA((1,))] + three)


def _ffn_out_loss_call(z, w_out, layer, h, g, target):
    t, f = z.shape
    tm = _tile(t, ROW_TILE)

    def body(z_ref, w_ref, h_ref, g_ref, t_ref, loss_ref, dh_ref, dh16_ref, dg_ref):
        x = h_ref[...] + jnp.dot(z_ref[...], w_ref[...], preferred_element_type=F32)
        gv = g_ref[...]
        err = _rms(x, gv) - t_ref[...]
        dx, dgt = _rms_bwd(x, gv, err * (1.0 / D_MODEL))
        dh_ref[...] = dx
        dh16_ref[...] = dx.astype(dh16_ref.dtype)

        @pl.when(pl.program_id(0) == 0)
        def _():
            dg_ref[...] = jnp.zeros_like(dg_ref)
            loss_ref[...] = jnp.zeros_like(loss_ref)

        dg_ref[...] += jnp.sum(dgt, axis=0, keepdims=True)
        per_token = jnp.mean(err * err, axis=-1, keepdims=True)
        loss_ref[...] += 0.5 * jnp.sum(per_token, axis=0, keepdims=True)

    row = pl.BlockSpec((tm, D_MODEL), lambda i: (i, 0))
    vec = pl.BlockSpec((1, D_MODEL), lambda i: (0, 0))
    return pl.pallas_call(
        body, name=f"ffn{layer}_out_loss", grid=(t // tm,),
        in_specs=[pl.BlockSpec((tm, f), lambda i: (i, 0)), pl.BlockSpec((None, f, D_MODEL), lambda i: (layer, 0, 0)),
                  row, vec, row],
        out_specs=[pl.BlockSpec((1, 1), lambda i: (0, 0)), row, row, vec],
        out_shape=[jax.ShapeDtypeStruct((1, 1), F32), jax.ShapeDtypeStruct((t, D_MODEL), F32),
                   jax.ShapeDtypeStruct((t, D_MODEL), MXU_DTYPE), jax.ShapeDtypeStruct((1, D_MODEL), F32)],
        compiler_params=_params("arbitrary"))(z, w_out, h, g, target)


def _ffn_fwd(hn, h, w, conv_w, conv_b, layer, hooks=None, next_norm=None, loss_head=None, gathers_w_in=False):
    t = hn.shape[0]
    tm = _tile(t, 1024)
    halves = D_FF // FF_TILE
    if gathers_w_in:
        u, w["w_in"] = _ride(hooks, "in", lambda hosted: _ffn_in_gathering_call(hn, w["w_in"], layer, hosted))
    else:
        u, = _ride(hooks, "in", lambda hosted: _matmul(
            hn, w["w_in"], mode="nn", name=f"ffn{layer}_in", grid=(2 * halves, t // tm, 1),
            a_spec=pl.BlockSpec((tm, D_MODEL), lambda j, i, k: (i, 0)),
            b_spec=pl.BlockSpec((None, D_MODEL, FF_TILE), lambda j, i, k: (layer, 0, j)),
            o_spec=pl.BlockSpec((None, tm, FF_TILE), lambda j, i, k: (j // halves, i, j % halves)),
            out_shape=jax.ShapeDtypeStruct((2, t, D_FF), U_DTYPE), hosted=hosted))
    z, = _ride(hooks, "glu", lambda hosted: _conv_glu_fwd_call(u, conv_w, conv_b, f"ffn{layer}_glu", hosted=hosted))
    if loss_head is not None:
        return (*_ffn_out_loss_call(z, w["w_out"], layer, h, *loss_head), u, z)
    outs = _ride(hooks, "out", lambda hosted: _mm_nn(z, w["w_out"], f"ffn{layer}_out", b_lead=layer, residual=h,
                                                     tk=D_FF, hosted=hosted, norm_g=next_norm))
    return outs[0], (outs[1] if next_norm is not None else None), u, z


def _ffn_bwd(dh, hn, u, z, w_in, w_out, conv_w, conv_b, layer, g_in_acc, g_out_acc, norm, hooks=None):
    t = hn.shape[0]
    dz, = _ride(hooks, "dz", lambda hosted: _mm_nt(dh, w_out, f"ffn{layer}_dz", b_lead=layer, out_dtype=MXU_DTYPE,
                                                   tn=FF_TILE, hosted=hosted))
    g_out = _mm_tn(z, dh, f"ffn{layer}_dwout", lead=layer, n_lead=2, into=g_out_acc, tm=WGRAD_TILE, tk=t,
                   out_dtype=WIRE_DTYPE)
    du, dcw, dcb = _ride(hooks, "dglu", lambda hosted: _conv_glu_bwd_call(u, dz, conv_w, conv_b, f"ffn{layer}_dglu",
                                                                          hosted=hosted), {"g_out": g_out})
    per_half = D_FF // WGRAD_TILE
    g_in, = _ride(hooks, "dwin", lambda hosted: _matmul(
        hn, du, mode="tn", name=f"ffn{layer}_dwin", grid=(1, 2 * per_half, 1),
        a_spec=pl.BlockSpec((t, D_MODEL), lambda i, j, k: (0, 0)),
        b_spec=pl.BlockSpec((None, t, WGRAD_TILE), lambda i, j, k: (j // per_half, 0, j % per_half)),
        o_spec=pl.BlockSpec((None, D_MODEL, WGRAD_TILE), lambda i, j, k: (layer, 0, j)),
        out_shape=jax.ShapeDtypeStruct((2, D_MODEL, 2 * D_FF), WIRE_DTYPE), into=g_in_acc, hosted=hosted))
    dx, dx16, dg = _ride(hooks, "dhn", lambda hosted: _ffn_dhn_call(du, w_in, layer, norm, hosted),
                         {"g_in": g_in, "g_out": g_out})
    return dx, dx16, dg, g_in, g_out, dcw, dcb


WGRAD_TILE = 256


_NT_DIMS = (((1,), (1,)), ((), ()))


def _dhn_call(name, a, a_spec, w, w_spec, product, t, tm, norm, hosted=None):
    h, g, dres = norm

    def body(ins, outs, scratch, first, last):
        a_ref, w_ref, h_ref, g_ref, dres_ref = ins
        dx_ref, dx16_ref, dg_ref = outs
        first()
        dx, dgt = _rms_bwd(h_ref[...], g_ref[...], product(a_ref, w_ref))
        dx = dres_ref[...] + dx
        dx_ref[...] = dx
        dx16_ref[...] = dx.astype(dx16_ref.dtype)

        @pl.when(pl.program_id(0) == 0)
        def _():
            dg_ref[...] = jnp.zeros_like(dg_ref)

        dg_ref[...] += jnp.sum(dgt, axis=0, keepdims=True)
        last()

    row = pl.BlockSpec((tm, D_MODEL), lambda i: (i, 0))
    vec = pl.BlockSpec((1, D_MODEL), lambda i: (0, 0))
    return _call(
        body, hosted, name=name, grid=(t // tm,), in_specs=[a_spec, w_spec, row, vec, row], out_specs=[row, row, vec],
        out_shape=[jax.ShapeDtypeStruct((t, D_MODEL), F32), jax.ShapeDtypeStruct((t, D_MODEL), MXU_DTYPE),
                   jax.ShapeDtypeStruct((1, D_MODEL), F32)],
        operands=(a, w, h, g, dres), semantics=("arbitrary",))


def _ffn_dhn_call(du, w_in, layer, norm, hosted=None):
    _, t, f = du.shape
    tm = _tile(t, 512)

    def product(du_ref, w_ref):
        return (lax.dot_general(du_ref[0], w_ref[:, :f], _NT_DIMS, preferred_element_type=F32)
                + lax.dot_general(du_ref[1], w_ref[:, f:], _NT_DIMS, preferred_element_type=F32))

    return _dhn_call(f"ffn{layer}_dhn", du, pl.BlockSpec((2, tm, f), lambda i: (0, i, 0)), w_in,
                     pl.BlockSpec((None, D_MODEL, 2 * f), lambda i: (layer, 0, 0)), product, t, tm, norm, hosted)


def _attn_dhn_call(dqkv, w_qkv, norm):
    t, n = dqkv.shape
    tm = _tile(t, 1024)

    def product(a_ref, w_ref):
        return lax.dot_general(a_ref[...], w_ref[...], _NT_DIMS, preferred_element_type=F32)

    return _dhn_call("attn_dhn", dqkv, pl.BlockSpec((tm, n), lambda i: (i, 0)), w_qkv,
                     pl.BlockSpec((D_MODEL, n), lambda i: (0, 0)), product, t, tm, norm)


def _local_step(x, target, w, *, norm_mix, norm_ffn, norm_f, pool_b, pool_scale, b_qkv, sinks, conv_w, conv_b,
                late_gather=None, reducer=None):
    cw = [conv_w[l].reshape(3, 2, D_FF) for l in range(2)]
    cb = [conv_b[l].reshape(2, D_FF) for l in range(2)]
    gm = [norm_mix[l:l + 1] for l in range(2)]
    gf = [norm_ffn[l:l + 1] for l in range(2)]

    w = dict(w)
    fwd_hooks = {}
    if late_gather:
        def gather_hook(stage):
            return (lambda env: late_gather(stage, w)), (lambda got: w.update(zip(_late_names(stage), got)))

        fwd_hooks = {stage: gather_hook(stage) for stage in GATHER_RIDES}

    h1, hn_f0 = _ride(fwd_hooks, "pool_fwd", lambda hosted: _pool_fwd_call(x, gm[0], w["pool_w"], pool_b, pool_scale,
                                                                            gf[0], hosted))
    h2, hn_a, u0, z0 = _ffn_fwd(hn_f0, h1, w, cw[0], cb[0], 0, hooks=fwd_hooks, next_norm=gm[1],
                                gathers_w_in=bool(late_gather))
    qkv = _mm_nn(hn_a, w["w_qkv"], "attn_qkv", bias=b_qkv, out_dtype=MXU_DTYPE, tn=768)
    o, = _ride(fwd_hooks, "attn_fwd", lambda hosted: _attn_fwd_call(qkv, sinks, hosted))
    h3, hn_f1 = _ride(fwd_hooks, "attn_out", lambda hosted: _mm_nn(o, w["w_o"], "attn_out", residual=h2, hosted=hosted,
                                                                   norm_g=gf[1]))
    loss, dh, dh16, d_norm_f, u1, z1 = _ffn_fwd(hn_f1, h3, w, cw[1], cb[1], 1, loss_head=(norm_f, target))

    rd = reducer
    first, mid, last = _Reducer.FIRST, _Reducer.MID, _Reducer.LAST
    hooks1 = hooks_attn = hooks0 = hooks_pool = None
    if rd:
        hooks1 = {"dhn": (lambda env: rd.swap(first, {"w_in": env["g_in"], "w_out": env["g_out"]}),
                          lambda got: rd.pair_sums(first, got))}
        hooks_attn = {"attn_bwd": (lambda env: rd.scatter(first), lambda got: rd.chip_sums(first, got))}

        def mid_and_last(got):
            rd.shared(mid, got[:len(mid)])
            rd.pair_sums(last, got[len(mid):])

        hooks0 = {
            "dz": (lambda env: rd.share(first), lambda got: rd.shared(first, got)),
            "dglu": (lambda env: rd.swap(mid, {"w_out": env["g_out"], "w_qkv": g_qkv, "w_o": g_o4}),
                     lambda got: rd.pair_sums(mid, got)),
            "dwin": (lambda env: rd.scatter(mid), lambda got: rd.chip_sums(mid, got)),
            "dhn": (lambda env: _both(rd.share(mid), rd.swap(last, {"w_in": env["g_in"]})), mid_and_last),
        }
        hooks_pool = {"pool_bwd": (lambda env: rd.scatter(last), lambda got: rd.chip_sums(last, got))}
    dh, dh16, d_gf1, g_in, g_out, dcw1, dcb1 = _ffn_bwd(
        dh16, hn_f1, u1, z1, w["w_in"], w["w_out"], cw[1], cb[1], 1, None, None, (h3, gf[1], dh), hooks=hooks1)
    do = _mm_nt(dh16, w["w_o"], "attn_do", out_dtype=MXU_DTYPE)
    g_o = _mm_tn(o, dh16, "attn_dwo", tn=WGRAD_TILE, tk=x.shape[0], out_dtype=WIRE_DTYPE)
    g_o4 = g_o.reshape(N_CHIPS, 2, W_O_ROWS // 2, D_MODEL)
    dq, dk, dv, d_bqkv, d_sinks = _ride(hooks_attn, "attn_bwd", lambda hosted: _attn_bwd_call(qkv, sinks, do, hosted))
    dqkv = jnp.concatenate([dq, dk.astype(MXU_DTYPE), dv.astype(MXU_DTYPE)], axis=1)
    g_qkv = _mm_tn(hn_a, dqkv, "attn_dwqkv", tn=WGRAD_TILE, tk=x.shape[0], out_dtype=WIRE_DTYPE)
    dh, dh16, d_gm1 = _attn_dhn_call(dqkv, w["w_qkv"], (h2, gm[1], dh))
    dh, _, d_gf0, g_in, g_out, dcw0, dcb0 = _ffn_bwd(
        dh16, hn_f0, u0, z0, w["w_in"], w["w_out"], cw[0], cb[0], 0, g_in, g_out, (h1, gf[0], dh), hooks=hooks0)
    grad_x, g_pool, dvec = _ride(hooks_pool, "pool_bwd", lambda hosted: _pool_bwd_call(
        x, dh, gm[0], w["pool_w"], pool_b, pool_scale, hosted))
    if rd:
        rd.shared(last, _alone(rd.share(last), "share_last"))

    grads = {
        "w_in": g_in, "w_out": g_out, "w_qkv": g_qkv, "w_o": g_o, "pool_w": g_pool,
        "pool_b": dvec[0:1], "pool_scale": dvec[1:2], "b_qkv": d_bqkv, "sinks": d_sinks[0:1, :N_HEADS],
        "norm_mix": jnp.concatenate([dvec[2:3], d_gm1], axis=0), "norm_ffn": jnp.concatenate([d_gf0, d_gf1], axis=0),
        "conv_w": jnp.stack([dcw0.reshape(3, 2 * D_FF), dcw1.reshape(3, 2 * D_FF)]),
        "conv_b": jnp.stack([dcb0.reshape(2 * D_FF), dcb1.reshape(2 * D_FF)]), "norm_f": d_norm_f,
    }
    return loss, grad_x, grads


W_IN_COLS = 2 * D_FF // N_CHIPS
W_OUT_ROWS = D_FF // N_CHIPS
QKV_COLS = QKV_DIM // N_CHIPS
W_O_ROWS = D_MODEL // N_CHIPS
POOL_ROWS = POOL_GC // N_CHIPS
BIG = ("w_in", "w_out", "w_qkv", "w_o", "pool_w")


def _mesh_place():
    x, y, c = lax.axis_index("x"), lax.axis_index("y"), lax.axis_index("c")
    chips = [(1 - x, y), (x, 1 - y), (1 - x, 1 - y)]
    return x, y, c, chips


def _col(k, width):
    return pl.ds(pl.multiple_of(k * width, 128), width)


def _row(k, height):
    return pl.ds(pl.multiple_of(k * height, 16), height)


def _full_piece(name, ref, k, h):
    if name == "w_in":
        return ref.at[h, :, _col(k, W_IN_COLS)]
    if name == "w_out":
        return ref.at[h, _row(k, W_OUT_ROWS), :]
    if name == "w_qkv":
        return ref.at[_row(h, D_MODEL // 2), _col(k, QKV_COLS)]
    if name == "w_o":
        return ref.at[_row(2 * k + h, W_O_ROWS // 2), :]
    return ref.at[pl.ds(2 * h, 2), _row(k, POOL_ROWS), :]


def _shard_half(name, ref, h):
    if name in ("w_in", "w_out"):
        return ref.at[h]
    if name == "w_qkv":
        return ref.at[_row(h, D_MODEL // 2), :]
    if name == "w_o":
        return ref.at[_row(h, W_O_ROWS // 2), :]
    return ref.at[pl.ds(2 * h, 2)]


FULL_SHAPES = {"w_in": (2, D_MODEL, 2 * D_FF), "w_out": (2, D_FF, D_MODEL), "w_qkv": (D_MODEL, QKV_DIM),
               "w_o": (D_MODEL, D_MODEL), "pool_w": (4, POOL_GC, POOL_GC)}
SHARD_SHAPES = {"w_in": (2, D_MODEL, W_IN_COLS), "w_out": (2, W_OUT_ROWS, D_MODEL), "w_qkv": (D_MODEL, QKV_COLS),
                "w_o": (W_O_ROWS, D_MODEL), "pool_w": (4, POOL_ROWS, POOL_GC)}
HALF_SHAPES = {"w_in": (D_MODEL, W_IN_COLS), "w_out": (W_OUT_ROWS, D_MODEL), "w_qkv": (D_MODEL // 2, QKV_COLS),
               "w_o": (W_O_ROWS // 2, D_MODEL), "pool_w": (2, POOL_ROWS, POOL_GC)}


def _place_shard(name, shard, me_arr):
    if name == "w_in":
        blk, grid = (1, 256, W_IN_COLS), (2, D_MODEL // 256)
        src, dst = (lambda a, b, me: (a, b, 0)), (lambda a, b, me: (a, b, me[0]))
    elif name == "w_out":
        blk, grid = (1, W_OUT_ROWS, D_MODEL), (2, 1)
        src, dst = (lambda a, b, me: (a, 0, 0)), (lambda a, b, me: (a, me[0], 0))
    elif name == "w_qkv":
        blk, grid = (256, QKV_COLS), (1, D_MODEL // 256)
        src, dst = (lambda a, b, me: (b, 0)), (lambda a, b, me: (b, me[0]))
    elif name == "w_o":
        blk, grid = (W_O_ROWS, D_MODEL), (1, 1)
        src, dst = (lambda a, b, me: (0, 0)), (lambda a, b, me: (me[0], 0))
    else:
        blk, grid = (4, POOL_ROWS, POOL_GC), (1, 1)
        src, dst = (lambda a, b, me: (0, 0, 0)), (lambda a, b, me: (0, me[0], 0))

    def body(me_ref, s_ref, o_ref):
        o_ref[...] = s_ref[...].astype(o_ref.dtype)

    return pl.pallas_call(
        body, name=f"place_{name}",
        grid_spec=pltpu.PrefetchScalarGridSpec(num_scalar_prefetch=1, grid=grid, in_specs=[pl.BlockSpec(blk, src)],
                                               out_specs=pl.BlockSpec(blk, dst)),
        out_shape=jax.ShapeDtypeStruct(FULL_SHAPES[name], WIRE_DTYPE), compiler_params=_params("parallel", "parallel"),
    )(me_arr, shard)


def _when_mover(half, c, fn):
    if half is None:
        fn(c)
    else:
        pl.when(c == half)(lambda: fn(half))


def _when_taker(half, c, fn):
    if half is None:
        fn(1 - c)
    else:
        pl.when(c != half)(lambda: fn(half))


def _remote(ref, sems, t, j, dev):
    send, recv = sems
    return pltpu.make_async_remote_copy(src_ref=ref, dst_ref=ref, send_sem=send.at[t, j], recv_sem=recv.at[t, j],
                                        device_id=dev, device_id_type=MESH)


def _gather_ici(items, refs, sems, finish, peers=(0, 1, 2)):
    x, y, c, chips = _mesh_place()
    me = 2 * x + y
    for t, (nm, half) in enumerate(items):
        def go(h, t=t, nm=nm):
            mine = _full_piece(nm, refs[t], me, h)
            for j, (cx, cy) in enumerate(chips):
                if j not in peers:
                    continue
                if not finish:
                    _remote(mine, sems, t, j, (cx, cy, c)).start()
                else:
                    _remote(_full_piece(nm, refs[t], 2 * cx + cy, h), sems, t, j, (cx, cy, c)).wait_recv()
                    _remote(mine, sems, t, j, (cx, cy, c)).wait_send()

        _when_mover(half, c, go)


def _gather_pass(items, refs, sems, finish):
    x, y, c, chips = _mesh_place()
    sibling = (x, y, 1 - c)
    for t, (nm, half) in enumerate(items):
        def give(h, t=t, nm=nm):
            for j, (cx, cy) in enumerate(chips):
                got = _remote(_full_piece(nm, refs[t], 2 * cx + cy, h), sems, t, j, sibling)
                if not finish:
                    got.start()
                else:
                    got.wait_send()

        def take(h, t=t, nm=nm):
            for j, (cx, cy) in enumerate(chips):
                _remote(_full_piece(nm, refs[t], 2 * cx + cy, h), sems, t, j, sibling).wait_recv()

        _when_mover(half, c, give)
        if finish:
            _when_taker(half, c, take)


def _gather_sems(n_items):
    return [pltpu.SemaphoreType.DMA((n_items, 3)), pltpu.SemaphoreType.DMA((n_items, 3))]


EARLY_ITEMS = (("pool_w", None),)
_LATE_A = (("w_out", 0),)
_LATE_B = (("w_in", 1),)
_LATE_C = (("w_out", 1),)
_LATE_D = (("w_qkv", None), ("w_o", None))
_EARLY_W_IN = (("w_in", 0),)
GATHER_RIDES = {"pool_fwd": ((), _EARLY_W_IN, ()), "in": ((), _LATE_A, ()), "glu": (_LATE_A, _LATE_D, ()),
                "out": (_LATE_D, _LATE_B, ()), "attn_fwd": (_LATE_B, _LATE_C, ()), "attn_out": (_LATE_C, (), ())}
NEIGHBOURS = (0, 1)
GATHER_PEERS = {"pool_fwd": NEIGHBOURS}


def _names(items):
    return [nm for nm, _ in items]


def _gather_early(full, small):
    items = EARLY_ITEMS
    n = len(items)

    def body(*refs):
        small_in = refs[n]
        outs, small_out = refs[n + 1:2 * n + 1], refs[2 * n + 1]
        scratch = refs[2 * n + 2:]
        ici_sems, pass_sems = scratch[0:2], scratch[2:4]
        local_sem, small_send, small_recv = scratch[4:]
        x, y, c, chips = _mesh_place()
        me = 2 * x + y
        local = pltpu.make_async_copy(small_in, small_out.at[me], local_sem.at[0])
        local.start()
        small_sends = [pltpu.make_async_remote_copy(
            src_ref=small_in, dst_ref=small_out.at[me], send_sem=small_send.at[j], recv_sem=small_recv.at[j],
            device_id=(cx, cy, c), device_id_type=MESH) for j, (cx, cy) in enumerate(chips)]
        for cp in small_sends:
            cp.start()
        _gather_ici(items, outs, ici_sems, finish=False)
        _gather_ici(items, outs, ici_sems, finish=True)
        _gather_pass(items, outs, pass_sems, finish=False)
        _gather_pass(items, outs, pass_sems, finish=True)
        for j, (cx, cy) in enumerate(chips):
            got = small_out.at[2 * cx + cy]
            pltpu.make_async_remote_copy(src_ref=got, dst_ref=got, send_sem=small_send.at[j],
                                         recv_sem=small_recv.at[j], device_id=(cx, cy, c), device_id_type=MESH).wait_recv()
        for cp in small_sends:
            cp.wait_send()
        local.wait()

    arrays = [full[nm] for nm in _names(items)]
    out_shape = [jax.ShapeDtypeStruct(a.shape, a.dtype) for a in arrays]
    out_shape.append(jax.ShapeDtypeStruct((N_CHIPS,) + small.shape, F32))
    res = pl.pallas_call(
        body, name="gather_early", in_specs=[ANY] * (n + 1), out_specs=[ANY] * (n + 1), out_shape=out_shape,
        input_output_aliases={t: t for t in range(n)},
        scratch_shapes=_gather_sems(n) + _gather_sems(n) + [pltpu.SemaphoreType.DMA((1,)), pltpu.SemaphoreType.DMA((3,)),
                                                            pltpu.SemaphoreType.DMA((3,))],
    )(*arrays, small)
    return {**full, **dict(zip(_names(items), res[:n]))}, res[n]


def _late_gather_host(stage, w):
    passing, sending, direct = GATHER_RIDES[stage]
    n_pass, n_send = len(passing), len(sending)

    def run(src, refs, new, sems, finish):
        if passing:
            _gather_pass(passing, refs[:n_pass], sems[:2], finish)
        if sending:
            _gather_ici(sending, refs[n_pass:n_pass + n_send], sems[2 * bool(passing):2 * bool(passing) + 2], finish,
                        GATHER_PEERS.get(stage, (0, 1, 2)))
        if direct:
            _gather_direct(direct, refs[n_pass + n_send:], sems[-2:], finish)

    sems = ((_gather_sems(n_pass) if passing else []) + (_gather_sems(n_send) if sending else [])
            + ([pltpu.SemaphoreType.DMA((len(direct), 6))] * 2 if direct else []))
    return _Hosted([], [w[nm] for nm in _late_names(stage)], [], sems, functools.partial(run, finish=False),
                   functools.partial(run, finish=True))


def _late_names(stage):
    return [nm for items in GATHER_RIDES[stage] for nm in _names(items)]


def _gather_direct(items, refs, sems, finish):
    x, y, c, chips = _mesh_place()
    me = 2 * x + y
    for t, (nm, half) in enumerate(items):
        assert half is None
        mine = _full_piece(nm, refs[t], me, c)
        for j, (cx, cy) in enumerate(chips):
            for other in range(2):
                pc = c if other == 0 else 1 - c
                if not finish:
                    _remote(mine, sems, t, 2 * j + other, (cx, cy, pc)).start()
                else:
                    _remote(_full_piece(nm, refs[t], 2 * cx + cy, pc), sems, t, 2 * j + other, (cx, cy, pc)).wait_recv()
                    _remote(mine, sems, t, 2 * j + other, (cx, cy, pc)).wait_send()


def _other_half_src(name, ref, h):
    if name in ("w_in", "w_out"):
        return ref.at[h]
    if name == "w_qkv":
        return ref.at[_row(h, D_MODEL // 2), :]
    if name == "w_o":
        return ref.at[:, pl.ds(h, 1)]
    return ref.at[pl.ds(2 * h, 2)]


SLAB_SHAPES = {"w_in": (D_MODEL, 2 * D_FF), "w_out": (D_FF, D_MODEL), "w_qkv": (D_MODEL // 2, QKV_DIM),
               "w_o": (N_CHIPS, 1, W_O_ROWS // 2, D_MODEL), "pool_w": (2, POOL_GC, POOL_GC)}


def _item_tag(item):
    return item[0] if item[1] is None else f"{item[0]}{item[1]}"


def _reduce_swap(items, grads, slabs, sems, finish):
    x, y, c, _ = _mesh_place()
    send, recv = sems
    for t, (nm, half) in enumerate(items):
        def copy(h, t=t, nm=nm):
            return pltpu.make_async_remote_copy(
                src_ref=_other_half_src(nm, grads[t], h), dst_ref=slabs[t], send_sem=send.at[t], recv_sem=recv.at[t],
                device_id=(x, y, 1 - c), device_id_type=MESH)

        def give(h):
            if finish:
                copy(h).wait_send()
            else:
                copy(h).start()

        if half is None:
            give(1 - c)
            if finish:
                copy(c).wait_recv()
        else:
            pl.when(c != half)(lambda: give(half))
            if finish:
                pl.when(c == half)(lambda: copy(half).wait_recv())


def _reduce_scatter(items, sums, parts, sems, finish):
    _, _, c, chips = _mesh_place()
    send, recv = sems
    for t, (nm, half) in enumerate(items):
        def go(h, t=t, nm=nm):
            for j, (cx, cy) in enumerate(chips):
                cp = pltpu.make_async_remote_copy(
                    src_ref=_slab_piece(nm, sums[t], 2 * cx + cy), dst_ref=parts[t].at[j], send_sem=send.at[t, j],
                    recv_sem=recv.at[t, j], device_id=(cx, cy, c), device_id_type=MESH)
                if finish:
                    cp.wait_recv()
                    cp.wait_send()
                else:
                    cp.start()

        _when_mover(half, c, go)


def _reduce_share(items, shards, sems, finish):
    x, y, c, _ = _mesh_place()
    send, recv = sems
    for t, (nm, half) in enumerate(items):
        def copy(h, t=t, nm=nm):
            part = _shard_half(nm, shards[t], h)
            return pltpu.make_async_remote_copy(src_ref=part, dst_ref=part, send_sem=send.at[t], recv_sem=recv.at[t],
                                                device_id=(x, y, 1 - c), device_id_type=MESH)

        def give(h):
            if finish:
                copy(h).wait_send()
            else:
                copy(h).start()

        _when_mover(half, c, give)
        if finish:
            _when_taker(half, c, lambda h: copy(h).wait_recv())


def _pair_sems(n):
    return [pltpu.SemaphoreType.DMA((n,)), pltpu.SemaphoreType.DMA((n,))]


def _swap_host(items, grads):
    def run(src, arr, new, sems, finish):
        _reduce_swap(items, src, new, sems, finish)

    return _Hosted([grads[nm] for nm, _ in items], [],
                   [jax.ShapeDtypeStruct(SLAB_SHAPES[nm], grads[nm].dtype) for nm, _ in items],
                   _pair_sems(len(items)), functools.partial(run, finish=False), functools.partial(run, finish=True))


def _scatter_host(items, sums):
    def run(src, arr, new, sems, finish):
        _reduce_scatter(items, src, new, sems, finish)

    new = [jax.ShapeDtypeStruct((N_CHIPS - 1,) + HALF_SHAPES[nm], WIRE_DTYPE) for nm, _ in items]
    return _Hosted(list(sums), [], new, _gather_sems(len(items)), functools.partial(run, finish=False),
                   functools.partial(run, finish=True))


def _share_host(items, shards):
    def run(src, arr, new, sems, finish):
        _reduce_share(items, arr, sems, finish)

    return _Hosted([], list(shards), [], _pair_sems(len(items)), functools.partial(run, finish=False),
                   functools.partial(run, finish=True))


def _alone(hosted, name):
    def body(ins, outs, scratch, first, last):
        first()
        last()

    return _call(body, hosted, name=name, grid=(1,), in_specs=[], out_specs=[], out_shape=[], operands=(),
                 semantics=("arbitrary",))


def _pair_sum(item, grad, slab, place):
    name, half = item
    shape = SLAB_SHAPES[name]
    hsel = (lambda p: p[1]) if half is None else (lambda p: half)
    if name == "w_in":
        blk = (256, 2 * D_FF)
        grid = (D_MODEL // 256,)
        g_spec = pl.BlockSpec((None,) + blk, lambda i, p: (hsel(p), i, 0))
        s_spec = pl.BlockSpec(blk, lambda i, p: (i, 0))
    elif name == "w_out":
        blk = (W_OUT_ROWS, D_MODEL)
        grid = (N_CHIPS,)
        g_spec = pl.BlockSpec((None,) + blk, lambda i, p: (hsel(p), i, 0))
        s_spec = pl.BlockSpec(blk, lambda i, p: (i, 0))
    elif name == "w_qkv":
        grid = (1,)
        g_spec = pl.BlockSpec(shape, lambda i, p: (hsel(p), 0))
        s_spec = pl.BlockSpec(shape, lambda i, p: (0, 0))
    else:
        grid = (1,)
        g_spec = pl.BlockSpec(shape, lambda i, p: (0, hsel(p), 0, 0))
        s_spec = pl.BlockSpec(shape, lambda i, p: (0, 0, 0, 0))

    def body(p_ref, g_ref, s_ref, o_ref):
        @pl.when(hsel(p_ref) == p_ref[1])
        def _():
            o_ref[...] = (g_ref[...].astype(F32) + s_ref[...].astype(F32)).astype(o_ref.dtype)

    return pl.pallas_call(
        body, name=f"pair_sum_{_item_tag(item)}",
        grid_spec=pltpu.PrefetchScalarGridSpec(num_scalar_prefetch=1, grid=grid, in_specs=[g_spec, s_spec],
                                               out_specs=s_spec),
        out_shape=jax.ShapeDtypeStruct(shape, WIRE_DTYPE), compiler_params=_params("parallel"),
    )(place, grad, slab)


def _slab_piece(name, ref, k):
    if name == "w_in":
        return ref.at[:, _col(k, W_IN_COLS)]
    if name == "w_out":
        return ref.at[_row(k, W_OUT_ROWS), :]
    if name == "w_qkv":
        return ref.at[:, _col(k, QKV_COLS)]
    if name == "w_o":
        return ref.at[k, 0]
    return ref.at[:, _row(k, POOL_ROWS), :]


def _sum_chips(item, slab, parts, place, into=None):
    name, sel = item
    half = HALF_SHAPES[name]
    hsel = (lambda p: p[1]) if sel is None else (lambda p: sel)
    if name == "w_in":
        blk, grid = (256, W_IN_COLS), (D_MODEL // 256,)
        own = pl.BlockSpec(blk, lambda i, p: (i, p[0]))
        got = pl.BlockSpec((N_CHIPS - 1,) + blk, lambda i, p: (0, i, 0))
        out = pl.BlockSpec((None,) + blk, lambda i, p: (hsel(p), i, 0))
    elif name == "w_out":
        grid = (1,)
        own = pl.BlockSpec(half, lambda i, p: (p[0], 0))
        got = pl.BlockSpec((N_CHIPS - 1,) + half, lambda i, p: (0, 0, 0))
        out = pl.BlockSpec((None,) + half, lambda i, p: (hsel(p), 0, 0))
    elif name == "w_qkv":
        grid = (1,)
        own = pl.BlockSpec(half, lambda i, p: (0, p[0]))
        got = pl.BlockSpec((N_CHIPS - 1,) + half, lambda i, p: (0, 0, 0))
        out = pl.BlockSpec(half, lambda i, p: (hsel(p), 0))
    else:
        grid = (1,)
        own = pl.BlockSpec((None, None) + half, lambda i, p: (p[0], 0, 0, 0))
        got = pl.BlockSpec((N_CHIPS - 1,) + half, lambda i, p: (0, 0, 0))
        out = pl.BlockSpec(half, lambda i, p: (hsel(p), 0))

    def body(p_ref, own_ref, got_ref, *rest):
        o_ref = rest[-1]

        @pl.when(hsel(p_ref) == p_ref[1])
        def _():
            acc = own_ref[...].astype(F32)
            for j in range(N_CHIPS - 1):
                acc = acc + got_ref[j].astype(F32)
            o_ref[...] = acc

    in_specs, operands, aliases = [own, got], [place, slab, parts], {}
    if into is not None:
        in_specs.append(ANY)
        operands.append(into)
        aliases = {3: 0}
    return pl.pallas_call(
        body, name=f"sum_chips_{_item_tag(item)}",
        grid_spec=pltpu.PrefetchScalarGridSpec(num_scalar_prefetch=1, grid=grid, in_specs=in_specs, out_specs=out),
        out_shape=jax.ShapeDtypeStruct(SHARD_SHAPES[name], F32), input_output_aliases=aliases,
        compiler_params=_params("parallel"),
    )(*operands)


class _Reducer:
    FIRST = (("w_in", 1), ("w_out", 1))
    MID = (("w_out", 0), ("w_qkv", None), ("w_o", None))
    LAST = (("w_in", 0),)

    def __init__(self, place):
        self.place = place
        self.shards = {}
        self.state = {}

    def swap(self, items, grads):
        self.state[items] = {"grads": [grads[nm] for nm, _ in items]}
        return _swap_host(items, grads)

    def pair_sums(self, items, slabs):
        st = self.state[items]
        st["sums"] = [_pair_sum(it, g, s, self.place) for it, g, s in zip(items, st["grads"], slabs)]

    def scatter(self, items):
        return _scatter_host(items, self.state[items]["sums"])

    def chip_sums(self, items, parts):
        for it, s, p in zip(items, self.state[items]["sums"], parts):
            self.shards[it[0]] = _sum_chips(it, s, p, self.place, into=self.shards.get(it[0]))

    def share(self, items):
        return _share_host(items, [self.shards[nm] for nm, _ in items])

    def shared(self, items, updated):
        self.shards.update(dict(zip(_names(items), updated)))


N_DEV = 8


def _all_reduce_small(vec, g_pool):
    def body(v_ref, gp_ref, o_ref, po_ref, slots, pslots, send_sems, recv_sems, psend, precv, hsend, hrecv):
        x, y, c, _ = _mesh_place()
        me = 4 * x + 2 * y + c

        def pool_piece(chip, h):
            return gp_ref.at[pl.ds(2 * h, 2), pl.ds(pl.multiple_of(chip * POOL_ROWS, POOL_ROWS), POOL_ROWS), :]

        slots[me] = v_ref[...]
        pslots[me] = pool_piece(2 * x + y, c)[...]
        copies = []
        for d in range(1, N_DEV):
            peer = (x ^ (d >> 2), y ^ ((d >> 1) & 1), c ^ (d & 1))
            copies.append(pltpu.make_async_remote_copy(
                src_ref=v_ref, dst_ref=slots.at[me], send_sem=send_sems.at[d - 1], recv_sem=recv_sems.at[d - 1],
                device_id=peer, device_id_type=MESH))
            copies.append(pltpu.make_async_remote_copy(
                src_ref=pool_piece(2 * peer[0] + peer[1], peer[2]), dst_ref=pslots.at[me], send_sem=psend.at[d - 1],
                recv_sem=precv.at[d - 1], device_id=peer, device_id_type=MESH))
        for cp in copies:
            cp.start()
        for d in range(1, N_DEV):
            peer = 4 * (x ^ (d >> 2)) + 2 * (y ^ ((d >> 1) & 1)) + (c ^ (d & 1))
            for buf, ss, rs in ((slots, send_sems, recv_sems), (pslots, psend, precv)):
                got = buf.at[peer]
                pltpu.make_async_remote_copy(src_ref=got, dst_ref=got, send_sem=ss.at[d - 1], recv_sem=rs.at[d - 1],
                                             device_id=(x, y, c), device_id_type=MESH).wait_recv()
        for cp in copies:
            cp.wait_send()
        acc, pacc = slots[0], pslots[0]
        for d in range(1, N_DEV):
            acc, pacc = acc + slots[d], pacc + pslots[d]
        o_ref[...] = acc
        po_ref[pl.ds(2 * c, 2)] = pacc
        mine, theirs = po_ref.at[pl.ds(2 * c, 2)], po_ref.at[pl.ds(2 * (1 - c), 2)]
        give = pltpu.make_async_remote_copy(src_ref=mine, dst_ref=mine, send_sem=hsend.at[0], recv_sem=hrecv.at[0],
                                            device_id=(x, y, 1 - c), device_id_type=MESH)
        give.start()
        pltpu.make_async_remote_copy(src_ref=theirs, dst_ref=theirs, send_sem=hsend.at[0], recv_sem=hrecv.at[0],
                                     device_id=(x, y, 1 - c), device_id_type=MESH).wait_recv()
        give.wait_send()

    vm = pl.BlockSpec(memory_space=pltpu.VMEM)
    piece = (2, POOL_ROWS, POOL_GC)
    return pl.pallas_call(
        body, name="all_reduce_small", in_specs=[vm, vm], out_specs=[vm, vm],
        out_shape=[jax.ShapeDtypeStruct(vec.shape, F32), jax.ShapeDtypeStruct(SHARD_SHAPES["pool_w"], F32)],
        scratch_shapes=[pltpu.VMEM((N_DEV,) + vec.shape, F32), pltpu.VMEM((N_DEV,) + piece, F32)]
        + [pltpu.SemaphoreType.DMA((N_DEV - 1,))] * 4 + [pltpu.SemaphoreType.DMA((1,))] * 2,
    )(vec, g_pool)


def _adamw_call(w, g, m, v, name, copy_grad=False):
    shape = w.shape
    cols = shape[-1]
    rows = w.size // cols
    tr = rows
    for cand in (256, 128, 64, 32, 16, 8):
        if rows > cand and rows % cand == 0 and cand * cols * 4 <= 2 * 1024 * 1024:
            tr = cand
            break

    def body(w_ref, g_ref, m_ref, v_ref, d_ref, nm_ref, nv_ref, *g_out):
        gv = g_ref[...]
        if copy_grad:
            g_out[0][...] = gv
        mn = ADAM_B1 * m_ref[...] + (1.0 - ADAM_B1) * gv
        vn = ADAM_B2 * v_ref[...] + (1.0 - ADAM_B2) * jnp.square(gv)
        m_hat = mn / (1.0 - ADAM_B1 ** ADAM_STEP)
        v_hat = vn / (1.0 - ADAM_B2 ** ADAM_STEP)
        d_ref[...] = -ADAM_LR * (m_hat / (jnp.sqrt(v_hat) + ADAM_EPS) + ADAM_WD * w_ref[...])
        nm_ref[...] = mn
        nv_ref[...] = vn

    spec = pl.BlockSpec((tr, cols), lambda i: (i, 0))
    flat = lambda a: a.reshape(rows, cols)
    n_out = 4 if copy_grad else 3
    outs = pl.pallas_call(
        body, name=name, grid=(rows // tr,), in_specs=[spec] * 4, out_specs=[spec] * n_out,
        out_shape=[jax.ShapeDtypeStruct((rows, cols), F32)] * n_out, compiler_params=_params("parallel"),
    )(flat(w), flat(g), flat(m), flat(v))
    return tuple(o.reshape(shape) for o in outs)


WEIGHTS = ("pool_w", "pool_b", "pool_scale", "attn_w_qkv", "attn_b_qkv", "attn_sinks", "attn_w_o", "norm_mix",
           "norm_ffn", "ffn_w_in", "ffn_conv_w", "ffn_conv_b", "ffn_w_out", "norm_f")
SMALL_GATHER = 128 * 71
_SMALL_SIZES = (("pool_b", 1024), ("pool_scale", 1024), ("b_qkv", 1536), ("sinks", 128), ("norm_mix", 2048),
                ("norm_ffn", 2048), ("conv_w", 6 * 2 * D_FF), ("conv_b", 4 * D_FF), ("norm_f", 1024))
SMALL_COLS = 6784


def _pack_small(grads):
    flat = jnp.concatenate([jnp.pad(grads[nm].reshape(-1), (0, size - grads[nm].size)) for nm, size in _SMALL_SIZES])
    return jnp.pad(flat, (0, 8 * SMALL_COLS - flat.size)).reshape(8, SMALL_COLS)


def _unpack_small(vec):
    flat = vec.reshape(-1)
    out, off = {}, 0
    for nm, size in _SMALL_SIZES:
        out[nm] = flat[off:off + size]
        off += size
    return out


def kernel(x, pool_w, pool_b, pool_scale, attn_w_qkv, attn_b_qkv, attn_sinks, attn_w_o, norm_mix, norm_ffn, ffn_w_in, ffn_conv_w, ffn_conv_b, ffn_w_out, norm_f, loss_target, m_pool_w, m_pool_b, m_pool_scale, m_attn_w_qkv, m_attn_b_qkv, m_attn_sinks, m_attn_w_o, m_norm_mix, m_norm_ffn, m_ffn_w_in, m_ffn_conv_w, m_ffn_conv_b, m_ffn_w_out, m_norm_f, v_pool_w, v_pool_b, v_pool_scale, v_attn_w_qkv, v_attn_b_qkv, v_attn_sinks, v_attn_w_o, v_norm_mix, v_norm_ffn, v_ffn_w_in, v_ffn_conv_w, v_ffn_conv_b, v_ffn_w_out, v_norm_f):
    weights = dict(pool_w=pool_w, pool_b=pool_b, pool_scale=pool_scale, attn_w_qkv=attn_w_qkv, attn_b_qkv=attn_b_qkv,
                   attn_sinks=attn_sinks, attn_w_o=attn_w_o, norm_mix=norm_mix, norm_ffn=norm_ffn, ffn_w_in=ffn_w_in,
                   ffn_conv_w=ffn_conv_w, ffn_conv_b=ffn_conv_b, ffn_w_out=ffn_w_out, norm_f=norm_f)
    m_in = dict(pool_w=m_pool_w, pool_b=m_pool_b, pool_scale=m_pool_scale, attn_w_qkv=m_attn_w_qkv,
                attn_b_qkv=m_attn_b_qkv, attn_sinks=m_attn_sinks, attn_w_o=m_attn_w_o, norm_mix=m_norm_mix,
                norm_ffn=m_norm_ffn, ffn_w_in=m_ffn_w_in, ffn_conv_w=m_ffn_conv_w, ffn_conv_b=m_ffn_conv_b,
                ffn_w_out=m_ffn_w_out, norm_f=m_norm_f)
    v_in = dict(pool_w=v_pool_w, pool_b=v_pool_b, pool_scale=v_pool_scale, attn_w_qkv=v_attn_w_qkv,
                attn_b_qkv=v_attn_b_qkv, attn_sinks=v_attn_sinks, attn_w_o=v_attn_w_o, norm_mix=v_norm_mix,
                norm_ffn=v_norm_ffn, ffn_w_in=v_ffn_w_in, ffn_conv_w=v_ffn_conv_w, ffn_conv_b=v_ffn_conv_b,
                ffn_w_out=v_ffn_w_out, norm_f=v_norm_f)
    chip = 2 * lax.axis_index("x") + lax.axis_index("y")
    core = lax.axis_index("c")

    place = jnp.stack([chip, core]).astype(jnp.int32)
    shards = {"w_in": ffn_w_in, "w_out": ffn_w_out, "w_qkv": attn_w_qkv[0], "w_o": attn_w_o[0], "pool_w": pool_w[0]}
    placed = {nm: _place_shard(nm, shards[nm], place) for nm in BIG}
    small = jnp.concatenate([pool_b.reshape(-1), attn_b_qkv.reshape(-1), ffn_conv_w.reshape(-1)]).reshape(1, SMALL_GATHER)
    full, small_all = _gather_early(placed, small)
    small_all = small_all.reshape(N_CHIPS, SMALL_GATHER)
    pool_b_full = small_all[:, :256].reshape(N_CHIPS, 4, POOL_ROWS).transpose(1, 0, 2).reshape(1, D_MODEL)
    b_qkv_full = small_all[:, 256:640].reshape(1, QKV_DIM)
    conv_w_full = small_all[:, 640:].reshape(N_CHIPS, 2, 3, W_IN_COLS).transpose(1, 2, 0, 3).reshape(2, 3, 2 * D_FF)

    reducer = _Reducer(place)
    loss, grad_x, grads = _local_step(
        x[0], loss_target[0], full, norm_mix=norm_mix, norm_ffn=norm_ffn, norm_f=norm_f.reshape(1, D_MODEL),
        pool_b=pool_b_full, pool_scale=pool_scale, b_qkv=b_qkv_full, sinks=attn_sinks[0], conv_w=conv_w_full,
        conv_b=ffn_conv_b, late_gather=_late_gather_host, reducer=reducer)
    loss = lax.psum(loss[0, 0], ("x", "y", "c"))
    reduced = dict(reducer.shards)

    small_sum, reduced["pool_w"] = _all_reduce_small(_pack_small(grads), grads["pool_w"])
    small_g = _unpack_small(small_sum)
    pool_b_g = lax.dynamic_slice_in_dim(small_g["pool_b"].reshape(4, N_CHIPS, POOL_ROWS), chip, 1, axis=1)
    b_qkv_g = lax.dynamic_slice_in_dim(small_g["b_qkv"].reshape(N_CHIPS, QKV_COLS), chip, 1, axis=0)
    conv_w_g = lax.dynamic_slice_in_dim(small_g["conv_w"].reshape(2, 3, N_CHIPS, W_IN_COLS), chip, 1, axis=2)
    grad_w = {
        "pool_w": reduced["pool_w"].reshape(pool_w.shape), "pool_b": pool_b_g.reshape(pool_b.shape),
        "pool_scale": small_g["pool_scale"].reshape(pool_scale.shape),
        "attn_w_qkv": reduced["w_qkv"].reshape(attn_w_qkv.shape), "attn_b_qkv": b_qkv_g.reshape(attn_b_qkv.shape),
        "attn_sinks": small_g["sinks"][:N_HEADS].reshape(attn_sinks.shape),
        "attn_w_o": reduced["w_o"].reshape(attn_w_o.shape), "norm_mix": small_g["norm_mix"].reshape(norm_mix.shape),
        "norm_ffn": small_g["norm_ffn"].reshape(norm_ffn.shape), "ffn_w_in": reduced["w_in"],
        "ffn_conv_w": conv_w_g.reshape(ffn_conv_w.shape), "ffn_conv_b": small_g["conv_b"].reshape(ffn_conv_b.shape),
        "ffn_w_out": reduced["w_out"], "norm_f": small_g["norm_f"].reshape(norm_f.shape),
    }

    delta, new_m, new_v = {}, {}, {}
    exchanged = ("pool_w", "attn_w_qkv", "attn_w_o", "ffn_w_in", "ffn_w_out")
    for nm in WEIGHTS:
        res = _adamw_call(weights[nm], grad_w[nm], m_in[nm], v_in[nm], f"adamw_{nm}", copy_grad=nm in exchanged)
        delta[nm], new_m[nm], new_v[nm] = res[:3]
        if nm in exchanged:
            grad_w[nm] = res[3]
    return (loss, grad_x.reshape(x.shape), *[grad_w[nm] for nm in WEIGHTS], *[delta[nm] for nm in WEIGHTS],
            *[new_m[nm] for nm in WEIGHTS], *[new_v[nm] for nm in WEIGHTS])
```

```python
import functools
from typing import Callable, NamedTuple

import jax
import jax.numpy as jnp
from jax import lax
from jax.experimental import pallas as pl
from jax.experimental.pallas import tpu as pltpu

F32 = jnp.float32
MXU_DTYPE = jnp.bfloat16
WIRE_DTYPE = jnp.bfloat16
U_DTYPE = jnp.float32

D_MODEL = 1024
D_FF = 2816
QKV_DIM = 1536
HEAD_DIM = 64
N_HEADS = 16
N_KV_HEADS = 4
ATT_BLOCK = 128
POOL_WINDOWS = (2, 4, 8, 16)
POOL_GC = 256
POOL_HALO = 16
CONV_HALO = 8
RMS_EPS = 1e-6
ATT_SCALE = HEAD_DIM ** -0.5
ALIBI_SLOPES = tuple(2.0 ** (-8.0 / N_HEADS * (h + 1)) for h in range(N_HEADS))

ADAM_LR = 0.001
ADAM_B1 = 0.9
ADAM_B2 = 0.999
ADAM_EPS = 1e-08
ADAM_WD = 0.01
ADAM_STEP = 10

N_CHIPS = 4
MESH = pl.DeviceIdType.MESH
VMEM_LIMIT_BYTES = 56 * 1024 * 1024
ANY = pl.BlockSpec(memory_space=pl.ANY)


def _params(*semantics):
    return pltpu.CompilerParams(dimension_semantics=semantics, vmem_limit_bytes=VMEM_LIMIT_BYTES)


def _rms(x, g):
    return x * lax.rsqrt(jnp.mean(x * x, axis=-1, keepdims=True) + RMS_EPS) * g


def _rms_bwd(x, g, dy):
    rstd = lax.rsqrt(jnp.mean(x * x, axis=-1, keepdims=True) + RMS_EPS)
    xhat = x * rstd
    dxhat = dy * g
    dx = rstd * (dxhat - xhat * jnp.mean(dxhat * xhat, axis=-1, keepdims=True))
    return dx, dy * xhat


def _shift_down(x, s):
    return pltpu.roll(x, s, axis=0)


def _shift_up(x, s):
    return pltpu.roll(x, x.shape[0] - s, axis=0)


def _sigmoid(x):
    return 1.0 / (1.0 + jnp.exp(-x))


_DOT_DIMS = {"nn": ((1,), (0,)), "nt": ((1,), (1,)), "tn": ((0,), (0,))}


class _Hosted(NamedTuple):
    sources: list
    arrays: list
    new: list
    sems: list
    first: Callable
    last: Callable


def _call(body, hosted, *, name, grid, in_specs, out_specs, out_shape, operands, semantics, scratch_shapes=(),
          aliases=None):
    n_in, n_out, n_scr = len(in_specs), len(out_specs), len(scratch_shapes)
    aliases = dict(aliases or {})
    if hosted is None:
        def plain(*refs):
            body(refs[:n_in], refs[n_in:n_in + n_out], refs[n_in + n_out:], lambda: None, lambda: None)

        return pl.pallas_call(plain, name=name, grid=grid, in_specs=list(in_specs), out_specs=list(out_specs),
                              out_shape=list(out_shape), scratch_shapes=list(scratch_shapes),
                              input_output_aliases=aliases, compiler_params=_params(*semantics))(*operands)
    ns, na, nn = len(hosted.sources), len(hosted.arrays), len(hosted.new)

    def carrying(*refs):
        ins, src = refs[:n_in], refs[n_in:n_in + ns]
        o0 = n_in + ns + na
        outs, arr = refs[o0:o0 + n_out], refs[o0 + n_out:o0 + n_out + na]
        new = refs[o0 + n_out + na:o0 + n_out + na + nn]
        s0 = o0 + n_out + na + nn
        scratch, sems = refs[s0:s0 + n_scr], refs[s0 + n_scr:]
        ids = [pl.program_id(d) for d in range(len(grid))]
        is_first = functools.reduce(lambda p, q: p & q, [i == 0 for i in ids])
        is_last = functools.reduce(lambda p, q: p & q, [i == n - 1 for i, n in zip(ids, grid)])
        body(ins, outs, scratch, lambda: pl.when(is_first)(lambda: hosted.first(src, arr, new, sems)),
             lambda: pl.when(is_last)(lambda: hosted.last(src, arr, new, sems)))

    for t in range(na):
        aliases[n_in + ns + t] = n_out + t
    return pl.pallas_call(
        carrying, name=name, grid=grid, in_specs=list(in_specs) + [ANY] * (ns + na),
        out_specs=list(out_specs) + [ANY] * (na + nn),
        out_shape=list(out_shape) + [jax.ShapeDtypeStruct(a.shape, a.dtype) for a in hosted.arrays] + list(hosted.new),
        scratch_shapes=list(scratch_shapes) + list(hosted.sems), input_output_aliases=aliases,
        compiler_params=_params(*["arbitrary"] * len(grid)))(*operands, *hosted.sources, *hosted.arrays)


def _both(h1, h2):
    def run(which, src, arr, new, sems):
        cut = lambda seq, n: (seq[:n], seq[n:])
        (s1, s2), (a1, a2) = cut(src, len(h1.sources)), cut(arr, len(h1.arrays))
        (n1, n2), (m1, m2) = cut(new, len(h1.new)), cut(sems, len(h1.sems))
        getattr(h1, which)(s1, a1, n1, m1)
        getattr(h2, which)(s2, a2, n2, m2)

    return _Hosted(h1.sources + h2.sources, h1.arrays + h2.arrays, h1.new + h2.new, h1.sems + h2.sems,
                   functools.partial(run, "first"), functools.partial(run, "last"))


def _ride(hooks, stage, call, env=None):
    if not hooks or stage not in hooks:
        res = call(None)
        return list(res) if isinstance(res, (list, tuple)) else [res]
    make, done = hooks[stage]
    hosted = make(env)
    res = call(hosted)
    own = len(res) - len(hosted.arrays) - len(hosted.new)
    done(res[own:])
    return list(res[:own])


def _matmul(a, b, *, mode, name, grid, a_spec, b_spec, o_spec, out_shape, bias=None, bias_spec=None, residual=None,
            residual_spec=None, into=None, hosted=None, norm_g=None):
    nk = grid[2]
    dims = (_DOT_DIMS[mode], ((), ()))
    acc_shape = tuple(d for d in o_spec.block_shape if d is not None)

    def body(ins, outs, scratch, first, last):
        a_ref, b_ref = ins[0], ins[1]
        pos = 2
        bias_ref = res_ref = None
        if bias is not None:
            bias_ref = ins[pos]
            pos += 1
        if residual is not None:
            res_ref = ins[pos]
            pos += 1
        o_ref = outs[0]
        acc_ref = scratch[0] if nk > 1 else None
        first()
        prod = lax.dot_general(a_ref[...].astype(MXU_DTYPE), b_ref[...].astype(MXU_DTYPE), dims,
                               preferred_element_type=F32)

        def finish(acc):
            if bias_ref is not None:
                acc = acc + bias_ref[...]
            if res_ref is not None:
                acc = acc + res_ref[...]
            o_ref[...] = acc.astype(o_ref.dtype)
            if norm_g is not None:
                outs[1][...] = _rms(acc, ins[pos][...]).astype(outs[1].dtype)

        if nk == 1:
            finish(prod)
        else:
            k = pl.program_id(2)

            @pl.when(k == 0)
            def _():
                acc_ref[...] = prod

            @pl.when(k > 0)
            def _():
                acc_ref[...] += prod

            @pl.when(k == nk - 1)
            def _():
                finish(acc_ref[...])

        last()

    operands, in_specs = [a, b], [a_spec, b_spec]
    if bias is not None:
        operands.append(bias)
        in_specs.append(bias_spec)
    if residual is not None:
        operands.append(residual)
        in_specs.append(residual_spec)
    out_specs, out_shapes = [o_spec], [out_shape]
    if norm_g is not None:
        assert acc_shape[-1] == norm_g.shape[-1], "the fused norm needs whole rows in an output tile"
        operands.append(norm_g)
        in_specs.append(bias_spec)
        out_specs.append(o_spec)
        out_shapes.append(jax.ShapeDtypeStruct(out_shape.shape, MXU_DTYPE))
    aliases = {}
    if into is not None:
        aliases = {len(operands): 0}
        operands.append(into)
        in_specs.append(ANY)
    res = _call(body, hosted, name=name, grid=grid, in_specs=in_specs, out_specs=out_specs, out_shape=out_shapes,
                operands=operands, semantics=("parallel", "parallel", "arbitrary"),
                scratch_shapes=[pltpu.VMEM(acc_shape, F32)] if nk > 1 else [], aliases=aliases)
    return res if hosted or norm_g is not None else res[0]


def _tile(n, want):
    t = min(n, want)
    assert n % t == 0, (n, want)
    return t


def _mm_nn(a, b, name, *, b_lead=None, bias=None, residual=None, out_dtype=F32, tm=1024, tn=1024, tk=1024,
           hosted=None, norm_g=None):
    (m, k), n = a.shape, b.shape[-1]
    tm, tn, tk = _tile(m, tm), _tile(n, tn), _tile(k, tk)
    if b_lead is None:
        b_spec = pl.BlockSpec((tk, tn), lambda i, j, kk: (kk, j))
    else:
        b_spec = pl.BlockSpec((None, tk, tn), lambda i, j, kk: (b_lead, kk, j))
    return _matmul(
        a, b, mode="nn", name=name, grid=(m // tm, n // tn, k // tk),
        a_spec=pl.BlockSpec((tm, tk), lambda i, j, kk: (i, kk)), b_spec=b_spec,
        o_spec=pl.BlockSpec((tm, tn), lambda i, j, kk: (i, j)), out_shape=jax.ShapeDtypeStruct((m, n), out_dtype),
        bias=bias, bias_spec=pl.BlockSpec((1, tn), lambda i, j, kk: (0, j)),
        residual=residual, residual_spec=pl.BlockSpec((tm, tn), lambda i, j, kk: (i, j)), hosted=hosted,
        norm_g=norm_g)


def _mm_nt(a, b, name, *, b_lead=None, out_dtype=F32, tm=1024, tn=1024, tk=1024, hosted=None):
    (m, k), n = a.shape, b.shape[-2]
    tm, tn, tk = _tile(m, tm), _tile(n, tn), _tile(k, tk)
    if b_lead is None:
        b_spec = pl.BlockSpec((tn, tk), lambda i, j, kk: (j, kk))
    else:
        b_spec = pl.BlockSpec((None, tn, tk), lambda i, j, kk: (b_lead, j, kk))
    return _matmul(
        a, b, mode="nt", name=name, grid=(m // tm, n // tn, k // tk),
        a_spec=pl.BlockSpec((tm, tk), lambda i, j, kk: (i, kk)), b_spec=b_spec,
        o_spec=pl.BlockSpec((tm, tn), lambda i, j, kk: (i, j)), out_shape=jax.ShapeDtypeStruct((m, n), out_dtype),
        hosted=hosted)


def _mm_tn(a, b, name, *, lead=None, n_lead=None, into=None, tm=1024, tn=1024, tk=1024, out_dtype=F32):
    (k, m), n = a.shape, b.shape[-1]
    tm, tn, tk = _tile(m, tm), _tile(n, tn), _tile(k, tk)
    if lead is None:
        o_spec = pl.BlockSpec((tm, tn), lambda i, j, kk: (i, j))
        out_shape = jax.ShapeDtypeStruct((m, n), out_dtype)
    else:
        o_spec = pl.BlockSpec((None, tm, tn), lambda i, j, kk: (lead, i, j))
        out_shape = jax.ShapeDtypeStruct((n_lead, m, n), out_dtype)
    return _matmul(
        a, b, mode="tn", name=name, grid=(m // tm, n // tn, k // tk),
        a_spec=pl.BlockSpec((tk, tm), lambda i, j, kk: (kk, i)),
        b_spec=pl.BlockSpec((tk, tn), lambda i, j, kk: (kk, j)), o_spec=o_spec, out_shape=out_shape, into=into)


ROW_TILE = 512


def _pool_windows_causal(ext, first_row):
    tt = ext.shape[0] - POOL_HALO
    t = first_row + lax.broadcasted_iota(jnp.int32, (tt, 1), 0)
    outs = []
    for gi, win in enumerate(POOL_WINDOWS):
        cols = slice(gi * POOL_GC, (gi + 1) * POOL_GC)
        s = ext[:, cols]
        sh = 1
        while sh < win:
            s = s + _shift_down(s, sh)
            sh *= 2
        count = jnp.minimum(t + 1, win).astype(F32)
        outs.append(s[POOL_HALO:] / count - ext[POOL_HALO:, cols])
    return outs


def _prev_halo_spec(tt, halo, width):
    per = tt // halo
    return pl.BlockSpec((halo, width), lambda i: (jnp.maximum(i * per - 1, 0), 0))


def _pool_fwd_call(h0, g_mix, pool_w, pool_b, pool_scale, g_ffn, hosted=None):
    t = h0.shape[0]
    tt = _tile(t, ROW_TILE)

    def body(ins, outs, scratch, first, last):
        h_ref, halo_ref, gm_ref, w_ref, b_ref, sc_ref, gf_ref = ins
        h1_ref, hn_ref = outs
        i = pl.program_id(0)
        first()
        x, gm = h_ref[...], gm_ref[...]
        top = jnp.where(i > 0, _rms(halo_ref[...], gm), 0.0)
        ext = jnp.concatenate([top, _rms(x, gm)], axis=0)
        ps = _pool_windows_causal(ext, i * tt)
        ys = [jnp.dot(p.astype(MXU_DTYPE), w_ref[gi], preferred_element_type=F32) for gi, p in enumerate(ps)]
        mix = (jnp.concatenate(ys, axis=1) + b_ref[...]) * sc_ref[...]
        h1 = x + mix
        h1_ref[...] = h1
        hn_ref[...] = _rms(h1, gf_ref[...]).astype(hn_ref.dtype)
        last()

    row = pl.BlockSpec((tt, D_MODEL), lambda i: (i, 0))
    vec = pl.BlockSpec((1, D_MODEL), lambda i: (0, 0))
    wsp = pl.BlockSpec((len(POOL_WINDOWS), POOL_GC, POOL_GC), lambda i: (0, 0, 0))
    return _call(
        body, hosted, name="pool_fwd", grid=(t // tt,),
        in_specs=[row, _prev_halo_spec(tt, POOL_HALO, D_MODEL), vec, wsp, vec, vec, vec], out_specs=[row, row],
        out_shape=[jax.ShapeDtypeStruct(h0.shape, F32), jax.ShapeDtypeStruct(h0.shape, MXU_DTYPE)],
        operands=(h0, h0, g_mix, pool_w, pool_b, pool_scale, g_ffn), semantics=("parallel",))


def _pool_bwd_call(h0, dh1, g_mix, pool_w, pool_b, pool_scale, hosted=None):
    t = h0.shape[0]
    tt = _tile(t, ROW_TILE)
    nt = t // tt
    per = tt // POOL_HALO

    def body(ins, outs, scratch, first, last):
        h_ref, halo_ref, d_ref, dnext_ref, gm_ref, w_ref, b_ref, sc_ref = ins
        gx_ref, dw_ref, dv_ref = outs
        i = pl.program_id(0)
        first()
        x, gm, sc = h_ref[...], gm_ref[...], sc_ref[...]
        top = jnp.where(i > 0, _rms(halo_ref[...], gm), 0.0)
        ext = jnp.concatenate([top, _rms(x, gm)], axis=0)
        ps = _pool_windows_causal(ext, i * tt)

        dy = d_ref[...]
        dy_ext = jnp.concatenate([dy, jnp.where(i < nt - 1, dnext_ref[...], 0.0)], axis=0)
        dpre_ext = dy_ext * sc
        t_ext = i * tt + lax.broadcasted_iota(jnp.int32, (tt + POOL_HALO, 1), 0)

        @pl.when(i == 0)
        def _():
            dw_ref[...] = jnp.zeros_like(dw_ref)
            dv_ref[...] = jnp.zeros_like(dv_ref)

        dhn, ypre = [], []
        for gi, win in enumerate(POOL_WINDOWS):
            cols = slice(gi * POOL_GC, (gi + 1) * POOL_GC)
            w = w_ref[gi]
            p16 = ps[gi].astype(MXU_DTYPE)
            dpre16 = dpre_ext[:, cols].astype(MXU_DTYPE)
            ypre.append(jnp.dot(p16, w, preferred_element_type=F32))
            dw_ref[gi] += lax.dot_general(p16, dpre16[:tt], (((0,), (0,)), ((), ())), preferred_element_type=F32)
            dp_ext = lax.dot_general(dpre16, w, (((1,), (1,)), ((), ())), preferred_element_type=F32)
            s = dp_ext / jnp.minimum(t_ext + 1, win).astype(F32)
            sh = 1
            while sh < win:
                s = s + _shift_up(s, sh)
                sh *= 2
            dhn.append(s[:tt] - dp_ext[:tt])
        dhn = jnp.concatenate(dhn, axis=1)
        ypre = jnp.concatenate(ypre, axis=1) + b_ref[...]
        dx, dgt = _rms_bwd(x, gm, dhn)
        gx_ref[...] = dy + dx
        dv_ref[0:1, :] += jnp.sum(dpre_ext[:tt], axis=0, keepdims=True)
        dv_ref[1:2, :] += jnp.sum(dy * ypre, axis=0, keepdims=True)
        dv_ref[2:3, :] += jnp.sum(dgt, axis=0, keepdims=True)
        last()

    row = pl.BlockSpec((tt, D_MODEL), lambda i: (i, 0))
    vec = pl.BlockSpec((1, D_MODEL), lambda i: (0, 0))
    nxt = pl.BlockSpec((POOL_HALO, D_MODEL), lambda i: (jnp.minimum((i + 1) * per, t // POOL_HALO - 1), 0))
    wsp = pl.BlockSpec((len(POOL_WINDOWS), POOL_GC, POOL_GC), lambda i: (0, 0, 0))
    return _call(
        body, hosted, name="pool_bwd", grid=(nt,),
        in_specs=[row, _prev_halo_spec(tt, POOL_HALO, D_MODEL), row, nxt, vec, wsp, vec, vec],
        out_specs=[row, wsp, pl.BlockSpec((8, D_MODEL), lambda i: (0, 0))],
        out_shape=[jax.ShapeDtypeStruct(h0.shape, F32),
                   jax.ShapeDtypeStruct((len(POOL_WINDOWS), POOL_GC, POOL_GC), F32),
                   jax.ShapeDtypeStruct((8, D_MODEL), F32)],
        operands=(h0, h0, dh1, dh1, g_mix, pool_w, pool_b, pool_scale), semantics=("arbitrary",))


CONV_ROWS = 512
CONV_SUB = 64
LANES = 128


def _causal_conv(u_ext, w, b):
    return w[0:1] * _shift_down(u_ext, 2) + w[1:2] * _shift_down(u_ext, 1) + w[2:3] * u_ext + b


def _piece_rows(load, s, sub, tt, top, bottom=None):
    lo, hi = s * sub - CONV_HALO, (s + 1) * sub + (CONV_HALO if bottom else 0)
    parts = [top()] if lo < 0 else []
    lo = max(lo, 0)
    parts.append(load(lo, min(hi, tt) - lo))
    if hi > tt:
        parts.append(bottom())
    return parts[0] if len(parts) == 1 else jnp.concatenate(parts, axis=0)


def _fold8(x):
    acc = x[0:8]
    for r in range(8, x.shape[0], 8):
        acc = acc + x[r:r + 8]
    return acc


def _conv_glu_fwd_call(u, conv_w, conv_b, name, hosted=None):
    _, t, f = u.shape
    tt = _tile(t, CONV_ROWS)
    sub = _tile(tt, CONV_SUB)
    per = tt // CONV_HALO

    def body(ins, outs, scratch, first, last):
        u_ref, up_ref, w_ref, b_ref = ins
        z_ref = outs[0]
        i = pl.program_id(0)
        first()

        def chunk(j, carry):
            lanes = pl.ds(pl.multiple_of(j * LANES, LANES), LANES)
            w, b = w_ref[:, :, lanes], b_ref[:, lanes]
            for s in range(tt // sub):
                c = []
                for hf in range(2):
                    ext = _piece_rows(lambda at, n: u_ref[hf, pl.ds(at, n), lanes].astype(F32), s, sub, tt,
                                      top=lambda: jnp.where(i > 0, up_ref[hf, :, lanes].astype(F32), 0.0))
                    c.append(_causal_conv(ext, w[:, hf, :], b[hf:hf + 1, :])[CONV_HALO:])
                z_ref[pl.ds(s * sub, sub), lanes] = (c[0] * c[1] * _sigmoid(c[1])).astype(z_ref.dtype)
            return carry

        lax.fori_loop(0, f // LANES, chunk, 0)
        last()

    res = _call(
        body, hosted, name=name, grid=(t // tt,),
        in_specs=[pl.BlockSpec((2, tt, f), lambda i: (0, i, 0)),
                  pl.BlockSpec((2, CONV_HALO, f), lambda i: (0, jnp.maximum(i * per - 1, 0), 0)),
                  pl.BlockSpec((3, 2, f), lambda i: (0, 0, 0)), pl.BlockSpec((2, f), lambda i: (0, 0))],
        out_specs=[pl.BlockSpec((tt, f), lambda i: (i, 0))], out_shape=[jax.ShapeDtypeStruct((t, f), MXU_DTYPE)],
        operands=(u, u, conv_w, conv_b), semantics=("parallel",))
    return res if hosted else res[0]


def _conv_glu_bwd_call(u, dz, conv_w, conv_b, name, hosted=None):
    _, t, f = u.shape
    tt = _tile(t, CONV_ROWS)
    sub = _tile(tt, 2 * CONV_SUB)
    nt = t // tt
    per = tt // CONV_HALO
    halo = CONV_HALO

    def body(ins, outs, scratch, first, last):
        u_ref, up_ref, un_ref, dz_ref, dzn_ref, w_ref, b_ref = ins
        du_ref, dw_ref, db_ref = outs
        dwacc, dbacc = scratch
        i = pl.program_id(0)
        first()

        @pl.when(i == 0)
        def _():
            dwacc[...] = jnp.zeros_like(dwacc)
            dbacc[...] = jnp.zeros_like(dbacc)

        main = slice(halo, halo + sub)

        def chunk(j, carry):
            lanes = pl.ds(pl.multiple_of(j * LANES, LANES), LANES)
            w, b = w_ref[:, :, lanes], b_ref[:, lanes]
            for s in range(tt // sub):
                ue = [_piece_rows(lambda at, n: u_ref[hf, pl.ds(at, n), lanes].astype(F32), s, sub, tt,
                                  top=lambda: jnp.where(i > 0, up_ref[hf, :, lanes].astype(F32), 0.0),
                                  bottom=lambda: un_ref[hf, :, lanes].astype(F32)) for hf in range(2)]
                a, g = [_causal_conv(ue[hf], w[:, hf, :], b[hf:hf + 1, :]) for hf in range(2)]
                sg = _sigmoid(g)
                if (s + 1) * sub < tt:
                    below = dz_ref[pl.ds(s * sub, sub + 2 * halo), lanes].astype(F32)[:sub + halo]
                else:
                    below = jnp.concatenate([dz_ref[pl.ds(s * sub, sub), lanes].astype(F32),
                                             jnp.where(i < nt - 1, dzn_ref[:, lanes].astype(F32), 0.0)], axis=0)
                dzs = jnp.concatenate([jnp.zeros((halo, LANES), F32), below], axis=0)
                gs = g * sg
                dc = [dzs * gs, dzs * a * (sg + gs - gs * sg)]
                for hf in range(2):
                    d = dc[hf]
                    d1, d2 = _shift_up(d, 1), _shift_up(d, 2)
                    du = w[2:3, hf, :] * d + w[1:2, hf, :] * d1 + w[0:1, hf, :] * d2
                    du_ref[hf, pl.ds(s * sub, sub), lanes] = du[main].astype(du_ref.dtype)
                    um = ue[hf][main]
                    for k, dd in enumerate((d2, d1, d)):
                        dwacc[k, hf, :, lanes] += _fold8(dd[main] * um)
                    dbacc[hf, :, lanes] += _fold8(d[main])
            return carry

        lax.fori_loop(0, f // LANES, chunk, 0)

        @pl.when(i == nt - 1)
        def _():
            dw_ref[...] = jnp.sum(dwacc[...], axis=2)
            db_ref[...] = jnp.sum(dbacc[...], axis=1)

        last()

    end = t // halo - 1
    return _call(
        body, hosted, name=name, grid=(nt,),
        in_specs=[pl.BlockSpec((2, tt, f), lambda i: (0, i, 0)),
                  pl.BlockSpec((2, halo, f), lambda i: (0, jnp.maximum(i * per - 1, 0), 0)),
                  pl.BlockSpec((2, halo, f), lambda i: (0, jnp.minimum((i + 1) * per, end), 0)),
                  pl.BlockSpec((tt, f), lambda i: (i, 0)),
                  pl.BlockSpec((halo, f), lambda i: (jnp.minimum((i + 1) * per, end), 0)),
                  pl.BlockSpec((3, 2, f), lambda i: (0, 0, 0)), pl.BlockSpec((2, f), lambda i: (0, 0))],
        out_specs=[pl.BlockSpec((2, tt, f), lambda i: (0, i, 0)),
                   pl.BlockSpec((3, 2, f), lambda i: (0, 0, 0)), pl.BlockSpec((2, f), lambda i: (0, 0))],
        out_shape=[jax.ShapeDtypeStruct(u.shape, MXU_DTYPE), jax.ShapeDtypeStruct((3, 2, f), F32),
                   jax.ShapeDtypeStruct((2, f), F32)],
        operands=(u, u, u, dz, dz, conv_w, conv_b), semantics=("arbitrary",),
        scratch_shapes=[pltpu.VMEM((3, 2, 8, f), F32), pltpu.VMEM((2, 8, f), F32)])


def _kv_even_odd(band, k):
    pair, parity = divmod(k, 2)
    blk = band[:, 128 * pair:128 * (pair + 1)].astype(F32)
    lane = lax.broadcasted_iota(jnp.int32, blk.shape, 1)
    own = jnp.where((lane >= HEAD_DIM) == (parity == 1), blk, 0.0)
    swapped = pltpu.roll(own, HEAD_DIM, axis=1)
    even, odd = (own, swapped) if parity == 0 else (swapped, own)
    return even.astype(MXU_DTYPE), odd.astype(MXU_DTYPE)


def _stack_pairs(ref, k):
    return jnp.concatenate([ref[:, 256 * k:256 * k + 128], ref[:, 256 * k + 128:256 * k + 256]], axis=0)


_GROUP_ORDER = (0, 2, 1, 3)


def _attn_fill_tables(bias_ref):
    shape = (4 * ATT_BLOCK, 2 * ATT_BLOCK)
    row = lax.broadcasted_iota(jnp.int32, shape, 0)
    col = lax.broadcasted_iota(jnp.int32, shape, 1)
    dist = (row & (ATT_BLOCK - 1)) + ATT_BLOCK - col
    ok = (dist >= 0) & (dist < ATT_BLOCK)
    distf = dist.astype(F32)
    rb = lax.broadcasted_iota(jnp.int32, (shape[0], 1), 0) // ATT_BLOCK
    for k in range(N_KV_HEADS):
        slope = jnp.zeros((shape[0], 1), F32)
        for r, gq in enumerate(_GROUP_ORDER):
            slope = jnp.where(rb == r, ALIBI_SLOPES[4 * k + gq], slope)
        bias = jnp.where(ok, -slope * distf, -jnp.inf)
        bias_ref[0, k] = bias
        bias_ref[1, k] = jnp.where(col >= ATT_BLOCK, bias, -jnp.inf)


def _attn_probs(qs, k_even, k_odd, bias, k, sink_ref):
    nt_dims = (((1,), (1,)), ((), ()))
    s = jnp.concatenate([lax.dot_general(qs, k_even, nt_dims, preferred_element_type=F32),
                         lax.dot_general(qs, k_odd, nt_dims, preferred_element_type=F32)], axis=0) + bias
    rb = lax.broadcasted_iota(jnp.int32, (s.shape[0], 1), 0) // ATT_BLOCK
    sink = jnp.zeros((s.shape[0], 1), F32)
    for r, gq in enumerate(_GROUP_ORDER):
        sink = jnp.where(rb == r, sink_ref[4 * k + gq], sink)
    m = jnp.maximum(jnp.max(s, axis=-1, keepdims=True), sink)
    p = jnp.exp(s - m)
    es = jnp.exp(sink - m)
    return p, 1.0 / (jnp.sum(p, axis=-1, keepdims=True) + es), es


_ATTN_TABLES = [pltpu.VMEM((2, N_KV_HEADS, 4 * ATT_BLOCK, 2 * ATT_BLOCK), F32)]


def _attn_specs(with_do):
    prev = lambda n: jnp.maximum(n - 1, 0)
    specs = [pl.BlockSpec(memory_space=pltpu.SMEM),
             pl.BlockSpec((ATT_BLOCK, D_MODEL), lambda n: (n, 0)),
             pl.BlockSpec((ATT_BLOCK, 256), lambda n: (prev(n), 4)), pl.BlockSpec((ATT_BLOCK, 256), lambda n: (n, 4)),
             pl.BlockSpec((ATT_BLOCK, 256), lambda n: (prev(n), 5)), pl.BlockSpec((ATT_BLOCK, 256), lambda n: (n, 5))]
    if with_do:
        specs.append(pl.BlockSpec((ATT_BLOCK, D_MODEL), lambda n: (n, 0)))
    return specs


def _attn_fwd_call(qkv, sinks, hosted=None):
    t = qkv.shape[0]

    def body(ins, outs, scratch, first, last):
        sink_ref, q_ref, kp_ref, kc_ref, vp_ref, vc_ref = ins
        o_ref, bias_ref = outs[0], scratch[0]
        n = pl.program_id(0)
        first()

        @pl.when(n == 0)
        def _():
            _attn_fill_tables(bias_ref)

        first = 1 - jnp.minimum(n, 1)
        kband = jnp.concatenate([kp_ref[...], kc_ref[...]], axis=0)
        vband = jnp.concatenate([vp_ref[...], vc_ref[...]], axis=0)
        for k in range(N_KV_HEADS):
            k_even, k_odd = _kv_even_odd(kband, k)
            v_even, v_odd = _kv_even_odd(vband, k)
            qs = _stack_pairs(q_ref, k) * ATT_SCALE
            p, rdenom, _ = _attn_probs(qs, k_even, k_odd, bias_ref[first, k], k, sink_ref)
            probs = (p * rdenom).astype(MXU_DTYPE)
            o = (jnp.dot(probs[:256], v_even, preferred_element_type=F32)
                 + jnp.dot(probs[256:], v_odd, preferred_element_type=F32))
            o_ref[:, 256 * k:256 * k + 128] = o[:128].astype(o_ref.dtype)
            o_ref[:, 256 * k + 128:256 * k + 256] = o[128:].astype(o_ref.dtype)
        last()

    return _call(
        body, hosted, name="attn_fwd", grid=(t // ATT_BLOCK,), in_specs=_attn_specs(False),
        out_specs=[pl.BlockSpec((ATT_BLOCK, D_MODEL), lambda n: (n, 0))],
        out_shape=[jax.ShapeDtypeStruct((t, D_MODEL), MXU_DTYPE)], operands=(sinks, qkv, qkv, qkv, qkv, qkv),
        semantics=("arbitrary",), scratch_shapes=_ATTN_TABLES)


def _attn_bwd_call(qkv, sinks, do, hosted=None):
    t = qkv.shape[0]
    nb = t // ATT_BLOCK
    tn_dims = (((0,), (0,)), ((), ()))
    nt_dims = (((1,), (1,)), ((), ()))

    def to_native(even_part, odd_part, parity):
        lane = lax.broadcasted_iota(jnp.int32, even_part.shape, 1)
        lo = lane < HEAD_DIM
        e, o = jnp.where(lo, even_part, 0.0), jnp.where(lo, 0.0, odd_part)
        if parity == 0:
            return e + pltpu.roll(o, HEAD_DIM, axis=1)
        return pltpu.roll(e, HEAD_DIM, axis=1) + o

    def body(ins, outs, scratch, first, last):
        sink_ref, q_ref, kp_ref, kc_ref, vp_ref, vc_ref, do_ref = ins
        dq_ref, dk_ref, dv_ref, db_ref, dsink_ref = outs
        bias_ref = scratch[0]
        n = pl.program_id(0)
        first()

        @pl.when(n == 0)
        def _():
            _attn_fill_tables(bias_ref)
            dk_ref[...] = jnp.zeros_like(dk_ref)
            dv_ref[...] = jnp.zeros_like(dv_ref)
            db_ref[...] = jnp.zeros_like(db_ref)
            dsink_ref[...] = jnp.zeros_like(dsink_ref)

        kband = jnp.concatenate([kp_ref[...], kc_ref[...]], axis=0)
        vband = jnp.concatenate([vp_ref[...], vc_ref[...]], axis=0)
        dk_pairs = [jnp.zeros((2 * ATT_BLOCK, 128), F32), jnp.zeros((2 * ATT_BLOCK, 128), F32)]
        dv_pairs = [jnp.zeros((2 * ATT_BLOCK, 128), F32), jnp.zeros((2 * ATT_BLOCK, 128), F32)]
        sink_lane = lax.broadcasted_iota(jnp.int32, (8, 128), 1)
        sink_row = lax.broadcasted_iota(jnp.int32, (8, 128), 0)
        dsink = jnp.zeros((8, 128), F32)
        first = 1 - jnp.minimum(n, 1)
        for k in range(N_KV_HEADS):
            k_even, k_odd = _kv_even_odd(kband, k)
            v_even, v_odd = _kv_even_odd(vband, k)
            qs = _stack_pairs(q_ref, k) * ATT_SCALE
            dos = _stack_pairs(do_ref, k)
            p, rdenom, es = _attn_probs(qs, k_even, k_odd, bias_ref[first, k], k, sink_ref)
            probs = p * rdenom
            dp = jnp.concatenate([lax.dot_general(dos, v_even, nt_dims, preferred_element_type=F32),
                                  lax.dot_general(dos, v_odd, nt_dims, preferred_element_type=F32)], axis=0)
            delta = jnp.sum(probs * dp, axis=-1, keepdims=True)
            ds16 = (probs * (dp - delta)).astype(MXU_DTYPE)
            p16 = probs.astype(MXU_DTYPE)
            dsink_rows = -(es * rdenom) * delta
            for r, gq in enumerate(_GROUP_ORDER):
                tot = jnp.sum(dsink_rows[ATT_BLOCK * r:ATT_BLOCK * (r + 1)], axis=0, keepdims=True)
                dsink = dsink + jnp.where((sink_lane == 4 * k + gq) & (sink_row == 0), tot, 0.0)
            dqs = (jnp.dot(ds16[:256], k_even, preferred_element_type=F32)
                   + jnp.dot(ds16[256:], k_odd, preferred_element_type=F32)) * ATT_SCALE
            dq_ref[:, 256 * k:256 * k + 128] = dqs[:128].astype(dq_ref.dtype)
            dq_ref[:, 256 * k + 128:256 * k + 256] = dqs[128:].astype(dq_ref.dtype)
            db_ref[:, 256 * k:256 * k + 128] += jnp.sum(dqs[:128], axis=0, keepdims=True)
            db_ref[:, 256 * k + 128:256 * k + 256] += jnp.sum(dqs[128:], axis=0, keepdims=True)
            pair, parity = divmod(k, 2)
            dk_pairs[pair] = dk_pairs[pair] + to_native(
                lax.dot_general(ds16[:256], qs, tn_dims, preferred_element_type=F32),
                lax.dot_general(ds16[256:], qs, tn_dims, preferred_element_type=F32), parity)
            dv_pairs[pair] = dv_pairs[pair] + to_native(
                lax.dot_general(p16[:256], dos, tn_dims, preferred_element_type=F32),
                lax.dot_general(p16[256:], dos, tn_dims, preferred_element_type=F32), parity)
        dsink_ref[...] += dsink
        dk_band = jnp.concatenate(dk_pairs, axis=1)
        dv_band = jnp.concatenate(dv_pairs, axis=1)
        cur = pl.multiple_of(n * ATT_BLOCK, ATT_BLOCK)
        dk_ref[pl.ds(cur, ATT_BLOCK), :] += dk_band[ATT_BLOCK:]
        dv_ref[pl.ds(cur, ATT_BLOCK), :] += dv_band[ATT_BLOCK:]

        @pl.when(n > 0)
        def _():
            prv = pl.multiple_of((n - 1) * ATT_BLOCK, ATT_BLOCK)
            dk_ref[pl.ds(prv, ATT_BLOCK), :] += dk_band[:ATT_BLOCK]
            dv_ref[pl.ds(prv, ATT_BLOCK), :] += dv_band[:ATT_BLOCK]

        @pl.when(n == nb - 1)
        def _():
            db_ref[:, 1024:1280] = jnp.sum(dk_ref[...], axis=0, keepdims=True)
            db_ref[:, 1280:1536] = jnp.sum(dv_ref[...], axis=0, keepdims=True)

        last()

    whole = lambda shape: pl.BlockSpec(shape, lambda n: (0, 0))
    return _call(
        body, hosted, name="attn_bwd", grid=(nb,), in_specs=_attn_specs(True),
        out_specs=[pl.BlockSpec((ATT_BLOCK, D_MODEL), lambda n: (n, 0)), whole((t, 256)), whole((t, 256)),
                   whole((1, QKV_DIM)), whole((8, 128))],
        out_shape=[jax.ShapeDtypeStruct((t, D_MODEL), MXU_DTYPE), jax.ShapeDtypeStruct((t, 256), F32),
                   jax.ShapeDtypeStruct((t, 256), F32), jax.ShapeDtypeStruct((1, QKV_DIM), F32),
                   jax.ShapeDtypeStruct((8, 128), F32)],
        operands=(sinks, qkv, qkv, qkv, qkv, qkv, do), semantics=("arbitrary",), scratch_shapes=_ATTN_TABLES)


FF_TILE = 1408


def _gathering_tile(j):
    me = 2 * lax.axis_index("x") + lax.axis_index("y")
    return me ^ (((j & 1) << 1) | (j >> 1))


def _ffn_in_gathering_call(hn, w_in, layer, hosted):
    t = hn.shape[0]
    tm = _tile(t, 1024)
    n_i = t // tm
    halves = D_FF // FF_TILE
    assert FF_TILE == W_IN_COLS and 2 * halves == N_CHIPS

    def body(ins, outs, scratch, first, last):
        hn_ref = ins[0]
        u_ref, w_ref = outs
        wbuf, load_sem, ici_send, ici_recv, hand_send, hand_recv = scratch
        j, i = pl.program_id(0), pl.program_id(1)
        x, y, c, chips = _mesh_place()
        me = 2 * x + y
        mover = c == layer

        def shard(k):
            return w_ref.at[layer, :, _col(k, W_IN_COLS)]

        def over_ici(jj, k):
            cx, cy = chips[jj]
            return pltpu.make_async_remote_copy(src_ref=shard(k), dst_ref=shard(k), send_sem=ici_send.at[jj],
                                                recv_sem=ici_recv.at[jj], device_id=(cx, cy, c), device_id_type=MESH)

        def handed(jj):
            cx, cy = chips[jj]
            return pltpu.make_async_remote_copy(
                src_ref=shard(2 * cx + cy), dst_ref=shard(2 * cx + cy), send_sem=hand_send.at[jj],
                recv_sem=hand_recv.at[jj], device_id=(x, y, 1 - c), device_id_type=MESH)

        far = N_CHIPS - 2

        @pl.when((j == 0) & (i == 0) & mover)
        def _():
            over_ici(far, me).start()
            for jj in NEIGHBOURS:
                handed(jj).start()

        first()

        @pl.when((j == far + 1) & (i == 0) & mover)
        def _():
            over_ici(far, 2 * chips[far][0] + chips[far][1]).wait_recv()
            handed(far).start()

        for jj in range(N_CHIPS - 1):
            @pl.when((j == jj + 1) & (i == 0) & jnp.logical_not(mover))
            def _(jj=jj):
                handed(jj).wait_recv()

        @pl.when(i == 0)
        def _():
            load = pltpu.make_async_copy(shard(_gathering_tile(j)), wbuf, load_sem.at[0])
            load.start()
            load.wait()

        u_ref[...] = jnp.dot(hn_ref[...].astype(MXU_DTYPE), wbuf[...].astype(MXU_DTYPE),
                             preferred_element_type=F32).astype(u_ref.dtype)

        @pl.when((j == N_CHIPS - 1) & (i == n_i - 1) & mover)
        def _():
            over_ici(far, me).wait_send()
            for jj in range(N_CHIPS - 1):
                handed(jj).wait_send()

        last()

    def u_index(j, i):
        tile = _gathering_tile(j)
        return tile // halves, i, tile % halves

    three = [pltpu.SemaphoreType.DMA((N_CHIPS - 1,))] * 4
    return _call(
        body, hosted, name=f"ffn{layer}_in", grid=(N_CHIPS, n_i),
        in_specs=[pl.BlockSpec((tm, D_MODEL), lambda j, i: (i, 0)), ANY],
        out_specs=[pl.BlockSpec((None, tm, FF_TILE), u_index), ANY],
        out_shape=[jax.ShapeDtypeStruct((2, t, D_FF), U_DTYPE), jax.ShapeDtypeStruct(w_in.shape, w_in.dtype)],
        operands=(hn, w_in), semantics=("arbitrary", "arbitrary"), aliases={1: 1},
        scratch_shapes=[pltpu.VMEM((D_MODEL, W_IN_COLS), w_in.dtype), pltpu.SemaphoreType.DMA((1,))] + three)


def _ffn_out_loss_call(z, w_out, layer, h, g, target):
    t, f = z.shape
    tm = _tile(t, ROW_TILE)

    def body(z_ref, w_ref, h_ref, g_ref, t_ref, loss_ref, dh_ref, dh16_ref, dg_ref):
        x = h_ref[...] + jnp.dot(z_ref[...], w_ref[...], preferred_element_type=F32)
        gv = g_ref[...]
        err = _rms(x, gv) - t_ref[...]
        dx, dgt = _rms_bwd(x, gv, err * (1.0 / D_MODEL))
        dh_ref[...] = dx
        dh16_ref[...] = dx.astype(dh16_ref.dtype)

        @pl.when(pl.program_id(0) == 0)
        def _():
            dg_ref[...] = jnp.zeros_like(dg_ref)
            loss_ref[...] = jnp.zeros_like(loss_ref)

        dg_ref[...] += jnp.sum(dgt, axis=0, keepdims=True)
        per_token = jnp.mean(err * err, axis=-1, keepdims=True)
        loss_ref[...] += 0.5 * jnp.sum(per_token, axis=0, keepdims=True)

    row = pl.BlockSpec((tm, D_MODEL), lambda i: (i, 0))
    vec = pl.BlockSpec((1, D_MODEL), lambda i: (0, 0))
    return pl.pallas_call(
        body, name=f"ffn{layer}_out_loss", grid=(t // tm,),
        in_specs=[pl.BlockSpec((tm, f), lambda i: (i, 0)), pl.BlockSpec((None, f, D_MODEL), lambda i: (layer, 0, 0)),
                  row, vec, row],
        out_specs=[pl.BlockSpec((1, 1), lambda i: (0, 0)), row, row, vec],
        out_shape=[jax.ShapeDtypeStruct((1, 1), F32), jax.ShapeDtypeStruct((t, D_MODEL), F32),
                   jax.ShapeDtypeStruct((t, D_MODEL), MXU_DTYPE), jax.ShapeDtypeStruct((1, D_MODEL), F32)],
        compiler_params=_params("arbitrary"))(z, w_out, h, g, target)


def _ffn_fwd(hn, h, w, conv_w, conv_b, layer, hooks=None, next_norm=None, loss_head=None, gathers_w_in=False):
    t = hn.shape[0]
    tm = _tile(t, 1024)
    halves = D_FF // FF_TILE
    if gathers_w_in:
        u, w["w_in"] = _ride(hooks, f"in{layer}", lambda hosted: _ffn_in_gathering_call(hn, w["w_in"], layer, hosted))
    else:
        u, = _ride(hooks, f"in{layer}", lambda hosted: _matmul(
            hn, w["w_in"], mode="nn", name=f"ffn{layer}_in", grid=(2 * halves, t // tm, 1),
            a_spec=pl.BlockSpec((tm, D_MODEL), lambda j, i, k: (i, 0)),
            b_spec=pl.BlockSpec((None, D_MODEL, FF_TILE), lambda j, i, k: (layer, 0, j)),
            o_spec=pl.BlockSpec((None, tm, FF_TILE), lambda j, i, k: (j // halves, i, j % halves)),
            out_shape=jax.ShapeDtypeStruct((2, t, D_FF), U_DTYPE), hosted=hosted))
    z, = _ride(hooks, f"glu{layer}", lambda hosted: _conv_glu_fwd_call(u, conv_w, conv_b, f"ffn{layer}_glu",
                                                                       hosted=hosted))
    if loss_head is not None:
        return (*_ffn_out_loss_call(z, w["w_out"], layer, h, *loss_head), u, z)
    outs = _ride(hooks, f"out{layer}", lambda hosted: _mm_nn(z, w["w_out"], f"ffn{layer}_out", b_lead=layer, residual=h,
                                                     tk=D_FF, hosted=hosted, norm_g=next_norm))
    return outs[0], (outs[1] if next_norm is not None else None), u, z


def _ffn_bwd(dh, hn, u, z, w_in, w_out, conv_w, conv_b, layer, g_in_acc, g_out_acc, norm, hooks=None):
    t = hn.shape[0]
    dz, = _ride(hooks, "dz", lambda hosted: _mm_nt(dh, w_out, f"ffn{layer}_dz", b_lead=layer, out_dtype=MXU_DTYPE,
                                                   tn=FF_TILE, hosted=hosted))
    g_out = _mm_tn(z, dh, f"ffn{layer}_dwout", lead=layer, n_lead=2, into=g_out_acc, tm=WGRAD_TILE, tk=t,
                   out_dtype=WIRE_DTYPE)
    du, dcw, dcb = _ride(hooks, "dglu", lambda hosted: _conv_glu_bwd_call(u, dz, conv_w, conv_b, f"ffn{layer}_dglu",
                                                                          hosted=hosted), {"g_out": g_out})
    per_half = D_FF // WGRAD_TILE
    g_in, = _ride(hooks, "dwin", lambda hosted: _matmul(
        hn, du, mode="tn", name=f"ffn{layer}_dwin", grid=(1, 2 * per_half, 1),
        a_spec=pl.BlockSpec((t, D_MODEL), lambda i, j, k: (0, 0)),
        b_spec=pl.BlockSpec((None, t, WGRAD_TILE), lambda i, j, k: (j // per_half, 0, j % per_half)),
        o_spec=pl.BlockSpec((None, D_MODEL, WGRAD_TILE), lambda i, j, k: (layer, 0, j)),
        out_shape=jax.ShapeDtypeStruct((2, D_MODEL, 2 * D_FF), WIRE_DTYPE), into=g_in_acc, hosted=hosted))
    dx, dx16, dg = _ride(hooks, "dhn", lambda hosted: _ffn_dhn_call(du, w_in, layer, norm, hosted),
                         {"g_in": g_in, "g_out": g_out})
    return dx, dx16, dg, g_in, g_out, dcw, dcb


WGRAD_TILE = 256


_NT_DIMS = (((1,), (1,)), ((), ()))


def _dhn_call(name, a, a_spec, w, w_spec, product, t, tm, norm, hosted=None):
    h, g, dres = norm

    def body(ins, outs, scratch, first, last):
        a_ref, w_ref, h_ref, g_ref, dres_ref = ins
        dx_ref, dx16_ref, dg_ref = outs
        first()
        dx, dgt = _rms_bwd(h_ref[...], g_ref[...], product(a_ref, w_ref))
        dx = dres_ref[...] + dx
        dx_ref[...] = dx
        dx16_ref[...] = dx.astype(dx16_ref.dtype)

        @pl.when(pl.program_id(0) == 0)
        def _():
            dg_ref[...] = jnp.zeros_like(dg_ref)

        dg_ref[...] += jnp.sum(dgt, axis=0, keepdims=True)
        last()

    row = pl.BlockSpec((tm, D_MODEL), lambda i: (i, 0))
    vec = pl.BlockSpec((1, D_MODEL), lambda i: (0, 0))
    return _call(
        body, hosted, name=name, grid=(t // tm,), in_specs=[a_spec, w_spec, row, vec, row], out_specs=[row, row, vec],
        out_shape=[jax.ShapeDtypeStruct((t, D_MODEL), F32), jax.ShapeDtypeStruct((t, D_MODEL), MXU_DTYPE),
                   jax.ShapeDtypeStruct((1, D_MODEL), F32)],
        operands=(a, w, h, g, dres), semantics=("arbitrary",))


def _ffn_dhn_call(du, w_in, layer, norm, hosted=None):
    _, t, f = du.shape
    tm = _tile(t, 512)

    def product(du_ref, w_ref):
        return (lax.dot_general(du_ref[0], w_ref[:, :f], _NT_DIMS, preferred_element_type=F32)
                + lax.dot_general(du_ref[1], w_ref[:, f:], _NT_DIMS, preferred_element_type=F32))

    return _dhn_call(f"ffn{layer}_dhn", du, pl.BlockSpec((2, tm, f), lambda i: (0, i, 0)), w_in,
                     pl.BlockSpec((None, D_MODEL, 2 * f), lambda i: (layer, 0, 0)), product, t, tm, norm, hosted)


def _attn_dhn_call(dqkv, w_qkv, norm):
    t, n = dqkv.shape
    tm = _tile(t, 1024)

    def product(a_ref, w_ref):
        return lax.dot_general(a_ref[...], w_ref[...], _NT_DIMS, preferred_element_type=F32)

    return _dhn_call("attn_dhn", dqkv, pl.BlockSpec((tm, n), lambda i: (i, 0)), w_qkv,
                     pl.BlockSpec((D_MODEL, n), lambda i: (0, 0)), product, t, tm, norm)


def _local_step(x, target, w, *, norm_mix, norm_ffn, norm_f, pool_b, pool_scale, b_qkv, sinks, conv_w, conv_b,
                late_gather=None, reducer=None):
    cw = [conv_w[l].reshape(3, 2, D_FF) for l in range(2)]
    cb = [conv_b[l].reshape(2, D_FF) for l in range(2)]
    gm = [norm_mix[l:l + 1] for l in range(2)]
    gf = [norm_ffn[l:l + 1] for l in range(2)]

    w = dict(w)
    fwd_hooks = {}
    if late_gather:
        def gather_hook(stage):
            return (lambda env: late_gather(stage, w)), (lambda got: w.update(zip(_late_names(stage), got)))

        fwd_hooks = {stage: gather_hook(stage) for stage in GATHER_RIDES}

    h1, hn_f0 = _ride(fwd_hooks, "pool_fwd", lambda hosted: _pool_fwd_call(x, gm[0], w["pool_w"], pool_b, pool_scale,
                                                                            gf[0], hosted))
    h2, hn_a, u0, z0 = _ffn_fwd(hn_f0, h1, w, cw[0], cb[0], 0, hooks=fwd_hooks, next_norm=gm[1],
                                gathers_w_in=bool(late_gather))
    qkv = _mm_nn(hn_a, w["w_qkv"], "attn_qkv", bias=b_qkv, out_dtype=MXU_DTYPE, tn=768)
    o, = _ride(fwd_hooks, "attn_fwd", lambda hosted: _attn_fwd_call(qkv, sinks, hosted))
    h3, hn_f1 = _ride(fwd_hooks, "attn_out", lambda hosted: _mm_nn(o, w["w_o"], "attn_out", residual=h2, hosted=hosted,
                                                                   norm_g=gf[1]))
    loss, dh, dh16, d_norm_f, u1, z1 = _ffn_fwd(hn_f1, h3, w, cw[1], cb[1], 1, hooks=fwd_hooks,
                                                loss_head=(norm_f, target))

    rd = reducer
    first, mid, last = _Reducer.FIRST, _Reducer.MID, _Reducer.LAST
    hooks1 = hooks_attn = hooks0 = hooks_pool = None
    if rd:
        hooks1 = {"dhn": (lambda env: rd.swap(first, {"w_in": env["g_in"], "w_out": env["g_out"]}),
                          lambda got: rd.pair_sums(first, got))}
        hooks_attn = {"attn_bwd": (lambda env: rd.scatter(first), lambda got: rd.chip_sums(first, got))}

        def mid_and_last(got):
            rd.shared(mid, got[:len(mid)])
            rd.pair_sums(last, got[len(mid):])

        hooks0 = {
            "dz": (lambda env: rd.share(first), lambda got: rd.shared(first, got)),
            "dglu": (lambda env: rd.swap(mid, {"w_out": env["g_out"], "w_qkv": g_qkv, "w_o": g_o4}),
                     lambda got: rd.pair_sums(mid, got)),
            "dwin": (lambda env: rd.scatter(mid), lambda got: rd.chip_sums(mid, got)),
            "dhn": (lambda env: _both(rd.share(mid), rd.swap(last, {"w_in": env["g_in"]})), mid_and_last),
        }
        hooks_pool = {"pool_bwd": (lambda env: rd.scatter(last), lambda got: rd.chip_sums(last, got))}
    dh, dh16, d_gf1, g_in, g_out, dcw1, dcb1 = _ffn_bwd(
        dh16, hn_f1, u1, z1, w["w_in"], w["w_out"], cw[1], cb[1], 1, None, None, (h3, gf[1], dh), hooks=hooks1)
    do = _mm_nt(dh16, w["w_o"], "attn_do", out_dtype=MXU_DTYPE)
    g_o = _mm_tn(o, dh16, "attn_dwo", tn=WGRAD_TILE, tk=x.shape[0], out_dtype=WIRE_DTYPE)
    g_o4 = g_o.reshape(N_CHIPS, 2, W_O_ROWS // 2, D_MODEL)
    dq, dk, dv, d_bqkv, d_sinks = _ride(hooks_attn, "attn_bwd", lambda hosted: _attn_bwd_call(qkv, sinks, do, hosted))
    dqkv = jnp.concatenate([dq, dk.astype(MXU_DTYPE), dv.astype(MXU_DTYPE)], axis=1)
    g_qkv = _mm_tn(hn_a, dqkv, "attn_dwqkv", tn=WGRAD_TILE, tk=x.shape[0], out_dtype=WIRE_DTYPE)
    dh, dh16, d_gm1 = _attn_dhn_call(dqkv, w["w_qkv"], (h2, gm[1], dh))
    dh, _, d_gf0, g_in, g_out, dcw0, dcb0 = _ffn_bwd(
        dh16, hn_f0, u0, z0, w["w_in"], w["w_out"], cw[0], cb[0], 0, g_in, g_out, (h1, gf[0], dh), hooks=hooks0)
    grad_x, g_pool, dvec = _ride(hooks_pool, "pool_bwd", lambda hosted: _pool_bwd_call(
        x, dh, gm[0], w["pool_w"], pool_b, pool_scale, hosted))
    if rd:
        rd.shared(last, _alone(rd.share(last), "share_last"))

    grads = {
        "w_in": g_in, "w_out": g_out, "w_qkv": g_qkv, "w_o": g_o, "pool_w": g_pool,
        "pool_b": dvec[0:1], "pool_scale": dvec[1:2], "b_qkv": d_bqkv, "sinks": d_sinks[0:1, :N_HEADS],
        "norm_mix": jnp.concatenate([dvec[2:3], d_gm1], axis=0), "norm_ffn": jnp.concatenate([d_gf0, d_gf1], axis=0),
        "conv_w": jnp.stack([dcw0.reshape(3, 2 * D_FF), dcw1.reshape(3, 2 * D_FF)]),
        "conv_b": jnp.stack([dcb0.reshape(2 * D_FF), dcb1.reshape(2 * D_FF)]), "norm_f": d_norm_f,
    }
    return loss, grad_x, grads


W_IN_COLS = 2 * D_FF // N_CHIPS
W_OUT_ROWS = D_FF // N_CHIPS
QKV_COLS = QKV_DIM // N_CHIPS
W_O_ROWS = D_MODEL // N_CHIPS
POOL_ROWS = POOL_GC // N_CHIPS
BIG = ("w_in", "w_out", "w_qkv", "w_o", "pool_w")


def _mesh_place():
    x, y, c = lax.axis_index("x"), lax.axis_index("y"), lax.axis_index("c")
    chips = [(1 - x, y), (x, 1 - y), (1 - x, 1 - y)]
    return x, y, c, chips


def _col(k, width):
    return pl.ds(pl.multiple_of(k * width, 128), width)


def _row(k, height):
    return pl.ds(pl.multiple_of(k * height, 16), height)


def _full_piece(name, ref, k, h):
    if name == "w_in":
        return ref.at[h, :, _col(k, W_IN_COLS)]
    if name == "w_out":
        return ref.at[h, _row(k, W_OUT_ROWS), :]
    if name == "w_qkv":
        return ref.at[_row(h, D_MODEL // 2), _col(k, QKV_COLS)]
    if name == "w_o":
        return ref.at[_row(2 * k + h, W_O_ROWS // 2), :]
    return ref.at[pl.ds(2 * h, 2), _row(k, POOL_ROWS), :]


def _shard_half(name, ref, h):
    if name in ("w_in", "w_out"):
        return ref.at[h]
    if name == "w_qkv":
        return ref.at[_row(h, D_MODEL // 2), :]
    if name == "w_o":
        return ref.at[_row(h, W_O_ROWS // 2), :]
    return ref.at[pl.ds(2 * h, 2)]


FULL_SHAPES = {"w_in": (2, D_MODEL, 2 * D_FF), "w_out": (2, D_FF, D_MODEL), "w_qkv": (D_MODEL, QKV_DIM),
               "w_o": (D_MODEL, D_MODEL), "pool_w": (4, POOL_GC, POOL_GC)}
SHARD_SHAPES = {"w_in": (2, D_MODEL, W_IN_COLS), "w_out": (2, W_OUT_ROWS, D_MODEL), "w_qkv": (D_MODEL, QKV_COLS),
                "w_o": (W_O_ROWS, D_MODEL), "pool_w": (4, POOL_ROWS, POOL_GC)}
HALF_SHAPES = {"w_in": (D_MODEL, W_IN_COLS), "w_out": (W_OUT_ROWS, D_MODEL), "w_qkv": (D_MODEL // 2, QKV_COLS),
               "w_o": (W_O_ROWS // 2, D_MODEL), "pool_w": (2, POOL_ROWS, POOL_GC)}


def _place_shard(name, shard, me_arr):
    if name == "w_in":
        blk, grid = (1, 256, W_IN_COLS), (2, D_MODEL // 256)
        src, dst = (lambda a, b, me: (a, b, 0)), (lambda a, b, me: (a, b, me[0]))
    elif name == "w_out":
        blk, grid = (1, W_OUT_ROWS, D_MODEL), (2, 1)
        src, dst = (lambda a, b, me: (a, 0, 0)), (lambda a, b, me: (a, me[0], 0))
    elif name == "w_qkv":
        blk, grid = (256, QKV_COLS), (1, D_MODEL // 256)
        src, dst = (lambda a, b, me: (b, 0)), (lambda a, b, me: (b, me[0]))
    elif name == "w_o":
        blk, grid = (W_O_ROWS, D_MODEL), (1, 1)
        src, dst = (lambda a, b, me: (0, 0)), (lambda a, b, me: (me[0], 0))
    else:
        blk, grid = (4, POOL_ROWS, POOL_GC), (1, 1)
        src, dst = (lambda a, b, me: (0, 0, 0)), (lambda a, b, me: (0, me[0], 0))

    def body(me_ref, s_ref, o_ref):
        o_ref[...] = s_ref[...].astype(o_ref.dtype)

    return pl.pallas_call(
        body, name=f"place_{name}",
        grid_spec=pltpu.PrefetchScalarGridSpec(num_scalar_prefetch=1, grid=grid, in_specs=[pl.BlockSpec(blk, src)],
                                               out_specs=pl.BlockSpec(blk, dst)),
        out_shape=jax.ShapeDtypeStruct(FULL_SHAPES[name], WIRE_DTYPE), compiler_params=_params("parallel", "parallel"),
    )(me_arr, shard)


def _when_mover(half, c, fn):
    if half is None:
        fn(c)
    else:
        pl.when(c == half)(lambda: fn(half))


def _when_taker(half, c, fn):
    if half is None:
        fn(1 - c)
    else:
        pl.when(c != half)(lambda: fn(half))


def _remote(ref, sems, t, j, dev):
    send, recv = sems
    return pltpu.make_async_remote_copy(src_ref=ref, dst_ref=ref, send_sem=send.at[t, j], recv_sem=recv.at[t, j],
                                        device_id=dev, device_id_type=MESH)


def _gather_ici(items, refs, sems, finish, peers=(0, 1, 2)):
    x, y, c, chips = _mesh_place()
    me = 2 * x + y
    for t, (nm, half) in enumerate(items):
        def go(h, t=t, nm=nm):
            mine = _full_piece(nm, refs[t], me, h)
            for j, (cx, cy) in enumerate(chips):
                if j not in peers:
                    continue
                if not finish:
                    _remote(mine, sems, t, j, (cx, cy, c)).start()
                else:
                    _remote(_full_piece(nm, refs[t], 2 * cx + cy, h), sems, t, j, (cx, cy, c)).wait_recv()
                    _remote(mine, sems, t, j, (cx, cy, c)).wait_send()

        _when_mover(half, c, go)


def _gather_pass(items, refs, sems, finish):
    x, y, c, chips = _mesh_place()
    sibling = (x, y, 1 - c)
    for t, (nm, half) in enumerate(items):
        def give(h, t=t, nm=nm):
            for j, (cx, cy) in enumerate(chips):
                got = _remote(_full_piece(nm, refs[t], 2 * cx + cy, h), sems, t, j, sibling)
                if not finish:
                    got.start()
                else:
                    got.wait_send()

        def take(h, t=t, nm=nm):
            for j, (cx, cy) in enumerate(chips):
                _remote(_full_piece(nm, refs[t], 2 * cx + cy, h), sems, t, j, sibling).wait_recv()

        _when_mover(half, c, give)
        if finish:
            _when_taker(half, c, take)


def _gather_sems(n_items):
    return [pltpu.SemaphoreType.DMA((n_items, 3)), pltpu.SemaphoreType.DMA((n_items, 3))]


EARLY_ITEMS = (("pool_w", None),)
_LATE_A = (("w_out", 0),)
_LATE_B = (("w_in", 1),)
_LATE_C = (("w_out", 1),)
_LATE_D = (("w_qkv", None), ("w_o", None))
_EARLY_W_IN = (("w_in", 0),)
GATHER_RIDES = {"pool_fwd": ((), _EARLY_W_IN, ()), "in0": ((), _LATE_A, ()), "glu0": (_LATE_A, _LATE_D, ()),
                "out0": (_LATE_D, (), ()), "attn_fwd": ((), _LATE_B, ()), "attn_out": (_LATE_B, (), ()),
                "in1": ((), _LATE_C, ()), "glu1": (_LATE_C, (), ())}
NEIGHBOURS = (0, 1)
GATHER_PEERS = {"pool_fwd": NEIGHBOURS}


def _names(items):
    return [nm for nm, _ in items]


def _gather_early(full, small):
    items = EARLY_ITEMS
    n = len(items)

    def body(*refs):
        small_in = refs[n]
        outs, small_out = refs[n + 1:2 * n + 1], refs[2 * n + 1]
        scratch = refs[2 * n + 2:]
        ici_sems, pass_sems = scratch[0:2], scratch[2:4]
        local_sem, small_send, small_recv = scratch[4:]
        x, y, c, chips = _mesh_place()
        me = 2 * x + y
        local = pltpu.make_async_copy(small_in, small_out.at[me], local_sem.at[0])
        local.start()
        small_sends = [pltpu.make_async_remote_copy(
            src_ref=small_in, dst_ref=small_out.at[me], send_sem=small_send.at[j], recv_sem=small_recv.at[j],
            device_id=(cx, cy, c), device_id_type=MESH) for j, (cx, cy) in enumerate(chips)]
        for cp in small_sends:
            cp.start()
        _gather_ici(items, outs, ici_sems, finish=False)
        _gather_ici(items, outs, ici_sems, finish=True)
        _gather_pass(items, outs, pass_sems, finish=False)
        _gather_pass(items, outs, pass_sems, finish=True)
        for j, (cx, cy) in enumerate(chips):
            got = small_out.at[2 * cx + cy]
            pltpu.make_async_remote_copy(src_ref=got, dst_ref=got, send_sem=small_send.at[j],
                                         recv_sem=small_recv.at[j], device_id=(cx, cy, c), device_id_type=MESH).wait_recv()
        for cp in small_sends:
            cp.wait_send()
        local.wait()

    arrays = [full[nm] for nm in _names(items)]
    out_shape = [jax.ShapeDtypeStruct(a.shape, a.dtype) for a in arrays]
    out_shape.append(jax.ShapeDtypeStruct((N_CHIPS,) + small.shape, F32))
    res = pl.pallas_call(
        body, name="gather_early", in_specs=[ANY] * (n + 1), out_specs=[ANY] * (n + 1), out_shape=out_shape,
        input_output_aliases={t: t for t in range(n)},
        scratch_shapes=_gather_sems(n) + _gather_sems(n) + [pltpu.SemaphoreType.DMA((1,)), pltpu.SemaphoreType.DMA((3,)),
                                                            pltpu.SemaphoreType.DMA((3,))],
    )(*arrays, small)
    return {**full, **dict(zip(_names(items), res[:n]))}, res[n]


def _late_gather_host(stage, w):
    passing, sending, direct = GATHER_RIDES[stage]
    n_pass, n_send = len(passing), len(sending)

    def run(src, refs, new, sems, finish):
        if passing:
            _gather_pass(passing, refs[:n_pass], sems[:2], finish)
        if sending:
            _gather_ici(sending, refs[n_pass:n_pass + n_send], sems[2 * bool(passing):2 * bool(passing) + 2], finish,
                        GATHER_PEERS.get(stage, (0, 1, 2)))
        if direct:
            _gather_direct(direct, refs[n_pass + n_send:], sems[-2:], finish)

    sems = ((_gather_sems(n_pass) if passing else []) + (_gather_sems(n_send) if sending else [])
            + ([pltpu.SemaphoreType.DMA((len(direct), 6))] * 2 if direct else []))
    return _Hosted([], [w[nm] for nm in _late_names(stage)], [], sems, functools.partial(run, finish=False),
                   functools.partial(run, finish=True))


def _late_names(stage):
    return [nm for items in GATHER_RIDES[stage] for nm in _names(items)]


def _gather_direct(items, refs, sems, finish):
    x, y, c, chips = _mesh_place()
    me = 2 * x + y
    for t, (nm, half) in enumerate(items):
        assert half is None
        mine = _full_piece(nm, refs[t], me, c)
        for j, (cx, cy) in enumerate(chips):
            for other in range(2):
                pc = c if other == 0 else 1 - c
                if not finish:
                    _remote(mine, sems, t, 2 * j + other, (cx, cy, pc)).start()
                else:
                    _remote(_full_piece(nm, refs[t], 2 * cx + cy, pc), sems, t, 2 * j + other, (cx, cy, pc)).wait_recv()
                    _remote(mine, sems, t, 2 * j + other, (cx, cy, pc)).wait_send()


def _other_half_src(name, ref, h):
    if name in ("w_in", "w_out"):
        return ref.at[h]
    if name == "w_qkv":
        return ref.at[_row(h, D_MODEL // 2), :]
    if name == "w_o":
        return ref.at[:, pl.ds(h, 1)]
    return ref.at[pl.ds(2 * h, 2)]


SLAB_SHAPES = {"w_in": (D_MODEL, 2 * D_FF), "w_out": (D_FF, D_MODEL), "w_qkv": (D_MODEL // 2, QKV_DIM),
               "w_o": (N_CHIPS, 1, W_O_ROWS // 2, D_MODEL), "pool_w": (2, POOL_GC, POOL_GC)}


def _item_tag(item):
    return item[0] if item[1] is None else f"{item[0]}{item[1]}"


def _reduce_swap(items, grads, slabs, sems, finish):
    x, y, c, _ = _mesh_place()
    send, recv = sems
    for t, (nm, half) in enumerate(items):
        def copy(h, t=t, nm=nm):
            return pltpu.make_async_remote_copy(
                src_ref=_other_half_src(nm, grads[t], h), dst_ref=slabs[t], send_sem=send.at[t], recv_sem=recv.at[t],
                device_id=(x, y, 1 - c), device_id_type=MESH)

        def give(h):
            if finish:
                copy(h).wait_send()
            else:
                copy(h).start()

        if half is None:
            give(1 - c)
            if finish:
                copy(c).wait_recv()
        else:
            pl.when(c != half)(lambda: give(half))
            if finish:
                pl.when(c == half)(lambda: copy(half).wait_recv())


def _reduce_scatter(items, sums, parts, sems, finish):
    _, _, c, chips = _mesh_place()
    send, recv = sems
    for t, (nm, half) in enumerate(items):
        def go(h, t=t, nm=nm):
            for j, (cx, cy) in enumerate(chips):
                cp = pltpu.make_async_remote_copy(
                    src_ref=_slab_piece(nm, sums[t], 2 * cx + cy), dst_ref=parts[t].at[j], send_sem=send.at[t, j],
                    recv_sem=recv.at[t, j], device_id=(cx, cy, c), device_id_type=MESH)
                if finish:
                    cp.wait_recv()
                    cp.wait_send()
                else:
                    cp.start()

        _when_mover(half, c, go)


def _reduce_share(items, shards, sems, finish):
    x, y, c, _ = _mesh_place()
    send, recv = sems
    for t, (nm, half) in enumerate(items):
        def copy(h, t=t, nm=nm):
            part = _shard_half(nm, shards[t], h)
            return pltpu.make_async_remote_copy(src_ref=part, dst_ref=part, send_sem=send.at[t], recv_sem=recv.at[t],
                                                device_id=(x, y, 1 - c), device_id_type=MESH)

        def give(h):
            if finish:
                copy(h).wait_send()
            else:
                copy(h).start()

        _when_mover(half, c, give)
        if finish:
            _when_taker(half, c, lambda h: copy(h).wait_recv())


def _pair_sems(n):
    return [pltpu.SemaphoreType.DMA((n,)), pltpu.SemaphoreType.DMA((n,))]


def _swap_host(items, grads):
    def run(src, arr, new, sems, finish):
        _reduce_swap(items, src, new, sems, finish)

    return _Hosted([grads[nm] for nm, _ in items], [],
                   [jax.ShapeDtypeStruct(SLAB_SHAPES[nm], grads[nm].dtype) for nm, _ in items],
                   _pair_sems(len(items)), functools.partial(run, finish=False), functools.partial(run, finish=True))


def _scatter_host(items, sums):
    def run(src, arr, new, sems, finish):
        _reduce_scatter(items, src, new, sems, finish)

    new = [jax.ShapeDtypeStruct((N_CHIPS - 1,) + HALF_SHAPES[nm], WIRE_DTYPE) for nm, _ in items]
    return _Hosted(list(sums), [], new, _gather_sems(len(items)), functools.partial(run, finish=False),
                   functools.partial(run, finish=True))


def _share_host(items, shards):
    def run(src, arr, new, sems, finish):
        _reduce_share(items, arr, sems, finish)

    return _Hosted([], list(shards), [], _pair_sems(len(items)), functools.partial(run, finish=False),
                   functools.partial(run, finish=True))


def _alone(hosted, name):
    def body(ins, outs, scratch, first, last):
        first()
        last()

    return _call(body, hosted, name=name, grid=(1,), in_specs=[], out_specs=[], out_shape=[], operands=(),
                 semantics=("arbitrary",))


def _pair_sum(item, grad, slab, place):
    name, half = item
    shape = SLAB_SHAPES[name]
    hsel = (lambda p: p[1]) if half is None else (lambda p: half)
    if name == "w_in":
        blk = (256, 2 * D_FF)
        grid = (D_MODEL // 256,)
        g_spec = pl.BlockSpec((None,) + blk, lambda i, p: (hsel(p), i, 0))
        s_spec = pl.BlockSpec(blk, lambda i, p: (i, 0))
    elif name == "w_out":
        blk = (W_OUT_ROWS, D_MODEL)
        grid = (N_CHIPS,)
        g_spec = pl.BlockSpec((None,) + blk, lambda i, p: (hsel(p), i, 0))
        s_spec = pl.BlockSpec(blk, lambda i, p: (i, 0))
    elif name == "w_qkv":
        grid = (1,)
        g_spec = pl.BlockSpec(shape, lambda i, p: (hsel(p), 0))
        s_spec = pl.BlockSpec(shape, lambda i, p: (0, 0))
    else:
        grid = (1,)
        g_spec = pl.BlockSpec(shape, lambda i, p: (0, hsel(p), 0, 0))
        s_spec = pl.BlockSpec(shape, lambda i, p: (0, 0, 0, 0))

    def body(p_ref, g_ref, s_ref, o_ref):
        @pl.when(hsel(p_ref) == p_ref[1])
        def _():
            o_ref[...] = (g_ref[...].astype(F32) + s_ref[...].astype(F32)).astype(o_ref.dtype)

    return pl.pallas_call(
        body, name=f"pair_sum_{_item_tag(item)}",
        grid_spec=pltpu.PrefetchScalarGridSpec(num_scalar_prefetch=1, grid=grid, in_specs=[g_spec, s_spec],
                                               out_specs=s_spec),
        out_shape=jax.ShapeDtypeStruct(shape, WIRE_DTYPE), compiler_params=_params("parallel"),
    )(place, grad, slab)


def _slab_piece(name, ref, k):
    if name == "w_in":
        return ref.at[:, _col(k, W_IN_COLS)]
    if name == "w_out":
        return ref.at[_row(k, W_OUT_ROWS), :]
    if name == "w_qkv":
        return ref.at[:, _col(k, QKV_COLS)]
    if name == "w_o":
        return ref.at[k, 0]
    return ref.at[:, _row(k, POOL_ROWS), :]


def _sum_chips(item, slab, parts, place, into=None):
    name, sel = item
    half = HALF_SHAPES[name]
    hsel = (lambda p: p[1]) if sel is None else (lambda p: sel)
    if name == "w_in":
        blk, grid = (256, W_IN_COLS), (D_MODEL // 256,)
        own = pl.BlockSpec(blk, lambda i, p: (i, p[0]))
        got = pl.BlockSpec((N_CHIPS - 1,) + blk, lambda i, p: (0, i, 0))
        out = pl.BlockSpec((None,) + blk, lambda i, p: (hsel(p), i, 0))
    elif name == "w_out":
        grid = (1,)
        own = pl.BlockSpec(half, lambda i, p: (p[0], 0))
        got = pl.BlockSpec((N_CHIPS - 1,) + half, lambda i, p: (0, 0, 0))
        out = pl.BlockSpec((None,) + half, lambda i, p: (hsel(p), 0, 0))
    elif name == "w_qkv":
        grid = (1,)
        own = pl.BlockSpec(half, lambda i, p: (0, p[0]))
        got = pl.BlockSpec((N_CHIPS - 1,) + half, lambda i, p: (0, 0, 0))
        out = pl.BlockSpec(half, lambda i, p: (hsel(p), 0))
    else:
        grid = (1,)
        own = pl.BlockSpec((None, None) + half, lambda i, p: (p[0], 0, 0, 0))
        got = pl.BlockSpec((N_CHIPS - 1,) + half, lambda i, p: (0, 0, 0))
        out = pl.BlockSpec(half, lambda i, p: (hsel(p), 0))

    def body(p_ref, own_ref, got_ref, *rest):
        o_ref = rest[-1]

        @pl.when(hsel(p_ref) == p_ref[1])
        def _():
            acc = own_ref[...].astype(F32)
            for j in range(N_CHIPS - 1):
                acc = acc + got_ref[j].astype(F32)
            o_ref[...] = acc

    in_specs, operands, aliases = [own, got], [place, slab, parts], {}
    if into is not None:
        in_specs.append(ANY)
        operands.append(into)
        aliases = {3: 0}
    return pl.pallas_call(
        body, name=f"sum_chips_{_item_tag(item)}",
        grid_spec=pltpu.PrefetchScalarGridSpec(num_scalar_prefetch=1, grid=grid, in_specs=in_specs, out_specs=out),
        out_shape=jax.ShapeDtypeStruct(SHARD_SHAPES[name], F32), input_output_aliases=aliases,
        compiler_params=_params("parallel"),
    )(*operands)


class _Reducer:
    FIRST = (("w_in", 1), ("w_out", 1))
    MID = (("w_out", 0), ("w_qkv", None), ("w_o", None))
    LAST = (("w_in", 0),)

    def __init__(self, place):
        self.place = place
        self.shards = {}
        self.state = {}

    def swap(self, items, grads):
        self.state[items] = {"grads": [grads[nm] for nm, _ in items]}
        return _swap_host(items, grads)

    def pair_sums(self, items, slabs):
        st = self.state[items]
        st["sums"] = [_pair_sum(it, g, s, self.place) for it, g, s in zip(items, st["grads"], slabs)]

    def scatter(self, items):
        return _scatter_host(items, self.state[items]["sums"])

    def chip_sums(self, items, parts):
        for it, s, p in zip(items, self.state[items]["sums"], parts):
            self.shards[it[0]] = _sum_chips(it, s, p, self.place, into=self.shards.get(it[0]))

    def share(self, items):
        return _share_host(items, [self.shards[nm] for nm, _ in items])

    def shared(self, items, updated):
        self.shards.update(dict(zip(_names(items), updated)))


N_DEV = 8


def _all_reduce_small(vec, g_pool):
    def body(v_ref, gp_ref, o_ref, po_ref, slots, pslots, send_sems, recv_sems, psend, precv, hsend, hrecv):
        x, y, c, _ = _mesh_place()
        me = 4 * x + 2 * y + c

        def pool_piece(chip, h):
            return gp_ref.at[pl.ds(2 * h, 2), pl.ds(pl.multiple_of(chip * POOL_ROWS, POOL_ROWS), POOL_ROWS), :]

        slots[me] = v_ref[...]
        pslots[me] = pool_piece(2 * x + y, c)[...]
        copies = []
        for d in range(1, N_DEV):
            peer = (x ^ (d >> 2), y ^ ((d >> 1) & 1), c ^ (d & 1))
            copies.append(pltpu.make_async_remote_copy(
                src_ref=v_ref, dst_ref=slots.at[me], send_sem=send_sems.at[d - 1], recv_sem=recv_sems.at[d - 1],
                device_id=peer, device_id_type=MESH))
            copies.append(pltpu.make_async_remote_copy(
                src_ref=pool_piece(2 * peer[0] + peer[1], peer[2]), dst_ref=pslots.at[me], send_sem=psend.at[d - 1],
                recv_sem=precv.at[d - 1], device_id=peer, device_id_type=MESH))
        for cp in copies:
            cp.start()
        for d in range(1, N_DEV):
            peer = 4 * (x ^ (d >> 2)) + 2 * (y ^ ((d >> 1) & 1)) + (c ^ (d & 1))
            for buf, ss, rs in ((slots, send_sems, recv_sems), (pslots, psend, precv)):
                got = buf.at[peer]
                pltpu.make_async_remote_copy(src_ref=got, dst_ref=got, send_sem=ss.at[d - 1], recv_sem=rs.at[d - 1],
                                             device_id=(x, y, c), device_id_type=MESH).wait_recv()
        for cp in copies:
            cp.wait_send()
        acc, pacc = slots[0], pslots[0]
        for d in range(1, N_DEV):
            acc, pacc = acc + slots[d], pacc + pslots[d]
        o_ref[...] = acc
        po_ref[pl.ds(2 * c, 2)] = pacc
        mine, theirs = po_ref.at[pl.ds(2 * c, 2)], po_ref.at[pl.ds(2 * (1 - c), 2)]
        give = pltpu.make_async_remote_copy(src_ref=mine, dst_ref=mine, send_sem=hsend.at[0], recv_sem=hrecv.at[0],
                                            device_id=(x, y, 1 - c), device_id_type=MESH)
        give.start()
        pltpu.make_async_remote_copy(src_ref=theirs, dst_ref=theirs, send_sem=hsend.at[0], recv_sem=hrecv.at[0],
                                     device_id=(x, y, 1 - c), device_id_type=MESH).wait_recv()
        give.wait_send()

    vm = pl.BlockSpec(memory_space=pltpu.VMEM)
    piece = (2, POOL_ROWS, POOL_GC)
    return pl.pallas_call(
        body, name="all_reduce_small", in_specs=[vm, vm], out_specs=[vm, vm],
        out_shape=[jax.ShapeDtypeStruct(vec.shape, F32), jax.ShapeDtypeStruct(SHARD_SHAPES["pool_w"], F32)],
        scratch_shapes=[pltpu.VMEM((N_DEV,) + vec.shape, F32), pltpu.VMEM((N_DEV,) + piece, F32)]
        + [pltpu.SemaphoreType.DMA((N_DEV - 1,))] * 4 + [pltpu.SemaphoreType.DMA((1,))] * 2,
    )(vec, g_pool)


def _adamw_call(w, g, m, v, name, copy_grad=False):
    shape = w.shape
    cols = shape[-1]
    rows = w.size // cols
    tr = rows
    for cand in (256, 128, 64, 32, 16, 8):
        if rows > cand and rows % cand == 0 and cand * cols * 4 <= 2 * 1024 * 1024:
            tr = cand
            break

    def body(w_ref, g_ref, m_ref, v_ref, d_ref, nm_ref, nv_ref, *g_out):
        gv = g_ref[...]
        if copy_grad:
            g_out[0][...] = gv
        mn = ADAM_B1 * m_ref[...] + (1.0 - ADAM_B1) * gv
        vn = ADAM_B2 * v_ref[...] + (1.0 - ADAM_B2) * jnp.square(gv)
        m_hat = mn / (1.0 - ADAM_B1 ** ADAM_STEP)
        v_hat = vn / (1.0 - ADAM_B2 ** ADAM_STEP)
        d_ref[...] = -ADAM_LR * (m_hat / (jnp.sqrt(v_hat) + ADAM_EPS) + ADAM_WD * w_ref[...])
        nm_ref[...] = mn
        nv_ref[...] = vn

    spec = pl.BlockSpec((tr, cols), lambda i: (i, 0))
    flat = lambda a: a.reshape(rows, cols)
    n_out = 4 if copy_grad else 3
    outs = pl.pallas_call(
        body, name=name, grid=(rows // tr,), in_specs=[spec] * 4, out_specs=[spec] * n_out,
        out_shape=[jax.ShapeDtypeStruct((rows, cols), F32)] * n_out, compiler_params=_params("parallel"),
    )(flat(w), flat(g), flat(m), flat(v))
    return tuple(o.reshape(shape) for o in outs)


WEIGHTS = ("pool_w", "pool_b", "pool_scale", "attn_w_qkv", "attn_b_qkv", "attn_sinks", "attn_w_o", "norm_mix",
           "norm_ffn", "ffn_w_in", "ffn_conv_w", "ffn_conv_b", "ffn_w_out", "norm_f")
SMALL_GATHER = 128 * 71
_SMALL_SIZES = (("pool_b", 1024), ("pool_scale", 1024), ("b_qkv", 1536), ("sinks", 128), ("norm_mix", 2048),
                ("norm_ffn", 2048), ("conv_w", 6 * 2 * D_FF), ("conv_b", 4 * D_FF), ("norm_f", 1024))
SMALL_COLS = 6784


def _pack_small(grads):
    flat = jnp.concatenate([jnp.pad(grads[nm].reshape(-1), (0, size - grads[nm].size)) for nm, size in _SMALL_SIZES])
    return jnp.pad(flat, (0, 8 * SMALL_COLS - flat.size)).reshape(8, SMALL_COLS)


def _unpack_small(vec):
    flat = vec.reshape(-1)
    out, off = {}, 0
    for nm, size in _SMALL_SIZES:
        out[nm] = flat[off:off + size]
        off += size
    return out


def kernel(x, pool_w, pool_b, pool_scale, attn_w_qkv, attn_b_qkv, attn_sinks, attn_w_o, norm_mix, norm_ffn, ffn_w_in, ffn_conv_w, ffn_conv_b, ffn_w_out, norm_f, loss_target, m_pool_w, m_pool_b, m_pool_scale, m_attn_w_qkv, m_attn_b_qkv, m_attn_sinks, m_attn_w_o, m_norm_mix, m_norm_ffn, m_ffn_w_in, m_ffn_conv_w, m_ffn_conv_b, m_ffn_w_out, m_norm_f, v_pool_w, v_pool_b, v_pool_scale, v_attn_w_qkv, v_attn_b_qkv, v_attn_sinks, v_attn_w_o, v_norm_mix, v_norm_ffn, v_ffn_w_in, v_ffn_conv_w, v_ffn_conv_b, v_ffn_w_out, v_norm_f):
    weights = dict(pool_w=pool_w, pool_b=pool_b, pool_scale=pool_scale, attn_w_qkv=attn_w_qkv, attn_b_qkv=attn_b_qkv,
                   attn_sinks=attn_sinks, attn_w_o=attn_w_o, norm_mix=norm_mix, norm_ffn=norm_ffn, ffn_w_in=ffn_w_in,
                   ffn_conv_w=ffn_conv_w, ffn_conv_b=ffn_conv_b, ffn_w_out=ffn_w_out, norm_f=norm_f)
    m_in = dict(pool_w=m_pool_w, pool_b=m_pool_b, pool_scale=m_pool_scale, attn_w_qkv=m_attn_w_qkv,
                attn_b_qkv=m_attn_b_qkv, attn_sinks=m_attn_sinks, attn_w_o=m_attn_w_o, norm_mix=m_norm_mix,
                norm_ffn=m_norm_ffn, ffn_w_in=m_ffn_w_in, ffn_conv_w=m_ffn_conv_w, ffn_conv_b=m_ffn_conv_b,
                ffn_w_out=m_ffn_w_out, norm_f=m_norm_f)
    v_in = dict(pool_w=v_pool_w, pool_b=v_pool_b, pool_scale=v_pool_scale, attn_w_qkv=v_attn_w_qkv,
                attn_b_qkv=v_attn_b_qkv, attn_sinks=v_attn_sinks, attn_w_o=v_attn_w_o, norm_mix=v_norm_mix,
                norm_ffn=v_norm_ffn, ffn_w_in=v_ffn_w_in, ffn_conv_w=v_ffn_conv_w, ffn_conv_b=v_ffn_conv_b,
                ffn_w_out=v_ffn_w_out, norm_f=v_norm_f)
    chip = 2 * lax.axis_index("x") + lax.axis_index("y")
    core = lax.axis_index("c")

    place = jnp.stack([chip, core]).astype(jnp.int32)
    shards = {"w_in": ffn_w_in, "w_out": ffn_w_out, "w_qkv": attn_w_qkv[0], "w_o": attn_w_o[0], "pool_w": pool_w[0]}
    placed = {nm: _place_shard(nm, shards[nm], place) for nm in BIG}
    small = jnp.concatenate([pool_b.reshape(-1), attn_b_qkv.reshape(-1), ffn_conv_w.reshape(-1)]).reshape(1, SMALL_GATHER)
    full, small_all = _gather_early(placed, small)
    small_all = small_all.reshape(N_CHIPS, SMALL_GATHER)
    pool_b_full = small_all[:, :256].reshape(N_CHIPS, 4, POOL_ROWS).transpose(1, 0, 2).reshape(1, D_MODEL)
    b_qkv_full = small_all[:, 256:640].reshape(1, QKV_DIM)
    conv_w_full = small_all[:, 640:].reshape(N_CHIPS, 2, 3, W_IN_COLS).transpose(1, 2, 0, 3).reshape(2, 3, 2 * D_FF)

    reducer = _Reducer(place)
    loss, grad_x, grads = _local_step(
        x[0], loss_target[0], full, norm_mix=norm_mix, norm_ffn=norm_ffn, norm_f=norm_f.reshape(1, D_MODEL),
        pool_b=pool_b_full, pool_scale=pool_scale, b_qkv=b_qkv_full, sinks=attn_sinks[0], conv_w=conv_w_full,
        conv_b=ffn_conv_b, late_gather=_late_gather_host, reducer=reducer)
    loss = lax.psum(loss[0, 0], ("x", "y", "c"))
    reduced = dict(reducer.shards)

    small_sum, reduced["pool_w"] = _all_reduce_small(_pack_small(grads), grads["pool_w"])
    small_g = _unpack_small(small_sum)
    pool_b_g = lax.dynamic_slice_in_dim(small_g["pool_b"].reshape(4, N_CHIPS, POOL_ROWS), chip, 1, axis=1)
    b_qkv_g = lax.dynamic_slice_in_dim(small_g["b_qkv"].reshape(N_CHIPS, QKV_COLS), chip, 1, axis=0)
    conv_w_g = lax.dynamic_slice_in_dim(small_g["conv_w"].reshape(2, 3, N_CHIPS, W_IN_COLS), chip, 1, axis=2)
    grad_w = {
        "pool_w": reduced["pool_w"].reshape(pool_w.shape), "pool_b": pool_b_g.reshape(pool_b.shape),
        "pool_scale": small_g["pool_scale"].reshape(pool_scale.shape),
        "attn_w_qkv": reduced["w_qkv"].reshape(attn_w_qkv.shape), "attn_b_qkv": b_qkv_g.reshape(attn_b_qkv.shape),
        "attn_sinks": small_g["sinks"][:N_HEADS].reshape(attn_sinks.shape),
        "attn_w_o": reduced["w_o"].reshape(attn_w_o.shape), "norm_mix": small_g["norm_mix"].reshape(norm_mix.shape),
        "norm_ffn": small_g["norm_ffn"].reshape(norm_ffn.shape), "ffn_w_in": reduced["w_in"],
        "ffn_conv_w": conv_w_g.reshape(ffn_conv_w.shape), "ffn_conv_b": small_g["conv_b"].reshape(ffn_conv_b.shape),
        "ffn_w_out": reduced["w_out"], "norm_f": small_g["norm_f"].reshape(norm_f.shape),
    }

    delta, new_m, new_v = {}, {}, {}
    exchanged = ("pool_w", "attn_w_qkv", "attn_w_o", "ffn_w_in", "ffn_w_out")
    for nm in WEIGHTS:
        res = _adamw_call(weights[nm], grad_w[nm], m_in[nm], v_in[nm], f"adamw_{nm}", copy_grad=nm in exchanged)
        delta[nm], new_m[nm], new_v[nm] = res[:3]
        if nm in exchanged:
            grad_w[nm] = res[3]
    return (loss, grad_x.reshape(x.shape), *[grad_w[nm] for nm in WEIGHTS], *[delta[nm] for nm in WEIGHTS],
            *[new_m[nm] for nm in WEIGHTS], *[new_v[nm] for nm in WEIGHTS])
```

```python
import functools
from typing import Callable, NamedTuple

import jax
import jax.numpy as jnp
from jax import lax
from jax.experimental import pallas as pl
from jax.experimental.pallas import tpu as pltpu

F32 = jnp.float32
MXU_DTYPE = jnp.bfloat16
WIRE_DTYPE = jnp.bfloat16
U_DTYPE = jnp.float32

D_MODEL = 1024
D_FF = 2816
QKV_DIM = 1536
HEAD_DIM = 64
N_HEADS = 16
N_KV_HEADS = 4
ATT_BLOCK = 128
POOL_WINDOWS = (2, 4, 8, 16)
POOL_GC = 256
POOL_HALO = 16
CONV_HALO = 8
RMS_EPS = 1e-6
ATT_SCALE = HEAD_DIM ** -0.5
ALIBI_SLOPES = tuple(2.0 ** (-8.0 / N_HEADS * (h + 1)) for h in range(N_HEADS))

ADAM_LR = 0.001
ADAM_B1 = 0.9
ADAM_B2 = 0.999
ADAM_EPS = 1e-08
ADAM_WD = 0.01
ADAM_STEP = 10

N_CHIPS = 4
MESH = pl.DeviceIdType.MESH
VMEM_LIMIT_BYTES = 56 * 1024 * 1024
ANY = pl.BlockSpec(memory_space=pl.ANY)


def _params(*semantics):
    return pltpu.CompilerParams(dimension_semantics=semantics, vmem_limit_bytes=VMEM_LIMIT_BYTES)


def _rms(x, g):
    return x * lax.rsqrt(jnp.mean(x * x, axis=-1, keepdims=True) + RMS_EPS) * g


def _rms_bwd(x, g, dy):
    rstd = lax.rsqrt(jnp.mean(x * x, axis=-1, keepdims=True) + RMS_EPS)
    xhat = x * rstd
    dxhat = dy * g
    dx = rstd * (dxhat - xhat * jnp.mean(dxhat * xhat, axis=-1, keepdims=True))
    return dx, dy * xhat


def _shift_down(x, s):
    return pltpu.roll(x, s, axis=0)


def _shift_up(x, s):
    return pltpu.roll(x, x.shape[0] - s, axis=0)


def _sigmoid(x):
    return 1.0 / (1.0 + jnp.exp(-x))


_DOT_DIMS = {"nn": ((1,), (0,)), "nt": ((1,), (1,)), "tn": ((0,), (0,))}


class _Hosted(NamedTuple):
    sources: list
    arrays: list
    new: list
    sems: list
    first: Callable
    last: Callable


def _call(body, hosted, *, name, grid, in_specs, out_specs, out_shape, operands, semantics, scratch_shapes=(),
          aliases=None):
    n_in, n_out, n_scr = len(in_specs), len(out_specs), len(scratch_shapes)
    aliases = dict(aliases or {})
    if hosted is None:
        def plain(*refs):
            body(refs[:n_in], refs[n_in:n_in + n_out], refs[n_in + n_out:], lambda: None, lambda: None)

        return pl.pallas_call(plain, name=name, grid=grid, in_specs=list(in_specs), out_specs=list(out_specs),
                              out_shape=list(out_shape), scratch_shapes=list(scratch_shapes),
                              input_output_aliases=aliases, compiler_params=_params(*semantics))(*operands)
    ns, na, nn = len(hosted.sources), len(hosted.arrays), len(hosted.new)

    def carrying(*refs):
        ins, src = refs[:n_in], refs[n_in:n_in + ns]
        o0 = n_in + ns + na
        outs, arr = refs[o0:o0 + n_out], refs[o0 + n_out:o0 + n_out + na]
        new = refs[o0 + n_out + na:o0 + n_out + na + nn]
        s0 = o0 + n_out + na + nn
        scratch, sems = refs[s0:s0 + n_scr], refs[s0 + n_scr:]
        ids = [pl.program_id(d) for d in range(len(grid))]
        is_first = functools.reduce(lambda p, q: p & q, [i == 0 for i in ids])
        is_last = functools.reduce(lambda p, q: p & q, [i == n - 1 for i, n in zip(ids, grid)])
        body(ins, outs, scratch, lambda: pl.when(is_first)(lambda: hosted.first(src, arr, new, sems)),
             lambda: pl.when(is_last)(lambda: hosted.last(src, arr, new, sems)))

    for t in range(na):
        aliases[n_in + ns + t] = n_out + t
    return pl.pallas_call(
        carrying, name=name, grid=grid, in_specs=list(in_specs) + [ANY] * (ns + na),
        out_specs=list(out_specs) + [ANY] * (na + nn),
        out_shape=list(out_shape) + [jax.ShapeDtypeStruct(a.shape, a.dtype) for a in hosted.arrays] + list(hosted.new),
        scratch_shapes=list(scratch_shapes) + list(hosted.sems), input_output_aliases=aliases,
        compiler_params=_params(*["arbitrary"] * len(grid)))(*operands, *hosted.sources, *hosted.arrays)


def _both(h1, h2):
    def run(which, src, arr, new, sems):
        cut = lambda seq, n: (seq[:n], seq[n:])
        (s1, s2), (a1, a2) = cut(src, len(h1.sources)), cut(arr, len(h1.arrays))
        (n1, n2), (m1, m2) = cut(new, len(h1.new)), cut(sems, len(h1.sems))
        getattr(h1, which)(s1, a1, n1, m1)
        getattr(h2, which)(s2, a2, n2, m2)

    return _Hosted(h1.sources + h2.sources, h1.arrays + h2.arrays, h1.new + h2.new, h1.sems + h2.sems,
                   functools.partial(run, "first"), functools.partial(run, "last"))


def _ride(hooks, stage, call, env=None):
    if not hooks or stage not in hooks:
        res = call(None)
        return list(res) if isinstance(res, (list, tuple)) else [res]
    make, done = hooks[stage]
    hosted = make(env)
    res = call(hosted)
    own = len(res) - len(hosted.arrays) - len(hosted.new)
    done(res[own:])
    return list(res[:own])


def _matmul(a, b, *, mode, name, grid, a_spec, b_spec, o_spec, out_shape, bias=None, bias_spec=None, residual=None,
            residual_spec=None, into=None, hosted=None, norm_g=None):
    nk = grid[2]
    dims = (_DOT_DIMS[mode], ((), ()))
    acc_shape = tuple(d for d in o_spec.block_shape if d is not None)

    def body(ins, outs, scratch, first, last):
        a_ref, b_ref = ins[0], ins[1]
        pos = 2
        bias_ref = res_ref = None
        if bias is not None:
            bias_ref = ins[pos]
            pos += 1
        if residual is not None:
            res_ref = ins[pos]
            pos += 1
        o_ref = outs[0]
        acc_ref = scratch[0] if nk > 1 else None
        first()
        prod = lax.dot_general(a_ref[...].astype(MXU_DTYPE), b_ref[...].astype(MXU_DTYPE), dims,
                               preferred_element_type=F32)

        def finish(acc):
            if bias_ref is not None:
                acc = acc + bias_ref[...]
            if res_ref is not None:
                acc = acc + res_ref[...]
            o_ref[...] = acc.astype(o_ref.dtype)
            if norm_g is not None:
                outs[1][...] = _rms(acc, ins[pos][...]).astype(outs[1].dtype)

        if nk == 1:
            finish(prod)
        else:
            k = pl.program_id(2)

            @pl.when(k == 0)
            def _():
                acc_ref[...] = prod

            @pl.when(k > 0)
            def _():
                acc_ref[...] += prod

            @pl.when(k == nk - 1)
            def _():
                finish(acc_ref[...])

        last()

    operands, in_specs = [a, b], [a_spec, b_spec]
    if bias is not None:
        operands.append(bias)
        in_specs.append(bias_spec)
    if residual is not None:
        operands.append(residual)
        in_specs.append(residual_spec)
    out_specs, out_shapes = [o_spec], [out_shape]
    if norm_g is not None:
        assert acc_shape[-1] == norm_g.shape[-1], "the fused norm needs whole rows in an output tile"
        operands.append(norm_g)
        in_specs.append(bias_spec)
        out_specs.append(o_spec)
        out_shapes.append(jax.ShapeDtypeStruct(out_shape.shape, MXU_DTYPE))
    aliases = {}
    if into is not None:
        aliases = {len(operands): 0}
        operands.append(into)
        in_specs.append(ANY)
    res = _call(body, hosted, name=name, grid=grid, in_specs=in_specs, out_specs=out_specs, out_shape=out_shapes,
                operands=operands, semantics=("parallel", "parallel", "arbitrary"),
                scratch_shapes=[pltpu.VMEM(acc_shape, F32)] if nk > 1 else [], aliases=aliases)
    return res if hosted or norm_g is not None else res[0]


def _tile(n, want):
    t = min(n, want)
    assert n % t == 0, (n, want)
    return t


def _mm_nn(a, b, name, *, b_lead=None, bias=None, residual=None, out_dtype=F32, tm=1024, tn=1024, tk=1024,
           hosted=None, norm_g=None):
    (m, k), n = a.shape, b.shape[-1]
    tm, tn, tk = _tile(m, tm), _tile(n, tn), _tile(k, tk)
    if b_lead is None:
        b_spec = pl.BlockSpec((tk, tn), lambda i, j, kk: (kk, j))
    else:
        b_spec = pl.BlockSpec((None, tk, tn), lambda i, j, kk: (b_lead, kk, j))
    return _matmul(
        a, b, mode="nn", name=name, grid=(m // tm, n // tn, k // tk),
        a_spec=pl.BlockSpec((tm, tk), lambda i, j, kk: (i, kk)), b_spec=b_spec,
        o_spec=pl.BlockSpec((tm, tn), lambda i, j, kk: (i, j)), out_shape=jax.ShapeDtypeStruct((m, n), out_dtype),
        bias=bias, bias_spec=pl.BlockSpec((1, tn), lambda i, j, kk: (0, j)),
        residual=residual, residual_spec=pl.BlockSpec((tm, tn), lambda i, j, kk: (i, j)), hosted=hosted,
        norm_g=norm_g)


def _mm_nt(a, b, name, *, b_lead=None, out_dtype=F32, tm=1024, tn=1024, tk=1024, hosted=None):
    (m, k), n = a.shape, b.shape[-2]
    tm, tn, tk = _tile(m, tm), _tile(n, tn), _tile(k, tk)
    if b_lead is None:
        b_spec = pl.BlockSpec((tn, tk), lambda i, j, kk: (j, kk))
    else:
        b_spec = pl.BlockSpec((None, tn, tk), lambda i, j, kk: (b_lead, j, kk))
    return _matmul(
        a, b, mode="nt", name=name, grid=(m // tm, n // tn, k // tk),
        a_spec=pl.BlockSpec((tm, tk), lambda i, j, kk: (i, kk)), b_spec=b_spec,
        o_spec=pl.BlockSpec((tm, tn), lambda i, j, kk: (i, j)), out_shape=jax.ShapeDtypeStruct((m, n), out_dtype),
        hosted=hosted)


def _mm_tn(a, b, name, *, lead=None, n_lead=None, into=None, tm=1024, tn=1024, tk=1024, out_dtype=F32):
    (k, m), n = a.shape, b.shape[-1]
    tm, tn, tk = _tile(m, tm), _tile(n, tn), _tile(k, tk)
    if lead is None:
        o_spec = pl.BlockSpec((tm, tn), lambda i, j, kk: (i, j))
        out_shape = jax.ShapeDtypeStruct((m, n), out_dtype)
    else:
        o_spec = pl.BlockSpec((None, tm, tn), lambda i, j, kk: (lead, i, j))
        out_shape = jax.ShapeDtypeStruct((n_lead, m, n), out_dtype)
    return _matmul(
        a, b, mode="tn", name=name, grid=(m // tm, n // tn, k // tk),
        a_spec=pl.BlockSpec((tk, tm), lambda i, j, kk: (kk, i)),
        b_spec=pl.BlockSpec((tk, tn), lambda i, j, kk: (kk, j)), o_spec=o_spec, out_shape=out_shape, into=into)


ROW_TILE = 512


def _pool_windows_causal(ext, first_row):
    tt = ext.shape[0] - POOL_HALO
    t = first_row + lax.broadcasted_iota(jnp.int32, (tt, 1), 0)
    outs = []
    for gi, win in enumerate(POOL_WINDOWS):
        cols = slice(gi * POOL_GC, (gi + 1) * POOL_GC)
        s = ext[:, cols]
        sh = 1
        while sh < win:
            s = s + _shift_down(s, sh)
            sh *= 2
        count = jnp.minimum(t + 1, win).astype(F32)
        outs.append(s[POOL_HALO:] / count - ext[POOL_HALO:, cols])
    return outs


def _prev_halo_spec(tt, halo, width):
    per = tt // halo
    return pl.BlockSpec((halo, width), lambda i: (jnp.maximum(i * per - 1, 0), 0))


def _pool_fwd_call(h0, g_mix, pool_w, pool_b, pool_scale, g_ffn, hosted=None):
    t = h0.shape[0]
    tt = _tile(t, ROW_TILE)

    def body(ins, outs, scratch, first, last):
        h_ref, halo_ref, gm_ref, w_ref, b_ref, sc_ref, gf_ref = ins
        h1_ref, hn_ref = outs
        i = pl.program_id(0)
        first()
        x, gm = h_ref[...], gm_ref[...]
        top = jnp.where(i > 0, _rms(halo_ref[...], gm), 0.0)
        ext = jnp.concatenate([top, _rms(x, gm)], axis=0)
        ps = _pool_windows_causal(ext, i * tt)
        ys = [jnp.dot(p.astype(MXU_DTYPE), w_ref[gi], preferred_element_type=F32) for gi, p in enumerate(ps)]
        mix = (jnp.concatenate(ys, axis=1) + b_ref[...]) * sc_ref[...]
        h1 = x + mix
        h1_ref[...] = h1
        hn_ref[...] = _rms(h1, gf_ref[...]).astype(hn_ref.dtype)
        last()

    row = pl.BlockSpec((tt, D_MODEL), lambda i: (i, 0))
    vec = pl.BlockSpec((1, D_MODEL), lambda i: (0, 0))
    wsp = pl.BlockSpec((len(POOL_WINDOWS), POOL_GC, POOL_GC), lambda i: (0, 0, 0))
    return _call(
        body, hosted, name="pool_fwd", grid=(t // tt,),
        in_specs=[row, _prev_halo_spec(tt, POOL_HALO, D_MODEL), vec, wsp, vec, vec, vec], out_specs=[row, row],
        out_shape=[jax.ShapeDtypeStruct(h0.shape, F32), jax.ShapeDtypeStruct(h0.shape, MXU_DTYPE)],
        operands=(h0, h0, g_mix, pool_w, pool_b, pool_scale, g_ffn), semantics=("parallel",))


def _pool_bwd_call(h0, dh1, g_mix, pool_w, pool_b, pool_scale, hosted=None):
    t = h0.shape[0]
    tt = _tile(t, ROW_TILE)
    nt = t // tt
    per = tt // POOL_HALO

    def body(ins, outs, scratch, first, last):
        h_ref, halo_ref, d_ref, dnext_ref, gm_ref, w_ref, b_ref, sc_ref = ins
        gx_ref, dw_ref, dv_ref = outs
        i = pl.program_id(0)
        first()
        x, gm, sc = h_ref[...], gm_ref[...], sc_ref[...]
        top = jnp.where(i > 0, _rms(halo_ref[...], gm), 0.0)
        ext = jnp.concatenate([top, _rms(x, gm)], axis=0)
        ps = _pool_windows_causal(ext, i * tt)

        dy = d_ref[...]
        dy_ext = jnp.concatenate([dy, jnp.where(i < nt - 1, dnext_ref[...], 0.0)], axis=0)
        dpre_ext = dy_ext * sc
        t_ext = i * tt + lax.broadcasted_iota(jnp.int32, (tt + POOL_HALO, 1), 0)

        @pl.when(i == 0)
        def _():
            dw_ref[...] = jnp.zeros_like(dw_ref)
            dv_ref[...] = jnp.zeros_like(dv_ref)

        dhn, ypre = [], []
        for gi, win in enumerate(POOL_WINDOWS):
            cols = slice(gi * POOL_GC, (gi + 1) * POOL_GC)
            w = w_ref[gi]
            p16 = ps[gi].astype(MXU_DTYPE)
            dpre16 = dpre_ext[:, cols].astype(MXU_DTYPE)
            ypre.append(jnp.dot(p16, w, preferred_element_type=F32))
            dw_ref[gi] += lax.dot_general(p16, dpre16[:tt], (((0,), (0,)), ((), ())), preferred_element_type=F32)
            dp_ext = lax.dot_general(dpre16, w, (((1,), (1,)), ((), ())), preferred_element_type=F32)
            s = dp_ext / jnp.minimum(t_ext + 1, win).astype(F32)
            sh = 1
            while sh < win:
                s = s + _shift_up(s, sh)
                sh *= 2
            dhn.append(s[:tt] - dp_ext[:tt])
        dhn = jnp.concatenate(dhn, axis=1)
        ypre = jnp.concatenate(ypre, axis=1) + b_ref[...]
        dx, dgt = _rms_bwd(x, gm, dhn)
        gx_ref[...] = dy + dx
        dv_ref[0:1, :] += jnp.sum(dpre_ext[:tt], axis=0, keepdims=True)
        dv_ref[1:2, :] += jnp.sum(dy * ypre, axis=0, keepdims=True)
        dv_ref[2:3, :] += jnp.sum(dgt, axis=0, keepdims=True)
        last()

    row = pl.BlockSpec((tt, D_MODEL), lambda i: (i, 0))
    vec = pl.BlockSpec((1, D_MODEL), lambda i: (0, 0))
    nxt = pl.BlockSpec((POOL_HALO, D_MODEL), lambda i: (jnp.minimum((i + 1) * per, t // POOL_HALO - 1), 0))
    wsp = pl.BlockSpec((len(POOL_WINDOWS), POOL_GC, POOL_GC), lambda i: (0, 0, 0))
    return _call(
        body, hosted, name="pool_bwd", grid=(nt,),
        in_specs=[row, _prev_halo_spec(tt, POOL_HALO, D_MODEL), row, nxt, vec, wsp, vec, vec],
        out_specs=[row, wsp, pl.BlockSpec((8, D_MODEL), lambda i: (0, 0))],
        out_shape=[jax.ShapeDtypeStruct(h0.shape, F32),
                   jax.ShapeDtypeStruct((len(POOL_WINDOWS), POOL_GC, POOL_GC), F32),
                   jax.ShapeDtypeStruct((8, D_MODEL), F32)],
        operands=(h0, h0, dh1, dh1, g_mix, pool_w, pool_b, pool_scale), semantics=("arbitrary",))


CONV_ROWS = 512
CONV_SUB = 64
LANES = 128


def _causal_conv(u_ext, w, b):
    return w[0:1] * _shift_down(u_ext, 2) + w[1:2] * _shift_down(u_ext, 1) + w[2:3] * u_ext + b


def _piece_rows(load, s, sub, tt, top, bottom=None):
    lo, hi = s * sub - CONV_HALO, (s + 1) * sub + (CONV_HALO if bottom else 0)
    parts = [top()] if lo < 0 else []
    lo = max(lo, 0)
    parts.append(load(lo, min(hi, tt) - lo))
    if hi > tt:
        parts.append(bottom())
    return parts[0] if len(parts) == 1 else jnp.concatenate(parts, axis=0)


def _fold8(x):
    acc = x[0:8]
    for r in range(8, x.shape[0], 8):
        acc = acc + x[r:r + 8]
    return acc


def _conv_glu_fwd_call(u, conv_w, conv_b, name, hosted=None):
    _, t, f = u.shape
    tt = _tile(t, CONV_ROWS)
    sub = _tile(tt, CONV_SUB)
    per = tt // CONV_HALO

    def body(ins, outs, scratch, first, last):
        u_ref, up_ref, w_ref, b_ref = ins
        z_ref = outs[0]
        i = pl.program_id(0)
        first()

        def chunk(j, carry):
            lanes = pl.ds(pl.multiple_of(j * LANES, LANES), LANES)
            w, b = w_ref[:, :, lanes], b_ref[:, lanes]
            for s in range(tt // sub):
                c = []
                for hf in range(2):
                    ext = _piece_rows(lambda at, n: u_ref[hf, pl.ds(at, n), lanes].astype(F32), s, sub, tt,
                                      top=lambda: jnp.where(i > 0, up_ref[hf, :, lanes].astype(F32), 0.0))
                    c.append(_causal_conv(ext, w[:, hf, :], b[hf:hf + 1, :])[CONV_HALO:])
                z_ref[pl.ds(s * sub, sub), lanes] = (c[0] * c[1] * _sigmoid(c[1])).astype(z_ref.dtype)
            return carry

        lax.fori_loop(0, f // LANES, chunk, 0)
        last()

    res = _call(
        body, hosted, name=name, grid=(t // tt,),
        in_specs=[pl.BlockSpec((2, tt, f), lambda i: (0, i, 0)),
                  pl.BlockSpec((2, CONV_HALO, f), lambda i: (0, jnp.maximum(i * per - 1, 0), 0)),
                  pl.BlockSpec((3, 2, f), lambda i: (0, 0, 0)), pl.BlockSpec((2, f), lambda i: (0, 0))],
        out_specs=[pl.BlockSpec((tt, f), lambda i: (i, 0))], out_shape=[jax.ShapeDtypeStruct((t, f), MXU_DTYPE)],
        operands=(u, u, conv_w, conv_b), semantics=("parallel",))
    return res if hosted else res[0]


def _conv_glu_bwd_call(u, dz, conv_w, conv_b, name, hosted=None):
    _, t, f = u.shape
    tt = _tile(t, CONV_ROWS)
    sub = _tile(tt, 2 * CONV_SUB)
    nt = t // tt
    per = tt // CONV_HALO
    halo = CONV_HALO

    def body(ins, outs, scratch, first, last):
        u_ref, up_ref, un_ref, dz_ref, dzn_ref, w_ref, b_ref = ins
        du_ref, dw_ref, db_ref = outs
        dwacc, dbacc = scratch
        i = pl.program_id(0)
        first()

        @pl.when(i == 0)
        def _():
            dwacc[...] = jnp.zeros_like(dwacc)
            dbacc[...] = jnp.zeros_like(dbacc)

        main = slice(halo, halo + sub)

        def chunk(j, carry):
            lanes = pl.ds(pl.multiple_of(j * LANES, LANES), LANES)
            w, b = w_ref[:, :, lanes], b_ref[:, lanes]
            for s in range(tt // sub):
                ue = [_piece_rows(lambda at, n: u_ref[hf, pl.ds(at, n), lanes].astype(F32), s, sub, tt,
                                  top=lambda: jnp.where(i > 0, up_ref[hf, :, lanes].astype(F32), 0.0),
                                  bottom=lambda: un_ref[hf, :, lanes].astype(F32)) for hf in range(2)]
                a, g = [_causal_conv(ue[hf], w[:, hf, :], b[hf:hf + 1, :]) for hf in range(2)]
                sg = _sigmoid(g)
                if (s + 1) * sub < tt:
                    below = dz_ref[pl.ds(s * sub, sub + 2 * halo), lanes].astype(F32)[:sub + halo]
                else:
                    below = jnp.concatenate([dz_ref[pl.ds(s * sub, sub), lanes].astype(F32),
                                             jnp.where(i < nt - 1, dzn_ref[:, lanes].astype(F32), 0.0)], axis=0)
                dzs = jnp.concatenate([jnp.zeros((halo, LANES), F32), below], axis=0)
                gs = g * sg
                dc = [dzs * gs, dzs * a * (sg + gs - gs * sg)]
                for hf in range(2):
                    d = dc[hf]
                    d1, d2 = _shift_up(d, 1), _shift_up(d, 2)
                    du = w[2:3, hf, :] * d + w[1:2, hf, :] * d1 + w[0:1, hf, :] * d2
                    du_ref[hf, pl.ds(s * sub, sub), lanes] = du[main].astype(du_ref.dtype)
                    um = ue[hf][main]
                    for k, dd in enumerate((d2, d1, d)):
                        dwacc[k, hf, :, lanes] += _fold8(dd[main] * um)
                    dbacc[hf, :, lanes] += _fold8(d[main])
            return carry

        lax.fori_loop(0, f // LANES, chunk, 0)

        @pl.when(i == nt - 1)
        def _():
            dw_ref[...] = jnp.sum(dwacc[...], axis=2)
            db_ref[...] = jnp.sum(dbacc[...], axis=1)

        last()

    end = t // halo - 1
    return _call(
        body, hosted, name=name, grid=(nt,),
        in_specs=[pl.BlockSpec((2, tt, f), lambda i: (0, i, 0)),
                  pl.BlockSpec((2, halo, f), lambda i: (0, jnp.maximum(i * per - 1, 0), 0)),
                  pl.BlockSpec((2, halo, f), lambda i: (0, jnp.minimum((i + 1) * per, end), 0)),
                  pl.BlockSpec((tt, f), lambda i: (i, 0)),
                  pl.BlockSpec((halo, f), lambda i: (jnp.minimum((i + 1) * per, end), 0)),
                  pl.BlockSpec((3, 2, f), lambda i: (0, 0, 0)), pl.BlockSpec((2, f), lambda i: (0, 0))],
        out_specs=[pl.BlockSpec((2, tt, f), lambda i: (0, i, 0)),
                   pl.BlockSpec((3, 2, f), lambda i: (0, 0, 0)), pl.BlockSpec((2, f), lambda i: (0, 0))],
        out_shape=[jax.ShapeDtypeStruct(u.shape, MXU_DTYPE), jax.ShapeDtypeStruct((3, 2, f), F32),
                   jax.ShapeDtypeStruct((2, f), F32)],
        operands=(u, u, u, dz, dz, conv_w, conv_b), semantics=("arbitrary",),
        scratch_shapes=[pltpu.VMEM((3, 2, 8, f), F32), pltpu.VMEM((2, 8, f), F32)])


def _kv_even_odd(band, k):
    pair, parity = divmod(k, 2)
    blk = band[:, 128 * pair:128 * (pair + 1)].astype(F32)
    lane = lax.broadcasted_iota(jnp.int32, blk.shape, 1)
    own = jnp.where((lane >= HEAD_DIM) == (parity == 1), blk, 0.0)
    swapped = pltpu.roll(own, HEAD_DIM, axis=1)
    even, odd = (own, swapped) if parity == 0 else (swapped, own)
    return even.astype(MXU_DTYPE), odd.astype(MXU_DTYPE)


def _stack_pairs(ref, k):
    return jnp.concatenate([ref[:, 256 * k:256 * k + 128], ref[:, 256 * k + 128:256 * k + 256]], axis=0)


_GROUP_ORDER = (0, 2, 1, 3)


def _attn_fill_tables(bias_ref):
    shape = (4 * ATT_BLOCK, 2 * ATT_BLOCK)
    row = lax.broadcasted_iota(jnp.int32, shape, 0)
    col = lax.broadcasted_iota(jnp.int32, shape, 1)
    dist = (row & (ATT_BLOCK - 1)) + ATT_BLOCK - col
    ok = (dist >= 0) & (dist < ATT_BLOCK)
    distf = dist.astype(F32)
    rb = lax.broadcasted_iota(jnp.int32, (shape[0], 1), 0) // ATT_BLOCK
    for k in range(N_KV_HEADS):
        slope = jnp.zeros((shape[0], 1), F32)
        for r, gq in enumerate(_GROUP_ORDER):
            slope = jnp.where(rb == r, ALIBI_SLOPES[4 * k + gq], slope)
        bias = jnp.where(ok, -slope * distf, -jnp.inf)
        bias_ref[0, k] = bias
        bias_ref[1, k] = jnp.where(col >= ATT_BLOCK, bias, -jnp.inf)


def _attn_probs(qs, k_even, k_odd, bias, k, sink_ref):
    nt_dims = (((1,), (1,)), ((), ()))
    s = jnp.concatenate([lax.dot_general(qs, k_even, nt_dims, preferred_element_type=F32),
                         lax.dot_general(qs, k_odd, nt_dims, preferred_element_type=F32)], axis=0) + bias
    rb = lax.broadcasted_iota(jnp.int32, (s.shape[0], 1), 0) // ATT_BLOCK
    sink = jnp.zeros((s.shape[0], 1), F32)
    for r, gq in enumerate(_GROUP_ORDER):
        sink = jnp.where(rb == r, sink_ref[4 * k + gq], sink)
    m = jnp.maximum(jnp.max(s, axis=-1, keepdims=True), sink)
    p = jnp.exp(s - m)
    es = jnp.exp(sink - m)
    return p, 1.0 / (jnp.sum(p, axis=-1, keepdims=True) + es), es


_ATTN_TABLES = [pltpu.VMEM((2, N_KV_HEADS, 4 * ATT_BLOCK, 2 * ATT_BLOCK), F32)]


def _attn_specs(with_do):
    prev = lambda n: jnp.maximum(n - 1, 0)
    specs = [pl.BlockSpec(memory_space=pltpu.SMEM),
             pl.BlockSpec((ATT_BLOCK, D_MODEL), lambda n: (n, 0)),
             pl.BlockSpec((ATT_BLOCK, 256), lambda n: (prev(n), 4)), pl.BlockSpec((ATT_BLOCK, 256), lambda n: (n, 4)),
             pl.BlockSpec((ATT_BLOCK, 256), lambda n: (prev(n), 5)), pl.BlockSpec((ATT_BLOCK, 256), lambda n: (n, 5))]
    if with_do:
        specs.append(pl.BlockSpec((ATT_BLOCK, D_MODEL), lambda n: (n, 0)))
    return specs


def _attn_fwd_call(qkv, sinks, hosted=None):
    t = qkv.shape[0]

    def body(ins, outs, scratch, first, last):
        sink_ref, q_ref, kp_ref, kc_ref, vp_ref, vc_ref = ins
        o_ref, bias_ref = outs[0], scratch[0]
        n = pl.program_id(0)
        first()

        @pl.when(n == 0)
        def _():
            _attn_fill_tables(bias_ref)

        first = 1 - jnp.minimum(n, 1)
        kband = jnp.concatenate([kp_ref[...], kc_ref[...]], axis=0)
        vband = jnp.concatenate([vp_ref[...], vc_ref[...]], axis=0)
        for k in range(N_KV_HEADS):
            k_even, k_odd = _kv_even_odd(kband, k)
            v_even, v_odd = _kv_even_odd(vband, k)
            qs = _stack_pairs(q_ref, k) * ATT_SCALE
            p, rdenom, _ = _attn_probs(qs, k_even, k_odd, bias_ref[first, k], k, sink_ref)
            probs = (p * rdenom).astype(MXU_DTYPE)
            o = (jnp.dot(probs[:256], v_even, preferred_element_type=F32)
                 + jnp.dot(probs[256:], v_odd, preferred_element_type=F32))
            o_ref[:, 256 * k:256 * k + 128] = o[:128].astype(o_ref.dtype)
            o_ref[:, 256 * k + 128:256 * k + 256] = o[128:].astype(o_ref.dtype)
        last()

    return _call(
        body, hosted, name="attn_fwd", grid=(t // ATT_BLOCK,), in_specs=_attn_specs(False),
        out_specs=[pl.BlockSpec((ATT_BLOCK, D_MODEL), lambda n: (n, 0))],
        out_shape=[jax.ShapeDtypeStruct((t, D_MODEL), MXU_DTYPE)], operands=(sinks, qkv, qkv, qkv, qkv, qkv),
        semantics=("arbitrary",), scratch_shapes=_ATTN_TABLES)


def _attn_bwd_call(qkv, sinks, do, hosted=None):
    t = qkv.shape[0]
    nb = t // ATT_BLOCK
    tn_dims = (((0,), (0,)), ((), ()))
    nt_dims = (((1,), (1,)), ((), ()))

    def to_native(even_part, odd_part, parity):
        lane = lax.broadcasted_iota(jnp.int32, even_part.shape, 1)
        lo = lane < HEAD_DIM
        e, o = jnp.where(lo, even_part, 0.0), jnp.where(lo, 0.0, odd_part)
        if parity == 0:
            return e + pltpu.roll(o, HEAD_DIM, axis=1)
        return pltpu.roll(e, HEAD_DIM, axis=1) + o

    def body(ins, outs, scratch, first, last):
        sink_ref, q_ref, kp_ref, kc_ref, vp_ref, vc_ref, do_ref = ins
        dq_ref, dk_ref, dv_ref, db_ref, dsink_ref = outs
        bias_ref = scratch[0]
        n = pl.program_id(0)
        first()

        @pl.when(n == 0)
        def _():
            _attn_fill_tables(bias_ref)
            dk_ref[...] = jnp.zeros_like(dk_ref)
            dv_ref[...] = jnp.zeros_like(dv_ref)
            db_ref[...] = jnp.zeros_like(db_ref)
            dsink_ref[...] = jnp.zeros_like(dsink_ref)

        kband = jnp.concatenate([kp_ref[...], kc_ref[...]], axis=0)
        vband = jnp.concatenate([vp_ref[...], vc_ref[...]], axis=0)
        dk_pairs = [jnp.zeros((2 * ATT_BLOCK, 128), F32), jnp.zeros((2 * ATT_BLOCK, 128), F32)]
        dv_pairs = [jnp.zeros((2 * ATT_BLOCK, 128), F32), jnp.zeros((2 * ATT_BLOCK, 128), F32)]
        sink_lane = lax.broadcasted_iota(jnp.int32, (8, 128), 1)
        sink_row = lax.broadcasted_iota(jnp.int32, (8, 128), 0)
        dsink = jnp.zeros((8, 128), F32)
        first = 1 - jnp.minimum(n, 1)
        for k in range(N_KV_HEADS):
            k_even, k_odd = _kv_even_odd(kband, k)
            v_even, v_odd = _kv_even_odd(vband, k)
            qs = _stack_pairs(q_ref, k) * ATT_SCALE
            dos = _stack_pairs(do_ref, k)
            p, rdenom, es = _attn_probs(qs, k_even, k_odd, bias_ref[first, k], k, sink_ref)
            probs = p * rdenom
            dp = jnp.concatenate([lax.dot_general(dos, v_even, nt_dims, preferred_element_type=F32),
                                  lax.dot_general(dos, v_odd, nt_dims, preferred_element_type=F32)], axis=0)
            delta = jnp.sum(probs * dp, axis=-1, keepdims=True)
            ds16 = (probs * (dp - delta)).astype(MXU_DTYPE)
            p16 = probs.astype(MXU_DTYPE)
            dsink_rows = -(es * rdenom) * delta
            for r, gq in enumerate(_GROUP_ORDER):
                tot = jnp.sum(dsink_rows[ATT_BLOCK * r:ATT_BLOCK * (r + 1)], axis=0, keepdims=True)
                dsink = dsink + jnp.where((sink_lane == 4 * k + gq) & (sink_row == 0), tot, 0.0)
            dqs = (jnp.dot(ds16[:256], k_even, preferred_element_type=F32)
                   + jnp.dot(ds16[256:], k_odd, preferred_element_type=F32)) * ATT_SCALE
            dq_ref[:, 256 * k:256 * k + 128] = dqs[:128].astype(dq_ref.dtype)
            dq_ref[:, 256 * k + 128:256 * k + 256] = dqs[128:].astype(dq_ref.dtype)
            db_ref[:, 256 * k:256 * k + 128] += jnp.sum(dqs[:128], axis=0, keepdims=True)
            db_ref[:, 256 * k + 128:256 * k + 256] += jnp.sum(dqs[128:], axis=0, keepdims=True)
            pair, parity = divmod(k, 2)
            dk_pairs[pair] = dk_pairs[pair] + to_native(
                lax.dot_general(ds16[:256], qs, tn_dims, preferred_element_type=F32),
                lax.dot_general(ds16[256:], qs, tn_dims, preferred_element_type=F32), parity)
            dv_pairs[pair] = dv_pairs[pair] + to_native(
                lax.dot_general(p16[:256], dos, tn_dims, preferred_element_type=F32),
                lax.dot_general(p16[256:], dos, tn_dims, preferred_element_type=F32), parity)
        dsink_ref[...] += dsink
        dk_band = jnp.concatenate(dk_pairs, axis=1)
        dv_band = jnp.concatenate(dv_pairs, axis=1)
        cur = pl.multiple_of(n * ATT_BLOCK, ATT_BLOCK)
        dk_ref[pl.ds(cur, ATT_BLOCK), :] += dk_band[ATT_BLOCK:]
        dv_ref[pl.ds(cur, ATT_BLOCK), :] += dv_band[ATT_BLOCK:]

        @pl.when(n > 0)
        def _():
            prv = pl.multiple_of((n - 1) * ATT_BLOCK, ATT_BLOCK)
            dk_ref[pl.ds(prv, ATT_BLOCK), :] += dk_band[:ATT_BLOCK]
            dv_ref[pl.ds(prv, ATT_BLOCK), :] += dv_band[:ATT_BLOCK]

        @pl.when(n == nb - 1)
        def _():
            db_ref[:, 1024:1280] = jnp.sum(dk_ref[...], axis=0, keepdims=True)
            db_ref[:, 1280:1536] = jnp.sum(dv_ref[...], axis=0, keepdims=True)

        last()

    whole = lambda shape: pl.BlockSpec(shape, lambda n: (0, 0))
    return _call(
        body, hosted, name="attn_bwd", grid=(nb,), in_specs=_attn_specs(True),
        out_specs=[pl.BlockSpec((ATT_BLOCK, D_MODEL), lambda n: (n, 0)), whole((t, 256)), whole((t, 256)),
                   whole((1, QKV_DIM)), whole((8, 128))],
        out_shape=[jax.ShapeDtypeStruct((t, D_MODEL), MXU_DTYPE), jax.ShapeDtypeStruct((t, 256), F32),
                   jax.ShapeDtypeStruct((t, 256), F32), jax.ShapeDtypeStruct((1, QKV_DIM), F32),
                   jax.ShapeDtypeStruct((8, 128), F32)],
        operands=(sinks, qkv, qkv, qkv, qkv, qkv, do), semantics=("arbitrary",), scratch_shapes=_ATTN_TABLES)


FF_TILE = 1408


def _gathering_tile(j):
    me = 2 * lax.axis_index("x") + lax.axis_index("y")
    return me ^ (((j & 1) << 1) | (j >> 1))


def _ffn_in_gathering_call(hn, w_in, layer, hosted):
    t = hn.shape[0]
    tm = _tile(t, 1024)
    n_i = t // tm
    halves = D_FF // FF_TILE
    assert FF_TILE == W_IN_COLS and 2 * halves == N_CHIPS

    def body(ins, outs, scratch, first, last):
        hn_ref = ins[0]
        u_ref, w_ref = outs
        wbuf, load_sem, ici_send, ici_recv, hand_send, hand_recv = scratch
        j, i = pl.program_id(0), pl.program_id(1)
        x, y, c, chips = _mesh_place()
        me = 2 * x + y
        mover = c == layer

        def shard(k):
            return w_ref.at[layer, :, _col(k, W_IN_COLS)]

        def over_ici(jj, k):
            cx, cy = chips[jj]
            return pltpu.make_async_remote_copy(src_ref=shard(k), dst_ref=shard(k), send_sem=ici_send.at[jj],
                                                recv_sem=ici_recv.at[jj], device_id=(cx, cy, c), device_id_type=MESH)

        def handed(jj):
            cx, cy = chips[jj]
            return pltpu.make_async_remote_copy(
                src_ref=shard(2 * cx + cy), dst_ref=shard(2 * cx + cy), send_sem=hand_send.at[jj],
                recv_sem=hand_recv.at[jj], device_id=(x, y, 1 - c), device_id_type=MESH)

        far = N_CHIPS - 2

        @pl.when((j == 0) & (i == 0) & mover)
        def _():
            over_ici(far, me).start()
            for jj in NEIGHBOURS:
                handed(jj).start()

        first()

        @pl.when((j == far + 1) & (i == 0) & mover)
        def _():
            over_ici(far, 2 * chips[far][0] + chips[far][1]).wait_recv()
            handed(far).start()

        for jj in range(N_CHIPS - 1):
            @pl.when((j == jj + 1) & (i == 0) & jnp.logical_not(mover))
            def _(jj=jj):
                handed(jj).wait_recv()

        @pl.when(i == 0)
        def _():
            load = pltpu.make_async_copy(shard(_gathering_tile(j)), wbuf, load_sem.at[0])
            load.start()
            load.wait()

        u_ref[...] = jnp.dot(hn_ref[...].astype(MXU_DTYPE), wbuf[...].astype(MXU_DTYPE),
                             preferred_element_type=F32).astype(u_ref.dtype)

        @pl.when((j == N_CHIPS - 1) & (i == n_i - 1) & mover)
        def _():
            over_ici(far, me).wait_send()
            for jj in range(N_CHIPS - 1):
                handed(jj).wait_send()

        last()

    def u_index(j, i):
        tile = _gathering_tile(j)
        return tile // halves, i, tile % halves

    three = [pltpu.SemaphoreType.DMA((N_CHIPS - 1,))] * 4
    return _call(
        body, hosted, name=f"ffn{layer}_in", grid=(N_CHIPS, n_i),
        in_specs=[pl.BlockSpec((tm, D_MODEL), lambda j, i: (i, 0)), ANY],
        out_specs=[pl.BlockSpec((None, tm, FF_TILE), u_index), ANY],
        out_shape=[jax.ShapeDtypeStruct((2, t, D_FF), U_DTYPE), jax.ShapeDtypeStruct(w_in.shape, w_in.dtype)],
        operands=(hn, w_in), semantics=("arbitrary", "arbitrary"), aliases={1: 1},
        scratch_shapes=[pltpu.VMEM((D_MODEL, W_IN_COLS), w_in.dtype), pltpu.SemaphoreType.DMA((1,))] + three)


def _ffn_out_loss_call(z, w_out, layer, h, g, target):
    t, f = z.shape
    tm = _tile(t, ROW_TILE)

    def body(z_ref, w_ref, h_ref, g_ref, t_ref, loss_ref, dh_ref, dh16_ref, dg_ref):
        x = h_ref[...] + jnp.dot(z_ref[...], w_ref[...], preferred_element_type=F32)
        gv = g_ref[...]
        err = _rms(x, gv) - t_ref[...]
        dx, dgt = _rms_bwd(x, gv, err * (1.0 / D_MODEL))
        dh_ref[...] = dx
        dh16_ref[...] = dx.astype(dh16_ref.dtype)

        @pl.when(pl.program_id(0) == 0)
        def _():
            dg_ref[...] = jnp.zeros_like(dg_ref)
            loss_ref[...] = jnp.zeros_like(loss_ref)

        dg_ref[...] += jnp.sum(dgt, axis=0, keepdims=True)
        per_token = jnp.mean(err * err, axis=-1, keepdims=True)
        loss_ref[...] += 0.5 * jnp.sum(per_token, axis=0, keepdims=True)

    row = pl.BlockSpec((tm, D_MODEL), lambda i: (i, 0))
    vec = pl.BlockSpec((1, D_MODEL), lambda i: (0, 0))
    return pl.pallas_call(
        body, name=f"ffn{layer}_out_loss", grid=(t // tm,),
        in_specs=[pl.BlockSpec((tm, f), lambda i: (i, 0)), pl.BlockSpec((None, f, D_MODEL), lambda i: (layer, 0, 0)),
                  row, vec, row],
        out_specs=[pl.BlockSpec((1, 1), lambda i: (0, 0)), row, row, vec],
        out_shape=[jax.ShapeDtypeStruct((1, 1), F32), jax.ShapeDtypeStruct((t, D_MODEL), F32),
                   jax.ShapeDtypeStruct((t, D_MODEL), MXU_DTYPE), jax.ShapeDtypeStruct((1, D_MODEL), F32)],
        compiler_params=_params("arbitrary"))(z, w_out, h, g, target)


def _ffn_fwd(hn, h, w, conv_w, conv_b, layer, hooks=None, next_norm=None, loss_head=None, gathers_w_in=False):
    t = hn.shape[0]
    tm = _tile(t, 1024)
    halves = D_FF // FF_TILE
    if gathers_w_in:
        u, w["w_in"] = _ride(hooks, f"in{layer}", lambda hosted: _ffn_in_gathering_call(hn, w["w_in"], layer, hosted))
    else:
        u, = _ride(hooks, f"in{layer}", lambda hosted: _matmul(
            hn, w["w_in"], mode="nn", name=f"ffn{layer}_in", grid=(2 * halves, t // tm, 1),
            a_spec=pl.BlockSpec((tm, D_MODEL), lambda j, i, k: (i, 0)),
            b_spec=pl.BlockSpec((None, D_MODEL, FF_TILE), lambda j, i, k: (layer, 0, j)),
            o_spec=pl.BlockSpec((None, tm, FF_TILE), lambda j, i, k: (j // halves, i, j % halves)),
            out_shape=jax.ShapeDtypeStruct((2, t, D_FF), U_DTYPE), hosted=hosted))
    z, = _ride(hooks, f"glu{layer}", lambda hosted: _conv_glu_fwd_call(u, conv_w, conv_b, f"ffn{layer}_glu",
                                                                       hosted=hosted))
    if loss_head is not None:
        return (*_ffn_out_loss_call(z, w["w_out"], layer, h, *loss_head), u, z)
    outs = _ride(hooks, f"out{layer}", lambda hosted: _mm_nn(z, w["w_out"], f"ffn{layer}_out", b_lead=layer, residual=h,
                                                     tk=D_FF, hosted=hosted, norm_g=next_norm))
    return outs[0], (outs[1] if next_norm is not None else None), u, z


def _ffn_bwd(dh, hn, u, z, w_in, w_out, conv_w, conv_b, layer, g_in_acc, g_out_acc, norm, hooks=None):
    t = hn.shape[0]
    dz, = _ride(hooks, "dz", lambda hosted: _mm_nt(dh, w_out, f"ffn{layer}_dz", b_lead=layer, out_dtype=MXU_DTYPE,
                                                   tn=FF_TILE, hosted=hosted))
    g_out = _mm_tn(z, dh, f"ffn{layer}_dwout", lead=layer, n_lead=2, into=g_out_acc, tm=WGRAD_TILE, tk=t,
                   out_dtype=WIRE_DTYPE)
    du, dcw, dcb = _ride(hooks, "dglu", lambda hosted: _conv_glu_bwd_call(u, dz, conv_w, conv_b, f"ffn{layer}_dglu",
                                                                          hosted=hosted), {"g_out": g_out})
    per_half = D_FF // WGRAD_TILE
    g_in, = _ride(hooks, "dwin", lambda hosted: _matmul(
        hn, du, mode="tn", name=f"ffn{layer}_dwin", grid=(1, 2 * per_half, 1),
        a_spec=pl.BlockSpec((t, D_MODEL), lambda i, j, k: (0, 0)),
        b_spec=pl.BlockSpec((None, t, WGRAD_TILE), lambda i, j, k: (j // per_half, 0, j % per_half)),
        o_spec=pl.BlockSpec((None, D_MODEL, WGRAD_TILE), lambda i, j, k: (layer, 0, j)),
        out_shape=jax.ShapeDtypeStruct((2, D_MODEL, 2 * D_FF), WIRE_DTYPE), into=g_in_acc, hosted=hosted))
    dx, dx16, dg = _ride(hooks, "dhn", lambda hosted: _ffn_dhn_call(du, w_in, layer, norm, hosted),
                         {"g_in": g_in, "g_out": g_out, "dcw": dcw, "dcb": dcb})
    return dx, dx16, dg, g_in, g_out, dcw, dcb


WGRAD_TILE = 256


_NT_DIMS = (((1,), (1,)), ((), ()))


def _dhn_call(name, a, a_spec, w, w_spec, product, t, tm, norm, hosted=None):
    h, g, dres = norm

    def body(ins, outs, scratch, first, last):
        a_ref, w_ref, h_ref, g_ref, dres_ref = ins
        dx_ref, dx16_ref, dg_ref = outs
        first()
        dx, dgt = _rms_bwd(h_ref[...], g_ref[...], product(a_ref, w_ref))
        dx = dres_ref[...] + dx
        dx_ref[...] = dx
        dx16_ref[...] = dx.astype(dx16_ref.dtype)

        @pl.when(pl.program_id(0) == 0)
        def _():
            dg_ref[...] = jnp.zeros_like(dg_ref)

        dg_ref[...] += jnp.sum(dgt, axis=0, keepdims=True)
        last()

    row = pl.BlockSpec((tm, D_MODEL), lambda i: (i, 0))
    vec = pl.BlockSpec((1, D_MODEL), lambda i: (0, 0))
    return _call(
        body, hosted, name=name, grid=(t // tm,), in_specs=[a_spec, w_spec, row, vec, row], out_specs=[row, row, vec],
        out_shape=[jax.ShapeDtypeStruct((t, D_MODEL), F32), jax.ShapeDtypeStruct((t, D_MODEL), MXU_DTYPE),
                   jax.ShapeDtypeStruct((1, D_MODEL), F32)],
        operands=(a, w, h, g, dres), semantics=("arbitrary",))


def _ffn_dhn_call(du, w_in, layer, norm, hosted=None):
    _, t, f = du.shape
    tm = _tile(t, 512)

    def product(du_ref, w_ref):
        return (lax.dot_general(du_ref[0], w_ref[:, :f], _NT_DIMS, preferred_element_type=F32)
                + lax.dot_general(du_ref[1], w_ref[:, f:], _NT_DIMS, preferred_element_type=F32))

    return _dhn_call(f"ffn{layer}_dhn", du, pl.BlockSpec((2, tm, f), lambda i: (0, i, 0)), w_in,
                     pl.BlockSpec((None, D_MODEL, 2 * f), lambda i: (layer, 0, 0)), product, t, tm, norm, hosted)


def _attn_dhn_call(dqkv, w_qkv, norm):
    t, n = dqkv.shape
    tm = _tile(t, 1024)

    def product(a_ref, w_ref):
        return lax.dot_general(a_ref[...], w_ref[...], _NT_DIMS, preferred_element_type=F32)

    return _dhn_call("attn_dhn", dqkv, pl.BlockSpec((tm, n), lambda i: (i, 0)), w_qkv,
                     pl.BlockSpec((D_MODEL, n), lambda i: (0, 0)), product, t, tm, norm)


def _local_step(x, target, w, *, norm_mix, norm_ffn, norm_f, pool_b, pool_scale, b_qkv, sinks, conv_w, conv_b,
                late_gather=None, reducer=None):
    cw = [conv_w[l].reshape(3, 2, D_FF) for l in range(2)]
    cb = [conv_b[l].reshape(2, D_FF) for l in range(2)]
    gm = [norm_mix[l:l + 1] for l in range(2)]
    gf = [norm_ffn[l:l + 1] for l in range(2)]

    w = dict(w)
    fwd_hooks = {}
    if late_gather:
        def gather_hook(stage):
            return (lambda env: late_gather(stage, w)), (lambda got: w.update(zip(_late_names(stage), got)))

        fwd_hooks = {stage: gather_hook(stage) for stage in GATHER_RIDES}

    h1, hn_f0 = _ride(fwd_hooks, "pool_fwd", lambda hosted: _pool_fwd_call(x, gm[0], w["pool_w"], pool_b, pool_scale,
                                                                            gf[0], hosted))
    h2, hn_a, u0, z0 = _ffn_fwd(hn_f0, h1, w, cw[0], cb[0], 0, hooks=fwd_hooks, next_norm=gm[1],
                                gathers_w_in=bool(late_gather))
    qkv = _mm_nn(hn_a, w["w_qkv"], "attn_qkv", bias=b_qkv, out_dtype=MXU_DTYPE, tn=768)
    o, = _ride(fwd_hooks, "attn_fwd", lambda hosted: _attn_fwd_call(qkv, sinks, hosted))
    h3, hn_f1 = _ride(fwd_hooks, "attn_out", lambda hosted: _mm_nn(o, w["w_o"], "attn_out", residual=h2, hosted=hosted,
                                                                   norm_g=gf[1]))
    loss, dh, dh16, d_norm_f, u1, z1 = _ffn_fwd(hn_f1, h3, w, cw[1], cb[1], 1, hooks=fwd_hooks,
                                                loss_head=(norm_f, target))

    rd = reducer
    first, mid, last = _Reducer.FIRST, _Reducer.MID, _Reducer.LAST
    hooks1 = hooks_attn = hooks0 = hooks_pool = None
    if rd:
        hooks1 = {"dhn": (lambda env: rd.swap(first, {"w_in": env["g_in"], "w_out": env["g_out"]}),
                          lambda got: rd.pair_sums(first, got))}
        hooks_attn = {"attn_bwd": (lambda env: rd.scatter(first), lambda got: rd.chip_sums(first, got))}

        def mid_and_last(got):
            rd.shared(mid, got[:len(mid)])
            rd.pair_sums(last, got[len(mid):len(mid) + len(last)])
            rd.small_early_done(got[-1])

        def early_pieces(env):
            return {"b_qkv": d_bqkv, "sinks": d_sinks[0:1], "norm_mix1": d_gm1, "norm_ffn1": d_gf1,
                    "conv_w": jnp.stack([env["dcw"].reshape(3, 2 * D_FF), dcw1.reshape(3, 2 * D_FF)]),
                    "conv_b": jnp.stack([env["dcb"].reshape(2 * D_FF), dcb1.reshape(2 * D_FF)]),
                    "norm_f": d_norm_f, "loss": loss}

        hooks0 = {
            "dz": (lambda env: rd.share(first), lambda got: rd.shared(first, got)),
            "dglu": (lambda env: rd.swap(mid, {"w_out": env["g_out"], "w_qkv": g_qkv, "w_o": g_o4}),
                     lambda got: rd.pair_sums(mid, got)),
            "dwin": (lambda env: rd.scatter(mid), lambda got: rd.chip_sums(mid, got)),
            "dhn": (lambda env: _both(_both(rd.share(mid), rd.swap(last, {"w_in": env["g_in"]})),
                                      rd.small_early(early_pieces(env))), mid_and_last),
        }
        hooks_pool = {"pool_bwd": (lambda env: rd.scatter(last), lambda got: rd.chip_sums(last, got))}
    dh, dh16, d_gf1, g_in, g_out, dcw1, dcb1 = _ffn_bwd(
        dh16, hn_f1, u1, z1, w["w_in"], w["w_out"], cw[1], cb[1], 1, None, None, (h3, gf[1], dh), hooks=hooks1)
    do = _mm_nt(dh16, w["w_o"], "attn_do", out_dtype=MXU_DTYPE)
    g_o = _mm_tn(o, dh16, "attn_dwo", tn=WGRAD_TILE, tk=x.shape[0], out_dtype=WIRE_DTYPE)
    g_o4 = g_o.reshape(N_CHIPS, 2, W_O_ROWS // 2, D_MODEL)
    dq, dk, dv, d_bqkv, d_sinks = _ride(hooks_attn, "attn_bwd", lambda hosted: _attn_bwd_call(qkv, sinks, do, hosted))
    dqkv = jnp.concatenate([dq, dk.astype(MXU_DTYPE), dv.astype(MXU_DTYPE)], axis=1)
    g_qkv = _mm_tn(hn_a, dqkv, "attn_dwqkv", tn=WGRAD_TILE, tk=x.shape[0], out_dtype=WIRE_DTYPE)
    dh, dh16, d_gm1 = _attn_dhn_call(dqkv, w["w_qkv"], (h2, gm[1], dh))
    dh, _, d_gf0, g_in, g_out, dcw0, dcb0 = _ffn_bwd(
        dh16, hn_f0, u0, z0, w["w_in"], w["w_out"], cw[0], cb[0], 0, g_in, g_out, (h1, gf[0], dh), hooks=hooks0)
    grad_x, g_pool, dvec = _ride(hooks_pool, "pool_bwd", lambda hosted: _pool_bwd_call(
        x, dh, gm[0], w["pool_w"], pool_b, pool_scale, hosted))
    if rd:
        rd.shared(last, _alone(rd.share(last), "share_last"))

    grads = {
        "w_in": g_in, "w_out": g_out, "w_qkv": g_qkv, "w_o": g_o, "pool_w": g_pool,
        "pool_b": dvec[0:1], "pool_scale": dvec[1:2], "b_qkv": d_bqkv, "sinks": d_sinks[0:1, :N_HEADS],
        "norm_mix": jnp.concatenate([dvec[2:3], d_gm1], axis=0), "norm_ffn": jnp.concatenate([d_gf0, d_gf1], axis=0),
        "conv_w": jnp.stack([dcw0.reshape(3, 2 * D_FF), dcw1.reshape(3, 2 * D_FF)]),
        "conv_b": jnp.stack([dcb0.reshape(2 * D_FF), dcb1.reshape(2 * D_FF)]), "norm_f": d_norm_f,
    }
    return loss, grad_x, grads


W_IN_COLS = 2 * D_FF // N_CHIPS
W_OUT_ROWS = D_FF // N_CHIPS
QKV_COLS = QKV_DIM // N_CHIPS
W_O_ROWS = D_MODEL // N_CHIPS
POOL_ROWS = POOL_GC // N_CHIPS
BIG = ("w_in", "w_out", "w_qkv", "w_o", "pool_w")


def _mesh_place():
    x, y, c = lax.axis_index("x"), lax.axis_index("y"), lax.axis_index("c")
    chips = [(1 - x, y), (x, 1 - y), (1 - x, 1 - y)]
    return x, y, c, chips


def _col(k, width):
    return pl.ds(pl.multiple_of(k * width, 128), width)


def _row(k, height):
    return pl.ds(pl.multiple_of(k * height, 16), height)


def _full_piece(name, ref, k, h):
    if name == "w_in":
        return ref.at[h, :, _col(k, W_IN_COLS)]
    if name == "w_out":
        return ref.at[h, _row(k, W_OUT_ROWS), :]
    if name == "w_qkv":
        return ref.at[_row(h, D_MODEL // 2), _col(k, QKV_COLS)]
    if name == "w_o":
        return ref.at[_row(2 * k + h, W_O_ROWS // 2), :]
    return ref.at[pl.ds(2 * h, 2), _row(k, POOL_ROWS), :]


def _shard_half(name, ref, h):
    if name in ("w_in", "w_out"):
        return ref.at[h]
    if name == "w_qkv":
        return ref.at[_row(h, D_MODEL // 2), :]
    if name == "w_o":
        return ref.at[_row(h, W_O_ROWS // 2), :]
    return ref.at[pl.ds(2 * h, 2)]


FULL_SHAPES = {"w_in": (2, D_MODEL, 2 * D_FF), "w_out": (2, D_FF, D_MODEL), "w_qkv": (D_MODEL, QKV_DIM),
               "w_o": (D_MODEL, D_MODEL), "pool_w": (4, POOL_GC, POOL_GC)}
SHARD_SHAPES = {"w_in": (2, D_MODEL, W_IN_COLS), "w_out": (2, W_OUT_ROWS, D_MODEL), "w_qkv": (D_MODEL, QKV_COLS),
                "w_o": (W_O_ROWS, D_MODEL), "pool_w": (4, POOL_ROWS, POOL_GC)}
HALF_SHAPES = {"w_in": (D_MODEL, W_IN_COLS), "w_out": (W_OUT_ROWS, D_MODEL), "w_qkv": (D_MODEL // 2, QKV_COLS),
               "w_o": (W_O_ROWS // 2, D_MODEL), "pool_w": (2, POOL_ROWS, POOL_GC)}


def _place_shard(name, shard, me_arr):
    if name == "w_in":
        blk, grid = (1, 256, W_IN_COLS), (2, D_MODEL // 256)
        src, dst = (lambda a, b, me: (a, b, 0)), (lambda a, b, me: (a, b, me[0]))
    elif name == "w_out":
        blk, grid = (1, W_OUT_ROWS, D_MODEL), (2, 1)
        src, dst = (lambda a, b, me: (a, 0, 0)), (lambda a, b, me: (a, me[0], 0))
    elif name == "w_qkv":
        blk, grid = (256, QKV_COLS), (1, D_MODEL // 256)
        src, dst = (lambda a, b, me: (b, 0)), (lambda a, b, me: (b, me[0]))
    elif name == "w_o":
        blk, grid = (W_O_ROWS, D_MODEL), (1, 1)
        src, dst = (lambda a, b, me: (0, 0)), (lambda a, b, me: (me[0], 0))
    else:
        blk, grid = (4, POOL_ROWS, POOL_GC), (1, 1)
        src, dst = (lambda a, b, me: (0, 0, 0)), (lambda a, b, me: (0, me[0], 0))

    def body(me_ref, s_ref, o_ref):
        o_ref[...] = s_ref[...].astype(o_ref.dtype)

    return pl.pallas_call(
        body, name=f"place_{name}",
        grid_spec=pltpu.PrefetchScalarGridSpec(num_scalar_prefetch=1, grid=grid, in_specs=[pl.BlockSpec(blk, src)],
                                               out_specs=pl.BlockSpec(blk, dst)),
        out_shape=jax.ShapeDtypeStruct(FULL_SHAPES[name], WIRE_DTYPE), compiler_params=_params("parallel", "parallel"),
    )(me_arr, shard)


def _when_mover(half, c, fn):
    if half is None:
        fn(c)
    else:
        pl.when(c == half)(lambda: fn(half))


def _when_taker(half, c, fn):
    if half is None:
        fn(1 - c)
    else:
        pl.when(c != half)(lambda: fn(half))


def _remote(ref, sems, t, j, dev):
    send, recv = sems
    return pltpu.make_async_remote_copy(src_ref=ref, dst_ref=ref, send_sem=send.at[t, j], recv_sem=recv.at[t, j],
                                        device_id=dev, device_id_type=MESH)


def _gather_ici(items, refs, sems, finish, peers=(0, 1, 2)):
    x, y, c, chips = _mesh_place()
    me = 2 * x + y
    for t, (nm, half) in enumerate(items):
        def go(h, t=t, nm=nm):
            mine = _full_piece(nm, refs[t], me, h)
            for j, (cx, cy) in enumerate(chips):
                if j not in peers:
                    continue
                if not finish:
                    _remote(mine, sems, t, j, (cx, cy, c)).start()
                else:
                    _remote(_full_piece(nm, refs[t], 2 * cx + cy, h), sems, t, j, (cx, cy, c)).wait_recv()
                    _remote(mine, sems, t, j, (cx, cy, c)).wait_send()

        _when_mover(half, c, go)


def _gather_pass(items, refs, sems, finish):
    x, y, c, chips = _mesh_place()
    sibling = (x, y, 1 - c)
    for t, (nm, half) in enumerate(items):
        def give(h, t=t, nm=nm):
            for j, (cx, cy) in enumerate(chips):
                got = _remote(_full_piece(nm, refs[t], 2 * cx + cy, h), sems, t, j, sibling)
                if not finish:
                    got.start()
                else:
                    got.wait_send()

        def take(h, t=t, nm=nm):
            for j, (cx, cy) in enumerate(chips):
                _remote(_full_piece(nm, refs[t], 2 * cx + cy, h), sems, t, j, sibling).wait_recv()

        _when_mover(half, c, give)
        if finish:
            _when_taker(half, c, take)


def _gather_sems(n_items):
    return [pltpu.SemaphoreType.DMA((n_items, 3)), pltpu.SemaphoreType.DMA((n_items, 3))]


EARLY_ITEMS = (("pool_w", None),)
_LATE_A = (("w_out", 0),)
_LATE_B = (("w_in", 1),)
_LATE_C = (("w_out", 1),)
_LATE_D = (("w_qkv", None), ("w_o", None))
_EARLY_W_IN = (("w_in", 0),)
GATHER_RIDES = {"pool_fwd": ((), _EARLY_W_IN, ()), "in0": ((), _LATE_A, ()), "glu0": (_LATE_A, _LATE_D, ()),
                "out0": (_LATE_D, (), ()), "attn_fwd": ((), _LATE_B, ()), "attn_out": (_LATE_B, (), ()),
                "in1": ((), _LATE_C, ()), "glu1": (_LATE_C, (), ())}
NEIGHBOURS = (0, 1)
GATHER_PEERS = {"pool_fwd": NEIGHBOURS}


def _names(items):
    return [nm for nm, _ in items]


def _gather_early(full, small):
    items = EARLY_ITEMS
    n = len(items)

    def body(*refs):
        small_in = refs[n]
        outs, small_out = refs[n + 1:2 * n + 1], refs[2 * n + 1]
        scratch = refs[2 * n + 2:]
        ici_sems, pass_sems = scratch[0:2], scratch[2:4]
        local_sem, small_send, small_recv = scratch[4:]
        x, y, c, chips = _mesh_place()
        me = 2 * x + y
        local = pltpu.make_async_copy(small_in, small_out.at[me], local_sem.at[0])
        local.start()
        small_sends = [pltpu.make_async_remote_copy(
            src_ref=small_in, dst_ref=small_out.at[me], send_sem=small_send.at[j], recv_sem=small_recv.at[j],
            device_id=(cx, cy, c), device_id_type=MESH) for j, (cx, cy) in enumerate(chips)]
        for cp in small_sends:
            cp.start()
        _gather_ici(items, outs, ici_sems, finish=False)
        _gather_ici(items, outs, ici_sems, finish=True)
        _gather_pass(items, outs, pass_sems, finish=False)
        _gather_pass(items, outs, pass_sems, finish=True)
        for j, (cx, cy) in enumerate(chips):
            got = small_out.at[2 * cx + cy]
            pltpu.make_async_remote_copy(src_ref=got, dst_ref=got, send_sem=small_send.at[j],
                                         recv_sem=small_recv.at[j], device_id=(cx, cy, c), device_id_type=MESH).wait_recv()
        for cp in small_sends:
            cp.wait_send()
        local.wait()

    arrays = [full[nm] for nm in _names(items)]
    out_shape = [jax.ShapeDtypeStruct(a.shape, a.dtype) for a in arrays]
    out_shape.append(jax.ShapeDtypeStruct((N_CHIPS,) + small.shape, F32))
    res = pl.pallas_call(
        body, name="gather_early", in_specs=[ANY] * (n + 1), out_specs=[ANY] * (n + 1), out_shape=out_shape,
        input_output_aliases={t: t for t in range(n)},
        scratch_shapes=_gather_sems(n) + _gather_sems(n) + [pltpu.SemaphoreType.DMA((1,)), pltpu.SemaphoreType.DMA((3,)),
                                                            pltpu.SemaphoreType.DMA((3,))],
    )(*arrays, small)
    return {**full, **dict(zip(_names(items), res[:n]))}, res[n]


def _late_gather_host(stage, w):
    passing, sending, direct = GATHER_RIDES[stage]
    n_pass, n_send = len(passing), len(sending)

    def run(src, refs, new, sems, finish):
        if passing:
            _gather_pass(passing, refs[:n_pass], sems[:2], finish)
        if sending:
            _gather_ici(sending, refs[n_pass:n_pass + n_send], sems[2 * bool(passing):2 * bool(passing) + 2], finish,
                        GATHER_PEERS.get(stage, (0, 1, 2)))
        if direct:
            _gather_direct(direct, refs[n_pass + n_send:], sems[-2:], finish)

    sems = ((_gather_sems(n_pass) if passing else []) + (_gather_sems(n_send) if sending else [])
            + ([pltpu.SemaphoreType.DMA((len(direct), 6))] * 2 if direct else []))
    return _Hosted([], [w[nm] for nm in _late_names(stage)], [], sems, functools.partial(run, finish=False),
                   functools.partial(run, finish=True))


def _late_names(stage):
    return [nm for items in GATHER_RIDES[stage] for nm in _names(items)]


def _gather_direct(items, refs, sems, finish):
    x, y, c, chips = _mesh_place()
    me = 2 * x + y
    for t, (nm, half) in enumerate(items):
        assert half is None
        mine = _full_piece(nm, refs[t], me, c)
        for j, (cx, cy) in enumerate(chips):
            for other in range(2):
                pc = c if other == 0 else 1 - c
                if not finish:
                    _remote(mine, sems, t, 2 * j + other, (cx, cy, pc)).start()
                else:
                    _remote(_full_piece(nm, refs[t], 2 * cx + cy, pc), sems, t, 2 * j + other, (cx, cy, pc)).wait_recv()
                    _remote(mine, sems, t, 2 * j + other, (cx, cy, pc)).wait_send()


def _other_half_src(name, ref, h):
    if name in ("w_in", "w_out"):
        return ref.at[h]
    if name == "w_qkv":
        return ref.at[_row(h, D_MODEL // 2), :]
    if name == "w_o":
        return ref.at[:, pl.ds(h, 1)]
    return ref.at[pl.ds(2 * h, 2)]


SLAB_SHAPES = {"w_in": (D_MODEL, 2 * D_FF), "w_out": (D_FF, D_MODEL), "w_qkv": (D_MODEL // 2, QKV_DIM),
               "w_o": (N_CHIPS, 1, W_O_ROWS // 2, D_MODEL), "pool_w": (2, POOL_GC, POOL_GC)}


def _item_tag(item):
    return item[0] if item[1] is None else f"{item[0]}{item[1]}"


def _reduce_swap(items, grads, slabs, sems, finish):
    x, y, c, _ = _mesh_place()
    send, recv = sems
    for t, (nm, half) in enumerate(items):
        def copy(h, t=t, nm=nm):
            return pltpu.make_async_remote_copy(
                src_ref=_other_half_src(nm, grads[t], h), dst_ref=slabs[t], send_sem=send.at[t], recv_sem=recv.at[t],
                device_id=(x, y, 1 - c), device_id_type=MESH)

        def give(h):
            if finish:
                copy(h).wait_send()
            else:
                copy(h).start()

        if half is None:
            give(1 - c)
            if finish:
                copy(c).wait_recv()
        else:
            pl.when(c != half)(lambda: give(half))
            if finish:
                pl.when(c == half)(lambda: copy(half).wait_recv())


def _reduce_scatter(items, sums, parts, sems, finish):
    _, _, c, chips = _mesh_place()
    send, recv = sems
    for t, (nm, half) in enumerate(items):
        def go(h, t=t, nm=nm):
            for j, (cx, cy) in enumerate(chips):
                cp = pltpu.make_async_remote_copy(
                    src_ref=_slab_piece(nm, sums[t], 2 * cx + cy), dst_ref=parts[t].at[j], send_sem=send.at[t, j],
                    recv_sem=recv.at[t, j], device_id=(cx, cy, c), device_id_type=MESH)
                if finish:
                    cp.wait_recv()
                    cp.wait_send()
                else:
                    cp.start()

        _when_mover(half, c, go)


def _reduce_share(items, shards, sems, finish):
    x, y, c, _ = _mesh_place()
    send, recv = sems
    for t, (nm, half) in enumerate(items):
        def copy(h, t=t, nm=nm):
            part = _shard_half(nm, shards[t], h)
            return pltpu.make_async_remote_copy(src_ref=part, dst_ref=part, send_sem=send.at[t], recv_sem=recv.at[t],
                                                device_id=(x, y, 1 - c), device_id_type=MESH)

        def give(h):
            if finish:
                copy(h).wait_send()
            else:
                copy(h).start()

        _when_mover(half, c, give)
        if finish:
            _when_taker(half, c, lambda h: copy(h).wait_recv())


def _pair_sems(n):
    return [pltpu.SemaphoreType.DMA((n,)), pltpu.SemaphoreType.DMA((n,))]


def _swap_host(items, grads):
    def run(src, arr, new, sems, finish):
        _reduce_swap(items, src, new, sems, finish)

    return _Hosted([grads[nm] for nm, _ in items], [],
                   [jax.ShapeDtypeStruct(SLAB_SHAPES[nm], grads[nm].dtype) for nm, _ in items],
                   _pair_sems(len(items)), functools.partial(run, finish=False), functools.partial(run, finish=True))


def _scatter_host(items, sums):
    def run(src, arr, new, sems, finish):
        _reduce_scatter(items, src, new, sems, finish)

    new = [jax.ShapeDtypeStruct((N_CHIPS - 1,) + HALF_SHAPES[nm], WIRE_DTYPE) for nm, _ in items]
    return _Hosted(list(sums), [], new, _gather_sems(len(items)), functools.partial(run, finish=False),
                   functools.partial(run, finish=True))


def _share_host(items, shards):
    def run(src, arr, new, sems, finish):
        _reduce_share(items, arr, sems, finish)

    return _Hosted([], list(shards), [], _pair_sems(len(items)), functools.partial(run, finish=False),
                   functools.partial(run, finish=True))


def _alone(hosted, name):
    def body(ins, outs, scratch, first, last):
        first()
        last()

    return _call(body, hosted, name=name, grid=(1,), in_specs=[], out_specs=[], out_shape=[], operands=(),
                 semantics=("arbitrary",))


def _pair_sum(item, grad, slab, place):
    name, half = item
    shape = SLAB_SHAPES[name]
    hsel = (lambda p: p[1]) if half is None else (lambda p: half)
    if name == "w_in":
        blk = (256, 2 * D_FF)
        grid = (D_MODEL // 256,)
        g_spec = pl.BlockSpec((None,) + blk, lambda i, p: (hsel(p), i, 0))
        s_spec = pl.BlockSpec(blk, lambda i, p: (i, 0))
    elif name == "w_out":
        blk = (W_OUT_ROWS, D_MODEL)
        grid = (N_CHIPS,)
        g_spec = pl.BlockSpec((None,) + blk, lambda i, p: (hsel(p), i, 0))
        s_spec = pl.BlockSpec(blk, lambda i, p: (i, 0))
    elif name == "w_qkv":
        grid = (1,)
        g_spec = pl.BlockSpec(shape, lambda i, p: (hsel(p), 0))
        s_spec = pl.BlockSpec(shape, lambda i, p: (0, 0))
    else:
        grid = (1,)
        g_spec = pl.BlockSpec(shape, lambda i, p: (0, hsel(p), 0, 0))
        s_spec = pl.BlockSpec(shape, lambda i, p: (0, 0, 0, 0))

    def body(p_ref, g_ref, s_ref, o_ref):
        @pl.when(hsel(p_ref) == p_ref[1])
        def _():
            o_ref[...] = (g_ref[...].astype(F32) + s_ref[...].astype(F32)).astype(o_ref.dtype)

    return pl.pallas_call(
        body, name=f"pair_sum_{_item_tag(item)}",
        grid_spec=pltpu.PrefetchScalarGridSpec(num_scalar_prefetch=1, grid=grid, in_specs=[g_spec, s_spec],
                                               out_specs=s_spec),
        out_shape=jax.ShapeDtypeStruct(shape, WIRE_DTYPE), compiler_params=_params("parallel"),
    )(place, grad, slab)


def _slab_piece(name, ref, k):
    if name == "w_in":
        return ref.at[:, _col(k, W_IN_COLS)]
    if name == "w_out":
        return ref.at[_row(k, W_OUT_ROWS), :]
    if name == "w_qkv":
        return ref.at[:, _col(k, QKV_COLS)]
    if name == "w_o":
        return ref.at[k, 0]
    return ref.at[:, _row(k, POOL_ROWS), :]


def _sum_chips(item, slab, parts, place, into=None):
    name, sel = item
    half = HALF_SHAPES[name]
    hsel = (lambda p: p[1]) if sel is None else (lambda p: sel)
    if name == "w_in":
        blk, grid = (256, W_IN_COLS), (D_MODEL // 256,)
        own = pl.BlockSpec(blk, lambda i, p: (i, p[0]))
        got = pl.BlockSpec((N_CHIPS - 1,) + blk, lambda i, p: (0, i, 0))
        out = pl.BlockSpec((None,) + blk, lambda i, p: (hsel(p), i, 0))
    elif name == "w_out":
        grid = (1,)
        own = pl.BlockSpec(half, lambda i, p: (p[0], 0))
        got = pl.BlockSpec((N_CHIPS - 1,) + half, lambda i, p: (0, 0, 0))
        out = pl.BlockSpec((None,) + half, lambda i, p: (hsel(p), 0, 0))
    elif name == "w_qkv":
        grid = (1,)
        own = pl.BlockSpec(half, lambda i, p: (0, p[0]))
        got = pl.BlockSpec((N_CHIPS - 1,) + half, lambda i, p: (0, 0, 0))
        out = pl.BlockSpec(half, lambda i, p: (hsel(p), 0))
    else:
        grid = (1,)
        own = pl.BlockSpec((None, None) + half, lambda i, p: (p[0], 0, 0, 0))
        got = pl.BlockSpec((N_CHIPS - 1,) + half, lambda i, p: (0, 0, 0))
        out = pl.BlockSpec(half, lambda i, p: (hsel(p), 0))

    def body(p_ref, own_ref, got_ref, *rest):
        o_ref = rest[-1]

        @pl.when(hsel(p_ref) == p_ref[1])
        def _():
            acc = own_ref[...].astype(F32)
            for j in range(N_CHIPS - 1):
                acc = acc + got_ref[j].astype(F32)
            o_ref[...] = acc

    in_specs, operands, aliases = [own, got], [place, slab, parts], {}
    if into is not None:
        in_specs.append(ANY)
        operands.append(into)
        aliases = {3: 0}
    return pl.pallas_call(
        body, name=f"sum_chips_{_item_tag(item)}",
        grid_spec=pltpu.PrefetchScalarGridSpec(num_scalar_prefetch=1, grid=grid, in_specs=in_specs, out_specs=out),
        out_shape=jax.ShapeDtypeStruct(SHARD_SHAPES[name], F32), input_output_aliases=aliases,
        compiler_params=_params("parallel"),
    )(*operands)


N_DEV = 8


def _peer(d):
    x, y, c, _ = _mesh_place()
    px, py, pc = x ^ (d >> 2), y ^ ((d >> 1) & 1), c ^ (d & 1)
    return (px, py, pc), 4 * px + 2 * py + pc


def _small_all_reduce_host(vec):
    rows, cols = vec.shape

    def run(src, arr, new, scratch, finish):
        slots, stage, send, recv, local = scratch
        x, y, c, _ = _mesh_place()
        me = 4 * x + 2 * y + c
        mine = pltpu.make_async_copy(src[0], slots.at[me], local.at[0])
        copies = []
        for d in range(1, N_DEV):
            dev, _ = _peer(d)
            copies.append(pltpu.make_async_remote_copy(src_ref=src[0], dst_ref=slots.at[me], send_sem=send.at[d - 1],
                                                       recv_sem=recv.at[d - 1], device_id=dev, device_id_type=MESH))
        if not finish:
            mine.start()
            for cp in copies:
                cp.start()
            return
        for d in range(1, N_DEV):
            _, flat = _peer(d)
            got = slots.at[flat]
            pltpu.make_async_remote_copy(src_ref=got, dst_ref=got, send_sem=send.at[d - 1], recv_sem=recv.at[d - 1],
                                         device_id=(x, y, c), device_id_type=MESH).wait_recv()
        for cp in copies:
            cp.wait_send()
        mine.wait()
        acc = slots[0]
        for d in range(1, N_DEV):
            acc = acc + slots[d]
        stage[...] = acc
        out = pltpu.make_async_copy(stage, new[0], local.at[0])
        out.start()
        out.wait()

    scratch = [pltpu.VMEM((N_DEV, rows, cols), F32), pltpu.VMEM((rows, cols), F32),
               pltpu.SemaphoreType.DMA((N_DEV - 1,)), pltpu.SemaphoreType.DMA((N_DEV - 1,)),
               pltpu.SemaphoreType.DMA((1,))]
    return _Hosted([vec], [], [jax.ShapeDtypeStruct(vec.shape, F32)], scratch, functools.partial(run, finish=False),
                   functools.partial(run, finish=True))


class _Reducer:
    FIRST = (("w_in", 1), ("w_out", 1))
    MID = (("w_out", 0), ("w_qkv", None), ("w_o", None))
    LAST = (("w_in", 0),)

    def __init__(self, place):
        self.place = place
        self.shards = {}
        self.state = {}

    def swap(self, items, grads):
        self.state[items] = {"grads": [grads[nm] for nm, _ in items]}
        return _swap_host(items, grads)

    def pair_sums(self, items, slabs):
        st = self.state[items]
        st["sums"] = [_pair_sum(it, g, s, self.place) for it, g, s in zip(items, st["grads"], slabs)]

    def scatter(self, items):
        return _scatter_host(items, self.state[items]["sums"])

    def chip_sums(self, items, parts):
        for it, s, p in zip(items, self.state[items]["sums"], parts):
            self.shards[it[0]] = _sum_chips(it, s, p, self.place, into=self.shards.get(it[0]))

    def share(self, items):
        return _share_host(items, [self.shards[nm] for nm, _ in items])

    def shared(self, items, updated):
        self.shards.update(dict(zip(_names(items), updated)))

    def small_early(self, pieces):
        return _small_all_reduce_host(_pack_small(pieces, EARLY_SMALL))

    def small_early_done(self, total):
        self.early_small = _unpack_small(total, EARLY_SMALL)


def _all_reduce_small(vec, g_pool):
    def body(v_ref, gp_ref, o_ref, po_ref, slots, pslots, send_sems, recv_sems, psend, precv, hsend, hrecv):
        x, y, c, _ = _mesh_place()
        me = 4 * x + 2 * y + c

        def pool_piece(chip, h):
            return gp_ref.at[pl.ds(2 * h, 2), pl.ds(pl.multiple_of(chip * POOL_ROWS, POOL_ROWS), POOL_ROWS), :]

        slots[me] = v_ref[...]
        pslots[me] = pool_piece(2 * x + y, c)[...]
        copies = []
        for d in range(1, N_DEV):
            peer = (x ^ (d >> 2), y ^ ((d >> 1) & 1), c ^ (d & 1))
            copies.append(pltpu.make_async_remote_copy(
                src_ref=v_ref, dst_ref=slots.at[me], send_sem=send_sems.at[d - 1], recv_sem=recv_sems.at[d - 1],
                device_id=peer, device_id_type=MESH))
            copies.append(pltpu.make_async_remote_copy(
                src_ref=pool_piece(2 * peer[0] + peer[1], peer[2]), dst_ref=pslots.at[me], send_sem=psend.at[d - 1],
                recv_sem=precv.at[d - 1], device_id=peer, device_id_type=MESH))
        for cp in copies:
            cp.start()
        for d in range(1, N_DEV):
            peer = 4 * (x ^ (d >> 2)) + 2 * (y ^ ((d >> 1) & 1)) + (c ^ (d & 1))
            for buf, ss, rs in ((slots, send_sems, recv_sems), (pslots, psend, precv)):
                got = buf.at[peer]
                pltpu.make_async_remote_copy(src_ref=got, dst_ref=got, send_sem=ss.at[d - 1], recv_sem=rs.at[d - 1],
                                             device_id=(x, y, c), device_id_type=MESH).wait_recv()
        for cp in copies:
            cp.wait_send()
        acc, pacc = slots[0], pslots[0]
        for d in range(1, N_DEV):
            acc, pacc = acc + slots[d], pacc + pslots[d]
        o_ref[...] = acc
        po_ref[pl.ds(2 * c, 2)] = pacc
        mine, theirs = po_ref.at[pl.ds(2 * c, 2)], po_ref.at[pl.ds(2 * (1 - c), 2)]
        give = pltpu.make_async_remote_copy(src_ref=mine, dst_ref=mine, send_sem=hsend.at[0], recv_sem=hrecv.at[0],
                                            device_id=(x, y, 1 - c), device_id_type=MESH)
        give.start()
        pltpu.make_async_remote_copy(src_ref=theirs, dst_ref=theirs, send_sem=hsend.at[0], recv_sem=hrecv.at[0],
                                     device_id=(x, y, 1 - c), device_id_type=MESH).wait_recv()
        give.wait_send()

    vm = pl.BlockSpec(memory_space=pltpu.VMEM)
    piece = (2, POOL_ROWS, POOL_GC)
    return pl.pallas_call(
        body, name="all_reduce_small", in_specs=[vm, vm], out_specs=[vm, vm],
        out_shape=[jax.ShapeDtypeStruct(vec.shape, F32), jax.ShapeDtypeStruct(SHARD_SHAPES["pool_w"], F32)],
        scratch_shapes=[pltpu.VMEM((N_DEV,) + vec.shape, F32), pltpu.VMEM((N_DEV,) + piece, F32)]
        + [pltpu.SemaphoreType.DMA((N_DEV - 1,))] * 4 + [pltpu.SemaphoreType.DMA((1,))] * 2,
    )(vec, g_pool)


def _adamw_call(w, g, m, v, name, copy_grad=False):
    shape = w.shape
    cols = shape[-1]
    rows = w.size // cols
    tr = rows
    for cand in (256, 128, 64, 32, 16, 8):
        if rows > cand and rows % cand == 0 and cand * cols * 4 <= 2 * 1024 * 1024:
            tr = cand
            break

    def body(w_ref, g_ref, m_ref, v_ref, d_ref, nm_ref, nv_ref, *g_out):
        gv = g_ref[...]
        if copy_grad:
            g_out[0][...] = gv
        mn = ADAM_B1 * m_ref[...] + (1.0 - ADAM_B1) * gv
        vn = ADAM_B2 * v_ref[...] + (1.0 - ADAM_B2) * jnp.square(gv)
        m_hat = mn / (1.0 - ADAM_B1 ** ADAM_STEP)
        v_hat = vn / (1.0 - ADAM_B2 ** ADAM_STEP)
        d_ref[...] = -ADAM_LR * (m_hat / (jnp.sqrt(v_hat) + ADAM_EPS) + ADAM_WD * w_ref[...])
        nm_ref[...] = mn
        nv_ref[...] = vn

    spec = pl.BlockSpec((tr, cols), lambda i: (i, 0))
    flat = lambda a: a.reshape(rows, cols)
    n_out = 4 if copy_grad else 3
    outs = pl.pallas_call(
        body, name=name, grid=(rows // tr,), in_specs=[spec] * 4, out_specs=[spec] * n_out,
        out_shape=[jax.ShapeDtypeStruct((rows, cols), F32)] * n_out, compiler_params=_params("parallel"),
    )(flat(w), flat(g), flat(m), flat(v))
    return tuple(o.reshape(shape) for o in outs)


WEIGHTS = ("pool_w", "pool_b", "pool_scale", "attn_w_qkv", "attn_b_qkv", "attn_sinks", "attn_w_o", "norm_mix",
           "norm_ffn", "ffn_w_in", "ffn_conv_w", "ffn_conv_b", "ffn_w_out", "norm_f")
SMALL_GATHER = 128 * 71
EARLY_SMALL = ((("b_qkv", 1536), ("sinks", 128), ("norm_mix1", 1024), ("norm_ffn1", 1024), ("conv_w", 6 * 2 * D_FF),
                ("conv_b", 4 * D_FF), ("norm_f", 1024), ("loss", 128)), 6272)
LATE_SMALL = ((("pool_b", 1024), ("pool_scale", 1024), ("norm_mix0", 1024), ("norm_ffn0", 1024)), 512)


def _pack_small(pieces, layout):
    sizes, cols = layout
    flat = jnp.concatenate([jnp.pad(pieces[nm].reshape(-1), (0, size - pieces[nm].size)) for nm, size in sizes])
    return jnp.pad(flat, (0, 8 * cols - flat.size)).reshape(8, cols)


def _unpack_small(vec, layout):
    flat = vec.reshape(-1)
    out, off = {}, 0
    for nm, size in layout[0]:
        out[nm] = flat[off:off + size]
        off += size
    return out


def kernel(x, pool_w, pool_b, pool_scale, attn_w_qkv, attn_b_qkv, attn_sinks, attn_w_o, norm_mix, norm_ffn, ffn_w_in, ffn_conv_w, ffn_conv_b, ffn_w_out, norm_f, loss_target, m_pool_w, m_pool_b, m_pool_scale, m_attn_w_qkv, m_attn_b_qkv, m_attn_sinks, m_attn_w_o, m_norm_mix, m_norm_ffn, m_ffn_w_in, m_ffn_conv_w, m_ffn_conv_b, m_ffn_w_out, m_norm_f, v_pool_w, v_pool_b, v_pool_scale, v_attn_w_qkv, v_attn_b_qkv, v_attn_sinks, v_attn_w_o, v_norm_mix, v_norm_ffn, v_ffn_w_in, v_ffn_conv_w, v_ffn_conv_b, v_ffn_w_out, v_norm_f):
    weights = dict(pool_w=pool_w, pool_b=pool_b, pool_scale=pool_scale, attn_w_qkv=attn_w_qkv, attn_b_qkv=attn_b_qkv,
                   attn_sinks=attn_sinks, attn_w_o=attn_w_o, norm_mix=norm_mix, norm_ffn=norm_ffn, ffn_w_in=ffn_w_in,
                   ffn_conv_w=ffn_conv_w, ffn_conv_b=ffn_conv_b, ffn_w_out=ffn_w_out, norm_f=norm_f)
    m_in = dict(pool_w=m_pool_w, pool_b=m_pool_b, pool_scale=m_pool_scale, attn_w_qkv=m_attn_w_qkv,
                attn_b_qkv=m_attn_b_qkv, attn_sinks=m_attn_sinks, attn_w_o=m_attn_w_o, norm_mix=m_norm_mix,
                norm_ffn=m_norm_ffn, ffn_w_in=m_ffn_w_in, ffn_conv_w=m_ffn_conv_w, ffn_conv_b=m_ffn_conv_b,
                ffn_w_out=m_ffn_w_out, norm_f=m_norm_f)
    v_in = dict(pool_w=v_pool_w, pool_b=v_pool_b, pool_scale=v_pool_scale, attn_w_qkv=v_attn_w_qkv,
                attn_b_qkv=v_attn_b_qkv, attn_sinks=v_attn_sinks, attn_w_o=v_attn_w_o, norm_mix=v_norm_mix,
                norm_ffn=v_norm_ffn, ffn_w_in=v_ffn_w_in, ffn_conv_w=v_ffn_conv_w, ffn_conv_b=v_ffn_conv_b,
                ffn_w_out=v_ffn_w_out, norm_f=v_norm_f)
    chip = 2 * lax.axis_index("x") + lax.axis_index("y")
    core = lax.axis_index("c")

    place = jnp.stack([chip, core]).astype(jnp.int32)
    shards = {"w_in": ffn_w_in, "w_out": ffn_w_out, "w_qkv": attn_w_qkv[0], "w_o": attn_w_o[0], "pool_w": pool_w[0]}
    placed = {nm: _place_shard(nm, shards[nm], place) for nm in BIG}
    small = jnp.concatenate([pool_b.reshape(-1), attn_b_qkv.reshape(-1), ffn_conv_w.reshape(-1)]).reshape(1, SMALL_GATHER)
    full, small_all = _gather_early(placed, small)
    small_all = small_all.reshape(N_CHIPS, SMALL_GATHER)
    pool_b_full = small_all[:, :256].reshape(N_CHIPS, 4, POOL_ROWS).transpose(1, 0, 2).reshape(1, D_MODEL)
    b_qkv_full = small_all[:, 256:640].reshape(1, QKV_DIM)
    conv_w_full = small_all[:, 640:].reshape(N_CHIPS, 2, 3, W_IN_COLS).transpose(1, 2, 0, 3).reshape(2, 3, 2 * D_FF)

    reducer = _Reducer(place)
    _, grad_x, grads = _local_step(
        x[0], loss_target[0], full, norm_mix=norm_mix, norm_ffn=norm_ffn, norm_f=norm_f.reshape(1, D_MODEL),
        pool_b=pool_b_full, pool_scale=pool_scale, b_qkv=b_qkv_full, sinks=attn_sinks[0], conv_w=conv_w_full,
        conv_b=ffn_conv_b, late_gather=_late_gather_host, reducer=reducer)
    reduced = dict(reducer.shards)

    late = {"pool_b": grads["pool_b"], "pool_scale": grads["pool_scale"], "norm_mix0": grads["norm_mix"][0],
            "norm_ffn0": grads["norm_ffn"][0]}
    late_sum, reduced["pool_w"] = _all_reduce_small(_pack_small(late, LATE_SMALL), grads["pool_w"])
    small_g = {**reducer.early_small, **_unpack_small(late_sum, LATE_SMALL)}
    small_g["norm_mix"] = jnp.concatenate([small_g["norm_mix0"], small_g["norm_mix1"]])
    small_g["norm_ffn"] = jnp.concatenate([small_g["norm_ffn0"], small_g["norm_ffn1"]])
    loss = small_g["loss"][0]
    pool_b_g = lax.dynamic_slice_in_dim(small_g["pool_b"].reshape(4, N_CHIPS, POOL_ROWS), chip, 1, axis=1)
    b_qkv_g = lax.dynamic_slice_in_dim(small_g["b_qkv"].reshape(N_CHIPS, QKV_COLS), chip, 1, axis=0)
    conv_w_g = lax.dynamic_slice_in_dim(small_g["conv_w"].reshape(2, 3, N_CHIPS, W_IN_COLS), chip, 1, axis=2)
    grad_w = {
        "pool_w": reduced["pool_w"].reshape(pool_w.shape), "pool_b": pool_b_g.reshape(pool_b.shape),
        "pool_scale": small_g["pool_scale"].reshape(pool_scale.shape),
        "attn_w_qkv": reduced["w_qkv"].reshape(attn_w_qkv.shape), "attn_b_qkv": b_qkv_g.reshape(attn_b_qkv.shape),
        "attn_sinks": small_g["sinks"][:N_HEADS].reshape(attn_sinks.shape),
        "attn_w_o": reduced["w_o"].reshape(attn_w_o.shape), "norm_mix": small_g["norm_mix"].reshape(norm_mix.shape),
        "norm_ffn": small_g["norm_ffn"].reshape(norm_ffn.shape), "ffn_w_in": reduced["w_in"],
        "ffn_conv_w": conv_w_g.reshape(ffn_conv_w.shape), "ffn_conv_b": small_g["conv_b"].reshape(ffn_conv_b.shape),
        "ffn_w_out": reduced["w_out"], "norm_f": small_g["norm_f"].reshape(norm_f.shape),
    }

    delta, new_m, new_v = {}, {}, {}
    exchanged = ("pool_w", "attn_w_qkv", "attn_w_o", "ffn_w_in", "ffn_w_out")
    for nm in WEIGHTS:
        res = _adamw_call(weights[nm], grad_w[nm], m_in[nm], v_in[nm], f"adamw_{nm}", copy_grad=nm in exchanged)
        delta[nm], new_m[nm], new_v[nm] = res[:3]
        if nm in exchanged:
            grad_w[nm] = res[3]
    return (loss, grad_x.reshape(x.shape), *[grad_w[nm] for nm in WEIGHTS], *[delta[nm] for nm in WEIGHTS],
            *[new_m[nm] for nm in WEIGHTS], *[new_v[nm] for nm in WEIGHTS])
```

```python
import functools
from typing import Callable, NamedTuple

import jax
import jax.numpy as jnp
from jax import lax
from jax.experimental import pallas as pl
from jax.experimental.pallas import tpu as pltpu

F32 = jnp.float32
MXU_DTYPE = jnp.bfloat16
WIRE_DTYPE = jnp.bfloat16
U_DTYPE = jnp.float32

D_MODEL = 1024
D_FF = 2816
QKV_DIM = 1536
HEAD_DIM = 64
N_HEADS = 16
N_KV_HEADS = 4
ATT_BLOCK = 128
POOL_WINDOWS = (2, 4, 8, 16)
POOL_GC = 256
POOL_HALO = 16
CONV_HALO = 8
RMS_EPS = 1e-6
ATT_SCALE = HEAD_DIM ** -0.5
ALIBI_SLOPES = tuple(2.0 ** (-8.0 / N_HEADS * (h + 1)) for h in range(N_HEADS))

ADAM_LR = 0.001
ADAM_B1 = 0.9
ADAM_B2 = 0.999
ADAM_EPS = 1e-08
ADAM_WD = 0.01
ADAM_STEP = 10

N_CHIPS = 4
MESH = pl.DeviceIdType.MESH
VMEM_LIMIT_BYTES = 56 * 1024 * 1024
ANY = pl.BlockSpec(memory_space=pl.ANY)


def _params(*semantics):
    return pltpu.CompilerParams(dimension_semantics=semantics, vmem_limit_bytes=VMEM_LIMIT_BYTES)


def _rms(x, g):
    return x * lax.rsqrt(jnp.mean(x * x, axis=-1, keepdims=True) + RMS_EPS) * g


def _rms_bwd(x, g, dy):
    rstd = lax.rsqrt(jnp.mean(x * x, axis=-1, keepdims=True) + RMS_EPS)
    xhat = x * rstd
    dxhat = dy * g
    dx = rstd * (dxhat - xhat * jnp.mean(dxhat * xhat, axis=-1, keepdims=True))
    return dx, dy * xhat


def _shift_down(x, s):
    return pltpu.roll(x, s, axis=0)


def _shift_up(x, s):
    return pltpu.roll(x, x.shape[0] - s, axis=0)


def _sigmoid(x):
    return 1.0 / (1.0 + jnp.exp(-x))


_DOT_DIMS = {"nn": ((1,), (0,)), "nt": ((1,), (1,)), "tn": ((0,), (0,))}


class _Hosted(NamedTuple):
    sources: list
    arrays: list
    new: list
    sems: list
    first: Callable
    last: Callable


def _call(body, hosted, *, name, grid, in_specs, out_specs, out_shape, operands, semantics, scratch_shapes=(),
          aliases=None):
    n_in, n_out, n_scr = len(in_specs), len(out_specs), len(scratch_shapes)
    aliases = dict(aliases or {})
    if hosted is None:
        def plain(*refs):
            body(refs[:n_in], refs[n_in:n_in + n_out], refs[n_in + n_out:], lambda: None, lambda: None)

        return pl.pallas_call(plain, name=name, grid=grid, in_specs=list(in_specs), out_specs=list(out_specs),
                              out_shape=list(out_shape), scratch_shapes=list(scratch_shapes),
                              input_output_aliases=aliases, compiler_params=_params(*semantics))(*operands)
    ns, na, nn = len(hosted.sources), len(hosted.arrays), len(hosted.new)

    def carrying(*refs):
        ins, src = refs[:n_in], refs[n_in:n_in + ns]
        o0 = n_in + ns + na
        outs, arr = refs[o0:o0 + n_out], refs[o0 + n_out:o0 + n_out + na]
        new = refs[o0 + n_out + na:o0 + n_out + na + nn]
        s0 = o0 + n_out + na + nn
        scratch, sems = refs[s0:s0 + n_scr], refs[s0 + n_scr:]
        ids = [pl.program_id(d) for d in range(len(grid))]
        is_first = functools.reduce(lambda p, q: p & q, [i == 0 for i in ids])
        is_last = functools.reduce(lambda p, q: p & q, [i == n - 1 for i, n in zip(ids, grid)])
        body(ins, outs, scratch, lambda: pl.when(is_first)(lambda: hosted.first(src, arr, new, sems)),
             lambda: pl.when(is_last)(lambda: hosted.last(src, arr, new, sems)))

    for t in range(na):
        aliases[n_in + ns + t] = n_out + t
    return pl.pallas_call(
        carrying, name=name, grid=grid, in_specs=list(in_specs) + [ANY] * (ns + na),
        out_specs=list(out_specs) + [ANY] * (na + nn),
        out_shape=list(out_shape) + [jax.ShapeDtypeStruct(a.shape, a.dtype) for a in hosted.arrays] + list(hosted.new),
        scratch_shapes=list(scratch_shapes) + list(hosted.sems), input_output_aliases=aliases,
        compiler_params=_params(*["arbitrary"] * len(grid)))(*operands, *hosted.sources, *hosted.arrays)


def _both(h1, h2):
    def run(which, src, arr, new, sems):
        cut = lambda seq, n: (seq[:n], seq[n:])
        (s1, s2), (a1, a2) = cut(src, len(h1.sources)), cut(arr, len(h1.arrays))
        (n1, n2), (m1, m2) = cut(new, len(h1.new)), cut(sems, len(h1.sems))
        getattr(h1, which)(s1, a1, n1, m1)
        getattr(h2, which)(s2, a2, n2, m2)

    return _Hosted(h1.sources + h2.sources, h1.arrays + h2.arrays, h1.new + h2.new, h1.sems + h2.sems,
                   functools.partial(run, "first"), functools.partial(run, "last"))


def _ride(hooks, stage, call, env=None):
    if not hooks or stage not in hooks:
        res = call(None)
        return list(res) if isinstance(res, (list, tuple)) else [res]
    make, done = hooks[stage]
    hosted = make(env)
    res = call(hosted)
    own = len(res) - len(hosted.arrays) - len(hosted.new)
    done(res[own:])
    return list(res[:own])


def _matmul(a, b, *, mode, name, grid, a_spec, b_spec, o_spec, out_shape, bias=None, bias_spec=None, residual=None,
            residual_spec=None, into=None, hosted=None, norm_g=None):
    nk = grid[2]
    dims = (_DOT_DIMS[mode], ((), ()))
    acc_shape = tuple(d for d in o_spec.block_shape if d is not None)

    def body(ins, outs, scratch, first, last):
        a_ref, b_ref = ins[0], ins[1]
        pos = 2
        bias_ref = res_ref = None
        if bias is not None:
            bias_ref = ins[pos]
            pos += 1
        if residual is not None:
            res_ref = ins[pos]
            pos += 1
        o_ref = outs[0]
        acc_ref = scratch[0] if nk > 1 else None
        first()
        prod = lax.dot_general(a_ref[...].astype(MXU_DTYPE), b_ref[...].astype(MXU_DTYPE), dims,
                               preferred_element_type=F32)

        def finish(acc):
            if bias_ref is not None:
                acc = acc + bias_ref[...]
            if res_ref is not None:
                acc = acc + res_ref[...]
            o_ref[...] = acc.astype(o_ref.dtype)
            if norm_g is not None:
                outs[1][...] = _rms(acc, ins[pos][...]).astype(outs[1].dtype)

        if nk == 1:
            finish(prod)
        else:
            k = pl.program_id(2)

            @pl.when(k == 0)
            def _():
                acc_ref[...] = prod

            @pl.when(k > 0)
            def _():
                acc_ref[...] += prod

            @pl.when(k == nk - 1)
            def _():
                finish(acc_ref[...])

        last()

    operands, in_specs = [a, b], [a_spec, b_spec]
    if bias is not None:
        operands.append(bias)
        in_specs.append(bias_spec)
    if residual is not None:
        operands.append(residual)
        in_specs.append(residual_spec)
    out_specs, out_shapes = [o_spec], [out_shape]
    if norm_g is not None:
        assert acc_shape[-1] == norm_g.shape[-1], "the fused norm needs whole rows in an output tile"
        operands.append(norm_g)
        in_specs.append(bias_spec)
        out_specs.append(o_spec)
        out_shapes.append(jax.ShapeDtypeStruct(out_shape.shape, MXU_DTYPE))
    aliases = {}
    if into is not None:
        aliases = {len(operands): 0}
        operands.append(into)
        in_specs.append(ANY)
    res = _call(body, hosted, name=name, grid=grid, in_specs=in_specs, out_specs=out_specs, out_shape=out_shapes,
                operands=operands, semantics=("parallel", "parallel", "arbitrary"),
                scratch_shapes=[pltpu.VMEM(acc_shape, F32)] if nk > 1 else [], aliases=aliases)
    return res if hosted or norm_g is not None else res[0]


def _tile(n, want):
    t = min(n, want)
    assert n % t == 0, (n, want)
    return t


def _mm_nn(a, b, name, *, b_lead=None, bias=None, residual=None, out_dtype=F32, tm=1024, tn=1024, tk=1024,
           hosted=None, norm_g=None):
    (m, k), n = a.shape, b.shape[-1]
    tm, tn, tk = _tile(m, tm), _tile(n, tn), _tile(k, tk)
    if b_lead is None:
        b_spec = pl.BlockSpec((tk, tn), lambda i, j, kk: (kk, j))
    else:
        b_spec = pl.BlockSpec((None, tk, tn), lambda i, j, kk: (b_lead, kk, j))
    return _matmul(
        a, b, mode="nn", name=name, grid=(m // tm, n // tn, k // tk),
        a_spec=pl.BlockSpec((tm, tk), lambda i, j, kk: (i, kk)), b_spec=b_spec,
        o_spec=pl.BlockSpec((tm, tn), lambda i, j, kk: (i, j)), out_shape=jax.ShapeDtypeStruct((m, n), out_dtype),
        bias=bias, bias_spec=pl.BlockSpec((1, tn), lambda i, j, kk: (0, j)),
        residual=residual, residual_spec=pl.BlockSpec((tm, tn), lambda i, j, kk: (i, j)), hosted=hosted,
        norm_g=norm_g)


def _mm_nt(a, b, name, *, b_lead=None, out_dtype=F32, tm=1024, tn=1024, tk=1024, hosted=None):
    (m, k), n = a.shape, b.shape[-2]
    tm, tn, tk = _tile(m, tm), _tile(n, tn), _tile(k, tk)
    if b_lead is None:
        b_spec = pl.BlockSpec((tn, tk), lambda i, j, kk: (j, kk))
    else:
        b_spec = pl.BlockSpec((None, tn, tk), lambda i, j, kk: (b_lead, j, kk))
    return _matmul(
        a, b, mode="nt", name=name, grid=(m // tm, n // tn, k // tk),
        a_spec=pl.BlockSpec((tm, tk), lambda i, j, kk: (i, kk)), b_spec=b_spec,
        o_spec=pl.BlockSpec((tm, tn), lambda i, j, kk: (i, j)), out_shape=jax.ShapeDtypeStruct((m, n), out_dtype),
        hosted=hosted)


def _mm_tn(a, b, name, *, lead=None, n_lead=None, into=None, tm=1024, tn=1024, tk=1024, out_dtype=F32):
    (k, m), n = a.shape, b.shape[-1]
    tm, tn, tk = _tile(m, tm), _tile(n, tn), _tile(k, tk)
    if lead is None:
        o_spec = pl.BlockSpec((tm, tn), lambda i, j, kk: (i, j))
        out_shape = jax.ShapeDtypeStruct((m, n), out_dtype)
    else:
        o_spec = pl.BlockSpec((None, tm, tn), lambda i, j, kk: (lead, i, j))
        out_shape = jax.ShapeDtypeStruct((n_lead, m, n), out_dtype)
    return _matmul(
        a, b, mode="tn", name=name, grid=(m // tm, n // tn, k // tk),
        a_spec=pl.BlockSpec((tk, tm), lambda i, j, kk: (kk, i)),
        b_spec=pl.BlockSpec((tk, tn), lambda i, j, kk: (kk, j)), o_spec=o_spec, out_shape=out_shape, into=into)


ROW_TILE = 512


def _pool_windows_causal(ext, first_row):
    tt = ext.shape[0] - POOL_HALO
    t = first_row + lax.broadcasted_iota(jnp.int32, (tt, 1), 0)
    outs = []
    for gi, win in enumerate(POOL_WINDOWS):
        cols = slice(gi * POOL_GC, (gi + 1) * POOL_GC)
        s = ext[:, cols]
        sh = 1
        while sh < win:
            s = s + _shift_down(s, sh)
            sh *= 2
        count = jnp.minimum(t + 1, win).astype(F32)
        outs.append(s[POOL_HALO:] / count - ext[POOL_HALO:, cols])
    return outs


def _prev_halo_spec(tt, halo, width):
    per = tt // halo
    return pl.BlockSpec((halo, width), lambda i: (jnp.maximum(i * per - 1, 0), 0))


def _pool_fwd_call(h0, g_mix, pool_w, pool_b, pool_scale, g_ffn, hosted=None):
    t = h0.shape[0]
    tt = _tile(t, ROW_TILE)

    def body(ins, outs, scratch, first, last):
        h_ref, halo_ref, gm_ref, w_ref, b_ref, sc_ref, gf_ref = ins
        h1_ref, hn_ref = outs
        i = pl.program_id(0)
        first()
        x, gm = h_ref[...], gm_ref[...]
        top = jnp.where(i > 0, _rms(halo_ref[...], gm), 0.0)
        ext = jnp.concatenate([top, _rms(x, gm)], axis=0)
        ps = _pool_windows_causal(ext, i * tt)
        ys = [jnp.dot(p.astype(MXU_DTYPE), w_ref[gi], preferred_element_type=F32) for gi, p in enumerate(ps)]
        mix = (jnp.concatenate(ys, axis=1) + b_ref[...]) * sc_ref[...]
        h1 = x + mix
        h1_ref[...] = h1
        hn_ref[...] = _rms(h1, gf_ref[...]).astype(hn_ref.dtype)
        last()

    row = pl.BlockSpec((tt, D_MODEL), lambda i: (i, 0))
    vec = pl.BlockSpec((1, D_MODEL), lambda i: (0, 0))
    wsp = pl.BlockSpec((len(POOL_WINDOWS), POOL_GC, POOL_GC), lambda i: (0, 0, 0))
    return _call(
        body, hosted, name="pool_fwd", grid=(t // tt,),
        in_specs=[row, _prev_halo_spec(tt, POOL_HALO, D_MODEL), vec, wsp, vec, vec, vec], out_specs=[row, row],
        out_shape=[jax.ShapeDtypeStruct(h0.shape, F32), jax.ShapeDtypeStruct(h0.shape, MXU_DTYPE)],
        operands=(h0, h0, g_mix, pool_w, pool_b, pool_scale, g_ffn), semantics=("parallel",))


def _pool_bwd_call(h0, dh1, g_mix, pool_w, pool_b, pool_scale, hosted=None):
    t = h0.shape[0]
    tt = _tile(t, ROW_TILE)
    nt = t // tt
    per = tt // POOL_HALO

    def body(ins, outs, scratch, first, last):
        h_ref, halo_ref, d_ref, dnext_ref, gm_ref, w_ref, b_ref, sc_ref = ins
        gx_ref, dw_ref, dv_ref = outs
        i = pl.program_id(0)
        first()
        x, gm, sc = h_ref[...], gm_ref[...], sc_ref[...]
        top = jnp.where(i > 0, _rms(halo_ref[...], gm), 0.0)
        ext = jnp.concatenate([top, _rms(x, gm)], axis=0)
        ps = _pool_windows_causal(ext, i * tt)

        dy = d_ref[...]
        dy_ext = jnp.concatenate([dy, jnp.where(i < nt - 1, dnext_ref[...], 0.0)], axis=0)
        dpre_ext = dy_ext * sc
        t_ext = i * tt + lax.broadcasted_iota(jnp.int32, (tt + POOL_HALO, 1), 0)

        @pl.when(i == 0)
        def _():
            dw_ref[...] = jnp.zeros_like(dw_ref)
            dv_ref[...] = jnp.zeros_like(dv_ref)

        dhn, ypre = [], []
        for gi, win in enumerate(POOL_WINDOWS):
            cols = slice(gi * POOL_GC, (gi + 1) * POOL_GC)
            w = w_ref[gi]
            p16 = ps[gi].astype(MXU_DTYPE)
            dpre16 = dpre_ext[:, cols].astype(MXU_DTYPE)
            ypre.append(jnp.dot(p16, w, preferred_element_type=F32))
            dw_ref[gi] += lax.dot_general(p16, dpre16[:tt], (((0,), (0,)), ((), ())), preferred_element_type=F32)
            dp_ext = lax.dot_general(dpre16, w, (((1,), (1,)), ((), ())), preferred_element_type=F32)
            s = dp_ext / jnp.minimum(t_ext + 1, win).astype(F32)
            sh = 1
            while sh < win:
                s = s + _shift_up(s, sh)
                sh *= 2
            dhn.append(s[:tt] - dp_ext[:tt])
        dhn = jnp.concatenate(dhn, axis=1)
        ypre = jnp.concatenate(ypre, axis=1) + b_ref[...]
        dx, dgt = _rms_bwd(x, gm, dhn)
        gx_ref[...] = dy + dx
        dv_ref[0:1, :] += jnp.sum(dpre_ext[:tt], axis=0, keepdims=True)
        dv_ref[1:2, :] += jnp.sum(dy * ypre, axis=0, keepdims=True)
        dv_ref[2:3, :] += jnp.sum(dgt, axis=0, keepdims=True)
        last()

    row = pl.BlockSpec((tt, D_MODEL), lambda i: (i, 0))
    vec = pl.BlockSpec((1, D_MODEL), lambda i: (0, 0))
    nxt = pl.BlockSpec((POOL_HALO, D_MODEL), lambda i: (jnp.minimum((i + 1) * per, t // POOL_HALO - 1), 0))
    wsp = pl.BlockSpec((len(POOL_WINDOWS), POOL_GC, POOL_GC), lambda i: (0, 0, 0))
    return _call(
        body, hosted, name="pool_bwd", grid=(nt,),
        in_specs=[row, _prev_halo_spec(tt, POOL_HALO, D_MODEL), row, nxt, vec, wsp, vec, vec],
        out_specs=[row, wsp, pl.BlockSpec((8, D_MODEL), lambda i: (0, 0))],
        out_shape=[jax.ShapeDtypeStruct(h0.shape, F32),
                   jax.ShapeDtypeStruct((len(POOL_WINDOWS), POOL_GC, POOL_GC), F32),
                   jax.ShapeDtypeStruct((8, D_MODEL), F32)],
        operands=(h0, h0, dh1, dh1, g_mix, pool_w, pool_b, pool_scale), semantics=("arbitrary",))


CONV_ROWS = 512
CONV_SUB = 64
LANES = 128


def _causal_conv(u_ext, w, b):
    return w[0:1] * _shift_down(u_ext, 2) + w[1:2] * _shift_down(u_ext, 1) + w[2:3] * u_ext + b


def _piece_rows(load, s, sub, tt, top, bottom=None):
    lo, hi = s * sub - CONV_HALO, (s + 1) * sub + (CONV_HALO if bottom else 0)
    parts = [top()] if lo < 0 else []
    lo = max(lo, 0)
    parts.append(load(lo, min(hi, tt) - lo))
    if hi > tt:
        parts.append(bottom())
    return parts[0] if len(parts) == 1 else jnp.concatenate(parts, axis=0)


def _fold8(x):
    acc = x[0:8]
    for r in range(8, x.shape[0], 8):
        acc = acc + x[r:r + 8]
    return acc


def _conv_glu_fwd_call(u, conv_w, conv_b, name, hosted=None):
    _, t, f = u.shape
    tt = _tile(t, CONV_ROWS)
    sub = _tile(tt, CONV_SUB)
    per = tt // CONV_HALO

    def body(ins, outs, scratch, first, last):
        u_ref, up_ref, w_ref, b_ref = ins
        z_ref = outs[0]
        i = pl.program_id(0)
        first()

        def chunk(j, carry):
            lanes = pl.ds(pl.multiple_of(j * LANES, LANES), LANES)
            w, b = w_ref[:, :, lanes], b_ref[:, lanes]
            for s in range(tt // sub):
                c = []
                for hf in range(2):
                    ext = _piece_rows(lambda at, n: u_ref[hf, pl.ds(at, n), lanes].astype(F32), s, sub, tt,
                                      top=lambda: jnp.where(i > 0, up_ref[hf, :, lanes].astype(F32), 0.0))
                    c.append(_causal_conv(ext, w[:, hf, :], b[hf:hf + 1, :])[CONV_HALO:])
                z_ref[pl.ds(s * sub, sub), lanes] = (c[0] * c[1] * _sigmoid(c[1])).astype(z_ref.dtype)
            return carry

        lax.fori_loop(0, f // LANES, chunk, 0)
        last()

    res = _call(
        body, hosted, name=name, grid=(t // tt,),
        in_specs=[pl.BlockSpec((2, tt, f), lambda i: (0, i, 0)),
                  pl.BlockSpec((2, CONV_HALO, f), lambda i: (0, jnp.maximum(i * per - 1, 0), 0)),
                  pl.BlockSpec((3, 2, f), lambda i: (0, 0, 0)), pl.BlockSpec((2, f), lambda i: (0, 0))],
        out_specs=[pl.BlockSpec((tt, f), lambda i: (i, 0))], out_shape=[jax.ShapeDtypeStruct((t, f), MXU_DTYPE)],
        operands=(u, u, conv_w, conv_b), semantics=("parallel",))
    return res if hosted else res[0]


def _conv_glu_bwd_call(u, dz, conv_w, conv_b, name, hosted=None):
    _, t, f = u.shape
    tt = _tile(t, CONV_ROWS)
    sub = _tile(tt, 2 * CONV_SUB)
    nt = t // tt
    per = tt // CONV_HALO
    halo = CONV_HALO

    def body(ins, outs, scratch, first, last):
        u_ref, up_ref, un_ref, dz_ref, dzn_ref, w_ref, b_ref = ins
        du_ref, dw_ref, db_ref = outs
        dwacc, dbacc = scratch
        i = pl.program_id(0)
        first()

        @pl.when(i == 0)
        def _():
            dwacc[...] = jnp.zeros_like(dwacc)
            dbacc[...] = jnp.zeros_like(dbacc)

        main = slice(halo, halo + sub)

        def chunk(j, carry):
            lanes = pl.ds(pl.multiple_of(j * LANES, LANES), LANES)
            w, b = w_ref[:, :, lanes], b_ref[:, lanes]
            for s in range(tt // sub):
                ue = [_piece_rows(lambda at, n: u_ref[hf, pl.ds(at, n), lanes].astype(F32), s, sub, tt,
                                  top=lambda: jnp.where(i > 0, up_ref[hf, :, lanes].astype(F32), 0.0),
                                  bottom=lambda: un_ref[hf, :, lanes].astype(F32)) for hf in range(2)]
                a, g = [_causal_conv(ue[hf], w[:, hf, :], b[hf:hf + 1, :])[halo:] for hf in range(2)]
                sg = _sigmoid(g)
                if (s + 1) * sub < tt:
                    dzs = dz_ref[pl.ds(s * sub, sub + 2 * halo), lanes].astype(F32)[:sub + halo]
                else:
                    dzs = jnp.concatenate([dz_ref[pl.ds(s * sub, sub), lanes].astype(F32),
                                           jnp.where(i < nt - 1, dzn_ref[:, lanes].astype(F32), 0.0)], axis=0)
                gs = g * sg
                dc = [dzs * gs, dzs * a * (sg + gs - gs * sg)]
                for hf in range(2):
                    d = dc[hf]
                    d0, d1, d2 = d[:sub], _shift_up(d, 1)[:sub], _shift_up(d, 2)[:sub]
                    du = w[2:3, hf, :] * d0 + w[1:2, hf, :] * d1 + w[0:1, hf, :] * d2
                    du_ref[hf, pl.ds(s * sub, sub), lanes] = du.astype(du_ref.dtype)
                    um = ue[hf][main]
                    for k, dd in enumerate((d2, d1, d0)):
                        dwacc[k, hf, :, lanes] += _fold8(dd * um)
                    dbacc[hf, :, lanes] += _fold8(d0)
            return carry

        lax.fori_loop(0, f // LANES, chunk, 0)

        @pl.when(i == nt - 1)
        def _():
            dw_ref[...] = jnp.sum(dwacc[...], axis=2)
            db_ref[...] = jnp.sum(dbacc[...], axis=1)

        last()

    end = t // halo - 1
    return _call(
        body, hosted, name=name, grid=(nt,),
        in_specs=[pl.BlockSpec((2, tt, f), lambda i: (0, i, 0)),
                  pl.BlockSpec((2, halo, f), lambda i: (0, jnp.maximum(i * per - 1, 0), 0)),
                  pl.BlockSpec((2, halo, f), lambda i: (0, jnp.minimum((i + 1) * per, end), 0)),
                  pl.BlockSpec((tt, f), lambda i: (i, 0)),
                  pl.BlockSpec((halo, f), lambda i: (jnp.minimum((i + 1) * per, end), 0)),
                  pl.BlockSpec((3, 2, f), lambda i: (0, 0, 0)), pl.BlockSpec((2, f), lambda i: (0, 0))],
        out_specs=[pl.BlockSpec((2, tt, f), lambda i: (0, i, 0)),
                   pl.BlockSpec((3, 2, f), lambda i: (0, 0, 0)), pl.BlockSpec((2, f), lambda i: (0, 0))],
        out_shape=[jax.ShapeDtypeStruct(u.shape, MXU_DTYPE), jax.ShapeDtypeStruct((3, 2, f), F32),
                   jax.ShapeDtypeStruct((2, f), F32)],
        operands=(u, u, u, dz, dz, conv_w, conv_b), semantics=("arbitrary",),
        scratch_shapes=[pltpu.VMEM((3, 2, 8, f), F32), pltpu.VMEM((2, 8, f), F32)])


def _kv_even_odd(band, k):
    pair, parity = divmod(k, 2)
    blk = band[:, 128 * pair:128 * (pair + 1)].astype(F32)
    lane = lax.broadcasted_iota(jnp.int32, blk.shape, 1)
    own = jnp.where((lane >= HEAD_DIM) == (parity == 1), blk, 0.0)
    swapped = pltpu.roll(own, HEAD_DIM, axis=1)
    even, odd = (own, swapped) if parity == 0 else (swapped, own)
    return even.astype(MXU_DTYPE), odd.astype(MXU_DTYPE)


def _stack_pairs(ref, k):
    return jnp.concatenate([ref[:, 256 * k:256 * k + 128], ref[:, 256 * k + 128:256 * k + 256]], axis=0)


_GROUP_ORDER = (0, 2, 1, 3)


def _attn_fill_tables(bias_ref):
    shape = (4 * ATT_BLOCK, 2 * ATT_BLOCK)
    row = lax.broadcasted_iota(jnp.int32, shape, 0)
    col = lax.broadcasted_iota(jnp.int32, shape, 1)
    dist = (row & (ATT_BLOCK - 1)) + ATT_BLOCK - col
    ok = (dist >= 0) & (dist < ATT_BLOCK)
    distf = dist.astype(F32)
    rb = lax.broadcasted_iota(jnp.int32, (shape[0], 1), 0) // ATT_BLOCK
    for k in range(N_KV_HEADS):
        slope = jnp.zeros((shape[0], 1), F32)
        for r, gq in enumerate(_GROUP_ORDER):
            slope = jnp.where(rb == r, ALIBI_SLOPES[4 * k + gq], slope)
        bias = jnp.where(ok, -slope * distf, -jnp.inf)
        bias_ref[0, k] = bias
        bias_ref[1, k] = jnp.where(col >= ATT_BLOCK, bias, -jnp.inf)


def _attn_probs(qs, k_even, k_odd, bias, k, sink_ref):
    nt_dims = (((1,), (1,)), ((), ()))
    s = jnp.concatenate([lax.dot_general(qs, k_even, nt_dims, preferred_element_type=F32),
                         lax.dot_general(qs, k_odd, nt_dims, preferred_element_type=F32)], axis=0) + bias
    rb = lax.broadcasted_iota(jnp.int32, (s.shape[0], 1), 0) // ATT_BLOCK
    sink = jnp.zeros((s.shape[0], 1), F32)
    for r, gq in enumerate(_GROUP_ORDER):
        sink = jnp.where(rb == r, sink_ref[4 * k + gq], sink)
    m = jnp.maximum(jnp.max(s, axis=-1, keepdims=True), sink)
    p = jnp.exp(s - m)
    es = jnp.exp(sink - m)
    return p, 1.0 / (jnp.sum(p, axis=-1, keepdims=True) + es), es


_ATTN_TABLES = [pltpu.VMEM((2, N_KV_HEADS, 4 * ATT_BLOCK, 2 * ATT_BLOCK), F32)]


def _attn_specs(with_do):
    prev = lambda n: jnp.maximum(n - 1, 0)
    specs = [pl.BlockSpec(memory_space=pltpu.SMEM),
             pl.BlockSpec((ATT_BLOCK, D_MODEL), lambda n: (n, 0)),
             pl.BlockSpec((ATT_BLOCK, 256), lambda n: (prev(n), 4)), pl.BlockSpec((ATT_BLOCK, 256), lambda n: (n, 4)),
             pl.BlockSpec((ATT_BLOCK, 256), lambda n: (prev(n), 5)), pl.BlockSpec((ATT_BLOCK, 256), lambda n: (n, 5))]
    if with_do:
        specs.append(pl.BlockSpec((ATT_BLOCK, D_MODEL), lambda n: (n, 0)))
    return specs


def _attn_fwd_call(qkv, sinks, hosted=None):
    t = qkv.shape[0]

    def body(ins, outs, scratch, first, last):
        sink_ref, q_ref, kp_ref, kc_ref, vp_ref, vc_ref = ins
        o_ref, bias_ref = outs[0], scratch[0]
        n = pl.program_id(0)
        first()

        @pl.when(n == 0)
        def _():
            _attn_fill_tables(bias_ref)

        first = 1 - jnp.minimum(n, 1)
        kband = jnp.concatenate([kp_ref[...], kc_ref[...]], axis=0)
        vband = jnp.concatenate([vp_ref[...], vc_ref[...]], axis=0)
        for k in range(N_KV_HEADS):
            k_even, k_odd = _kv_even_odd(kband, k)
            v_even, v_odd = _kv_even_odd(vband, k)
            qs = _stack_pairs(q_ref, k) * ATT_SCALE
            p, rdenom, _ = _attn_probs(qs, k_even, k_odd, bias_ref[first, k], k, sink_ref)
            probs = (p * rdenom).astype(MXU_DTYPE)
            o = (jnp.dot(probs[:256], v_even, preferred_element_type=F32)
                 + jnp.dot(probs[256:], v_odd, preferred_element_type=F32))
            o_ref[:, 256 * k:256 * k + 128] = o[:128].astype(o_ref.dtype)
            o_ref[:, 256 * k + 128:256 * k + 256] = o[128:].astype(o_ref.dtype)
        last()

    return _call(
        body, hosted, name="attn_fwd", grid=(t // ATT_BLOCK,), in_specs=_attn_specs(False),
        out_specs=[pl.BlockSpec((ATT_BLOCK, D_MODEL), lambda n: (n, 0))],
        out_shape=[jax.ShapeDtypeStruct((t, D_MODEL), MXU_DTYPE)], operands=(sinks, qkv, qkv, qkv, qkv, qkv),
        semantics=("arbitrary",), scratch_shapes=_ATTN_TABLES)


def _attn_bwd_call(qkv, sinks, do, hosted=None):
    t = qkv.shape[0]
    nb = t // ATT_BLOCK
    tn_dims = (((0,), (0,)), ((), ()))
    nt_dims = (((1,), (1,)), ((), ()))

    def to_native(even_part, odd_part, parity):
        lane = lax.broadcasted_iota(jnp.int32, even_part.shape, 1)
        lo = lane < HEAD_DIM
        e, o = jnp.where(lo, even_part, 0.0), jnp.where(lo, 0.0, odd_part)
        if parity == 0:
            return e + pltpu.roll(o, HEAD_DIM, axis=1)
        return pltpu.roll(e, HEAD_DIM, axis=1) + o

    def body(ins, outs, scratch, first, last):
        sink_ref, q_ref, kp_ref, kc_ref, vp_ref, vc_ref, do_ref = ins
        dq_ref, dk_ref, dv_ref, db_ref, dsink_ref = outs
        bias_ref = scratch[0]
        n = pl.program_id(0)
        first()

        @pl.when(n == 0)
        def _():
            _attn_fill_tables(bias_ref)
            dk_ref[...] = jnp.zeros_like(dk_ref)
            dv_ref[...] = jnp.zeros_like(dv_ref)
            db_ref[...] = jnp.zeros_like(db_ref)
            dsink_ref[...] = jnp.zeros_like(dsink_ref)

        kband = jnp.concatenate([kp_ref[...], kc_ref[...]], axis=0)
        vband = jnp.concatenate([vp_ref[...], vc_ref[...]], axis=0)
        dk_pairs = [jnp.zeros((2 * ATT_BLOCK, 128), F32), jnp.zeros((2 * ATT_BLOCK, 128), F32)]
        dv_pairs = [jnp.zeros((2 * ATT_BLOCK, 128), F32), jnp.zeros((2 * ATT_BLOCK, 128), F32)]
        sink_lane = lax.broadcasted_iota(jnp.int32, (8, 128), 1)
        sink_row = lax.broadcasted_iota(jnp.int32, (8, 128), 0)
        dsink = jnp.zeros((8, 128), F32)
        first = 1 - jnp.minimum(n, 1)
        for k in range(N_KV_HEADS):
            k_even, k_odd = _kv_even_odd(kband, k)
            v_even, v_odd = _kv_even_odd(vband, k)
            qs = _stack_pairs(q_ref, k) * ATT_SCALE
            dos = _stack_pairs(do_ref, k)
            p, rdenom, es = _attn_probs(qs, k_even, k_odd, bias_ref[first, k], k, sink_ref)
            probs = p * rdenom
            dp = jnp.concatenate([lax.dot_general(dos, v_even, nt_dims, preferred_element_type=F32),
                                  lax.dot_general(dos, v_odd, nt_dims, preferred_element_type=F32)], axis=0)
            delta = jnp.sum(probs * dp, axis=-1, keepdims=True)
            ds16 = (probs * (dp - delta)).astype(MXU_DTYPE)
            p16 = probs.astype(MXU_DTYPE)
            dsink_rows = -(es * rdenom) * delta
            for r, gq in enumerate(_GROUP_ORDER):
                tot = jnp.sum(dsink_rows[ATT_BLOCK * r:ATT_BLOCK * (r + 1)], axis=0, keepdims=True)
                dsink = dsink + jnp.where((sink_lane == 4 * k + gq) & (sink_row == 0), tot, 0.0)
            dqs = (jnp.dot(ds16[:256], k_even, preferred_element_type=F32)
                   + jnp.dot(ds16[256:], k_odd, preferred_element_type=F32)) * ATT_SCALE
            dq_ref[:, 256 * k:256 * k + 128] = dqs[:128].astype(dq_ref.dtype)
            dq_ref[:, 256 * k + 128:256 * k + 256] = dqs[128:].astype(dq_ref.dtype)
            db_ref[:, 256 * k:256 * k + 128] += jnp.sum(dqs[:128], axis=0, keepdims=True)
            db_ref[:, 256 * k + 128:256 * k + 256] += jnp.sum(dqs[128:], axis=0, keepdims=True)
            pair, parity = divmod(k, 2)
            dk_pairs[pair] = dk_pairs[pair] + to_native(
                lax.dot_general(ds16[:256], qs, tn_dims, preferred_element_type=F32),
                lax.dot_general(ds16[256:], qs, tn_dims, preferred_element_type=F32), parity)
            dv_pairs[pair] = dv_pairs[pair] + to_native(
                lax.dot_general(p16[:256], dos, tn_dims, preferred_element_type=F32),
                lax.dot_general(p16[256:], dos, tn_dims, preferred_element_type=F32), parity)
        dsink_ref[...] += dsink
        dk_band = jnp.concatenate(dk_pairs, axis=1)
        dv_band = jnp.concatenate(dv_pairs, axis=1)
        cur = pl.multiple_of(n * ATT_BLOCK, ATT_BLOCK)
        dk_ref[pl.ds(cur, ATT_BLOCK), :] += dk_band[ATT_BLOCK:]
        dv_ref[pl.ds(cur, ATT_BLOCK), :] += dv_band[ATT_BLOCK:]

        @pl.when(n > 0)
        def _():
            prv = pl.multiple_of((n - 1) * ATT_BLOCK, ATT_BLOCK)
            dk_ref[pl.ds(prv, ATT_BLOCK), :] += dk_band[:ATT_BLOCK]
            dv_ref[pl.ds(prv, ATT_BLOCK), :] += dv_band[:ATT_BLOCK]

        @pl.when(n == nb - 1)
        def _():
            db_ref[:, 1024:1280] = jnp.sum(dk_ref[...], axis=0, keepdims=True)
            db_ref[:, 1280:1536] = jnp.sum(dv_ref[...], axis=0, keepdims=True)

        last()

    whole = lambda shape: pl.BlockSpec(shape, lambda n: (0, 0))
    return _call(
        body, hosted, name="attn_bwd", grid=(nb,), in_specs=_attn_specs(True),
        out_specs=[pl.BlockSpec((ATT_BLOCK, D_MODEL), lambda n: (n, 0)), whole((t, 256)), whole((t, 256)),
                   whole((1, QKV_DIM)), whole((8, 128))],
        out_shape=[jax.ShapeDtypeStruct((t, D_MODEL), MXU_DTYPE), jax.ShapeDtypeStruct((t, 256), F32),
                   jax.ShapeDtypeStruct((t, 256), F32), jax.ShapeDtypeStruct((1, QKV_DIM), F32),
                   jax.ShapeDtypeStruct((8, 128), F32)],
        operands=(sinks, qkv, qkv, qkv, qkv, qkv, do), semantics=("arbitrary",), scratch_shapes=_ATTN_TABLES)


FF_TILE = 1408


def _gathering_tile(j):
    me = 2 * lax.axis_index("x") + lax.axis_index("y")
    return me ^ (((j & 1) << 1) | (j >> 1))


def _ffn_in_gathering_call(hn, w_in, layer, hosted):
    t = hn.shape[0]
    tm = _tile(t, 1024)
    n_i = t // tm
    halves = D_FF // FF_TILE
    assert FF_TILE == W_IN_COLS and 2 * halves == N_CHIPS

    def body(ins, outs, scratch, first, last):
        hn_ref = ins[0]
        u_ref, w_ref = outs
        wbuf, load_sem, ici_send, ici_recv, hand_send, hand_recv = scratch
        j, i = pl.program_id(0), pl.program_id(1)
        x, y, c, chips = _mesh_place()
        me = 2 * x + y
        mover = c == layer

        def shard(k):
            return w_ref.at[layer, :, _col(k, W_IN_COLS)]

        def over_ici(jj, k):
            cx, cy = chips[jj]
            return pltpu.make_async_remote_copy(src_ref=shard(k), dst_ref=shard(k), send_sem=ici_send.at[jj],
                                                recv_sem=ici_recv.at[jj], device_id=(cx, cy, c), device_id_type=MESH)

        def handed(jj):
            cx, cy = chips[jj]
            return pltpu.make_async_remote_copy(
                src_ref=shard(2 * cx + cy), dst_ref=shard(2 * cx + cy), send_sem=hand_send.at[jj],
                recv_sem=hand_recv.at[jj], device_id=(x, y, 1 - c), device_id_type=MESH)

        far = N_CHIPS - 2

        @pl.when((j == 0) & (i == 0) & mover)
        def _():
            over_ici(far, me).start()
            for jj in NEIGHBOURS:
                handed(jj).start()

        first()

        @pl.when((j == far + 1) & (i == 0) & mover)
        def _():
            over_ici(far, 2 * chips[far][0] + chips[far][1]).wait_recv()
            handed(far).start()

        for jj in range(N_CHIPS - 1):
            @pl.when((j == jj + 1) & (i == 0) & jnp.logical_not(mover))
            def _(jj=jj):
                handed(jj).wait_recv()

        @pl.when(i == 0)
        def _():
            load = pltpu.make_async_copy(shard(_gathering_tile(j)), wbuf, load_sem.at[0])
            load.start()
            load.wait()

        u_ref[...] = jnp.dot(hn_ref[...].astype(MXU_DTYPE), wbuf[...].astype(MXU_DTYPE),
                             preferred_element_type=F32).astype(u_ref.dtype)

        @pl.when((j == N_CHIPS - 1) & (i == n_i - 1) & mover)
        def _():
            over_ici(far, me).wait_send()
            for jj in range(N_CHIPS - 1):
                handed(jj).wait_send()

        last()

    def u_index(j, i):
        tile = _gathering_tile(j)
        return tile // halves, i, tile % halves

    three = [pltpu.SemaphoreType.DMA((N_CHIPS - 1,))] * 4
    return _call(
        body, hosted, name=f"ffn{layer}_in", grid=(N_CHIPS, n_i),
        in_specs=[pl.BlockSpec((tm, D_MODEL), lambda j, i: (i, 0)), ANY],
        out_specs=[pl.BlockSpec((None, tm, FF_TILE), u_index), ANY],
        out_shape=[jax.ShapeDtypeStruct((2, t, D_FF), U_DTYPE), jax.ShapeDtypeStruct(w_in.shape, w_in.dtype)],
        operands=(hn, w_in), semantics=("arbitrary", "arbitrary"), aliases={1: 1},
        scratch_shapes=[pltpu.VMEM((D_MODEL, W_IN_COLS), w_in.dtype), pltpu.SemaphoreType.DMA((1,))] + three)


def _ffn_out_loss_call(z, w_out, layer, h, g, target):
    t, f = z.shape
    tm = _tile(t, ROW_TILE)

    def body(z_ref, w_ref, h_ref, g_ref, t_ref, loss_ref, dh_ref, dh16_ref, dg_ref):
        x = h_ref[...] + jnp.dot(z_ref[...], w_ref[...], preferred_element_type=F32)
        gv = g_ref[...]
        err = _rms(x, gv) - t_ref[...]
        dx, dgt = _rms_bwd(x, gv, err * (1.0 / D_MODEL))
        dh_ref[...] = dx
        dh16_ref[...] = dx.astype(dh16_ref.dtype)

        @pl.when(pl.program_id(0) == 0)
        def _():
            dg_ref[...] = jnp.zeros_like(dg_ref)
            loss_ref[...] = jnp.zeros_like(loss_ref)

        dg_ref[...] += jnp.sum(dgt, axis=0, keepdims=True)
        per_token = jnp.mean(err * err, axis=-1, keepdims=True)
        loss_ref[...] += 0.5 * jnp.sum(per_token, axis=0, keepdims=True)

    row = pl.BlockSpec((tm, D_MODEL), lambda i: (i, 0))
    vec = pl.BlockSpec((1, D_MODEL), lambda i: (0, 0))
    return pl.pallas_call(
        body, name=f"ffn{layer}_out_loss", grid=(t // tm,),
        in_specs=[pl.BlockSpec((tm, f), lambda i: (i, 0)), pl.BlockSpec((None, f, D_MODEL), lambda i: (layer, 0, 0)),
                  row, vec, row],
        out_specs=[pl.BlockSpec((1, 1), lambda i: (0, 0)), row, row, vec],
        out_shape=[jax.ShapeDtypeStruct((1, 1), F32), jax.ShapeDtypeStruct((t, D_MODEL), F32),
                   jax.ShapeDtypeStruct((t, D_MODEL), MXU_DTYPE), jax.ShapeDtypeStruct((1, D_MODEL), F32)],
        compiler_params=_params("arbitrary"))(z, w_out, h, g, target)


def _ffn_fwd(hn, h, w, conv_w, conv_b, layer, hooks=None, next_norm=None, loss_head=None, gathers_w_in=False):
    t = hn.shape[0]
    tm = _tile(t, 1024)
    halves = D_FF // FF_TILE
    if gathers_w_in:
        u, w["w_in"] = _ride(hooks, f"in{layer}", lambda hosted: _ffn_in_gathering_call(hn, w["w_in"], layer, hosted))
    else:
        u, = _ride(hooks, f"in{layer}", lambda hosted: _matmul(
            hn, w["w_in"], mode="nn", name=f"ffn{layer}_in", grid=(2 * halves, t // tm, 1),
            a_spec=pl.BlockSpec((tm, D_MODEL), lambda j, i, k: (i, 0)),
            b_spec=pl.BlockSpec((None, D_MODEL, FF_TILE), lambda j, i, k: (layer, 0, j)),
            o_spec=pl.BlockSpec((None, tm, FF_TILE), lambda j, i, k: (j // halves, i, j % halves)),
            out_shape=jax.ShapeDtypeStruct((2, t, D_FF), U_DTYPE), hosted=hosted))
    z, = _ride(hooks, f"glu{layer}", lambda hosted: _conv_glu_fwd_call(u, conv_w, conv_b, f"ffn{layer}_glu",
                                                                       hosted=hosted))
    if loss_head is not None:
        return (*_ffn_out_loss_call(z, w["w_out"], layer, h, *loss_head), u, z)
    outs = _ride(hooks, f"out{layer}", lambda hosted: _mm_nn(z, w["w_out"], f"ffn{layer}_out", b_lead=layer, residual=h,
                                                     tk=D_FF, hosted=hosted, norm_g=next_norm))
    return outs[0], (outs[1] if next_norm is not None else None), u, z


def _ffn_bwd(dh, hn, u, z, w_in, w_out, conv_w, conv_b, layer, g_in_acc, g_out_acc, norm, hooks=None):
    t = hn.shape[0]
    dz, = _ride(hooks, "dz", lambda hosted: _mm_nt(dh, w_out, f"ffn{layer}_dz", b_lead=layer, out_dtype=MXU_DTYPE,
                                                   tn=FF_TILE, hosted=hosted))
    g_out = _mm_tn(z, dh, f"ffn{layer}_dwout", lead=layer, n_lead=2, into=g_out_acc, tm=WGRAD_TILE, tk=t,
                   out_dtype=WIRE_DTYPE)
    du, dcw, dcb = _ride(hooks, "dglu", lambda hosted: _conv_glu_bwd_call(u, dz, conv_w, conv_b, f"ffn{layer}_dglu",
                                                                          hosted=hosted), {"g_out": g_out})
    per_half = D_FF // WGRAD_TILE
    g_in, = _ride(hooks, "dwin", lambda hosted: _matmul(
        hn, du, mode="tn", name=f"ffn{layer}_dwin", grid=(1, 2 * per_half, 1),
        a_spec=pl.BlockSpec((t, D_MODEL), lambda i, j, k: (0, 0)),
        b_spec=pl.BlockSpec((None, t, WGRAD_TILE), lambda i, j, k: (j // per_half, 0, j % per_half)),
        o_spec=pl.BlockSpec((None, D_MODEL, WGRAD_TILE), lambda i, j, k: (layer, 0, j)),
        out_shape=jax.ShapeDtypeStruct((2, D_MODEL, 2 * D_FF), WIRE_DTYPE), into=g_in_acc, hosted=hosted))
    dx, dx16, dg = _ride(hooks, "dhn", lambda hosted: _ffn_dhn_call(du, w_in, layer, norm, hosted),
                         {"g_in": g_in, "g_out": g_out, "dcw": dcw, "dcb": dcb})
    return dx, dx16, dg, g_in, g_out, dcw, dcb


WGRAD_TILE = 256


_NT_DIMS = (((1,), (1,)), ((), ()))


def _dhn_call(name, a, a_spec, w, w_spec, product, t, tm, norm, hosted=None):
    h, g, dres = norm

    def body(ins, outs, scratch, first, last):
        a_ref, w_ref, h_ref, g_ref, dres_ref = ins
        dx_ref, dx16_ref, dg_ref = outs
        first()
        dx, dgt = _rms_bwd(h_ref[...], g_ref[...], product(a_ref, w_ref))
        dx = dres_ref[...] + dx
        dx_ref[...] = dx
        dx16_ref[...] = dx.astype(dx16_ref.dtype)

        @pl.when(pl.program_id(0) == 0)
        def _():
            dg_ref[...] = jnp.zeros_like(dg_ref)

        dg_ref[...] += jnp.sum(dgt, axis=0, keepdims=True)
        last()

    row = pl.BlockSpec((tm, D_MODEL), lambda i: (i, 0))
    vec = pl.BlockSpec((1, D_MODEL), lambda i: (0, 0))
    return _call(
        body, hosted, name=name, grid=(t // tm,), in_specs=[a_spec, w_spec, row, vec, row], out_specs=[row, row, vec],
        out_shape=[jax.ShapeDtypeStruct((t, D_MODEL), F32), jax.ShapeDtypeStruct((t, D_MODEL), MXU_DTYPE),
                   jax.ShapeDtypeStruct((1, D_MODEL), F32)],
        operands=(a, w, h, g, dres), semantics=("arbitrary",))


def _ffn_dhn_call(du, w_in, layer, norm, hosted=None):
    _, t, f = du.shape
    tm = _tile(t, 512)

    def product(du_ref, w_ref):
        return (lax.dot_general(du_ref[0], w_ref[:, :f], _NT_DIMS, preferred_element_type=F32)
                + lax.dot_general(du_ref[1], w_ref[:, f:], _NT_DIMS, preferred_element_type=F32))

    return _dhn_call(f"ffn{layer}_dhn", du, pl.BlockSpec((2, tm, f), lambda i: (0, i, 0)), w_in,
                     pl.BlockSpec((None, D_MODEL, 2 * f), lambda i: (layer, 0, 0)), product, t, tm, norm, hosted)


def _attn_dhn_call(dqkv, w_qkv, norm):
    t, n = dqkv.shape
    tm = _tile(t, 1024)

    def product(a_ref, w_ref):
        return lax.dot_general(a_ref[...], w_ref[...], _NT_DIMS, preferred_element_type=F32)

    return _dhn_call("attn_dhn", dqkv, pl.BlockSpec((tm, n), lambda i: (i, 0)), w_qkv,
                     pl.BlockSpec((D_MODEL, n), lambda i: (0, 0)), product, t, tm, norm)


def _local_step(x, target, w, *, norm_mix, norm_ffn, norm_f, pool_b, pool_scale, b_qkv, sinks, conv_w, conv_b,
                late_gather=None, reducer=None):
    cw = [conv_w[l].reshape(3, 2, D_FF) for l in range(2)]
    cb = [conv_b[l].reshape(2, D_FF) for l in range(2)]
    gm = [norm_mix[l:l + 1] for l in range(2)]
    gf = [norm_ffn[l:l + 1] for l in range(2)]

    w = dict(w)
    fwd_hooks = {}
    if late_gather:
        def gather_hook(stage):
            return (lambda env: late_gather(stage, w)), (lambda got: w.update(zip(_late_names(stage), got)))

        fwd_hooks = {stage: gather_hook(stage) for stage in GATHER_RIDES}

    h1, hn_f0 = _ride(fwd_hooks, "pool_fwd", lambda hosted: _pool_fwd_call(x, gm[0], w["pool_w"], pool_b, pool_scale,
                                                                            gf[0], hosted))
    h2, hn_a, u0, z0 = _ffn_fwd(hn_f0, h1, w, cw[0], cb[0], 0, hooks=fwd_hooks, next_norm=gm[1],
                                gathers_w_in=bool(late_gather))
    qkv = _mm_nn(hn_a, w["w_qkv"], "attn_qkv", bias=b_qkv, out_dtype=MXU_DTYPE, tn=QKV_DIM)
    o, = _ride(fwd_hooks, "attn_fwd", lambda hosted: _attn_fwd_call(qkv, sinks, hosted))
    h3, hn_f1 = _ride(fwd_hooks, "attn_out", lambda hosted: _mm_nn(o, w["w_o"], "attn_out", residual=h2, hosted=hosted,
                                                                   norm_g=gf[1]))
    loss, dh, dh16, d_norm_f, u1, z1 = _ffn_fwd(hn_f1, h3, w, cw[1], cb[1], 1, hooks=fwd_hooks,
                                                loss_head=(norm_f, target))

    rd = reducer
    first, mid, last = _Reducer.FIRST, _Reducer.MID, _Reducer.LAST
    hooks1 = hooks_attn = hooks0 = hooks_pool = None
    if rd:
        hooks1 = {"dhn": (lambda env: rd.swap(first, {"w_in": env["g_in"], "w_out": env["g_out"]}),
                          lambda got: rd.pair_sums(first, got))}
        hooks_attn = {"attn_bwd": (lambda env: rd.scatter(first), lambda got: rd.chip_sums(first, got))}

        def mid_and_last(got):
            rd.shared(mid, got[:len(mid)])
            rd.pair_sums(last, got[len(mid):len(mid) + len(last)])
            rd.small_early_done(got[-1])

        def early_pieces(env):
            return {"b_qkv": d_bqkv, "sinks": d_sinks[0:1], "norm_mix1": d_gm1, "norm_ffn1": d_gf1,
                    "conv_w": jnp.stack([env["dcw"].reshape(3, 2 * D_FF), dcw1.reshape(3, 2 * D_FF)]),
                    "conv_b": jnp.stack([env["dcb"].reshape(2 * D_FF), dcb1.reshape(2 * D_FF)]),
                    "norm_f": d_norm_f, "loss": loss}

        hooks0 = {
            "dz": (lambda env: rd.share(first), lambda got: rd.shared(first, got)),
            "dglu": (lambda env: rd.swap(mid, {"w_out": env["g_out"], "w_qkv": g_qkv, "w_o": g_o4}),
                     lambda got: rd.pair_sums(mid, got)),
            "dwin": (lambda env: rd.scatter(mid), lambda got: rd.chip_sums(mid, got)),
            "dhn": (lambda env: _both(_both(rd.share(mid), rd.swap(last, {"w_in": env["g_in"]})),
                                      rd.small_early(early_pieces(env))), mid_and_last),
        }
        hooks_pool = {"pool_bwd": (lambda env: rd.scatter(last), lambda got: rd.chip_sums(last, got))}
    dh, dh16, d_gf1, g_in, g_out, dcw1, dcb1 = _ffn_bwd(
        dh16, hn_f1, u1, z1, w["w_in"], w["w_out"], cw[1], cb[1], 1, None, None, (h3, gf[1], dh), hooks=hooks1)
    do = _mm_nt(dh16, w["w_o"], "attn_do", out_dtype=MXU_DTYPE)
    g_o = _mm_tn(o, dh16, "attn_dwo", tn=WGRAD_TILE, tk=x.shape[0], out_dtype=WIRE_DTYPE)
    g_o4 = g_o.reshape(N_CHIPS, 2, W_O_ROWS // 2, D_MODEL)
    dq, dk, dv, d_bqkv, d_sinks = _ride(hooks_attn, "attn_bwd", lambda hosted: _attn_bwd_call(qkv, sinks, do, hosted))
    dqkv = jnp.concatenate([dq, dk.astype(MXU_DTYPE), dv.astype(MXU_DTYPE)], axis=1)
    g_qkv = _mm_tn(hn_a, dqkv, "attn_dwqkv", tn=WGRAD_TILE, tk=x.shape[0], out_dtype=WIRE_DTYPE)
    dh, dh16, d_gm1 = _attn_dhn_call(dqkv, w["w_qkv"], (h2, gm[1], dh))
    dh, _, d_gf0, g_in, g_out, dcw0, dcb0 = _ffn_bwd(
        dh16, hn_f0, u0, z0, w["w_in"], w["w_out"], cw[0], cb[0], 0, g_in, g_out, (h1, gf[0], dh), hooks=hooks0)
    grad_x, g_pool, dvec = _ride(hooks_pool, "pool_bwd", lambda hosted: _pool_bwd_call(
        x, dh, gm[0], w["pool_w"], pool_b, pool_scale, hosted))

    grads = {
        "w_in": g_in, "w_out": g_out, "w_qkv": g_qkv, "w_o": g_o, "pool_w": g_pool,
        "pool_b": dvec[0:1], "pool_scale": dvec[1:2], "b_qkv": d_bqkv, "sinks": d_sinks[0:1, :N_HEADS],
        "norm_mix": jnp.concatenate([dvec[2:3], d_gm1], axis=0), "norm_ffn": jnp.concatenate([d_gf0, d_gf1], axis=0),
        "conv_w": jnp.stack([dcw0.reshape(3, 2 * D_FF), dcw1.reshape(3, 2 * D_FF)]),
        "conv_b": jnp.stack([dcb0.reshape(2 * D_FF), dcb1.reshape(2 * D_FF)]), "norm_f": d_norm_f,
    }
    return loss, grad_x, grads


W_IN_COLS = 2 * D_FF // N_CHIPS
W_OUT_ROWS = D_FF // N_CHIPS
QKV_COLS = QKV_DIM // N_CHIPS
W_O_ROWS = D_MODEL // N_CHIPS
POOL_ROWS = POOL_GC // N_CHIPS
BIG = ("w_in", "w_out", "w_qkv", "w_o", "pool_w")


def _mesh_place():
    x, y, c = lax.axis_index("x"), lax.axis_index("y"), lax.axis_index("c")
    chips = [(1 - x, y), (x, 1 - y), (1 - x, 1 - y)]
    return x, y, c, chips


def _col(k, width):
    return pl.ds(pl.multiple_of(k * width, 128), width)


def _row(k, height):
    return pl.ds(pl.multiple_of(k * height, 16), height)


def _full_piece(name, ref, k, h):
    if name == "w_in":
        return ref.at[h, :, _col(k, W_IN_COLS)]
    if name == "w_out":
        return ref.at[h, _row(k, W_OUT_ROWS), :]
    if name == "w_qkv":
        return ref.at[_row(h, D_MODEL // 2), _col(k, QKV_COLS)]
    if name == "w_o":
        return ref.at[_row(2 * k + h, W_O_ROWS // 2), :]
    return ref.at[pl.ds(2 * h, 2), _row(k, POOL_ROWS), :]


def _shard_half(name, ref, h):
    if name in ("w_in", "w_out"):
        return ref.at[h]
    if name == "w_qkv":
        return ref.at[_row(h, D_MODEL // 2), :]
    if name == "w_o":
        return ref.at[_row(h, W_O_ROWS // 2), :]
    return ref.at[pl.ds(2 * h, 2)]


FULL_SHAPES = {"w_in": (2, D_MODEL, 2 * D_FF), "w_out": (2, D_FF, D_MODEL), "w_qkv": (D_MODEL, QKV_DIM),
               "w_o": (D_MODEL, D_MODEL), "pool_w": (4, POOL_GC, POOL_GC)}
SHARD_SHAPES = {"w_in": (2, D_MODEL, W_IN_COLS), "w_out": (2, W_OUT_ROWS, D_MODEL), "w_qkv": (D_MODEL, QKV_COLS),
                "w_o": (W_O_ROWS, D_MODEL), "pool_w": (4, POOL_ROWS, POOL_GC)}
HALF_SHAPES = {"w_in": (D_MODEL, W_IN_COLS), "w_out": (W_OUT_ROWS, D_MODEL), "w_qkv": (D_MODEL // 2, QKV_COLS),
               "w_o": (W_O_ROWS // 2, D_MODEL), "pool_w": (2, POOL_ROWS, POOL_GC)}


def _place_shard(name, shard, me_arr):
    if name == "w_in":
        blk, grid = (1, 256, W_IN_COLS), (2, D_MODEL // 256)
        src, dst = (lambda a, b, me: (a, b, 0)), (lambda a, b, me: (a, b, me[0]))
    elif name == "w_out":
        blk, grid = (1, W_OUT_ROWS, D_MODEL), (2, 1)
        src, dst = (lambda a, b, me: (a, 0, 0)), (lambda a, b, me: (a, me[0], 0))
    elif name == "w_qkv":
        blk, grid = (256, QKV_COLS), (1, D_MODEL // 256)
        src, dst = (lambda a, b, me: (b, 0)), (lambda a, b, me: (b, me[0]))
    elif name == "w_o":
        blk, grid = (W_O_ROWS, D_MODEL), (1, 1)
        src, dst = (lambda a, b, me: (0, 0)), (lambda a, b, me: (me[0], 0))
    else:
        blk, grid = (4, POOL_ROWS, POOL_GC), (1, 1)
        src, dst = (lambda a, b, me: (0, 0, 0)), (lambda a, b, me: (0, me[0], 0))

    def body(me_ref, s_ref, o_ref):
        o_ref[...] = s_ref[...].astype(o_ref.dtype)

    return pl.pallas_call(
        body, name=f"place_{name}",
        grid_spec=pltpu.PrefetchScalarGridSpec(num_scalar_prefetch=1, grid=grid, in_specs=[pl.BlockSpec(blk, src)],
                                               out_specs=pl.BlockSpec(blk, dst)),
        out_shape=jax.ShapeDtypeStruct(FULL_SHAPES[name], WIRE_DTYPE), compiler_params=_params("parallel", "parallel"),
    )(me_arr, shard)


def _when_mover(half, c, fn):
    if half is None:
        fn(c)
    else:
        pl.when(c == half)(lambda: fn(half))


def _when_taker(half, c, fn):
    if half is None:
        fn(1 - c)
    else:
        pl.when(c != half)(lambda: fn(half))


def _remote(ref, sems, t, j, dev):
    send, recv = sems
    return pltpu.make_async_remote_copy(src_ref=ref, dst_ref=ref, send_sem=send.at[t, j], recv_sem=recv.at[t, j],
                                        device_id=dev, device_id_type=MESH)


def _gather_ici(items, refs, sems, finish, peers=(0, 1, 2)):
    x, y, c, chips = _mesh_place()
    me = 2 * x + y
    for t, (nm, half) in enumerate(items):
        def go(h, t=t, nm=nm):
            mine = _full_piece(nm, refs[t], me, h)
            for j, (cx, cy) in enumerate(chips):
                if j not in peers:
                    continue
                if not finish:
                    _remote(mine, sems, t, j, (cx, cy, c)).start()
                else:
                    _remote(_full_piece(nm, refs[t], 2 * cx + cy, h), sems, t, j, (cx, cy, c)).wait_recv()
                    _remote(mine, sems, t, j, (cx, cy, c)).wait_send()

        _when_mover(half, c, go)


def _gather_pass(items, refs, sems, finish):
    x, y, c, chips = _mesh_place()
    sibling = (x, y, 1 - c)
    for t, (nm, half) in enumerate(items):
        def give(h, t=t, nm=nm):
            for j, (cx, cy) in enumerate(chips):
                got = _remote(_full_piece(nm, refs[t], 2 * cx + cy, h), sems, t, j, sibling)
                if not finish:
                    got.start()
                else:
                    got.wait_send()

        def take(h, t=t, nm=nm):
            for j, (cx, cy) in enumerate(chips):
                _remote(_full_piece(nm, refs[t], 2 * cx + cy, h), sems, t, j, sibling).wait_recv()

        _when_mover(half, c, give)
        if finish:
            _when_taker(half, c, take)


def _gather_sems(n_items):
    return [pltpu.SemaphoreType.DMA((n_items, 3)), pltpu.SemaphoreType.DMA((n_items, 3))]


EARLY_ITEMS = (("pool_w", None),)
_LATE_A = (("w_out", 0),)
_LATE_B = (("w_in", 1),)
_LATE_C = (("w_out", 1),)
_LATE_D = (("w_qkv", None), ("w_o", None))
_EARLY_W_IN = (("w_in", 0),)
GATHER_RIDES = {"pool_fwd": ((), _EARLY_W_IN), "in0": ((), _LATE_A), "glu0": (_LATE_A, _LATE_D), "out0": (_LATE_D, ()),
                "attn_fwd": ((), _LATE_B), "attn_out": (_LATE_B, ()), "in1": ((), _LATE_C), "glu1": (_LATE_C, ())}
NEIGHBOURS = (0, 1)
GATHER_PEERS = {"pool_fwd": NEIGHBOURS}


def _names(items):
    return [nm for nm, _ in items]


def _gather_early(full, small):
    items = EARLY_ITEMS
    n = len(items)

    def body(*refs):
        small_in = refs[n]
        outs, small_out = refs[n + 1:2 * n + 1], refs[2 * n + 1]
        scratch = refs[2 * n + 2:]
        ici_sems, pass_sems = scratch[0:2], scratch[2:4]
        local_sem, small_send, small_recv = scratch[4:]
        x, y, c, chips = _mesh_place()
        me = 2 * x + y
        local = pltpu.make_async_copy(small_in, small_out.at[me], local_sem.at[0])
        local.start()
        small_sends = [pltpu.make_async_remote_copy(
            src_ref=small_in, dst_ref=small_out.at[me], send_sem=small_send.at[j], recv_sem=small_recv.at[j],
            device_id=(cx, cy, c), device_id_type=MESH) for j, (cx, cy) in enumerate(chips)]
        for cp in small_sends:
            cp.start()
        _gather_ici(items, outs, ici_sems, finish=False)
        _gather_ici(items, outs, ici_sems, finish=True)
        _gather_pass(items, outs, pass_sems, finish=False)
        _gather_pass(items, outs, pass_sems, finish=True)
        for j, (cx, cy) in enumerate(chips):
            got = small_out.at[2 * cx + cy]
            pltpu.make_async_remote_copy(src_ref=got, dst_ref=got, send_sem=small_send.at[j],
                                         recv_sem=small_recv.at[j], device_id=(cx, cy, c), device_id_type=MESH).wait_recv()
        for cp in small_sends:
            cp.wait_send()
        local.wait()

    arrays = [full[nm] for nm in _names(items)]
    out_shape = [jax.ShapeDtypeStruct(a.shape, a.dtype) for a in arrays]
    out_shape.append(jax.ShapeDtypeStruct((N_CHIPS,) + small.shape, F32))
    res = pl.pallas_call(
        body, name="gather_early", in_specs=[ANY] * (n + 1), out_specs=[ANY] * (n + 1), out_shape=out_shape,
        input_output_aliases={t: t for t in range(n)},
        scratch_shapes=_gather_sems(n) + _gather_sems(n) + [pltpu.SemaphoreType.DMA((1,)), pltpu.SemaphoreType.DMA((3,)),
                                                            pltpu.SemaphoreType.DMA((3,))],
    )(*arrays, small)
    return {**full, **dict(zip(_names(items), res[:n]))}, res[n]


def _late_gather_host(stage, w):
    passing, sending = GATHER_RIDES[stage]
    n_pass = len(passing)

    def run(src, refs, new, sems, finish):
        if passing:
            _gather_pass(passing, refs[:n_pass], sems[:2], finish)
        if sending:
            _gather_ici(sending, refs[n_pass:], sems[-2:], finish, GATHER_PEERS.get(stage, (0, 1, 2)))

    sems = (_gather_sems(n_pass) if passing else []) + (_gather_sems(len(sending)) if sending else [])
    return _Hosted([], [w[nm] for nm in _late_names(stage)], [], sems, functools.partial(run, finish=False),
                   functools.partial(run, finish=True))


def _late_names(stage):
    return [nm for items in GATHER_RIDES[stage] for nm in _names(items)]


def _other_half_src(name, ref, h):
    if name in ("w_in", "w_out"):
        return ref.at[h]
    if name == "w_qkv":
        return ref.at[_row(h, D_MODEL // 2), :]
    if name == "w_o":
        return ref.at[:, pl.ds(h, 1)]
    return ref.at[pl.ds(2 * h, 2)]


SLAB_SHAPES = {"w_in": (D_MODEL, 2 * D_FF), "w_out": (D_FF, D_MODEL), "w_qkv": (D_MODEL // 2, QKV_DIM),
               "w_o": (N_CHIPS, 1, W_O_ROWS // 2, D_MODEL), "pool_w": (2, POOL_GC, POOL_GC)}


def _item_tag(item):
    return item[0] if item[1] is None else f"{item[0]}{item[1]}"


def _reduce_swap(items, grads, slabs, sems, finish):
    x, y, c, _ = _mesh_place()
    send, recv = sems
    for t, (nm, half) in enumerate(items):
        def copy(h, t=t, nm=nm):
            return pltpu.make_async_remote_copy(
                src_ref=_other_half_src(nm, grads[t], h), dst_ref=slabs[t], send_sem=send.at[t], recv_sem=recv.at[t],
                device_id=(x, y, 1 - c), device_id_type=MESH)

        def give(h):
            if finish:
                copy(h).wait_send()
            else:
                copy(h).start()

        if half is None:
            give(1 - c)
            if finish:
                copy(c).wait_recv()
        else:
            pl.when(c != half)(lambda: give(half))
            if finish:
                pl.when(c == half)(lambda: copy(half).wait_recv())


def _reduce_scatter(items, sums, parts, sems, finish):
    _, _, c, chips = _mesh_place()
    send, recv = sems
    for t, (nm, half) in enumerate(items):
        def go(h, t=t, nm=nm):
            for j, (cx, cy) in enumerate(chips):
                cp = pltpu.make_async_remote_copy(
                    src_ref=_slab_piece(nm, sums[t], 2 * cx + cy), dst_ref=parts[t].at[j], send_sem=send.at[t, j],
                    recv_sem=recv.at[t, j], device_id=(cx, cy, c), device_id_type=MESH)
                if finish:
                    cp.wait_recv()
                    cp.wait_send()
                else:
                    cp.start()

        _when_mover(half, c, go)


def _reduce_share(items, shards, sems, finish):
    x, y, c, _ = _mesh_place()
    send, recv = sems
    for t, (nm, half) in enumerate(items):
        def copy(h, t=t, nm=nm):
            part = _shard_half(nm, shards[t], h)
            return pltpu.make_async_remote_copy(src_ref=part, dst_ref=part, send_sem=send.at[t], recv_sem=recv.at[t],
                                                device_id=(x, y, 1 - c), device_id_type=MESH)

        def give(h):
            if finish:
                copy(h).wait_send()
            else:
                copy(h).start()

        _when_mover(half, c, give)
        if finish:
            _when_taker(half, c, lambda h: copy(h).wait_recv())


def _pair_sems(n):
    return [pltpu.SemaphoreType.DMA((n,)), pltpu.SemaphoreType.DMA((n,))]


def _swap_host(items, grads):
    def run(src, arr, new, sems, finish):
        _reduce_swap(items, src, new, sems, finish)

    return _Hosted([grads[nm] for nm, _ in items], [],
                   [jax.ShapeDtypeStruct(SLAB_SHAPES[nm], grads[nm].dtype) for nm, _ in items],
                   _pair_sems(len(items)), functools.partial(run, finish=False), functools.partial(run, finish=True))


def _scatter_host(items, sums):
    def run(src, arr, new, sems, finish):
        _reduce_scatter(items, src, new, sems, finish)

    new = [jax.ShapeDtypeStruct((N_CHIPS - 1,) + HALF_SHAPES[nm], WIRE_DTYPE) for nm, _ in items]
    return _Hosted(list(sums), [], new, _gather_sems(len(items)), functools.partial(run, finish=False),
                   functools.partial(run, finish=True))


def _share_host(items, shards):
    def run(src, arr, new, sems, finish):
        _reduce_share(items, arr, sems, finish)

    return _Hosted([], list(shards), [], _pair_sems(len(items)), functools.partial(run, finish=False),
                   functools.partial(run, finish=True))


def _pair_sum(item, grad, slab, place):
    name, half = item
    shape = SLAB_SHAPES[name]
    hsel = (lambda p: p[1]) if half is None else (lambda p: half)
    if name == "w_in":
        blk = (256, 2 * D_FF)
        grid = (D_MODEL // 256,)
        g_spec = pl.BlockSpec((None,) + blk, lambda i, p: (hsel(p), i, 0))
        s_spec = pl.BlockSpec(blk, lambda i, p: (i, 0))
    elif name == "w_out":
        blk = (W_OUT_ROWS, D_MODEL)
        grid = (N_CHIPS,)
        g_spec = pl.BlockSpec((None,) + blk, lambda i, p: (hsel(p), i, 0))
        s_spec = pl.BlockSpec(blk, lambda i, p: (i, 0))
    elif name == "w_qkv":
        grid = (1,)
        g_spec = pl.BlockSpec(shape, lambda i, p: (hsel(p), 0))
        s_spec = pl.BlockSpec(shape, lambda i, p: (0, 0))
    else:
        grid = (1,)
        g_spec = pl.BlockSpec(shape, lambda i, p: (0, hsel(p), 0, 0))
        s_spec = pl.BlockSpec(shape, lambda i, p: (0, 0, 0, 0))

    def body(p_ref, g_ref, s_ref, o_ref):
        @pl.when(hsel(p_ref) == p_ref[1])
        def _():
            o_ref[...] = (g_ref[...].astype(F32) + s_ref[...].astype(F32)).astype(o_ref.dtype)

    return pl.pallas_call(
        body, name=f"pair_sum_{_item_tag(item)}",
        grid_spec=pltpu.PrefetchScalarGridSpec(num_scalar_prefetch=1, grid=grid, in_specs=[g_spec, s_spec],
                                               out_specs=s_spec),
        out_shape=jax.ShapeDtypeStruct(shape, WIRE_DTYPE), compiler_params=_params("parallel"),
    )(place, grad, slab)


def _slab_piece(name, ref, k):
    if name == "w_in":
        return ref.at[:, _col(k, W_IN_COLS)]
    if name == "w_out":
        return ref.at[_row(k, W_OUT_ROWS), :]
    if name == "w_qkv":
        return ref.at[:, _col(k, QKV_COLS)]
    if name == "w_o":
        return ref.at[k, 0]
    return ref.at[:, _row(k, POOL_ROWS), :]


def _sum_chips(item, slab, parts, place, into=None):
    name, sel = item
    half = HALF_SHAPES[name]
    hsel = (lambda p: p[1]) if sel is None else (lambda p: sel)
    if name == "w_in":
        blk, grid = (256, W_IN_COLS), (D_MODEL // 256,)
        own = pl.BlockSpec(blk, lambda i, p: (i, p[0]))
        got = pl.BlockSpec((N_CHIPS - 1,) + blk, lambda i, p: (0, i, 0))
        out = pl.BlockSpec((None,) + blk, lambda i, p: (hsel(p), i, 0))
    elif name == "w_out":
        grid = (1,)
        own = pl.BlockSpec(half, lambda i, p: (p[0], 0))
        got = pl.BlockSpec((N_CHIPS - 1,) + half, lambda i, p: (0, 0, 0))
        out = pl.BlockSpec((None,) + half, lambda i, p: (hsel(p), 0, 0))
    elif name == "w_qkv":
        grid = (1,)
        own = pl.BlockSpec(half, lambda i, p: (0, p[0]))
        got = pl.BlockSpec((N_CHIPS - 1,) + half, lambda i, p: (0, 0, 0))
        out = pl.BlockSpec(half, lambda i, p: (hsel(p), 0))
    else:
        grid = (1,)
        own = pl.BlockSpec((None, None) + half, lambda i, p: (p[0], 0, 0, 0))
        got = pl.BlockSpec((N_CHIPS - 1,) + half, lambda i, p: (0, 0, 0))
        out = pl.BlockSpec(half, lambda i, p: (hsel(p), 0))

    def body(p_ref, own_ref, got_ref, *rest):
        o_ref = rest[-1]

        @pl.when(hsel(p_ref) == p_ref[1])
        def _():
            acc = own_ref[...].astype(F32)
            for j in range(N_CHIPS - 1):
                acc = acc + got_ref[j].astype(F32)
            o_ref[...] = acc

    in_specs, operands, aliases = [own, got], [place, slab, parts], {}
    if into is not None:
        in_specs.append(ANY)
        operands.append(into)
        aliases = {3: 0}
    return pl.pallas_call(
        body, name=f"sum_chips_{_item_tag(item)}",
        grid_spec=pltpu.PrefetchScalarGridSpec(num_scalar_prefetch=1, grid=grid, in_specs=in_specs, out_specs=out),
        out_shape=jax.ShapeDtypeStruct(SHARD_SHAPES[name], F32), input_output_aliases=aliases,
        compiler_params=_params("parallel"),
    )(*operands)


N_DEV = 8


def _peer(d):
    x, y, c, _ = _mesh_place()
    px, py, pc = x ^ (d >> 2), y ^ ((d >> 1) & 1), c ^ (d & 1)
    return (px, py, pc), 4 * px + 2 * py + pc


def _small_all_reduce_host(vec):
    rows, cols = vec.shape

    def run(src, arr, new, scratch, finish):
        slots, stage, send, recv, local = scratch
        x, y, c, _ = _mesh_place()
        me = 4 * x + 2 * y + c
        mine = pltpu.make_async_copy(src[0], slots.at[me], local.at[0])
        copies = []
        for d in range(1, N_DEV):
            dev, _ = _peer(d)
            copies.append(pltpu.make_async_remote_copy(src_ref=src[0], dst_ref=slots.at[me], send_sem=send.at[d - 1],
                                                       recv_sem=recv.at[d - 1], device_id=dev, device_id_type=MESH))
        if not finish:
            mine.start()
            for cp in copies:
                cp.start()
            return
        for d in range(1, N_DEV):
            _, flat = _peer(d)
            got = slots.at[flat]
            pltpu.make_async_remote_copy(src_ref=got, dst_ref=got, send_sem=send.at[d - 1], recv_sem=recv.at[d - 1],
                                         device_id=(x, y, c), device_id_type=MESH).wait_recv()
        for cp in copies:
            cp.wait_send()
        mine.wait()
        acc = slots[0]
        for d in range(1, N_DEV):
            acc = acc + slots[d]
        stage[...] = acc
        out = pltpu.make_async_copy(stage, new[0], local.at[0])
        out.start()
        out.wait()

    scratch = [pltpu.VMEM((N_DEV, rows, cols), F32), pltpu.VMEM((rows, cols), F32),
               pltpu.SemaphoreType.DMA((N_DEV - 1,)), pltpu.SemaphoreType.DMA((N_DEV - 1,)),
               pltpu.SemaphoreType.DMA((1,))]
    return _Hosted([vec], [], [jax.ShapeDtypeStruct(vec.shape, F32)], scratch, functools.partial(run, finish=False),
                   functools.partial(run, finish=True))


class _Reducer:
    FIRST = (("w_in", 1), ("w_out", 1))
    MID = (("w_out", 0), ("w_qkv", None), ("w_o", None))
    LAST = (("w_in", 0),)

    def __init__(self, place):
        self.place = place
        self.shards = {}
        self.state = {}

    def swap(self, items, grads):
        self.state[items] = {"grads": [grads[nm] for nm, _ in items]}
        return _swap_host(items, grads)

    def pair_sums(self, items, slabs):
        st = self.state[items]
        st["sums"] = [_pair_sum(it, g, s, self.place) for it, g, s in zip(items, st["grads"], slabs)]

    def scatter(self, items):
        return _scatter_host(items, self.state[items]["sums"])

    def chip_sums(self, items, parts):
        for it, s, p in zip(items, self.state[items]["sums"], parts):
            self.shards[it[0]] = _sum_chips(it, s, p, self.place, into=self.shards.get(it[0]))

    def share(self, items):
        return _share_host(items, [self.shards[nm] for nm, _ in items])

    def shared(self, items, updated):
        self.shards.update(dict(zip(_names(items), updated)))

    def small_early(self, pieces):
        return _small_all_reduce_host(_pack_small(pieces, EARLY_SMALL))

    def small_early_done(self, total):
        self.early_small = _unpack_small(total, EARLY_SMALL)


def _all_reduce_small(vec, g_pool, share_items, shards):
    n_share = len(share_items)

    def body(v_ref, gp_ref, *rest):
        o_ref, po_ref = rest[n_share:n_share + 2]
        shard_refs = rest[n_share + 2:2 * n_share + 2]
        (slots, pslots, send_sems, recv_sems, psend, precv, hsend, hrecv, share_send,
         share_recv) = rest[2 * n_share + 2:]
        x, y, c, _ = _mesh_place()
        me = 4 * x + 2 * y + c
        _reduce_share(share_items, shard_refs, (share_send, share_recv), finish=False)

        def pool_piece(chip, h):
            return gp_ref.at[pl.ds(2 * h, 2), pl.ds(pl.multiple_of(chip * POOL_ROWS, POOL_ROWS), POOL_ROWS), :]

        slots[me] = v_ref[...]
        pslots[me] = pool_piece(2 * x + y, c)[...]
        copies = []
        for d in range(1, N_DEV):
            peer = (x ^ (d >> 2), y ^ ((d >> 1) & 1), c ^ (d & 1))
            copies.append(pltpu.make_async_remote_copy(
                src_ref=v_ref, dst_ref=slots.at[me], send_sem=send_sems.at[d - 1], recv_sem=recv_sems.at[d - 1],
                device_id=peer, device_id_type=MESH))
            copies.append(pltpu.make_async_remote_copy(
                src_ref=pool_piece(2 * peer[0] + peer[1], peer[2]), dst_ref=pslots.at[me], send_sem=psend.at[d - 1],
                recv_sem=precv.at[d - 1], device_id=peer, device_id_type=MESH))
        for cp in copies:
            cp.start()
        for d in range(1, N_DEV):
            peer = 4 * (x ^ (d >> 2)) + 2 * (y ^ ((d >> 1) & 1)) + (c ^ (d & 1))
            for buf, ss, rs in ((slots, send_sems, recv_sems), (pslots, psend, precv)):
                got = buf.at[peer]
                pltpu.make_async_remote_copy(src_ref=got, dst_ref=got, send_sem=ss.at[d - 1], recv_sem=rs.at[d - 1],
                                             device_id=(x, y, c), device_id_type=MESH).wait_recv()
        for cp in copies:
            cp.wait_send()
        acc, pacc = slots[0], pslots[0]
        for d in range(1, N_DEV):
            acc, pacc = acc + slots[d], pacc + pslots[d]
        o_ref[...] = acc
        po_ref[pl.ds(2 * c, 2)] = pacc
        mine, theirs = po_ref.at[pl.ds(2 * c, 2)], po_ref.at[pl.ds(2 * (1 - c), 2)]
        give = pltpu.make_async_remote_copy(src_ref=mine, dst_ref=mine, send_sem=hsend.at[0], recv_sem=hrecv.at[0],
                                            device_id=(x, y, 1 - c), device_id_type=MESH)
        give.start()
        pltpu.make_async_remote_copy(src_ref=theirs, dst_ref=theirs, send_sem=hsend.at[0], recv_sem=hrecv.at[0],
                                     device_id=(x, y, 1 - c), device_id_type=MESH).wait_recv()
        give.wait_send()
        _reduce_share(share_items, shard_refs, (share_send, share_recv), finish=True)

    vm = pl.BlockSpec(memory_space=pltpu.VMEM)
    piece = (2, POOL_ROWS, POOL_GC)
    return pl.pallas_call(
        body, name="all_reduce_small", in_specs=[vm, vm] + [ANY] * n_share, out_specs=[vm, vm] + [ANY] * n_share,
        out_shape=[jax.ShapeDtypeStruct(vec.shape, F32), jax.ShapeDtypeStruct(SHARD_SHAPES["pool_w"], F32)]
        + [jax.ShapeDtypeStruct(a.shape, a.dtype) for a in shards],
        input_output_aliases={2 + t: 2 + t for t in range(n_share)},
        scratch_shapes=[pltpu.VMEM((N_DEV,) + vec.shape, F32), pltpu.VMEM((N_DEV,) + piece, F32)]
        + [pltpu.SemaphoreType.DMA((N_DEV - 1,))] * 4 + [pltpu.SemaphoreType.DMA((1,))] * 2 + _pair_sems(n_share),
    )(vec, g_pool, *shards)


def _adamw_call(w, g, m, v, name, copy_grad=False):
    shape = w.shape
    cols = shape[-1]
    rows = w.size // cols
    tr = rows
    for cand in (256, 128, 64, 32, 16, 8):
        if rows > cand and rows % cand == 0 and cand * cols * 4 <= 2 * 1024 * 1024:
            tr = cand
            break

    def body(w_ref, g_ref, m_ref, v_ref, d_ref, nm_ref, nv_ref, *g_out):
        gv = g_ref[...]
        if copy_grad:
            g_out[0][...] = gv
        mn = ADAM_B1 * m_ref[...] + (1.0 - ADAM_B1) * gv
        vn = ADAM_B2 * v_ref[...] + (1.0 - ADAM_B2) * jnp.square(gv)
        m_hat = mn / (1.0 - ADAM_B1 ** ADAM_STEP)
        v_hat = vn / (1.0 - ADAM_B2 ** ADAM_STEP)
        d_ref[...] = -ADAM_LR * (m_hat / (jnp.sqrt(v_hat) + ADAM_EPS) + ADAM_WD * w_ref[...])
        nm_ref[...] = mn
        nv_ref[...] = vn

    spec = pl.BlockSpec((tr, cols), lambda i: (i, 0))
    flat = lambda a: a.reshape(rows, cols)
    n_out = 4 if copy_grad else 3
    outs = pl.pallas_call(
        body, name=name, grid=(rows // tr,), in_specs=[spec] * 4, out_specs=[spec] * n_out,
        out_shape=[jax.ShapeDtypeStruct((rows, cols), F32)] * n_out, compiler_params=_params("parallel"),
    )(flat(w), flat(g), flat(m), flat(v))
    return tuple(o.reshape(shape) for o in outs)


WEIGHTS = ("pool_w", "pool_b", "pool_scale", "attn_w_qkv", "attn_b_qkv", "attn_sinks", "attn_w_o", "norm_mix",
           "norm_ffn", "ffn_w_in", "ffn_conv_w", "ffn_conv_b", "ffn_w_out", "norm_f")
SMALL_GATHER = 128 * 71
EARLY_SMALL = ((("b_qkv", 1536), ("sinks", 128), ("norm_mix1", 1024), ("norm_ffn1", 1024), ("conv_w", 6 * 2 * D_FF),
                ("conv_b", 4 * D_FF), ("norm_f", 1024), ("loss", 128)), 6272)
LATE_SMALL = ((("pool_b", 1024), ("pool_scale", 1024), ("norm_mix0", 1024), ("norm_ffn0", 1024)), 512)


def _pack_small(pieces, layout):
    sizes, cols = layout
    flat = jnp.concatenate([jnp.pad(pieces[nm].reshape(-1), (0, size - pieces[nm].size)) for nm, size in sizes])
    return jnp.pad(flat, (0, 8 * cols - flat.size)).reshape(8, cols)


def _unpack_small(vec, layout):
    flat = vec.reshape(-1)
    out, off = {}, 0
    for nm, size in layout[0]:
        out[nm] = flat[off:off + size]
        off += size
    return out


def kernel(x, pool_w, pool_b, pool_scale, attn_w_qkv, attn_b_qkv, attn_sinks, attn_w_o, norm_mix, norm_ffn, ffn_w_in, ffn_conv_w, ffn_conv_b, ffn_w_out, norm_f, loss_target, m_pool_w, m_pool_b, m_pool_scale, m_attn_w_qkv, m_attn_b_qkv, m_attn_sinks, m_attn_w_o, m_norm_mix, m_norm_ffn, m_ffn_w_in, m_ffn_conv_w, m_ffn_conv_b, m_ffn_w_out, m_norm_f, v_pool_w, v_pool_b, v_pool_scale, v_attn_w_qkv, v_attn_b_qkv, v_attn_sinks, v_attn_w_o, v_norm_mix, v_norm_ffn, v_ffn_w_in, v_ffn_conv_w, v_ffn_conv_b, v_ffn_w_out, v_norm_f):
    weights = dict(pool_w=pool_w, pool_b=pool_b, pool_scale=pool_scale, attn_w_qkv=attn_w_qkv, attn_b_qkv=attn_b_qkv,
                   attn_sinks=attn_sinks, attn_w_o=attn_w_o, norm_mix=norm_mix, norm_ffn=norm_ffn, ffn_w_in=ffn_w_in,
                   ffn_conv_w=ffn_conv_w, ffn_conv_b=ffn_conv_b, ffn_w_out=ffn_w_out, norm_f=norm_f)
    m_in = dict(pool_w=m_pool_w, pool_b=m_pool_b, pool_scale=m_pool_scale, attn_w_qkv=m_attn_w_qkv,
                attn_b_qkv=m_attn_b_qkv, attn_sinks=m_attn_sinks, attn_w_o=m_attn_w_o, norm_mix=m_norm_mix,
                norm_ffn=m_norm_ffn, ffn_w_in=m_ffn_w_in, ffn_conv_w=m_ffn_conv_w, ffn_conv_b=m_ffn_conv_b,
                ffn_w_out=m_ffn_w_out, norm_f=m_norm_f)
    v_in = dict(pool_w=v_pool_w, pool_b=v_pool_b, pool_scale=v_pool_scale, attn_w_qkv=v_attn_w_qkv,
                attn_b_qkv=v_attn_b_qkv, attn_sinks=v_attn_sinks, attn_w_o=v_attn_w_o, norm_mix=v_norm_mix,
                norm_ffn=v_norm_ffn, ffn_w_in=v_ffn_w_in, ffn_conv_w=v_ffn_conv_w, ffn_conv_b=v_ffn_conv_b,
                ffn_w_out=v_ffn_w_out, norm_f=v_norm_f)
    chip = 2 * lax.axis_index("x") + lax.axis_index("y")
    core = lax.axis_index("c")

    place = jnp.stack([chip, core]).astype(jnp.int32)
    shards = {"w_in": ffn_w_in, "w_out": ffn_w_out, "w_qkv": attn_w_qkv[0], "w_o": attn_w_o[0], "pool_w": pool_w[0]}
    placed = {nm: _place_shard(nm, shards[nm], place) for nm in BIG}
    small = jnp.concatenate([pool_b.reshape(-1), attn_b_qkv.reshape(-1), ffn_conv_w.reshape(-1)]).reshape(1, SMALL_GATHER)
    full, small_all = _gather_early(placed, small)
    small_all = small_all.reshape(N_CHIPS, SMALL_GATHER)
    pool_b_full = small_all[:, :256].reshape(N_CHIPS, 4, POOL_ROWS).transpose(1, 0, 2).reshape(1, D_MODEL)
    b_qkv_full = small_all[:, 256:640].reshape(1, QKV_DIM)
    conv_w_full = small_all[:, 640:].reshape(N_CHIPS, 2, 3, W_IN_COLS).transpose(1, 2, 0, 3).reshape(2, 3, 2 * D_FF)

    reducer = _Reducer(place)
    _, grad_x, grads = _local_step(
        x[0], loss_target[0], full, norm_mix=norm_mix, norm_ffn=norm_ffn, norm_f=norm_f.reshape(1, D_MODEL),
        pool_b=pool_b_full, pool_scale=pool_scale, b_qkv=b_qkv_full, sinks=attn_sinks[0], conv_w=conv_w_full,
        conv_b=ffn_conv_b, late_gather=_late_gather_host, reducer=reducer)
    reduced = dict(reducer.shards)

    late = {"pool_b": grads["pool_b"], "pool_scale": grads["pool_scale"], "norm_mix0": grads["norm_mix"][0],
            "norm_ffn0": grads["norm_ffn"][0]}
    last = _Reducer.LAST
    late_sum, reduced["pool_w"], *shared = _all_reduce_small(_pack_small(late, LATE_SMALL), grads["pool_w"], last,
                                                             [reduced[nm] for nm in _names(last)])
    reduced.update(zip(_names(last), shared))
    small_g = {**reducer.early_small, **_unpack_small(late_sum, LATE_SMALL)}
    small_g["norm_mix"] = jnp.concatenate([small_g["norm_mix0"], small_g["norm_mix1"]])
    small_g["norm_ffn"] = jnp.concatenate([small_g["norm_ffn0"], small_g["norm_ffn1"]])
    loss = small_g["loss"][0]
    pool_b_g = lax.dynamic_slice_in_dim(small_g["pool_b"].reshape(4, N_CHIPS, POOL_ROWS), chip, 1, axis=1)
    b_qkv_g = lax.dynamic_slice_in_dim(small_g["b_qkv"].reshape(N_CHIPS, QKV_COLS), chip, 1, axis=0)
    conv_w_g = lax.dynamic_slice_in_dim(small_g["conv_w"].reshape(2, 3, N_CHIPS, W_IN_COLS), chip, 1, axis=2)
    grad_w = {
        "pool_w": reduced["pool_w"].reshape(pool_w.shape), "pool_b": pool_b_g.reshape(pool_b.shape),
        "pool_scale": small_g["pool_scale"].reshape(pool_scale.shape),
        "attn_w_qkv": reduced["w_qkv"].reshape(attn_w_qkv.shape), "attn_b_qkv": b_qkv_g.reshape(attn_b_qkv.shape),
        "attn_sinks": small_g["sinks"][:N_HEADS].reshape(attn_sinks.shape),
        "attn_w_o": reduced["w_o"].reshape(attn_w_o.shape), "norm_mix": small_g["norm_mix"].reshape(norm_mix.shape),
        "norm_ffn": small_g["norm_ffn"].reshape(norm_ffn.shape), "ffn_w_in": reduced["w_in"],
        "ffn_conv_w": conv_w_g.reshape(ffn_conv_w.shape), "ffn_conv_b": small_g["conv_b"].reshape(ffn_conv_b.shape),
        "ffn_w_out": reduced["w_out"], "norm_f": small_g["norm_f"].reshape(norm_f.shape),
    }

    delta, new_m, new_v = {}, {}, {}
    exchanged = ("pool_w", "attn_w_qkv", "attn_w_o", "ffn_w_in", "ffn_w_out")
    for nm in WEIGHTS:
        res = _adamw_call(weights[nm], grad_w[nm], m_in[nm], v_in[nm], f"adamw_{nm}", copy_grad=nm in exchanged)
        delta[nm], new_m[nm], new_v[nm] = res[:3]
        if nm in exchanged:
            grad_w[nm] = res[3]
    return (loss, grad_x.reshape(x.shape), *[grad_w[nm] for nm in WEIGHTS], *[delta[nm] for nm in WEIGHTS],
            *[new_m[nm] for nm in WEIGHTS], *[new_v[nm] for nm in WEIGHTS])
```

```python
import functools
from typing import Callable, NamedTuple

import jax
import jax.numpy as jnp
from jax import lax
from jax.experimental import pallas as pl
from jax.experimental.pallas import tpu as pltpu

F32 = jnp.float32
MXU_DTYPE = jnp.bfloat16
WIRE_DTYPE = jnp.bfloat16
U_DTYPE = jnp.float32

D_MODEL = 1024
D_FF = 2816
QKV_DIM = 1536
HEAD_DIM = 64
N_HEADS = 16
N_KV_HEADS = 4
ATT_BLOCK = 128
POOL_WINDOWS = (2, 4, 8, 16)
POOL_GC = 256
POOL_HALO = 16
CONV_HALO = 8
RMS_EPS = 1e-6
ATT_SCALE = HEAD_DIM ** -0.5
ALIBI_SLOPES = tuple(2.0 ** (-8.0 / N_HEADS * (h + 1)) for h in range(N_HEADS))

ADAM_LR = 0.001
ADAM_B1 = 0.9
ADAM_B2 = 0.999
ADAM_EPS = 1e-08
ADAM_WD = 0.01
ADAM_STEP = 10

N_CHIPS = 4
MESH = pl.DeviceIdType.MESH
VMEM_LIMIT_BYTES = 56 * 1024 * 1024
ANY = pl.BlockSpec(memory_space=pl.ANY)


def _params(*semantics):
    return pltpu.CompilerParams(dimension_semantics=semantics, vmem_limit_bytes=VMEM_LIMIT_BYTES)


def _rms(x, g):
    return x * lax.rsqrt(jnp.mean(x * x, axis=-1, keepdims=True) + RMS_EPS) * g


def _rms_bwd(x, g, dy):
    rstd = lax.rsqrt(jnp.mean(x * x, axis=-1, keepdims=True) + RMS_EPS)
    xhat = x * rstd
    dxhat = dy * g
    dx = rstd * (dxhat - xhat * jnp.mean(dxhat * xhat, axis=-1, keepdims=True))
    return dx, dy * xhat


def _shift_down(x, s):
    return pltpu.roll(x, s, axis=0)


def _shift_up(x, s):
    return pltpu.roll(x, x.shape[0] - s, axis=0)


def _sigmoid(x):
    return 1.0 / (1.0 + jnp.exp(-x))


_DOT_DIMS = {"nn": ((1,), (0,)), "nt": ((1,), (1,)), "tn": ((0,), (0,))}


class _Hosted(NamedTuple):
    sources: list
    arrays: list
    new: list
    sems: list
    first: Callable
    last: Callable


def _call(body, hosted, *, name, grid, in_specs, out_specs, out_shape, operands, semantics, scratch_shapes=(),
          aliases=None):
    n_in, n_out, n_scr = len(in_specs), len(out_specs), len(scratch_shapes)
    aliases = dict(aliases or {})
    if hosted is None:
        def plain(*refs):
            body(refs[:n_in], refs[n_in:n_in + n_out], refs[n_in + n_out:], lambda: None, lambda: None)

        return pl.pallas_call(plain, name=name, grid=grid, in_specs=list(in_specs), out_specs=list(out_specs),
                              out_shape=list(out_shape), scratch_shapes=list(scratch_shapes),
                              input_output_aliases=aliases, compiler_params=_params(*semantics))(*operands)
    ns, na, nn = len(hosted.sources), len(hosted.arrays), len(hosted.new)

    def carrying(*refs):
        ins, src = refs[:n_in], refs[n_in:n_in + ns]
        o0 = n_in + ns + na
        outs, arr = refs[o0:o0 + n_out], refs[o0 + n_out:o0 + n_out + na]
        new = refs[o0 + n_out + na:o0 + n_out + na + nn]
        s0 = o0 + n_out + na + nn
        scratch, sems = refs[s0:s0 + n_scr], refs[s0 + n_scr:]
        ids = [pl.program_id(d) for d in range(len(grid))]
        is_first = functools.reduce(lambda p, q: p & q, [i == 0 for i in ids])
        is_last = functools.reduce(lambda p, q: p & q, [i == n - 1 for i, n in zip(ids, grid)])
        body(ins, outs, scratch, lambda: pl.when(is_first)(lambda: hosted.first(src, arr, new, sems)),
             lambda: pl.when(is_last)(lambda: hosted.last(src, arr, new, sems)))

    for t in range(na):
        aliases[n_in + ns + t] = n_out + t
    return pl.pallas_call(
        carrying, name=name, grid=grid, in_specs=list(in_specs) + [ANY] * (ns + na),
        out_specs=list(out_specs) + [ANY] * (na + nn),
        out_shape=list(out_shape) + [jax.ShapeDtypeStruct(a.shape, a.dtype) for a in hosted.arrays] + list(hosted.new),
        scratch_shapes=list(scratch_shapes) + list(hosted.sems), input_output_aliases=aliases,
        compiler_params=_params(*["arbitrary"] * len(grid)))(*operands, *hosted.sources, *hosted.arrays)


def _both(h1, h2):
    def run(which, src, arr, new, sems):
        cut = lambda seq, n: (seq[:n], seq[n:])
        (s1, s2), (a1, a2) = cut(src, len(h1.sources)), cut(arr, len(h1.arrays))
        (n1, n2), (m1, m2) = cut(new, len(h1.new)), cut(sems, len(h1.sems))
        getattr(h1, which)(s1, a1, n1, m1)
        getattr(h2, which)(s2, a2, n2, m2)

    return _Hosted(h1.sources + h2.sources, h1.arrays + h2.arrays, h1.new + h2.new, h1.sems + h2.sems,
                   functools.partial(run, "first"), functools.partial(run, "last"))


def _ride(hooks, stage, call, env=None):
    if not hooks or stage not in hooks:
        res = call(None)
        return list(res) if isinstance(res, (list, tuple)) else [res]
    make, done = hooks[stage]
    hosted = make(env)
    res = call(hosted)
    own = len(res) - len(hosted.arrays) - len(hosted.new)
    done(res[own:])
    return list(res[:own])


def _matmul(a, b, *, mode, name, grid, a_spec, b_spec, o_spec, out_shape, bias=None, bias_spec=None, residual=None,
            residual_spec=None, into=None, hosted=None, norm_g=None):
    nk = grid[2]
    dims = (_DOT_DIMS[mode], ((), ()))
    acc_shape = tuple(d for d in o_spec.block_shape if d is not None)

    def body(ins, outs, scratch, first, last):
        a_ref, b_ref = ins[0], ins[1]
        pos = 2
        bias_ref = res_ref = None
        if bias is not None:
            bias_ref = ins[pos]
            pos += 1
        if residual is not None:
            res_ref = ins[pos]
            pos += 1
        o_ref = outs[0]
        acc_ref = scratch[0] if nk > 1 else None
        first()
        prod = lax.dot_general(a_ref[...].astype(MXU_DTYPE), b_ref[...].astype(MXU_DTYPE), dims,
                               preferred_element_type=F32)

        def finish(acc):
            if bias_ref is not None:
                acc = acc + bias_ref[...]
            if res_ref is not None:
                acc = acc + res_ref[...]
            o_ref[...] = acc.astype(o_ref.dtype)
            if norm_g is not None:
                outs[1][...] = _rms(acc, ins[pos][...]).astype(outs[1].dtype)

        if nk == 1:
            finish(prod)
        else:
            k = pl.program_id(2)

            @pl.when(k == 0)
            def _():
                acc_ref[...] = prod

            @pl.when(k > 0)
            def _():
                acc_ref[...] += prod

            @pl.when(k == nk - 1)
            def _():
                finish(acc_ref[...])

        last()

    operands, in_specs = [a, b], [a_spec, b_spec]
    if bias is not None:
        operands.append(bias)
        in_specs.append(bias_spec)
    if residual is not None:
        operands.append(residual)
        in_specs.append(residual_spec)
    out_specs, out_shapes = [o_spec], [out_shape]
    if norm_g is not None:
        assert acc_shape[-1] == norm_g.shape[-1], "the fused norm needs whole rows in an output tile"
        operands.append(norm_g)
        in_specs.append(bias_spec)
        out_specs.append(o_spec)
        out_shapes.append(jax.ShapeDtypeStruct(out_shape.shape, MXU_DTYPE))
    aliases = {}
    if into is not None:
        aliases = {len(operands): 0}
        operands.append(into)
        in_specs.append(ANY)
    res = _call(body, hosted, name=name, grid=grid, in_specs=in_specs, out_specs=out_specs, out_shape=out_shapes,
                operands=operands, semantics=("parallel", "parallel", "arbitrary"),
                scratch_shapes=[pltpu.VMEM(acc_shape, F32)] if nk > 1 else [], aliases=aliases)
    return res if hosted or norm_g is not None else res[0]


def _tile(n, want):
    t = min(n, want)
    assert n % t == 0, (n, want)
    return t


def _mm_nn(a, b, name, *, b_lead=None, bias=None, residual=None, out_dtype=F32, tm=1024, tn=1024, tk=1024,
           hosted=None, norm_g=None):
    (m, k), n = a.shape, b.shape[-1]
    tm, tn, tk = _tile(m, tm), _tile(n, tn), _tile(k, tk)
    if b_lead is None:
        b_spec = pl.BlockSpec((tk, tn), lambda i, j, kk: (kk, j))
    else:
        b_spec = pl.BlockSpec((None, tk, tn), lambda i, j, kk: (b_lead, kk, j))
    return _matmul(
        a, b, mode="nn", name=name, grid=(m // tm, n // tn, k // tk),
        a_spec=pl.BlockSpec((tm, tk), lambda i, j, kk: (i, kk)), b_spec=b_spec,
        o_spec=pl.BlockSpec((tm, tn), lambda i, j, kk: (i, j)), out_shape=jax.ShapeDtypeStruct((m, n), out_dtype),
        bias=bias, bias_spec=pl.BlockSpec((1, tn), lambda i, j, kk: (0, j)),
        residual=residual, residual_spec=pl.BlockSpec((tm, tn), lambda i, j, kk: (i, j)), hosted=hosted,
        norm_g=norm_g)


def _mm_nt(a, b, name, *, b_lead=None, out_dtype=F32, tm=1024, tn=1024, tk=1024, hosted=None):
    (m, k), n = a.shape, b.shape[-2]
    tm, tn, tk = _tile(m, tm), _tile(n, tn), _tile(k, tk)
    if b_lead is None:
        b_spec = pl.BlockSpec((tn, tk), lambda i, j, kk: (j, kk))
    else:
        b_spec = pl.BlockSpec((None, tn, tk), lambda i, j, kk: (b_lead, j, kk))
    return _matmul(
        a, b, mode="nt", name=name, grid=(m // tm, n // tn, k // tk),
        a_spec=pl.BlockSpec((tm, tk), lambda i, j, kk: (i, kk)), b_spec=b_spec,
        o_spec=pl.BlockSpec((tm, tn), lambda i, j, kk: (i, j)), out_shape=jax.ShapeDtypeStruct((m, n), out_dtype),
        hosted=hosted)


def _mm_tn(a, b, name, *, lead=None, n_lead=None, into=None, tm=1024, tn=1024, tk=1024, out_dtype=F32):
    (k, m), n = a.shape, b.shape[-1]
    tm, tn, tk = _tile(m, tm), _tile(n, tn), _tile(k, tk)
    if lead is None:
        o_spec = pl.BlockSpec((tm, tn), lambda i, j, kk: (i, j))
        out_shape = jax.ShapeDtypeStruct((m, n), out_dtype)
    else:
        o_spec = pl.BlockSpec((None, tm, tn), lambda i, j, kk: (lead, i, j))
        out_shape = jax.ShapeDtypeStruct((n_lead, m, n), out_dtype)
    return _matmul(
        a, b, mode="tn", name=name, grid=(m // tm, n // tn, k // tk),
        a_spec=pl.BlockSpec((tk, tm), lambda i, j, kk: (kk, i)),
        b_spec=pl.BlockSpec((tk, tn), lambda i, j, kk: (kk, j)), o_spec=o_spec, out_shape=out_shape, into=into)


ROW_TILE = 512


def _pool_windows_causal(ext, first_row):
    tt = ext.shape[0] - POOL_HALO
    t = first_row + lax.broadcasted_iota(jnp.int32, (tt, 1), 0)
    outs = []
    for gi, win in enumerate(POOL_WINDOWS):
        cols = slice(gi * POOL_GC, (gi + 1) * POOL_GC)
        s = ext[:, cols]
        sh = 1
        while sh < win:
            s = s + _shift_down(s, sh)
            sh *= 2
        count = jnp.minimum(t + 1, win).astype(F32)
        outs.append(s[POOL_HALO:] / count - ext[POOL_HALO:, cols])
    return outs


def _prev_halo_spec(tt, halo, width):
    per = tt // halo
    return pl.BlockSpec((halo, width), lambda i: (jnp.maximum(i * per - 1, 0), 0))


def _pool_fwd_call(h0, g_mix, pool_w, pool_b, pool_scale, g_ffn, hosted=None):
    t = h0.shape[0]
    tt = _tile(t, ROW_TILE)

    def body(ins, outs, scratch, first, last):
        h_ref, halo_ref, gm_ref, w_ref, b_ref, sc_ref, gf_ref = ins
        h1_ref, hn_ref = outs
        i = pl.program_id(0)
        first()
        x, gm = h_ref[...], gm_ref[...]
        top = jnp.where(i > 0, _rms(halo_ref[...], gm), 0.0)
        ext = jnp.concatenate([top, _rms(x, gm)], axis=0)
        ps = _pool_windows_causal(ext, i * tt)
        ys = [jnp.dot(p.astype(MXU_DTYPE), w_ref[gi], preferred_element_type=F32) for gi, p in enumerate(ps)]
        mix = (jnp.concatenate(ys, axis=1) + b_ref[...]) * sc_ref[...]
        h1 = x + mix
        h1_ref[...] = h1
        hn_ref[...] = _rms(h1, gf_ref[...]).astype(hn_ref.dtype)
        last()

    row = pl.BlockSpec((tt, D_MODEL), lambda i: (i, 0))
    vec = pl.BlockSpec((1, D_MODEL), lambda i: (0, 0))
    wsp = pl.BlockSpec((len(POOL_WINDOWS), POOL_GC, POOL_GC), lambda i: (0, 0, 0))
    return _call(
        body, hosted, name="pool_fwd", grid=(t // tt,),
        in_specs=[row, _prev_halo_spec(tt, POOL_HALO, D_MODEL), vec, wsp, vec, vec, vec], out_specs=[row, row],
        out_shape=[jax.ShapeDtypeStruct(h0.shape, F32), jax.ShapeDtypeStruct(h0.shape, MXU_DTYPE)],
        operands=(h0, h0, g_mix, pool_w, pool_b, pool_scale, g_ffn), semantics=("parallel",))


def _pool_bwd_call(h0, dh1, g_mix, pool_w, pool_b, pool_scale, hosted=None):
    t = h0.shape[0]
    tt = _tile(t, ROW_TILE)
    nt = t // tt
    per = tt // POOL_HALO

    def body(ins, outs, scratch, first, last):
        h_ref, halo_ref, d_ref, dnext_ref, gm_ref, w_ref, b_ref, sc_ref = ins
        gx_ref, dw_ref, dv_ref = outs
        i = pl.program_id(0)
        first()
        x, gm, sc = h_ref[...], gm_ref[...], sc_ref[...]
        top = jnp.where(i > 0, _rms(halo_ref[...], gm), 0.0)
        ext = jnp.concatenate([top, _rms(x, gm)], axis=0)
        ps = _pool_windows_causal(ext, i * tt)

        dy = d_ref[...]
        dy_ext = jnp.concatenate([dy, jnp.where(i < nt - 1, dnext_ref[...], 0.0)], axis=0)
        dpre_ext = dy_ext * sc
        t_ext = i * tt + lax.broadcasted_iota(jnp.int32, (tt + POOL_HALO, 1), 0)

        @pl.when(i == 0)
        def _():
            dw_ref[...] = jnp.zeros_like(dw_ref)
            dv_ref[...] = jnp.zeros_like(dv_ref)

        dhn, ypre = [], []
        for gi, win in enumerate(POOL_WINDOWS):
            cols = slice(gi * POOL_GC, (gi + 1) * POOL_GC)
            w = w_ref[gi]
            p16 = ps[gi].astype(MXU_DTYPE)
            dpre16 = dpre_ext[:, cols].astype(MXU_DTYPE)
            ypre.append(jnp.dot(p16, w, preferred_element_type=F32))
            dw_ref[gi] += lax.dot_general(p16, dpre16[:tt], (((0,), (0,)), ((), ())), preferred_element_type=F32)
            dp_ext = lax.dot_general(dpre16, w, (((1,), (1,)), ((), ())), preferred_element_type=F32)
            s = dp_ext / jnp.minimum(t_ext + 1, win).astype(F32)
            sh = 1
            while sh < win:
                s = s + _shift_up(s, sh)
                sh *= 2
            dhn.append(s[:tt] - dp_ext[:tt])
        dhn = jnp.concatenate(dhn, axis=1)
        ypre = jnp.concatenate(ypre, axis=1) + b_ref[...]
        dx, dgt = _rms_bwd(x, gm, dhn)
        gx_ref[...] = dy + dx
        dv_ref[0:1, :] += jnp.sum(dpre_ext[:tt], axis=0, keepdims=True)
        dv_ref[1:2, :] += jnp.sum(dy * ypre, axis=0, keepdims=True)
        dv_ref[2:3, :] += jnp.sum(dgt, axis=0, keepdims=True)
        last()

    row = pl.BlockSpec((tt, D_MODEL), lambda i: (i, 0))
    vec = pl.BlockSpec((1, D_MODEL), lambda i: (0, 0))
    nxt = pl.BlockSpec((POOL_HALO, D_MODEL), lambda i: (jnp.minimum((i + 1) * per, t // POOL_HALO - 1), 0))
    wsp = pl.BlockSpec((len(POOL_WINDOWS), POOL_GC, POOL_GC), lambda i: (0, 0, 0))
    return _call(
        body, hosted, name="pool_bwd", grid=(nt,),
        in_specs=[row, _prev_halo_spec(tt, POOL_HALO, D_MODEL), row, nxt, vec, wsp, vec, vec],
        out_specs=[row, wsp, pl.BlockSpec((8, D_MODEL), lambda i: (0, 0))],
        out_shape=[jax.ShapeDtypeStruct(h0.shape, F32),
                   jax.ShapeDtypeStruct((len(POOL_WINDOWS), POOL_GC, POOL_GC), F32),
                   jax.ShapeDtypeStruct((8, D_MODEL), F32)],
        operands=(h0, h0, dh1, dh1, g_mix, pool_w, pool_b, pool_scale), semantics=("arbitrary",))


CONV_ROWS = 512
CONV_SUB = 128
LANES = 128


def _causal_conv(u_ext, w, b):
    return w[0:1] * _shift_down(u_ext, 2) + w[1:2] * _shift_down(u_ext, 1) + w[2:3] * u_ext + b


def _piece_rows(load, s, sub, tt, top, bottom=None):
    lo, hi = s * sub - CONV_HALO, (s + 1) * sub + (CONV_HALO if bottom else 0)
    parts = [top()] if lo < 0 else []
    lo = max(lo, 0)
    parts.append(load(lo, min(hi, tt) - lo))
    if hi > tt:
        parts.append(bottom())
    return parts[0] if len(parts) == 1 else jnp.concatenate(parts, axis=0)


def _fold8(x):
    acc = x[0:8]
    for r in range(8, x.shape[0], 8):
        acc = acc + x[r:r + 8]
    return acc


def _conv_glu_fwd_call(u, conv_w, conv_b, name, hosted=None):
    _, t, f = u.shape
    tt = _tile(t, CONV_ROWS)
    sub = _tile(tt, CONV_SUB)
    per = tt // CONV_HALO

    def body(ins, outs, scratch, first, last):
        u_ref, up_ref, w_ref, b_ref = ins
        z_ref = outs[0]
        i = pl.program_id(0)
        first()

        def chunk(j, carry):
            lanes = pl.ds(pl.multiple_of(j * LANES, LANES), LANES)
            w, b = w_ref[:, :, lanes], b_ref[:, lanes]
            for s in range(tt // sub):
                c = []
                for hf in range(2):
                    ext = _piece_rows(lambda at, n: u_ref[hf, pl.ds(at, n), lanes].astype(F32), s, sub, tt,
                                      top=lambda: jnp.where(i > 0, up_ref[hf, :, lanes].astype(F32), 0.0))
                    c.append(_causal_conv(ext, w[:, hf, :], b[hf:hf + 1, :])[CONV_HALO:])
                z_ref[pl.ds(s * sub, sub), lanes] = (c[0] * c[1] * _sigmoid(c[1])).astype(z_ref.dtype)
            return carry

        lax.fori_loop(0, f // LANES, chunk, 0)
        last()

    res = _call(
        body, hosted, name=name, grid=(t // tt,),
        in_specs=[pl.BlockSpec((2, tt, f), lambda i: (0, i, 0)),
                  pl.BlockSpec((2, CONV_HALO, f), lambda i: (0, jnp.maximum(i * per - 1, 0), 0)),
                  pl.BlockSpec((3, 2, f), lambda i: (0, 0, 0)), pl.BlockSpec((2, f), lambda i: (0, 0))],
        out_specs=[pl.BlockSpec((tt, f), lambda i: (i, 0))], out_shape=[jax.ShapeDtypeStruct((t, f), MXU_DTYPE)],
        operands=(u, u, conv_w, conv_b), semantics=("parallel",))
    return res if hosted else res[0]


def _conv_glu_bwd_call(u, dz, conv_w, conv_b, name, hosted=None):
    _, t, f = u.shape
    tt = _tile(t, CONV_ROWS)
    sub = _tile(tt, CONV_SUB)
    nt = t // tt
    per = tt // CONV_HALO
    halo = CONV_HALO

    def body(ins, outs, scratch, first, last):
        u_ref, up_ref, un_ref, dz_ref, dzn_ref, w_ref, b_ref = ins
        du_ref, dw_ref, db_ref = outs
        dwacc, dbacc = scratch
        i = pl.program_id(0)
        first()

        @pl.when(i == 0)
        def _():
            dwacc[...] = jnp.zeros_like(dwacc)
            dbacc[...] = jnp.zeros_like(dbacc)

        main = slice(halo, halo + sub)

        def chunk(j, carry):
            lanes = pl.ds(pl.multiple_of(j * LANES, LANES), LANES)
            w, b = w_ref[:, :, lanes], b_ref[:, lanes]
            for s in range(tt // sub):
                ue = [_piece_rows(lambda at, n: u_ref[hf, pl.ds(at, n), lanes].astype(F32), s, sub, tt,
                                  top=lambda: jnp.where(i > 0, up_ref[hf, :, lanes].astype(F32), 0.0),
                                  bottom=lambda: un_ref[hf, :, lanes].astype(F32)) for hf in range(2)]
                a, g = [_causal_conv(ue[hf], w[:, hf, :], b[hf:hf + 1, :])[halo:] for hf in range(2)]
                sg = _sigmoid(g)
                if (s + 1) * sub < tt:
                    dzs = dz_ref[pl.ds(s * sub, sub + 2 * halo), lanes].astype(F32)[:sub + halo]
                else:
                    dzs = jnp.concatenate([dz_ref[pl.ds(s * sub, sub), lanes].astype(F32),
                                           jnp.where(i < nt - 1, dzn_ref[:, lanes].astype(F32), 0.0)], axis=0)
                gs = g * sg
                dc = [dzs * gs, dzs * a * (sg + gs - gs * sg)]
                for hf in range(2):
                    d = dc[hf]
                    d0, d1, d2 = d[:sub], _shift_up(d, 1)[:sub], _shift_up(d, 2)[:sub]
                    du = w[2:3, hf, :] * d0 + w[1:2, hf, :] * d1 + w[0:1, hf, :] * d2
                    du_ref[hf, pl.ds(s * sub, sub), lanes] = du.astype(du_ref.dtype)
                    um = ue[hf][main]
                    for k, dd in enumerate((d2, d1, d0)):
                        dwacc[k, hf, :, lanes] += _fold8(dd * um)
                    dbacc[hf, :, lanes] += _fold8(d0)
            return carry

        lax.fori_loop(0, f // LANES, chunk, 0)

        @pl.when(i == nt - 1)
        def _():
            dw_ref[...] = jnp.sum(dwacc[...], axis=2)
            db_ref[...] = jnp.sum(dbacc[...], axis=1)

        last()

    end = t // halo - 1
    return _call(
        body, hosted, name=name, grid=(nt,),
        in_specs=[pl.BlockSpec((2, tt, f), lambda i: (0, i, 0)),
                  pl.BlockSpec((2, halo, f), lambda i: (0, jnp.maximum(i * per - 1, 0), 0)),
                  pl.BlockSpec((2, halo, f), lambda i: (0, jnp.minimum((i + 1) * per, end), 0)),
                  pl.BlockSpec((tt, f), lambda i: (i, 0)),
                  pl.BlockSpec((halo, f), lambda i: (jnp.minimum((i + 1) * per, end), 0)),
                  pl.BlockSpec((3, 2, f), lambda i: (0, 0, 0)), pl.BlockSpec((2, f), lambda i: (0, 0))],
        out_specs=[pl.BlockSpec((2, tt, f), lambda i: (0, i, 0)),
                   pl.BlockSpec((3, 2, f), lambda i: (0, 0, 0)), pl.BlockSpec((2, f), lambda i: (0, 0))],
        out_shape=[jax.ShapeDtypeStruct(u.shape, MXU_DTYPE), jax.ShapeDtypeStruct((3, 2, f), F32),
                   jax.ShapeDtypeStruct((2, f), F32)],
        operands=(u, u, u, dz, dz, conv_w, conv_b), semantics=("arbitrary",),
        scratch_shapes=[pltpu.VMEM((3, 2, 8, f), F32), pltpu.VMEM((2, 8, f), F32)])


def _kv_even_odd(band, k):
    pair, parity = divmod(k, 2)
    blk = band[:, 128 * pair:128 * (pair + 1)].astype(F32)
    lane = lax.broadcasted_iota(jnp.int32, blk.shape, 1)
    own = jnp.where((lane >= HEAD_DIM) == (parity == 1), blk, 0.0)
    swapped = pltpu.roll(own, HEAD_DIM, axis=1)
    even, odd = (own, swapped) if parity == 0 else (swapped, own)
    return even.astype(MXU_DTYPE), odd.astype(MXU_DTYPE)


def _stack_pairs(ref, k):
    return jnp.concatenate([ref[:, 256 * k:256 * k + 128], ref[:, 256 * k + 128:256 * k + 256]], axis=0)


_GROUP_ORDER = (0, 2, 1, 3)


def _attn_fill_tables(bias_ref):
    shape = (4 * ATT_BLOCK, 2 * ATT_BLOCK)
    row = lax.broadcasted_iota(jnp.int32, shape, 0)
    col = lax.broadcasted_iota(jnp.int32, shape, 1)
    dist = (row & (ATT_BLOCK - 1)) + ATT_BLOCK - col
    ok = (dist >= 0) & (dist < ATT_BLOCK)
    distf = dist.astype(F32)
    rb = lax.broadcasted_iota(jnp.int32, (shape[0], 1), 0) // ATT_BLOCK
    for k in range(N_KV_HEADS):
        slope = jnp.zeros((shape[0], 1), F32)
        for r, gq in enumerate(_GROUP_ORDER):
            slope = jnp.where(rb == r, ALIBI_SLOPES[4 * k + gq], slope)
        bias = jnp.where(ok, -slope * distf, -jnp.inf)
        bias_ref[0, k] = bias
        bias_ref[1, k] = jnp.where(col >= ATT_BLOCK, bias, -jnp.inf)


def _attn_probs(qs, k_even, k_odd, bias, k, sink_ref):
    nt_dims = (((1,), (1,)), ((), ()))
    s = jnp.concatenate([lax.dot_general(qs, k_even, nt_dims, preferred_element_type=F32),
                         lax.dot_general(qs, k_odd, nt_dims, preferred_element_type=F32)], axis=0) + bias
    rb = lax.broadcasted_iota(jnp.int32, (s.shape[0], 1), 0) // ATT_BLOCK
    sink = jnp.zeros((s.shape[0], 1), F32)
    for r, gq in enumerate(_GROUP_ORDER):
        sink = jnp.where(rb == r, sink_ref[4 * k + gq], sink)
    m = jnp.maximum(jnp.max(s, axis=-1, keepdims=True), sink)
    p = jnp.exp(s - m)
    es = jnp.exp(sink - m)
    return p, 1.0 / (jnp.sum(p, axis=-1, keepdims=True) + es), es


_ATTN_TABLES = [pltpu.VMEM((2, N_KV_HEADS, 4 * ATT_BLOCK, 2 * ATT_BLOCK), F32)]


def _attn_specs(with_do):
    prev = lambda n: jnp.maximum(n - 1, 0)
    specs = [pl.BlockSpec(memory_space=pltpu.SMEM),
             pl.BlockSpec((ATT_BLOCK, D_MODEL), lambda n: (n, 0)),
             pl.BlockSpec((ATT_BLOCK, 256), lambda n: (prev(n), 4)), pl.BlockSpec((ATT_BLOCK, 256), lambda n: (n, 4)),
             pl.BlockSpec((ATT_BLOCK, 256), lambda n: (prev(n), 5)), pl.BlockSpec((ATT_BLOCK, 256), lambda n: (n, 5))]
    if with_do:
        specs.append(pl.BlockSpec((ATT_BLOCK, D_MODEL), lambda n: (n, 0)))
    return specs


def _attn_fwd_call(qkv, sinks, hosted=None):
    t = qkv.shape[0]

    def body(ins, outs, scratch, first, last):
        sink_ref, q_ref, kp_ref, kc_ref, vp_ref, vc_ref = ins
        o_ref, bias_ref = outs[0], scratch[0]
        n = pl.program_id(0)
        first()

        @pl.when(n == 0)
        def _():
            _attn_fill_tables(bias_ref)

        first = 1 - jnp.minimum(n, 1)
        kband = jnp.concatenate([kp_ref[...], kc_ref[...]], axis=0)
        vband = jnp.concatenate([vp_ref[...], vc_ref[...]], axis=0)
        for k in range(N_KV_HEADS):
            k_even, k_odd = _kv_even_odd(kband, k)
            v_even, v_odd = _kv_even_odd(vband, k)
            qs = _stack_pairs(q_ref, k) * ATT_SCALE
            p, rdenom, _ = _attn_probs(qs, k_even, k_odd, bias_ref[first, k], k, sink_ref)
            probs = (p * rdenom).astype(MXU_DTYPE)
            o = (jnp.dot(probs[:256], v_even, preferred_element_type=F32)
                 + jnp.dot(probs[256:], v_odd, preferred_element_type=F32))
            o_ref[:, 256 * k:256 * k + 128] = o[:128].astype(o_ref.dtype)
            o_ref[:, 256 * k + 128:256 * k + 256] = o[128:].astype(o_ref.dtype)
        last()

    return _call(
        body, hosted, name="attn_fwd", grid=(t // ATT_BLOCK,), in_specs=_attn_specs(False),
        out_specs=[pl.BlockSpec((ATT_BLOCK, D_MODEL), lambda n: (n, 0))],
        out_shape=[jax.ShapeDtypeStruct((t, D_MODEL), MXU_DTYPE)], operands=(sinks, qkv, qkv, qkv, qkv, qkv),
        semantics=("arbitrary",), scratch_shapes=_ATTN_TABLES)


def _attn_bwd_call(qkv, sinks, do, hosted=None):
    t = qkv.shape[0]
    nb = t // ATT_BLOCK
    tn_dims = (((0,), (0,)), ((), ()))
    nt_dims = (((1,), (1,)), ((), ()))

    def to_native(even_part, odd_part, parity):
        lane = lax.broadcasted_iota(jnp.int32, even_part.shape, 1)
        lo = lane < HEAD_DIM
        e, o = jnp.where(lo, even_part, 0.0), jnp.where(lo, 0.0, odd_part)
        if parity == 0:
            return e + pltpu.roll(o, HEAD_DIM, axis=1)
        return pltpu.roll(e, HEAD_DIM, axis=1) + o

    def body(ins, outs, scratch, first, last):
        sink_ref, q_ref, kp_ref, kc_ref, vp_ref, vc_ref, do_ref = ins
        dq_ref, dk_ref, dv_ref, db_ref, dsink_ref = outs
        bias_ref = scratch[0]
        n = pl.program_id(0)
        first()

        @pl.when(n == 0)
        def _():
            _attn_fill_tables(bias_ref)
            dk_ref[...] = jnp.zeros_like(dk_ref)
            dv_ref[...] = jnp.zeros_like(dv_ref)
            db_ref[...] = jnp.zeros_like(db_ref)
            dsink_ref[...] = jnp.zeros_like(dsink_ref)

        kband = jnp.concatenate([kp_ref[...], kc_ref[...]], axis=0)
        vband = jnp.concatenate([vp_ref[...], vc_ref[...]], axis=0)
        dk_pairs = [jnp.zeros((2 * ATT_BLOCK, 128), F32), jnp.zeros((2 * ATT_BLOCK, 128), F32)]
        dv_pairs = [jnp.zeros((2 * ATT_BLOCK, 128), F32), jnp.zeros((2 * ATT_BLOCK, 128), F32)]
        sink_lane = lax.broadcasted_iota(jnp.int32, (8, 128), 1)
        sink_row = lax.broadcasted_iota(jnp.int32, (8, 128), 0)
        dsink = jnp.zeros((8, 128), F32)
        first = 1 - jnp.minimum(n, 1)
        for k in range(N_KV_HEADS):
            k_even, k_odd = _kv_even_odd(kband, k)
            v_even, v_odd = _kv_even_odd(vband, k)
            qs = _stack_pairs(q_ref, k) * ATT_SCALE
            dos = _stack_pairs(do_ref, k)
            p, rdenom, es = _attn_probs(qs, k_even, k_odd, bias_ref[first, k], k, sink_ref)
            probs = p * rdenom
            dp = jnp.concatenate([lax.dot_general(dos, v_even, nt_dims, preferred_element_type=F32),
                                  lax.dot_general(dos, v_odd, nt_dims, preferred_element_type=F32)], axis=0)
            delta = jnp.sum(probs * dp, axis=-1, keepdims=True)
            ds16 = (probs * (dp - delta)).astype(MXU_DTYPE)
            p16 = probs.astype(MXU_DTYPE)
            dsink_rows = -(es * rdenom) * delta
            for r, gq in enumerate(_GROUP_ORDER):
                tot = jnp.sum(dsink_rows[ATT_BLOCK * r:ATT_BLOCK * (r + 1)], axis=0, keepdims=True)
                dsink = dsink + jnp.where((sink_lane == 4 * k + gq) & (sink_row == 0), tot, 0.0)
            dqs = (jnp.dot(ds16[:256], k_even, preferred_element_type=F32)
                   + jnp.dot(ds16[256:], k_odd, preferred_element_type=F32)) * ATT_SCALE
            dq_ref[:, 256 * k:256 * k + 128] = dqs[:128].astype(dq_ref.dtype)
            dq_ref[:, 256 * k + 128:256 * k + 256] = dqs[128:].astype(dq_ref.dtype)
            db_ref[:, 256 * k:256 * k + 128] += jnp.sum(dqs[:128], axis=0, keepdims=True)
            db_ref[:, 256 * k + 128:256 * k + 256] += jnp.sum(dqs[128:], axis=0, keepdims=True)
            pair, parity = divmod(k, 2)
            dk_pairs[pair] = dk_pairs[pair] + to_native(
                lax.dot_general(ds16[:256], qs, tn_dims, preferred_element_type=F32),
                lax.dot_general(ds16[256:], qs, tn_dims, preferred_element_type=F32), parity)
            dv_pairs[pair] = dv_pairs[pair] + to_native(
                lax.dot_general(p16[:256], dos, tn_dims, preferred_element_type=F32),
                lax.dot_general(p16[256:], dos, tn_dims, preferred_element_type=F32), parity)
        dsink_ref[...] += dsink
        dk_band = jnp.concatenate(dk_pairs, axis=1)
        dv_band = jnp.concatenate(dv_pairs, axis=1)
        cur = pl.multiple_of(n * ATT_BLOCK, ATT_BLOCK)
        dk_ref[pl.ds(cur, ATT_BLOCK), :] += dk_band[ATT_BLOCK:]
        dv_ref[pl.ds(cur, ATT_BLOCK), :] += dv_band[ATT_BLOCK:]

        @pl.when(n > 0)
        def _():
            prv = pl.multiple_of((n - 1) * ATT_BLOCK, ATT_BLOCK)
            dk_ref[pl.ds(prv, ATT_BLOCK), :] += dk_band[:ATT_BLOCK]
            dv_ref[pl.ds(prv, ATT_BLOCK), :] += dv_band[:ATT_BLOCK]

        @pl.when(n == nb - 1)
        def _():
            db_ref[:, 1024:1280] = jnp.sum(dk_ref[...], axis=0, keepdims=True)
            db_ref[:, 1280:1536] = jnp.sum(dv_ref[...], axis=0, keepdims=True)

        last()

    whole = lambda shape: pl.BlockSpec(shape, lambda n: (0, 0))
    return _call(
        body, hosted, name="attn_bwd", grid=(nb,), in_specs=_attn_specs(True),
        out_specs=[pl.BlockSpec((ATT_BLOCK, D_MODEL), lambda n: (n, 0)), whole((t, 256)), whole((t, 256)),
                   whole((1, QKV_DIM)), whole((8, 128))],
        out_shape=[jax.ShapeDtypeStruct((t, D_MODEL), MXU_DTYPE), jax.ShapeDtypeStruct((t, 256), F32),
                   jax.ShapeDtypeStruct((t, 256), F32), jax.ShapeDtypeStruct((1, QKV_DIM), F32),
                   jax.ShapeDtypeStruct((8, 128), F32)],
        operands=(sinks, qkv, qkv, qkv, qkv, qkv, do), semantics=("arbitrary",), scratch_shapes=_ATTN_TABLES)


FF_TILE = 1408


def _gathering_tile(j):
    me = 2 * lax.axis_index("x") + lax.axis_index("y")
    return me ^ (((j & 1) << 1) | (j >> 1))


def _ffn_in_gathering_call(hn, w_in, layer, hosted):
    t = hn.shape[0]
    tm = _tile(t, 1024)
    n_i = t // tm
    halves = D_FF // FF_TILE
    assert FF_TILE == W_IN_COLS and 2 * halves == N_CHIPS

    def body(ins, outs, scratch, first, last):
        hn_ref = ins[0]
        u_ref, w_ref = outs
        wbuf, load_sem, ici_send, ici_recv, hand_send, hand_recv = scratch
        j, i = pl.program_id(0), pl.program_id(1)
        x, y, c, chips = _mesh_place()
        me = 2 * x + y
        mover = c == layer

        def shard(k):
            return w_ref.at[layer, :, _col(k, W_IN_COLS)]

        def over_ici(jj, k):
            cx, cy = chips[jj]
            return pltpu.make_async_remote_copy(src_ref=shard(k), dst_ref=shard(k), send_sem=ici_send.at[jj],
                                                recv_sem=ici_recv.at[jj], device_id=(cx, cy, c), device_id_type=MESH)

        def handed(jj):
            cx, cy = chips[jj]
            return pltpu.make_async_remote_copy(
                src_ref=shard(2 * cx + cy), dst_ref=shard(2 * cx + cy), send_sem=hand_send.at[jj],
                recv_sem=hand_recv.at[jj], device_id=(x, y, 1 - c), device_id_type=MESH)

        far = N_CHIPS - 2

        @pl.when((j == 0) & (i == 0) & mover)
        def _():
            over_ici(far, me).start()
            for jj in NEIGHBOURS:
                handed(jj).start()

        first()

        @pl.when((j == far + 1) & (i == 0) & mover)
        def _():
            over_ici(far, 2 * chips[far][0] + chips[far][1]).wait_recv()
            handed(far).start()

        for jj in range(N_CHIPS - 1):
            @pl.when((j == jj + 1) & (i == 0) & jnp.logical_not(mover))
            def _(jj=jj):
                handed(jj).wait_recv()

        @pl.when(i == 0)
        def _():
            load = pltpu.make_async_copy(shard(_gathering_tile(j)), wbuf, load_sem.at[0])
            load.start()
            load.wait()

        u_ref[...] = jnp.dot(hn_ref[...].astype(MXU_DTYPE), wbuf[...].astype(MXU_DTYPE),
                             preferred_element_type=F32).astype(u_ref.dtype)

        @pl.when((j == N_CHIPS - 1) & (i == n_i - 1) & mover)
        def _():
            over_ici(far, me).wait_send()
            for jj in range(N_CHIPS - 1):
                handed(jj).wait_send()

        last()

    def u_index(j, i):
        tile = _gathering_tile(j)
        return tile // halves, i, tile % halves

    three = [pltpu.SemaphoreType.DMA((N_CHIPS - 1,))] * 4
    return _call(
        body, hosted, name=f"ffn{layer}_in", grid=(N_CHIPS, n_i),
        in_specs=[pl.BlockSpec((tm, D_MODEL), lambda j, i: (i, 0)), ANY],
        out_specs=[pl.BlockSpec((None, tm, FF_TILE), u_index), ANY],
        out_shape=[jax.ShapeDtypeStruct((2, t, D_FF), U_DTYPE), jax.ShapeDtypeStruct(w_in.shape, w_in.dtype)],
        operands=(hn, w_in), semantics=("arbitrary", "arbitrary"), aliases={1: 1},
        scratch_shapes=[pltpu.VMEM((D_MODEL, W_IN_COLS), w_in.dtype), pltpu.SemaphoreType.DMA((1,))] + three)


def _ffn_out_loss_call(z, w_out, layer, h, g, target):
    t, f = z.shape
    tm = _tile(t, ROW_TILE)

    def body(z_ref, w_ref, h_ref, g_ref, t_ref, loss_ref, dh_ref, dh16_ref, dg_ref):
        x = h_ref[...] + jnp.dot(z_ref[...], w_ref[...], preferred_element_type=F32)
        gv = g_ref[...]
        err = _rms(x, gv) - t_ref[...]
        dx, dgt = _rms_bwd(x, gv, err * (1.0 / D_MODEL))
        dh_ref[...] = dx
        dh16_ref[...] = dx.astype(dh16_ref.dtype)

        @pl.when(pl.program_id(0) == 0)
        def _():
            dg_ref[...] = jnp.zeros_like(dg_ref)
            loss_ref[...] = jnp.zeros_like(loss_ref)

        dg_ref[...] += jnp.sum(dgt, axis=0, keepdims=True)
        per_token = jnp.mean(err * err, axis=-1, keepdims=True)
        loss_ref[...] += 0.5 * jnp.sum(per_token, axis=0, keepdims=True)

    row = pl.BlockSpec((tm, D_MODEL), lambda i: (i, 0))
    vec = pl.BlockSpec((1, D_MODEL), lambda i: (0, 0))
    return pl.pallas_call(
        body, name=f"ffn{layer}_out_loss", grid=(t // tm,),
        in_specs=[pl.BlockSpec((tm, f), lambda i: (i, 0)), pl.BlockSpec((None, f, D_MODEL), lambda i: (layer, 0, 0)),
                  row, vec, row],
        out_specs=[pl.BlockSpec((1, 1), lambda i: (0, 0)), row, row, vec],
        out_shape=[jax.ShapeDtypeStruct((1, 1), F32), jax.ShapeDtypeStruct((t, D_MODEL), F32),
                   jax.ShapeDtypeStruct((t, D_MODEL), MXU_DTYPE), jax.ShapeDtypeStruct((1, D_MODEL), F32)],
        compiler_params=_params("arbitrary"))(z, w_out, h, g, target)


def _ffn_fwd(hn, h, w, conv_w, conv_b, layer, hooks=None, next_norm=None, loss_head=None, gathers_w_in=False):
    t = hn.shape[0]
    tm = _tile(t, 1024)
    halves = D_FF // FF_TILE
    if gathers_w_in:
        u, w["w_in"] = _ride(hooks, f"in{layer}", lambda hosted: _ffn_in_gathering_call(hn, w["w_in"], layer, hosted))
    else:
        u, = _ride(hooks, f"in{layer}", lambda hosted: _matmul(
            hn, w["w_in"], mode="nn", name=f"ffn{layer}_in", grid=(2 * halves, t // tm, 1),
            a_spec=pl.BlockSpec((tm, D_MODEL), lambda j, i, k: (i, 0)),
            b_spec=pl.BlockSpec((None, D_MODEL, FF_TILE), lambda j, i, k: (layer, 0, j)),
            o_spec=pl.BlockSpec((None, tm, FF_TILE), lambda j, i, k: (j // halves, i, j % halves)),
            out_shape=jax.ShapeDtypeStruct((2, t, D_FF), U_DTYPE), hosted=hosted))
    z, = _ride(hooks, f"glu{layer}", lambda hosted: _conv_glu_fwd_call(u, conv_w, conv_b, f"ffn{layer}_glu",
                                                                       hosted=hosted))
    if loss_head is not None:
        return (*_ffn_out_loss_call(z, w["w_out"], layer, h, *loss_head), u, z)
    outs = _ride(hooks, f"out{layer}", lambda hosted: _mm_nn(z, w["w_out"], f"ffn{layer}_out", b_lead=layer, residual=h,
                                                     tk=D_FF, hosted=hosted, norm_g=next_norm))
    return outs[0], (outs[1] if next_norm is not None else None), u, z


def _ffn_bwd(dh, hn, u, z, w_in, w_out, conv_w, conv_b, layer, g_in_acc, g_out_acc, norm, hooks=None):
    t = hn.shape[0]
    dz, = _ride(hooks, "dz", lambda hosted: _mm_nt(dh, w_out, f"ffn{layer}_dz", b_lead=layer, out_dtype=MXU_DTYPE,
                                                   tn=FF_TILE, hosted=hosted))
    g_out = _mm_tn(z, dh, f"ffn{layer}_dwout", lead=layer, n_lead=2, into=g_out_acc, tm=WGRAD_TILE, tk=t,
                   out_dtype=WIRE_DTYPE)
    du, dcw, dcb = _ride(hooks, "dglu", lambda hosted: _conv_glu_bwd_call(u, dz, conv_w, conv_b, f"ffn{layer}_dglu",
                                                                          hosted=hosted), {"g_out": g_out})
    halves = D_FF // FF_TILE
    rows = D_MODEL // 2
    g_in, = _ride(hooks, "dwin", lambda hosted: _matmul(
        hn, du, mode="tn", name=f"ffn{layer}_dwin", grid=(2 * halves, D_MODEL // rows, 1),
        a_spec=pl.BlockSpec((t, rows), lambda j, i, k: (0, i)),
        b_spec=pl.BlockSpec((None, t, FF_TILE), lambda j, i, k: (j // halves, 0, j % halves)),
        o_spec=pl.BlockSpec((None, rows, FF_TILE), lambda j, i, k: (layer, i, j)),
        out_shape=jax.ShapeDtypeStruct((2, D_MODEL, 2 * D_FF), WIRE_DTYPE), into=g_in_acc, hosted=hosted))
    dx, dx16, dg = _ride(hooks, "dhn", lambda hosted: _ffn_dhn_call(du, w_in, layer, norm, hosted),
                         {"g_in": g_in, "g_out": g_out, "dcw": dcw, "dcb": dcb})
    return dx, dx16, dg, g_in, g_out, dcw, dcb


WGRAD_TILE = 256


_NT_DIMS = (((1,), (1,)), ((), ()))


def _dhn_call(name, a, a_spec, w, w_spec, product, t, tm, norm, hosted=None):
    h, g, dres = norm

    def body(ins, outs, scratch, first, last):
        a_ref, w_ref, h_ref, g_ref, dres_ref = ins
        dx_ref, dx16_ref, dg_ref = outs
        first()
        dx, dgt = _rms_bwd(h_ref[...], g_ref[...], product(a_ref, w_ref))
        dx = dres_ref[...] + dx
        dx_ref[...] = dx
        dx16_ref[...] = dx.astype(dx16_ref.dtype)

        @pl.when(pl.program_id(0) == 0)
        def _():
            dg_ref[...] = jnp.zeros_like(dg_ref)

        dg_ref[...] += jnp.sum(dgt, axis=0, keepdims=True)
        last()

    row = pl.BlockSpec((tm, D_MODEL), lambda i: (i, 0))
    vec = pl.BlockSpec((1, D_MODEL), lambda i: (0, 0))
    return _call(
        body, hosted, name=name, grid=(t // tm,), in_specs=[a_spec, w_spec, row, vec, row], out_specs=[row, row, vec],
        out_shape=[jax.ShapeDtypeStruct((t, D_MODEL), F32), jax.ShapeDtypeStruct((t, D_MODEL), MXU_DTYPE),
                   jax.ShapeDtypeStruct((1, D_MODEL), F32)],
        operands=(a, w, h, g, dres), semantics=("arbitrary",))


def _ffn_dhn_call(du, w_in, layer, norm, hosted=None):
    _, t, f = du.shape
    tm = _tile(t, 512)

    def product(du_ref, w_ref):
        return (lax.dot_general(du_ref[0], w_ref[:, :f], _NT_DIMS, preferred_element_type=F32)
                + lax.dot_general(du_ref[1], w_ref[:, f:], _NT_DIMS, preferred_element_type=F32))

    return _dhn_call(f"ffn{layer}_dhn", du, pl.BlockSpec((2, tm, f), lambda i: (0, i, 0)), w_in,
                     pl.BlockSpec((None, D_MODEL, 2 * f), lambda i: (layer, 0, 0)), product, t, tm, norm, hosted)


def _attn_dhn_call(dqkv, w_qkv, norm):
    t, n = dqkv.shape
    tm = _tile(t, 1024)

    def product(a_ref, w_ref):
        return lax.dot_general(a_ref[...], w_ref[...], _NT_DIMS, preferred_element_type=F32)

    return _dhn_call("attn_dhn", dqkv, pl.BlockSpec((tm, n), lambda i: (i, 0)), w_qkv,
                     pl.BlockSpec((D_MODEL, n), lambda i: (0, 0)), product, t, tm, norm)


def _local_step(x, target, w, *, norm_mix, norm_ffn, norm_f, pool_b, pool_scale, b_qkv, sinks, conv_w, conv_b,
                late_gather=None, reducer=None):
    cw = [conv_w[l].reshape(3, 2, D_FF) for l in range(2)]
    cb = [conv_b[l].reshape(2, D_FF) for l in range(2)]
    gm = [norm_mix[l:l + 1] for l in range(2)]
    gf = [norm_ffn[l:l + 1] for l in range(2)]

    w = dict(w)
    fwd_hooks = {}
    if late_gather:
        def gather_hook(stage):
            return (lambda env: late_gather(stage, w)), (lambda got: w.update(zip(_late_names(stage), got)))

        fwd_hooks = {stage: gather_hook(stage) for stage in GATHER_RIDES}

    h1, hn_f0 = _ride(fwd_hooks, "pool_fwd", lambda hosted: _pool_fwd_call(x, gm[0], w["pool_w"], pool_b, pool_scale,
                                                                            gf[0], hosted))
    h2, hn_a, u0, z0 = _ffn_fwd(hn_f0, h1, w, cw[0], cb[0], 0, hooks=fwd_hooks, next_norm=gm[1],
                                gathers_w_in=bool(late_gather))
    qkv = _mm_nn(hn_a, w["w_qkv"], "attn_qkv", bias=b_qkv, out_dtype=MXU_DTYPE, tn=QKV_DIM)
    o, = _ride(fwd_hooks, "attn_fwd", lambda hosted: _attn_fwd_call(qkv, sinks, hosted))
    h3, hn_f1 = _ride(fwd_hooks, "attn_out", lambda hosted: _mm_nn(o, w["w_o"], "attn_out", residual=h2, hosted=hosted,
                                                                   norm_g=gf[1]))
    loss, dh, dh16, d_norm_f, u1, z1 = _ffn_fwd(hn_f1, h3, w, cw[1], cb[1], 1, hooks=fwd_hooks,
                                                loss_head=(norm_f, target))

    rd = reducer
    first, mid, last = _Reducer.FIRST, _Reducer.MID, _Reducer.LAST
    hooks1 = hooks_attn = hooks0 = hooks_pool = None
    if rd:
        hooks1 = {"dhn": (lambda env: rd.swap(first, {"w_in": env["g_in"], "w_out": env["g_out"]}),
                          lambda got: rd.pair_sums(first, got))}
        hooks_attn = {"attn_bwd": (lambda env: rd.scatter(first), lambda got: rd.chip_sums(first, got))}

        def mid_and_last(got):
            rd.shared(mid, got[:len(mid)])
            rd.pair_sums(last, got[len(mid):len(mid) + len(last)])
            rd.small_early_done(got[-1])

        def early_pieces(env):
            return {"b_qkv": d_bqkv, "sinks": d_sinks[0:1], "norm_mix1": d_gm1, "norm_ffn1": d_gf1,
                    "conv_w": jnp.stack([env["dcw"].reshape(3, 2 * D_FF), dcw1.reshape(3, 2 * D_FF)]),
                    "conv_b": jnp.stack([env["dcb"].reshape(2 * D_FF), dcb1.reshape(2 * D_FF)]),
                    "norm_f": d_norm_f, "loss": loss}

        hooks0 = {
            "dz": (lambda env: rd.share(first), lambda got: rd.shared(first, got)),
            "dglu": (lambda env: rd.swap(mid, {"w_out": env["g_out"], "w_qkv": g_qkv, "w_o": g_o4}),
                     lambda got: rd.pair_sums(mid, got)),
            "dwin": (lambda env: rd.scatter(mid), lambda got: rd.chip_sums(mid, got)),
            "dhn": (lambda env: _both(_both(rd.share(mid), rd.swap(last, {"w_in": env["g_in"]})),
                                      rd.small_early(early_pieces(env))), mid_and_last),
        }
        hooks_pool = {"pool_bwd": (lambda env: rd.scatter(last), lambda got: rd.chip_sums(last, got))}
    dh, dh16, d_gf1, g_in, g_out, dcw1, dcb1 = _ffn_bwd(
        dh16, hn_f1, u1, z1, w["w_in"], w["w_out"], cw[1], cb[1], 1, None, None, (h3, gf[1], dh), hooks=hooks1)
    do = _mm_nt(dh16, w["w_o"], "attn_do", out_dtype=MXU_DTYPE)
    g_o = _mm_tn(o, dh16, "attn_dwo", tn=WGRAD_TILE, tk=x.shape[0], out_dtype=WIRE_DTYPE)
    g_o4 = g_o.reshape(N_CHIPS, 2, W_O_ROWS // 2, D_MODEL)
    dq, dk, dv, d_bqkv, d_sinks = _ride(hooks_attn, "attn_bwd", lambda hosted: _attn_bwd_call(qkv, sinks, do, hosted))
    dqkv = jnp.concatenate([dq, dk.astype(MXU_DTYPE), dv.astype(MXU_DTYPE)], axis=1)
    g_qkv = _mm_tn(hn_a, dqkv, "attn_dwqkv", tn=WGRAD_TILE, tk=x.shape[0], out_dtype=WIRE_DTYPE)
    dh, dh16, d_gm1 = _attn_dhn_call(dqkv, w["w_qkv"], (h2, gm[1], dh))
    dh, _, d_gf0, g_in, g_out, dcw0, dcb0 = _ffn_bwd(
        dh16, hn_f0, u0, z0, w["w_in"], w["w_out"], cw[0], cb[0], 0, g_in, g_out, (h1, gf[0], dh), hooks=hooks0)
    grad_x, g_pool, dvec = _ride(hooks_pool, "pool_bwd", lambda hosted: _pool_bwd_call(
        x, dh, gm[0], w["pool_w"], pool_b, pool_scale, hosted))

    grads = {
        "w_in": g_in, "w_out": g_out, "w_qkv": g_qkv, "w_o": g_o, "pool_w": g_pool,
        "pool_b": dvec[0:1], "pool_scale": dvec[1:2], "b_qkv": d_bqkv, "sinks": d_sinks[0:1, :N_HEADS],
        "norm_mix": jnp.concatenate([dvec[2:3], d_gm1], axis=0), "norm_ffn": jnp.concatenate([d_gf0, d_gf1], axis=0),
        "conv_w": jnp.stack([dcw0.reshape(3, 2 * D_FF), dcw1.reshape(3, 2 * D_FF)]),
        "conv_b": jnp.stack([dcb0.reshape(2 * D_FF), dcb1.reshape(2 * D_FF)]), "norm_f": d_norm_f,
    }
    return loss, grad_x, grads


W_IN_COLS = 2 * D_FF // N_CHIPS
W_OUT_ROWS = D_FF // N_CHIPS
QKV_COLS = QKV_DIM // N_CHIPS
W_O_ROWS = D_MODEL // N_CHIPS
POOL_ROWS = POOL_GC // N_CHIPS
BIG = ("w_in", "w_out", "w_qkv", "w_o", "pool_w")


def _mesh_place():
    x, y, c = lax.axis_index("x"), lax.axis_index("y"), lax.axis_index("c")
    chips = [(1 - x, y), (x, 1 - y), (1 - x, 1 - y)]
    return x, y, c, chips


def _col(k, width):
    return pl.ds(pl.multiple_of(k * width, 128), width)


def _row(k, height):
    return pl.ds(pl.multiple_of(k * height, 16), height)


def _full_piece(name, ref, k, h):
    if name == "w_in":
        return ref.at[h, :, _col(k, W_IN_COLS)]
    if name == "w_out":
        return ref.at[h, _row(k, W_OUT_ROWS), :]
    if name == "w_qkv":
        return ref.at[_row(h, D_MODEL // 2), _col(k, QKV_COLS)]
    if name == "w_o":
        return ref.at[_row(2 * k + h, W_O_ROWS // 2), :]
    return ref.at[pl.ds(2 * h, 2), _row(k, POOL_ROWS), :]


def _shard_half(name, ref, h):
    if name in ("w_in", "w_out"):
        return ref.at[h]
    if name == "w_qkv":
        return ref.at[_row(h, D_MODEL // 2), :]
    if name == "w_o":
        return ref.at[_row(h, W_O_ROWS // 2), :]
    return ref.at[pl.ds(2 * h, 2)]


FULL_SHAPES = {"w_in": (2, D_MODEL, 2 * D_FF), "w_out": (2, D_FF, D_MODEL), "w_qkv": (D_MODEL, QKV_DIM),
               "w_o": (D_MODEL, D_MODEL), "pool_w": (4, POOL_GC, POOL_GC)}
SHARD_SHAPES = {"w_in": (2, D_MODEL, W_IN_COLS), "w_out": (2, W_OUT_ROWS, D_MODEL), "w_qkv": (D_MODEL, QKV_COLS),
                "w_o": (W_O_ROWS, D_MODEL), "pool_w": (4, POOL_ROWS, POOL_GC)}
HALF_SHAPES = {"w_in": (D_MODEL, W_IN_COLS), "w_out": (W_OUT_ROWS, D_MODEL), "w_qkv": (D_MODEL // 2, QKV_COLS),
               "w_o": (W_O_ROWS // 2, D_MODEL), "pool_w": (2, POOL_ROWS, POOL_GC)}


def _place_shard(name, shard, me_arr):
    if name == "w_in":
        blk, grid = (1, 256, W_IN_COLS), (2, D_MODEL // 256)
        src, dst = (lambda a, b, me: (a, b, 0)), (lambda a, b, me: (a, b, me[0]))
    elif name == "w_out":
        blk, grid = (1, W_OUT_ROWS, D_MODEL), (2, 1)
        src, dst = (lambda a, b, me: (a, 0, 0)), (lambda a, b, me: (a, me[0], 0))
    elif name == "w_qkv":
        blk, grid = (256, QKV_COLS), (1, D_MODEL // 256)
        src, dst = (lambda a, b, me: (b, 0)), (lambda a, b, me: (b, me[0]))
    elif name == "w_o":
        blk, grid = (W_O_ROWS, D_MODEL), (1, 1)
        src, dst = (lambda a, b, me: (0, 0)), (lambda a, b, me: (me[0], 0))
    else:
        blk, grid = (4, POOL_ROWS, POOL_GC), (1, 1)
        src, dst = (lambda a, b, me: (0, 0, 0)), (lambda a, b, me: (0, me[0], 0))

    def body(me_ref, s_ref, o_ref):
        o_ref[...] = s_ref[...].astype(o_ref.dtype)

    return pl.pallas_call(
        body, name=f"place_{name}",
        grid_spec=pltpu.PrefetchScalarGridSpec(num_scalar_prefetch=1, grid=grid, in_specs=[pl.BlockSpec(blk, src)],
                                               out_specs=pl.BlockSpec(blk, dst)),
        out_shape=jax.ShapeDtypeStruct(FULL_SHAPES[name], WIRE_DTYPE), compiler_params=_params("parallel", "parallel"),
    )(me_arr, shard)


def _when_mover(half, c, fn):
    if half is None:
        fn(c)
    else:
        pl.when(c == half)(lambda: fn(half))


def _when_taker(half, c, fn):
    if half is None:
        fn(1 - c)
    else:
        pl.when(c != half)(lambda: fn(half))


def _remote(ref, sems, t, j, dev):
    send, recv = sems
    return pltpu.make_async_remote_copy(src_ref=ref, dst_ref=ref, send_sem=send.at[t, j], recv_sem=recv.at[t, j],
                                        device_id=dev, device_id_type=MESH)


def _gather_ici(items, refs, sems, finish, peers=(0, 1, 2)):
    x, y, c, chips = _mesh_place()
    me = 2 * x + y
    for t, (nm, half) in enumerate(items):
        def go(h, t=t, nm=nm):
            mine = _full_piece(nm, refs[t], me, h)
            for j, (cx, cy) in enumerate(chips):
                if j not in peers:
                    continue
                if not finish:
                    _remote(mine, sems, t, j, (cx, cy, c)).start()
                else:
                    _remote(_full_piece(nm, refs[t], 2 * cx + cy, h), sems, t, j, (cx, cy, c)).wait_recv()
                    _remote(mine, sems, t, j, (cx, cy, c)).wait_send()

        _when_mover(half, c, go)


def _gather_pass(items, refs, sems, finish):
    x, y, c, chips = _mesh_place()
    sibling = (x, y, 1 - c)
    for t, (nm, half) in enumerate(items):
        def give(h, t=t, nm=nm):
            for j, (cx, cy) in enumerate(chips):
                got = _remote(_full_piece(nm, refs[t], 2 * cx + cy, h), sems, t, j, sibling)
                if not finish:
                    got.start()
                else:
                    got.wait_send()

        def take(h, t=t, nm=nm):
            for j, (cx, cy) in enumerate(chips):
                _remote(_full_piece(nm, refs[t], 2 * cx + cy, h), sems, t, j, sibling).wait_recv()

        _when_mover(half, c, give)
        if finish:
            _when_taker(half, c, take)


def _gather_sems(n_items):
    return [pltpu.SemaphoreType.DMA((n_items, 3)), pltpu.SemaphoreType.DMA((n_items, 3))]


EARLY_ITEMS = (("pool_w", None),)
_LATE_A = (("w_out", 0),)
_LATE_B = (("w_in", 1),)
_LATE_C = (("w_out", 1),)
_LATE_D = (("w_qkv", None), ("w_o", None))
_EARLY_W_IN = (("w_in", 0),)
GATHER_RIDES = {"pool_fwd": ((), _EARLY_W_IN), "in0": ((), _LATE_A), "glu0": (_LATE_A, _LATE_D), "out0": (_LATE_D, ()),
                "attn_fwd": ((), _LATE_B), "attn_out": (_LATE_B, ()), "in1": ((), _LATE_C), "glu1": (_LATE_C, ())}
NEIGHBOURS = (0, 1)
GATHER_PEERS = {"pool_fwd": NEIGHBOURS}


def _names(items):
    return [nm for nm, _ in items]


def _gather_early(full, small):
    items = EARLY_ITEMS
    n = len(items)

    def body(*refs):
        small_in = refs[n]
        outs, small_out = refs[n + 1:2 * n + 1], refs[2 * n + 1]
        scratch = refs[2 * n + 2:]
        ici_sems, pass_sems = scratch[0:2], scratch[2:4]
        local_sem, small_send, small_recv = scratch[4:]
        x, y, c, chips = _mesh_place()
        me = 2 * x + y
        local = pltpu.make_async_copy(small_in, small_out.at[me], local_sem.at[0])
        local.start()
        small_sends = [pltpu.make_async_remote_copy(
            src_ref=small_in, dst_ref=small_out.at[me], send_sem=small_send.at[j], recv_sem=small_recv.at[j],
            device_id=(cx, cy, c), device_id_type=MESH) for j, (cx, cy) in enumerate(chips)]
        for cp in small_sends:
            cp.start()
        _gather_ici(items, outs, ici_sems, finish=False)
        _gather_ici(items, outs, ici_sems, finish=True)
        _gather_pass(items, outs, pass_sems, finish=False)
        _gather_pass(items, outs, pass_sems, finish=True)
        for j, (cx, cy) in enumerate(chips):
            got = small_out.at[2 * cx + cy]
            pltpu.make_async_remote_copy(src_ref=got, dst_ref=got, send_sem=small_send.at[j],
                                         recv_sem=small_recv.at[j], device_id=(cx, cy, c), device_id_type=MESH).wait_recv()
        for cp in small_sends:
            cp.wait_send()
        local.wait()

    arrays = [full[nm] for nm in _names(items)]
    out_shape = [jax.ShapeDtypeStruct(a.shape, a.dtype) for a in arrays]
    out_shape.append(jax.ShapeDtypeStruct((N_CHIPS,) + small.shape, F32))
    res = pl.pallas_call(
        body, name="gather_early", in_specs=[ANY] * (n + 1), out_specs=[ANY] * (n + 1), out_shape=out_shape,
        input_output_aliases={t: t for t in range(n)},
        scratch_shapes=_gather_sems(n) + _gather_sems(n) + [pltpu.SemaphoreType.DMA((1,)), pltpu.SemaphoreType.DMA((3,)),
                                                            pltpu.SemaphoreType.DMA((3,))],
    )(*arrays, small)
    return {**full, **dict(zip(_names(items), res[:n]))}, res[n]


def _late_gather_host(stage, w):
    passing, sending = GATHER_RIDES[stage]
    n_pass = len(passing)

    def run(src, refs, new, sems, finish):
        if passing:
            _gather_pass(passing, refs[:n_pass], sems[:2], finish)
        if sending:
            _gather_ici(sending, refs[n_pass:], sems[-2:], finish, GATHER_PEERS.get(stage, (0, 1, 2)))

    sems = (_gather_sems(n_pass) if passing else []) + (_gather_sems(len(sending)) if sending else [])
    return _Hosted([], [w[nm] for nm in _late_names(stage)], [], sems, functools.partial(run, finish=False),
                   functools.partial(run, finish=True))


def _late_names(stage):
    return [nm for items in GATHER_RIDES[stage] for nm in _names(items)]


def _other_half_src(name, ref, h):
    if name in ("w_in", "w_out"):
        return ref.at[h]
    if name == "w_qkv":
        return ref.at[_row(h, D_MODEL // 2), :]
    if name == "w_o":
        return ref.at[:, pl.ds(h, 1)]
    return ref.at[pl.ds(2 * h, 2)]


SLAB_SHAPES = {"w_in": (D_MODEL, 2 * D_FF), "w_out": (D_FF, D_MODEL), "w_qkv": (D_MODEL // 2, QKV_DIM),
               "w_o": (N_CHIPS, 1, W_O_ROWS // 2, D_MODEL), "pool_w": (2, POOL_GC, POOL_GC)}


def _item_tag(item):
    return item[0] if item[1] is None else f"{item[0]}{item[1]}"


def _reduce_swap(items, grads, slabs, sems, finish):
    x, y, c, _ = _mesh_place()
    send, recv = sems
    for t, (nm, half) in enumerate(items):
        def copy(h, t=t, nm=nm):
            return pltpu.make_async_remote_copy(
                src_ref=_other_half_src(nm, grads[t], h), dst_ref=slabs[t], send_sem=send.at[t], recv_sem=recv.at[t],
                device_id=(x, y, 1 - c), device_id_type=MESH)

        def give(h):
            if finish:
                copy(h).wait_send()
            else:
                copy(h).start()

        if half is None:
            give(1 - c)
            if finish:
                copy(c).wait_recv()
        else:
            pl.when(c != half)(lambda: give(half))
            if finish:
                pl.when(c == half)(lambda: copy(half).wait_recv())


def _reduce_scatter(items, sums, parts, sems, finish):
    _, _, c, chips = _mesh_place()
    send, recv = sems
    for t, (nm, half) in enumerate(items):
        def go(h, t=t, nm=nm):
            for j, (cx, cy) in enumerate(chips):
                cp = pltpu.make_async_remote_copy(
                    src_ref=_slab_piece(nm, sums[t], 2 * cx + cy), dst_ref=parts[t].at[j], send_sem=send.at[t, j],
                    recv_sem=recv.at[t, j], device_id=(cx, cy, c), device_id_type=MESH)
                if finish:
                    cp.wait_recv()
                    cp.wait_send()
                else:
                    cp.start()

        _when_mover(half, c, go)


def _reduce_share(items, shards, sems, finish):
    x, y, c, _ = _mesh_place()
    send, recv = sems
    for t, (nm, half) in enumerate(items):
        def copy(h, t=t, nm=nm):
            part = _shard_half(nm, shards[t], h)
            return pltpu.make_async_remote_copy(src_ref=part, dst_ref=part, send_sem=send.at[t], recv_sem=recv.at[t],
                                                device_id=(x, y, 1 - c), device_id_type=MESH)

        def give(h):
            if finish:
                copy(h).wait_send()
            else:
                copy(h).start()

        _when_mover(half, c, give)
        if finish:
            _when_taker(half, c, lambda h: copy(h).wait_recv())


def _pair_sems(n):
    return [pltpu.SemaphoreType.DMA((n,)), pltpu.SemaphoreType.DMA((n,))]


def _swap_host(items, grads):
    def run(src, arr, new, sems, finish):
        _reduce_swap(items, src, new, sems, finish)

    return _Hosted([grads[nm] for nm, _ in items], [],
                   [jax.ShapeDtypeStruct(SLAB_SHAPES[nm], grads[nm].dtype) for nm, _ in items],
                   _pair_sems(len(items)), functools.partial(run, finish=False), functools.partial(run, finish=True))


def _scatter_host(items, sums):
    def run(src, arr, new, sems, finish):
        _reduce_scatter(items, src, new, sems, finish)

    new = [jax.ShapeDtypeStruct((N_CHIPS - 1,) + HALF_SHAPES[nm], WIRE_DTYPE) for nm, _ in items]
    return _Hosted(list(sums), [], new, _gather_sems(len(items)), functools.partial(run, finish=False),
                   functools.partial(run, finish=True))


def _share_host(items, shards):
    def run(src, arr, new, sems, finish):
        _reduce_share(items, arr, sems, finish)

    return _Hosted([], list(shards), [], _pair_sems(len(items)), functools.partial(run, finish=False),
                   functools.partial(run, finish=True))


def _pair_sum(item, grad, slab, place):
    name, half = item
    shape = SLAB_SHAPES[name]
    hsel = (lambda p: p[1]) if half is None else (lambda p: half)
    if name == "w_in":
        blk = (256, 2 * D_FF)
        grid = (D_MODEL // 256,)
        g_spec = pl.BlockSpec((None,) + blk, lambda i, p: (hsel(p), i, 0))
        s_spec = pl.BlockSpec(blk, lambda i, p: (i, 0))
    elif name == "w_out":
        blk = (W_OUT_ROWS, D_MODEL)
        grid = (N_CHIPS,)
        g_spec = pl.BlockSpec((None,) + blk, lambda i, p: (hsel(p), i, 0))
        s_spec = pl.BlockSpec(blk, lambda i, p: (i, 0))
    elif name == "w_qkv":
        grid = (1,)
        g_spec = pl.BlockSpec(shape, lambda i, p: (hsel(p), 0))
        s_spec = pl.BlockSpec(shape, lambda i, p: (0, 0))
    else:
        grid = (1,)
        g_spec = pl.BlockSpec(shape, lambda i, p: (0, hsel(p), 0, 0))
        s_spec = pl.BlockSpec(shape, lambda i, p: (0, 0, 0, 0))

    def body(p_ref, g_ref, s_ref, o_ref):
        @pl.when(hsel(p_ref) == p_ref[1])
        def _():
            o_ref[...] = (g_ref[...].astype(F32) + s_ref[...].astype(F32)).astype(o_ref.dtype)

    return pl.pallas_call(
        body, name=f"pair_sum_{_item_tag(item)}",
        grid_spec=pltpu.PrefetchScalarGridSpec(num_scalar_prefetch=1, grid=grid, in_specs=[g_spec, s_spec],
                                               out_specs=s_spec),
        out_shape=jax.ShapeDtypeStruct(shape, WIRE_DTYPE), compiler_params=_params("parallel"),
    )(place, grad, slab)


def _slab_piece(name, ref, k):
    if name == "w_in":
        return ref.at[:, _col(k, W_IN_COLS)]
    if name == "w_out":
        return ref.at[_row(k, W_OUT_ROWS), :]
    if name == "w_qkv":
        return ref.at[:, _col(k, QKV_COLS)]
    if name == "w_o":
        return ref.at[k, 0]
    return ref.at[:, _row(k, POOL_ROWS), :]


def _sum_chips(item, slab, parts, place, into=None):
    name, sel = item
    half = HALF_SHAPES[name]
    hsel = (lambda p: p[1]) if sel is None else (lambda p: sel)
    if name == "w_in":
        blk, grid = (256, W_IN_COLS), (D_MODEL // 256,)
        own = pl.BlockSpec(blk, lambda i, p: (i, p[0]))
        got = pl.BlockSpec((N_CHIPS - 1,) + blk, lambda i, p: (0, i, 0))
        out = pl.BlockSpec((None,) + blk, lambda i, p: (hsel(p), i, 0))
    elif name == "w_out":
        grid = (1,)
        own = pl.BlockSpec(half, lambda i, p: (p[0], 0))
        got = pl.BlockSpec((N_CHIPS - 1,) + half, lambda i, p: (0, 0, 0))
        out = pl.BlockSpec((None,) + half, lambda i, p: (hsel(p), 0, 0))
    elif name == "w_qkv":
        grid = (1,)
        own = pl.BlockSpec(half, lambda i, p: (0, p[0]))
        got = pl.BlockSpec((N_CHIPS - 1,) + half, lambda i, p: (0, 0, 0))
        out = pl.BlockSpec(half, lambda i, p: (hsel(p), 0))
    else:
        grid = (1,)
        own = pl.BlockSpec((None, None) + half, lambda i, p: (p[0], 0, 0, 0))
        got = pl.BlockSpec((N_CHIPS - 1,) + half, lambda i, p: (0, 0, 0))
        out = pl.BlockSpec(half, lambda i, p: (hsel(p), 0))

    def body(p_ref, own_ref, got_ref, *rest):
        o_ref = rest[-1]

        @pl.when(hsel(p_ref) == p_ref[1])
        def _():
            acc = own_ref[...].astype(F32)
            for j in range(N_CHIPS - 1):
                acc = acc + got_ref[j].astype(F32)
            o_ref[...] = acc

    in_specs, operands, aliases = [own, got], [place, slab, parts], {}
    if into is not None:
        in_specs.append(ANY)
        operands.append(into)
        aliases = {3: 0}
    return pl.pallas_call(
        body, name=f"sum_chips_{_item_tag(item)}",
        grid_spec=pltpu.PrefetchScalarGridSpec(num_scalar_prefetch=1, grid=grid, in_specs=in_specs, out_specs=out),
        out_shape=jax.ShapeDtypeStruct(SHARD_SHAPES[name], F32), input_output_aliases=aliases,
        compiler_params=_params("parallel"),
    )(*operands)


N_DEV = 8


def _peer(d):
    x, y, c, _ = _mesh_place()
    px, py, pc = x ^ (d >> 2), y ^ ((d >> 1) & 1), c ^ (d & 1)
    return (px, py, pc), 4 * px + 2 * py + pc


def _small_all_reduce_host(vec):
    rows, cols = vec.shape

    def run(src, arr, new, scratch, finish):
        slots, stage, send, recv, local = scratch
        x, y, c, _ = _mesh_place()
        me = 4 * x + 2 * y + c
        mine = pltpu.make_async_copy(src[0], slots.at[me], local.at[0])
        copies = []
        for d in range(1, N_DEV):
            dev, _ = _peer(d)
            copies.append(pltpu.make_async_remote_copy(src_ref=src[0], dst_ref=slots.at[me], send_sem=send.at[d - 1],
                                                       recv_sem=recv.at[d - 1], device_id=dev, device_id_type=MESH))
        if not finish:
            mine.start()
            for cp in copies:
                cp.start()
            return
        for d in range(1, N_DEV):
            _, flat = _peer(d)
            got = slots.at[flat]
            pltpu.make_async_remote_copy(src_ref=got, dst_ref=got, send_sem=send.at[d - 1], recv_sem=recv.at[d - 1],
                                         device_id=(x, y, c), device_id_type=MESH).wait_recv()
        for cp in copies:
            cp.wait_send()
        mine.wait()
        acc = slots[0]
        for d in range(1, N_DEV):
            acc = acc + slots[d]
        stage[...] = acc
        out = pltpu.make_async_copy(stage, new[0], local.at[0])
        out.start()
        out.wait()

    scratch = [pltpu.VMEM((N_DEV, rows, cols), F32), pltpu.VMEM((rows, cols), F32),
               pltpu.SemaphoreType.DMA((N_DEV - 1,)), pltpu.SemaphoreType.DMA((N_DEV - 1,)),
               pltpu.SemaphoreType.DMA((1,))]
    return _Hosted([vec], [], [jax.ShapeDtypeStruct(vec.shape, F32)], scratch, functools.partial(run, finish=False),
                   functools.partial(run, finish=True))


class _Reducer:
    FIRST = (("w_in", 1), ("w_out", 1))
    MID = (("w_out", 0), ("w_qkv", None), ("w_o", None))
    LAST = (("w_in", 0),)

    def __init__(self, place):
        self.place = place
        self.shards = {}
        self.state = {}

    def swap(self, items, grads):
        self.state[items] = {"grads": [grads[nm] for nm, _ in items]}
        return _swap_host(items, grads)

    def pair_sums(self, items, slabs):
        st = self.state[items]
        st["sums"] = [_pair_sum(it, g, s, self.place) for it, g, s in zip(items, st["grads"], slabs)]

    def scatter(self, items):
        return _scatter_host(items, self.state[items]["sums"])

    def chip_sums(self, items, parts):
        for it, s, p in zip(items, self.state[items]["sums"], parts):
            self.shards[it[0]] = _sum_chips(it, s, p, self.place, into=self.shards.get(it[0]))

    def share(self, items):
        return _share_host(items, [self.shards[nm] for nm, _ in items])

    def shared(self, items, updated):
        self.shards.update(dict(zip(_names(items), updated)))

    def small_early(self, pieces):
        return _small_all_reduce_host(_pack_small(pieces, EARLY_SMALL))

    def small_early_done(self, total):
        self.early_small = _unpack_small(total, EARLY_SMALL)


def _all_reduce_small(vec, g_pool, share_items, shards):
    n_share = len(share_items)

    def body(v_ref, gp_ref, *rest):
        o_ref, po_ref = rest[n_share:n_share + 2]
        shard_refs = rest[n_share + 2:2 * n_share + 2]
        (slots, pslots, send_sems, recv_sems, psend, precv, hsend, hrecv, share_send,
         share_recv) = rest[2 * n_share + 2:]
        x, y, c, _ = _mesh_place()
        me = 4 * x + 2 * y + c
        _reduce_share(share_items, shard_refs, (share_send, share_recv), finish=False)

        def pool_piece(chip, h):
            return gp_ref.at[pl.ds(2 * h, 2), pl.ds(pl.multiple_of(chip * POOL_ROWS, POOL_ROWS), POOL_ROWS), :]

        slots[me] = v_ref[...]
        pslots[me] = pool_piece(2 * x + y, c)[...]
        copies = []
        for d in range(1, N_DEV):
            peer = (x ^ (d >> 2), y ^ ((d >> 1) & 1), c ^ (d & 1))
            copies.append(pltpu.make_async_remote_copy(
                src_ref=v_ref, dst_ref=slots.at[me], send_sem=send_sems.at[d - 1], recv_sem=recv_sems.at[d - 1],
                device_id=peer, device_id_type=MESH))
            copies.append(pltpu.make_async_remote_copy(
                src_ref=pool_piece(2 * peer[0] + peer[1], peer[2]), dst_ref=pslots.at[me], send_sem=psend.at[d - 1],
                recv_sem=precv.at[d - 1], device_id=peer, device_id_type=MESH))
        for cp in copies:
            cp.start()
        for d in range(1, N_DEV):
            peer = 4 * (x ^ (d >> 2)) + 2 * (y ^ ((d >> 1) & 1)) + (c ^ (d & 1))
            for buf, ss, rs in ((slots, send_sems, recv_sems), (pslots, psend, precv)):
                got = buf.at[peer]
                pltpu.make_async_remote_copy(src_ref=got, dst_ref=got, send_sem=ss.at[d - 1], recv_sem=rs.at[d - 1],
                                             device_id=(x, y, c), device_id_type=MESH).wait_recv()
        for cp in copies:
            cp.wait_send()
        acc, pacc = slots[0], pslots[0]
        for d in range(1, N_DEV):
            acc, pacc = acc + slots[d], pacc + pslots[d]
        o_ref[...] = acc
        po_ref[pl.ds(2 * c, 2)] = pacc
        mine, theirs = po_ref.at[pl.ds(2 * c, 2)], po_ref.at[pl.ds(2 * (1 - c), 2)]
        give = pltpu.make_async_remote_copy(src_ref=mine, dst_ref=mine, send_sem=hsend.at[0], recv_sem=hrecv.at[0],
                                            device_id=(x, y, 1 - c), device_id_type=MESH)
        give.start()
        pltpu.make_async_remote_copy(src_ref=theirs, dst_ref=theirs, send_sem=hsend.at[0], recv_sem=hrecv.at[0],
                                     device_id=(x, y, 1 - c), device_id_type=MESH).wait_recv()
        give.wait_send()
        _reduce_share(share_items, shard_refs, (share_send, share_recv), finish=True)

    vm = pl.BlockSpec(memory_space=pltpu.VMEM)
    piece = (2, POOL_ROWS, POOL_GC)
    return pl.pallas_call(
        body, name="all_reduce_small", in_specs=[vm, vm] + [ANY] * n_share, out_specs=[vm, vm] + [ANY] * n_share,
        out_shape=[jax.ShapeDtypeStruct(vec.shape, F32), jax.ShapeDtypeStruct(SHARD_SHAPES["pool_w"], F32)]
        + [jax.ShapeDtypeStruct(a.shape, a.dtype) for a in shards],
        input_output_aliases={2 + t: 2 + t for t in range(n_share)},
        scratch_shapes=[pltpu.VMEM((N_DEV,) + vec.shape, F32), pltpu.VMEM((N_DEV,) + piece, F32)]
        + [pltpu.SemaphoreType.DMA((N_DEV - 1,))] * 4 + [pltpu.SemaphoreType.DMA((1,))] * 2 + _pair_sems(n_share),
    )(vec, g_pool, *shards)


def _adamw_call(w, g, m, v, name, copy_grad=False):
    shape = w.shape
    cols = shape[-1]
    rows = w.size // cols
    tr = rows
    for cand in (256, 128, 64, 32, 16, 8):
        if rows > cand and rows % cand == 0 and cand * cols * 4 <= 2 * 1024 * 1024:
            tr = cand
            break

    def body(w_ref, g_ref, m_ref, v_ref, d_ref, nm_ref, nv_ref, *g_out):
        gv = g_ref[...]
        if copy_grad:
            g_out[0][...] = gv
        mn = ADAM_B1 * m_ref[...] + (1.0 - ADAM_B1) * gv
        vn = ADAM_B2 * v_ref[...] + (1.0 - ADAM_B2) * jnp.square(gv)
        m_hat = mn / (1.0 - ADAM_B1 ** ADAM_STEP)
        v_hat = vn / (1.0 - ADAM_B2 ** ADAM_STEP)
        d_ref[...] = -ADAM_LR * (m_hat / (jnp.sqrt(v_hat) + ADAM_EPS) + ADAM_WD * w_ref[...])
        nm_ref[...] = mn
        nv_ref[...] = vn

    spec = pl.BlockSpec((tr, cols), lambda i: (i, 0))
    flat = lambda a: a.reshape(rows, cols)
    n_out = 4 if copy_grad else 3
    outs = pl.pallas_call(
        body, name=name, grid=(rows // tr,), in_specs=[spec] * 4, out_specs=[spec] * n_out,
        out_shape=[jax.ShapeDtypeStruct((rows, cols), F32)] * n_out, compiler_params=_params("parallel"),
    )(flat(w), flat(g), flat(m), flat(v))
    return tuple(o.reshape(shape) for o in outs)


WEIGHTS = ("pool_w", "pool_b", "pool_scale", "attn_w_qkv", "attn_b_qkv", "attn_sinks", "attn_w_o", "norm_mix",
           "norm_ffn", "ffn_w_in", "ffn_conv_w", "ffn_conv_b", "ffn_w_out", "norm_f")
SMALL_GATHER = 128 * 71
EARLY_SMALL = ((("b_qkv", 1536), ("sinks", 128), ("norm_mix1", 1024), ("norm_ffn1", 1024), ("conv_w", 6 * 2 * D_FF),
                ("conv_b", 4 * D_FF), ("norm_f", 1024), ("loss", 128)), 6272)
LATE_SMALL = ((("pool_b", 1024), ("pool_scale", 1024), ("norm_mix0", 1024), ("norm_ffn0", 1024)), 512)


def _pack_small(pieces, layout):
    sizes, cols = layout
    flat = jnp.concatenate([jnp.pad(pieces[nm].reshape(-1), (0, size - pieces[nm].size)) for nm, size in sizes])
    return jnp.pad(flat, (0, 8 * cols - flat.size)).reshape(8, cols)


def _unpack_small(vec, layout):
    flat = vec.reshape(-1)
    out, off = {}, 0
    for nm, size in layout[0]:
        out[nm] = flat[off:off + size]
        off += size
    return out


def kernel(x, pool_w, pool_b, pool_scale, attn_w_qkv, attn_b_qkv, attn_sinks, attn_w_o, norm_mix, norm_ffn, ffn_w_in, ffn_conv_w, ffn_conv_b, ffn_w_out, norm_f, loss_target, m_pool_w, m_pool_b, m_pool_scale, m_attn_w_qkv, m_attn_b_qkv, m_attn_sinks, m_attn_w_o, m_norm_mix, m_norm_ffn, m_ffn_w_in, m_ffn_conv_w, m_ffn_conv_b, m_ffn_w_out, m_norm_f, v_pool_w, v_pool_b, v_pool_scale, v_attn_w_qkv, v_attn_b_qkv, v_attn_sinks, v_attn_w_o, v_norm_mix, v_norm_ffn, v_ffn_w_in, v_ffn_conv_w, v_ffn_conv_b, v_ffn_w_out, v_norm_f):
    weights = dict(pool_w=pool_w, pool_b=pool_b, pool_scale=pool_scale, attn_w_qkv=attn_w_qkv, attn_b_qkv=attn_b_qkv,
                   attn_sinks=attn_sinks, attn_w_o=attn_w_o, norm_mix=norm_mix, norm_ffn=norm_ffn, ffn_w_in=ffn_w_in,
                   ffn_conv_w=ffn_conv_w, ffn_conv_b=ffn_conv_b, ffn_w_out=ffn_w_out, norm_f=norm_f)
    m_in = dict(pool_w=m_pool_w, pool_b=m_pool_b, pool_scale=m_pool_scale, attn_w_qkv=m_attn_w_qkv,
                attn_b_qkv=m_attn_b_qkv, attn_sinks=m_attn_sinks, attn_w_o=m_attn_w_o, norm_mix=m_norm_mix,
                norm_ffn=m_norm_ffn, ffn_w_in=m_ffn_w_in, ffn_conv_w=m_ffn_conv_w, ffn_conv_b=m_ffn_conv_b,
                ffn_w_out=m_ffn_w_out, norm_f=m_norm_f)
    v_in = dict(pool_w=v_pool_w, pool_b=v_pool_b, pool_scale=v_pool_scale, attn_w_qkv=v_attn_w_qkv,
                attn_b_qkv=v_attn_b_qkv, attn_sinks=v_attn_sinks, attn_w_o=v_attn_w_o, norm_mix=v_norm_mix,
                norm_ffn=v_norm_ffn, ffn_w_in=v_ffn_w_in, ffn_conv_w=v_ffn_conv_w, ffn_conv_b=v_ffn_conv_b,
                ffn_w_out=v_ffn_w_out, norm_f=v_norm_f)
    chip = 2 * lax.axis_index("x") + lax.axis_index("y")
    core = lax.axis_index("c")

    place = jnp.stack([chip, core]).astype(jnp.int32)
    shards = {"w_in": ffn_w_in, "w_out": ffn_w_out, "w_qkv": attn_w_qkv[0], "w_o": attn_w_o[0], "pool_w": pool_w[0]}
    placed = {nm: _place_shard(nm, shards[nm], place) for nm in BIG}
    small = jnp.concatenate([pool_b.reshape(-1), attn_b_qkv.reshape(-1), ffn_conv_w.reshape(-1)]).reshape(1, SMALL_GATHER)
    full, small_all = _gather_early(placed, small)
    small_all = small_all.reshape(N_CHIPS, SMALL_GATHER)
    pool_b_full = small_all[:, :256].reshape(N_CHIPS, 4, POOL_ROWS).transpose(1, 0, 2).reshape(1, D_MODEL)
    b_qkv_full = small_all[:, 256:640].reshape(1, QKV_DIM)
    conv_w_full = small_all[:, 640:].reshape(N_CHIPS, 2, 3, W_IN_COLS).transpose(1, 2, 0, 3).reshape(2, 3, 2 * D_FF)

    reducer = _Reducer(place)
    _, grad_x, grads = _local_step(
        x[0], loss_target[0], full, norm_mix=norm_mix, norm_ffn=norm_ffn, norm_f=norm_f.reshape(1, D_MODEL),
        pool_b=pool_b_full, pool_scale=pool_scale, b_qkv=b_qkv_full, sinks=attn_sinks[0], conv_w=conv_w_full,
        conv_b=ffn_conv_b, late_gather=_late_gather_host, reducer=reducer)
    reduced = dict(reducer.shards)

    late = {"pool_b": grads["pool_b"], "pool_scale": grads["pool_scale"], "norm_mix0": grads["norm_mix"][0],
            "norm_ffn0": grads["norm_ffn"][0]}
    last = _Reducer.LAST
    late_sum, reduced["pool_w"], *shared = _all_reduce_small(_pack_small(late, LATE_SMALL), grads["pool_w"], last,
                                                             [reduced[nm] for nm in _names(last)])
    reduced.update(zip(_names(last), shared))
    small_g = {**reducer.early_small, **_unpack_small(late_sum, LATE_SMALL)}
    small_g["norm_mix"] = jnp.concatenate([small_g["norm_mix0"], small_g["norm_mix1"]])
    small_g["norm_ffn"] = jnp.concatenate([small_g["norm_ffn0"], small_g["norm_ffn1"]])
    loss = small_g["loss"][0]
    pool_b_g = lax.dynamic_slice_in_dim(small_g["pool_b"].reshape(4, N_CHIPS, POOL_ROWS), chip, 1, axis=1)
    b_qkv_g = lax.dynamic_slice_in_dim(small_g["b_qkv"].reshape(N_CHIPS, QKV_COLS), chip, 1, axis=0)
    conv_w_g = lax.dynamic_slice_in_dim(small_g["conv_w"].reshape(2, 3, N_CHIPS, W_IN_COLS), chip, 1, axis=2)
    grad_w = {
        "pool_w": reduced["pool_w"].reshape(pool_w.shape), "pool_b": pool_b_g.reshape(pool_b.shape),
        "pool_scale": small_g["pool_scale"].reshape(pool_scale.shape),
        "attn_w_qkv": reduced["w_qkv"].reshape(attn_w_qkv.shape), "attn_b_qkv": b_qkv_g.reshape(attn_b_qkv.shape),
        "attn_sinks": small_g["sinks"][:N_HEADS].reshape(attn_sinks.shape),
        "attn_w_o": reduced["w_o"].reshape(attn_w_o.shape), "norm_mix": small_g["norm_mix"].reshape(norm_mix.shape),
        "norm_ffn": small_g["norm_ffn"].reshape(norm_ffn.shape), "ffn_w_in": reduced["w_in"],
        "ffn_conv_w": conv_w_g.reshape(ffn_conv_w.shape), "ffn_conv_b": small_g["conv_b"].reshape(ffn_conv_b.shape),
        "ffn_w_out": reduced["w_out"], "norm_f": small_g["norm_f"].reshape(norm_f.shape),
    }

    delta, new_m, new_v = {}, {}, {}
    exchanged = ("pool_w", "attn_w_qkv", "attn_w_o", "ffn_w_in", "ffn_w_out")
    for nm in WEIGHTS:
        res = _adamw_call(weights[nm], grad_w[nm], m_in[nm], v_in[nm], f"adamw_{nm}", copy_grad=nm in exchanged)
        delta[nm], new_m[nm], new_v[nm] = res[:3]
        if nm in exchanged:
            grad_w[nm] = res[3]
    return (loss, grad_x.reshape(x.shape), *[grad_w[nm] for nm in WEIGHTS], *[delta[nm] for nm in WEIGHTS],
            *[new_m[nm] for nm in WEIGHTS], *[new_v[nm] for nm in WEIGHTS])
```

```python
import functools
from typing import Callable, NamedTuple

import jax
import jax.numpy as jnp
from jax import lax
from jax.experimental import pallas as pl
from jax.experimental.pallas import tpu as pltpu

F32 = jnp.float32
MXU_DTYPE = jnp.bfloat16
WIRE_DTYPE = jnp.bfloat16
U_DTYPE = jnp.float32

D_MODEL = 1024
D_FF = 2816
QKV_DIM = 1536
HEAD_DIM = 64
N_HEADS = 16
N_KV_HEADS = 4
ATT_BLOCK = 128
POOL_WINDOWS = (2, 4, 8, 16)
POOL_GC = 256
POOL_HALO = 16
CONV_HALO = 8
RMS_EPS = 1e-6
ATT_SCALE = HEAD_DIM ** -0.5
ALIBI_SLOPES = tuple(2.0 ** (-8.0 / N_HEADS * (h + 1)) for h in range(N_HEADS))

ADAM_LR = 0.001
ADAM_B1 = 0.9
ADAM_B2 = 0.999
ADAM_EPS = 1e-08
ADAM_WD = 0.01
ADAM_STEP = 10

N_CHIPS = 4
MESH = pl.DeviceIdType.MESH
VMEM_LIMIT_BYTES = 56 * 1024 * 1024
ANY = pl.BlockSpec(memory_space=pl.ANY)


def _params(*semantics):
    return pltpu.CompilerParams(dimension_semantics=semantics, vmem_limit_bytes=VMEM_LIMIT_BYTES)


def _rms(x, g):
    return x * lax.rsqrt(jnp.mean(x * x, axis=-1, keepdims=True) + RMS_EPS) * g


def _rms_bwd(x, g, dy):
    rstd = lax.rsqrt(jnp.mean(x * x, axis=-1, keepdims=True) + RMS_EPS)
    xhat = x * rstd
    dxhat = dy * g
    dx = rstd * (dxhat - xhat * jnp.mean(dxhat * xhat, axis=-1, keepdims=True))
    return dx, dy * xhat


def _shift_down(x, s):
    return pltpu.roll(x, s, axis=0)


def _shift_up(x, s):
    return pltpu.roll(x, x.shape[0] - s, axis=0)


def _sigmoid(x):
    return 1.0 / (1.0 + jnp.exp(-x))


_DOT_DIMS = {"nn": ((1,), (0,)), "nt": ((1,), (1,)), "tn": ((0,), (0,))}


class _Hosted(NamedTuple):
    sources: list
    arrays: list
    new: list
    sems: list
    first: Callable
    last: Callable


def _call(body, hosted, *, name, grid, in_specs, out_specs, out_shape, operands, semantics, scratch_shapes=(),
          aliases=None):
    n_in, n_out, n_scr = len(in_specs), len(out_specs), len(scratch_shapes)
    aliases = dict(aliases or {})
    if hosted is None:
        def plain(*refs):
            body(refs[:n_in], refs[n_in:n_in + n_out], refs[n_in + n_out:], lambda: None, lambda: None)

        return pl.pallas_call(plain, name=name, grid=grid, in_specs=list(in_specs), out_specs=list(out_specs),
                              out_shape=list(out_shape), scratch_shapes=list(scratch_shapes),
                              input_output_aliases=aliases, compiler_params=_params(*semantics))(*operands)
    ns, na, nn = len(hosted.sources), len(hosted.arrays), len(hosted.new)

    def carrying(*refs):
        ins, src = refs[:n_in], refs[n_in:n_in + ns]
        o0 = n_in + ns + na
        outs, arr = refs[o0:o0 + n_out], refs[o0 + n_out:o0 + n_out + na]
        new = refs[o0 + n_out + na:o0 + n_out + na + nn]
        s0 = o0 + n_out + na + nn
        scratch, sems = refs[s0:s0 + n_scr], refs[s0 + n_scr:]
        ids = [pl.program_id(d) for d in range(len(grid))]
        is_first = functools.reduce(lambda p, q: p & q, [i == 0 for i in ids])
        is_last = functools.reduce(lambda p, q: p & q, [i == n - 1 for i, n in zip(ids, grid)])
        body(ins, outs, scratch, lambda: pl.when(is_first)(lambda: hosted.first(src, arr, new, sems)),
             lambda: pl.when(is_last)(lambda: hosted.last(src, arr, new, sems)))

    for t in range(na):
        aliases[n_in + ns + t] = n_out + t
    return pl.pallas_call(
        carrying, name=name, grid=grid, in_specs=list(in_specs) + [ANY] * (ns + na),
        out_specs=list(out_specs) + [ANY] * (na + nn),
        out_shape=list(out_shape) + [jax.ShapeDtypeStruct(a.shape, a.dtype) for a in hosted.arrays] + list(hosted.new),
        scratch_shapes=list(scratch_shapes) + list(hosted.sems), input_output_aliases=aliases,
        compiler_params=_params(*["arbitrary"] * len(grid)))(*operands, *hosted.sources, *hosted.arrays)


def _both(h1, h2):
    def run(which, src, arr, new, sems):
        cut = lambda seq, n: (seq[:n], seq[n:])
        (s1, s2), (a1, a2) = cut(src, len(h1.sources)), cut(arr, len(h1.arrays))
        (n1, n2), (m1, m2) = cut(new, len(h1.new)), cut(sems, len(h1.sems))
        getattr(h1, which)(s1, a1, n1, m1)
        getattr(h2, which)(s2, a2, n2, m2)

    return _Hosted(h1.sources + h2.sources, h1.arrays + h2.arrays, h1.new + h2.new, h1.sems + h2.sems,
                   functools.partial(run, "first"), functools.partial(run, "last"))


def _ride(hooks, stage, call, env=None):
    if not hooks or stage not in hooks:
        res = call(None)
        return list(res) if isinstance(res, (list, tuple)) else [res]
    make, done = hooks[stage]
    hosted = make(env)
    res = call(hosted)
    own = len(res) - len(hosted.arrays) - len(hosted.new)
    done(res[own:])
    return list(res[:own])


def _matmul(a, b, *, mode, name, grid, a_spec, b_spec, o_spec, out_shape, bias=None, bias_spec=None, residual=None,
            residual_spec=None, into=None, hosted=None, norm_g=None):
    nk = grid[2]
    dims = (_DOT_DIMS[mode], ((), ()))
    acc_shape = tuple(d for d in o_spec.block_shape if d is not None)

    def body(ins, outs, scratch, first, last):
        a_ref, b_ref = ins[0], ins[1]
        pos = 2
        bias_ref = res_ref = None
        if bias is not None:
            bias_ref = ins[pos]
            pos += 1
        if residual is not None:
            res_ref = ins[pos]
            pos += 1
        o_ref = outs[0]
        acc_ref = scratch[0] if nk > 1 else None
        first()
        prod = lax.dot_general(a_ref[...].astype(MXU_DTYPE), b_ref[...].astype(MXU_DTYPE), dims,
                               preferred_element_type=F32)

        def finish(acc):
            if bias_ref is not None:
                acc = acc + bias_ref[...]
            if res_ref is not None:
                acc = acc + res_ref[...]
            o_ref[...] = acc.astype(o_ref.dtype)
            if norm_g is not None:
                outs[1][...] = _rms(acc, ins[pos][...]).astype(outs[1].dtype)

        if nk == 1:
            finish(prod)
        else:
            k = pl.program_id(2)

            @pl.when(k == 0)
            def _():
                acc_ref[...] = prod

            @pl.when(k > 0)
            def _():
                acc_ref[...] += prod

            @pl.when(k == nk - 1)
            def _():
                finish(acc_ref[...])

        last()

    operands, in_specs = [a, b], [a_spec, b_spec]
    if bias is not None:
        operands.append(bias)
        in_specs.append(bias_spec)
    if residual is not None:
        operands.append(residual)
        in_specs.append(residual_spec)
    out_specs, out_shapes = [o_spec], [out_shape]
    if norm_g is not None:
        assert acc_shape[-1] == norm_g.shape[-1], "the fused norm needs whole rows in an output tile"
        operands.append(norm_g)
        in_specs.append(bias_spec)
        out_specs.append(o_spec)
        out_shapes.append(jax.ShapeDtypeStruct(out_shape.shape, MXU_DTYPE))
    aliases = {}
    if into is not None:
        aliases = {len(operands): 0}
        operands.append(into)
        in_specs.append(ANY)
    res = _call(body, hosted, name=name, grid=grid, in_specs=in_specs, out_specs=out_specs, out_shape=out_shapes,
                operands=operands, semantics=("parallel", "parallel", "arbitrary"),
                scratch_shapes=[pltpu.VMEM(acc_shape, F32)] if nk > 1 else [], aliases=aliases)
    return res if hosted or norm_g is not None else res[0]


def _tile(n, want):
    t = min(n, want)
    assert n % t == 0, (n, want)
    return t


def _mm_nn(a, b, name, *, b_lead=None, bias=None, residual=None, out_dtype=F32, tm=1024, tn=1024, tk=1024,
           hosted=None, norm_g=None):
    (m, k), n = a.shape, b.shape[-1]
    tm, tn, tk = _tile(m, tm), _tile(n, tn), _tile(k, tk)
    if b_lead is None:
        b_spec = pl.BlockSpec((tk, tn), lambda i, j, kk: (kk, j))
    else:
        b_spec = pl.BlockSpec((None, tk, tn), lambda i, j, kk: (b_lead, kk, j))
    return _matmul(
        a, b, mode="nn", name=name, grid=(m // tm, n // tn, k // tk),
        a_spec=pl.BlockSpec((tm, tk), lambda i, j, kk: (i, kk)), b_spec=b_spec,
        o_spec=pl.BlockSpec((tm, tn), lambda i, j, kk: (i, j)), out_shape=jax.ShapeDtypeStruct((m, n), out_dtype),
        bias=bias, bias_spec=pl.BlockSpec((1, tn), lambda i, j, kk: (0, j)),
        residual=residual, residual_spec=pl.BlockSpec((tm, tn), lambda i, j, kk: (i, j)), hosted=hosted,
        norm_g=norm_g)


def _mm_nt(a, b, name, *, b_lead=None, out_dtype=F32, tm=1024, tn=1024, tk=1024, hosted=None):
    (m, k), n = a.shape, b.shape[-2]
    tm, tn, tk = _tile(m, tm), _tile(n, tn), _tile(k, tk)
    if b_lead is None:
        b_spec = pl.BlockSpec((tn, tk), lambda i, j, kk: (j, kk))
    else:
        b_spec = pl.BlockSpec((None, tn, tk), lambda i, j, kk: (b_lead, j, kk))
    return _matmul(
        a, b, mode="nt", name=name, grid=(m // tm, n // tn, k // tk),
        a_spec=pl.BlockSpec((tm, tk), lambda i, j, kk: (i, kk)), b_spec=b_spec,
        o_spec=pl.BlockSpec((tm, tn), lambda i, j, kk: (i, j)), out_shape=jax.ShapeDtypeStruct((m, n), out_dtype),
        hosted=hosted)


def _mm_tn(a, b, name, *, lead=None, n_lead=None, into=None, tm=1024, tn=1024, tk=1024, out_dtype=F32, hosted=None):
    (k, m), n = a.shape, b.shape[-1]
    tm, tn, tk = _tile(m, tm), _tile(n, tn), _tile(k, tk)
    if lead is None:
        o_spec = pl.BlockSpec((tm, tn), lambda i, j, kk: (i, j))
        out_shape = jax.ShapeDtypeStruct((m, n), out_dtype)
    else:
        o_spec = pl.BlockSpec((None, tm, tn), lambda i, j, kk: (lead, i, j))
        out_shape = jax.ShapeDtypeStruct((n_lead, m, n), out_dtype)
    return _matmul(
        a, b, mode="tn", name=name, grid=(m // tm, n // tn, k // tk),
        a_spec=pl.BlockSpec((tk, tm), lambda i, j, kk: (kk, i)),
        b_spec=pl.BlockSpec((tk, tn), lambda i, j, kk: (kk, j)), o_spec=o_spec, out_shape=out_shape, into=into,
        hosted=hosted)


ROW_TILE = 512


def _pool_windows_causal(ext, first_row):
    tt = ext.shape[0] - POOL_HALO
    t = first_row + lax.broadcasted_iota(jnp.int32, (tt, 1), 0)
    outs = []
    for gi, win in enumerate(POOL_WINDOWS):
        cols = slice(gi * POOL_GC, (gi + 1) * POOL_GC)
        s = ext[:, cols]
        sh = 1
        while sh < win:
            s = s + _shift_down(s, sh)
            sh *= 2
        count = jnp.minimum(t + 1, win).astype(F32)
        outs.append(s[POOL_HALO:] / count - ext[POOL_HALO:, cols])
    return outs


def _prev_halo_spec(tt, halo, width):
    per = tt // halo
    return pl.BlockSpec((halo, width), lambda i: (jnp.maximum(i * per - 1, 0), 0))


def _pool_fwd_call(h0, g_mix, pool_w, pool_b, pool_scale, g_ffn, hosted=None):
    t = h0.shape[0]
    tt = _tile(t, ROW_TILE)

    def body(ins, outs, scratch, first, last):
        h_ref, halo_ref, gm_ref, w_ref, b_ref, sc_ref, gf_ref = ins
        h1_ref, hn_ref = outs
        i = pl.program_id(0)
        first()
        x, gm = h_ref[...], gm_ref[...]
        top = jnp.where(i > 0, _rms(halo_ref[...], gm), 0.0)
        ext = jnp.concatenate([top, _rms(x, gm)], axis=0)
        ps = _pool_windows_causal(ext, i * tt)
        ys = [jnp.dot(p.astype(MXU_DTYPE), w_ref[gi], preferred_element_type=F32) for gi, p in enumerate(ps)]
        mix = (jnp.concatenate(ys, axis=1) + b_ref[...]) * sc_ref[...]
        h1 = x + mix
        h1_ref[...] = h1
        hn_ref[...] = _rms(h1, gf_ref[...]).astype(hn_ref.dtype)
        last()

    row = pl.BlockSpec((tt, D_MODEL), lambda i: (i, 0))
    vec = pl.BlockSpec((1, D_MODEL), lambda i: (0, 0))
    wsp = pl.BlockSpec((len(POOL_WINDOWS), POOL_GC, POOL_GC), lambda i: (0, 0, 0))
    return _call(
        body, hosted, name="pool_fwd", grid=(t // tt,),
        in_specs=[row, _prev_halo_spec(tt, POOL_HALO, D_MODEL), vec, wsp, vec, vec, vec], out_specs=[row, row],
        out_shape=[jax.ShapeDtypeStruct(h0.shape, F32), jax.ShapeDtypeStruct(h0.shape, MXU_DTYPE)],
        operands=(h0, h0, g_mix, pool_w, pool_b, pool_scale, g_ffn), semantics=("parallel",))


def _pool_bwd_call(h0, dh1, g_mix, pool_w, pool_b, pool_scale, hosted=None):
    t = h0.shape[0]
    tt = _tile(t, ROW_TILE)
    nt = t // tt
    per = tt // POOL_HALO

    def body(ins, outs, scratch, first, last):
        h_ref, halo_ref, d_ref, dnext_ref, gm_ref, w_ref, b_ref, sc_ref = ins
        gx_ref, dw_ref, dv_ref = outs
        i = pl.program_id(0)
        first()
        x, gm, sc = h_ref[...], gm_ref[...], sc_ref[...]
        top = jnp.where(i > 0, _rms(halo_ref[...], gm), 0.0)
        ext = jnp.concatenate([top, _rms(x, gm)], axis=0)
        ps = _pool_windows_causal(ext, i * tt)

        dy = d_ref[...]
        dy_ext = jnp.concatenate([dy, jnp.where(i < nt - 1, dnext_ref[...], 0.0)], axis=0)
        dpre_ext = dy_ext * sc
        t_ext = i * tt + lax.broadcasted_iota(jnp.int32, (tt + POOL_HALO, 1), 0)

        @pl.when(i == 0)
        def _():
            dw_ref[...] = jnp.zeros_like(dw_ref)
            dv_ref[...] = jnp.zeros_like(dv_ref)

        dhn, ypre = [], []
        for gi, win in enumerate(POOL_WINDOWS):
            cols = slice(gi * POOL_GC, (gi + 1) * POOL_GC)
            w = w_ref[gi]
            p16 = ps[gi].astype(MXU_DTYPE)
            dpre16 = dpre_ext[:, cols].astype(MXU_DTYPE)
            ypre.append(jnp.dot(p16, w, preferred_element_type=F32))
            dw_ref[gi] += lax.dot_general(p16, dpre16[:tt], (((0,), (0,)), ((), ())), preferred_element_type=F32)
            dp_ext = lax.dot_general(dpre16, w, (((1,), (1,)), ((), ())), preferred_element_type=F32)
            s = dp_ext / jnp.minimum(t_ext + 1, win).astype(F32)
            sh = 1
            while sh < win:
                s = s + _shift_up(s, sh)
                sh *= 2
            dhn.append(s[:tt] - dp_ext[:tt])
        dhn = jnp.concatenate(dhn, axis=1)
        ypre = jnp.concatenate(ypre, axis=1) + b_ref[...]
        dx, dgt = _rms_bwd(x, gm, dhn)
        gx_ref[...] = dy + dx
        dv_ref[0:1, :] += jnp.sum(dpre_ext[:tt], axis=0, keepdims=True)
        dv_ref[1:2, :] += jnp.sum(dy * ypre, axis=0, keepdims=True)
        dv_ref[2:3, :] += jnp.sum(dgt, axis=0, keepdims=True)
        last()

    row = pl.BlockSpec((tt, D_MODEL), lambda i: (i, 0))
    vec = pl.BlockSpec((1, D_MODEL), lambda i: (0, 0))
    nxt = pl.BlockSpec((POOL_HALO, D_MODEL), lambda i: (jnp.minimum((i + 1) * per, t // POOL_HALO - 1), 0))
    wsp = pl.BlockSpec((len(POOL_WINDOWS), POOL_GC, POOL_GC), lambda i: (0, 0, 0))
    return _call(
        body, hosted, name="pool_bwd", grid=(nt,),
        in_specs=[row, _prev_halo_spec(tt, POOL_HALO, D_MODEL), row, nxt, vec, wsp, vec, vec],
        out_specs=[row, wsp, pl.BlockSpec((8, D_MODEL), lambda i: (0, 0))],
        out_shape=[jax.ShapeDtypeStruct(h0.shape, F32),
                   jax.ShapeDtypeStruct((len(POOL_WINDOWS), POOL_GC, POOL_GC), F32),
                   jax.ShapeDtypeStruct((8, D_MODEL), F32)],
        operands=(h0, h0, dh1, dh1, g_mix, pool_w, pool_b, pool_scale), semantics=("arbitrary",))


CONV_ROWS = 512
CONV_SUB = 128
LANES = 128


def _causal_conv(u_ext, w, b):
    return w[0:1] * _shift_down(u_ext, 2) + w[1:2] * _shift_down(u_ext, 1) + w[2:3] * u_ext + b


def _piece_rows(load, s, sub, tt, top, bottom=None):
    lo, hi = s * sub - CONV_HALO, (s + 1) * sub + (CONV_HALO if bottom else 0)
    parts = [top()] if lo < 0 else []
    lo = max(lo, 0)
    parts.append(load(lo, min(hi, tt) - lo))
    if hi > tt:
        parts.append(bottom())
    return parts[0] if len(parts) == 1 else jnp.concatenate(parts, axis=0)


def _fold8(x):
    acc = x[0:8]
    for r in range(8, x.shape[0], 8):
        acc = acc + x[r:r + 8]
    return acc


def _conv_glu_fwd_call(u, conv_w, conv_b, name, hosted=None):
    _, t, f = u.shape
    tt = _tile(t, CONV_ROWS)
    sub = _tile(tt, CONV_SUB)
    per = tt // CONV_HALO

    def body(ins, outs, scratch, first, last):
        u_ref, up_ref, w_ref, b_ref = ins
        z_ref = outs[0]
        i = pl.program_id(0)
        first()

        def chunk(j, carry):
            lanes = pl.ds(pl.multiple_of(j * LANES, LANES), LANES)
            w, b = w_ref[:, :, lanes], b_ref[:, lanes]
            for s in range(tt // sub):
                c = []
                for hf in range(2):
                    ext = _piece_rows(lambda at, n: u_ref[hf, pl.ds(at, n), lanes].astype(F32), s, sub, tt,
                                      top=lambda: jnp.where(i > 0, up_ref[hf, :, lanes].astype(F32), 0.0))
                    c.append(_causal_conv(ext, w[:, hf, :], b[hf:hf + 1, :])[CONV_HALO:])
                z_ref[pl.ds(s * sub, sub), lanes] = (c[0] * c[1] * _sigmoid(c[1])).astype(z_ref.dtype)
            return carry

        lax.fori_loop(0, f // LANES, chunk, 0)
        last()

    res = _call(
        body, hosted, name=name, grid=(t // tt,),
        in_specs=[pl.BlockSpec((2, tt, f), lambda i: (0, i, 0)),
                  pl.BlockSpec((2, CONV_HALO, f), lambda i: (0, jnp.maximum(i * per - 1, 0), 0)),
                  pl.BlockSpec((3, 2, f), lambda i: (0, 0, 0)), pl.BlockSpec((2, f), lambda i: (0, 0))],
        out_specs=[pl.BlockSpec((tt, f), lambda i: (i, 0))], out_shape=[jax.ShapeDtypeStruct((t, f), MXU_DTYPE)],
        operands=(u, u, conv_w, conv_b), semantics=("parallel",))
    return res if hosted else res[0]


def _conv_glu_bwd_call(u, dz, conv_w, conv_b, name, hosted=None):
    _, t, f = u.shape
    tt = _tile(t, CONV_ROWS)
    sub = _tile(tt, CONV_SUB)
    nt = t // tt
    per = tt // CONV_HALO
    halo = CONV_HALO

    def body(ins, outs, scratch, first, last):
        u_ref, up_ref, un_ref, dz_ref, dzn_ref, w_ref, b_ref = ins
        du_ref, dw_ref, db_ref = outs
        dwacc, dbacc = scratch
        i = pl.program_id(0)
        first()

        @pl.when(i == 0)
        def _():
            dwacc[...] = jnp.zeros_like(dwacc)
            dbacc[...] = jnp.zeros_like(dbacc)

        main = slice(halo, halo + sub)

        def chunk(j, carry):
            lanes = pl.ds(pl.multiple_of(j * LANES, LANES), LANES)
            w, b = w_ref[:, :, lanes], b_ref[:, lanes]
            for s in range(tt // sub):
                ue = [_piece_rows(lambda at, n: u_ref[hf, pl.ds(at, n), lanes].astype(F32), s, sub, tt,
                                  top=lambda: jnp.where(i > 0, up_ref[hf, :, lanes].astype(F32), 0.0),
                                  bottom=lambda: un_ref[hf, :, lanes].astype(F32)) for hf in range(2)]
                a, g = [_causal_conv(ue[hf], w[:, hf, :], b[hf:hf + 1, :])[halo:] for hf in range(2)]
                sg = _sigmoid(g)
                if (s + 1) * sub < tt:
                    dzs = dz_ref[pl.ds(s * sub, sub + 2 * halo), lanes].astype(F32)[:sub + halo]
                else:
                    dzs = jnp.concatenate([dz_ref[pl.ds(s * sub, sub), lanes].astype(F32),
                                           jnp.where(i < nt - 1, dzn_ref[:, lanes].astype(F32), 0.0)], axis=0)
                gs = g * sg
                dc = [dzs * gs, dzs * a * (sg + gs - gs * sg)]
                for hf in range(2):
                    d = dc[hf]
                    d0, d1, d2 = d[:sub], _shift_up(d, 1)[:sub], _shift_up(d, 2)[:sub]
                    du = w[2:3, hf, :] * d0 + w[1:2, hf, :] * d1 + w[0:1, hf, :] * d2
                    du_ref[hf, pl.ds(s * sub, sub), lanes] = du.astype(du_ref.dtype)
                    um = ue[hf][main]
                    for k, dd in enumerate((d2, d1, d0)):
                        dwacc[k, hf, :, lanes] += _fold8(dd * um)
                    dbacc[hf, :, lanes] += _fold8(d0)
            return carry

        lax.fori_loop(0, f // LANES, chunk, 0)

        @pl.when(i == nt - 1)
        def _():
            dw_ref[...] = jnp.sum(dwacc[...], axis=2)
            db_ref[...] = jnp.sum(dbacc[...], axis=1)

        last()

    end = t // halo - 1
    return _call(
        body, hosted, name=name, grid=(nt,),
        in_specs=[pl.BlockSpec((2, tt, f), lambda i: (0, i, 0)),
                  pl.BlockSpec((2, halo, f), lambda i: (0, jnp.maximum(i * per - 1, 0), 0)),
                  pl.BlockSpec((2, halo, f), lambda i: (0, jnp.minimum((i + 1) * per, end), 0)),
                  pl.BlockSpec((tt, f), lambda i: (i, 0)),
                  pl.BlockSpec((halo, f), lambda i: (jnp.minimum((i + 1) * per, end), 0)),
                  pl.BlockSpec((3, 2, f), lambda i: (0, 0, 0)), pl.BlockSpec((2, f), lambda i: (0, 0))],
        out_specs=[pl.BlockSpec((2, tt, f), lambda i: (0, i, 0)),
                   pl.BlockSpec((3, 2, f), lambda i: (0, 0, 0)), pl.BlockSpec((2, f), lambda i: (0, 0))],
        out_shape=[jax.ShapeDtypeStruct(u.shape, MXU_DTYPE), jax.ShapeDtypeStruct((3, 2, f), F32),
                   jax.ShapeDtypeStruct((2, f), F32)],
        operands=(u, u, u, dz, dz, conv_w, conv_b), semantics=("arbitrary",),
        scratch_shapes=[pltpu.VMEM((3, 2, 8, f), F32), pltpu.VMEM((2, 8, f), F32)])


def _kv_even_odd(band, k):
    pair, parity = divmod(k, 2)
    blk = band[:, 128 * pair:128 * (pair + 1)].astype(F32)
    lane = lax.broadcasted_iota(jnp.int32, blk.shape, 1)
    own = jnp.where((lane >= HEAD_DIM) == (parity == 1), blk, 0.0)
    swapped = pltpu.roll(own, HEAD_DIM, axis=1)
    even, odd = (own, swapped) if parity == 0 else (swapped, own)
    return even.astype(MXU_DTYPE), odd.astype(MXU_DTYPE)


def _stack_pairs(ref, k):
    return jnp.concatenate([ref[:, 256 * k:256 * k + 128], ref[:, 256 * k + 128:256 * k + 256]], axis=0)


_GROUP_ORDER = (0, 2, 1, 3)


def _attn_fill_tables(bias_ref):
    shape = (4 * ATT_BLOCK, 2 * ATT_BLOCK)
    row = lax.broadcasted_iota(jnp.int32, shape, 0)
    col = lax.broadcasted_iota(jnp.int32, shape, 1)
    dist = (row & (ATT_BLOCK - 1)) + ATT_BLOCK - col
    ok = (dist >= 0) & (dist < ATT_BLOCK)
    distf = dist.astype(F32)
    rb = lax.broadcasted_iota(jnp.int32, (shape[0], 1), 0) // ATT_BLOCK
    for k in range(N_KV_HEADS):
        slope = jnp.zeros((shape[0], 1), F32)
        for r, gq in enumerate(_GROUP_ORDER):
            slope = jnp.where(rb == r, ALIBI_SLOPES[4 * k + gq], slope)
        bias = jnp.where(ok, -slope * distf, -jnp.inf)
        bias_ref[0, k] = bias
        bias_ref[1, k] = jnp.where(col >= ATT_BLOCK, bias, -jnp.inf)


def _attn_probs(qs, k_even, k_odd, bias, k, sink_ref):
    nt_dims = (((1,), (1,)), ((), ()))
    s = jnp.concatenate([lax.dot_general(qs, k_even, nt_dims, preferred_element_type=F32),
                         lax.dot_general(qs, k_odd, nt_dims, preferred_element_type=F32)], axis=0) + bias
    rb = lax.broadcasted_iota(jnp.int32, (s.shape[0], 1), 0) // ATT_BLOCK
    sink = jnp.zeros((s.shape[0], 1), F32)
    for r, gq in enumerate(_GROUP_ORDER):
        sink = jnp.where(rb == r, sink_ref[4 * k + gq], sink)
    m = jnp.maximum(jnp.max(s, axis=-1, keepdims=True), sink)
    p = jnp.exp(s - m)
    es = jnp.exp(sink - m)
    return p, 1.0 / (jnp.sum(p, axis=-1, keepdims=True) + es), es


_ATTN_TABLES = [pltpu.VMEM((2, N_KV_HEADS, 4 * ATT_BLOCK, 2 * ATT_BLOCK), F32)]


def _attn_specs(with_do):
    prev = lambda n: jnp.maximum(n - 1, 0)
    specs = [pl.BlockSpec(memory_space=pltpu.SMEM),
             pl.BlockSpec((ATT_BLOCK, D_MODEL), lambda n: (n, 0)),
             pl.BlockSpec((ATT_BLOCK, 256), lambda n: (prev(n), 4)), pl.BlockSpec((ATT_BLOCK, 256), lambda n: (n, 4)),
             pl.BlockSpec((ATT_BLOCK, 256), lambda n: (prev(n), 5)), pl.BlockSpec((ATT_BLOCK, 256), lambda n: (n, 5))]
    if with_do:
        specs.append(pl.BlockSpec((ATT_BLOCK, D_MODEL), lambda n: (n, 0)))
    return specs


def _attn_fwd_call(qkv, sinks, hosted=None):
    t = qkv.shape[0]

    def body(ins, outs, scratch, first, last):
        sink_ref, q_ref, kp_ref, kc_ref, vp_ref, vc_ref = ins
        o_ref, bias_ref = outs[0], scratch[0]
        n = pl.program_id(0)
        first()

        @pl.when(n == 0)
        def _():
            _attn_fill_tables(bias_ref)

        first = 1 - jnp.minimum(n, 1)
        kband = jnp.concatenate([kp_ref[...], kc_ref[...]], axis=0)
        vband = jnp.concatenate([vp_ref[...], vc_ref[...]], axis=0)
        for k in range(N_KV_HEADS):
            k_even, k_odd = _kv_even_odd(kband, k)
            v_even, v_odd = _kv_even_odd(vband, k)
            qs = _stack_pairs(q_ref, k) * ATT_SCALE
            p, rdenom, _ = _attn_probs(qs, k_even, k_odd, bias_ref[first, k], k, sink_ref)
            probs = (p * rdenom).astype(MXU_DTYPE)
            o = (jnp.dot(probs[:256], v_even, preferred_element_type=F32)
                 + jnp.dot(probs[256:], v_odd, preferred_element_type=F32))
            o_ref[:, 256 * k:256 * k + 128] = o[:128].astype(o_ref.dtype)
            o_ref[:, 256 * k + 128:256 * k + 256] = o[128:].astype(o_ref.dtype)
        last()

    return _call(
        body, hosted, name="attn_fwd", grid=(t // ATT_BLOCK,), in_specs=_attn_specs(False),
        out_specs=[pl.BlockSpec((ATT_BLOCK, D_MODEL), lambda n: (n, 0))],
        out_shape=[jax.ShapeDtypeStruct((t, D_MODEL), MXU_DTYPE)], operands=(sinks, qkv, qkv, qkv, qkv, qkv),
        semantics=("arbitrary",), scratch_shapes=_ATTN_TABLES)


def _attn_bwd_call(qkv, sinks, do, hosted=None):
    t = qkv.shape[0]
    nb = t // ATT_BLOCK
    tn_dims = (((0,), (0,)), ((), ()))
    nt_dims = (((1,), (1,)), ((), ()))

    def to_native(even_part, odd_part, parity):
        lane = lax.broadcasted_iota(jnp.int32, even_part.shape, 1)
        lo = lane < HEAD_DIM
        e, o = jnp.where(lo, even_part, 0.0), jnp.where(lo, 0.0, odd_part)
        if parity == 0:
            return e + pltpu.roll(o, HEAD_DIM, axis=1)
        return pltpu.roll(e, HEAD_DIM, axis=1) + o

    def body(ins, outs, scratch, first, last):
        sink_ref, q_ref, kp_ref, kc_ref, vp_ref, vc_ref, do_ref = ins
        dq_ref, dk_ref, dv_ref, db_ref, dsink_ref = outs
        bias_ref = scratch[0]
        n = pl.program_id(0)
        first()

        @pl.when(n == 0)
        def _():
            _attn_fill_tables(bias_ref)
            dk_ref[...] = jnp.zeros_like(dk_ref)
            dv_ref[...] = jnp.zeros_like(dv_ref)
            db_ref[...] = jnp.zeros_like(db_ref)
            dsink_ref[...] = jnp.zeros_like(dsink_ref)

        kband = jnp.concatenate([kp_ref[...], kc_ref[...]], axis=0)
        vband = jnp.concatenate([vp_ref[...], vc_ref[...]], axis=0)
        dk_pairs = [jnp.zeros((2 * ATT_BLOCK, 128), F32), jnp.zeros((2 * ATT_BLOCK, 128), F32)]
        dv_pairs = [jnp.zeros((2 * ATT_BLOCK, 128), F32), jnp.zeros((2 * ATT_BLOCK, 128), F32)]
        sink_lane = lax.broadcasted_iota(jnp.int32, (8, 128), 1)
        sink_row = lax.broadcasted_iota(jnp.int32, (8, 128), 0)
        dsink = jnp.zeros((8, 128), F32)
        first = 1 - jnp.minimum(n, 1)
        for k in range(N_KV_HEADS):
            k_even, k_odd = _kv_even_odd(kband, k)
            v_even, v_odd = _kv_even_odd(vband, k)
            qs = _stack_pairs(q_ref, k) * ATT_SCALE
            dos = _stack_pairs(do_ref, k)
            p, rdenom, es = _attn_probs(qs, k_even, k_odd, bias_ref[first, k], k, sink_ref)
            probs = p * rdenom
            dp = jnp.concatenate([lax.dot_general(dos, v_even, nt_dims, preferred_element_type=F32),
                                  lax.dot_general(dos, v_odd, nt_dims, preferred_element_type=F32)], axis=0)
            delta = jnp.sum(probs * dp, axis=-1, keepdims=True)
            ds16 = (probs * (dp - delta)).astype(MXU_DTYPE)
            p16 = probs.astype(MXU_DTYPE)
            dsink_rows = -(es * rdenom) * delta
            for r, gq in enumerate(_GROUP_ORDER):
                tot = jnp.sum(dsink_rows[ATT_BLOCK * r:ATT_BLOCK * (r + 1)], axis=0, keepdims=True)
                dsink = dsink + jnp.where((sink_lane == 4 * k + gq) & (sink_row == 0), tot, 0.0)
            dqs = (jnp.dot(ds16[:256], k_even, preferred_element_type=F32)
                   + jnp.dot(ds16[256:], k_odd, preferred_element_type=F32)) * ATT_SCALE
            dq_ref[:, 256 * k:256 * k + 128] = dqs[:128].astype(dq_ref.dtype)
            dq_ref[:, 256 * k + 128:256 * k + 256] = dqs[128:].astype(dq_ref.dtype)
            db_ref[:, 256 * k:256 * k + 128] += jnp.sum(dqs[:128], axis=0, keepdims=True)
            db_ref[:, 256 * k + 128:256 * k + 256] += jnp.sum(dqs[128:], axis=0, keepdims=True)
            pair, parity = divmod(k, 2)
            dk_pairs[pair] = dk_pairs[pair] + to_native(
                lax.dot_general(ds16[:256], qs, tn_dims, preferred_element_type=F32),
                lax.dot_general(ds16[256:], qs, tn_dims, preferred_element_type=F32), parity)
            dv_pairs[pair] = dv_pairs[pair] + to_native(
                lax.dot_general(p16[:256], dos, tn_dims, preferred_element_type=F32),
                lax.dot_general(p16[256:], dos, tn_dims, preferred_element_type=F32), parity)
        dsink_ref[...] += dsink
        dk_band = jnp.concatenate(dk_pairs, axis=1)
        dv_band = jnp.concatenate(dv_pairs, axis=1)
        cur = pl.multiple_of(n * ATT_BLOCK, ATT_BLOCK)
        dk_ref[pl.ds(cur, ATT_BLOCK), :] += dk_band[ATT_BLOCK:]
        dv_ref[pl.ds(cur, ATT_BLOCK), :] += dv_band[ATT_BLOCK:]

        @pl.when(n > 0)
        def _():
            prv = pl.multiple_of((n - 1) * ATT_BLOCK, ATT_BLOCK)
            dk_ref[pl.ds(prv, ATT_BLOCK), :] += dk_band[:ATT_BLOCK]
            dv_ref[pl.ds(prv, ATT_BLOCK), :] += dv_band[:ATT_BLOCK]

        @pl.when(n == nb - 1)
        def _():
            db_ref[:, 1024:1280] = jnp.sum(dk_ref[...], axis=0, keepdims=True)
            db_ref[:, 1280:1536] = jnp.sum(dv_ref[...], axis=0, keepdims=True)

        last()

    whole = lambda shape: pl.BlockSpec(shape, lambda n: (0, 0))
    return _call(
        body, hosted, name="attn_bwd", grid=(nb,), in_specs=_attn_specs(True),
        out_specs=[pl.BlockSpec((ATT_BLOCK, D_MODEL), lambda n: (n, 0)), whole((t, 256)), whole((t, 256)),
                   whole((1, QKV_DIM)), whole((8, 128))],
        out_shape=[jax.ShapeDtypeStruct((t, D_MODEL), MXU_DTYPE), jax.ShapeDtypeStruct((t, 256), F32),
                   jax.ShapeDtypeStruct((t, 256), F32), jax.ShapeDtypeStruct((1, QKV_DIM), F32),
                   jax.ShapeDtypeStruct((8, 128), F32)],
        operands=(sinks, qkv, qkv, qkv, qkv, qkv, do), semantics=("arbitrary",), scratch_shapes=_ATTN_TABLES)


FF_TILE = 1408


def _gathering_tile(j):
    me = 2 * lax.axis_index("x") + lax.axis_index("y")
    return me ^ (((j & 1) << 1) | (j >> 1))


def _ffn_in_gathering_call(hn, w_in, layer, hosted):
    t = hn.shape[0]
    tm = _tile(t, 1024)
    n_i = t // tm
    halves = D_FF // FF_TILE
    assert FF_TILE == W_IN_COLS and 2 * halves == N_CHIPS

    def body(ins, outs, scratch, first, last):
        hn_ref = ins[0]
        u_ref, w_ref = outs
        wbuf, load_sem, ici_send, ici_recv, hand_send, hand_recv = scratch
        j, i = pl.program_id(0), pl.program_id(1)
        x, y, c, chips = _mesh_place()
        me = 2 * x + y
        mover = c == layer

        def shard(k):
            return w_ref.at[layer, :, _col(k, W_IN_COLS)]

        def over_ici(jj, k):
            cx, cy = chips[jj]
            return pltpu.make_async_remote_copy(src_ref=shard(k), dst_ref=shard(k), send_sem=ici_send.at[jj],
                                                recv_sem=ici_recv.at[jj], device_id=(cx, cy, c), device_id_type=MESH)

        def handed(jj):
            cx, cy = chips[jj]
            return pltpu.make_async_remote_copy(
                src_ref=shard(2 * cx + cy), dst_ref=shard(2 * cx + cy), send_sem=hand_send.at[jj],
                recv_sem=hand_recv.at[jj], device_id=(x, y, 1 - c), device_id_type=MESH)

        far = N_CHIPS - 2

        @pl.when((j == 0) & (i == 0) & mover)
        def _():
            over_ici(far, me).start()
            for jj in NEIGHBOURS:
                handed(jj).start()

        first()

        @pl.when((j == far + 1) & (i == 0) & mover)
        def _():
            over_ici(far, 2 * chips[far][0] + chips[far][1]).wait_recv()
            handed(far).start()

        for jj in range(N_CHIPS - 1):
            @pl.when((j == jj + 1) & (i == 0) & jnp.logical_not(mover))
            def _(jj=jj):
                handed(jj).wait_recv()

        @pl.when(i == 0)
        def _():
            load = pltpu.make_async_copy(shard(_gathering_tile(j)), wbuf, load_sem.at[0])
            load.start()
            load.wait()

        u_ref[...] = jnp.dot(hn_ref[...].astype(MXU_DTYPE), wbuf[...].astype(MXU_DTYPE),
                             preferred_element_type=F32).astype(u_ref.dtype)

        @pl.when((j == N_CHIPS - 1) & (i == n_i - 1) & mover)
        def _():
            over_ici(far, me).wait_send()
            for jj in range(N_CHIPS - 1):
                handed(jj).wait_send()

        last()

    def u_index(j, i):
        tile = _gathering_tile(j)
        return tile // halves, i, tile % halves

    three = [pltpu.SemaphoreType.DMA((N_CHIPS - 1,))] * 4
    return _call(
        body, hosted, name=f"ffn{layer}_in", grid=(N_CHIPS, n_i),
        in_specs=[pl.BlockSpec((tm, D_MODEL), lambda j, i: (i, 0)), ANY],
        out_specs=[pl.BlockSpec((None, tm, FF_TILE), u_index), ANY],
        out_shape=[jax.ShapeDtypeStruct((2, t, D_FF), U_DTYPE), jax.ShapeDtypeStruct(w_in.shape, w_in.dtype)],
        operands=(hn, w_in), semantics=("arbitrary", "arbitrary"), aliases={1: 1},
        scratch_shapes=[pltpu.VMEM((D_MODEL, W_IN_COLS), w_in.dtype), pltpu.SemaphoreType.DMA((1,))] + three)


def _ffn_out_loss_call(z, w_out, layer, h, g, target):
    t, f = z.shape
    tm = _tile(t, ROW_TILE)

    def body(z_ref, w_ref, h_ref, g_ref, t_ref, loss_ref, dh_ref, dh16_ref, dg_ref):
        x = h_ref[...] + jnp.dot(z_ref[...], w_ref[...], preferred_element_type=F32)
        gv = g_ref[...]
        err = _rms(x, gv) - t_ref[...]
        dx, dgt = _rms_bwd(x, gv, err * (1.0 / D_MODEL))
        dh_ref[...] = dx
        dh16_ref[...] = dx.astype(dh16_ref.dtype)

        @pl.when(pl.program_id(0) == 0)
        def _():
            dg_ref[...] = jnp.zeros_like(dg_ref)
            loss_ref[...] = jnp.zeros_like(loss_ref)

        dg_ref[...] += jnp.sum(dgt, axis=0, keepdims=True)
        per_token = jnp.mean(err * err, axis=-1, keepdims=True)
        loss_ref[...] += 0.5 * jnp.sum(per_token, axis=0, keepdims=True)

    row = pl.BlockSpec((tm, D_MODEL), lambda i: (i, 0))
    vec = pl.BlockSpec((1, D_MODEL), lambda i: (0, 0))
    return pl.pallas_call(
        body, name=f"ffn{layer}_out_loss", grid=(t // tm,),
        in_specs=[pl.BlockSpec((tm, f), lambda i: (i, 0)), pl.BlockSpec((None, f, D_MODEL), lambda i: (layer, 0, 0)),
                  row, vec, row],
        out_specs=[pl.BlockSpec((1, 1), lambda i: (0, 0)), row, row, vec],
        out_shape=[jax.ShapeDtypeStruct((1, 1), F32), jax.ShapeDtypeStruct((t, D_MODEL), F32),
                   jax.ShapeDtypeStruct((t, D_MODEL), MXU_DTYPE), jax.ShapeDtypeStruct((1, D_MODEL), F32)],
        compiler_params=_params("arbitrary"))(z, w_out, h, g, target)


def _ffn_fwd(hn, h, w, conv_w, conv_b, layer, hooks=None, next_norm=None, loss_head=None, gathers_w_in=False):
    t = hn.shape[0]
    tm = _tile(t, 1024)
    halves = D_FF // FF_TILE
    if gathers_w_in:
        u, w["w_in"] = _ride(hooks, f"in{layer}", lambda hosted: _ffn_in_gathering_call(hn, w["w_in"], layer, hosted))
    else:
        u, = _ride(hooks, f"in{layer}", lambda hosted: _matmul(
            hn, w["w_in"], mode="nn", name=f"ffn{layer}_in", grid=(2 * halves, t // tm, 1),
            a_spec=pl.BlockSpec((tm, D_MODEL), lambda j, i, k: (i, 0)),
            b_spec=pl.BlockSpec((None, D_MODEL, FF_TILE), lambda j, i, k: (layer, 0, j)),
            o_spec=pl.BlockSpec((None, tm, FF_TILE), lambda j, i, k: (j // halves, i, j % halves)),
            out_shape=jax.ShapeDtypeStruct((2, t, D_FF), U_DTYPE), hosted=hosted))
    z, = _ride(hooks, f"glu{layer}", lambda hosted: _conv_glu_fwd_call(u, conv_w, conv_b, f"ffn{layer}_glu",
                                                                       hosted=hosted))
    if loss_head is not None:
        return (*_ffn_out_loss_call(z, w["w_out"], layer, h, *loss_head), u, z)
    outs = _ride(hooks, f"out{layer}", lambda hosted: _mm_nn(z, w["w_out"], f"ffn{layer}_out", b_lead=layer, residual=h,
                                                     tk=D_FF, hosted=hosted, norm_g=next_norm))
    return outs[0], (outs[1] if next_norm is not None else None), u, z


def _ffn_bwd(dh, hn, u, z, w_in, w_out, conv_w, conv_b, layer, g_in_acc, g_out_acc, norm, hooks=None):
    t = hn.shape[0]
    dz, = _ride(hooks, "dz", lambda hosted: _mm_nt(dh, w_out, f"ffn{layer}_dz", b_lead=layer, out_dtype=MXU_DTYPE,
                                                   tn=FF_TILE, hosted=hosted))
    du, dcw, dcb = _ride(hooks, "dglu", lambda hosted: _conv_glu_bwd_call(u, dz, conv_w, conv_b, f"ffn{layer}_dglu",
                                                                          hosted=hosted))
    halves = D_FF // FF_TILE
    rows = D_MODEL // 2
    g_in, = _ride(hooks, "dwin", lambda hosted: _matmul(
        hn, du, mode="tn", name=f"ffn{layer}_dwin", grid=(2 * halves, D_MODEL // rows, 1),
        a_spec=pl.BlockSpec((t, rows), lambda j, i, k: (0, i)),
        b_spec=pl.BlockSpec((None, t, FF_TILE), lambda j, i, k: (j // halves, 0, j % halves)),
        o_spec=pl.BlockSpec((None, rows, FF_TILE), lambda j, i, k: (layer, i, j)),
        out_shape=jax.ShapeDtypeStruct((2, D_MODEL, 2 * D_FF), WIRE_DTYPE), into=g_in_acc, hosted=hosted),
        {"dcw": dcw, "dcb": dcb})
    g_out, = _ride(hooks, "dwout", lambda hosted: _mm_tn(
        z, dh, f"ffn{layer}_dwout", lead=layer, n_lead=2, into=g_out_acc, tm=WGRAD_TILE, tk=t, out_dtype=WIRE_DTYPE,
        hosted=hosted), {"g_in": g_in})
    dx, dx16, dg = _ride(hooks, "dhn", lambda hosted: _ffn_dhn_call(du, w_in, layer, norm, hosted),
                         {"g_in": g_in, "g_out": g_out})
    return dx, dx16, dg, g_in, g_out, dcw, dcb


WGRAD_TILE = 256


_NT_DIMS = (((1,), (1,)), ((), ()))


def _dhn_call(name, a, a_spec, w, w_spec, product, t, tm, norm, hosted=None):
    h, g, dres = norm

    def body(ins, outs, scratch, first, last):
        a_ref, w_ref, h_ref, g_ref, dres_ref = ins
        dx_ref, dx16_ref, dg_ref = outs
        first()
        dx, dgt = _rms_bwd(h_ref[...], g_ref[...], product(a_ref, w_ref))
        dx = dres_ref[...] + dx
        dx_ref[...] = dx
        dx16_ref[...] = dx.astype(dx16_ref.dtype)

        @pl.when(pl.program_id(0) == 0)
        def _():
            dg_ref[...] = jnp.zeros_like(dg_ref)

        dg_ref[...] += jnp.sum(dgt, axis=0, keepdims=True)
        last()

    row = pl.BlockSpec((tm, D_MODEL), lambda i: (i, 0))
    vec = pl.BlockSpec((1, D_MODEL), lambda i: (0, 0))
    return _call(
        body, hosted, name=name, grid=(t // tm,), in_specs=[a_spec, w_spec, row, vec, row], out_specs=[row, row, vec],
        out_shape=[jax.ShapeDtypeStruct((t, D_MODEL), F32), jax.ShapeDtypeStruct((t, D_MODEL), MXU_DTYPE),
                   jax.ShapeDtypeStruct((1, D_MODEL), F32)],
        operands=(a, w, h, g, dres), semantics=("arbitrary",))


def _ffn_dhn_call(du, w_in, layer, norm, hosted=None):
    _, t, f = du.shape
    tm = _tile(t, 512)

    def product(du_ref, w_ref):
        return (lax.dot_general(du_ref[0], w_ref[:, :f], _NT_DIMS, preferred_element_type=F32)
                + lax.dot_general(du_ref[1], w_ref[:, f:], _NT_DIMS, preferred_element_type=F32))

    return _dhn_call(f"ffn{layer}_dhn", du, pl.BlockSpec((2, tm, f), lambda i: (0, i, 0)), w_in,
                     pl.BlockSpec((None, D_MODEL, 2 * f), lambda i: (layer, 0, 0)), product, t, tm, norm, hosted)


def _attn_dhn_call(dqkv, w_qkv, norm):
    t, n = dqkv.shape
    tm = _tile(t, 1024)

    def product(a_ref, w_ref):
        return lax.dot_general(a_ref[...], w_ref[...], _NT_DIMS, preferred_element_type=F32)

    return _dhn_call("attn_dhn", dqkv, pl.BlockSpec((tm, n), lambda i: (i, 0)), w_qkv,
                     pl.BlockSpec((D_MODEL, n), lambda i: (0, 0)), product, t, tm, norm)


def _local_step(x, target, w, *, norm_mix, norm_ffn, norm_f, pool_b, pool_scale, b_qkv, sinks, conv_w, conv_b,
                late_gather=None, reducer=None):
    cw = [conv_w[l].reshape(3, 2, D_FF) for l in range(2)]
    cb = [conv_b[l].reshape(2, D_FF) for l in range(2)]
    gm = [norm_mix[l:l + 1] for l in range(2)]
    gf = [norm_ffn[l:l + 1] for l in range(2)]

    w = dict(w)
    fwd_hooks = {}
    if late_gather:
        def gather_hook(stage):
            return (lambda env: late_gather(stage, w)), (lambda got: w.update(zip(_late_names(stage), got)))

        fwd_hooks = {stage: gather_hook(stage) for stage in GATHER_RIDES}

    h1, hn_f0 = _ride(fwd_hooks, "pool_fwd", lambda hosted: _pool_fwd_call(x, gm[0], w["pool_w"], pool_b, pool_scale,
                                                                            gf[0], hosted))
    h2, hn_a, u0, z0 = _ffn_fwd(hn_f0, h1, w, cw[0], cb[0], 0, hooks=fwd_hooks, next_norm=gm[1],
                                gathers_w_in=bool(late_gather))
    qkv = _mm_nn(hn_a, w["w_qkv"], "attn_qkv", bias=b_qkv, out_dtype=MXU_DTYPE, tn=QKV_DIM)
    o, = _ride(fwd_hooks, "attn_fwd", lambda hosted: _attn_fwd_call(qkv, sinks, hosted))
    h3, hn_f1 = _ride(fwd_hooks, "attn_out", lambda hosted: _mm_nn(o, w["w_o"], "attn_out", residual=h2, hosted=hosted,
                                                                   norm_g=gf[1]))
    loss, dh, dh16, d_norm_f, u1, z1 = _ffn_fwd(hn_f1, h3, w, cw[1], cb[1], 1, hooks=fwd_hooks,
                                                loss_head=(norm_f, target))

    rd = reducer
    first, att, w_in0, w_out0 = _Reducer.FIRST, _Reducer.ATT, _Reducer.W_IN0, _Reducer.W_OUT0
    hooks1 = hooks_attn = hooks0 = hooks_pool = None
    if rd:
        hooks1 = {"dhn": (lambda env: rd.swap(first, {"w_in": env["g_in"], "w_out": env["g_out"]}),
                          lambda got: rd.pair_sums(first, got))}
        hooks_attn = {"attn_bwd": (lambda env: rd.scatter(first), lambda got: rd.chip_sums(first, got))}

        def early_pieces(env):
            return {"b_qkv": d_bqkv, "sinks": d_sinks[0:1], "norm_mix1": d_gm1, "norm_ffn1": d_gf1,
                    "conv_w": jnp.stack([env["dcw"].reshape(3, 2 * D_FF), dcw1.reshape(3, 2 * D_FF)]),
                    "conv_b": jnp.stack([env["dcb"].reshape(2 * D_FF), dcb1.reshape(2 * D_FF)]),
                    "norm_f": d_norm_f, "loss": loss}

        def after_dz(got):
            rd.shared(first, got[:len(first)])
            rd.pair_sums(att, got[len(first):])

        def after_dwin(got):
            rd.shared(att, got[:len(att)])
            rd.small_early_done(got[-1])

        def after_dhn(got):
            rd.chip_sums(w_in0, got[:1])
            rd.pair_sums(w_out0, got[1:])

        def after_pool_bwd(got):
            rd.shared(w_in0, got[:1])
            rd.chip_sums(w_out0, got[1:])

        hooks0 = {
            "dz": (lambda env: _both(rd.share(first), rd.swap(att, {"w_qkv": g_qkv, "w_o": g_o4})), after_dz),
            "dglu": (lambda env: rd.scatter(att), lambda got: rd.chip_sums(att, got)),
            "dwin": (lambda env: _both(rd.share(att), rd.small_early(early_pieces(env))), after_dwin),
            "dwout": (lambda env: rd.swap(w_in0, {"w_in": env["g_in"]}), lambda got: rd.pair_sums(w_in0, got)),
            "dhn": (lambda env: _both(rd.scatter(w_in0), rd.swap(w_out0, {"w_out": env["g_out"]})), after_dhn),
        }
        hooks_pool = {"pool_bwd": (lambda env: _both(rd.share(w_in0), rd.scatter(w_out0)), after_pool_bwd)}
    dh, dh16, d_gf1, g_in, g_out, dcw1, dcb1 = _ffn_bwd(
        dh16, hn_f1, u1, z1, w["w_in"], w["w_out"], cw[1], cb[1], 1, None, None, (h3, gf[1], dh), hooks=hooks1)
    do = _mm_nt(dh16, w["w_o"], "attn_do", out_dtype=MXU_DTYPE)
    g_o = _mm_tn(o, dh16, "attn_dwo", tn=WGRAD_TILE, tk=x.shape[0], out_dtype=WIRE_DTYPE)
    g_o4 = g_o.reshape(N_CHIPS, 2, W_O_ROWS // 2, D_MODEL)
    dq, dk, dv, d_bqkv, d_sinks = _ride(hooks_attn, "attn_bwd", lambda hosted: _attn_bwd_call(qkv, sinks, do, hosted))
    dqkv = jnp.concatenate([dq, dk.astype(MXU_DTYPE), dv.astype(MXU_DTYPE)], axis=1)
    g_qkv = _mm_tn(hn_a, dqkv, "attn_dwqkv", tn=WGRAD_TILE, tk=x.shape[0], out_dtype=WIRE_DTYPE)
    dh, dh16, d_gm1 = _attn_dhn_call(dqkv, w["w_qkv"], (h2, gm[1], dh))
    dh, _, d_gf0, g_in, g_out, dcw0, dcb0 = _ffn_bwd(
        dh16, hn_f0, u0, z0, w["w_in"], w["w_out"], cw[0], cb[0], 0, g_in, g_out, (h1, gf[0], dh), hooks=hooks0)
    grad_x, g_pool, dvec = _ride(hooks_pool, "pool_bwd", lambda hosted: _pool_bwd_call(
        x, dh, gm[0], w["pool_w"], pool_b, pool_scale, hosted))

    grads = {
        "w_in": g_in, "w_out": g_out, "w_qkv": g_qkv, "w_o": g_o, "pool_w": g_pool,
        "pool_b": dvec[0:1], "pool_scale": dvec[1:2], "b_qkv": d_bqkv, "sinks": d_sinks[0:1, :N_HEADS],
        "norm_mix": jnp.concatenate([dvec[2:3], d_gm1], axis=0), "norm_ffn": jnp.concatenate([d_gf0, d_gf1], axis=0),
        "conv_w": jnp.stack([dcw0.reshape(3, 2 * D_FF), dcw1.reshape(3, 2 * D_FF)]),
        "conv_b": jnp.stack([dcb0.reshape(2 * D_FF), dcb1.reshape(2 * D_FF)]), "norm_f": d_norm_f,
    }
    return loss, grad_x, grads


W_IN_COLS = 2 * D_FF // N_CHIPS
W_OUT_ROWS = D_FF // N_CHIPS
QKV_COLS = QKV_DIM // N_CHIPS
W_O_ROWS = D_MODEL // N_CHIPS
POOL_ROWS = POOL_GC // N_CHIPS
BIG = ("w_in", "w_out", "w_qkv", "w_o", "pool_w")


def _mesh_place():
    x, y, c = lax.axis_index("x"), lax.axis_index("y"), lax.axis_index("c")
    chips = [(1 - x, y), (x, 1 - y), (1 - x, 1 - y)]
    return x, y, c, chips


def _col(k, width):
    return pl.ds(pl.multiple_of(k * width, 128), width)


def _row(k, height):
    return pl.ds(pl.multiple_of(k * height, 16), height)


def _full_piece(name, ref, k, h):
    if name == "w_in":
        return ref.at[h, :, _col(k, W_IN_COLS)]
    if name == "w_out":
        return ref.at[h, _row(k, W_OUT_ROWS), :]
    if name == "w_qkv":
        return ref.at[_row(h, D_MODEL // 2), _col(k, QKV_COLS)]
    if name == "w_o":
        return ref.at[_row(2 * k + h, W_O_ROWS // 2), :]
    return ref.at[pl.ds(2 * h, 2), _row(k, POOL_ROWS), :]


def _shard_half(name, ref, h):
    if name in ("w_in", "w_out"):
        return ref.at[h]
    if name == "w_qkv":
        return ref.at[_row(h, D_MODEL // 2), :]
    if name == "w_o":
        return ref.at[_row(h, W_O_ROWS // 2), :]
    return ref.at[pl.ds(2 * h, 2)]


FULL_SHAPES = {"w_in": (2, D_MODEL, 2 * D_FF), "w_out": (2, D_FF, D_MODEL), "w_qkv": (D_MODEL, QKV_DIM),
               "w_o": (D_MODEL, D_MODEL), "pool_w": (4, POOL_GC, POOL_GC)}
SHARD_SHAPES = {"w_in": (2, D_MODEL, W_IN_COLS), "w_out": (2, W_OUT_ROWS, D_MODEL), "w_qkv": (D_MODEL, QKV_COLS),
                "w_o": (W_O_ROWS, D_MODEL), "pool_w": (4, POOL_ROWS, POOL_GC)}
HALF_SHAPES = {"w_in": (D_MODEL, W_IN_COLS), "w_out": (W_OUT_ROWS, D_MODEL), "w_qkv": (D_MODEL // 2, QKV_COLS),
               "w_o": (W_O_ROWS // 2, D_MODEL), "pool_w": (2, POOL_ROWS, POOL_GC)}


def _place_shard(name, shard, me_arr):
    if name == "w_in":
        blk, grid = (1, 256, W_IN_COLS), (2, D_MODEL // 256)
        src, dst = (lambda a, b, me: (a, b, 0)), (lambda a, b, me: (a, b, me[0]))
    elif name == "w_out":
        blk, grid = (1, W_OUT_ROWS, D_MODEL), (2, 1)
        src, dst = (lambda a, b, me: (a, 0, 0)), (lambda a, b, me: (a, me[0], 0))
    elif name == "w_qkv":
        blk, grid = (256, QKV_COLS), (1, D_MODEL // 256)
        src, dst = (lambda a, b, me: (b, 0)), (lambda a, b, me: (b, me[0]))
    elif name == "w_o":
        blk, grid = (W_O_ROWS, D_MODEL), (1, 1)
        src, dst = (lambda a, b, me: (0, 0)), (lambda a, b, me: (me[0], 0))
    else:
        blk, grid = (4, POOL_ROWS, POOL_GC), (1, 1)
        src, dst = (lambda a, b, me: (0, 0, 0)), (lambda a, b, me: (0, me[0], 0))

    def body(me_ref, s_ref, o_ref):
        o_ref[...] = s_ref[...].astype(o_ref.dtype)

    return pl.pallas_call(
        body, name=f"place_{name}",
        grid_spec=pltpu.PrefetchScalarGridSpec(num_scalar_prefetch=1, grid=grid, in_specs=[pl.BlockSpec(blk, src)],
                                               out_specs=pl.BlockSpec(blk, dst)),
        out_shape=jax.ShapeDtypeStruct(FULL_SHAPES[name], WIRE_DTYPE), compiler_params=_params("parallel", "parallel"),
    )(me_arr, shard)


def _when_mover(half, c, fn):
    if half is None:
        fn(c)
    else:
        pl.when(c == half)(lambda: fn(half))


def _when_taker(half, c, fn):
    if half is None:
        fn(1 - c)
    else:
        pl.when(c != half)(lambda: fn(half))


def _remote(ref, sems, t, j, dev):
    send, recv = sems
    return pltpu.make_async_remote_copy(src_ref=ref, dst_ref=ref, send_sem=send.at[t, j], recv_sem=recv.at[t, j],
                                        device_id=dev, device_id_type=MESH)


def _gather_ici(items, refs, sems, finish, peers=(0, 1, 2)):
    x, y, c, chips = _mesh_place()
    me = 2 * x + y
    for t, (nm, half) in enumerate(items):
        def go(h, t=t, nm=nm):
            mine = _full_piece(nm, refs[t], me, h)
            for j, (cx, cy) in enumerate(chips):
                if j not in peers:
                    continue
                if not finish:
                    _remote(mine, sems, t, j, (cx, cy, c)).start()
                else:
                    _remote(_full_piece(nm, refs[t], 2 * cx + cy, h), sems, t, j, (cx, cy, c)).wait_recv()
                    _remote(mine, sems, t, j, (cx, cy, c)).wait_send()

        _when_mover(half, c, go)


def _gather_pass(items, refs, sems, finish):
    x, y, c, chips = _mesh_place()
    sibling = (x, y, 1 - c)
    for t, (nm, half) in enumerate(items):
        def give(h, t=t, nm=nm):
            for j, (cx, cy) in enumerate(chips):
                got = _remote(_full_piece(nm, refs[t], 2 * cx + cy, h), sems, t, j, sibling)
                if not finish:
                    got.start()
                else:
                    got.wait_send()

        def take(h, t=t, nm=nm):
            for j, (cx, cy) in enumerate(chips):
                _remote(_full_piece(nm, refs[t], 2 * cx + cy, h), sems, t, j, sibling).wait_recv()

        _when_mover(half, c, give)
        if finish:
            _when_taker(half, c, take)


def _gather_sems(n_items):
    return [pltpu.SemaphoreType.DMA((n_items, 3)), pltpu.SemaphoreType.DMA((n_items, 3))]


EARLY_ITEMS = (("pool_w", None),)
_LATE_A = (("w_out", 0),)
_LATE_B = (("w_in", 1),)
_LATE_C = (("w_out", 1),)
_LATE_D = (("w_qkv", None), ("w_o", None))
_EARLY_W_IN = (("w_in", 0),)
GATHER_RIDES = {"pool_fwd": ((), _EARLY_W_IN), "in0": ((), _LATE_A), "glu0": (_LATE_A, _LATE_D), "out0": (_LATE_D, ()),
                "attn_fwd": ((), _LATE_B), "attn_out": (_LATE_B, ()), "in1": ((), _LATE_C), "glu1": (_LATE_C, ())}
NEIGHBOURS = (0, 1)
GATHER_PEERS = {"pool_fwd": NEIGHBOURS}


def _names(items):
    return [nm for nm, _ in items]


def _gather_early(full, small):
    items = EARLY_ITEMS
    n = len(items)

    def body(*refs):
        small_in = refs[n]
        outs, small_out = refs[n + 1:2 * n + 1], refs[2 * n + 1]
        scratch = refs[2 * n + 2:]
        ici_sems, pass_sems = scratch[0:2], scratch[2:4]
        local_sem, small_send, small_recv = scratch[4:]
        x, y, c, chips = _mesh_place()
        me = 2 * x + y
        local = pltpu.make_async_copy(small_in, small_out.at[me], local_sem.at[0])
        local.start()
        small_sends = [pltpu.make_async_remote_copy(
            src_ref=small_in, dst_ref=small_out.at[me], send_sem=small_send.at[j], recv_sem=small_recv.at[j],
            device_id=(cx, cy, c), device_id_type=MESH) for j, (cx, cy) in enumerate(chips)]
        for cp in small_sends:
            cp.start()
        _gather_ici(items, outs, ici_sems, finish=False)
        _gather_ici(items, outs, ici_sems, finish=True)
        _gather_pass(items, outs, pass_sems, finish=False)
        _gather_pass(items, outs, pass_sems, finish=True)
        for j, (cx, cy) in enumerate(chips):
            got = small_out.at[2 * cx + cy]
            pltpu.make_async_remote_copy(src_ref=got, dst_ref=got, send_sem=small_send.at[j],
                                         recv_sem=small_recv.at[j], device_id=(cx, cy, c), device_id_type=MESH).wait_recv()
        for cp in small_sends:
            cp.wait_send()
        local.wait()

    arrays = [full[nm] for nm in _names(items)]
    out_shape = [jax.ShapeDtypeStruct(a.shape, a.dtype) for a in arrays]
    out_shape.append(jax.ShapeDtypeStruct((N_CHIPS,) + small.shape, F32))
    res = pl.pallas_call(
        body, name="gather_early", in_specs=[ANY] * (n + 1), out_specs=[ANY] * (n + 1), out_shape=out_shape,
        input_output_aliases={t: t for t in range(n)},
        scratch_shapes=_gather_sems(n) + _gather_sems(n) + [pltpu.SemaphoreType.DMA((1,)), pltpu.SemaphoreType.DMA((3,)),
                                                            pltpu.SemaphoreType.DMA((3,))],
    )(*arrays, small)
    return {**full, **dict(zip(_names(items), res[:n]))}, res[n]


def _late_gather_host(stage, w):
    passing, sending = GATHER_RIDES[stage]
    n_pass = len(passing)

    def run(src, refs, new, sems, finish):
        if passing:
            _gather_pass(passing, refs[:n_pass], sems[:2], finish)
        if sending:
            _gather_ici(sending, refs[n_pass:], sems[-2:], finish, GATHER_PEERS.get(stage, (0, 1, 2)))

    sems = (_gather_sems(n_pass) if passing else []) + (_gather_sems(len(sending)) if sending else [])
    return _Hosted([], [w[nm] for nm in _late_names(stage)], [], sems, functools.partial(run, finish=False),
                   functools.partial(run, finish=True))


def _late_names(stage):
    return [nm for items in GATHER_RIDES[stage] for nm in _names(items)]


def _other_half_src(name, ref, h):
    if name in ("w_in", "w_out"):
        return ref.at[h]
    if name == "w_qkv":
        return ref.at[_row(h, D_MODEL // 2), :]
    if name == "w_o":
        return ref.at[:, pl.ds(h, 1)]
    return ref.at[pl.ds(2 * h, 2)]


SLAB_SHAPES = {"w_in": (D_MODEL, 2 * D_FF), "w_out": (D_FF, D_MODEL), "w_qkv": (D_MODEL // 2, QKV_DIM),
               "w_o": (N_CHIPS, 1, W_O_ROWS // 2, D_MODEL), "pool_w": (2, POOL_GC, POOL_GC)}


def _item_tag(item):
    return item[0] if item[1] is None else f"{item[0]}{item[1]}"


def _reduce_swap(items, grads, slabs, sems, finish):
    x, y, c, _ = _mesh_place()
    send, recv = sems
    for t, (nm, half) in enumerate(items):
        def copy(h, t=t, nm=nm):
            return pltpu.make_async_remote_copy(
                src_ref=_other_half_src(nm, grads[t], h), dst_ref=slabs[t], send_sem=send.at[t], recv_sem=recv.at[t],
                device_id=(x, y, 1 - c), device_id_type=MESH)

        def give(h):
            if finish:
                copy(h).wait_send()
            else:
                copy(h).start()

        if half is None:
            give(1 - c)
            if finish:
                copy(c).wait_recv()
        else:
            pl.when(c != half)(lambda: give(half))
            if finish:
                pl.when(c == half)(lambda: copy(half).wait_recv())


def _reduce_scatter(items, sums, parts, sems, finish):
    _, _, c, chips = _mesh_place()
    send, recv = sems
    for t, (nm, half) in enumerate(items):
        def go(h, t=t, nm=nm):
            for j, (cx, cy) in enumerate(chips):
                cp = pltpu.make_async_remote_copy(
                    src_ref=_slab_piece(nm, sums[t], 2 * cx + cy), dst_ref=parts[t].at[j], send_sem=send.at[t, j],
                    recv_sem=recv.at[t, j], device_id=(cx, cy, c), device_id_type=MESH)
                if finish:
                    cp.wait_recv()
                    cp.wait_send()
                else:
                    cp.start()

        _when_mover(half, c, go)


def _reduce_share(items, shards, sems, finish):
    x, y, c, _ = _mesh_place()
    send, recv = sems
    for t, (nm, half) in enumerate(items):
        def copy(h, t=t, nm=nm):
            part = _shard_half(nm, shards[t], h)
            return pltpu.make_async_remote_copy(src_ref=part, dst_ref=part, send_sem=send.at[t], recv_sem=recv.at[t],
                                                device_id=(x, y, 1 - c), device_id_type=MESH)

        def give(h):
            if finish:
                copy(h).wait_send()
            else:
                copy(h).start()

        _when_mover(half, c, give)
        if finish:
            _when_taker(half, c, lambda h: copy(h).wait_recv())


def _pair_sems(n):
    return [pltpu.SemaphoreType.DMA((n,)), pltpu.SemaphoreType.DMA((n,))]


def _swap_host(items, grads):
    def run(src, arr, new, sems, finish):
        _reduce_swap(items, src, new, sems, finish)

    return _Hosted([grads[nm] for nm, _ in items], [],
                   [jax.ShapeDtypeStruct(SLAB_SHAPES[nm], grads[nm].dtype) for nm, _ in items],
                   _pair_sems(len(items)), functools.partial(run, finish=False), functools.partial(run, finish=True))


def _scatter_host(items, sums):
    def run(src, arr, new, sems, finish):
        _reduce_scatter(items, src, new, sems, finish)

    new = [jax.ShapeDtypeStruct((N_CHIPS - 1,) + HALF_SHAPES[nm], WIRE_DTYPE) for nm, _ in items]
    return _Hosted(list(sums), [], new, _gather_sems(len(items)), functools.partial(run, finish=False),
                   functools.partial(run, finish=True))


def _share_host(items, shards):
    def run(src, arr, new, sems, finish):
        _reduce_share(items, arr, sems, finish)

    return _Hosted([], list(shards), [], _pair_sems(len(items)), functools.partial(run, finish=False),
                   functools.partial(run, finish=True))


def _pair_sum(item, grad, slab, place):
    name, half = item
    shape = SLAB_SHAPES[name]
    hsel = (lambda p: p[1]) if half is None else (lambda p: half)
    if name == "w_in":
        blk = (256, 2 * D_FF)
        grid = (D_MODEL // 256,)
        g_spec = pl.BlockSpec((None,) + blk, lambda i, p: (hsel(p), i, 0))
        s_spec = pl.BlockSpec(blk, lambda i, p: (i, 0))
    elif name == "w_out":
        blk = (W_OUT_ROWS, D_MODEL)
        grid = (N_CHIPS,)
        g_spec = pl.BlockSpec((None,) + blk, lambda i, p: (hsel(p), i, 0))
        s_spec = pl.BlockSpec(blk, lambda i, p: (i, 0))
    elif name == "w_qkv":
        grid = (1,)
        g_spec = pl.BlockSpec(shape, lambda i, p: (hsel(p), 0))
        s_spec = pl.BlockSpec(shape, lambda i, p: (0, 0))
    else:
        grid = (1,)
        g_spec = pl.BlockSpec(shape, lambda i, p: (0, hsel(p), 0, 0))
        s_spec = pl.BlockSpec(shape, lambda i, p: (0, 0, 0, 0))

    def body(p_ref, g_ref, s_ref, o_ref):
        @pl.when(hsel(p_ref) == p_ref[1])
        def _():
            o_ref[...] = (g_ref[...].astype(F32) + s_ref[...].astype(F32)).astype(o_ref.dtype)

    return pl.pallas_call(
        body, name=f"pair_sum_{_item_tag(item)}",
        grid_spec=pltpu.PrefetchScalarGridSpec(num_scalar_prefetch=1, grid=grid, in_specs=[g_spec, s_spec],
                                               out_specs=s_spec),
        out_shape=jax.ShapeDtypeStruct(shape, WIRE_DTYPE), compiler_params=_params("parallel"),
    )(place, grad, slab)


def _slab_piece(name, ref, k):
    if name == "w_in":
        return ref.at[:, _col(k, W_IN_COLS)]
    if name == "w_out":
        return ref.at[_row(k, W_OUT_ROWS), :]
    if name == "w_qkv":
        return ref.at[:, _col(k, QKV_COLS)]
    if name == "w_o":
        return ref.at[k, 0]
    return ref.at[:, _row(k, POOL_ROWS), :]


def _sum_chips(item, slab, parts, place, into=None):
    name, sel = item
    half = HALF_SHAPES[name]
    hsel = (lambda p: p[1]) if sel is None else (lambda p: sel)
    if name == "w_in":
        blk, grid = (256, W_IN_COLS), (D_MODEL // 256,)
        own = pl.BlockSpec(blk, lambda i, p: (i, p[0]))
        got = pl.BlockSpec((N_CHIPS - 1,) + blk, lambda i, p: (0, i, 0))
        out = pl.BlockSpec((None,) + blk, lambda i, p: (hsel(p), i, 0))
    elif name == "w_out":
        grid = (1,)
        own = pl.BlockSpec(half, lambda i, p: (p[0], 0))
        got = pl.BlockSpec((N_CHIPS - 1,) + half, lambda i, p: (0, 0, 0))
        out = pl.BlockSpec((None,) + half, lambda i, p: (hsel(p), 0, 0))
    elif name == "w_qkv":
        grid = (1,)
        own = pl.BlockSpec(half, lambda i, p: (0, p[0]))
        got = pl.BlockSpec((N_CHIPS - 1,) + half, lambda i, p: (0, 0, 0))
        out = pl.BlockSpec(half, lambda i, p: (hsel(p), 0))
    else:
        grid = (1,)
        own = pl.BlockSpec((None, None) + half, lambda i, p: (p[0], 0, 0, 0))
        got = pl.BlockSpec((N_CHIPS - 1,) + half, lambda i, p: (0, 0, 0))
        out = pl.BlockSpec(half, lambda i, p: (hsel(p), 0))

    def body(p_ref, own_ref, got_ref, *rest):
        o_ref = rest[-1]

        @pl.when(hsel(p_ref) == p_ref[1])
        def _():
            acc = own_ref[...].astype(F32)
            for j in range(N_CHIPS - 1):
                acc = acc + got_ref[j].astype(F32)
            o_ref[...] = acc

    in_specs, operands, aliases = [own, got], [place, slab, parts], {}
    if into is not None:
        in_specs.append(ANY)
        operands.append(into)
        aliases = {3: 0}
    return pl.pallas_call(
        body, name=f"sum_chips_{_item_tag(item)}",
        grid_spec=pltpu.PrefetchScalarGridSpec(num_scalar_prefetch=1, grid=grid, in_specs=in_specs, out_specs=out),
        out_shape=jax.ShapeDtypeStruct(SHARD_SHAPES[name], F32), input_output_aliases=aliases,
        compiler_params=_params("parallel"),
    )(*operands)


N_DEV = 8


def _peer(d):
    x, y, c, _ = _mesh_place()
    px, py, pc = x ^ (d >> 2), y ^ ((d >> 1) & 1), c ^ (d & 1)
    return (px, py, pc), 4 * px + 2 * py + pc


def _small_all_reduce_host(vec):
    rows, cols = vec.shape

    def run(src, arr, new, scratch, finish):
        slots, stage, send, recv, local = scratch
        x, y, c, _ = _mesh_place()
        me = 4 * x + 2 * y + c
        mine = pltpu.make_async_copy(src[0], slots.at[me], local.at[0])
        copies = []
        for d in range(1, N_DEV):
            dev, _ = _peer(d)
            copies.append(pltpu.make_async_remote_copy(src_ref=src[0], dst_ref=slots.at[me], send_sem=send.at[d - 1],
                                                       recv_sem=recv.at[d - 1], device_id=dev, device_id_type=MESH))
        if not finish:
            mine.start()
            for cp in copies:
                cp.start()
            return
        for d in range(1, N_DEV):
            _, flat = _peer(d)
            got = slots.at[flat]
            pltpu.make_async_remote_copy(src_ref=got, dst_ref=got, send_sem=send.at[d - 1], recv_sem=recv.at[d - 1],
                                         device_id=(x, y, c), device_id_type=MESH).wait_recv()
        for cp in copies:
            cp.wait_send()
        mine.wait()
        acc = slots[0]
        for d in range(1, N_DEV):
            acc = acc + slots[d]
        stage[...] = acc
        out = pltpu.make_async_copy(stage, new[0], local.at[0])
        out.start()
        out.wait()

    scratch = [pltpu.VMEM((N_DEV, rows, cols), F32), pltpu.VMEM((rows, cols), F32),
               pltpu.SemaphoreType.DMA((N_DEV - 1,)), pltpu.SemaphoreType.DMA((N_DEV - 1,)),
               pltpu.SemaphoreType.DMA((1,))]
    return _Hosted([vec], [], [jax.ShapeDtypeStruct(vec.shape, F32)], scratch, functools.partial(run, finish=False),
                   functools.partial(run, finish=True))


class _Reducer:
    FIRST = (("w_in", 1), ("w_out", 1))
    ATT = (("w_qkv", None), ("w_o", None))
    W_IN0 = (("w_in", 0),)
    W_OUT0 = (("w_out", 0),)

    def __init__(self, place):
        self.place = place
        self.shards = {}
        self.state = {}

    def swap(self, items, grads):
        self.state[items] = {"grads": [grads[nm] for nm, _ in items]}
        return _swap_host(items, grads)

    def pair_sums(self, items, slabs):
        st = self.state[items]
        st["sums"] = [_pair_sum(it, g, s, self.place) for it, g, s in zip(items, st["grads"], slabs)]

    def scatter(self, items):
        return _scatter_host(items, self.state[items]["sums"])

    def chip_sums(self, items, parts):
        for it, s, p in zip(items, self.state[items]["sums"], parts):
            self.shards[it[0]] = _sum_chips(it, s, p, self.place, into=self.shards.get(it[0]))

    def share(self, items):
        return _share_host(items, [self.shards[nm] for nm, _ in items])

    def shared(self, items, updated):
        self.shards.update(dict(zip(_names(items), updated)))

    def small_early(self, pieces):
        return _small_all_reduce_host(_pack_small(pieces, EARLY_SMALL))

    def small_early_done(self, total):
        self.early_small = _unpack_small(total, EARLY_SMALL)


def _all_reduce_small(vec, g_pool, share_items, shards):
    n_share = len(share_items)

    def body(v_ref, gp_ref, *rest):
        o_ref, po_ref = rest[n_share:n_share + 2]
        shard_refs = rest[n_share + 2:2 * n_share + 2]
        (slots, pslots, send_sems, recv_sems, psend, precv, hsend, hrecv, share_send,
         share_recv) = rest[2 * n_share + 2:]
        x, y, c, _ = _mesh_place()
        me = 4 * x + 2 * y + c
        _reduce_share(share_items, shard_refs, (share_send, share_recv), finish=False)

        def pool_piece(chip, h):
            return gp_ref.at[pl.ds(2 * h, 2), pl.ds(pl.multiple_of(chip * POOL_ROWS, POOL_ROWS), POOL_ROWS), :]

        slots[me] = v_ref[...]
        pslots[me] = pool_piece(2 * x + y, c)[...]
        copies = []
        for d in range(1, N_DEV):
            peer = (x ^ (d >> 2), y ^ ((d >> 1) & 1), c ^ (d & 1))
            copies.append(pltpu.make_async_remote_copy(
                src_ref=v_ref, dst_ref=slots.at[me], send_sem=send_sems.at[d - 1], recv_sem=recv_sems.at[d - 1],
                device_id=peer, device_id_type=MESH))
            copies.append(pltpu.make_async_remote_copy(
                src_ref=pool_piece(2 * peer[0] + peer[1], peer[2]), dst_ref=pslots.at[me], send_sem=psend.at[d - 1],
                recv_sem=precv.at[d - 1], device_id=peer, device_id_type=MESH))
        for cp in copies:
            cp.start()
        for d in range(1, N_DEV):
            peer = 4 * (x ^ (d >> 2)) + 2 * (y ^ ((d >> 1) & 1)) + (c ^ (d & 1))
            for buf, ss, rs in ((slots, send_sems, recv_sems), (pslots, psend, precv)):
                got = buf.at[peer]
                pltpu.make_async_remote_copy(src_ref=got, dst_ref=got, send_sem=ss.at[d - 1], recv_sem=rs.at[d - 1],
                                             device_id=(x, y, c), device_id_type=MESH).wait_recv()
        for cp in copies:
            cp.wait_send()
        acc, pacc = slots[0], pslots[0]
        for d in range(1, N_DEV):
            acc, pacc = acc + slots[d], pacc + pslots[d]
        o_ref[...] = acc
        po_ref[pl.ds(2 * c, 2)] = pacc
        mine, theirs = po_ref.at[pl.ds(2 * c, 2)], po_ref.at[pl.ds(2 * (1 - c), 2)]
        give = pltpu.make_async_remote_copy(src_ref=mine, dst_ref=mine, send_sem=hsend.at[0], recv_sem=hrecv.at[0],
                                            device_id=(x, y, 1 - c), device_id_type=MESH)
        give.start()
        pltpu.make_async_remote_copy(src_ref=theirs, dst_ref=theirs, send_sem=hsend.at[0], recv_sem=hrecv.at[0],
                                     device_id=(x, y, 1 - c), device_id_type=MESH).wait_recv()
        give.wait_send()
        _reduce_share(share_items, shard_refs, (share_send, share_recv), finish=True)

    vm = pl.BlockSpec(memory_space=pltpu.VMEM)
    piece = (2, POOL_ROWS, POOL_GC)
    return pl.pallas_call(
        body, name="all_reduce_small", in_specs=[vm, vm] + [ANY] * n_share, out_specs=[vm, vm] + [ANY] * n_share,
        out_shape=[jax.ShapeDtypeStruct(vec.shape, F32), jax.ShapeDtypeStruct(SHARD_SHAPES["pool_w"], F32)]
        + [jax.ShapeDtypeStruct(a.shape, a.dtype) for a in shards],
        input_output_aliases={2 + t: 2 + t for t in range(n_share)},
        scratch_shapes=[pltpu.VMEM((N_DEV,) + vec.shape, F32), pltpu.VMEM((N_DEV,) + piece, F32)]
        + [pltpu.SemaphoreType.DMA((N_DEV - 1,))] * 4 + [pltpu.SemaphoreType.DMA((1,))] * 2 + _pair_sems(n_share),
    )(vec, g_pool, *shards)


def _adamw_call(w, g, m, v, name, copy_grad=False):
    shape = w.shape
    cols = shape[-1]
    rows = w.size // cols
    tr = rows
    for cand in (256, 128, 64, 32, 16, 8):
        if rows > cand and rows % cand == 0 and cand * cols * 4 <= 2 * 1024 * 1024:
            tr = cand
            break

    def body(w_ref, g_ref, m_ref, v_ref, d_ref, nm_ref, nv_ref, *g_out):
        gv = g_ref[...]
        if copy_grad:
            g_out[0][...] = gv
        mn = ADAM_B1 * m_ref[...] + (1.0 - ADAM_B1) * gv
        vn = ADAM_B2 * v_ref[...] + (1.0 - ADAM_B2) * jnp.square(gv)
        m_hat = mn / (1.0 - ADAM_B1 ** ADAM_STEP)
        v_hat = vn / (1.0 - ADAM_B2 ** ADAM_STEP)
        d_ref[...] = -ADAM_LR * (m_hat / (jnp.sqrt(v_hat) + ADAM_EPS) + ADAM_WD * w_ref[...])
        nm_ref[...] = mn
        nv_ref[...] = vn

    spec = pl.BlockSpec((tr, cols), lambda i: (i, 0))
    flat = lambda a: a.reshape(rows, cols)
    n_out = 4 if copy_grad else 3
    outs = pl.pallas_call(
        body, name=name, grid=(rows // tr,), in_specs=[spec] * 4, out_specs=[spec] * n_out,
        out_shape=[jax.ShapeDtypeStruct((rows, cols), F32)] * n_out, compiler_params=_params("parallel"),
    )(flat(w), flat(g), flat(m), flat(v))
    return tuple(o.reshape(shape) for o in outs)


WEIGHTS = ("pool_w", "pool_b", "pool_scale", "attn_w_qkv", "attn_b_qkv", "attn_sinks", "attn_w_o", "norm_mix",
           "norm_ffn", "ffn_w_in", "ffn_conv_w", "ffn_conv_b", "ffn_w_out", "norm_f")
SMALL_GATHER = 128 * 71
EARLY_SMALL = ((("b_qkv", 1536), ("sinks", 128), ("norm_mix1", 1024), ("norm_ffn1", 1024), ("conv_w", 6 * 2 * D_FF),
                ("conv_b", 4 * D_FF), ("norm_f", 1024), ("loss", 128)), 6272)
LATE_SMALL = ((("pool_b", 1024), ("pool_scale", 1024), ("norm_mix0", 1024), ("norm_ffn0", 1024)), 512)


def _pack_small(pieces, layout):
    sizes, cols = layout
    flat = jnp.concatenate([jnp.pad(pieces[nm].reshape(-1), (0, size - pieces[nm].size)) for nm, size in sizes])
    return jnp.pad(flat, (0, 8 * cols - flat.size)).reshape(8, cols)


def _unpack_small(vec, layout):
    flat = vec.reshape(-1)
    out, off = {}, 0
    for nm, size in layout[0]:
        out[nm] = flat[off:off + size]
        off += size
    return out


def kernel(x, pool_w, pool_b, pool_scale, attn_w_qkv, attn_b_qkv, attn_sinks, attn_w_o, norm_mix, norm_ffn, ffn_w_in, ffn_conv_w, ffn_conv_b, ffn_w_out, norm_f, loss_target, m_pool_w, m_pool_b, m_pool_scale, m_attn_w_qkv, m_attn_b_qkv, m_attn_sinks, m_attn_w_o, m_norm_mix, m_norm_ffn, m_ffn_w_in, m_ffn_conv_w, m_ffn_conv_b, m_ffn_w_out, m_norm_f, v_pool_w, v_pool_b, v_pool_scale, v_attn_w_qkv, v_attn_b_qkv, v_attn_sinks, v_attn_w_o, v_norm_mix, v_norm_ffn, v_ffn_w_in, v_ffn_conv_w, v_ffn_conv_b, v_ffn_w_out, v_norm_f):
    weights = dict(pool_w=pool_w, pool_b=pool_b, pool_scale=pool_scale, attn_w_qkv=attn_w_qkv, attn_b_qkv=attn_b_qkv,
                   attn_sinks=attn_sinks, attn_w_o=attn_w_o, norm_mix=norm_mix, norm_ffn=norm_ffn, ffn_w_in=ffn_w_in,
                   ffn_conv_w=ffn_conv_w, ffn_conv_b=ffn_conv_b, ffn_w_out=ffn_w_out, norm_f=norm_f)
    m_in = dict(pool_w=m_pool_w, pool_b=m_pool_b, pool_scale=m_pool_scale, attn_w_qkv=m_attn_w_qkv,
                attn_b_qkv=m_attn_b_qkv, attn_sinks=m_attn_sinks, attn_w_o=m_attn_w_o, norm_mix=m_norm_mix,
                norm_ffn=m_norm_ffn, ffn_w_in=m_ffn_w_in, ffn_conv_w=m_ffn_conv_w, ffn_conv_b=m_ffn_conv_b,
                ffn_w_out=m_ffn_w_out, norm_f=m_norm_f)
    v_in = dict(pool_w=v_pool_w, pool_b=v_pool_b, pool_scale=v_pool_scale, attn_w_qkv=v_attn_w_qkv,
                attn_b_qkv=v_attn_b_qkv, attn_sinks=v_attn_sinks, attn_w_o=v_attn_w_o, norm_mix=v_norm_mix,
                norm_ffn=v_norm_ffn, ffn_w_in=v_ffn_w_in, ffn_conv_w=v_ffn_conv_w, ffn_conv_b=v_ffn_conv_b,
                ffn_w_out=v_ffn_w_out, norm_f=v_norm_f)
    chip = 2 * lax.axis_index("x") + lax.axis_index("y")
    core = lax.axis_index("c")

    place = jnp.stack([chip, core]).astype(jnp.int32)
    shards = {"w_in": ffn_w_in, "w_out": ffn_w_out, "w_qkv": attn_w_qkv[0], "w_o": attn_w_o[0], "pool_w": pool_w[0]}
    placed = {nm: _place_shard(nm, shards[nm], place) for nm in BIG}
    small = jnp.concatenate([pool_b.reshape(-1), attn_b_qkv.reshape(-1), ffn_conv_w.reshape(-1)]).reshape(1, SMALL_GATHER)
    full, small_all = _gather_early(placed, small)
    small_all = small_all.reshape(N_CHIPS, SMALL_GATHER)
    pool_b_full = small_all[:, :256].reshape(N_CHIPS, 4, POOL_ROWS).transpose(1, 0, 2).reshape(1, D_MODEL)
    b_qkv_full = small_all[:, 256:640].reshape(1, QKV_DIM)
    conv_w_full = small_all[:, 640:].reshape(N_CHIPS, 2, 3, W_IN_COLS).transpose(1, 2, 0, 3).reshape(2, 3, 2 * D_FF)

    reducer = _Reducer(place)
    _, grad_x, grads = _local_step(
        x[0], loss_target[0], full, norm_mix=norm_mix, norm_ffn=norm_ffn, norm_f=norm_f.reshape(1, D_MODEL),
        pool_b=pool_b_full, pool_scale=pool_scale, b_qkv=b_qkv_full, sinks=attn_sinks[0], conv_w=conv_w_full,
        conv_b=ffn_conv_b, late_gather=_late_gather_host, reducer=reducer)
    reduced = dict(reducer.shards)

    late = {"pool_b": grads["pool_b"], "pool_scale": grads["pool_scale"], "norm_mix0": grads["norm_mix"][0],
            "norm_ffn0": grads["norm_ffn"][0]}
    last = _Reducer.W_OUT0
    late_sum, reduced["pool_w"], *shared = _all_reduce_small(_pack_small(late, LATE_SMALL), grads["pool_w"], last,
                                                             [reduced[nm] for nm in _names(last)])
    reduced.update(zip(_names(last), shared))
    small_g = {**reducer.early_small, **_unpack_small(late_sum, LATE_SMALL)}
    small_g["norm_mix"] = jnp.concatenate([small_g["norm_mix0"], small_g["norm_mix1"]])
    small_g["norm_ffn"] = jnp.concatenate([small_g["norm_ffn0"], small_g["norm_ffn1"]])
    loss = small_g["loss"][0]
    pool_b_g = lax.dynamic_slice_in_dim(small_g["pool_b"].reshape(4, N_CHIPS, POOL_ROWS), chip, 1, axis=1)
    b_qkv_g = lax.dynamic_slice_in_dim(small_g["b_qkv"].reshape(N_CHIPS, QKV_COLS), chip, 1, axis=0)
    conv_w_g = lax.dynamic_slice_in_dim(small_g["conv_w"].reshape(2, 3, N_CHIPS, W_IN_COLS), chip, 1, axis=2)
    grad_w = {
        "pool_w": reduced["pool_w"].reshape(pool_w.shape), "pool_b": pool_b_g.reshape(pool_b.shape),
        "pool_scale": small_g["pool_scale"].reshape(pool_scale.shape),
        "attn_w_qkv": reduced["w_qkv"].reshape(attn_w_qkv.shape), "attn_b_qkv": b_qkv_g.reshape(attn_b_qkv.shape),
        "attn_sinks": small_g["sinks"][:N_HEADS].reshape(attn_sinks.shape),
        "attn_w_o": reduced["w_o"].reshape(attn_w_o.shape), "norm_mix": small_g["norm_mix"].reshape(norm_mix.shape),
        "norm_ffn": small_g["norm_ffn"].reshape(norm_ffn.shape), "ffn_w_in": reduced["w_in"],
        "ffn_conv_w": conv_w_g.reshape(ffn_conv_w.shape), "ffn_conv_b": small_g["conv_b"].reshape(ffn_conv_b.shape),
        "ffn_w_out": reduced["w_out"], "norm_f": small_g["norm_f"].reshape(norm_f.shape),
    }

    delta, new_m, new_v = {}, {}, {}
    exchanged = ("pool_w", "attn_w_qkv", "attn_w_o", "ffn_w_in", "ffn_w_out")
    for nm in WEIGHTS:
        res = _adamw_call(weights[nm], grad_w[nm], m_in[nm], v_in[nm], f"adamw_{nm}", copy_grad=nm in exchanged)
        delta[nm], new_m[nm], new_v[nm] = res[:3]
        if nm in exchanged:
            grad_w[nm] = res[3]
    return (loss, grad_x.reshape(x.shape), *[grad_w[nm] for nm in WEIGHTS], *[delta[nm] for nm in WEIGHTS],
            *[new_m[nm] for nm in WEIGHTS], *[new_v[nm] for nm in WEIGHTS])
```

```python
import functools
from typing import Callable, NamedTuple

import jax
import jax.numpy as jnp
from jax import lax
from jax.experimental import pallas as pl
from jax.experimental.pallas import tpu as pltpu

F32 = jnp.float32
MXU_DTYPE = jnp.bfloat16
WIRE_DTYPE = jnp.bfloat16
U_DTYPE = jnp.float32

D_MODEL = 1024
D_FF = 2816
QKV_DIM = 1536
HEAD_DIM = 64
N_HEADS = 16
N_KV_HEADS = 4
ATT_BLOCK = 128
POOL_WINDOWS = (2, 4, 8, 16)
POOL_GC = 256
POOL_HALO = 16
CONV_HALO = 8
RMS_EPS = 1e-6
ATT_SCALE = HEAD_DIM ** -0.5
ALIBI_SLOPES = tuple(2.0 ** (-8.0 / N_HEADS * (h + 1)) for h in range(N_HEADS))

ADAM_LR = 0.001
ADAM_B1 = 0.9
ADAM_B2 = 0.999
ADAM_EPS = 1e-08
ADAM_WD = 0.01
ADAM_STEP = 10

N_CHIPS = 4
MESH = pl.DeviceIdType.MESH
VMEM_LIMIT_BYTES = 56 * 1024 * 1024
ANY = pl.BlockSpec(memory_space=pl.ANY)


def _params(*semantics):
    return pltpu.CompilerParams(dimension_semantics=semantics, vmem_limit_bytes=VMEM_LIMIT_BYTES)


def _rms(x, g):
    return x * lax.rsqrt(jnp.mean(x * x, axis=-1, keepdims=True) + RMS_EPS) * g


def _rms_bwd(x, g, dy):
    rstd = lax.rsqrt(jnp.mean(x * x, axis=-1, keepdims=True) + RMS_EPS)
    xhat = x * rstd
    dxhat = dy * g
    dx = rstd * (dxhat - xhat * jnp.mean(dxhat * xhat, axis=-1, keepdims=True))
    return dx, dy * xhat


def _shift_down(x, s):
    return pltpu.roll(x, s, axis=0)


def _shift_up(x, s):
    return pltpu.roll(x, x.shape[0] - s, axis=0)


def _sigmoid(x):
    return 1.0 / (1.0 + jnp.exp(-x))


_DOT_DIMS = {"nn": ((1,), (0,)), "nt": ((1,), (1,)), "tn": ((0,), (0,))}


class _Hosted(NamedTuple):
    sources: list
    arrays: list
    new: list
    sems: list
    first: Callable
    last: Callable


def _call(body, hosted, *, name, grid, in_specs, out_specs, out_shape, operands, semantics, scratch_shapes=(),
          aliases=None):
    n_in, n_out, n_scr = len(in_specs), len(out_specs), len(scratch_shapes)
    aliases = dict(aliases or {})
    if hosted is None:
        def plain(*refs):
            body(refs[:n_in], refs[n_in:n_in + n_out], refs[n_in + n_out:], lambda: None, lambda: None)

        return pl.pallas_call(plain, name=name, grid=grid, in_specs=list(in_specs), out_specs=list(out_specs),
                              out_shape=list(out_shape), scratch_shapes=list(scratch_shapes),
                              input_output_aliases=aliases, compiler_params=_params(*semantics))(*operands)
    ns, na, nn = len(hosted.sources), len(hosted.arrays), len(hosted.new)

    def carrying(*refs):
        ins, src = refs[:n_in], refs[n_in:n_in + ns]
        o0 = n_in + ns + na
        outs, arr = refs[o0:o0 + n_out], refs[o0 + n_out:o0 + n_out + na]
        new = refs[o0 + n_out + na:o0 + n_out + na + nn]
        s0 = o0 + n_out + na + nn
        scratch, sems = refs[s0:s0 + n_scr], refs[s0 + n_scr:]
        ids = [pl.program_id(d) for d in range(len(grid))]
        is_first = functools.reduce(lambda p, q: p & q, [i == 0 for i in ids])
        is_last = functools.reduce(lambda p, q: p & q, [i == n - 1 for i, n in zip(ids, grid)])
        body(ins, outs, scratch, lambda: pl.when(is_first)(lambda: hosted.first(src, arr, new, sems)),
             lambda: pl.when(is_last)(lambda: hosted.last(src, arr, new, sems)))

    for t in range(na):
        aliases[n_in + ns + t] = n_out + t
    return pl.pallas_call(
        carrying, name=name, grid=grid, in_specs=list(in_specs) + [ANY] * (ns + na),
        out_specs=list(out_specs) + [ANY] * (na + nn),
        out_shape=list(out_shape) + [jax.ShapeDtypeStruct(a.shape, a.dtype) for a in hosted.arrays] + list(hosted.new),
        scratch_shapes=list(scratch_shapes) + list(hosted.sems), input_output_aliases=aliases,
        compiler_params=_params(*["arbitrary"] * len(grid)))(*operands, *hosted.sources, *hosted.arrays)


def _both(h1, h2):
    def run(which, src, arr, new, sems):
        cut = lambda seq, n: (seq[:n], seq[n:])
        (s1, s2), (a1, a2) = cut(src, len(h1.sources)), cut(arr, len(h1.arrays))
        (n1, n2), (m1, m2) = cut(new, len(h1.new)), cut(sems, len(h1.sems))
        getattr(h1, which)(s1, a1, n1, m1)
        getattr(h2, which)(s2, a2, n2, m2)

    return _Hosted(h1.sources + h2.sources, h1.arrays + h2.arrays, h1.new + h2.new, h1.sems + h2.sems,
                   functools.partial(run, "first"), functools.partial(run, "last"))


def _ride(hooks, stage, call, env=None):
    if not hooks or stage not in hooks:
        res = call(None)
        return list(res) if isinstance(res, (list, tuple)) else [res]
    make, done = hooks[stage]
    hosted = make(env)
    res = call(hosted)
    own = len(res) - len(hosted.arrays) - len(hosted.new)
    done(res[own:])
    return list(res[:own])


def _matmul(a, b, *, mode, name, grid, a_spec, b_spec, o_spec, out_shape, bias=None, bias_spec=None, residual=None,
            residual_spec=None, into=None, hosted=None, norm_g=None):
    nk = grid[2]
    dims = (_DOT_DIMS[mode], ((), ()))
    acc_shape = tuple(d for d in o_spec.block_shape if d is not None)

    def body(ins, outs, scratch, first, last):
        a_ref, b_ref = ins[0], ins[1]
        pos = 2
        bias_ref = res_ref = None
        if bias is not None:
            bias_ref = ins[pos]
            pos += 1
        if residual is not None:
            res_ref = ins[pos]
            pos += 1
        o_ref = outs[0]
        acc_ref = scratch[0] if nk > 1 else None
        first()
        prod = lax.dot_general(a_ref[...].astype(MXU_DTYPE), b_ref[...].astype(MXU_DTYPE), dims,
                               preferred_element_type=F32)

        def finish(acc):
            if bias_ref is not None:
                acc = acc + bias_ref[...]
            if res_ref is not None:
                acc = acc + res_ref[...]
            o_ref[...] = acc.astype(o_ref.dtype)
            if norm_g is not None:
                outs[1][...] = _rms(acc, ins[pos][...]).astype(outs[1].dtype)

        if nk == 1:
            finish(prod)
        else:
            k = pl.program_id(2)

            @pl.when(k == 0)
            def _():
                acc_ref[...] = prod

            @pl.when(k > 0)
            def _():
                acc_ref[...] += prod

            @pl.when(k == nk - 1)
            def _():
                finish(acc_ref[...])

        last()

    operands, in_specs = [a, b], [a_spec, b_spec]
    if bias is not None:
        operands.append(bias)
        in_specs.append(bias_spec)
    if residual is not None:
        operands.append(residual)
        in_specs.append(residual_spec)
    out_specs, out_shapes = [o_spec], [out_shape]
    if norm_g is not None:
        assert acc_shape[-1] == norm_g.shape[-1], "the fused norm needs whole rows in an output tile"
        operands.append(norm_g)
        in_specs.append(bias_spec)
        out_specs.append(o_spec)
        out_shapes.append(jax.ShapeDtypeStruct(out_shape.shape, MXU_DTYPE))
    aliases = {}
    if into is not None:
        aliases = {len(operands): 0}
        operands.append(into)
        in_specs.append(ANY)
    res = _call(body, hosted, name=name, grid=grid, in_specs=in_specs, out_specs=out_specs, out_shape=out_shapes,
                operands=operands, semantics=("parallel", "parallel", "arbitrary"),
                scratch_shapes=[pltpu.VMEM(acc_shape, F32)] if nk > 1 else [], aliases=aliases)
    return res if hosted or norm_g is not None else res[0]


def _tile(n, want):
    t = min(n, want)
    assert n % t == 0, (n, want)
    return t


def _mm_nn(a, b, name, *, b_lead=None, bias=None, residual=None, out_dtype=F32, tm=1024, tn=1024, tk=1024,
           hosted=None, norm_g=None):
    (m, k), n = a.shape, b.shape[-1]
    tm, tn, tk = _tile(m, tm), _tile(n, tn), _tile(k, tk)
    if b_lead is None:
        b_spec = pl.BlockSpec((tk, tn), lambda i, j, kk: (kk, j))
    else:
        b_spec = pl.BlockSpec((None, tk, tn), lambda i, j, kk: (b_lead, kk, j))
    return _matmul(
        a, b, mode="nn", name=name, grid=(m // tm, n // tn, k // tk),
        a_spec=pl.BlockSpec((tm, tk), lambda i, j, kk: (i, kk)), b_spec=b_spec,
        o_spec=pl.BlockSpec((tm, tn), lambda i, j, kk: (i, j)), out_shape=jax.ShapeDtypeStruct((m, n), out_dtype),
        bias=bias, bias_spec=pl.BlockSpec((1, tn), lambda i, j, kk: (0, j)),
        residual=residual, residual_spec=pl.BlockSpec((tm, tn), lambda i, j, kk: (i, j)), hosted=hosted,
        norm_g=norm_g)


def _mm_nt(a, b, name, *, b_lead=None, out_dtype=F32, tm=1024, tn=1024, tk=1024, hosted=None):
    (m, k), n = a.shape, b.shape[-2]
    tm, tn, tk = _tile(m, tm), _tile(n, tn), _tile(k, tk)
    if b_lead is None:
        b_spec = pl.BlockSpec((tn, tk), lambda i, j, kk: (j, kk))
    else:
        b_spec = pl.BlockSpec((None, tn, tk), lambda i, j, kk: (b_lead, j, kk))
    return _matmul(
        a, b, mode="nt", name=name, grid=(m // tm, n // tn, k // tk),
        a_spec=pl.BlockSpec((tm, tk), lambda i, j, kk: (i, kk)), b_spec=b_spec,
        o_spec=pl.BlockSpec((tm, tn), lambda i, j, kk: (i, j)), out_shape=jax.ShapeDtypeStruct((m, n), out_dtype),
        hosted=hosted)


def _mm_tn(a, b, name, *, lead=None, n_lead=None, into=None, tm=1024, tn=1024, tk=1024, out_dtype=F32, hosted=None):
    (k, m), n = a.shape, b.shape[-1]
    tm, tn, tk = _tile(m, tm), _tile(n, tn), _tile(k, tk)
    if lead is None:
        o_spec = pl.BlockSpec((tm, tn), lambda i, j, kk: (i, j))
        out_shape = jax.ShapeDtypeStruct((m, n), out_dtype)
    else:
        o_spec = pl.BlockSpec((None, tm, tn), lambda i, j, kk: (lead, i, j))
        out_shape = jax.ShapeDtypeStruct((n_lead, m, n), out_dtype)
    return _matmul(
        a, b, mode="tn", name=name, grid=(m // tm, n // tn, k // tk),
        a_spec=pl.BlockSpec((tk, tm), lambda i, j, kk: (kk, i)),
        b_spec=pl.BlockSpec((tk, tn), lambda i, j, kk: (kk, j)), o_spec=o_spec, out_shape=out_shape, into=into,
        hosted=hosted)


ROW_TILE = 512


def _pool_windows_causal(ext, first_row):
    tt = ext.shape[0] - POOL_HALO
    t = first_row + lax.broadcasted_iota(jnp.int32, (tt, 1), 0)
    outs = []
    for gi, win in enumerate(POOL_WINDOWS):
        cols = slice(gi * POOL_GC, (gi + 1) * POOL_GC)
        s = ext[:, cols]
        sh = 1
        while sh < win:
            s = s + _shift_down(s, sh)
            sh *= 2
        count = jnp.minimum(t + 1, win).astype(F32)
        outs.append(s[POOL_HALO:] / count - ext[POOL_HALO:, cols])
    return outs


def _prev_halo_spec(tt, halo, width):
    per = tt // halo
    return pl.BlockSpec((halo, width), lambda i: (jnp.maximum(i * per - 1, 0), 0))


def _pool_fwd_call(h0, g_mix, pool_w, pool_b, pool_scale, g_ffn, hosted=None):
    t = h0.shape[0]
    tt = _tile(t, ROW_TILE)

    def body(ins, outs, scratch, first, last):
        h_ref, halo_ref, gm_ref, w_ref, b_ref, sc_ref, gf_ref = ins
        h1_ref, hn_ref = outs
        i = pl.program_id(0)
        first()
        x, gm = h_ref[...], gm_ref[...]
        top = jnp.where(i > 0, _rms(halo_ref[...], gm), 0.0)
        ext = jnp.concatenate([top, _rms(x, gm)], axis=0)
        ps = _pool_windows_causal(ext, i * tt)
        ys = [jnp.dot(p.astype(MXU_DTYPE), w_ref[gi], preferred_element_type=F32) for gi, p in enumerate(ps)]
        mix = (jnp.concatenate(ys, axis=1) + b_ref[...]) * sc_ref[...]
        h1 = x + mix
        h1_ref[...] = h1
        hn_ref[...] = _rms(h1, gf_ref[...]).astype(hn_ref.dtype)
        last()

    row = pl.BlockSpec((tt, D_MODEL), lambda i: (i, 0))
    vec = pl.BlockSpec((1, D_MODEL), lambda i: (0, 0))
    wsp = pl.BlockSpec((len(POOL_WINDOWS), POOL_GC, POOL_GC), lambda i: (0, 0, 0))
    return _call(
        body, hosted, name="pool_fwd", grid=(t // tt,),
        in_specs=[row, _prev_halo_spec(tt, POOL_HALO, D_MODEL), vec, wsp, vec, vec, vec], out_specs=[row, row],
        out_shape=[jax.ShapeDtypeStruct(h0.shape, F32), jax.ShapeDtypeStruct(h0.shape, MXU_DTYPE)],
        operands=(h0, h0, g_mix, pool_w, pool_b, pool_scale, g_ffn), semantics=("parallel",))


def _pool_bwd_call(h0, dh1, g_mix, pool_w, pool_b, pool_scale, hosted=None):
    t = h0.shape[0]
    tt = _tile(t, ROW_TILE)
    nt = t // tt
    per = tt // POOL_HALO

    def body(ins, outs, scratch, first, last):
        h_ref, halo_ref, d_ref, dnext_ref, gm_ref, w_ref, b_ref, sc_ref = ins
        gx_ref, dw_ref, dv_ref = outs
        i = pl.program_id(0)
        first()
        x, gm, sc = h_ref[...], gm_ref[...], sc_ref[...]
        top = jnp.where(i > 0, _rms(halo_ref[...], gm), 0.0)
        ext = jnp.concatenate([top, _rms(x, gm)], axis=0)
        ps = _pool_windows_causal(ext, i * tt)

        dy = d_ref[...]
        dy_ext = jnp.concatenate([dy, jnp.where(i < nt - 1, dnext_ref[...], 0.0)], axis=0)
        dpre_ext = dy_ext * sc
        t_ext = i * tt + lax.broadcasted_iota(jnp.int32, (tt + POOL_HALO, 1), 0)

        @pl.when(i == 0)
        def _():
            dw_ref[...] = jnp.zeros_like(dw_ref)
            dv_ref[...] = jnp.zeros_like(dv_ref)

        dhn, ypre = [], []
        for gi, win in enumerate(POOL_WINDOWS):
            cols = slice(gi * POOL_GC, (gi + 1) * POOL_GC)
            w = w_ref[gi]
            p16 = ps[gi].astype(MXU_DTYPE)
            dpre16 = dpre_ext[:, cols].astype(MXU_DTYPE)
            ypre.append(jnp.dot(p16, w, preferred_element_type=F32))
            dw_ref[gi] += lax.dot_general(p16, dpre16[:tt], (((0,), (0,)), ((), ())), preferred_element_type=F32)
            dp_ext = lax.dot_general(dpre16, w, (((1,), (1,)), ((), ())), preferred_element_type=F32)
            s = dp_ext / jnp.minimum(t_ext + 1, win).astype(F32)
            sh = 1
            while sh < win:
                s = s + _shift_up(s, sh)
                sh *= 2
            dhn.append(s[:tt] - dp_ext[:tt])
        dhn = jnp.concatenate(dhn, axis=1)
        ypre = jnp.concatenate(ypre, axis=1) + b_ref[...]
        dx, dgt = _rms_bwd(x, gm, dhn)
        gx_ref[...] = dy + dx
        dv_ref[0:1, :] += jnp.sum(dpre_ext[:tt], axis=0, keepdims=True)
        dv_ref[1:2, :] += jnp.sum(dy * ypre, axis=0, keepdims=True)
        dv_ref[2:3, :] += jnp.sum(dgt, axis=0, keepdims=True)
        last()

    row = pl.BlockSpec((tt, D_MODEL), lambda i: (i, 0))
    vec = pl.BlockSpec((1, D_MODEL), lambda i: (0, 0))
    nxt = pl.BlockSpec((POOL_HALO, D_MODEL), lambda i: (jnp.minimum((i + 1) * per, t // POOL_HALO - 1), 0))
    wsp = pl.BlockSpec((len(POOL_WINDOWS), POOL_GC, POOL_GC), lambda i: (0, 0, 0))
    return _call(
        body, hosted, name="pool_bwd", grid=(nt,),
        in_specs=[row, _prev_halo_spec(tt, POOL_HALO, D_MODEL), row, nxt, vec, wsp, vec, vec],
        out_specs=[row, wsp, pl.BlockSpec((8, D_MODEL), lambda i: (0, 0))],
        out_shape=[jax.ShapeDtypeStruct(h0.shape, F32),
                   jax.ShapeDtypeStruct((len(POOL_WINDOWS), POOL_GC, POOL_GC), F32),
                   jax.ShapeDtypeStruct((8, D_MODEL), F32)],
        operands=(h0, h0, dh1, dh1, g_mix, pool_w, pool_b, pool_scale), semantics=("arbitrary",))


CONV_ROWS = 512
CONV_SUB = 128
LANES = 128


def _causal_conv(u_ext, w, b):
    return w[0:1] * _shift_down(u_ext, 2) + w[1:2] * _shift_down(u_ext, 1) + w[2:3] * u_ext + b


def _piece_rows(load, s, sub, tt, top, bottom=None):
    lo, hi = s * sub - CONV_HALO, (s + 1) * sub + (CONV_HALO if bottom else 0)
    parts = [top()] if lo < 0 else []
    lo = max(lo, 0)
    parts.append(load(lo, min(hi, tt) - lo))
    if hi > tt:
        parts.append(bottom())
    return parts[0] if len(parts) == 1 else jnp.concatenate(parts, axis=0)


def _fold8(x):
    acc = x[0:8]
    for r in range(8, x.shape[0], 8):
        acc = acc + x[r:r + 8]
    return acc


def _conv_glu_fwd_call(u, conv_w, conv_b, name, hosted=None):
    _, t, f = u.shape
    tt = _tile(t, CONV_ROWS)
    sub = _tile(tt, CONV_SUB)
    per = tt // CONV_HALO

    def body(ins, outs, scratch, first, last):
        u_ref, up_ref, w_ref, b_ref = ins
        z_ref = outs[0]
        i = pl.program_id(0)
        first()

        def chunk(j, carry):
            lanes = pl.ds(pl.multiple_of(j * LANES, LANES), LANES)
            w, b = w_ref[:, :, lanes], b_ref[:, lanes]
            for s in range(tt // sub):
                c = []
                for hf in range(2):
                    ext = _piece_rows(lambda at, n: u_ref[hf, pl.ds(at, n), lanes].astype(F32), s, sub, tt,
                                      top=lambda: jnp.where(i > 0, up_ref[hf, :, lanes].astype(F32), 0.0))
                    c.append(_causal_conv(ext, w[:, hf, :], b[hf:hf + 1, :])[CONV_HALO:])
                z_ref[pl.ds(s * sub, sub), lanes] = (c[0] * c[1] * _sigmoid(c[1])).astype(z_ref.dtype)
            return carry

        lax.fori_loop(0, f // LANES, chunk, 0)
        last()

    res = _call(
        body, hosted, name=name, grid=(t // tt,),
        in_specs=[pl.BlockSpec((2, tt, f), lambda i: (0, i, 0)),
                  pl.BlockSpec((2, CONV_HALO, f), lambda i: (0, jnp.maximum(i * per - 1, 0), 0)),
                  pl.BlockSpec((3, 2, f), lambda i: (0, 0, 0)), pl.BlockSpec((2, f), lambda i: (0, 0))],
        out_specs=[pl.BlockSpec((tt, f), lambda i: (i, 0))], out_shape=[jax.ShapeDtypeStruct((t, f), MXU_DTYPE)],
        operands=(u, u, conv_w, conv_b), semantics=("parallel",))
    return res if hosted else res[0]


def _conv_glu_bwd_call(u, dz, conv_w, conv_b, name, hosted=None):
    _, t, f = u.shape
    tt = _tile(t, CONV_ROWS)
    sub = _tile(tt, CONV_SUB)
    nt = t // tt
    per = tt // CONV_HALO
    halo = CONV_HALO

    def body(ins, outs, scratch, first, last):
        u_ref, up_ref, un_ref, dz_ref, dzn_ref, w_ref, b_ref = ins
        du_ref, dw_ref, db_ref = outs
        dwacc, dbacc = scratch
        i = pl.program_id(0)
        first()

        @pl.when(i == 0)
        def _():
            dwacc[...] = jnp.zeros_like(dwacc)
            dbacc[...] = jnp.zeros_like(dbacc)

        main = slice(halo, halo + sub)

        def chunk(j, carry):
            lanes = pl.ds(pl.multiple_of(j * LANES, LANES), LANES)
            w, b = w_ref[:, :, lanes], b_ref[:, lanes]
            for s in range(tt // sub):
                ue = [_piece_rows(lambda at, n: u_ref[hf, pl.ds(at, n), lanes].astype(F32), s, sub, tt,
                                  top=lambda: jnp.where(i > 0, up_ref[hf, :, lanes].astype(F32), 0.0),
                                  bottom=lambda: un_ref[hf, :, lanes].astype(F32)) for hf in range(2)]
                a, g = [_causal_conv(ue[hf], w[:, hf, :], b[hf:hf + 1, :])[halo:] for hf in range(2)]
                sg = _sigmoid(g)
                if (s + 1) * sub < tt:
                    dzs = dz_ref[pl.ds(s * sub, sub + 2 * halo), lanes].astype(F32)[:sub + halo]
                else:
                    dzs = jnp.concatenate([dz_ref[pl.ds(s * sub, sub), lanes].astype(F32),
                                           jnp.where(i < nt - 1, dzn_ref[:, lanes].astype(F32), 0.0)], axis=0)
                gs = g * sg
                dc = [dzs * gs, dzs * a * (sg + gs - gs * sg)]
                for hf in range(2):
                    d = dc[hf]
                    d0, d1, d2 = d[:sub], _shift_up(d, 1)[:sub], _shift_up(d, 2)[:sub]
                    du = w[2:3, hf, :] * d0 + w[1:2, hf, :] * d1 + w[0:1, hf, :] * d2
                    du_ref[hf, pl.ds(s * sub, sub), lanes] = du.astype(du_ref.dtype)
                    um = ue[hf][main]
                    for k, dd in enumerate((d2, d1, d0)):
                        dwacc[k, hf, :, lanes] += _fold8(dd * um)
                    dbacc[hf, :, lanes] += _fold8(d0)
            return carry

        lax.fori_loop(0, f // LANES, chunk, 0)

        @pl.when(i == nt - 1)
        def _():
            dw_ref[...] = jnp.sum(dwacc[...], axis=2)
            db_ref[...] = jnp.sum(dbacc[...], axis=1)

        last()

    end = t // halo - 1
    return _call(
        body, hosted, name=name, grid=(nt,),
        in_specs=[pl.BlockSpec((2, tt, f), lambda i: (0, i, 0)),
                  pl.BlockSpec((2, halo, f), lambda i: (0, jnp.maximum(i * per - 1, 0), 0)),
                  pl.BlockSpec((2, halo, f), lambda i: (0, jnp.minimum((i + 1) * per, end), 0)),
                  pl.BlockSpec((tt, f), lambda i: (i, 0)),
                  pl.BlockSpec((halo, f), lambda i: (jnp.minimum((i + 1) * per, end), 0)),
                  pl.BlockSpec((3, 2, f), lambda i: (0, 0, 0)), pl.BlockSpec((2, f), lambda i: (0, 0))],
        out_specs=[pl.BlockSpec((2, tt, f), lambda i: (0, i, 0)),
                   pl.BlockSpec((3, 2, f), lambda i: (0, 0, 0)), pl.BlockSpec((2, f), lambda i: (0, 0))],
        out_shape=[jax.ShapeDtypeStruct(u.shape, MXU_DTYPE), jax.ShapeDtypeStruct((3, 2, f), F32),
                   jax.ShapeDtypeStruct((2, f), F32)],
        operands=(u, u, u, dz, dz, conv_w, conv_b), semantics=("arbitrary",),
        scratch_shapes=[pltpu.VMEM((3, 2, 8, f), F32), pltpu.VMEM((2, 8, f), F32)])


def _kv_even_odd(band, k):
    pair, parity = divmod(k, 2)
    blk = band[:, 128 * pair:128 * (pair + 1)].astype(F32)
    lane = lax.broadcasted_iota(jnp.int32, blk.shape, 1)
    own = jnp.where((lane >= HEAD_DIM) == (parity == 1), blk, 0.0)
    swapped = pltpu.roll(own, HEAD_DIM, axis=1)
    even, odd = (own, swapped) if parity == 0 else (swapped, own)
    return even.astype(MXU_DTYPE), odd.astype(MXU_DTYPE)


def _stack_pairs(ref, k):
    return jnp.concatenate([ref[:, 256 * k:256 * k + 128], ref[:, 256 * k + 128:256 * k + 256]], axis=0)


_GROUP_ORDER = (0, 2, 1, 3)


def _attn_fill_tables(bias_ref):
    shape = (4 * ATT_BLOCK, 2 * ATT_BLOCK)
    row = lax.broadcasted_iota(jnp.int32, shape, 0)
    col = lax.broadcasted_iota(jnp.int32, shape, 1)
    dist = (row & (ATT_BLOCK - 1)) + ATT_BLOCK - col
    ok = (dist >= 0) & (dist < ATT_BLOCK)
    distf = dist.astype(F32)
    rb = lax.broadcasted_iota(jnp.int32, (shape[0], 1), 0) // ATT_BLOCK
    for k in range(N_KV_HEADS):
        slope = jnp.zeros((shape[0], 1), F32)
        for r, gq in enumerate(_GROUP_ORDER):
            slope = jnp.where(rb == r, ALIBI_SLOPES[4 * k + gq], slope)
        bias = jnp.where(ok, -slope * distf, -jnp.inf)
        bias_ref[0, k] = bias
        bias_ref[1, k] = jnp.where(col >= ATT_BLOCK, bias, -jnp.inf)


def _attn_probs(qs, k_even, k_odd, bias, k, sink_ref):
    nt_dims = (((1,), (1,)), ((), ()))
    s = jnp.concatenate([lax.dot_general(qs, k_even, nt_dims, preferred_element_type=F32),
                         lax.dot_general(qs, k_odd, nt_dims, preferred_element_type=F32)], axis=0) + bias
    rb = lax.broadcasted_iota(jnp.int32, (s.shape[0], 1), 0) // ATT_BLOCK
    sink = jnp.zeros((s.shape[0], 1), F32)
    for r, gq in enumerate(_GROUP_ORDER):
        sink = jnp.where(rb == r, sink_ref[4 * k + gq], sink)
    m = jnp.maximum(jnp.max(s, axis=-1, keepdims=True), sink)
    p = jnp.exp(s - m)
    es = jnp.exp(sink - m)
    return p, 1.0 / (jnp.sum(p, axis=-1, keepdims=True) + es), es


_ATTN_TABLES = [pltpu.VMEM((2, N_KV_HEADS, 4 * ATT_BLOCK, 2 * ATT_BLOCK), F32)]


def _attn_specs(with_do):
    prev = lambda n: jnp.maximum(n - 1, 0)
    specs = [pl.BlockSpec(memory_space=pltpu.SMEM),
             pl.BlockSpec((ATT_BLOCK, D_MODEL), lambda n: (n, 0)),
             pl.BlockSpec((ATT_BLOCK, 256), lambda n: (prev(n), 4)), pl.BlockSpec((ATT_BLOCK, 256), lambda n: (n, 4)),
             pl.BlockSpec((ATT_BLOCK, 256), lambda n: (prev(n), 5)), pl.BlockSpec((ATT_BLOCK, 256), lambda n: (n, 5))]
    if with_do:
        specs.append(pl.BlockSpec((ATT_BLOCK, D_MODEL), lambda n: (n, 0)))
    return specs


def _attn_fwd_call(qkv, sinks, hosted=None):
    t = qkv.shape[0]

    def body(ins, outs, scratch, first, last):
        sink_ref, q_ref, kp_ref, kc_ref, vp_ref, vc_ref = ins
        o_ref, bias_ref = outs[0], scratch[0]
        n = pl.program_id(0)
        first()

        @pl.when(n == 0)
        def _():
            _attn_fill_tables(bias_ref)

        first = 1 - jnp.minimum(n, 1)
        kband = jnp.concatenate([kp_ref[...], kc_ref[...]], axis=0)
        vband = jnp.concatenate([vp_ref[...], vc_ref[...]], axis=0)
        for k in range(N_KV_HEADS):
            k_even, k_odd = _kv_even_odd(kband, k)
            v_even, v_odd = _kv_even_odd(vband, k)
            qs = _stack_pairs(q_ref, k) * ATT_SCALE
            p, rdenom, _ = _attn_probs(qs, k_even, k_odd, bias_ref[first, k], k, sink_ref)
            probs = (p * rdenom).astype(MXU_DTYPE)
            o = (jnp.dot(probs[:256], v_even, preferred_element_type=F32)
                 + jnp.dot(probs[256:], v_odd, preferred_element_type=F32))
            o_ref[:, 256 * k:256 * k + 128] = o[:128].astype(o_ref.dtype)
            o_ref[:, 256 * k + 128:256 * k + 256] = o[128:].astype(o_ref.dtype)
        last()

    return _call(
        body, hosted, name="attn_fwd", grid=(t // ATT_BLOCK,), in_specs=_attn_specs(False),
        out_specs=[pl.BlockSpec((ATT_BLOCK, D_MODEL), lambda n: (n, 0))],
        out_shape=[jax.ShapeDtypeStruct((t, D_MODEL), MXU_DTYPE)], operands=(sinks, qkv, qkv, qkv, qkv, qkv),
        semantics=("arbitrary",), scratch_shapes=_ATTN_TABLES)


def _attn_bwd_call(qkv, sinks, do, hosted=None):
    t = qkv.shape[0]
    nb = t // ATT_BLOCK
    tn_dims = (((0,), (0,)), ((), ()))
    nt_dims = (((1,), (1,)), ((), ()))

    def to_native(even_part, odd_part, parity):
        lane = lax.broadcasted_iota(jnp.int32, even_part.shape, 1)
        lo = lane < HEAD_DIM
        e, o = jnp.where(lo, even_part, 0.0), jnp.where(lo, 0.0, odd_part)
        if parity == 0:
            return e + pltpu.roll(o, HEAD_DIM, axis=1)
        return pltpu.roll(e, HEAD_DIM, axis=1) + o

    def body(ins, outs, scratch, first, last):
        sink_ref, q_ref, kp_ref, kc_ref, vp_ref, vc_ref, do_ref = ins
        dq_ref, dk_ref, dv_ref, db_ref, dsink_ref = outs
        bias_ref = scratch[0]
        n = pl.program_id(0)
        first()

        @pl.when(n == 0)
        def _():
            _attn_fill_tables(bias_ref)
            dk_ref[...] = jnp.zeros_like(dk_ref)
            dv_ref[...] = jnp.zeros_like(dv_ref)
            db_ref[...] = jnp.zeros_like(db_ref)
            dsink_ref[...] = jnp.zeros_like(dsink_ref)

        kband = jnp.concatenate([kp_ref[...], kc_ref[...]], axis=0)
        vband = jnp.concatenate([vp_ref[...], vc_ref[...]], axis=0)
        dk_pairs = [jnp.zeros((2 * ATT_BLOCK, 128), F32), jnp.zeros((2 * ATT_BLOCK, 128), F32)]
        dv_pairs = [jnp.zeros((2 * ATT_BLOCK, 128), F32), jnp.zeros((2 * ATT_BLOCK, 128), F32)]
        sink_lane = lax.broadcasted_iota(jnp.int32, (8, 128), 1)
        sink_row = lax.broadcasted_iota(jnp.int32, (8, 128), 0)
        dsink = jnp.zeros((8, 128), F32)
        first = 1 - jnp.minimum(n, 1)
        for k in range(N_KV_HEADS):
            k_even, k_odd = _kv_even_odd(kband, k)
            v_even, v_odd = _kv_even_odd(vband, k)
            qs = _stack_pairs(q_ref, k) * ATT_SCALE
            dos = _stack_pairs(do_ref, k)
            p, rdenom, es = _attn_probs(qs, k_even, k_odd, bias_ref[first, k], k, sink_ref)
            probs = p * rdenom
            dp = jnp.concatenate([lax.dot_general(dos, v_even, nt_dims, preferred_element_type=F32),
                                  lax.dot_general(dos, v_odd, nt_dims, preferred_element_type=F32)], axis=0)
            delta = jnp.sum(probs * dp, axis=-1, keepdims=True)
            ds16 = (probs * (dp - delta)).astype(MXU_DTYPE)
            p16 = probs.astype(MXU_DTYPE)
            dsink_rows = -(es * rdenom) * delta
            for r, gq in enumerate(_GROUP_ORDER):
                tot = jnp.sum(dsink_rows[ATT_BLOCK * r:ATT_BLOCK * (r + 1)], axis=0, keepdims=True)
                dsink = dsink + jnp.where((sink_lane == 4 * k + gq) & (sink_row == 0), tot, 0.0)
            dqs = (jnp.dot(ds16[:256], k_even, preferred_element_type=F32)
                   + jnp.dot(ds16[256:], k_odd, preferred_element_type=F32)) * ATT_SCALE
            dq_ref[:, 256 * k:256 * k + 128] = dqs[:128].astype(dq_ref.dtype)
            dq_ref[:, 256 * k + 128:256 * k + 256] = dqs[128:].astype(dq_ref.dtype)
            db_ref[:, 256 * k:256 * k + 128] += jnp.sum(dqs[:128], axis=0, keepdims=True)
            db_ref[:, 256 * k + 128:256 * k + 256] += jnp.sum(dqs[128:], axis=0, keepdims=True)
            pair, parity = divmod(k, 2)
            dk_pairs[pair] = dk_pairs[pair] + to_native(
                lax.dot_general(ds16[:256], qs, tn_dims, preferred_element_type=F32),
                lax.dot_general(ds16[256:], qs, tn_dims, preferred_element_type=F32), parity)
            dv_pairs[pair] = dv_pairs[pair] + to_native(
                lax.dot_general(p16[:256], dos, tn_dims, preferred_element_type=F32),
                lax.dot_general(p16[256:], dos, tn_dims, preferred_element_type=F32), parity)
        dsink_ref[...] += dsink
        dk_band = jnp.concatenate(dk_pairs, axis=1)
        dv_band = jnp.concatenate(dv_pairs, axis=1)
        cur = pl.multiple_of(n * ATT_BLOCK, ATT_BLOCK)
        dk_ref[pl.ds(cur, ATT_BLOCK), :] += dk_band[ATT_BLOCK:]
        dv_ref[pl.ds(cur, ATT_BLOCK), :] += dv_band[ATT_BLOCK:]

        @pl.when(n > 0)
        def _():
            prv = pl.multiple_of((n - 1) * ATT_BLOCK, ATT_BLOCK)
            dk_ref[pl.ds(prv, ATT_BLOCK), :] += dk_band[:ATT_BLOCK]
            dv_ref[pl.ds(prv, ATT_BLOCK), :] += dv_band[:ATT_BLOCK]

        @pl.when(n == nb - 1)
        def _():
            db_ref[:, 1024:1280] = jnp.sum(dk_ref[...], axis=0, keepdims=True)
            db_ref[:, 1280:1536] = jnp.sum(dv_ref[...], axis=0, keepdims=True)

        last()

    whole = lambda shape: pl.BlockSpec(shape, lambda n: (0, 0))
    return _call(
        body, hosted, name="attn_bwd", grid=(nb,), in_specs=_attn_specs(True),
        out_specs=[pl.BlockSpec((ATT_BLOCK, D_MODEL), lambda n: (n, 0)), whole((t, 256)), whole((t, 256)),
                   whole((1, QKV_DIM)), whole((8, 128))],
        out_shape=[jax.ShapeDtypeStruct((t, D_MODEL), MXU_DTYPE), jax.ShapeDtypeStruct((t, 256), F32),
                   jax.ShapeDtypeStruct((t, 256), F32), jax.ShapeDtypeStruct((1, QKV_DIM), F32),
                   jax.ShapeDtypeStruct((8, 128), F32)],
        operands=(sinks, qkv, qkv, qkv, qkv, qkv, do), semantics=("arbitrary",), scratch_shapes=_ATTN_TABLES)


FF_TILE = 1408


def _gathering_tile(j):
    me = 2 * lax.axis_index("x") + lax.axis_index("y")
    return me ^ (((j & 1) << 1) | (j >> 1))


def _ffn_in_gathering_call(hn, w_in, layer, hosted):
    t = hn.shape[0]
    tm = _tile(t, 1024)
    n_i = t // tm
    halves = D_FF // FF_TILE
    assert FF_TILE == W_IN_COLS and 2 * halves == N_CHIPS

    def body(ins, outs, scratch, first, last):
        hn_ref = ins[0]
        u_ref, w_ref = outs
        wbuf, load_sem, ici_send, ici_recv, hand_send, hand_recv = scratch
        j, i = pl.program_id(0), pl.program_id(1)
        x, y, c, chips = _mesh_place()
        me = 2 * x + y
        mover = c == layer

        def shard(k):
            return w_ref.at[layer, :, _col(k, W_IN_COLS)]

        def over_ici(jj, k):
            cx, cy = chips[jj]
            return pltpu.make_async_remote_copy(src_ref=shard(k), dst_ref=shard(k), send_sem=ici_send.at[jj],
                                                recv_sem=ici_recv.at[jj], device_id=(cx, cy, c), device_id_type=MESH)

        def handed(jj):
            cx, cy = chips[jj]
            return pltpu.make_async_remote_copy(
                src_ref=shard(2 * cx + cy), dst_ref=shard(2 * cx + cy), send_sem=hand_send.at[jj],
                recv_sem=hand_recv.at[jj], device_id=(x, y, 1 - c), device_id_type=MESH)

        far = N_CHIPS - 2

        @pl.when((j == 0) & (i == 0) & mover)
        def _():
            over_ici(far, me).start()
            for jj in NEIGHBOURS:
                handed(jj).start()

        first()

        def load(step):
            return pltpu.make_async_copy(shard(_gathering_tile(step)), wbuf.at[step % 2], load_sem.at[step % 2])

        @pl.when((j == 0) & (i == 0))
        def _():
            load(0).start()

        for jj in range(N_CHIPS - 1):
            @pl.when((j == jj) & (i == n_i - 1))
            def _(jj=jj):
                if jj == far:
                    @pl.when(mover)
                    def _():
                        over_ici(far, 2 * chips[far][0] + chips[far][1]).wait_recv()
                        handed(far).start()

                @pl.when(jnp.logical_not(mover))
                def _():
                    handed(jj).wait_recv()

                load(jj + 1).start()

        @pl.when(i == 0)
        def _():
            load(j).wait()

        u_ref[...] = jnp.dot(hn_ref[...].astype(MXU_DTYPE), wbuf[j % 2].astype(MXU_DTYPE),
                             preferred_element_type=F32).astype(u_ref.dtype)

        @pl.when((j == N_CHIPS - 1) & (i == n_i - 1) & mover)
        def _():
            over_ici(far, me).wait_send()
            for jj in range(N_CHIPS - 1):
                handed(jj).wait_send()

        last()

    def u_index(j, i):
        tile = _gathering_tile(j)
        return tile // halves, i, tile % halves

    three = [pltpu.SemaphoreType.DMA((N_CHIPS - 1,))] * 4
    return _call(
        body, hosted, name=f"ffn{layer}_in", grid=(N_CHIPS, n_i),
        in_specs=[pl.BlockSpec((tm, D_MODEL), lambda j, i: (i, 0)), ANY],
        out_specs=[pl.BlockSpec((None, tm, FF_TILE), u_index), ANY],
        out_shape=[jax.ShapeDtypeStruct((2, t, D_FF), U_DTYPE), jax.ShapeDtypeStruct(w_in.shape, w_in.dtype)],
        operands=(hn, w_in), semantics=("arbitrary", "arbitrary"), aliases={1: 1},
        scratch_shapes=[pltpu.VMEM((2, D_MODEL, W_IN_COLS), w_in.dtype), pltpu.SemaphoreType.DMA((2,))] + three)


def _ffn_out_loss_call(z, w_out, layer, h, g, target):
    t, f = z.shape
    tm = _tile(t, ROW_TILE)

    def body(z_ref, w_ref, h_ref, g_ref, t_ref, loss_ref, dh_ref, dh16_ref, dg_ref):
        x = h_ref[...] + jnp.dot(z_ref[...], w_ref[...], preferred_element_type=F32)
        gv = g_ref[...]
        err = _rms(x, gv) - t_ref[...]
        dx, dgt = _rms_bwd(x, gv, err * (1.0 / D_MODEL))
        dh_ref[...] = dx
        dh16_ref[...] = dx.astype(dh16_ref.dtype)

        @pl.when(pl.program_id(0) == 0)
        def _():
            dg_ref[...] = jnp.zeros_like(dg_ref)
            loss_ref[...] = jnp.zeros_like(loss_ref)

        dg_ref[...] += jnp.sum(dgt, axis=0, keepdims=True)
        per_token = jnp.mean(err * err, axis=-1, keepdims=True)
        loss_ref[...] += 0.5 * jnp.sum(per_token, axis=0, keepdims=True)

    row = pl.BlockSpec((tm, D_MODEL), lambda i: (i, 0))
    vec = pl.BlockSpec((1, D_MODEL), lambda i: (0, 0))
    return pl.pallas_call(
        body, name=f"ffn{layer}_out_loss", grid=(t // tm,),
        in_specs=[pl.BlockSpec((tm, f), lambda i: (i, 0)), pl.BlockSpec((None, f, D_MODEL), lambda i: (layer, 0, 0)),
                  row, vec, row],
        out_specs=[pl.BlockSpec((1, 1), lambda i: (0, 0)), row, row, vec],
        out_shape=[jax.ShapeDtypeStruct((1, 1), F32), jax.ShapeDtypeStruct((t, D_MODEL), F32),
                   jax.ShapeDtypeStruct((t, D_MODEL), MXU_DTYPE), jax.ShapeDtypeStruct((1, D_MODEL), F32)],
        compiler_params=_params("arbitrary"))(z, w_out, h, g, target)


def _ffn_fwd(hn, h, w, conv_w, conv_b, layer, hooks=None, next_norm=None, loss_head=None, gathers_w_in=False):
    t = hn.shape[0]
    tm = _tile(t, 1024)
    halves = D_FF // FF_TILE
    if gathers_w_in:
        u, w["w_in"] = _ride(hooks, f"in{layer}", lambda hosted: _ffn_in_gathering_call(hn, w["w_in"], layer, hosted))
    else:
        u, = _ride(hooks, f"in{layer}", lambda hosted: _matmul(
            hn, w["w_in"], mode="nn", name=f"ffn{layer}_in", grid=(2 * halves, t // tm, 1),
            a_spec=pl.BlockSpec((tm, D_MODEL), lambda j, i, k: (i, 0)),
            b_spec=pl.BlockSpec((None, D_MODEL, FF_TILE), lambda j, i, k: (layer, 0, j)),
            o_spec=pl.BlockSpec((None, tm, FF_TILE), lambda j, i, k: (j // halves, i, j % halves)),
            out_shape=jax.ShapeDtypeStruct((2, t, D_FF), U_DTYPE), hosted=hosted))
    z, = _ride(hooks, f"glu{layer}", lambda hosted: _conv_glu_fwd_call(u, conv_w, conv_b, f"ffn{layer}_glu",
                                                                       hosted=hosted))
    if loss_head is not None:
        return (*_ffn_out_loss_call(z, w["w_out"], layer, h, *loss_head), u, z)
    outs = _ride(hooks, f"out{layer}", lambda hosted: _mm_nn(z, w["w_out"], f"ffn{layer}_out", b_lead=layer, residual=h,
                                                     tk=D_FF, hosted=hosted, norm_g=next_norm))
    return outs[0], (outs[1] if next_norm is not None else None), u, z


def _ffn_bwd(dh, hn, u, z, w_in, w_out, conv_w, conv_b, layer, g_in_acc, g_out_acc, norm, hooks=None):
    t = hn.shape[0]
    dz, = _ride(hooks, "dz", lambda hosted: _mm_nt(dh, w_out, f"ffn{layer}_dz", b_lead=layer, out_dtype=MXU_DTYPE,
                                                   tn=FF_TILE, hosted=hosted))
    du, dcw, dcb = _ride(hooks, "dglu", lambda hosted: _conv_glu_bwd_call(u, dz, conv_w, conv_b, f"ffn{layer}_dglu",
                                                                          hosted=hosted))
    halves = D_FF // FF_TILE
    rows = D_MODEL // 2
    g_in, = _ride(hooks, "dwin", lambda hosted: _matmul(
        hn, du, mode="tn", name=f"ffn{layer}_dwin", grid=(2 * halves, D_MODEL // rows, 1),
        a_spec=pl.BlockSpec((t, rows), lambda j, i, k: (0, i)),
        b_spec=pl.BlockSpec((None, t, FF_TILE), lambda j, i, k: (j // halves, 0, j % halves)),
        o_spec=pl.BlockSpec((None, rows, FF_TILE), lambda j, i, k: (layer, i, j)),
        out_shape=jax.ShapeDtypeStruct((2, D_MODEL, 2 * D_FF), WIRE_DTYPE), into=g_in_acc, hosted=hosted),
        {"dcw": dcw, "dcb": dcb})
    g_out, = _ride(hooks, "dwout", lambda hosted: _mm_tn(
        z, dh, f"ffn{layer}_dwout", lead=layer, n_lead=2, into=g_out_acc, tm=WGRAD_TILE, tk=t, out_dtype=WIRE_DTYPE,
        hosted=hosted), {"g_in": g_in})
    dx, dx16, dg = _ride(hooks, "dhn", lambda hosted: _ffn_dhn_call(du, w_in, layer, norm, hosted),
                         {"g_in": g_in, "g_out": g_out})
    return dx, dx16, dg, g_in, g_out, dcw, dcb


WGRAD_TILE = 256


_NT_DIMS = (((1,), (1,)), ((), ()))


def _dhn_call(name, a, a_spec, w, w_spec, product, t, tm, norm, hosted=None):
    h, g, dres = norm

    def body(ins, outs, scratch, first, last):
        a_ref, w_ref, h_ref, g_ref, dres_ref = ins
        dx_ref, dx16_ref, dg_ref = outs
        first()
        dx, dgt = _rms_bwd(h_ref[...], g_ref[...], product(a_ref, w_ref))
        dx = dres_ref[...] + dx
        dx_ref[...] = dx
        dx16_ref[...] = dx.astype(dx16_ref.dtype)

        @pl.when(pl.program_id(0) == 0)
        def _():
            dg_ref[...] = jnp.zeros_like(dg_ref)

        dg_ref[...] += jnp.sum(dgt, axis=0, keepdims=True)
        last()

    row = pl.BlockSpec((tm, D_MODEL), lambda i: (i, 0))
    vec = pl.BlockSpec((1, D_MODEL), lambda i: (0, 0))
    return _call(
        body, hosted, name=name, grid=(t // tm,), in_specs=[a_spec, w_spec, row, vec, row], out_specs=[row, row, vec],
        out_shape=[jax.ShapeDtypeStruct((t, D_MODEL), F32), jax.ShapeDtypeStruct((t, D_MODEL), MXU_DTYPE),
                   jax.ShapeDtypeStruct((1, D_MODEL), F32)],
        operands=(a, w, h, g, dres), semantics=("arbitrary",))


def _ffn_dhn_call(du, w_in, layer, norm, hosted=None):
    _, t, f = du.shape
    tm = _tile(t, 512)

    def product(du_ref, w_ref):
        return (lax.dot_general(du_ref[0], w_ref[:, :f], _NT_DIMS, preferred_element_type=F32)
                + lax.dot_general(du_ref[1], w_ref[:, f:], _NT_DIMS, preferred_element_type=F32))

    return _dhn_call(f"ffn{layer}_dhn", du, pl.BlockSpec((2, tm, f), lambda i: (0, i, 0)), w_in,
                     pl.BlockSpec((None, D_MODEL, 2 * f), lambda i: (layer, 0, 0)), product, t, tm, norm, hosted)


def _attn_dhn_call(dqkv, w_qkv, norm):
    t, n = dqkv.shape
    tm = _tile(t, 1024)

    def product(a_ref, w_ref):
        return lax.dot_general(a_ref[...], w_ref[...], _NT_DIMS, preferred_element_type=F32)

    return _dhn_call("attn_dhn", dqkv, pl.BlockSpec((tm, n), lambda i: (i, 0)), w_qkv,
                     pl.BlockSpec((D_MODEL, n), lambda i: (0, 0)), product, t, tm, norm)


def _local_step(x, target, w, *, norm_mix, norm_ffn, norm_f, pool_b, pool_scale, b_qkv, sinks, conv_w, conv_b,
                late_gather=None, reducer=None):
    cw = [conv_w[l].reshape(3, 2, D_FF) for l in range(2)]
    cb = [conv_b[l].reshape(2, D_FF) for l in range(2)]
    gm = [norm_mix[l:l + 1] for l in range(2)]
    gf = [norm_ffn[l:l + 1] for l in range(2)]

    w = dict(w)
    fwd_hooks = {}
    if late_gather:
        def gather_hook(stage):
            return (lambda env: late_gather(stage, w)), (lambda got: w.update(zip(_late_names(stage), got)))

        fwd_hooks = {stage: gather_hook(stage) for stage in GATHER_RIDES}

    h1, hn_f0 = _ride(fwd_hooks, "pool_fwd", lambda hosted: _pool_fwd_call(x, gm[0], w["pool_w"], pool_b, pool_scale,
                                                                            gf[0], hosted))
    h2, hn_a, u0, z0 = _ffn_fwd(hn_f0, h1, w, cw[0], cb[0], 0, hooks=fwd_hooks, next_norm=gm[1],
                                gathers_w_in=bool(late_gather))
    qkv = _mm_nn(hn_a, w["w_qkv"], "attn_qkv", bias=b_qkv, out_dtype=MXU_DTYPE, tn=QKV_DIM)
    o, = _ride(fwd_hooks, "attn_fwd", lambda hosted: _attn_fwd_call(qkv, sinks, hosted))
    h3, hn_f1 = _ride(fwd_hooks, "attn_out", lambda hosted: _mm_nn(o, w["w_o"], "attn_out", residual=h2, hosted=hosted,
                                                                   norm_g=gf[1]))
    loss, dh, dh16, d_norm_f, u1, z1 = _ffn_fwd(hn_f1, h3, w, cw[1], cb[1], 1, hooks=fwd_hooks,
                                                loss_head=(norm_f, target))

    rd = reducer
    first, att, w_in0, w_out0 = _Reducer.FIRST, _Reducer.ATT, _Reducer.W_IN0, _Reducer.W_OUT0
    hooks1 = hooks_attn = hooks0 = hooks_pool = None
    if rd:
        hooks1 = {"dhn": (lambda env: rd.swap(first, {"w_in": env["g_in"], "w_out": env["g_out"]}),
                          lambda got: rd.pair_sums(first, got))}
        hooks_attn = {"attn_bwd": (lambda env: rd.scatter(first), lambda got: rd.chip_sums(first, got))}

        def early_pieces(env):
            return {"b_qkv": d_bqkv, "sinks": d_sinks[0:1], "norm_mix1": d_gm1, "norm_ffn1": d_gf1,
                    "conv_w": jnp.stack([env["dcw"].reshape(3, 2 * D_FF), dcw1.reshape(3, 2 * D_FF)]),
                    "conv_b": jnp.stack([env["dcb"].reshape(2 * D_FF), dcb1.reshape(2 * D_FF)]),
                    "norm_f": d_norm_f, "loss": loss}

        def after_dz(got):
            rd.shared(first, got[:len(first)])
            rd.pair_sums(att, got[len(first):])

        def after_dwin(got):
            rd.shared(att, got[:len(att)])
            rd.small_early_done(got[-1])

        def after_dhn(got):
            rd.chip_sums(w_in0, got[:1])
            rd.pair_sums(w_out0, got[1:])

        def after_pool_bwd(got):
            rd.shared(w_in0, got[:1])
            rd.chip_sums(w_out0, got[1:])

        hooks0 = {
            "dz": (lambda env: _both(rd.share(first), rd.swap(att, {"w_qkv": g_qkv, "w_o": g_o4})), after_dz),
            "dglu": (lambda env: rd.scatter(att), lambda got: rd.chip_sums(att, got)),
            "dwin": (lambda env: _both(rd.share(att), rd.small_early(early_pieces(env))), after_dwin),
            "dwout": (lambda env: rd.swap(w_in0, {"w_in": env["g_in"]}), lambda got: rd.pair_sums(w_in0, got)),
            "dhn": (lambda env: _both(rd.scatter(w_in0), rd.swap(w_out0, {"w_out": env["g_out"]})), after_dhn),
        }
        hooks_pool = {"pool_bwd": (lambda env: _both(rd.share(w_in0), rd.scatter(w_out0)), after_pool_bwd)}
    dh, dh16, d_gf1, g_in, g_out, dcw1, dcb1 = _ffn_bwd(
        dh16, hn_f1, u1, z1, w["w_in"], w["w_out"], cw[1], cb[1], 1, None, None, (h3, gf[1], dh), hooks=hooks1)
    do = _mm_nt(dh16, w["w_o"], "attn_do", out_dtype=MXU_DTYPE)
    g_o = _mm_tn(o, dh16, "attn_dwo", tn=WGRAD_TILE, tk=x.shape[0], out_dtype=WIRE_DTYPE)
    g_o4 = g_o.reshape(N_CHIPS, 2, W_O_ROWS // 2, D_MODEL)
    dq, dk, dv, d_bqkv, d_sinks = _ride(hooks_attn, "attn_bwd", lambda hosted: _attn_bwd_call(qkv, sinks, do, hosted))
    dqkv = jnp.concatenate([dq, dk.astype(MXU_DTYPE), dv.astype(MXU_DTYPE)], axis=1)
    g_qkv = _mm_tn(hn_a, dqkv, "attn_dwqkv", tn=WGRAD_TILE, tk=x.shape[0], out_dtype=WIRE_DTYPE)
    dh, dh16, d_gm1 = _attn_dhn_call(dqkv, w["w_qkv"], (h2, gm[1], dh))
    dh, _, d_gf0, g_in, g_out, dcw0, dcb0 = _ffn_bwd(
        dh16, hn_f0, u0, z0, w["w_in"], w["w_out"], cw[0], cb[0], 0, g_in, g_out, (h1, gf[0], dh), hooks=hooks0)
    grad_x, g_pool, dvec = _ride(hooks_pool, "pool_bwd", lambda hosted: _pool_bwd_call(
        x, dh, gm[0], w["pool_w"], pool_b, pool_scale, hosted))

    grads = {
        "w_in": g_in, "w_out": g_out, "w_qkv": g_qkv, "w_o": g_o, "pool_w": g_pool,
        "pool_b": dvec[0:1], "pool_scale": dvec[1:2], "b_qkv": d_bqkv, "sinks": d_sinks[0:1, :N_HEADS],
        "norm_mix": jnp.concatenate([dvec[2:3], d_gm1], axis=0), "norm_ffn": jnp.concatenate([d_gf0, d_gf1], axis=0),
        "conv_w": jnp.stack([dcw0.reshape(3, 2 * D_FF), dcw1.reshape(3, 2 * D_FF)]),
        "conv_b": jnp.stack([dcb0.reshape(2 * D_FF), dcb1.reshape(2 * D_FF)]), "norm_f": d_norm_f,
    }
    return loss, grad_x, grads


W_IN_COLS = 2 * D_FF // N_CHIPS
W_OUT_ROWS = D_FF // N_CHIPS
QKV_COLS = QKV_DIM // N_CHIPS
W_O_ROWS = D_MODEL // N_CHIPS
POOL_ROWS = POOL_GC // N_CHIPS
BIG = ("w_in", "w_out", "w_qkv", "w_o", "pool_w")


def _mesh_place():
    x, y, c = lax.axis_index("x"), lax.axis_index("y"), lax.axis_index("c")
    chips = [(1 - x, y), (x, 1 - y), (1 - x, 1 - y)]
    return x, y, c, chips


def _col(k, width):
    return pl.ds(pl.multiple_of(k * width, 128), width)


def _row(k, height):
    return pl.ds(pl.multiple_of(k * height, 16), height)


def _full_piece(name, ref, k, h):
    if name == "w_in":
        return ref.at[h, :, _col(k, W_IN_COLS)]
    if name == "w_out":
        return ref.at[h, _row(k, W_OUT_ROWS), :]
    if name == "w_qkv":
        return ref.at[_row(h, D_MODEL // 2), _col(k, QKV_COLS)]
    if name == "w_o":
        return ref.at[_row(2 * k + h, W_O_ROWS // 2), :]
    return ref.at[pl.ds(2 * h, 2), _row(k, POOL_ROWS), :]


def _shard_half(name, ref, h):
    if name in ("w_in", "w_out"):
        return ref.at[h]
    if name == "w_qkv":
        return ref.at[_row(h, D_MODEL // 2), :]
    if name == "w_o":
        return ref.at[_row(h, W_O_ROWS // 2), :]
    return ref.at[pl.ds(2 * h, 2)]


FULL_SHAPES = {"w_in": (2, D_MODEL, 2 * D_FF), "w_out": (2, D_FF, D_MODEL), "w_qkv": (D_MODEL, QKV_DIM),
               "w_o": (D_MODEL, D_MODEL), "pool_w": (4, POOL_GC, POOL_GC)}
SHARD_SHAPES = {"w_in": (2, D_MODEL, W_IN_COLS), "w_out": (2, W_OUT_ROWS, D_MODEL), "w_qkv": (D_MODEL, QKV_COLS),
                "w_o": (W_O_ROWS, D_MODEL), "pool_w": (4, POOL_ROWS, POOL_GC)}
HALF_SHAPES = {"w_in": (D_MODEL, W_IN_COLS), "w_out": (W_OUT_ROWS, D_MODEL), "w_qkv": (D_MODEL // 2, QKV_COLS),
               "w_o": (W_O_ROWS // 2, D_MODEL), "pool_w": (2, POOL_ROWS, POOL_GC)}


def _place_shard(name, shard, me_arr):
    if name == "w_in":
        blk, grid = (1, 256, W_IN_COLS), (2, D_MODEL // 256)
        src, dst = (lambda a, b, me: (a, b, 0)), (lambda a, b, me: (a, b, me[0]))
    elif name == "w_out":
        blk, grid = (1, W_OUT_ROWS, D_MODEL), (2, 1)
        src, dst = (lambda a, b, me: (a, 0, 0)), (lambda a, b, me: (a, me[0], 0))
    elif name == "w_qkv":
        blk, grid = (256, QKV_COLS), (1, D_MODEL // 256)
        src, dst = (lambda a, b, me: (b, 0)), (lambda a, b, me: (b, me[0]))
    elif name == "w_o":
        blk, grid = (W_O_ROWS, D_MODEL), (1, 1)
        src, dst = (lambda a, b, me: (0, 0)), (lambda a, b, me: (me[0], 0))
    else:
        blk, grid = (4, POOL_ROWS, POOL_GC), (1, 1)
        src, dst = (lambda a, b, me: (0, 0, 0)), (lambda a, b, me: (0, me[0], 0))

    def body(me_ref, s_ref, o_ref):
        o_ref[...] = s_ref[...].astype(o_ref.dtype)

    return pl.pallas_call(
        body, name=f"place_{name}",
        grid_spec=pltpu.PrefetchScalarGridSpec(num_scalar_prefetch=1, grid=grid, in_specs=[pl.BlockSpec(blk, src)],
                                               out_specs=pl.BlockSpec(blk, dst)),
        out_shape=jax.ShapeDtypeStruct(FULL_SHAPES[name], WIRE_DTYPE), compiler_params=_params("parallel", "parallel"),
    )(me_arr, shard)


def _when_mover(half, c, fn):
    if half is None:
        fn(c)
    else:
        pl.when(c == half)(lambda: fn(half))


def _when_taker(half, c, fn):
    if half is None:
        fn(1 - c)
    else:
        pl.when(c != half)(lambda: fn(half))


def _remote(ref, sems, t, j, dev):
    send, recv = sems
    return pltpu.make_async_remote_copy(src_ref=ref, dst_ref=ref, send_sem=send.at[t, j], recv_sem=recv.at[t, j],
                                        device_id=dev, device_id_type=MESH)


def _gather_ici(items, refs, sems, finish, peers=(0, 1, 2)):
    x, y, c, chips = _mesh_place()
    me = 2 * x + y
    for t, (nm, half) in enumerate(items):
        def go(h, t=t, nm=nm):
            mine = _full_piece(nm, refs[t], me, h)
            for j, (cx, cy) in enumerate(chips):
                if j not in peers:
                    continue
                if not finish:
                    _remote(mine, sems, t, j, (cx, cy, c)).start()
                else:
                    _remote(_full_piece(nm, refs[t], 2 * cx + cy, h), sems, t, j, (cx, cy, c)).wait_recv()
                    _remote(mine, sems, t, j, (cx, cy, c)).wait_send()

        _when_mover(half, c, go)


def _gather_pass(items, refs, sems, finish):
    x, y, c, chips = _mesh_place()
    sibling = (x, y, 1 - c)
    for t, (nm, half) in enumerate(items):
        def give(h, t=t, nm=nm):
            for j, (cx, cy) in enumerate(chips):
                got = _remote(_full_piece(nm, refs[t], 2 * cx + cy, h), sems, t, j, sibling)
                if not finish:
                    got.start()
                else:
                    got.wait_send()

        def take(h, t=t, nm=nm):
            for j, (cx, cy) in enumerate(chips):
                _remote(_full_piece(nm, refs[t], 2 * cx + cy, h), sems, t, j, sibling).wait_recv()

        _when_mover(half, c, give)
        if finish:
            _when_taker(half, c, take)


def _gather_sems(n_items):
    return [pltpu.SemaphoreType.DMA((n_items, 3)), pltpu.SemaphoreType.DMA((n_items, 3))]


EARLY_ITEMS = (("pool_w", None),)
_LATE_A = (("w_out", 0),)
_LATE_B = (("w_in", 1),)
_LATE_C = (("w_out", 1),)
_LATE_D = (("w_qkv", None), ("w_o", None))
_EARLY_W_IN = (("w_in", 0),)
GATHER_RIDES = {"pool_fwd": ((), _EARLY_W_IN), "in0": ((), _LATE_A), "glu0": (_LATE_A, _LATE_D), "out0": (_LATE_D, ()),
                "attn_fwd": ((), _LATE_B), "attn_out": (_LATE_B, ()), "in1": ((), _LATE_C), "glu1": (_LATE_C, ())}
NEIGHBOURS = (0, 1)
GATHER_PEERS = {"pool_fwd": NEIGHBOURS}


def _names(items):
    return [nm for nm, _ in items]


def _gather_early(full, small):
    items = EARLY_ITEMS
    n = len(items)

    def body(*refs):
        small_in = refs[n]
        outs, small_out = refs[n + 1:2 * n + 1], refs[2 * n + 1]
        scratch = refs[2 * n + 2:]
        ici_sems, pass_sems = scratch[0:2], scratch[2:4]
        local_sem, small_send, small_recv = scratch[4:]
        x, y, c, chips = _mesh_place()
        me = 2 * x + y
        local = pltpu.make_async_copy(small_in, small_out.at[me], local_sem.at[0])
        local.start()
        small_sends = [pltpu.make_async_remote_copy(
            src_ref=small_in, dst_ref=small_out.at[me], send_sem=small_send.at[j], recv_sem=small_recv.at[j],
            device_id=(cx, cy, c), device_id_type=MESH) for j, (cx, cy) in enumerate(chips)]
        for cp in small_sends:
            cp.start()
        _gather_ici(items, outs, ici_sems, finish=False)
        _gather_ici(items, outs, ici_sems, finish=True)
        _gather_pass(items, outs, pass_sems, finish=False)
        _gather_pass(items, outs, pass_sems, finish=True)
        for j, (cx, cy) in enumerate(chips):
            got = small_out.at[2 * cx + cy]
            pltpu.make_async_remote_copy(src_ref=got, dst_ref=got, send_sem=small_send.at[j],
                                         recv_sem=small_recv.at[j], device_id=(cx, cy, c), device_id_type=MESH).wait_recv()
        for cp in small_sends:
            cp.wait_send()
        local.wait()

    arrays = [full[nm] for nm in _names(items)]
    out_shape = [jax.ShapeDtypeStruct(a.shape, a.dtype) for a in arrays]
    out_shape.append(jax.ShapeDtypeStruct((N_CHIPS,) + small.shape, F32))
    res = pl.pallas_call(
        body, name="gather_early", in_specs=[ANY] * (n + 1), out_specs=[ANY] * (n + 1), out_shape=out_shape,
        input_output_aliases={t: t for t in range(n)},
        scratch_shapes=_gather_sems(n) + _gather_sems(n) + [pltpu.SemaphoreType.DMA((1,)), pltpu.SemaphoreType.DMA((3,)),
                                                            pltpu.SemaphoreType.DMA((3,))],
    )(*arrays, small)
    return {**full, **dict(zip(_names(items), res[:n]))}, res[n]


def _late_gather_host(stage, w):
    passing, sending = GATHER_RIDES[stage]
    n_pass = len(passing)

    def run(src, refs, new, sems, finish):
        if passing:
            _gather_pass(passing, refs[:n_pass], sems[:2], finish)
        if sending:
            _gather_ici(sending, refs[n_pass:], sems[-2:], finish, GATHER_PEERS.get(stage, (0, 1, 2)))

    sems = (_gather_sems(n_pass) if passing else []) + (_gather_sems(len(sending)) if sending else [])
    return _Hosted([], [w[nm] for nm in _late_names(stage)], [], sems, functools.partial(run, finish=False),
                   functools.partial(run, finish=True))


def _late_names(stage):
    return [nm for items in GATHER_RIDES[stage] for nm in _names(items)]


def _other_half_src(name, ref, h):
    if name in ("w_in", "w_out"):
        return ref.at[h]
    if name == "w_qkv":
        return ref.at[_row(h, D_MODEL // 2), :]
    if name == "w_o":
        return ref.at[:, pl.ds(h, 1)]
    return ref.at[pl.ds(2 * h, 2)]


SLAB_SHAPES = {"w_in": (D_MODEL, 2 * D_FF), "w_out": (D_FF, D_MODEL), "w_qkv": (D_MODEL // 2, QKV_DIM),
               "w_o": (N_CHIPS, 1, W_O_ROWS // 2, D_MODEL), "pool_w": (2, POOL_GC, POOL_GC)}


def _item_tag(item):
    return item[0] if item[1] is None else f"{item[0]}{item[1]}"


def _reduce_swap(items, grads, slabs, sems, finish):
    x, y, c, _ = _mesh_place()
    send, recv = sems
    for t, (nm, half) in enumerate(items):
        def copy(h, t=t, nm=nm):
            return pltpu.make_async_remote_copy(
                src_ref=_other_half_src(nm, grads[t], h), dst_ref=slabs[t], send_sem=send.at[t], recv_sem=recv.at[t],
                device_id=(x, y, 1 - c), device_id_type=MESH)

        def give(h):
            if finish:
                copy(h).wait_send()
            else:
                copy(h).start()

        if half is None:
            give(1 - c)
            if finish:
                copy(c).wait_recv()
        else:
            pl.when(c != half)(lambda: give(half))
            if finish:
                pl.when(c == half)(lambda: copy(half).wait_recv())


def _reduce_scatter(items, sums, parts, sems, finish):
    _, _, c, chips = _mesh_place()
    send, recv = sems
    for t, (nm, half) in enumerate(items):
        def go(h, t=t, nm=nm):
            for j, (cx, cy) in enumerate(chips):
                cp = pltpu.make_async_remote_copy(
                    src_ref=_slab_piece(nm, sums[t], 2 * cx + cy), dst_ref=parts[t].at[j], send_sem=send.at[t, j],
                    recv_sem=recv.at[t, j], device_id=(cx, cy, c), device_id_type=MESH)
                if finish:
                    cp.wait_recv()
                    cp.wait_send()
                else:
                    cp.start()

        _when_mover(half, c, go)


def _reduce_share(items, shards, sems, finish):
    x, y, c, _ = _mesh_place()
    send, recv = sems
    for t, (nm, half) in enumerate(items):
        def copy(h, t=t, nm=nm):
            part = _shard_half(nm, shards[t], h)
            return pltpu.make_async_remote_copy(src_ref=part, dst_ref=part, send_sem=send.at[t], recv_sem=recv.at[t],
                                                device_id=(x, y, 1 - c), device_id_type=MESH)

        def give(h):
            if finish:
                copy(h).wait_send()
            else:
                copy(h).start()

        _when_mover(half, c, give)
        if finish:
            _when_taker(half, c, lambda h: copy(h).wait_recv())


def _pair_sems(n):
    return [pltpu.SemaphoreType.DMA((n,)), pltpu.SemaphoreType.DMA((n,))]


def _swap_host(items, grads):
    def run(src, arr, new, sems, finish):
        _reduce_swap(items, src, new, sems, finish)

    return _Hosted([grads[nm] for nm, _ in items], [],
                   [jax.ShapeDtypeStruct(SLAB_SHAPES[nm], grads[nm].dtype) for nm, _ in items],
                   _pair_sems(len(items)), functools.partial(run, finish=False), functools.partial(run, finish=True))


def _scatter_host(items, sums):
    def run(src, arr, new, sems, finish):
        _reduce_scatter(items, src, new, sems, finish)

    new = [jax.ShapeDtypeStruct((N_CHIPS - 1,) + HALF_SHAPES[nm], WIRE_DTYPE) for nm, _ in items]
    return _Hosted(list(sums), [], new, _gather_sems(len(items)), functools.partial(run, finish=False),
                   functools.partial(run, finish=True))


def _share_host(items, shards):
    def run(src, arr, new, sems, finish):
        _reduce_share(items, arr, sems, finish)

    return _Hosted([], list(shards), [], _pair_sems(len(items)), functools.partial(run, finish=False),
                   functools.partial(run, finish=True))


def _pair_sum(item, grad, slab, place):
    name, half = item
    shape = SLAB_SHAPES[name]
    hsel = (lambda p: p[1]) if half is None else (lambda p: half)
    if name == "w_in":
        blk = (256, 2 * D_FF)
        grid = (D_MODEL // 256,)
        g_spec = pl.BlockSpec((None,) + blk, lambda i, p: (hsel(p), i, 0))
        s_spec = pl.BlockSpec(blk, lambda i, p: (i, 0))
    elif name == "w_out":
        blk = (W_OUT_ROWS, D_MODEL)
        grid = (N_CHIPS,)
        g_spec = pl.BlockSpec((None,) + blk, lambda i, p: (hsel(p), i, 0))
        s_spec = pl.BlockSpec(blk, lambda i, p: (i, 0))
    elif name == "w_qkv":
        grid = (1,)
        g_spec = pl.BlockSpec(shape, lambda i, p: (hsel(p), 0))
        s_spec = pl.BlockSpec(shape, lambda i, p: (0, 0))
    else:
        grid = (1,)
        g_spec = pl.BlockSpec(shape, lambda i, p: (0, hsel(p), 0, 0))
        s_spec = pl.BlockSpec(shape, lambda i, p: (0, 0, 0, 0))

    def body(p_ref, g_ref, s_ref, o_ref):
        @pl.when(hsel(p_ref) == p_ref[1])
        def _():
            o_ref[...] = (g_ref[...].astype(F32) + s_ref[...].astype(F32)).astype(o_ref.dtype)

    return pl.pallas_call(
        body, name=f"pair_sum_{_item_tag(item)}",
        grid_spec=pltpu.PrefetchScalarGridSpec(num_scalar_prefetch=1, grid=grid, in_specs=[g_spec, s_spec],
                                               out_specs=s_spec),
        out_shape=jax.ShapeDtypeStruct(shape, WIRE_DTYPE), compiler_params=_params("parallel"),
    )(place, grad, slab)


def _slab_piece(name, ref, k):
    if name == "w_in":
        return ref.at[:, _col(k, W_IN_COLS)]
    if name == "w_out":
        return ref.at[_row(k, W_OUT_ROWS), :]
    if name == "w_qkv":
        return ref.at[:, _col(k, QKV_COLS)]
    if name == "w_o":
        return ref.at[k, 0]
    return ref.at[:, _row(k, POOL_ROWS), :]


def _sum_chips(item, slab, parts, place, into=None):
    name, sel = item
    half = HALF_SHAPES[name]
    hsel = (lambda p: p[1]) if sel is None else (lambda p: sel)
    if name == "w_in":
        blk, grid = (256, W_IN_COLS), (D_MODEL // 256,)
        own = pl.BlockSpec(blk, lambda i, p: (i, p[0]))
        got = pl.BlockSpec((N_CHIPS - 1,) + blk, lambda i, p: (0, i, 0))
        out = pl.BlockSpec((None,) + blk, lambda i, p: (hsel(p), i, 0))
    elif name == "w_out":
        grid = (1,)
        own = pl.BlockSpec(half, lambda i, p: (p[0], 0))
        got = pl.BlockSpec((N_CHIPS - 1,) + half, lambda i, p: (0, 0, 0))
        out = pl.BlockSpec((None,) + half, lambda i, p: (hsel(p), 0, 0))
    elif name == "w_qkv":
        grid = (1,)
        own = pl.BlockSpec(half, lambda i, p: (0, p[0]))
        got = pl.BlockSpec((N_CHIPS - 1,) + half, lambda i, p: (0, 0, 0))
        out = pl.BlockSpec(half, lambda i, p: (hsel(p), 0))
    else:
        grid = (1,)
        own = pl.BlockSpec((None, None) + half, lambda i, p: (p[0], 0, 0, 0))
        got = pl.BlockSpec((N_CHIPS - 1,) + half, lambda i, p: (0, 0, 0))
        out = pl.BlockSpec(half, lambda i, p: (hsel(p), 0))

    def body(p_ref, own_ref, got_ref, *rest):
        o_ref = rest[-1]

        @pl.when(hsel(p_ref) == p_ref[1])
        def _():
            acc = own_ref[...].astype(F32)
            for j in range(N_CHIPS - 1):
                acc = acc + got_ref[j].astype(F32)
            o_ref[...] = acc

    in_specs, operands, aliases = [own, got], [place, slab, parts], {}
    if into is not None:
        in_specs.append(ANY)
        operands.append(into)
        aliases = {3: 0}
    return pl.pallas_call(
        body, name=f"sum_chips_{_item_tag(item)}",
        grid_spec=pltpu.PrefetchScalarGridSpec(num_scalar_prefetch=1, grid=grid, in_specs=in_specs, out_specs=out),
        out_shape=jax.ShapeDtypeStruct(SHARD_SHAPES[name], F32), input_output_aliases=aliases,
        compiler_params=_params("parallel"),
    )(*operands)


N_DEV = 8


def _peer(d):
    x, y, c, _ = _mesh_place()
    px, py, pc = x ^ (d >> 2), y ^ ((d >> 1) & 1), c ^ (d & 1)
    return (px, py, pc), 4 * px + 2 * py + pc


def _small_all_reduce_host(vec):
    rows, cols = vec.shape

    def run(src, arr, new, scratch, finish):
        slots, stage, send, recv, local = scratch
        x, y, c, _ = _mesh_place()
        me = 4 * x + 2 * y + c
        mine = pltpu.make_async_copy(src[0], slots.at[me], local.at[0])
        copies = []
        for d in range(1, N_DEV):
            dev, _ = _peer(d)
            copies.append(pltpu.make_async_remote_copy(src_ref=src[0], dst_ref=slots.at[me], send_sem=send.at[d - 1],
                                                       recv_sem=recv.at[d - 1], device_id=dev, device_id_type=MESH))
        if not finish:
            mine.start()
            for cp in copies:
                cp.start()
            return
        for d in range(1, N_DEV):
            _, flat = _peer(d)
            got = slots.at[flat]
            pltpu.make_async_remote_copy(src_ref=got, dst_ref=got, send_sem=send.at[d - 1], recv_sem=recv.at[d - 1],
                                         device_id=(x, y, c), device_id_type=MESH).wait_recv()
        for cp in copies:
            cp.wait_send()
        mine.wait()
        acc = slots[0]
        for d in range(1, N_DEV):
            acc = acc + slots[d]
        stage[...] = acc
        out = pltpu.make_async_copy(stage, new[0], local.at[0])
        out.start()
        out.wait()

    scratch = [pltpu.VMEM((N_DEV, rows, cols), F32), pltpu.VMEM((rows, cols), F32),
               pltpu.SemaphoreType.DMA((N_DEV - 1,)), pltpu.SemaphoreType.DMA((N_DEV - 1,)),
               pltpu.SemaphoreType.DMA((1,))]
    return _Hosted([vec], [], [jax.ShapeDtypeStruct(vec.shape, F32)], scratch, functools.partial(run, finish=False),
                   functools.partial(run, finish=True))


class _Reducer:
    FIRST = (("w_in", 1), ("w_out", 1))
    ATT = (("w_qkv", None), ("w_o", None))
    W_IN0 = (("w_in", 0),)
    W_OUT0 = (("w_out", 0),)

    def __init__(self, place):
        self.place = place
        self.shards = {}
        self.state = {}

    def swap(self, items, grads):
        self.state[items] = {"grads": [grads[nm] for nm, _ in items]}
        return _swap_host(items, grads)

    def pair_sums(self, items, slabs):
        st = self.state[items]
        st["sums"] = [_pair_sum(it, g, s, self.place) for it, g, s in zip(items, st["grads"], slabs)]

    def scatter(self, items):
        return _scatter_host(items, self.state[items]["sums"])

    def chip_sums(self, items, parts):
        for it, s, p in zip(items, self.state[items]["sums"], parts):
            self.shards[it[0]] = _sum_chips(it, s, p, self.place, into=self.shards.get(it[0]))

    def share(self, items):
        return _share_host(items, [self.shards[nm] for nm, _ in items])

    def shared(self, items, updated):
        self.shards.update(dict(zip(_names(items), updated)))

    def small_early(self, pieces):
        return _small_all_reduce_host(_pack_small(pieces, EARLY_SMALL))

    def small_early_done(self, total):
        self.early_small = _unpack_small(total, EARLY_SMALL)


def _all_reduce_small(vec, g_pool, share_items, shards):
    n_share = len(share_items)

    def body(v_ref, gp_ref, *rest):
        o_ref, po_ref = rest[n_share:n_share + 2]
        shard_refs = rest[n_share + 2:2 * n_share + 2]
        (slots, pslots, send_sems, recv_sems, psend, precv, hsend, hrecv, share_send,
         share_recv) = rest[2 * n_share + 2:]
        x, y, c, _ = _mesh_place()
        me = 4 * x + 2 * y + c
        _reduce_share(share_items, shard_refs, (share_send, share_recv), finish=False)

        def pool_piece(chip, h):
            return gp_ref.at[pl.ds(2 * h, 2), pl.ds(pl.multiple_of(chip * POOL_ROWS, POOL_ROWS), POOL_ROWS), :]

        slots[me] = v_ref[...]
        pslots[me] = pool_piece(2 * x + y, c)[...]
        copies = []
        for d in range(1, N_DEV):
            peer = (x ^ (d >> 2), y ^ ((d >> 1) & 1), c ^ (d & 1))
            copies.append(pltpu.make_async_remote_copy(
                src_ref=v_ref, dst_ref=slots.at[me], send_sem=send_sems.at[d - 1], recv_sem=recv_sems.at[d - 1],
                device_id=peer, device_id_type=MESH))
            copies.append(pltpu.make_async_remote_copy(
                src_ref=pool_piece(2 * peer[0] + peer[1], peer[2]), dst_ref=pslots.at[me], send_sem=psend.at[d - 1],
                recv_sem=precv.at[d - 1], device_id=peer, device_id_type=MESH))
        for cp in copies:
            cp.start()
        for d in range(1, N_DEV):
            peer = 4 * (x ^ (d >> 2)) + 2 * (y ^ ((d >> 1) & 1)) + (c ^ (d & 1))
            for buf, ss, rs in ((slots, send_sems, recv_sems), (pslots, psend, precv)):
                got = buf.at[peer]
                pltpu.make_async_remote_copy(src_ref=got, dst_ref=got, send_sem=ss.at[d - 1], recv_sem=rs.at[d - 1],
                                             device_id=(x, y, c), device_id_type=MESH).wait_recv()
        for cp in copies:
            cp.wait_send()
        acc, pacc = slots[0], pslots[0]
        for d in range(1, N_DEV):
            acc, pacc = acc + slots[d], pacc + pslots[d]
        o_ref[...] = acc
        po_ref[pl.ds(2 * c, 2)] = pacc
        mine, theirs = po_ref.at[pl.ds(2 * c, 2)], po_ref.at[pl.ds(2 * (1 - c), 2)]
        give = pltpu.make_async_remote_copy(src_ref=mine, dst_ref=mine, send_sem=hsend.at[0], recv_sem=hrecv.at[0],
                                            device_id=(x, y, 1 - c), device_id_type=MESH)
        give.start()
        pltpu.make_async_remote_copy(src_ref=theirs, dst_ref=theirs, send_sem=hsend.at[0], recv_sem=hrecv.at[0],
                                     device_id=(x, y, 1 - c), device_id_type=MESH).wait_recv()
        give.wait_send()
        _reduce_share(share_items, shard_refs, (share_send, share_recv), finish=True)

    vm = pl.BlockSpec(memory_space=pltpu.VMEM)
    piece = (2, POOL_ROWS, POOL_GC)
    return pl.pallas_call(
        body, name="all_reduce_small", in_specs=[vm, vm] + [ANY] * n_share, out_specs=[vm, vm] + [ANY] * n_share,
        out_shape=[jax.ShapeDtypeStruct(vec.shape, F32), jax.ShapeDtypeStruct(SHARD_SHAPES["pool_w"], F32)]
        + [jax.ShapeDtypeStruct(a.shape, a.dtype) for a in shards],
        input_output_aliases={2 + t: 2 + t for t in range(n_share)},
        scratch_shapes=[pltpu.VMEM((N_DEV,) + vec.shape, F32), pltpu.VMEM((N_DEV,) + piece, F32)]
        + [pltpu.SemaphoreType.DMA((N_DEV - 1,))] * 4 + [pltpu.SemaphoreType.DMA((1,))] * 2 + _pair_sems(n_share),
    )(vec, g_pool, *shards)


def _adamw_call(w, g, m, v, name, copy_grad=False):
    shape = w.shape
    cols = shape[-1]
    rows = w.size // cols
    tr = rows
    for cand in (256, 128, 64, 32, 16, 8):
        if rows > cand and rows % cand == 0 and cand * cols * 4 <= 2 * 1024 * 1024:
            tr = cand
            break

    def body(w_ref, g_ref, m_ref, v_ref, d_ref, nm_ref, nv_ref, *g_out):
        gv = g_ref[...]
        if copy_grad:
            g_out[0][...] = gv
        mn = ADAM_B1 * m_ref[...] + (1.0 - ADAM_B1) * gv
        vn = ADAM_B2 * v_ref[...] + (1.0 - ADAM_B2) * jnp.square(gv)
        m_hat = mn / (1.0 - ADAM_B1 ** ADAM_STEP)
        v_hat = vn / (1.0 - ADAM_B2 ** ADAM_STEP)
        d_ref[...] = -ADAM_LR * (m_hat / (jnp.sqrt(v_hat) + ADAM_EPS) + ADAM_WD * w_ref[...])
        nm_ref[...] = mn
        nv_ref[...] = vn

    spec = pl.BlockSpec((tr, cols), lambda i: (i, 0))
    flat = lambda a: a.reshape(rows, cols)
    n_out = 4 if copy_grad else 3
    outs = pl.pallas_call(
        body, name=name, grid=(rows // tr,), in_specs=[spec] * 4, out_specs=[spec] * n_out,
        out_shape=[jax.ShapeDtypeStruct((rows, cols), F32)] * n_out, compiler_params=_params("parallel"),
    )(flat(w), flat(g), flat(m), flat(v))
    return tuple(o.reshape(shape) for o in outs)


WEIGHTS = ("pool_w", "pool_b", "pool_scale", "attn_w_qkv", "attn_b_qkv", "attn_sinks", "attn_w_o", "norm_mix",
           "norm_ffn", "ffn_w_in", "ffn_conv_w", "ffn_conv_b", "ffn_w_out", "norm_f")
SMALL_GATHER = 128 * 71
EARLY_SMALL = ((("b_qkv", 1536), ("sinks", 128), ("norm_mix1", 1024), ("norm_ffn1", 1024), ("conv_w", 6 * 2 * D_FF),
                ("conv_b", 4 * D_FF), ("norm_f", 1024), ("loss", 128)), 6272)
LATE_SMALL = ((("pool_b", 1024), ("pool_scale", 1024), ("norm_mix0", 1024), ("norm_ffn0", 1024)), 512)


def _pack_small(pieces, layout):
    sizes, cols = layout
    flat = jnp.concatenate([jnp.pad(pieces[nm].reshape(-1), (0, size - pieces[nm].size)) for nm, size in sizes])
    return jnp.pad(flat, (0, 8 * cols - flat.size)).reshape(8, cols)


def _unpack_small(vec, layout):
    flat = vec.reshape(-1)
    out, off = {}, 0
    for nm, size in layout[0]:
        out[nm] = flat[off:off + size]
        off += size
    return out


def kernel(x, pool_w, pool_b, pool_scale, attn_w_qkv, attn_b_qkv, attn_sinks, attn_w_o, norm_mix, norm_ffn, ffn_w_in, ffn_conv_w, ffn_conv_b, ffn_w_out, norm_f, loss_target, m_pool_w, m_pool_b, m_pool_scale, m_attn_w_qkv, m_attn_b_qkv, m_attn_sinks, m_attn_w_o, m_norm_mix, m_norm_ffn, m_ffn_w_in, m_ffn_conv_w, m_ffn_conv_b, m_ffn_w_out, m_norm_f, v_pool_w, v_pool_b, v_pool_scale, v_attn_w_qkv, v_attn_b_qkv, v_attn_sinks, v_attn_w_o, v_norm_mix, v_norm_ffn, v_ffn_w_in, v_ffn_conv_w, v_ffn_conv_b, v_ffn_w_out, v_norm_f):
    weights = dict(pool_w=pool_w, pool_b=pool_b, pool_scale=pool_scale, attn_w_qkv=attn_w_qkv, attn_b_qkv=attn_b_qkv,
                   attn_sinks=attn_sinks, attn_w_o=attn_w_o, norm_mix=norm_mix, norm_ffn=norm_ffn, ffn_w_in=ffn_w_in,
                   ffn_conv_w=ffn_conv_w, ffn_conv_b=ffn_conv_b, ffn_w_out=ffn_w_out, norm_f=norm_f)
    m_in = dict(pool_w=m_pool_w, pool_b=m_pool_b, pool_scale=m_pool_scale, attn_w_qkv=m_attn_w_qkv,
                attn_b_qkv=m_attn_b_qkv, attn_sinks=m_attn_sinks, attn_w_o=m_attn_w_o, norm_mix=m_norm_mix,
                norm_ffn=m_norm_ffn, ffn_w_in=m_ffn_w_in, ffn_conv_w=m_ffn_conv_w, ffn_conv_b=m_ffn_conv_b,
                ffn_w_out=m_ffn_w_out, norm_f=m_norm_f)
    v_in = dict(pool_w=v_pool_w, pool_b=v_pool_b, pool_scale=v_pool_scale, attn_w_qkv=v_attn_w_qkv,
                attn_b_qkv=v_attn_b_qkv, attn_sinks=v_attn_sinks, attn_w_o=v_attn_w_o, norm_mix=v_norm_mix,
                norm_ffn=v_norm_ffn, ffn_w_in=v_ffn_w_in, ffn_conv_w=v_ffn_conv_w, ffn_conv_b=v_ffn_conv_b,
                ffn_w_out=v_ffn_w_out, norm_f=v_norm_f)
    chip = 2 * lax.axis_index("x") + lax.axis_index("y")
    core = lax.axis_index("c")

    place = jnp.stack([chip, core]).astype(jnp.int32)
    shards = {"w_in": ffn_w_in, "w_out": ffn_w_out, "w_qkv": attn_w_qkv[0], "w_o": attn_w_o[0], "pool_w": pool_w[0]}
    placed = {nm: _place_shard(nm, shards[nm], place) for nm in BIG}
    small = jnp.concatenate([pool_b.reshape(-1), attn_b_qkv.reshape(-1), ffn_conv_w.reshape(-1)]).reshape(1, SMALL_GATHER)
    full, small_all = _gather_early(placed, small)
    small_all = small_all.reshape(N_CHIPS, SMALL_GATHER)
    pool_b_full = small_all[:, :256].reshape(N_CHIPS, 4, POOL_ROWS).transpose(1, 0, 2).reshape(1, D_MODEL)
    b_qkv_full = small_all[:, 256:640].reshape(1, QKV_DIM)
    conv_w_full = small_all[:, 640:].reshape(N_CHIPS, 2, 3, W_IN_COLS).transpose(1, 2, 0, 3).reshape(2, 3, 2 * D_FF)

    reducer = _Reducer(place)
    _, grad_x, grads = _local_step(
        x[0], loss_target[0], full, norm_mix=norm_mix, norm_ffn=norm_ffn, norm_f=norm_f.reshape(1, D_MODEL),
        pool_b=pool_b_full, pool_scale=pool_scale, b_qkv=b_qkv_full, sinks=attn_sinks[0], conv_w=conv_w_full,
        conv_b=ffn_conv_b, late_gather=_late_gather_host, reducer=reducer)
    reduced = dict(reducer.shards)

    late = {"pool_b": grads["pool_b"], "pool_scale": grads["pool_scale"], "norm_mix0": grads["norm_mix"][0],
            "norm_ffn0": grads["norm_ffn"][0]}
    last = _Reducer.W_OUT0
    late_sum, reduced["pool_w"], *shared = _all_reduce_small(_pack_small(late, LATE_SMALL), grads["pool_w"], last,
                                                             [reduced[nm] for nm in _names(last)])
    reduced.update(zip(_names(last), shared))
    small_g = {**reducer.early_small, **_unpack_small(late_sum, LATE_SMALL)}
    small_g["norm_mix"] = jnp.concatenate([small_g["norm_mix0"], small_g["norm_mix1"]])
    small_g["norm_ffn"] = jnp.concatenate([small_g["norm_ffn0"], small_g["norm_ffn1"]])
    loss = small_g["loss"][0]
    pool_b_g = lax.dynamic_slice_in_dim(small_g["pool_b"].reshape(4, N_CHIPS, POOL_ROWS), chip, 1, axis=1)
    b_qkv_g = lax.dynamic_slice_in_dim(small_g["b_qkv"].reshape(N_CHIPS, QKV_COLS), chip, 1, axis=0)
    conv_w_g = lax.dynamic_slice_in_dim(small_g["conv_w"].reshape(2, 3, N_CHIPS, W_IN_COLS), chip, 1, axis=2)
    grad_w = {
        "pool_w": reduced["pool_w"].reshape(pool_w.shape), "pool_b": pool_b_g.reshape(pool_b.shape),
        "pool_scale": small_g["pool_scale"].reshape(pool_scale.shape),
        "attn_w_qkv": reduced["w_qkv"].reshape(attn_w_qkv.shape), "attn_b_qkv": b_qkv_g.reshape(attn_b_qkv.shape),
        "attn_sinks": small_g["sinks"][:N_HEADS].reshape(attn_sinks.shape),
        "attn_w_o": reduced["w_o"].reshape(attn_w_o.shape), "norm_mix": small_g["norm_mix"].reshape(norm_mix.shape),
        "norm_ffn": small_g["norm_ffn"].reshape(norm_ffn.shape), "ffn_w_in": reduced["w_in"],
        "ffn_conv_w": conv_w_g.reshape(ffn_conv_w.shape), "ffn_conv_b": small_g["conv_b"].reshape(ffn_conv_b.shape),
        "ffn_w_out": reduced["w_out"], "norm_f": small_g["norm_f"].reshape(norm_f.shape),
    }

    delta, new_m, new_v = {}, {}, {}
    exchanged = ("pool_w", "attn_w_qkv", "attn_w_o", "ffn_w_in", "ffn_w_out")
    for nm in WEIGHTS:
        res = _adamw_call(weights[nm], grad_w[nm], m_in[nm], v_in[nm], f"adamw_{nm}", copy_grad=nm in exchanged)
        delta[nm], new_m[nm], new_v[nm] = res[:3]
        if nm in exchanged:
            grad_w[nm] = res[3]
    return (loss, grad_x.reshape(x.shape), *[grad_w[nm] for nm in WEIGHTS], *[delta[nm] for nm in WEIGHTS],
            *[new_m[nm] for nm in WEIGHTS], *[new_v[nm] for nm in WEIGHTS])
```

```python
import functools
from typing import Callable, NamedTuple

import jax
import jax.numpy as jnp
from jax import lax
from jax.experimental import pallas as pl
from jax.experimental.pallas import tpu as pltpu

F32 = jnp.float32
MXU_DTYPE = jnp.bfloat16
WIRE_DTYPE = jnp.bfloat16
U_DTYPE = jnp.float32

D_MODEL = 1024
D_FF = 2816
QKV_DIM = 1536
HEAD_DIM = 64
N_HEADS = 16
N_KV_HEADS = 4
ATT_BLOCK = 128
POOL_WINDOWS = (2, 4, 8, 16)
POOL_GC = 256
POOL_HALO = 16
CONV_HALO = 8
RMS_EPS = 1e-6
ATT_SCALE = HEAD_DIM ** -0.5
ALIBI_SLOPES = tuple(2.0 ** (-8.0 / N_HEADS * (h + 1)) for h in range(N_HEADS))

ADAM_LR = 0.001
ADAM_B1 = 0.9
ADAM_B2 = 0.999
ADAM_EPS = 1e-08
ADAM_WD = 0.01
ADAM_STEP = 10

N_CHIPS = 4
MESH = pl.DeviceIdType.MESH
VMEM_LIMIT_BYTES = 56 * 1024 * 1024
ANY = pl.BlockSpec(memory_space=pl.ANY)


def _params(*semantics):
    return pltpu.CompilerParams(dimension_semantics=semantics, vmem_limit_bytes=VMEM_LIMIT_BYTES)


def _rms(x, g):
    return x * lax.rsqrt(jnp.mean(x * x, axis=-1, keepdims=True) + RMS_EPS) * g


def _rms_bwd(x, g, dy):
    rstd = lax.rsqrt(jnp.mean(x * x, axis=-1, keepdims=True) + RMS_EPS)
    xhat = x * rstd
    dxhat = dy * g
    dx = rstd * (dxhat - xhat * jnp.mean(dxhat * xhat, axis=-1, keepdims=True))
    return dx, dy * xhat


def _shift_down(x, s):
    return pltpu.roll(x, s, axis=0)


def _shift_up(x, s):
    return pltpu.roll(x, x.shape[0] - s, axis=0)


def _sigmoid(x):
    return 1.0 / (1.0 + jnp.exp(-x))


_DOT_DIMS = {"nn": ((1,), (0,)), "nt": ((1,), (1,)), "tn": ((0,), (0,))}


class _Hosted(NamedTuple):
    sources: list
    arrays: list
    new: list
    sems: list
    first: Callable
    last: Callable


def _call(body, hosted, *, name, grid, in_specs, out_specs, out_shape, operands, semantics, scratch_shapes=(),
          aliases=None):
    n_in, n_out, n_scr = len(in_specs), len(out_specs), len(scratch_shapes)
    aliases = dict(aliases or {})
    if hosted is None:
        def plain(*refs):
            body(refs[:n_in], refs[n_in:n_in + n_out], refs[n_in + n_out:], lambda: None, lambda: None)

        return pl.pallas_call(plain, name=name, grid=grid, in_specs=list(in_specs), out_specs=list(out_specs),
                              out_shape=list(out_shape), scratch_shapes=list(scratch_shapes),
                              input_output_aliases=aliases, compiler_params=_params(*semantics))(*operands)
    ns, na, nn = len(hosted.sources), len(hosted.arrays), len(hosted.new)

    def carrying(*refs):
        ins, src = refs[:n_in], refs[n_in:n_in + ns]
        o0 = n_in + ns + na
        outs, arr = refs[o0:o0 + n_out], refs[o0 + n_out:o0 + n_out + na]
        new = refs[o0 + n_out + na:o0 + n_out + na + nn]
        s0 = o0 + n_out + na + nn
        scratch, sems = refs[s0:s0 + n_scr], refs[s0 + n_scr:]
        ids = [pl.program_id(d) for d in range(len(grid))]
        is_first = functools.reduce(lambda p, q: p & q, [i == 0 for i in ids])
        is_last = functools.reduce(lambda p, q: p & q, [i == n - 1 for i, n in zip(ids, grid)])
        body(ins, outs, scratch, lambda: pl.when(is_first)(lambda: hosted.first(src, arr, new, sems)),
             lambda: pl.when(is_last)(lambda: hosted.last(src, arr, new, sems)))

    for t in range(na):
        aliases[n_in + ns + t] = n_out + t
    return pl.pallas_call(
        carrying, name=name, grid=grid, in_specs=list(in_specs) + [ANY] * (ns + na),
        out_specs=list(out_specs) + [ANY] * (na + nn),
        out_shape=list(out_shape) + [jax.ShapeDtypeStruct(a.shape, a.dtype) for a in hosted.arrays] + list(hosted.new),
        scratch_shapes=list(scratch_shapes) + list(hosted.sems), input_output_aliases=aliases,
        compiler_params=_params(*["arbitrary"] * len(grid)))(*operands, *hosted.sources, *hosted.arrays)


def _both(h1, h2):
    def run(which, src, arr, new, sems):
        cut = lambda seq, n: (seq[:n], seq[n:])
        (s1, s2), (a1, a2) = cut(src, len(h1.sources)), cut(arr, len(h1.arrays))
        (n1, n2), (m1, m2) = cut(new, len(h1.new)), cut(sems, len(h1.sems))
        getattr(h1, which)(s1, a1, n1, m1)
        getattr(h2, which)(s2, a2, n2, m2)

    return _Hosted(h1.sources + h2.sources, h1.arrays + h2.arrays, h1.new + h2.new, h1.sems + h2.sems,
                   functools.partial(run, "first"), functools.partial(run, "last"))


def _ride(hooks, stage, call, env=None):
    if not hooks or stage not in hooks:
        res = call(None)
        return list(res) if isinstance(res, (list, tuple)) else [res]
    make, done = hooks[stage]
    hosted = make(env)
    res = call(hosted)
    own = len(res) - len(hosted.arrays) - len(hosted.new)
    done(res[own:])
    return list(res[:own])


def _matmul(a, b, *, mode, name, grid, a_spec, b_spec, o_spec, out_shape, bias=None, bias_spec=None, residual=None,
            residual_spec=None, into=None, hosted=None, norm_g=None):
    nk = grid[2]
    dims = (_DOT_DIMS[mode], ((), ()))
    acc_shape = tuple(d for d in o_spec.block_shape if d is not None)

    def body(ins, outs, scratch, first, last):
        a_ref, b_ref = ins[0], ins[1]
        pos = 2
        bias_ref = res_ref = None
        if bias is not None:
            bias_ref = ins[pos]
            pos += 1
        if residual is not None:
            res_ref = ins[pos]
            pos += 1
        o_ref = outs[0]
        acc_ref = scratch[0] if nk > 1 else None
        first()
        prod = lax.dot_general(a_ref[...].astype(MXU_DTYPE), b_ref[...].astype(MXU_DTYPE), dims,
                               preferred_element_type=F32)

        def finish(acc):
            if bias_ref is not None:
                acc = acc + bias_ref[...]
            if res_ref is not None:
                acc = acc + res_ref[...]
            o_ref[...] = acc.astype(o_ref.dtype)
            if norm_g is not None:
                outs[1][...] = _rms(acc, ins[pos][...]).astype(outs[1].dtype)

        if nk == 1:
            finish(prod)
        else:
            k = pl.program_id(2)

            @pl.when(k == 0)
            def _():
                acc_ref[...] = prod

            @pl.when(k > 0)
            def _():
                acc_ref[...] += prod

            @pl.when(k == nk - 1)
            def _():
                finish(acc_ref[...])

        last()

    operands, in_specs = [a, b], [a_spec, b_spec]
    if bias is not None:
        operands.append(bias)
        in_specs.append(bias_spec)
    if residual is not None:
        operands.append(residual)
        in_specs.append(residual_spec)
    out_specs, out_shapes = [o_spec], [out_shape]
    if norm_g is not None:
        assert acc_shape[-1] == norm_g.shape[-1], "the fused norm needs whole rows in an output tile"
        operands.append(norm_g)
        in_specs.append(bias_spec)
        out_specs.append(o_spec)
        out_shapes.append(jax.ShapeDtypeStruct(out_shape.shape, MXU_DTYPE))
    aliases = {}
    if into is not None:
        aliases = {len(operands): 0}
        operands.append(into)
        in_specs.append(ANY)
    res = _call(body, hosted, name=name, grid=grid, in_specs=in_specs, out_specs=out_specs, out_shape=out_shapes,
                operands=operands, semantics=("parallel", "parallel", "arbitrary"),
                scratch_shapes=[pltpu.VMEM(acc_shape, F32)] if nk > 1 else [], aliases=aliases)
    return res if hosted or norm_g is not None else res[0]


def _tile(n, want):
    t = min(n, want)
    assert n % t == 0, (n, want)
    return t


def _mm_nn(a, b, name, *, b_lead=None, bias=None, residual=None, out_dtype=F32, tm=1024, tn=1024, tk=1024,
           hosted=None, norm_g=None):
    (m, k), n = a.shape, b.shape[-1]
    tm, tn, tk = _tile(m, tm), _tile(n, tn), _tile(k, tk)
    if b_lead is None:
        b_spec = pl.BlockSpec((tk, tn), lambda i, j, kk: (kk, j))
    else:
        b_spec = pl.BlockSpec((None, tk, tn), lambda i, j, kk: (b_lead, kk, j))
    return _matmul(
        a, b, mode="nn", name=name, grid=(m // tm, n // tn, k // tk),
        a_spec=pl.BlockSpec((tm, tk), lambda i, j, kk: (i, kk)), b_spec=b_spec,
        o_spec=pl.BlockSpec((tm, tn), lambda i, j, kk: (i, j)), out_shape=jax.ShapeDtypeStruct((m, n), out_dtype),
        bias=bias, bias_spec=pl.BlockSpec((1, tn), lambda i, j, kk: (0, j)),
        residual=residual, residual_spec=pl.BlockSpec((tm, tn), lambda i, j, kk: (i, j)), hosted=hosted,
        norm_g=norm_g)


def _mm_nt(a, b, name, *, b_lead=None, out_dtype=F32, tm=1024, tn=1024, tk=1024, hosted=None):
    (m, k), n = a.shape, b.shape[-2]
    tm, tn, tk = _tile(m, tm), _tile(n, tn), _tile(k, tk)
    if b_lead is None:
        b_spec = pl.BlockSpec((tn, tk), lambda i, j, kk: (j, kk))
    else:
        b_spec = pl.BlockSpec((None, tn, tk), lambda i, j, kk: (b_lead, j, kk))
    return _matmul(
        a, b, mode="nt", name=name, grid=(m // tm, n // tn, k // tk),
        a_spec=pl.BlockSpec((tm, tk), lambda i, j, kk: (i, kk)), b_spec=b_spec,
        o_spec=pl.BlockSpec((tm, tn), lambda i, j, kk: (i, j)), out_shape=jax.ShapeDtypeStruct((m, n), out_dtype),
        hosted=hosted)


def _mm_tn(a, b, name, *, lead=None, n_lead=None, into=None, tm=1024, tn=1024, tk=1024, out_dtype=F32, hosted=None):
    (k, m), n = a.shape, b.shape[-1]
    tm, tn, tk = _tile(m, tm), _tile(n, tn), _tile(k, tk)
    if lead is None:
        o_spec = pl.BlockSpec((tm, tn), lambda i, j, kk: (i, j))
        out_shape = jax.ShapeDtypeStruct((m, n), out_dtype)
    else:
        o_spec = pl.BlockSpec((None, tm, tn), lambda i, j, kk: (lead, i, j))
        out_shape = jax.ShapeDtypeStruct((n_lead, m, n), out_dtype)
    return _matmul(
        a, b, mode="tn", name=name, grid=(m // tm, n // tn, k // tk),
        a_spec=pl.BlockSpec((tk, tm), lambda i, j, kk: (kk, i)),
        b_spec=pl.BlockSpec((tk, tn), lambda i, j, kk: (kk, j)), o_spec=o_spec, out_shape=out_shape, into=into,
        hosted=hosted)


ROW_TILE = 512


def _pool_windows_causal(ext, first_row):
    tt = ext.shape[0] - POOL_HALO
    t = first_row + lax.broadcasted_iota(jnp.int32, (tt, 1), 0)
    outs = []
    for gi, win in enumerate(POOL_WINDOWS):
        cols = slice(gi * POOL_GC, (gi + 1) * POOL_GC)
        s = ext[:, cols]
        sh = 1
        while sh < win:
            s = s + _shift_down(s, sh)
            sh *= 2
        count = jnp.minimum(t + 1, win).astype(F32)
        outs.append(s[POOL_HALO:] / count - ext[POOL_HALO:, cols])
    return outs


def _prev_halo_spec(tt, halo, width):
    per = tt // halo
    return pl.BlockSpec((halo, width), lambda i: (jnp.maximum(i * per - 1, 0), 0))


def _pool_fwd_call(h0, g_mix, pool_w, pool_b, pool_scale, g_ffn, hosted=None):
    t = h0.shape[0]
    tt = _tile(t, ROW_TILE)

    def body(ins, outs, scratch, first, last):
        h_ref, halo_ref, gm_ref, w_ref, b_ref, sc_ref, gf_ref = ins
        h1_ref, hn_ref = outs
        i = pl.program_id(0)
        first()
        x, gm = h_ref[...], gm_ref[...]
        top = jnp.where(i > 0, _rms(halo_ref[...], gm), 0.0)
        ext = jnp.concatenate([top, _rms(x, gm)], axis=0)
        ps = _pool_windows_causal(ext, i * tt)
        ys = [jnp.dot(p.astype(MXU_DTYPE), w_ref[gi], preferred_element_type=F32) for gi, p in enumerate(ps)]
        mix = (jnp.concatenate(ys, axis=1) + b_ref[...]) * sc_ref[...]
        h1 = x + mix
        h1_ref[...] = h1
        hn_ref[...] = _rms(h1, gf_ref[...]).astype(hn_ref.dtype)
        last()

    row = pl.BlockSpec((tt, D_MODEL), lambda i: (i, 0))
    vec = pl.BlockSpec((1, D_MODEL), lambda i: (0, 0))
    wsp = pl.BlockSpec((len(POOL_WINDOWS), POOL_GC, POOL_GC), lambda i: (0, 0, 0))
    return _call(
        body, hosted, name="pool_fwd", grid=(t // tt,),
        in_specs=[row, _prev_halo_spec(tt, POOL_HALO, D_MODEL), vec, wsp, vec, vec, vec], out_specs=[row, row],
        out_shape=[jax.ShapeDtypeStruct(h0.shape, F32), jax.ShapeDtypeStruct(h0.shape, MXU_DTYPE)],
        operands=(h0, h0, g_mix, pool_w, pool_b, pool_scale, g_ffn), semantics=("parallel",))


def _pool_bwd_call(h0, dh1, g_mix, pool_w, pool_b, pool_scale, hosted=None):
    t = h0.shape[0]
    tt = _tile(t, ROW_TILE)
    nt = t // tt
    per = tt // POOL_HALO

    def body(ins, outs, scratch, first, last):
        h_ref, halo_ref, d_ref, dnext_ref, gm_ref, w_ref, b_ref, sc_ref = ins
        gx_ref, dw_ref, dv_ref = outs
        i = pl.program_id(0)
        first()
        x, gm, sc = h_ref[...], gm_ref[...], sc_ref[...]
        top = jnp.where(i > 0, _rms(halo_ref[...], gm), 0.0)
        ext = jnp.concatenate([top, _rms(x, gm)], axis=0)
        ps = _pool_windows_causal(ext, i * tt)

        dy = d_ref[...]
        dy_ext = jnp.concatenate([dy, jnp.where(i < nt - 1, dnext_ref[...], 0.0)], axis=0)
        dpre_ext = dy_ext * sc
        t_ext = i * tt + lax.broadcasted_iota(jnp.int32, (tt + POOL_HALO, 1), 0)

        @pl.when(i == 0)
        def _():
            dw_ref[...] = jnp.zeros_like(dw_ref)
            dv_ref[...] = jnp.zeros_like(dv_ref)

        dhn, ypre = [], []
        for gi, win in enumerate(POOL_WINDOWS):
            cols = slice(gi * POOL_GC, (gi + 1) * POOL_GC)
            w = w_ref[gi]
            p16 = ps[gi].astype(MXU_DTYPE)
            dpre16 = dpre_ext[:, cols].astype(MXU_DTYPE)
            ypre.append(jnp.dot(p16, w, preferred_element_type=F32))
            dw_ref[gi] += lax.dot_general(p16, dpre16[:tt], (((0,), (0,)), ((), ())), preferred_element_type=F32)
            dp_ext = lax.dot_general(dpre16, w, (((1,), (1,)), ((), ())), preferred_element_type=F32)
            s = dp_ext / jnp.minimum(t_ext + 1, win).astype(F32)
            sh = 1
            while sh < win:
                s = s + _shift_up(s, sh)
                sh *= 2
            dhn.append(s[:tt] - dp_ext[:tt])
        dhn = jnp.concatenate(dhn, axis=1)
        ypre = jnp.concatenate(ypre, axis=1) + b_ref[...]
        dx, dgt = _rms_bwd(x, gm, dhn)
        gx_ref[...] = dy + dx
        dv_ref[0:1, :] += jnp.sum(dpre_ext[:tt], axis=0, keepdims=True)
        dv_ref[1:2, :] += jnp.sum(dy * ypre, axis=0, keepdims=True)
        dv_ref[2:3, :] += jnp.sum(dgt, axis=0, keepdims=True)
        last()

    row = pl.BlockSpec((tt, D_MODEL), lambda i: (i, 0))
    vec = pl.BlockSpec((1, D_MODEL), lambda i: (0, 0))
    nxt = pl.BlockSpec((POOL_HALO, D_MODEL), lambda i: (jnp.minimum((i + 1) * per, t // POOL_HALO - 1), 0))
    wsp = pl.BlockSpec((len(POOL_WINDOWS), POOL_GC, POOL_GC), lambda i: (0, 0, 0))
    return _call(
        body, hosted, name="pool_bwd", grid=(nt,),
        in_specs=[row, _prev_halo_spec(tt, POOL_HALO, D_MODEL), row, nxt, vec, wsp, vec, vec],
        out_specs=[row, wsp, pl.BlockSpec((8, D_MODEL), lambda i: (0, 0))],
        out_shape=[jax.ShapeDtypeStruct(h0.shape, F32),
                   jax.ShapeDtypeStruct((len(POOL_WINDOWS), POOL_GC, POOL_GC), F32),
                   jax.ShapeDtypeStruct((8, D_MODEL), F32)],
        operands=(h0, h0, dh1, dh1, g_mix, pool_w, pool_b, pool_scale), semantics=("arbitrary",))


CONV_ROWS = 512
CONV_SUB = 128
LANES = 128


def _causal_conv(u_ext, w, b):
    return w[0:1] * _shift_down(u_ext, 2) + w[1:2] * _shift_down(u_ext, 1) + w[2:3] * u_ext + b


def _piece_rows(load, s, sub, tt, top, bottom=None):
    lo, hi = s * sub - CONV_HALO, (s + 1) * sub + (CONV_HALO if bottom else 0)
    parts = [top()] if lo < 0 else []
    lo = max(lo, 0)
    parts.append(load(lo, min(hi, tt) - lo))
    if hi > tt:
        parts.append(bottom())
    return parts[0] if len(parts) == 1 else jnp.concatenate(parts, axis=0)


def _fold8(x):
    acc = x[0:8]
    for r in range(8, x.shape[0], 8):
        acc = acc + x[r:r + 8]
    return acc


def _conv_glu_fwd_call(u, conv_w, conv_b, name, hosted=None):
    _, t, f = u.shape
    tt = _tile(t, CONV_ROWS)
    sub = _tile(tt, CONV_SUB)
    per = tt // CONV_HALO

    def body(ins, outs, scratch, first, last):
        u_ref, up_ref, w_ref, b_ref = ins
        z_ref = outs[0]
        i = pl.program_id(0)
        first()

        def chunk(j, carry):
            lanes = pl.ds(pl.multiple_of(j * LANES, LANES), LANES)
            w, b = w_ref[:, :, lanes], b_ref[:, lanes]
            for s in range(tt // sub):
                c = []
                for hf in range(2):
                    ext = _piece_rows(lambda at, n: u_ref[hf, pl.ds(at, n), lanes].astype(F32), s, sub, tt,
                                      top=lambda: jnp.where(i > 0, up_ref[hf, :, lanes].astype(F32), 0.0))
                    c.append(_causal_conv(ext, w[:, hf, :], b[hf:hf + 1, :])[CONV_HALO:])
                z_ref[pl.ds(s * sub, sub), lanes] = (c[0] * c[1] * _sigmoid(c[1])).astype(z_ref.dtype)
            return carry

        lax.fori_loop(0, f // LANES, chunk, 0)
        last()

    res = _call(
        body, hosted, name=name, grid=(t // tt,),
        in_specs=[pl.BlockSpec((2, tt, f), lambda i: (0, i, 0)),
                  pl.BlockSpec((2, CONV_HALO, f), lambda i: (0, jnp.maximum(i * per - 1, 0), 0)),
                  pl.BlockSpec((3, 2, f), lambda i: (0, 0, 0)), pl.BlockSpec((2, f), lambda i: (0, 0))],
        out_specs=[pl.BlockSpec((tt, f), lambda i: (i, 0))], out_shape=[jax.ShapeDtypeStruct((t, f), MXU_DTYPE)],
        operands=(u, u, conv_w, conv_b), semantics=("parallel",))
    return res if hosted else res[0]


def _conv_glu_bwd_call(u, dz, conv_w, conv_b, name, hosted=None):
    _, t, f = u.shape
    tt = _tile(t, CONV_ROWS)
    sub = _tile(tt, CONV_SUB)
    nt = t // tt
    per = tt // CONV_HALO
    halo = CONV_HALO

    def body(ins, outs, scratch, first, last):
        u_ref, up_ref, un_ref, dz_ref, dzn_ref, w_ref, b_ref = ins
        du_ref, dw_ref, db_ref = outs
        dwacc, dbacc = scratch
        i = pl.program_id(0)
        first()

        @pl.when(i == 0)
        def _():
            dwacc[...] = jnp.zeros_like(dwacc)
            dbacc[...] = jnp.zeros_like(dbacc)

        main = slice(halo, halo + sub)

        def chunk(j, carry):
            lanes = pl.ds(pl.multiple_of(j * LANES, LANES), LANES)
            w, b = w_ref[:, :, lanes], b_ref[:, lanes]
            for s in range(tt // sub):
                ue = [_piece_rows(lambda at, n: u_ref[hf, pl.ds(at, n), lanes].astype(F32), s, sub, tt,
                                  top=lambda: jnp.where(i > 0, up_ref[hf, :, lanes].astype(F32), 0.0),
                                  bottom=lambda: un_ref[hf, :, lanes].astype(F32)) for hf in range(2)]
                a, g = [_causal_conv(ue[hf], w[:, hf, :], b[hf:hf + 1, :])[halo:] for hf in range(2)]
                sg = _sigmoid(g)
                if (s + 1) * sub < tt:
                    dzs = dz_ref[pl.ds(s * sub, sub + 2 * halo), lanes].astype(F32)[:sub + halo]
                else:
                    dzs = jnp.concatenate([dz_ref[pl.ds(s * sub, sub), lanes].astype(F32),
                                           jnp.where(i < nt - 1, dzn_ref[:, lanes].astype(F32), 0.0)], axis=0)
                gs = g * sg
                dc = [dzs * gs, dzs * a * (sg + gs - gs * sg)]
                for hf in range(2):
                    d = dc[hf]
                    d0, d1, d2 = d[:sub], _shift_up(d, 1)[:sub], _shift_up(d, 2)[:sub]
                    du = w[2:3, hf, :] * d0 + w[1:2, hf, :] * d1 + w[0:1, hf, :] * d2
                    du_ref[hf, pl.ds(s * sub, sub), lanes] = du.astype(du_ref.dtype)
                    um = ue[hf][main]
                    for k, dd in enumerate((d2, d1, d0)):
                        dwacc[k, hf, :, lanes] += _fold8(dd * um)
                    dbacc[hf, :, lanes] += _fold8(d0)
            return carry

        lax.fori_loop(0, f // LANES, chunk, 0)

        @pl.when(i == nt - 1)
        def _():
            dw_ref[...] = jnp.sum(dwacc[...], axis=2)
            db_ref[...] = jnp.sum(dbacc[...], axis=1)

        last()

    end = t // halo - 1
    return _call(
        body, hosted, name=name, grid=(nt,),
        in_specs=[pl.BlockSpec((2, tt, f), lambda i: (0, i, 0)),
                  pl.BlockSpec((2, halo, f), lambda i: (0, jnp.maximum(i * per - 1, 0), 0)),
                  pl.BlockSpec((2, halo, f), lambda i: (0, jnp.minimum((i + 1) * per, end), 0)),
                  pl.BlockSpec((tt, f), lambda i: (i, 0)),
                  pl.BlockSpec((halo, f), lambda i: (jnp.minimum((i + 1) * per, end), 0)),
                  pl.BlockSpec((3, 2, f), lambda i: (0, 0, 0)), pl.BlockSpec((2, f), lambda i: (0, 0))],
        out_specs=[pl.BlockSpec((2, tt, f), lambda i: (0, i, 0)),
                   pl.BlockSpec((3, 2, f), lambda i: (0, 0, 0)), pl.BlockSpec((2, f), lambda i: (0, 0))],
        out_shape=[jax.ShapeDtypeStruct(u.shape, MXU_DTYPE), jax.ShapeDtypeStruct((3, 2, f), F32),
                   jax.ShapeDtypeStruct((2, f), F32)],
        operands=(u, u, u, dz, dz, conv_w, conv_b), semantics=("arbitrary",),
        scratch_shapes=[pltpu.VMEM((3, 2, 8, f), F32), pltpu.VMEM((2, 8, f), F32)])


def _kv_even_odd(band, k):
    pair, parity = divmod(k, 2)
    blk = band[:, 128 * pair:128 * (pair + 1)].astype(F32)
    lane = lax.broadcasted_iota(jnp.int32, blk.shape, 1)
    own = jnp.where((lane >= HEAD_DIM) == (parity == 1), blk, 0.0)
    swapped = pltpu.roll(own, HEAD_DIM, axis=1)
    even, odd = (own, swapped) if parity == 0 else (swapped, own)
    return even.astype(MXU_DTYPE), odd.astype(MXU_DTYPE)


def _stack_pairs(ref, k):
    return jnp.concatenate([ref[:, 256 * k:256 * k + 128], ref[:, 256 * k + 128:256 * k + 256]], axis=0)


_GROUP_ORDER = (0, 2, 1, 3)


def _attn_fill_tables(bias_ref):
    shape = (4 * ATT_BLOCK, 2 * ATT_BLOCK)
    row = lax.broadcasted_iota(jnp.int32, shape, 0)
    col = lax.broadcasted_iota(jnp.int32, shape, 1)
    dist = (row & (ATT_BLOCK - 1)) + ATT_BLOCK - col
    ok = (dist >= 0) & (dist < ATT_BLOCK)
    distf = dist.astype(F32)
    rb = lax.broadcasted_iota(jnp.int32, (shape[0], 1), 0) // ATT_BLOCK
    for k in range(N_KV_HEADS):
        slope = jnp.zeros((shape[0], 1), F32)
        for r, gq in enumerate(_GROUP_ORDER):
            slope = jnp.where(rb == r, ALIBI_SLOPES[4 * k + gq], slope)
        bias = jnp.where(ok, -slope * distf, -jnp.inf)
        bias_ref[0, k] = bias
        bias_ref[1, k] = jnp.where(col >= ATT_BLOCK, bias, -jnp.inf)


def _attn_probs(qs, k_even, k_odd, bias, k, sink_ref):
    nt_dims = (((1,), (1,)), ((), ()))
    s = jnp.concatenate([lax.dot_general(qs, k_even, nt_dims, preferred_element_type=F32),
                         lax.dot_general(qs, k_odd, nt_dims, preferred_element_type=F32)], axis=0) + bias
    rb = lax.broadcasted_iota(jnp.int32, (s.shape[0], 1), 0) // ATT_BLOCK
    sink = jnp.zeros((s.shape[0], 1), F32)
    for r, gq in enumerate(_GROUP_ORDER):
        sink = jnp.where(rb == r, sink_ref[4 * k + gq], sink)
    m = jnp.maximum(jnp.max(s, axis=-1, keepdims=True), sink)
    p = jnp.exp(s - m)
    es = jnp.exp(sink - m)
    return p, 1.0 / (jnp.sum(p, axis=-1, keepdims=True) + es), es


_ATTN_TABLES = [pltpu.VMEM((2, N_KV_HEADS, 4 * ATT_BLOCK, 2 * ATT_BLOCK), F32)]


def _attn_specs(with_do):
    prev = lambda n: jnp.maximum(n - 1, 0)
    specs = [pl.BlockSpec(memory_space=pltpu.SMEM),
             pl.BlockSpec((ATT_BLOCK, D_MODEL), lambda n: (n, 0)),
             pl.BlockSpec((ATT_BLOCK, 256), lambda n: (prev(n), 4)), pl.BlockSpec((ATT_BLOCK, 256), lambda n: (n, 4)),
             pl.BlockSpec((ATT_BLOCK, 256), lambda n: (prev(n), 5)), pl.BlockSpec((ATT_BLOCK, 256), lambda n: (n, 5))]
    if with_do:
        specs.append(pl.BlockSpec((ATT_BLOCK, D_MODEL), lambda n: (n, 0)))
    return specs


def _attn_fwd_call(qkv, sinks, hosted=None):
    t = qkv.shape[0]

    def body(ins, outs, scratch, first, last):
        sink_ref, q_ref, kp_ref, kc_ref, vp_ref, vc_ref = ins
        o_ref, bias_ref = outs[0], scratch[0]
        n = pl.program_id(0)
        first()

        @pl.when(n == 0)
        def _():
            _attn_fill_tables(bias_ref)

        first = 1 - jnp.minimum(n, 1)
        kband = jnp.concatenate([kp_ref[...], kc_ref[...]], axis=0)
        vband = jnp.concatenate([vp_ref[...], vc_ref[...]], axis=0)
        for k in range(N_KV_HEADS):
            k_even, k_odd = _kv_even_odd(kband, k)
            v_even, v_odd = _kv_even_odd(vband, k)
            qs = _stack_pairs(q_ref, k) * ATT_SCALE
            p, rdenom, _ = _attn_probs(qs, k_even, k_odd, bias_ref[first, k], k, sink_ref)
            probs = (p * rdenom).astype(MXU_DTYPE)
            o = (jnp.dot(probs[:256], v_even, preferred_element_type=F32)
                 + jnp.dot(probs[256:], v_odd, preferred_element_type=F32))
            o_ref[:, 256 * k:256 * k + 128] = o[:128].astype(o_ref.dtype)
            o_ref[:, 256 * k + 128:256 * k + 256] = o[128:].astype(o_ref.dtype)
        last()

    return _call(
        body, hosted, name="attn_fwd", grid=(t // ATT_BLOCK,), in_specs=_attn_specs(False),
        out_specs=[pl.BlockSpec((ATT_BLOCK, D_MODEL), lambda n: (n, 0))],
        out_shape=[jax.ShapeDtypeStruct((t, D_MODEL), MXU_DTYPE)], operands=(sinks, qkv, qkv, qkv, qkv, qkv),
        semantics=("arbitrary",), scratch_shapes=_ATTN_TABLES)


def _attn_bwd_call(qkv, sinks, do, hosted=None):
    t = qkv.shape[0]
    nb = t // ATT_BLOCK
    tn_dims = (((0,), (0,)), ((), ()))
    nt_dims = (((1,), (1,)), ((), ()))

    def to_native(even_part, odd_part, parity):
        lane = lax.broadcasted_iota(jnp.int32, even_part.shape, 1)
        lo = lane < HEAD_DIM
        e, o = jnp.where(lo, even_part, 0.0), jnp.where(lo, 0.0, odd_part)
        if parity == 0:
            return e + pltpu.roll(o, HEAD_DIM, axis=1)
        return pltpu.roll(e, HEAD_DIM, axis=1) + o

    def body(ins, outs, scratch, first, last):
        sink_ref, q_ref, kp_ref, kc_ref, vp_ref, vc_ref, do_ref = ins
        dq_ref, dk_ref, dv_ref, db_ref, dsink_ref = outs
        bias_ref = scratch[0]
        n = pl.program_id(0)
        first()

        @pl.when(n == 0)
        def _():
            _attn_fill_tables(bias_ref)
            dk_ref[...] = jnp.zeros_like(dk_ref)
            dv_ref[...] = jnp.zeros_like(dv_ref)
            db_ref[...] = jnp.zeros_like(db_ref)
            dsink_ref[...] = jnp.zeros_like(dsink_ref)

        kband = jnp.concatenate([kp_ref[...], kc_ref[...]], axis=0)
        vband = jnp.concatenate([vp_ref[...], vc_ref[...]], axis=0)
        dk_pairs = [jnp.zeros((2 * ATT_BLOCK, 128), F32), jnp.zeros((2 * ATT_BLOCK, 128), F32)]
        dv_pairs = [jnp.zeros((2 * ATT_BLOCK, 128), F32), jnp.zeros((2 * ATT_BLOCK, 128), F32)]
        sink_lane = lax.broadcasted_iota(jnp.int32, (8, 128), 1)
        sink_row = lax.broadcasted_iota(jnp.int32, (8, 128), 0)
        dsink = jnp.zeros((8, 128), F32)
        first = 1 - jnp.minimum(n, 1)
        for k in range(N_KV_HEADS):
            k_even, k_odd = _kv_even_odd(kband, k)
            v_even, v_odd = _kv_even_odd(vband, k)
            qs = _stack_pairs(q_ref, k) * ATT_SCALE
            dos = _stack_pairs(do_ref, k)
            p, rdenom, es = _attn_probs(qs, k_even, k_odd, bias_ref[first, k], k, sink_ref)
            probs = p * rdenom
            dp = jnp.concatenate([lax.dot_general(dos, v_even, nt_dims, preferred_element_type=F32),
                                  lax.dot_general(dos, v_odd, nt_dims, preferred_element_type=F32)], axis=0)
            delta = jnp.sum(probs * dp, axis=-1, keepdims=True)
            ds16 = (probs * (dp - delta)).astype(MXU_DTYPE)
            p16 = probs.astype(MXU_DTYPE)
            dsink_rows = -(es * rdenom) * delta
            for r, gq in enumerate(_GROUP_ORDER):
                tot = jnp.sum(dsink_rows[ATT_BLOCK * r:ATT_BLOCK * (r + 1)], axis=0, keepdims=True)
                dsink = dsink + jnp.where((sink_lane == 4 * k + gq) & (sink_row == 0), tot, 0.0)
            dqs = (jnp.dot(ds16[:256], k_even, preferred_element_type=F32)
                   + jnp.dot(ds16[256:], k_odd, preferred_element_type=F32)) * ATT_SCALE
            dq_ref[:, 256 * k:256 * k + 128] = dqs[:128].astype(dq_ref.dtype)
            dq_ref[:, 256 * k + 128:256 * k + 256] = dqs[128:].astype(dq_ref.dtype)
            db_ref[:, 256 * k:256 * k + 128] += jnp.sum(dqs[:128], axis=0, keepdims=True)
            db_ref[:, 256 * k + 128:256 * k + 256] += jnp.sum(dqs[128:], axis=0, keepdims=True)
            pair, parity = divmod(k, 2)
            dk_pairs[pair] = dk_pairs[pair] + to_native(
                lax.dot_general(ds16[:256], qs, tn_dims, preferred_element_type=F32),
                lax.dot_general(ds16[256:], qs, tn_dims, preferred_element_type=F32), parity)
            dv_pairs[pair] = dv_pairs[pair] + to_native(
                lax.dot_general(p16[:256], dos, tn_dims, preferred_element_type=F32),
                lax.dot_general(p16[256:], dos, tn_dims, preferred_element_type=F32), parity)
        dsink_ref[...] += dsink
        dk_band = jnp.concatenate(dk_pairs, axis=1)
        dv_band = jnp.concatenate(dv_pairs, axis=1)
        cur = pl.multiple_of(n * ATT_BLOCK, ATT_BLOCK)
        dk_ref[pl.ds(cur, ATT_BLOCK), :] += dk_band[ATT_BLOCK:]
        dv_ref[pl.ds(cur, ATT_BLOCK), :] += dv_band[ATT_BLOCK:]

        @pl.when(n > 0)
        def _():
            prv = pl.multiple_of((n - 1) * ATT_BLOCK, ATT_BLOCK)
            dk_ref[pl.ds(prv, ATT_BLOCK), :] += dk_band[:ATT_BLOCK]
            dv_ref[pl.ds(prv, ATT_BLOCK), :] += dv_band[:ATT_BLOCK]

        @pl.when(n == nb - 1)
        def _():
            db_ref[:, 1024:1280] = jnp.sum(dk_ref[...], axis=0, keepdims=True)
            db_ref[:, 1280:1536] = jnp.sum(dv_ref[...], axis=0, keepdims=True)

        last()

    whole = lambda shape: pl.BlockSpec(shape, lambda n: (0, 0))
    return _call(
        body, hosted, name="attn_bwd", grid=(nb,), in_specs=_attn_specs(True),
        out_specs=[pl.BlockSpec((ATT_BLOCK, D_MODEL), lambda n: (n, 0)), whole((t, 256)), whole((t, 256)),
                   whole((1, QKV_DIM)), whole((8, 128))],
        out_shape=[jax.ShapeDtypeStruct((t, D_MODEL), MXU_DTYPE), jax.ShapeDtypeStruct((t, 256), F32),
                   jax.ShapeDtypeStruct((t, 256), F32), jax.ShapeDtypeStruct((1, QKV_DIM), F32),
                   jax.ShapeDtypeStruct((8, 128), F32)],
        operands=(sinks, qkv, qkv, qkv, qkv, qkv, do), semantics=("arbitrary",), scratch_shapes=_ATTN_TABLES)


FF_TILE = 1408


def _gathering_tile(j):
    me = 2 * lax.axis_index("x") + lax.axis_index("y")
    return me ^ (((j & 1) << 1) | (j >> 1))


def _ffn_in_gathering_call(hn, w_in, layer, hosted):
    t = hn.shape[0]
    tm = _tile(t, 1024)
    n_i = t // tm
    halves = D_FF // FF_TILE
    assert FF_TILE == W_IN_COLS and 2 * halves == N_CHIPS

    def body(ins, outs, scratch, first, last):
        hn_ref = ins[0]
        u_ref, w_ref = outs
        wbuf, load_sem, ici_send, ici_recv, hand_send, hand_recv = scratch
        j, i = pl.program_id(0), pl.program_id(1)
        x, y, c, chips = _mesh_place()
        me = 2 * x + y
        mover = c == layer

        def shard(k):
            return w_ref.at[layer, :, _col(k, W_IN_COLS)]

        def over_ici(jj, k):
            cx, cy = chips[jj]
            return pltpu.make_async_remote_copy(src_ref=shard(k), dst_ref=shard(k), send_sem=ici_send.at[jj],
                                                recv_sem=ici_recv.at[jj], device_id=(cx, cy, c), device_id_type=MESH)

        def handed(jj):
            cx, cy = chips[jj]
            return pltpu.make_async_remote_copy(
                src_ref=shard(2 * cx + cy), dst_ref=shard(2 * cx + cy), send_sem=hand_send.at[jj],
                recv_sem=hand_recv.at[jj], device_id=(x, y, 1 - c), device_id_type=MESH)

        far = N_CHIPS - 2

        @pl.when((j == 0) & (i == 0) & mover)
        def _():
            over_ici(far, me).start()
            for jj in NEIGHBOURS:
                handed(jj).start()

        first()

        def load(step):
            return pltpu.make_async_copy(shard(_gathering_tile(step)), wbuf.at[step % 2], load_sem.at[step % 2])

        @pl.when((j == 0) & (i == 0))
        def _():
            load(0).start()

        for jj in range(N_CHIPS - 1):
            @pl.when((j == jj) & (i == n_i - 1))
            def _(jj=jj):
                if jj == far:
                    @pl.when(mover)
                    def _():
                        over_ici(far, 2 * chips[far][0] + chips[far][1]).wait_recv()
                        handed(far).start()

                @pl.when(jnp.logical_not(mover))
                def _():
                    handed(jj).wait_recv()

                load(jj + 1).start()

        @pl.when(i == 0)
        def _():
            load(j).wait()

        u_ref[...] = jnp.dot(hn_ref[...].astype(MXU_DTYPE), wbuf[j % 2].astype(MXU_DTYPE),
                             preferred_element_type=F32).astype(u_ref.dtype)

        @pl.when((j == N_CHIPS - 1) & (i == n_i - 1) & mover)
        def _():
            over_ici(far, me).wait_send()
            for jj in range(N_CHIPS - 1):
                handed(jj).wait_send()

        last()

    def u_index(j, i):
        tile = _gathering_tile(j)
        return tile // halves, i, tile % halves

    three = [pltpu.SemaphoreType.DMA((N_CHIPS - 1,))] * 4
    return _call(
        body, hosted, name=f"ffn{layer}_in", grid=(N_CHIPS, n_i),
        in_specs=[pl.BlockSpec((tm, D_MODEL), lambda j, i: (i, 0)), ANY],
        out_specs=[pl.BlockSpec((None, tm, FF_TILE), u_index), ANY],
        out_shape=[jax.ShapeDtypeStruct((2, t, D_FF), U_DTYPE), jax.ShapeDtypeStruct(w_in.shape, w_in.dtype)],
        operands=(hn, w_in), semantics=("arbitrary", "arbitrary"), aliases={1: 1},
        scratch_shapes=[pltpu.VMEM((2, D_MODEL, W_IN_COLS), w_in.dtype), pltpu.SemaphoreType.DMA((2,))] + three)


def _ffn_out_loss_call(z, w_out, layer, h, g, target):
    t, f = z.shape
    tm = _tile(t, ROW_TILE)

    def body(z_ref, w_ref, h_ref, g_ref, t_ref, loss_ref, dh_ref, dh16_ref, dg_ref):
        x = h_ref[...] + jnp.dot(z_ref[...], w_ref[...], preferred_element_type=F32)
        gv = g_ref[...]
        err = _rms(x, gv) - t_ref[...]
        dx, dgt = _rms_bwd(x, gv, err * (1.0 / D_MODEL))
        dh_ref[...] = dx
        dh16_ref[...] = dx.astype(dh16_ref.dtype)

        @pl.when(pl.program_id(0) == 0)
        def _():
            dg_ref[...] = jnp.zeros_like(dg_ref)
            loss_ref[...] = jnp.zeros_like(loss_ref)

        dg_ref[...] += jnp.sum(dgt, axis=0, keepdims=True)
        per_token = jnp.mean(err * err, axis=-1, keepdims=True)
        loss_ref[...] += 0.5 * jnp.sum(per_token, axis=0, keepdims=True)

    row = pl.BlockSpec((tm, D_MODEL), lambda i: (i, 0))
    vec = pl.BlockSpec((1, D_MODEL), lambda i: (0, 0))
    return pl.pallas_call(
        body, name=f"ffn{layer}_out_loss", grid=(t // tm,),
        in_specs=[pl.BlockSpec((tm, f), lambda i: (i, 0)), pl.BlockSpec((None, f, D_MODEL), lambda i: (layer, 0, 0)),
                  row, vec, row],
        out_specs=[pl.BlockSpec((1, 1), lambda i: (0, 0)), row, row, vec],
        out_shape=[jax.ShapeDtypeStruct((1, 1), F32), jax.ShapeDtypeStruct((t, D_MODEL), F32),
                   jax.ShapeDtypeStruct((t, D_MODEL), MXU_DTYPE), jax.ShapeDtypeStruct((1, D_MODEL), F32)],
        compiler_params=_params("arbitrary"))(z, w_out, h, g, target)


def _ffn_fwd(hn, h, w, conv_w, conv_b, layer, hooks=None, next_norm=None, loss_head=None, gathers_w_in=False):
    t = hn.shape[0]
    tm = _tile(t, 1024)
    halves = D_FF // FF_TILE
    if gathers_w_in:
        u, w["w_in"] = _ride(hooks, f"in{layer}", lambda hosted: _ffn_in_gathering_call(hn, w["w_in"], layer, hosted))
    else:
        u, = _ride(hooks, f"in{layer}", lambda hosted: _matmul(
            hn, w["w_in"], mode="nn", name=f"ffn{layer}_in", grid=(2 * halves, t // tm, 1),
            a_spec=pl.BlockSpec((tm, D_MODEL), lambda j, i, k: (i, 0)),
            b_spec=pl.BlockSpec((None, D_MODEL, FF_TILE), lambda j, i, k: (layer, 0, j)),
            o_spec=pl.BlockSpec((None, tm, FF_TILE), lambda j, i, k: (j // halves, i, j % halves)),
            out_shape=jax.ShapeDtypeStruct((2, t, D_FF), U_DTYPE), hosted=hosted))
    z, = _ride(hooks, f"glu{layer}", lambda hosted: _conv_glu_fwd_call(u, conv_w, conv_b, f"ffn{layer}_glu",
                                                                       hosted=hosted))
    if loss_head is not None:
        return (*_ffn_out_loss_call(z, w["w_out"], layer, h, *loss_head), u, z)
    outs = _ride(hooks, f"out{layer}", lambda hosted: _mm_nn(z, w["w_out"], f"ffn{layer}_out", b_lead=layer, residual=h,
                                                     tk=D_FF, hosted=hosted, norm_g=next_norm))
    return outs[0], (outs[1] if next_norm is not None else None), u, z


def _ffn_bwd(dh, hn, u, z, w_in, w_out, conv_w, conv_b, layer, g_in_acc, g_out_acc, norm, hooks=None):
    t = hn.shape[0]
    dz, = _ride(hooks, "dz", lambda hosted: _mm_nt(dh, w_out, f"ffn{layer}_dz", b_lead=layer, out_dtype=MXU_DTYPE,
                                                   tn=FF_TILE, hosted=hosted))
    du, dcw, dcb = _ride(hooks, "dglu", lambda hosted: _conv_glu_bwd_call(u, dz, conv_w, conv_b, f"ffn{layer}_dglu",
                                                                          hosted=hosted))
    halves = D_FF // FF_TILE
    rows = D_MODEL // 2
    g_in, = _ride(hooks, "dwin", lambda hosted: _matmul(
        hn, du, mode="tn", name=f"ffn{layer}_dwin", grid=(2 * halves, D_MODEL // rows, 1),
        a_spec=pl.BlockSpec((t, rows), lambda j, i, k: (0, i)),
        b_spec=pl.BlockSpec((None, t, FF_TILE), lambda j, i, k: (j // halves, 0, j % halves)),
        o_spec=pl.BlockSpec((None, rows, FF_TILE), lambda j, i, k: (layer, i, j)),
        out_shape=jax.ShapeDtypeStruct((2, D_MODEL, 2 * D_FF), WIRE_DTYPE), into=g_in_acc, hosted=hosted),
        {"dcw": dcw, "dcb": dcb})
    g_out, = _ride(hooks, "dwout", lambda hosted: _mm_tn(
        z, dh, f"ffn{layer}_dwout", lead=layer, n_lead=2, into=g_out_acc, tm=WGRAD_TILE, tk=t, out_dtype=WIRE_DTYPE,
        hosted=hosted), {"g_in": g_in})
    dx, dx16, dg = _ride(hooks, "dhn", lambda hosted: _ffn_dhn_call(du, w_in, layer, norm, hosted),
                         {"g_in": g_in, "g_out": g_out})
    return dx, dx16, dg, g_in, g_out, dcw, dcb


WGRAD_TILE = 256


_NT_DIMS = (((1,), (1,)), ((), ()))


def _dhn_call(name, a, a_spec, w, w_spec, product, t, tm, norm, hosted=None, staged=None):
    h, g, dres = norm

    def body(ins, outs, scratch, first, last):
        a_ref, w_ref, h_ref, g_ref, dres_ref = ins
        dx_ref, dx16_ref, dg_ref = outs
        first()
        if staged:
            wbuf, wsem = scratch

            @pl.when(pl.program_id(0) == 0)
            def _():
                fetch = pltpu.make_async_copy(staged(w_ref), wbuf, wsem.at[0])
                fetch.start()
                fetch.wait()

            w_ref = wbuf
        dx, dgt = _rms_bwd(h_ref[...], g_ref[...], product(a_ref, w_ref))
        dx = dres_ref[...] + dx
        dx_ref[...] = dx
        dx16_ref[...] = dx.astype(dx16_ref.dtype)

        @pl.when(pl.program_id(0) == 0)
        def _():
            dg_ref[...] = jnp.zeros_like(dg_ref)

        dg_ref[...] += jnp.sum(dgt, axis=0, keepdims=True)
        last()

    row = pl.BlockSpec((tm, D_MODEL), lambda i: (i, 0))
    vec = pl.BlockSpec((1, D_MODEL), lambda i: (0, 0))
    scratch = []
    if staged:
        block = tuple(d for d in w_spec.block_shape if d is not None)
        scratch = [pltpu.VMEM(block, w.dtype), pltpu.SemaphoreType.DMA((1,))]
        w_spec = ANY
    return _call(
        body, hosted, name=name, grid=(t // tm,), in_specs=[a_spec, w_spec, row, vec, row], out_specs=[row, row, vec],
        out_shape=[jax.ShapeDtypeStruct((t, D_MODEL), F32), jax.ShapeDtypeStruct((t, D_MODEL), MXU_DTYPE),
                   jax.ShapeDtypeStruct((1, D_MODEL), F32)],
        operands=(a, w, h, g, dres), semantics=("arbitrary",), scratch_shapes=scratch)


def _ffn_dhn_call(du, w_in, layer, norm, hosted=None):
    _, t, f = du.shape
    tm = _tile(t, 512)

    def product(du_ref, w_ref):
        return (lax.dot_general(du_ref[0], w_ref[:, :f], _NT_DIMS, preferred_element_type=F32)
                + lax.dot_general(du_ref[1], w_ref[:, f:], _NT_DIMS, preferred_element_type=F32))

    return _dhn_call(f"ffn{layer}_dhn", du, pl.BlockSpec((2, tm, f), lambda i: (0, i, 0)), w_in,
                     pl.BlockSpec((None, D_MODEL, 2 * f), lambda i: (layer, 0, 0)), product, t, tm, norm, hosted,
                     staged=(lambda ref: ref.at[layer]) if hosted else None)


def _attn_dhn_call(dqkv, w_qkv, norm):
    t, n = dqkv.shape
    tm = _tile(t, 1024)

    def product(a_ref, w_ref):
        return lax.dot_general(a_ref[...], w_ref[...], _NT_DIMS, preferred_element_type=F32)

    return _dhn_call("attn_dhn", dqkv, pl.BlockSpec((tm, n), lambda i: (i, 0)), w_qkv,
                     pl.BlockSpec((D_MODEL, n), lambda i: (0, 0)), product, t, tm, norm)


def _local_step(x, target, w, *, norm_mix, norm_ffn, norm_f, pool_b, pool_scale, b_qkv, sinks, conv_w, conv_b,
                late_gather=None, reducer=None):
    cw = [conv_w[l].reshape(3, 2, D_FF) for l in range(2)]
    cb = [conv_b[l].reshape(2, D_FF) for l in range(2)]
    gm = [norm_mix[l:l + 1] for l in range(2)]
    gf = [norm_ffn[l:l + 1] for l in range(2)]

    w = dict(w)
    fwd_hooks = {}
    if late_gather:
        def gather_hook(stage):
            return (lambda env: late_gather(stage, w)), (lambda got: w.update(zip(_late_names(stage), got)))

        fwd_hooks = {stage: gather_hook(stage) for stage in GATHER_RIDES}

    h1, hn_f0 = _ride(fwd_hooks, "pool_fwd", lambda hosted: _pool_fwd_call(x, gm[0], w["pool_w"], pool_b, pool_scale,
                                                                            gf[0], hosted))
    h2, hn_a, u0, z0 = _ffn_fwd(hn_f0, h1, w, cw[0], cb[0], 0, hooks=fwd_hooks, next_norm=gm[1],
                                gathers_w_in=bool(late_gather))
    qkv = _mm_nn(hn_a, w["w_qkv"], "attn_qkv", bias=b_qkv, out_dtype=MXU_DTYPE, tn=QKV_DIM)
    o, = _ride(fwd_hooks, "attn_fwd", lambda hosted: _attn_fwd_call(qkv, sinks, hosted))
    h3, hn_f1 = _ride(fwd_hooks, "attn_out", lambda hosted: _mm_nn(o, w["w_o"], "attn_out", residual=h2, hosted=hosted,
                                                                   norm_g=gf[1]))
    loss, dh, dh16, d_norm_f, u1, z1 = _ffn_fwd(hn_f1, h3, w, cw[1], cb[1], 1, hooks=fwd_hooks,
                                                loss_head=(norm_f, target))

    rd = reducer
    first, att, w_in0, w_out0 = _Reducer.FIRST, _Reducer.ATT, _Reducer.W_IN0, _Reducer.W_OUT0
    hooks1 = hooks_attn = hooks0 = hooks_pool = None
    if rd:
        hooks1 = {"dhn": (lambda env: rd.swap(first, {"w_in": env["g_in"], "w_out": env["g_out"]}),
                          lambda got: rd.pair_sums(first, got))}
        hooks_attn = {"attn_bwd": (lambda env: rd.scatter(first), lambda got: rd.chip_sums(first, got))}

        def early_pieces(env):
            return {"b_qkv": d_bqkv, "sinks": d_sinks[0:1], "norm_mix1": d_gm1, "norm_ffn1": d_gf1,
                    "conv_w": jnp.stack([env["dcw"].reshape(3, 2 * D_FF), dcw1.reshape(3, 2 * D_FF)]),
                    "conv_b": jnp.stack([env["dcb"].reshape(2 * D_FF), dcb1.reshape(2 * D_FF)]),
                    "norm_f": d_norm_f, "loss": loss}

        def after_dz(got):
            rd.shared(first, got[:len(first)])
            rd.pair_sums(att, got[len(first):])

        def after_dwin(got):
            rd.shared(att, got[:len(att)])
            rd.small_early_done(got[-1])

        def after_dhn(got):
            rd.chip_sums(w_in0, got[:1])
            rd.pair_sums(w_out0, got[1:])

        def after_pool_bwd(got):
            rd.shared(w_in0, got[:1])
            rd.chip_sums(w_out0, got[1:])

        hooks0 = {
            "dz": (lambda env: _both(rd.share(first), rd.swap(att, {"w_qkv": g_qkv, "w_o": g_o4})), after_dz),
            "dglu": (lambda env: rd.scatter(att), lambda got: rd.chip_sums(att, got)),
            "dwin": (lambda env: _both(rd.share(att), rd.small_early(early_pieces(env))), after_dwin),
            "dwout": (lambda env: rd.swap(w_in0, {"w_in": env["g_in"]}), lambda got: rd.pair_sums(w_in0, got)),
            "dhn": (lambda env: _both(rd.scatter(w_in0), rd.swap(w_out0, {"w_out": env["g_out"]})), after_dhn),
        }
        hooks_pool = {"pool_bwd": (lambda env: _both(rd.share(w_in0), rd.scatter(w_out0)), after_pool_bwd)}
    dh, dh16, d_gf1, g_in, g_out, dcw1, dcb1 = _ffn_bwd(
        dh16, hn_f1, u1, z1, w["w_in"], w["w_out"], cw[1], cb[1], 1, None, None, (h3, gf[1], dh), hooks=hooks1)
    do = _mm_nt(dh16, w["w_o"], "attn_do", out_dtype=MXU_DTYPE)
    g_o = _mm_tn(o, dh16, "attn_dwo", tn=WGRAD_TILE, tk=x.shape[0], out_dtype=WIRE_DTYPE)
    g_o4 = g_o.reshape(N_CHIPS, 2, W_O_ROWS // 2, D_MODEL)
    dq, dk, dv, d_bqkv, d_sinks = _ride(hooks_attn, "attn_bwd", lambda hosted: _attn_bwd_call(qkv, sinks, do, hosted))
    dqkv = jnp.concatenate([dq, dk.astype(MXU_DTYPE), dv.astype(MXU_DTYPE)], axis=1)
    g_qkv = _mm_tn(hn_a, dqkv, "attn_dwqkv", tn=WGRAD_TILE, tk=x.shape[0], out_dtype=WIRE_DTYPE)
    dh, dh16, d_gm1 = _attn_dhn_call(dqkv, w["w_qkv"], (h2, gm[1], dh))
    dh, _, d_gf0, g_in, g_out, dcw0, dcb0 = _ffn_bwd(
        dh16, hn_f0, u0, z0, w["w_in"], w["w_out"], cw[0], cb[0], 0, g_in, g_out, (h1, gf[0], dh), hooks=hooks0)
    grad_x, g_pool, dvec = _ride(hooks_pool, "pool_bwd", lambda hosted: _pool_bwd_call(
        x, dh, gm[0], w["pool_w"], pool_b, pool_scale, hosted))

    grads = {
        "w_in": g_in, "w_out": g_out, "w_qkv": g_qkv, "w_o": g_o, "pool_w": g_pool,
        "pool_b": dvec[0:1], "pool_scale": dvec[1:2], "b_qkv": d_bqkv, "sinks": d_sinks[0:1, :N_HEADS],
        "norm_mix": jnp.concatenate([dvec[2:3], d_gm1], axis=0), "norm_ffn": jnp.concatenate([d_gf0, d_gf1], axis=0),
        "conv_w": jnp.stack([dcw0.reshape(3, 2 * D_FF), dcw1.reshape(3, 2 * D_FF)]),
        "conv_b": jnp.stack([dcb0.reshape(2 * D_FF), dcb1.reshape(2 * D_FF)]), "norm_f": d_norm_f,
    }
    return loss, grad_x, grads


W_IN_COLS = 2 * D_FF // N_CHIPS
W_OUT_ROWS = D_FF // N_CHIPS
QKV_COLS = QKV_DIM // N_CHIPS
W_O_ROWS = D_MODEL // N_CHIPS
POOL_ROWS = POOL_GC // N_CHIPS
BIG = ("w_in", "w_out", "w_qkv", "w_o", "pool_w")


def _mesh_place():
    x, y, c = lax.axis_index("x"), lax.axis_index("y"), lax.axis_index("c")
    chips = [(1 - x, y), (x, 1 - y), (1 - x, 1 - y)]
    return x, y, c, chips


def _col(k, width):
    return pl.ds(pl.multiple_of(k * width, 128), width)


def _row(k, height):
    return pl.ds(pl.multiple_of(k * height, 16), height)


def _full_piece(name, ref, k, h):
    if name == "w_in":
        return ref.at[h, :, _col(k, W_IN_COLS)]
    if name == "w_out":
        return ref.at[h, _row(k, W_OUT_ROWS), :]
    if name == "w_qkv":
        return ref.at[_row(h, D_MODEL // 2), _col(k, QKV_COLS)]
    if name == "w_o":
        return ref.at[_row(2 * k + h, W_O_ROWS // 2), :]
    return ref.at[pl.ds(2 * h, 2), _row(k, POOL_ROWS), :]


def _shard_half(name, ref, h):
    if name in ("w_in", "w_out"):
        return ref.at[h]
    if name == "w_qkv":
        return ref.at[_row(h, D_MODEL // 2), :]
    if name == "w_o":
        return ref.at[_row(h, W_O_ROWS // 2), :]
    return ref.at[pl.ds(2 * h, 2)]


FULL_SHAPES = {"w_in": (2, D_MODEL, 2 * D_FF), "w_out": (2, D_FF, D_MODEL), "w_qkv": (D_MODEL, QKV_DIM),
               "w_o": (D_MODEL, D_MODEL), "pool_w": (4, POOL_GC, POOL_GC)}
SHARD_SHAPES = {"w_in": (2, D_MODEL, W_IN_COLS), "w_out": (2, W_OUT_ROWS, D_MODEL), "w_qkv": (D_MODEL, QKV_COLS),
                "w_o": (W_O_ROWS, D_MODEL), "pool_w": (4, POOL_ROWS, POOL_GC)}
HALF_SHAPES = {"w_in": (D_MODEL, W_IN_COLS), "w_out": (W_OUT_ROWS, D_MODEL), "w_qkv": (D_MODEL // 2, QKV_COLS),
               "w_o": (W_O_ROWS // 2, D_MODEL), "pool_w": (2, POOL_ROWS, POOL_GC)}


def _place_shard(name, shard, me_arr):
    if name == "w_in":
        blk, grid = (1, 256, W_IN_COLS), (2, D_MODEL // 256)
        src, dst = (lambda a, b, me: (a, b, 0)), (lambda a, b, me: (a, b, me[0]))
    elif name == "w_out":
        blk, grid = (1, W_OUT_ROWS, D_MODEL), (2, 1)
        src, dst = (lambda a, b, me: (a, 0, 0)), (lambda a, b, me: (a, me[0], 0))
    elif name == "w_qkv":
        blk, grid = (256, QKV_COLS), (1, D_MODEL // 256)
        src, dst = (lambda a, b, me: (b, 0)), (lambda a, b, me: (b, me[0]))
    elif name == "w_o":
        blk, grid = (W_O_ROWS, D_MODEL), (1, 1)
        src, dst = (lambda a, b, me: (0, 0)), (lambda a, b, me: (me[0], 0))
    else:
        blk, grid = (4, POOL_ROWS, POOL_GC), (1, 1)
        src, dst = (lambda a, b, me: (0, 0, 0)), (lambda a, b, me: (0, me[0], 0))

    def body(me_ref, s_ref, o_ref):
        o_ref[...] = s_ref[...].astype(o_ref.dtype)

    return pl.pallas_call(
        body, name=f"place_{name}",
        grid_spec=pltpu.PrefetchScalarGridSpec(num_scalar_prefetch=1, grid=grid, in_specs=[pl.BlockSpec(blk, src)],
                                               out_specs=pl.BlockSpec(blk, dst)),
        out_shape=jax.ShapeDtypeStruct(FULL_SHAPES[name], WIRE_DTYPE), compiler_params=_params("parallel", "parallel"),
    )(me_arr, shard)


def _when_mover(half, c, fn):
    if half is None:
        fn(c)
    else:
        pl.when(c == half)(lambda: fn(half))


def _when_taker(half, c, fn):
    if half is None:
        fn(1 - c)
    else:
        pl.when(c != half)(lambda: fn(half))


def _remote(ref, sems, t, j, dev):
    send, recv = sems
    return pltpu.make_async_remote_copy(src_ref=ref, dst_ref=ref, send_sem=send.at[t, j], recv_sem=recv.at[t, j],
                                        device_id=dev, device_id_type=MESH)


def _gather_ici(items, refs, sems, finish, peers=(0, 1, 2)):
    x, y, c, chips = _mesh_place()
    me = 2 * x + y
    for t, (nm, half) in enumerate(items):
        def go(h, t=t, nm=nm):
            mine = _full_piece(nm, refs[t], me, h)
            for j, (cx, cy) in enumerate(chips):
                if j not in peers:
                    continue
                if not finish:
                    _remote(mine, sems, t, j, (cx, cy, c)).start()
                else:
                    _remote(_full_piece(nm, refs[t], 2 * cx + cy, h), sems, t, j, (cx, cy, c)).wait_recv()
                    _remote(mine, sems, t, j, (cx, cy, c)).wait_send()

        _when_mover(half, c, go)


def _gather_pass(items, refs, sems, finish):
    x, y, c, chips = _mesh_place()
    sibling = (x, y, 1 - c)
    for t, (nm, half) in enumerate(items):
        def give(h, t=t, nm=nm):
            for j, (cx, cy) in enumerate(chips):
                got = _remote(_full_piece(nm, refs[t], 2 * cx + cy, h), sems, t, j, sibling)
                if not finish:
                    got.start()
                else:
                    got.wait_send()

        def take(h, t=t, nm=nm):
            for j, (cx, cy) in enumerate(chips):
                _remote(_full_piece(nm, refs[t], 2 * cx + cy, h), sems, t, j, sibling).wait_recv()

        _when_mover(half, c, give)
        if finish:
            _when_taker(half, c, take)


def _gather_sems(n_items):
    return [pltpu.SemaphoreType.DMA((n_items, 3)), pltpu.SemaphoreType.DMA((n_items, 3))]


EARLY_ITEMS = (("pool_w", None),)
_LATE_A = (("w_out", 0),)
_LATE_B = (("w_in", 1),)
_LATE_C = (("w_out", 1),)
_LATE_D = (("w_qkv", None), ("w_o", None))
_EARLY_W_IN = (("w_in", 0),)
GATHER_RIDES = {"pool_fwd": ((), _EARLY_W_IN), "in0": ((), _LATE_A), "glu0": (_LATE_A, _LATE_D), "out0": (_LATE_D, ()),
                "attn_fwd": ((), _LATE_B), "attn_out": (_LATE_B, ()), "in1": ((), _LATE_C), "glu1": (_LATE_C, ())}
NEIGHBOURS = (0, 1)
GATHER_PEERS = {"pool_fwd": NEIGHBOURS}


def _names(items):
    return [nm for nm, _ in items]


def _gather_early(full, small):
    items = EARLY_ITEMS
    n = len(items)

    def body(*refs):
        small_in = refs[n]
        outs, small_out = refs[n + 1:2 * n + 1], refs[2 * n + 1]
        scratch = refs[2 * n + 2:]
        ici_sems, pass_sems = scratch[0:2], scratch[2:4]
        local_sem, small_send, small_recv = scratch[4:]
        x, y, c, chips = _mesh_place()
        me = 2 * x + y
        local = pltpu.make_async_copy(small_in, small_out.at[me], local_sem.at[0])
        local.start()
        small_sends = [pltpu.make_async_remote_copy(
            src_ref=small_in, dst_ref=small_out.at[me], send_sem=small_send.at[j], recv_sem=small_recv.at[j],
            device_id=(cx, cy, c), device_id_type=MESH) for j, (cx, cy) in enumerate(chips)]
        for cp in small_sends:
            cp.start()
        _gather_ici(items, outs, ici_sems, finish=False)
        _gather_ici(items, outs, ici_sems, finish=True)
        _gather_pass(items, outs, pass_sems, finish=False)
        _gather_pass(items, outs, pass_sems, finish=True)
        for j, (cx, cy) in enumerate(chips):
            got = small_out.at[2 * cx + cy]
            pltpu.make_async_remote_copy(src_ref=got, dst_ref=got, send_sem=small_send.at[j],
                                         recv_sem=small_recv.at[j], device_id=(cx, cy, c), device_id_type=MESH).wait_recv()
        for cp in small_sends:
            cp.wait_send()
        local.wait()

    arrays = [full[nm] for nm in _names(items)]
    out_shape = [jax.ShapeDtypeStruct(a.shape, a.dtype) for a in arrays]
    out_shape.append(jax.ShapeDtypeStruct((N_CHIPS,) + small.shape, F32))
    res = pl.pallas_call(
        body, name="gather_early", in_specs=[ANY] * (n + 1), out_specs=[ANY] * (n + 1), out_shape=out_shape,
        input_output_aliases={t: t for t in range(n)},
        scratch_shapes=_gather_sems(n) + _gather_sems(n) + [pltpu.SemaphoreType.DMA((1,)), pltpu.SemaphoreType.DMA((3,)),
                                                            pltpu.SemaphoreType.DMA((3,))],
    )(*arrays, small)
    return {**full, **dict(zip(_names(items), res[:n]))}, res[n]


def _late_gather_host(stage, w):
    passing, sending = GATHER_RIDES[stage]
    n_pass = len(passing)

    def run(src, refs, new, sems, finish):
        if passing:
            _gather_pass(passing, refs[:n_pass], sems[:2], finish)
        if sending:
            _gather_ici(sending, refs[n_pass:], sems[-2:], finish, GATHER_PEERS.get(stage, (0, 1, 2)))

    sems = (_gather_sems(n_pass) if passing else []) + (_gather_sems(len(sending)) if sending else [])
    return _Hosted([], [w[nm] for nm in _late_names(stage)], [], sems, functools.partial(run, finish=False),
                   functools.partial(run, finish=True))


def _late_names(stage):
    return [nm for items in GATHER_RIDES[stage] for nm in _names(items)]


def _other_half_src(name, ref, h):
    if name in ("w_in", "w_out"):
        return ref.at[h]
    if name == "w_qkv":
        return ref.at[_row(h, D_MODEL // 2), :]
    if name == "w_o":
        return ref.at[:, pl.ds(h, 1)]
    return ref.at[pl.ds(2 * h, 2)]


SLAB_SHAPES = {"w_in": (D_MODEL, 2 * D_FF), "w_out": (D_FF, D_MODEL), "w_qkv": (D_MODEL // 2, QKV_DIM),
               "w_o": (N_CHIPS, 1, W_O_ROWS // 2, D_MODEL), "pool_w": (2, POOL_GC, POOL_GC)}


def _item_tag(item):
    return item[0] if item[1] is None else f"{item[0]}{item[1]}"


def _reduce_swap(items, grads, slabs, sems, finish):
    x, y, c, _ = _mesh_place()
    send, recv = sems
    for t, (nm, half) in enumerate(items):
        def copy(h, t=t, nm=nm):
            return pltpu.make_async_remote_copy(
                src_ref=_other_half_src(nm, grads[t], h), dst_ref=slabs[t], send_sem=send.at[t], recv_sem=recv.at[t],
                device_id=(x, y, 1 - c), device_id_type=MESH)

        def give(h):
            if finish:
                copy(h).wait_send()
            else:
                copy(h).start()

        if half is None:
            give(1 - c)
            if finish:
                copy(c).wait_recv()
        else:
            pl.when(c != half)(lambda: give(half))
            if finish:
                pl.when(c == half)(lambda: copy(half).wait_recv())


def _reduce_scatter(items, sums, parts, sems, finish):
    _, _, c, chips = _mesh_place()
    send, recv = sems
    for t, (nm, half) in enumerate(items):
        def go(h, t=t, nm=nm):
            for j, (cx, cy) in enumerate(chips):
                cp = pltpu.make_async_remote_copy(
                    src_ref=_slab_piece(nm, sums[t], 2 * cx + cy), dst_ref=parts[t].at[j], send_sem=send.at[t, j],
                    recv_sem=recv.at[t, j], device_id=(cx, cy, c), device_id_type=MESH)
                if finish:
                    cp.wait_recv()
                    cp.wait_send()
                else:
                    cp.start()

        _when_mover(half, c, go)


def _reduce_share(items, shards, sems, finish):
    x, y, c, _ = _mesh_place()
    send, recv = sems
    for t, (nm, half) in enumerate(items):
        def copy(h, t=t, nm=nm):
            part = _shard_half(nm, shards[t], h)
            return pltpu.make_async_remote_copy(src_ref=part, dst_ref=part, send_sem=send.at[t], recv_sem=recv.at[t],
                                                device_id=(x, y, 1 - c), device_id_type=MESH)

        def give(h):
            if finish:
                copy(h).wait_send()
            else:
                copy(h).start()

        _when_mover(half, c, give)
        if finish:
            _when_taker(half, c, lambda h: copy(h).wait_recv())


def _pair_sems(n):
    return [pltpu.SemaphoreType.DMA((n,)), pltpu.SemaphoreType.DMA((n,))]


def _swap_host(items, grads):
    def run(src, arr, new, sems, finish):
        _reduce_swap(items, src, new, sems, finish)

    return _Hosted([grads[nm] for nm, _ in items], [],
                   [jax.ShapeDtypeStruct(SLAB_SHAPES[nm], grads[nm].dtype) for nm, _ in items],
                   _pair_sems(len(items)), functools.partial(run, finish=False), functools.partial(run, finish=True))


def _scatter_host(items, sums):
    def run(src, arr, new, sems, finish):
        _reduce_scatter(items, src, new, sems, finish)

    new = [jax.ShapeDtypeStruct((N_CHIPS - 1,) + HALF_SHAPES[nm], WIRE_DTYPE) for nm, _ in items]
    return _Hosted(list(sums), [], new, _gather_sems(len(items)), functools.partial(run, finish=False),
                   functools.partial(run, finish=True))


def _share_host(items, shards):
    def run(src, arr, new, sems, finish):
        _reduce_share(items, arr, sems, finish)

    return _Hosted([], list(shards), [], _pair_sems(len(items)), functools.partial(run, finish=False),
                   functools.partial(run, finish=True))


def _pair_sum(item, grad, slab, place):
    name, half = item
    shape = SLAB_SHAPES[name]
    hsel = (lambda p: p[1]) if half is None else (lambda p: half)
    if name == "w_in":
        blk = (256, 2 * D_FF)
        grid = (D_MODEL // 256,)
        g_spec = pl.BlockSpec((None,) + blk, lambda i, p: (hsel(p), i, 0))
        s_spec = pl.BlockSpec(blk, lambda i, p: (i, 0))
    elif name == "w_out":
        blk = (W_OUT_ROWS, D_MODEL)
        grid = (N_CHIPS,)
        g_spec = pl.BlockSpec((None,) + blk, lambda i, p: (hsel(p), i, 0))
        s_spec = pl.BlockSpec(blk, lambda i, p: (i, 0))
    elif name == "w_qkv":
        grid = (1,)
        g_spec = pl.BlockSpec(shape, lambda i, p: (hsel(p), 0))
        s_spec = pl.BlockSpec(shape, lambda i, p: (0, 0))
    else:
        grid = (1,)
        g_spec = pl.BlockSpec(shape, lambda i, p: (0, hsel(p), 0, 0))
        s_spec = pl.BlockSpec(shape, lambda i, p: (0, 0, 0, 0))

    def body(p_ref, g_ref, s_ref, o_ref):
        @pl.when(hsel(p_ref) == p_ref[1])
        def _():
            o_ref[...] = (g_ref[...].astype(F32) + s_ref[...].astype(F32)).astype(o_ref.dtype)

    return pl.pallas_call(
        body, name=f"pair_sum_{_item_tag(item)}",
        grid_spec=pltpu.PrefetchScalarGridSpec(num_scalar_prefetch=1, grid=grid, in_specs=[g_spec, s_spec],
                                               out_specs=s_spec),
        out_shape=jax.ShapeDtypeStruct(shape, WIRE_DTYPE), compiler_params=_params("parallel"),
    )(place, grad, slab)


def _slab_piece(name, ref, k):
    if name == "w_in":
        return ref.at[:, _col(k, W_IN_COLS)]
    if name == "w_out":
        return ref.at[_row(k, W_OUT_ROWS), :]
    if name == "w_qkv":
        return ref.at[:, _col(k, QKV_COLS)]
    if name == "w_o":
        return ref.at[k, 0]
    return ref.at[:, _row(k, POOL_ROWS), :]


def _sum_chips(item, slab, parts, place, into=None):
    name, sel = item
    half = HALF_SHAPES[name]
    hsel = (lambda p: p[1]) if sel is None else (lambda p: sel)
    if name == "w_in":
        blk, grid = (256, W_IN_COLS), (D_MODEL // 256,)
        own = pl.BlockSpec(blk, lambda i, p: (i, p[0]))
        got = pl.BlockSpec((N_CHIPS - 1,) + blk, lambda i, p: (0, i, 0))
        out = pl.BlockSpec((None,) + blk, lambda i, p: (hsel(p), i, 0))
    elif name == "w_out":
        grid = (1,)
        own = pl.BlockSpec(half, lambda i, p: (p[0], 0))
        got = pl.BlockSpec((N_CHIPS - 1,) + half, lambda i, p: (0, 0, 0))
        out = pl.BlockSpec((None,) + half, lambda i, p: (hsel(p), 0, 0))
    elif name == "w_qkv":
        grid = (1,)
        own = pl.BlockSpec(half, lambda i, p: (0, p[0]))
        got = pl.BlockSpec((N_CHIPS - 1,) + half, lambda i, p: (0, 0, 0))
        out = pl.BlockSpec(half, lambda i, p: (hsel(p), 0))
    else:
        grid = (1,)
        own = pl.BlockSpec((None, None) + half, lambda i, p: (p[0], 0, 0, 0))
        got = pl.BlockSpec((N_CHIPS - 1,) + half, lambda i, p: (0, 0, 0))
        out = pl.BlockSpec(half, lambda i, p: (hsel(p), 0))

    def body(p_ref, own_ref, got_ref, *rest):
        o_ref = rest[-1]

        @pl.when(hsel(p_ref) == p_ref[1])
        def _():
            acc = own_ref[...].astype(F32)
            for j in range(N_CHIPS - 1):
                acc = acc + got_ref[j].astype(F32)
            o_ref[...] = acc

    in_specs, operands, aliases = [own, got], [place, slab, parts], {}
    if into is not None:
        in_specs.append(ANY)
        operands.append(into)
        aliases = {3: 0}
    return pl.pallas_call(
        body, name=f"sum_chips_{_item_tag(item)}",
        grid_spec=pltpu.PrefetchScalarGridSpec(num_scalar_prefetch=1, grid=grid, in_specs=in_specs, out_specs=out),
        out_shape=jax.ShapeDtypeStruct(SHARD_SHAPES[name], F32), input_output_aliases=aliases,
        compiler_params=_params("parallel"),
    )(*operands)


N_DEV = 8


def _peer(d):
    x, y, c, _ = _mesh_place()
    px, py, pc = x ^ (d >> 2), y ^ ((d >> 1) & 1), c ^ (d & 1)
    return (px, py, pc), 4 * px + 2 * py + pc


def _small_all_reduce_host(vec):
    rows, cols = vec.shape

    def run(src, arr, new, scratch, finish):
        slots, stage, send, recv, local = scratch
        x, y, c, _ = _mesh_place()
        me = 4 * x + 2 * y + c
        mine = pltpu.make_async_copy(src[0], slots.at[me], local.at[0])
        copies = []
        for d in range(1, N_DEV):
            dev, _ = _peer(d)
            copies.append(pltpu.make_async_remote_copy(src_ref=src[0], dst_ref=slots.at[me], send_sem=send.at[d - 1],
                                                       recv_sem=recv.at[d - 1], device_id=dev, device_id_type=MESH))
        if not finish:
            mine.start()
            for cp in copies:
                cp.start()
            return
        for d in range(1, N_DEV):
            _, flat = _peer(d)
            got = slots.at[flat]
            pltpu.make_async_remote_copy(src_ref=got, dst_ref=got, send_sem=send.at[d - 1], recv_sem=recv.at[d - 1],
                                         device_id=(x, y, c), device_id_type=MESH).wait_recv()
        for cp in copies:
            cp.wait_send()
        mine.wait()
        acc = slots[0]
        for d in range(1, N_DEV):
            acc = acc + slots[d]
        stage[...] = acc
        out = pltpu.make_async_copy(stage, new[0], local.at[0])
        out.start()
        out.wait()

    scratch = [pltpu.VMEM((N_DEV, rows, cols), F32), pltpu.VMEM((rows, cols), F32),
               pltpu.SemaphoreType.DMA((N_DEV - 1,)), pltpu.SemaphoreType.DMA((N_DEV - 1,)),
               pltpu.SemaphoreType.DMA((1,))]
    return _Hosted([vec], [], [jax.ShapeDtypeStruct(vec.shape, F32)], scratch, functools.partial(run, finish=False),
                   functools.partial(run, finish=True))


class _Reducer:
    FIRST = (("w_in", 1), ("w_out", 1))
    ATT = (("w_qkv", None), ("w_o", None))
    W_IN0 = (("w_in", 0),)
    W_OUT0 = (("w_out", 0),)

    def __init__(self, place):
        self.place = place
        self.shards = {}
        self.state = {}

    def swap(self, items, grads):
        self.state[items] = {"grads": [grads[nm] for nm, _ in items]}
        return _swap_host(items, grads)

    def pair_sums(self, items, slabs):
        st = self.state[items]
        st["sums"] = [_pair_sum(it, g, s, self.place) for it, g, s in zip(items, st["grads"], slabs)]

    def scatter(self, items):
        return _scatter_host(items, self.state[items]["sums"])

    def chip_sums(self, items, parts):
        for it, s, p in zip(items, self.state[items]["sums"], parts):
            self.shards[it[0]] = _sum_chips(it, s, p, self.place, into=self.shards.get(it[0]))

    def share(self, items):
        return _share_host(items, [self.shards[nm] for nm, _ in items])

    def shared(self, items, updated):
        self.shards.update(dict(zip(_names(items), updated)))

    def small_early(self, pieces):
        return _small_all_reduce_host(_pack_small(pieces, EARLY_SMALL))

    def small_early_done(self, total):
        self.early_small = _unpack_small(total, EARLY_SMALL)


def _all_reduce_small(vec, g_pool, share_items, shards):
    n_share = len(share_items)

    def body(v_ref, gp_ref, *rest):
        o_ref, po_ref = rest[n_share:n_share + 2]
        shard_refs = rest[n_share + 2:2 * n_share + 2]
        (slots, pslots, send_sems, recv_sems, psend, precv, hsend, hrecv, share_send,
         share_recv) = rest[2 * n_share + 2:]
        x, y, c, _ = _mesh_place()
        me = 4 * x + 2 * y + c
        _reduce_share(share_items, shard_refs, (share_send, share_recv), finish=False)

        def pool_piece(chip, h):
            return gp_ref.at[pl.ds(2 * h, 2), pl.ds(pl.multiple_of(chip * POOL_ROWS, POOL_ROWS), POOL_ROWS), :]

        slots[me] = v_ref[...]
        pslots[me] = pool_piece(2 * x + y, c)[...]
        copies = []
        for d in range(1, N_DEV):
            peer = (x ^ (d >> 2), y ^ ((d >> 1) & 1), c ^ (d & 1))
            copies.append(pltpu.make_async_remote_copy(
                src_ref=v_ref, dst_ref=slots.at[me], send_sem=send_sems.at[d - 1], recv_sem=recv_sems.at[d - 1],
                device_id=peer, device_id_type=MESH))
            copies.append(pltpu.make_async_remote_copy(
                src_ref=pool_piece(2 * peer[0] + peer[1], peer[2]), dst_ref=pslots.at[me], send_sem=psend.at[d - 1],
                recv_sem=precv.at[d - 1], device_id=peer, device_id_type=MESH))
        for cp in copies:
            cp.start()
        for d in range(1, N_DEV):
            peer = 4 * (x ^ (d >> 2)) + 2 * (y ^ ((d >> 1) & 1)) + (c ^ (d & 1))
            for buf, ss, rs in ((slots, send_sems, recv_sems), (pslots, psend, precv)):
                got = buf.at[peer]
                pltpu.make_async_remote_copy(src_ref=got, dst_ref=got, send_sem=ss.at[d - 1], recv_sem=rs.at[d - 1],
                                             device_id=(x, y, c), device_id_type=MESH).wait_recv()
        for cp in copies:
            cp.wait_send()
        acc, pacc = slots[0], pslots[0]
        for d in range(1, N_DEV):
            acc, pacc = acc + slots[d], pacc + pslots[d]
        o_ref[...] = acc
        po_ref[pl.ds(2 * c, 2)] = pacc
        mine, theirs = po_ref.at[pl.ds(2 * c, 2)], po_ref.at[pl.ds(2 * (1 - c), 2)]
        give = pltpu.make_async_remote_copy(src_ref=mine, dst_ref=mine, send_sem=hsend.at[0], recv_sem=hrecv.at[0],
                                            device_id=(x, y, 1 - c), device_id_type=MESH)
        give.start()
        pltpu.make_async_remote_copy(src_ref=theirs, dst_ref=theirs, send_sem=hsend.at[0], recv_sem=hrecv.at[0],
                                     device_id=(x, y, 1 - c), device_id_type=MESH).wait_recv()
        give.wait_send()
        _reduce_share(share_items, shard_refs, (share_send, share_recv), finish=True)

    vm = pl.BlockSpec(memory_space=pltpu.VMEM)
    piece = (2, POOL_ROWS, POOL_GC)
    return pl.pallas_call(
        body, name="all_reduce_small", in_specs=[vm, vm] + [ANY] * n_share, out_specs=[vm, vm] + [ANY] * n_share,
        out_shape=[jax.ShapeDtypeStruct(vec.shape, F32), jax.ShapeDtypeStruct(SHARD_SHAPES["pool_w"], F32)]
        + [jax.ShapeDtypeStruct(a.shape, a.dtype) for a in shards],
        input_output_aliases={2 + t: 2 + t for t in range(n_share)},
        scratch_shapes=[pltpu.VMEM((N_DEV,) + vec.shape, F32), pltpu.VMEM((N_DEV,) + piece, F32)]
        + [pltpu.SemaphoreType.DMA((N_DEV - 1,))] * 4 + [pltpu.SemaphoreType.DMA((1,))] * 2 + _pair_sems(n_share),
    )(vec, g_pool, *shards)


def _adamw_call(w, g, m, v, name, copy_grad=False):
    shape = w.shape
    cols = shape[-1]
    rows = w.size // cols
    tr = rows
    for cand in (256, 128, 64, 32, 16, 8):
        if rows > cand and rows % cand == 0 and cand * cols * 4 <= 2 * 1024 * 1024:
            tr = cand
            break

    def body(w_ref, g_ref, m_ref, v_ref, d_ref, nm_ref, nv_ref, *g_out):
        gv = g_ref[...]
        if copy_grad:
            g_out[0][...] = gv
        mn = ADAM_B1 * m_ref[...] + (1.0 - ADAM_B1) * gv
        vn = ADAM_B2 * v_ref[...] + (1.0 - ADAM_B2) * jnp.square(gv)
        m_hat = mn / (1.0 - ADAM_B1 ** ADAM_STEP)
        v_hat = vn / (1.0 - ADAM_B2 ** ADAM_STEP)
        d_ref[...] = -ADAM_LR * (m_hat / (jnp.sqrt(v_hat) + ADAM_EPS) + ADAM_WD * w_ref[...])
        nm_ref[...] = mn
        nv_ref[...] = vn

    spec = pl.BlockSpec((tr, cols), lambda i: (i, 0))
    flat = lambda a: a.reshape(rows, cols)
    n_out = 4 if copy_grad else 3
    outs = pl.pallas_call(
        body, name=name, grid=(rows // tr,), in_specs=[spec] * 4, out_specs=[spec] * n_out,
        out_shape=[jax.ShapeDtypeStruct((rows, cols), F32)] * n_out, compiler_params=_params("parallel"),
    )(flat(w), flat(g), flat(m), flat(v))
    return tuple(o.reshape(shape) for o in outs)


WEIGHTS = ("pool_w", "pool_b", "pool_scale", "attn_w_qkv", "attn_b_qkv", "attn_sinks", "attn_w_o", "norm_mix",
           "norm_ffn", "ffn_w_in", "ffn_conv_w", "ffn_conv_b", "ffn_w_out", "norm_f")
SMALL_GATHER = 128 * 71
EARLY_SMALL = ((("b_qkv", 1536), ("sinks", 128), ("norm_mix1", 1024), ("norm_ffn1", 1024), ("conv_w", 6 * 2 * D_FF),
                ("conv_b", 4 * D_FF), ("norm_f", 1024), ("loss", 128)), 6272)
LATE_SMALL = ((("pool_b", 1024), ("pool_scale", 1024), ("norm_mix0", 1024), ("norm_ffn0", 1024)), 512)


def _pack_small(pieces, layout):
    sizes, cols = layout
    flat = jnp.concatenate([jnp.pad(pieces[nm].reshape(-1), (0, size - pieces[nm].size)) for nm, size in sizes])
    return jnp.pad(flat, (0, 8 * cols - flat.size)).reshape(8, cols)


def _unpack_small(vec, layout):
    flat = vec.reshape(-1)
    out, off = {}, 0
    for nm, size in layout[0]:
        out[nm] = flat[off:off + size]
        off += size
    return out


def kernel(x, pool_w, pool_b, pool_scale, attn_w_qkv, attn_b_qkv, attn_sinks, attn_w_o, norm_mix, norm_ffn, ffn_w_in, ffn_conv_w, ffn_conv_b, ffn_w_out, norm_f, loss_target, m_pool_w, m_pool_b, m_pool_scale, m_attn_w_qkv, m_attn_b_qkv, m_attn_sinks, m_attn_w_o, m_norm_mix, m_norm_ffn, m_ffn_w_in, m_ffn_conv_w, m_ffn_conv_b, m_ffn_w_out, m_norm_f, v_pool_w, v_pool_b, v_pool_scale, v_attn_w_qkv, v_attn_b_qkv, v_attn_sinks, v_attn_w_o, v_norm_mix, v_norm_ffn, v_ffn_w_in, v_ffn_conv_w, v_ffn_conv_b, v_ffn_w_out, v_norm_f):
    weights = dict(pool_w=pool_w, pool_b=pool_b, pool_scale=pool_scale, attn_w_qkv=attn_w_qkv, attn_b_qkv=attn_b_qkv,
                   attn_sinks=attn_sinks, attn_w_o=attn_w_o, norm_mix=norm_mix, norm_ffn=norm_ffn, ffn_w_in=ffn_w_in,
                   ffn_conv_w=ffn_conv_w, ffn_conv_b=ffn_conv_b, ffn_w_out=ffn_w_out, norm_f=norm_f)
    m_in = dict(pool_w=m_pool_w, pool_b=m_pool_b, pool_scale=m_pool_scale, attn_w_qkv=m_attn_w_qkv,
                attn_b_qkv=m_attn_b_qkv, attn_sinks=m_attn_sinks, attn_w_o=m_attn_w_o, norm_mix=m_norm_mix,
                norm_ffn=m_norm_ffn, ffn_w_in=m_ffn_w_in, ffn_conv_w=m_ffn_conv_w, ffn_conv_b=m_ffn_conv_b,
                ffn_w_out=m_ffn_w_out, norm_f=m_norm_f)
    v_in = dict(pool_w=v_pool_w, pool_b=v_pool_b, pool_scale=v_pool_scale, attn_w_qkv=v_attn_w_qkv,
                attn_b_qkv=v_attn_b_qkv, attn_sinks=v_attn_sinks, attn_w_o=v_attn_w_o, norm_mix=v_norm_mix,
                norm_ffn=v_norm_ffn, ffn_w_in=v_ffn_w_in, ffn_conv_w=v_ffn_conv_w, ffn_conv_b=v_ffn_conv_b,
                ffn_w_out=v_ffn_w_out, norm_f=v_norm_f)
    chip = 2 * lax.axis_index("x") + lax.axis_index("y")
    core = lax.axis_index("c")

    place = jnp.stack([chip, core]).astype(jnp.int32)
    shards = {"w_in": ffn_w_in, "w_out": ffn_w_out, "w_qkv": attn_w_qkv[0], "w_o": attn_w_o[0], "pool_w": pool_w[0]}
    placed = {nm: _place_shard(nm, shards[nm], place) for nm in BIG}
    small = jnp.concatenate([pool_b.reshape(-1), attn_b_qkv.reshape(-1), ffn_conv_w.reshape(-1)]).reshape(1, SMALL_GATHER)
    full, small_all = _gather_early(placed, small)
    small_all = small_all.reshape(N_CHIPS, SMALL_GATHER)
    pool_b_full = small_all[:, :256].reshape(N_CHIPS, 4, POOL_ROWS).transpose(1, 0, 2).reshape(1, D_MODEL)
    b_qkv_full = small_all[:, 256:640].reshape(1, QKV_DIM)
    conv_w_full = small_all[:, 640:].reshape(N_CHIPS, 2, 3, W_IN_COLS).transpose(1, 2, 0, 3).reshape(2, 3, 2 * D_FF)

    reducer = _Reducer(place)
    _, grad_x, grads = _local_step(
        x[0], loss_target[0], full, norm_mix=norm_mix, norm_ffn=norm_ffn, norm_f=norm_f.reshape(1, D_MODEL),
        pool_b=pool_b_full, pool_scale=pool_scale, b_qkv=b_qkv_full, sinks=attn_sinks[0], conv_w=conv_w_full,
        conv_b=ffn_conv_b, late_gather=_late_gather_host, reducer=reducer)
    reduced = dict(reducer.shards)

    late = {"pool_b": grads["pool_b"], "pool_scale": grads["pool_scale"], "norm_mix0": grads["norm_mix"][0],
            "norm_ffn0": grads["norm_ffn"][0]}
    last = _Reducer.W_OUT0
    late_sum, reduced["pool_w"], *shared = _all_reduce_small(_pack_small(late, LATE_SMALL), grads["pool_w"], last,
                                                             [reduced[nm] for nm in _names(last)])
    reduced.update(zip(_names(last), shared))
    small_g = {**reducer.early_small, **_unpack_small(late_sum, LATE_SMALL)}
    small_g["norm_mix"] = jnp.concatenate([small_g["norm_mix0"], small_g["norm_mix1"]])
    small_g["norm_ffn"] = jnp.concatenate([small_g["norm_ffn0"], small_g["norm_ffn1"]])
    loss = small_g["loss"][0]
    pool_b_g = lax.dynamic_slice_in_dim(small_g["pool_b"].reshape(4, N_CHIPS, POOL_ROWS), chip, 1, axis=1)
    b_qkv_g = lax.dynamic_slice_in_dim(small_g["b_qkv"].reshape(N_CHIPS, QKV_COLS), chip, 1, axis=0)
    conv_w_g = lax.dynamic_slice_in_dim(small_g["conv_w"].reshape(2, 3, N_CHIPS, W_IN_COLS), chip, 1, axis=2)
    grad_w = {
        "pool_w": reduced["pool_w"].reshape(pool_w.shape), "pool_b": pool_b_g.reshape(pool_b.shape),
        "pool_scale": small_g["pool_scale"].reshape(pool_scale.shape),
        "attn_w_qkv": reduced["w_qkv"].reshape(attn_w_qkv.shape), "attn_b_qkv": b_qkv_g.reshape(attn_b_qkv.shape),
        "attn_sinks": small_g["sinks"][:N_HEADS].reshape(attn_sinks.shape),
        "attn_w_o": reduced["w_o"].reshape(attn_w_o.shape), "norm_mix": small_g["norm_mix"].reshape(norm_mix.shape),
        "norm_ffn": small_g["norm_ffn"].reshape(norm_ffn.shape), "ffn_w_in": reduced["w_in"],
        "ffn_conv_w": conv_w_g.reshape(ffn_conv_w.shape), "ffn_conv_b": small_g["conv_b"].reshape(ffn_conv_b.shape),
        "ffn_w_out": reduced["w_out"], "norm_f": small_g["norm_f"].reshape(norm_f.shape),
    }

    delta, new_m, new_v = {}, {}, {}
    exchanged = ("pool_w", "attn_w_qkv", "attn_w_o", "ffn_w_in", "ffn_w_out")
    for nm in WEIGHTS:
        res = _adamw_call(weights[nm], grad_w[nm], m_in[nm], v_in[nm], f"adamw_{nm}", copy_grad=nm in exchanged)
        delta[nm], new_m[nm], new_v[nm] = res[:3]
        if nm in exchanged:
            grad_w[nm] = res[3]
    return (loss, grad_x.reshape(x.shape), *[grad_w[nm] for nm in WEIGHTS], *[delta[nm] for nm in WEIGHTS],
            *[new_m[nm] for nm in WEIGHTS], *[new_v[nm] for nm in WEIGHTS])
```

```python
import functools
from typing import Callable, NamedTuple

import jax
import jax.numpy as jnp
from jax import lax
from jax.experimental import pallas as pl
from jax.experimental.pallas import tpu as pltpu

F32 = jnp.float32
MXU_DTYPE = jnp.bfloat16
WIRE_DTYPE = jnp.bfloat16
U_DTYPE = jnp.float32

D_MODEL = 1024
D_FF = 2816
QKV_DIM = 1536
HEAD_DIM = 64
N_HEADS = 16
N_KV_HEADS = 4
ATT_BLOCK = 128
POOL_WINDOWS = (2, 4, 8, 16)
POOL_GC = 256
POOL_HALO = 16
CONV_HALO = 8
RMS_EPS = 1e-6
ATT_SCALE = HEAD_DIM ** -0.5
ALIBI_SLOPES = tuple(2.0 ** (-8.0 / N_HEADS * (h + 1)) for h in range(N_HEADS))

ADAM_LR = 0.001
ADAM_B1 = 0.9
ADAM_B2 = 0.999
ADAM_EPS = 1e-08
ADAM_WD = 0.01
ADAM_STEP = 10

N_CHIPS = 4
MESH = pl.DeviceIdType.MESH
VMEM_LIMIT_BYTES = 56 * 1024 * 1024
ANY = pl.BlockSpec(memory_space=pl.ANY)


def _params(*semantics):
    return pltpu.CompilerParams(dimension_semantics=semantics, vmem_limit_bytes=VMEM_LIMIT_BYTES)


def _rms(x, g):
    return x * lax.rsqrt(jnp.mean(x * x, axis=-1, keepdims=True) + RMS_EPS) * g


def _rms_bwd(x, g, dy):
    rstd = lax.rsqrt(jnp.mean(x * x, axis=-1, keepdims=True) + RMS_EPS)
    xhat = x * rstd
    dxhat = dy * g
    dx = rstd * (dxhat - xhat * jnp.mean(dxhat * xhat, axis=-1, keepdims=True))
    return dx, dy * xhat


def _shift_down(x, s):
    return pltpu.roll(x, s, axis=0)


def _shift_up(x, s):
    return pltpu.roll(x, x.shape[0] - s, axis=0)


def _sigmoid(x):
    return 1.0 / (1.0 + jnp.exp(-x))


_DOT_DIMS = {"nn": ((1,), (0,)), "nt": ((1,), (1,)), "tn": ((0,), (0,))}


class _Hosted(NamedTuple):
    sources: list
    arrays: list
    new: list
    sems: list
    first: Callable
    last: Callable


def _call(body, hosted, *, name, grid, in_specs, out_specs, out_shape, operands, semantics, scratch_shapes=(),
          aliases=None):
    n_in, n_out, n_scr = len(in_specs), len(out_specs), len(scratch_shapes)
    aliases = dict(aliases or {})
    if hosted is None:
        def plain(*refs):
            body(refs[:n_in], refs[n_in:n_in + n_out], refs[n_in + n_out:], lambda: None, lambda: None)

        return pl.pallas_call(plain, name=name, grid=grid, in_specs=list(in_specs), out_specs=list(out_specs),
                              out_shape=list(out_shape), scratch_shapes=list(scratch_shapes),
                              input_output_aliases=aliases, compiler_params=_params(*semantics))(*operands)
    ns, na, nn = len(hosted.sources), len(hosted.arrays), len(hosted.new)

    def carrying(*refs):
        ins, src = refs[:n_in], refs[n_in:n_in + ns]
        o0 = n_in + ns + na
        outs, arr = refs[o0:o0 + n_out], refs[o0 + n_out:o0 + n_out + na]
        new = refs[o0 + n_out + na:o0 + n_out + na + nn]
        s0 = o0 + n_out + na + nn
        scratch, sems = refs[s0:s0 + n_scr], refs[s0 + n_scr:]
        ids = [pl.program_id(d) for d in range(len(grid))]
        is_first = functools.reduce(lambda p, q: p & q, [i == 0 for i in ids])
        is_last = functools.reduce(lambda p, q: p & q, [i == n - 1 for i, n in zip(ids, grid)])
        body(ins, outs, scratch, lambda: pl.when(is_first)(lambda: hosted.first(src, arr, new, sems)),
             lambda: pl.when(is_last)(lambda: hosted.last(src, arr, new, sems)))

    for t in range(na):
        aliases[n_in + ns + t] = n_out + t
    return pl.pallas_call(
        carrying, name=name, grid=grid, in_specs=list(in_specs) + [ANY] * (ns + na),
        out_specs=list(out_specs) + [ANY] * (na + nn),
        out_shape=list(out_shape) + [jax.ShapeDtypeStruct(a.shape, a.dtype) for a in hosted.arrays] + list(hosted.new),
        scratch_shapes=list(scratch_shapes) + list(hosted.sems), input_output_aliases=aliases,
        compiler_params=_params(*["arbitrary"] * len(grid)))(*operands, *hosted.sources, *hosted.arrays)


def _both(h1, h2):
    def run(which, src, arr, new, sems):
        cut = lambda seq, n: (seq[:n], seq[n:])
        (s1, s2), (a1, a2) = cut(src, len(h1.sources)), cut(arr, len(h1.arrays))
        (n1, n2), (m1, m2) = cut(new, len(h1.new)), cut(sems, len(h1.sems))
        getattr(h1, which)(s1, a1, n1, m1)
        getattr(h2, which)(s2, a2, n2, m2)

    return _Hosted(h1.sources + h2.sources, h1.arrays + h2.arrays, h1.new + h2.new, h1.sems + h2.sems,
                   functools.partial(run, "first"), functools.partial(run, "last"))


def _ride(hooks, stage, call, env=None):
    if not hooks or stage not in hooks:
        res = call(None)
        return list(res) if isinstance(res, (list, tuple)) else [res]
    make, done = hooks[stage]
    hosted = make(env)
    res = call(hosted)
    own = len(res) - len(hosted.arrays) - len(hosted.new)
    done(res[own:])
    return list(res[:own])


def _matmul(a, b, *, mode, name, grid, a_spec, b_spec, o_spec, out_shape, bias=None, bias_spec=None, residual=None,
            residual_spec=None, into=None, hosted=None, norm_g=None):
    nk = grid[2]
    dims = (_DOT_DIMS[mode], ((), ()))
    acc_shape = tuple(d for d in o_spec.block_shape if d is not None)

    def body(ins, outs, scratch, first, last):
        a_ref, b_ref = ins[0], ins[1]
        pos = 2
        bias_ref = res_ref = None
        if bias is not None:
            bias_ref = ins[pos]
            pos += 1
        if residual is not None:
            res_ref = ins[pos]
            pos += 1
        o_ref = outs[0]
        acc_ref = scratch[0] if nk > 1 else None
        first()
        prod = lax.dot_general(a_ref[...].astype(MXU_DTYPE), b_ref[...].astype(MXU_DTYPE), dims,
                               preferred_element_type=F32)

        def finish(acc):
            if bias_ref is not None:
                acc = acc + bias_ref[...]
            if res_ref is not None:
                acc = acc + res_ref[...]
            o_ref[...] = acc.astype(o_ref.dtype)
            if norm_g is not None:
                outs[1][...] = _rms(acc, ins[pos][...]).astype(outs[1].dtype)

        if nk == 1:
            finish(prod)
        else:
            k = pl.program_id(2)

            @pl.when(k == 0)
            def _():
                acc_ref[...] = prod

            @pl.when(k > 0)
            def _():
                acc_ref[...] += prod

            @pl.when(k == nk - 1)
            def _():
                finish(acc_ref[...])

        last()

    operands, in_specs = [a, b], [a_spec, b_spec]
    if bias is not None:
        operands.append(bias)
        in_specs.append(bias_spec)
    if residual is not None:
        operands.append(residual)
        in_specs.append(residual_spec)
    out_specs, out_shapes = [o_spec], [out_shape]
    if norm_g is not None:
        assert acc_shape[-1] == norm_g.shape[-1], "the fused norm needs whole rows in an output tile"
        operands.append(norm_g)
        in_specs.append(bias_spec)
        out_specs.append(o_spec)
        out_shapes.append(jax.ShapeDtypeStruct(out_shape.shape, MXU_DTYPE))
    aliases = {}
    if into is not None:
        aliases = {len(operands): 0}
        operands.append(into)
        in_specs.append(ANY)
    res = _call(body, hosted, name=name, grid=grid, in_specs=in_specs, out_specs=out_specs, out_shape=out_shapes,
                operands=operands, semantics=("parallel", "parallel", "arbitrary"),
                scratch_shapes=[pltpu.VMEM(acc_shape, F32)] if nk > 1 else [], aliases=aliases)
    return res if hosted or norm_g is not None else res[0]


def _tile(n, want):
    t = min(n, want)
    assert n % t == 0, (n, want)
    return t


def _mm_nn(a, b, name, *, b_lead=None, bias=None, residual=None, out_dtype=F32, tm=1024, tn=1024, tk=1024,
           hosted=None, norm_g=None):
    (m, k), n = a.shape, b.shape[-1]
    tm, tn, tk = _tile(m, tm), _tile(n, tn), _tile(k, tk)
    if b_lead is None:
        b_spec = pl.BlockSpec((tk, tn), lambda i, j, kk: (kk, j))
    else:
        b_spec = pl.BlockSpec((None, tk, tn), lambda i, j, kk: (b_lead, kk, j))
    return _matmul(
        a, b, mode="nn", name=name, grid=(m // tm, n // tn, k // tk),
        a_spec=pl.BlockSpec((tm, tk), lambda i, j, kk: (i, kk)), b_spec=b_spec,
        o_spec=pl.BlockSpec((tm, tn), lambda i, j, kk: (i, j)), out_shape=jax.ShapeDtypeStruct((m, n), out_dtype),
        bias=bias, bias_spec=pl.BlockSpec((1, tn), lambda i, j, kk: (0, j)),
        residual=residual, residual_spec=pl.BlockSpec((tm, tn), lambda i, j, kk: (i, j)), hosted=hosted,
        norm_g=norm_g)


def _mm_nt(a, b, name, *, b_lead=None, out_dtype=F32, tm=1024, tn=1024, tk=1024, hosted=None):
    (m, k), n = a.shape, b.shape[-2]
    tm, tn, tk = _tile(m, tm), _tile(n, tn), _tile(k, tk)
    if b_lead is None:
        b_spec = pl.BlockSpec((tn, tk), lambda i, j, kk: (j, kk))
    else:
        b_spec = pl.BlockSpec((None, tn, tk), lambda i, j, kk: (b_lead, j, kk))
    return _matmul(
        a, b, mode="nt", name=name, grid=(m // tm, n // tn, k // tk),
        a_spec=pl.BlockSpec((tm, tk), lambda i, j, kk: (i, kk)), b_spec=b_spec,
        o_spec=pl.BlockSpec((tm, tn), lambda i, j, kk: (i, j)), out_shape=jax.ShapeDtypeStruct((m, n), out_dtype),
        hosted=hosted)


def _mm_tn(a, b, name, *, lead=None, n_lead=None, into=None, tm=1024, tn=1024, tk=1024, out_dtype=F32, hosted=None):
    (k, m), n = a.shape, b.shape[-1]
    tm, tn, tk = _tile(m, tm), _tile(n, tn), _tile(k, tk)
    if lead is None:
        o_spec = pl.BlockSpec((tm, tn), lambda i, j, kk: (i, j))
        out_shape = jax.ShapeDtypeStruct((m, n), out_dtype)
    else:
        o_spec = pl.BlockSpec((None, tm, tn), lambda i, j, kk: (lead, i, j))
        out_shape = jax.ShapeDtypeStruct((n_lead, m, n), out_dtype)
    return _matmul(
        a, b, mode="tn", name=name, grid=(m // tm, n // tn, k // tk),
        a_spec=pl.BlockSpec((tk, tm), lambda i, j, kk: (kk, i)),
        b_spec=pl.BlockSpec((tk, tn), lambda i, j, kk: (kk, j)), o_spec=o_spec, out_shape=out_shape, into=into,
        hosted=hosted)


ROW_TILE = 512


def _pool_windows_causal(ext, first_row):
    tt = ext.shape[0] - POOL_HALO
    t = first_row + lax.broadcasted_iota(jnp.int32, (tt, 1), 0)
    outs = []
    for gi, win in enumerate(POOL_WINDOWS):
        cols = slice(gi * POOL_GC, (gi + 1) * POOL_GC)
        s = ext[:, cols]
        sh = 1
        while sh < win:
            s = s + _shift_down(s, sh)
            sh *= 2
        count = jnp.minimum(t + 1, win).astype(F32)
        outs.append(s[POOL_HALO:] / count - ext[POOL_HALO:, cols])
    return outs


def _prev_halo_spec(tt, halo, width):
    per = tt // halo
    return pl.BlockSpec((halo, width), lambda i: (jnp.maximum(i * per - 1, 0), 0))


def _pool_fwd_call(h0, g_mix, pool_w, pool_b, pool_scale, g_ffn, hosted=None):
    t = h0.shape[0]
    tt = _tile(t, ROW_TILE)

    def body(ins, outs, scratch, first, last):
        h_ref, halo_ref, gm_ref, w_ref, b_ref, sc_ref, gf_ref = ins
        h1_ref, hn_ref = outs
        i = pl.program_id(0)
        first()
        x, gm = h_ref[...], gm_ref[...]
        top = jnp.where(i > 0, _rms(halo_ref[...], gm), 0.0)
        ext = jnp.concatenate([top, _rms(x, gm)], axis=0)
        ps = _pool_windows_causal(ext, i * tt)
        ys = [jnp.dot(p.astype(MXU_DTYPE), w_ref[gi], preferred_element_type=F32) for gi, p in enumerate(ps)]
        mix = (jnp.concatenate(ys, axis=1) + b_ref[...]) * sc_ref[...]
        h1 = x + mix
        h1_ref[...] = h1
        hn_ref[...] = _rms(h1, gf_ref[...]).astype(hn_ref.dtype)
        last()

    row = pl.BlockSpec((tt, D_MODEL), lambda i: (i, 0))
    vec = pl.BlockSpec((1, D_MODEL), lambda i: (0, 0))
    wsp = pl.BlockSpec((len(POOL_WINDOWS), POOL_GC, POOL_GC), lambda i: (0, 0, 0))
    return _call(
        body, hosted, name="pool_fwd", grid=(t // tt,),
        in_specs=[row, _prev_halo_spec(tt, POOL_HALO, D_MODEL), vec, wsp, vec, vec, vec], out_specs=[row, row],
        out_shape=[jax.ShapeDtypeStruct(h0.shape, F32), jax.ShapeDtypeStruct(h0.shape, MXU_DTYPE)],
        operands=(h0, h0, g_mix, pool_w, pool_b, pool_scale, g_ffn), semantics=("parallel",))


def _pool_bwd_call(h0, dh1, g_mix, pool_w, pool_b, pool_scale, hosted=None):
    t = h0.shape[0]
    tt = _tile(t, ROW_TILE)
    nt = t // tt
    per = tt // POOL_HALO

    def body(ins, outs, scratch, first, last):
        h_ref, halo_ref, d_ref, dnext_ref, gm_ref, w_ref, b_ref, sc_ref = ins
        gx_ref, dw_ref, dv_ref = outs
        i = pl.program_id(0)
        first()
        x, gm, sc = h_ref[...], gm_ref[...], sc_ref[...]
        top = jnp.where(i > 0, _rms(halo_ref[...], gm), 0.0)
        ext = jnp.concatenate([top, _rms(x, gm)], axis=0)
        ps = _pool_windows_causal(ext, i * tt)

        dy = d_ref[...]
        dy_ext = jnp.concatenate([dy, jnp.where(i < nt - 1, dnext_ref[...], 0.0)], axis=0)
        dpre_ext = dy_ext * sc
        t_ext = i * tt + lax.broadcasted_iota(jnp.int32, (tt + POOL_HALO, 1), 0)

        @pl.when(i == 0)
        def _():
            dw_ref[...] = jnp.zeros_like(dw_ref)
            dv_ref[...] = jnp.zeros_like(dv_ref)

        dhn, ypre = [], []
        for gi, win in enumerate(POOL_WINDOWS):
            cols = slice(gi * POOL_GC, (gi + 1) * POOL_GC)
            w = w_ref[gi]
            p16 = ps[gi].astype(MXU_DTYPE)
            dpre16 = dpre_ext[:, cols].astype(MXU_DTYPE)
            ypre.append(jnp.dot(p16, w, preferred_element_type=F32))
            dw_ref[gi] += lax.dot_general(p16, dpre16[:tt], (((0,), (0,)), ((), ())), preferred_element_type=F32)
            dp_ext = lax.dot_general(dpre16, w, (((1,), (1,)), ((), ())), preferred_element_type=F32)
            s = dp_ext / jnp.minimum(t_ext + 1, win).astype(F32)
            sh = 1
            while sh < win:
                s = s + _shift_up(s, sh)
                sh *= 2
            dhn.append(s[:tt] - dp_ext[:tt])
        dhn = jnp.concatenate(dhn, axis=1)
        ypre = jnp.concatenate(ypre, axis=1) + b_ref[...]
        dx, dgt = _rms_bwd(x, gm, dhn)
        gx_ref[...] = dy + dx
        dv_ref[0:1, :] += jnp.sum(dpre_ext[:tt], axis=0, keepdims=True)
        dv_ref[1:2, :] += jnp.sum(dy * ypre, axis=0, keepdims=True)
        dv_ref[2:3, :] += jnp.sum(dgt, axis=0, keepdims=True)
        last()

    row = pl.BlockSpec((tt, D_MODEL), lambda i: (i, 0))
    vec = pl.BlockSpec((1, D_MODEL), lambda i: (0, 0))
    nxt = pl.BlockSpec((POOL_HALO, D_MODEL), lambda i: (jnp.minimum((i + 1) * per, t // POOL_HALO - 1), 0))
    wsp = pl.BlockSpec((len(POOL_WINDOWS), POOL_GC, POOL_GC), lambda i: (0, 0, 0))
    return _call(
        body, hosted, name="pool_bwd", grid=(nt,),
        in_specs=[row, _prev_halo_spec(tt, POOL_HALO, D_MODEL), row, nxt, vec, wsp, vec, vec],
        out_specs=[row, wsp, pl.BlockSpec((8, D_MODEL), lambda i: (0, 0))],
        out_shape=[jax.ShapeDtypeStruct(h0.shape, F32),
                   jax.ShapeDtypeStruct((len(POOL_WINDOWS), POOL_GC, POOL_GC), F32),
                   jax.ShapeDtypeStruct((8, D_MODEL), F32)],
        operands=(h0, h0, dh1, dh1, g_mix, pool_w, pool_b, pool_scale), semantics=("arbitrary",))


CONV_ROWS = 512
CONV_SUB = 128
LANES = 128


def _causal_conv(u_ext, w, b):
    return w[0:1] * _shift_down(u_ext, 2) + w[1:2] * _shift_down(u_ext, 1) + w[2:3] * u_ext + b


def _piece_rows(load, s, sub, tt, top, bottom=None):
    lo, hi = s * sub - CONV_HALO, (s + 1) * sub + (CONV_HALO if bottom else 0)
    parts = [top()] if lo < 0 else []
    lo = max(lo, 0)
    parts.append(load(lo, min(hi, tt) - lo))
    if hi > tt:
        parts.append(bottom())
    return parts[0] if len(parts) == 1 else jnp.concatenate(parts, axis=0)


def _fold8(x):
    acc = x[0:8]
    for r in range(8, x.shape[0], 8):
        acc = acc + x[r:r + 8]
    return acc


def _conv_glu_fwd_call(u, conv_w, conv_b, name, hosted=None):
    _, t, f = u.shape
    tt = _tile(t, CONV_ROWS)
    sub = _tile(tt, CONV_SUB)
    per = tt // CONV_HALO

    def body(ins, outs, scratch, first, last):
        u_ref, up_ref, w_ref, b_ref = ins
        z_ref = outs[0]
        i = pl.program_id(0)
        first()

        def chunk(j, carry):
            lanes = pl.ds(pl.multiple_of(j * LANES, LANES), LANES)
            w, b = w_ref[:, :, lanes], b_ref[:, lanes]
            for s in range(tt // sub):
                c = []
                for hf in range(2):
                    ext = _piece_rows(lambda at, n: u_ref[hf, pl.ds(at, n), lanes].astype(F32), s, sub, tt,
                                      top=lambda: jnp.where(i > 0, up_ref[hf, :, lanes].astype(F32), 0.0))
                    c.append(_causal_conv(ext, w[:, hf, :], b[hf:hf + 1, :])[CONV_HALO:])
                z_ref[pl.ds(s * sub, sub), lanes] = (c[0] * c[1] * _sigmoid(c[1])).astype(z_ref.dtype)
            return carry

        lax.fori_loop(0, f // LANES, chunk, 0)
        last()

    res = _call(
        body, hosted, name=name, grid=(t // tt,),
        in_specs=[pl.BlockSpec((2, tt, f), lambda i: (0, i, 0)),
                  pl.BlockSpec((2, CONV_HALO, f), lambda i: (0, jnp.maximum(i * per - 1, 0), 0)),
                  pl.BlockSpec((3, 2, f), lambda i: (0, 0, 0)), pl.BlockSpec((2, f), lambda i: (0, 0))],
        out_specs=[pl.BlockSpec((tt, f), lambda i: (i, 0))], out_shape=[jax.ShapeDtypeStruct((t, f), MXU_DTYPE)],
        operands=(u, u, conv_w, conv_b), semantics=("parallel",))
    return res if hosted else res[0]


def _conv_glu_bwd_call(u, dz, conv_w, conv_b, name, hosted=None):
    _, t, f = u.shape
    tt = _tile(t, CONV_ROWS)
    sub = _tile(tt, CONV_SUB)
    nt = t // tt
    per = tt // CONV_HALO
    halo = CONV_HALO

    def body(ins, outs, scratch, first, last):
        u_ref, up_ref, un_ref, dz_ref, dzn_ref, w_ref, b_ref = ins
        du_ref, dw_ref, db_ref = outs
        dwacc, dbacc = scratch
        i = pl.program_id(0)
        first()

        @pl.when(i == 0)
        def _():
            dwacc[...] = jnp.zeros_like(dwacc)
            dbacc[...] = jnp.zeros_like(dbacc)

        main = slice(halo, halo + sub)

        def chunk(j, carry):
            lanes = pl.ds(pl.multiple_of(j * LANES, LANES), LANES)
            w, b = w_ref[:, :, lanes], b_ref[:, lanes]
            for s in range(tt // sub):
                ue = [_piece_rows(lambda at, n: u_ref[hf, pl.ds(at, n), lanes].astype(F32), s, sub, tt,
                                  top=lambda: jnp.where(i > 0, up_ref[hf, :, lanes].astype(F32), 0.0),
                                  bottom=lambda: un_ref[hf, :, lanes].astype(F32)) for hf in range(2)]
                a, g = [_causal_conv(ue[hf], w[:, hf, :], b[hf:hf + 1, :])[halo:] for hf in range(2)]
                sg = _sigmoid(g)
                if (s + 1) * sub < tt:
                    dzs = dz_ref[pl.ds(s * sub, sub + 2 * halo), lanes].astype(F32)[:sub + halo]
                else:
                    dzs = jnp.concatenate([dz_ref[pl.ds(s * sub, sub), lanes].astype(F32),
                                           jnp.where(i < nt - 1, dzn_ref[:, lanes].astype(F32), 0.0)], axis=0)
                gs = g * sg
                dc = [dzs * gs, dzs * a * (sg + gs - gs * sg)]
                for hf in range(2):
                    d = dc[hf]
                    d0, d1, d2 = d[:sub], _shift_up(d, 1)[:sub], _shift_up(d, 2)[:sub]
                    du = w[2:3, hf, :] * d0 + w[1:2, hf, :] * d1 + w[0:1, hf, :] * d2
                    du_ref[hf, pl.ds(s * sub, sub), lanes] = du.astype(du_ref.dtype)
                    um = ue[hf][main]
                    for k, dd in enumerate((d2, d1, d0)):
                        dwacc[k, hf, :, lanes] += _fold8(dd * um)
                    dbacc[hf, :, lanes] += _fold8(d0)
            return carry

        lax.fori_loop(0, f // LANES, chunk, 0)

        @pl.when(i == nt - 1)
        def _():
            dw_ref[...] = jnp.sum(dwacc[...], axis=2)
            db_ref[...] = jnp.sum(dbacc[...], axis=1)

        last()

    end = t // halo - 1
    return _call(
        body, hosted, name=name, grid=(nt,),
        in_specs=[pl.BlockSpec((2, tt, f), lambda i: (0, i, 0)),
                  pl.BlockSpec((2, halo, f), lambda i: (0, jnp.maximum(i * per - 1, 0), 0)),
                  pl.BlockSpec((2, halo, f), lambda i: (0, jnp.minimum((i + 1) * per, end), 0)),
                  pl.BlockSpec((tt, f), lambda i: (i, 0)),
                  pl.BlockSpec((halo, f), lambda i: (jnp.minimum((i + 1) * per, end), 0)),
                  pl.BlockSpec((3, 2, f), lambda i: (0, 0, 0)), pl.BlockSpec((2, f), lambda i: (0, 0))],
        out_specs=[pl.BlockSpec((2, tt, f), lambda i: (0, i, 0)),
                   pl.BlockSpec((3, 2, f), lambda i: (0, 0, 0)), pl.BlockSpec((2, f), lambda i: (0, 0))],
        out_shape=[jax.ShapeDtypeStruct(u.shape, MXU_DTYPE), jax.ShapeDtypeStruct((3, 2, f), F32),
                   jax.ShapeDtypeStruct((2, f), F32)],
        operands=(u, u, u, dz, dz, conv_w, conv_b), semantics=("arbitrary",),
        scratch_shapes=[pltpu.VMEM((3, 2, 8, f), F32), pltpu.VMEM((2, 8, f), F32)])


def _kv_even_odd(band, k):
    pair, parity = divmod(k, 2)
    blk = band[:, 128 * pair:128 * (pair + 1)].astype(F32)
    lane = lax.broadcasted_iota(jnp.int32, blk.shape, 1)
    own = jnp.where((lane >= HEAD_DIM) == (parity == 1), blk, 0.0)
    swapped = pltpu.roll(own, HEAD_DIM, axis=1)
    even, odd = (own, swapped) if parity == 0 else (swapped, own)
    return even.astype(MXU_DTYPE), odd.astype(MXU_DTYPE)


def _stack_pairs(ref, k):
    return jnp.concatenate([ref[:, 256 * k:256 * k + 128], ref[:, 256 * k + 128:256 * k + 256]], axis=0)


_GROUP_ORDER = (0, 2, 1, 3)


def _attn_fill_tables(bias_ref):
    shape = (4 * ATT_BLOCK, 2 * ATT_BLOCK)
    row = lax.broadcasted_iota(jnp.int32, shape, 0)
    col = lax.broadcasted_iota(jnp.int32, shape, 1)
    dist = (row & (ATT_BLOCK - 1)) + ATT_BLOCK - col
    ok = (dist >= 0) & (dist < ATT_BLOCK)
    distf = dist.astype(F32)
    rb = lax.broadcasted_iota(jnp.int32, (shape[0], 1), 0) // ATT_BLOCK
    for k in range(N_KV_HEADS):
        slope = jnp.zeros((shape[0], 1), F32)
        for r, gq in enumerate(_GROUP_ORDER):
            slope = jnp.where(rb == r, ALIBI_SLOPES[4 * k + gq], slope)
        bias = jnp.where(ok, -slope * distf, -jnp.inf)
        bias_ref[0, k] = bias
        bias_ref[1, k] = jnp.where(col >= ATT_BLOCK, bias, -jnp.inf)


def _attn_probs(qs, k_even, k_odd, bias, k, sink_ref):
    nt_dims = (((1,), (1,)), ((), ()))
    s = jnp.concatenate([lax.dot_general(qs, k_even, nt_dims, preferred_element_type=F32),
                         lax.dot_general(qs, k_odd, nt_dims, preferred_element_type=F32)], axis=0) + bias
    rb = lax.broadcasted_iota(jnp.int32, (s.shape[0], 1), 0) // ATT_BLOCK
    sink = jnp.zeros((s.shape[0], 1), F32)
    for r, gq in enumerate(_GROUP_ORDER):
        sink = jnp.where(rb == r, sink_ref[4 * k + gq], sink)
    m = jnp.maximum(jnp.max(s, axis=-1, keepdims=True), sink)
    p = jnp.exp(s - m)
    es = jnp.exp(sink - m)
    return p, 1.0 / (jnp.sum(p, axis=-1, keepdims=True) + es), es


_ATTN_TABLES = [pltpu.VMEM((2, N_KV_HEADS, 4 * ATT_BLOCK, 2 * ATT_BLOCK), F32)]


def _attn_specs(with_do):
    prev = lambda n: jnp.maximum(n - 1, 0)
    specs = [pl.BlockSpec(memory_space=pltpu.SMEM),
             pl.BlockSpec((ATT_BLOCK, D_MODEL), lambda n: (n, 0)),
             pl.BlockSpec((ATT_BLOCK, 256), lambda n: (prev(n), 4)), pl.BlockSpec((ATT_BLOCK, 256), lambda n: (n, 4)),
             pl.BlockSpec((ATT_BLOCK, 256), lambda n: (prev(n), 5)), pl.BlockSpec((ATT_BLOCK, 256), lambda n: (n, 5))]
    if with_do:
        specs.append(pl.BlockSpec((ATT_BLOCK, D_MODEL), lambda n: (n, 0)))
    return specs


def _attn_fwd_call(qkv, sinks, hosted=None):
    t = qkv.shape[0]

    def body(ins, outs, scratch, first, last):
        sink_ref, q_ref, kp_ref, kc_ref, vp_ref, vc_ref = ins
        o_ref, bias_ref = outs[0], scratch[0]
        n = pl.program_id(0)
        first()

        @pl.when(n == 0)
        def _():
            _attn_fill_tables(bias_ref)

        first = 1 - jnp.minimum(n, 1)
        kband = jnp.concatenate([kp_ref[...], kc_ref[...]], axis=0)
        vband = jnp.concatenate([vp_ref[...], vc_ref[...]], axis=0)
        for k in range(N_KV_HEADS):
            k_even, k_odd = _kv_even_odd(kband, k)
            v_even, v_odd = _kv_even_odd(vband, k)
            qs = _stack_pairs(q_ref, k) * ATT_SCALE
            p, rdenom, _ = _attn_probs(qs, k_even, k_odd, bias_ref[first, k], k, sink_ref)
            probs = (p * rdenom).astype(MXU_DTYPE)
            o = (jnp.dot(probs[:256], v_even, preferred_element_type=F32)
                 + jnp.dot(probs[256:], v_odd, preferred_element_type=F32))
            o_ref[:, 256 * k:256 * k + 128] = o[:128].astype(o_ref.dtype)
            o_ref[:, 256 * k + 128:256 * k + 256] = o[128:].astype(o_ref.dtype)
        last()

    return _call(
        body, hosted, name="attn_fwd", grid=(t // ATT_BLOCK,), in_specs=_attn_specs(False),
        out_specs=[pl.BlockSpec((ATT_BLOCK, D_MODEL), lambda n: (n, 0))],
        out_shape=[jax.ShapeDtypeStruct((t, D_MODEL), MXU_DTYPE)], operands=(sinks, qkv, qkv, qkv, qkv, qkv),
        semantics=("arbitrary",), scratch_shapes=_ATTN_TABLES)


def _attn_bwd_call(qkv, sinks, do, hosted=None):
    t = qkv.shape[0]
    nb = t // ATT_BLOCK
    tn_dims = (((0,), (0,)), ((), ()))
    nt_dims = (((1,), (1,)), ((), ()))

    def to_native(even_part, odd_part, parity):
        lane = lax.broadcasted_iota(jnp.int32, even_part.shape, 1)
        lo = lane < HEAD_DIM
        e, o = jnp.where(lo, even_part, 0.0), jnp.where(lo, 0.0, odd_part)
        if parity == 0:
            return e + pltpu.roll(o, HEAD_DIM, axis=1)
        return pltpu.roll(e, HEAD_DIM, axis=1) + o

    def body(ins, outs, scratch, first, last):
        sink_ref, q_ref, kp_ref, kc_ref, vp_ref, vc_ref, do_ref = ins
        dqkv_ref, db_ref, dsink_ref = outs
        bias_ref, dk_ref, dv_ref = scratch[:3]
        n = pl.program_id(0)
        first()
        cur = pl.multiple_of(n * ATT_BLOCK, ATT_BLOCK)

        @pl.when(n == 0)
        def _():
            _attn_fill_tables(bias_ref)
            dk_ref[...] = jnp.zeros_like(dk_ref)
            dv_ref[...] = jnp.zeros_like(dv_ref)
            db_ref[...] = jnp.zeros_like(db_ref)
            dsink_ref[...] = jnp.zeros_like(dsink_ref)

        kband = jnp.concatenate([kp_ref[...], kc_ref[...]], axis=0)
        vband = jnp.concatenate([vp_ref[...], vc_ref[...]], axis=0)
        dk_pairs = [jnp.zeros((2 * ATT_BLOCK, 128), F32), jnp.zeros((2 * ATT_BLOCK, 128), F32)]
        dv_pairs = [jnp.zeros((2 * ATT_BLOCK, 128), F32), jnp.zeros((2 * ATT_BLOCK, 128), F32)]
        sink_lane = lax.broadcasted_iota(jnp.int32, (8, 128), 1)
        sink_row = lax.broadcasted_iota(jnp.int32, (8, 128), 0)
        dsink = jnp.zeros((8, 128), F32)
        first = 1 - jnp.minimum(n, 1)
        for k in range(N_KV_HEADS):
            k_even, k_odd = _kv_even_odd(kband, k)
            v_even, v_odd = _kv_even_odd(vband, k)
            qs = _stack_pairs(q_ref, k) * ATT_SCALE
            dos = _stack_pairs(do_ref, k)
            p, rdenom, es = _attn_probs(qs, k_even, k_odd, bias_ref[first, k], k, sink_ref)
            probs = p * rdenom
            dp = jnp.concatenate([lax.dot_general(dos, v_even, nt_dims, preferred_element_type=F32),
                                  lax.dot_general(dos, v_odd, nt_dims, preferred_element_type=F32)], axis=0)
            delta = jnp.sum(probs * dp, axis=-1, keepdims=True)
            ds16 = (probs * (dp - delta)).astype(MXU_DTYPE)
            p16 = probs.astype(MXU_DTYPE)
            dsink_rows = -(es * rdenom) * delta
            for r, gq in enumerate(_GROUP_ORDER):
                tot = jnp.sum(dsink_rows[ATT_BLOCK * r:ATT_BLOCK * (r + 1)], axis=0, keepdims=True)
                dsink = dsink + jnp.where((sink_lane == 4 * k + gq) & (sink_row == 0), tot, 0.0)
            dqs = (jnp.dot(ds16[:256], k_even, preferred_element_type=F32)
                   + jnp.dot(ds16[256:], k_odd, preferred_element_type=F32)) * ATT_SCALE
            dqkv_ref[pl.ds(cur, ATT_BLOCK), 256 * k:256 * k + 128] = dqs[:128].astype(dqkv_ref.dtype)
            dqkv_ref[pl.ds(cur, ATT_BLOCK), 256 * k + 128:256 * k + 256] = dqs[128:].astype(dqkv_ref.dtype)
            db_ref[:, 256 * k:256 * k + 128] += jnp.sum(dqs[:128], axis=0, keepdims=True)
            db_ref[:, 256 * k + 128:256 * k + 256] += jnp.sum(dqs[128:], axis=0, keepdims=True)
            pair, parity = divmod(k, 2)
            dk_pairs[pair] = dk_pairs[pair] + to_native(
                lax.dot_general(ds16[:256], qs, tn_dims, preferred_element_type=F32),
                lax.dot_general(ds16[256:], qs, tn_dims, preferred_element_type=F32), parity)
            dv_pairs[pair] = dv_pairs[pair] + to_native(
                lax.dot_general(p16[:256], dos, tn_dims, preferred_element_type=F32),
                lax.dot_general(p16[256:], dos, tn_dims, preferred_element_type=F32), parity)
        dsink_ref[...] += dsink
        dk_band = jnp.concatenate(dk_pairs, axis=1)
        dv_band = jnp.concatenate(dv_pairs, axis=1)
        dk_ref[pl.ds(cur, ATT_BLOCK), :] += dk_band[ATT_BLOCK:]
        dv_ref[pl.ds(cur, ATT_BLOCK), :] += dv_band[ATT_BLOCK:]

        @pl.when(n > 0)
        def _():
            prv = pl.multiple_of((n - 1) * ATT_BLOCK, ATT_BLOCK)
            dk_ref[pl.ds(prv, ATT_BLOCK), :] += dk_band[:ATT_BLOCK]
            dv_ref[pl.ds(prv, ATT_BLOCK), :] += dv_band[:ATT_BLOCK]

        @pl.when(n == nb - 1)
        def _():
            dk, dv = dk_ref[...], dv_ref[...]
            db_ref[:, 1024:1280] = jnp.sum(dk, axis=0, keepdims=True)
            db_ref[:, 1280:1536] = jnp.sum(dv, axis=0, keepdims=True)
            dqkv_ref[:, 1024:1280] = dk.astype(dqkv_ref.dtype)
            dqkv_ref[:, 1280:1536] = dv.astype(dqkv_ref.dtype)

        last()

    whole = lambda shape: pl.BlockSpec(shape, lambda n: (0, 0))
    return _call(
        body, hosted, name="attn_bwd", grid=(nb,), in_specs=_attn_specs(True),
        out_specs=[whole((t, QKV_DIM)), whole((1, QKV_DIM)), whole((8, 128))],
        out_shape=[jax.ShapeDtypeStruct((t, QKV_DIM), MXU_DTYPE), jax.ShapeDtypeStruct((1, QKV_DIM), F32),
                   jax.ShapeDtypeStruct((8, 128), F32)],
        operands=(sinks, qkv, qkv, qkv, qkv, qkv, do), semantics=("arbitrary",),
        scratch_shapes=_ATTN_TABLES + [pltpu.VMEM((t, 256), F32), pltpu.VMEM((t, 256), F32)])


FF_TILE = 1408


def _gathering_tile(j):
    me = 2 * lax.axis_index("x") + lax.axis_index("y")
    return me ^ (((j & 1) << 1) | (j >> 1))


def _ffn_in_gathering_call(hn, w_in, layer, hosted):
    t = hn.shape[0]
    tm = _tile(t, 1024)
    n_i = t // tm
    halves = D_FF // FF_TILE
    assert FF_TILE == W_IN_COLS and 2 * halves == N_CHIPS

    def body(ins, outs, scratch, first, last):
        hn_ref = ins[0]
        u_ref, w_ref = outs
        wbuf, load_sem, ici_send, ici_recv, hand_send, hand_recv = scratch
        j, i = pl.program_id(0), pl.program_id(1)
        x, y, c, chips = _mesh_place()
        me = 2 * x + y
        mover = c == layer

        def shard(k):
            return w_ref.at[layer, :, _col(k, W_IN_COLS)]

        def over_ici(jj, k):
            cx, cy = chips[jj]
            return pltpu.make_async_remote_copy(src_ref=shard(k), dst_ref=shard(k), send_sem=ici_send.at[jj],
                                                recv_sem=ici_recv.at[jj], device_id=(cx, cy, c), device_id_type=MESH)

        def handed(jj):
            cx, cy = chips[jj]
            return pltpu.make_async_remote_copy(
                src_ref=shard(2 * cx + cy), dst_ref=shard(2 * cx + cy), send_sem=hand_send.at[jj],
                recv_sem=hand_recv.at[jj], device_id=(x, y, 1 - c), device_id_type=MESH)

        far = N_CHIPS - 2

        @pl.when((j == 0) & (i == 0) & mover)
        def _():
            over_ici(far, me).start()
            for jj in NEIGHBOURS:
                handed(jj).start()

        first()

        def load(step):
            return pltpu.make_async_copy(shard(_gathering_tile(step)), wbuf.at[step % 2], load_sem.at[step % 2])

        @pl.when((j == 0) & (i == 0))
        def _():
            load(0).start()

        for jj in range(N_CHIPS - 1):
            @pl.when((j == jj) & (i == n_i - 1))
            def _(jj=jj):
                if jj == far:
                    @pl.when(mover)
                    def _():
                        over_ici(far, 2 * chips[far][0] + chips[far][1]).wait_recv()
                        handed(far).start()

                @pl.when(jnp.logical_not(mover))
                def _():
                    handed(jj).wait_recv()

                load(jj + 1).start()

        @pl.when(i == 0)
        def _():
            load(j).wait()

        u_ref[...] = jnp.dot(hn_ref[...].astype(MXU_DTYPE), wbuf[j % 2].astype(MXU_DTYPE),
                             preferred_element_type=F32).astype(u_ref.dtype)

        @pl.when((j == N_CHIPS - 1) & (i == n_i - 1) & mover)
        def _():
            over_ici(far, me).wait_send()
            for jj in range(N_CHIPS - 1):
                handed(jj).wait_send()

        last()

    def u_index(j, i):
        tile = _gathering_tile(j)
        return tile // halves, i, tile % halves

    three = [pltpu.SemaphoreType.DMA((N_CHIPS - 1,))] * 4
    return _call(
        body, hosted, name=f"ffn{layer}_in", grid=(N_CHIPS, n_i),
        in_specs=[pl.BlockSpec((tm, D_MODEL), lambda j, i: (i, 0)), ANY],
        out_specs=[pl.BlockSpec((None, tm, FF_TILE), u_index), ANY],
        out_shape=[jax.ShapeDtypeStruct((2, t, D_FF), U_DTYPE), jax.ShapeDtypeStruct(w_in.shape, w_in.dtype)],
        operands=(hn, w_in), semantics=("arbitrary", "arbitrary"), aliases={1: 1},
        scratch_shapes=[pltpu.VMEM((2, D_MODEL, W_IN_COLS), w_in.dtype), pltpu.SemaphoreType.DMA((2,))] + three)


def _ffn_out_loss_call(z, w_out, layer, h, g, target):
    t, f = z.shape
    tm = _tile(t, ROW_TILE)

    def body(z_ref, w_ref, h_ref, g_ref, t_ref, loss_ref, dh_ref, dh16_ref, dg_ref):
        x = h_ref[...] + jnp.dot(z_ref[...], w_ref[...], preferred_element_type=F32)
        gv = g_ref[...]
        err = _rms(x, gv) - t_ref[...]
        dx, dgt = _rms_bwd(x, gv, err * (1.0 / D_MODEL))
        dh_ref[...] = dx
        dh16_ref[...] = dx.astype(dh16_ref.dtype)

        @pl.when(pl.program_id(0) == 0)
        def _():
            dg_ref[...] = jnp.zeros_like(dg_ref)
            loss_ref[...] = jnp.zeros_like(loss_ref)

        dg_ref[...] += jnp.sum(dgt, axis=0, keepdims=True)
        per_token = jnp.mean(err * err, axis=-1, keepdims=True)
        loss_ref[...] += 0.5 * jnp.sum(per_token, axis=0, keepdims=True)

    row = pl.BlockSpec((tm, D_MODEL), lambda i: (i, 0))
    vec = pl.BlockSpec((1, D_MODEL), lambda i: (0, 0))
    return pl.pallas_call(
        body, name=f"ffn{layer}_out_loss", grid=(t // tm,),
        in_specs=[pl.BlockSpec((tm, f), lambda i: (i, 0)), pl.BlockSpec((None, f, D_MODEL), lambda i: (layer, 0, 0)),
                  row, vec, row],
        out_specs=[pl.BlockSpec((1, 1), lambda i: (0, 0)), row, row, vec],
        out_shape=[jax.ShapeDtypeStruct((1, 1), F32), jax.ShapeDtypeStruct((t, D_MODEL), F32),
                   jax.ShapeDtypeStruct((t, D_MODEL), MXU_DTYPE), jax.ShapeDtypeStruct((1, D_MODEL), F32)],
        compiler_params=_params("arbitrary"))(z, w_out, h, g, target)


def _ffn_fwd(hn, h, w, conv_w, conv_b, layer, hooks=None, next_norm=None, loss_head=None, gathers_w_in=False):
    t = hn.shape[0]
    tm = _tile(t, 1024)
    halves = D_FF // FF_TILE
    if gathers_w_in:
        u, w["w_in"] = _ride(hooks, f"in{layer}", lambda hosted: _ffn_in_gathering_call(hn, w["w_in"], layer, hosted))
    else:
        u, = _ride(hooks, f"in{layer}", lambda hosted: _matmul(
            hn, w["w_in"], mode="nn", name=f"ffn{layer}_in", grid=(2 * halves, t // tm, 1),
            a_spec=pl.BlockSpec((tm, D_MODEL), lambda j, i, k: (i, 0)),
            b_spec=pl.BlockSpec((None, D_MODEL, FF_TILE), lambda j, i, k: (layer, 0, j)),
            o_spec=pl.BlockSpec((None, tm, FF_TILE), lambda j, i, k: (j // halves, i, j % halves)),
            out_shape=jax.ShapeDtypeStruct((2, t, D_FF), U_DTYPE), hosted=hosted))
    z, = _ride(hooks, f"glu{layer}", lambda hosted: _conv_glu_fwd_call(u, conv_w, conv_b, f"ffn{layer}_glu",
                                                                       hosted=hosted))
    if loss_head is not None:
        return (*_ffn_out_loss_call(z, w["w_out"], layer, h, *loss_head), u, z)
    outs = _ride(hooks, f"out{layer}", lambda hosted: _mm_nn(z, w["w_out"], f"ffn{layer}_out", b_lead=layer, residual=h,
                                                     tk=D_FF, hosted=hosted, norm_g=next_norm))
    return outs[0], (outs[1] if next_norm is not None else None), u, z


def _ffn_bwd(dh, hn, u, z, w_in, w_out, conv_w, conv_b, layer, g_in_acc, g_out_acc, norm, hooks=None):
    t = hn.shape[0]
    dz, = _ride(hooks, "dz", lambda hosted: _mm_nt(dh, w_out, f"ffn{layer}_dz", b_lead=layer, out_dtype=MXU_DTYPE,
                                                   tn=FF_TILE, hosted=hosted))
    du, dcw, dcb = _ride(hooks, "dglu", lambda hosted: _conv_glu_bwd_call(u, dz, conv_w, conv_b, f"ffn{layer}_dglu",
                                                                          hosted=hosted))
    halves = D_FF // FF_TILE
    rows = D_MODEL // 2
    g_in, = _ride(hooks, "dwin", lambda hosted: _matmul(
        hn, du, mode="tn", name=f"ffn{layer}_dwin", grid=(2 * halves, D_MODEL // rows, 1),
        a_spec=pl.BlockSpec((t, rows), lambda j, i, k: (0, i)),
        b_spec=pl.BlockSpec((None, t, FF_TILE), lambda j, i, k: (j // halves, 0, j % halves)),
        o_spec=pl.BlockSpec((None, rows, FF_TILE), lambda j, i, k: (layer, i, j)),
        out_shape=jax.ShapeDtypeStruct((2, D_MODEL, 2 * D_FF), WIRE_DTYPE), into=g_in_acc, hosted=hosted),
        {"dcw": dcw, "dcb": dcb})
    g_out, = _ride(hooks, "dwout", lambda hosted: _mm_tn(
        z, dh, f"ffn{layer}_dwout", lead=layer, n_lead=2, into=g_out_acc, tm=WGRAD_TILE, tk=t, out_dtype=WIRE_DTYPE,
        hosted=hosted), {"g_in": g_in})
    dx, dx16, dg = _ride(hooks, "dhn", lambda hosted: _ffn_dhn_call(du, w_in, layer, norm, hosted),
                         {"g_in": g_in, "g_out": g_out})
    return dx, dx16, dg, g_in, g_out, dcw, dcb


WGRAD_TILE = 256


_NT_DIMS = (((1,), (1,)), ((), ()))


def _dhn_call(name, a, a_spec, w, w_spec, product, t, tm, norm, hosted=None):
    h, g, dres = norm

    def body(ins, outs, scratch, first, last):
        a_ref, w_ref, h_ref, g_ref, dres_ref = ins
        dx_ref, dx16_ref, dg_ref = outs
        first()
        dx, dgt = _rms_bwd(h_ref[...], g_ref[...], product(a_ref, w_ref))
        dx = dres_ref[...] + dx
        dx_ref[...] = dx
        dx16_ref[...] = dx.astype(dx16_ref.dtype)

        @pl.when(pl.program_id(0) == 0)
        def _():
            dg_ref[...] = jnp.zeros_like(dg_ref)

        dg_ref[...] += jnp.sum(dgt, axis=0, keepdims=True)
        last()

    row = pl.BlockSpec((tm, D_MODEL), lambda i: (i, 0))
    vec = pl.BlockSpec((1, D_MODEL), lambda i: (0, 0))
    return _call(
        body, hosted, name=name, grid=(t // tm,), in_specs=[a_spec, w_spec, row, vec, row], out_specs=[row, row, vec],
        out_shape=[jax.ShapeDtypeStruct((t, D_MODEL), F32), jax.ShapeDtypeStruct((t, D_MODEL), MXU_DTYPE),
                   jax.ShapeDtypeStruct((1, D_MODEL), F32)],
        operands=(a, w, h, g, dres), semantics=("arbitrary",))


def _ffn_dhn_call(du, w_in, layer, norm, hosted=None):
    _, t, f = du.shape
    tm = _tile(t, 512)

    def product(du_ref, w_ref):
        return (lax.dot_general(du_ref[0], w_ref[:, :f], _NT_DIMS, preferred_element_type=F32)
                + lax.dot_general(du_ref[1], w_ref[:, f:], _NT_DIMS, preferred_element_type=F32))

    return _dhn_call(f"ffn{layer}_dhn", du, pl.BlockSpec((2, tm, f), lambda i: (0, i, 0)), w_in,
                     pl.BlockSpec((None, D_MODEL, 2 * f), lambda i: (layer, 0, 0)), product, t, tm, norm, hosted)


def _attn_dhn_call(dqkv, w_qkv, norm):
    t, n = dqkv.shape
    tm = _tile(t, 1024)

    def product(a_ref, w_ref):
        return lax.dot_general(a_ref[...], w_ref[...], _NT_DIMS, preferred_element_type=F32)

    return _dhn_call("attn_dhn", dqkv, pl.BlockSpec((tm, n), lambda i: (i, 0)), w_qkv,
                     pl.BlockSpec((D_MODEL, n), lambda i: (0, 0)), product, t, tm, norm)


def _local_step(x, target, w, *, norm_mix, norm_ffn, norm_f, pool_b, pool_scale, b_qkv, sinks, conv_w, conv_b,
                late_gather=None, reducer=None):
    cw = [conv_w[l].reshape(3, 2, D_FF) for l in range(2)]
    cb = [conv_b[l].reshape(2, D_FF) for l in range(2)]
    gm = [norm_mix[l:l + 1] for l in range(2)]
    gf = [norm_ffn[l:l + 1] for l in range(2)]

    w = dict(w)
    fwd_hooks = {}
    if late_gather:
        def gather_hook(stage):
            return (lambda env: late_gather(stage, w)), (lambda got: w.update(zip(_late_names(stage), got)))

        fwd_hooks = {stage: gather_hook(stage) for stage in GATHER_RIDES}

    h1, hn_f0 = _ride(fwd_hooks, "pool_fwd", lambda hosted: _pool_fwd_call(x, gm[0], w["pool_w"], pool_b, pool_scale,
                                                                            gf[0], hosted))
    h2, hn_a, u0, z0 = _ffn_fwd(hn_f0, h1, w, cw[0], cb[0], 0, hooks=fwd_hooks, next_norm=gm[1],
                                gathers_w_in=bool(late_gather))
    qkv = _mm_nn(hn_a, w["w_qkv"], "attn_qkv", bias=b_qkv, out_dtype=MXU_DTYPE, tn=QKV_DIM)
    o, = _ride(fwd_hooks, "attn_fwd", lambda hosted: _attn_fwd_call(qkv, sinks, hosted))
    h3, hn_f1 = _ride(fwd_hooks, "attn_out", lambda hosted: _mm_nn(o, w["w_o"], "attn_out", residual=h2, hosted=hosted,
                                                                   norm_g=gf[1]))
    loss, dh, dh16, d_norm_f, u1, z1 = _ffn_fwd(hn_f1, h3, w, cw[1], cb[1], 1, hooks=fwd_hooks,
                                                loss_head=(norm_f, target))

    rd = reducer
    first, att, w_in0, w_out0 = _Reducer.FIRST, _Reducer.ATT, _Reducer.W_IN0, _Reducer.W_OUT0
    hooks1 = hooks_attn = hooks0 = hooks_pool = None
    if rd:
        hooks1 = {"dhn": (lambda env: rd.swap(first, {"w_in": env["g_in"], "w_out": env["g_out"]}),
                          lambda got: rd.pair_sums(first, got))}
        hooks_attn = {"attn_bwd": (lambda env: rd.scatter(first), lambda got: rd.chip_sums(first, got))}

        def early_pieces(env):
            return {"b_qkv": d_bqkv, "sinks": d_sinks[0:1], "norm_mix1": d_gm1, "norm_ffn1": d_gf1,
                    "conv_w": jnp.stack([env["dcw"].reshape(3, 2 * D_FF), dcw1.reshape(3, 2 * D_FF)]),
                    "conv_b": jnp.stack([env["dcb"].reshape(2 * D_FF), dcb1.reshape(2 * D_FF)]),
                    "norm_f": d_norm_f, "loss": loss}

        def after_dz(got):
            rd.shared(first, got[:len(first)])
            rd.pair_sums(att, got[len(first):])

        def after_dwin(got):
            rd.shared(att, got[:len(att)])
            rd.small_early_done(got[-1])

        def after_dhn(got):
            rd.chip_sums(w_in0, got[:1])
            rd.pair_sums(w_out0, got[1:])

        def after_pool_bwd(got):
            rd.shared(w_in0, got[:1])
            rd.chip_sums(w_out0, got[1:])

        hooks0 = {
            "dz": (lambda env: _both(rd.share(first), rd.swap(att, {"w_qkv": g_qkv, "w_o": g_o4})), after_dz),
            "dglu": (lambda env: rd.scatter(att), lambda got: rd.chip_sums(att, got)),
            "dwin": (lambda env: _both(rd.share(att), rd.small_early(early_pieces(env))), after_dwin),
            "dwout": (lambda env: rd.swap(w_in0, {"w_in": env["g_in"]}), lambda got: rd.pair_sums(w_in0, got)),
            "dhn": (lambda env: _both(rd.scatter(w_in0), rd.swap(w_out0, {"w_out": env["g_out"]})), after_dhn),
        }
        hooks_pool = {"pool_bwd": (lambda env: _both(rd.share(w_in0), rd.scatter(w_out0)), after_pool_bwd)}
    dh, dh16, d_gf1, g_in, g_out, dcw1, dcb1 = _ffn_bwd(
        dh16, hn_f1, u1, z1, w["w_in"], w["w_out"], cw[1], cb[1], 1, None, None, (h3, gf[1], dh), hooks=hooks1)
    do = _mm_nt(dh16, w["w_o"], "attn_do", out_dtype=MXU_DTYPE)
    g_o = _mm_tn(o, dh16, "attn_dwo", tn=WGRAD_TILE, tk=x.shape[0], out_dtype=WIRE_DTYPE)
    g_o4 = g_o.reshape(N_CHIPS, 2, W_O_ROWS // 2, D_MODEL)
    dqkv, d_bqkv, d_sinks = _ride(hooks_attn, "attn_bwd", lambda hosted: _attn_bwd_call(qkv, sinks, do, hosted))
    g_qkv = _mm_tn(hn_a, dqkv, "attn_dwqkv", tn=WGRAD_TILE, tk=x.shape[0], out_dtype=WIRE_DTYPE)
    dh, dh16, d_gm1 = _attn_dhn_call(dqkv, w["w_qkv"], (h2, gm[1], dh))
    dh, _, d_gf0, g_in, g_out, dcw0, dcb0 = _ffn_bwd(
        dh16, hn_f0, u0, z0, w["w_in"], w["w_out"], cw[0], cb[0], 0, g_in, g_out, (h1, gf[0], dh), hooks=hooks0)
    grad_x, g_pool, dvec = _ride(hooks_pool, "pool_bwd", lambda hosted: _pool_bwd_call(
        x, dh, gm[0], w["pool_w"], pool_b, pool_scale, hosted))

    grads = {
        "w_in": g_in, "w_out": g_out, "w_qkv": g_qkv, "w_o": g_o, "pool_w": g_pool,
        "pool_b": dvec[0:1], "pool_scale": dvec[1:2], "b_qkv": d_bqkv, "sinks": d_sinks[0:1, :N_HEADS],
        "norm_mix": jnp.concatenate([dvec[2:3], d_gm1], axis=0), "norm_ffn": jnp.concatenate([d_gf0, d_gf1], axis=0),
        "conv_w": jnp.stack([dcw0.reshape(3, 2 * D_FF), dcw1.reshape(3, 2 * D_FF)]),
        "conv_b": jnp.stack([dcb0.reshape(2 * D_FF), dcb1.reshape(2 * D_FF)]), "norm_f": d_norm_f,
    }
    return loss, grad_x, grads


W_IN_COLS = 2 * D_FF // N_CHIPS
W_OUT_ROWS = D_FF // N_CHIPS
QKV_COLS = QKV_DIM // N_CHIPS
W_O_ROWS = D_MODEL // N_CHIPS
POOL_ROWS = POOL_GC // N_CHIPS
BIG = ("w_in", "w_out", "w_qkv", "w_o", "pool_w")


def _mesh_place():
    x, y, c = lax.axis_index("x"), lax.axis_index("y"), lax.axis_index("c")
    chips = [(1 - x, y), (x, 1 - y), (1 - x, 1 - y)]
    return x, y, c, chips


def _col(k, width):
    return pl.ds(pl.multiple_of(k * width, 128), width)


def _row(k, height):
    return pl.ds(pl.multiple_of(k * height, 16), height)


def _full_piece(name, ref, k, h):
    if name == "w_in":
        return ref.at[h, :, _col(k, W_IN_COLS)]
    if name == "w_out":
        return ref.at[h, _row(k, W_OUT_ROWS), :]
    if name == "w_qkv":
        return ref.at[_row(h, D_MODEL // 2), _col(k, QKV_COLS)]
    if name == "w_o":
        return ref.at[_row(2 * k + h, W_O_ROWS // 2), :]
    return ref.at[pl.ds(2 * h, 2), _row(k, POOL_ROWS), :]


def _shard_half(name, ref, h):
    if name in ("w_in", "w_out"):
        return ref.at[h]
    if name == "w_qkv":
        return ref.at[_row(h, D_MODEL // 2), :]
    if name == "w_o":
        return ref.at[_row(h, W_O_ROWS // 2), :]
    return ref.at[pl.ds(2 * h, 2)]


FULL_SHAPES = {"w_in": (2, D_MODEL, 2 * D_FF), "w_out": (2, D_FF, D_MODEL), "w_qkv": (D_MODEL, QKV_DIM),
               "w_o": (D_MODEL, D_MODEL), "pool_w": (4, POOL_GC, POOL_GC)}
SHARD_SHAPES = {"w_in": (2, D_MODEL, W_IN_COLS), "w_out": (2, W_OUT_ROWS, D_MODEL), "w_qkv": (D_MODEL, QKV_COLS),
                "w_o": (W_O_ROWS, D_MODEL), "pool_w": (4, POOL_ROWS, POOL_GC)}
HALF_SHAPES = {"w_in": (D_MODEL, W_IN_COLS), "w_out": (W_OUT_ROWS, D_MODEL), "w_qkv": (D_MODEL // 2, QKV_COLS),
               "w_o": (W_O_ROWS // 2, D_MODEL), "pool_w": (2, POOL_ROWS, POOL_GC)}


def _place_shard(name, shard, me_arr):
    if name == "w_in":
        blk, grid = (1, 256, W_IN_COLS), (2, D_MODEL // 256)
        src, dst = (lambda a, b, me: (a, b, 0)), (lambda a, b, me: (a, b, me[0]))
    elif name == "w_out":
        blk, grid = (1, W_OUT_ROWS, D_MODEL), (2, 1)
        src, dst = (lambda a, b, me: (a, 0, 0)), (lambda a, b, me: (a, me[0], 0))
    elif name == "w_qkv":
        blk, grid = (256, QKV_COLS), (1, D_MODEL // 256)
        src, dst = (lambda a, b, me: (b, 0)), (lambda a, b, me: (b, me[0]))
    elif name == "w_o":
        blk, grid = (W_O_ROWS, D_MODEL), (1, 1)
        src, dst = (lambda a, b, me: (0, 0)), (lambda a, b, me: (me[0], 0))
    else:
        blk, grid = (4, POOL_ROWS, POOL_GC), (1, 1)
        src, dst = (lambda a, b, me: (0, 0, 0)), (lambda a, b, me: (0, me[0], 0))

    def body(me_ref, s_ref, o_ref):
        o_ref[...] = s_ref[...].astype(o_ref.dtype)

    return pl.pallas_call(
        body, name=f"place_{name}",
        grid_spec=pltpu.PrefetchScalarGridSpec(num_scalar_prefetch=1, grid=grid, in_specs=[pl.BlockSpec(blk, src)],
                                               out_specs=pl.BlockSpec(blk, dst)),
        out_shape=jax.ShapeDtypeStruct(FULL_SHAPES[name], WIRE_DTYPE), compiler_params=_params("parallel", "parallel"),
    )(me_arr, shard)


def _when_mover(half, c, fn):
    if half is None:
        fn(c)
    else:
        pl.when(c == half)(lambda: fn(half))


def _when_taker(half, c, fn):
    if half is None:
        fn(1 - c)
    else:
        pl.when(c != half)(lambda: fn(half))


def _remote(ref, sems, t, j, dev):
    send, recv = sems
    return pltpu.make_async_remote_copy(src_ref=ref, dst_ref=ref, send_sem=send.at[t, j], recv_sem=recv.at[t, j],
                                        device_id=dev, device_id_type=MESH)


def _gather_ici(items, refs, sems, finish, peers=(0, 1, 2)):
    x, y, c, chips = _mesh_place()
    me = 2 * x + y
    for t, (nm, half) in enumerate(items):
        def go(h, t=t, nm=nm):
            mine = _full_piece(nm, refs[t], me, h)
            for j, (cx, cy) in enumerate(chips):
                if j not in peers:
                    continue
                if not finish:
                    _remote(mine, sems, t, j, (cx, cy, c)).start()
                else:
                    _remote(_full_piece(nm, refs[t], 2 * cx + cy, h), sems, t, j, (cx, cy, c)).wait_recv()
                    _remote(mine, sems, t, j, (cx, cy, c)).wait_send()

        _when_mover(half, c, go)


def _gather_pass(items, refs, sems, finish):
    x, y, c, chips = _mesh_place()
    sibling = (x, y, 1 - c)
    for t, (nm, half) in enumerate(items):
        def give(h, t=t, nm=nm):
            for j, (cx, cy) in enumerate(chips):
                got = _remote(_full_piece(nm, refs[t], 2 * cx + cy, h), sems, t, j, sibling)
                if not finish:
                    got.start()
                else:
                    got.wait_send()

        def take(h, t=t, nm=nm):
            for j, (cx, cy) in enumerate(chips):
                _remote(_full_piece(nm, refs[t], 2 * cx + cy, h), sems, t, j, sibling).wait_recv()

        _when_mover(half, c, give)
        if finish:
            _when_taker(half, c, take)


def _gather_sems(n_items):
    return [pltpu.SemaphoreType.DMA((n_items, 3)), pltpu.SemaphoreType.DMA((n_items, 3))]


EARLY_ITEMS = (("pool_w", None),)
_LATE_A = (("w_out", 0),)
_LATE_B = (("w_in", 1),)
_LATE_C = (("w_out", 1),)
_LATE_D = (("w_qkv", None), ("w_o", None))
_EARLY_W_IN = (("w_in", 0),)
GATHER_RIDES = {"pool_fwd": ((), _EARLY_W_IN), "in0": ((), _LATE_A), "glu0": (_LATE_A, _LATE_D), "out0": (_LATE_D, ()),
                "attn_fwd": ((), _LATE_B), "attn_out": (_LATE_B, ()), "in1": ((), _LATE_C), "glu1": (_LATE_C, ())}
NEIGHBOURS = (0, 1)
GATHER_PEERS = {"pool_fwd": NEIGHBOURS}


def _names(items):
    return [nm for nm, _ in items]


def _gather_early(full, small):
    items = EARLY_ITEMS
    n = len(items)

    def body(*refs):
        small_in = refs[n]
        outs, small_out = refs[n + 1:2 * n + 1], refs[2 * n + 1]
        scratch = refs[2 * n + 2:]
        ici_sems, pass_sems = scratch[0:2], scratch[2:4]
        local_sem, small_send, small_recv = scratch[4:]
        x, y, c, chips = _mesh_place()
        me = 2 * x + y
        local = pltpu.make_async_copy(small_in, small_out.at[me], local_sem.at[0])
        local.start()
        small_sends = [pltpu.make_async_remote_copy(
            src_ref=small_in, dst_ref=small_out.at[me], send_sem=small_send.at[j], recv_sem=small_recv.at[j],
            device_id=(cx, cy, c), device_id_type=MESH) for j, (cx, cy) in enumerate(chips)]
        for cp in small_sends:
            cp.start()
        _gather_ici(items, outs, ici_sems, finish=False)
        _gather_ici(items, outs, ici_sems, finish=True)
        _gather_pass(items, outs, pass_sems, finish=False)
        _gather_pass(items, outs, pass_sems, finish=True)
        for j, (cx, cy) in enumerate(chips):
            got = small_out.at[2 * cx + cy]
            pltpu.make_async_remote_copy(src_ref=got, dst_ref=got, send_sem=small_send.at[j],
                                         recv_sem=small_recv.at[j], device_id=(cx, cy, c), device_id_type=MESH).wait_recv()
        for cp in small_sends:
            cp.wait_send()
        local.wait()

    arrays = [full[nm] for nm in _names(items)]
    out_shape = [jax.ShapeDtypeStruct(a.shape, a.dtype) for a in arrays]
    out_shape.append(jax.ShapeDtypeStruct((N_CHIPS,) + small.shape, F32))
    res = pl.pallas_call(
        body, name="gather_early", in_specs=[ANY] * (n + 1), out_specs=[ANY] * (n + 1), out_shape=out_shape,
        input_output_aliases={t: t for t in range(n)},
        scratch_shapes=_gather_sems(n) + _gather_sems(n) + [pltpu.SemaphoreType.DMA((1,)), pltpu.SemaphoreType.DMA((3,)),
                                                            pltpu.SemaphoreType.DMA((3,))],
    )(*arrays, small)
    return {**full, **dict(zip(_names(items), res[:n]))}, res[n]


def _late_gather_host(stage, w):
    passing, sending = GATHER_RIDES[stage]
    n_pass = len(passing)

    def run(src, refs, new, sems, finish):
        if passing:
            _gather_pass(passing, refs[:n_pass], sems[:2], finish)
        if sending:
            _gather_ici(sending, refs[n_pass:], sems[-2:], finish, GATHER_PEERS.get(stage, (0, 1, 2)))

    sems = (_gather_sems(n_pass) if passing else []) + (_gather_sems(len(sending)) if sending else [])
    return _Hosted([], [w[nm] for nm in _late_names(stage)], [], sems, functools.partial(run, finish=False),
                   functools.partial(run, finish=True))


def _late_names(stage):
    return [nm for items in GATHER_RIDES[stage] for nm in _names(items)]


def _other_half_src(name, ref, h):
    if name in ("w_in", "w_out"):
        return ref.at[h]
    if name == "w_qkv":
        return ref.at[_row(h, D_MODEL // 2), :]
    if name == "w_o":
        return ref.at[:, pl.ds(h, 1)]
    return ref.at[pl.ds(2 * h, 2)]


SLAB_SHAPES = {"w_in": (D_MODEL, 2 * D_FF), "w_out": (D_FF, D_MODEL), "w_qkv": (D_MODEL // 2, QKV_DIM),
               "w_o": (N_CHIPS, 1, W_O_ROWS // 2, D_MODEL), "pool_w": (2, POOL_GC, POOL_GC)}


def _item_tag(item):
    return item[0] if item[1] is None else f"{item[0]}{item[1]}"


def _reduce_swap(items, grads, slabs, sems, finish):
    x, y, c, _ = _mesh_place()
    send, recv = sems
    for t, (nm, half) in enumerate(items):
        def copy(h, t=t, nm=nm):
            return pltpu.make_async_remote_copy(
                src_ref=_other_half_src(nm, grads[t], h), dst_ref=slabs[t], send_sem=send.at[t], recv_sem=recv.at[t],
                device_id=(x, y, 1 - c), device_id_type=MESH)

        def give(h):
            if finish:
                copy(h).wait_send()
            else:
                copy(h).start()

        if half is None:
            give(1 - c)
            if finish:
                copy(c).wait_recv()
        else:
            pl.when(c != half)(lambda: give(half))
            if finish:
                pl.when(c == half)(lambda: copy(half).wait_recv())


def _reduce_scatter(items, sums, parts, sems, finish):
    _, _, c, chips = _mesh_place()
    send, recv = sems
    for t, (nm, half) in enumerate(items):
        def go(h, t=t, nm=nm):
            for j, (cx, cy) in enumerate(chips):
                cp = pltpu.make_async_remote_copy(
                    src_ref=_slab_piece(nm, sums[t], 2 * cx + cy), dst_ref=parts[t].at[j], send_sem=send.at[t, j],
                    recv_sem=recv.at[t, j], device_id=(cx, cy, c), device_id_type=MESH)
                if finish:
                    cp.wait_recv()
                    cp.wait_send()
                else:
                    cp.start()

        _when_mover(half, c, go)


def _reduce_share(items, shards, sems, finish):
    x, y, c, _ = _mesh_place()
    send, recv = sems
    for t, (nm, half) in enumerate(items):
        def copy(h, t=t, nm=nm):
            part = _shard_half(nm, shards[t], h)
            return pltpu.make_async_remote_copy(src_ref=part, dst_ref=part, send_sem=send.at[t], recv_sem=recv.at[t],
                                                device_id=(x, y, 1 - c), device_id_type=MESH)

        def give(h):
            if finish:
                copy(h).wait_send()
            else:
                copy(h).start()

        _when_mover(half, c, give)
        if finish:
            _when_taker(half, c, lambda h: copy(h).wait_recv())


def _pair_sems(n):
    return [pltpu.SemaphoreType.DMA((n,)), pltpu.SemaphoreType.DMA((n,))]


def _swap_host(items, grads):
    def run(src, arr, new, sems, finish):
        _reduce_swap(items, src, new, sems, finish)

    return _Hosted([grads[nm] for nm, _ in items], [],
                   [jax.ShapeDtypeStruct(SLAB_SHAPES[nm], grads[nm].dtype) for nm, _ in items],
                   _pair_sems(len(items)), functools.partial(run, finish=False), functools.partial(run, finish=True))


def _scatter_host(items, sums):
    def run(src, arr, new, sems, finish):
        _reduce_scatter(items, src, new, sems, finish)

    new = [jax.ShapeDtypeStruct((N_CHIPS - 1,) + HALF_SHAPES[nm], WIRE_DTYPE) for nm, _ in items]
    return _Hosted(list(sums), [], new, _gather_sems(len(items)), functools.partial(run, finish=False),
                   functools.partial(run, finish=True))


def _share_host(items, shards):
    def run(src, arr, new, sems, finish):
        _reduce_share(items, arr, sems, finish)

    return _Hosted([], list(shards), [], _pair_sems(len(items)), functools.partial(run, finish=False),
                   functools.partial(run, finish=True))


def _pair_sum(item, grad, slab, place):
    name, half = item
    shape = SLAB_SHAPES[name]
    hsel = (lambda p: p[1]) if half is None else (lambda p: half)
    if name == "w_in":
        blk = (256, 2 * D_FF)
        grid = (D_MODEL // 256,)
        g_spec = pl.BlockSpec((None,) + blk, lambda i, p: (hsel(p), i, 0))
        s_spec = pl.BlockSpec(blk, lambda i, p: (i, 0))
    elif name == "w_out":
        blk = (W_OUT_ROWS, D_MODEL)
        grid = (N_CHIPS,)
        g_spec = pl.BlockSpec((None,) + blk, lambda i, p: (hsel(p), i, 0))
        s_spec = pl.BlockSpec(blk, lambda i, p: (i, 0))
    elif name == "w_qkv":
        grid = (1,)
        g_spec = pl.BlockSpec(shape, lambda i, p: (hsel(p), 0))
        s_spec = pl.BlockSpec(shape, lambda i, p: (0, 0))
    else:
        grid = (1,)
        g_spec = pl.BlockSpec(shape, lambda i, p: (0, hsel(p), 0, 0))
        s_spec = pl.BlockSpec(shape, lambda i, p: (0, 0, 0, 0))

    def body(p_ref, g_ref, s_ref, o_ref):
        @pl.when(hsel(p_ref) == p_ref[1])
        def _():
            o_ref[...] = (g_ref[...].astype(F32) + s_ref[...].astype(F32)).astype(o_ref.dtype)

    return pl.pallas_call(
        body, name=f"pair_sum_{_item_tag(item)}",
        grid_spec=pltpu.PrefetchScalarGridSpec(num_scalar_prefetch=1, grid=grid, in_specs=[g_spec, s_spec],
                                               out_specs=s_spec),
        out_shape=jax.ShapeDtypeStruct(shape, WIRE_DTYPE), compiler_params=_params("parallel"),
    )(place, grad, slab)


def _slab_piece(name, ref, k):
    if name == "w_in":
        return ref.at[:, _col(k, W_IN_COLS)]
    if name == "w_out":
        return ref.at[_row(k, W_OUT_ROWS), :]
    if name == "w_qkv":
        return ref.at[:, _col(k, QKV_COLS)]
    if name == "w_o":
        return ref.at[k, 0]
    return ref.at[:, _row(k, POOL_ROWS), :]


def _sum_chips(item, slab, parts, place, into=None):
    name, sel = item
    half = HALF_SHAPES[name]
    hsel = (lambda p: p[1]) if sel is None else (lambda p: sel)
    if name == "w_in":
        blk, grid = (256, W_IN_COLS), (D_MODEL // 256,)
        own = pl.BlockSpec(blk, lambda i, p: (i, p[0]))
        got = pl.BlockSpec((N_CHIPS - 1,) + blk, lambda i, p: (0, i, 0))
        out = pl.BlockSpec((None,) + blk, lambda i, p: (hsel(p), i, 0))
    elif name == "w_out":
        grid = (1,)
        own = pl.BlockSpec(half, lambda i, p: (p[0], 0))
        got = pl.BlockSpec((N_CHIPS - 1,) + half, lambda i, p: (0, 0, 0))
        out = pl.BlockSpec((None,) + half, lambda i, p: (hsel(p), 0, 0))
    elif name == "w_qkv":
        grid = (1,)
        own = pl.BlockSpec(half, lambda i, p: (0, p[0]))
        got = pl.BlockSpec((N_CHIPS - 1,) + half, lambda i, p: (0, 0, 0))
        out = pl.BlockSpec(half, lambda i, p: (hsel(p), 0))
    else:
        grid = (1,)
        own = pl.BlockSpec((None, None) + half, lambda i, p: (p[0], 0, 0, 0))
        got = pl.BlockSpec((N_CHIPS - 1,) + half, lambda i, p: (0, 0, 0))
        out = pl.BlockSpec(half, lambda i, p: (hsel(p), 0))

    def body(p_ref, own_ref, got_ref, *rest):
        o_ref = rest[-1]

        @pl.when(hsel(p_ref) == p_ref[1])
        def _():
            acc = own_ref[...].astype(F32)
            for j in range(N_CHIPS - 1):
                acc = acc + got_ref[j].astype(F32)
            o_ref[...] = acc

    in_specs, operands, aliases = [own, got], [place, slab, parts], {}
    if into is not None:
        in_specs.append(ANY)
        operands.append(into)
        aliases = {3: 0}
    return pl.pallas_call(
        body, name=f"sum_chips_{_item_tag(item)}",
        grid_spec=pltpu.PrefetchScalarGridSpec(num_scalar_prefetch=1, grid=grid, in_specs=in_specs, out_specs=out),
        out_shape=jax.ShapeDtypeStruct(SHARD_SHAPES[name], F32), input_output_aliases=aliases,
        compiler_params=_params("parallel"),
    )(*operands)


N_DEV = 8


def _peer(d):
    x, y, c, _ = _mesh_place()
    px, py, pc = x ^ (d >> 2), y ^ ((d >> 1) & 1), c ^ (d & 1)
    return (px, py, pc), 4 * px + 2 * py + pc


def _small_all_reduce_host(vec):
    rows, cols = vec.shape

    def run(src, arr, new, scratch, finish):
        slots, stage, send, recv, local = scratch
        x, y, c, _ = _mesh_place()
        me = 4 * x + 2 * y + c
        mine = pltpu.make_async_copy(src[0], slots.at[me], local.at[0])
        copies = []
        for d in range(1, N_DEV):
            dev, _ = _peer(d)
            copies.append(pltpu.make_async_remote_copy(src_ref=src[0], dst_ref=slots.at[me], send_sem=send.at[d - 1],
                                                       recv_sem=recv.at[d - 1], device_id=dev, device_id_type=MESH))
        if not finish:
            mine.start()
            for cp in copies:
                cp.start()
            return
        for d in range(1, N_DEV):
            _, flat = _peer(d)
            got = slots.at[flat]
            pltpu.make_async_remote_copy(src_ref=got, dst_ref=got, send_sem=send.at[d - 1], recv_sem=recv.at[d - 1],
                                         device_id=(x, y, c), device_id_type=MESH).wait_recv()
        for cp in copies:
            cp.wait_send()
        mine.wait()
        acc = slots[0]
        for d in range(1, N_DEV):
            acc = acc + slots[d]
        stage[...] = acc
        out = pltpu.make_async_copy(stage, new[0], local.at[0])
        out.start()
        out.wait()

    scratch = [pltpu.VMEM((N_DEV, rows, cols), F32), pltpu.VMEM((rows, cols), F32),
               pltpu.SemaphoreType.DMA((N_DEV - 1,)), pltpu.SemaphoreType.DMA((N_DEV - 1,)),
               pltpu.SemaphoreType.DMA((1,))]
    return _Hosted([vec], [], [jax.ShapeDtypeStruct(vec.shape, F32)], scratch, functools.partial(run, finish=False),
                   functools.partial(run, finish=True))


class _Reducer:
    FIRST = (("w_in", 1), ("w_out", 1))
    ATT = (("w_qkv", None), ("w_o", None))
    W_IN0 = (("w_in", 0),)
    W_OUT0 = (("w_out", 0),)

    def __init__(self, place):
        self.place = place
        self.shards = {}
        self.state = {}

    def swap(self, items, grads):
        self.state[items] = {"grads": [grads[nm] for nm, _ in items]}
        return _swap_host(items, grads)

    def pair_sums(self, items, slabs):
        st = self.state[items]
        st["sums"] = [_pair_sum(it, g, s, self.place) for it, g, s in zip(items, st["grads"], slabs)]

    def scatter(self, items):
        return _scatter_host(items, self.state[items]["sums"])

    def chip_sums(self, items, parts):
        for it, s, p in zip(items, self.state[items]["sums"], parts):
            self.shards[it[0]] = _sum_chips(it, s, p, self.place, into=self.shards.get(it[0]))

    def share(self, items):
        return _share_host(items, [self.shards[nm] for nm, _ in items])

    def shared(self, items, updated):
        self.shards.update(dict(zip(_names(items), updated)))

    def small_early(self, pieces):
        return _small_all_reduce_host(_pack_small(pieces, EARLY_SMALL))

    def small_early_done(self, total):
        self.early_small = _unpack_small(total, EARLY_SMALL)


def _all_reduce_small(vec, g_pool, share_items, shards):
    n_share = len(share_items)

    def body(v_ref, gp_ref, *rest):
        o_ref, po_ref = rest[n_share:n_share + 2]
        shard_refs = rest[n_share + 2:2 * n_share + 2]
        (slots, pslots, send_sems, recv_sems, psend, precv, hsend, hrecv, share_send,
         share_recv) = rest[2 * n_share + 2:]
        x, y, c, _ = _mesh_place()
        me = 4 * x + 2 * y + c
        _reduce_share(share_items, shard_refs, (share_send, share_recv), finish=False)

        def pool_piece(chip, h):
            return gp_ref.at[pl.ds(2 * h, 2), pl.ds(pl.multiple_of(chip * POOL_ROWS, POOL_ROWS), POOL_ROWS), :]

        slots[me] = v_ref[...]
        pslots[me] = pool_piece(2 * x + y, c)[...]
        copies = []
        for d in range(1, N_DEV):
            peer = (x ^ (d >> 2), y ^ ((d >> 1) & 1), c ^ (d & 1))
            copies.append(pltpu.make_async_remote_copy(
                src_ref=v_ref, dst_ref=slots.at[me], send_sem=send_sems.at[d - 1], recv_sem=recv_sems.at[d - 1],
                device_id=peer, device_id_type=MESH))
            copies.append(pltpu.make_async_remote_copy(
                src_ref=pool_piece(2 * peer[0] + peer[1], peer[2]), dst_ref=pslots.at[me], send_sem=psend.at[d - 1],
                recv_sem=precv.at[d - 1], device_id=peer, device_id_type=MESH))
        for cp in copies:
            cp.start()
        for d in range(1, N_DEV):
            peer = 4 * (x ^ (d >> 2)) + 2 * (y ^ ((d >> 1) & 1)) + (c ^ (d & 1))
            for buf, ss, rs in ((slots, send_sems, recv_sems), (pslots, psend, precv)):
                got = buf.at[peer]
                pltpu.make_async_remote_copy(src_ref=got, dst_ref=got, send_sem=ss.at[d - 1], recv_sem=rs.at[d - 1],
                                             device_id=(x, y, c), device_id_type=MESH).wait_recv()
        for cp in copies:
            cp.wait_send()
        acc, pacc = slots[0], pslots[0]
        for d in range(1, N_DEV):
            acc, pacc = acc + slots[d], pacc + pslots[d]
        o_ref[...] = acc
        po_ref[pl.ds(2 * c, 2)] = pacc
        mine, theirs = po_ref.at[pl.ds(2 * c, 2)], po_ref.at[pl.ds(2 * (1 - c), 2)]
        give = pltpu.make_async_remote_copy(src_ref=mine, dst_ref=mine, send_sem=hsend.at[0], recv_sem=hrecv.at[0],
                                            device_id=(x, y, 1 - c), device_id_type=MESH)
        give.start()
        pltpu.make_async_remote_copy(src_ref=theirs, dst_ref=theirs, send_sem=hsend.at[0], recv_sem=hrecv.at[0],
                                     device_id=(x, y, 1 - c), device_id_type=MESH).wait_recv()
        give.wait_send()
        _reduce_share(share_items, shard_refs, (share_send, share_recv), finish=True)

    vm = pl.BlockSpec(memory_space=pltpu.VMEM)
    piece = (2, POOL_ROWS, POOL_GC)
    return pl.pallas_call(
        body, name="all_reduce_small", in_specs=[vm, vm] + [ANY] * n_share, out_specs=[vm, vm] + [ANY] * n_share,
        out_shape=[jax.ShapeDtypeStruct(vec.shape, F32), jax.ShapeDtypeStruct(SHARD_SHAPES["pool_w"], F32)]
        + [jax.ShapeDtypeStruct(a.shape, a.dtype) for a in shards],
        input_output_aliases={2 + t: 2 + t for t in range(n_share)},
        scratch_shapes=[pltpu.VMEM((N_DEV,) + vec.shape, F32), pltpu.VMEM((N_DEV,) + piece, F32)]
        + [pltpu.SemaphoreType.DMA((N_DEV - 1,))] * 4 + [pltpu.SemaphoreType.DMA((1,))] * 2 + _pair_sems(n_share),
    )(vec, g_pool, *shards)


def _adamw_call(w, g, m, v, name, copy_grad=False):
    shape = w.shape
    cols = shape[-1]
    rows = w.size // cols
    tr = rows
    for cand in (256, 128, 64, 32, 16, 8):
        if rows > cand and rows % cand == 0 and cand * cols * 4 <= 2 * 1024 * 1024:
            tr = cand
            break

    def body(w_ref, g_ref, m_ref, v_ref, d_ref, nm_ref, nv_ref, *g_out):
        gv = g_ref[...]
        if copy_grad:
            g_out[0][...] = gv
        mn = ADAM_B1 * m_ref[...] + (1.0 - ADAM_B1) * gv
        vn = ADAM_B2 * v_ref[...] + (1.0 - ADAM_B2) * jnp.square(gv)
        m_hat = mn / (1.0 - ADAM_B1 ** ADAM_STEP)
        v_hat = vn / (1.0 - ADAM_B2 ** ADAM_STEP)
        d_ref[...] = -ADAM_LR * (m_hat / (jnp.sqrt(v_hat) + ADAM_EPS) + ADAM_WD * w_ref[...])
        nm_ref[...] = mn
        nv_ref[...] = vn

    spec = pl.BlockSpec((tr, cols), lambda i: (i, 0))
    flat = lambda a: a.reshape(rows, cols)
    n_out = 4 if copy_grad else 3
    outs = pl.pallas_call(
        body, name=name, grid=(rows // tr,), in_specs=[spec] * 4, out_specs=[spec] * n_out,
        out_shape=[jax.ShapeDtypeStruct((rows, cols), F32)] * n_out, compiler_params=_params("parallel"),
    )(flat(w), flat(g), flat(m), flat(v))
    return tuple(o.reshape(shape) for o in outs)


WEIGHTS = ("pool_w", "pool_b", "pool_scale", "attn_w_qkv", "attn_b_qkv", "attn_sinks", "attn_w_o", "norm_mix",
           "norm_ffn", "ffn_w_in", "ffn_conv_w", "ffn_conv_b", "ffn_w_out", "norm_f")
SMALL_GATHER = 128 * 71
EARLY_SMALL = ((("b_qkv", 1536), ("sinks", 128), ("norm_mix1", 1024), ("norm_ffn1", 1024), ("conv_w", 6 * 2 * D_FF),
                ("conv_b", 4 * D_FF), ("norm_f", 1024), ("loss", 128)), 6272)
LATE_SMALL = ((("pool_b", 1024), ("pool_scale", 1024), ("norm_mix0", 1024), ("norm_ffn0", 1024)), 512)


def _pack_small(pieces, layout):
    sizes, cols = layout
    flat = jnp.concatenate([jnp.pad(pieces[nm].reshape(-1), (0, size - pieces[nm].size)) for nm, size in sizes])
    return jnp.pad(flat, (0, 8 * cols - flat.size)).reshape(8, cols)


def _unpack_small(vec, layout):
    flat = vec.reshape(-1)
    out, off = {}, 0
    for nm, size in layout[0]:
        out[nm] = flat[off:off + size]
        off += size
    return out


def kernel(x, pool_w, pool_b, pool_scale, attn_w_qkv, attn_b_qkv, attn_sinks, attn_w_o, norm_mix, norm_ffn, ffn_w_in, ffn_conv_w, ffn_conv_b, ffn_w_out, norm_f, loss_target, m_pool_w, m_pool_b, m_pool_scale, m_attn_w_qkv, m_attn_b_qkv, m_attn_sinks, m_attn_w_o, m_norm_mix, m_norm_ffn, m_ffn_w_in, m_ffn_conv_w, m_ffn_conv_b, m_ffn_w_out, m_norm_f, v_pool_w, v_pool_b, v_pool_scale, v_attn_w_qkv, v_attn_b_qkv, v_attn_sinks, v_attn_w_o, v_norm_mix, v_norm_ffn, v_ffn_w_in, v_ffn_conv_w, v_ffn_conv_b, v_ffn_w_out, v_norm_f):
    weights = dict(pool_w=pool_w, pool_b=pool_b, pool_scale=pool_scale, attn_w_qkv=attn_w_qkv, attn_b_qkv=attn_b_qkv,
                   attn_sinks=attn_sinks, attn_w_o=attn_w_o, norm_mix=norm_mix, norm_ffn=norm_ffn, ffn_w_in=ffn_w_in,
                   ffn_conv_w=ffn_conv_w, ffn_conv_b=ffn_conv_b, ffn_w_out=ffn_w_out, norm_f=norm_f)
    m_in = dict(pool_w=m_pool_w, pool_b=m_pool_b, pool_scale=m_pool_scale, attn_w_qkv=m_attn_w_qkv,
                attn_b_qkv=m_attn_b_qkv, attn_sinks=m_attn_sinks, attn_w_o=m_attn_w_o, norm_mix=m_norm_mix,
                norm_ffn=m_norm_ffn, ffn_w_in=m_ffn_w_in, ffn_conv_w=m_ffn_conv_w, ffn_conv_b=m_ffn_conv_b,
                ffn_w_out=m_ffn_w_out, norm_f=m_norm_f)
    v_in = dict(pool_w=v_pool_w, pool_b=v_pool_b, pool_scale=v_pool_scale, attn_w_qkv=v_attn_w_qkv,
                attn_b_qkv=v_attn_b_qkv, attn_sinks=v_attn_sinks, attn_w_o=v_attn_w_o, norm_mix=v_norm_mix,
                norm_ffn=v_norm_ffn, ffn_w_in=v_ffn_w_in, ffn_conv_w=v_ffn_conv_w, ffn_conv_b=v_ffn_conv_b,
                ffn_w_out=v_ffn_w_out, norm_f=v_norm_f)
    chip = 2 * lax.axis_index("x") + lax.axis_index("y")
    core = lax.axis_index("c")

    place = jnp.stack([chip, core]).astype(jnp.int32)
    shards = {"w_in": ffn_w_in, "w_out": ffn_w_out, "w_qkv": attn_w_qkv[0], "w_o": attn_w_o[0], "pool_w": pool_w[0]}
    placed = {nm: _place_shard(nm, shards[nm], place) for nm in BIG}
    small = jnp.concatenate([pool_b.reshape(-1), attn_b_qkv.reshape(-1), ffn_conv_w.reshape(-1)]).reshape(1, SMALL_GATHER)
    full, small_all = _gather_early(placed, small)
    small_all = small_all.reshape(N_CHIPS, SMALL_GATHER)
    pool_b_full = small_all[:, :256].reshape(N_CHIPS, 4, POOL_ROWS).transpose(1, 0, 2).reshape(1, D_MODEL)
    b_qkv_full = small_all[:, 256:640].reshape(1, QKV_DIM)
    conv_w_full = small_all[:, 640:].reshape(N_CHIPS, 2, 3, W_IN_COLS).transpose(1, 2, 0, 3).reshape(2, 3, 2 * D_FF)

    reducer = _Reducer(place)
    _, grad_x, grads = _local_step(
        x[0], loss_target[0], full, norm_mix=norm_mix, norm_ffn=norm_ffn, norm_f=norm_f.reshape(1, D_MODEL),
        pool_b=pool_b_full, pool_scale=pool_scale, b_qkv=b_qkv_full, sinks=attn_sinks[0], conv_w=conv_w_full,
        conv_b=ffn_conv_b, late_gather=_late_gather_host, reducer=reducer)
    reduced = dict(reducer.shards)

    late = {"pool_b": grads["pool_b"], "pool_scale": grads["pool_scale"], "norm_mix0": grads["norm_mix"][0],
            "norm_ffn0": grads["norm_ffn"][0]}
    last = _Reducer.W_OUT0
    late_sum, reduced["pool_w"], *shared = _all_reduce_small(_pack_small(late, LATE_SMALL), grads["pool_w"], last,
                                                             [reduced[nm] for nm in _names(last)])
    reduced.update(zip(_names(last), shared))
    small_g = {**reducer.early_small, **_unpack_small(late_sum, LATE_SMALL)}
    small_g["norm_mix"] = jnp.concatenate([small_g["norm_mix0"], small_g["norm_mix1"]])
    small_g["norm_ffn"] = jnp.concatenate([small_g["norm_ffn0"], small_g["norm_ffn1"]])
    loss = small_g["loss"][0]
    pool_b_g = lax.dynamic_slice_in_dim(small_g["pool_b"].reshape(4, N_CHIPS, POOL_ROWS), chip, 1, axis=1)
    b_qkv_g = lax.dynamic_slice_in_dim(small_g["b_qkv"].reshape(N_CHIPS, QKV_COLS), chip, 1, axis=0)
    conv_w_g = lax.dynamic_slice_in_dim(small_g["conv_w"].reshape(2, 3, N_CHIPS, W_IN_COLS), chip, 1, axis=2)
    grad_w = {
        "pool_w": reduced["pool_w"].reshape(pool_w.shape), "pool_b": pool_b_g.reshape(pool_b.shape),
        "pool_scale": small_g["pool_scale"].reshape(pool_scale.shape),
        "attn_w_qkv": reduced["w_qkv"].reshape(attn_w_qkv.shape), "attn_b_qkv": b_qkv_g.reshape(attn_b_qkv.shape),
        "attn_sinks": small_g["sinks"][:N_HEADS].reshape(attn_sinks.shape),
        "attn_w_o": reduced["w_o"].reshape(attn_w_o.shape), "norm_mix": small_g["norm_mix"].reshape(norm_mix.shape),
        "norm_ffn": small_g["norm_ffn"].reshape(norm_ffn.shape), "ffn_w_in": reduced["w_in"],
        "ffn_conv_w": conv_w_g.reshape(ffn_conv_w.shape), "ffn_conv_b": small_g["conv_b"].reshape(ffn_conv_b.shape),
        "ffn_w_out": reduced["w_out"], "norm_f": small_g["norm_f"].reshape(norm_f.shape),
    }

    delta, new_m, new_v = {}, {}, {}
    exchanged = ("pool_w", "attn_w_qkv", "attn_w_o", "ffn_w_in", "ffn_w_out")
    for nm in WEIGHTS:
        res = _adamw_call(weights[nm], grad_w[nm], m_in[nm], v_in[nm], f"adamw_{nm}", copy_grad=nm in exchanged)
        delta[nm], new_m[nm], new_v[nm] = res[:3]
        if nm in exchanged:
            grad_w[nm] = res[3]
    return (loss, grad_x.reshape(x.shape), *[grad_w[nm] for nm in WEIGHTS], *[delta[nm] for nm in WEIGHTS],
            *[new_m[nm] for nm in WEIGHTS], *[new_v[nm] for nm in WEIGHTS])
```
